```python
import jax, jax.numpy as jnp
from jax import lax
import numpy as np

D_MODEL = 1024
BATCH = 8
SEQ = 8192
DEPTH = 1

MEM_LEN = 256
CHUNK = 128
SG_GROUPS = 8
SG_GROUP_DIM = 64
SG_WIDTH = SG_GROUPS * SG_GROUP_DIM
MLA_HEADS = 8
MLA_NOPE = 64
MLA_ROPE = 32
MLA_V = 64
MLA_QK = MLA_NOPE + MLA_ROPE
MLA_Q_RANK = 384
MLA_KV_RANK = 256
MLA_WIDTH = MLA_HEADS * MLA_V
MEM_HEADS = 4
MEM_HEAD_DIM = 128
MEM_WIDTH = MEM_HEADS * MEM_HEAD_DIM
N_BRANCH = 3
D_FF = 2816
ROPE_BASE = 10000.0
EPS = 1e-6
Q_BLOCK = 128
NEG = -1e30

COL_U = 0
COL_V = COL_U + SG_WIDTH
COL_CQ = COL_V + SG_WIDTH
COL_CKV = COL_CQ + MLA_Q_RANK
COL_KR = COL_CKV + MLA_KV_RANK
COL_QM = COL_KR + MLA_ROPE
COL_GATE = COL_QM + MEM_WIDTH
IN_COLS = COL_GATE + N_BRANCH * D_MODEL

kernel_name = "hybrid_gated_sgu_mla_memxattn_macaron"


def rmsnorm(x, g):
    xf = x.astype(jnp.float32)
    y = xf * lax.rsqrt(jnp.mean(xf * xf, axis=-1, keepdims=True) + EPS)
    return (y * g.astype(jnp.float32)).astype(x.dtype)


def layernorm(x, g, b):
    xf = x.astype(jnp.float32)
    mu = jnp.mean(xf, axis=-1, keepdims=True)
    xc = xf - mu
    y = xc * lax.rsqrt(jnp.mean(xc * xc, axis=-1, keepdims=True) + EPS)
    return (y * g.astype(jnp.float32) + b.astype(jnp.float32)).astype(x.dtype)


def rope(x, positions):
    half = x.shape[-1] // 2
    inv = ROPE_BASE ** (-jnp.arange(half, dtype=jnp.float32) / half)
    ang = positions.astype(jnp.float32)[:, :, None] * inv
    cos = jnp.cos(ang)[:, :, None, :]
    sin = jnp.sin(ang)[:, :, None, :]
    x1 = x[..., :half].astype(jnp.float32)
    x2 = x[..., half:].astype(jnp.float32)
    return jnp.concatenate([x1 * cos - x2 * sin, x2 * cos + x1 * sin], axis=-1).astype(x.dtype)


def swiglu(x, w_gu, w_down):
    g, u = jnp.split(x @ w_gu, 2, axis=-1)
    return (jax.nn.silu(g) * u) @ w_down


def spatial_gating(u, v, ln_g, ln_b, w_s, b_s):
    B, S, _ = v.shape
    nc = S // CHUNK
    v = layernorm(v, ln_g, ln_b).reshape(B, nc, CHUNK, SG_GROUPS, SG_GROUP_DIM)
    causal = jnp.tril(jnp.ones((CHUNK, CHUNK), dtype=bool))
    w = jnp.where(causal[None], w_s, jnp.zeros_like(w_s))
    mixed = jnp.einsum('gts,bcsgd->bctgd', w, v) + b_s.T[None, None, :, :, None]
    return u * mixed.reshape(B, S, SG_WIDTH)


def causal_block_attention(q, k, v):
    B, S, H, Dqk = q.shape
    Dv = v.shape[-1]
    nb = S // Q_BLOCK
    scale = Dqk ** -0.5
    qb = q.reshape(B, nb, Q_BLOCK, H, Dqk).transpose(1, 0, 2, 3, 4)
    kpos = jnp.arange(S)

    def one_block(args):
        i, qi = args
        s = jnp.einsum('bqhd,bkhd->bhqk', qi, k).astype(jnp.float32) * scale
        qpos = i * Q_BLOCK + jnp.arange(Q_BLOCK)
        mask = kpos[None, :] <= qpos[:, None]
        s = jnp.where(mask[None, None], s, NEG)
        p = jax.nn.softmax(s, axis=-1)
        return jnp.einsum('bhqk,bkhd->bqhd', p.astype(v.dtype), v)

    out = lax.map(one_block, (jnp.arange(nb), qb))
    return out.transpose(1, 0, 2, 3, 4).reshape(B, S, H * Dv)


def mla(c_q, c_kv, k_rope, positions, cq_norm, w_uq, ckv_norm, w_ukv, q_norm, k_norm):
    B, S, _ = c_q.shape
    q = (rmsnorm(c_q, cq_norm) @ w_uq).reshape(B, S, MLA_HEADS, MLA_QK)
    q = rmsnorm(q, q_norm)
    q = jnp.concatenate([q[..., :MLA_NOPE], rope(q[..., MLA_NOPE:], positions)], axis=-1)
    kv = (rmsnorm(c_kv, ckv_norm) @ w_ukv).reshape(B, S, MLA_HEADS, MLA_NOPE + MLA_V)
    k_nope, v = kv[..., :MLA_NOPE], kv[..., MLA_NOPE:]
    k_pe = jnp.broadcast_to(k_rope[:, :, None, :], (B, S, MLA_HEADS, MLA_ROPE))
    k = rmsnorm(jnp.concatenate([k_nope, k_pe], axis=-1), k_norm)
    k = jnp.concatenate([k[..., :MLA_NOPE], rope(k[..., MLA_NOPE:], positions)], axis=-1)
    return causal_block_attention(q, k, v)


def memory_attention(q_m, mem, mem_norm, w_kv, q_norm, k_norm):
    B, S, _ = q_m.shape
    q = rmsnorm(q_m.reshape(B, S, MEM_HEADS, MEM_HEAD_DIM), q_norm)
    kv = rmsnorm(mem, mem_norm) @ w_kv
    M = mem.shape[1]
    k = rmsnorm(kv[..., :MEM_WIDTH].reshape(B, M, MEM_HEADS, MEM_HEAD_DIM), k_norm)
    v = kv[..., MEM_WIDTH:].reshape(B, M, MEM_HEADS, MEM_HEAD_DIM)
    s = jnp.einsum('bshd,bmhd->bhsm', q, k).astype(jnp.float32) * (MEM_HEAD_DIM ** -0.5)
    p = jax.nn.softmax(s, axis=-1)
    return jnp.einsum('bhsm,bmhd->bshd', p.astype(v.dtype), v).reshape(B, S, MEM_WIDTH)


def _fwd_setup_inputs(seed: int = 0) -> dict:
    key = jax.random.key(seed)
    ks = iter(jax.random.split(key, 40))

    def nrm(shape, scale):
        return jax.random.normal(next(ks), shape, jnp.float32) * scale

    def gain(n):
        return 1.0 + nrm((DEPTH, n), 0.05)

    L = DEPTH
    d = D_MODEL
    x = nrm((BATCH, SEQ, d), 1.0)
    mem = nrm((BATCH, MEM_LEN, d), 1.0)
    offset = jax.random.randint(next(ks), (BATCH, 1), 0, 1024, dtype=jnp.int32)
    positions = offset + jnp.arange(SEQ, dtype=jnp.int32)[None, :]
    return {
        "x": x,
        "mem": mem,
        "positions": positions,
        "ffn1_norm": gain(d),
        "ffn1_w_gu": nrm((L, d, 2 * D_FF), d ** -0.5),
        "ffn1_w_down": nrm((L, D_FF, d), D_FF ** -0.5),
        "mix_norm": gain(d),
        "w_in": nrm((L, d, IN_COLS), d ** -0.5),
        "b_gate": nrm((L, N_BRANCH * d), 0.02),
        "sg_ln_g": gain(SG_WIDTH),
        "sg_ln_b": nrm((L, SG_WIDTH), 0.02),
        "sg_w": nrm((L, SG_GROUPS, CHUNK, CHUNK), 0.5 * CHUNK ** -0.5),
        "sg_b": 1.0 + nrm((L, SG_GROUPS, CHUNK), 0.1),
        "mla_cq_norm": gain(MLA_Q_RANK),
        "mla_w_uq": nrm((L, MLA_Q_RANK, MLA_HEADS * MLA_QK), MLA_Q_RANK ** -0.5),
        "mla_ckv_norm": gain(MLA_KV_RANK),
        "mla_w_ukv": nrm((L, MLA_KV_RANK, MLA_HEADS * (MLA_NOPE + MLA_V)), MLA_KV_RANK ** -0.5),
        "mla_q_norm": gain(MLA_QK),
        "mla_k_norm": gain(MLA_QK),
        "mem_norm": gain(d),
        "mem_w_kv": nrm((L, d, 2 * MEM_WIDTH), d ** -0.5),
        "mem_q_norm": gain(MEM_HEAD_DIM),
        "mem_k_norm": gain(MEM_HEAD_DIM),
        "w_branch_a": nrm((L, SG_WIDTH, d), SG_WIDTH ** -0.5),
        "w_branch_b": nrm((L, MLA_WIDTH, d), MLA_WIDTH ** -0.5),
        "w_branch_c": nrm((L, MEM_WIDTH, d), MEM_WIDTH ** -0.5),
        "w_out": nrm((L, d, d), d ** -0.5),
        "ffn2_norm": gain(d),
        "ffn2_w_gu": nrm((L, d, 2 * D_FF), d ** -0.5),
        "ffn2_w_down": nrm((L, D_FF, d), D_FF ** -0.5),
    }


def _fwd_reference(x, mem, positions, ffn1_norm, ffn1_w_gu, ffn1_w_down, mix_norm, w_in, b_gate,
              sg_ln_g, sg_ln_b, sg_w, sg_b, mla_cq_norm, mla_w_uq, mla_ckv_norm, mla_w_ukv,
              mla_q_norm, mla_k_norm, mem_norm, mem_w_kv, mem_q_norm, mem_k_norm,
              w_branch_a, w_branch_b, w_branch_c, w_out, ffn2_norm, ffn2_w_gu, ffn2_w_down):
    B, S, _ = x.shape
    for l in range(DEPTH):
        x = x + 0.5 * swiglu(rmsnorm(x, ffn1_norm[l]), ffn1_w_gu[l], ffn1_w_down[l])
        h = rmsnorm(x, mix_norm[l])
        z = h @ w_in[l]
        u = jax.nn.gelu(z[..., COL_U:COL_V], approximate=False)
        v = jax.nn.gelu(z[..., COL_V:COL_CQ], approximate=False)
        y_a = spatial_gating(u, v, sg_ln_g[l], sg_ln_b[l], sg_w[l], sg_b[l])
        y_b = mla(z[..., COL_CQ:COL_CKV], z[..., COL_CKV:COL_KR], z[..., COL_KR:COL_QM], positions,
                  mla_cq_norm[l], mla_w_uq[l], mla_ckv_norm[l], mla_w_ukv[l],
                  mla_q_norm[l], mla_k_norm[l])
        y_c = memory_attention(z[..., COL_QM:COL_GATE], mem, mem_norm[l], mem_w_kv[l],
                               mem_q_norm[l], mem_k_norm[l])
        gates = jax.nn.sigmoid(z[..., COL_GATE:] + b_gate[l]).reshape(B, S, N_BRANCH, D_MODEL)
        merged = (gates[:, :, 0] * (y_a @ w_branch_a[l])
                  + gates[:, :, 1] * (y_b @ w_branch_b[l])
                  + gates[:, :, 2] * (y_c @ w_branch_c[l]))
        x = x + merged @ w_out[l]
        x = x + 0.5 * swiglu(rmsnorm(x, ffn2_norm[l]), ffn2_w_gu[l], ffn2_w_down[l])
    return x


import jax as _jax
import jax.numpy as _jnp

TWIN_FORMAT = 'train_step'
FWD_PARAMS = ['x', 'mem', 'positions', 'ffn1_norm', 'ffn1_w_gu', 'ffn1_w_down', 'mix_norm', 'w_in', 'b_gate', 'sg_ln_g', 'sg_ln_b', 'sg_w', 'sg_b', 'mla_cq_norm', 'mla_w_uq', 'mla_ckv_norm', 'mla_w_ukv', 'mla_q_norm', 'mla_k_norm', 'mem_norm', 'mem_w_kv', 'mem_q_norm', 'mem_k_norm', 'w_branch_a', 'w_branch_b', 'w_branch_c', 'w_out', 'ffn2_norm', 'ffn2_w_gu', 'ffn2_w_down']
TWIN_WEIGHTS = ['ffn1_norm', 'ffn1_w_gu', 'ffn1_w_down', 'mix_norm', 'w_in', 'b_gate', 'sg_ln_g', 'sg_ln_b', 'sg_w', 'sg_b', 'mla_cq_norm', 'mla_w_uq', 'mla_ckv_norm', 'mla_w_ukv', 'mla_q_norm', 'mla_k_norm', 'mem_norm', 'mem_w_kv', 'mem_q_norm', 'mem_k_norm', 'w_branch_a', 'w_branch_b', 'w_branch_c', 'w_out', 'ffn2_norm', 'ffn2_w_gu', 'ffn2_w_down']
TWIN_DIFF_INPUT = 'x'
TWIN_INPUTS = ['x', 'mem', 'positions', 'ffn1_norm', 'ffn1_w_gu', 'ffn1_w_down', 'mix_norm', 'w_in', 'b_gate', 'sg_ln_g', 'sg_ln_b', 'sg_w', 'sg_b', 'mla_cq_norm', 'mla_w_uq', 'mla_ckv_norm', 'mla_w_ukv', 'mla_q_norm', 'mla_k_norm', 'mem_norm', 'mem_w_kv', 'mem_q_norm', 'mem_k_norm', 'w_branch_a', 'w_branch_b', 'w_branch_c', 'w_out', 'ffn2_norm', 'ffn2_w_gu', 'ffn2_w_down', 'loss_target', 'm_ffn1_norm', 'm_ffn1_w_gu', 'm_ffn1_w_down', 'm_mix_norm', 'm_w_in', 'm_b_gate', 'm_sg_ln_g', 'm_sg_ln_b', 'm_sg_w', 'm_sg_b', 'm_mla_cq_norm', 'm_mla_w_uq', 'm_mla_ckv_norm', 'm_mla_w_ukv', 'm_mla_q_norm', 'm_mla_k_norm', 'm_mem_norm', 'm_mem_w_kv', 'm_mem_q_norm', 'm_mem_k_norm', 'm_w_branch_a', 'm_w_branch_b', 'm_w_branch_c', 'm_w_out', 'm_ffn2_norm', 'm_ffn2_w_gu', 'm_ffn2_w_down', 'v_ffn1_norm', 'v_ffn1_w_gu', 'v_ffn1_w_down', 'v_mix_norm', 'v_w_in', 'v_b_gate', 'v_sg_ln_g', 'v_sg_ln_b', 'v_sg_w', 'v_sg_b', 'v_mla_cq_norm', 'v_mla_w_uq', 'v_mla_ckv_norm', 'v_mla_w_ukv', 'v_mla_q_norm', 'v_mla_k_norm', 'v_mem_norm', 'v_mem_w_kv', 'v_mem_q_norm', 'v_mem_k_norm', 'v_w_branch_a', 'v_w_branch_b', 'v_w_branch_c', 'v_w_out', 'v_ffn2_norm', 'v_ffn2_w_gu', 'v_ffn2_w_down']
TWIN_OUTPUTS = ['loss', 'grad_x', 'grad_ffn1_norm', 'grad_ffn1_w_gu', 'grad_ffn1_w_down', 'grad_mix_norm', 'grad_w_in', 'grad_b_gate', 'grad_sg_ln_g', 'grad_sg_ln_b', 'grad_sg_w', 'grad_sg_b', 'grad_mla_cq_norm', 'grad_mla_w_uq', 'grad_mla_ckv_norm', 'grad_mla_w_ukv', 'grad_mla_q_norm', 'grad_mla_k_norm', 'grad_mem_norm', 'grad_mem_w_kv', 'grad_mem_q_norm', 'grad_mem_k_norm', 'grad_w_branch_a', 'grad_w_branch_b', 'grad_w_branch_c', 'grad_w_out', 'grad_ffn2_norm', 'grad_ffn2_w_gu', 'grad_ffn2_w_down', 'delta_ffn1_norm', 'delta_ffn1_w_gu', 'delta_ffn1_w_down', 'delta_mix_norm', 'delta_w_in', 'delta_b_gate', 'delta_sg_ln_g', 'delta_sg_ln_b', 'delta_sg_w', 'delta_sg_b', 'delta_mla_cq_norm', 'delta_mla_w_uq', 'delta_mla_ckv_norm', 'delta_mla_w_ukv', 'delta_mla_q_norm', 'delta_mla_k_norm', 'delta_mem_norm', 'delta_mem_w_kv', 'delta_mem_q_norm', 'delta_mem_k_norm', 'delta_w_branch_a', 'delta_w_branch_b', 'delta_w_branch_c', 'delta_w_out', 'delta_ffn2_norm', 'delta_ffn2_w_gu', 'delta_ffn2_w_down', 'new_m_ffn1_norm', 'new_m_ffn1_w_gu', 'new_m_ffn1_w_down', 'new_m_mix_norm', 'new_m_w_in', 'new_m_b_gate', 'new_m_sg_ln_g', 'new_m_sg_ln_b', 'new_m_sg_w', 'new_m_sg_b', 'new_m_mla_cq_norm', 'new_m_mla_w_uq', 'new_m_mla_ckv_norm', 'new_m_mla_w_ukv', 'new_m_mla_q_norm', 'new_m_mla_k_norm', 'new_m_mem_norm', 'new_m_mem_w_kv', 'new_m_mem_q_norm', 'new_m_mem_k_norm', 'new_m_w_branch_a', 'new_m_w_branch_b', 'new_m_w_branch_c', 'new_m_w_out', 'new_m_ffn2_norm', 'new_m_ffn2_w_gu', 'new_m_ffn2_w_down', 'new_v_ffn1_norm', 'new_v_ffn1_w_gu', 'new_v_ffn1_w_down', 'new_v_mix_norm', 'new_v_w_in', 'new_v_b_gate', 'new_v_sg_ln_g', 'new_v_sg_ln_b', 'new_v_sg_w', 'new_v_sg_b', 'new_v_mla_cq_norm', 'new_v_mla_w_uq', 'new_v_mla_ckv_norm', 'new_v_mla_w_ukv', 'new_v_mla_q_norm', 'new_v_mla_k_norm', 'new_v_mem_norm', 'new_v_mem_w_kv', 'new_v_mem_q_norm', 'new_v_mem_k_norm', 'new_v_w_branch_a', 'new_v_w_branch_b', 'new_v_w_branch_c', 'new_v_w_out', 'new_v_ffn2_norm', 'new_v_ffn2_w_gu', 'new_v_ffn2_w_down']
TWIN_LEAF_KINDS = {'loss': 'loss', 'grad_x': 'grad_x', 'grad_ffn1_norm': 'grad_w', 'grad_ffn1_w_gu': 'grad_w', 'grad_ffn1_w_down': 'grad_w', 'grad_mix_norm': 'grad_w', 'grad_w_in': 'grad_w', 'grad_b_gate': 'grad_w', 'grad_sg_ln_g': 'grad_w', 'grad_sg_ln_b': 'grad_w', 'grad_sg_w': 'grad_w', 'grad_sg_b': 'grad_w', 'grad_mla_cq_norm': 'grad_w', 'grad_mla_w_uq': 'grad_w', 'grad_mla_ckv_norm': 'grad_w', 'grad_mla_w_ukv': 'grad_w', 'grad_mla_q_norm': 'grad_w', 'grad_mla_k_norm': 'grad_w', 'grad_mem_norm': 'grad_w', 'grad_mem_w_kv': 'grad_w', 'grad_mem_q_norm': 'grad_w', 'grad_mem_k_norm': 'grad_w', 'grad_w_branch_a': 'grad_w', 'grad_w_branch_b': 'grad_w', 'grad_w_branch_c': 'grad_w', 'grad_w_out': 'grad_w', 'grad_ffn2_norm': 'grad_w', 'grad_ffn2_w_gu': 'grad_w', 'grad_ffn2_w_down': 'grad_w', 'delta_ffn1_norm': 'delta_w', 'delta_ffn1_w_gu': 'delta_w', 'delta_ffn1_w_down': 'delta_w', 'delta_mix_norm': 'delta_w', 'delta_w_in': 'delta_w', 'delta_b_gate': 'delta_w', 'delta_sg_ln_g': 'delta_w', 'delta_sg_ln_b': 'delta_w', 'delta_sg_w': 'delta_w', 'delta_sg_b': 'delta_w', 'delta_mla_cq_norm': 'delta_w', 'delta_mla_w_uq': 'delta_w', 'delta_mla_ckv_norm': 'delta_w', 'delta_mla_w_ukv': 'delta_w', 'delta_mla_q_norm': 'delta_w', 'delta_mla_k_norm': 'delta_w', 'delta_mem_norm': 'delta_w', 'delta_mem_w_kv': 'delta_w', 'delta_mem_q_norm': 'delta_w', 'delta_mem_k_norm': 'delta_w', 'delta_w_branch_a': 'delta_w', 'delta_w_branch_b': 'delta_w', 'delta_w_branch_c': 'delta_w', 'delta_w_out': 'delta_w', 'delta_ffn2_norm': 'delta_w', 'delta_ffn2_w_gu': 'delta_w', 'delta_ffn2_w_down': 'delta_w', 'new_m_ffn1_norm': 'new_m', 'new_m_ffn1_w_gu': 'new_m', 'new_m_ffn1_w_down': 'new_m', 'new_m_mix_norm': 'new_m', 'new_m_w_in': 'new_m', 'new_m_b_gate': 'new_m', 'new_m_sg_ln_g': 'new_m', 'new_m_sg_ln_b': 'new_m', 'new_m_sg_w': 'new_m', 'new_m_sg_b': 'new_m', 'new_m_mla_cq_norm': 'new_m', 'new_m_mla_w_uq': 'new_m', 'new_m_mla_ckv_norm': 'new_m', 'new_m_mla_w_ukv': 'new_m', 'new_m_mla_q_norm': 'new_m', 'new_m_mla_k_norm': 'new_m', 'new_m_mem_norm': 'new_m', 'new_m_mem_w_kv': 'new_m', 'new_m_mem_q_norm': 'new_m', 'new_m_mem_k_norm': 'new_m', 'new_m_w_branch_a': 'new_m', 'new_m_w_branch_b': 'new_m', 'new_m_w_branch_c': 'new_m', 'new_m_w_out': 'new_m', 'new_m_ffn2_norm': 'new_m', 'new_m_ffn2_w_gu': 'new_m', 'new_m_ffn2_w_down': 'new_m', 'new_v_ffn1_norm': 'new_v', 'new_v_ffn1_w_gu': 'new_v', 'new_v_ffn1_w_down': 'new_v', 'new_v_mix_norm': 'new_v', 'new_v_w_in': 'new_v', 'new_v_b_gate': 'new_v', 'new_v_sg_ln_g': 'new_v', 'new_v_sg_ln_b': 'new_v', 'new_v_sg_w': 'new_v', 'new_v_sg_b': 'new_v', 'new_v_mla_cq_norm': 'new_v', 'new_v_mla_w_uq': 'new_v', 'new_v_mla_ckv_norm': 'new_v', 'new_v_mla_w_ukv': 'new_v', 'new_v_mla_q_norm': 'new_v', 'new_v_mla_k_norm': 'new_v', 'new_v_mem_norm': 'new_v', 'new_v_mem_w_kv': 'new_v', 'new_v_mem_q_norm': 'new_v', 'new_v_mem_k_norm': 'new_v', 'new_v_w_branch_a': 'new_v', 'new_v_w_branch_b': 'new_v', 'new_v_w_branch_c': 'new_v', 'new_v_w_out': 'new_v', 'new_v_ffn2_norm': 'new_v', 'new_v_ffn2_w_gu': 'new_v', 'new_v_ffn2_w_down': 'new_v'}


def _forward(args):
    return _fwd_reference(*[args[k] for k in FWD_PARAMS])


def _output_shape():
    def fwd():
        inp = _fwd_setup_inputs(0)
        return _fwd_reference(*[inp[k] for k in FWD_PARAMS])
    out = _jax.eval_shape(fwd)
    return out.shape, out.dtype

N_MICROBATCH = 1
ADAM_LR = 0.001
ADAM_B1 = 0.9
ADAM_B2 = 0.999
ADAM_EPS = 1e-08
ADAM_WD = 0.01
ADAM_STEP = 10
PER_EXAMPLE_BATCH_AXIS = {'x': 0, 'mem': 0, 'positions': 0, 'loss_target': 0}
SHARED_INPUTS = []
_WEIGHT_DTYPES = {'ffn1_norm': _jnp.float32, 'ffn1_w_gu': _jnp.float32, 'ffn1_w_down': _jnp.float32, 'mix_norm': _jnp.float32, 'w_in': _jnp.float32, 'b_gate': _jnp.float32, 'sg_ln_g': _jnp.float32, 'sg_ln_b': _jnp.float32, 'sg_w': _jnp.float32, 'sg_b': _jnp.float32, 'mla_cq_norm': _jnp.float32, 'mla_w_uq': _jnp.float32, 'mla_ckv_norm': _jnp.float32, 'mla_w_ukv': _jnp.float32, 'mla_q_norm': _jnp.float32, 'mla_k_norm': _jnp.float32, 'mem_norm': _jnp.float32, 'mem_w_kv': _jnp.float32, 'mem_q_norm': _jnp.float32, 'mem_k_norm': _jnp.float32, 'w_branch_a': _jnp.float32, 'w_branch_b': _jnp.float32, 'w_branch_c': _jnp.float32, 'w_out': _jnp.float32, 'ffn2_norm': _jnp.float32, 'ffn2_w_gu': _jnp.float32, 'ffn2_w_down': _jnp.float32}
MOMENT_SCALE = {'ffn1_norm': 1.217092e+01, 'ffn1_w_gu': 1.176658e-01, 'ffn1_w_down': 2.135333e-01, 'mix_norm': 1.169308e+01, 'w_in': 2.348335e-01, 'b_gate': 1.985077e+00, 'sg_ln_g': 1.995743e+00, 'sg_ln_b': 4.666918e-01, 'sg_w': 2.012362e-01, 'sg_b': 8.206870e+00, 'mla_cq_norm': 7.872426e-02, 'mla_w_uq': 5.578088e-02, 'mla_ckv_norm': 7.016409e-01, 'mla_w_ukv': 8.728561e-02, 'mla_q_norm': 1.032864e+00, 'mla_k_norm': 1.018398e+00, 'mem_norm': 2.140494e-01, 'mem_w_kv': 1.294023e-01, 'mem_q_norm': 1.466674e+00, 'mem_k_norm': 1.476942e+00, 'w_branch_a': 2.174843e+00, 'w_branch_b': 7.144564e-02, 'w_branch_c': 1.254953e-01, 'w_out': 1.475607e+00, 'ffn2_norm': 1.228624e+01, 'ffn2_w_gu': 1.648220e-01, 'ffn2_w_down': 2.392012e-01}


def _to_microbatches(a, axis):
    t = _jnp.moveaxis(a, axis, 0)
    t = t.reshape((N_MICROBATCH, t.shape[0] // N_MICROBATCH) + t.shape[1:])
    return _jnp.moveaxis(t, 1, axis + 1)


def setup_inputs(seed: int = 0) -> dict:
    inp = _fwd_setup_inputs(seed)
    key = _jax.random.fold_in(_jax.random.key(seed), 7919)
    shape, _ = _output_shape()
    out = dict(inp)
    out["loss_target"] = _jax.random.normal(_jax.random.fold_in(key, 0), shape, _jnp.float32)
    for i, name in enumerate(TWIN_WEIGHTS):
        w = inp[name].astype(_jnp.float32)
        if MOMENT_SCALE is None:
            s = _jnp.sqrt(_jnp.mean(_jnp.square(w)) + 1e-30)
        else:
            s = MOMENT_SCALE[name]
        km, kv = _jax.random.split(_jax.random.fold_in(key, i + 1))
        out[name] = w
        out["m_" + name] = s * _jax.random.normal(km, w.shape, _jnp.float32)
        out["v_" + name] = (s * s) * _jax.random.uniform(kv, w.shape, _jnp.float32, 0.5, 1.5)
    if N_MICROBATCH > 1:
        for name, axis in PER_EXAMPLE_BATCH_AXIS.items():
            out[name] = _to_microbatches(out[name], axis)
    return {'x': out['x'], 'mem': out['mem'], 'positions': out['positions'], 'ffn1_norm': out['ffn1_norm'], 'ffn1_w_gu': out['ffn1_w_gu'], 'ffn1_w_down': out['ffn1_w_down'], 'mix_norm': out['mix_norm'], 'w_in': out['w_in'], 'b_gate': out['b_gate'], 'sg_ln_g': out['sg_ln_g'], 'sg_ln_b': out['sg_ln_b'], 'sg_w': out['sg_w'], 'sg_b': out['sg_b'], 'mla_cq_norm': out['mla_cq_norm'], 'mla_w_uq': out['mla_w_uq'], 'mla_ckv_norm': out['mla_ckv_norm'], 'mla_w_ukv': out['mla_w_ukv'], 'mla_q_norm': out['mla_q_norm'], 'mla_k_norm': out['mla_k_norm'], 'mem_norm': out['mem_norm'], 'mem_w_kv': out['mem_w_kv'], 'mem_q_norm': out['mem_q_norm'], 'mem_k_norm': out['mem_k_norm'], 'w_branch_a': out['w_branch_a'], 'w_branch_b': out['w_branch_b'], 'w_branch_c': out['w_branch_c'], 'w_out': out['w_out'], 'ffn2_norm': out['ffn2_norm'], 'ffn2_w_gu': out['ffn2_w_gu'], 'ffn2_w_down': out['ffn2_w_down'], 'loss_target': out['loss_target'], 'm_ffn1_norm': out['m_ffn1_norm'], 'm_ffn1_w_gu': out['m_ffn1_w_gu'], 'm_ffn1_w_down': out['m_ffn1_w_down'], 'm_mix_norm': out['m_mix_norm'], 'm_w_in': out['m_w_in'], 'm_b_gate': out['m_b_gate'], 'm_sg_ln_g': out['m_sg_ln_g'], 'm_sg_ln_b': out['m_sg_ln_b'], 'm_sg_w': out['m_sg_w'], 'm_sg_b': out['m_sg_b'], 'm_mla_cq_norm': out['m_mla_cq_norm'], 'm_mla_w_uq': out['m_mla_w_uq'], 'm_mla_ckv_norm': out['m_mla_ckv_norm'], 'm_mla_w_ukv': out['m_mla_w_ukv'], 'm_mla_q_norm': out['m_mla_q_norm'], 'm_mla_k_norm': out['m_mla_k_norm'], 'm_mem_norm': out['m_mem_norm'], 'm_mem_w_kv': out['m_mem_w_kv'], 'm_mem_q_norm': out['m_mem_q_norm'], 'm_mem_k_norm': out['m_mem_k_norm'], 'm_w_branch_a': out['m_w_branch_a'], 'm_w_branch_b': out['m_w_branch_b'], 'm_w_branch_c': out['m_w_branch_c'], 'm_w_out': out['m_w_out'], 'm_ffn2_norm': out['m_ffn2_norm'], 'm_ffn2_w_gu': out['m_ffn2_w_gu'], 'm_ffn2_w_down': out['m_ffn2_w_down'], 'v_ffn1_norm': out['v_ffn1_norm'], 'v_ffn1_w_gu': out['v_ffn1_w_gu'], 'v_ffn1_w_down': out['v_ffn1_w_down'], 'v_mix_norm': out['v_mix_norm'], 'v_w_in': out['v_w_in'], 'v_b_gate': out['v_b_gate'], 'v_sg_ln_g': out['v_sg_ln_g'], 'v_sg_ln_b': out['v_sg_ln_b'], 'v_sg_w': out['v_sg_w'], 'v_sg_b': out['v_sg_b'], 'v_mla_cq_norm': out['v_mla_cq_norm'], 'v_mla_w_uq': out['v_mla_w_uq'], 'v_mla_ckv_norm': out['v_mla_ckv_norm'], 'v_mla_w_ukv': out['v_mla_w_ukv'], 'v_mla_q_norm': out['v_mla_q_norm'], 'v_mla_k_norm': out['v_mla_k_norm'], 'v_mem_norm': out['v_mem_norm'], 'v_mem_w_kv': out['v_mem_w_kv'], 'v_mem_q_norm': out['v_mem_q_norm'], 'v_mem_k_norm': out['v_mem_k_norm'], 'v_w_branch_a': out['v_w_branch_a'], 'v_w_branch_b': out['v_w_branch_b'], 'v_w_branch_c': out['v_w_branch_c'], 'v_w_out': out['v_w_out'], 'v_ffn2_norm': out['v_ffn2_norm'], 'v_ffn2_w_gu': out['v_ffn2_w_gu'], 'v_ffn2_w_down': out['v_ffn2_w_down']}


def _loss(weights, diff, rest, loss_target):
    with _jax.named_scope("forward"):
        args = {**rest, TWIN_DIFF_INPUT: diff, **{k: w.astype(_WEIGHT_DTYPES[k]) for k, w in weights.items()}}
        y = _forward(args)
    with _jax.named_scope("loss_head"):
        err = _jnp.square(y.astype(_jnp.float32) - loss_target)
        return 0.5 * _jnp.sum(_jnp.mean(err, axis=-1)) if err.ndim else 0.5 * err


def _adamw(w, g, m, v):
    m = ADAM_B1 * m + (1.0 - ADAM_B1) * g
    v = ADAM_B2 * v + (1.0 - ADAM_B2) * _jnp.square(g)
    m_hat = m / (1.0 - ADAM_B1 ** ADAM_STEP)
    v_hat = v / (1.0 - ADAM_B2 ** ADAM_STEP)
    delta = -ADAM_LR * (m_hat / (_jnp.sqrt(v_hat) + ADAM_EPS) + ADAM_WD * w)
    return delta, m, v


def reference(x, mem, positions, ffn1_norm, ffn1_w_gu, ffn1_w_down, mix_norm, w_in, b_gate, sg_ln_g, sg_ln_b, sg_w, sg_b, mla_cq_norm, mla_w_uq, mla_ckv_norm, mla_w_ukv, mla_q_norm, mla_k_norm, mem_norm, mem_w_kv, mem_q_norm, mem_k_norm, w_branch_a, w_branch_b, w_branch_c, w_out, ffn2_norm, ffn2_w_gu, ffn2_w_down, loss_target, m_ffn1_norm, m_ffn1_w_gu, m_ffn1_w_down, m_mix_norm, m_w_in, m_b_gate, m_sg_ln_g, m_sg_ln_b, m_sg_w, m_sg_b, m_mla_cq_norm, m_mla_w_uq, m_mla_ckv_norm, m_mla_w_ukv, m_mla_q_norm, m_mla_k_norm, m_mem_norm, m_mem_w_kv, m_mem_q_norm, m_mem_k_norm, m_w_branch_a, m_w_branch_b, m_w_branch_c, m_w_out, m_ffn2_norm, m_ffn2_w_gu, m_ffn2_w_down, v_ffn1_norm, v_ffn1_w_gu, v_ffn1_w_down, v_mix_norm, v_w_in, v_b_gate, v_sg_ln_g, v_sg_ln_b, v_sg_w, v_sg_b, v_mla_cq_norm, v_mla_w_uq, v_mla_ckv_norm, v_mla_w_ukv, v_mla_q_norm, v_mla_k_norm, v_mem_norm, v_mem_w_kv, v_mem_q_norm, v_mem_k_norm, v_w_branch_a, v_w_branch_b, v_w_branch_c, v_w_out, v_ffn2_norm, v_ffn2_w_gu, v_ffn2_w_down):
    given = dict(x=x, mem=mem, positions=positions, ffn1_norm=ffn1_norm, ffn1_w_gu=ffn1_w_gu, ffn1_w_down=ffn1_w_down, mix_norm=mix_norm, w_in=w_in, b_gate=b_gate, sg_ln_g=sg_ln_g, sg_ln_b=sg_ln_b, sg_w=sg_w, sg_b=sg_b, mla_cq_norm=mla_cq_norm, mla_w_uq=mla_w_uq, mla_ckv_norm=mla_ckv_norm, mla_w_ukv=mla_w_ukv, mla_q_norm=mla_q_norm, mla_k_norm=mla_k_norm, mem_norm=mem_norm, mem_w_kv=mem_w_kv, mem_q_norm=mem_q_norm, mem_k_norm=mem_k_norm, w_branch_a=w_branch_a, w_branch_b=w_branch_b, w_branch_c=w_branch_c, w_out=w_out, ffn2_norm=ffn2_norm, ffn2_w_gu=ffn2_w_gu, ffn2_w_down=ffn2_w_down, loss_target=loss_target, m_ffn1_norm=m_ffn1_norm, m_ffn1_w_gu=m_ffn1_w_gu, m_ffn1_w_down=m_ffn1_w_down, m_mix_norm=m_mix_norm, m_w_in=m_w_in, m_b_gate=m_b_gate, m_sg_ln_g=m_sg_ln_g, m_sg_ln_b=m_sg_ln_b, m_sg_w=m_sg_w, m_sg_b=m_sg_b, m_mla_cq_norm=m_mla_cq_norm, m_mla_w_uq=m_mla_w_uq, m_mla_ckv_norm=m_mla_ckv_norm, m_mla_w_ukv=m_mla_w_ukv, m_mla_q_norm=m_mla_q_norm, m_mla_k_norm=m_mla_k_norm, m_mem_norm=m_mem_norm, m_mem_w_kv=m_mem_w_kv, m_mem_q_norm=m_mem_q_norm, m_mem_k_norm=m_mem_k_norm, m_w_branch_a=m_w_branch_a, m_w_branch_b=m_w_branch_b, m_w_branch_c=m_w_branch_c, m_w_out=m_w_out, m_ffn2_norm=m_ffn2_norm, m_ffn2_w_gu=m_ffn2_w_gu, m_ffn2_w_down=m_ffn2_w_down, v_ffn1_norm=v_ffn1_norm, v_ffn1_w_gu=v_ffn1_w_gu, v_ffn1_w_down=v_ffn1_w_down, v_mix_norm=v_mix_norm, v_w_in=v_w_in, v_b_gate=v_b_gate, v_sg_ln_g=v_sg_ln_g, v_sg_ln_b=v_sg_ln_b, v_sg_w=v_sg_w, v_sg_b=v_sg_b, v_mla_cq_norm=v_mla_cq_norm, v_mla_w_uq=v_mla_w_uq, v_mla_ckv_norm=v_mla_ckv_norm, v_mla_w_ukv=v_mla_w_ukv, v_mla_q_norm=v_mla_q_norm, v_mla_k_norm=v_mla_k_norm, v_mem_norm=v_mem_norm, v_mem_w_kv=v_mem_w_kv, v_mem_q_norm=v_mem_q_norm, v_mem_k_norm=v_mem_k_norm, v_w_branch_a=v_w_branch_a, v_w_branch_b=v_w_branch_b, v_w_branch_c=v_w_branch_c, v_w_out=v_w_out, v_ffn2_norm=v_ffn2_norm, v_ffn2_w_gu=v_ffn2_w_gu, v_ffn2_w_down=v_ffn2_w_down)
    weights = {n: given[n] for n in TWIN_WEIGHTS}
    shared = {n: given[n] for n in SHARED_INPUTS}
    per_example = {n: given[n] for n in ['x', 'mem', 'positions']}
    grad_fn = _jax.value_and_grad(_loss, argnums=(0, 1))

    def one_microbatch(ex, loss_target):
        ex = dict(ex)
        diff = ex.pop(TWIN_DIFF_INPUT)
        return grad_fn(weights, diff, {**shared, **ex}, loss_target)

    if N_MICROBATCH == 1:
        loss, (grad_w, grad_x) = one_microbatch(per_example, given["loss_target"])
    else:
        def body(carry, xs):
            loss_sum, grad_sum = carry
            l_k, (gw_k, gx_k) = one_microbatch(xs[0], xs[1])
            with _jax.named_scope("update"):
                return (loss_sum + l_k, _jax.tree.map(_jnp.add, grad_sum, gw_k)), gx_k

        init = (_jnp.zeros((), _jnp.float32), _jax.tree.map(_jnp.zeros_like, weights))
        (loss, grad_w), grad_x = _jax.lax.scan(body, init, (per_example, given["loss_target"]))
    with _jax.named_scope("update"):
        delta_w, new_m, new_v = {}, {}, {}
        for n in TWIN_WEIGHTS:
            delta_w[n], new_m[n], new_v[n] = _adamw(weights[n], grad_w[n], given["m_" + n], given["v_" + n])
    return (loss, grad_x, *[grad_w[n] for n in TWIN_WEIGHTS], *[delta_w[n] for n in TWIN_WEIGHTS],
            *[new_m[n] for n in TWIN_WEIGHTS], *[new_v[n] for n in TWIN_WEIGHTS])
```

```python
import functools
import math

import numpy as np
import jax
import jax.numpy as jnp
from jax import lax
from jax.experimental import pallas as pl
from jax.experimental.pallas import tpu as pltpu

F32 = jnp.float32
BF16 = jnp.bfloat16

D_MODEL = 1024
D_FF = 2816
FF_TILE = 1408
SG_WIDTH = 512
SG_GROUPS = 8
CHUNK = 128
MLA_HEADS = 8
MLA_QK = 96
MLA_NOPE = 64
MLA_ROPE = 32
MLA_Q_RANK = 384
MLA_KV_RANK = 256
MEM_HEADS = 4
MEM_LEN = 256
LANES = 128
EPS = 1e-6
NEG = -1e30
ROPE_BASE = 10000.0
N_CHIPS = 4
N_DEV = 8

ADAM_LR = 0.001
ADAM_B1 = 0.9
ADAM_B2 = 0.999
ADAM_EPS = 1e-08
ADAM_WD = 0.01
ADAM_STEP = 10

COL_V = 512
COL_CQ = 1024
COL_CKV = 1408
COL_KR = 1664
COL_QM = 1696
COL_GATE = 2208
IN_COLS = 5280

VMEM_LIMIT_BYTES = 56 * 1024 * 1024
INV_SQRT2 = 0.7071067811865476
INV_SQRT_2PI = 0.3989422804014327

SHARDED = (
    ("ffn1_w_gu", 1024, 1408, "col"),
    ("ffn1_w_down", 704, 1024, "row"),
    ("w_in", 1024, 1320, "col"),
    ("mla_w_uq", 384, 192, "col"),
    ("mla_w_ukv", 256, 256, "col"),
    ("mem_w_kv", 256, 1024, "row"),
    ("w_branch_a", 512, 256, "col"),
    ("w_branch_b", 512, 256, "col"),
    ("w_branch_c", 512, 256, "col"),
    ("w_out", 256, 1024, "row"),
    ("ffn2_w_gu", 1024, 1408, "col"),
    ("ffn2_w_down", 704, 1024, "row"),
)
SMALL = (
    ("ffn1_norm", (1, 1024)), ("mix_norm", (1, 1024)), ("b_gate", (1, 3072)),
    ("sg_ln_g", (1, 512)), ("sg_ln_b", (1, 512)), ("sg_w", (1, 8, 128, 128)),
    ("sg_b", (1, 8, 128)), ("mla_cq_norm", (1, 384)), ("mla_ckv_norm", (1, 256)),
    ("mla_q_norm", (1, 96)), ("mla_k_norm", (1, 96)), ("mem_norm", (1, 1024)),
    ("mem_q_norm", (1, 128)), ("mem_k_norm", (1, 128)), ("ffn2_norm", (1, 1024)),
)
WEIGHT_ORDER = (
    "ffn1_norm", "ffn1_w_gu", "ffn1_w_down", "mix_norm", "w_in", "b_gate", "sg_ln_g", "sg_ln_b",
    "sg_w", "sg_b", "mla_cq_norm", "mla_w_uq", "mla_ckv_norm", "mla_w_ukv", "mla_q_norm",
    "mla_k_norm", "mem_norm", "mem_w_kv", "mem_q_norm", "mem_k_norm", "w_branch_a", "w_branch_b",
    "w_branch_c", "w_out", "ffn2_norm", "ffn2_w_gu", "ffn2_w_down",
)

PACK_COLS = 1024
PACK_ROW_ALIGN = 256
_N_SHARD = sum(r * c for _, r, c, _ in SHARDED)
PACK_HALF_ROWS = -(-(_N_SHARD // PACK_COLS) // (2 * PACK_ROW_ALIGN)) * PACK_ROW_ALIGN
PACK_N = 2 * PACK_HALF_ROWS * PACK_COLS
_N_SMALL = sum(int(np.prod(s)) for _, s in SMALL)
SMALL_ROWS = -(-_N_SMALL // (LANES * 8)) * 8

MESH = pl.DeviceIdType.MESH


def _cparams():
    return pltpu.CompilerParams(vmem_limit_bytes=VMEM_LIMIT_BYTES)


def _dot(a, b):
    return jnp.dot(a, b, preferred_element_type=F32)


def _dot_nt(a, b):
    return lax.dot_general(a, b, (((1,), (1,)), ((), ())), preferred_element_type=F32)


def _dot_tn(a, b):
    return lax.dot_general(a, b, (((0,), (0,)), ((), ())), preferred_element_type=F32)


def _gelu(x):
    return 0.5 * x * (1.0 + lax.erf(x * INV_SQRT2))


def _gelu_grad(x):
    return 0.5 * (1.0 + lax.erf(x * INV_SQRT2)) + x * jnp.exp(-0.5 * x * x) * INV_SQRT_2PI


def _rstd(x, n):
    return lax.rsqrt(jnp.sum(x * x, axis=-1, keepdims=True) * (1.0 / n) + EPS)


def _rms_vjp(x, r, g, dy, n):
    dxh = dy * g
    dx = r * dxh - x * (r * r * r) * (jnp.sum(dxh * x, axis=-1, keepdims=True) * (1.0 / n))
    return dx, dy * x * r


def _row_tile(t, want):
    return min(t, want)


def _wide_tile(n):
    if n <= 1024:
        return n
    if n % 1024 == 0:
        return 1024
    assert n % FF_TILE == 0, n
    return FF_TILE


def _rms_fwd(x, g, name):
    t, d = x.shape
    tm = _row_tile(t, 512)

    def body(x_ref, g_ref, o_ref):
        xv = x_ref[...]
        o_ref[...] = (xv * _rstd(xv, d) * g_ref[...]).astype(BF16)

    return pl.pallas_call(
        body, name=name, grid=(t // tm,),
        in_specs=[pl.BlockSpec((tm, d), lambda i: (i, 0)), pl.BlockSpec((1, d), lambda i: (0, 0))],
        out_specs=pl.BlockSpec((tm, d), lambda i: (i, 0)),
        out_shape=jax.ShapeDtypeStruct((t, d), BF16), compiler_params=_cparams())(x, g)


def _rms_bwd(x, g, dxn, dres, name):
    t, d = x.shape
    tm = _row_tile(t, 256)

    def body(x_ref, g_ref, d_ref, r_ref, dx_ref, dg_ref):
        @pl.when(pl.program_id(0) == 0)
        def _():
            dg_ref[...] = jnp.zeros_like(dg_ref)

        xv = x_ref[...]
        r = _rstd(xv, d)
        dx, dgr = _rms_vjp(xv, r, g_ref[...], d_ref[...].astype(F32), d)
        dx_ref[...] = r_ref[...] + dx
        dg_ref[...] += jnp.sum(dgr, axis=0, keepdims=True)

    row = pl.BlockSpec((tm, d), lambda i: (i, 0))
    vec = pl.BlockSpec((1, d), lambda i: (0, 0))
    return pl.pallas_call(
        body, name=name, grid=(t // tm,), in_specs=[row, vec, row, row], out_specs=[row, vec],
        out_shape=[jax.ShapeDtypeStruct((t, d), F32), jax.ShapeDtypeStruct((1, d), F32)],
        compiler_params=_cparams())(x, g, dxn, dres)


def _mm(pairs, out_dtype, name):
    t = pairs[0][0].shape[0]
    n = pairs[0][1].shape[1]
    tm = _row_tile(t, 512)
    tn = _wide_tile(n)
    np_ = len(pairs)

    def body(*refs):
        o_ref = refs[2 * np_]
        acc = None
        for a_ref, w_ref in zip(refs[:np_], refs[np_:2 * np_]):
            part = _dot(a_ref[...].astype(BF16), w_ref[...])
            acc = part if acc is None else acc + part
        o_ref[...] = acc.astype(out_dtype)

    in_specs = [pl.BlockSpec((tm, a.shape[1]), lambda i, j: (i, 0)) for a, _ in pairs]
    in_specs += [pl.BlockSpec((w.shape[0], tn), lambda i, j: (0, j)) for _, w in pairs]
    return pl.pallas_call(
        body, name=name, grid=(t // tm, n // tn), in_specs=in_specs,
        out_specs=pl.BlockSpec((tm, tn), lambda i, j: (i, j)),
        out_shape=jax.ShapeDtypeStruct((t, n), out_dtype), compiler_params=_cparams(),
    )(*[a for a, _ in pairs], *[w for _, w in pairs])


def _mm_tn(a, b, name, scale=1.0):
    t, m = a.shape
    n = b.shape[1]
    tm, tn = _wide_tile(m), _wide_tile(n)
    tk = _row_tile(t, 512)
    nk = t // tk

    def body(a_ref, b_ref, o_ref):
        k = pl.program_id(2)

        @pl.when(k == 0)
        def _():
            o_ref[...] = jnp.zeros_like(o_ref)

        o_ref[...] += _dot_tn(a_ref[...].astype(BF16), b_ref[...].astype(BF16))
        if scale != 1.0:
            @pl.when(k == nk - 1)
            def _():
                o_ref[...] = o_ref[...] * scale

    return pl.pallas_call(
        body, name=name, grid=(m // tm, n // tn, nk),
        in_specs=[pl.BlockSpec((tk, tm), lambda i, j, k: (k, i)),
                  pl.BlockSpec((tk, tn), lambda i, j, k: (k, j))],
        out_specs=pl.BlockSpec((tm, tn), lambda i, j, k: (i, j)),
        out_shape=jax.ShapeDtypeStruct((m, n), F32), compiler_params=_cparams())(a, b)


def _ffn_fwd(x, g, wgu4, wd2, name):
    t, d = x.shape
    tm = _row_tile(t, 512)

    def body(x_ref, g_ref, wg_ref, wu_ref, wd_ref, o_ref, xn_scr, acc_scr):
        j = pl.program_id(1)

        @pl.when(j == 0)
        def _():
            xv = x_ref[...]
            xn_scr[...] = (xv * _rstd(xv, d) * g_ref[...]).astype(BF16)
            acc_scr[...] = jnp.zeros_like(acc_scr)

        xn = xn_scr[...]
        gg = _dot(xn, wg_ref[0])
        uu = _dot(xn, wu_ref[0])
        act = gg * jax.nn.sigmoid(gg) * uu
        acc_scr[...] += _dot(act.astype(BF16), wd_ref[0])

        @pl.when(j == 1)
        def _():
            o_ref[...] = x_ref[...] + 0.5 * acc_scr[...]

    row = pl.BlockSpec((tm, d), lambda i, j: (i, 0))
    return pl.pallas_call(
        body, name=name, grid=(t // tm, 2),
        in_specs=[row, pl.BlockSpec((1, d), lambda i, j: (0, 0)),
                  pl.BlockSpec((1, d, FF_TILE), lambda i, j: (j, 0, 0)),
                  pl.BlockSpec((1, d, FF_TILE), lambda i, j: (j + 2, 0, 0)),
                  pl.BlockSpec((1, FF_TILE, d), lambda i, j: (j, 0, 0))],
        out_specs=row, out_shape=jax.ShapeDtypeStruct((t, d), F32),
        scratch_shapes=[pltpu.VMEM((tm, d), BF16), pltpu.VMEM((tm, d), F32)],
        compiler_params=_cparams())(x, g, wgu4, wgu4, wd2)


def _ffn_bwd(x, g, dy, wgu4, wd2, name):
    t, d = x.shape
    tm = _row_tile(t, 256)

    def body(x_ref, g_ref, dy_ref, wg_ref, wu_ref, wd_ref,
             dx_ref, dgain_ref, xn_ref, dg_ref, du_ref, act_ref, acc_scr):
        i, j = pl.program_id(0), pl.program_id(1)

        @pl.when((i == 0) & (j == 0))
        def _():
            dgain_ref[...] = jnp.zeros_like(dgain_ref)

        xv = x_ref[...]
        r = _rstd(xv, d)
        xn = (xv * r * g_ref[...]).astype(BF16)

        @pl.when(j == 0)
        def _():
            xn_ref[...] = xn
            acc_scr[...] = jnp.zeros_like(acc_scr)

        gg = _dot(xn, wg_ref[0])
        uu = _dot(xn, wu_ref[0])
        sg = jax.nn.sigmoid(gg)
        silu = gg * sg
        act_ref[...] = (silu * uu).astype(BF16)
        dyh = (0.5 * dy_ref[...]).astype(BF16)
        dact = _dot_nt(dyh, wd_ref[0])
        du = (dact * silu).astype(BF16)
        dgt = (dact * uu * (sg * (1.0 + gg * (1.0 - sg)))).astype(BF16)
        du_ref[...] = du
        dg_ref[...] = dgt
        acc_scr[...] += _dot_nt(dgt, wg_ref[0]) + _dot_nt(du, wu_ref[0])

        @pl.when(j == 1)
        def _():
            dx, dgr = _rms_vjp(xv, r, g_ref[...], acc_scr[...], d)
            dx_ref[...] = dy_ref[...] + dx
            dgain_ref[...] += jnp.sum(dgr, axis=0, keepdims=True)

    row = pl.BlockSpec((tm, d), lambda i, j: (i, 0))
    vec = pl.BlockSpec((1, d), lambda i, j: (0, 0))
    ffb = pl.BlockSpec((tm, FF_TILE), lambda i, j: (i, j))
    return pl.pallas_call(
        body, name=name, grid=(t // tm, 2),
        in_specs=[row, vec, row,
                  pl.BlockSpec((1, d, FF_TILE), lambda i, j: (j, 0, 0)),
                  pl.BlockSpec((1, d, FF_TILE), lambda i, j: (j + 2, 0, 0)),
                  pl.BlockSpec((1, FF_TILE, d), lambda i, j: (j, 0, 0))],
        out_specs=[row, vec, row, ffb, ffb, ffb],
        out_shape=[jax.ShapeDtypeStruct((t, d), F32), jax.ShapeDtypeStruct((1, d), F32),
                   jax.ShapeDtypeStruct((t, d), BF16), jax.ShapeDtypeStruct((t, D_FF), BF16),
                   jax.ShapeDtypeStruct((t, D_FF), BF16), jax.ShapeDtypeStruct((t, D_FF), BF16)],
        scratch_shapes=[pltpu.VMEM((tm, d), F32)],
        compiler_params=_cparams())(x, g, dy, wgu4, wgu4, wd2)


def _loss_head(y, tgt, name):
    t, d = y.shape
    tm = _row_tile(t, 512)

    def body(y_ref, t_ref, dy_ref, loss_ref):
        @pl.when(pl.program_id(0) == 0)
        def _():
            loss_ref[...] = jnp.zeros_like(loss_ref)

        e = y_ref[...] - t_ref[...]
        dy_ref[...] = e * (1.0 / d)
        part = 0.5 * jnp.sum(jnp.sum(e * e, axis=-1, keepdims=True) * (1.0 / d), axis=0, keepdims=True)
        loss_ref[...] += jnp.broadcast_to(part, loss_ref.shape)

    row = pl.BlockSpec((tm, d), lambda i: (i, 0))
    return pl.pallas_call(
        body, name=name, grid=(t // tm,), in_specs=[row, row],
        out_specs=[row, pl.BlockSpec((1, LANES), lambda i: (0, 0))],
        out_shape=[jax.ShapeDtypeStruct((t, d), F32), jax.ShapeDtypeStruct((1, LANES), F32)],
        compiler_params=_cparams())(y, tgt)


def _sgu_layernorm(vpre, lg, lb):
    v = _gelu(vpre)
    mu = jnp.mean(v, axis=-1, keepdims=True)
    xc = v - mu
    rstd = lax.rsqrt(jnp.mean(xc * xc, axis=-1, keepdims=True) + EPS)
    xhat = xc * rstd
    return xhat, rstd, xhat * lg + lb


def _sgu_fwd(zuv, lg, lb, wt, bias_l, name):
    t = zuv.shape[0]
    tm = _row_tile(t, 512)

    def body(u_ref, v_ref, lg_ref, lb_ref, wt_ref, bl_ref, o_ref, vln_scr):
        _, _, vln = _sgu_layernorm(v_ref[...], lg_ref[...], lb_ref[...])
        vln_scr[...] = vln.astype(BF16)
        lo = lax.broadcasted_iota(jnp.int32, (CHUNK, LANES), 1) < 64
        for c in range(tm // CHUNK):
            rows = slice(c * CHUNK, (c + 1) * CHUNK)
            for p in range(SG_GROUPS // 2):
                cols = slice(p * LANES, (p + 1) * LANES)
                vp = vln_scr[rows, cols]
                mixed = jnp.where(lo, _dot(wt_ref[2 * p], vp), _dot(wt_ref[2 * p + 1], vp)) + bl_ref[:, cols]
                o_ref[rows, cols] = (_gelu(u_ref[rows, cols]) * mixed).astype(BF16)

    half = lambda k: pl.BlockSpec((tm, SG_WIDTH), lambda i: (i, k))
    vec = pl.BlockSpec((1, SG_WIDTH), lambda i: (0, 0))
    return pl.pallas_call(
        body, name=name, grid=(t // tm,),
        in_specs=[half(0), half(1), vec, vec,
                  pl.BlockSpec((SG_GROUPS, CHUNK, CHUNK), lambda i: (0, 0, 0)),
                  pl.BlockSpec((CHUNK, SG_WIDTH), lambda i: (0, 0))],
        out_specs=pl.BlockSpec((tm, SG_WIDTH), lambda i: (i, 0)),
        out_shape=jax.ShapeDtypeStruct((t, SG_WIDTH), BF16),
        scratch_shapes=[pltpu.VMEM((tm, SG_WIDTH), BF16)],
        compiler_params=_cparams())(zuv, zuv, lg, lb, wt, bias_l)


def _sgu_bwd(zuv, dya, lg, lb, wt, wt_t, bias_l, name):
    t = zuv.shape[0]
    tm = _row_tile(t, 256)
    nsteps = t // tm

    def body(u_ref, v_ref, dy_ref, lg_ref, lb_ref, wt_ref, wtt_ref, bl_ref,
             dz_ref, dwt_ref, dbl_ref, dlg_ref, dlb_ref, vln_scr, dvln_scr, dbacc_scr):
        step = pl.program_id(0)

        @pl.when(step == 0)
        def _():
            dwt_ref[...] = jnp.zeros_like(dwt_ref)
            dlg_ref[...] = jnp.zeros_like(dlg_ref)
            dlb_ref[...] = jnp.zeros_like(dlb_ref)
            dbl_ref[...] = jnp.zeros_like(dbl_ref)
            dbacc_scr[...] = jnp.zeros_like(dbacc_scr)

        vpre = v_ref[...]
        lgv = lg_ref[...]
        xhat, rstd, vln = _sgu_layernorm(vpre, lgv, lb_ref[...])
        vln_scr[...] = vln.astype(BF16)
        lo = lax.broadcasted_iota(jnp.int32, (CHUNK, LANES), 1) < 64
        for c in range(tm // CHUNK):
            rows = slice(c * CHUNK, (c + 1) * CHUNK)
            for p in range(SG_GROUPS // 2):
                cols = slice(p * LANES, (p + 1) * LANES)
                vp = vln_scr[rows, cols]
                mixed = jnp.where(lo, _dot(wt_ref[2 * p], vp), _dot(wt_ref[2 * p + 1], vp)) + bl_ref[:, cols]
                upre = u_ref[rows, cols]
                dyp = dy_ref[rows, cols]
                dz_ref[rows, cols] = (dyp * mixed * _gelu_grad(upre)).astype(BF16)
                dm = dyp * _gelu(upre)
                dbacc_scr[:, cols] += dm
                dlo = jnp.where(lo, dm, 0.0).astype(BF16)
                dhi = jnp.where(lo, 0.0, dm).astype(BF16)
                dvln_scr[rows, cols] = _dot(wtt_ref[2 * p], dlo) + _dot(wtt_ref[2 * p + 1], dhi)
                dwt_ref[2 * p] += _dot_nt(dlo, vp)
                dwt_ref[2 * p + 1] += _dot_nt(dhi, vp)
        dvln = dvln_scr[...]
        dlg_ref[...] += jnp.sum(dvln * xhat, axis=0, keepdims=True)
        dlb_ref[...] += jnp.sum(dvln, axis=0, keepdims=True)
        dxh = dvln * lgv
        dv = rstd * (dxh - jnp.mean(dxh, axis=-1, keepdims=True)
                     - xhat * jnp.mean(dxh * xhat, axis=-1, keepdims=True))
        dz_ref[:, SG_WIDTH:] = (dv * _gelu_grad(vpre)).astype(BF16)

        @pl.when(step == nsteps - 1)
        def _():
            rr = lax.broadcasted_iota(jnp.int32, (CHUNK, CHUNK), 0)
            cc = lax.broadcasted_iota(jnp.int32, (CHUNK, CHUNK), 1)
            tril = (cc <= rr).astype(F32)
            for gidx in range(SG_GROUPS):
                dwt_ref[gidx] = dwt_ref[gidx] * tril
            kk = lax.broadcasted_iota(jnp.int32, (SG_WIDTH, LANES), 0)
            gg = lax.broadcasted_iota(jnp.int32, (SG_WIDTH, LANES), 1)
            sel = ((kk // 64) == gg).astype(F32)
            dbl_ref[...] = jnp.dot(dbacc_scr[...], sel, preferred_element_type=F32,
                                   precision=lax.Precision.HIGHEST)

    half = lambda k: pl.BlockSpec((tm, SG_WIDTH), lambda i: (i, k))
    vec = pl.BlockSpec((1, SG_WIDTH), lambda i: (0, 0))
    wspec = pl.BlockSpec((SG_GROUPS, CHUNK, CHUNK), lambda i: (0, 0, 0))
    return pl.pallas_call(
        body, name=name, grid=(nsteps,),
        in_specs=[half(0), half(1), pl.BlockSpec((tm, SG_WIDTH), lambda i: (i, 0)), vec, vec,
                  wspec, wspec, pl.BlockSpec((CHUNK, SG_WIDTH), lambda i: (0, 0))],
        out_specs=[pl.BlockSpec((tm, 2 * SG_WIDTH), lambda i: (i, 0)), wspec,
                   pl.BlockSpec((CHUNK, LANES), lambda i: (0, 0)), vec, vec],
        out_shape=[jax.ShapeDtypeStruct((t, 2 * SG_WIDTH), BF16),
                   jax.ShapeDtypeStruct((SG_GROUPS, CHUNK, CHUNK), F32),
                   jax.ShapeDtypeStruct((CHUNK, LANES), F32),
                   jax.ShapeDtypeStruct((1, SG_WIDTH), F32), jax.ShapeDtypeStruct((1, SG_WIDTH), F32)],
        scratch_shapes=[pltpu.VMEM((tm, SG_WIDTH), BF16), pltpu.VMEM((tm, SG_WIDTH), F32),
                        pltpu.VMEM((CHUNK, SG_WIDTH), F32)],
        compiler_params=_cparams())(zuv, zuv, dya, lg, lb, wt, wt_t, bias_l)


def _rope(x, c, s1, s2):
    return x * c + pltpu.roll(x, LANES - 16, 1) * s1 + pltpu.roll(x, 16, 1) * s2


def _rope_t(dy, c, s1, s2):
    return dy * c + pltpu.roll(dy * s1, 16, 1) + pltpu.roll(dy * s2, LANES - 16, 1)


def _mla_prep_fwd(zcq, zckv, zkr, gcq, gckv, qg, kg, wuq, wuk, wuv, rc, rs1, rs2, name):
    t = zcq.shape[0]
    tm = _row_tile(t, 256)
    hd = MLA_HEADS * LANES

    def body(zcq_ref, zckv_ref, zkr_ref, gcq_ref, gckv_ref, qg_ref, kg_ref, wuq_ref, wuk_ref, wuv_ref,
             c_ref, s1_ref, s2_ref, q_ref, k_ref, v_ref, cqn_ref, ckvn_ref):
        c, s1, s2 = c_ref[...], s1_ref[...], s2_ref[...]
        xq = zcq_ref[...]
        cqn = (xq * _rstd(xq, MLA_Q_RANK) * gcq_ref[...]).astype(BF16)
        cqn_ref[...] = cqn
        ql = _dot(cqn, wuq_ref[...])
        xk = zckv_ref[...]
        ckvn = (xk * _rstd(xk, MLA_KV_RANK) * gckv_ref[...]).astype(BF16)
        ckvn_ref[...] = ckvn
        kl = _dot(ckvn, wuk_ref[...])
        v_ref[...] = _dot(ckvn, wuv_ref[...]).astype(BF16)
        kr = zkr_ref[...]
        for h in range(MLA_HEADS):
            sl = slice(h * LANES, (h + 1) * LANES)
            qh = ql[:, sl]
            q_ref[:, sl] = _rope(qh * _rstd(qh, MLA_QK) * qg_ref[...], c, s1, s2).astype(BF16)
            kh = kl[:, sl] + kr
            k_ref[:, sl] = _rope(kh * _rstd(kh, MLA_QK) * kg_ref[...], c, s1, s2).astype(BF16)

    row = lambda n: pl.BlockSpec((tm, n), lambda i: (i, 0))
    full = lambda a: pl.BlockSpec(a.shape, lambda i: (0, 0))
    return pl.pallas_call(
        body, name=name, grid=(t // tm,),
        in_specs=[row(MLA_Q_RANK), row(MLA_KV_RANK), row(LANES), full(gcq), full(gckv), full(qg), full(kg),
                  full(wuq), full(wuk), full(wuv), row(LANES), row(LANES), row(LANES)],
        out_specs=[row(hd), row(hd), row(hd), row(MLA_Q_RANK), row(MLA_KV_RANK)],
        out_shape=[jax.ShapeDtypeStruct((t, hd), BF16)] * 3
        + [jax.ShapeDtypeStruct((t, MLA_Q_RANK), BF16), jax.ShapeDtypeStruct((t, MLA_KV_RANK), BF16)],
        compiler_params=_cparams(),
    )(zcq, zckv, zkr, gcq, gckv, qg, kg, wuq, wuk, wuv, rc, rs1, rs2)


def _mla_prep_bwd(zcq, zckv, zkr, gcq, gckv, qg, kg, wuq, wuk, wuv, rc, rs1, rs2, dq, dk, dv, name):
    t = zcq.shape[0]
    tm = _row_tile(t, 256)
    hd = MLA_HEADS * LANES

    def body(zcq_ref, zckv_ref, zkr_ref, gcq_ref, gckv_ref, qg_ref, kg_ref, wuq_ref, wuk_ref, wuv_ref,
             c_ref, s1_ref, s2_ref, dq_ref, dk_ref, dv_ref,
             dzcq_ref, dzckv_ref, dzkr_ref, dql_ref, dkl_ref, dgcq_ref, dgckv_ref, dqg_ref, dkg_ref):
        @pl.when(pl.program_id(0) == 0)
        def _():
            for ref in (dgcq_ref, dgckv_ref, dqg_ref, dkg_ref):
                ref[...] = jnp.zeros_like(ref)

        c, s1, s2 = c_ref[...], s1_ref[...], s2_ref[...]
        qgv, kgv = qg_ref[...], kg_ref[...]
        xq = zcq_ref[...]
        rq = _rstd(xq, MLA_Q_RANK)
        ql = _dot((xq * rq * gcq_ref[...]).astype(BF16), wuq_ref[...])
        xk = zckv_ref[...]
        rk = _rstd(xk, MLA_KV_RANK)
        kl = _dot((xk * rk * gckv_ref[...]).astype(BF16), wuk_ref[...])
        kr = zkr_ref[...]
        dqg_acc = jnp.zeros((tm, LANES), F32)
        dkg_acc = jnp.zeros((tm, LANES), F32)
        dkr = jnp.zeros((tm, LANES), F32)
        for h in range(MLA_HEADS):
            sl = slice(h * LANES, (h + 1) * LANES)
            qh = ql[:, sl]
            dqh, dgr = _rms_vjp(qh, _rstd(qh, MLA_QK), qgv, _rope_t(dq_ref[:, sl], c, s1, s2), MLA_QK)
            dql_ref[:, sl] = dqh.astype(BF16)
            dqg_acc += dgr
            kh = kl[:, sl] + kr
            dkh, dgr = _rms_vjp(kh, _rstd(kh, MLA_QK), kgv, _rope_t(dk_ref[:, sl], c, s1, s2), MLA_QK)
            dkl_ref[:, sl] = dkh.astype(BF16)
            dkg_acc += dgr
            dkr += dkh
        dqg_ref[...] += jnp.sum(dqg_acc, axis=0, keepdims=True)
        dkg_ref[...] += jnp.sum(dkg_acc, axis=0, keepdims=True)
        lane = lax.broadcasted_iota(jnp.int32, (tm, LANES), 1)
        dzkr_ref[...] = jnp.where((lane >= MLA_NOPE) & (lane < MLA_QK), dkr, 0.0).astype(BF16)
        dcqn = _dot_nt(dql_ref[...], wuq_ref[...])
        dx, dgr = _rms_vjp(xq, rq, gcq_ref[...], dcqn, MLA_Q_RANK)
        dzcq_ref[...] = dx.astype(BF16)
        dgcq_ref[...] += jnp.sum(dgr, axis=0, keepdims=True)
        dckvn = _dot_nt(dkl_ref[...], wuk_ref[...]) + _dot_nt(dv_ref[...].astype(BF16), wuv_ref[...])
        dx, dgr = _rms_vjp(xk, rk, gckv_ref[...], dckvn, MLA_KV_RANK)
        dzckv_ref[...] = dx.astype(BF16)
        dgckv_ref[...] += jnp.sum(dgr, axis=0, keepdims=True)

    row = lambda n: pl.BlockSpec((tm, n), lambda i: (i, 0))
    full = lambda a: pl.BlockSpec(a.shape, lambda i: (0, 0))
    vec = lambda n: pl.BlockSpec((1, n), lambda i: (0, 0))
    return pl.pallas_call(
        body, name=name, grid=(t // tm,),
        in_specs=[row(MLA_Q_RANK), row(MLA_KV_RANK), row(LANES), full(gcq), full(gckv), full(qg), full(kg),
                  full(wuq), full(wuk), full(wuv), row(LANES), row(LANES), row(LANES), row(hd), row(hd), row(hd)],
        out_specs=[row(MLA_Q_RANK), row(MLA_KV_RANK), row(LANES), row(hd), row(hd),
                   vec(MLA_Q_RANK), vec(MLA_KV_RANK), vec(LANES), vec(LANES)],
        out_shape=[jax.ShapeDtypeStruct((t, MLA_Q_RANK), BF16), jax.ShapeDtypeStruct((t, MLA_KV_RANK), BF16),
                   jax.ShapeDtypeStruct((t, LANES), BF16), jax.ShapeDtypeStruct((t, hd), BF16),
                   jax.ShapeDtypeStruct((t, hd), BF16), jax.ShapeDtypeStruct((1, MLA_Q_RANK), F32),
                   jax.ShapeDtypeStruct((1, MLA_KV_RANK), F32), jax.ShapeDtypeStruct((1, LANES), F32),
                   jax.ShapeDtypeStruct((1, LANES), F32)],
        compiler_params=_cparams(),
    )(zcq, zckv, zkr, gcq, gckv, qg, kg, wuq, wuk, wuv, rc, rs1, rs2, dq, dk, dv)


def _attn_tile(t):
    return 512 if t >= 2048 else 128


def _causal_mask(tile):
    row = lax.broadcasted_iota(jnp.int32, (tile, tile), 0)
    col = lax.broadcasted_iota(jnp.int32, (tile, tile), 1)
    return col <= row


def _attn_fwd(q, k, v, name):
    t, hd = q.shape
    heads = hd // LANES
    tile = _attn_tile(t)
    n = t // tile
    ii = np.array([i for i in range(n) for _ in range(i + 1)], np.int32)
    jj = np.array([j for i in range(n) for j in range(i + 1)], np.int32)
    scale = MLA_QK ** -0.5

    def body(ii_ref, jj_ref, q_ref, k_ref, v_ref, o_ref, lse_ref, m_scr, l_scr, acc_scr):
        s_id = pl.program_id(1)
        i, j = ii_ref[s_id], jj_ref[s_id]

        @pl.when(j == 0)
        def _():
            m_scr[...] = jnp.full_like(m_scr, NEG)
            l_scr[...] = jnp.zeros_like(l_scr)
            acc_scr[...] = jnp.zeros_like(acc_scr)

        def step(masked):
            s = _dot_nt(q_ref[...], k_ref[...]) * scale
            if masked:
                s = jnp.where(_causal_mask(tile), s, NEG)
            m_prev = m_scr[...]
            m_new = jnp.maximum(m_prev, jnp.max(s, axis=1, keepdims=True))
            p = jnp.exp(s - m_new)
            alpha = jnp.exp(m_prev - m_new)
            l_scr[...] = alpha * l_scr[...] + jnp.sum(p, axis=1, keepdims=True)
            acc_scr[...] = alpha * acc_scr[...] + _dot(p.astype(BF16), v_ref[...])
            m_scr[...] = m_new

        @pl.when(j < i)
        def _():
            step(False)

        @pl.when(j == i)
        def _():
            step(True)
            l = l_scr[...]
            o_ref[...] = (acc_scr[...] / l).astype(BF16)
            lse_ref[...] = jnp.broadcast_to(m_scr[...] + jnp.log(l), (tile, LANES))

    qspec = pl.BlockSpec((tile, LANES), lambda h, s, ii_r, jj_r: (ii_r[s], h))
    kspec = pl.BlockSpec((tile, LANES), lambda h, s, ii_r, jj_r: (jj_r[s], h))
    return pl.pallas_call(
        body, name=name,
        grid_spec=pltpu.PrefetchScalarGridSpec(
            num_scalar_prefetch=2, grid=(heads, len(ii)), in_specs=[qspec, kspec, kspec],
            out_specs=[qspec, qspec],
            scratch_shapes=[pltpu.VMEM((tile, 1), F32), pltpu.VMEM((tile, 1), F32),
                            pltpu.VMEM((tile, LANES), F32)]),
        out_shape=[jax.ShapeDtypeStruct((t, hd), BF16), jax.ShapeDtypeStruct((t, hd), F32)],
        compiler_params=_cparams())(jnp.asarray(ii), jnp.asarray(jj), q, k, v)


def _attn_bwd(q, k, v, o, lse, do, name):
    t, hd = q.shape
    heads = hd // LANES
    tile = _attn_tile(t)
    n = t // tile
    jj = np.array([j for j in range(n) for _ in range(j, n)], np.int32)
    ii = np.array([i for j in range(n) for i in range(j, n)], np.int32)
    scale = MLA_QK ** -0.5

    def body(jj_ref, ii_ref, q_ref, k_ref, v_ref, o_ref, lse_ref, do_ref, dq_ref, dk_ref, dv_ref, dk_scr, dv_scr):
        s_id = pl.program_id(1)
        i, j = ii_ref[s_id], jj_ref[s_id]

        @pl.when(s_id == 0)
        def _():
            dq_ref[...] = jnp.zeros_like(dq_ref)

        @pl.when(i == j)
        def _():
            dk_scr[...] = jnp.zeros_like(dk_scr)
            dv_scr[...] = jnp.zeros_like(dv_scr)

        qv, kv, dov = q_ref[...], k_ref[...], do_ref[...]
        s = _dot_nt(qv, kv) * scale
        s = jnp.where(_causal_mask(tile) | (j < i), s, NEG)
        p = jnp.exp(s - jnp.max(lse_ref[...], axis=1, keepdims=True))
        delta = jnp.sum(dov.astype(F32) * o_ref[...].astype(F32), axis=1, keepdims=True)
        dv_scr[...] += _dot_tn(p.astype(BF16), dov)
        dp = _dot_nt(dov, v_ref[...])
        ds = (p * (dp - delta) * scale).astype(BF16)
        dk_scr[...] += _dot_tn(ds, qv)
        rows = pl.ds(pl.multiple_of(i * tile, tile), tile)
        dq_ref[rows, :] += _dot(ds, kv)

        @pl.when(i == n - 1)
        def _():
            dk_ref[...] = dk_scr[...]
            dv_ref[...] = dv_scr[...]

    qspec = pl.BlockSpec((tile, LANES), lambda h, s, jj_r, ii_r: (ii_r[s], h))
    kspec = pl.BlockSpec((tile, LANES), lambda h, s, jj_r, ii_r: (jj_r[s], h))
    return pl.pallas_call(
        body, name=name,
        grid_spec=pltpu.PrefetchScalarGridSpec(
            num_scalar_prefetch=2, grid=(heads, len(ii)),
            in_specs=[qspec, kspec, kspec, qspec, qspec, qspec],
            out_specs=[pl.BlockSpec((t, LANES), lambda h, s, jj_r, ii_r: (0, h)), kspec, kspec],
            scratch_shapes=[pltpu.VMEM((tile, LANES), F32), pltpu.VMEM((tile, LANES), F32)]),
        out_shape=[jax.ShapeDtypeStruct((t, hd), F32)] * 3,
        compiler_params=_cparams())(jnp.asarray(jj), jnp.asarray(ii), q, k, v, o, lse, do)


MEM_W = MEM_HEADS * LANES


def _mem_kv_fwd(mem, gmem, wkv, kg, name):
    m, d = mem.shape

    def body(mem_ref, g_ref, w_ref, kg_ref, k_ref, v_ref, mn_ref):
        xv = mem_ref[...]
        mn = (xv * _rstd(xv, d) * g_ref[...]).astype(BF16)
        mn_ref[...] = mn
        kvm = _dot(mn, w_ref[...])
        v_ref[...] = kvm[:, MEM_W:].astype(BF16)
        for h in range(MEM_HEADS):
            sl = slice(h * LANES, (h + 1) * LANES)
            kh = kvm[:, sl]
            k_ref[:, sl] = (kh * _rstd(kh, LANES) * kg_ref[...]).astype(BF16)

    full = lambda a: pl.BlockSpec(a.shape, lambda i: (0, 0))
    return pl.pallas_call(
        body, name=name, grid=(1,), in_specs=[full(mem), full(gmem), full(wkv), full(kg)],
        out_specs=[pl.BlockSpec((m, MEM_W), lambda i: (0, 0)), pl.BlockSpec((m, MEM_W), lambda i: (0, 0)),
                   pl.BlockSpec((m, d), lambda i: (0, 0))],
        out_shape=[jax.ShapeDtypeStruct((m, MEM_W), BF16), jax.ShapeDtypeStruct((m, MEM_W), BF16),
                   jax.ShapeDtypeStruct((m, d), BF16)],
        compiler_params=_cparams())(mem, gmem, wkv, kg)


def _mem_softmax(qn, kh):
    s = _dot_nt(qn, kh) * (LANES ** -0.5)
    e = jnp.exp(s - jnp.max(s, axis=1, keepdims=True))
    return e / jnp.sum(e, axis=1, keepdims=True)


def _mem_attn_fwd(zqm, qg, km, vm, name):
    t = zqm.shape[0]
    tm = _row_tile(t, 512)

    def body(q_ref, qg_ref, k_ref, v_ref, o_ref):
        for h in range(MEM_HEADS):
            sl = slice(h * LANES, (h + 1) * LANES)
            qh = q_ref[:, sl]
            qn = (qh * _rstd(qh, LANES) * qg_ref[...]).astype(BF16)
            p = _mem_softmax(qn, k_ref[:, sl])
            o_ref[:, sl] = _dot(p.astype(BF16), v_ref[:, sl]).astype(BF16)

    row = pl.BlockSpec((tm, MEM_W), lambda i: (i, 0))
    full = lambda a: pl.BlockSpec(a.shape, lambda i: (0, 0))
    return pl.pallas_call(
        body, name=name, grid=(t // tm,), in_specs=[row, full(qg), full(km), full(vm)], out_specs=row,
        out_shape=jax.ShapeDtypeStruct((t, MEM_W), BF16), compiler_params=_cparams())(zqm, qg, km, vm)


def _mem_attn_bwd(zqm, dyc, qg, km, vm, name):
    t = zqm.shape[0]
    m = km.shape[0]
    tm = _row_tile(t, 256)

    def body(q_ref, dy_ref, qg_ref, k_ref, v_ref, dz_ref, dk_ref, dv_ref, dqg_ref):
        @pl.when(pl.program_id(0) == 0)
        def _():
            dk_ref[...] = jnp.zeros_like(dk_ref)
            dv_ref[...] = jnp.zeros_like(dv_ref)
            dqg_ref[...] = jnp.zeros_like(dqg_ref)

        qgv = qg_ref[...]
        dqg_acc = jnp.zeros((tm, LANES), F32)
        for h in range(MEM_HEADS):
            sl = slice(h * LANES, (h + 1) * LANES)
            qh = q_ref[:, sl]
            r = _rstd(qh, LANES)
            qn = (qh * r * qgv).astype(BF16)
            kh = k_ref[:, sl]
            p = _mem_softmax(qn, kh)
            dov = dy_ref[:, sl]
            dv_ref[:, sl] += _dot_tn(p.astype(BF16), dov)
            dp = _dot_nt(dov, v_ref[:, sl])
            ds = (p * (dp - jnp.sum(dp * p, axis=1, keepdims=True)) * (LANES ** -0.5)).astype(BF16)
            dk_ref[:, sl] += _dot_tn(ds, qn)
            dqh, dgr = _rms_vjp(qh, r, qgv, _dot(ds, kh), LANES)
            dz_ref[:, sl] = dqh.astype(BF16)
            dqg_acc += dgr
        dqg_ref[...] += jnp.sum(dqg_acc, axis=0, keepdims=True)

    row = pl.BlockSpec((tm, MEM_W), lambda i: (i, 0))
    full = lambda a: pl.BlockSpec(a.shape, lambda i: (0, 0))
    acc = pl.BlockSpec((m, MEM_W), lambda i: (0, 0))
    return pl.pallas_call(
        body, name=name, grid=(t // tm,), in_specs=[row, row, full(qg), full(km), full(vm)],
        out_specs=[row, acc, acc, pl.BlockSpec((1, LANES), lambda i: (0, 0))],
        out_shape=[jax.ShapeDtypeStruct((t, MEM_W), BF16), jax.ShapeDtypeStruct((m, MEM_W), F32),
                   jax.ShapeDtypeStruct((m, MEM_W), F32), jax.ShapeDtypeStruct((1, LANES), F32)],
        compiler_params=_cparams())(zqm, dyc, qg, km, vm)


def _mem_kv_bwd(mem, gmem, wkv, kg, dkn, dvm, name):
    m, d = mem.shape

    def body(mem_ref, g_ref, w_ref, kg_ref, dk_ref, dv_ref, dw_ref, dkg_ref, dg_ref, dkv_scr):
        xv = mem_ref[...]
        r = _rstd(xv, d)
        mn = (xv * r * g_ref[...]).astype(BF16)
        kvm = _dot(mn, w_ref[...])
        dkv_scr[:, MEM_W:] = dv_ref[...].astype(BF16)
        dkg_acc = jnp.zeros((m, LANES), F32)
        for h in range(MEM_HEADS):
            sl = slice(h * LANES, (h + 1) * LANES)
            kh = kvm[:, sl]
            dkh, dgr = _rms_vjp(kh, _rstd(kh, LANES), kg_ref[...], dk_ref[:, sl], LANES)
            dkv_scr[:, sl] = dkh.astype(BF16)
            dkg_acc += dgr
        dkg_ref[...] = jnp.sum(dkg_acc, axis=0, keepdims=True)
        dkv = dkv_scr[...]
        dw_ref[...] = _dot_tn(mn, dkv)
        dmn = _dot_nt(dkv, w_ref[...])
        dg_ref[...] = jnp.sum(dmn * xv * r, axis=0, keepdims=True)

    full = lambda a: pl.BlockSpec(a.shape, lambda i: (0, 0))
    return pl.pallas_call(
        body, name=name, grid=(1,),
        in_specs=[full(mem), full(gmem), full(wkv), full(kg), full(dkn), full(dvm)],
        out_specs=[pl.BlockSpec((d, 2 * MEM_W), lambda i: (0, 0)), pl.BlockSpec((1, LANES), lambda i: (0, 0)),
                   pl.BlockSpec((1, d), lambda i: (0, 0))],
        out_shape=[jax.ShapeDtypeStruct((d, 2 * MEM_W), F32), jax.ShapeDtypeStruct((1, LANES), F32),
                   jax.ShapeDtypeStruct((1, d), F32)],
        scratch_shapes=[pltpu.VMEM((m, 2 * MEM_W), BF16)],
        compiler_params=_cparams())(mem, gmem, wkv, kg, dkn, dvm)


def _merge_fwd(x1, ya, yb, yc, zg, bg, wa, wb, wc, wo, name):
    t, d = x1.shape
    tm = _row_tile(t, 256)

    def body(x_ref, ya_ref, yb_ref, yc_ref, zg_ref, bg_ref, wa_ref, wb_ref, wc_ref, wo_ref,
             x2_ref, mg_ref, pa_ref, pb_ref, pc_ref):
        merged = None
        for k, (y_ref, w_ref, p_ref) in enumerate(
                ((ya_ref, wa_ref, pa_ref), (yb_ref, wb_ref, pb_ref), (yc_ref, wc_ref, pc_ref))):
            sl = slice(k * d, (k + 1) * d)
            pr = _dot(y_ref[...], w_ref[...])
            p_ref[...] = pr.astype(BF16)
            term = jax.nn.sigmoid(zg_ref[:, sl] + bg_ref[:, sl]) * pr
            merged = term if merged is None else merged + term
        mb = merged.astype(BF16)
        mg_ref[...] = mb
        x2_ref[...] = x_ref[...] + _dot(mb, wo_ref[...])

    row = lambda n: pl.BlockSpec((tm, n), lambda i: (i, 0))
    full = lambda a: pl.BlockSpec(a.shape, lambda i: (0, 0))
    return pl.pallas_call(
        body, name=name, grid=(t // tm,),
        in_specs=[row(d), row(ya.shape[1]), row(yb.shape[1]), row(yc.shape[1]), row(3 * d), full(bg),
                  full(wa), full(wb), full(wc), full(wo)],
        out_specs=[row(d)] * 5,
        out_shape=[jax.ShapeDtypeStruct((t, d), F32)] + [jax.ShapeDtypeStruct((t, d), BF16)] * 4,
        compiler_params=_cparams())(x1, ya, yb, yc, zg, bg, wa, wb, wc, wo)


def _merge_bwd(dx2, pa, pb, pc, zg, bg, wa, wb, wc, wo, name):
    t, d = dx2.shape
    tm = _row_tile(t, 256)

    def body(dx_ref, pa_ref, pb_ref, pc_ref, zg_ref, bg_ref, wa_ref, wb_ref, wc_ref, wo_ref,
             dpa_ref, dpb_ref, dpc_ref, dzg_ref, dbg_ref, dya_ref, dyb_ref, dyc_ref):
        @pl.when(pl.program_id(0) == 0)
        def _():
            dbg_ref[...] = jnp.zeros_like(dbg_ref)

        dm = _dot_nt(dx_ref[...].astype(BF16), wo_ref[...])
        for k, (p_ref, w_ref, dp_ref, dy_ref) in enumerate(
                ((pa_ref, wa_ref, dpa_ref, dya_ref), (pb_ref, wb_ref, dpb_ref, dyb_ref),
                 (pc_ref, wc_ref, dpc_ref, dyc_ref))):
            sl = slice(k * d, (k + 1) * d)
            gate = jax.nn.sigmoid(zg_ref[:, sl] + bg_ref[:, sl])
            dpr = (dm * gate).astype(BF16)
            dp_ref[...] = dpr
            dzg = dm * p_ref[...].astype(F32) * gate * (1.0 - gate)
            dzg_ref[:, sl] = dzg.astype(BF16)
            dbg_ref[:, sl] += jnp.sum(dzg, axis=0, keepdims=True)
            dy_ref[...] = _dot_nt(dpr, w_ref[...]).astype(dy_ref.dtype)

    row = lambda n: pl.BlockSpec((tm, n), lambda i: (i, 0))
    full = lambda a: pl.BlockSpec(a.shape, lambda i: (0, 0))
    na, nb, nc = wa.shape[0], wb.shape[0], wc.shape[0]
    return pl.pallas_call(
        body, name=name, grid=(t // tm,),
        in_specs=[row(d), row(d), row(d), row(d), row(3 * d), full(bg), full(wa), full(wb), full(wc), full(wo)],
        out_specs=[row(d), row(d), row(d), row(3 * d), pl.BlockSpec((1, 3 * d), lambda i: (0, 0)),
                   row(na), row(nb), row(nc)],
        out_shape=[jax.ShapeDtypeStruct((t, d), BF16)] * 3
        + [jax.ShapeDtypeStruct((t, 3 * d), BF16), jax.ShapeDtypeStruct((1, 3 * d), F32),
           jax.ShapeDtypeStruct((t, na), F32), jax.ShapeDtypeStruct((t, nb), BF16),
           jax.ShapeDtypeStruct((t, nc), BF16)],
        compiler_params=_cparams())(dx2, pa, pb, pc, zg, bg, wa, wb, wc, wo)


def _adamw(w, g, m, v, name):
    rows, cols = w.shape
    tr = rows
    for cand in (256, 128, 64, 32, 16, 8):
        if rows % cand == 0:
            tr = cand
            break
    bc1 = 1.0 - ADAM_B1 ** ADAM_STEP
    bc2 = 1.0 - ADAM_B2 ** ADAM_STEP

    def body(w_ref, g_ref, m_ref, v_ref, d_ref, nm_ref, nv_ref):
        gv = g_ref[...]
        nm = ADAM_B1 * m_ref[...] + (1.0 - ADAM_B1) * gv
        nv = ADAM_B2 * v_ref[...] + (1.0 - ADAM_B2) * (gv * gv)
        nm_ref[...] = nm
        nv_ref[...] = nv
        d_ref[...] = -ADAM_LR * ((nm / bc1) / (jnp.sqrt(nv / bc2) + ADAM_EPS) + ADAM_WD * w_ref[...])

    blk = pl.BlockSpec((tr, cols), lambda i: (i, 0))
    return pl.pallas_call(
        body, name=name, grid=(rows // tr,), in_specs=[blk] * 4, out_specs=[blk] * 3,
        out_shape=[jax.ShapeDtypeStruct((rows, cols), F32)] * 3, compiler_params=_cparams())(w, g, m, v)


ANY = pl.BlockSpec(memory_space=pl.ANY)


def _place():
    x, y, c = lax.axis_index("x"), lax.axis_index("y"), lax.axis_index("c")
    other_chips = [(1 - x, y), (x, 1 - y), (1 - x, 1 - y)]
    return x, y, c, other_chips


def _gather_weights(pack):
    _, r, cols = pack.shape

    def body(s_ref, g_ref, send_sems, recv_sems, local_sem):
        x, y, c, chips = _place()
        me = 2 * x + y
        sibling = (x, y, 1 - c)
        mine = pltpu.make_async_copy(s_ref, g_ref.at[me], local_sem)
        mine.start()

        def copy(k, src, dst, to):
            return pltpu.make_async_remote_copy(src_ref=src, dst_ref=dst, send_sem=send_sems.at[k],
                                                recv_sem=recv_sems.at[k], device_id=to, device_id_type=MESH)

        first = [copy(k, s_ref.at[c], g_ref.at[me, c], (cx, cy, c)) for k, (cx, cy) in enumerate(chips)]
        for cp in first:
            cp.start()
        passed = []
        for k, (cx, cy) in enumerate(chips):
            slab = g_ref.at[2 * cx + cy, c]
            copy(k, slab, slab, (cx, cy, c)).wait_recv()
            fwd = copy(3 + k, slab, slab, sibling)
            fwd.start()
            passed.append(fwd)
        for k, (cx, cy) in enumerate(chips):
            slab = g_ref.at[2 * cx + cy, 1 - c]
            copy(3 + k, slab, slab, sibling).wait_recv()
        for cp in first + passed:
            cp.wait_send()
        mine.wait()

    return pl.pallas_call(
        body, name="gather_weights", in_specs=[ANY], out_specs=ANY,
        out_shape=jax.ShapeDtypeStruct((N_CHIPS, 2, r, cols), BF16),
        scratch_shapes=[pltpu.SemaphoreType.DMA((6,)), pltpu.SemaphoreType.DMA((6,)), pltpu.SemaphoreType.DMA],
    )(pack)


def _swap_halves(gpack):
    _, nchip, r, cols = gpack.shape

    def body(g_ref, own_ref, sib_ref, send_sem, recv_sem, local_sem):
        x, y, c, _ = _place()
        keep = pltpu.make_async_copy(g_ref.at[c], own_ref, local_sem)
        keep.start()
        cp = pltpu.make_async_remote_copy(src_ref=g_ref.at[1 - c], dst_ref=sib_ref, send_sem=send_sem,
                                          recv_sem=recv_sem, device_id=(x, y, 1 - c), device_id_type=MESH)
        cp.start()
        cp.wait_recv()
        cp.wait_send()
        keep.wait()

    shp = jax.ShapeDtypeStruct((nchip, r, cols), BF16)
    return pl.pallas_call(
        body, name="grad_swap_halves", in_specs=[ANY], out_specs=[ANY, ANY], out_shape=[shp, shp],
        scratch_shapes=[pltpu.SemaphoreType.DMA, pltpu.SemaphoreType.DMA, pltpu.SemaphoreType.DMA],
    )(gpack)


def _pair_sum(own, sib):
    nchip, r, cols = own.shape
    tr = PACK_ROW_ALIGN

    def body(a_ref, b_ref, o_ref):
        o_ref[...] = (a_ref[...].astype(F32) + b_ref[...].astype(F32)).astype(BF16)

    blk = pl.BlockSpec((1, tr, cols), lambda s, i: (s, i, 0))
    return pl.pallas_call(
        body, name="grad_pair_sum", grid=(nchip, r // tr), in_specs=[blk, blk], out_specs=blk,
        out_shape=jax.ShapeDtypeStruct((nchip, r, cols), BF16), compiler_params=_cparams())(own, sib)


def _scatter_to_owners(ps):
    nchip, r, cols = ps.shape

    def body(p_ref, o_ref, send_sems, recv_sems, local_sem):
        x, y, c, chips = _place()
        me = 2 * x + y
        mine = pltpu.make_async_copy(p_ref.at[me], o_ref.at[3], local_sem)
        mine.start()
        sends = []
        for k, (cx, cy) in enumerate(chips):
            cp = pltpu.make_async_remote_copy(src_ref=p_ref.at[2 * cx + cy], dst_ref=o_ref.at[k],
                                              send_sem=send_sems.at[k], recv_sem=recv_sems.at[k],
                                              device_id=(cx, cy, c), device_id_type=MESH)
            cp.start()
            sends.append(cp)
        for cp in sends:
            cp.wait_recv()
        for cp in sends:
            cp.wait_send()
        mine.wait()

    return pl.pallas_call(
        body, name="grad_scatter", in_specs=[ANY], out_specs=ANY,
        out_shape=jax.ShapeDtypeStruct((nchip, r, cols), BF16),
        scratch_shapes=[pltpu.SemaphoreType.DMA((3,)), pltpu.SemaphoreType.DMA((3,)), pltpu.SemaphoreType.DMA],
    )(ps)


def _sum_slots(parts):
    nchip, r, cols = parts.shape
    tr = PACK_ROW_ALIGN

    def body(p_ref, o_ref):
        acc = p_ref[0].astype(F32)
        for k in range(1, nchip):
            acc = acc + p_ref[k].astype(F32)
        o_ref[...] = acc

    return pl.pallas_call(
        body, name="grad_sum_slots", grid=(r // tr,),
        in_specs=[pl.BlockSpec((nchip, tr, cols), lambda i: (0, i, 0))],
        out_specs=pl.BlockSpec((tr, cols), lambda i: (i, 0)),
        out_shape=jax.ShapeDtypeStruct((r, cols), F32), compiler_params=_cparams())(parts)


def _join_halves(half):
    r, cols = half.shape

    def body(h_ref, o_ref, send_sem, recv_sem, local_sem):
        x, y, c, _ = _place()
        keep = pltpu.make_async_copy(h_ref, o_ref.at[c], local_sem)
        keep.start()
        cp = pltpu.make_async_remote_copy(src_ref=h_ref, dst_ref=o_ref.at[c], send_sem=send_sem,
                                          recv_sem=recv_sem, device_id=(x, y, 1 - c), device_id_type=MESH)
        cp.start()
        cp.wait_recv()
        cp.wait_send()
        keep.wait()

    return pl.pallas_call(
        body, name="grad_join_halves", in_specs=[ANY], out_specs=ANY,
        out_shape=jax.ShapeDtypeStruct((2, r, cols), F32),
        scratch_shapes=[pltpu.SemaphoreType.DMA, pltpu.SemaphoreType.DMA, pltpu.SemaphoreType.DMA],
    )(half)


def _allreduce_small(vec):
    m_per, n = vec.shape

    def body(x_ref, out_ref, gath_ref, send_sems, recv_sems, local_sem):
        x, y, c, chips = _place()
        me, sibling = (x, y, c), (x, y, 1 - c)

        def rows(px, py, pc):
            return gath_ref.at[pl.ds((4 * px + 2 * py + pc) * m_per, m_per), :]

        def copy(k, block, to, src=None):
            return pltpu.make_async_remote_copy(
                src_ref=rows(*block) if src is None else src, dst_ref=rows(*block),
                send_sem=send_sems.at[k], recv_sem=recv_sems.at[k], device_id=to, device_id_type=MESH)

        mine = pltpu.make_async_copy(x_ref, rows(*me), local_sem)
        mine.start()
        first = [copy(0, me, sibling, src=x_ref)]
        first += [copy(1 + j, me, (*chip, c), src=x_ref) for j, chip in enumerate(chips)]
        for cp in first:
            cp.start()
        passed = [copy(4 + j, (*chip, c), sibling) for j, chip in enumerate(chips)]
        for j, chip in enumerate(chips):
            copy(1 + j, (*chip, c), me).wait_recv()
            passed[j].start()
        copy(0, sibling, me).wait_recv()
        for j, chip in enumerate(chips):
            copy(4 + j, (*chip, 1 - c), me).wait_recv()
        for cp in first + passed:
            cp.wait_send()
        mine.wait()
        acc = gath_ref[pl.ds(0, m_per), :]
        for k in range(1, N_DEV):
            acc = acc + gath_ref[pl.ds(k * m_per, m_per), :]
        out_ref[...] = acc

    vm = pl.BlockSpec(memory_space=pltpu.VMEM)
    return pl.pallas_call(
        body, name="allreduce_small", in_specs=[vm], out_specs=vm,
        out_shape=jax.ShapeDtypeStruct((m_per, n), F32),
        scratch_shapes=[pltpu.VMEM((N_DEV * m_per, n), F32), pltpu.SemaphoreType.DMA((7,)),
                        pltpu.SemaphoreType.DMA((7,)), pltpu.SemaphoreType.DMA],
    )(vec)


def _pack_shards(shards, dtype):
    flat = jnp.concatenate([shards[name].reshape(-1).astype(dtype) for name, *_ in SHARDED])
    flat = jnp.pad(flat, (0, PACK_N - flat.shape[0]))
    return flat.reshape(2, PACK_HALF_ROWS, PACK_COLS)


def _unpack_gathered(g):
    out, off = {}, 0
    for name, r, c, kind in SHARDED:
        blk = g[:, off:off + r * c].reshape(N_CHIPS, r, c)
        off += r * c
        out[name] = blk if kind == "blocked" else (
            blk.transpose(1, 0, 2).reshape(r, N_CHIPS * c) if kind == "col" else blk.reshape(N_CHIPS * r, c))
        out[name + "#blocks"] = blk
    return out


def _pack_full_grads(grads):
    cols = []
    for name, r, c, kind in SHARDED:
        gw = grads[name]
        if kind == "col":
            blk = gw.reshape(r, N_CHIPS, c).transpose(1, 0, 2).reshape(N_CHIPS, r * c)
        else:
            blk = gw.reshape(N_CHIPS, r * c)
        cols.append(blk.astype(BF16))
    flat = jnp.concatenate(cols, axis=1)
    flat = jnp.pad(flat, ((0, 0), (0, PACK_N - flat.shape[1])))
    return flat.reshape(N_CHIPS, 2, PACK_HALF_ROWS, PACK_COLS).transpose(1, 0, 2, 3)


def _unpack_shard(flat):
    out, off = {}, 0
    for name, r, c, _ in SHARDED:
        out[name] = flat[off:off + r * c].reshape(r, c)
        off += r * c
    return out


def _pack_small(vals):
    flat = jnp.concatenate([vals[name].reshape(-1).astype(F32) for name, _ in SMALL])
    flat = jnp.pad(flat, (0, SMALL_ROWS * LANES - flat.shape[0]))
    return flat.reshape(SMALL_ROWS, LANES)


def _unpack_small(packed):
    flat = packed.reshape(-1)
    out, off = {}, 0
    for name, shape in SMALL:
        n = int(np.prod(shape))
        out[name] = flat[off:off + n].reshape(shape)
        off += n
    return out


def _head_pad_cols(w, heads, real):
    k = w.shape[0]
    return jnp.pad(w.reshape(k, heads, real), ((0, 0), (0, 0), (0, LANES - real))).reshape(k, heads * LANES)


def _rope_tables(positions):
    half = MLA_ROPE // 2
    inv = ROPE_BASE ** (-jnp.arange(half, dtype=F32) / half)
    ang = positions.astype(F32)[:, None] * inv
    cos, sin = jnp.cos(ang), jnp.sin(ang)
    t = positions.shape[0]
    z = lambda n: jnp.zeros((t, n), F32)
    rc = jnp.concatenate([jnp.ones((t, MLA_NOPE), F32), cos, cos, z(LANES - MLA_QK)], axis=1)
    rs1 = jnp.concatenate([z(MLA_NOPE), -sin, z(LANES - MLA_NOPE - half)], axis=1)
    rs2 = jnp.concatenate([z(MLA_NOPE + half), sin, z(LANES - MLA_QK)], axis=1)
    return rc, rs1, rs2


def _local_step(x, mem, positions, tgt, small, big):
    d = D_MODEL
    g_ffn1, g_mix, g_ffn2 = small["ffn1_norm"], small["mix_norm"], small["ffn2_norm"]
    wgu1, wd1 = big["ffn1_w_gu#blocks"], big["ffn1_w_down"].reshape(2, FF_TILE, d)
    wgu2, wd2 = big["ffn2_w_gu#blocks"], big["ffn2_w_down"].reshape(2, FF_TILE, d)
    w_in = big["w_in"]
    w_uv_, w_cq, w_ckv = w_in[:, :COL_CQ], w_in[:, COL_CQ:COL_CKV], w_in[:, COL_CKV:COL_KR]
    w_kr = jnp.pad(w_in[:, COL_KR:COL_QM], ((0, 0), (MLA_NOPE, LANES - MLA_QK)))
    w_qm, w_g = w_in[:, COL_QM:COL_GATE], w_in[:, COL_GATE:]
    segs = (w_uv_, w_cq, w_ckv, w_kr, w_qm, w_g)
    wuq = _head_pad_cols(big["mla_w_uq"], MLA_HEADS, MLA_QK)
    ukv = big["mla_w_ukv"].reshape(MLA_KV_RANK, MLA_HEADS, 2, MLA_NOPE)
    wuk = _head_pad_cols(ukv[:, :, 0].reshape(MLA_KV_RANK, -1), MLA_HEADS, MLA_NOPE)
    wuv = _head_pad_cols(ukv[:, :, 1].reshape(MLA_KV_RANK, -1), MLA_HEADS, MLA_NOPE)
    wkv = big["mem_w_kv"]
    wa, wc, wo = big["w_branch_a"], big["w_branch_c"], big["w_out"]
    wb = jnp.pad(big["w_branch_b"].reshape(MLA_HEADS, MLA_NOPE, d),
                 ((0, 0), (0, LANES - MLA_NOPE), (0, 0))).reshape(MLA_HEADS * LANES, d)
    qg = jnp.pad(small["mla_q_norm"], ((0, 0), (0, LANES - MLA_QK)))
    kg = jnp.pad(small["mla_k_norm"], ((0, 0), (0, LANES - MLA_QK)))
    causal = jnp.tril(jnp.ones((CHUNK, CHUNK), bool))
    wt_f = jnp.where(causal[None], small["sg_w"][0], 0.0)
    wt, wt_t = wt_f.astype(BF16), wt_f.transpose(0, 2, 1).astype(BF16)
    bias_l = jnp.repeat(small["sg_b"][0].T, 64, axis=1)
    rc, rs1, rs2 = _rope_tables(positions)

    x1 = _ffn_fwd(x, g_ffn1, wgu1, wd1, "ffn1_fwd")
    h = _rms_fwd(x1, g_mix, "mix_norm_fwd")
    zuv, zcq, zckv, zkr, zqm, zg = [
        _mm([(h, w)], F32, "in_proj_%d" % k) for k, w in enumerate(segs)]
    ya = _sgu_fwd(zuv, small["sg_ln_g"], small["sg_ln_b"], wt, bias_l, "sgu_fwd")
    q, k, v, cqn, ckvn = _mla_prep_fwd(zcq, zckv, zkr, small["mla_cq_norm"], small["mla_ckv_norm"], qg, kg,
                                       wuq, wuk, wuv, rc, rs1, rs2, "mla_prep_fwd")
    yb, lse = _attn_fwd(q, k, v, "mla_attn_fwd")
    km, vm, memn = _mem_kv_fwd(mem, small["mem_norm"], wkv, small["mem_k_norm"], "mem_kv_fwd")
    yc = _mem_attn_fwd(zqm, small["mem_q_norm"], km, vm, "mem_attn_fwd")
    x2, merged, pa, pb, pc = _merge_fwd(x1, ya, yb, yc, zg, small["b_gate"], wa, wb, wc, wo, "merge_fwd")
    x3 = _ffn_fwd(x2, g_ffn2, wgu2, wd2, "ffn2_fwd")
    dy, loss_row = _loss_head(x3, tgt, "loss_head")

    gw, gs = {}, {}

    def ffn_grads(prefix, xin, gain, dyin, wgu, wd):
        dx, dgain, xn, dgt, dup, act = _ffn_bwd(xin, gain, dyin, wgu, wd, prefix + "_bwd")
        gw[prefix + "_w_gu"] = jnp.concatenate(
            [_mm_tn(xn, dgt, prefix + "_dwg"), _mm_tn(xn, dup, prefix + "_dwu")], axis=1)
        gw[prefix + "_w_down"] = _mm_tn(act, dyin, prefix + "_dwd", scale=0.5)
        gs[prefix + "_norm"] = dgain
        return dx

    dx2 = ffn_grads("ffn2", x2, g_ffn2, dy, wgu2, wd2)
    dpa, dpb, dpc, dzg, dbg, dya, dyb, dyc = _merge_bwd(dx2, pa, pb, pc, zg, small["b_gate"], wa, wb, wc, wo,
                                                        "merge_bwd")
    gs["b_gate"] = dbg
    gw["w_out"] = _mm_tn(merged, dx2, "dw_out")
    gw["w_branch_a"] = _mm_tn(ya, dpa, "dw_branch_a")
    gw["w_branch_b"] = _mm_tn(yb, dpb, "dw_branch_b").reshape(MLA_HEADS, LANES, d)[:, :MLA_NOPE].reshape(-1, d)
    gw["w_branch_c"] = _mm_tn(yc, dpc, "dw_branch_c")

    dzuv, dwt, dbl, dlg, dlb = _sgu_bwd(zuv, dya, small["sg_ln_g"], small["sg_ln_b"], wt, wt_t, bias_l, "sgu_bwd")
    gs["sg_w"], gs["sg_b"] = dwt[None], dbl[:, :SG_GROUPS].T[None]
    gs["sg_ln_g"], gs["sg_ln_b"] = dlg, dlb

    dq, dk, dv = _attn_bwd(q, k, v, yb, lse, dyb, "mla_attn_bwd")
    dzcq, dzckv, dzkr, dql, dkl, dgcq, dgckv, dqg, dkg = _mla_prep_bwd(
        zcq, zckv, zkr, small["mla_cq_norm"], small["mla_ckv_norm"], qg, kg, wuq, wuk, wuv, rc, rs1, rs2,
        dq, dk, dv, "mla_prep_bwd")
    gs["mla_cq_norm"], gs["mla_ckv_norm"] = dgcq, dgckv
    gs["mla_q_norm"], gs["mla_k_norm"] = dqg[:, :MLA_QK], dkg[:, :MLA_QK]
    gw["mla_w_uq"] = _mm_tn(cqn, dql, "dw_uq").reshape(MLA_Q_RANK, MLA_HEADS, LANES)[:, :, :MLA_QK].reshape(
        MLA_Q_RANK, -1)
    dwuk = _mm_tn(ckvn, dkl, "dw_uk").reshape(MLA_KV_RANK, MLA_HEADS, LANES)[:, :, :MLA_NOPE]
    dwuv = _mm_tn(ckvn, dv, "dw_uv").reshape(MLA_KV_RANK, MLA_HEADS, LANES)[:, :, :MLA_NOPE]
    gw["mla_w_ukv"] = jnp.concatenate([dwuk, dwuv], axis=2).reshape(MLA_KV_RANK, -1)

    dzqm, dkn, dvm, dmqg = _mem_attn_bwd(zqm, dyc, small["mem_q_norm"], km, vm, "mem_attn_bwd")
    gs["mem_q_norm"] = dmqg
    gw["mem_w_kv"], gs["mem_k_norm"], gs["mem_norm"] = _mem_kv_bwd(
        mem, small["mem_norm"], wkv, small["mem_k_norm"], dkn, dvm, "mem_kv_bwd")

    dzs = (dzuv, dzcq, dzckv, dzkr, dzqm, dzg)
    dh = _mm([(dz, w.T) for dz, w in zip(dzs, segs)], F32, "in_proj_bwd")
    dws = [_mm_tn(h, dz, "dw_in_%d" % k) for k, dz in enumerate(dzs)]
    dws[3] = dws[3][:, MLA_NOPE:MLA_QK]
    gw["w_in"] = jnp.concatenate(dws, axis=1)
    dx1, gs["mix_norm"] = _rms_bwd(x1, g_mix, dh, dx2, "mix_norm_bwd")
    dx = ffn_grads("ffn1", x, g_ffn1, dx1, wgu1, wd1)
    return loss_row, dx, gw, gs


def kernel(x, mem, positions, ffn1_norm, ffn1_w_gu, ffn1_w_down, mix_norm, w_in, b_gate, sg_ln_g, sg_ln_b, sg_w, sg_b, mla_cq_norm, mla_w_uq, mla_ckv_norm, mla_w_ukv, mla_q_norm, mla_k_norm, mem_norm, mem_w_kv, mem_q_norm, mem_k_norm, w_branch_a, w_branch_b, w_branch_c, w_out, ffn2_norm, ffn2_w_gu, ffn2_w_down, loss_target, m_ffn1_norm, m_ffn1_w_gu, m_ffn1_w_down, m_mix_norm, m_w_in, m_b_gate, m_sg_ln_g, m_sg_ln_b, m_sg_w, m_sg_b, m_mla_cq_norm, m_mla_w_uq, m_mla_ckv_norm, m_mla_w_ukv, m_mla_q_norm, m_mla_k_norm, m_mem_norm, m_mem_w_kv, m_mem_q_norm, m_mem_k_norm, m_w_branch_a, m_w_branch_b, m_w_branch_c, m_w_out, m_ffn2_norm, m_ffn2_w_gu, m_ffn2_w_down, v_ffn1_norm, v_ffn1_w_gu, v_ffn1_w_down, v_mix_norm, v_w_in, v_b_gate, v_sg_ln_g, v_sg_ln_b, v_sg_w, v_sg_b, v_mla_cq_norm, v_mla_w_uq, v_mla_ckv_norm, v_mla_w_ukv, v_mla_q_norm, v_mla_k_norm, v_mem_norm, v_mem_w_kv, v_mem_q_norm, v_mem_k_norm, v_w_branch_a, v_w_branch_b, v_w_branch_c, v_w_out, v_ffn2_norm, v_ffn2_w_gu, v_ffn2_w_down):
    args = dict(locals())
    weights = {n: args[n] for n in WEIGHT_ORDER}
    mom_m = {n: args["m_" + n] for n in WEIGHT_ORDER}
    mom_v = {n: args["v_" + n] for n in WEIGHT_ORDER}
    small = {n: weights[n] for n, _ in SMALL}
    shards = {n: weights[n][0] for n, *_ in SHARDED}

    gathered = _gather_weights(_pack_shards(shards, BF16)).reshape(N_CHIPS, PACK_N)
    big = _unpack_gathered(gathered)

    loss_row, dx, gw, gs = _local_step(x[0], mem[0], positions[0], loss_target[0], small, big)
    loss = lax.psum(loss_row[0, 0], ("x", "y", "c"))

    own, sib = _swap_halves(_pack_full_grads(gw))
    parts = _scatter_to_owners(_pair_sum(own, sib))
    shard_grads = _unpack_shard(_join_halves(_sum_slots(parts)).reshape(-1))
    small_grads = _unpack_small(_allreduce_small(_pack_small(gs)))

    grads, deltas, new_m, new_v = {}, {}, {}, {}
    for name, r, c, _ in SHARDED:
        g2 = shard_grads[name]
        dlt, nm, nv = _adamw(weights[name][0], g2, mom_m[name][0], mom_v[name][0], "adamw_" + name)
        shape = weights[name].shape
        grads[name], deltas[name] = g2.reshape(shape), dlt.reshape(shape)
        new_m[name], new_v[name] = nm.reshape(shape), nv.reshape(shape)
    dlt, nm, nv = _adamw(_pack_small(small), _pack_small(small_grads), _pack_small({n: mom_m[n] for n, _ in SMALL}),
                         _pack_small({n: mom_v[n] for n, _ in SMALL}), "adamw_small")
    for name, _ in SMALL:
        grads[name] = small_grads[name]
    deltas.update(_unpack_small(dlt))
    new_m.update(_unpack_small(nm))
    new_v.update(_unpack_small(nv))

    return (loss, dx[None], *[grads[n] for n in WEIGHT_ORDER], *[deltas[n] for n in WEIGHT_ORDER],
            *[new_m[n] for n in WEIGHT_ORDER], *[new_v[n] for n in WEIGHT_ORDER])
```

```python
import functools
import math

import numpy as np
import jax
import jax.numpy as jnp
from jax import lax
from jax.experimental import pallas as pl
from jax.experimental.pallas import tpu as pltpu

F32 = jnp.float32
BF16 = jnp.bfloat16

D_MODEL = 1024
D_FF = 2816
FF_TILE = 1408
SG_WIDTH = 512
SG_GROUPS = 8
CHUNK = 128
MLA_HEADS = 8
MLA_QK = 96
MLA_NOPE = 64
MLA_ROPE = 32
MLA_Q_RANK = 384
MLA_KV_RANK = 256
MEM_HEADS = 4
MEM_LEN = 256
LANES = 128
EPS = 1e-6
NEG = -1e30
ROPE_BASE = 10000.0
N_CHIPS = 4
N_DEV = 8

ADAM_LR = 0.001
ADAM_B1 = 0.9
ADAM_B2 = 0.999
ADAM_EPS = 1e-08
ADAM_WD = 0.01
ADAM_STEP = 10

COL_V = 512
COL_CQ = 1024
COL_CKV = 1408
COL_KR = 1664
COL_QM = 1696
COL_GATE = 2208
IN_COLS = 5280

VMEM_LIMIT_BYTES = 56 * 1024 * 1024
INV_SQRT2 = 0.7071067811865476
INV_SQRT_2PI = 0.3989422804014327

SHARDED = (
    ("ffn1_w_gu", 1024, 1408, "col"),
    ("ffn1_w_down", 704, 1024, "row"),
    ("w_in", 1024, 1320, "col"),
    ("mla_w_uq", 384, 192, "col"),
    ("mla_w_ukv", 256, 256, "col"),
    ("mem_w_kv", 256, 1024, "row"),
    ("w_branch_a", 512, 256, "col"),
    ("w_branch_b", 512, 256, "col"),
    ("w_branch_c", 512, 256, "col"),
    ("w_out", 256, 1024, "row"),
    ("ffn2_w_gu", 1024, 1408, "col"),
    ("ffn2_w_down", 704, 1024, "row"),
)
SMALL = (
    ("ffn1_norm", (1, 1024)), ("mix_norm", (1, 1024)), ("b_gate", (1, 3072)),
    ("sg_ln_g", (1, 512)), ("sg_ln_b", (1, 512)), ("sg_w", (1, 8, 128, 128)),
    ("sg_b", (1, 8, 128)), ("mla_cq_norm", (1, 384)), ("mla_ckv_norm", (1, 256)),
    ("mla_q_norm", (1, 96)), ("mla_k_norm", (1, 96)), ("mem_norm", (1, 1024)),
    ("mem_q_norm", (1, 128)), ("mem_k_norm", (1, 128)), ("ffn2_norm", (1, 1024)),
)
WEIGHT_ORDER = (
    "ffn1_norm", "ffn1_w_gu", "ffn1_w_down", "mix_norm", "w_in", "b_gate", "sg_ln_g", "sg_ln_b",
    "sg_w", "sg_b", "mla_cq_norm", "mla_w_uq", "mla_ckv_norm", "mla_w_ukv", "mla_q_norm",
    "mla_k_norm", "mem_norm", "mem_w_kv", "mem_q_norm", "mem_k_norm", "w_branch_a", "w_branch_b",
    "w_branch_c", "w_out", "ffn2_norm", "ffn2_w_gu", "ffn2_w_down",
)

_N_SMALL = sum(int(np.prod(s)) for _, s in SMALL)
SMALL_ROWS = -(-_N_SMALL // (LANES * 8)) * 8

MESH = pl.DeviceIdType.MESH


def _cparams():
    return pltpu.CompilerParams(vmem_limit_bytes=VMEM_LIMIT_BYTES)


def _dot(a, b):
    return jnp.dot(a, b, preferred_element_type=F32)


def _dot_nt(a, b):
    return lax.dot_general(a, b, (((1,), (1,)), ((), ())), preferred_element_type=F32)


def _dot_tn(a, b):
    return lax.dot_general(a, b, (((0,), (0,)), ((), ())), preferred_element_type=F32)


def _gelu(x):
    return 0.5 * x * (1.0 + lax.erf(x * INV_SQRT2))


def _gelu_grad(x):
    return 0.5 * (1.0 + lax.erf(x * INV_SQRT2)) + x * jnp.exp(-0.5 * x * x) * INV_SQRT_2PI


def _rstd(x, n):
    return lax.rsqrt(jnp.sum(x * x, axis=-1, keepdims=True) * (1.0 / n) + EPS)


def _rms_vjp(x, r, g, dy, n):
    dxh = dy * g
    dx = r * dxh - x * (r * r * r) * (jnp.sum(dxh * x, axis=-1, keepdims=True) * (1.0 / n))
    return dx, dy * x * r


def _row_tile(t, want):
    return min(t, want)


def _wide_tile(n):
    if n <= 1024:
        return n
    if n % 1024 == 0:
        return 1024
    assert n % FF_TILE == 0, n
    return FF_TILE


def _rms_fwd(x, g, name):
    t, d = x.shape
    tm = _row_tile(t, 512)

    def body(x_ref, g_ref, o_ref):
        xv = x_ref[...]
        o_ref[...] = (xv * _rstd(xv, d) * g_ref[...]).astype(BF16)

    return pl.pallas_call(
        body, name=name, grid=(t // tm,),
        in_specs=[pl.BlockSpec((tm, d), lambda i: (i, 0)), pl.BlockSpec((1, d), lambda i: (0, 0))],
        out_specs=pl.BlockSpec((tm, d), lambda i: (i, 0)),
        out_shape=jax.ShapeDtypeStruct((t, d), BF16), compiler_params=_cparams())(x, g)


def _rms_bwd(x, g, dxn, dres, name):
    t, d = x.shape
    tm = _row_tile(t, 256)

    def body(x_ref, g_ref, d_ref, r_ref, dx_ref, dg_ref):
        @pl.when(pl.program_id(0) == 0)
        def _():
            dg_ref[...] = jnp.zeros_like(dg_ref)

        xv = x_ref[...]
        r = _rstd(xv, d)
        dx, dgr = _rms_vjp(xv, r, g_ref[...], d_ref[...].astype(F32), d)
        dx_ref[...] = r_ref[...] + dx
        dg_ref[...] += jnp.sum(dgr, axis=0, keepdims=True)

    row = pl.BlockSpec((tm, d), lambda i: (i, 0))
    vec = pl.BlockSpec((1, d), lambda i: (0, 0))
    return pl.pallas_call(
        body, name=name, grid=(t // tm,), in_specs=[row, vec, row, row], out_specs=[row, vec],
        out_shape=[jax.ShapeDtypeStruct((t, d), F32), jax.ShapeDtypeStruct((1, d), F32)],
        compiler_params=_cparams())(x, g, dxn, dres)


def _mm(pairs, out_dtype, name):
    t = pairs[0][0].shape[0]
    n = pairs[0][1].shape[1]
    tm = _row_tile(t, 512)
    tn = _wide_tile(n)
    np_ = len(pairs)

    def body(*refs):
        o_ref = refs[2 * np_]
        acc = None
        for a_ref, w_ref in zip(refs[:np_], refs[np_:2 * np_]):
            part = _dot(a_ref[...].astype(BF16), w_ref[...])
            acc = part if acc is None else acc + part
        o_ref[...] = acc.astype(out_dtype)

    in_specs = [pl.BlockSpec((tm, a.shape[1]), lambda i, j: (i, 0)) for a, _ in pairs]
    in_specs += [pl.BlockSpec((w.shape[0], tn), lambda i, j: (0, j)) for _, w in pairs]
    return pl.pallas_call(
        body, name=name, grid=(t // tm, n // tn), in_specs=in_specs,
        out_specs=pl.BlockSpec((tm, tn), lambda i, j: (i, j)),
        out_shape=jax.ShapeDtypeStruct((t, n), out_dtype), compiler_params=_cparams(),
    )(*[a for a, _ in pairs], *[w for _, w in pairs])


def _mm_tn(a, b, name, scale=1.0):
    t, m = a.shape
    n = b.shape[1]
    tm, tn = _wide_tile(m), _wide_tile(n)
    tk = _row_tile(t, 512)
    nk = t // tk

    def body(a_ref, b_ref, o_ref):
        k = pl.program_id(2)

        @pl.when(k == 0)
        def _():
            o_ref[...] = jnp.zeros_like(o_ref)

        o_ref[...] += _dot_tn(a_ref[...].astype(BF16), b_ref[...].astype(BF16))
        if scale != 1.0:
            @pl.when(k == nk - 1)
            def _():
                o_ref[...] = o_ref[...] * scale

    return pl.pallas_call(
        body, name=name, grid=(m // tm, n // tn, nk),
        in_specs=[pl.BlockSpec((tk, tm), lambda i, j, k: (k, i)),
                  pl.BlockSpec((tk, tn), lambda i, j, k: (k, j))],
        out_specs=pl.BlockSpec((tm, tn), lambda i, j, k: (i, j)),
        out_shape=jax.ShapeDtypeStruct((m, n), F32), compiler_params=_cparams())(a, b)


def _ffn_fwd(x, g, wgu4, wd2, name):
    t, d = x.shape
    tm = _row_tile(t, 512)

    def body(x_ref, g_ref, wg_ref, wu_ref, wd_ref, o_ref, xn_scr, acc_scr):
        j = pl.program_id(1)

        @pl.when(j == 0)
        def _():
            xv = x_ref[...]
            xn_scr[...] = (xv * _rstd(xv, d) * g_ref[...]).astype(BF16)
            acc_scr[...] = jnp.zeros_like(acc_scr)

        xn = xn_scr[...]
        gg = _dot(xn, wg_ref[0])
        uu = _dot(xn, wu_ref[0])
        act = gg * jax.nn.sigmoid(gg) * uu
        acc_scr[...] += _dot(act.astype(BF16), wd_ref[0])

        @pl.when(j == 1)
        def _():
            o_ref[...] = x_ref[...] + 0.5 * acc_scr[...]

    row = pl.BlockSpec((tm, d), lambda i, j: (i, 0))
    return pl.pallas_call(
        body, name=name, grid=(t // tm, 2),
        in_specs=[row, pl.BlockSpec((1, d), lambda i, j: (0, 0)),
                  pl.BlockSpec((1, d, FF_TILE), lambda i, j: (j, 0, 0)),
                  pl.BlockSpec((1, d, FF_TILE), lambda i, j: (j + 2, 0, 0)),
                  pl.BlockSpec((1, FF_TILE, d), lambda i, j: (j, 0, 0))],
        out_specs=row, out_shape=jax.ShapeDtypeStruct((t, d), F32),
        scratch_shapes=[pltpu.VMEM((tm, d), BF16), pltpu.VMEM((tm, d), F32)],
        compiler_params=_cparams())(x, g, wgu4, wgu4, wd2)


def _ffn_bwd(x, g, dy, wgu4, wd2, name):
    t, d = x.shape
    tm = _row_tile(t, 256)

    def body(x_ref, g_ref, dy_ref, wg_ref, wu_ref, wd_ref,
             dx_ref, dgain_ref, xn_ref, dg_ref, du_ref, act_ref, acc_scr):
        i, j = pl.program_id(0), pl.program_id(1)

        @pl.when((i == 0) & (j == 0))
        def _():
            dgain_ref[...] = jnp.zeros_like(dgain_ref)

        xv = x_ref[...]
        r = _rstd(xv, d)
        xn = (xv * r * g_ref[...]).astype(BF16)

        @pl.when(j == 0)
        def _():
            xn_ref[...] = xn
            acc_scr[...] = jnp.zeros_like(acc_scr)

        gg = _dot(xn, wg_ref[0])
        uu = _dot(xn, wu_ref[0])
        sg = jax.nn.sigmoid(gg)
        silu = gg * sg
        act_ref[...] = (silu * uu).astype(BF16)
        dyh = (0.5 * dy_ref[...]).astype(BF16)
        dact = _dot_nt(dyh, wd_ref[0])
        du = (dact * silu).astype(BF16)
        dgt = (dact * uu * (sg * (1.0 + gg * (1.0 - sg)))).astype(BF16)
        du_ref[...] = du
        dg_ref[...] = dgt
        acc_scr[...] += _dot_nt(dgt, wg_ref[0]) + _dot_nt(du, wu_ref[0])

        @pl.when(j == 1)
        def _():
            dx, dgr = _rms_vjp(xv, r, g_ref[...], acc_scr[...], d)
            dx_ref[...] = dy_ref[...] + dx
            dgain_ref[...] += jnp.sum(dgr, axis=0, keepdims=True)

    row = pl.BlockSpec((tm, d), lambda i, j: (i, 0))
    vec = pl.BlockSpec((1, d), lambda i, j: (0, 0))
    ffb = pl.BlockSpec((tm, FF_TILE), lambda i, j: (i, j))
    return pl.pallas_call(
        body, name=name, grid=(t // tm, 2),
        in_specs=[row, vec, row,
                  pl.BlockSpec((1, d, FF_TILE), lambda i, j: (j, 0, 0)),
                  pl.BlockSpec((1, d, FF_TILE), lambda i, j: (j + 2, 0, 0)),
                  pl.BlockSpec((1, FF_TILE, d), lambda i, j: (j, 0, 0))],
        out_specs=[row, vec, row, ffb, ffb, ffb],
        out_shape=[jax.ShapeDtypeStruct((t, d), F32), jax.ShapeDtypeStruct((1, d), F32),
                   jax.ShapeDtypeStruct((t, d), BF16), jax.ShapeDtypeStruct((t, D_FF), BF16),
                   jax.ShapeDtypeStruct((t, D_FF), BF16), jax.ShapeDtypeStruct((t, D_FF), BF16)],
        scratch_shapes=[pltpu.VMEM((tm, d), F32)],
        compiler_params=_cparams())(x, g, dy, wgu4, wgu4, wd2)


def _loss_head(y, tgt, name):
    t, d = y.shape
    tm = _row_tile(t, 512)

    def body(y_ref, t_ref, dy_ref, loss_ref):
        @pl.when(pl.program_id(0) == 0)
        def _():
            loss_ref[...] = jnp.zeros_like(loss_ref)

        e = y_ref[...] - t_ref[...]
        dy_ref[...] = e * (1.0 / d)
        part = 0.5 * jnp.sum(jnp.sum(e * e, axis=-1, keepdims=True) * (1.0 / d), axis=0, keepdims=True)
        loss_ref[...] += jnp.broadcast_to(part, loss_ref.shape)

    row = pl.BlockSpec((tm, d), lambda i: (i, 0))
    return pl.pallas_call(
        body, name=name, grid=(t // tm,), in_specs=[row, row],
        out_specs=[row, pl.BlockSpec((1, LANES), lambda i: (0, 0))],
        out_shape=[jax.ShapeDtypeStruct((t, d), F32), jax.ShapeDtypeStruct((1, LANES), F32)],
        compiler_params=_cparams())(y, tgt)


def _sgu_layernorm(vpre, lg, lb):
    v = _gelu(vpre)
    mu = jnp.mean(v, axis=-1, keepdims=True)
    xc = v - mu
    rstd = lax.rsqrt(jnp.mean(xc * xc, axis=-1, keepdims=True) + EPS)
    xhat = xc * rstd
    return xhat, rstd, xhat * lg + lb


def _sgu_fwd(zuv, lg, lb, wt, bias_l, name):
    t = zuv.shape[0]
    tm = _row_tile(t, 512)

    def body(u_ref, v_ref, lg_ref, lb_ref, wt_ref, bl_ref, o_ref, vln_scr):
        _, _, vln = _sgu_layernorm(v_ref[...], lg_ref[...], lb_ref[...])
        vln_scr[...] = vln.astype(BF16)
        lo = lax.broadcasted_iota(jnp.int32, (CHUNK, LANES), 1) < 64
        for c in range(tm // CHUNK):
            rows = slice(c * CHUNK, (c + 1) * CHUNK)
            for p in range(SG_GROUPS // 2):
                cols = slice(p * LANES, (p + 1) * LANES)
                vp = vln_scr[rows, cols]
                mixed = jnp.where(lo, _dot(wt_ref[2 * p], vp), _dot(wt_ref[2 * p + 1], vp)) + bl_ref[:, cols]
                o_ref[rows, cols] = (_gelu(u_ref[rows, cols]) * mixed).astype(BF16)

    half = lambda k: pl.BlockSpec((tm, SG_WIDTH), lambda i: (i, k))
    vec = pl.BlockSpec((1, SG_WIDTH), lambda i: (0, 0))
    return pl.pallas_call(
        body, name=name, grid=(t // tm,),
        in_specs=[half(0), half(1), vec, vec,
                  pl.BlockSpec((SG_GROUPS, CHUNK, CHUNK), lambda i: (0, 0, 0)),
                  pl.BlockSpec((CHUNK, SG_WIDTH), lambda i: (0, 0))],
        out_specs=pl.BlockSpec((tm, SG_WIDTH), lambda i: (i, 0)),
        out_shape=jax.ShapeDtypeStruct((t, SG_WIDTH), BF16),
        scratch_shapes=[pltpu.VMEM((tm, SG_WIDTH), BF16)],
        compiler_params=_cparams())(zuv, zuv, lg, lb, wt, bias_l)


def _sgu_bwd(zuv, dya, lg, lb, wt, wt_t, bias_l, name):
    t = zuv.shape[0]
    tm = _row_tile(t, 256)
    nsteps = t // tm

    def body(u_ref, v_ref, dy_ref, lg_ref, lb_ref, wt_ref, wtt_ref, bl_ref,
             dz_ref, dwt_ref, dbl_ref, dlg_ref, dlb_ref, vln_scr, dvln_scr, dbacc_scr):
        step = pl.program_id(0)

        @pl.when(step == 0)
        def _():
            dwt_ref[...] = jnp.zeros_like(dwt_ref)
            dlg_ref[...] = jnp.zeros_like(dlg_ref)
            dlb_ref[...] = jnp.zeros_like(dlb_ref)
            dbl_ref[...] = jnp.zeros_like(dbl_ref)
            dbacc_scr[...] = jnp.zeros_like(dbacc_scr)

        vpre = v_ref[...]
        lgv = lg_ref[...]
        xhat, rstd, vln = _sgu_layernorm(vpre, lgv, lb_ref[...])
        vln_scr[...] = vln.astype(BF16)
        lo = lax.broadcasted_iota(jnp.int32, (CHUNK, LANES), 1) < 64
        for c in range(tm // CHUNK):
            rows = slice(c * CHUNK, (c + 1) * CHUNK)
            for p in range(SG_GROUPS // 2):
                cols = slice(p * LANES, (p + 1) * LANES)
                vp = vln_scr[rows, cols]
                mixed = jnp.where(lo, _dot(wt_ref[2 * p], vp), _dot(wt_ref[2 * p + 1], vp)) + bl_ref[:, cols]
                upre = u_ref[rows, cols]
                dyp = dy_ref[rows, cols]
                dz_ref[rows, cols] = (dyp * mixed * _gelu_grad(upre)).astype(BF16)
                dm = dyp * _gelu(upre)
                dbacc_scr[:, cols] += dm
                dlo = jnp.where(lo, dm, 0.0).astype(BF16)
                dhi = jnp.where(lo, 0.0, dm).astype(BF16)
                dvln_scr[rows, cols] = _dot(wtt_ref[2 * p], dlo) + _dot(wtt_ref[2 * p + 1], dhi)
                dwt_ref[2 * p] += _dot_nt(dlo, vp)
                dwt_ref[2 * p + 1] += _dot_nt(dhi, vp)
        dvln = dvln_scr[...]
        dlg_ref[...] += jnp.sum(dvln * xhat, axis=0, keepdims=True)
        dlb_ref[...] += jnp.sum(dvln, axis=0, keepdims=True)
        dxh = dvln * lgv
        dv = rstd * (dxh - jnp.mean(dxh, axis=-1, keepdims=True)
                     - xhat * jnp.mean(dxh * xhat, axis=-1, keepdims=True))
        dz_ref[:, SG_WIDTH:] = (dv * _gelu_grad(vpre)).astype(BF16)

        @pl.when(step == nsteps - 1)
        def _():
            rr = lax.broadcasted_iota(jnp.int32, (CHUNK, CHUNK), 0)
            cc = lax.broadcasted_iota(jnp.int32, (CHUNK, CHUNK), 1)
            tril = (cc <= rr).astype(F32)
            for gidx in range(SG_GROUPS):
                dwt_ref[gidx] = dwt_ref[gidx] * tril
            kk = lax.broadcasted_iota(jnp.int32, (SG_WIDTH, LANES), 0)
            gg = lax.broadcasted_iota(jnp.int32, (SG_WIDTH, LANES), 1)
            sel = ((kk // 64) == gg).astype(F32)
            dbl_ref[...] = jnp.dot(dbacc_scr[...], sel, preferred_element_type=F32,
                                   precision=lax.Precision.HIGHEST)

    half = lambda k: pl.BlockSpec((tm, SG_WIDTH), lambda i: (i, k))
    vec = pl.BlockSpec((1, SG_WIDTH), lambda i: (0, 0))
    wspec = pl.BlockSpec((SG_GROUPS, CHUNK, CHUNK), lambda i: (0, 0, 0))
    return pl.pallas_call(
        body, name=name, grid=(nsteps,),
        in_specs=[half(0), half(1), pl.BlockSpec((tm, SG_WIDTH), lambda i: (i, 0)), vec, vec,
                  wspec, wspec, pl.BlockSpec((CHUNK, SG_WIDTH), lambda i: (0, 0))],
        out_specs=[pl.BlockSpec((tm, 2 * SG_WIDTH), lambda i: (i, 0)), wspec,
                   pl.BlockSpec((CHUNK, LANES), lambda i: (0, 0)), vec, vec],
        out_shape=[jax.ShapeDtypeStruct((t, 2 * SG_WIDTH), BF16),
                   jax.ShapeDtypeStruct((SG_GROUPS, CHUNK, CHUNK), F32),
                   jax.ShapeDtypeStruct((CHUNK, LANES), F32),
                   jax.ShapeDtypeStruct((1, SG_WIDTH), F32), jax.ShapeDtypeStruct((1, SG_WIDTH), F32)],
        scratch_shapes=[pltpu.VMEM((tm, SG_WIDTH), BF16), pltpu.VMEM((tm, SG_WIDTH), F32),
                        pltpu.VMEM((CHUNK, SG_WIDTH), F32)],
        compiler_params=_cparams())(zuv, zuv, dya, lg, lb, wt, wt_t, bias_l)


def _rope(x, c, s1, s2):
    return x * c + pltpu.roll(x, LANES - 16, 1) * s1 + pltpu.roll(x, 16, 1) * s2


def _rope_t(dy, c, s1, s2):
    return dy * c + pltpu.roll(dy * s1, 16, 1) + pltpu.roll(dy * s2, LANES - 16, 1)


def _mla_prep_fwd(zcq, zckv, zkr, gcq, gckv, qg, kg, wuq, wuk, wuv, rc, rs1, rs2, name):
    t = zcq.shape[0]
    tm = _row_tile(t, 256)
    hd = MLA_HEADS * LANES

    def body(zcq_ref, zckv_ref, zkr_ref, gcq_ref, gckv_ref, qg_ref, kg_ref, wuq_ref, wuk_ref, wuv_ref,
             c_ref, s1_ref, s2_ref, q_ref, k_ref, v_ref, cqn_ref, ckvn_ref):
        c, s1, s2 = c_ref[...], s1_ref[...], s2_ref[...]
        xq = zcq_ref[...]
        cqn = (xq * _rstd(xq, MLA_Q_RANK) * gcq_ref[...]).astype(BF16)
        cqn_ref[...] = cqn
        ql = _dot(cqn, wuq_ref[...])
        xk = zckv_ref[...]
        ckvn = (xk * _rstd(xk, MLA_KV_RANK) * gckv_ref[...]).astype(BF16)
        ckvn_ref[...] = ckvn
        kl = _dot(ckvn, wuk_ref[...])
        v_ref[...] = _dot(ckvn, wuv_ref[...]).astype(BF16)
        kr = zkr_ref[...]
        for h in range(MLA_HEADS):
            sl = slice(h * LANES, (h + 1) * LANES)
            qh = ql[:, sl]
            q_ref[:, sl] = _rope(qh * _rstd(qh, MLA_QK) * qg_ref[...], c, s1, s2).astype(BF16)
            kh = kl[:, sl] + kr
            k_ref[:, sl] = _rope(kh * _rstd(kh, MLA_QK) * kg_ref[...], c, s1, s2).astype(BF16)

    row = lambda n: pl.BlockSpec((tm, n), lambda i: (i, 0))
    full = lambda a: pl.BlockSpec(a.shape, lambda i: (0, 0))
    return pl.pallas_call(
        body, name=name, grid=(t // tm,),
        in_specs=[row(MLA_Q_RANK), row(MLA_KV_RANK), row(LANES), full(gcq), full(gckv), full(qg), full(kg),
                  full(wuq), full(wuk), full(wuv), row(LANES), row(LANES), row(LANES)],
        out_specs=[row(hd), row(hd), row(hd), row(MLA_Q_RANK), row(MLA_KV_RANK)],
        out_shape=[jax.ShapeDtypeStruct((t, hd), BF16)] * 3
        + [jax.ShapeDtypeStruct((t, MLA_Q_RANK), BF16), jax.ShapeDtypeStruct((t, MLA_KV_RANK), BF16)],
        compiler_params=_cparams(),
    )(zcq, zckv, zkr, gcq, gckv, qg, kg, wuq, wuk, wuv, rc, rs1, rs2)


def _mla_prep_bwd(zcq, zckv, zkr, gcq, gckv, qg, kg, wuq, wuk, wuv, rc, rs1, rs2, dq, dk, dv, name):
    t = zcq.shape[0]
    tm = _row_tile(t, 256)
    hd = MLA_HEADS * LANES

    def body(zcq_ref, zckv_ref, zkr_ref, gcq_ref, gckv_ref, qg_ref, kg_ref, wuq_ref, wuk_ref, wuv_ref,
             c_ref, s1_ref, s2_ref, dq_ref, dk_ref, dv_ref,
             dzcq_ref, dzckv_ref, dzkr_ref, dql_ref, dkl_ref, dgcq_ref, dgckv_ref, dqg_ref, dkg_ref):
        @pl.when(pl.program_id(0) == 0)
        def _():
            for ref in (dgcq_ref, dgckv_ref, dqg_ref, dkg_ref):
                ref[...] = jnp.zeros_like(ref)

        c, s1, s2 = c_ref[...], s1_ref[...], s2_ref[...]
        qgv, kgv = qg_ref[...], kg_ref[...]
        xq = zcq_ref[...]
        rq = _rstd(xq, MLA_Q_RANK)
        ql = _dot((xq * rq * gcq_ref[...]).astype(BF16), wuq_ref[...])
        xk = zckv_ref[...]
        rk = _rstd(xk, MLA_KV_RANK)
        kl = _dot((xk * rk * gckv_ref[...]).astype(BF16), wuk_ref[...])
        kr = zkr_ref[...]
        dqg_acc = jnp.zeros((tm, LANES), F32)
        dkg_acc = jnp.zeros((tm, LANES), F32)
        dkr = jnp.zeros((tm, LANES), F32)
        for h in range(MLA_HEADS):
            sl = slice(h * LANES, (h + 1) * LANES)
            qh = ql[:, sl]
            dqh, dgr = _rms_vjp(qh, _rstd(qh, MLA_QK), qgv, _rope_t(dq_ref[:, sl], c, s1, s2), MLA_QK)
            dql_ref[:, sl] = dqh.astype(BF16)
            dqg_acc += dgr
            kh = kl[:, sl] + kr
            dkh, dgr = _rms_vjp(kh, _rstd(kh, MLA_QK), kgv, _rope_t(dk_ref[:, sl], c, s1, s2), MLA_QK)
            dkl_ref[:, sl] = dkh.astype(BF16)
            dkg_acc += dgr
            dkr += dkh
        dqg_ref[...] += jnp.sum(dqg_acc, axis=0, keepdims=True)
        dkg_ref[...] += jnp.sum(dkg_acc, axis=0, keepdims=True)
        lane = lax.broadcasted_iota(jnp.int32, (tm, LANES), 1)
        dzkr_ref[...] = jnp.where((lane >= MLA_NOPE) & (lane < MLA_QK), dkr, 0.0).astype(BF16)
        dcqn = _dot_nt(dql_ref[...], wuq_ref[...])
        dx, dgr = _rms_vjp(xq, rq, gcq_ref[...], dcqn, MLA_Q_RANK)
        dzcq_ref[...] = dx.astype(BF16)
        dgcq_ref[...] += jnp.sum(dgr, axis=0, keepdims=True)
        dckvn = _dot_nt(dkl_ref[...], wuk_ref[...]) + _dot_nt(dv_ref[...].astype(BF16), wuv_ref[...])
        dx, dgr = _rms_vjp(xk, rk, gckv_ref[...], dckvn, MLA_KV_RANK)
        dzckv_ref[...] = dx.astype(BF16)
        dgckv_ref[...] += jnp.sum(dgr, axis=0, keepdims=True)

    row = lambda n: pl.BlockSpec((tm, n), lambda i: (i, 0))
    full = lambda a: pl.BlockSpec(a.shape, lambda i: (0, 0))
    vec = lambda n: pl.BlockSpec((1, n), lambda i: (0, 0))
    return pl.pallas_call(
        body, name=name, grid=(t // tm,),
        in_specs=[row(MLA_Q_RANK), row(MLA_KV_RANK), row(LANES), full(gcq), full(gckv), full(qg), full(kg),
                  full(wuq), full(wuk), full(wuv), row(LANES), row(LANES), row(LANES), row(hd), row(hd), row(hd)],
        out_specs=[row(MLA_Q_RANK), row(MLA_KV_RANK), row(LANES), row(hd), row(hd),
                   vec(MLA_Q_RANK), vec(MLA_KV_RANK), vec(LANES), vec(LANES)],
        out_shape=[jax.ShapeDtypeStruct((t, MLA_Q_RANK), BF16), jax.ShapeDtypeStruct((t, MLA_KV_RANK), BF16),
                   jax.ShapeDtypeStruct((t, LANES), BF16), jax.ShapeDtypeStruct((t, hd), BF16),
                   jax.ShapeDtypeStruct((t, hd), BF16), jax.ShapeDtypeStruct((1, MLA_Q_RANK), F32),
                   jax.ShapeDtypeStruct((1, MLA_KV_RANK), F32), jax.ShapeDtypeStruct((1, LANES), F32),
                   jax.ShapeDtypeStruct((1, LANES), F32)],
        compiler_params=_cparams(),
    )(zcq, zckv, zkr, gcq, gckv, qg, kg, wuq, wuk, wuv, rc, rs1, rs2, dq, dk, dv)


def _attn_tile(t):
    return 512 if t >= 2048 else 128


def _causal_mask(tile):
    row = lax.broadcasted_iota(jnp.int32, (tile, tile), 0)
    col = lax.broadcasted_iota(jnp.int32, (tile, tile), 1)
    return col <= row


ATTN_FWD_HEADS_PER_STEP = 4
ATTN_BWD_HEADS_PER_STEP = 2


def _attn_fwd(q, k, v, name):
    t, hd = q.shape
    hp = ATTN_FWD_HEADS_PER_STEP
    tile = _attn_tile(t)
    n = t // tile
    ii = np.array([i for i in range(n) for _ in range(i + 1)], np.int32)
    jj = np.array([j for i in range(n) for j in range(i + 1)], np.int32)
    scale = MLA_QK ** -0.5

    def body(ii_ref, jj_ref, q_ref, k_ref, v_ref, o_ref, lse_ref, m_scr, l_scr, acc_scr):
        s_id = pl.program_id(1)
        i, j = ii_ref[s_id], jj_ref[s_id]

        @pl.when(j == 0)
        def _():
            m_scr[...] = jnp.full_like(m_scr, NEG)
            l_scr[...] = jnp.zeros_like(l_scr)
            acc_scr[...] = jnp.zeros_like(acc_scr)

        def step(masked):
            for hh in range(hp):
                sl = slice(hh * LANES, (hh + 1) * LANES)
                s = _dot_nt(q_ref[:, sl], k_ref[:, sl]) * scale
                if masked:
                    s = jnp.where(_causal_mask(tile), s, NEG)
                m_prev = m_scr[hh]
                m_new = jnp.maximum(m_prev, jnp.max(s, axis=1, keepdims=True))
                p = jnp.exp(s - m_new)
                alpha = jnp.exp(m_prev - m_new)
                l_new = alpha * l_scr[hh] + jnp.sum(p, axis=1, keepdims=True)
                acc = alpha * acc_scr[:, sl] + _dot(p.astype(BF16), v_ref[:, sl])
                if masked:
                    o_ref[:, sl] = (acc / l_new).astype(BF16)
                    lse_ref[:, sl] = jnp.broadcast_to(m_new + jnp.log(l_new), (tile, LANES))
                else:
                    l_scr[hh] = l_new
                    acc_scr[:, sl] = acc
                    m_scr[hh] = m_new

        @pl.when(j < i)
        def _():
            step(False)

        @pl.when(j == i)
        def _():
            step(True)

    w = hp * LANES
    qspec = pl.BlockSpec((tile, w), lambda h, s, ii_r, jj_r: (ii_r[s], h))
    kspec = pl.BlockSpec((tile, w), lambda h, s, ii_r, jj_r: (jj_r[s], h))
    return pl.pallas_call(
        body, name=name,
        grid_spec=pltpu.PrefetchScalarGridSpec(
            num_scalar_prefetch=2, grid=(hd // w, len(ii)), in_specs=[qspec, kspec, kspec],
            out_specs=[qspec, qspec],
            scratch_shapes=[pltpu.VMEM((hp, tile, 1), F32), pltpu.VMEM((hp, tile, 1), F32),
                            pltpu.VMEM((tile, w), F32)]),
        out_shape=[jax.ShapeDtypeStruct((t, hd), BF16), jax.ShapeDtypeStruct((t, hd), F32)],
        compiler_params=_cparams())(jnp.asarray(ii), jnp.asarray(jj), q, k, v)


def _attn_bwd(q, k, v, o, lse, do, name):
    t, hd = q.shape
    hp = ATTN_BWD_HEADS_PER_STEP
    tile = _attn_tile(t)
    n = t // tile
    jj = np.array([j for j in range(n) for _ in range(j, n)], np.int32)
    ii = np.array([i for j in range(n) for i in range(j, n)], np.int32)
    scale = MLA_QK ** -0.5

    def body(jj_ref, ii_ref, q_ref, k_ref, v_ref, o_ref, lse_ref, do_ref, dq_ref, dk_ref, dv_ref, dk_scr, dv_scr):
        s_id = pl.program_id(1)
        i, j = ii_ref[s_id], jj_ref[s_id]

        @pl.when(s_id == 0)
        def _():
            dq_ref[...] = jnp.zeros_like(dq_ref)

        @pl.when(i == j)
        def _():
            dk_scr[...] = jnp.zeros_like(dk_scr)
            dv_scr[...] = jnp.zeros_like(dv_scr)

        rows = pl.ds(pl.multiple_of(i * tile, tile), tile)

        def step(masked):
            for hh in range(hp):
                sl = slice(hh * LANES, (hh + 1) * LANES)
                qv, kv, dov = q_ref[:, sl], k_ref[:, sl], do_ref[:, sl]
                s = _dot_nt(qv, kv) * scale
                if masked:
                    s = jnp.where(_causal_mask(tile), s, NEG)
                p = jnp.exp(s - jnp.max(lse_ref[:, sl], axis=1, keepdims=True))
                delta = jnp.sum(dov.astype(F32) * o_ref[:, sl].astype(F32), axis=1, keepdims=True)
                dv_scr[:, sl] += _dot_tn(p.astype(BF16), dov)
                dp = _dot_nt(dov, v_ref[:, sl])
                ds = (p * (dp - delta) * scale).astype(BF16)
                dk_scr[:, sl] += _dot_tn(ds, qv)
                dq_ref[rows, sl] += _dot(ds, kv)

        @pl.when(j < i)
        def _():
            step(False)

        @pl.when(j == i)
        def _():
            step(True)

        @pl.when(i == n - 1)
        def _():
            dk_ref[...] = dk_scr[...]
            dv_ref[...] = dv_scr[...]

    w = hp * LANES
    qspec = pl.BlockSpec((tile, w), lambda h, s, jj_r, ii_r: (ii_r[s], h))
    kspec = pl.BlockSpec((tile, w), lambda h, s, jj_r, ii_r: (jj_r[s], h))
    return pl.pallas_call(
        body, name=name,
        grid_spec=pltpu.PrefetchScalarGridSpec(
            num_scalar_prefetch=2, grid=(hd // w, len(ii)),
            in_specs=[qspec, kspec, kspec, qspec, qspec, qspec],
            out_specs=[pl.BlockSpec((t, w), lambda h, s, jj_r, ii_r: (0, h)), kspec, kspec],
            scratch_shapes=[pltpu.VMEM((tile, w), F32), pltpu.VMEM((tile, w), F32)]),
        out_shape=[jax.ShapeDtypeStruct((t, hd), F32)] * 3,
        compiler_params=_cparams())(jnp.asarray(jj), jnp.asarray(ii), q, k, v, o, lse, do)


MEM_W = MEM_HEADS * LANES


def _mem_kv_fwd(mem, gmem, wkv, kg, name):
    m, d = mem.shape

    def body(mem_ref, g_ref, w_ref, kg_ref, k_ref, v_ref, mn_ref):
        xv = mem_ref[...]
        mn = (xv * _rstd(xv, d) * g_ref[...]).astype(BF16)
        mn_ref[...] = mn
        kvm = _dot(mn, w_ref[...])
        v_ref[...] = kvm[:, MEM_W:].astype(BF16)
        for h in range(MEM_HEADS):
            sl = slice(h * LANES, (h + 1) * LANES)
            kh = kvm[:, sl]
            k_ref[:, sl] = (kh * _rstd(kh, LANES) * kg_ref[...]).astype(BF16)

    full = lambda a: pl.BlockSpec(a.shape, lambda i: (0, 0))
    return pl.pallas_call(
        body, name=name, grid=(1,), in_specs=[full(mem), full(gmem), full(wkv), full(kg)],
        out_specs=[pl.BlockSpec((m, MEM_W), lambda i: (0, 0)), pl.BlockSpec((m, MEM_W), lambda i: (0, 0)),
                   pl.BlockSpec((m, d), lambda i: (0, 0))],
        out_shape=[jax.ShapeDtypeStruct((m, MEM_W), BF16), jax.ShapeDtypeStruct((m, MEM_W), BF16),
                   jax.ShapeDtypeStruct((m, d), BF16)],
        compiler_params=_cparams())(mem, gmem, wkv, kg)


def _mem_softmax(qn, kh):
    s = _dot_nt(qn, kh) * (LANES ** -0.5)
    e = jnp.exp(s - jnp.max(s, axis=1, keepdims=True))
    return e / jnp.sum(e, axis=1, keepdims=True)


def _mem_attn_fwd(zqm, qg, km, vm, name):
    t = zqm.shape[0]
    tm = _row_tile(t, 512)

    def body(q_ref, qg_ref, k_ref, v_ref, o_ref):
        for h in range(MEM_HEADS):
            sl = slice(h * LANES, (h + 1) * LANES)
            qh = q_ref[:, sl]
            qn = (qh * _rstd(qh, LANES) * qg_ref[...]).astype(BF16)
            p = _mem_softmax(qn, k_ref[:, sl])
            o_ref[:, sl] = _dot(p.astype(BF16), v_ref[:, sl]).astype(BF16)

    row = pl.BlockSpec((tm, MEM_W), lambda i: (i, 0))
    full = lambda a: pl.BlockSpec(a.shape, lambda i: (0, 0))
    return pl.pallas_call(
        body, name=name, grid=(t // tm,), in_specs=[row, full(qg), full(km), full(vm)], out_specs=row,
        out_shape=jax.ShapeDtypeStruct((t, MEM_W), BF16), compiler_params=_cparams())(zqm, qg, km, vm)


def _mem_attn_bwd(zqm, dyc, qg, km, vm, name):
    t = zqm.shape[0]
    m = km.shape[0]
    tm = _row_tile(t, 256)

    def body(q_ref, dy_ref, qg_ref, k_ref, v_ref, dz_ref, dk_ref, dv_ref, dqg_ref):
        @pl.when(pl.program_id(0) == 0)
        def _():
            dk_ref[...] = jnp.zeros_like(dk_ref)
            dv_ref[...] = jnp.zeros_like(dv_ref)
            dqg_ref[...] = jnp.zeros_like(dqg_ref)

        qgv = qg_ref[...]
        dqg_acc = jnp.zeros((tm, LANES), F32)
        for h in range(MEM_HEADS):
            sl = slice(h * LANES, (h + 1) * LANES)
            qh = q_ref[:, sl]
            r = _rstd(qh, LANES)
            qn = (qh * r * qgv).astype(BF16)
            kh = k_ref[:, sl]
            p = _mem_softmax(qn, kh)
            dov = dy_ref[:, sl]
            dv_ref[:, sl] += _dot_tn(p.astype(BF16), dov)
            dp = _dot_nt(dov, v_ref[:, sl])
            ds = (p * (dp - jnp.sum(dp * p, axis=1, keepdims=True)) * (LANES ** -0.5)).astype(BF16)
            dk_ref[:, sl] += _dot_tn(ds, qn)
            dqh, dgr = _rms_vjp(qh, r, qgv, _dot(ds, kh), LANES)
            dz_ref[:, sl] = dqh.astype(BF16)
            dqg_acc += dgr
        dqg_ref[...] += jnp.sum(dqg_acc, axis=0, keepdims=True)

    row = pl.BlockSpec((tm, MEM_W), lambda i: (i, 0))
    full = lambda a: pl.BlockSpec(a.shape, lambda i: (0, 0))
    acc = pl.BlockSpec((m, MEM_W), lambda i: (0, 0))
    return pl.pallas_call(
        body, name=name, grid=(t // tm,), in_specs=[row, row, full(qg), full(km), full(vm)],
        out_specs=[row, acc, acc, pl.BlockSpec((1, LANES), lambda i: (0, 0))],
        out_shape=[jax.ShapeDtypeStruct((t, MEM_W), BF16), jax.ShapeDtypeStruct((m, MEM_W), F32),
                   jax.ShapeDtypeStruct((m, MEM_W), F32), jax.ShapeDtypeStruct((1, LANES), F32)],
        compiler_params=_cparams())(zqm, dyc, qg, km, vm)


def _mem_kv_bwd(mem, gmem, wkv, kg, dkn, dvm, name):
    m, d = mem.shape

    def body(mem_ref, g_ref, w_ref, kg_ref, dk_ref, dv_ref, dw_ref, dkg_ref, dg_ref, dkv_scr):
        xv = mem_ref[...]
        r = _rstd(xv, d)
        mn = (xv * r * g_ref[...]).astype(BF16)
        kvm = _dot(mn, w_ref[...])
        dkv_scr[:, MEM_W:] = dv_ref[...].astype(BF16)
        dkg_acc = jnp.zeros((m, LANES), F32)
        for h in range(MEM_HEADS):
            sl = slice(h * LANES, (h + 1) * LANES)
            kh = kvm[:, sl]
            dkh, dgr = _rms_vjp(kh, _rstd(kh, LANES), kg_ref[...], dk_ref[:, sl], LANES)
            dkv_scr[:, sl] = dkh.astype(BF16)
            dkg_acc += dgr
        dkg_ref[...] = jnp.sum(dkg_acc, axis=0, keepdims=True)
        dkv = dkv_scr[...]
        dw_ref[...] = _dot_tn(mn, dkv)
        dmn = _dot_nt(dkv, w_ref[...])
        dg_ref[...] = jnp.sum(dmn * xv * r, axis=0, keepdims=True)

    full = lambda a: pl.BlockSpec(a.shape, lambda i: (0, 0))
    return pl.pallas_call(
        body, name=name, grid=(1,),
        in_specs=[full(mem), full(gmem), full(wkv), full(kg), full(dkn), full(dvm)],
        out_specs=[pl.BlockSpec((d, 2 * MEM_W), lambda i: (0, 0)), pl.BlockSpec((1, LANES), lambda i: (0, 0)),
                   pl.BlockSpec((1, d), lambda i: (0, 0))],
        out_shape=[jax.ShapeDtypeStruct((d, 2 * MEM_W), F32), jax.ShapeDtypeStruct((1, LANES), F32),
                   jax.ShapeDtypeStruct((1, d), F32)],
        scratch_shapes=[pltpu.VMEM((m, 2 * MEM_W), BF16)],
        compiler_params=_cparams())(mem, gmem, wkv, kg, dkn, dvm)


def _merge_fwd(x1, ya, yb, yc, zg, bg, wa, wb, wc, wo, name):
    t, d = x1.shape
    tm = _row_tile(t, 256)

    def body(x_ref, ya_ref, yb_ref, yc_ref, zg_ref, bg_ref, wa_ref, wb_ref, wc_ref, wo_ref,
             x2_ref, mg_ref, pa_ref, pb_ref, pc_ref):
        merged = None
        for k, (y_ref, w_ref, p_ref) in enumerate(
                ((ya_ref, wa_ref, pa_ref), (yb_ref, wb_ref, pb_ref), (yc_ref, wc_ref, pc_ref))):
            sl = slice(k * d, (k + 1) * d)
            pr = _dot(y_ref[...], w_ref[...])
            p_ref[...] = pr.astype(BF16)
            term = jax.nn.sigmoid(zg_ref[:, sl] + bg_ref[:, sl]) * pr
            merged = term if merged is None else merged + term
        mb = merged.astype(BF16)
        mg_ref[...] = mb
        x2_ref[...] = x_ref[...] + _dot(mb, wo_ref[...])

    row = lambda n: pl.BlockSpec((tm, n), lambda i: (i, 0))
    full = lambda a: pl.BlockSpec(a.shape, lambda i: (0, 0))
    return pl.pallas_call(
        body, name=name, grid=(t // tm,),
        in_specs=[row(d), row(ya.shape[1]), row(yb.shape[1]), row(yc.shape[1]), row(3 * d), full(bg),
                  full(wa), full(wb), full(wc), full(wo)],
        out_specs=[row(d)] * 5,
        out_shape=[jax.ShapeDtypeStruct((t, d), F32)] + [jax.ShapeDtypeStruct((t, d), BF16)] * 4,
        compiler_params=_cparams())(x1, ya, yb, yc, zg, bg, wa, wb, wc, wo)


def _merge_bwd(dx2, pa, pb, pc, zg, bg, wa, wb, wc, wo, name):
    t, d = dx2.shape
    tm = _row_tile(t, 256)

    def body(dx_ref, pa_ref, pb_ref, pc_ref, zg_ref, bg_ref, wa_ref, wb_ref, wc_ref, wo_ref,
             dpa_ref, dpb_ref, dpc_ref, dzg_ref, dbg_ref, dya_ref, dyb_ref, dyc_ref):
        @pl.when(pl.program_id(0) == 0)
        def _():
            dbg_ref[...] = jnp.zeros_like(dbg_ref)

        dm = _dot_nt(dx_ref[...].astype(BF16), wo_ref[...])
        for k, (p_ref, w_ref, dp_ref, dy_ref) in enumerate(
                ((pa_ref, wa_ref, dpa_ref, dya_ref), (pb_ref, wb_ref, dpb_ref, dyb_ref),
                 (pc_ref, wc_ref, dpc_ref, dyc_ref))):
            sl = slice(k * d, (k + 1) * d)
            gate = jax.nn.sigmoid(zg_ref[:, sl] + bg_ref[:, sl])
            dpr = (dm * gate).astype(BF16)
            dp_ref[...] = dpr
            dzg = dm * p_ref[...].astype(F32) * gate * (1.0 - gate)
            dzg_ref[:, sl] = dzg.astype(BF16)
            dbg_ref[:, sl] += jnp.sum(dzg, axis=0, keepdims=True)
            dy_ref[...] = _dot_nt(dpr, w_ref[...]).astype(dy_ref.dtype)

    row = lambda n: pl.BlockSpec((tm, n), lambda i: (i, 0))
    full = lambda a: pl.BlockSpec(a.shape, lambda i: (0, 0))
    na, nb, nc = wa.shape[0], wb.shape[0], wc.shape[0]
    return pl.pallas_call(
        body, name=name, grid=(t // tm,),
        in_specs=[row(d), row(d), row(d), row(d), row(3 * d), full(bg), full(wa), full(wb), full(wc), full(wo)],
        out_specs=[row(d), row(d), row(d), row(3 * d), pl.BlockSpec((1, 3 * d), lambda i: (0, 0)),
                   row(na), row(nb), row(nc)],
        out_shape=[jax.ShapeDtypeStruct((t, d), BF16)] * 3
        + [jax.ShapeDtypeStruct((t, 3 * d), BF16), jax.ShapeDtypeStruct((1, 3 * d), F32),
           jax.ShapeDtypeStruct((t, na), F32), jax.ShapeDtypeStruct((t, nb), BF16),
           jax.ShapeDtypeStruct((t, nc), BF16)],
        compiler_params=_cparams())(dx2, pa, pb, pc, zg, bg, wa, wb, wc, wo)


def _adamw_math(w, g, m, v):
    bc1 = 1.0 - ADAM_B1 ** ADAM_STEP
    bc2 = 1.0 - ADAM_B2 ** ADAM_STEP
    nm = ADAM_B1 * m + (1.0 - ADAM_B1) * g
    nv = ADAM_B2 * v + (1.0 - ADAM_B2) * (g * g)
    delta = -ADAM_LR * ((nm / bc1) / (jnp.sqrt(nv / bc2) + ADAM_EPS) + ADAM_WD * w)
    return delta, nm, nv


def _div_tile(n, cap, mult):
    best = None
    for cand in range(mult, min(n, cap) + 1, mult):
        if n % cand == 0:
            best = cand
    assert best is not None, (n, cap, mult)
    return best


def _adamw(w, g, m, v, name):
    rows, cols = w.shape
    tr = _div_tile(rows, 256, 8)

    def body(w_ref, g_ref, m_ref, v_ref, d_ref, nm_ref, nv_ref):
        d_ref[...], nm_ref[...], nv_ref[...] = _adamw_math(w_ref[...], g_ref[...], m_ref[...], v_ref[...])

    blk = pl.BlockSpec((tr, cols), lambda i: (i, 0))
    return pl.pallas_call(
        body, name=name, grid=(rows // tr,), in_specs=[blk] * 4, out_specs=[blk] * 3,
        out_shape=[jax.ShapeDtypeStruct((rows, cols), F32)] * 3, compiler_params=_cparams())(w, g, m, v)


def _adamw_slots(w, slots, m, v, name):
    _, hr, cols = w.shape
    tr = _div_tile(hr, 128, 16)

    def body(w_ref, s_ref, m_ref, v_ref, g_ref, d_ref, nm_ref, nv_ref):
        g = s_ref[0, 0].astype(F32)
        for k in range(1, N_CHIPS):
            g = g + s_ref[0, k].astype(F32)
        g_ref[0] = g
        d_ref[0], nm_ref[0], nv_ref[0] = _adamw_math(w_ref[0], g, m_ref[0], v_ref[0])

    blk = pl.BlockSpec((1, tr, cols), lambda h, i: (h, i, 0))
    return pl.pallas_call(
        body, name=name, grid=(2, hr // tr),
        in_specs=[blk, pl.BlockSpec((1, N_CHIPS, tr, cols), lambda h, i: (h, 0, i, 0)), blk, blk],
        out_specs=[blk] * 4, out_shape=[jax.ShapeDtypeStruct((2, hr, cols), F32)] * 4,
        compiler_params=_cparams())(w, slots, m, v)


ANY = pl.BlockSpec(memory_space=pl.ANY)


def _place():
    x, y, c = lax.axis_index("x"), lax.axis_index("y"), lax.axis_index("c")
    other_chips = [(1 - x, y), (x, 1 - y), (1 - x, 1 - y)]
    return x, y, c, other_chips


def _remote(src, dst, send_sem, recv_sem, to):
    return pltpu.make_async_remote_copy(src_ref=src, dst_ref=dst, send_sem=send_sem, recv_sem=recv_sem,
                                        device_id=to, device_id_type=MESH)


def _gather_weights(shards):
    nw = len(shards)

    def body(*refs):
        s_refs, g_refs = refs[:nw], refs[nw:2 * nw]
        send_sems, recv_sems, local_sems = refs[2 * nw:]
        x, y, c, chips = _place()
        me = 2 * x + y
        sibling = (x, y, 1 - c)
        mine = [pltpu.make_async_copy(s_refs[w], g_refs[w].at[me], local_sems.at[w]) for w in range(nw)]
        for cp in mine:
            cp.start()
        first = [_remote(s_refs[w].at[c], g_refs[w].at[me, c], send_sems.at[k, w], recv_sems.at[k, w], (cx, cy, c))
                 for k, (cx, cy) in enumerate(chips) for w in range(nw)]
        for cp in first:
            cp.start()
        passed = []
        for k, (cx, cy) in enumerate(chips):
            for w in range(nw):
                slab = g_refs[w].at[2 * cx + cy, c]
                _remote(slab, slab, send_sems.at[k, w], recv_sems.at[k, w], (cx, cy, c)).wait_recv()
                fwd = _remote(slab, slab, send_sems.at[3 + k, w], recv_sems.at[3 + k, w], sibling)
                fwd.start()
                passed.append(fwd)
        for k, (cx, cy) in enumerate(chips):
            for w in range(nw):
                slab = g_refs[w].at[2 * cx + cy, 1 - c]
                _remote(slab, slab, send_sems.at[3 + k, w], recv_sems.at[3 + k, w], sibling).wait_recv()
        for cp in first + passed:
            cp.wait_send()
        for cp in mine:
            cp.wait()

    return pl.pallas_call(
        body, name="gather_weights", in_specs=[ANY] * nw, out_specs=[ANY] * nw,
        out_shape=[jax.ShapeDtypeStruct((N_CHIPS,) + s.shape, BF16) for s in shards],
        scratch_shapes=[pltpu.SemaphoreType.DMA((6, nw)), pltpu.SemaphoreType.DMA((6, nw)),
                        pltpu.SemaphoreType.DMA((nw,))],
    )(*shards)


def _swap_halves(grads):
    nw = len(grads)

    def body(*refs):
        g_refs, sib_refs = refs[:nw], refs[nw:2 * nw]
        send_sems, recv_sems = refs[2 * nw:]
        x, y, c, _ = _place()
        copies = [_remote(g_refs[w].at[s, 1 - c], sib_refs[w].at[s], send_sems.at[s, w], recv_sems.at[s, w],
                          (x, y, 1 - c)) for w in range(nw) for s in range(N_CHIPS)]
        for cp in copies:
            cp.start()
        for cp in copies:
            cp.wait_recv()
        for cp in copies:
            cp.wait_send()

    return pl.pallas_call(
        body, name="grad_swap_halves", in_specs=[ANY] * nw, out_specs=[ANY] * nw,
        out_shape=[jax.ShapeDtypeStruct((N_CHIPS,) + g.shape[2:], BF16) for g in grads],
        scratch_shapes=[pltpu.SemaphoreType.DMA((N_CHIPS, nw)), pltpu.SemaphoreType.DMA((N_CHIPS, nw))],
    )(*grads)


def _pair_sum(grad, sib, core, name):
    nchip, _, hr, cols = grad.shape
    tr = _div_tile(hr, 256, 16)

    def body(core_ref, a_ref, b_ref, o_ref):
        o_ref[...] = (a_ref[0].astype(F32) + b_ref[...].astype(F32)).astype(BF16)

    return pl.pallas_call(
        body, name=name,
        grid_spec=pltpu.PrefetchScalarGridSpec(
            num_scalar_prefetch=1, grid=(nchip, hr // tr),
            in_specs=[pl.BlockSpec((1, 1, tr, cols), lambda s, i, core_r: (s, core_r[0], i, 0)),
                      pl.BlockSpec((1, tr, cols), lambda s, i, core_r: (s, i, 0))],
            out_specs=pl.BlockSpec((1, tr, cols), lambda s, i, core_r: (s, i, 0))),
        out_shape=jax.ShapeDtypeStruct((nchip, hr, cols), BF16), compiler_params=_cparams())(core, grad, sib)


def _exchange_pair_sums(sums):
    nw = len(sums)

    def body(*refs):
        p_refs, o_refs = refs[:nw], refs[nw:2 * nw]
        send_sems, recv_sems, local_sems = refs[2 * nw:]
        x, y, c, chips = _place()
        me = 2 * x + y
        sibling = (x, y, 1 - c)
        mine = [pltpu.make_async_copy(p_refs[w].at[me], o_refs[w].at[c, 3], local_sems.at[w]) for w in range(nw)]
        for cp in mine:
            cp.start()
        first = [_remote(p_refs[w].at[2 * cx + cy], o_refs[w].at[c, k], send_sems.at[k, w], recv_sems.at[k, w],
                         (cx, cy, c)) for k, (cx, cy) in enumerate(chips) for w in range(nw)]
        for cp in first:
            cp.start()
        passed = []
        for k in range(N_CHIPS):
            for w in range(nw):
                slab = o_refs[w].at[c, k]
                if k < 3:
                    first[k * nw + w].wait_recv()
                else:
                    mine[w].wait()
                fwd = _remote(slab, slab, send_sems.at[3 + k, w], recv_sems.at[3 + k, w], sibling)
                fwd.start()
                passed.append(fwd)
        for k in range(N_CHIPS):
            for w in range(nw):
                slab = o_refs[w].at[1 - c, k]
                _remote(slab, slab, send_sems.at[3 + k, w], recv_sems.at[3 + k, w], sibling).wait_recv()
        for cp in first + passed:
            cp.wait_send()

    return pl.pallas_call(
        body, name="grad_exchange", in_specs=[ANY] * nw, out_specs=[ANY] * nw,
        out_shape=[jax.ShapeDtypeStruct((2,) + p.shape, BF16) for p in sums],
        scratch_shapes=[pltpu.SemaphoreType.DMA((7, nw)), pltpu.SemaphoreType.DMA((7, nw)),
                        pltpu.SemaphoreType.DMA((nw,))],
    )(*sums)


def _allreduce_small(vec):
    m_per, n = vec.shape

    def body(x_ref, out_ref, gath_ref, send_sems, recv_sems, local_sem):
        x, y, c, chips = _place()
        me, sibling = (x, y, c), (x, y, 1 - c)

        def rows(px, py, pc):
            return gath_ref.at[pl.ds((4 * px + 2 * py + pc) * m_per, m_per), :]

        def copy(k, block, to, src=None):
            return pltpu.make_async_remote_copy(
                src_ref=rows(*block) if src is None else src, dst_ref=rows(*block),
                send_sem=send_sems.at[k], recv_sem=recv_sems.at[k], device_id=to, device_id_type=MESH)

        mine = pltpu.make_async_copy(x_ref, rows(*me), local_sem)
        mine.start()
        first = [copy(0, me, sibling, src=x_ref)]
        first += [copy(1 + j, me, (*chip, c), src=x_ref) for j, chip in enumerate(chips)]
        for cp in first:
            cp.start()
        passed = [copy(4 + j, (*chip, c), sibling) for j, chip in enumerate(chips)]
        for j, chip in enumerate(chips):
            copy(1 + j, (*chip, c), me).wait_recv()
            passed[j].start()
        copy(0, sibling, me).wait_recv()
        for j, chip in enumerate(chips):
            copy(4 + j, (*chip, 1 - c), me).wait_recv()
        for cp in first + passed:
            cp.wait_send()
        mine.wait()
        acc = gath_ref[pl.ds(0, m_per), :]
        for k in range(1, N_DEV):
            acc = acc + gath_ref[pl.ds(k * m_per, m_per), :]
        out_ref[...] = acc

    vm = pl.BlockSpec(memory_space=pltpu.VMEM)
    return pl.pallas_call(
        body, name="allreduce_small", in_specs=[vm], out_specs=vm,
        out_shape=jax.ShapeDtypeStruct((m_per, n), F32),
        scratch_shapes=[pltpu.VMEM((N_DEV * m_per, n), F32), pltpu.SemaphoreType.DMA((7,)),
                        pltpu.SemaphoreType.DMA((7,)), pltpu.SemaphoreType.DMA],
    )(vec)


def _pack_small(vals):
    flat = jnp.concatenate([vals[name].reshape(-1).astype(F32) for name, _ in SMALL])
    flat = jnp.pad(flat, (0, SMALL_ROWS * LANES - flat.shape[0]))
    return flat.reshape(SMALL_ROWS, LANES)


def _unpack_small(packed):
    flat = packed.reshape(-1)
    out, off = {}, 0
    for name, shape in SMALL:
        n = int(np.prod(shape))
        out[name] = flat[off:off + n].reshape(shape)
        off += n
    return out


def _head_pad_cols(w, heads, real):
    k = w.shape[0]
    return jnp.pad(w.reshape(k, heads, real), ((0, 0), (0, 0), (0, LANES - real))).reshape(k, heads * LANES)


def _rope_tables(positions):
    half = MLA_ROPE // 2
    inv = ROPE_BASE ** (-jnp.arange(half, dtype=F32) / half)
    ang = positions.astype(F32)[:, None] * inv
    cos, sin = jnp.cos(ang), jnp.sin(ang)
    t = positions.shape[0]
    z = lambda n: jnp.zeros((t, n), F32)
    rc = jnp.concatenate([jnp.ones((t, MLA_NOPE), F32), cos, cos, z(LANES - MLA_QK)], axis=1)
    rs1 = jnp.concatenate([z(MLA_NOPE), -sin, z(LANES - MLA_NOPE - half)], axis=1)
    rs2 = jnp.concatenate([z(MLA_NOPE + half), sin, z(LANES - MLA_QK)], axis=1)
    return rc, rs1, rs2


def _local_step(x, mem, positions, tgt, small, big):
    d = D_MODEL
    g_ffn1, g_mix, g_ffn2 = small["ffn1_norm"], small["mix_norm"], small["ffn2_norm"]
    wgu1, wd1 = big["ffn1_w_gu#blocks"], big["ffn1_w_down"].reshape(2, FF_TILE, d)
    wgu2, wd2 = big["ffn2_w_gu#blocks"], big["ffn2_w_down"].reshape(2, FF_TILE, d)
    w_in = big["w_in"]
    w_uv_, w_cq, w_ckv = w_in[:, :COL_CQ], w_in[:, COL_CQ:COL_CKV], w_in[:, COL_CKV:COL_KR]
    w_kr = jnp.pad(w_in[:, COL_KR:COL_QM], ((0, 0), (MLA_NOPE, LANES - MLA_QK)))
    w_qm, w_g = w_in[:, COL_QM:COL_GATE], w_in[:, COL_GATE:]
    segs = (w_uv_, w_cq, w_ckv, w_kr, w_qm, w_g)
    wuq = _head_pad_cols(big["mla_w_uq"], MLA_HEADS, MLA_QK)
    ukv = big["mla_w_ukv"].reshape(MLA_KV_RANK, MLA_HEADS, 2, MLA_NOPE)
    wuk = _head_pad_cols(ukv[:, :, 0].reshape(MLA_KV_RANK, -1), MLA_HEADS, MLA_NOPE)
    wuv = _head_pad_cols(ukv[:, :, 1].reshape(MLA_KV_RANK, -1), MLA_HEADS, MLA_NOPE)
    wkv = big["mem_w_kv"]
    wa, wc, wo = big["w_branch_a"], big["w_branch_c"], big["w_out"]
    wb = jnp.pad(big["w_branch_b"].reshape(MLA_HEADS, MLA_NOPE, d),
                 ((0, 0), (0, LANES - MLA_NOPE), (0, 0))).reshape(MLA_HEADS * LANES, d)
    qg = jnp.pad(small["mla_q_norm"], ((0, 0), (0, LANES - MLA_QK)))
    kg = jnp.pad(small["mla_k_norm"], ((0, 0), (0, LANES - MLA_QK)))
    causal = jnp.tril(jnp.ones((CHUNK, CHUNK), bool))
    wt_f = jnp.where(causal[None], small["sg_w"][0], 0.0)
    wt, wt_t = wt_f.astype(BF16), wt_f.transpose(0, 2, 1).astype(BF16)
    bias_l = jnp.repeat(small["sg_b"][0].T, 64, axis=1)
    rc, rs1, rs2 = _rope_tables(positions)

    x1 = _ffn_fwd(x, g_ffn1, wgu1, wd1, "ffn1_fwd")
    h = _rms_fwd(x1, g_mix, "mix_norm_fwd")
    zuv, zcq, zckv, zkr, zqm, zg = [
        _mm([(h, w)], F32, "in_proj_%d" % k) for k, w in enumerate(segs)]
    ya = _sgu_fwd(zuv, small["sg_ln_g"], small["sg_ln_b"], wt, bias_l, "sgu_fwd")
    q, k, v, cqn, ckvn = _mla_prep_fwd(zcq, zckv, zkr, small["mla_cq_norm"], small["mla_ckv_norm"], qg, kg,
                                       wuq, wuk, wuv, rc, rs1, rs2, "mla_prep_fwd")
    yb, lse = _attn_fwd(q, k, v, "mla_attn_fwd")
    km, vm, memn = _mem_kv_fwd(mem, small["mem_norm"], wkv, small["mem_k_norm"], "mem_kv_fwd")
    yc = _mem_attn_fwd(zqm, small["mem_q_norm"], km, vm, "mem_attn_fwd")
    x2, merged, pa, pb, pc = _merge_fwd(x1, ya, yb, yc, zg, small["b_gate"], wa, wb, wc, wo, "merge_fwd")
    x3 = _ffn_fwd(x2, g_ffn2, wgu2, wd2, "ffn2_fwd")
    dy, loss_row = _loss_head(x3, tgt, "loss_head")

    gw, gs = {}, {}

    def ffn_grads(prefix, xin, gain, dyin, wgu, wd):
        dx, dgain, xn, dgt, dup, act = _ffn_bwd(xin, gain, dyin, wgu, wd, prefix + "_bwd")
        gw[prefix + "_w_gu"] = jnp.concatenate(
            [_mm_tn(xn, dgt, prefix + "_dwg"), _mm_tn(xn, dup, prefix + "_dwu")], axis=1)
        gw[prefix + "_w_down"] = _mm_tn(act, dyin, prefix + "_dwd", scale=0.5)
        gs[prefix + "_norm"] = dgain
        return dx

    dx2 = ffn_grads("ffn2", x2, g_ffn2, dy, wgu2, wd2)
    dpa, dpb, dpc, dzg, dbg, dya, dyb, dyc = _merge_bwd(dx2, pa, pb, pc, zg, small["b_gate"], wa, wb, wc, wo,
                                                        "merge_bwd")
    gs["b_gate"] = dbg
    gw["w_out"] = _mm_tn(merged, dx2, "dw_out")
    gw["w_branch_a"] = _mm_tn(ya, dpa, "dw_branch_a")
    gw["w_branch_b"] = _mm_tn(yb, dpb, "dw_branch_b").reshape(MLA_HEADS, LANES, d)[:, :MLA_NOPE].reshape(-1, d)
    gw["w_branch_c"] = _mm_tn(yc, dpc, "dw_branch_c")

    dzuv, dwt, dbl, dlg, dlb = _sgu_bwd(zuv, dya, small["sg_ln_g"], small["sg_ln_b"], wt, wt_t, bias_l, "sgu_bwd")
    gs["sg_w"], gs["sg_b"] = dwt[None], dbl[:, :SG_GROUPS].T[None]
    gs["sg_ln_g"], gs["sg_ln_b"] = dlg, dlb

    dq, dk, dv = _attn_bwd(q, k, v, yb, lse, dyb, "mla_attn_bwd")
    dzcq, dzckv, dzkr, dql, dkl, dgcq, dgckv, dqg, dkg = _mla_prep_bwd(
        zcq, zckv, zkr, small["mla_cq_norm"], small["mla_ckv_norm"], qg, kg, wuq, wuk, wuv, rc, rs1, rs2,
        dq, dk, dv, "mla_prep_bwd")
    gs["mla_cq_norm"], gs["mla_ckv_norm"] = dgcq, dgckv
    gs["mla_q_norm"], gs["mla_k_norm"] = dqg[:, :MLA_QK], dkg[:, :MLA_QK]
    gw["mla_w_uq"] = _mm_tn(cqn, dql, "dw_uq").reshape(MLA_Q_RANK, MLA_HEADS, LANES)[:, :, :MLA_QK].reshape(
        MLA_Q_RANK, -1)
    dwuk = _mm_tn(ckvn, dkl, "dw_uk").reshape(MLA_KV_RANK, MLA_HEADS, LANES)[:, :, :MLA_NOPE]
    dwuv = _mm_tn(ckvn, dv, "dw_uv").reshape(MLA_KV_RANK, MLA_HEADS, LANES)[:, :, :MLA_NOPE]
    gw["mla_w_ukv"] = jnp.concatenate([dwuk, dwuv], axis=2).reshape(MLA_KV_RANK, -1)

    dzqm, dkn, dvm, dmqg = _mem_attn_bwd(zqm, dyc, small["mem_q_norm"], km, vm, "mem_attn_bwd")
    gs["mem_q_norm"] = dmqg
    gw["mem_w_kv"], gs["mem_k_norm"], gs["mem_norm"] = _mem_kv_bwd(
        mem, small["mem_norm"], wkv, small["mem_k_norm"], dkn, dvm, "mem_kv_bwd")

    dzs = (dzuv, dzcq, dzckv, dzkr, dzqm, dzg)
    dh = _mm([(dz, w.T) for dz, w in zip(dzs, segs)], F32, "in_proj_bwd")
    dws = [_mm_tn(h, dz, "dw_in_%d" % k) for k, dz in enumerate(dzs)]
    dws[3] = dws[3][:, MLA_NOPE:MLA_QK]
    gw["w_in"] = jnp.concatenate(dws, axis=1)
    dx1, gs["mix_norm"] = _rms_bwd(x1, g_mix, dh, dx2, "mix_norm_bwd")
    dx = ffn_grads("ffn1", x, g_ffn1, dx1, wgu1, wd1)
    return loss_row, dx, gw, gs


def kernel(x, mem, positions, ffn1_norm, ffn1_w_gu, ffn1_w_down, mix_norm, w_in, b_gate, sg_ln_g, sg_ln_b, sg_w, sg_b, mla_cq_norm, mla_w_uq, mla_ckv_norm, mla_w_ukv, mla_q_norm, mla_k_norm, mem_norm, mem_w_kv, mem_q_norm, mem_k_norm, w_branch_a, w_branch_b, w_branch_c, w_out, ffn2_norm, ffn2_w_gu, ffn2_w_down, loss_target, m_ffn1_norm, m_ffn1_w_gu, m_ffn1_w_down, m_mix_norm, m_w_in, m_b_gate, m_sg_ln_g, m_sg_ln_b, m_sg_w, m_sg_b, m_mla_cq_norm, m_mla_w_uq, m_mla_ckv_norm, m_mla_w_ukv, m_mla_q_norm, m_mla_k_norm, m_mem_norm, m_mem_w_kv, m_mem_q_norm, m_mem_k_norm, m_w_branch_a, m_w_branch_b, m_w_branch_c, m_w_out, m_ffn2_norm, m_ffn2_w_gu, m_ffn2_w_down, v_ffn1_norm, v_ffn1_w_gu, v_ffn1_w_down, v_mix_norm, v_w_in, v_b_gate, v_sg_ln_g, v_sg_ln_b, v_sg_w, v_sg_b, v_mla_cq_norm, v_mla_w_uq, v_mla_ckv_norm, v_mla_w_ukv, v_mla_q_norm, v_mla_k_norm, v_mem_norm, v_mem_w_kv, v_mem_q_norm, v_mem_k_norm, v_w_branch_a, v_w_branch_b, v_w_branch_c, v_w_out, v_ffn2_norm, v_ffn2_w_gu, v_ffn2_w_down):
    args = dict(locals())
    weights = {n: args[n] for n in WEIGHT_ORDER}
    mom_m = {n: args["m_" + n] for n in WEIGHT_ORDER}
    mom_v = {n: args["v_" + n] for n in WEIGHT_ORDER}
    small = {n: weights[n] for n, _ in SMALL}
    halves = lambda a, r, c: a.reshape(2, r // 2, c)

    gathered = _gather_weights([halves(weights[n][0].astype(BF16), r, c) for n, r, c, _ in SHARDED])
    big = {}
    for (name, r, c, kind), g in zip(SHARDED, gathered):
        blk = g.reshape(N_CHIPS, r, c)
        big[name + "#blocks"] = blk
        big[name] = blk.transpose(1, 0, 2).reshape(r, N_CHIPS * c) if kind == "col" else blk.reshape(N_CHIPS * r, c)

    loss_row, dx, gw, gs = _local_step(x[0], mem[0], positions[0], loss_target[0], small, big)
    loss = lax.psum(loss_row[0, 0], ("x", "y", "c"))

    by_owner = []
    for name, r, c, kind in SHARDED:
        blk = gw[name].reshape(r, N_CHIPS, c).transpose(1, 0, 2) if kind == "col" else gw[name].reshape(N_CHIPS, r, c)
        by_owner.append(blk.astype(BF16).reshape(N_CHIPS, 2, r // 2, c))
    received = _swap_halves(by_owner)
    core = lax.axis_index("c").astype(jnp.int32).reshape(1)
    sums = [_pair_sum(g, s, core, "pair_sum_" + n) for g, s, (n, *_) in zip(by_owner, received, SHARDED)]
    slots = _exchange_pair_sums(sums)
    small_grads = _unpack_small(_allreduce_small(_pack_small(gs)))

    grads, deltas, new_m, new_v = {}, {}, {}, {}
    for (name, r, c, _), sl in zip(SHARDED, slots):
        outs = _adamw_slots(halves(weights[name][0], r, c), sl, halves(mom_m[name][0], r, c),
                            halves(mom_v[name][0], r, c), "adamw_" + name)
        shape = weights[name].shape
        grads[name], deltas[name], new_m[name], new_v[name] = [o.reshape(shape) for o in outs]
    dlt, nm, nv = _adamw(_pack_small(small), _pack_small(small_grads), _pack_small({n: mom_m[n] for n, _ in SMALL}),
                         _pack_small({n: mom_v[n] for n, _ in SMALL}), "adamw_small")
    for name, _ in SMALL:
        grads[name] = small_grads[name]
    deltas.update(_unpack_small(dlt))
    new_m.update(_unpack_small(nm))
    new_v.update(_unpack_small(nv))

    return (loss, dx[None], *[grads[n] for n in WEIGHT_ORDER], *[deltas[n] for n in WEIGHT_ORDER],
            *[new_m[n] for n in WEIGHT_ORDER], *[new_v[n] for n in WEIGHT_ORDER])
```

```python
import functools
import math

import numpy as np
import jax
import jax.numpy as jnp
from jax import lax
from jax.experimental import pallas as pl
from jax.experimental.pallas import tpu as pltpu

F32 = jnp.float32
BF16 = jnp.bfloat16

D_MODEL = 1024
D_FF = 2816
FF_TILE = 1408
SG_WIDTH = 512
SG_GROUPS = 8
CHUNK = 128
MLA_HEADS = 8
MLA_QK = 96
MLA_NOPE = 64
MLA_ROPE = 32
MLA_Q_RANK = 384
MLA_KV_RANK = 256
MEM_HEADS = 4
MEM_LEN = 256
LANES = 128
EPS = 1e-6
NEG = -1e30
ROPE_BASE = 10000.0
N_CHIPS = 4
N_DEV = 8

ADAM_LR = 0.001
ADAM_B1 = 0.9
ADAM_B2 = 0.999
ADAM_EPS = 1e-08
ADAM_WD = 0.01
ADAM_STEP = 10

COL_V = 512
COL_CQ = 1024
COL_CKV = 1408
COL_KR = 1664
COL_QM = 1696
COL_GATE = 2208
IN_COLS = 5280

VMEM_LIMIT_BYTES = 56 * 1024 * 1024
INV_SQRT2 = 0.7071067811865476
INV_SQRT_2PI = 0.3989422804014327

SHARDED = (
    ("ffn1_w_gu", 1024, 1408, "col"),
    ("ffn1_w_down", 704, 1024, "row"),
    ("w_in", 1024, 1320, "col"),
    ("mla_w_uq", 384, 192, "col"),
    ("mla_w_ukv", 256, 256, "col"),
    ("mem_w_kv", 256, 1024, "row"),
    ("w_branch_a", 512, 256, "col"),
    ("w_branch_b", 512, 256, "col"),
    ("w_branch_c", 512, 256, "col"),
    ("w_out", 256, 1024, "row"),
    ("ffn2_w_gu", 1024, 1408, "col"),
    ("ffn2_w_down", 704, 1024, "row"),
)
SMALL = (
    ("ffn1_norm", (1, 1024)), ("mix_norm", (1, 1024)), ("b_gate", (1, 3072)),
    ("sg_ln_g", (1, 512)), ("sg_ln_b", (1, 512)), ("sg_w", (1, 8, 128, 128)),
    ("sg_b", (1, 8, 128)), ("mla_cq_norm", (1, 384)), ("mla_ckv_norm", (1, 256)),
    ("mla_q_norm", (1, 96)), ("mla_k_norm", (1, 96)), ("mem_norm", (1, 1024)),
    ("mem_q_norm", (1, 128)), ("mem_k_norm", (1, 128)), ("ffn2_norm", (1, 1024)),
)
WEIGHT_ORDER = (
    "ffn1_norm", "ffn1_w_gu", "ffn1_w_down", "mix_norm", "w_in", "b_gate", "sg_ln_g", "sg_ln_b",
    "sg_w", "sg_b", "mla_cq_norm", "mla_w_uq", "mla_ckv_norm", "mla_w_ukv", "mla_q_norm",
    "mla_k_norm", "mem_norm", "mem_w_kv", "mem_q_norm", "mem_k_norm", "w_branch_a", "w_branch_b",
    "w_branch_c", "w_out", "ffn2_norm", "ffn2_w_gu", "ffn2_w_down",
)

_N_SMALL = sum(int(np.prod(s)) for _, s in SMALL)
SMALL_ROWS = -(-_N_SMALL // (LANES * 8)) * 8

MESH = pl.DeviceIdType.MESH


def _cparams():
    return pltpu.CompilerParams(vmem_limit_bytes=VMEM_LIMIT_BYTES)


def _dot(a, b):
    return jnp.dot(a, b, preferred_element_type=F32)


def _dot_nt(a, b):
    return lax.dot_general(a, b, (((1,), (1,)), ((), ())), preferred_element_type=F32)


def _dot_tn(a, b):
    return lax.dot_general(a, b, (((0,), (0,)), ((), ())), preferred_element_type=F32)


def _gelu(x):
    return 0.5 * x * (1.0 + lax.erf(x * INV_SQRT2))


def _gelu_grad(x):
    return 0.5 * (1.0 + lax.erf(x * INV_SQRT2)) + x * jnp.exp(-0.5 * x * x) * INV_SQRT_2PI


def _rstd(x, n):
    return lax.rsqrt(jnp.sum(x * x, axis=-1, keepdims=True) * (1.0 / n) + EPS)


def _rms_vjp(x, r, g, dy, n):
    dxh = dy * g
    dx = r * dxh - x * (r * r * r) * (jnp.sum(dxh * x, axis=-1, keepdims=True) * (1.0 / n))
    return dx, dy * x * r


def _row_tile(t, want):
    return min(t, want)


def _wide_tile(n):
    if n <= 1024:
        return n
    if n % 1024 == 0:
        return 1024
    assert n % FF_TILE == 0, n
    return FF_TILE


def _rms_fwd(x, g, name):
    t, d = x.shape
    tm = _row_tile(t, 512)

    def body(x_ref, g_ref, o_ref):
        xv = x_ref[...]
        o_ref[...] = (xv * _rstd(xv, d) * g_ref[...]).astype(BF16)

    return pl.pallas_call(
        body, name=name, grid=(t // tm,),
        in_specs=[pl.BlockSpec((tm, d), lambda i: (i, 0)), pl.BlockSpec((1, d), lambda i: (0, 0))],
        out_specs=pl.BlockSpec((tm, d), lambda i: (i, 0)),
        out_shape=jax.ShapeDtypeStruct((t, d), BF16), compiler_params=_cparams())(x, g)


def _rms_bwd(x, g, dxn, dres, name):
    t, d = x.shape
    tm = _row_tile(t, 256)

    def body(x_ref, g_ref, d_ref, r_ref, dx_ref, dg_ref):
        @pl.when(pl.program_id(0) == 0)
        def _():
            dg_ref[...] = jnp.zeros_like(dg_ref)

        xv = x_ref[...]
        r = _rstd(xv, d)
        dx, dgr = _rms_vjp(xv, r, g_ref[...], d_ref[...].astype(F32), d)
        dx_ref[...] = r_ref[...] + dx
        dg_ref[...] += jnp.sum(dgr, axis=0, keepdims=True)

    row = pl.BlockSpec((tm, d), lambda i: (i, 0))
    vec = pl.BlockSpec((1, d), lambda i: (0, 0))
    return pl.pallas_call(
        body, name=name, grid=(t // tm,), in_specs=[row, vec, row, row], out_specs=[row, vec],
        out_shape=[jax.ShapeDtypeStruct((t, d), F32), jax.ShapeDtypeStruct((1, d), F32)],
        compiler_params=_cparams())(x, g, dxn, dres)


def _mm(pairs, out_dtype, name):
    t = pairs[0][0].shape[0]
    n = pairs[0][1].shape[1]
    tm = _row_tile(t, 512)
    tn = _wide_tile(n)
    np_ = len(pairs)

    def body(*refs):
        o_ref = refs[2 * np_]
        acc = None
        for a_ref, w_ref in zip(refs[:np_], refs[np_:2 * np_]):
            part = _dot(a_ref[...].astype(BF16), w_ref[...])
            acc = part if acc is None else acc + part
        o_ref[...] = acc.astype(out_dtype)

    in_specs = [pl.BlockSpec((tm, a.shape[1]), lambda i, j: (i, 0)) for a, _ in pairs]
    in_specs += [pl.BlockSpec((w.shape[0], tn), lambda i, j: (0, j)) for _, w in pairs]
    return pl.pallas_call(
        body, name=name, grid=(t // tm, n // tn), in_specs=in_specs,
        out_specs=pl.BlockSpec((tm, tn), lambda i, j: (i, j)),
        out_shape=jax.ShapeDtypeStruct((t, n), out_dtype), compiler_params=_cparams(),
    )(*[a for a, _ in pairs], *[w for _, w in pairs])


def _mm_tn(a, b, name, scale=1.0):
    t, m = a.shape
    n = b.shape[1]
    tm, tn = _wide_tile(m), _wide_tile(n)
    tk = _row_tile(t, 512)
    nk = t // tk

    def body(a_ref, b_ref, o_ref):
        k = pl.program_id(2)

        @pl.when(k == 0)
        def _():
            o_ref[...] = jnp.zeros_like(o_ref)

        o_ref[...] += _dot_tn(a_ref[...].astype(BF16), b_ref[...].astype(BF16))
        if scale != 1.0:
            @pl.when(k == nk - 1)
            def _():
                o_ref[...] = o_ref[...] * scale

    return pl.pallas_call(
        body, name=name, grid=(m // tm, n // tn, nk),
        in_specs=[pl.BlockSpec((tk, tm), lambda i, j, k: (k, i)),
                  pl.BlockSpec((tk, tn), lambda i, j, k: (k, j))],
        out_specs=pl.BlockSpec((tm, tn), lambda i, j, k: (i, j)),
        out_shape=jax.ShapeDtypeStruct((m, n), F32), compiler_params=_cparams())(a, b)


def _ffn_fwd(x, g, wgu4, wd2, name):
    t, d = x.shape
    tm = _row_tile(t, 512)

    def body(x_ref, g_ref, wg_ref, wu_ref, wd_ref, o_ref, xn_scr, acc_scr):
        j = pl.program_id(1)

        @pl.when(j == 0)
        def _():
            xv = x_ref[...]
            xn_scr[...] = (xv * _rstd(xv, d) * g_ref[...]).astype(BF16)
            acc_scr[...] = jnp.zeros_like(acc_scr)

        xn = xn_scr[...]
        gg = _dot(xn, wg_ref[0])
        uu = _dot(xn, wu_ref[0])
        act = gg * jax.nn.sigmoid(gg) * uu
        acc_scr[...] += _dot(act.astype(BF16), wd_ref[0])

        @pl.when(j == 1)
        def _():
            o_ref[...] = x_ref[...] + 0.5 * acc_scr[...]

    row = pl.BlockSpec((tm, d), lambda i, j: (i, 0))
    return pl.pallas_call(
        body, name=name, grid=(t // tm, 2),
        in_specs=[row, pl.BlockSpec((1, d), lambda i, j: (0, 0)),
                  pl.BlockSpec((1, d, FF_TILE), lambda i, j: (j, 0, 0)),
                  pl.BlockSpec((1, d, FF_TILE), lambda i, j: (j + 2, 0, 0)),
                  pl.BlockSpec((1, FF_TILE, d), lambda i, j: (j, 0, 0))],
        out_specs=row, out_shape=jax.ShapeDtypeStruct((t, d), F32),
        scratch_shapes=[pltpu.VMEM((tm, d), BF16), pltpu.VMEM((tm, d), F32)],
        compiler_params=_cparams())(x, g, wgu4, wgu4, wd2)


def _ffn_bwd(x, g, dy, wgu4, wd2, name):
    t, d = x.shape
    tm = _row_tile(t, 256)

    def body(x_ref, g_ref, dy_ref, wg_ref, wu_ref, wd_ref,
             dx_ref, dgain_ref, xn_ref, dg_ref, du_ref, act_ref, acc_scr):
        i, j = pl.program_id(0), pl.program_id(1)

        @pl.when((i == 0) & (j == 0))
        def _():
            dgain_ref[...] = jnp.zeros_like(dgain_ref)

        xv = x_ref[...]
        r = _rstd(xv, d)
        xn = (xv * r * g_ref[...]).astype(BF16)

        @pl.when(j == 0)
        def _():
            xn_ref[...] = xn
            acc_scr[...] = jnp.zeros_like(acc_scr)

        gg = _dot(xn, wg_ref[0])
        uu = _dot(xn, wu_ref[0])
        sg = jax.nn.sigmoid(gg)
        silu = gg * sg
        act_ref[...] = (silu * uu).astype(BF16)
        dyh = (0.5 * dy_ref[...]).astype(BF16)
        dact = _dot_nt(dyh, wd_ref[0])
        du = (dact * silu).astype(BF16)
        dgt = (dact * uu * (sg * (1.0 + gg * (1.0 - sg)))).astype(BF16)
        du_ref[...] = du
        dg_ref[...] = dgt
        acc_scr[...] += _dot_nt(dgt, wg_ref[0]) + _dot_nt(du, wu_ref[0])

        @pl.when(j == 1)
        def _():
            dx, dgr = _rms_vjp(xv, r, g_ref[...], acc_scr[...], d)
            dx_ref[...] = dy_ref[...] + dx
            dgain_ref[...] += jnp.sum(dgr, axis=0, keepdims=True)

    row = pl.BlockSpec((tm, d), lambda i, j: (i, 0))
    vec = pl.BlockSpec((1, d), lambda i, j: (0, 0))
    ffb = pl.BlockSpec((tm, FF_TILE), lambda i, j: (i, j))
    return pl.pallas_call(
        body, name=name, grid=(t // tm, 2),
        in_specs=[row, vec, row,
                  pl.BlockSpec((1, d, FF_TILE), lambda i, j: (j, 0, 0)),
                  pl.BlockSpec((1, d, FF_TILE), lambda i, j: (j + 2, 0, 0)),
                  pl.BlockSpec((1, FF_TILE, d), lambda i, j: (j, 0, 0))],
        out_specs=[row, vec, row, ffb, ffb, ffb],
        out_shape=[jax.ShapeDtypeStruct((t, d), F32), jax.ShapeDtypeStruct((1, d), F32),
                   jax.ShapeDtypeStruct((t, d), BF16), jax.ShapeDtypeStruct((t, D_FF), BF16),
                   jax.ShapeDtypeStruct((t, D_FF), BF16), jax.ShapeDtypeStruct((t, D_FF), BF16)],
        scratch_shapes=[pltpu.VMEM((tm, d), F32)],
        compiler_params=_cparams())(x, g, dy, wgu4, wgu4, wd2)


def _loss_head(y, tgt, name):
    t, d = y.shape
    tm = _row_tile(t, 512)

    def body(y_ref, t_ref, dy_ref, loss_ref):
        @pl.when(pl.program_id(0) == 0)
        def _():
            loss_ref[...] = jnp.zeros_like(loss_ref)

        e = y_ref[...] - t_ref[...]
        dy_ref[...] = e * (1.0 / d)
        part = 0.5 * jnp.sum(jnp.sum(e * e, axis=-1, keepdims=True) * (1.0 / d), axis=0, keepdims=True)
        loss_ref[...] += jnp.broadcast_to(part, loss_ref.shape)

    row = pl.BlockSpec((tm, d), lambda i: (i, 0))
    return pl.pallas_call(
        body, name=name, grid=(t // tm,), in_specs=[row, row],
        out_specs=[row, pl.BlockSpec((1, LANES), lambda i: (0, 0))],
        out_shape=[jax.ShapeDtypeStruct((t, d), F32), jax.ShapeDtypeStruct((1, LANES), F32)],
        compiler_params=_cparams())(y, tgt)


def _sgu_layernorm(vpre, lg, lb):
    v = _gelu(vpre)
    mu = jnp.mean(v, axis=-1, keepdims=True)
    xc = v - mu
    rstd = lax.rsqrt(jnp.mean(xc * xc, axis=-1, keepdims=True) + EPS)
    xhat = xc * rstd
    return xhat, rstd, xhat * lg + lb


def _sgu_fwd(zuv, lg, lb, wt, bias_l, name):
    t = zuv.shape[0]
    tm = _row_tile(t, 512)

    def body(u_ref, v_ref, lg_ref, lb_ref, wt_ref, bl_ref, o_ref, vln_scr):
        _, _, vln = _sgu_layernorm(v_ref[...], lg_ref[...], lb_ref[...])
        vln_scr[...] = vln.astype(BF16)
        lo = lax.broadcasted_iota(jnp.int32, (CHUNK, LANES), 1) < 64
        for c in range(tm // CHUNK):
            rows = slice(c * CHUNK, (c + 1) * CHUNK)
            for p in range(SG_GROUPS // 2):
                cols = slice(p * LANES, (p + 1) * LANES)
                vp = vln_scr[rows, cols]
                mixed = jnp.where(lo, _dot(wt_ref[2 * p], vp), _dot(wt_ref[2 * p + 1], vp)) + bl_ref[:, cols]
                o_ref[rows, cols] = (_gelu(u_ref[rows, cols]) * mixed).astype(BF16)

    half = lambda k: pl.BlockSpec((tm, SG_WIDTH), lambda i: (i, k))
    vec = pl.BlockSpec((1, SG_WIDTH), lambda i: (0, 0))
    return pl.pallas_call(
        body, name=name, grid=(t // tm,),
        in_specs=[half(0), half(1), vec, vec,
                  pl.BlockSpec((SG_GROUPS, CHUNK, CHUNK), lambda i: (0, 0, 0)),
                  pl.BlockSpec((CHUNK, SG_WIDTH), lambda i: (0, 0))],
        out_specs=pl.BlockSpec((tm, SG_WIDTH), lambda i: (i, 0)),
        out_shape=jax.ShapeDtypeStruct((t, SG_WIDTH), BF16),
        scratch_shapes=[pltpu.VMEM((tm, SG_WIDTH), BF16)],
        compiler_params=_cparams())(zuv, zuv, lg, lb, wt, bias_l)


def _sgu_bwd(zuv, dya, lg, lb, wt, wt_t, bias_l, name):
    t = zuv.shape[0]
    tm = _row_tile(t, 256)
    nsteps = t // tm

    def body(u_ref, v_ref, dy_ref, lg_ref, lb_ref, wt_ref, wtt_ref, bl_ref,
             dz_ref, dwt_ref, dbl_ref, dlg_ref, dlb_ref, vln_scr, dvln_scr, dbacc_scr):
        step = pl.program_id(0)

        @pl.when(step == 0)
        def _():
            dwt_ref[...] = jnp.zeros_like(dwt_ref)
            dlg_ref[...] = jnp.zeros_like(dlg_ref)
            dlb_ref[...] = jnp.zeros_like(dlb_ref)
            dbl_ref[...] = jnp.zeros_like(dbl_ref)
            dbacc_scr[...] = jnp.zeros_like(dbacc_scr)

        vpre = v_ref[...]
        lgv = lg_ref[...]
        xhat, rstd, vln = _sgu_layernorm(vpre, lgv, lb_ref[...])
        vln_scr[...] = vln.astype(BF16)
        lo = lax.broadcasted_iota(jnp.int32, (CHUNK, LANES), 1) < 64
        for c in range(tm // CHUNK):
            rows = slice(c * CHUNK, (c + 1) * CHUNK)
            for p in range(SG_GROUPS // 2):
                cols = slice(p * LANES, (p + 1) * LANES)
                vp = vln_scr[rows, cols]
                mixed = jnp.where(lo, _dot(wt_ref[2 * p], vp), _dot(wt_ref[2 * p + 1], vp)) + bl_ref[:, cols]
                upre = u_ref[rows, cols]
                dyp = dy_ref[rows, cols]
                dz_ref[rows, cols] = (dyp * mixed * _gelu_grad(upre)).astype(BF16)
                dm = dyp * _gelu(upre)
                dbacc_scr[:, cols] += dm
                dlo = jnp.where(lo, dm, 0.0).astype(BF16)
                dhi = jnp.where(lo, 0.0, dm).astype(BF16)
                dvln_scr[rows, cols] = _dot(wtt_ref[2 * p], dlo) + _dot(wtt_ref[2 * p + 1], dhi)
                dwt_ref[2 * p] += _dot_nt(dlo, vp)
                dwt_ref[2 * p + 1] += _dot_nt(dhi, vp)
        dvln = dvln_scr[...]
        dlg_ref[...] += jnp.sum(dvln * xhat, axis=0, keepdims=True)
        dlb_ref[...] += jnp.sum(dvln, axis=0, keepdims=True)
        dxh = dvln * lgv
        dv = rstd * (dxh - jnp.mean(dxh, axis=-1, keepdims=True)
                     - xhat * jnp.mean(dxh * xhat, axis=-1, keepdims=True))
        dz_ref[:, SG_WIDTH:] = (dv * _gelu_grad(vpre)).astype(BF16)

        @pl.when(step == nsteps - 1)
        def _():
            rr = lax.broadcasted_iota(jnp.int32, (CHUNK, CHUNK), 0)
            cc = lax.broadcasted_iota(jnp.int32, (CHUNK, CHUNK), 1)
            tril = (cc <= rr).astype(F32)
            for gidx in range(SG_GROUPS):
                dwt_ref[gidx] = dwt_ref[gidx] * tril
            kk = lax.broadcasted_iota(jnp.int32, (SG_WIDTH, LANES), 0)
            gg = lax.broadcasted_iota(jnp.int32, (SG_WIDTH, LANES), 1)
            sel = ((kk // 64) == gg).astype(F32)
            dbl_ref[...] = jnp.dot(dbacc_scr[...], sel, preferred_element_type=F32,
                                   precision=lax.Precision.HIGHEST)

    half = lambda k: pl.BlockSpec((tm, SG_WIDTH), lambda i: (i, k))
    vec = pl.BlockSpec((1, SG_WIDTH), lambda i: (0, 0))
    wspec = pl.BlockSpec((SG_GROUPS, CHUNK, CHUNK), lambda i: (0, 0, 0))
    return pl.pallas_call(
        body, name=name, grid=(nsteps,),
        in_specs=[half(0), half(1), pl.BlockSpec((tm, SG_WIDTH), lambda i: (i, 0)), vec, vec,
                  wspec, wspec, pl.BlockSpec((CHUNK, SG_WIDTH), lambda i: (0, 0))],
        out_specs=[pl.BlockSpec((tm, 2 * SG_WIDTH), lambda i: (i, 0)), wspec,
                   pl.BlockSpec((CHUNK, LANES), lambda i: (0, 0)), vec, vec],
        out_shape=[jax.ShapeDtypeStruct((t, 2 * SG_WIDTH), BF16),
                   jax.ShapeDtypeStruct((SG_GROUPS, CHUNK, CHUNK), F32),
                   jax.ShapeDtypeStruct((CHUNK, LANES), F32),
                   jax.ShapeDtypeStruct((1, SG_WIDTH), F32), jax.ShapeDtypeStruct((1, SG_WIDTH), F32)],
        scratch_shapes=[pltpu.VMEM((tm, SG_WIDTH), BF16), pltpu.VMEM((tm, SG_WIDTH), F32),
                        pltpu.VMEM((CHUNK, SG_WIDTH), F32)],
        compiler_params=_cparams())(zuv, zuv, dya, lg, lb, wt, wt_t, bias_l)


def _rope(x, c, s1, s2):
    return x * c + pltpu.roll(x, LANES - 16, 1) * s1 + pltpu.roll(x, 16, 1) * s2


def _rope_t(dy, c, s1, s2):
    return dy * c + pltpu.roll(dy * s1, 16, 1) + pltpu.roll(dy * s2, LANES - 16, 1)


def _mla_prep_fwd(zcq, zckv, zkr, gcq, gckv, qg, kg, wuq, wuk, wuv, rc, rs1, rs2, name):
    t = zcq.shape[0]
    tm = _row_tile(t, 256)
    hd = MLA_HEADS * LANES

    def body(zcq_ref, zckv_ref, zkr_ref, gcq_ref, gckv_ref, qg_ref, kg_ref, wuq_ref, wuk_ref, wuv_ref,
             c_ref, s1_ref, s2_ref, q_ref, k_ref, v_ref, cqn_ref, ckvn_ref):
        c, s1, s2 = c_ref[...], s1_ref[...], s2_ref[...]
        xq = zcq_ref[...]
        cqn = (xq * _rstd(xq, MLA_Q_RANK) * gcq_ref[...]).astype(BF16)
        cqn_ref[...] = cqn
        ql = _dot(cqn, wuq_ref[...])
        xk = zckv_ref[...]
        ckvn = (xk * _rstd(xk, MLA_KV_RANK) * gckv_ref[...]).astype(BF16)
        ckvn_ref[...] = ckvn
        kl = _dot(ckvn, wuk_ref[...])
        v_ref[...] = _dot(ckvn, wuv_ref[...]).astype(BF16)
        kr = zkr_ref[...]
        for h in range(MLA_HEADS):
            sl = slice(h * LANES, (h + 1) * LANES)
            qh = ql[:, sl]
            q_ref[:, sl] = _rope(qh * _rstd(qh, MLA_QK) * qg_ref[...], c, s1, s2).astype(BF16)
            kh = kl[:, sl] + kr
            k_ref[:, sl] = _rope(kh * _rstd(kh, MLA_QK) * kg_ref[...], c, s1, s2).astype(BF16)

    row = lambda n: pl.BlockSpec((tm, n), lambda i: (i, 0))
    full = lambda a: pl.BlockSpec(a.shape, lambda i: (0, 0))
    return pl.pallas_call(
        body, name=name, grid=(t // tm,),
        in_specs=[row(MLA_Q_RANK), row(MLA_KV_RANK), row(LANES), full(gcq), full(gckv), full(qg), full(kg),
                  full(wuq), full(wuk), full(wuv), row(LANES), row(LANES), row(LANES)],
        out_specs=[row(hd), row(hd), row(hd), row(MLA_Q_RANK), row(MLA_KV_RANK)],
        out_shape=[jax.ShapeDtypeStruct((t, hd), BF16)] * 3
        + [jax.ShapeDtypeStruct((t, MLA_Q_RANK), BF16), jax.ShapeDtypeStruct((t, MLA_KV_RANK), BF16)],
        compiler_params=_cparams(),
    )(zcq, zckv, zkr, gcq, gckv, qg, kg, wuq, wuk, wuv, rc, rs1, rs2)


def _mla_prep_bwd(zcq, zckv, zkr, gcq, gckv, qg, kg, wuq, wuk, wuv, rc, rs1, rs2, dq, dk, dv, name):
    t = zcq.shape[0]
    tm = _row_tile(t, 256)
    hd = MLA_HEADS * LANES

    def body(zcq_ref, zckv_ref, zkr_ref, gcq_ref, gckv_ref, qg_ref, kg_ref, wuq_ref, wuk_ref, wuv_ref,
             c_ref, s1_ref, s2_ref, dq_ref, dk_ref, dv_ref,
             dzcq_ref, dzckv_ref, dzkr_ref, dql_ref, dkl_ref, dgcq_ref, dgckv_ref, dqg_ref, dkg_ref):
        @pl.when(pl.program_id(0) == 0)
        def _():
            for ref in (dgcq_ref, dgckv_ref, dqg_ref, dkg_ref):
                ref[...] = jnp.zeros_like(ref)

        c, s1, s2 = c_ref[...], s1_ref[...], s2_ref[...]
        qgv, kgv = qg_ref[...], kg_ref[...]
        xq = zcq_ref[...]
        rq = _rstd(xq, MLA_Q_RANK)
        ql = _dot((xq * rq * gcq_ref[...]).astype(BF16), wuq_ref[...])
        xk = zckv_ref[...]
        rk = _rstd(xk, MLA_KV_RANK)
        kl = _dot((xk * rk * gckv_ref[...]).astype(BF16), wuk_ref[...])
        kr = zkr_ref[...]
        dqg_acc = jnp.zeros((tm, LANES), F32)
        dkg_acc = jnp.zeros((tm, LANES), F32)
        dkr = jnp.zeros((tm, LANES), F32)
        for h in range(MLA_HEADS):
            sl = slice(h * LANES, (h + 1) * LANES)
            qh = ql[:, sl]
            dqh, dgr = _rms_vjp(qh, _rstd(qh, MLA_QK), qgv, _rope_t(dq_ref[:, sl], c, s1, s2), MLA_QK)
            dql_ref[:, sl] = dqh.astype(BF16)
            dqg_acc += dgr
            kh = kl[:, sl] + kr
            dkh, dgr = _rms_vjp(kh, _rstd(kh, MLA_QK), kgv, _rope_t(dk_ref[:, sl], c, s1, s2), MLA_QK)
            dkl_ref[:, sl] = dkh.astype(BF16)
            dkg_acc += dgr
            dkr += dkh
        dqg_ref[...] += jnp.sum(dqg_acc, axis=0, keepdims=True)
        dkg_ref[...] += jnp.sum(dkg_acc, axis=0, keepdims=True)
        lane = lax.broadcasted_iota(jnp.int32, (tm, LANES), 1)
        dzkr_ref[...] = jnp.where((lane >= MLA_NOPE) & (lane < MLA_QK), dkr, 0.0).astype(BF16)
        dcqn = _dot_nt(dql_ref[...], wuq_ref[...])
        dx, dgr = _rms_vjp(xq, rq, gcq_ref[...], dcqn, MLA_Q_RANK)
        dzcq_ref[...] = dx.astype(BF16)
        dgcq_ref[...] += jnp.sum(dgr, axis=0, keepdims=True)
        dckvn = _dot_nt(dkl_ref[...], wuk_ref[...]) + _dot_nt(dv_ref[...].astype(BF16), wuv_ref[...])
        dx, dgr = _rms_vjp(xk, rk, gckv_ref[...], dckvn, MLA_KV_RANK)
        dzckv_ref[...] = dx.astype(BF16)
        dgckv_ref[...] += jnp.sum(dgr, axis=0, keepdims=True)

    row = lambda n: pl.BlockSpec((tm, n), lambda i: (i, 0))
    full = lambda a: pl.BlockSpec(a.shape, lambda i: (0, 0))
    vec = lambda n: pl.BlockSpec((1, n), lambda i: (0, 0))
    return pl.pallas_call(
        body, name=name, grid=(t // tm,),
        in_specs=[row(MLA_Q_RANK), row(MLA_KV_RANK), row(LANES), full(gcq), full(gckv), full(qg), full(kg),
                  full(wuq), full(wuk), full(wuv), row(LANES), row(LANES), row(LANES), row(hd), row(hd), row(hd)],
        out_specs=[row(MLA_Q_RANK), row(MLA_KV_RANK), row(LANES), row(hd), row(hd),
                   vec(MLA_Q_RANK), vec(MLA_KV_RANK), vec(LANES), vec(LANES)],
        out_shape=[jax.ShapeDtypeStruct((t, MLA_Q_RANK), BF16), jax.ShapeDtypeStruct((t, MLA_KV_RANK), BF16),
                   jax.ShapeDtypeStruct((t, LANES), BF16), jax.ShapeDtypeStruct((t, hd), BF16),
                   jax.ShapeDtypeStruct((t, hd), BF16), jax.ShapeDtypeStruct((1, MLA_Q_RANK), F32),
                   jax.ShapeDtypeStruct((1, MLA_KV_RANK), F32), jax.ShapeDtypeStruct((1, LANES), F32),
                   jax.ShapeDtypeStruct((1, LANES), F32)],
        compiler_params=_cparams(),
    )(zcq, zckv, zkr, gcq, gckv, qg, kg, wuq, wuk, wuv, rc, rs1, rs2, dq, dk, dv)


def _attn_tiles(t):
    tq = 512 if t >= 2048 else 128
    return tq, min(t, 4 * tq), min(t, 2 * tq)


def _causal_keep(tq, tk, i, j):
    row = lax.broadcasted_iota(jnp.int32, (tq, tk), 0)
    col = lax.broadcasted_iota(jnp.int32, (tq, tk), 1)
    return (col - row) <= (i * tq - j * tk)


ATTN_FWD_HEADS_PER_STEP = 2
ATTN_BWD_HEADS_PER_STEP = 2


def _attn_fwd(q, k, v, name):
    t, hd = q.shape
    hp = ATTN_FWD_HEADS_PER_STEP
    tq, tk, _ = _attn_tiles(t)
    pairs = [(i, j) for i in range(t // tq) for j in range(((i + 1) * tq - 1) // tk + 1)]
    ii = np.array([p[0] for p in pairs], np.int32)
    jj = np.array([p[1] for p in pairs], np.int32)
    scale = MLA_QK ** -0.5

    def body(ii_ref, jj_ref, q_ref, k_ref, v_ref, o_ref, lse_ref, m_scr, l_scr, acc_scr):
        s_id = pl.program_id(1)
        i, j = ii_ref[s_id], jj_ref[s_id]
        last = j == ((i + 1) * tq - 1) // tk

        @pl.when(j == 0)
        def _():
            m_scr[...] = jnp.full_like(m_scr, NEG)
            l_scr[...] = jnp.zeros_like(l_scr)
            acc_scr[...] = jnp.zeros_like(acc_scr)

        def step(masked):
            for hh in range(hp):
                sl = slice(hh * LANES, (hh + 1) * LANES)
                s = _dot_nt(q_ref[:, sl], k_ref[:, sl]) * scale
                if masked:
                    s = jnp.where(_causal_keep(tq, tk, i, j), s, NEG)
                m_prev = m_scr[hh]
                m_new = jnp.maximum(m_prev, jnp.max(s, axis=1, keepdims=True))
                p = jnp.exp(s - m_new)
                alpha = jnp.exp(m_prev - m_new)
                l_new = alpha * l_scr[hh] + jnp.sum(p, axis=1, keepdims=True)
                acc = alpha * acc_scr[:, sl] + _dot(p.astype(BF16), v_ref[:, sl])
                if masked:
                    o_ref[:, sl] = (acc / l_new).astype(BF16)
                    lse_ref[:, sl] = jnp.broadcast_to(m_new + jnp.log(l_new), (tq, LANES))
                else:
                    l_scr[hh] = l_new
                    acc_scr[:, sl] = acc
                    m_scr[hh] = m_new

        @pl.when(jnp.logical_not(last))
        def _():
            step(False)

        @pl.when(last)
        def _():
            step(True)

    w = hp * LANES
    qspec = pl.BlockSpec((tq, w), lambda h, s, ii_r, jj_r: (ii_r[s], h))
    kspec = pl.BlockSpec((tk, w), lambda h, s, ii_r, jj_r: (jj_r[s], h))
    return pl.pallas_call(
        body, name=name,
        grid_spec=pltpu.PrefetchScalarGridSpec(
            num_scalar_prefetch=2, grid=(hd // w, len(pairs)), in_specs=[qspec, kspec, kspec],
            out_specs=[qspec, qspec],
            scratch_shapes=[pltpu.VMEM((hp, tq, 1), F32), pltpu.VMEM((hp, tq, 1), F32),
                            pltpu.VMEM((tq, w), F32)]),
        out_shape=[jax.ShapeDtypeStruct((t, hd), BF16), jax.ShapeDtypeStruct((t, hd), F32)],
        compiler_params=_cparams())(jnp.asarray(ii), jnp.asarray(jj), q, k, v)


def _attn_bwd(q, k, v, o, lse, do, name):
    t, hd = q.shape
    hp = ATTN_BWD_HEADS_PER_STEP
    tq, _, tk = _attn_tiles(t)
    nq = t // tq
    pairs = [(i, j) for j in range(t // tk) for i in range((j * tk) // tq, nq)]
    ii = np.array([p[0] for p in pairs], np.int32)
    jj = np.array([p[1] for p in pairs], np.int32)
    scale = MLA_QK ** -0.5

    def body(jj_ref, ii_ref, q_ref, k_ref, v_ref, o_ref, lse_ref, do_ref, dq_ref, dk_ref, dv_ref, dk_scr, dv_scr):
        s_id = pl.program_id(1)
        i, j = ii_ref[s_id], jj_ref[s_id]

        @pl.when(s_id == 0)
        def _():
            dq_ref[...] = jnp.zeros_like(dq_ref)

        @pl.when(i == (j * tk) // tq)
        def _():
            dk_scr[...] = jnp.zeros_like(dk_scr)
            dv_scr[...] = jnp.zeros_like(dv_scr)

        rows = pl.ds(pl.multiple_of(i * tq, tq), tq)

        def step(masked):
            for hh in range(hp):
                sl = slice(hh * LANES, (hh + 1) * LANES)
                qv, kv, dov = q_ref[:, sl], k_ref[:, sl], do_ref[:, sl]
                s = _dot_nt(qv, kv) * scale
                if masked:
                    s = jnp.where(_causal_keep(tq, tk, i, j), s, NEG)
                p = jnp.exp(s - jnp.max(lse_ref[:, sl], axis=1, keepdims=True))
                delta = jnp.sum(dov.astype(F32) * o_ref[:, sl].astype(F32), axis=1, keepdims=True)
                dv_scr[:, sl] += _dot_tn(p.astype(BF16), dov)
                dp = _dot_nt(dov, v_ref[:, sl])
                ds = (p * (dp - delta) * scale).astype(BF16)
                dk_scr[:, sl] += _dot_tn(ds, qv)
                dq_ref[rows, sl] += _dot(ds, kv)

        crosses = (j + 1) * tk - 1 > i * tq

        @pl.when(jnp.logical_not(crosses))
        def _():
            step(False)

        @pl.when(crosses)
        def _():
            step(True)

        @pl.when(i == nq - 1)
        def _():
            dk_ref[...] = dk_scr[...]
            dv_ref[...] = dv_scr[...]

    w = hp * LANES
    qspec = pl.BlockSpec((tq, w), lambda h, s, jj_r, ii_r: (ii_r[s], h))
    kspec = pl.BlockSpec((tk, w), lambda h, s, jj_r, ii_r: (jj_r[s], h))
    return pl.pallas_call(
        body, name=name,
        grid_spec=pltpu.PrefetchScalarGridSpec(
            num_scalar_prefetch=2, grid=(hd // w, len(pairs)),
            in_specs=[qspec, kspec, kspec, qspec, qspec, qspec],
            out_specs=[pl.BlockSpec((t, w), lambda h, s, jj_r, ii_r: (0, h)), kspec, kspec],
            scratch_shapes=[pltpu.VMEM((tk, w), F32), pltpu.VMEM((tk, w), F32)]),
        out_shape=[jax.ShapeDtypeStruct((t, hd), F32)] * 3,
        compiler_params=_cparams())(jnp.asarray(jj), jnp.asarray(ii), q, k, v, o, lse, do)


MEM_W = MEM_HEADS * LANES


def _mem_kv_fwd(mem, gmem, wkv, kg, name):
    m, d = mem.shape

    def body(mem_ref, g_ref, w_ref, kg_ref, k_ref, v_ref, mn_ref):
        xv = mem_ref[...]
        mn = (xv * _rstd(xv, d) * g_ref[...]).astype(BF16)
        mn_ref[...] = mn
        kvm = _dot(mn, w_ref[...])
        v_ref[...] = kvm[:, MEM_W:].astype(BF16)
        for h in range(MEM_HEADS):
            sl = slice(h * LANES, (h + 1) * LANES)
            kh = kvm[:, sl]
            k_ref[:, sl] = (kh * _rstd(kh, LANES) * kg_ref[...]).astype(BF16)

    full = lambda a: pl.BlockSpec(a.shape, lambda i: (0, 0))
    return pl.pallas_call(
        body, name=name, grid=(1,), in_specs=[full(mem), full(gmem), full(wkv), full(kg)],
        out_specs=[pl.BlockSpec((m, MEM_W), lambda i: (0, 0)), pl.BlockSpec((m, MEM_W), lambda i: (0, 0)),
                   pl.BlockSpec((m, d), lambda i: (0, 0))],
        out_shape=[jax.ShapeDtypeStruct((m, MEM_W), BF16), jax.ShapeDtypeStruct((m, MEM_W), BF16),
                   jax.ShapeDtypeStruct((m, d), BF16)],
        compiler_params=_cparams())(mem, gmem, wkv, kg)


def _mem_softmax(qn, kh):
    s = _dot_nt(qn, kh) * (LANES ** -0.5)
    e = jnp.exp(s - jnp.max(s, axis=1, keepdims=True))
    return e / jnp.sum(e, axis=1, keepdims=True)


def _mem_attn_fwd(zqm, qg, km, vm, name):
    t = zqm.shape[0]
    tm = _row_tile(t, 512)

    def body(q_ref, qg_ref, k_ref, v_ref, o_ref):
        for h in range(MEM_HEADS):
            sl = slice(h * LANES, (h + 1) * LANES)
            qh = q_ref[:, sl]
            qn = (qh * _rstd(qh, LANES) * qg_ref[...]).astype(BF16)
            p = _mem_softmax(qn, k_ref[:, sl])
            o_ref[:, sl] = _dot(p.astype(BF16), v_ref[:, sl]).astype(BF16)

    row = pl.BlockSpec((tm, MEM_W), lambda i: (i, 0))
    full = lambda a: pl.BlockSpec(a.shape, lambda i: (0, 0))
    return pl.pallas_call(
        body, name=name, grid=(t // tm,), in_specs=[row, full(qg), full(km), full(vm)], out_specs=row,
        out_shape=jax.ShapeDtypeStruct((t, MEM_W), BF16), compiler_params=_cparams())(zqm, qg, km, vm)


def _mem_attn_bwd(zqm, dyc, qg, km, vm, name):
    t = zqm.shape[0]
    m = km.shape[0]
    tm = _row_tile(t, 256)

    def body(q_ref, dy_ref, qg_ref, k_ref, v_ref, dz_ref, dk_ref, dv_ref, dqg_ref):
        @pl.when(pl.program_id(0) == 0)
        def _():
            dk_ref[...] = jnp.zeros_like(dk_ref)
            dv_ref[...] = jnp.zeros_like(dv_ref)
            dqg_ref[...] = jnp.zeros_like(dqg_ref)

        qgv = qg_ref[...]
        dqg_acc = jnp.zeros((tm, LANES), F32)
        for h in range(MEM_HEADS):
            sl = slice(h * LANES, (h + 1) * LANES)
            qh = q_ref[:, sl]
            r = _rstd(qh, LANES)
            qn = (qh * r * qgv).astype(BF16)
            kh = k_ref[:, sl]
            p = _mem_softmax(qn, kh)
            dov = dy_ref[:, sl]
            dv_ref[:, sl] += _dot_tn(p.astype(BF16), dov)
            dp = _dot_nt(dov, v_ref[:, sl])
            ds = (p * (dp - jnp.sum(dp * p, axis=1, keepdims=True)) * (LANES ** -0.5)).astype(BF16)
            dk_ref[:, sl] += _dot_tn(ds, qn)
            dqh, dgr = _rms_vjp(qh, r, qgv, _dot(ds, kh), LANES)
            dz_ref[:, sl] = dqh.astype(BF16)
            dqg_acc += dgr
        dqg_ref[...] += jnp.sum(dqg_acc, axis=0, keepdims=True)

    row = pl.BlockSpec((tm, MEM_W), lambda i: (i, 0))
    full = lambda a: pl.BlockSpec(a.shape, lambda i: (0, 0))
    acc = pl.BlockSpec((m, MEM_W), lambda i: (0, 0))
    return pl.pallas_call(
        body, name=name, grid=(t // tm,), in_specs=[row, row, full(qg), full(km), full(vm)],
        out_specs=[row, acc, acc, pl.BlockSpec((1, LANES), lambda i: (0, 0))],
        out_shape=[jax.ShapeDtypeStruct((t, MEM_W), BF16), jax.ShapeDtypeStruct((m, MEM_W), F32),
                   jax.ShapeDtypeStruct((m, MEM_W), F32), jax.ShapeDtypeStruct((1, LANES), F32)],
        compiler_params=_cparams())(zqm, dyc, qg, km, vm)


def _mem_kv_bwd(mem, gmem, wkv, kg, dkn, dvm, name):
    m, d = mem.shape

    def body(mem_ref, g_ref, w_ref, kg_ref, dk_ref, dv_ref, dw_ref, dkg_ref, dg_ref, dkv_scr):
        xv = mem_ref[...]
        r = _rstd(xv, d)
        mn = (xv * r * g_ref[...]).astype(BF16)
        kvm = _dot(mn, w_ref[...])
        dkv_scr[:, MEM_W:] = dv_ref[...].astype(BF16)
        dkg_acc = jnp.zeros((m, LANES), F32)
        for h in range(MEM_HEADS):
            sl = slice(h * LANES, (h + 1) * LANES)
            kh = kvm[:, sl]
            dkh, dgr = _rms_vjp(kh, _rstd(kh, LANES), kg_ref[...], dk_ref[:, sl], LANES)
            dkv_scr[:, sl] = dkh.astype(BF16)
            dkg_acc += dgr
        dkg_ref[...] = jnp.sum(dkg_acc, axis=0, keepdims=True)
        dkv = dkv_scr[...]
        dw_ref[...] = _dot_tn(mn, dkv)
        dmn = _dot_nt(dkv, w_ref[...])
        dg_ref[...] = jnp.sum(dmn * xv * r, axis=0, keepdims=True)

    full = lambda a: pl.BlockSpec(a.shape, lambda i: (0, 0))
    return pl.pallas_call(
        body, name=name, grid=(1,),
        in_specs=[full(mem), full(gmem), full(wkv), full(kg), full(dkn), full(dvm)],
        out_specs=[pl.BlockSpec((d, 2 * MEM_W), lambda i: (0, 0)), pl.BlockSpec((1, LANES), lambda i: (0, 0)),
                   pl.BlockSpec((1, d), lambda i: (0, 0))],
        out_shape=[jax.ShapeDtypeStruct((d, 2 * MEM_W), F32), jax.ShapeDtypeStruct((1, LANES), F32),
                   jax.ShapeDtypeStruct((1, d), F32)],
        scratch_shapes=[pltpu.VMEM((m, 2 * MEM_W), BF16)],
        compiler_params=_cparams())(mem, gmem, wkv, kg, dkn, dvm)


def _merge_fwd(x1, ya, yb, yc, zg, bg, wa, wb, wc, wo, name):
    t, d = x1.shape
    tm = _row_tile(t, 256)

    def body(x_ref, ya_ref, yb_ref, yc_ref, zg_ref, bg_ref, wa_ref, wb_ref, wc_ref, wo_ref,
             x2_ref, mg_ref, pa_ref, pb_ref, pc_ref):
        merged = None
        for k, (y_ref, w_ref, p_ref) in enumerate(
                ((ya_ref, wa_ref, pa_ref), (yb_ref, wb_ref, pb_ref), (yc_ref, wc_ref, pc_ref))):
            sl = slice(k * d, (k + 1) * d)
            pr = _dot(y_ref[...], w_ref[...])
            p_ref[...] = pr.astype(BF16)
            term = jax.nn.sigmoid(zg_ref[:, sl] + bg_ref[:, sl]) * pr
            merged = term if merged is None else merged + term
        mb = merged.astype(BF16)
        mg_ref[...] = mb
        x2_ref[...] = x_ref[...] + _dot(mb, wo_ref[...])

    row = lambda n: pl.BlockSpec((tm, n), lambda i: (i, 0))
    full = lambda a: pl.BlockSpec(a.shape, lambda i: (0, 0))
    return pl.pallas_call(
        body, name=name, grid=(t // tm,),
        in_specs=[row(d), row(ya.shape[1]), row(yb.shape[1]), row(yc.shape[1]), row(3 * d), full(bg),
                  full(wa), full(wb), full(wc), full(wo)],
        out_specs=[row(d)] * 5,
        out_shape=[jax.ShapeDtypeStruct((t, d), F32)] + [jax.ShapeDtypeStruct((t, d), BF16)] * 4,
        compiler_params=_cparams())(x1, ya, yb, yc, zg, bg, wa, wb, wc, wo)


def _merge_bwd(dx2, pa, pb, pc, zg, bg, wa, wb, wc, wo, name):
    t, d = dx2.shape
    tm = _row_tile(t, 256)

    def body(dx_ref, pa_ref, pb_ref, pc_ref, zg_ref, bg_ref, wa_ref, wb_ref, wc_ref, wo_ref,
             dpa_ref, dpb_ref, dpc_ref, dzg_ref, dbg_ref, dya_ref, dyb_ref, dyc_ref):
        @pl.when(pl.program_id(0) == 0)
        def _():
            dbg_ref[...] = jnp.zeros_like(dbg_ref)

        dm = _dot_nt(dx_ref[...].astype(BF16), wo_ref[...])
        for k, (p_ref, w_ref, dp_ref, dy_ref) in enumerate(
                ((pa_ref, wa_ref, dpa_ref, dya_ref), (pb_ref, wb_ref, dpb_ref, dyb_ref),
                 (pc_ref, wc_ref, dpc_ref, dyc_ref))):
            sl = slice(k * d, (k + 1) * d)
            gate = jax.nn.sigmoid(zg_ref[:, sl] + bg_ref[:, sl])
            dpr = (dm * gate).astype(BF16)
            dp_ref[...] = dpr
            dzg = dm * p_ref[...].astype(F32) * gate * (1.0 - gate)
            dzg_ref[:, sl] = dzg.astype(BF16)
            dbg_ref[:, sl] += jnp.sum(dzg, axis=0, keepdims=True)
            dy_ref[...] = _dot_nt(dpr, w_ref[...]).astype(dy_ref.dtype)

    row = lambda n: pl.BlockSpec((tm, n), lambda i: (i, 0))
    full = lambda a: pl.BlockSpec(a.shape, lambda i: (0, 0))
    na, nb, nc = wa.shape[0], wb.shape[0], wc.shape[0]
    return pl.pallas_call(
        body, name=name, grid=(t // tm,),
        in_specs=[row(d), row(d), row(d), row(d), row(3 * d), full(bg), full(wa), full(wb), full(wc), full(wo)],
        out_specs=[row(d), row(d), row(d), row(3 * d), pl.BlockSpec((1, 3 * d), lambda i: (0, 0)),
                   row(na), row(nb), row(nc)],
        out_shape=[jax.ShapeDtypeStruct((t, d), BF16)] * 3
        + [jax.ShapeDtypeStruct((t, 3 * d), BF16), jax.ShapeDtypeStruct((1, 3 * d), F32),
           jax.ShapeDtypeStruct((t, na), F32), jax.ShapeDtypeStruct((t, nb), BF16),
           jax.ShapeDtypeStruct((t, nc), BF16)],
        compiler_params=_cparams())(dx2, pa, pb, pc, zg, bg, wa, wb, wc, wo)


def _adamw_math(w, g, m, v):
    bc1 = 1.0 - ADAM_B1 ** ADAM_STEP
    bc2 = 1.0 - ADAM_B2 ** ADAM_STEP
    nm = ADAM_B1 * m + (1.0 - ADAM_B1) * g
    nv = ADAM_B2 * v + (1.0 - ADAM_B2) * (g * g)
    delta = -ADAM_LR * ((nm / bc1) / (jnp.sqrt(nv / bc2) + ADAM_EPS) + ADAM_WD * w)
    return delta, nm, nv


def _div_tile(n, cap, mult):
    best = None
    for cand in range(mult, min(n, cap) + 1, mult):
        if n % cand == 0:
            best = cand
    assert best is not None, (n, cap, mult)
    return best


def _adamw(w, g, m, v, name):
    rows, cols = w.shape
    tr = _div_tile(rows, 256, 8)

    def body(w_ref, g_ref, m_ref, v_ref, d_ref, nm_ref, nv_ref):
        d_ref[...], nm_ref[...], nv_ref[...] = _adamw_math(w_ref[...], g_ref[...], m_ref[...], v_ref[...])

    blk = pl.BlockSpec((tr, cols), lambda i: (i, 0))
    return pl.pallas_call(
        body, name=name, grid=(rows // tr,), in_specs=[blk] * 4, out_specs=[blk] * 3,
        out_shape=[jax.ShapeDtypeStruct((rows, cols), F32)] * 3, compiler_params=_cparams())(w, g, m, v)


def _adamw_slots(w, slots, m, v, name):
    _, hr, cols = w.shape
    tr = _div_tile(hr, 128, 16)

    def body(w_ref, s_ref, m_ref, v_ref, g_ref, d_ref, nm_ref, nv_ref):
        g = s_ref[0, 0].astype(F32)
        for k in range(1, N_CHIPS):
            g = g + s_ref[0, k].astype(F32)
        g_ref[0] = g
        d_ref[0], nm_ref[0], nv_ref[0] = _adamw_math(w_ref[0], g, m_ref[0], v_ref[0])

    blk = pl.BlockSpec((1, tr, cols), lambda h, i: (h, i, 0))
    return pl.pallas_call(
        body, name=name, grid=(2, hr // tr),
        in_specs=[blk, pl.BlockSpec((1, N_CHIPS, tr, cols), lambda h, i: (h, 0, i, 0)), blk, blk],
        out_specs=[blk] * 4, out_shape=[jax.ShapeDtypeStruct((2, hr, cols), F32)] * 4,
        compiler_params=_cparams())(w, slots, m, v)


ANY = pl.BlockSpec(memory_space=pl.ANY)


def _place():
    x, y, c = lax.axis_index("x"), lax.axis_index("y"), lax.axis_index("c")
    other_chips = [(1 - x, y), (x, 1 - y), (1 - x, 1 - y)]
    return x, y, c, other_chips


def _remote(src, dst, send_sem, recv_sem, to):
    return pltpu.make_async_remote_copy(src_ref=src, dst_ref=dst, send_sem=send_sem, recv_sem=recv_sem,
                                        device_id=to, device_id_type=MESH)


def _gather_weights(shards):
    nw = len(shards)

    def body(*refs):
        s_refs, g_refs = refs[:nw], refs[nw:2 * nw]
        send_sems, recv_sems, local_sems = refs[2 * nw:]
        x, y, c, chips = _place()
        me = 2 * x + y
        sibling = (x, y, 1 - c)
        mine = [pltpu.make_async_copy(s_refs[w], g_refs[w].at[me], local_sems.at[w]) for w in range(nw)]
        for cp in mine:
            cp.start()
        first = [_remote(s_refs[w].at[c], g_refs[w].at[me, c], send_sems.at[k, w], recv_sems.at[k, w], (cx, cy, c))
                 for k, (cx, cy) in enumerate(chips) for w in range(nw)]
        for cp in first:
            cp.start()
        passed = []
        for k, (cx, cy) in enumerate(chips):
            for w in range(nw):
                slab = g_refs[w].at[2 * cx + cy, c]
                _remote(slab, slab, send_sems.at[k, w], recv_sems.at[k, w], (cx, cy, c)).wait_recv()
                fwd = _remote(slab, slab, send_sems.at[3 + k, w], recv_sems.at[3 + k, w], sibling)
                fwd.start()
                passed.append(fwd)
        for k, (cx, cy) in enumerate(chips):
            for w in range(nw):
                slab = g_refs[w].at[2 * cx + cy, 1 - c]
                _remote(slab, slab, send_sems.at[3 + k, w], recv_sems.at[3 + k, w], sibling).wait_recv()
        for cp in first + passed:
            cp.wait_send()
        for cp in mine:
            cp.wait()

    return pl.pallas_call(
        body, name="gather_weights", in_specs=[ANY] * nw, out_specs=[ANY] * nw,
        out_shape=[jax.ShapeDtypeStruct((N_CHIPS,) + s.shape, BF16) for s in shards],
        scratch_shapes=[pltpu.SemaphoreType.DMA((6, nw)), pltpu.SemaphoreType.DMA((6, nw)),
                        pltpu.SemaphoreType.DMA((nw,))],
    )(*shards)


def _swap_halves(grads):
    nw = len(grads)

    def body(*refs):
        g_refs, sib_refs = refs[:nw], refs[nw:2 * nw]
        send_sems, recv_sems = refs[2 * nw:]
        x, y, c, _ = _place()
        copies = [_remote(g_refs[w].at[s, 1 - c], sib_refs[w].at[s], send_sems.at[s, w], recv_sems.at[s, w],
                          (x, y, 1 - c)) for w in range(nw) for s in range(N_CHIPS)]
        for cp in copies:
            cp.start()
        for cp in copies:
            cp.wait_recv()
        for cp in copies:
            cp.wait_send()

    return pl.pallas_call(
        body, name="grad_swap_halves", in_specs=[ANY] * nw, out_specs=[ANY] * nw,
        out_shape=[jax.ShapeDtypeStruct((N_CHIPS,) + g.shape[2:], BF16) for g in grads],
        scratch_shapes=[pltpu.SemaphoreType.DMA((N_CHIPS, nw)), pltpu.SemaphoreType.DMA((N_CHIPS, nw))],
    )(*grads)


def _pair_sum(grad, sib, core, name):
    nchip, _, hr, cols = grad.shape
    tr = _div_tile(hr, 256, 16)

    def body(core_ref, a_ref, b_ref, o_ref):
        o_ref[...] = (a_ref[0].astype(F32) + b_ref[...].astype(F32)).astype(BF16)

    return pl.pallas_call(
        body, name=name,
        grid_spec=pltpu.PrefetchScalarGridSpec(
            num_scalar_prefetch=1, grid=(nchip, hr // tr),
            in_specs=[pl.BlockSpec((1, 1, tr, cols), lambda s, i, core_r: (s, core_r[0], i, 0)),
                      pl.BlockSpec((1, tr, cols), lambda s, i, core_r: (s, i, 0))],
            out_specs=pl.BlockSpec((1, tr, cols), lambda s, i, core_r: (s, i, 0))),
        out_shape=jax.ShapeDtypeStruct((nchip, hr, cols), BF16), compiler_params=_cparams())(core, grad, sib)


def _exchange_pair_sums(sums):
    nw = len(sums)

    def body(*refs):
        p_refs, o_refs = refs[:nw], refs[nw:2 * nw]
        send_sems, recv_sems, local_sems = refs[2 * nw:]
        x, y, c, chips = _place()
        me = 2 * x + y
        sibling = (x, y, 1 - c)
        mine = [pltpu.make_async_copy(p_refs[w].at[me], o_refs[w].at[c, 3], local_sems.at[w]) for w in range(nw)]
        for cp in mine:
            cp.start()
        first = [_remote(p_refs[w].at[2 * cx + cy], o_refs[w].at[c, k], send_sems.at[k, w], recv_sems.at[k, w],
                         (cx, cy, c)) for k, (cx, cy) in enumerate(chips) for w in range(nw)]
        for cp in first:
            cp.start()
        passed = []
        for k in range(N_CHIPS):
            for w in range(nw):
                slab = o_refs[w].at[c, k]
                if k < 3:
                    first[k * nw + w].wait_recv()
                else:
                    mine[w].wait()
                fwd = _remote(slab, slab, send_sems.at[3 + k, w], recv_sems.at[3 + k, w], sibling)
                fwd.start()
                passed.append(fwd)
        for k in range(N_CHIPS):
            for w in range(nw):
                slab = o_refs[w].at[1 - c, k]
                _remote(slab, slab, send_sems.at[3 + k, w], recv_sems.at[3 + k, w], sibling).wait_recv()
        for cp in first + passed:
            cp.wait_send()

    return pl.pallas_call(
        body, name="grad_exchange", in_specs=[ANY] * nw, out_specs=[ANY] * nw,
        out_shape=[jax.ShapeDtypeStruct((2,) + p.shape, BF16) for p in sums],
        scratch_shapes=[pltpu.SemaphoreType.DMA((7, nw)), pltpu.SemaphoreType.DMA((7, nw)),
                        pltpu.SemaphoreType.DMA((nw,))],
    )(*sums)


def _allreduce_small(vec):
    m_per, n = vec.shape

    def body(x_ref, out_ref, gath_ref, send_sems, recv_sems, local_sem):
        x, y, c, chips = _place()
        me, sibling = (x, y, c), (x, y, 1 - c)

        def rows(px, py, pc):
            return gath_ref.at[pl.ds((4 * px + 2 * py + pc) * m_per, m_per), :]

        def copy(k, block, to, src=None):
            return pltpu.make_async_remote_copy(
                src_ref=rows(*block) if src is None else src, dst_ref=rows(*block),
                send_sem=send_sems.at[k], recv_sem=recv_sems.at[k], device_id=to, device_id_type=MESH)

        mine = pltpu.make_async_copy(x_ref, rows(*me), local_sem)
        mine.start()
        first = [copy(0, me, sibling, src=x_ref)]
        first += [copy(1 + j, me, (*chip, c), src=x_ref) for j, chip in enumerate(chips)]
        for cp in first:
            cp.start()
        passed = [copy(4 + j, (*chip, c), sibling) for j, chip in enumerate(chips)]
        for j, chip in enumerate(chips):
            copy(1 + j, (*chip, c), me).wait_recv()
            passed[j].start()
        copy(0, sibling, me).wait_recv()
        for j, chip in enumerate(chips):
            copy(4 + j, (*chip, 1 - c), me).wait_recv()
        for cp in first + passed:
            cp.wait_send()
        mine.wait()
        acc = gath_ref[pl.ds(0, m_per), :]
        for k in range(1, N_DEV):
            acc = acc + gath_ref[pl.ds(k * m_per, m_per), :]
        out_ref[...] = acc

    vm = pl.BlockSpec(memory_space=pltpu.VMEM)
    return pl.pallas_call(
        body, name="allreduce_small", in_specs=[vm], out_specs=vm,
        out_shape=jax.ShapeDtypeStruct((m_per, n), F32),
        scratch_shapes=[pltpu.VMEM((N_DEV * m_per, n), F32), pltpu.SemaphoreType.DMA((7,)),
                        pltpu.SemaphoreType.DMA((7,)), pltpu.SemaphoreType.DMA],
    )(vec)


def _pack_small(vals):
    flat = jnp.concatenate([vals[name].reshape(-1).astype(F32) for name, _ in SMALL])
    flat = jnp.pad(flat, (0, SMALL_ROWS * LANES - flat.shape[0]))
    return flat.reshape(SMALL_ROWS, LANES)


def _unpack_small(packed):
    flat = packed.reshape(-1)
    out, off = {}, 0
    for name, shape in SMALL:
        n = int(np.prod(shape))
        out[name] = flat[off:off + n].reshape(shape)
        off += n
    return out


def _head_pad_cols(w, heads, real):
    k = w.shape[0]
    return jnp.pad(w.reshape(k, heads, real), ((0, 0), (0, 0), (0, LANES - real))).reshape(k, heads * LANES)


def _rope_tables(positions):
    half = MLA_ROPE // 2
    inv = ROPE_BASE ** (-jnp.arange(half, dtype=F32) / half)
    ang = positions.astype(F32)[:, None] * inv
    cos, sin = jnp.cos(ang), jnp.sin(ang)
    t = positions.shape[0]
    z = lambda n: jnp.zeros((t, n), F32)
    rc = jnp.concatenate([jnp.ones((t, MLA_NOPE), F32), cos, cos, z(LANES - MLA_QK)], axis=1)
    rs1 = jnp.concatenate([z(MLA_NOPE), -sin, z(LANES - MLA_NOPE - half)], axis=1)
    rs2 = jnp.concatenate([z(MLA_NOPE + half), sin, z(LANES - MLA_QK)], axis=1)
    return rc, rs1, rs2


def _local_step(x, mem, positions, tgt, small, big):
    d = D_MODEL
    g_ffn1, g_mix, g_ffn2 = small["ffn1_norm"], small["mix_norm"], small["ffn2_norm"]
    wgu1, wd1 = big["ffn1_w_gu#blocks"], big["ffn1_w_down"].reshape(2, FF_TILE, d)
    wgu2, wd2 = big["ffn2_w_gu#blocks"], big["ffn2_w_down"].reshape(2, FF_TILE, d)
    w_in = big["w_in"]
    w_uv_, w_cq, w_ckv = w_in[:, :COL_CQ], w_in[:, COL_CQ:COL_CKV], w_in[:, COL_CKV:COL_KR]
    w_kr = jnp.pad(w_in[:, COL_KR:COL_QM], ((0, 0), (MLA_NOPE, LANES - MLA_QK)))
    w_qm, w_g = w_in[:, COL_QM:COL_GATE], w_in[:, COL_GATE:]
    segs = (w_uv_, w_cq, w_ckv, w_kr, w_qm, w_g)
    wuq = _head_pad_cols(big["mla_w_uq"], MLA_HEADS, MLA_QK)
    ukv = big["mla_w_ukv"].reshape(MLA_KV_RANK, MLA_HEADS, 2, MLA_NOPE)
    wuk = _head_pad_cols(ukv[:, :, 0].reshape(MLA_KV_RANK, -1), MLA_HEADS, MLA_NOPE)
    wuv = _head_pad_cols(ukv[:, :, 1].reshape(MLA_KV_RANK, -1), MLA_HEADS, MLA_NOPE)
    wkv = big["mem_w_kv"]
    wa, wc, wo = big["w_branch_a"], big["w_branch_c"], big["w_out"]
    wb = jnp.pad(big["w_branch_b"].reshape(MLA_HEADS, MLA_NOPE, d),
                 ((0, 0), (0, LANES - MLA_NOPE), (0, 0))).reshape(MLA_HEADS * LANES, d)
    qg = jnp.pad(small["mla_q_norm"], ((0, 0), (0, LANES - MLA_QK)))
    kg = jnp.pad(small["mla_k_norm"], ((0, 0), (0, LANES - MLA_QK)))
    causal = jnp.tril(jnp.ones((CHUNK, CHUNK), bool))
    wt_f = jnp.where(causal[None], small["sg_w"][0], 0.0)
    wt, wt_t = wt_f.astype(BF16), wt_f.transpose(0, 2, 1).astype(BF16)
    bias_l = jnp.repeat(small["sg_b"][0].T, 64, axis=1)
    rc, rs1, rs2 = _rope_tables(positions)

    x1 = _ffn_fwd(x, g_ffn1, wgu1, wd1, "ffn1_fwd")
    h = _rms_fwd(x1, g_mix, "mix_norm_fwd")
    zuv, zcq, zckv, zkr, zqm, zg = [
        _mm([(h, w)], F32, "in_proj_%d" % k) for k, w in enumerate(segs)]
    ya = _sgu_fwd(zuv, small["sg_ln_g"], small["sg_ln_b"], wt, bias_l, "sgu_fwd")
    q, k, v, cqn, ckvn = _mla_prep_fwd(zcq, zckv, zkr, small["mla_cq_norm"], small["mla_ckv_norm"], qg, kg,
                                       wuq, wuk, wuv, rc, rs1, rs2, "mla_prep_fwd")
    yb, lse = _attn_fwd(q, k, v, "mla_attn_fwd")
    km, vm, memn = _mem_kv_fwd(mem, small["mem_norm"], wkv, small["mem_k_norm"], "mem_kv_fwd")
    yc = _mem_attn_fwd(zqm, small["mem_q_norm"], km, vm, "mem_attn_fwd")
    x2, merged, pa, pb, pc = _merge_fwd(x1, ya, yb, yc, zg, small["b_gate"], wa, wb, wc, wo, "merge_fwd")
    x3 = _ffn_fwd(x2, g_ffn2, wgu2, wd2, "ffn2_fwd")
    dy, loss_row = _loss_head(x3, tgt, "loss_head")

    gw, gs = {}, {}

    def ffn_grads(prefix, xin, gain, dyin, wgu, wd):
        dx, dgain, xn, dgt, dup, act = _ffn_bwd(xin, gain, dyin, wgu, wd, prefix + "_bwd")
        gw[prefix + "_w_gu"] = jnp.concatenate(
            [_mm_tn(xn, dgt, prefix + "_dwg"), _mm_tn(xn, dup, prefix + "_dwu")], axis=1)
        gw[prefix + "_w_down"] = _mm_tn(act, dyin, prefix + "_dwd", scale=0.5)
        gs[prefix + "_norm"] = dgain
        return dx

    dx2 = ffn_grads("ffn2", x2, g_ffn2, dy, wgu2, wd2)
    dpa, dpb, dpc, dzg, dbg, dya, dyb, dyc = _merge_bwd(dx2, pa, pb, pc, zg, small["b_gate"], wa, wb, wc, wo,
                                                        "merge_bwd")
    gs["b_gate"] = dbg
    gw["w_out"] = _mm_tn(merged, dx2, "dw_out")
    gw["w_branch_a"] = _mm_tn(ya, dpa, "dw_branch_a")
    gw["w_branch_b"] = _mm_tn(yb, dpb, "dw_branch_b").reshape(MLA_HEADS, LANES, d)[:, :MLA_NOPE].reshape(-1, d)
    gw["w_branch_c"] = _mm_tn(yc, dpc, "dw_branch_c")

    dzuv, dwt, dbl, dlg, dlb = _sgu_bwd(zuv, dya, small["sg_ln_g"], small["sg_ln_b"], wt, wt_t, bias_l, "sgu_bwd")
    gs["sg_w"], gs["sg_b"] = dwt[None], dbl[:, :SG_GROUPS].T[None]
    gs["sg_ln_g"], gs["sg_ln_b"] = dlg, dlb

    dq, dk, dv = _attn_bwd(q, k, v, yb, lse, dyb, "mla_attn_bwd")
    dzcq, dzckv, dzkr, dql, dkl, dgcq, dgckv, dqg, dkg = _mla_prep_bwd(
        zcq, zckv, zkr, small["mla_cq_norm"], small["mla_ckv_norm"], qg, kg, wuq, wuk, wuv, rc, rs1, rs2,
        dq, dk, dv, "mla_prep_bwd")
    gs["mla_cq_norm"], gs["mla_ckv_norm"] = dgcq, dgckv
    gs["mla_q_norm"], gs["mla_k_norm"] = dqg[:, :MLA_QK], dkg[:, :MLA_QK]
    gw["mla_w_uq"] = _mm_tn(cqn, dql, "dw_uq").reshape(MLA_Q_RANK, MLA_HEADS, LANES)[:, :, :MLA_QK].reshape(
        MLA_Q_RANK, -1)
    dwuk = _mm_tn(ckvn, dkl, "dw_uk").reshape(MLA_KV_RANK, MLA_HEADS, LANES)[:, :, :MLA_NOPE]
    dwuv = _mm_tn(ckvn, dv, "dw_uv").reshape(MLA_KV_RANK, MLA_HEADS, LANES)[:, :, :MLA_NOPE]
    gw["mla_w_ukv"] = jnp.concatenate([dwuk, dwuv], axis=2).reshape(MLA_KV_RANK, -1)

    dzqm, dkn, dvm, dmqg = _mem_attn_bwd(zqm, dyc, small["mem_q_norm"], km, vm, "mem_attn_bwd")
    gs["mem_q_norm"] = dmqg
    gw["mem_w_kv"], gs["mem_k_norm"], gs["mem_norm"] = _mem_kv_bwd(
        mem, small["mem_norm"], wkv, small["mem_k_norm"], dkn, dvm, "mem_kv_bwd")

    dzs = (dzuv, dzcq, dzckv, dzkr, dzqm, dzg)
    dh = _mm([(dz, w.T) for dz, w in zip(dzs, segs)], F32, "in_proj_bwd")
    dws = [_mm_tn(h, dz, "dw_in_%d" % k) for k, dz in enumerate(dzs)]
    dws[3] = dws[3][:, MLA_NOPE:MLA_QK]
    gw["w_in"] = jnp.concatenate(dws, axis=1)
    dx1, gs["mix_norm"] = _rms_bwd(x1, g_mix, dh, dx2, "mix_norm_bwd")
    dx = ffn_grads("ffn1", x, g_ffn1, dx1, wgu1, wd1)
    return loss_row, dx, gw, gs


def kernel(x, mem, positions, ffn1_norm, ffn1_w_gu, ffn1_w_down, mix_norm, w_in, b_gate, sg_ln_g, sg_ln_b, sg_w, sg_b, mla_cq_norm, mla_w_uq, mla_ckv_norm, mla_w_ukv, mla_q_norm, mla_k_norm, mem_norm, mem_w_kv, mem_q_norm, mem_k_norm, w_branch_a, w_branch_b, w_branch_c, w_out, ffn2_norm, ffn2_w_gu, ffn2_w_down, loss_target, m_ffn1_norm, m_ffn1_w_gu, m_ffn1_w_down, m_mix_norm, m_w_in, m_b_gate, m_sg_ln_g, m_sg_ln_b, m_sg_w, m_sg_b, m_mla_cq_norm, m_mla_w_uq, m_mla_ckv_norm, m_mla_w_ukv, m_mla_q_norm, m_mla_k_norm, m_mem_norm, m_mem_w_kv, m_mem_q_norm, m_mem_k_norm, m_w_branch_a, m_w_branch_b, m_w_branch_c, m_w_out, m_ffn2_norm, m_ffn2_w_gu, m_ffn2_w_down, v_ffn1_norm, v_ffn1_w_gu, v_ffn1_w_down, v_mix_norm, v_w_in, v_b_gate, v_sg_ln_g, v_sg_ln_b, v_sg_w, v_sg_b, v_mla_cq_norm, v_mla_w_uq, v_mla_ckv_norm, v_mla_w_ukv, v_mla_q_norm, v_mla_k_norm, v_mem_norm, v_mem_w_kv, v_mem_q_norm, v_mem_k_norm, v_w_branch_a, v_w_branch_b, v_w_branch_c, v_w_out, v_ffn2_norm, v_ffn2_w_gu, v_ffn2_w_down):
    args = dict(locals())
    weights = {n: args[n] for n in WEIGHT_ORDER}
    mom_m = {n: args["m_" + n] for n in WEIGHT_ORDER}
    mom_v = {n: args["v_" + n] for n in WEIGHT_ORDER}
    small = {n: weights[n] for n, _ in SMALL}
    halves = lambda a, r, c: a.reshape(2, r // 2, c)

    gathered = _gather_weights([halves(weights[n][0].astype(BF16), r, c) for n, r, c, _ in SHARDED])
    big = {}
    for (name, r, c, kind), g in zip(SHARDED, gathered):
        blk = g.reshape(N_CHIPS, r, c)
        big[name + "#blocks"] = blk
        big[name] = blk.transpose(1, 0, 2).reshape(r, N_CHIPS * c) if kind == "col" else blk.reshape(N_CHIPS * r, c)

    loss_row, dx, gw, gs = _local_step(x[0], mem[0], positions[0], loss_target[0], small, big)
    loss = lax.psum(loss_row[0, 0], ("x", "y", "c"))

    by_owner = []
    for name, r, c, kind in SHARDED:
        blk = gw[name].reshape(r, N_CHIPS, c).transpose(1, 0, 2) if kind == "col" else gw[name].reshape(N_CHIPS, r, c)
        by_owner.append(blk.astype(BF16).reshape(N_CHIPS, 2, r // 2, c))
    received = _swap_halves(by_owner)
    core = lax.axis_index("c").astype(jnp.int32).reshape(1)
    sums = [_pair_sum(g, s, core, "pair_sum_" + n) for g, s, (n, *_) in zip(by_owner, received, SHARDED)]
    slots = _exchange_pair_sums(sums)
    small_grads = _unpack_small(_allreduce_small(_pack_small(gs)))

    grads, deltas, new_m, new_v = {}, {}, {}, {}
    for (name, r, c, _), sl in zip(SHARDED, slots):
        outs = _adamw_slots(halves(weights[name][0], r, c), sl, halves(mom_m[name][0], r, c),
                            halves(mom_v[name][0], r, c), "adamw_" + name)
        shape = weights[name].shape
        grads[name], deltas[name], new_m[name], new_v[name] = [o.reshape(shape) for o in outs]
    dlt, nm, nv = _adamw(_pack_small(small), _pack_small(small_grads), _pack_small({n: mom_m[n] for n, _ in SMALL}),
                         _pack_small({n: mom_v[n] for n, _ in SMALL}), "adamw_small")
    for name, _ in SMALL:
        grads[name] = small_grads[name]
    deltas.update(_unpack_small(dlt))
    new_m.update(_unpack_small(nm))
    new_v.update(_unpack_small(nv))

    return (loss, dx[None], *[grads[n] for n in WEIGHT_ORDER], *[deltas[n] for n in WEIGHT_ORDER],
            *[new_m[n] for n in WEIGHT_ORDER], *[new_v[n] for n in WEIGHT_ORDER])
```

```python
import functools
from typing import Callable, NamedTuple

import numpy as np
import jax
import jax.numpy as jnp
from jax import lax
from jax.experimental import pallas as pl
from jax.experimental.pallas import tpu as pltpu

F32 = jnp.float32
BF16 = jnp.bfloat16

D_MODEL = 1024
D_FF = 2816
FF_TILE = 1408
SG_WIDTH = 512
SG_GROUPS = 8
CHUNK = 128
MLA_HEADS = 8
MLA_QK = 96
MLA_NOPE = 64
MLA_ROPE = 32
MLA_Q_RANK = 384
MLA_KV_RANK = 256
MEM_HEADS = 4
MEM_LEN = 256
LANES = 128
EPS = 1e-6
NEG = -1e30
ROPE_BASE = 10000.0
N_CHIPS = 4
N_DEV = 8

ADAM_LR = 0.001
ADAM_B1 = 0.9
ADAM_B2 = 0.999
ADAM_EPS = 1e-08
ADAM_WD = 0.01
ADAM_STEP = 10

COL_V = 512
COL_CQ = 1024
COL_CKV = 1408
COL_KR = 1664
COL_QM = 1696
COL_GATE = 2208
IN_COLS = 5280

VMEM_LIMIT_BYTES = 56 * 1024 * 1024
INV_SQRT2 = 0.7071067811865476
INV_SQRT_2PI = 0.3989422804014327

SHARDED = (
    ("ffn1_w_gu", 1024, 1408, "col"),
    ("ffn1_w_down", 704, 1024, "row"),
    ("w_in", 1024, 1320, "col"),
    ("mla_w_uq", 384, 192, "col"),
    ("mla_w_ukv", 256, 256, "col"),
    ("mem_w_kv", 256, 1024, "row"),
    ("w_branch_a", 512, 256, "col"),
    ("w_branch_b", 512, 256, "col"),
    ("w_branch_c", 512, 256, "col"),
    ("w_out", 256, 1024, "row"),
    ("ffn2_w_gu", 1024, 1408, "col"),
    ("ffn2_w_down", 704, 1024, "row"),
)
SMALL = (
    ("ffn1_norm", (1, 1024)), ("mix_norm", (1, 1024)), ("b_gate", (1, 3072)),
    ("sg_ln_g", (1, 512)), ("sg_ln_b", (1, 512)), ("sg_w", (1, 8, 128, 128)),
    ("sg_b", (1, 8, 128)), ("mla_cq_norm", (1, 384)), ("mla_ckv_norm", (1, 256)),
    ("mla_q_norm", (1, 96)), ("mla_k_norm", (1, 96)), ("mem_norm", (1, 1024)),
    ("mem_q_norm", (1, 128)), ("mem_k_norm", (1, 128)), ("ffn2_norm", (1, 1024)),
)
WEIGHT_ORDER = (
    "ffn1_norm", "ffn1_w_gu", "ffn1_w_down", "mix_norm", "w_in", "b_gate", "sg_ln_g", "sg_ln_b",
    "sg_w", "sg_b", "mla_cq_norm", "mla_w_uq", "mla_ckv_norm", "mla_w_ukv", "mla_q_norm",
    "mla_k_norm", "mem_norm", "mem_w_kv", "mem_q_norm", "mem_k_norm", "w_branch_a", "w_branch_b",
    "w_branch_c", "w_out", "ffn2_norm", "ffn2_w_gu", "ffn2_w_down",
)

_N_SMALL = sum(int(np.prod(s)) for _, s in SMALL)
SMALL_ROWS = -(-_N_SMALL // (LANES * 8)) * 8

MESH = pl.DeviceIdType.MESH


def _cparams():
    return pltpu.CompilerParams(vmem_limit_bytes=VMEM_LIMIT_BYTES)


def _dot(a, b):
    return jnp.dot(a, b, preferred_element_type=F32)


def _dot_nt(a, b):
    return lax.dot_general(a, b, (((1,), (1,)), ((), ())), preferred_element_type=F32)


def _dot_tn(a, b):
    return lax.dot_general(a, b, (((0,), (0,)), ((), ())), preferred_element_type=F32)


def _gelu(x):
    return 0.5 * x * (1.0 + lax.erf(x * INV_SQRT2))


def _gelu_grad(x):
    return 0.5 * (1.0 + lax.erf(x * INV_SQRT2)) + x * jnp.exp(-0.5 * x * x) * INV_SQRT_2PI


def _rstd(x, n):
    return lax.rsqrt(jnp.sum(x * x, axis=-1, keepdims=True) * (1.0 / n) + EPS)


def _rms_vjp(x, r, g, dy, n):
    dxh = dy * g
    dx = r * dxh - x * (r * r * r) * (jnp.sum(dxh * x, axis=-1, keepdims=True) * (1.0 / n))
    return dx, dy * x * r


def _row_tile(t, want):
    return min(t, want)


def _wide_tile(n):
    if n <= 1024:
        return n
    if n % 1024 == 0:
        return 1024
    assert n % FF_TILE == 0, n
    return FF_TILE


def _rms_fwd(x, g, name):
    t, d = x.shape
    tm = _row_tile(t, 512)

    def body(x_ref, g_ref, o_ref):
        xv = x_ref[...]
        o_ref[...] = (xv * _rstd(xv, d) * g_ref[...]).astype(BF16)

    return pl.pallas_call(
        body, name=name, grid=(t // tm,),
        in_specs=[pl.BlockSpec((tm, d), lambda i: (i, 0)), pl.BlockSpec((1, d), lambda i: (0, 0))],
        out_specs=pl.BlockSpec((tm, d), lambda i: (i, 0)),
        out_shape=jax.ShapeDtypeStruct((t, d), BF16), compiler_params=_cparams())(x, g)


def _rms_bwd(x, g, dxn, dres, name):
    t, d = x.shape
    tm = _row_tile(t, 256)

    def body(x_ref, g_ref, d_ref, r_ref, dx_ref, dg_ref):
        @pl.when(pl.program_id(0) == 0)
        def _():
            dg_ref[...] = jnp.zeros_like(dg_ref)

        xv = x_ref[...]
        r = _rstd(xv, d)
        dx, dgr = _rms_vjp(xv, r, g_ref[...], d_ref[...].astype(F32), d)
        dx_ref[...] = r_ref[...] + dx
        dg_ref[...] += jnp.sum(dgr, axis=0, keepdims=True)

    row = pl.BlockSpec((tm, d), lambda i: (i, 0))
    vec = pl.BlockSpec((1, d), lambda i: (0, 0))
    return pl.pallas_call(
        body, name=name, grid=(t // tm,), in_specs=[row, vec, row, row], out_specs=[row, vec],
        out_shape=[jax.ShapeDtypeStruct((t, d), F32), jax.ShapeDtypeStruct((1, d), F32)],
        compiler_params=_cparams())(x, g, dxn, dres)


def _mm(pairs, out_dtype, name):
    t = pairs[0][0].shape[0]
    n = pairs[0][1].shape[1]
    tm = _row_tile(t, 512)
    tn = _wide_tile(n)
    np_ = len(pairs)

    def body(*refs):
        o_ref = refs[2 * np_]
        acc = None
        for a_ref, w_ref in zip(refs[:np_], refs[np_:2 * np_]):
            part = _dot(a_ref[...].astype(BF16), w_ref[...])
            acc = part if acc is None else acc + part
        o_ref[...] = acc.astype(out_dtype)

    in_specs = [pl.BlockSpec((tm, a.shape[1]), lambda i, j: (i, 0)) for a, _ in pairs]
    in_specs += [pl.BlockSpec((w.shape[0], tn), lambda i, j: (0, j)) for _, w in pairs]
    return pl.pallas_call(
        body, name=name, grid=(t // tm, n // tn), in_specs=in_specs,
        out_specs=pl.BlockSpec((tm, tn), lambda i, j: (i, j)),
        out_shape=jax.ShapeDtypeStruct((t, n), out_dtype), compiler_params=_cparams(),
    )(*[a for a, _ in pairs], *[w for _, w in pairs])


def _mm_tn(a, b, name, scale=1.0):
    t, m = a.shape
    n = b.shape[1]
    tm, tn = _wide_tile(m), _wide_tile(n)
    tk = _row_tile(t, 512)
    nk = t // tk

    def body(a_ref, b_ref, o_ref):
        k = pl.program_id(2)

        @pl.when(k == 0)
        def _():
            o_ref[...] = jnp.zeros_like(o_ref)

        o_ref[...] += _dot_tn(a_ref[...].astype(BF16), b_ref[...].astype(BF16))
        if scale != 1.0:
            @pl.when(k == nk - 1)
            def _():
                o_ref[...] = o_ref[...] * scale

    return pl.pallas_call(
        body, name=name, grid=(m // tm, n // tn, nk),
        in_specs=[pl.BlockSpec((tk, tm), lambda i, j, k: (k, i)),
                  pl.BlockSpec((tk, tn), lambda i, j, k: (k, j))],
        out_specs=pl.BlockSpec((tm, tn), lambda i, j, k: (i, j)),
        out_shape=jax.ShapeDtypeStruct((m, n), F32), compiler_params=_cparams())(a, b)


class _Exchange(NamedTuple):
    operands: list
    out_shapes: list
    sem_shapes: list
    build: Callable


def _call_with_exchange(ex, body, name, grid, in_specs, out_specs, out_shape, scratch_shapes, operands):
    if ex is None:
        return pl.pallas_call(body, name=name, grid=grid, in_specs=in_specs, out_specs=out_specs, out_shape=out_shape,
                              scratch_shapes=scratch_shapes, compiler_params=_cparams())(*operands)
    n_in, n_out, n_scr = len(in_specs), len(out_specs), len(scratch_shapes)
    k_in, k_out = len(ex.operands), len(ex.out_shapes)

    def carried(*refs):
        a, b = n_in, n_in + k_in
        c, e = b + n_out, b + n_out + k_out
        f = e + n_scr
        start, finish = ex.build(refs[a:b], refs[c:e], refs[f:])
        steps = [pl.program_id(ax) for ax in range(len(grid))]
        first = functools.reduce(jnp.logical_and, [s == 0 for s in steps])
        last = functools.reduce(jnp.logical_and, [s == n - 1 for s, n in zip(steps, grid)])
        pl.when(first)(start)
        body(*refs[:a], *refs[b:c], *refs[e:f])
        pl.when(last)(finish)

    return pl.pallas_call(
        carried, name=name, grid=grid, in_specs=list(in_specs) + [ANY] * k_in,
        out_specs=list(out_specs) + [ANY] * k_out, out_shape=list(out_shape) + list(ex.out_shapes),
        scratch_shapes=list(scratch_shapes) + list(ex.sem_shapes), compiler_params=_cparams(),
    )(*operands, *ex.operands)


def _run_exchange(ex, name):
    k_in, k_out = len(ex.operands), len(ex.out_shapes)

    def body(*refs):
        start, finish = ex.build(refs[:k_in], refs[k_in:k_in + k_out], refs[k_in + k_out:])
        start()
        finish()

    return pl.pallas_call(body, name=name, in_specs=[ANY] * k_in, out_specs=[ANY] * k_out,
                          out_shape=list(ex.out_shapes), scratch_shapes=list(ex.sem_shapes))(*ex.operands)


def _ffn_fwd(x, g, wgu4, wd2, name, ex=None):
    t, d = x.shape
    tm = _row_tile(t, 512)

    def body(x_ref, g_ref, wg_ref, wu_ref, wd_ref, o_ref, xn_scr, acc_scr):
        j = pl.program_id(1)

        @pl.when(j == 0)
        def _():
            xv = x_ref[...]
            xn_scr[...] = (xv * _rstd(xv, d) * g_ref[...]).astype(BF16)
            acc_scr[...] = jnp.zeros_like(acc_scr)

        xn = xn_scr[...]
        gg = _dot(xn, wg_ref[0])
        uu = _dot(xn, wu_ref[0])
        act = gg * jax.nn.sigmoid(gg) * uu
        acc_scr[...] += _dot(act.astype(BF16), wd_ref[0])

        @pl.when(j == 1)
        def _():
            o_ref[...] = x_ref[...] + 0.5 * acc_scr[...]

    row = pl.BlockSpec((tm, d), lambda i, j: (i, 0))
    return _call_with_exchange(
        ex, body, name, (t // tm, 2),
        [row, pl.BlockSpec((1, d), lambda i, j: (0, 0)),
         pl.BlockSpec((1, d, FF_TILE), lambda i, j: (j, 0, 0)),
         pl.BlockSpec((1, d, FF_TILE), lambda i, j: (j + 2, 0, 0)),
         pl.BlockSpec((1, FF_TILE, d), lambda i, j: (j, 0, 0))],
        [row], [jax.ShapeDtypeStruct((t, d), F32)],
        [pltpu.VMEM((tm, d), BF16), pltpu.VMEM((tm, d), F32)], (x, g, wgu4, wgu4, wd2))


def _ffn_bwd(x, g, dy, wgu4, wd2, name, ex=None):
    t, d = x.shape
    tm = _row_tile(t, 256)

    def body(x_ref, g_ref, dy_ref, wg_ref, wu_ref, wd_ref,
             dx_ref, dgain_ref, xn_ref, dg_ref, du_ref, act_ref, acc_scr):
        i, j = pl.program_id(0), pl.program_id(1)

        @pl.when((i == 0) & (j == 0))
        def _():
            dgain_ref[...] = jnp.zeros_like(dgain_ref)

        xv = x_ref[...]
        r = _rstd(xv, d)
        xn = (xv * r * g_ref[...]).astype(BF16)

        @pl.when(j == 0)
        def _():
            xn_ref[...] = xn
            acc_scr[...] = jnp.zeros_like(acc_scr)

        gg = _dot(xn, wg_ref[0])
        uu = _dot(xn, wu_ref[0])
        sg = jax.nn.sigmoid(gg)
        silu = gg * sg
        act_ref[...] = (silu * uu).astype(BF16)
        dyh = (0.5 * dy_ref[...]).astype(BF16)
        dact = _dot_nt(dyh, wd_ref[0])
        du = (dact * silu).astype(BF16)
        dgt = (dact * uu * (sg * (1.0 + gg * (1.0 - sg)))).astype(BF16)
        du_ref[...] = du
        dg_ref[...] = dgt
        acc_scr[...] += _dot_nt(dgt, wg_ref[0]) + _dot_nt(du, wu_ref[0])

        @pl.when(j == 1)
        def _():
            dx, dgr = _rms_vjp(xv, r, g_ref[...], acc_scr[...], d)
            dx_ref[...] = dy_ref[...] + dx
            dgain_ref[...] += jnp.sum(dgr, axis=0, keepdims=True)

    row = pl.BlockSpec((tm, d), lambda i, j: (i, 0))
    vec = pl.BlockSpec((1, d), lambda i, j: (0, 0))
    ffb = pl.BlockSpec((tm, FF_TILE), lambda i, j: (i, j))
    return _call_with_exchange(
        ex, body, name, (t // tm, 2),
        [row, vec, row,
         pl.BlockSpec((1, d, FF_TILE), lambda i, j: (j, 0, 0)),
         pl.BlockSpec((1, d, FF_TILE), lambda i, j: (j + 2, 0, 0)),
         pl.BlockSpec((1, FF_TILE, d), lambda i, j: (j, 0, 0))],
        [row, vec, row, ffb, ffb, ffb],
        [jax.ShapeDtypeStruct((t, d), F32), jax.ShapeDtypeStruct((1, d), F32),
         jax.ShapeDtypeStruct((t, d), BF16), jax.ShapeDtypeStruct((t, D_FF), BF16),
         jax.ShapeDtypeStruct((t, D_FF), BF16), jax.ShapeDtypeStruct((t, D_FF), BF16)],
        [pltpu.VMEM((tm, d), F32)], (x, g, dy, wgu4, wgu4, wd2))


def _loss_head(y, tgt, name):
    t, d = y.shape
    tm = _row_tile(t, 512)

    def body(y_ref, t_ref, dy_ref, loss_ref):
        @pl.when(pl.program_id(0) == 0)
        def _():
            loss_ref[...] = jnp.zeros_like(loss_ref)

        e = y_ref[...] - t_ref[...]
        dy_ref[...] = e * (1.0 / d)
        part = 0.5 * jnp.sum(jnp.sum(e * e, axis=-1, keepdims=True) * (1.0 / d), axis=0, keepdims=True)
        loss_ref[...] += jnp.broadcast_to(part, loss_ref.shape)

    row = pl.BlockSpec((tm, d), lambda i: (i, 0))
    return pl.pallas_call(
        body, name=name, grid=(t // tm,), in_specs=[row, row],
        out_specs=[row, pl.BlockSpec((1, LANES), lambda i: (0, 0))],
        out_shape=[jax.ShapeDtypeStruct((t, d), F32), jax.ShapeDtypeStruct((1, LANES), F32)],
        compiler_params=_cparams())(y, tgt)


def _sgu_layernorm(vpre, lg, lb):
    v = _gelu(vpre)
    mu = jnp.mean(v, axis=-1, keepdims=True)
    xc = v - mu
    rstd = lax.rsqrt(jnp.mean(xc * xc, axis=-1, keepdims=True) + EPS)
    xhat = xc * rstd
    return xhat, rstd, xhat * lg + lb


def _sgu_fwd(zuv, lg, lb, wt, bias_l, name):
    t = zuv.shape[0]
    tm = _row_tile(t, 512)

    def body(u_ref, v_ref, lg_ref, lb_ref, wt_ref, bl_ref, o_ref, vln_scr):
        _, _, vln = _sgu_layernorm(v_ref[...], lg_ref[...], lb_ref[...])
        vln_scr[...] = vln.astype(BF16)
        lo = lax.broadcasted_iota(jnp.int32, (CHUNK, LANES), 1) < 64
        for c in range(tm // CHUNK):
            rows = slice(c * CHUNK, (c + 1) * CHUNK)
            for p in range(SG_GROUPS // 2):
                cols = slice(p * LANES, (p + 1) * LANES)
                vp = vln_scr[rows, cols]
                mixed = jnp.where(lo, _dot(wt_ref[2 * p], vp), _dot(wt_ref[2 * p + 1], vp)) + bl_ref[:, cols]
                o_ref[rows, cols] = (_gelu(u_ref[rows, cols]) * mixed).astype(BF16)

    half = lambda k: pl.BlockSpec((tm, SG_WIDTH), lambda i: (i, k))
    vec = pl.BlockSpec((1, SG_WIDTH), lambda i: (0, 0))
    return pl.pallas_call(
        body, name=name, grid=(t // tm,),
        in_specs=[half(0), half(1), vec, vec,
                  pl.BlockSpec((SG_GROUPS, CHUNK, CHUNK), lambda i: (0, 0, 0)),
                  pl.BlockSpec((CHUNK, SG_WIDTH), lambda i: (0, 0))],
        out_specs=pl.BlockSpec((tm, SG_WIDTH), lambda i: (i, 0)),
        out_shape=jax.ShapeDtypeStruct((t, SG_WIDTH), BF16),
        scratch_shapes=[pltpu.VMEM((tm, SG_WIDTH), BF16)],
        compiler_params=_cparams())(zuv, zuv, lg, lb, wt, bias_l)


def _sgu_bwd(zuv, dya, lg, lb, wt, wt_t, bias_l, name):
    t = zuv.shape[0]
    tm = _row_tile(t, 256)
    nsteps = t // tm

    def body(u_ref, v_ref, dy_ref, lg_ref, lb_ref, wt_ref, wtt_ref, bl_ref,
             dz_ref, dwt_ref, dbl_ref, dlg_ref, dlb_ref, vln_scr, dvln_scr, dbacc_scr):
        step = pl.program_id(0)

        @pl.when(step == 0)
        def _():
            dwt_ref[...] = jnp.zeros_like(dwt_ref)
            dlg_ref[...] = jnp.zeros_like(dlg_ref)
            dlb_ref[...] = jnp.zeros_like(dlb_ref)
            dbl_ref[...] = jnp.zeros_like(dbl_ref)
            dbacc_scr[...] = jnp.zeros_like(dbacc_scr)

        vpre = v_ref[...]
        lgv = lg_ref[...]
        xhat, rstd, vln = _sgu_layernorm(vpre, lgv, lb_ref[...])
        vln_scr[...] = vln.astype(BF16)
        lo = lax.broadcasted_iota(jnp.int32, (CHUNK, LANES), 1) < 64
        for c in range(tm // CHUNK):
            rows = slice(c * CHUNK, (c + 1) * CHUNK)
            for p in range(SG_GROUPS // 2):
                cols = slice(p * LANES, (p + 1) * LANES)
                vp = vln_scr[rows, cols]
                mixed = jnp.where(lo, _dot(wt_ref[2 * p], vp), _dot(wt_ref[2 * p + 1], vp)) + bl_ref[:, cols]
                upre = u_ref[rows, cols]
                dyp = dy_ref[rows, cols]
                dz_ref[rows, cols] = (dyp * mixed * _gelu_grad(upre)).astype(BF16)
                dm = dyp * _gelu(upre)
                dbacc_scr[:, cols] += dm
                dlo = jnp.where(lo, dm, 0.0).astype(BF16)
                dhi = jnp.where(lo, 0.0, dm).astype(BF16)
                dvln_scr[rows, cols] = _dot(wtt_ref[2 * p], dlo) + _dot(wtt_ref[2 * p + 1], dhi)
                dwt_ref[2 * p] += _dot_nt(dlo, vp)
                dwt_ref[2 * p + 1] += _dot_nt(dhi, vp)
        dvln = dvln_scr[...]
        dlg_ref[...] += jnp.sum(dvln * xhat, axis=0, keepdims=True)
        dlb_ref[...] += jnp.sum(dvln, axis=0, keepdims=True)
        dxh = dvln * lgv
        dv = rstd * (dxh - jnp.mean(dxh, axis=-1, keepdims=True)
                     - xhat * jnp.mean(dxh * xhat, axis=-1, keepdims=True))
        dz_ref[:, SG_WIDTH:] = (dv * _gelu_grad(vpre)).astype(BF16)

        @pl.when(step == nsteps - 1)
        def _():
            rr = lax.broadcasted_iota(jnp.int32, (CHUNK, CHUNK), 0)
            cc = lax.broadcasted_iota(jnp.int32, (CHUNK, CHUNK), 1)
            tril = (cc <= rr).astype(F32)
            for gidx in range(SG_GROUPS):
                dwt_ref[gidx] = dwt_ref[gidx] * tril
            kk = lax.broadcasted_iota(jnp.int32, (SG_WIDTH, LANES), 0)
            gg = lax.broadcasted_iota(jnp.int32, (SG_WIDTH, LANES), 1)
            sel = ((kk // 64) == gg).astype(F32)
            dbl_ref[...] = jnp.dot(dbacc_scr[...], sel, preferred_element_type=F32,
                                   precision=lax.Precision.HIGHEST)

    half = lambda k: pl.BlockSpec((tm, SG_WIDTH), lambda i: (i, k))
    vec = pl.BlockSpec((1, SG_WIDTH), lambda i: (0, 0))
    wspec = pl.BlockSpec((SG_GROUPS, CHUNK, CHUNK), lambda i: (0, 0, 0))
    return pl.pallas_call(
        body, name=name, grid=(nsteps,),
        in_specs=[half(0), half(1), pl.BlockSpec((tm, SG_WIDTH), lambda i: (i, 0)), vec, vec,
                  wspec, wspec, pl.BlockSpec((CHUNK, SG_WIDTH), lambda i: (0, 0))],
        out_specs=[pl.BlockSpec((tm, 2 * SG_WIDTH), lambda i: (i, 0)), wspec,
                   pl.BlockSpec((CHUNK, LANES), lambda i: (0, 0)), vec, vec],
        out_shape=[jax.ShapeDtypeStruct((t, 2 * SG_WIDTH), BF16),
                   jax.ShapeDtypeStruct((SG_GROUPS, CHUNK, CHUNK), F32),
                   jax.ShapeDtypeStruct((CHUNK, LANES), F32),
                   jax.ShapeDtypeStruct((1, SG_WIDTH), F32), jax.ShapeDtypeStruct((1, SG_WIDTH), F32)],
        scratch_shapes=[pltpu.VMEM((tm, SG_WIDTH), BF16), pltpu.VMEM((tm, SG_WIDTH), F32),
                        pltpu.VMEM((CHUNK, SG_WIDTH), F32)],
        compiler_params=_cparams())(zuv, zuv, dya, lg, lb, wt, wt_t, bias_l)


def _rope(x, c, s1, s2):
    return x * c + pltpu.roll(x, LANES - 16, 1) * s1 + pltpu.roll(x, 16, 1) * s2


def _rope_t(dy, c, s1, s2):
    return dy * c + pltpu.roll(dy * s1, 16, 1) + pltpu.roll(dy * s2, LANES - 16, 1)


def _mla_prep_fwd(zcq, zckv, zkr, gcq, gckv, qg, kg, wuq, wuk, wuv, rc, rs1, rs2, name):
    t = zcq.shape[0]
    tm = _row_tile(t, 256)
    hd = MLA_HEADS * LANES

    def body(zcq_ref, zckv_ref, zkr_ref, gcq_ref, gckv_ref, qg_ref, kg_ref, wuq_ref, wuk_ref, wuv_ref,
             c_ref, s1_ref, s2_ref, q_ref, k_ref, v_ref, cqn_ref, ckvn_ref):
        c, s1, s2 = c_ref[...], s1_ref[...], s2_ref[...]
        xq = zcq_ref[...]
        cqn = (xq * _rstd(xq, MLA_Q_RANK) * gcq_ref[...]).astype(BF16)
        cqn_ref[...] = cqn
        ql = _dot(cqn, wuq_ref[...])
        xk = zckv_ref[...]
        ckvn = (xk * _rstd(xk, MLA_KV_RANK) * gckv_ref[...]).astype(BF16)
        ckvn_ref[...] = ckvn
        kl = _dot(ckvn, wuk_ref[...])
        v_ref[...] = _dot(ckvn, wuv_ref[...]).astype(BF16)
        kr = zkr_ref[...]
        for h in range(MLA_HEADS):
            sl = slice(h * LANES, (h + 1) * LANES)
            qh = ql[:, sl]
            q_ref[:, sl] = _rope(qh * _rstd(qh, MLA_QK) * qg_ref[...], c, s1, s2).astype(BF16)
            kh = kl[:, sl] + kr
            k_ref[:, sl] = _rope(kh * _rstd(kh, MLA_QK) * kg_ref[...], c, s1, s2).astype(BF16)

    row = lambda n: pl.BlockSpec((tm, n), lambda i: (i, 0))
    full = lambda a: pl.BlockSpec(a.shape, lambda i: (0, 0))
    return pl.pallas_call(
        body, name=name, grid=(t // tm,),
        in_specs=[row(MLA_Q_RANK), row(MLA_KV_RANK), row(LANES), full(gcq), full(gckv), full(qg), full(kg),
                  full(wuq), full(wuk), full(wuv), row(LANES), row(LANES), row(LANES)],
        out_specs=[row(hd), row(hd), row(hd), row(MLA_Q_RANK), row(MLA_KV_RANK)],
        out_shape=[jax.ShapeDtypeStruct((t, hd), BF16)] * 3
        + [jax.ShapeDtypeStruct((t, MLA_Q_RANK), BF16), jax.ShapeDtypeStruct((t, MLA_KV_RANK), BF16)],
        compiler_params=_cparams(),
    )(zcq, zckv, zkr, gcq, gckv, qg, kg, wuq, wuk, wuv, rc, rs1, rs2)


def _mla_prep_bwd(zcq, zckv, zkr, gcq, gckv, qg, kg, wuq, wuk, wuv, rc, rs1, rs2, dq, dk, dv, name):
    t = zcq.shape[0]
    tm = _row_tile(t, 256)
    hd = MLA_HEADS * LANES

    def body(zcq_ref, zckv_ref, zkr_ref, gcq_ref, gckv_ref, qg_ref, kg_ref, wuq_ref, wuk_ref, wuv_ref,
             c_ref, s1_ref, s2_ref, dq_ref, dk_ref, dv_ref,
             dzcq_ref, dzckv_ref, dzkr_ref, dql_ref, dkl_ref, dgcq_ref, dgckv_ref, dqg_ref, dkg_ref):
        @pl.when(pl.program_id(0) == 0)
        def _():
            for ref in (dgcq_ref, dgckv_ref, dqg_ref, dkg_ref):
                ref[...] = jnp.zeros_like(ref)

        c, s1, s2 = c_ref[...], s1_ref[...], s2_ref[...]
        qgv, kgv = qg_ref[...], kg_ref[...]
        xq = zcq_ref[...]
        rq = _rstd(xq, MLA_Q_RANK)
        ql = _dot((xq * rq * gcq_ref[...]).astype(BF16), wuq_ref[...])
        xk = zckv_ref[...]
        rk = _rstd(xk, MLA_KV_RANK)
        kl = _dot((xk * rk * gckv_ref[...]).astype(BF16), wuk_ref[...])
        kr = zkr_ref[...]
        dqg_acc = jnp.zeros((tm, LANES), F32)
        dkg_acc = jnp.zeros((tm, LANES), F32)
        dkr = jnp.zeros((tm, LANES), F32)
        for h in range(MLA_HEADS):
            sl = slice(h * LANES, (h + 1) * LANES)
            qh = ql[:, sl]
            dqh, dgr = _rms_vjp(qh, _rstd(qh, MLA_QK), qgv, _rope_t(dq_ref[:, sl], c, s1, s2), MLA_QK)
            dql_ref[:, sl] = dqh.astype(BF16)
            dqg_acc += dgr
            kh = kl[:, sl] + kr
            dkh, dgr = _rms_vjp(kh, _rstd(kh, MLA_QK), kgv, _rope_t(dk_ref[:, sl], c, s1, s2), MLA_QK)
            dkl_ref[:, sl] = dkh.astype(BF16)
            dkg_acc += dgr
            dkr += dkh
        dqg_ref[...] += jnp.sum(dqg_acc, axis=0, keepdims=True)
        dkg_ref[...] += jnp.sum(dkg_acc, axis=0, keepdims=True)
        lane = lax.broadcasted_iota(jnp.int32, (tm, LANES), 1)
        dzkr_ref[...] = jnp.where((lane >= MLA_NOPE) & (lane < MLA_QK), dkr, 0.0).astype(BF16)
        dcqn = _dot_nt(dql_ref[...], wuq_ref[...])
        dx, dgr = _rms_vjp(xq, rq, gcq_ref[...], dcqn, MLA_Q_RANK)
        dzcq_ref[...] = dx.astype(BF16)
        dgcq_ref[...] += jnp.sum(dgr, axis=0, keepdims=True)
        dckvn = _dot_nt(dkl_ref[...], wuk_ref[...]) + _dot_nt(dv_ref[...].astype(BF16), wuv_ref[...])
        dx, dgr = _rms_vjp(xk, rk, gckv_ref[...], dckvn, MLA_KV_RANK)
        dzckv_ref[...] = dx.astype(BF16)
        dgckv_ref[...] += jnp.sum(dgr, axis=0, keepdims=True)

    row = lambda n: pl.BlockSpec((tm, n), lambda i: (i, 0))
    full = lambda a: pl.BlockSpec(a.shape, lambda i: (0, 0))
    vec = lambda n: pl.BlockSpec((1, n), lambda i: (0, 0))
    return pl.pallas_call(
        body, name=name, grid=(t // tm,),
        in_specs=[row(MLA_Q_RANK), row(MLA_KV_RANK), row(LANES), full(gcq), full(gckv), full(qg), full(kg),
                  full(wuq), full(wuk), full(wuv), row(LANES), row(LANES), row(LANES), row(hd), row(hd), row(hd)],
        out_specs=[row(MLA_Q_RANK), row(MLA_KV_RANK), row(LANES), row(hd), row(hd),
                   vec(MLA_Q_RANK), vec(MLA_KV_RANK), vec(LANES), vec(LANES)],
        out_shape=[jax.ShapeDtypeStruct((t, MLA_Q_RANK), BF16), jax.ShapeDtypeStruct((t, MLA_KV_RANK), BF16),
                   jax.ShapeDtypeStruct((t, LANES), BF16), jax.ShapeDtypeStruct((t, hd), BF16),
                   jax.ShapeDtypeStruct((t, hd), BF16), jax.ShapeDtypeStruct((1, MLA_Q_RANK), F32),
                   jax.ShapeDtypeStruct((1, MLA_KV_RANK), F32), jax.ShapeDtypeStruct((1, LANES), F32),
                   jax.ShapeDtypeStruct((1, LANES), F32)],
        compiler_params=_cparams(),
    )(zcq, zckv, zkr, gcq, gckv, qg, kg, wuq, wuk, wuv, rc, rs1, rs2, dq, dk, dv)


def _attn_tiles(t):
    tq = 512 if t >= 2048 else 128
    return tq, min(t, 4 * tq), min(t, 2 * tq)


def _causal_keep(tq, tk, i, j):
    row = lax.broadcasted_iota(jnp.int32, (tq, tk), 0)
    col = lax.broadcasted_iota(jnp.int32, (tq, tk), 1)
    return (col - row) <= (i * tq - j * tk)


ATTN_FWD_HEADS_PER_STEP = 2
ATTN_BWD_HEADS_PER_STEP = 2


def _attn_fwd(q, k, v, name):
    t, hd = q.shape
    hp = ATTN_FWD_HEADS_PER_STEP
    tq, tk, _ = _attn_tiles(t)
    pairs = [(i, j) for i in range(t // tq) for j in range(((i + 1) * tq - 1) // tk + 1)]
    ii = np.array([p[0] for p in pairs], np.int32)
    jj = np.array([p[1] for p in pairs], np.int32)
    scale = MLA_QK ** -0.5

    def body(ii_ref, jj_ref, q_ref, k_ref, v_ref, o_ref, lse_ref, m_scr, l_scr, acc_scr):
        s_id = pl.program_id(1)
        i, j = ii_ref[s_id], jj_ref[s_id]
        last = j == ((i + 1) * tq - 1) // tk

        @pl.when(j == 0)
        def _():
            m_scr[...] = jnp.full_like(m_scr, NEG)
            l_scr[...] = jnp.zeros_like(l_scr)
            acc_scr[...] = jnp.zeros_like(acc_scr)

        def step(masked):
            for hh in range(hp):
                sl = slice(hh * LANES, (hh + 1) * LANES)
                s = _dot_nt(q_ref[:, sl], k_ref[:, sl]) * scale
                if masked:
                    s = jnp.where(_causal_keep(tq, tk, i, j), s, NEG)
                m_prev = m_scr[hh]
                m_new = jnp.maximum(m_prev, jnp.max(s, axis=1, keepdims=True))
                p = jnp.exp(s - m_new)
                alpha = jnp.exp(m_prev - m_new)
                l_new = alpha * l_scr[hh] + jnp.sum(p, axis=1, keepdims=True)
                acc = alpha * acc_scr[:, sl] + _dot(p.astype(BF16), v_ref[:, sl])
                if masked:
                    o_ref[:, sl] = (acc / l_new).astype(BF16)
                    lse_ref[:, sl] = jnp.broadcast_to(m_new + jnp.log(l_new), (tq, LANES))
                else:
                    l_scr[hh] = l_new
                    acc_scr[:, sl] = acc
                    m_scr[hh] = m_new

        @pl.when(jnp.logical_not(last))
        def _():
            step(False)

        @pl.when(last)
        def _():
            step(True)

    w = hp * LANES
    qspec = pl.BlockSpec((tq, w), lambda h, s, ii_r, jj_r: (ii_r[s], h))
    kspec = pl.BlockSpec((tk, w), lambda h, s, ii_r, jj_r: (jj_r[s], h))
    return pl.pallas_call(
        body, name=name,
        grid_spec=pltpu.PrefetchScalarGridSpec(
            num_scalar_prefetch=2, grid=(hd // w, len(pairs)), in_specs=[qspec, kspec, kspec],
            out_specs=[qspec, qspec],
            scratch_shapes=[pltpu.VMEM((hp, tq, 1), F32), pltpu.VMEM((hp, tq, 1), F32),
                            pltpu.VMEM((tq, w), F32)]),
        out_shape=[jax.ShapeDtypeStruct((t, hd), BF16), jax.ShapeDtypeStruct((t, hd), F32)],
        compiler_params=_cparams())(jnp.asarray(ii), jnp.asarray(jj), q, k, v)


def _attn_bwd(q, k, v, o, lse, do, name):
    t, hd = q.shape
    hp = ATTN_BWD_HEADS_PER_STEP
    tq, _, tk = _attn_tiles(t)
    nq = t // tq
    pairs = [(i, j) for j in range(t // tk) for i in range((j * tk) // tq, nq)]
    ii = np.array([p[0] for p in pairs], np.int32)
    jj = np.array([p[1] for p in pairs], np.int32)
    scale = MLA_QK ** -0.5

    def body(jj_ref, ii_ref, q_ref, k_ref, v_ref, o_ref, lse_ref, do_ref, dq_ref, dk_ref, dv_ref, dk_scr, dv_scr):
        s_id = pl.program_id(1)
        i, j = ii_ref[s_id], jj_ref[s_id]

        @pl.when(s_id == 0)
        def _():
            dq_ref[...] = jnp.zeros_like(dq_ref)

        @pl.when(i == (j * tk) // tq)
        def _():
            dk_scr[...] = jnp.zeros_like(dk_scr)
            dv_scr[...] = jnp.zeros_like(dv_scr)

        rows = pl.ds(pl.multiple_of(i * tq, tq), tq)

        def step(masked):
            for hh in range(hp):
                sl = slice(hh * LANES, (hh + 1) * LANES)
                qv, kv, dov = q_ref[:, sl], k_ref[:, sl], do_ref[:, sl]
                s = _dot_nt(qv, kv) * scale
                if masked:
                    s = jnp.where(_causal_keep(tq, tk, i, j), s, NEG)
                p = jnp.exp(s - jnp.max(lse_ref[:, sl], axis=1, keepdims=True))
                delta = jnp.sum(dov.astype(F32) * o_ref[:, sl].astype(F32), axis=1, keepdims=True)
                dv_scr[:, sl] += _dot_tn(p.astype(BF16), dov)
                dp = _dot_nt(dov, v_ref[:, sl])
                ds = (p * (dp - delta) * scale).astype(BF16)
                dk_scr[:, sl] += _dot_tn(ds, qv)
                dq_ref[rows, sl] += _dot(ds, kv)

        crosses = (j + 1) * tk - 1 > i * tq

        @pl.when(jnp.logical_not(crosses))
        def _():
            step(False)

        @pl.when(crosses)
        def _():
            step(True)

        @pl.when(i == nq - 1)
        def _():
            dk_ref[...] = dk_scr[...]
            dv_ref[...] = dv_scr[...]

    w = hp * LANES
    qspec = pl.BlockSpec((tq, w), lambda h, s, jj_r, ii_r: (ii_r[s], h))
    kspec = pl.BlockSpec((tk, w), lambda h, s, jj_r, ii_r: (jj_r[s], h))
    return pl.pallas_call(
        body, name=name,
        grid_spec=pltpu.PrefetchScalarGridSpec(
            num_scalar_prefetch=2, grid=(hd // w, len(pairs)),
            in_specs=[qspec, kspec, kspec, qspec, qspec, qspec],
            out_specs=[pl.BlockSpec((t, w), lambda h, s, jj_r, ii_r: (0, h)), kspec, kspec],
            scratch_shapes=[pltpu.VMEM((tk, w), F32), pltpu.VMEM((tk, w), F32)]),
        out_shape=[jax.ShapeDtypeStruct((t, hd), F32)] * 3,
        compiler_params=_cparams())(jnp.asarray(jj), jnp.asarray(ii), q, k, v, o, lse, do)


MEM_W = MEM_HEADS * LANES


def _mem_kv_fwd(mem, gmem, wkv, kg, name):
    m, d = mem.shape

    def body(mem_ref, g_ref, w_ref, kg_ref, k_ref, v_ref, mn_ref):
        xv = mem_ref[...]
        mn = (xv * _rstd(xv, d) * g_ref[...]).astype(BF16)
        mn_ref[...] = mn
        kvm = _dot(mn, w_ref[...])
        v_ref[...] = kvm[:, MEM_W:].astype(BF16)
        for h in range(MEM_HEADS):
            sl = slice(h * LANES, (h + 1) * LANES)
            kh = kvm[:, sl]
            k_ref[:, sl] = (kh * _rstd(kh, LANES) * kg_ref[...]).astype(BF16)

    full = lambda a: pl.BlockSpec(a.shape, lambda i: (0, 0))
    return pl.pallas_call(
        body, name=name, grid=(1,), in_specs=[full(mem), full(gmem), full(wkv), full(kg)],
        out_specs=[pl.BlockSpec((m, MEM_W), lambda i: (0, 0)), pl.BlockSpec((m, MEM_W), lambda i: (0, 0)),
                   pl.BlockSpec((m, d), lambda i: (0, 0))],
        out_shape=[jax.ShapeDtypeStruct((m, MEM_W), BF16), jax.ShapeDtypeStruct((m, MEM_W), BF16),
                   jax.ShapeDtypeStruct((m, d), BF16)],
        compiler_params=_cparams())(mem, gmem, wkv, kg)


def _mem_softmax(qn, kh):
    s = _dot_nt(qn, kh) * (LANES ** -0.5)
    e = jnp.exp(s - jnp.max(s, axis=1, keepdims=True))
    return e / jnp.sum(e, axis=1, keepdims=True)


def _mem_attn_fwd(zqm, qg, km, vm, name):
    t = zqm.shape[0]
    tm = _row_tile(t, 512)

    def body(q_ref, qg_ref, k_ref, v_ref, o_ref):
        for h in range(MEM_HEADS):
            sl = slice(h * LANES, (h + 1) * LANES)
            qh = q_ref[:, sl]
            qn = (qh * _rstd(qh, LANES) * qg_ref[...]).astype(BF16)
            p = _mem_softmax(qn, k_ref[:, sl])
            o_ref[:, sl] = _dot(p.astype(BF16), v_ref[:, sl]).astype(BF16)

    row = pl.BlockSpec((tm, MEM_W), lambda i: (i, 0))
    full = lambda a: pl.BlockSpec(a.shape, lambda i: (0, 0))
    return pl.pallas_call(
        body, name=name, grid=(t // tm,), in_specs=[row, full(qg), full(km), full(vm)], out_specs=row,
        out_shape=jax.ShapeDtypeStruct((t, MEM_W), BF16), compiler_params=_cparams())(zqm, qg, km, vm)


def _mem_attn_bwd(zqm, dyc, qg, km, vm, name):
    t = zqm.shape[0]
    m = km.shape[0]
    tm = _row_tile(t, 256)

    def body(q_ref, dy_ref, qg_ref, k_ref, v_ref, dz_ref, dk_ref, dv_ref, dqg_ref):
        @pl.when(pl.program_id(0) == 0)
        def _():
            dk_ref[...] = jnp.zeros_like(dk_ref)
            dv_ref[...] = jnp.zeros_like(dv_ref)
            dqg_ref[...] = jnp.zeros_like(dqg_ref)

        qgv = qg_ref[...]
        dqg_acc = jnp.zeros((tm, LANES), F32)
        for h in range(MEM_HEADS):
            sl = slice(h * LANES, (h + 1) * LANES)
            qh = q_ref[:, sl]
            r = _rstd(qh, LANES)
            qn = (qh * r * qgv).astype(BF16)
            kh = k_ref[:, sl]
            p = _mem_softmax(qn, kh)
            dov = dy_ref[:, sl]
            dv_ref[:, sl] += _dot_tn(p.astype(BF16), dov)
            dp = _dot_nt(dov, v_ref[:, sl])
            ds = (p * (dp - jnp.sum(dp * p, axis=1, keepdims=True)) * (LANES ** -0.5)).astype(BF16)
            dk_ref[:, sl] += _dot_tn(ds, qn)
            dqh, dgr = _rms_vjp(qh, r, qgv, _dot(ds, kh), LANES)
            dz_ref[:, sl] = dqh.astype(BF16)
            dqg_acc += dgr
        dqg_ref[...] += jnp.sum(dqg_acc, axis=0, keepdims=True)

    row = pl.BlockSpec((tm, MEM_W), lambda i: (i, 0))
    full = lambda a: pl.BlockSpec(a.shape, lambda i: (0, 0))
    acc = pl.BlockSpec((m, MEM_W), lambda i: (0, 0))
    return pl.pallas_call(
        body, name=name, grid=(t // tm,), in_specs=[row, row, full(qg), full(km), full(vm)],
        out_specs=[row, acc, acc, pl.BlockSpec((1, LANES), lambda i: (0, 0))],
        out_shape=[jax.ShapeDtypeStruct((t, MEM_W), BF16), jax.ShapeDtypeStruct((m, MEM_W), F32),
                   jax.ShapeDtypeStruct((m, MEM_W), F32), jax.ShapeDtypeStruct((1, LANES), F32)],
        compiler_params=_cparams())(zqm, dyc, qg, km, vm)


def _mem_kv_bwd(mem, gmem, wkv, kg, dkn, dvm, name):
    m, d = mem.shape

    def body(mem_ref, g_ref, w_ref, kg_ref, dk_ref, dv_ref, dw_ref, dkg_ref, dg_ref, dkv_scr):
        xv = mem_ref[...]
        r = _rstd(xv, d)
        mn = (xv * r * g_ref[...]).astype(BF16)
        kvm = _dot(mn, w_ref[...])
        dkv_scr[:, MEM_W:] = dv_ref[...].astype(BF16)
        dkg_acc = jnp.zeros((m, LANES), F32)
        for h in range(MEM_HEADS):
            sl = slice(h * LANES, (h + 1) * LANES)
            kh = kvm[:, sl]
            dkh, dgr = _rms_vjp(kh, _rstd(kh, LANES), kg_ref[...], dk_ref[:, sl], LANES)
            dkv_scr[:, sl] = dkh.astype(BF16)
            dkg_acc += dgr
        dkg_ref[...] = jnp.sum(dkg_acc, axis=0, keepdims=True)
        dkv = dkv_scr[...]
        dw_ref[...] = _dot_tn(mn, dkv)
        dmn = _dot_nt(dkv, w_ref[...])
        dg_ref[...] = jnp.sum(dmn * xv * r, axis=0, keepdims=True)

    full = lambda a: pl.BlockSpec(a.shape, lambda i: (0, 0))
    return pl.pallas_call(
        body, name=name, grid=(1,),
        in_specs=[full(mem), full(gmem), full(wkv), full(kg), full(dkn), full(dvm)],
        out_specs=[pl.BlockSpec((d, 2 * MEM_W), lambda i: (0, 0)), pl.BlockSpec((1, LANES), lambda i: (0, 0)),
                   pl.BlockSpec((1, d), lambda i: (0, 0))],
        out_shape=[jax.ShapeDtypeStruct((d, 2 * MEM_W), F32), jax.ShapeDtypeStruct((1, LANES), F32),
                   jax.ShapeDtypeStruct((1, d), F32)],
        scratch_shapes=[pltpu.VMEM((m, 2 * MEM_W), BF16)],
        compiler_params=_cparams())(mem, gmem, wkv, kg, dkn, dvm)


def _merge_fwd(x1, ya, yb, yc, zg, bg, wa, wb, wc, wo, name):
    t, d = x1.shape
    tm = _row_tile(t, 256)

    def body(x_ref, ya_ref, yb_ref, yc_ref, zg_ref, bg_ref, wa_ref, wb_ref, wc_ref, wo_ref,
             x2_ref, mg_ref, pa_ref, pb_ref, pc_ref):
        merged = None
        for k, (y_ref, w_ref, p_ref) in enumerate(
                ((ya_ref, wa_ref, pa_ref), (yb_ref, wb_ref, pb_ref), (yc_ref, wc_ref, pc_ref))):
            sl = slice(k * d, (k + 1) * d)
            pr = _dot(y_ref[...], w_ref[...])
            p_ref[...] = pr.astype(BF16)
            term = jax.nn.sigmoid(zg_ref[:, sl] + bg_ref[:, sl]) * pr
            merged = term if merged is None else merged + term
        mb = merged.astype(BF16)
        mg_ref[...] = mb
        x2_ref[...] = x_ref[...] + _dot(mb, wo_ref[...])

    row = lambda n: pl.BlockSpec((tm, n), lambda i: (i, 0))
    full = lambda a: pl.BlockSpec(a.shape, lambda i: (0, 0))
    return pl.pallas_call(
        body, name=name, grid=(t // tm,),
        in_specs=[row(d), row(ya.shape[1]), row(yb.shape[1]), row(yc.shape[1]), row(3 * d), full(bg),
                  full(wa), full(wb), full(wc), full(wo)],
        out_specs=[row(d)] * 5,
        out_shape=[jax.ShapeDtypeStruct((t, d), F32)] + [jax.ShapeDtypeStruct((t, d), BF16)] * 4,
        compiler_params=_cparams())(x1, ya, yb, yc, zg, bg, wa, wb, wc, wo)


def _merge_bwd(dx2, pa, pb, pc, zg, bg, wa, wb, wc, wo, name, ex=None):
    t, d = dx2.shape
    tm = _row_tile(t, 256)

    def body(dx_ref, pa_ref, pb_ref, pc_ref, zg_ref, bg_ref, wa_ref, wb_ref, wc_ref, wo_ref,
             dpa_ref, dpb_ref, dpc_ref, dzg_ref, dbg_ref, dya_ref, dyb_ref, dyc_ref):
        @pl.when(pl.program_id(0) == 0)
        def _():
            dbg_ref[...] = jnp.zeros_like(dbg_ref)

        dm = _dot_nt(dx_ref[...].astype(BF16), wo_ref[...])
        for k, (p_ref, w_ref, dp_ref, dy_ref) in enumerate(
                ((pa_ref, wa_ref, dpa_ref, dya_ref), (pb_ref, wb_ref, dpb_ref, dyb_ref),
                 (pc_ref, wc_ref, dpc_ref, dyc_ref))):
            sl = slice(k * d, (k + 1) * d)
            gate = jax.nn.sigmoid(zg_ref[:, sl] + bg_ref[:, sl])
            dpr = (dm * gate).astype(BF16)
            dp_ref[...] = dpr
            dzg = dm * p_ref[...].astype(F32) * gate * (1.0 - gate)
            dzg_ref[:, sl] = dzg.astype(BF16)
            dbg_ref[:, sl] += jnp.sum(dzg, axis=0, keepdims=True)
            dy_ref[...] = _dot_nt(dpr, w_ref[...]).astype(dy_ref.dtype)

    row = lambda n: pl.BlockSpec((tm, n), lambda i: (i, 0))
    full = lambda a: pl.BlockSpec(a.shape, lambda i: (0, 0))
    na, nb, nc = wa.shape[0], wb.shape[0], wc.shape[0]
    return _call_with_exchange(
        ex, body, name, (t // tm,),
        [row(d), row(d), row(d), row(d), row(3 * d), full(bg), full(wa), full(wb), full(wc), full(wo)],
        [row(d), row(d), row(d), row(3 * d), pl.BlockSpec((1, 3 * d), lambda i: (0, 0)), row(na), row(nb), row(nc)],
        [jax.ShapeDtypeStruct((t, d), BF16)] * 3
        + [jax.ShapeDtypeStruct((t, 3 * d), BF16), jax.ShapeDtypeStruct((1, 3 * d), F32),
           jax.ShapeDtypeStruct((t, na), F32), jax.ShapeDtypeStruct((t, nb), BF16),
           jax.ShapeDtypeStruct((t, nc), BF16)],
        [], (dx2, pa, pb, pc, zg, bg, wa, wb, wc, wo))


def _adamw_math(w, g, m, v):
    bc1 = 1.0 - ADAM_B1 ** ADAM_STEP
    bc2 = 1.0 - ADAM_B2 ** ADAM_STEP
    nm = ADAM_B1 * m + (1.0 - ADAM_B1) * g
    nv = ADAM_B2 * v + (1.0 - ADAM_B2) * (g * g)
    delta = -ADAM_LR * ((nm / bc1) / (jnp.sqrt(nv / bc2) + ADAM_EPS) + ADAM_WD * w)
    return delta, nm, nv


def _div_tile(n, cap, mult):
    best = None
    for cand in range(mult, min(n, cap) + 1, mult):
        if n % cand == 0:
            best = cand
    assert best is not None, (n, cap, mult)
    return best


def _adamw(w, g, m, v, name):
    rows, cols = w.shape
    tr = rows if rows * cols <= 256 * 1024 else _div_tile(rows, 256, 8)

    def body(w_ref, g_ref, m_ref, v_ref, d_ref, nm_ref, nv_ref):
        d_ref[...], nm_ref[...], nv_ref[...] = _adamw_math(w_ref[...], g_ref[...], m_ref[...], v_ref[...])

    blk = pl.BlockSpec((tr, cols), lambda i: (i, 0))
    return pl.pallas_call(
        body, name=name, grid=(rows // tr,), in_specs=[blk] * 4, out_specs=[blk] * 3,
        out_shape=[jax.ShapeDtypeStruct((rows, cols), F32)] * 3, compiler_params=_cparams())(w, g, m, v)


def _adamw_slots(w, slots, m, v, name):
    _, hr, cols = w.shape
    tr = _div_tile(hr, 128, 16)

    def body(w_ref, s_ref, m_ref, v_ref, g_ref, d_ref, nm_ref, nv_ref):
        g = s_ref[0, 0].astype(F32)
        for k in range(1, N_CHIPS):
            g = g + s_ref[0, k].astype(F32)
        g_ref[0] = g
        d_ref[0], nm_ref[0], nv_ref[0] = _adamw_math(w_ref[0], g, m_ref[0], v_ref[0])

    blk = pl.BlockSpec((1, tr, cols), lambda h, i: (h, i, 0))
    return pl.pallas_call(
        body, name=name, grid=(2, hr // tr),
        in_specs=[blk, pl.BlockSpec((1, N_CHIPS, tr, cols), lambda h, i: (h, 0, i, 0)), blk, blk],
        out_specs=[blk] * 4, out_shape=[jax.ShapeDtypeStruct((2, hr, cols), F32)] * 4,
        compiler_params=_cparams())(w, slots, m, v)


ANY = pl.BlockSpec(memory_space=pl.ANY)


def _place():
    x, y, c = lax.axis_index("x"), lax.axis_index("y"), lax.axis_index("c")
    other_chips = [(1 - x, y), (x, 1 - y), (1 - x, 1 - y)]
    return x, y, c, other_chips


def _remote(src, dst, send_sem, recv_sem, to):
    return pltpu.make_async_remote_copy(src_ref=src, dst_ref=dst, send_sem=send_sem, recv_sem=recv_sem,
                                        device_id=to, device_id_type=MESH)


def _gather_exchange(shards):
    nw = len(shards)

    def build(s_refs, g_refs, sems):
        send_sems, recv_sems, local_sems = sems
        x, y, c, chips = _place()
        me = 2 * x + y
        sibling = (x, y, 1 - c)
        mine = [pltpu.make_async_copy(s_refs[w], g_refs[w].at[me], local_sems.at[w]) for w in range(nw)]
        first = [_remote(s_refs[w].at[c], g_refs[w].at[me, c], send_sems.at[k, w], recv_sems.at[k, w], (cx, cy, c))
                 for k, (cx, cy) in enumerate(chips) for w in range(nw)]

        def start():
            for cp in mine + first:
                cp.start()

        def finish():
            passed = []
            for k, (cx, cy) in enumerate(chips):
                for w in range(nw):
                    slab = g_refs[w].at[2 * cx + cy, c]
                    _remote(slab, slab, send_sems.at[k, w], recv_sems.at[k, w], (cx, cy, c)).wait_recv()
                    fwd = _remote(slab, slab, send_sems.at[3 + k, w], recv_sems.at[3 + k, w], sibling)
                    fwd.start()
                    passed.append(fwd)
            for k, (cx, cy) in enumerate(chips):
                for w in range(nw):
                    slab = g_refs[w].at[2 * cx + cy, 1 - c]
                    _remote(slab, slab, send_sems.at[3 + k, w], recv_sems.at[3 + k, w], sibling).wait_recv()
            for cp in first + passed:
                cp.wait_send()
            for cp in mine:
                cp.wait()

        return start, finish

    return _Exchange(list(shards), [jax.ShapeDtypeStruct((N_CHIPS,) + s.shape, BF16) for s in shards],
                     [pltpu.SemaphoreType.DMA((6, nw)), pltpu.SemaphoreType.DMA((6, nw)),
                      pltpu.SemaphoreType.DMA((nw,))], build)


def _swap_halves(grads, name):
    nw = len(grads)

    def body(*refs):
        g_refs, sib_refs = refs[:nw], refs[nw:2 * nw]
        send_sems, recv_sems = refs[2 * nw:]
        x, y, c, _ = _place()
        copies = [_remote(g_refs[w].at[s, 1 - c], sib_refs[w].at[s], send_sems.at[s, w], recv_sems.at[s, w],
                          (x, y, 1 - c)) for w in range(nw) for s in range(N_CHIPS)]
        for cp in copies:
            cp.start()
        for cp in copies:
            cp.wait_recv()
        for cp in copies:
            cp.wait_send()

    return pl.pallas_call(
        body, name=name, in_specs=[ANY] * nw, out_specs=[ANY] * nw,
        out_shape=[jax.ShapeDtypeStruct((N_CHIPS,) + g.shape[2:], BF16) for g in grads],
        scratch_shapes=[pltpu.SemaphoreType.DMA((N_CHIPS, nw)), pltpu.SemaphoreType.DMA((N_CHIPS, nw))],
    )(*grads)


def _pair_sum(grad, sib, core, name):
    nchip, _, hr, cols = grad.shape
    tr = _div_tile(hr, 256, 16)

    def body(core_ref, a_ref, b_ref, o_ref):
        o_ref[...] = (a_ref[0].astype(F32) + b_ref[...].astype(F32)).astype(BF16)

    return pl.pallas_call(
        body, name=name,
        grid_spec=pltpu.PrefetchScalarGridSpec(
            num_scalar_prefetch=1, grid=(nchip, hr // tr),
            in_specs=[pl.BlockSpec((1, 1, tr, cols), lambda s, i, core_r: (s, core_r[0], i, 0)),
                      pl.BlockSpec((1, tr, cols), lambda s, i, core_r: (s, i, 0))],
            out_specs=pl.BlockSpec((1, tr, cols), lambda s, i, core_r: (s, i, 0))),
        out_shape=jax.ShapeDtypeStruct((nchip, hr, cols), BF16), compiler_params=_cparams())(core, grad, sib)


def _pair_sum_exchange(sums):
    nw = len(sums)

    def build(p_refs, o_refs, sems):
        send_sems, recv_sems, local_sems = sems
        x, y, c, chips = _place()
        me = 2 * x + y
        sibling = (x, y, 1 - c)
        mine = [pltpu.make_async_copy(p_refs[w].at[me], o_refs[w].at[c, 3], local_sems.at[w]) for w in range(nw)]
        first = [_remote(p_refs[w].at[2 * cx + cy], o_refs[w].at[c, k], send_sems.at[k, w], recv_sems.at[k, w],
                         (cx, cy, c)) for k, (cx, cy) in enumerate(chips) for w in range(nw)]

        def start():
            for cp in mine + first:
                cp.start()

        def finish():
            passed = []
            for k in range(N_CHIPS):
                for w in range(nw):
                    slab = o_refs[w].at[c, k]
                    if k < 3:
                        first[k * nw + w].wait_recv()
                    else:
                        mine[w].wait()
                    fwd = _remote(slab, slab, send_sems.at[3 + k, w], recv_sems.at[3 + k, w], sibling)
                    fwd.start()
                    passed.append(fwd)
            for k in range(N_CHIPS):
                for w in range(nw):
                    slab = o_refs[w].at[1 - c, k]
                    _remote(slab, slab, send_sems.at[3 + k, w], recv_sems.at[3 + k, w], sibling).wait_recv()
            for cp in first + passed:
                cp.wait_send()

        return start, finish

    return _Exchange(list(sums), [jax.ShapeDtypeStruct((2,) + p.shape, BF16) for p in sums],
                     [pltpu.SemaphoreType.DMA((7, nw)), pltpu.SemaphoreType.DMA((7, nw)),
                      pltpu.SemaphoreType.DMA((nw,))], build)


def _allreduce_small(vec):
    m_per, n = vec.shape

    def body(x_ref, out_ref, gath_ref, send_sems, recv_sems, local_sem):
        x, y, c, chips = _place()
        me, sibling = (x, y, c), (x, y, 1 - c)

        def rows(px, py, pc):
            return gath_ref.at[pl.ds((4 * px + 2 * py + pc) * m_per, m_per), :]

        def copy(k, block, to, src=None):
            return pltpu.make_async_remote_copy(
                src_ref=rows(*block) if src is None else src, dst_ref=rows(*block),
                send_sem=send_sems.at[k], recv_sem=recv_sems.at[k], device_id=to, device_id_type=MESH)

        mine = pltpu.make_async_copy(x_ref, rows(*me), local_sem)
        mine.start()
        first = [copy(0, me, sibling, src=x_ref)]
        first += [copy(1 + j, me, (*chip, c), src=x_ref) for j, chip in enumerate(chips)]
        for cp in first:
            cp.start()
        passed = [copy(4 + j, (*chip, c), sibling) for j, chip in enumerate(chips)]
        for j, chip in enumerate(chips):
            copy(1 + j, (*chip, c), me).wait_recv()
            passed[j].start()
        copy(0, sibling, me).wait_recv()
        for j, chip in enumerate(chips):
            copy(4 + j, (*chip, 1 - c), me).wait_recv()
        for cp in first + passed:
            cp.wait_send()
        mine.wait()
        acc = gath_ref[pl.ds(0, m_per), :]
        for k in range(1, N_DEV):
            acc = acc + gath_ref[pl.ds(k * m_per, m_per), :]
        out_ref[...] = acc

    vm = pl.BlockSpec(memory_space=pltpu.VMEM)
    return pl.pallas_call(
        body, name="allreduce_small", in_specs=[vm], out_specs=vm,
        out_shape=jax.ShapeDtypeStruct((m_per, n), F32),
        scratch_shapes=[pltpu.VMEM((N_DEV * m_per, n), F32), pltpu.SemaphoreType.DMA((7,)),
                        pltpu.SemaphoreType.DMA((7,)), pltpu.SemaphoreType.DMA],
    )(vec)


def _pack_small(vals):
    flat = jnp.concatenate([vals[name].reshape(-1).astype(F32) for name, _ in SMALL])
    flat = jnp.pad(flat, (0, SMALL_ROWS * LANES - flat.shape[0]))
    return flat.reshape(SMALL_ROWS, LANES)


def _unpack_small(packed):
    flat = packed.reshape(-1)
    out, off = {}, 0
    for name, shape in SMALL:
        n = int(np.prod(shape))
        out[name] = flat[off:off + n].reshape(shape)
        off += n
    return out


def _head_pad_cols(w, heads, real):
    k = w.shape[0]
    return jnp.pad(w.reshape(k, heads, real), ((0, 0), (0, 0), (0, LANES - real))).reshape(k, heads * LANES)


def _rope_tables(positions):
    half = MLA_ROPE // 2
    inv = ROPE_BASE ** (-jnp.arange(half, dtype=F32) / half)
    ang = positions.astype(F32)[:, None] * inv
    cos, sin = jnp.cos(ang), jnp.sin(ang)
    t = positions.shape[0]
    z = lambda n: jnp.zeros((t, n), F32)
    rc = jnp.concatenate([jnp.ones((t, MLA_NOPE), F32), cos, cos, z(LANES - MLA_QK)], axis=1)
    rs1 = jnp.concatenate([z(MLA_NOPE), -sin, z(LANES - MLA_NOPE - half)], axis=1)
    rs2 = jnp.concatenate([z(MLA_NOPE + half), sin, z(LANES - MLA_QK)], axis=1)
    return rc, rs1, rs2


FFN1_WEIGHTS = ("ffn1_w_gu", "ffn1_w_down")
FFN2_WEIGHTS = ("ffn2_w_gu", "ffn2_w_down")
MIXER_WEIGHTS = tuple(n for n, *_ in SHARDED if n not in FFN1_WEIGHTS + FFN2_WEIGHTS)
SHARD_SHAPE = {n: (r, c, kind) for n, r, c, kind in SHARDED}


def _from_blocks(name, gathered):
    r, c, kind = SHARD_SHAPE[name]
    blk = gathered.reshape(N_CHIPS, r, c)
    return blk, (blk.transpose(1, 0, 2).reshape(r, N_CHIPS * c) if kind == "col" else blk.reshape(N_CHIPS * r, c))


def _grad_pair_sums(names, gw, core, tag):
    by_owner = []
    for name in names:
        r, c, kind = SHARD_SHAPE[name]
        blk = gw[name].reshape(r, N_CHIPS, c).transpose(1, 0, 2) if kind == "col" else gw[name].reshape(N_CHIPS, r, c)
        by_owner.append(blk.astype(BF16).reshape(N_CHIPS, 2, r // 2, c))
    received = _swap_halves(by_owner, "grad_swap_" + tag)
    return [_pair_sum(g, s, core, "pair_sum_" + n) for g, s, n in zip(by_owner, received, names)]


def _device_step(x, mem, positions, tgt, small, shards, core):
    d = D_MODEL
    g_ffn1, g_mix, g_ffn2 = small["ffn1_norm"], small["mix_norm"], small["ffn2_norm"]
    big = {}
    for name, g in zip(FFN1_WEIGHTS, _run_exchange(_gather_exchange([shards[n] for n in FFN1_WEIGHTS]), "gather_ffn1")):
        big[name + "#blocks"], big[name] = _from_blocks(name, g)
    wgu1, wd1 = big["ffn1_w_gu#blocks"], big["ffn1_w_down"].reshape(2, FF_TILE, d)
    others = FFN2_WEIGHTS + MIXER_WEIGHTS
    x1, *rest = _ffn_fwd(x, g_ffn1, wgu1, wd1, "ffn1_fwd", ex=_gather_exchange([shards[n] for n in others]))
    for name, g in zip(others, rest):
        big[name + "#blocks"], big[name] = _from_blocks(name, g)
    wgu2, wd2 = big["ffn2_w_gu#blocks"], big["ffn2_w_down"].reshape(2, FF_TILE, d)
    w_in = big["w_in"]
    w_uv_, w_cq, w_ckv = w_in[:, :COL_CQ], w_in[:, COL_CQ:COL_CKV], w_in[:, COL_CKV:COL_KR]
    w_kr = jnp.pad(w_in[:, COL_KR:COL_QM], ((0, 0), (MLA_NOPE, LANES - MLA_QK)))
    w_qm, w_g = w_in[:, COL_QM:COL_GATE], w_in[:, COL_GATE:]
    segs = (w_uv_, w_cq, w_ckv, w_kr, w_qm, w_g)
    wuq = _head_pad_cols(big["mla_w_uq"], MLA_HEADS, MLA_QK)
    ukv = big["mla_w_ukv"].reshape(MLA_KV_RANK, MLA_HEADS, 2, MLA_NOPE)
    wuk = _head_pad_cols(ukv[:, :, 0].reshape(MLA_KV_RANK, -1), MLA_HEADS, MLA_NOPE)
    wuv = _head_pad_cols(ukv[:, :, 1].reshape(MLA_KV_RANK, -1), MLA_HEADS, MLA_NOPE)
    wkv = big["mem_w_kv"]
    wa, wc, wo = big["w_branch_a"], big["w_branch_c"], big["w_out"]
    wb = jnp.pad(big["w_branch_b"].reshape(MLA_HEADS, MLA_NOPE, d),
                 ((0, 0), (0, LANES - MLA_NOPE), (0, 0))).reshape(MLA_HEADS * LANES, d)
    qg = jnp.pad(small["mla_q_norm"], ((0, 0), (0, LANES - MLA_QK)))
    kg = jnp.pad(small["mla_k_norm"], ((0, 0), (0, LANES - MLA_QK)))
    causal = jnp.tril(jnp.ones((CHUNK, CHUNK), bool))
    wt_f = jnp.where(causal[None], small["sg_w"][0], 0.0)
    wt, wt_t = wt_f.astype(BF16), wt_f.transpose(0, 2, 1).astype(BF16)
    bias_l = jnp.repeat(small["sg_b"][0].T, 64, axis=1)
    rc, rs1, rs2 = _rope_tables(positions)

    h = _rms_fwd(x1, g_mix, "mix_norm_fwd")
    zuv, zcq, zckv, zkr, zqm, zg = [
        _mm([(h, w)], F32, "in_proj_%d" % k) for k, w in enumerate(segs)]
    ya = _sgu_fwd(zuv, small["sg_ln_g"], small["sg_ln_b"], wt, bias_l, "sgu_fwd")
    q, k, v, cqn, ckvn = _mla_prep_fwd(zcq, zckv, zkr, small["mla_cq_norm"], small["mla_ckv_norm"], qg, kg,
                                       wuq, wuk, wuv, rc, rs1, rs2, "mla_prep_fwd")
    yb, lse = _attn_fwd(q, k, v, "mla_attn_fwd")
    km, vm, memn = _mem_kv_fwd(mem, small["mem_norm"], wkv, small["mem_k_norm"], "mem_kv_fwd")
    yc = _mem_attn_fwd(zqm, small["mem_q_norm"], km, vm, "mem_attn_fwd")
    x2, merged, pa, pb, pc = _merge_fwd(x1, ya, yb, yc, zg, small["b_gate"], wa, wb, wc, wo, "merge_fwd")
    x3, = _ffn_fwd(x2, g_ffn2, wgu2, wd2, "ffn2_fwd")
    dy, loss_row = _loss_head(x3, tgt, "loss_head")

    gw, gs, slots = {}, {}, {}

    def ffn_grads(prefix, xin, gain, dyin, wgu, wd, ex=None, ex_names=()):
        dx, dgain, xn, dgt, dup, act, *got = _ffn_bwd(xin, gain, dyin, wgu, wd, prefix + "_bwd", ex=ex)
        slots.update(zip(ex_names, got))
        gw[prefix + "_w_gu"] = jnp.concatenate(
            [_mm_tn(xn, dgt, prefix + "_dwg"), _mm_tn(xn, dup, prefix + "_dwu")], axis=1)
        gw[prefix + "_w_down"] = _mm_tn(act, dyin, prefix + "_dwd", scale=0.5)
        gs[prefix + "_norm"] = dgain
        return dx

    dx2 = ffn_grads("ffn2", x2, g_ffn2, dy, wgu2, wd2)
    ffn2_sums = _pair_sum_exchange(_grad_pair_sums(FFN2_WEIGHTS, gw, core, "ffn2"))
    dpa, dpb, dpc, dzg, dbg, dya, dyb, dyc, *got = _merge_bwd(dx2, pa, pb, pc, zg, small["b_gate"], wa, wb, wc, wo,
                                                              "merge_bwd", ex=ffn2_sums)
    slots.update(zip(FFN2_WEIGHTS, got))
    gs["b_gate"] = dbg
    gw["w_out"] = _mm_tn(merged, dx2, "dw_out")
    gw["w_branch_a"] = _mm_tn(ya, dpa, "dw_branch_a")
    gw["w_branch_b"] = _mm_tn(yb, dpb, "dw_branch_b").reshape(MLA_HEADS, LANES, d)[:, :MLA_NOPE].reshape(-1, d)
    gw["w_branch_c"] = _mm_tn(yc, dpc, "dw_branch_c")

    dzuv, dwt, dbl, dlg, dlb = _sgu_bwd(zuv, dya, small["sg_ln_g"], small["sg_ln_b"], wt, wt_t, bias_l, "sgu_bwd")
    gs["sg_w"], gs["sg_b"] = dwt[None], dbl[:, :SG_GROUPS].T[None]
    gs["sg_ln_g"], gs["sg_ln_b"] = dlg, dlb

    dq, dk, dv = _attn_bwd(q, k, v, yb, lse, dyb, "mla_attn_bwd")
    dzcq, dzckv, dzkr, dql, dkl, dgcq, dgckv, dqg, dkg = _mla_prep_bwd(
        zcq, zckv, zkr, small["mla_cq_norm"], small["mla_ckv_norm"], qg, kg, wuq, wuk, wuv, rc, rs1, rs2,
        dq, dk, dv, "mla_prep_bwd")
    gs["mla_cq_norm"], gs["mla_ckv_norm"] = dgcq, dgckv
    gs["mla_q_norm"], gs["mla_k_norm"] = dqg[:, :MLA_QK], dkg[:, :MLA_QK]
    gw["mla_w_uq"] = _mm_tn(cqn, dql, "dw_uq").reshape(MLA_Q_RANK, MLA_HEADS, LANES)[:, :, :MLA_QK].reshape(
        MLA_Q_RANK, -1)
    dwuk = _mm_tn(ckvn, dkl, "dw_uk").reshape(MLA_KV_RANK, MLA_HEADS, LANES)[:, :, :MLA_NOPE]
    dwuv = _mm_tn(ckvn, dv, "dw_uv").reshape(MLA_KV_RANK, MLA_HEADS, LANES)[:, :, :MLA_NOPE]
    gw["mla_w_ukv"] = jnp.concatenate([dwuk, dwuv], axis=2).reshape(MLA_KV_RANK, -1)

    dzqm, dkn, dvm, dmqg = _mem_attn_bwd(zqm, dyc, small["mem_q_norm"], km, vm, "mem_attn_bwd")
    gs["mem_q_norm"] = dmqg
    gw["mem_w_kv"], gs["mem_k_norm"], gs["mem_norm"] = _mem_kv_bwd(
        mem, small["mem_norm"], wkv, small["mem_k_norm"], dkn, dvm, "mem_kv_bwd")

    dzs = (dzuv, dzcq, dzckv, dzkr, dzqm, dzg)
    dh = _mm([(dz, w.T) for dz, w in zip(dzs, segs)], F32, "in_proj_bwd")
    dws = [_mm_tn(h, dz, "dw_in_%d" % k) for k, dz in enumerate(dzs)]
    dws[3] = dws[3][:, MLA_NOPE:MLA_QK]
    gw["w_in"] = jnp.concatenate(dws, axis=1)
    dx1, gs["mix_norm"] = _rms_bwd(x1, g_mix, dh, dx2, "mix_norm_bwd")
    mixer_sums = _pair_sum_exchange(_grad_pair_sums(MIXER_WEIGHTS, gw, core, "mixer"))
    dx = ffn_grads("ffn1", x, g_ffn1, dx1, wgu1, wd1, ex=mixer_sums, ex_names=MIXER_WEIGHTS)
    ffn1_sums = _pair_sum_exchange(_grad_pair_sums(FFN1_WEIGHTS, gw, core, "ffn1"))
    slots.update(zip(FFN1_WEIGHTS, _run_exchange(ffn1_sums, "grad_exchange_ffn1")))
    return loss_row, dx, slots, gs


def kernel(x, mem, positions, ffn1_norm, ffn1_w_gu, ffn1_w_down, mix_norm, w_in, b_gate, sg_ln_g, sg_ln_b, sg_w, sg_b, mla_cq_norm, mla_w_uq, mla_ckv_norm, mla_w_ukv, mla_q_norm, mla_k_norm, mem_norm, mem_w_kv, mem_q_norm, mem_k_norm, w_branch_a, w_branch_b, w_branch_c, w_out, ffn2_norm, ffn2_w_gu, ffn2_w_down, loss_target, m_ffn1_norm, m_ffn1_w_gu, m_ffn1_w_down, m_mix_norm, m_w_in, m_b_gate, m_sg_ln_g, m_sg_ln_b, m_sg_w, m_sg_b, m_mla_cq_norm, m_mla_w_uq, m_mla_ckv_norm, m_mla_w_ukv, m_mla_q_norm, m_mla_k_norm, m_mem_norm, m_mem_w_kv, m_mem_q_norm, m_mem_k_norm, m_w_branch_a, m_w_branch_b, m_w_branch_c, m_w_out, m_ffn2_norm, m_ffn2_w_gu, m_ffn2_w_down, v_ffn1_norm, v_ffn1_w_gu, v_ffn1_w_down, v_mix_norm, v_w_in, v_b_gate, v_sg_ln_g, v_sg_ln_b, v_sg_w, v_sg_b, v_mla_cq_norm, v_mla_w_uq, v_mla_ckv_norm, v_mla_w_ukv, v_mla_q_norm, v_mla_k_norm, v_mem_norm, v_mem_w_kv, v_mem_q_norm, v_mem_k_norm, v_w_branch_a, v_w_branch_b, v_w_branch_c, v_w_out, v_ffn2_norm, v_ffn2_w_gu, v_ffn2_w_down):
    args = dict(locals())
    weights = {n: args[n] for n in WEIGHT_ORDER}
    mom_m = {n: args["m_" + n] for n in WEIGHT_ORDER}
    mom_v = {n: args["v_" + n] for n in WEIGHT_ORDER}
    small = {n: weights[n] for n, _ in SMALL}
    halves = lambda a, r, c: a.reshape(2, r // 2, c)

    shards = {n: halves(weights[n][0].astype(BF16), r, c) for n, r, c, _ in SHARDED}
    core = lax.axis_index("c").astype(jnp.int32).reshape(1)
    loss_row, dx, slots, gs = _device_step(x[0], mem[0], positions[0], loss_target[0], small, shards, core)
    loss = lax.psum(loss_row[0, 0], ("x", "y", "c"))
    small_grads = _unpack_small(_allreduce_small(_pack_small(gs)))

    grads, deltas, new_m, new_v = {}, {}, {}, {}
    for name, r, c, _ in SHARDED:
        outs = _adamw_slots(halves(weights[name][0], r, c), slots[name], halves(mom_m[name][0], r, c),
                            halves(mom_v[name][0], r, c), "adamw_" + name)
        shape = weights[name].shape
        grads[name], deltas[name], new_m[name], new_v[name] = [o.reshape(shape) for o in outs]
    dlt, nm, nv = _adamw(_pack_small(small), _pack_small(small_grads), _pack_small({n: mom_m[n] for n, _ in SMALL}),
                         _pack_small({n: mom_v[n] for n, _ in SMALL}), "adamw_small")
    for name, _ in SMALL:
        grads[name] = small_grads[name]
    deltas.update(_unpack_small(dlt))
    new_m.update(_unpack_small(nm))
    new_v.update(_unpack_small(nv))

    return (loss, dx[None], *[grads[n] for n in WEIGHT_ORDER], *[deltas[n] for n in WEIGHT_ORDER],
            *[new_m[n] for n in WEIGHT_ORDER], *[new_v[n] for n in WEIGHT_ORDER])
```

```python
import functools
from typing import Callable, NamedTuple

import numpy as np
import jax
import jax.numpy as jnp
from jax import lax
from jax.experimental import pallas as pl
from jax.experimental.pallas import tpu as pltpu

F32 = jnp.float32
BF16 = jnp.bfloat16

D_MODEL = 1024
D_FF = 2816
FF_TILE = 1408
SG_WIDTH = 512
SG_GROUPS = 8
CHUNK = 128
MLA_HEADS = 8
MLA_QK = 96
MLA_NOPE = 64
MLA_ROPE = 32
MLA_Q_RANK = 384
MLA_KV_RANK = 256
MEM_HEADS = 4
MEM_LEN = 256
LANES = 128
EPS = 1e-6
NEG = -1e30
ROPE_BASE = 10000.0
N_CHIPS = 4
N_DEV = 8

ADAM_LR = 0.001
ADAM_B1 = 0.9
ADAM_B2 = 0.999
ADAM_EPS = 1e-08
ADAM_WD = 0.01
ADAM_STEP = 10

COL_V = 512
COL_CQ = 1024
COL_CKV = 1408
COL_KR = 1664
COL_QM = 1696
COL_GATE = 2208
IN_COLS = 5280

VMEM_LIMIT_BYTES = 56 * 1024 * 1024
INV_SQRT2 = 0.7071067811865476
INV_SQRT_2PI = 0.3989422804014327
LOG2E = 1.4426950408889634

SHARDED = (
    ("ffn1_w_gu", 1024, 1408, "col"),
    ("ffn1_w_down", 704, 1024, "row"),
    ("w_in", 1024, 1320, "col"),
    ("mla_w_uq", 384, 192, "col"),
    ("mla_w_ukv", 256, 256, "col"),
    ("mem_w_kv", 256, 1024, "row"),
    ("w_branch_a", 512, 256, "col"),
    ("w_branch_b", 512, 256, "col"),
    ("w_branch_c", 512, 256, "col"),
    ("w_out", 256, 1024, "row"),
    ("ffn2_w_gu", 1024, 1408, "col"),
    ("ffn2_w_down", 704, 1024, "row"),
)
SMALL = (
    ("ffn1_norm", (1, 1024)), ("mix_norm", (1, 1024)), ("b_gate", (1, 3072)),
    ("sg_ln_g", (1, 512)), ("sg_ln_b", (1, 512)), ("sg_w", (1, 8, 128, 128)),
    ("sg_b", (1, 8, 128)), ("mla_cq_norm", (1, 384)), ("mla_ckv_norm", (1, 256)),
    ("mla_q_norm", (1, 96)), ("mla_k_norm", (1, 96)), ("mem_norm", (1, 1024)),
    ("mem_q_norm", (1, 128)), ("mem_k_norm", (1, 128)), ("ffn2_norm", (1, 1024)),
)
WEIGHT_ORDER = (
    "ffn1_norm", "ffn1_w_gu", "ffn1_w_down", "mix_norm", "w_in", "b_gate", "sg_ln_g", "sg_ln_b",
    "sg_w", "sg_b", "mla_cq_norm", "mla_w_uq", "mla_ckv_norm", "mla_w_ukv", "mla_q_norm",
    "mla_k_norm", "mem_norm", "mem_w_kv", "mem_q_norm", "mem_k_norm", "w_branch_a", "w_branch_b",
    "w_branch_c", "w_out", "ffn2_norm", "ffn2_w_gu", "ffn2_w_down",
)

_N_SMALL = sum(int(np.prod(s)) for _, s in SMALL)
SMALL_ROWS = -(-_N_SMALL // (LANES * 8)) * 8

MESH = pl.DeviceIdType.MESH


def _cparams():
    return pltpu.CompilerParams(vmem_limit_bytes=VMEM_LIMIT_BYTES)


def _dot(a, b):
    return jnp.dot(a, b, preferred_element_type=F32)


def _dot_nt(a, b):
    return lax.dot_general(a, b, (((1,), (1,)), ((), ())), preferred_element_type=F32)


def _dot_tn(a, b):
    return lax.dot_general(a, b, (((0,), (0,)), ((), ())), preferred_element_type=F32)


def _gelu(x):
    return 0.5 * x * (1.0 + lax.erf(x * INV_SQRT2))


def _gelu_grad(x):
    return 0.5 * (1.0 + lax.erf(x * INV_SQRT2)) + x * jnp.exp(-0.5 * x * x) * INV_SQRT_2PI


def _rstd(x, n):
    return lax.rsqrt(jnp.sum(x * x, axis=-1, keepdims=True) * (1.0 / n) + EPS)


def _rms_vjp(x, r, g, dy, n):
    dxh = dy * g
    dx = r * dxh - x * (r * r * r) * (jnp.sum(dxh * x, axis=-1, keepdims=True) * (1.0 / n))
    return dx, dy * x * r


def _row_tile(t, want):
    return min(t, want)


def _wide_tile(n):
    if n <= 1024:
        return n
    if n % 1024 == 0:
        return 1024
    assert n % FF_TILE == 0, n
    return FF_TILE


def _rms_fwd(x, g, name):
    t, d = x.shape
    tm = _row_tile(t, 512)

    def body(x_ref, g_ref, o_ref):
        xv = x_ref[...]
        o_ref[...] = (xv * _rstd(xv, d) * g_ref[...]).astype(BF16)

    return pl.pallas_call(
        body, name=name, grid=(t // tm,),
        in_specs=[pl.BlockSpec((tm, d), lambda i: (i, 0)), pl.BlockSpec((1, d), lambda i: (0, 0))],
        out_specs=pl.BlockSpec((tm, d), lambda i: (i, 0)),
        out_shape=jax.ShapeDtypeStruct((t, d), BF16), compiler_params=_cparams())(x, g)


def _rms_bwd(x, g, dxn, dres, name):
    t, d = x.shape
    tm = _row_tile(t, 256)

    def body(x_ref, g_ref, d_ref, r_ref, dx_ref, dg_ref):
        @pl.when(pl.program_id(0) == 0)
        def _():
            dg_ref[...] = jnp.zeros_like(dg_ref)

        xv = x_ref[...]
        r = _rstd(xv, d)
        dx, dgr = _rms_vjp(xv, r, g_ref[...], d_ref[...].astype(F32), d)
        dx_ref[...] = r_ref[...] + dx
        dg_ref[...] += jnp.sum(dgr, axis=0, keepdims=True)

    row = pl.BlockSpec((tm, d), lambda i: (i, 0))
    vec = pl.BlockSpec((1, d), lambda i: (0, 0))
    return pl.pallas_call(
        body, name=name, grid=(t // tm,), in_specs=[row, vec, row, row], out_specs=[row, vec],
        out_shape=[jax.ShapeDtypeStruct((t, d), F32), jax.ShapeDtypeStruct((1, d), F32)],
        compiler_params=_cparams())(x, g, dxn, dres)


def _mm(pairs, out_dtype, name):
    t = pairs[0][0].shape[0]
    n = pairs[0][1].shape[1]
    tm = _row_tile(t, 512)
    tn = _wide_tile(n)
    np_ = len(pairs)

    def body(*refs):
        o_ref = refs[2 * np_]
        acc = None
        for a_ref, w_ref in zip(refs[:np_], refs[np_:2 * np_]):
            part = _dot(a_ref[...].astype(BF16), w_ref[...])
            acc = part if acc is None else acc + part
        o_ref[...] = acc.astype(out_dtype)

    in_specs = [pl.BlockSpec((tm, a.shape[1]), lambda i, j: (i, 0)) for a, _ in pairs]
    in_specs += [pl.BlockSpec((w.shape[0], tn), lambda i, j: (0, j)) for _, w in pairs]
    return pl.pallas_call(
        body, name=name, grid=(t // tm, n // tn), in_specs=in_specs,
        out_specs=pl.BlockSpec((tm, tn), lambda i, j: (i, j)),
        out_shape=jax.ShapeDtypeStruct((t, n), out_dtype), compiler_params=_cparams(),
    )(*[a for a, _ in pairs], *[w for _, w in pairs])


def _mm_tn(a, b, name, scale=1.0):
    t, m = a.shape
    n = b.shape[1]
    tm, tn = _wide_tile(m), _wide_tile(n)
    tk = _row_tile(t, 512)
    nk = t // tk

    def body(a_ref, b_ref, o_ref):
        k = pl.program_id(2)

        @pl.when(k == 0)
        def _():
            o_ref[...] = jnp.zeros_like(o_ref)

        o_ref[...] += _dot_tn(a_ref[...].astype(BF16), b_ref[...].astype(BF16))
        if scale != 1.0:
            @pl.when(k == nk - 1)
            def _():
                o_ref[...] = o_ref[...] * scale

    return pl.pallas_call(
        body, name=name, grid=(m // tm, n // tn, nk),
        in_specs=[pl.BlockSpec((tk, tm), lambda i, j, k: (k, i)),
                  pl.BlockSpec((tk, tn), lambda i, j, k: (k, j))],
        out_specs=pl.BlockSpec((tm, tn), lambda i, j, k: (i, j)),
        out_shape=jax.ShapeDtypeStruct((m, n), F32), compiler_params=_cparams())(a, b)


class _Exchange(NamedTuple):
    operands: list
    out_shapes: list
    sem_shapes: list
    build: Callable


def _call_with_exchange(ex, body, name, grid, in_specs, out_specs, out_shape, scratch_shapes, operands):
    if ex is None:
        return pl.pallas_call(body, name=name, grid=grid, in_specs=in_specs, out_specs=out_specs, out_shape=out_shape,
                              scratch_shapes=scratch_shapes, compiler_params=_cparams())(*operands)
    n_in, n_out, n_scr = len(in_specs), len(out_specs), len(scratch_shapes)
    k_in, k_out = len(ex.operands), len(ex.out_shapes)

    def carried(*refs):
        a, b = n_in, n_in + k_in
        c, e = b + n_out, b + n_out + k_out
        f = e + n_scr
        start, finish = ex.build(refs[a:b], refs[c:e], refs[f:])
        steps = [pl.program_id(ax) for ax in range(len(grid))]
        first = functools.reduce(jnp.logical_and, [s == 0 for s in steps])
        last = functools.reduce(jnp.logical_and, [s == n - 1 for s, n in zip(steps, grid)])
        pl.when(first)(start)
        body(*refs[:a], *refs[b:c], *refs[e:f])
        pl.when(last)(finish)

    return pl.pallas_call(
        carried, name=name, grid=grid, in_specs=list(in_specs) + [ANY] * k_in,
        out_specs=list(out_specs) + [ANY] * k_out, out_shape=list(out_shape) + list(ex.out_shapes),
        scratch_shapes=list(scratch_shapes) + list(ex.sem_shapes), compiler_params=_cparams(),
    )(*operands, *ex.operands)


def _run_exchange(ex, name):
    k_in, k_out = len(ex.operands), len(ex.out_shapes)

    def body(*refs):
        start, finish = ex.build(refs[:k_in], refs[k_in:k_in + k_out], refs[k_in + k_out:])
        start()
        finish()

    return pl.pallas_call(body, name=name, in_specs=[ANY] * k_in, out_specs=[ANY] * k_out,
                          out_shape=list(ex.out_shapes), scratch_shapes=list(ex.sem_shapes))(*ex.operands)


def _ffn_fwd(x, g, wgu4, wd2, name, ex=None):
    t, d = x.shape
    tm = _row_tile(t, 512)

    def body(x_ref, g_ref, wg_ref, wu_ref, wd_ref, o_ref, gg_ref, uu_ref, xn_scr, acc_scr):
        j = pl.program_id(1)

        @pl.when(j == 0)
        def _():
            xv = x_ref[...]
            xn_scr[...] = (xv * _rstd(xv, d) * g_ref[...]).astype(BF16)
            acc_scr[...] = jnp.zeros_like(acc_scr)

        xn = xn_scr[...]
        gg = _dot(xn, wg_ref[0])
        uu = _dot(xn, wu_ref[0])
        gg_ref[...] = gg.astype(BF16)
        uu_ref[...] = uu.astype(BF16)
        act = gg * jax.nn.sigmoid(gg) * uu
        acc_scr[...] += _dot(act.astype(BF16), wd_ref[0])

        @pl.when(j == 1)
        def _():
            o_ref[...] = x_ref[...] + 0.5 * acc_scr[...]

    row = pl.BlockSpec((tm, d), lambda i, j: (i, 0))
    ffb = pl.BlockSpec((tm, FF_TILE), lambda i, j: (i, j))
    return _call_with_exchange(
        ex, body, name, (t // tm, 2),
        [row, pl.BlockSpec((1, d), lambda i, j: (0, 0)),
         pl.BlockSpec((1, d, FF_TILE), lambda i, j: (j, 0, 0)),
         pl.BlockSpec((1, d, FF_TILE), lambda i, j: (j + 2, 0, 0)),
         pl.BlockSpec((1, FF_TILE, d), lambda i, j: (j, 0, 0))],
        [row, ffb, ffb],
        [jax.ShapeDtypeStruct((t, d), F32), jax.ShapeDtypeStruct((t, D_FF), BF16), jax.ShapeDtypeStruct((t, D_FF), BF16)],
        [pltpu.VMEM((tm, d), BF16), pltpu.VMEM((tm, d), F32)], (x, g, wgu4, wgu4, wd2))


def _ffn_bwd(x, g, dy, gpre, upre, wgu4, wd2, name, ex=None):
    t, d = x.shape
    tm = _row_tile(t, 256)

    def body(x_ref, g_ref, dy_ref, gg_ref, uu_ref, wg_ref, wu_ref, wd_ref,
             dx_ref, dgain_ref, xn_ref, dg_ref, du_ref, act_ref, acc_scr):
        i, j = pl.program_id(0), pl.program_id(1)

        @pl.when((i == 0) & (j == 0))
        def _():
            dgain_ref[...] = jnp.zeros_like(dgain_ref)

        @pl.when(j == 0)
        def _():
            xv = x_ref[...]
            xn_ref[...] = (xv * _rstd(xv, d) * g_ref[...]).astype(BF16)
            acc_scr[...] = jnp.zeros_like(acc_scr)

        gg = gg_ref[...].astype(F32)
        uu = uu_ref[...].astype(F32)
        sg = jax.nn.sigmoid(gg)
        silu = gg * sg
        act_ref[...] = (silu * uu).astype(BF16)
        dyh = (0.5 * dy_ref[...]).astype(BF16)
        dact = _dot_nt(dyh, wd_ref[0])
        du = (dact * silu).astype(BF16)
        dgt = (dact * uu * (sg * (1.0 + gg * (1.0 - sg)))).astype(BF16)
        du_ref[...] = du
        dg_ref[...] = dgt
        acc_scr[...] += _dot_nt(dgt, wg_ref[0]) + _dot_nt(du, wu_ref[0])

        @pl.when(j == 1)
        def _():
            xv = x_ref[...]
            dx, dgr = _rms_vjp(xv, _rstd(xv, d), g_ref[...], acc_scr[...], d)
            dx_ref[...] = dy_ref[...] + dx
            dgain_ref[...] += jnp.sum(dgr, axis=0, keepdims=True)

    row = pl.BlockSpec((tm, d), lambda i, j: (i, 0))
    vec = pl.BlockSpec((1, d), lambda i, j: (0, 0))
    ffb = pl.BlockSpec((tm, FF_TILE), lambda i, j: (i, j))
    return _call_with_exchange(
        ex, body, name, (t // tm, 2),
        [row, vec, row, ffb, ffb,
         pl.BlockSpec((1, d, FF_TILE), lambda i, j: (j, 0, 0)),
         pl.BlockSpec((1, d, FF_TILE), lambda i, j: (j + 2, 0, 0)),
         pl.BlockSpec((1, FF_TILE, d), lambda i, j: (j, 0, 0))],
        [row, vec, row, ffb, ffb, ffb],
        [jax.ShapeDtypeStruct((t, d), F32), jax.ShapeDtypeStruct((1, d), F32),
         jax.ShapeDtypeStruct((t, d), BF16), jax.ShapeDtypeStruct((t, D_FF), BF16),
         jax.ShapeDtypeStruct((t, D_FF), BF16), jax.ShapeDtypeStruct((t, D_FF), BF16)],
        [pltpu.VMEM((tm, d), F32)], (x, g, dy, gpre, upre, wgu4, wgu4, wd2))


def _loss_head(y, tgt, name):
    t, d = y.shape
    tm = _row_tile(t, 512)

    def body(y_ref, t_ref, dy_ref, loss_ref):
        @pl.when(pl.program_id(0) == 0)
        def _():
            loss_ref[...] = jnp.zeros_like(loss_ref)

        e = y_ref[...] - t_ref[...]
        dy_ref[...] = e * (1.0 / d)
        part = 0.5 * jnp.sum(jnp.sum(e * e, axis=-1, keepdims=True) * (1.0 / d), axis=0, keepdims=True)
        loss_ref[...] += jnp.broadcast_to(part, loss_ref.shape)

    row = pl.BlockSpec((tm, d), lambda i: (i, 0))
    return pl.pallas_call(
        body, name=name, grid=(t // tm,), in_specs=[row, row],
        out_specs=[row, pl.BlockSpec((1, LANES), lambda i: (0, 0))],
        out_shape=[jax.ShapeDtypeStruct((t, d), F32), jax.ShapeDtypeStruct((1, LANES), F32)],
        compiler_params=_cparams())(y, tgt)


def _sgu_layernorm(vpre, lg, lb):
    v = _gelu(vpre)
    mu = jnp.mean(v, axis=-1, keepdims=True)
    xc = v - mu
    rstd = lax.rsqrt(jnp.mean(xc * xc, axis=-1, keepdims=True) + EPS)
    xhat = xc * rstd
    return xhat, rstd, xhat * lg + lb


def _sgu_fwd(zuv, lg, lb, wt, bias_l, name):
    t = zuv.shape[0]
    tm = _row_tile(t, 512)

    def body(u_ref, v_ref, lg_ref, lb_ref, wt_ref, bl_ref, o_ref, vln_scr):
        _, _, vln = _sgu_layernorm(v_ref[...], lg_ref[...], lb_ref[...])
        vln_scr[...] = vln.astype(BF16)
        lo = lax.broadcasted_iota(jnp.int32, (CHUNK, LANES), 1) < 64
        for c in range(tm // CHUNK):
            rows = slice(c * CHUNK, (c + 1) * CHUNK)
            for p in range(SG_GROUPS // 2):
                cols = slice(p * LANES, (p + 1) * LANES)
                vp = vln_scr[rows, cols]
                mixed = jnp.where(lo, _dot(wt_ref[2 * p], vp), _dot(wt_ref[2 * p + 1], vp)) + bl_ref[:, cols]
                o_ref[rows, cols] = (_gelu(u_ref[rows, cols]) * mixed).astype(BF16)

    half = lambda k: pl.BlockSpec((tm, SG_WIDTH), lambda i: (i, k))
    vec = pl.BlockSpec((1, SG_WIDTH), lambda i: (0, 0))
    return pl.pallas_call(
        body, name=name, grid=(t // tm,),
        in_specs=[half(0), half(1), vec, vec,
                  pl.BlockSpec((SG_GROUPS, CHUNK, CHUNK), lambda i: (0, 0, 0)),
                  pl.BlockSpec((CHUNK, SG_WIDTH), lambda i: (0, 0))],
        out_specs=pl.BlockSpec((tm, SG_WIDTH), lambda i: (i, 0)),
        out_shape=jax.ShapeDtypeStruct((t, SG_WIDTH), BF16),
        scratch_shapes=[pltpu.VMEM((tm, SG_WIDTH), BF16)],
        compiler_params=_cparams())(zuv, zuv, lg, lb, wt, bias_l)


def _sgu_bwd(zuv, dya, lg, lb, wt, wt_t, bias_l, name):
    t = zuv.shape[0]
    tm = _row_tile(t, 256)
    nsteps = t // tm

    def body(u_ref, v_ref, dy_ref, lg_ref, lb_ref, wt_ref, wtt_ref, bl_ref,
             dz_ref, dwt_ref, dbl_ref, dlg_ref, dlb_ref, vln_scr, dvln_scr, dbacc_scr):
        step = pl.program_id(0)

        @pl.when(step == 0)
        def _():
            dwt_ref[...] = jnp.zeros_like(dwt_ref)
            dlg_ref[...] = jnp.zeros_like(dlg_ref)
            dlb_ref[...] = jnp.zeros_like(dlb_ref)
            dbl_ref[...] = jnp.zeros_like(dbl_ref)
            dbacc_scr[...] = jnp.zeros_like(dbacc_scr)

        vpre = v_ref[...]
        lgv = lg_ref[...]
        xhat, rstd, vln = _sgu_layernorm(vpre, lgv, lb_ref[...])
        vln_scr[...] = vln.astype(BF16)
        lo = lax.broadcasted_iota(jnp.int32, (CHUNK, LANES), 1) < 64
        for c in range(tm // CHUNK):
            rows = slice(c * CHUNK, (c + 1) * CHUNK)
            for p in range(SG_GROUPS // 2):
                cols = slice(p * LANES, (p + 1) * LANES)
                vp = vln_scr[rows, cols]
                mixed = jnp.where(lo, _dot(wt_ref[2 * p], vp), _dot(wt_ref[2 * p + 1], vp)) + bl_ref[:, cols]
                upre = u_ref[rows, cols]
                dyp = dy_ref[rows, cols]
                dz_ref[rows, cols] = (dyp * mixed * _gelu_grad(upre)).astype(BF16)
                dm = dyp * _gelu(upre)
                dbacc_scr[:, cols] += dm
                dlo = jnp.where(lo, dm, 0.0).astype(BF16)
                dhi = jnp.where(lo, 0.0, dm).astype(BF16)
                dvln_scr[rows, cols] = _dot(wtt_ref[2 * p], dlo) + _dot(wtt_ref[2 * p + 1], dhi)
                dwt_ref[2 * p] += _dot_nt(dlo, vp)
                dwt_ref[2 * p + 1] += _dot_nt(dhi, vp)
        dvln = dvln_scr[...]
        dlg_ref[...] += jnp.sum(dvln * xhat, axis=0, keepdims=True)
        dlb_ref[...] += jnp.sum(dvln, axis=0, keepdims=True)
        dxh = dvln * lgv
        dv = rstd * (dxh - jnp.mean(dxh, axis=-1, keepdims=True)
                     - xhat * jnp.mean(dxh * xhat, axis=-1, keepdims=True))
        dz_ref[:, SG_WIDTH:] = (dv * _gelu_grad(vpre)).astype(BF16)

        @pl.when(step == nsteps - 1)
        def _():
            rr = lax.broadcasted_iota(jnp.int32, (CHUNK, CHUNK), 0)
            cc = lax.broadcasted_iota(jnp.int32, (CHUNK, CHUNK), 1)
            tril = (cc <= rr).astype(F32)
            for gidx in range(SG_GROUPS):
                dwt_ref[gidx] = dwt_ref[gidx] * tril
            kk = lax.broadcasted_iota(jnp.int32, (SG_WIDTH, LANES), 0)
            gg = lax.broadcasted_iota(jnp.int32, (SG_WIDTH, LANES), 1)
            sel = ((kk // 64) == gg).astype(F32)
            dbl_ref[...] = jnp.dot(dbacc_scr[...], sel, preferred_element_type=F32,
                                   precision=lax.Precision.HIGHEST)

    half = lambda k: pl.BlockSpec((tm, SG_WIDTH), lambda i: (i, k))
    vec = pl.BlockSpec((1, SG_WIDTH), lambda i: (0, 0))
    wspec = pl.BlockSpec((SG_GROUPS, CHUNK, CHUNK), lambda i: (0, 0, 0))
    return pl.pallas_call(
        body, name=name, grid=(nsteps,),
        in_specs=[half(0), half(1), pl.BlockSpec((tm, SG_WIDTH), lambda i: (i, 0)), vec, vec,
                  wspec, wspec, pl.BlockSpec((CHUNK, SG_WIDTH), lambda i: (0, 0))],
        out_specs=[pl.BlockSpec((tm, 2 * SG_WIDTH), lambda i: (i, 0)), wspec,
                   pl.BlockSpec((CHUNK, LANES), lambda i: (0, 0)), vec, vec],
        out_shape=[jax.ShapeDtypeStruct((t, 2 * SG_WIDTH), BF16),
                   jax.ShapeDtypeStruct((SG_GROUPS, CHUNK, CHUNK), F32),
                   jax.ShapeDtypeStruct((CHUNK, LANES), F32),
                   jax.ShapeDtypeStruct((1, SG_WIDTH), F32), jax.ShapeDtypeStruct((1, SG_WIDTH), F32)],
        scratch_shapes=[pltpu.VMEM((tm, SG_WIDTH), BF16), pltpu.VMEM((tm, SG_WIDTH), F32),
                        pltpu.VMEM((CHUNK, SG_WIDTH), F32)],
        compiler_params=_cparams())(zuv, zuv, dya, lg, lb, wt, wt_t, bias_l)


def _rope(x, c, s1, s2):
    return x * c + pltpu.roll(x, LANES - 16, 1) * s1 + pltpu.roll(x, 16, 1) * s2


def _rope_t(dy, c, s1, s2):
    return dy * c + pltpu.roll(dy * s1, 16, 1) + pltpu.roll(dy * s2, LANES - 16, 1)


def _mla_prep_fwd(zcq, zckv, zkr, gcq, gckv, qg, kg, wuq, wuk, wuv, rc, rs1, rs2, name, ex=None):
    t = zcq.shape[0]
    tm = _row_tile(t, 256)
    hd = MLA_HEADS * LANES

    def body(zcq_ref, zckv_ref, zkr_ref, gcq_ref, gckv_ref, qg_ref, kg_ref, wuq_ref, wuk_ref, wuv_ref,
             c_ref, s1_ref, s2_ref, q_ref, k_ref, v_ref, cqn_ref, ckvn_ref):
        c, s1, s2 = c_ref[...], s1_ref[...], s2_ref[...]
        xq = zcq_ref[...]
        cqn = (xq * _rstd(xq, MLA_Q_RANK) * gcq_ref[...]).astype(BF16)
        cqn_ref[...] = cqn
        ql = _dot(cqn, wuq_ref[...])
        xk = zckv_ref[...]
        ckvn = (xk * _rstd(xk, MLA_KV_RANK) * gckv_ref[...]).astype(BF16)
        ckvn_ref[...] = ckvn
        kl = _dot(ckvn, wuk_ref[...])
        v_ref[...] = _dot(ckvn, wuv_ref[...]).astype(BF16)
        kr = zkr_ref[...]
        for h in range(MLA_HEADS):
            sl = slice(h * LANES, (h + 1) * LANES)
            qh = ql[:, sl]
            q_ref[:, sl] = _rope(qh * _rstd(qh, MLA_QK) * qg_ref[...], c, s1, s2).astype(BF16)
            kh = kl[:, sl] + kr
            k_ref[:, sl] = _rope(kh * _rstd(kh, MLA_QK) * kg_ref[...], c, s1, s2).astype(BF16)

    row = lambda n: pl.BlockSpec((tm, n), lambda i: (i, 0))
    full = lambda a: pl.BlockSpec(a.shape, lambda i: (0, 0))
    return _call_with_exchange(
        ex, body, name, (t // tm,),
        [row(MLA_Q_RANK), row(MLA_KV_RANK), row(LANES), full(gcq), full(gckv), full(qg), full(kg),
         full(wuq), full(wuk), full(wuv), row(LANES), row(LANES), row(LANES)],
        [row(hd), row(hd), row(hd), row(MLA_Q_RANK), row(MLA_KV_RANK)],
        [jax.ShapeDtypeStruct((t, hd), BF16)] * 3
        + [jax.ShapeDtypeStruct((t, MLA_Q_RANK), BF16), jax.ShapeDtypeStruct((t, MLA_KV_RANK), BF16)],
        [], (zcq, zckv, zkr, gcq, gckv, qg, kg, wuq, wuk, wuv, rc, rs1, rs2))


def _mla_prep_bwd(zcq, zckv, zkr, gcq, gckv, qg, kg, wuq, wuk, wuv, rc, rs1, rs2, dq, dk, dv, name):
    t = zcq.shape[0]
    tm = _row_tile(t, 256)
    hd = MLA_HEADS * LANES

    def body(zcq_ref, zckv_ref, zkr_ref, gcq_ref, gckv_ref, qg_ref, kg_ref, wuq_ref, wuk_ref, wuv_ref,
             c_ref, s1_ref, s2_ref, dq_ref, dk_ref, dv_ref,
             dzcq_ref, dzckv_ref, dzkr_ref, dql_ref, dkl_ref, dgcq_ref, dgckv_ref, dqg_ref, dkg_ref):
        @pl.when(pl.program_id(0) == 0)
        def _():
            for ref in (dgcq_ref, dgckv_ref, dqg_ref, dkg_ref):
                ref[...] = jnp.zeros_like(ref)

        c, s1, s2 = c_ref[...], s1_ref[...], s2_ref[...]
        qgv, kgv = qg_ref[...], kg_ref[...]
        xq = zcq_ref[...]
        rq = _rstd(xq, MLA_Q_RANK)
        ql = _dot((xq * rq * gcq_ref[...]).astype(BF16), wuq_ref[...])
        xk = zckv_ref[...]
        rk = _rstd(xk, MLA_KV_RANK)
        kl = _dot((xk * rk * gckv_ref[...]).astype(BF16), wuk_ref[...])
        kr = zkr_ref[...]
        dqg_acc = jnp.zeros((tm, LANES), F32)
        dkg_acc = jnp.zeros((tm, LANES), F32)
        dkr = jnp.zeros((tm, LANES), F32)
        for h in range(MLA_HEADS):
            sl = slice(h * LANES, (h + 1) * LANES)
            qh = ql[:, sl]
            dqh, dgr = _rms_vjp(qh, _rstd(qh, MLA_QK), qgv, _rope_t(dq_ref[:, sl], c, s1, s2), MLA_QK)
            dql_ref[:, sl] = dqh.astype(BF16)
            dqg_acc += dgr
            kh = kl[:, sl] + kr
            dkh, dgr = _rms_vjp(kh, _rstd(kh, MLA_QK), kgv, _rope_t(dk_ref[:, sl], c, s1, s2), MLA_QK)
            dkl_ref[:, sl] = dkh.astype(BF16)
            dkg_acc += dgr
            dkr += dkh
        dqg_ref[...] += jnp.sum(dqg_acc, axis=0, keepdims=True)
        dkg_ref[...] += jnp.sum(dkg_acc, axis=0, keepdims=True)
        lane = lax.broadcasted_iota(jnp.int32, (tm, LANES), 1)
        dzkr_ref[...] = jnp.where((lane >= MLA_NOPE) & (lane < MLA_QK), dkr, 0.0).astype(BF16)
        dcqn = _dot_nt(dql_ref[...], wuq_ref[...])
        dx, dgr = _rms_vjp(xq, rq, gcq_ref[...], dcqn, MLA_Q_RANK)
        dzcq_ref[...] = dx.astype(BF16)
        dgcq_ref[...] += jnp.sum(dgr, axis=0, keepdims=True)
        dckvn = _dot_nt(dkl_ref[...], wuk_ref[...]) + _dot_nt(dv_ref[...].astype(BF16), wuv_ref[...])
        dx, dgr = _rms_vjp(xk, rk, gckv_ref[...], dckvn, MLA_KV_RANK)
        dzckv_ref[...] = dx.astype(BF16)
        dgckv_ref[...] += jnp.sum(dgr, axis=0, keepdims=True)

    row = lambda n: pl.BlockSpec((tm, n), lambda i: (i, 0))
    full = lambda a: pl.BlockSpec(a.shape, lambda i: (0, 0))
    vec = lambda n: pl.BlockSpec((1, n), lambda i: (0, 0))
    return pl.pallas_call(
        body, name=name, grid=(t // tm,),
        in_specs=[row(MLA_Q_RANK), row(MLA_KV_RANK), row(LANES), full(gcq), full(gckv), full(qg), full(kg),
                  full(wuq), full(wuk), full(wuv), row(LANES), row(LANES), row(LANES), row(hd), row(hd), row(hd)],
        out_specs=[row(MLA_Q_RANK), row(MLA_KV_RANK), row(LANES), row(hd), row(hd),
                   vec(MLA_Q_RANK), vec(MLA_KV_RANK), vec(LANES), vec(LANES)],
        out_shape=[jax.ShapeDtypeStruct((t, MLA_Q_RANK), BF16), jax.ShapeDtypeStruct((t, MLA_KV_RANK), BF16),
                   jax.ShapeDtypeStruct((t, LANES), BF16), jax.ShapeDtypeStruct((t, hd), BF16),
                   jax.ShapeDtypeStruct((t, hd), BF16), jax.ShapeDtypeStruct((1, MLA_Q_RANK), F32),
                   jax.ShapeDtypeStruct((1, MLA_KV_RANK), F32), jax.ShapeDtypeStruct((1, LANES), F32),
                   jax.ShapeDtypeStruct((1, LANES), F32)],
        compiler_params=_cparams(),
    )(zcq, zckv, zkr, gcq, gckv, qg, kg, wuq, wuk, wuv, rc, rs1, rs2, dq, dk, dv)


def _attn_tiles(t):
    tq = 512 if t >= 2048 else 128
    return tq, min(t, 4 * tq), min(t, 2 * tq)


def _causal_keep(tq, tk, i, j):
    row = lax.broadcasted_iota(jnp.int32, (tq, tk), 0)
    col = lax.broadcasted_iota(jnp.int32, (tq, tk), 1)
    return (col - row) <= (i * tq - j * tk)


ATTN_FWD_HEADS_PER_STEP = 2
ATTN_BWD_HEADS_PER_STEP = 2


def _attn_fwd(q, k, v, name):
    t, hd = q.shape
    hp = ATTN_FWD_HEADS_PER_STEP
    tq, tk, _ = _attn_tiles(t)
    pairs = [(i, j) for i in range(t // tq) for j in range(((i + 1) * tq - 1) // tk + 1)]
    ii = np.array([p[0] for p in pairs], np.int32)
    jj = np.array([p[1] for p in pairs], np.int32)
    scale2 = MLA_QK ** -0.5 * LOG2E

    def body(ii_ref, jj_ref, q_ref, k_ref, v_ref, o_ref, lse_ref, m_scr, l_scr, acc_scr):
        s_id = pl.program_id(1)
        i, j = ii_ref[s_id], jj_ref[s_id]
        last = j == ((i + 1) * tq - 1) // tk

        @pl.when(j == 0)
        def _():
            m_scr[...] = jnp.full_like(m_scr, NEG)
            l_scr[...] = jnp.zeros_like(l_scr)
            acc_scr[...] = jnp.zeros_like(acc_scr)

        def step(masked):
            for hh in range(hp):
                sl = slice(hh * LANES, (hh + 1) * LANES)
                s = _dot_nt(q_ref[:, sl], k_ref[:, sl]) * scale2
                if masked:
                    s = jnp.where(_causal_keep(tq, tk, i, j), s, NEG)
                m_prev = m_scr[hh]
                m_new = jnp.maximum(m_prev, jnp.max(s, axis=1, keepdims=True))
                p = jnp.exp2(s - m_new)
                alpha = jnp.exp2(m_prev - m_new)
                l_new = alpha * l_scr[hh] + jnp.sum(p, axis=1, keepdims=True)
                acc = alpha * acc_scr[:, sl] + _dot(p.astype(BF16), v_ref[:, sl])
                if masked:
                    o_ref[:, sl] = (acc / l_new).astype(BF16)
                    lse_ref[:, sl] = jnp.broadcast_to(m_new + jnp.log(l_new) * LOG2E, (tq, LANES))
                else:
                    l_scr[hh] = l_new
                    acc_scr[:, sl] = acc
                    m_scr[hh] = m_new

        @pl.when(jnp.logical_not(last))
        def _():
            step(False)

        @pl.when(last)
        def _():
            step(True)

    w = hp * LANES
    qspec = pl.BlockSpec((tq, w), lambda h, s, ii_r, jj_r: (ii_r[s], h))
    kspec = pl.BlockSpec((tk, w), lambda h, s, ii_r, jj_r: (jj_r[s], h))
    return pl.pallas_call(
        body, name=name,
        grid_spec=pltpu.PrefetchScalarGridSpec(
            num_scalar_prefetch=2, grid=(hd // w, len(pairs)), in_specs=[qspec, kspec, kspec],
            out_specs=[qspec, qspec],
            scratch_shapes=[pltpu.VMEM((hp, tq, 1), F32), pltpu.VMEM((hp, tq, 1), F32),
                            pltpu.VMEM((tq, w), F32)]),
        out_shape=[jax.ShapeDtypeStruct((t, hd), BF16), jax.ShapeDtypeStruct((t, hd), F32)],
        compiler_params=_cparams())(jnp.asarray(ii), jnp.asarray(jj), q, k, v)


def _attn_bwd(q, k, v, o, lse, do, name):
    t, hd = q.shape
    hp = ATTN_BWD_HEADS_PER_STEP
    tq, _, tk = _attn_tiles(t)
    nq = t // tq
    pairs = [(i, j) for j in range(t // tk) for i in range((j * tk) // tq, nq)]
    ii = np.array([p[0] for p in pairs], np.int32)
    jj = np.array([p[1] for p in pairs], np.int32)
    scale = MLA_QK ** -0.5

    def body(jj_ref, ii_ref, q_ref, k_ref, v_ref, o_ref, lse_ref, do_ref, dq_ref, dk_ref, dv_ref, dk_scr, dv_scr):
        s_id = pl.program_id(1)
        i, j = ii_ref[s_id], jj_ref[s_id]

        @pl.when(s_id == 0)
        def _():
            dq_ref[...] = jnp.zeros_like(dq_ref)

        @pl.when(i == (j * tk) // tq)
        def _():
            dk_scr[...] = jnp.zeros_like(dk_scr)
            dv_scr[...] = jnp.zeros_like(dv_scr)

        rows = pl.ds(pl.multiple_of(i * tq, tq), tq)

        def step(masked):
            for hh in range(hp):
                sl = slice(hh * LANES, (hh + 1) * LANES)
                qv, kv, dov = q_ref[:, sl], k_ref[:, sl], do_ref[:, sl]
                s = _dot_nt(qv, kv) * (scale * LOG2E)
                if masked:
                    s = jnp.where(_causal_keep(tq, tk, i, j), s, NEG)
                p = jnp.exp2(s - jnp.max(lse_ref[:, sl], axis=1, keepdims=True))
                delta = jnp.sum(dov.astype(F32) * o_ref[:, sl].astype(F32), axis=1, keepdims=True)
                dv_scr[:, sl] += _dot_tn(p.astype(BF16), dov)
                dp = _dot_nt(dov, v_ref[:, sl])
                ds = (p * (dp - delta) * scale).astype(BF16)
                dk_scr[:, sl] += _dot_tn(ds, qv)
                dq_ref[rows, sl] += _dot(ds, kv)

        crosses = (j + 1) * tk - 1 > i * tq

        @pl.when(jnp.logical_not(crosses))
        def _():
            step(False)

        @pl.when(crosses)
        def _():
            step(True)

        @pl.when(i == nq - 1)
        def _():
            dk_ref[...] = dk_scr[...]
            dv_ref[...] = dv_scr[...]

    w = hp * LANES
    qspec = pl.BlockSpec((tq, w), lambda h, s, jj_r, ii_r: (ii_r[s], h))
    kspec = pl.BlockSpec((tk, w), lambda h, s, jj_r, ii_r: (jj_r[s], h))
    return pl.pallas_call(
        body, name=name,
        grid_spec=pltpu.PrefetchScalarGridSpec(
            num_scalar_prefetch=2, grid=(hd // w, len(pairs)),
            in_specs=[qspec, kspec, kspec, qspec, qspec, qspec],
            out_specs=[pl.BlockSpec((t, w), lambda h, s, jj_r, ii_r: (0, h)), kspec, kspec],
            scratch_shapes=[pltpu.VMEM((tk, w), F32), pltpu.VMEM((tk, w), F32)]),
        out_shape=[jax.ShapeDtypeStruct((t, hd), F32)] * 3,
        compiler_params=_cparams())(jnp.asarray(jj), jnp.asarray(ii), q, k, v, o, lse, do)


MEM_W = MEM_HEADS * LANES


def _mem_kv_fwd(mem, gmem, wkv, kg, name):
    m, d = mem.shape

    def body(mem_ref, g_ref, w_ref, kg_ref, k_ref, v_ref, mn_ref):
        xv = mem_ref[...]
        mn = (xv * _rstd(xv, d) * g_ref[...]).astype(BF16)
        mn_ref[...] = mn
        kvm = _dot(mn, w_ref[...])
        v_ref[...] = kvm[:, MEM_W:].astype(BF16)
        for h in range(MEM_HEADS):
            sl = slice(h * LANES, (h + 1) * LANES)
            kh = kvm[:, sl]
            k_ref[:, sl] = (kh * _rstd(kh, LANES) * kg_ref[...]).astype(BF16)

    full = lambda a: pl.BlockSpec(a.shape, lambda i: (0, 0))
    return pl.pallas_call(
        body, name=name, grid=(1,), in_specs=[full(mem), full(gmem), full(wkv), full(kg)],
        out_specs=[pl.BlockSpec((m, MEM_W), lambda i: (0, 0)), pl.BlockSpec((m, MEM_W), lambda i: (0, 0)),
                   pl.BlockSpec((m, d), lambda i: (0, 0))],
        out_shape=[jax.ShapeDtypeStruct((m, MEM_W), BF16), jax.ShapeDtypeStruct((m, MEM_W), BF16),
                   jax.ShapeDtypeStruct((m, d), BF16)],
        compiler_params=_cparams())(mem, gmem, wkv, kg)


def _mem_softmax(qn, kh):
    s = _dot_nt(qn, kh) * (LANES ** -0.5)
    e = jnp.exp(s - jnp.max(s, axis=1, keepdims=True))
    return e / jnp.sum(e, axis=1, keepdims=True)


def _mem_attn_fwd(zqm, qg, km, vm, name):
    t = zqm.shape[0]
    tm = _row_tile(t, 512)

    def body(q_ref, qg_ref, k_ref, v_ref, o_ref):
        for h in range(MEM_HEADS):
            sl = slice(h * LANES, (h + 1) * LANES)
            qh = q_ref[:, sl]
            qn = (qh * _rstd(qh, LANES) * qg_ref[...]).astype(BF16)
            p = _mem_softmax(qn, k_ref[:, sl])
            o_ref[:, sl] = _dot(p.astype(BF16), v_ref[:, sl]).astype(BF16)

    row = pl.BlockSpec((tm, MEM_W), lambda i: (i, 0))
    full = lambda a: pl.BlockSpec(a.shape, lambda i: (0, 0))
    return pl.pallas_call(
        body, name=name, grid=(t // tm,), in_specs=[row, full(qg), full(km), full(vm)], out_specs=row,
        out_shape=jax.ShapeDtypeStruct((t, MEM_W), BF16), compiler_params=_cparams())(zqm, qg, km, vm)


def _mem_attn_bwd(zqm, dyc, qg, km, vm, name):
    t = zqm.shape[0]
    m = km.shape[0]
    tm = _row_tile(t, 256)

    def body(q_ref, dy_ref, qg_ref, k_ref, v_ref, dz_ref, dk_ref, dv_ref, dqg_ref):
        @pl.when(pl.program_id(0) == 0)
        def _():
            dk_ref[...] = jnp.zeros_like(dk_ref)
            dv_ref[...] = jnp.zeros_like(dv_ref)
            dqg_ref[...] = jnp.zeros_like(dqg_ref)

        qgv = qg_ref[...]
        dqg_acc = jnp.zeros((tm, LANES), F32)
        for h in range(MEM_HEADS):
            sl = slice(h * LANES, (h + 1) * LANES)
            qh = q_ref[:, sl]
            r = _rstd(qh, LANES)
            qn = (qh * r * qgv).astype(BF16)
            kh = k_ref[:, sl]
            p = _mem_softmax(qn, kh)
            dov = dy_ref[:, sl]
            dv_ref[:, sl] += _dot_tn(p.astype(BF16), dov)
            dp = _dot_nt(dov, v_ref[:, sl])
            ds = (p * (dp - jnp.sum(dp * p, axis=1, keepdims=True)) * (LANES ** -0.5)).astype(BF16)
            dk_ref[:, sl] += _dot_tn(ds, qn)
            dqh, dgr = _rms_vjp(qh, r, qgv, _dot(ds, kh), LANES)
            dz_ref[:, sl] = dqh.astype(BF16)
            dqg_acc += dgr
        dqg_ref[...] += jnp.sum(dqg_acc, axis=0, keepdims=True)

    row = pl.BlockSpec((tm, MEM_W), lambda i: (i, 0))
    full = lambda a: pl.BlockSpec(a.shape, lambda i: (0, 0))
    acc = pl.BlockSpec((m, MEM_W), lambda i: (0, 0))
    return pl.pallas_call(
        body, name=name, grid=(t // tm,), in_specs=[row, row, full(qg), full(km), full(vm)],
        out_specs=[row, acc, acc, pl.BlockSpec((1, LANES), lambda i: (0, 0))],
        out_shape=[jax.ShapeDtypeStruct((t, MEM_W), BF16), jax.ShapeDtypeStruct((m, MEM_W), F32),
                   jax.ShapeDtypeStruct((m, MEM_W), F32), jax.ShapeDtypeStruct((1, LANES), F32)],
        compiler_params=_cparams())(zqm, dyc, qg, km, vm)


def _mem_kv_bwd(mem, gmem, wkv, kg, dkn, dvm, name):
    m, d = mem.shape

    def body(mem_ref, g_ref, w_ref, kg_ref, dk_ref, dv_ref, dw_ref, dkg_ref, dg_ref, dkv_scr):
        xv = mem_ref[...]
        r = _rstd(xv, d)
        mn = (xv * r * g_ref[...]).astype(BF16)
        kvm = _dot(mn, w_ref[...])
        dkv_scr[:, MEM_W:] = dv_ref[...].astype(BF16)
        dkg_acc = jnp.zeros((m, LANES), F32)
        for h in range(MEM_HEADS):
            sl = slice(h * LANES, (h + 1) * LANES)
            kh = kvm[:, sl]
            dkh, dgr = _rms_vjp(kh, _rstd(kh, LANES), kg_ref[...], dk_ref[:, sl], LANES)
            dkv_scr[:, sl] = dkh.astype(BF16)
            dkg_acc += dgr
        dkg_ref[...] = jnp.sum(dkg_acc, axis=0, keepdims=True)
        dkv = dkv_scr[...]
        dw_ref[...] = _dot_tn(mn, dkv)
        dmn = _dot_nt(dkv, w_ref[...])
        dg_ref[...] = jnp.sum(dmn * xv * r, axis=0, keepdims=True)

    full = lambda a: pl.BlockSpec(a.shape, lambda i: (0, 0))
    return pl.pallas_call(
        body, name=name, grid=(1,),
        in_specs=[full(mem), full(gmem), full(wkv), full(kg), full(dkn), full(dvm)],
        out_specs=[pl.BlockSpec((d, 2 * MEM_W), lambda i: (0, 0)), pl.BlockSpec((1, LANES), lambda i: (0, 0)),
                   pl.BlockSpec((1, d), lambda i: (0, 0))],
        out_shape=[jax.ShapeDtypeStruct((d, 2 * MEM_W), F32), jax.ShapeDtypeStruct((1, LANES), F32),
                   jax.ShapeDtypeStruct((1, d), F32)],
        scratch_shapes=[pltpu.VMEM((m, 2 * MEM_W), BF16)],
        compiler_params=_cparams())(mem, gmem, wkv, kg, dkn, dvm)


def _merge_fwd(x1, ya, yb, yc, zg, bg, wa, wb, wc, wo, name):
    t, d = x1.shape
    tm = _row_tile(t, 256)

    def body(x_ref, ya_ref, yb_ref, yc_ref, zg_ref, bg_ref, wa_ref, wb_ref, wc_ref, wo_ref,
             x2_ref, mg_ref, pa_ref, pb_ref, pc_ref):
        merged = None
        for k, (y_ref, w_ref, p_ref) in enumerate(
                ((ya_ref, wa_ref, pa_ref), (yb_ref, wb_ref, pb_ref), (yc_ref, wc_ref, pc_ref))):
            sl = slice(k * d, (k + 1) * d)
            pr = _dot(y_ref[...], w_ref[...])
            p_ref[...] = pr.astype(BF16)
            term = jax.nn.sigmoid(zg_ref[:, sl] + bg_ref[:, sl]) * pr
            merged = term if merged is None else merged + term
        mb = merged.astype(BF16)
        mg_ref[...] = mb
        x2_ref[...] = x_ref[...] + _dot(mb, wo_ref[...])

    row = lambda n: pl.BlockSpec((tm, n), lambda i: (i, 0))
    full = lambda a: pl.BlockSpec(a.shape, lambda i: (0, 0))
    return pl.pallas_call(
        body, name=name, grid=(t // tm,),
        in_specs=[row(d), row(ya.shape[1]), row(yb.shape[1]), row(yc.shape[1]), row(3 * d), full(bg),
                  full(wa), full(wb), full(wc), full(wo)],
        out_specs=[row(d)] * 5,
        out_shape=[jax.ShapeDtypeStruct((t, d), F32)] + [jax.ShapeDtypeStruct((t, d), BF16)] * 4,
        compiler_params=_cparams())(x1, ya, yb, yc, zg, bg, wa, wb, wc, wo)


def _merge_bwd(dx2, pa, pb, pc, zg, bg, wa, wb, wc, wo, name, ex=None):
    t, d = dx2.shape
    tm = _row_tile(t, 256)

    def body(dx_ref, pa_ref, pb_ref, pc_ref, zg_ref, bg_ref, wa_ref, wb_ref, wc_ref, wo_ref,
             dpa_ref, dpb_ref, dpc_ref, dzg_ref, dbg_ref, dya_ref, dyb_ref, dyc_ref):
        @pl.when(pl.program_id(0) == 0)
        def _():
            dbg_ref[...] = jnp.zeros_like(dbg_ref)

        dm = _dot_nt(dx_ref[...].astype(BF16), wo_ref[...])
        for k, (p_ref, w_ref, dp_ref, dy_ref) in enumerate(
                ((pa_ref, wa_ref, dpa_ref, dya_ref), (pb_ref, wb_ref, dpb_ref, dyb_ref),
                 (pc_ref, wc_ref, dpc_ref, dyc_ref))):
            sl = slice(k * d, (k + 1) * d)
            gate = jax.nn.sigmoid(zg_ref[:, sl] + bg_ref[:, sl])
            dpr = (dm * gate).astype(BF16)
            dp_ref[...] = dpr
            dzg = dm * p_ref[...].astype(F32) * gate * (1.0 - gate)
            dzg_ref[:, sl] = dzg.astype(BF16)
            dbg_ref[:, sl] += jnp.sum(dzg, axis=0, keepdims=True)
            dy_ref[...] = _dot_nt(dpr, w_ref[...]).astype(dy_ref.dtype)

    row = lambda n: pl.BlockSpec((tm, n), lambda i: (i, 0))
    full = lambda a: pl.BlockSpec(a.shape, lambda i: (0, 0))
    na, nb, nc = wa.shape[0], wb.shape[0], wc.shape[0]
    return _call_with_exchange(
        ex, body, name, (t // tm,),
        [row(d), row(d), row(d), row(d), row(3 * d), full(bg), full(wa), full(wb), full(wc), full(wo)],
        [row(d), row(d), row(d), row(3 * d), pl.BlockSpec((1, 3 * d), lambda i: (0, 0)), row(na), row(nb), row(nc)],
        [jax.ShapeDtypeStruct((t, d), BF16)] * 3
        + [jax.ShapeDtypeStruct((t, 3 * d), BF16), jax.ShapeDtypeStruct((1, 3 * d), F32),
           jax.ShapeDtypeStruct((t, na), F32), jax.ShapeDtypeStruct((t, nb), BF16),
           jax.ShapeDtypeStruct((t, nc), BF16)],
        [], (dx2, pa, pb, pc, zg, bg, wa, wb, wc, wo))


def _adamw_math(w, g, m, v):
    bc1 = 1.0 - ADAM_B1 ** ADAM_STEP
    bc2 = 1.0 - ADAM_B2 ** ADAM_STEP
    nm = ADAM_B1 * m + (1.0 - ADAM_B1) * g
    nv = ADAM_B2 * v + (1.0 - ADAM_B2) * (g * g)
    delta = -ADAM_LR * ((nm / bc1) / (jnp.sqrt(nv / bc2) + ADAM_EPS) + ADAM_WD * w)
    return delta, nm, nv


def _div_tile(n, cap, mult):
    best = None
    for cand in range(mult, min(n, cap) + 1, mult):
        if n % cand == 0:
            best = cand
    assert best is not None, (n, cap, mult)
    return best


def _adamw(w, g, m, v, name):
    rows, cols = w.shape
    tr = rows if rows * cols <= 256 * 1024 else _div_tile(rows, 256, 8)

    def body(w_ref, g_ref, m_ref, v_ref, d_ref, nm_ref, nv_ref):
        d_ref[...], nm_ref[...], nv_ref[...] = _adamw_math(w_ref[...], g_ref[...], m_ref[...], v_ref[...])

    blk = pl.BlockSpec((tr, cols), lambda i: (i, 0))
    return pl.pallas_call(
        body, name=name, grid=(rows // tr,), in_specs=[blk] * 4, out_specs=[blk] * 3,
        out_shape=[jax.ShapeDtypeStruct((rows, cols), F32)] * 3, compiler_params=_cparams())(w, g, m, v)


def _adamw_slots(w, slots, m, v, name):
    _, hr, cols = w.shape
    tr = _div_tile(hr, 128, 16)

    def body(w_ref, s_ref, m_ref, v_ref, g_ref, d_ref, nm_ref, nv_ref):
        g = s_ref[0, 0].astype(F32)
        for k in range(1, N_CHIPS):
            g = g + s_ref[0, k].astype(F32)
        g_ref[0] = g
        d_ref[0], nm_ref[0], nv_ref[0] = _adamw_math(w_ref[0], g, m_ref[0], v_ref[0])

    blk = pl.BlockSpec((1, tr, cols), lambda h, i: (h, i, 0))
    return pl.pallas_call(
        body, name=name, grid=(2, hr // tr),
        in_specs=[blk, pl.BlockSpec((1, N_CHIPS, tr, cols), lambda h, i: (h, 0, i, 0)), blk, blk],
        out_specs=[blk] * 4, out_shape=[jax.ShapeDtypeStruct((2, hr, cols), F32)] * 4,
        compiler_params=_cparams())(w, slots, m, v)


ANY = pl.BlockSpec(memory_space=pl.ANY)


def _place():
    x, y, c = lax.axis_index("x"), lax.axis_index("y"), lax.axis_index("c")
    other_chips = [(1 - x, y), (x, 1 - y), (1 - x, 1 - y)]
    return x, y, c, other_chips


def _remote(src, dst, send_sem, recv_sem, to):
    return pltpu.make_async_remote_copy(src_ref=src, dst_ref=dst, send_sem=send_sem, recv_sem=recv_sem,
                                        device_id=to, device_id_type=MESH)


def _gather_exchange(shards):
    nw = len(shards)

    def build(s_refs, g_refs, sems):
        send_sems, recv_sems, local_sems = sems
        x, y, c, chips = _place()
        me = 2 * x + y
        sibling = (x, y, 1 - c)
        mine = [pltpu.make_async_copy(s_refs[w], g_refs[w].at[me], local_sems.at[w]) for w in range(nw)]
        first = [_remote(s_refs[w].at[c], g_refs[w].at[me, c], send_sems.at[k, w], recv_sems.at[k, w], (cx, cy, c))
                 for k, (cx, cy) in enumerate(chips) for w in range(nw)]

        def start():
            for cp in mine + first:
                cp.start()

        def finish():
            passed = []
            for k, (cx, cy) in enumerate(chips):
                for w in range(nw):
                    slab = g_refs[w].at[2 * cx + cy, c]
                    _remote(slab, slab, send_sems.at[k, w], recv_sems.at[k, w], (cx, cy, c)).wait_recv()
                    fwd = _remote(slab, slab, send_sems.at[3 + k, w], recv_sems.at[3 + k, w], sibling)
                    fwd.start()
                    passed.append(fwd)
            for k, (cx, cy) in enumerate(chips):
                for w in range(nw):
                    slab = g_refs[w].at[2 * cx + cy, 1 - c]
                    _remote(slab, slab, send_sems.at[3 + k, w], recv_sems.at[3 + k, w], sibling).wait_recv()
            for cp in first + passed:
                cp.wait_send()
            for cp in mine:
                cp.wait()

        return start, finish

    return _Exchange(list(shards), [jax.ShapeDtypeStruct((N_CHIPS,) + s.shape, BF16) for s in shards],
                     [pltpu.SemaphoreType.DMA((6, nw)), pltpu.SemaphoreType.DMA((6, nw)),
                      pltpu.SemaphoreType.DMA((nw,))], build)


def _swap_halves(grads, name):
    nw = len(grads)

    def body(*refs):
        g_refs, sib_refs = refs[:nw], refs[nw:2 * nw]
        send_sems, recv_sems = refs[2 * nw:]
        x, y, c, _ = _place()
        copies = [_remote(g_refs[w].at[s, 1 - c], sib_refs[w].at[s], send_sems.at[s, w], recv_sems.at[s, w],
                          (x, y, 1 - c)) for w in range(nw) for s in range(N_CHIPS)]
        for cp in copies:
            cp.start()
        for cp in copies:
            cp.wait_recv()
        for cp in copies:
            cp.wait_send()

    return pl.pallas_call(
        body, name=name, in_specs=[ANY] * nw, out_specs=[ANY] * nw,
        out_shape=[jax.ShapeDtypeStruct((N_CHIPS,) + g.shape[2:], BF16) for g in grads],
        scratch_shapes=[pltpu.SemaphoreType.DMA((N_CHIPS, nw)), pltpu.SemaphoreType.DMA((N_CHIPS, nw))],
    )(*grads)


def _pair_sum(grad, sib, core, name):
    nchip, _, hr, cols = grad.shape
    tr = _div_tile(hr, 256, 16)

    def body(core_ref, a_ref, b_ref, o_ref):
        o_ref[...] = (a_ref[0].astype(F32) + b_ref[...].astype(F32)).astype(BF16)

    return pl.pallas_call(
        body, name=name,
        grid_spec=pltpu.PrefetchScalarGridSpec(
            num_scalar_prefetch=1, grid=(nchip, hr // tr),
            in_specs=[pl.BlockSpec((1, 1, tr, cols), lambda s, i, core_r: (s, core_r[0], i, 0)),
                      pl.BlockSpec((1, tr, cols), lambda s, i, core_r: (s, i, 0))],
            out_specs=pl.BlockSpec((1, tr, cols), lambda s, i, core_r: (s, i, 0))),
        out_shape=jax.ShapeDtypeStruct((nchip, hr, cols), BF16), compiler_params=_cparams())(core, grad, sib)


def _pair_sum_exchange(sums):
    nw = len(sums)

    def build(p_refs, o_refs, sems):
        send_sems, recv_sems, local_sems = sems
        x, y, c, chips = _place()
        me = 2 * x + y
        sibling = (x, y, 1 - c)
        mine = [pltpu.make_async_copy(p_refs[w].at[me], o_refs[w].at[c, 3], local_sems.at[w]) for w in range(nw)]
        first = [_remote(p_refs[w].at[2 * cx + cy], o_refs[w].at[c, k], send_sems.at[k, w], recv_sems.at[k, w],
                         (cx, cy, c)) for k, (cx, cy) in enumerate(chips) for w in range(nw)]

        def start():
            for cp in mine + first:
                cp.start()

        def finish():
            passed = []
            for k in range(N_CHIPS):
                for w in range(nw):
                    slab = o_refs[w].at[c, k]
                    if k < 3:
                        first[k * nw + w].wait_recv()
                    else:
                        mine[w].wait()
                    fwd = _remote(slab, slab, send_sems.at[3 + k, w], recv_sems.at[3 + k, w], sibling)
                    fwd.start()
                    passed.append(fwd)
            for k in range(N_CHIPS):
                for w in range(nw):
                    slab = o_refs[w].at[1 - c, k]
                    _remote(slab, slab, send_sems.at[3 + k, w], recv_sems.at[3 + k, w], sibling).wait_recv()
            for cp in first + passed:
                cp.wait_send()

        return start, finish

    return _Exchange(list(sums), [jax.ShapeDtypeStruct((2,) + p.shape, BF16) for p in sums],
                     [pltpu.SemaphoreType.DMA((7, nw)), pltpu.SemaphoreType.DMA((7, nw)),
                      pltpu.SemaphoreType.DMA((nw,))], build)


def _allreduce_small(vec):
    m_per, n = vec.shape

    def body(x_ref, out_ref, gath_ref, send_sems, recv_sems, local_sem):
        x, y, c, chips = _place()
        me, sibling = (x, y, c), (x, y, 1 - c)

        def rows(px, py, pc):
            return gath_ref.at[pl.ds((4 * px + 2 * py + pc) * m_per, m_per), :]

        def copy(k, block, to, src=None):
            return pltpu.make_async_remote_copy(
                src_ref=rows(*block) if src is None else src, dst_ref=rows(*block),
                send_sem=send_sems.at[k], recv_sem=recv_sems.at[k], device_id=to, device_id_type=MESH)

        mine = pltpu.make_async_copy(x_ref, rows(*me), local_sem)
        mine.start()
        first = [copy(0, me, sibling, src=x_ref)]
        first += [copy(1 + j, me, (*chip, c), src=x_ref) for j, chip in enumerate(chips)]
        for cp in first:
            cp.start()
        passed = [copy(4 + j, (*chip, c), sibling) for j, chip in enumerate(chips)]
        for j, chip in enumerate(chips):
            copy(1 + j, (*chip, c), me).wait_recv()
            passed[j].start()
        copy(0, sibling, me).wait_recv()
        for j, chip in enumerate(chips):
            copy(4 + j, (*chip, 1 - c), me).wait_recv()
        for cp in first + passed:
            cp.wait_send()
        mine.wait()
        acc = gath_ref[pl.ds(0, m_per), :]
        for k in range(1, N_DEV):
            acc = acc + gath_ref[pl.ds(k * m_per, m_per), :]
        out_ref[...] = acc

    vm = pl.BlockSpec(memory_space=pltpu.VMEM)
    return pl.pallas_call(
        body, name="allreduce_small", in_specs=[vm], out_specs=vm,
        out_shape=jax.ShapeDtypeStruct((m_per, n), F32),
        scratch_shapes=[pltpu.VMEM((N_DEV * m_per, n), F32), pltpu.SemaphoreType.DMA((7,)),
                        pltpu.SemaphoreType.DMA((7,)), pltpu.SemaphoreType.DMA],
    )(vec)


def _pack_small(vals):
    flat = jnp.concatenate([vals[name].reshape(-1).astype(F32) for name, _ in SMALL])
    flat = jnp.pad(flat, (0, SMALL_ROWS * LANES - flat.shape[0]))
    return flat.reshape(SMALL_ROWS, LANES)


def _unpack_small(packed):
    flat = packed.reshape(-1)
    out, off = {}, 0
    for name, shape in SMALL:
        n = int(np.prod(shape))
        out[name] = flat[off:off + n].reshape(shape)
        off += n
    return out


def _head_pad_cols(w, heads, real):
    k = w.shape[0]
    return jnp.pad(w.reshape(k, heads, real), ((0, 0), (0, 0), (0, LANES - real))).reshape(k, heads * LANES)


def _rope_tables(positions):
    half = MLA_ROPE // 2
    inv = ROPE_BASE ** (-jnp.arange(half, dtype=F32) / half)
    ang = positions.astype(F32)[:, None] * inv
    cos, sin = jnp.cos(ang), jnp.sin(ang)
    t = positions.shape[0]
    z = lambda n: jnp.zeros((t, n), F32)
    rc = jnp.concatenate([jnp.ones((t, MLA_NOPE), F32), cos, cos, z(LANES - MLA_QK)], axis=1)
    rs1 = jnp.concatenate([z(MLA_NOPE), -sin, z(LANES - MLA_NOPE - half)], axis=1)
    rs2 = jnp.concatenate([z(MLA_NOPE + half), sin, z(LANES - MLA_QK)], axis=1)
    return rc, rs1, rs2


FFN1_WEIGHTS = ("ffn1_w_gu", "ffn1_w_down")
FFN2_WEIGHTS = ("ffn2_w_gu", "ffn2_w_down")
MIXER_WEIGHTS = tuple(n for n, *_ in SHARDED if n not in FFN1_WEIGHTS + FFN2_WEIGHTS)
SHARD_SHAPE = {n: (r, c, kind) for n, r, c, kind in SHARDED}


def _from_blocks(name, gathered):
    r, c, kind = SHARD_SHAPE[name]
    blk = gathered.reshape(N_CHIPS, r, c)
    return blk, (blk.transpose(1, 0, 2).reshape(r, N_CHIPS * c) if kind == "col" else blk.reshape(N_CHIPS * r, c))


def _grad_pair_sums(names, gw, core, tag):
    by_owner = []
    for name in names:
        r, c, kind = SHARD_SHAPE[name]
        blk = gw[name].reshape(r, N_CHIPS, c).transpose(1, 0, 2) if kind == "col" else gw[name].reshape(N_CHIPS, r, c)
        by_owner.append(blk.astype(BF16).reshape(N_CHIPS, 2, r // 2, c))
    received = _swap_halves(by_owner, "grad_swap_" + tag)
    return [_pair_sum(g, s, core, "pair_sum_" + n) for g, s, n in zip(by_owner, received, names)]


def _device_step(x, mem, positions, tgt, small, shards, core):
    d = D_MODEL
    g_ffn1, g_mix, g_ffn2 = small["ffn1_norm"], small["mix_norm"], small["ffn2_norm"]
    big = {}
    for name, g in zip(FFN1_WEIGHTS, _run_exchange(_gather_exchange([shards[n] for n in FFN1_WEIGHTS]), "gather_ffn1")):
        big[name + "#blocks"], big[name] = _from_blocks(name, g)
    wgu1, wd1 = big["ffn1_w_gu#blocks"], big["ffn1_w_down"].reshape(2, FF_TILE, d)
    x1, gpre1, upre1, *rest = _ffn_fwd(x, g_ffn1, wgu1, wd1, "ffn1_fwd",
                                       ex=_gather_exchange([shards[n] for n in MIXER_WEIGHTS]))
    for name, g in zip(MIXER_WEIGHTS, rest):
        big[name + "#blocks"], big[name] = _from_blocks(name, g)
    w_in = big["w_in"]
    w_uv_, w_cq, w_ckv = w_in[:, :COL_CQ], w_in[:, COL_CQ:COL_CKV], w_in[:, COL_CKV:COL_KR]
    w_kr = jnp.pad(w_in[:, COL_KR:COL_QM], ((0, 0), (MLA_NOPE, LANES - MLA_QK)))
    w_qm, w_g = w_in[:, COL_QM:COL_GATE], w_in[:, COL_GATE:]
    segs = (w_uv_, w_cq, w_ckv, w_kr, w_qm, w_g)
    wuq = _head_pad_cols(big["mla_w_uq"], MLA_HEADS, MLA_QK)
    ukv = big["mla_w_ukv"].reshape(MLA_KV_RANK, MLA_HEADS, 2, MLA_NOPE)
    wuk = _head_pad_cols(ukv[:, :, 0].reshape(MLA_KV_RANK, -1), MLA_HEADS, MLA_NOPE)
    wuv = _head_pad_cols(ukv[:, :, 1].reshape(MLA_KV_RANK, -1), MLA_HEADS, MLA_NOPE)
    wkv = big["mem_w_kv"]
    wa, wc, wo = big["w_branch_a"], big["w_branch_c"], big["w_out"]
    wb = jnp.pad(big["w_branch_b"].reshape(MLA_HEADS, MLA_NOPE, d),
                 ((0, 0), (0, LANES - MLA_NOPE), (0, 0))).reshape(MLA_HEADS * LANES, d)
    qg = jnp.pad(small["mla_q_norm"], ((0, 0), (0, LANES - MLA_QK)))
    kg = jnp.pad(small["mla_k_norm"], ((0, 0), (0, LANES - MLA_QK)))
    causal = jnp.tril(jnp.ones((CHUNK, CHUNK), bool))
    wt_f = jnp.where(causal[None], small["sg_w"][0], 0.0)
    wt, wt_t = wt_f.astype(BF16), wt_f.transpose(0, 2, 1).astype(BF16)
    bias_l = jnp.repeat(small["sg_b"][0].T, 64, axis=1)
    rc, rs1, rs2 = _rope_tables(positions)

    h = _rms_fwd(x1, g_mix, "mix_norm_fwd")
    zuv, zcq, zckv, zkr, zqm, zg = [
        _mm([(h, w)], F32, "in_proj_%d" % k) for k, w in enumerate(segs)]
    ya = _sgu_fwd(zuv, small["sg_ln_g"], small["sg_ln_b"], wt, bias_l, "sgu_fwd")
    q, k, v, cqn, ckvn, *rest = _mla_prep_fwd(zcq, zckv, zkr, small["mla_cq_norm"], small["mla_ckv_norm"], qg, kg,
                                              wuq, wuk, wuv, rc, rs1, rs2, "mla_prep_fwd",
                                              ex=_gather_exchange([shards[n] for n in FFN2_WEIGHTS]))
    for name, g in zip(FFN2_WEIGHTS, rest):
        big[name + "#blocks"], big[name] = _from_blocks(name, g)
    wgu2, wd2 = big["ffn2_w_gu#blocks"], big["ffn2_w_down"].reshape(2, FF_TILE, d)
    yb, lse = _attn_fwd(q, k, v, "mla_attn_fwd")
    km, vm, memn = _mem_kv_fwd(mem, small["mem_norm"], wkv, small["mem_k_norm"], "mem_kv_fwd")
    yc = _mem_attn_fwd(zqm, small["mem_q_norm"], km, vm, "mem_attn_fwd")
    x2, merged, pa, pb, pc = _merge_fwd(x1, ya, yb, yc, zg, small["b_gate"], wa, wb, wc, wo, "merge_fwd")
    x3, gpre2, upre2 = _ffn_fwd(x2, g_ffn2, wgu2, wd2, "ffn2_fwd")
    dy, loss_row = _loss_head(x3, tgt, "loss_head")

    gw, gs, slots = {}, {}, {}

    def ffn_grads(prefix, xin, gain, dyin, gpre, upre, wgu, wd, ex=None, ex_names=()):
        dx, dgain, xn, dgt, dup, act, *got = _ffn_bwd(xin, gain, dyin, gpre, upre, wgu, wd, prefix + "_bwd", ex=ex)
        slots.update(zip(ex_names, got))
        gw[prefix + "_w_gu"] = jnp.concatenate(
            [_mm_tn(xn, dgt, prefix + "_dwg"), _mm_tn(xn, dup, prefix + "_dwu")], axis=1)
        gw[prefix + "_w_down"] = _mm_tn(act, dyin, prefix + "_dwd", scale=0.5)
        gs[prefix + "_norm"] = dgain
        return dx

    dx2 = ffn_grads("ffn2", x2, g_ffn2, dy, gpre2, upre2, wgu2, wd2)
    ffn2_sums = _pair_sum_exchange(_grad_pair_sums(FFN2_WEIGHTS, gw, core, "ffn2"))
    dpa, dpb, dpc, dzg, dbg, dya, dyb, dyc, *got = _merge_bwd(dx2, pa, pb, pc, zg, small["b_gate"], wa, wb, wc, wo,
                                                              "merge_bwd", ex=ffn2_sums)
    slots.update(zip(FFN2_WEIGHTS, got))
    gs["b_gate"] = dbg
    gw["w_out"] = _mm_tn(merged, dx2, "dw_out")
    gw["w_branch_a"] = _mm_tn(ya, dpa, "dw_branch_a")
    gw["w_branch_b"] = _mm_tn(yb, dpb, "dw_branch_b").reshape(MLA_HEADS, LANES, d)[:, :MLA_NOPE].reshape(-1, d)
    gw["w_branch_c"] = _mm_tn(yc, dpc, "dw_branch_c")

    dzuv, dwt, dbl, dlg, dlb = _sgu_bwd(zuv, dya, small["sg_ln_g"], small["sg_ln_b"], wt, wt_t, bias_l, "sgu_bwd")
    gs["sg_w"], gs["sg_b"] = dwt[None], dbl[:, :SG_GROUPS].T[None]
    gs["sg_ln_g"], gs["sg_ln_b"] = dlg, dlb

    dq, dk, dv = _attn_bwd(q, k, v, yb, lse, dyb, "mla_attn_bwd")
    dzcq, dzckv, dzkr, dql, dkl, dgcq, dgckv, dqg, dkg = _mla_prep_bwd(
        zcq, zckv, zkr, small["mla_cq_norm"], small["mla_ckv_norm"], qg, kg, wuq, wuk, wuv, rc, rs1, rs2,
        dq, dk, dv, "mla_prep_bwd")
    gs["mla_cq_norm"], gs["mla_ckv_norm"] = dgcq, dgckv
    gs["mla_q_norm"], gs["mla_k_norm"] = dqg[:, :MLA_QK], dkg[:, :MLA_QK]
    gw["mla_w_uq"] = _mm_tn(cqn, dql, "dw_uq").reshape(MLA_Q_RANK, MLA_HEADS, LANES)[:, :, :MLA_QK].reshape(
        MLA_Q_RANK, -1)
    dwuk = _mm_tn(ckvn, dkl, "dw_uk").reshape(MLA_KV_RANK, MLA_HEADS, LANES)[:, :, :MLA_NOPE]
    dwuv = _mm_tn(ckvn, dv, "dw_uv").reshape(MLA_KV_RANK, MLA_HEADS, LANES)[:, :, :MLA_NOPE]
    gw["mla_w_ukv"] = jnp.concatenate([dwuk, dwuv], axis=2).reshape(MLA_KV_RANK, -1)

    dzqm, dkn, dvm, dmqg = _mem_attn_bwd(zqm, dyc, small["mem_q_norm"], km, vm, "mem_attn_bwd")
    gs["mem_q_norm"] = dmqg
    gw["mem_w_kv"], gs["mem_k_norm"], gs["mem_norm"] = _mem_kv_bwd(
        mem, small["mem_norm"], wkv, small["mem_k_norm"], dkn, dvm, "mem_kv_bwd")

    dzs = (dzuv, dzcq, dzckv, dzkr, dzqm, dzg)
    dh = _mm([(dz, w.T) for dz, w in zip(dzs, segs)], F32, "in_proj_bwd")
    dws = [_mm_tn(h, dz, "dw_in_%d" % k) for k, dz in enumerate(dzs)]
    dws[3] = dws[3][:, MLA_NOPE:MLA_QK]
    gw["w_in"] = jnp.concatenate(dws, axis=1)
    dx1, gs["mix_norm"] = _rms_bwd(x1, g_mix, dh, dx2, "mix_norm_bwd")
    mixer_sums = _pair_sum_exchange(_grad_pair_sums(MIXER_WEIGHTS, gw, core, "mixer"))
    dx = ffn_grads("ffn1", x, g_ffn1, dx1, gpre1, upre1, wgu1, wd1, ex=mixer_sums, ex_names=MIXER_WEIGHTS)
    ffn1_sums = _pair_sum_exchange(_grad_pair_sums(FFN1_WEIGHTS, gw, core, "ffn1"))
    slots.update(zip(FFN1_WEIGHTS, _run_exchange(ffn1_sums, "grad_exchange_ffn1")))
    return loss_row, dx, slots, gs


def kernel(x, mem, positions, ffn1_norm, ffn1_w_gu, ffn1_w_down, mix_norm, w_in, b_gate, sg_ln_g, sg_ln_b, sg_w, sg_b, mla_cq_norm, mla_w_uq, mla_ckv_norm, mla_w_ukv, mla_q_norm, mla_k_norm, mem_norm, mem_w_kv, mem_q_norm, mem_k_norm, w_branch_a, w_branch_b, w_branch_c, w_out, ffn2_norm, ffn2_w_gu, ffn2_w_down, loss_target, m_ffn1_norm, m_ffn1_w_gu, m_ffn1_w_down, m_mix_norm, m_w_in, m_b_gate, m_sg_ln_g, m_sg_ln_b, m_sg_w, m_sg_b, m_mla_cq_norm, m_mla_w_uq, m_mla_ckv_norm, m_mla_w_ukv, m_mla_q_norm, m_mla_k_norm, m_mem_norm, m_mem_w_kv, m_mem_q_norm, m_mem_k_norm, m_w_branch_a, m_w_branch_b, m_w_branch_c, m_w_out, m_ffn2_norm, m_ffn2_w_gu, m_ffn2_w_down, v_ffn1_norm, v_ffn1_w_gu, v_ffn1_w_down, v_mix_norm, v_w_in, v_b_gate, v_sg_ln_g, v_sg_ln_b, v_sg_w, v_sg_b, v_mla_cq_norm, v_mla_w_uq, v_mla_ckv_norm, v_mla_w_ukv, v_mla_q_norm, v_mla_k_norm, v_mem_norm, v_mem_w_kv, v_mem_q_norm, v_mem_k_norm, v_w_branch_a, v_w_branch_b, v_w_branch_c, v_w_out, v_ffn2_norm, v_ffn2_w_gu, v_ffn2_w_down):
    args = dict(locals())
    weights = {n: args[n] for n in WEIGHT_ORDER}
    mom_m = {n: args["m_" + n] for n in WEIGHT_ORDER}
    mom_v = {n: args["v_" + n] for n in WEIGHT_ORDER}
    small = {n: weights[n] for n, _ in SMALL}
    halves = lambda a, r, c: a.reshape(2, r // 2, c)

    shards = {n: halves(weights[n][0].astype(BF16), r, c) for n, r, c, _ in SHARDED}
    core = lax.axis_index("c").astype(jnp.int32).reshape(1)
    loss_row, dx, slots, gs = _device_step(x[0], mem[0], positions[0], loss_target[0], small, shards, core)
    loss = lax.psum(loss_row[0, 0], ("x", "y", "c"))
    small_grads = _unpack_small(_allreduce_small(_pack_small(gs)))

    grads, deltas, new_m, new_v = {}, {}, {}, {}
    for name, r, c, _ in SHARDED:
        outs = _adamw_slots(halves(weights[name][0], r, c), slots[name], halves(mom_m[name][0], r, c),
                            halves(mom_v[name][0], r, c), "adamw_" + name)
        shape = weights[name].shape
        grads[name], deltas[name], new_m[name], new_v[name] = [o.reshape(shape) for o in outs]
    dlt, nm, nv = _adamw(_pack_small(small), _pack_small(small_grads), _pack_small({n: mom_m[n] for n, _ in SMALL}),
                         _pack_small({n: mom_v[n] for n, _ in SMALL}), "adamw_small")
    for name, _ in SMALL:
        grads[name] = small_grads[name]
    deltas.update(_unpack_small(dlt))
    new_m.update(_unpack_small(nm))
    new_v.update(_unpack_small(nv))

    return (loss, dx[None], *[grads[n] for n in WEIGHT_ORDER], *[deltas[n] for n in WEIGHT_ORDER],
            *[new_m[n] for n in WEIGHT_ORDER], *[new_v[n] for n in WEIGHT_ORDER])
```

```python
import functools
from typing import Callable, NamedTuple

import numpy as np
import jax
import jax.numpy as jnp
from jax import lax
from jax.experimental import pallas as pl
from jax.experimental.pallas import tpu as pltpu

F32 = jnp.float32
BF16 = jnp.bfloat16

D_MODEL = 1024
D_FF = 2816
FF_TILE = 1408
SG_WIDTH = 512
SG_GROUPS = 8
CHUNK = 128
MLA_HEADS = 8
MLA_QK = 96
MLA_NOPE = 64
MLA_ROPE = 32
MLA_Q_RANK = 384
MLA_KV_RANK = 256
MEM_HEADS = 4
MEM_LEN = 256
LANES = 128
EPS = 1e-6
NEG = -1e30
ROPE_BASE = 10000.0
N_CHIPS = 4
N_DEV = 8

ADAM_LR = 0.001
ADAM_B1 = 0.9
ADAM_B2 = 0.999
ADAM_EPS = 1e-08
ADAM_WD = 0.01
ADAM_STEP = 10

COL_V = 512
COL_CQ = 1024
COL_CKV = 1408
COL_KR = 1664
COL_QM = 1696
COL_GATE = 2208
IN_COLS = 5280

VMEM_LIMIT_BYTES = 56 * 1024 * 1024
INV_SQRT2 = 0.7071067811865476
INV_SQRT_2PI = 0.3989422804014327
LOG2E = 1.4426950408889634

SHARDED = (
    ("ffn1_w_gu", 1024, 1408, "col"),
    ("ffn1_w_down", 704, 1024, "row"),
    ("w_in", 1024, 1320, "col"),
    ("mla_w_uq", 384, 192, "col"),
    ("mla_w_ukv", 256, 256, "col"),
    ("mem_w_kv", 256, 1024, "row"),
    ("w_branch_a", 512, 256, "col"),
    ("w_branch_b", 512, 256, "col"),
    ("w_branch_c", 512, 256, "col"),
    ("w_out", 256, 1024, "row"),
    ("ffn2_w_gu", 1024, 1408, "col"),
    ("ffn2_w_down", 704, 1024, "row"),
)
SMALL = (
    ("ffn1_norm", (1, 1024)), ("mix_norm", (1, 1024)), ("b_gate", (1, 3072)),
    ("sg_ln_g", (1, 512)), ("sg_ln_b", (1, 512)), ("sg_w", (1, 8, 128, 128)),
    ("sg_b", (1, 8, 128)), ("mla_cq_norm", (1, 384)), ("mla_ckv_norm", (1, 256)),
    ("mla_q_norm", (1, 96)), ("mla_k_norm", (1, 96)), ("mem_norm", (1, 1024)),
    ("mem_q_norm", (1, 128)), ("mem_k_norm", (1, 128)), ("ffn2_norm", (1, 1024)),
)
WEIGHT_ORDER = (
    "ffn1_norm", "ffn1_w_gu", "ffn1_w_down", "mix_norm", "w_in", "b_gate", "sg_ln_g", "sg_ln_b",
    "sg_w", "sg_b", "mla_cq_norm", "mla_w_uq", "mla_ckv_norm", "mla_w_ukv", "mla_q_norm",
    "mla_k_norm", "mem_norm", "mem_w_kv", "mem_q_norm", "mem_k_norm", "w_branch_a", "w_branch_b",
    "w_branch_c", "w_out", "ffn2_norm", "ffn2_w_gu", "ffn2_w_down",
)

_N_SMALL = sum(int(np.prod(s)) for _, s in SMALL)
SMALL_ROWS = -(-_N_SMALL // (LANES * 8)) * 8

MESH = pl.DeviceIdType.MESH


def _cparams():
    return pltpu.CompilerParams(vmem_limit_bytes=VMEM_LIMIT_BYTES)


def _dot(a, b):
    return jnp.dot(a, b, preferred_element_type=F32)


def _dot_nt(a, b):
    return lax.dot_general(a, b, (((1,), (1,)), ((), ())), preferred_element_type=F32)


def _dot_tn(a, b):
    return lax.dot_general(a, b, (((0,), (0,)), ((), ())), preferred_element_type=F32)


def _gelu(x):
    return 0.5 * x * (1.0 + lax.erf(x * INV_SQRT2))


def _gelu_grad(x):
    return 0.5 * (1.0 + lax.erf(x * INV_SQRT2)) + x * jnp.exp(-0.5 * x * x) * INV_SQRT_2PI


def _rstd(x, n):
    return lax.rsqrt(jnp.sum(x * x, axis=-1, keepdims=True) * (1.0 / n) + EPS)


def _rms_vjp(x, r, g, dy, n):
    dxh = dy * g
    dx = r * dxh - x * (r * r * r) * (jnp.sum(dxh * x, axis=-1, keepdims=True) * (1.0 / n))
    return dx, dy * x * r


def _row_tile(t, want):
    return min(t, want)


def _wide_tile(n):
    if n <= 1024:
        return n
    if n % 1024 == 0:
        return 1024
    assert n % FF_TILE == 0, n
    return FF_TILE


def _rms_fwd(x, g, name):
    t, d = x.shape
    tm = _row_tile(t, 512)

    def body(x_ref, g_ref, o_ref):
        xv = x_ref[...]
        o_ref[...] = (xv * _rstd(xv, d) * g_ref[...]).astype(BF16)

    return pl.pallas_call(
        body, name=name, grid=(t // tm,),
        in_specs=[pl.BlockSpec((tm, d), lambda i: (i, 0)), pl.BlockSpec((1, d), lambda i: (0, 0))],
        out_specs=pl.BlockSpec((tm, d), lambda i: (i, 0)),
        out_shape=jax.ShapeDtypeStruct((t, d), BF16), compiler_params=_cparams())(x, g)


def _rms_bwd(x, g, dxn, dres, name):
    t, d = x.shape
    tm = _row_tile(t, 256)

    def body(x_ref, g_ref, d_ref, r_ref, dx_ref, dg_ref):
        @pl.when(pl.program_id(0) == 0)
        def _():
            dg_ref[...] = jnp.zeros_like(dg_ref)

        xv = x_ref[...]
        r = _rstd(xv, d)
        dx, dgr = _rms_vjp(xv, r, g_ref[...], d_ref[...].astype(F32), d)
        dx_ref[...] = r_ref[...] + dx
        dg_ref[...] += jnp.sum(dgr, axis=0, keepdims=True)

    row = pl.BlockSpec((tm, d), lambda i: (i, 0))
    vec = pl.BlockSpec((1, d), lambda i: (0, 0))
    return pl.pallas_call(
        body, name=name, grid=(t // tm,), in_specs=[row, vec, row, row], out_specs=[row, vec],
        out_shape=[jax.ShapeDtypeStruct((t, d), F32), jax.ShapeDtypeStruct((1, d), F32)],
        compiler_params=_cparams())(x, g, dxn, dres)


def _mm(pairs, out_dtype, name):
    t = pairs[0][0].shape[0]
    n = pairs[0][1].shape[1]
    tm = _row_tile(t, 512)
    tn = _wide_tile(n)
    np_ = len(pairs)

    def body(*refs):
        o_ref = refs[2 * np_]
        acc = None
        for a_ref, w_ref in zip(refs[:np_], refs[np_:2 * np_]):
            part = _dot(a_ref[...].astype(BF16), w_ref[...])
            acc = part if acc is None else acc + part
        o_ref[...] = acc.astype(out_dtype)

    in_specs = [pl.BlockSpec((tm, a.shape[1]), lambda i, j: (i, 0)) for a, _ in pairs]
    in_specs += [pl.BlockSpec((w.shape[0], tn), lambda i, j: (0, j)) for _, w in pairs]
    return pl.pallas_call(
        body, name=name, grid=(t // tm, n // tn), in_specs=in_specs,
        out_specs=pl.BlockSpec((tm, tn), lambda i, j: (i, j)),
        out_shape=jax.ShapeDtypeStruct((t, n), out_dtype), compiler_params=_cparams(),
    )(*[a for a, _ in pairs], *[w for _, w in pairs])


def _mm_tn(a, b, name, scale=1.0):
    t, m = a.shape
    n = b.shape[1]
    tm, tn = _wide_tile(m), _wide_tile(n)
    tk = _row_tile(t, 512)
    nk = t // tk

    def body(a_ref, b_ref, o_ref):
        k = pl.program_id(2)

        @pl.when(k == 0)
        def _():
            o_ref[...] = jnp.zeros_like(o_ref)

        o_ref[...] += _dot_tn(a_ref[...].astype(BF16), b_ref[...].astype(BF16))
        if scale != 1.0:
            @pl.when(k == nk - 1)
            def _():
                o_ref[...] = o_ref[...] * scale

    return pl.pallas_call(
        body, name=name, grid=(m // tm, n // tn, nk),
        in_specs=[pl.BlockSpec((tk, tm), lambda i, j, k: (k, i)),
                  pl.BlockSpec((tk, tn), lambda i, j, k: (k, j))],
        out_specs=pl.BlockSpec((tm, tn), lambda i, j, k: (i, j)),
        out_shape=jax.ShapeDtypeStruct((m, n), F32), compiler_params=_cparams())(a, b)


class _Exchange(NamedTuple):
    operands: list
    out_shapes: list
    sem_shapes: list
    build: Callable


def _call_with_exchange(ex, body, name, grid, in_specs, out_specs, out_shape, scratch_shapes, operands):
    if ex is None:
        return pl.pallas_call(body, name=name, grid=grid, in_specs=in_specs, out_specs=out_specs, out_shape=out_shape,
                              scratch_shapes=scratch_shapes, compiler_params=_cparams())(*operands)
    n_in, n_out, n_scr = len(in_specs), len(out_specs), len(scratch_shapes)
    k_in, k_out = len(ex.operands), len(ex.out_shapes)

    def carried(*refs):
        a, b = n_in, n_in + k_in
        c, e = b + n_out, b + n_out + k_out
        f = e + n_scr
        start, finish = ex.build(refs[a:b], refs[c:e], refs[f:])
        steps = [pl.program_id(ax) for ax in range(len(grid))]
        first = functools.reduce(jnp.logical_and, [s == 0 for s in steps])
        last = functools.reduce(jnp.logical_and, [s == n - 1 for s, n in zip(steps, grid)])
        pl.when(first)(start)
        body(*refs[:a], *refs[b:c], *refs[e:f])
        pl.when(last)(finish)

    return pl.pallas_call(
        carried, name=name, grid=grid, in_specs=list(in_specs) + [ANY] * k_in,
        out_specs=list(out_specs) + [ANY] * k_out, out_shape=list(out_shape) + list(ex.out_shapes),
        scratch_shapes=list(scratch_shapes) + list(ex.sem_shapes), compiler_params=_cparams(),
    )(*operands, *ex.operands)


def _run_exchange(ex, name):
    k_in, k_out = len(ex.operands), len(ex.out_shapes)

    def body(*refs):
        start, finish = ex.build(refs[:k_in], refs[k_in:k_in + k_out], refs[k_in + k_out:])
        start()
        finish()

    return pl.pallas_call(body, name=name, in_specs=[ANY] * k_in, out_specs=[ANY] * k_out,
                          out_shape=list(ex.out_shapes), scratch_shapes=list(ex.sem_shapes))(*ex.operands)


def _ffn_fwd(x, g, wgu4, wd2, name, ex=None):
    t, d = x.shape
    tm = _row_tile(t, 512)

    def body(x_ref, g_ref, wg_ref, wu_ref, wd_ref, o_ref, gg_ref, uu_ref, xn_scr, acc_scr):
        j = pl.program_id(1)

        @pl.when(j == 0)
        def _():
            xv = x_ref[...]
            xn_scr[...] = (xv * _rstd(xv, d) * g_ref[...]).astype(BF16)
            acc_scr[...] = jnp.zeros_like(acc_scr)

        xn = xn_scr[...]
        gg = _dot(xn, wg_ref[0])
        uu = _dot(xn, wu_ref[0])
        gg_ref[...] = gg.astype(BF16)
        uu_ref[...] = uu.astype(BF16)
        act = gg * jax.nn.sigmoid(gg) * uu
        acc_scr[...] += _dot(act.astype(BF16), wd_ref[0])

        @pl.when(j == 1)
        def _():
            o_ref[...] = x_ref[...] + 0.5 * acc_scr[...]

    row = pl.BlockSpec((tm, d), lambda i, j: (i, 0))
    ffb = pl.BlockSpec((tm, FF_TILE), lambda i, j: (i, j))
    return _call_with_exchange(
        ex, body, name, (t // tm, 2),
        [row, pl.BlockSpec((1, d), lambda i, j: (0, 0)),
         pl.BlockSpec((1, d, FF_TILE), lambda i, j: (j, 0, 0)),
         pl.BlockSpec((1, d, FF_TILE), lambda i, j: (j + 2, 0, 0)),
         pl.BlockSpec((1, FF_TILE, d), lambda i, j: (j, 0, 0))],
        [row, ffb, ffb],
        [jax.ShapeDtypeStruct((t, d), F32), jax.ShapeDtypeStruct((t, D_FF), BF16), jax.ShapeDtypeStruct((t, D_FF), BF16)],
        [pltpu.VMEM((tm, d), BF16), pltpu.VMEM((tm, d), F32)], (x, g, wgu4, wgu4, wd2))


def _ffn_bwd(x, g, dy, gpre, upre, wgu4, wd2, name, ex=None):
    t, d = x.shape
    tm = _row_tile(t, 256)

    def body(dy_ref, gg_ref, uu_ref, wg_ref, wu_ref, wd_ref, dg_ref, du_ref, act_ref, part_ref):
        gg = gg_ref[...].astype(F32)
        uu = uu_ref[...].astype(F32)
        sg = jax.nn.sigmoid(gg)
        silu = gg * sg
        act_ref[...] = (silu * uu).astype(BF16)
        dyh = (0.5 * dy_ref[...]).astype(BF16)
        dact = _dot_nt(dyh, wd_ref[0])
        du = (dact * silu).astype(BF16)
        dgt = (dact * uu * (sg * (1.0 + gg * (1.0 - sg)))).astype(BF16)
        du_ref[...] = du
        dg_ref[...] = dgt
        part_ref[0] = _dot_nt(dgt, wg_ref[0]) + _dot_nt(du, wu_ref[0])

    row = pl.BlockSpec((tm, d), lambda j, i: (i, 0))
    ffb = pl.BlockSpec((tm, FF_TILE), lambda j, i: (i, j))
    dgt, dup, act, parts, *got = _call_with_exchange(
        ex, body, name, (2, t // tm),
        [row, ffb, ffb,
         pl.BlockSpec((1, d, FF_TILE), lambda j, i: (j, 0, 0)),
         pl.BlockSpec((1, d, FF_TILE), lambda j, i: (j + 2, 0, 0)),
         pl.BlockSpec((1, FF_TILE, d), lambda j, i: (j, 0, 0))],
        [ffb, ffb, ffb, pl.BlockSpec((1, tm, d), lambda j, i: (j, i, 0))],
        [jax.ShapeDtypeStruct((t, D_FF), BF16)] * 3 + [jax.ShapeDtypeStruct((2, t, d), F32)],
        [], (dy, gpre, upre, wgu4, wgu4, wd2))

    def norm_body(x_ref, g_ref, p_ref, dy_ref, dx_ref, dgain_ref, xn_ref):
        @pl.when(pl.program_id(0) == 0)
        def _():
            dgain_ref[...] = jnp.zeros_like(dgain_ref)

        xv = x_ref[...]
        r = _rstd(xv, d)
        xn_ref[...] = (xv * r * g_ref[...]).astype(BF16)
        dx, dgr = _rms_vjp(xv, r, g_ref[...], p_ref[0] + p_ref[1], d)
        dx_ref[...] = dy_ref[...] + dx
        dgain_ref[...] += jnp.sum(dgr, axis=0, keepdims=True)

    nrow = pl.BlockSpec((tm, d), lambda i: (i, 0))
    vec = pl.BlockSpec((1, d), lambda i: (0, 0))
    dx, dgain, xn = pl.pallas_call(
        norm_body, name=name + "_norm", grid=(t // tm,),
        in_specs=[nrow, vec, pl.BlockSpec((2, tm, d), lambda i: (0, i, 0)), nrow],
        out_specs=[nrow, vec, nrow],
        out_shape=[jax.ShapeDtypeStruct((t, d), F32), jax.ShapeDtypeStruct((1, d), F32),
                   jax.ShapeDtypeStruct((t, d), BF16)],
        compiler_params=_cparams())(x, g, parts, dy)
    return [dx, dgain, xn, dgt, dup, act] + got


def _loss_head(y, tgt, name):
    t, d = y.shape
    tm = _row_tile(t, 512)

    def body(y_ref, t_ref, dy_ref, loss_ref):
        @pl.when(pl.program_id(0) == 0)
        def _():
            loss_ref[...] = jnp.zeros_like(loss_ref)

        e = y_ref[...] - t_ref[...]
        dy_ref[...] = e * (1.0 / d)
        part = 0.5 * jnp.sum(jnp.sum(e * e, axis=-1, keepdims=True) * (1.0 / d), axis=0, keepdims=True)
        loss_ref[...] += jnp.broadcast_to(part, loss_ref.shape)

    row = pl.BlockSpec((tm, d), lambda i: (i, 0))
    return pl.pallas_call(
        body, name=name, grid=(t // tm,), in_specs=[row, row],
        out_specs=[row, pl.BlockSpec((1, LANES), lambda i: (0, 0))],
        out_shape=[jax.ShapeDtypeStruct((t, d), F32), jax.ShapeDtypeStruct((1, LANES), F32)],
        compiler_params=_cparams())(y, tgt)


def _sgu_layernorm(vpre, lg, lb):
    v = _gelu(vpre)
    mu = jnp.mean(v, axis=-1, keepdims=True)
    xc = v - mu
    rstd = lax.rsqrt(jnp.mean(xc * xc, axis=-1, keepdims=True) + EPS)
    xhat = xc * rstd
    return xhat, rstd, xhat * lg + lb


def _sgu_fwd(zuv, lg, lb, wt, bias_l, name):
    t = zuv.shape[0]
    tm = _row_tile(t, 512)

    def body(u_ref, v_ref, lg_ref, lb_ref, wt_ref, bl_ref, o_ref, vln_scr):
        _, _, vln = _sgu_layernorm(v_ref[...], lg_ref[...], lb_ref[...])
        vln_scr[...] = vln.astype(BF16)
        lo = lax.broadcasted_iota(jnp.int32, (CHUNK, LANES), 1) < 64
        for c in range(tm // CHUNK):
            rows = slice(c * CHUNK, (c + 1) * CHUNK)
            for p in range(SG_GROUPS // 2):
                cols = slice(p * LANES, (p + 1) * LANES)
                vp = vln_scr[rows, cols]
                mixed = jnp.where(lo, _dot(wt_ref[2 * p], vp), _dot(wt_ref[2 * p + 1], vp)) + bl_ref[:, cols]
                o_ref[rows, cols] = (_gelu(u_ref[rows, cols]) * mixed).astype(BF16)

    half = lambda k: pl.BlockSpec((tm, SG_WIDTH), lambda i: (i, k))
    vec = pl.BlockSpec((1, SG_WIDTH), lambda i: (0, 0))
    return pl.pallas_call(
        body, name=name, grid=(t // tm,),
        in_specs=[half(0), half(1), vec, vec,
                  pl.BlockSpec((SG_GROUPS, CHUNK, CHUNK), lambda i: (0, 0, 0)),
                  pl.BlockSpec((CHUNK, SG_WIDTH), lambda i: (0, 0))],
        out_specs=pl.BlockSpec((tm, SG_WIDTH), lambda i: (i, 0)),
        out_shape=jax.ShapeDtypeStruct((t, SG_WIDTH), BF16),
        scratch_shapes=[pltpu.VMEM((tm, SG_WIDTH), BF16)],
        compiler_params=_cparams())(zuv, zuv, lg, lb, wt, bias_l)


def _sgu_bwd(zuv, dya, lg, lb, wt, wt_t, bias_l, name):
    t = zuv.shape[0]
    tm = _row_tile(t, 256)
    nsteps = t // tm

    def body(u_ref, v_ref, dy_ref, lg_ref, lb_ref, wt_ref, wtt_ref, bl_ref,
             dz_ref, dwt_ref, dbl_ref, dlg_ref, dlb_ref, vln_scr, dvln_scr, dbacc_scr):
        step = pl.program_id(0)

        @pl.when(step == 0)
        def _():
            dwt_ref[...] = jnp.zeros_like(dwt_ref)
            dlg_ref[...] = jnp.zeros_like(dlg_ref)
            dlb_ref[...] = jnp.zeros_like(dlb_ref)
            dbl_ref[...] = jnp.zeros_like(dbl_ref)
            dbacc_scr[...] = jnp.zeros_like(dbacc_scr)

        vpre = v_ref[...]
        lgv = lg_ref[...]
        xhat, rstd, vln = _sgu_layernorm(vpre, lgv, lb_ref[...])
        vln_scr[...] = vln.astype(BF16)
        lo = lax.broadcasted_iota(jnp.int32, (CHUNK, LANES), 1) < 64
        for c in range(tm // CHUNK):
            rows = slice(c * CHUNK, (c + 1) * CHUNK)
            for p in range(SG_GROUPS // 2):
                cols = slice(p * LANES, (p + 1) * LANES)
                vp = vln_scr[rows, cols]
                mixed = jnp.where(lo, _dot(wt_ref[2 * p], vp), _dot(wt_ref[2 * p + 1], vp)) + bl_ref[:, cols]
                upre = u_ref[rows, cols]
                dyp = dy_ref[rows, cols]
                dz_ref[rows, cols] = (dyp * mixed * _gelu_grad(upre)).astype(BF16)
                dm = dyp * _gelu(upre)
                dbacc_scr[:, cols] += dm
                dlo = jnp.where(lo, dm, 0.0).astype(BF16)
                dhi = jnp.where(lo, 0.0, dm).astype(BF16)
                dvln_scr[rows, cols] = _dot(wtt_ref[2 * p], dlo) + _dot(wtt_ref[2 * p + 1], dhi)
                dwt_ref[2 * p] += _dot_nt(dlo, vp)
                dwt_ref[2 * p + 1] += _dot_nt(dhi, vp)
        dvln = dvln_scr[...]
        dlg_ref[...] += jnp.sum(dvln * xhat, axis=0, keepdims=True)
        dlb_ref[...] += jnp.sum(dvln, axis=0, keepdims=True)
        dxh = dvln * lgv
        dv = rstd * (dxh - jnp.mean(dxh, axis=-1, keepdims=True)
                     - xhat * jnp.mean(dxh * xhat, axis=-1, keepdims=True))
        dz_ref[:, SG_WIDTH:] = (dv * _gelu_grad(vpre)).astype(BF16)

        @pl.when(step == nsteps - 1)
        def _():
            rr = lax.broadcasted_iota(jnp.int32, (CHUNK, CHUNK), 0)
            cc = lax.broadcasted_iota(jnp.int32, (CHUNK, CHUNK), 1)
            tril = (cc <= rr).astype(F32)
            for gidx in range(SG_GROUPS):
                dwt_ref[gidx] = dwt_ref[gidx] * tril
            kk = lax.broadcasted_iota(jnp.int32, (SG_WIDTH, LANES), 0)
            gg = lax.broadcasted_iota(jnp.int32, (SG_WIDTH, LANES), 1)
            sel = ((kk // 64) == gg).astype(F32)
            dbl_ref[...] = jnp.dot(dbacc_scr[...], sel, preferred_element_type=F32,
                                   precision=lax.Precision.HIGHEST)

    half = lambda k: pl.BlockSpec((tm, SG_WIDTH), lambda i: (i, k))
    vec = pl.BlockSpec((1, SG_WIDTH), lambda i: (0, 0))
    wspec = pl.BlockSpec((SG_GROUPS, CHUNK, CHUNK), lambda i: (0, 0, 0))
    return pl.pallas_call(
        body, name=name, grid=(nsteps,),
        in_specs=[half(0), half(1), pl.BlockSpec((tm, SG_WIDTH), lambda i: (i, 0)), vec, vec,
                  wspec, wspec, pl.BlockSpec((CHUNK, SG_WIDTH), lambda i: (0, 0))],
        out_specs=[pl.BlockSpec((tm, 2 * SG_WIDTH), lambda i: (i, 0)), wspec,
                   pl.BlockSpec((CHUNK, LANES), lambda i: (0, 0)), vec, vec],
        out_shape=[jax.ShapeDtypeStruct((t, 2 * SG_WIDTH), BF16),
                   jax.ShapeDtypeStruct((SG_GROUPS, CHUNK, CHUNK), F32),
                   jax.ShapeDtypeStruct((CHUNK, LANES), F32),
                   jax.ShapeDtypeStruct((1, SG_WIDTH), F32), jax.ShapeDtypeStruct((1, SG_WIDTH), F32)],
        scratch_shapes=[pltpu.VMEM((tm, SG_WIDTH), BF16), pltpu.VMEM((tm, SG_WIDTH), F32),
                        pltpu.VMEM((CHUNK, SG_WIDTH), F32)],
        compiler_params=_cparams())(zuv, zuv, dya, lg, lb, wt, wt_t, bias_l)


def _rope(x, c, s1, s2):
    return x * c + pltpu.roll(x, LANES - 16, 1) * s1 + pltpu.roll(x, 16, 1) * s2


def _rope_t(dy, c, s1, s2):
    return dy * c + pltpu.roll(dy * s1, 16, 1) + pltpu.roll(dy * s2, LANES - 16, 1)


def _mla_prep_fwd(zcq, zckv, zkr, gcq, gckv, qg, kg, wuq, wuk, wuv, rc, rs1, rs2, name, ex=None):
    t = zcq.shape[0]
    tm = _row_tile(t, 256)
    hd = MLA_HEADS * LANES

    def body(zcq_ref, zckv_ref, zkr_ref, gcq_ref, gckv_ref, qg_ref, kg_ref, wuq_ref, wuk_ref, wuv_ref,
             c_ref, s1_ref, s2_ref, q_ref, k_ref, v_ref, cqn_ref, ckvn_ref):
        c, s1, s2 = c_ref[...], s1_ref[...], s2_ref[...]
        xq = zcq_ref[...]
        cqn = (xq * _rstd(xq, MLA_Q_RANK) * gcq_ref[...]).astype(BF16)
        cqn_ref[...] = cqn
        ql = _dot(cqn, wuq_ref[...])
        xk = zckv_ref[...]
        ckvn = (xk * _rstd(xk, MLA_KV_RANK) * gckv_ref[...]).astype(BF16)
        ckvn_ref[...] = ckvn
        kl = _dot(ckvn, wuk_ref[...])
        v_ref[...] = _dot(ckvn, wuv_ref[...]).astype(BF16)
        kr = zkr_ref[...]
        for h in range(MLA_HEADS):
            sl = slice(h * LANES, (h + 1) * LANES)
            qh = ql[:, sl]
            q_ref[:, sl] = _rope(qh * _rstd(qh, MLA_QK) * qg_ref[...], c, s1, s2).astype(BF16)
            kh = kl[:, sl] + kr
            k_ref[:, sl] = _rope(kh * _rstd(kh, MLA_QK) * kg_ref[...], c, s1, s2).astype(BF16)

    row = lambda n: pl.BlockSpec((tm, n), lambda i: (i, 0))
    full = lambda a: pl.BlockSpec(a.shape, lambda i: (0, 0))
    return _call_with_exchange(
        ex, body, name, (t // tm,),
        [row(MLA_Q_RANK), row(MLA_KV_RANK), row(LANES), full(gcq), full(gckv), full(qg), full(kg),
         full(wuq), full(wuk), full(wuv), row(LANES), row(LANES), row(LANES)],
        [row(hd), row(hd), row(hd), row(MLA_Q_RANK), row(MLA_KV_RANK)],
        [jax.ShapeDtypeStruct((t, hd), BF16)] * 3
        + [jax.ShapeDtypeStruct((t, MLA_Q_RANK), BF16), jax.ShapeDtypeStruct((t, MLA_KV_RANK), BF16)],
        [], (zcq, zckv, zkr, gcq, gckv, qg, kg, wuq, wuk, wuv, rc, rs1, rs2))


def _mla_prep_bwd(zcq, zckv, zkr, gcq, gckv, qg, kg, wuq, wuk, wuv, rc, rs1, rs2, dq, dk, dv, name):
    t = zcq.shape[0]
    tm = _row_tile(t, 256)
    hd = MLA_HEADS * LANES

    def body(zcq_ref, zckv_ref, zkr_ref, gcq_ref, gckv_ref, qg_ref, kg_ref, wuq_ref, wuk_ref, wuv_ref,
             c_ref, s1_ref, s2_ref, dq_ref, dk_ref, dv_ref,
             dzcq_ref, dzckv_ref, dzkr_ref, dql_ref, dkl_ref, dgcq_ref, dgckv_ref, dqg_ref, dkg_ref):
        @pl.when(pl.program_id(0) == 0)
        def _():
            for ref in (dgcq_ref, dgckv_ref, dqg_ref, dkg_ref):
                ref[...] = jnp.zeros_like(ref)

        c, s1, s2 = c_ref[...], s1_ref[...], s2_ref[...]
        qgv, kgv = qg_ref[...], kg_ref[...]
        xq = zcq_ref[...]
        rq = _rstd(xq, MLA_Q_RANK)
        ql = _dot((xq * rq * gcq_ref[...]).astype(BF16), wuq_ref[...])
        xk = zckv_ref[...]
        rk = _rstd(xk, MLA_KV_RANK)
        kl = _dot((xk * rk * gckv_ref[...]).astype(BF16), wuk_ref[...])
        kr = zkr_ref[...]
        dqg_acc = jnp.zeros((tm, LANES), F32)
        dkg_acc = jnp.zeros((tm, LANES), F32)
        dkr = jnp.zeros((tm, LANES), F32)
        for h in range(MLA_HEADS):
            sl = slice(h * LANES, (h + 1) * LANES)
            qh = ql[:, sl]
            dqh, dgr = _rms_vjp(qh, _rstd(qh, MLA_QK), qgv, _rope_t(dq_ref[:, sl], c, s1, s2), MLA_QK)
            dql_ref[:, sl] = dqh.astype(BF16)
            dqg_acc += dgr
            kh = kl[:, sl] + kr
            dkh, dgr = _rms_vjp(kh, _rstd(kh, MLA_QK), kgv, _rope_t(dk_ref[:, sl], c, s1, s2), MLA_QK)
            dkl_ref[:, sl] = dkh.astype(BF16)
            dkg_acc += dgr
            dkr += dkh
        dqg_ref[...] += jnp.sum(dqg_acc, axis=0, keepdims=True)
        dkg_ref[...] += jnp.sum(dkg_acc, axis=0, keepdims=True)
        lane = lax.broadcasted_iota(jnp.int32, (tm, LANES), 1)
        dzkr_ref[...] = jnp.where((lane >= MLA_NOPE) & (lane < MLA_QK), dkr, 0.0).astype(BF16)
        dcqn = _dot_nt(dql_ref[...], wuq_ref[...])
        dx, dgr = _rms_vjp(xq, rq, gcq_ref[...], dcqn, MLA_Q_RANK)
        dzcq_ref[...] = dx.astype(BF16)
        dgcq_ref[...] += jnp.sum(dgr, axis=0, keepdims=True)
        dckvn = _dot_nt(dkl_ref[...], wuk_ref[...]) + _dot_nt(dv_ref[...].astype(BF16), wuv_ref[...])
        dx, dgr = _rms_vjp(xk, rk, gckv_ref[...], dckvn, MLA_KV_RANK)
        dzckv_ref[...] = dx.astype(BF16)
        dgckv_ref[...] += jnp.sum(dgr, axis=0, keepdims=True)

    row = lambda n: pl.BlockSpec((tm, n), lambda i: (i, 0))
    full = lambda a: pl.BlockSpec(a.shape, lambda i: (0, 0))
    vec = lambda n: pl.BlockSpec((1, n), lambda i: (0, 0))
    return pl.pallas_call(
        body, name=name, grid=(t // tm,),
        in_specs=[row(MLA_Q_RANK), row(MLA_KV_RANK), row(LANES), full(gcq), full(gckv), full(qg), full(kg),
                  full(wuq), full(wuk), full(wuv), row(LANES), row(LANES), row(LANES), row(hd), row(hd), row(hd)],
        out_specs=[row(MLA_Q_RANK), row(MLA_KV_RANK), row(LANES), row(hd), row(hd),
                   vec(MLA_Q_RANK), vec(MLA_KV_RANK), vec(LANES), vec(LANES)],
        out_shape=[jax.ShapeDtypeStruct((t, MLA_Q_RANK), BF16), jax.ShapeDtypeStruct((t, MLA_KV_RANK), BF16),
                   jax.ShapeDtypeStruct((t, LANES), BF16), jax.ShapeDtypeStruct((t, hd), BF16),
                   jax.ShapeDtypeStruct((t, hd), BF16), jax.ShapeDtypeStruct((1, MLA_Q_RANK), F32),
                   jax.ShapeDtypeStruct((1, MLA_KV_RANK), F32), jax.ShapeDtypeStruct((1, LANES), F32),
                   jax.ShapeDtypeStruct((1, LANES), F32)],
        compiler_params=_cparams(),
    )(zcq, zckv, zkr, gcq, gckv, qg, kg, wuq, wuk, wuv, rc, rs1, rs2, dq, dk, dv)


def _attn_tiles(t):
    tq = 512 if t >= 2048 else 128
    return tq, min(t, 4 * tq), min(t, 2 * tq)


def _causal_keep(tq, tk, i, j):
    row = lax.broadcasted_iota(jnp.int32, (tq, tk), 0)
    col = lax.broadcasted_iota(jnp.int32, (tq, tk), 1)
    return (col - row) <= (i * tq - j * tk)


def _causal_keep_t(tq, tk, i, j):
    key = lax.broadcasted_iota(jnp.int32, (tk, tq), 0)
    qry = lax.broadcasted_iota(jnp.int32, (tk, tq), 1)
    return (key - qry) <= (i * tq - j * tk)


ATTN_FWD_HEADS_PER_STEP = 2
ATTN_BWD_HEADS_PER_STEP = 2


def _attn_fwd(q, k, v, name):
    t, hd = q.shape
    hp = ATTN_FWD_HEADS_PER_STEP
    tq, tk, _ = _attn_tiles(t)
    pairs = [(i, j) for i in range(t // tq) for j in range(((i + 1) * tq - 1) // tk + 1)]
    ii = np.array([p[0] for p in pairs], np.int32)
    jj = np.array([p[1] for p in pairs], np.int32)
    scale2 = MLA_QK ** -0.5 * LOG2E

    def body(ii_ref, jj_ref, q_ref, k_ref, v_ref, o_ref, lse_ref, m_scr, l_scr, acc_scr):
        s_id = pl.program_id(1)
        i, j = ii_ref[s_id], jj_ref[s_id]
        last = j == ((i + 1) * tq - 1) // tk

        @pl.when(j == 0)
        def _():
            m_scr[...] = jnp.full_like(m_scr, NEG)
            l_scr[...] = jnp.zeros_like(l_scr)
            acc_scr[...] = jnp.zeros_like(acc_scr)

        def step(masked):
            for hh in range(hp):
                sl = slice(hh * LANES, (hh + 1) * LANES)
                s = _dot_nt(q_ref[:, sl], k_ref[:, sl]) * scale2
                if masked:
                    s = jnp.where(_causal_keep(tq, tk, i, j), s, NEG)
                m_prev = m_scr[hh]
                m_new = jnp.maximum(m_prev, jnp.max(s, axis=1, keepdims=True))
                p = jnp.exp2(s - m_new)
                alpha = jnp.exp2(m_prev - m_new)
                l_new = alpha * l_scr[hh] + jnp.sum(p, axis=1, keepdims=True)
                acc = alpha * acc_scr[:, sl] + _dot(p.astype(BF16), v_ref[:, sl])
                if masked:
                    o_ref[:, sl] = (acc / l_new).astype(BF16)
                    lse_ref[:, sl] = jnp.broadcast_to(m_new + jnp.log(l_new) * LOG2E, (tq, LANES))
                else:
                    l_scr[hh] = l_new
                    acc_scr[:, sl] = acc
                    m_scr[hh] = m_new

        @pl.when(jnp.logical_not(last))
        def _():
            step(False)

        @pl.when(last)
        def _():
            step(True)

    w = hp * LANES
    qspec = pl.BlockSpec((tq, w), lambda h, s, ii_r, jj_r: (ii_r[s], h))
    kspec = pl.BlockSpec((tk, w), lambda h, s, ii_r, jj_r: (jj_r[s], h))
    return pl.pallas_call(
        body, name=name,
        grid_spec=pltpu.PrefetchScalarGridSpec(
            num_scalar_prefetch=2, grid=(hd // w, len(pairs)), in_specs=[qspec, kspec, kspec],
            out_specs=[qspec, qspec],
            scratch_shapes=[pltpu.VMEM((hp, tq, 1), F32), pltpu.VMEM((hp, tq, 1), F32),
                            pltpu.VMEM((tq, w), F32)]),
        out_shape=[jax.ShapeDtypeStruct((t, hd), BF16), jax.ShapeDtypeStruct((t, hd), F32)],
        compiler_params=_cparams())(jnp.asarray(ii), jnp.asarray(jj), q, k, v)


def _attn_bwd_rows(o, lse, do, name):
    t, hd = o.shape
    heads = hd // LANES
    tm = _row_tile(t, 512)

    def body(o_ref, lse_ref, do_ref, out_ref):
        lane = lax.broadcasted_iota(jnp.int32, (tm, LANES), 1)
        acc = jnp.zeros((tm, LANES), F32)
        for h in range(heads):
            sl = slice(h * LANES, (h + 1) * LANES)
            delta = jnp.sum(do_ref[:, sl].astype(F32) * o_ref[:, sl].astype(F32), axis=1, keepdims=True)
            acc = jnp.where(lane == h, delta, acc)
            acc = jnp.where(lane == heads + h, lse_ref[:, sl], acc)
        out_ref[...] = acc

    row = pl.BlockSpec((tm, hd), lambda i: (i, 0))
    cols = pl.pallas_call(
        body, name=name, grid=(t // tm,), in_specs=[row, row, row],
        out_specs=pl.BlockSpec((tm, LANES), lambda i: (i, 0)),
        out_shape=jax.ShapeDtypeStruct((t, LANES), F32), compiler_params=_cparams())(o, lse, do)
    rows = cols.T
    return rows[:heads].reshape(heads, 1, t), rows[heads:2 * heads].reshape(heads, 1, t)


def _attn_bwd(q, k, v, delta_rows, lse_rows, do, name):
    t, hd = q.shape
    hp = ATTN_BWD_HEADS_PER_STEP
    tq, _, tk = _attn_tiles(t)
    nq = t // tq
    pairs = [(i, j) for j in range(t // tk) for i in range((j * tk) // tq, nq)]
    ii = np.array([p[0] for p in pairs], np.int32)
    jj = np.array([p[1] for p in pairs], np.int32)
    scale = MLA_QK ** -0.5

    def body(jj_ref, ii_ref, q_ref, k_ref, v_ref, delta_ref, lse_ref, do_ref, dq_ref, dk_ref, dv_ref,
             dk_scr, dv_scr):
        s_id = pl.program_id(1)
        i, j = ii_ref[s_id], jj_ref[s_id]

        @pl.when(s_id == 0)
        def _():
            dq_ref[...] = jnp.zeros_like(dq_ref)

        @pl.when(i == (j * tk) // tq)
        def _():
            dk_scr[...] = jnp.zeros_like(dk_scr)
            dv_scr[...] = jnp.zeros_like(dv_scr)

        rows = pl.ds(pl.multiple_of(i * tq, tq), tq)

        def step(masked):
            for hh in range(hp):
                sl = slice(hh * LANES, (hh + 1) * LANES)
                qv, kv, dov = q_ref[:, sl], k_ref[:, sl], do_ref[:, sl]
                st = _dot_nt(kv, qv) * (scale * LOG2E)
                if masked:
                    st = jnp.where(_causal_keep_t(tq, tk, i, j), st, NEG)
                pt = jnp.exp2(st - lse_ref[hh])
                dv_scr[:, sl] += _dot(pt.astype(BF16), dov)
                dpt = _dot_nt(v_ref[:, sl], dov)
                dst = (pt * (dpt - delta_ref[hh]) * scale).astype(BF16)
                dk_scr[:, sl] += _dot(dst, qv)
                dq_ref[rows, sl] += _dot_tn(dst, kv)

        crosses = (j + 1) * tk - 1 > i * tq

        @pl.when(jnp.logical_not(crosses))
        def _():
            step(False)

        @pl.when(crosses)
        def _():
            step(True)

        @pl.when(i == nq - 1)
        def _():
            dk_ref[...] = dk_scr[...]
            dv_ref[...] = dv_scr[...]

    w = hp * LANES
    qspec = pl.BlockSpec((tq, w), lambda h, s, jj_r, ii_r: (ii_r[s], h))
    kspec = pl.BlockSpec((tk, w), lambda h, s, jj_r, ii_r: (jj_r[s], h))
    rspec = pl.BlockSpec((hp, 1, tq), lambda h, s, jj_r, ii_r: (h, 0, ii_r[s]))
    return pl.pallas_call(
        body, name=name,
        grid_spec=pltpu.PrefetchScalarGridSpec(
            num_scalar_prefetch=2, grid=(hd // w, len(pairs)),
            in_specs=[qspec, kspec, kspec, rspec, rspec, qspec],
            out_specs=[pl.BlockSpec((t, w), lambda h, s, jj_r, ii_r: (0, h)), kspec, kspec],
            scratch_shapes=[pltpu.VMEM((tk, w), F32), pltpu.VMEM((tk, w), F32)]),
        out_shape=[jax.ShapeDtypeStruct((t, hd), F32)] * 3,
        compiler_params=_cparams())(jnp.asarray(jj), jnp.asarray(ii), q, k, v, delta_rows, lse_rows, do)


MEM_W = MEM_HEADS * LANES


def _mem_kv_fwd(mem, gmem, wkv, kg, name):
    m, d = mem.shape

    def body(mem_ref, g_ref, w_ref, kg_ref, k_ref, v_ref, mn_ref):
        xv = mem_ref[...]
        mn = (xv * _rstd(xv, d) * g_ref[...]).astype(BF16)
        mn_ref[...] = mn
        kvm = _dot(mn, w_ref[...])
        v_ref[...] = kvm[:, MEM_W:].astype(BF16)
        for h in range(MEM_HEADS):
            sl = slice(h * LANES, (h + 1) * LANES)
            kh = kvm[:, sl]
            k_ref[:, sl] = (kh * _rstd(kh, LANES) * kg_ref[...]).astype(BF16)

    full = lambda a: pl.BlockSpec(a.shape, lambda i: (0, 0))
    return pl.pallas_call(
        body, name=name, grid=(1,), in_specs=[full(mem), full(gmem), full(wkv), full(kg)],
        out_specs=[pl.BlockSpec((m, MEM_W), lambda i: (0, 0)), pl.BlockSpec((m, MEM_W), lambda i: (0, 0)),
                   pl.BlockSpec((m, d), lambda i: (0, 0))],
        out_shape=[jax.ShapeDtypeStruct((m, MEM_W), BF16), jax.ShapeDtypeStruct((m, MEM_W), BF16),
                   jax.ShapeDtypeStruct((m, d), BF16)],
        compiler_params=_cparams())(mem, gmem, wkv, kg)


def _mem_softmax(qn, kh):
    s = _dot_nt(qn, kh) * (LANES ** -0.5)
    e = jnp.exp(s - jnp.max(s, axis=1, keepdims=True))
    return e / jnp.sum(e, axis=1, keepdims=True)


def _mem_attn_fwd(zqm, qg, km, vm, name):
    t = zqm.shape[0]
    tm = _row_tile(t, 512)

    def body(q_ref, qg_ref, k_ref, v_ref, o_ref):
        for h in range(MEM_HEADS):
            sl = slice(h * LANES, (h + 1) * LANES)
            qh = q_ref[:, sl]
            qn = (qh * _rstd(qh, LANES) * qg_ref[...]).astype(BF16)
            p = _mem_softmax(qn, k_ref[:, sl])
            o_ref[:, sl] = _dot(p.astype(BF16), v_ref[:, sl]).astype(BF16)

    row = pl.BlockSpec((tm, MEM_W), lambda i: (i, 0))
    full = lambda a: pl.BlockSpec(a.shape, lambda i: (0, 0))
    return pl.pallas_call(
        body, name=name, grid=(t // tm,), in_specs=[row, full(qg), full(km), full(vm)], out_specs=row,
        out_shape=jax.ShapeDtypeStruct((t, MEM_W), BF16), compiler_params=_cparams())(zqm, qg, km, vm)


def _mem_attn_bwd(zqm, dyc, qg, km, vm, name):
    t = zqm.shape[0]
    m = km.shape[0]
    tm = _row_tile(t, 256)

    def body(q_ref, dy_ref, qg_ref, k_ref, v_ref, dz_ref, dk_ref, dv_ref, dqg_ref):
        @pl.when(pl.program_id(0) == 0)
        def _():
            dk_ref[...] = jnp.zeros_like(dk_ref)
            dv_ref[...] = jnp.zeros_like(dv_ref)
            dqg_ref[...] = jnp.zeros_like(dqg_ref)

        qgv = qg_ref[...]
        dqg_acc = jnp.zeros((tm, LANES), F32)
        for h in range(MEM_HEADS):
            sl = slice(h * LANES, (h + 1) * LANES)
            qh = q_ref[:, sl]
            r = _rstd(qh, LANES)
            qn = (qh * r * qgv).astype(BF16)
            kh = k_ref[:, sl]
            p = _mem_softmax(qn, kh)
            dov = dy_ref[:, sl]
            dv_ref[:, sl] += _dot_tn(p.astype(BF16), dov)
            dp = _dot_nt(dov, v_ref[:, sl])
            ds = (p * (dp - jnp.sum(dp * p, axis=1, keepdims=True)) * (LANES ** -0.5)).astype(BF16)
            dk_ref[:, sl] += _dot_tn(ds, qn)
            dqh, dgr = _rms_vjp(qh, r, qgv, _dot(ds, kh), LANES)
            dz_ref[:, sl] = dqh.astype(BF16)
            dqg_acc += dgr
        dqg_ref[...] += jnp.sum(dqg_acc, axis=0, keepdims=True)

    row = pl.BlockSpec((tm, MEM_W), lambda i: (i, 0))
    full = lambda a: pl.BlockSpec(a.shape, lambda i: (0, 0))
    acc = pl.BlockSpec((m, MEM_W), lambda i: (0, 0))
    return pl.pallas_call(
        body, name=name, grid=(t // tm,), in_specs=[row, row, full(qg), full(km), full(vm)],
        out_specs=[row, acc, acc, pl.BlockSpec((1, LANES), lambda i: (0, 0))],
        out_shape=[jax.ShapeDtypeStruct((t, MEM_W), BF16), jax.ShapeDtypeStruct((m, MEM_W), F32),
                   jax.ShapeDtypeStruct((m, MEM_W), F32), jax.ShapeDtypeStruct((1, LANES), F32)],
        compiler_params=_cparams())(zqm, dyc, qg, km, vm)


def _mem_kv_bwd(mem, gmem, wkv, kg, dkn, dvm, name):
    m, d = mem.shape

    def body(mem_ref, g_ref, w_ref, kg_ref, dk_ref, dv_ref, dw_ref, dkg_ref, dg_ref, dkv_scr):
        xv = mem_ref[...]
        r = _rstd(xv, d)
        mn = (xv * r * g_ref[...]).astype(BF16)
        kvm = _dot(mn, w_ref[...])
        dkv_scr[:, MEM_W:] = dv_ref[...].astype(BF16)
        dkg_acc = jnp.zeros((m, LANES), F32)
        for h in range(MEM_HEADS):
            sl = slice(h * LANES, (h + 1) * LANES)
            kh = kvm[:, sl]
            dkh, dgr = _rms_vjp(kh, _rstd(kh, LANES), kg_ref[...], dk_ref[:, sl], LANES)
            dkv_scr[:, sl] = dkh.astype(BF16)
            dkg_acc += dgr
        dkg_ref[...] = jnp.sum(dkg_acc, axis=0, keepdims=True)
        dkv = dkv_scr[...]
        dw_ref[...] = _dot_tn(mn, dkv)
        dmn = _dot_nt(dkv, w_ref[...])
        dg_ref[...] = jnp.sum(dmn * xv * r, axis=0, keepdims=True)

    full = lambda a: pl.BlockSpec(a.shape, lambda i: (0, 0))
    return pl.pallas_call(
        body, name=name, grid=(1,),
        in_specs=[full(mem), full(gmem), full(wkv), full(kg), full(dkn), full(dvm)],
        out_specs=[pl.BlockSpec((d, 2 * MEM_W), lambda i: (0, 0)), pl.BlockSpec((1, LANES), lambda i: (0, 0)),
                   pl.BlockSpec((1, d), lambda i: (0, 0))],
        out_shape=[jax.ShapeDtypeStruct((d, 2 * MEM_W), F32), jax.ShapeDtypeStruct((1, LANES), F32),
                   jax.ShapeDtypeStruct((1, d), F32)],
        scratch_shapes=[pltpu.VMEM((m, 2 * MEM_W), BF16)],
        compiler_params=_cparams())(mem, gmem, wkv, kg, dkn, dvm)


def _merge_fwd(x1, ya, yb, yc, zg, bg, wa, wb, wc, wo, name):
    t, d = x1.shape
    tm = _row_tile(t, 256)

    def body(x_ref, ya_ref, yb_ref, yc_ref, zg_ref, bg_ref, wa_ref, wb_ref, wc_ref, wo_ref,
             x2_ref, mg_ref, pa_ref, pb_ref, pc_ref):
        merged = None
        for k, (y_ref, w_ref, p_ref) in enumerate(
                ((ya_ref, wa_ref, pa_ref), (yb_ref, wb_ref, pb_ref), (yc_ref, wc_ref, pc_ref))):
            sl = slice(k * d, (k + 1) * d)
            pr = _dot(y_ref[...], w_ref[...])
            p_ref[...] = pr.astype(BF16)
            term = jax.nn.sigmoid(zg_ref[:, sl] + bg_ref[:, sl]) * pr
            merged = term if merged is None else merged + term
        mb = merged.astype(BF16)
        mg_ref[...] = mb
        x2_ref[...] = x_ref[...] + _dot(mb, wo_ref[...])

    row = lambda n: pl.BlockSpec((tm, n), lambda i: (i, 0))
    full = lambda a: pl.BlockSpec(a.shape, lambda i: (0, 0))
    return pl.pallas_call(
        body, name=name, grid=(t // tm,),
        in_specs=[row(d), row(ya.shape[1]), row(yb.shape[1]), row(yc.shape[1]), row(3 * d), full(bg),
                  full(wa), full(wb), full(wc), full(wo)],
        out_specs=[row(d)] * 5,
        out_shape=[jax.ShapeDtypeStruct((t, d), F32)] + [jax.ShapeDtypeStruct((t, d), BF16)] * 4,
        compiler_params=_cparams())(x1, ya, yb, yc, zg, bg, wa, wb, wc, wo)


def _merge_bwd(dx2, pa, pb, pc, zg, bg, wa, wb, wc, wo, name, ex=None):
    t, d = dx2.shape
    tm = _row_tile(t, 256)

    def body(dx_ref, pa_ref, pb_ref, pc_ref, zg_ref, bg_ref, wa_ref, wb_ref, wc_ref, wo_ref,
             dpa_ref, dpb_ref, dpc_ref, dzg_ref, dbg_ref, dya_ref, dyb_ref, dyc_ref):
        @pl.when(pl.program_id(0) == 0)
        def _():
            dbg_ref[...] = jnp.zeros_like(dbg_ref)

        dm = _dot_nt(dx_ref[...].astype(BF16), wo_ref[...])
        for k, (p_ref, w_ref, dp_ref, dy_ref) in enumerate(
                ((pa_ref, wa_ref, dpa_ref, dya_ref), (pb_ref, wb_ref, dpb_ref, dyb_ref),
                 (pc_ref, wc_ref, dpc_ref, dyc_ref))):
            sl = slice(k * d, (k + 1) * d)
            gate = jax.nn.sigmoid(zg_ref[:, sl] + bg_ref[:, sl])
            dpr = (dm * gate).astype(BF16)
            dp_ref[...] = dpr
            dzg = dm * p_ref[...].astype(F32) * gate * (1.0 - gate)
            dzg_ref[:, sl] = dzg.astype(BF16)
            dbg_ref[:, sl] += jnp.sum(dzg, axis=0, keepdims=True)
            dy_ref[...] = _dot_nt(dpr, w_ref[...]).astype(dy_ref.dtype)

    row = lambda n: pl.BlockSpec((tm, n), lambda i: (i, 0))
    full = lambda a: pl.BlockSpec(a.shape, lambda i: (0, 0))
    na, nb, nc = wa.shape[0], wb.shape[0], wc.shape[0]
    return _call_with_exchange(
        ex, body, name, (t // tm,),
        [row(d), row(d), row(d), row(d), row(3 * d), full(bg), full(wa), full(wb), full(wc), full(wo)],
        [row(d), row(d), row(d), row(3 * d), pl.BlockSpec((1, 3 * d), lambda i: (0, 0)), row(na), row(nb), row(nc)],
        [jax.ShapeDtypeStruct((t, d), BF16)] * 3
        + [jax.ShapeDtypeStruct((t, 3 * d), BF16), jax.ShapeDtypeStruct((1, 3 * d), F32),
           jax.ShapeDtypeStruct((t, na), F32), jax.ShapeDtypeStruct((t, nb), BF16),
           jax.ShapeDtypeStruct((t, nc), BF16)],
        [], (dx2, pa, pb, pc, zg, bg, wa, wb, wc, wo))


def _adamw_math(w, g, m, v):
    bc1 = 1.0 - ADAM_B1 ** ADAM_STEP
    bc2 = 1.0 - ADAM_B2 ** ADAM_STEP
    nm = ADAM_B1 * m + (1.0 - ADAM_B1) * g
    nv = ADAM_B2 * v + (1.0 - ADAM_B2) * (g * g)
    delta = -ADAM_LR * ((nm / bc1) / (jnp.sqrt(nv / bc2) + ADAM_EPS) + ADAM_WD * w)
    return delta, nm, nv


def _div_tile(n, cap, mult):
    best = None
    for cand in range(mult, min(n, cap) + 1, mult):
        if n % cand == 0:
            best = cand
    assert best is not None, (n, cap, mult)
    return best


def _adamw(w, g, m, v, name):
    rows, cols = w.shape
    tr = rows if rows * cols <= 256 * 1024 else _div_tile(rows, 256, 8)

    def body(w_ref, g_ref, m_ref, v_ref, d_ref, nm_ref, nv_ref):
        d_ref[...], nm_ref[...], nv_ref[...] = _adamw_math(w_ref[...], g_ref[...], m_ref[...], v_ref[...])

    blk = pl.BlockSpec((tr, cols), lambda i: (i, 0))
    return pl.pallas_call(
        body, name=name, grid=(rows // tr,), in_specs=[blk] * 4, out_specs=[blk] * 3,
        out_shape=[jax.ShapeDtypeStruct((rows, cols), F32)] * 3, compiler_params=_cparams())(w, g, m, v)


def _adamw_slots(w, slots, m, v, name):
    _, hr, cols = w.shape
    tr = _div_tile(hr, 128, 16)

    def body(w_ref, s_ref, m_ref, v_ref, g_ref, d_ref, nm_ref, nv_ref):
        g = s_ref[0, 0].astype(F32)
        for k in range(1, N_CHIPS):
            g = g + s_ref[0, k].astype(F32)
        g_ref[0] = g
        d_ref[0], nm_ref[0], nv_ref[0] = _adamw_math(w_ref[0], g, m_ref[0], v_ref[0])

    blk = pl.BlockSpec((1, tr, cols), lambda h, i: (h, i, 0))
    return pl.pallas_call(
        body, name=name, grid=(2, hr // tr),
        in_specs=[blk, pl.BlockSpec((1, N_CHIPS, tr, cols), lambda h, i: (h, 0, i, 0)), blk, blk],
        out_specs=[blk] * 4, out_shape=[jax.ShapeDtypeStruct((2, hr, cols), F32)] * 4,
        compiler_params=_cparams())(w, slots, m, v)


ANY = pl.BlockSpec(memory_space=pl.ANY)


def _place():
    x, y, c = lax.axis_index("x"), lax.axis_index("y"), lax.axis_index("c")
    other_chips = [(1 - x, y), (x, 1 - y), (1 - x, 1 - y)]
    return x, y, c, other_chips


def _remote(src, dst, send_sem, recv_sem, to):
    return pltpu.make_async_remote_copy(src_ref=src, dst_ref=dst, send_sem=send_sem, recv_sem=recv_sem,
                                        device_id=to, device_id_type=MESH)


def _gather_exchange(shards):
    nw = len(shards)

    def build(s_refs, g_refs, sems):
        send_sems, recv_sems, local_sems = sems
        x, y, c, chips = _place()
        me = 2 * x + y
        sibling = (x, y, 1 - c)
        mine = [pltpu.make_async_copy(s_refs[w], g_refs[w].at[me], local_sems.at[w]) for w in range(nw)]
        first = [_remote(s_refs[w].at[c], g_refs[w].at[me, c], send_sems.at[k, w], recv_sems.at[k, w], (cx, cy, c))
                 for k, (cx, cy) in enumerate(chips) for w in range(nw)]

        def start():
            for cp in mine + first:
                cp.start()

        def finish():
            passed = []
            for k, (cx, cy) in enumerate(chips):
                for w in range(nw):
                    slab = g_refs[w].at[2 * cx + cy, c]
                    _remote(slab, slab, send_sems.at[k, w], recv_sems.at[k, w], (cx, cy, c)).wait_recv()
                    fwd = _remote(slab, slab, send_sems.at[3 + k, w], recv_sems.at[3 + k, w], sibling)
                    fwd.start()
                    passed.append(fwd)
            for k, (cx, cy) in enumerate(chips):
                for w in range(nw):
                    slab = g_refs[w].at[2 * cx + cy, 1 - c]
                    _remote(slab, slab, send_sems.at[3 + k, w], recv_sems.at[3 + k, w], sibling).wait_recv()
            for cp in first + passed:
                cp.wait_send()
            for cp in mine:
                cp.wait()

        return start, finish

    return _Exchange(list(shards), [jax.ShapeDtypeStruct((N_CHIPS,) + s.shape, BF16) for s in shards],
                     [pltpu.SemaphoreType.DMA((6, nw)), pltpu.SemaphoreType.DMA((6, nw)),
                      pltpu.SemaphoreType.DMA((nw,))], build)


def _swap_halves(grads, name):
    nw = len(grads)

    def body(*refs):
        g_refs, sib_refs = refs[:nw], refs[nw:2 * nw]
        send_sems, recv_sems = refs[2 * nw:]
        x, y, c, _ = _place()
        copies = [_remote(g_refs[w].at[s, 1 - c], sib_refs[w].at[s], send_sems.at[s, w], recv_sems.at[s, w],
                          (x, y, 1 - c)) for w in range(nw) for s in range(N_CHIPS)]
        for cp in copies:
            cp.start()
        for cp in copies:
            cp.wait_recv()
        for cp in copies:
            cp.wait_send()

    return pl.pallas_call(
        body, name=name, in_specs=[ANY] * nw, out_specs=[ANY] * nw,
        out_shape=[jax.ShapeDtypeStruct((N_CHIPS,) + g.shape[2:], BF16) for g in grads],
        scratch_shapes=[pltpu.SemaphoreType.DMA((N_CHIPS, nw)), pltpu.SemaphoreType.DMA((N_CHIPS, nw))],
    )(*grads)


def _pair_sum(grad, sib, core, name):
    nchip, _, hr, cols = grad.shape
    tr = _div_tile(hr, 256, 16)

    def body(core_ref, a_ref, b_ref, o_ref):
        o_ref[...] = (a_ref[0].astype(F32) + b_ref[...].astype(F32)).astype(BF16)

    return pl.pallas_call(
        body, name=name,
        grid_spec=pltpu.PrefetchScalarGridSpec(
            num_scalar_prefetch=1, grid=(nchip, hr // tr),
            in_specs=[pl.BlockSpec((1, 1, tr, cols), lambda s, i, core_r: (s, core_r[0], i, 0)),
                      pl.BlockSpec((1, tr, cols), lambda s, i, core_r: (s, i, 0))],
            out_specs=pl.BlockSpec((1, tr, cols), lambda s, i, core_r: (s, i, 0))),
        out_shape=jax.ShapeDtypeStruct((nchip, hr, cols), BF16), compiler_params=_cparams())(core, grad, sib)


def _pair_sum_exchange(sums):
    nw = len(sums)

    def build(p_refs, o_refs, sems):
        send_sems, recv_sems, local_sems = sems
        x, y, c, chips = _place()
        me = 2 * x + y
        sibling = (x, y, 1 - c)
        mine = [pltpu.make_async_copy(p_refs[w].at[me], o_refs[w].at[c, 3], local_sems.at[w]) for w in range(nw)]
        first = [_remote(p_refs[w].at[2 * cx + cy], o_refs[w].at[c, k], send_sems.at[k, w], recv_sems.at[k, w],
                         (cx, cy, c)) for k, (cx, cy) in enumerate(chips) for w in range(nw)]

        def start():
            for cp in mine + first:
                cp.start()

        def finish():
            passed = []
            for k in range(N_CHIPS):
                for w in range(nw):
                    slab = o_refs[w].at[c, k]
                    if k < 3:
                        first[k * nw + w].wait_recv()
                    else:
                        mine[w].wait()
                    fwd = _remote(slab, slab, send_sems.at[3 + k, w], recv_sems.at[3 + k, w], sibling)
                    fwd.start()
                    passed.append(fwd)
            for k in range(N_CHIPS):
                for w in range(nw):
                    slab = o_refs[w].at[1 - c, k]
                    _remote(slab, slab, send_sems.at[3 + k, w], recv_sems.at[3 + k, w], sibling).wait_recv()
            for cp in first + passed:
                cp.wait_send()

        return start, finish

    return _Exchange(list(sums), [jax.ShapeDtypeStruct((2,) + p.shape, BF16) for p in sums],
                     [pltpu.SemaphoreType.DMA((7, nw)), pltpu.SemaphoreType.DMA((7, nw)),
                      pltpu.SemaphoreType.DMA((nw,))], build)


def _allreduce_small(vec):
    m_per, n = vec.shape

    def body(x_ref, out_ref, gath_ref, send_sems, recv_sems, local_sem):
        x, y, c, chips = _place()
        me, sibling = (x, y, c), (x, y, 1 - c)

        def rows(px, py, pc):
            return gath_ref.at[pl.ds((4 * px + 2 * py + pc) * m_per, m_per), :]

        def copy(k, block, to, src=None):
            return pltpu.make_async_remote_copy(
                src_ref=rows(*block) if src is None else src, dst_ref=rows(*block),
                send_sem=send_sems.at[k], recv_sem=recv_sems.at[k], device_id=to, device_id_type=MESH)

        mine = pltpu.make_async_copy(x_ref, rows(*me), local_sem)
        mine.start()
        first = [copy(0, me, sibling, src=x_ref)]
        first += [copy(1 + j, me, (*chip, c), src=x_ref) for j, chip in enumerate(chips)]
        for cp in first:
            cp.start()
        passed = [copy(4 + j, (*chip, c), sibling) for j, chip in enumerate(chips)]
        for j, chip in enumerate(chips):
            copy(1 + j, (*chip, c), me).wait_recv()
            passed[j].start()
        copy(0, sibling, me).wait_recv()
        for j, chip in enumerate(chips):
            copy(4 + j, (*chip, 1 - c), me).wait_recv()
        for cp in first + passed:
            cp.wait_send()
        mine.wait()
        acc = gath_ref[pl.ds(0, m_per), :]
        for k in range(1, N_DEV):
            acc = acc + gath_ref[pl.ds(k * m_per, m_per), :]
        out_ref[...] = acc

    vm = pl.BlockSpec(memory_space=pltpu.VMEM)
    return pl.pallas_call(
        body, name="allreduce_small", in_specs=[vm], out_specs=vm,
        out_shape=jax.ShapeDtypeStruct((m_per, n), F32),
        scratch_shapes=[pltpu.VMEM((N_DEV * m_per, n), F32), pltpu.SemaphoreType.DMA((7,)),
                        pltpu.SemaphoreType.DMA((7,)), pltpu.SemaphoreType.DMA],
    )(vec)


def _pack_small(vals):
    flat = jnp.concatenate([vals[name].reshape(-1).astype(F32) for name, _ in SMALL])
    flat = jnp.pad(flat, (0, SMALL_ROWS * LANES - flat.shape[0]))
    return flat.reshape(SMALL_ROWS, LANES)


def _unpack_small(packed):
    flat = packed.reshape(-1)
    out, off = {}, 0
    for name, shape in SMALL:
        n = int(np.prod(shape))
        out[name] = flat[off:off + n].reshape(shape)
        off += n
    return out


def _head_pad_cols(w, heads, real):
    k = w.shape[0]
    return jnp.pad(w.reshape(k, heads, real), ((0, 0), (0, 0), (0, LANES - real))).reshape(k, heads * LANES)


def _rope_tables(positions):
    half = MLA_ROPE // 2
    inv = ROPE_BASE ** (-jnp.arange(half, dtype=F32) / half)
    ang = positions.astype(F32)[:, None] * inv
    cos, sin = jnp.cos(ang), jnp.sin(ang)
    t = positions.shape[0]
    z = lambda n: jnp.zeros((t, n), F32)
    rc = jnp.concatenate([jnp.ones((t, MLA_NOPE), F32), cos, cos, z(LANES - MLA_QK)], axis=1)
    rs1 = jnp.concatenate([z(MLA_NOPE), -sin, z(LANES - MLA_NOPE - half)], axis=1)
    rs2 = jnp.concatenate([z(MLA_NOPE + half), sin, z(LANES - MLA_QK)], axis=1)
    return rc, rs1, rs2


FFN1_WEIGHTS = ("ffn1_w_gu", "ffn1_w_down")
FFN2_WEIGHTS = ("ffn2_w_gu", "ffn2_w_down")
MIXER_WEIGHTS = tuple(n for n, *_ in SHARDED if n not in FFN1_WEIGHTS + FFN2_WEIGHTS)
SHARD_SHAPE = {n: (r, c, kind) for n, r, c, kind in SHARDED}


def _from_blocks(name, gathered):
    r, c, kind = SHARD_SHAPE[name]
    blk = gathered.reshape(N_CHIPS, r, c)
    return blk, (blk.transpose(1, 0, 2).reshape(r, N_CHIPS * c) if kind == "col" else blk.reshape(N_CHIPS * r, c))


def _grad_pair_sums(names, gw, core, tag):
    by_owner = []
    for name in names:
        r, c, kind = SHARD_SHAPE[name]
        blk = gw[name].reshape(r, N_CHIPS, c).transpose(1, 0, 2) if kind == "col" else gw[name].reshape(N_CHIPS, r, c)
        by_owner.append(blk.astype(BF16).reshape(N_CHIPS, 2, r // 2, c))
    received = _swap_halves(by_owner, "grad_swap_" + tag)
    return [_pair_sum(g, s, core, "pair_sum_" + n) for g, s, n in zip(by_owner, received, names)]


def _device_step(x, mem, positions, tgt, small, shards, core):
    d = D_MODEL
    g_ffn1, g_mix, g_ffn2 = small["ffn1_norm"], small["mix_norm"], small["ffn2_norm"]
    big = {}
    for name, g in zip(FFN1_WEIGHTS, _run_exchange(_gather_exchange([shards[n] for n in FFN1_WEIGHTS]), "gather_ffn1")):
        big[name + "#blocks"], big[name] = _from_blocks(name, g)
    wgu1, wd1 = big["ffn1_w_gu#blocks"], big["ffn1_w_down"].reshape(2, FF_TILE, d)
    x1, gpre1, upre1, *rest = _ffn_fwd(x, g_ffn1, wgu1, wd1, "ffn1_fwd",
                                       ex=_gather_exchange([shards[n] for n in MIXER_WEIGHTS]))
    for name, g in zip(MIXER_WEIGHTS, rest):
        big[name + "#blocks"], big[name] = _from_blocks(name, g)
    w_in = big["w_in"]
    w_uv_, w_cq, w_ckv = w_in[:, :COL_CQ], w_in[:, COL_CQ:COL_CKV], w_in[:, COL_CKV:COL_KR]
    w_kr = jnp.pad(w_in[:, COL_KR:COL_QM], ((0, 0), (MLA_NOPE, LANES - MLA_QK)))
    w_qm, w_g = w_in[:, COL_QM:COL_GATE], w_in[:, COL_GATE:]
    segs = (w_uv_, w_cq, w_ckv, w_kr, w_qm, w_g)
    wuq = _head_pad_cols(big["mla_w_uq"], MLA_HEADS, MLA_QK)
    ukv = big["mla_w_ukv"].reshape(MLA_KV_RANK, MLA_HEADS, 2, MLA_NOPE)
    wuk = _head_pad_cols(ukv[:, :, 0].reshape(MLA_KV_RANK, -1), MLA_HEADS, MLA_NOPE)
    wuv = _head_pad_cols(ukv[:, :, 1].reshape(MLA_KV_RANK, -1), MLA_HEADS, MLA_NOPE)
    wkv = big["mem_w_kv"]
    wa, wc, wo = big["w_branch_a"], big["w_branch_c"], big["w_out"]
    wb = jnp.pad(big["w_branch_b"].reshape(MLA_HEADS, MLA_NOPE, d),
                 ((0, 0), (0, LANES - MLA_NOPE), (0, 0))).reshape(MLA_HEADS * LANES, d)
    qg = jnp.pad(small["mla_q_norm"], ((0, 0), (0, LANES - MLA_QK)))
    kg = jnp.pad(small["mla_k_norm"], ((0, 0), (0, LANES - MLA_QK)))
    causal = jnp.tril(jnp.ones((CHUNK, CHUNK), bool))
    wt_f = jnp.where(causal[None], small["sg_w"][0], 0.0)
    wt, wt_t = wt_f.astype(BF16), wt_f.transpose(0, 2, 1).astype(BF16)
    bias_l = jnp.repeat(small["sg_b"][0].T, 64, axis=1)
    rc, rs1, rs2 = _rope_tables(positions)

    h = _rms_fwd(x1, g_mix, "mix_norm_fwd")
    zuv, zcq, zckv, zkr, zqm, zg = [
        _mm([(h, w)], F32, "in_proj_%d" % k) for k, w in enumerate(segs)]
    ya = _sgu_fwd(zuv, small["sg_ln_g"], small["sg_ln_b"], wt, bias_l, "sgu_fwd")
    q, k, v, cqn, ckvn, *rest = _mla_prep_fwd(zcq, zckv, zkr, small["mla_cq_norm"], small["mla_ckv_norm"], qg, kg,
                                              wuq, wuk, wuv, rc, rs1, rs2, "mla_prep_fwd",
                                              ex=_gather_exchange([shards[n] for n in FFN2_WEIGHTS]))
    for name, g in zip(FFN2_WEIGHTS, rest):
        big[name + "#blocks"], big[name] = _from_blocks(name, g)
    wgu2, wd2 = big["ffn2_w_gu#blocks"], big["ffn2_w_down"].reshape(2, FF_TILE, d)
    yb, lse = _attn_fwd(q, k, v, "mla_attn_fwd")
    km, vm, memn = _mem_kv_fwd(mem, small["mem_norm"], wkv, small["mem_k_norm"], "mem_kv_fwd")
    yc = _mem_attn_fwd(zqm, small["mem_q_norm"], km, vm, "mem_attn_fwd")
    x2, merged, pa, pb, pc = _merge_fwd(x1, ya, yb, yc, zg, small["b_gate"], wa, wb, wc, wo, "merge_fwd")
    x3, gpre2, upre2 = _ffn_fwd(x2, g_ffn2, wgu2, wd2, "ffn2_fwd")
    dy, loss_row = _loss_head(x3, tgt, "loss_head")

    gw, gs, slots = {}, {}, {}

    def ffn_grads(prefix, xin, gain, dyin, gpre, upre, wgu, wd, ex=None, ex_names=()):
        dx, dgain, xn, dgt, dup, act, *got = _ffn_bwd(xin, gain, dyin, gpre, upre, wgu, wd, prefix + "_bwd", ex=ex)
        slots.update(zip(ex_names, got))
        gw[prefix + "_w_gu"] = jnp.concatenate(
            [_mm_tn(xn, dgt, prefix + "_dwg"), _mm_tn(xn, dup, prefix + "_dwu")], axis=1)
        gw[prefix + "_w_down"] = _mm_tn(act, dyin, prefix + "_dwd", scale=0.5)
        gs[prefix + "_norm"] = dgain
        return dx

    dx2 = ffn_grads("ffn2", x2, g_ffn2, dy, gpre2, upre2, wgu2, wd2)
    ffn2_sums = _pair_sum_exchange(_grad_pair_sums(FFN2_WEIGHTS, gw, core, "ffn2"))
    dpa, dpb, dpc, dzg, dbg, dya, dyb, dyc, *got = _merge_bwd(dx2, pa, pb, pc, zg, small["b_gate"], wa, wb, wc, wo,
                                                              "merge_bwd", ex=ffn2_sums)
    slots.update(zip(FFN2_WEIGHTS, got))
    gs["b_gate"] = dbg
    gw["w_out"] = _mm_tn(merged, dx2, "dw_out")
    gw["w_branch_a"] = _mm_tn(ya, dpa, "dw_branch_a")
    gw["w_branch_b"] = _mm_tn(yb, dpb, "dw_branch_b").reshape(MLA_HEADS, LANES, d)[:, :MLA_NOPE].reshape(-1, d)
    gw["w_branch_c"] = _mm_tn(yc, dpc, "dw_branch_c")

    dzuv, dwt, dbl, dlg, dlb = _sgu_bwd(zuv, dya, small["sg_ln_g"], small["sg_ln_b"], wt, wt_t, bias_l, "sgu_bwd")
    gs["sg_w"], gs["sg_b"] = dwt[None], dbl[:, :SG_GROUPS].T[None]
    gs["sg_ln_g"], gs["sg_ln_b"] = dlg, dlb

    delta_rows, lse_rows = _attn_bwd_rows(yb, lse, dyb, "mla_attn_bwd_rows")
    dq, dk, dv = _attn_bwd(q, k, v, delta_rows, lse_rows, dyb, "mla_attn_bwd")
    dzcq, dzckv, dzkr, dql, dkl, dgcq, dgckv, dqg, dkg = _mla_prep_bwd(
        zcq, zckv, zkr, small["mla_cq_norm"], small["mla_ckv_norm"], qg, kg, wuq, wuk, wuv, rc, rs1, rs2,
        dq, dk, dv, "mla_prep_bwd")
    gs["mla_cq_norm"], gs["mla_ckv_norm"] = dgcq, dgckv
    gs["mla_q_norm"], gs["mla_k_norm"] = dqg[:, :MLA_QK], dkg[:, :MLA_QK]
    gw["mla_w_uq"] = _mm_tn(cqn, dql, "dw_uq").reshape(MLA_Q_RANK, MLA_HEADS, LANES)[:, :, :MLA_QK].reshape(
        MLA_Q_RANK, -1)
    dwuk = _mm_tn(ckvn, dkl, "dw_uk").reshape(MLA_KV_RANK, MLA_HEADS, LANES)[:, :, :MLA_NOPE]
    dwuv = _mm_tn(ckvn, dv, "dw_uv").reshape(MLA_KV_RANK, MLA_HEADS, LANES)[:, :, :MLA_NOPE]
    gw["mla_w_ukv"] = jnp.concatenate([dwuk, dwuv], axis=2).reshape(MLA_KV_RANK, -1)

    dzqm, dkn, dvm, dmqg = _mem_attn_bwd(zqm, dyc, small["mem_q_norm"], km, vm, "mem_attn_bwd")
    gs["mem_q_norm"] = dmqg
    gw["mem_w_kv"], gs["mem_k_norm"], gs["mem_norm"] = _mem_kv_bwd(
        mem, small["mem_norm"], wkv, small["mem_k_norm"], dkn, dvm, "mem_kv_bwd")

    dzs = (dzuv, dzcq, dzckv, dzkr, dzqm, dzg)
    dh = _mm([(dz, w.T) for dz, w in zip(dzs, segs)], F32, "in_proj_bwd")
    dws = [_mm_tn(h, dz, "dw_in_%d" % k) for k, dz in enumerate(dzs)]
    dws[3] = dws[3][:, MLA_NOPE:MLA_QK]
    gw["w_in"] = jnp.concatenate(dws, axis=1)
    dx1, gs["mix_norm"] = _rms_bwd(x1, g_mix, dh, dx2, "mix_norm_bwd")
    mixer_sums = _pair_sum_exchange(_grad_pair_sums(MIXER_WEIGHTS, gw, core, "mixer"))
    dx = ffn_grads("ffn1", x, g_ffn1, dx1, gpre1, upre1, wgu1, wd1, ex=mixer_sums, ex_names=MIXER_WEIGHTS)
    ffn1_sums = _pair_sum_exchange(_grad_pair_sums(FFN1_WEIGHTS, gw, core, "ffn1"))
    slots.update(zip(FFN1_WEIGHTS, _run_exchange(ffn1_sums, "grad_exchange_ffn1")))
    return loss_row, dx, slots, gs


def kernel(x, mem, positions, ffn1_norm, ffn1_w_gu, ffn1_w_down, mix_norm, w_in, b_gate, sg_ln_g, sg_ln_b, sg_w, sg_b, mla_cq_norm, mla_w_uq, mla_ckv_norm, mla_w_ukv, mla_q_norm, mla_k_norm, mem_norm, mem_w_kv, mem_q_norm, mem_k_norm, w_branch_a, w_branch_b, w_branch_c, w_out, ffn2_norm, ffn2_w_gu, ffn2_w_down, loss_target, m_ffn1_norm, m_ffn1_w_gu, m_ffn1_w_down, m_mix_norm, m_w_in, m_b_gate, m_sg_ln_g, m_sg_ln_b, m_sg_w, m_sg_b, m_mla_cq_norm, m_mla_w_uq, m_mla_ckv_norm, m_mla_w_ukv, m_mla_q_norm, m_mla_k_norm, m_mem_norm, m_mem_w_kv, m_mem_q_norm, m_mem_k_norm, m_w_branch_a, m_w_branch_b, m_w_branch_c, m_w_out, m_ffn2_norm, m_ffn2_w_gu, m_ffn2_w_down, v_ffn1_norm, v_ffn1_w_gu, v_ffn1_w_down, v_mix_norm, v_w_in, v_b_gate, v_sg_ln_g, v_sg_ln_b, v_sg_w, v_sg_b, v_mla_cq_norm, v_mla_w_uq, v_mla_ckv_norm, v_mla_w_ukv, v_mla_q_norm, v_mla_k_norm, v_mem_norm, v_mem_w_kv, v_mem_q_norm, v_mem_k_norm, v_w_branch_a, v_w_branch_b, v_w_branch_c, v_w_out, v_ffn2_norm, v_ffn2_w_gu, v_ffn2_w_down):
    args = dict(locals())
    weights = {n: args[n] for n in WEIGHT_ORDER}
    mom_m = {n: args["m_" + n] for n in WEIGHT_ORDER}
    mom_v = {n: args["v_" + n] for n in WEIGHT_ORDER}
    small = {n: weights[n] for n, _ in SMALL}
    halves = lambda a, r, c: a.reshape(2, r // 2, c)

    shards = {n: halves(weights[n][0].astype(BF16), r, c) for n, r, c, _ in SHARDED}
    core = lax.axis_index("c").astype(jnp.int32).reshape(1)
    loss_row, dx, slots, gs = _device_step(x[0], mem[0], positions[0], loss_target[0], small, shards, core)
    loss = lax.psum(loss_row[0, 0], ("x", "y", "c"))
    small_grads = _unpack_small(_allreduce_small(_pack_small(gs)))

    grads, deltas, new_m, new_v = {}, {}, {}, {}
    for name, r, c, _ in SHARDED:
        outs = _adamw_slots(halves(weights[name][0], r, c), slots[name], halves(mom_m[name][0], r, c),
                            halves(mom_v[name][0], r, c), "adamw_" + name)
        shape = weights[name].shape
        grads[name], deltas[name], new_m[name], new_v[name] = [o.reshape(shape) for o in outs]
    dlt, nm, nv = _adamw(_pack_small(small), _pack_small(small_grads), _pack_small({n: mom_m[n] for n, _ in SMALL}),
                         _pack_small({n: mom_v[n] for n, _ in SMALL}), "adamw_small")
    for name, _ in SMALL:
        grads[name] = small_grads[name]
    deltas.update(_unpack_small(dlt))
    new_m.update(_unpack_small(nm))
    new_v.update(_unpack_small(nv))

    return (loss, dx[None], *[grads[n] for n in WEIGHT_ORDER], *[deltas[n] for n in WEIGHT_ORDER],
            *[new_m[n] for n in WEIGHT_ORDER], *[new_v[n] for n in WEIGHT_ORDER])
```

```python
import functools
from typing import Callable, NamedTuple

import numpy as np
import jax
import jax.numpy as jnp
from jax import lax
from jax.experimental import pallas as pl
from jax.experimental.pallas import tpu as pltpu

F32 = jnp.float32
BF16 = jnp.bfloat16

D_MODEL = 1024
D_FF = 2816
FF_TILE = 1408
SG_WIDTH = 512
SG_GROUPS = 8
CHUNK = 128
MLA_HEADS = 8
MLA_QK = 96
MLA_NOPE = 64
MLA_ROPE = 32
MLA_Q_RANK = 384
MLA_KV_RANK = 256
MEM_HEADS = 4
MEM_LEN = 256
LANES = 128
EPS = 1e-6
NEG = -1e30
ROPE_BASE = 10000.0
N_CHIPS = 4
N_DEV = 8

ADAM_LR = 0.001
ADAM_B1 = 0.9
ADAM_B2 = 0.999
ADAM_EPS = 1e-08
ADAM_WD = 0.01
ADAM_STEP = 10

COL_V = 512
COL_CQ = 1024
COL_CKV = 1408
COL_KR = 1664
COL_QM = 1696
COL_GATE = 2208
IN_COLS = 5280

VMEM_LIMIT_BYTES = 56 * 1024 * 1024
INV_SQRT2 = 0.7071067811865476
INV_SQRT_2PI = 0.3989422804014327
LOG2E = 1.4426950408889634
ATTN_SCALE = MLA_QK ** -0.5
ATTN_SCALE2 = ATTN_SCALE * LOG2E

SHARDED = (
    ("ffn1_w_gu", 1024, 1408, "col"),
    ("ffn1_w_down", 704, 1024, "row"),
    ("w_in", 1024, 1320, "col"),
    ("mla_w_uq", 384, 192, "col"),
    ("mla_w_ukv", 256, 256, "col"),
    ("mem_w_kv", 256, 1024, "row"),
    ("w_branch_a", 512, 256, "col"),
    ("w_branch_b", 512, 256, "col"),
    ("w_branch_c", 512, 256, "col"),
    ("w_out", 256, 1024, "row"),
    ("ffn2_w_gu", 1024, 1408, "col"),
    ("ffn2_w_down", 704, 1024, "row"),
)
SMALL = (
    ("ffn1_norm", (1, 1024)), ("mix_norm", (1, 1024)), ("b_gate", (1, 3072)),
    ("sg_ln_g", (1, 512)), ("sg_ln_b", (1, 512)), ("sg_w", (1, 8, 128, 128)),
    ("sg_b", (1, 8, 128)), ("mla_cq_norm", (1, 384)), ("mla_ckv_norm", (1, 256)),
    ("mla_q_norm", (1, 96)), ("mla_k_norm", (1, 96)), ("mem_norm", (1, 1024)),
    ("mem_q_norm", (1, 128)), ("mem_k_norm", (1, 128)), ("ffn2_norm", (1, 1024)),
)
WEIGHT_ORDER = (
    "ffn1_norm", "ffn1_w_gu", "ffn1_w_down", "mix_norm", "w_in", "b_gate", "sg_ln_g", "sg_ln_b",
    "sg_w", "sg_b", "mla_cq_norm", "mla_w_uq", "mla_ckv_norm", "mla_w_ukv", "mla_q_norm",
    "mla_k_norm", "mem_norm", "mem_w_kv", "mem_q_norm", "mem_k_norm", "w_branch_a", "w_branch_b",
    "w_branch_c", "w_out", "ffn2_norm", "ffn2_w_gu", "ffn2_w_down",
)

_N_SMALL = sum(int(np.prod(s)) for _, s in SMALL)
SMALL_ROWS = -(-_N_SMALL // (LANES * 8)) * 8

MESH = pl.DeviceIdType.MESH


def _cparams():
    return pltpu.CompilerParams(vmem_limit_bytes=VMEM_LIMIT_BYTES)


def _dot(a, b):
    return jnp.dot(a, b, preferred_element_type=F32)


def _dot_nt(a, b):
    return lax.dot_general(a, b, (((1,), (1,)), ((), ())), preferred_element_type=F32)


def _dot_tn(a, b):
    return lax.dot_general(a, b, (((0,), (0,)), ((), ())), preferred_element_type=F32)


def _gelu(x):
    return 0.5 * x * (1.0 + lax.erf(x * INV_SQRT2))


def _gelu_grad(x):
    return 0.5 * (1.0 + lax.erf(x * INV_SQRT2)) + x * jnp.exp(-0.5 * x * x) * INV_SQRT_2PI


def _rstd(x, n):
    return lax.rsqrt(jnp.sum(x * x, axis=-1, keepdims=True) * (1.0 / n) + EPS)


def _rms_vjp(x, r, g, dy, n):
    dxh = dy * g
    dx = r * dxh - x * (r * r * r) * (jnp.sum(dxh * x, axis=-1, keepdims=True) * (1.0 / n))
    return dx, dy * x * r


def _row_tile(t, want):
    return min(t, want)


def _wide_tile(n):
    if n <= 1024:
        return n
    if n % 1024 == 0:
        return 1024
    assert n % FF_TILE == 0, n
    return FF_TILE


def _rms_fwd(x, g, name):
    t, d = x.shape
    tm = _row_tile(t, 512)

    def body(x_ref, g_ref, o_ref):
        xv = x_ref[...]
        o_ref[...] = (xv * _rstd(xv, d) * g_ref[...]).astype(BF16)

    return pl.pallas_call(
        body, name=name, grid=(t // tm,),
        in_specs=[pl.BlockSpec((tm, d), lambda i: (i, 0)), pl.BlockSpec((1, d), lambda i: (0, 0))],
        out_specs=pl.BlockSpec((tm, d), lambda i: (i, 0)),
        out_shape=jax.ShapeDtypeStruct((t, d), BF16), compiler_params=_cparams())(x, g)


def _rms_bwd(x, g, dxn, dres, name):
    t, d = x.shape
    tm = _row_tile(t, 256)

    def body(x_ref, g_ref, d_ref, r_ref, dx_ref, dg_ref):
        @pl.when(pl.program_id(0) == 0)
        def _():
            dg_ref[...] = jnp.zeros_like(dg_ref)

        xv = x_ref[...]
        r = _rstd(xv, d)
        dx, dgr = _rms_vjp(xv, r, g_ref[...], d_ref[...].astype(F32), d)
        dx_ref[...] = r_ref[...] + dx
        dg_ref[...] += jnp.sum(dgr, axis=0, keepdims=True)

    row = pl.BlockSpec((tm, d), lambda i: (i, 0))
    vec = pl.BlockSpec((1, d), lambda i: (0, 0))
    return pl.pallas_call(
        body, name=name, grid=(t // tm,), in_specs=[row, vec, row, row], out_specs=[row, vec],
        out_shape=[jax.ShapeDtypeStruct((t, d), F32), jax.ShapeDtypeStruct((1, d), F32)],
        compiler_params=_cparams())(x, g, dxn, dres)


def _mm(pairs, out_dtype, name):
    t = pairs[0][0].shape[0]
    n = pairs[0][1].shape[1]
    tm = _row_tile(t, 512)
    tn = _wide_tile(n)
    np_ = len(pairs)

    def body(*refs):
        o_ref = refs[2 * np_]
        acc = None
        for a_ref, w_ref in zip(refs[:np_], refs[np_:2 * np_]):
            part = _dot(a_ref[...].astype(BF16), w_ref[...])
            acc = part if acc is None else acc + part
        o_ref[...] = acc.astype(out_dtype)

    in_specs = [pl.BlockSpec((tm, a.shape[1]), lambda i, j: (i, 0)) for a, _ in pairs]
    in_specs += [pl.BlockSpec((w.shape[0], tn), lambda i, j: (0, j)) for _, w in pairs]
    return pl.pallas_call(
        body, name=name, grid=(t // tm, n // tn), in_specs=in_specs,
        out_specs=pl.BlockSpec((tm, tn), lambda i, j: (i, j)),
        out_shape=jax.ShapeDtypeStruct((t, n), out_dtype), compiler_params=_cparams(),
    )(*[a for a, _ in pairs], *[w for _, w in pairs])


def _mm_cols(a, ws, out_dtypes, name):
    t, kdim = a.shape
    tm = _row_tile(t, 256)
    n = len(ws)

    def body(*refs):
        av = refs[0][...]
        for w_ref, o_ref in zip(refs[1:1 + n], refs[1 + n:]):
            o_ref[...] = _dot(av, w_ref[...]).astype(o_ref.dtype)

    row = lambda width: pl.BlockSpec((tm, width), lambda i: (i, 0))
    return pl.pallas_call(
        body, name=name, grid=(t // tm,),
        in_specs=[row(kdim)] + [pl.BlockSpec(w.shape, lambda i: (0, 0)) for w in ws],
        out_specs=[row(w.shape[1]) for w in ws],
        out_shape=[jax.ShapeDtypeStruct((t, w.shape[1]), dt) for w, dt in zip(ws, out_dtypes)],
        compiler_params=_cparams())(a, *ws)


def _mm_tn_cols(a, bs, name):
    t, m = a.shape
    tk = _row_tile(t, 512)
    n = len(bs)

    def body(*refs):
        @pl.when(pl.program_id(0) == 0)
        def _():
            for o_ref in refs[1 + n:]:
                o_ref[...] = jnp.zeros_like(o_ref)

        av = refs[0][...].astype(BF16)
        for b_ref, o_ref in zip(refs[1:1 + n], refs[1 + n:]):
            o_ref[...] += _dot_tn(av, b_ref[...].astype(BF16))

    row = lambda width: pl.BlockSpec((tk, width), lambda k: (k, 0))
    return pl.pallas_call(
        body, name=name, grid=(t // tk,), in_specs=[row(m)] + [row(b.shape[1]) for b in bs],
        out_specs=[pl.BlockSpec((m, b.shape[1]), lambda k: (0, 0)) for b in bs],
        out_shape=[jax.ShapeDtypeStruct((m, b.shape[1]), F32) for b in bs],
        compiler_params=_cparams())(a, *bs)


def _mm_tn(a, b, name, scale=1.0):
    t, m = a.shape
    n = b.shape[1]
    tm, tn = _wide_tile(m), _wide_tile(n)
    tk = _row_tile(t, 512)
    nk = t // tk

    def body(a_ref, b_ref, o_ref):
        k = pl.program_id(2)

        @pl.when(k == 0)
        def _():
            o_ref[...] = jnp.zeros_like(o_ref)

        o_ref[...] += _dot_tn(a_ref[...].astype(BF16), b_ref[...].astype(BF16))
        if scale != 1.0:
            @pl.when(k == nk - 1)
            def _():
                o_ref[...] = o_ref[...] * scale

    return pl.pallas_call(
        body, name=name, grid=(m // tm, n // tn, nk),
        in_specs=[pl.BlockSpec((tk, tm), lambda i, j, k: (k, i)),
                  pl.BlockSpec((tk, tn), lambda i, j, k: (k, j))],
        out_specs=pl.BlockSpec((tm, tn), lambda i, j, k: (i, j)),
        out_shape=jax.ShapeDtypeStruct((m, n), F32), compiler_params=_cparams())(a, b)


class _Exchange(NamedTuple):
    operands: list
    out_shapes: list
    sem_shapes: list
    build: Callable


def _call_with_exchange(ex, body, name, grid, in_specs, out_specs, out_shape, scratch_shapes, operands):
    if ex is None:
        return pl.pallas_call(body, name=name, grid=grid, in_specs=in_specs, out_specs=out_specs, out_shape=out_shape,
                              scratch_shapes=scratch_shapes, compiler_params=_cparams())(*operands)
    n_in, n_out, n_scr = len(in_specs), len(out_specs), len(scratch_shapes)
    k_in, k_out = len(ex.operands), len(ex.out_shapes)

    def carried(*refs):
        a, b = n_in, n_in + k_in
        c, e = b + n_out, b + n_out + k_out
        f = e + n_scr
        start, finish = ex.build(refs[a:b], refs[c:e], refs[f:])
        steps = [pl.program_id(ax) for ax in range(len(grid))]
        first = functools.reduce(jnp.logical_and, [s == 0 for s in steps])
        last = functools.reduce(jnp.logical_and, [s == n - 1 for s, n in zip(steps, grid)])
        pl.when(first)(start)
        body(*refs[:a], *refs[b:c], *refs[e:f])
        pl.when(last)(finish)

    return pl.pallas_call(
        carried, name=name, grid=grid, in_specs=list(in_specs) + [ANY] * k_in,
        out_specs=list(out_specs) + [ANY] * k_out, out_shape=list(out_shape) + list(ex.out_shapes),
        scratch_shapes=list(scratch_shapes) + list(ex.sem_shapes), compiler_params=_cparams(),
    )(*operands, *ex.operands)


def _run_exchange(ex, name):
    k_in, k_out = len(ex.operands), len(ex.out_shapes)

    def body(*refs):
        start, finish = ex.build(refs[:k_in], refs[k_in:k_in + k_out], refs[k_in + k_out:])
        start()
        finish()

    return pl.pallas_call(body, name=name, in_specs=[ANY] * k_in, out_specs=[ANY] * k_out,
                          out_shape=list(ex.out_shapes), scratch_shapes=list(ex.sem_shapes))(*ex.operands)


def _ffn_fwd(x, g, wgu4, wd2, name, ex=None):
    t, d = x.shape
    tm = _row_tile(t, 512)

    def body(x_ref, g_ref, wg_ref, wu_ref, wd_ref, o_ref, gg_ref, uu_ref, xn_scr, acc_scr):
        j = pl.program_id(1)

        @pl.when(j == 0)
        def _():
            xv = x_ref[...]
            xn_scr[...] = (xv * _rstd(xv, d) * g_ref[...]).astype(BF16)
            acc_scr[...] = jnp.zeros_like(acc_scr)

        xn = xn_scr[...]
        gg = _dot(xn, wg_ref[0])
        uu = _dot(xn, wu_ref[0])
        gg_ref[...] = gg.astype(BF16)
        uu_ref[...] = uu.astype(BF16)
        act = gg * jax.nn.sigmoid(gg) * uu
        acc_scr[...] += _dot(act.astype(BF16), wd_ref[0])

        @pl.when(j == 1)
        def _():
            o_ref[...] = x_ref[...] + 0.5 * acc_scr[...]

    row = pl.BlockSpec((tm, d), lambda i, j: (i, 0))
    ffb = pl.BlockSpec((tm, FF_TILE), lambda i, j: (i, j))
    return _call_with_exchange(
        ex, body, name, (t // tm, 2),
        [row, pl.BlockSpec((1, d), lambda i, j: (0, 0)),
         pl.BlockSpec((1, d, FF_TILE), lambda i, j: (j, 0, 0)),
         pl.BlockSpec((1, d, FF_TILE), lambda i, j: (j + 2, 0, 0)),
         pl.BlockSpec((1, FF_TILE, d), lambda i, j: (j, 0, 0))],
        [row, ffb, ffb],
        [jax.ShapeDtypeStruct((t, d), F32), jax.ShapeDtypeStruct((t, D_FF), BF16), jax.ShapeDtypeStruct((t, D_FF), BF16)],
        [pltpu.VMEM((tm, d), BF16), pltpu.VMEM((tm, d), F32)], (x, g, wgu4, wgu4, wd2))


def _ffn_bwd(x, g, dy, gpre, upre, wgu4, wd2, name, ex=None):
    t, d = x.shape
    tm = _row_tile(t, 256)

    def body(dy_ref, gg_ref, uu_ref, wg_ref, wu_ref, wd_ref, dg_ref, du_ref, act_ref, part_ref):
        gg = gg_ref[...].astype(F32)
        uu = uu_ref[...].astype(F32)
        sg = jax.nn.sigmoid(gg)
        silu = gg * sg
        act_ref[...] = (silu * uu).astype(BF16)
        dyh = (0.5 * dy_ref[...]).astype(BF16)
        dact = _dot_nt(dyh, wd_ref[0])
        du = (dact * silu).astype(BF16)
        dgt = (dact * uu * (sg * (1.0 + gg * (1.0 - sg)))).astype(BF16)
        du_ref[...] = du
        dg_ref[...] = dgt
        part_ref[0] = _dot_nt(dgt, wg_ref[0]) + _dot_nt(du, wu_ref[0])

    row = pl.BlockSpec((tm, d), lambda j, i: (i, 0))
    ffb = pl.BlockSpec((tm, FF_TILE), lambda j, i: (i, j))
    dgt, dup, act, parts, *got = _call_with_exchange(
        ex, body, name, (2, t // tm),
        [row, ffb, ffb,
         pl.BlockSpec((1, d, FF_TILE), lambda j, i: (j, 0, 0)),
         pl.BlockSpec((1, d, FF_TILE), lambda j, i: (j + 2, 0, 0)),
         pl.BlockSpec((1, FF_TILE, d), lambda j, i: (j, 0, 0))],
        [ffb, ffb, ffb, pl.BlockSpec((1, tm, d), lambda j, i: (j, i, 0))],
        [jax.ShapeDtypeStruct((t, D_FF), BF16)] * 3 + [jax.ShapeDtypeStruct((2, t, d), F32)],
        [], (dy, gpre, upre, wgu4, wgu4, wd2))

    def norm_body(x_ref, g_ref, p_ref, dy_ref, dx_ref, dgain_ref, xn_ref):
        @pl.when(pl.program_id(0) == 0)
        def _():
            dgain_ref[...] = jnp.zeros_like(dgain_ref)

        xv = x_ref[...]
        r = _rstd(xv, d)
        xn_ref[...] = (xv * r * g_ref[...]).astype(BF16)
        dx, dgr = _rms_vjp(xv, r, g_ref[...], p_ref[0] + p_ref[1], d)
        dx_ref[...] = dy_ref[...] + dx
        dgain_ref[...] += jnp.sum(dgr, axis=0, keepdims=True)

    nrow = pl.BlockSpec((tm, d), lambda i: (i, 0))
    vec = pl.BlockSpec((1, d), lambda i: (0, 0))
    dx, dgain, xn = pl.pallas_call(
        norm_body, name=name + "_norm", grid=(t // tm,),
        in_specs=[nrow, vec, pl.BlockSpec((2, tm, d), lambda i: (0, i, 0)), nrow],
        out_specs=[nrow, vec, nrow],
        out_shape=[jax.ShapeDtypeStruct((t, d), F32), jax.ShapeDtypeStruct((1, d), F32),
                   jax.ShapeDtypeStruct((t, d), BF16)],
        compiler_params=_cparams())(x, g, parts, dy)
    return [dx, dgain, xn, dgt, dup, act] + got


def _loss_head(y, tgt, name):
    t, d = y.shape
    tm = _row_tile(t, 512)

    def body(y_ref, t_ref, dy_ref, loss_ref):
        @pl.when(pl.program_id(0) == 0)
        def _():
            loss_ref[...] = jnp.zeros_like(loss_ref)

        e = y_ref[...] - t_ref[...]
        dy_ref[...] = e * (1.0 / d)
        part = 0.5 * jnp.sum(jnp.sum(e * e, axis=-1, keepdims=True) * (1.0 / d), axis=0, keepdims=True)
        loss_ref[...] += jnp.broadcast_to(part, loss_ref.shape)

    row = pl.BlockSpec((tm, d), lambda i: (i, 0))
    return pl.pallas_call(
        body, name=name, grid=(t // tm,), in_specs=[row, row],
        out_specs=[row, pl.BlockSpec((1, LANES), lambda i: (0, 0))],
        out_shape=[jax.ShapeDtypeStruct((t, d), F32), jax.ShapeDtypeStruct((1, LANES), F32)],
        compiler_params=_cparams())(y, tgt)


def _sgu_layernorm(vpre, lg, lb):
    v = _gelu(vpre)
    mu = jnp.mean(v, axis=-1, keepdims=True)
    xc = v - mu
    rstd = lax.rsqrt(jnp.mean(xc * xc, axis=-1, keepdims=True) + EPS)
    xhat = xc * rstd
    return xhat, rstd, xhat * lg + lb


def _sgu_fwd(zuv, lg, lb, wt, bias_l, name):
    t = zuv.shape[0]
    tm = _row_tile(t, 512)

    def body(u_ref, v_ref, lg_ref, lb_ref, wt_ref, bl_ref, o_ref, vln_scr):
        _, _, vln = _sgu_layernorm(v_ref[...], lg_ref[...], lb_ref[...])
        vln_scr[...] = vln.astype(BF16)
        lo = lax.broadcasted_iota(jnp.int32, (CHUNK, LANES), 1) < 64
        for c in range(tm // CHUNK):
            rows = slice(c * CHUNK, (c + 1) * CHUNK)
            for p in range(SG_GROUPS // 2):
                cols = slice(p * LANES, (p + 1) * LANES)
                vp = vln_scr[rows, cols]
                mixed = jnp.where(lo, _dot(wt_ref[2 * p], vp), _dot(wt_ref[2 * p + 1], vp)) + bl_ref[:, cols]
                o_ref[rows, cols] = (_gelu(u_ref[rows, cols]) * mixed).astype(BF16)

    half = lambda k: pl.BlockSpec((tm, SG_WIDTH), lambda i: (i, k))
    vec = pl.BlockSpec((1, SG_WIDTH), lambda i: (0, 0))
    return pl.pallas_call(
        body, name=name, grid=(t // tm,),
        in_specs=[half(0), half(1), vec, vec,
                  pl.BlockSpec((SG_GROUPS, CHUNK, CHUNK), lambda i: (0, 0, 0)),
                  pl.BlockSpec((CHUNK, SG_WIDTH), lambda i: (0, 0))],
        out_specs=pl.BlockSpec((tm, SG_WIDTH), lambda i: (i, 0)),
        out_shape=jax.ShapeDtypeStruct((t, SG_WIDTH), BF16),
        scratch_shapes=[pltpu.VMEM((tm, SG_WIDTH), BF16)],
        compiler_params=_cparams())(zuv, zuv, lg, lb, wt, bias_l)


def _sgu_bwd(zuv, dya, lg, lb, wt, wt_t, bias_l, name):
    t = zuv.shape[0]
    tm = _row_tile(t, 256)
    nsteps = t // tm

    def body(u_ref, v_ref, dy_ref, lg_ref, lb_ref, wt_ref, wtt_ref, bl_ref,
             dz_ref, dwt_ref, dbl_ref, dlg_ref, dlb_ref, vln_scr, dvln_scr, dbacc_scr):
        step = pl.program_id(0)

        @pl.when(step == 0)
        def _():
            dwt_ref[...] = jnp.zeros_like(dwt_ref)
            dlg_ref[...] = jnp.zeros_like(dlg_ref)
            dlb_ref[...] = jnp.zeros_like(dlb_ref)
            dbl_ref[...] = jnp.zeros_like(dbl_ref)
            dbacc_scr[...] = jnp.zeros_like(dbacc_scr)

        vpre = v_ref[...]
        lgv = lg_ref[...]
        xhat, rstd, vln = _sgu_layernorm(vpre, lgv, lb_ref[...])
        vln_scr[...] = vln.astype(BF16)
        lo = lax.broadcasted_iota(jnp.int32, (CHUNK, LANES), 1) < 64
        for c in range(tm // CHUNK):
            rows = slice(c * CHUNK, (c + 1) * CHUNK)
            for p in range(SG_GROUPS // 2):
                cols = slice(p * LANES, (p + 1) * LANES)
                vp = vln_scr[rows, cols]
                mixed = jnp.where(lo, _dot(wt_ref[2 * p], vp), _dot(wt_ref[2 * p + 1], vp)) + bl_ref[:, cols]
                upre = u_ref[rows, cols]
                dyp = dy_ref[rows, cols]
                dz_ref[rows, cols] = (dyp * mixed * _gelu_grad(upre)).astype(BF16)
                dm = dyp * _gelu(upre)
                dbacc_scr[:, cols] += dm
                dlo = jnp.where(lo, dm, 0.0).astype(BF16)
                dhi = jnp.where(lo, 0.0, dm).astype(BF16)
                dvln_scr[rows, cols] = _dot(wtt_ref[2 * p], dlo) + _dot(wtt_ref[2 * p + 1], dhi)
                dwt_ref[2 * p] += _dot_nt(dlo, vp)
                dwt_ref[2 * p + 1] += _dot_nt(dhi, vp)
        dvln = dvln_scr[...]
        dlg_ref[...] += jnp.sum(dvln * xhat, axis=0, keepdims=True)
        dlb_ref[...] += jnp.sum(dvln, axis=0, keepdims=True)
        dxh = dvln * lgv
        dv = rstd * (dxh - jnp.mean(dxh, axis=-1, keepdims=True)
                     - xhat * jnp.mean(dxh * xhat, axis=-1, keepdims=True))
        dz_ref[:, SG_WIDTH:] = (dv * _gelu_grad(vpre)).astype(BF16)

        @pl.when(step == nsteps - 1)
        def _():
            rr = lax.broadcasted_iota(jnp.int32, (CHUNK, CHUNK), 0)
            cc = lax.broadcasted_iota(jnp.int32, (CHUNK, CHUNK), 1)
            tril = (cc <= rr).astype(F32)
            for gidx in range(SG_GROUPS):
                dwt_ref[gidx] = dwt_ref[gidx] * tril
            kk = lax.broadcasted_iota(jnp.int32, (SG_WIDTH, LANES), 0)
            gg = lax.broadcasted_iota(jnp.int32, (SG_WIDTH, LANES), 1)
            sel = ((kk // 64) == gg).astype(F32)
            dbl_ref[...] = jnp.dot(dbacc_scr[...], sel, preferred_element_type=F32,
                                   precision=lax.Precision.HIGHEST)

    half = lambda k: pl.BlockSpec((tm, SG_WIDTH), lambda i: (i, k))
    vec = pl.BlockSpec((1, SG_WIDTH), lambda i: (0, 0))
    wspec = pl.BlockSpec((SG_GROUPS, CHUNK, CHUNK), lambda i: (0, 0, 0))
    return pl.pallas_call(
        body, name=name, grid=(nsteps,),
        in_specs=[half(0), half(1), pl.BlockSpec((tm, SG_WIDTH), lambda i: (i, 0)), vec, vec,
                  wspec, wspec, pl.BlockSpec((CHUNK, SG_WIDTH), lambda i: (0, 0))],
        out_specs=[pl.BlockSpec((tm, 2 * SG_WIDTH), lambda i: (i, 0)), wspec,
                   pl.BlockSpec((CHUNK, LANES), lambda i: (0, 0)), vec, vec],
        out_shape=[jax.ShapeDtypeStruct((t, 2 * SG_WIDTH), BF16),
                   jax.ShapeDtypeStruct((SG_GROUPS, CHUNK, CHUNK), F32),
                   jax.ShapeDtypeStruct((CHUNK, LANES), F32),
                   jax.ShapeDtypeStruct((1, SG_WIDTH), F32), jax.ShapeDtypeStruct((1, SG_WIDTH), F32)],
        scratch_shapes=[pltpu.VMEM((tm, SG_WIDTH), BF16), pltpu.VMEM((tm, SG_WIDTH), F32),
                        pltpu.VMEM((CHUNK, SG_WIDTH), F32)],
        compiler_params=_cparams())(zuv, zuv, dya, lg, lb, wt, wt_t, bias_l)


def _rope(x, c, s1, s2):
    return x * c + pltpu.roll(x, LANES - 16, 1) * s1 + pltpu.roll(x, 16, 1) * s2


def _rope_t(dy, c, s1, s2):
    return dy * c + pltpu.roll(dy * s1, 16, 1) + pltpu.roll(dy * s2, LANES - 16, 1)


def _mla_prep_fwd(zcq, zckv, zkr, gcq, gckv, qg, kg, wuq, wuk, wuv, rc, rs1, rs2, name, ex=None):
    t = zcq.shape[0]
    tm = _row_tile(t, 256)
    hd = MLA_HEADS * LANES

    def body(zcq_ref, zckv_ref, zkr_ref, gcq_ref, gckv_ref, qg_ref, kg_ref, wuq_ref, wuk_ref, wuv_ref,
             c_ref, s1_ref, s2_ref, q_ref, k_ref, v_ref, cqn_ref, ckvn_ref):
        c, s1, s2 = c_ref[...], s1_ref[...], s2_ref[...]
        xq = zcq_ref[...]
        cqn = (xq * _rstd(xq, MLA_Q_RANK) * gcq_ref[...]).astype(BF16)
        cqn_ref[...] = cqn
        ql = _dot(cqn, wuq_ref[...])
        xk = zckv_ref[...]
        ckvn = (xk * _rstd(xk, MLA_KV_RANK) * gckv_ref[...]).astype(BF16)
        ckvn_ref[...] = ckvn
        kl = _dot(ckvn, wuk_ref[...])
        v_ref[...] = _dot(ckvn, wuv_ref[...]).astype(BF16)
        kr = zkr_ref[...]
        for h in range(MLA_HEADS):
            sl = slice(h * LANES, (h + 1) * LANES)
            qh = ql[:, sl]
            q_ref[:, sl] = (_rope(qh * _rstd(qh, MLA_QK) * qg_ref[...], c, s1, s2) * ATTN_SCALE2).astype(BF16)
            kh = kl[:, sl] + kr
            k_ref[:, sl] = _rope(kh * _rstd(kh, MLA_QK) * kg_ref[...], c, s1, s2).astype(BF16)

    row = lambda n: pl.BlockSpec((tm, n), lambda i: (i, 0))
    full = lambda a: pl.BlockSpec(a.shape, lambda i: (0, 0))
    return _call_with_exchange(
        ex, body, name, (t // tm,),
        [row(MLA_Q_RANK), row(MLA_KV_RANK), row(LANES), full(gcq), full(gckv), full(qg), full(kg),
         full(wuq), full(wuk), full(wuv), row(LANES), row(LANES), row(LANES)],
        [row(hd), row(hd), row(hd), row(MLA_Q_RANK), row(MLA_KV_RANK)],
        [jax.ShapeDtypeStruct((t, hd), BF16)] * 3
        + [jax.ShapeDtypeStruct((t, MLA_Q_RANK), BF16), jax.ShapeDtypeStruct((t, MLA_KV_RANK), BF16)],
        [], (zcq, zckv, zkr, gcq, gckv, qg, kg, wuq, wuk, wuv, rc, rs1, rs2))


def _mla_prep_bwd(zcq, zckv, zkr, gcq, gckv, qg, kg, wuq, wuk, wuv, rc, rs1, rs2, dq, dk, dv, name):
    t = zcq.shape[0]
    tm = _row_tile(t, 256)
    hd = MLA_HEADS * LANES

    def body(zcq_ref, zckv_ref, zkr_ref, gcq_ref, gckv_ref, qg_ref, kg_ref, wuq_ref, wuk_ref, wuv_ref,
             c_ref, s1_ref, s2_ref, dq_ref, dk_ref, dv_ref,
             dzcq_ref, dzckv_ref, dzkr_ref, dql_ref, dkl_ref, dgcq_ref, dgckv_ref, dqg_ref, dkg_ref):
        @pl.when(pl.program_id(0) == 0)
        def _():
            for ref in (dgcq_ref, dgckv_ref, dqg_ref, dkg_ref):
                ref[...] = jnp.zeros_like(ref)

        c, s1, s2 = c_ref[...], s1_ref[...], s2_ref[...]
        qgv, kgv = qg_ref[...], kg_ref[...]
        xq = zcq_ref[...]
        rq = _rstd(xq, MLA_Q_RANK)
        ql = _dot((xq * rq * gcq_ref[...]).astype(BF16), wuq_ref[...])
        xk = zckv_ref[...]
        rk = _rstd(xk, MLA_KV_RANK)
        kl = _dot((xk * rk * gckv_ref[...]).astype(BF16), wuk_ref[...])
        kr = zkr_ref[...]
        dqg_acc = jnp.zeros((tm, LANES), F32)
        dkg_acc = jnp.zeros((tm, LANES), F32)
        dkr = jnp.zeros((tm, LANES), F32)
        for h in range(MLA_HEADS):
            sl = slice(h * LANES, (h + 1) * LANES)
            qh = ql[:, sl]
            dqh, dgr = _rms_vjp(qh, _rstd(qh, MLA_QK), qgv, _rope_t(dq_ref[:, sl], c, s1, s2), MLA_QK)
            dql_ref[:, sl] = dqh.astype(BF16)
            dqg_acc += dgr
            kh = kl[:, sl] + kr
            dkh, dgr = _rms_vjp(kh, _rstd(kh, MLA_QK), kgv, _rope_t(dk_ref[:, sl], c, s1, s2), MLA_QK)
            dkl_ref[:, sl] = dkh.astype(BF16)
            dkg_acc += dgr
            dkr += dkh
        dqg_ref[...] += jnp.sum(dqg_acc, axis=0, keepdims=True)
        dkg_ref[...] += jnp.sum(dkg_acc, axis=0, keepdims=True)
        lane = lax.broadcasted_iota(jnp.int32, (tm, LANES), 1)
        dzkr_ref[...] = jnp.where((lane >= MLA_NOPE) & (lane < MLA_QK), dkr, 0.0).astype(BF16)
        dcqn = _dot_nt(dql_ref[...], wuq_ref[...])
        dx, dgr = _rms_vjp(xq, rq, gcq_ref[...], dcqn, MLA_Q_RANK)
        dzcq_ref[...] = dx.astype(BF16)
        dgcq_ref[...] += jnp.sum(dgr, axis=0, keepdims=True)
        dckvn = _dot_nt(dkl_ref[...], wuk_ref[...]) + _dot_nt(dv_ref[...].astype(BF16), wuv_ref[...])
        dx, dgr = _rms_vjp(xk, rk, gckv_ref[...], dckvn, MLA_KV_RANK)
        dzckv_ref[...] = dx.astype(BF16)
        dgckv_ref[...] += jnp.sum(dgr, axis=0, keepdims=True)

    row = lambda n: pl.BlockSpec((tm, n), lambda i: (i, 0))
    full = lambda a: pl.BlockSpec(a.shape, lambda i: (0, 0))
    vec = lambda n: pl.BlockSpec((1, n), lambda i: (0, 0))
    return pl.pallas_call(
        body, name=name, grid=(t // tm,),
        in_specs=[row(MLA_Q_RANK), row(MLA_KV_RANK), row(LANES), full(gcq), full(gckv), full(qg), full(kg),
                  full(wuq), full(wuk), full(wuv), row(LANES), row(LANES), row(LANES), row(hd), row(hd), row(hd)],
        out_specs=[row(MLA_Q_RANK), row(MLA_KV_RANK), row(LANES), row(hd), row(hd),
                   vec(MLA_Q_RANK), vec(MLA_KV_RANK), vec(LANES), vec(LANES)],
        out_shape=[jax.ShapeDtypeStruct((t, MLA_Q_RANK), BF16), jax.ShapeDtypeStruct((t, MLA_KV_RANK), BF16),
                   jax.ShapeDtypeStruct((t, LANES), BF16), jax.ShapeDtypeStruct((t, hd), BF16),
                   jax.ShapeDtypeStruct((t, hd), BF16), jax.ShapeDtypeStruct((1, MLA_Q_RANK), F32),
                   jax.ShapeDtypeStruct((1, MLA_KV_RANK), F32), jax.ShapeDtypeStruct((1, LANES), F32),
                   jax.ShapeDtypeStruct((1, LANES), F32)],
        compiler_params=_cparams(),
    )(zcq, zckv, zkr, gcq, gckv, qg, kg, wuq, wuk, wuv, rc, rs1, rs2, dq, dk, dv)


def _attn_tiles(t):
    tq = 512 if t >= 2048 else 128
    return tq, min(t, 4 * tq), min(t, 2 * tq)


def _causal_keep(tq, tk, i, j):
    row = lax.broadcasted_iota(jnp.int32, (tq, tk), 0)
    col = lax.broadcasted_iota(jnp.int32, (tq, tk), 1)
    return (col - row) <= (i * tq - j * tk)


def _causal_keep_t(tq, tk, i, j):
    key = lax.broadcasted_iota(jnp.int32, (tk, tq), 0)
    qry = lax.broadcasted_iota(jnp.int32, (tk, tq), 1)
    return (key - qry) <= (i * tq - j * tk)


ATTN_FWD_HEADS_PER_STEP = 2
ATTN_BWD_HEADS_PER_STEP = 2


def _attn_fwd(q, k, v, name):
    t, hd = q.shape
    hp = ATTN_FWD_HEADS_PER_STEP
    tq, tk, _ = _attn_tiles(t)
    pairs = [(i, j) for i in range(t // tq) for j in range(((i + 1) * tq - 1) // tk + 1)]
    ii = np.array([p[0] for p in pairs], np.int32)
    jj = np.array([p[1] for p in pairs], np.int32)

    def body(ii_ref, jj_ref, q_ref, k_ref, v_ref, o_ref, lse_ref, m_scr, l_scr, acc_scr):
        s_id = pl.program_id(1)
        i, j = ii_ref[s_id], jj_ref[s_id]
        last = j == ((i + 1) * tq - 1) // tk

        @pl.when(j == 0)
        def _():
            m_scr[...] = jnp.full_like(m_scr, NEG)
            l_scr[...] = jnp.zeros_like(l_scr)
            acc_scr[...] = jnp.zeros_like(acc_scr)

        def step(masked):
            for hh in range(hp):
                sl = slice(hh * LANES, (hh + 1) * LANES)
                s = _dot_nt(q_ref[:, sl], k_ref[:, sl])
                if masked:
                    s = jnp.where(_causal_keep(tq, tk, i, j), s, NEG)
                m_prev = m_scr[hh]
                m_new = jnp.maximum(m_prev, jnp.max(s, axis=1, keepdims=True))
                p = jnp.exp2(s - m_new)
                alpha = jnp.exp2(m_prev - m_new)
                l_new = alpha * l_scr[hh] + jnp.sum(p, axis=1, keepdims=True)
                acc = alpha * acc_scr[:, sl] + _dot(p.astype(BF16), v_ref[:, sl])
                if masked:
                    o_ref[:, sl] = (acc / l_new).astype(BF16)
                    lse_ref[:, sl] = jnp.broadcast_to(m_new + jnp.log(l_new) * LOG2E, (tq, LANES))
                else:
                    l_scr[hh] = l_new
                    acc_scr[:, sl] = acc
                    m_scr[hh] = m_new

        @pl.when(jnp.logical_not(last))
        def _():
            step(False)

        @pl.when(last)
        def _():
            step(True)

    w = hp * LANES
    qspec = pl.BlockSpec((tq, w), lambda h, s, ii_r, jj_r: (ii_r[s], h))
    kspec = pl.BlockSpec((tk, w), lambda h, s, ii_r, jj_r: (jj_r[s], h))
    return pl.pallas_call(
        body, name=name,
        grid_spec=pltpu.PrefetchScalarGridSpec(
            num_scalar_prefetch=2, grid=(hd // w, len(pairs)), in_specs=[qspec, kspec, kspec],
            out_specs=[qspec, qspec],
            scratch_shapes=[pltpu.VMEM((hp, tq, 1), F32), pltpu.VMEM((hp, tq, 1), F32),
                            pltpu.VMEM((tq, w), F32)]),
        out_shape=[jax.ShapeDtypeStruct((t, hd), BF16), jax.ShapeDtypeStruct((t, hd), F32)],
        compiler_params=_cparams())(jnp.asarray(ii), jnp.asarray(jj), q, k, v)


def _attn_bwd_rows(o, lse, do, name):
    t, hd = o.shape
    heads = hd // LANES
    tm = _row_tile(t, 512)

    def body(o_ref, lse_ref, do_ref, out_ref):
        lane = lax.broadcasted_iota(jnp.int32, (tm, LANES), 1)
        acc = jnp.zeros((tm, LANES), F32)
        for h in range(heads):
            sl = slice(h * LANES, (h + 1) * LANES)
            delta = jnp.sum(do_ref[:, sl].astype(F32) * o_ref[:, sl].astype(F32), axis=1, keepdims=True)
            acc = jnp.where(lane == h, delta, acc)
            acc = jnp.where(lane == heads + h, lse_ref[:, sl], acc)
        out_ref[...] = acc

    row = pl.BlockSpec((tm, hd), lambda i: (i, 0))
    cols = pl.pallas_call(
        body, name=name, grid=(t // tm,), in_specs=[row, row, row],
        out_specs=pl.BlockSpec((tm, LANES), lambda i: (i, 0)),
        out_shape=jax.ShapeDtypeStruct((t, LANES), F32), compiler_params=_cparams())(o, lse, do)
    rows = cols.T
    return rows[:heads].reshape(heads, 1, t), rows[heads:2 * heads].reshape(heads, 1, t)


def _attn_bwd(q, k, v, delta_rows, lse_rows, do, name):
    t, hd = q.shape
    hp = ATTN_BWD_HEADS_PER_STEP
    tq, _, tk = _attn_tiles(t)
    nq = t // tq
    pairs = [(i, j) for j in range(t // tk) for i in range((j * tk) // tq, nq)]
    ii = np.array([p[0] for p in pairs], np.int32)
    jj = np.array([p[1] for p in pairs], np.int32)

    def body(jj_ref, ii_ref, q_ref, k_ref, v_ref, delta_ref, lse_ref, do_ref, dq_ref, dk_ref, dv_ref,
             dk_scr, dv_scr):
        s_id = pl.program_id(1)
        i, j = ii_ref[s_id], jj_ref[s_id]

        @pl.when(s_id == 0)
        def _():
            dq_ref[...] = jnp.zeros_like(dq_ref)

        @pl.when(i == (j * tk) // tq)
        def _():
            dk_scr[...] = jnp.zeros_like(dk_scr)
            dv_scr[...] = jnp.zeros_like(dv_scr)

        rows = pl.ds(pl.multiple_of(i * tq, tq), tq)

        def step(masked):
            for hh in range(hp):
                sl = slice(hh * LANES, (hh + 1) * LANES)
                qv, kv, dov = q_ref[:, sl], k_ref[:, sl], do_ref[:, sl]
                st = _dot_nt(kv, qv)
                if masked:
                    st = jnp.where(_causal_keep_t(tq, tk, i, j), st, NEG)
                pt = jnp.exp2(st - lse_ref[hh])
                dv_scr[:, sl] += _dot(pt.astype(BF16), dov)
                dpt = _dot_nt(v_ref[:, sl], dov)
                dst = (pt * (dpt - delta_ref[hh]) * ATTN_SCALE).astype(BF16)
                dk_scr[:, sl] += _dot(dst, qv)
                dq_ref[rows, sl] += _dot_tn(dst, kv)

        crosses = (j + 1) * tk - 1 > i * tq

        @pl.when(jnp.logical_not(crosses))
        def _():
            step(False)

        @pl.when(crosses)
        def _():
            step(True)

        @pl.when(i == nq - 1)
        def _():
            dk_ref[...] = dk_scr[...] * (1.0 / ATTN_SCALE2)
            dv_ref[...] = dv_scr[...]

    w = hp * LANES
    qspec = pl.BlockSpec((tq, w), lambda h, s, jj_r, ii_r: (ii_r[s], h))
    kspec = pl.BlockSpec((tk, w), lambda h, s, jj_r, ii_r: (jj_r[s], h))
    rspec = pl.BlockSpec((hp, 1, tq), lambda h, s, jj_r, ii_r: (h, 0, ii_r[s]))
    return pl.pallas_call(
        body, name=name,
        grid_spec=pltpu.PrefetchScalarGridSpec(
            num_scalar_prefetch=2, grid=(hd // w, len(pairs)),
            in_specs=[qspec, kspec, kspec, rspec, rspec, qspec],
            out_specs=[pl.BlockSpec((t, w), lambda h, s, jj_r, ii_r: (0, h)), kspec, kspec],
            scratch_shapes=[pltpu.VMEM((tk, w), F32), pltpu.VMEM((tk, w), F32)]),
        out_shape=[jax.ShapeDtypeStruct((t, hd), F32)] * 3,
        compiler_params=_cparams())(jnp.asarray(jj), jnp.asarray(ii), q, k, v, delta_rows, lse_rows, do)


MEM_W = MEM_HEADS * LANES


def _mem_kv_fwd(mem, gmem, wkv, kg, name):
    m, d = mem.shape

    def body(mem_ref, g_ref, w_ref, kg_ref, k_ref, v_ref, mn_ref):
        xv = mem_ref[...]
        mn = (xv * _rstd(xv, d) * g_ref[...]).astype(BF16)
        mn_ref[...] = mn
        kvm = _dot(mn, w_ref[...])
        v_ref[...] = kvm[:, MEM_W:].astype(BF16)
        for h in range(MEM_HEADS):
            sl = slice(h * LANES, (h + 1) * LANES)
            kh = kvm[:, sl]
            k_ref[:, sl] = (kh * _rstd(kh, LANES) * kg_ref[...]).astype(BF16)

    full = lambda a: pl.BlockSpec(a.shape, lambda i: (0, 0))
    return pl.pallas_call(
        body, name=name, grid=(1,), in_specs=[full(mem), full(gmem), full(wkv), full(kg)],
        out_specs=[pl.BlockSpec((m, MEM_W), lambda i: (0, 0)), pl.BlockSpec((m, MEM_W), lambda i: (0, 0)),
                   pl.BlockSpec((m, d), lambda i: (0, 0))],
        out_shape=[jax.ShapeDtypeStruct((m, MEM_W), BF16), jax.ShapeDtypeStruct((m, MEM_W), BF16),
                   jax.ShapeDtypeStruct((m, d), BF16)],
        compiler_params=_cparams())(mem, gmem, wkv, kg)


def _mem_softmax(qn, kh):
    s = _dot_nt(qn, kh) * (LANES ** -0.5)
    e = jnp.exp(s - jnp.max(s, axis=1, keepdims=True))
    return e / jnp.sum(e, axis=1, keepdims=True)


def _mem_attn_fwd(zqm, qg, km, vm, name):
    t = zqm.shape[0]
    tm = _row_tile(t, 512)

    def body(q_ref, qg_ref, k_ref, v_ref, o_ref):
        for h in range(MEM_HEADS):
            sl = slice(h * LANES, (h + 1) * LANES)
            qh = q_ref[:, sl]
            qn = (qh * _rstd(qh, LANES) * qg_ref[...]).astype(BF16)
            p = _mem_softmax(qn, k_ref[:, sl])
            o_ref[:, sl] = _dot(p.astype(BF16), v_ref[:, sl]).astype(BF16)

    row = pl.BlockSpec((tm, MEM_W), lambda i: (i, 0))
    full = lambda a: pl.BlockSpec(a.shape, lambda i: (0, 0))
    return pl.pallas_call(
        body, name=name, grid=(t // tm,), in_specs=[row, full(qg), full(km), full(vm)], out_specs=row,
        out_shape=jax.ShapeDtypeStruct((t, MEM_W), BF16), compiler_params=_cparams())(zqm, qg, km, vm)


def _mem_attn_bwd(zqm, dyc, qg, km, vm, name):
    t = zqm.shape[0]
    m = km.shape[0]
    tm = _row_tile(t, 256)

    def body(q_ref, dy_ref, qg_ref, k_ref, v_ref, dz_ref, dk_ref, dv_ref, dqg_ref):
        @pl.when(pl.program_id(0) == 0)
        def _():
            dk_ref[...] = jnp.zeros_like(dk_ref)
            dv_ref[...] = jnp.zeros_like(dv_ref)
            dqg_ref[...] = jnp.zeros_like(dqg_ref)

        qgv = qg_ref[...]
        dqg_acc = jnp.zeros((tm, LANES), F32)
        for h in range(MEM_HEADS):
            sl = slice(h * LANES, (h + 1) * LANES)
            qh = q_ref[:, sl]
            r = _rstd(qh, LANES)
            qn = (qh * r * qgv).astype(BF16)
            kh = k_ref[:, sl]
            p = _mem_softmax(qn, kh)
            dov = dy_ref[:, sl]
            dv_ref[:, sl] += _dot_tn(p.astype(BF16), dov)
            dp = _dot_nt(dov, v_ref[:, sl])
            ds = (p * (dp - jnp.sum(dp * p, axis=1, keepdims=True)) * (LANES ** -0.5)).astype(BF16)
            dk_ref[:, sl] += _dot_tn(ds, qn)
            dqh, dgr = _rms_vjp(qh, r, qgv, _dot(ds, kh), LANES)
            dz_ref[:, sl] = dqh.astype(BF16)
            dqg_acc += dgr
        dqg_ref[...] += jnp.sum(dqg_acc, axis=0, keepdims=True)

    row = pl.BlockSpec((tm, MEM_W), lambda i: (i, 0))
    full = lambda a: pl.BlockSpec(a.shape, lambda i: (0, 0))
    acc = pl.BlockSpec((m, MEM_W), lambda i: (0, 0))
    return pl.pallas_call(
        body, name=name, grid=(t // tm,), in_specs=[row, row, full(qg), full(km), full(vm)],
        out_specs=[row, acc, acc, pl.BlockSpec((1, LANES), lambda i: (0, 0))],
        out_shape=[jax.ShapeDtypeStruct((t, MEM_W), BF16), jax.ShapeDtypeStruct((m, MEM_W), F32),
                   jax.ShapeDtypeStruct((m, MEM_W), F32), jax.ShapeDtypeStruct((1, LANES), F32)],
        compiler_params=_cparams())(zqm, dyc, qg, km, vm)


def _mem_kv_bwd(mem, gmem, wkv, kg, dkn, dvm, name):
    m, d = mem.shape

    def body(mem_ref, g_ref, w_ref, kg_ref, dk_ref, dv_ref, dw_ref, dkg_ref, dg_ref, dkv_scr):
        xv = mem_ref[...]
        r = _rstd(xv, d)
        mn = (xv * r * g_ref[...]).astype(BF16)
        kvm = _dot(mn, w_ref[...])
        dkv_scr[:, MEM_W:] = dv_ref[...].astype(BF16)
        dkg_acc = jnp.zeros((m, LANES), F32)
        for h in range(MEM_HEADS):
            sl = slice(h * LANES, (h + 1) * LANES)
            kh = kvm[:, sl]
            dkh, dgr = _rms_vjp(kh, _rstd(kh, LANES), kg_ref[...], dk_ref[:, sl], LANES)
            dkv_scr[:, sl] = dkh.astype(BF16)
            dkg_acc += dgr
        dkg_ref[...] = jnp.sum(dkg_acc, axis=0, keepdims=True)
        dkv = dkv_scr[...]
        dw_ref[...] = _dot_tn(mn, dkv)
        dmn = _dot_nt(dkv, w_ref[...])
        dg_ref[...] = jnp.sum(dmn * xv * r, axis=0, keepdims=True)

    full = lambda a: pl.BlockSpec(a.shape, lambda i: (0, 0))
    return pl.pallas_call(
        body, name=name, grid=(1,),
        in_specs=[full(mem), full(gmem), full(wkv), full(kg), full(dkn), full(dvm)],
        out_specs=[pl.BlockSpec((d, 2 * MEM_W), lambda i: (0, 0)), pl.BlockSpec((1, LANES), lambda i: (0, 0)),
                   pl.BlockSpec((1, d), lambda i: (0, 0))],
        out_shape=[jax.ShapeDtypeStruct((d, 2 * MEM_W), F32), jax.ShapeDtypeStruct((1, LANES), F32),
                   jax.ShapeDtypeStruct((1, d), F32)],
        scratch_shapes=[pltpu.VMEM((m, 2 * MEM_W), BF16)],
        compiler_params=_cparams())(mem, gmem, wkv, kg, dkn, dvm)


def _merge_fwd(x1, ya, yb, yc, zg, bg, wa, wb, wc, wo, name):
    t, d = x1.shape
    tm = _row_tile(t, 256)

    def body(x_ref, ya_ref, yb_ref, yc_ref, zg_ref, bg_ref, wa_ref, wb_ref, wc_ref, wo_ref,
             x2_ref, mg_ref, pa_ref, pb_ref, pc_ref):
        merged = None
        for k, (y_ref, w_ref, p_ref) in enumerate(
                ((ya_ref, wa_ref, pa_ref), (yb_ref, wb_ref, pb_ref), (yc_ref, wc_ref, pc_ref))):
            sl = slice(k * d, (k + 1) * d)
            pr = _dot(y_ref[...], w_ref[...])
            p_ref[...] = pr.astype(BF16)
            term = jax.nn.sigmoid(zg_ref[:, sl] + bg_ref[:, sl]) * pr
            merged = term if merged is None else merged + term
        mb = merged.astype(BF16)
        mg_ref[...] = mb
        x2_ref[...] = x_ref[...] + _dot(mb, wo_ref[...])

    row = lambda n: pl.BlockSpec((tm, n), lambda i: (i, 0))
    full = lambda a: pl.BlockSpec(a.shape, lambda i: (0, 0))
    return pl.pallas_call(
        body, name=name, grid=(t // tm,),
        in_specs=[row(d), row(ya.shape[1]), row(yb.shape[1]), row(yc.shape[1]), row(3 * d), full(bg),
                  full(wa), full(wb), full(wc), full(wo)],
        out_specs=[row(d)] * 5,
        out_shape=[jax.ShapeDtypeStruct((t, d), F32)] + [jax.ShapeDtypeStruct((t, d), BF16)] * 4,
        compiler_params=_cparams())(x1, ya, yb, yc, zg, bg, wa, wb, wc, wo)


def _merge_bwd(dx2, pa, pb, pc, zg, bg, wa, wb, wc, wo, name, ex=None):
    t, d = dx2.shape
    tm = _row_tile(t, 256)

    def body(dx_ref, pa_ref, pb_ref, pc_ref, zg_ref, bg_ref, wa_ref, wb_ref, wc_ref, wo_ref,
             dpa_ref, dpb_ref, dpc_ref, dzg_ref, dbg_ref, dya_ref, dyb_ref, dyc_ref):
        @pl.when(pl.program_id(0) == 0)
        def _():
            dbg_ref[...] = jnp.zeros_like(dbg_ref)

        dm = _dot_nt(dx_ref[...].astype(BF16), wo_ref[...])
        for k, (p_ref, w_ref, dp_ref, dy_ref) in enumerate(
                ((pa_ref, wa_ref, dpa_ref, dya_ref), (pb_ref, wb_ref, dpb_ref, dyb_ref),
                 (pc_ref, wc_ref, dpc_ref, dyc_ref))):
            sl = slice(k * d, (k + 1) * d)
            gate = jax.nn.sigmoid(zg_ref[:, sl] + bg_ref[:, sl])
            dpr = (dm * gate).astype(BF16)
            dp_ref[...] = dpr
            dzg = dm * p_ref[...].astype(F32) * gate * (1.0 - gate)
            dzg_ref[:, sl] = dzg.astype(BF16)
            dbg_ref[:, sl] += jnp.sum(dzg, axis=0, keepdims=True)
            dy_ref[...] = _dot_nt(dpr, w_ref[...]).astype(dy_ref.dtype)

    row = lambda n: pl.BlockSpec((tm, n), lambda i: (i, 0))
    full = lambda a: pl.BlockSpec(a.shape, lambda i: (0, 0))
    na, nb, nc = wa.shape[0], wb.shape[0], wc.shape[0]
    return _call_with_exchange(
        ex, body, name, (t // tm,),
        [row(d), row(d), row(d), row(d), row(3 * d), full(bg), full(wa), full(wb), full(wc), full(wo)],
        [row(d), row(d), row(d), row(3 * d), pl.BlockSpec((1, 3 * d), lambda i: (0, 0)), row(na), row(nb), row(nc)],
        [jax.ShapeDtypeStruct((t, d), BF16)] * 3
        + [jax.ShapeDtypeStruct((t, 3 * d), BF16), jax.ShapeDtypeStruct((1, 3 * d), F32),
           jax.ShapeDtypeStruct((t, na), F32), jax.ShapeDtypeStruct((t, nb), BF16),
           jax.ShapeDtypeStruct((t, nc), BF16)],
        [], (dx2, pa, pb, pc, zg, bg, wa, wb, wc, wo))


def _adamw_math(w, g, m, v):
    bc1 = 1.0 - ADAM_B1 ** ADAM_STEP
    bc2 = 1.0 - ADAM_B2 ** ADAM_STEP
    nm = ADAM_B1 * m + (1.0 - ADAM_B1) * g
    nv = ADAM_B2 * v + (1.0 - ADAM_B2) * (g * g)
    delta = -ADAM_LR * ((nm / bc1) / (jnp.sqrt(nv / bc2) + ADAM_EPS) + ADAM_WD * w)
    return delta, nm, nv


def _div_tile(n, cap, mult):
    best = None
    for cand in range(mult, min(n, cap) + 1, mult):
        if n % cand == 0:
            best = cand
    assert best is not None, (n, cap, mult)
    return best


def _adamw(w, g, m, v, name):
    rows, cols = w.shape
    tr = rows if rows * cols <= 256 * 1024 else _div_tile(rows, 256, 8)

    def body(w_ref, g_ref, m_ref, v_ref, d_ref, nm_ref, nv_ref):
        d_ref[...], nm_ref[...], nv_ref[...] = _adamw_math(w_ref[...], g_ref[...], m_ref[...], v_ref[...])

    blk = pl.BlockSpec((tr, cols), lambda i: (i, 0))
    return pl.pallas_call(
        body, name=name, grid=(rows // tr,), in_specs=[blk] * 4, out_specs=[blk] * 3,
        out_shape=[jax.ShapeDtypeStruct((rows, cols), F32)] * 3, compiler_params=_cparams())(w, g, m, v)


def _adamw_slots(w, slots, m, v, name):
    _, hr, cols = w.shape
    tr = _div_tile(hr, 128, 16)

    def body(w_ref, s_ref, m_ref, v_ref, g_ref, d_ref, nm_ref, nv_ref):
        g = s_ref[0, 0].astype(F32)
        for k in range(1, N_CHIPS):
            g = g + s_ref[0, k].astype(F32)
        g_ref[0] = g
        d_ref[0], nm_ref[0], nv_ref[0] = _adamw_math(w_ref[0], g, m_ref[0], v_ref[0])

    blk = pl.BlockSpec((1, tr, cols), lambda h, i: (h, i, 0))
    return pl.pallas_call(
        body, name=name, grid=(2, hr // tr),
        in_specs=[blk, pl.BlockSpec((1, N_CHIPS, tr, cols), lambda h, i: (h, 0, i, 0)), blk, blk],
        out_specs=[blk] * 4, out_shape=[jax.ShapeDtypeStruct((2, hr, cols), F32)] * 4,
        compiler_params=_cparams())(w, slots, m, v)


ANY = pl.BlockSpec(memory_space=pl.ANY)


def _place():
    x, y, c = lax.axis_index("x"), lax.axis_index("y"), lax.axis_index("c")
    other_chips = [(1 - x, y), (x, 1 - y), (1 - x, 1 - y)]
    return x, y, c, other_chips


def _remote(src, dst, send_sem, recv_sem, to):
    return pltpu.make_async_remote_copy(src_ref=src, dst_ref=dst, send_sem=send_sem, recv_sem=recv_sem,
                                        device_id=to, device_id_type=MESH)


def _gather_exchange(shards):
    nw = len(shards)

    def build(s_refs, g_refs, sems):
        send_sems, recv_sems, local_sems = sems
        x, y, c, chips = _place()
        me = 2 * x + y
        sibling = (x, y, 1 - c)
        mine = [pltpu.make_async_copy(s_refs[w], g_refs[w].at[me], local_sems.at[w]) for w in range(nw)]
        first = [_remote(s_refs[w].at[c], g_refs[w].at[me, c], send_sems.at[k, w], recv_sems.at[k, w], (cx, cy, c))
                 for k, (cx, cy) in enumerate(chips) for w in range(nw)]

        def start():
            for cp in mine + first:
                cp.start()

        def finish():
            passed = []
            for k, (cx, cy) in enumerate(chips):
                for w in range(nw):
                    slab = g_refs[w].at[2 * cx + cy, c]
                    _remote(slab, slab, send_sems.at[k, w], recv_sems.at[k, w], (cx, cy, c)).wait_recv()
                    fwd = _remote(slab, slab, send_sems.at[3 + k, w], recv_sems.at[3 + k, w], sibling)
                    fwd.start()
                    passed.append(fwd)
            for k, (cx, cy) in enumerate(chips):
                for w in range(nw):
                    slab = g_refs[w].at[2 * cx + cy, 1 - c]
                    _remote(slab, slab, send_sems.at[3 + k, w], recv_sems.at[3 + k, w], sibling).wait_recv()
            for cp in first + passed:
                cp.wait_send()
            for cp in mine:
                cp.wait()

        return start, finish

    return _Exchange(list(shards), [jax.ShapeDtypeStruct((N_CHIPS,) + s.shape, BF16) for s in shards],
                     [pltpu.SemaphoreType.DMA((6, nw)), pltpu.SemaphoreType.DMA((6, nw)),
                      pltpu.SemaphoreType.DMA((nw,))], build)


def _swap_halves(grads, name):
    nw = len(grads)

    def body(*refs):
        g_refs, sib_refs = refs[:nw], refs[nw:2 * nw]
        send_sems, recv_sems = refs[2 * nw:]
        x, y, c, _ = _place()
        copies = [_remote(g_refs[w].at[s, 1 - c], sib_refs[w].at[s], send_sems.at[s, w], recv_sems.at[s, w],
                          (x, y, 1 - c)) for w in range(nw) for s in range(N_CHIPS)]
        for cp in copies:
            cp.start()
        for cp in copies:
            cp.wait_recv()
        for cp in copies:
            cp.wait_send()

    return pl.pallas_call(
        body, name=name, in_specs=[ANY] * nw, out_specs=[ANY] * nw,
        out_shape=[jax.ShapeDtypeStruct((N_CHIPS,) + g.shape[2:], BF16) for g in grads],
        scratch_shapes=[pltpu.SemaphoreType.DMA((N_CHIPS, nw)), pltpu.SemaphoreType.DMA((N_CHIPS, nw))],
    )(*grads)


def _pair_sum(grad, sib, core, name):
    nchip, _, hr, cols = grad.shape
    tr = _div_tile(hr, 256, 16)

    def body(core_ref, a_ref, b_ref, o_ref):
        o_ref[...] = (a_ref[0].astype(F32) + b_ref[...].astype(F32)).astype(BF16)

    return pl.pallas_call(
        body, name=name,
        grid_spec=pltpu.PrefetchScalarGridSpec(
            num_scalar_prefetch=1, grid=(nchip, hr // tr),
            in_specs=[pl.BlockSpec((1, 1, tr, cols), lambda s, i, core_r: (s, core_r[0], i, 0)),
                      pl.BlockSpec((1, tr, cols), lambda s, i, core_r: (s, i, 0))],
            out_specs=pl.BlockSpec((1, tr, cols), lambda s, i, core_r: (s, i, 0))),
        out_shape=jax.ShapeDtypeStruct((nchip, hr, cols), BF16), compiler_params=_cparams())(core, grad, sib)


def _pair_sum_exchange(sums):
    nw = len(sums)

    def build(p_refs, o_refs, sems):
        send_sems, recv_sems, local_sems = sems
        x, y, c, chips = _place()
        me = 2 * x + y
        sibling = (x, y, 1 - c)
        mine = [pltpu.make_async_copy(p_refs[w].at[me], o_refs[w].at[c, 3], local_sems.at[w]) for w in range(nw)]
        first = [_remote(p_refs[w].at[2 * cx + cy], o_refs[w].at[c, k], send_sems.at[k, w], recv_sems.at[k, w],
                         (cx, cy, c)) for k, (cx, cy) in enumerate(chips) for w in range(nw)]

        def start():
            for cp in mine + first:
                cp.start()

        def finish():
            passed = []
            for k in range(N_CHIPS):
                for w in range(nw):
                    slab = o_refs[w].at[c, k]
                    if k < 3:
                        first[k * nw + w].wait_recv()
                    else:
                        mine[w].wait()
                    fwd = _remote(slab, slab, send_sems.at[3 + k, w], recv_sems.at[3 + k, w], sibling)
                    fwd.start()
                    passed.append(fwd)
            for k in range(N_CHIPS):
                for w in range(nw):
                    slab = o_refs[w].at[1 - c, k]
                    _remote(slab, slab, send_sems.at[3 + k, w], recv_sems.at[3 + k, w], sibling).wait_recv()
            for cp in first + passed:
                cp.wait_send()

        return start, finish

    return _Exchange(list(sums), [jax.ShapeDtypeStruct((2,) + p.shape, BF16) for p in sums],
                     [pltpu.SemaphoreType.DMA((7, nw)), pltpu.SemaphoreType.DMA((7, nw)),
                      pltpu.SemaphoreType.DMA((nw,))], build)


def _allreduce_small(vec):
    m_per, n = vec.shape

    def body(x_ref, out_ref, gath_ref, send_sems, recv_sems, local_sem):
        x, y, c, chips = _place()
        me, sibling = (x, y, c), (x, y, 1 - c)

        def rows(px, py, pc):
            return gath_ref.at[pl.ds((4 * px + 2 * py + pc) * m_per, m_per), :]

        def copy(k, block, to, src=None):
            return pltpu.make_async_remote_copy(
                src_ref=rows(*block) if src is None else src, dst_ref=rows(*block),
                send_sem=send_sems.at[k], recv_sem=recv_sems.at[k], device_id=to, device_id_type=MESH)

        mine = pltpu.make_async_copy(x_ref, rows(*me), local_sem)
        mine.start()
        first = [copy(0, me, sibling, src=x_ref)]
        first += [copy(1 + j, me, (*chip, c), src=x_ref) for j, chip in enumerate(chips)]
        for cp in first:
            cp.start()
        passed = [copy(4 + j, (*chip, c), sibling) for j, chip in enumerate(chips)]
        for j, chip in enumerate(chips):
            copy(1 + j, (*chip, c), me).wait_recv()
            passed[j].start()
        copy(0, sibling, me).wait_recv()
        for j, chip in enumerate(chips):
            copy(4 + j, (*chip, 1 - c), me).wait_recv()
        for cp in first + passed:
            cp.wait_send()
        mine.wait()
        acc = gath_ref[pl.ds(0, m_per), :]
        for k in range(1, N_DEV):
            acc = acc + gath_ref[pl.ds(k * m_per, m_per), :]
        out_ref[...] = acc

    vm = pl.BlockSpec(memory_space=pltpu.VMEM)
    return pl.pallas_call(
        body, name="allreduce_small", in_specs=[vm], out_specs=vm,
        out_shape=jax.ShapeDtypeStruct((m_per, n), F32),
        scratch_shapes=[pltpu.VMEM((N_DEV * m_per, n), F32), pltpu.SemaphoreType.DMA((7,)),
                        pltpu.SemaphoreType.DMA((7,)), pltpu.SemaphoreType.DMA],
    )(vec)


def _pack_small(vals):
    flat = jnp.concatenate([vals[name].reshape(-1).astype(F32) for name, _ in SMALL])
    flat = jnp.pad(flat, (0, SMALL_ROWS * LANES - flat.shape[0]))
    return flat.reshape(SMALL_ROWS, LANES)


def _unpack_small(packed):
    flat = packed.reshape(-1)
    out, off = {}, 0
    for name, shape in SMALL:
        n = int(np.prod(shape))
        out[name] = flat[off:off + n].reshape(shape)
        off += n
    return out


def _head_pad_cols(w, heads, real):
    k = w.shape[0]
    return jnp.pad(w.reshape(k, heads, real), ((0, 0), (0, 0), (0, LANES - real))).reshape(k, heads * LANES)


def _rope_tables(positions):
    half = MLA_ROPE // 2
    inv = ROPE_BASE ** (-jnp.arange(half, dtype=F32) / half)
    ang = positions.astype(F32)[:, None] * inv
    cos, sin = jnp.cos(ang), jnp.sin(ang)
    t = positions.shape[0]
    z = lambda n: jnp.zeros((t, n), F32)
    rc = jnp.concatenate([jnp.ones((t, MLA_NOPE), F32), cos, cos, z(LANES - MLA_QK)], axis=1)
    rs1 = jnp.concatenate([z(MLA_NOPE), -sin, z(LANES - MLA_NOPE - half)], axis=1)
    rs2 = jnp.concatenate([z(MLA_NOPE + half), sin, z(LANES - MLA_QK)], axis=1)
    return rc, rs1, rs2


FFN1_WEIGHTS = ("ffn1_w_gu", "ffn1_w_down")
FFN2_WEIGHTS = ("ffn2_w_gu", "ffn2_w_down")
MIXER_WEIGHTS = tuple(n for n, *_ in SHARDED if n not in FFN1_WEIGHTS + FFN2_WEIGHTS)
SHARD_SHAPE = {n: (r, c, kind) for n, r, c, kind in SHARDED}


def _from_blocks(name, gathered):
    r, c, kind = SHARD_SHAPE[name]
    blk = gathered.reshape(N_CHIPS, r, c)
    return blk, (blk.transpose(1, 0, 2).reshape(r, N_CHIPS * c) if kind == "col" else blk.reshape(N_CHIPS * r, c))


def _grad_pair_sums(names, gw, core, tag):
    by_owner = []
    for name in names:
        r, c, kind = SHARD_SHAPE[name]
        blk = gw[name].reshape(r, N_CHIPS, c).transpose(1, 0, 2) if kind == "col" else gw[name].reshape(N_CHIPS, r, c)
        by_owner.append(blk.astype(BF16).reshape(N_CHIPS, 2, r // 2, c))
    received = _swap_halves(by_owner, "grad_swap_" + tag)
    return [_pair_sum(g, s, core, "pair_sum_" + n) for g, s, n in zip(by_owner, received, names)]


def _device_step(x, mem, positions, tgt, small, shards, core):
    d = D_MODEL
    g_ffn1, g_mix, g_ffn2 = small["ffn1_norm"], small["mix_norm"], small["ffn2_norm"]
    big = {}
    for name, g in zip(FFN1_WEIGHTS, _run_exchange(_gather_exchange([shards[n] for n in FFN1_WEIGHTS]), "gather_ffn1")):
        big[name + "#blocks"], big[name] = _from_blocks(name, g)
    wgu1, wd1 = big["ffn1_w_gu#blocks"], big["ffn1_w_down"].reshape(2, FF_TILE, d)
    x1, gpre1, upre1, *rest = _ffn_fwd(x, g_ffn1, wgu1, wd1, "ffn1_fwd",
                                       ex=_gather_exchange([shards[n] for n in MIXER_WEIGHTS]))
    for name, g in zip(MIXER_WEIGHTS, rest):
        big[name + "#blocks"], big[name] = _from_blocks(name, g)
    w_in = big["w_in"]
    w_uv_, w_cq, w_ckv = w_in[:, :COL_CQ], w_in[:, COL_CQ:COL_CKV], w_in[:, COL_CKV:COL_KR]
    w_kr = jnp.pad(w_in[:, COL_KR:COL_QM], ((0, 0), (MLA_NOPE, LANES - MLA_QK)))
    w_qm, w_g = w_in[:, COL_QM:COL_GATE], w_in[:, COL_GATE:]
    segs = (w_uv_, w_cq, w_ckv, w_kr, w_qm, w_g)
    wuq = _head_pad_cols(big["mla_w_uq"], MLA_HEADS, MLA_QK)
    ukv = big["mla_w_ukv"].reshape(MLA_KV_RANK, MLA_HEADS, 2, MLA_NOPE)
    wuk = _head_pad_cols(ukv[:, :, 0].reshape(MLA_KV_RANK, -1), MLA_HEADS, MLA_NOPE)
    wuv = _head_pad_cols(ukv[:, :, 1].reshape(MLA_KV_RANK, -1), MLA_HEADS, MLA_NOPE)
    wkv = big["mem_w_kv"]
    wa, wc, wo = big["w_branch_a"], big["w_branch_c"], big["w_out"]
    wb = jnp.pad(big["w_branch_b"].reshape(MLA_HEADS, MLA_NOPE, d),
                 ((0, 0), (0, LANES - MLA_NOPE), (0, 0))).reshape(MLA_HEADS * LANES, d)
    qg = jnp.pad(small["mla_q_norm"], ((0, 0), (0, LANES - MLA_QK)))
    kg = jnp.pad(small["mla_k_norm"], ((0, 0), (0, LANES - MLA_QK)))
    causal = jnp.tril(jnp.ones((CHUNK, CHUNK), bool))
    wt_f = jnp.where(causal[None], small["sg_w"][0], 0.0)
    wt, wt_t = wt_f.astype(BF16), wt_f.transpose(0, 2, 1).astype(BF16)
    bias_l = jnp.repeat(small["sg_b"][0].T, 64, axis=1)
    rc, rs1, rs2 = _rope_tables(positions)

    h = _rms_fwd(x1, g_mix, "mix_norm_fwd")
    zuv, zcq, zckv, zkr, zqm, zg = _mm_cols(h, segs, [F32] * 5 + [BF16], "in_proj")
    ya = _sgu_fwd(zuv, small["sg_ln_g"], small["sg_ln_b"], wt, bias_l, "sgu_fwd")
    q, k, v, cqn, ckvn, *rest = _mla_prep_fwd(zcq, zckv, zkr, small["mla_cq_norm"], small["mla_ckv_norm"], qg, kg,
                                              wuq, wuk, wuv, rc, rs1, rs2, "mla_prep_fwd",
                                              ex=_gather_exchange([shards[n] for n in FFN2_WEIGHTS]))
    for name, g in zip(FFN2_WEIGHTS, rest):
        big[name + "#blocks"], big[name] = _from_blocks(name, g)
    wgu2, wd2 = big["ffn2_w_gu#blocks"], big["ffn2_w_down"].reshape(2, FF_TILE, d)
    yb, lse = _attn_fwd(q, k, v, "mla_attn_fwd")
    km, vm, memn = _mem_kv_fwd(mem, small["mem_norm"], wkv, small["mem_k_norm"], "mem_kv_fwd")
    yc = _mem_attn_fwd(zqm, small["mem_q_norm"], km, vm, "mem_attn_fwd")
    x2, merged, pa, pb, pc = _merge_fwd(x1, ya, yb, yc, zg, small["b_gate"], wa, wb, wc, wo, "merge_fwd")
    x3, gpre2, upre2 = _ffn_fwd(x2, g_ffn2, wgu2, wd2, "ffn2_fwd")
    dy, loss_row = _loss_head(x3, tgt, "loss_head")

    gw, gs, slots = {}, {}, {}

    def ffn_grads(prefix, xin, gain, dyin, gpre, upre, wgu, wd, ex=None, ex_names=()):
        dx, dgain, xn, dgt, dup, act, *got = _ffn_bwd(xin, gain, dyin, gpre, upre, wgu, wd, prefix + "_bwd", ex=ex)
        slots.update(zip(ex_names, got))
        gw[prefix + "_w_gu"] = jnp.concatenate(
            [_mm_tn(xn, dgt, prefix + "_dwg"), _mm_tn(xn, dup, prefix + "_dwu")], axis=1)
        gw[prefix + "_w_down"] = _mm_tn(act, dyin, prefix + "_dwd", scale=0.5)
        gs[prefix + "_norm"] = dgain
        return dx

    dx2 = ffn_grads("ffn2", x2, g_ffn2, dy, gpre2, upre2, wgu2, wd2)
    ffn2_sums = _pair_sum_exchange(_grad_pair_sums(FFN2_WEIGHTS, gw, core, "ffn2"))
    dpa, dpb, dpc, dzg, dbg, dya, dyb, dyc, *got = _merge_bwd(dx2, pa, pb, pc, zg, small["b_gate"], wa, wb, wc, wo,
                                                              "merge_bwd", ex=ffn2_sums)
    slots.update(zip(FFN2_WEIGHTS, got))
    gs["b_gate"] = dbg
    gw["w_out"] = _mm_tn(merged, dx2, "dw_out")
    gw["w_branch_a"] = _mm_tn(ya, dpa, "dw_branch_a")
    gw["w_branch_b"] = _mm_tn(yb, dpb, "dw_branch_b").reshape(MLA_HEADS, LANES, d)[:, :MLA_NOPE].reshape(-1, d)
    gw["w_branch_c"] = _mm_tn(yc, dpc, "dw_branch_c")

    dzuv, dwt, dbl, dlg, dlb = _sgu_bwd(zuv, dya, small["sg_ln_g"], small["sg_ln_b"], wt, wt_t, bias_l, "sgu_bwd")
    gs["sg_w"], gs["sg_b"] = dwt[None], dbl[:, :SG_GROUPS].T[None]
    gs["sg_ln_g"], gs["sg_ln_b"] = dlg, dlb

    delta_rows, lse_rows = _attn_bwd_rows(yb, lse, dyb, "mla_attn_bwd_rows")
    dq, dk, dv = _attn_bwd(q, k, v, delta_rows, lse_rows, dyb, "mla_attn_bwd")
    dzcq, dzckv, dzkr, dql, dkl, dgcq, dgckv, dqg, dkg = _mla_prep_bwd(
        zcq, zckv, zkr, small["mla_cq_norm"], small["mla_ckv_norm"], qg, kg, wuq, wuk, wuv, rc, rs1, rs2,
        dq, dk, dv, "mla_prep_bwd")
    gs["mla_cq_norm"], gs["mla_ckv_norm"] = dgcq, dgckv
    gs["mla_q_norm"], gs["mla_k_norm"] = dqg[:, :MLA_QK], dkg[:, :MLA_QK]
    gw["mla_w_uq"] = _mm_tn(cqn, dql, "dw_uq").reshape(MLA_Q_RANK, MLA_HEADS, LANES)[:, :, :MLA_QK].reshape(
        MLA_Q_RANK, -1)
    dwuk = _mm_tn(ckvn, dkl, "dw_uk").reshape(MLA_KV_RANK, MLA_HEADS, LANES)[:, :, :MLA_NOPE]
    dwuv = _mm_tn(ckvn, dv, "dw_uv").reshape(MLA_KV_RANK, MLA_HEADS, LANES)[:, :, :MLA_NOPE]
    gw["mla_w_ukv"] = jnp.concatenate([dwuk, dwuv], axis=2).reshape(MLA_KV_RANK, -1)

    dzqm, dkn, dvm, dmqg = _mem_attn_bwd(zqm, dyc, small["mem_q_norm"], km, vm, "mem_attn_bwd")
    gs["mem_q_norm"] = dmqg
    gw["mem_w_kv"], gs["mem_k_norm"], gs["mem_norm"] = _mem_kv_bwd(
        mem, small["mem_norm"], wkv, small["mem_k_norm"], dkn, dvm, "mem_kv_bwd")

    dzs = (dzuv, dzcq, dzckv, dzkr, dzqm, dzg)
    dh = _mm([(dz, w.T) for dz, w in zip(dzs, segs)], F32, "in_proj_bwd")
    dws = list(_mm_tn_cols(h, dzs[:5], "dw_in_narrow")) + [_mm_tn(h, dzg, "dw_in_gate")]
    dws[3] = dws[3][:, MLA_NOPE:MLA_QK]
    gw["w_in"] = jnp.concatenate(dws, axis=1)
    dx1, gs["mix_norm"] = _rms_bwd(x1, g_mix, dh, dx2, "mix_norm_bwd")
    mixer_sums = _pair_sum_exchange(_grad_pair_sums(MIXER_WEIGHTS, gw, core, "mixer"))
    dx = ffn_grads("ffn1", x, g_ffn1, dx1, gpre1, upre1, wgu1, wd1, ex=mixer_sums, ex_names=MIXER_WEIGHTS)
    ffn1_sums = _pair_sum_exchange(_grad_pair_sums(FFN1_WEIGHTS, gw, core, "ffn1"))
    slots.update(zip(FFN1_WEIGHTS, _run_exchange(ffn1_sums, "grad_exchange_ffn1")))
    return loss_row, dx, slots, gs


def kernel(x, mem, positions, ffn1_norm, ffn1_w_gu, ffn1_w_down, mix_norm, w_in, b_gate, sg_ln_g, sg_ln_b, sg_w, sg_b, mla_cq_norm, mla_w_uq, mla_ckv_norm, mla_w_ukv, mla_q_norm, mla_k_norm, mem_norm, mem_w_kv, mem_q_norm, mem_k_norm, w_branch_a, w_branch_b, w_branch_c, w_out, ffn2_norm, ffn2_w_gu, ffn2_w_down, loss_target, m_ffn1_norm, m_ffn1_w_gu, m_ffn1_w_down, m_mix_norm, m_w_in, m_b_gate, m_sg_ln_g, m_sg_ln_b, m_sg_w, m_sg_b, m_mla_cq_norm, m_mla_w_uq, m_mla_ckv_norm, m_mla_w_ukv, m_mla_q_norm, m_mla_k_norm, m_mem_norm, m_mem_w_kv, m_mem_q_norm, m_mem_k_norm, m_w_branch_a, m_w_branch_b, m_w_branch_c, m_w_out, m_ffn2_norm, m_ffn2_w_gu, m_ffn2_w_down, v_ffn1_norm, v_ffn1_w_gu, v_ffn1_w_down, v_mix_norm, v_w_in, v_b_gate, v_sg_ln_g, v_sg_ln_b, v_sg_w, v_sg_b, v_mla_cq_norm, v_mla_w_uq, v_mla_ckv_norm, v_mla_w_ukv, v_mla_q_norm, v_mla_k_norm, v_mem_norm, v_mem_w_kv, v_mem_q_norm, v_mem_k_norm, v_w_branch_a, v_w_branch_b, v_w_branch_c, v_w_out, v_ffn2_norm, v_ffn2_w_gu, v_ffn2_w_down):
    args = dict(locals())
    weights = {n: args[n] for n in WEIGHT_ORDER}
    mom_m = {n: args["m_" + n] for n in WEIGHT_ORDER}
    mom_v = {n: args["v_" + n] for n in WEIGHT_ORDER}
    small = {n: weights[n] for n, _ in SMALL}
    halves = lambda a, r, c: a.reshape(2, r // 2, c)

    shards = {n: halves(weights[n][0].astype(BF16), r, c) for n, r, c, _ in SHARDED}
    core = lax.axis_index("c").astype(jnp.int32).reshape(1)
    loss_row, dx, slots, gs = _device_step(x[0], mem[0], positions[0], loss_target[0], small, shards, core)
    loss = lax.psum(loss_row[0, 0], ("x", "y", "c"))
    small_grads = _unpack_small(_allreduce_small(_pack_small(gs)))

    grads, deltas, new_m, new_v = {}, {}, {}, {}
    for name, r, c, _ in SHARDED:
        outs = _adamw_slots(halves(weights[name][0], r, c), slots[name], halves(mom_m[name][0], r, c),
                            halves(mom_v[name][0], r, c), "adamw_" + name)
        shape = weights[name].shape
        grads[name], deltas[name], new_m[name], new_v[name] = [o.reshape(shape) for o in outs]
    dlt, nm, nv = _adamw(_pack_small(small), _pack_small(small_grads), _pack_small({n: mom_m[n] for n, _ in SMALL}),
                         _pack_small({n: mom_v[n] for n, _ in SMALL}), "adamw_small")
    for name, _ in SMALL:
        grads[name] = small_grads[name]
    deltas.update(_unpack_small(dlt))
    new_m.update(_unpack_small(nm))
    new_v.update(_unpack_small(nv))

    return (loss, dx[None], *[grads[n] for n in WEIGHT_ORDER], *[deltas[n] for n in WEIGHT_ORDER],
            *[new_m[n] for n in WEIGHT_ORDER], *[new_v[n] for n in WEIGHT_ORDER])
```

```python
import functools
from typing import Callable, NamedTuple

import numpy as np
import jax
import jax.numpy as jnp
from jax import lax
from jax.experimental import pallas as pl
from jax.experimental.pallas import tpu as pltpu

F32 = jnp.float32
BF16 = jnp.bfloat16

D_MODEL = 1024
D_FF = 2816
FF_TILE = 1408
SG_WIDTH = 512
SG_GROUPS = 8
CHUNK = 128
MLA_HEADS = 8
MLA_QK = 96
MLA_NOPE = 64
MLA_ROPE = 32
MLA_Q_RANK = 384
MLA_KV_RANK = 256
MEM_HEADS = 4
MEM_LEN = 256
LANES = 128
EPS = 1e-6
NEG = -1e30
ROPE_BASE = 10000.0
N_CHIPS = 4
N_DEV = 8

ADAM_LR = 0.001
ADAM_B1 = 0.9
ADAM_B2 = 0.999
ADAM_EPS = 1e-08
ADAM_WD = 0.01
ADAM_STEP = 10

COL_V = 512
COL_CQ = 1024
COL_CKV = 1408
COL_KR = 1664
COL_QM = 1696
COL_GATE = 2208
IN_COLS = 5280

VMEM_LIMIT_BYTES = 56 * 1024 * 1024
INV_SQRT2 = 0.7071067811865476
INV_SQRT_2PI = 0.3989422804014327
LOG2E = 1.4426950408889634
ATTN_SCALE = MLA_QK ** -0.5
V_ONES_LANE = 64
ATTN_SCALE2 = ATTN_SCALE * LOG2E

SHARDED = (
    ("ffn1_w_gu", 1024, 1408, "col"),
    ("ffn1_w_down", 704, 1024, "row"),
    ("w_in", 1024, 1320, "col"),
    ("mla_w_uq", 384, 192, "col"),
    ("mla_w_ukv", 256, 256, "col"),
    ("mem_w_kv", 256, 1024, "row"),
    ("w_branch_a", 512, 256, "col"),
    ("w_branch_b", 512, 256, "col"),
    ("w_branch_c", 512, 256, "col"),
    ("w_out", 256, 1024, "row"),
    ("ffn2_w_gu", 1024, 1408, "col"),
    ("ffn2_w_down", 704, 1024, "row"),
)
SMALL = (
    ("ffn1_norm", (1, 1024)), ("mix_norm", (1, 1024)), ("b_gate", (1, 3072)),
    ("sg_ln_g", (1, 512)), ("sg_ln_b", (1, 512)), ("sg_w", (1, 8, 128, 128)),
    ("sg_b", (1, 8, 128)), ("mla_cq_norm", (1, 384)), ("mla_ckv_norm", (1, 256)),
    ("mla_q_norm", (1, 96)), ("mla_k_norm", (1, 96)), ("mem_norm", (1, 1024)),
    ("mem_q_norm", (1, 128)), ("mem_k_norm", (1, 128)), ("ffn2_norm", (1, 1024)),
)
WEIGHT_ORDER = (
    "ffn1_norm", "ffn1_w_gu", "ffn1_w_down", "mix_norm", "w_in", "b_gate", "sg_ln_g", "sg_ln_b",
    "sg_w", "sg_b", "mla_cq_norm", "mla_w_uq", "mla_ckv_norm", "mla_w_ukv", "mla_q_norm",
    "mla_k_norm", "mem_norm", "mem_w_kv", "mem_q_norm", "mem_k_norm", "w_branch_a", "w_branch_b",
    "w_branch_c", "w_out", "ffn2_norm", "ffn2_w_gu", "ffn2_w_down",
)

_N_SMALL = sum(int(np.prod(s)) for _, s in SMALL)
SMALL_ROWS = -(-_N_SMALL // (LANES * 8)) * 8

MESH = pl.DeviceIdType.MESH


def _cparams():
    return pltpu.CompilerParams(vmem_limit_bytes=VMEM_LIMIT_BYTES)


def _dot(a, b):
    return jnp.dot(a, b, preferred_element_type=F32)


def _dot_nt(a, b):
    return lax.dot_general(a, b, (((1,), (1,)), ((), ())), preferred_element_type=F32)


def _dot_tn(a, b):
    return lax.dot_general(a, b, (((0,), (0,)), ((), ())), preferred_element_type=F32)


def _gelu(x):
    return 0.5 * x * (1.0 + lax.erf(x * INV_SQRT2))


def _gelu_grad(x):
    return 0.5 * (1.0 + lax.erf(x * INV_SQRT2)) + x * jnp.exp(-0.5 * x * x) * INV_SQRT_2PI


def _rstd(x, n):
    return lax.rsqrt(jnp.sum(x * x, axis=-1, keepdims=True) * (1.0 / n) + EPS)


def _rms_vjp(x, r, g, dy, n):
    dxh = dy * g
    dx = r * dxh - x * (r * r * r) * (jnp.sum(dxh * x, axis=-1, keepdims=True) * (1.0 / n))
    return dx, dy * x * r


def _row_tile(t, want):
    return min(t, want)


def _wide_tile(n):
    if n <= 1024:
        return n
    if n % 1024 == 0:
        return 1024
    assert n % FF_TILE == 0, n
    return FF_TILE


def _rms_fwd(x, g, name):
    t, d = x.shape
    tm = _row_tile(t, 512)

    def body(x_ref, g_ref, o_ref):
        xv = x_ref[...]
        o_ref[...] = (xv * _rstd(xv, d) * g_ref[...]).astype(BF16)

    return pl.pallas_call(
        body, name=name, grid=(t // tm,),
        in_specs=[pl.BlockSpec((tm, d), lambda i: (i, 0)), pl.BlockSpec((1, d), lambda i: (0, 0))],
        out_specs=pl.BlockSpec((tm, d), lambda i: (i, 0)),
        out_shape=jax.ShapeDtypeStruct((t, d), BF16), compiler_params=_cparams())(x, g)


def _rms_bwd(x, g, dxn, dres, name):
    t, d = x.shape
    tm = _row_tile(t, 256)

    def body(x_ref, g_ref, d_ref, r_ref, dx_ref, dg_ref):
        @pl.when(pl.program_id(0) == 0)
        def _():
            dg_ref[...] = jnp.zeros_like(dg_ref)

        xv = x_ref[...]
        r = _rstd(xv, d)
        dx, dgr = _rms_vjp(xv, r, g_ref[...], d_ref[...].astype(F32), d)
        dx_ref[...] = r_ref[...] + dx
        dg_ref[...] += jnp.sum(dgr, axis=0, keepdims=True)

    row = pl.BlockSpec((tm, d), lambda i: (i, 0))
    vec = pl.BlockSpec((1, d), lambda i: (0, 0))
    return pl.pallas_call(
        body, name=name, grid=(t // tm,), in_specs=[row, vec, row, row], out_specs=[row, vec],
        out_shape=[jax.ShapeDtypeStruct((t, d), F32), jax.ShapeDtypeStruct((1, d), F32)],
        compiler_params=_cparams())(x, g, dxn, dres)


def _mm(pairs, out_dtype, name):
    t = pairs[0][0].shape[0]
    n = pairs[0][1].shape[1]
    tm = _row_tile(t, 512)
    tn = _wide_tile(n)
    np_ = len(pairs)

    def body(*refs):
        o_ref = refs[2 * np_]
        acc = None
        for a_ref, w_ref in zip(refs[:np_], refs[np_:2 * np_]):
            part = _dot(a_ref[...].astype(BF16), w_ref[...])
            acc = part if acc is None else acc + part
        o_ref[...] = acc.astype(out_dtype)

    in_specs = [pl.BlockSpec((tm, a.shape[1]), lambda i, j: (i, 0)) for a, _ in pairs]
    in_specs += [pl.BlockSpec((w.shape[0], tn), lambda i, j: (0, j)) for _, w in pairs]
    return pl.pallas_call(
        body, name=name, grid=(t // tm, n // tn), in_specs=in_specs,
        out_specs=pl.BlockSpec((tm, tn), lambda i, j: (i, j)),
        out_shape=jax.ShapeDtypeStruct((t, n), out_dtype), compiler_params=_cparams(),
    )(*[a for a, _ in pairs], *[w for _, w in pairs])


def _mm_cols(a, ws, out_dtypes, name):
    t, kdim = a.shape
    tm = _row_tile(t, 256)
    n = len(ws)

    def body(*refs):
        av = refs[0][...]
        for w_ref, o_ref in zip(refs[1:1 + n], refs[1 + n:]):
            o_ref[...] = _dot(av, w_ref[...]).astype(o_ref.dtype)

    row = lambda width: pl.BlockSpec((tm, width), lambda i: (i, 0))
    return pl.pallas_call(
        body, name=name, grid=(t // tm,),
        in_specs=[row(kdim)] + [pl.BlockSpec(w.shape, lambda i: (0, 0)) for w in ws],
        out_specs=[row(w.shape[1]) for w in ws],
        out_shape=[jax.ShapeDtypeStruct((t, w.shape[1]), dt) for w, dt in zip(ws, out_dtypes)],
        compiler_params=_cparams())(a, *ws)


def _mm_tn_cols(a, bs, name):
    t, m = a.shape
    tk = _row_tile(t, 512)
    n = len(bs)

    def body(*refs):
        @pl.when(pl.program_id(0) == 0)
        def _():
            for o_ref in refs[1 + n:]:
                o_ref[...] = jnp.zeros_like(o_ref)

        av = refs[0][...].astype(BF16)
        for b_ref, o_ref in zip(refs[1:1 + n], refs[1 + n:]):
            o_ref[...] += _dot_tn(av, b_ref[...].astype(BF16))

    row = lambda width: pl.BlockSpec((tk, width), lambda k: (k, 0))
    return pl.pallas_call(
        body, name=name, grid=(t // tk,), in_specs=[row(m)] + [row(b.shape[1]) for b in bs],
        out_specs=[pl.BlockSpec((m, b.shape[1]), lambda k: (0, 0)) for b in bs],
        out_shape=[jax.ShapeDtypeStruct((m, b.shape[1]), F32) for b in bs],
        compiler_params=_cparams())(a, *bs)


def _mm_tn(a, b, name, scale=1.0):
    t, m = a.shape
    n = b.shape[1]
    tm, tn = _wide_tile(m), _wide_tile(n)
    tk = _row_tile(t, 1024)
    nk = t // tk

    def body(a_ref, b_ref, o_ref):
        k = pl.program_id(2)

        @pl.when(k == 0)
        def _():
            o_ref[...] = jnp.zeros_like(o_ref)

        o_ref[...] += _dot_tn(a_ref[...].astype(BF16), b_ref[...].astype(BF16))
        if scale != 1.0:
            @pl.when(k == nk - 1)
            def _():
                o_ref[...] = o_ref[...] * scale

    return pl.pallas_call(
        body, name=name, grid=(m // tm, n // tn, nk),
        in_specs=[pl.BlockSpec((tk, tm), lambda i, j, k: (k, i)),
                  pl.BlockSpec((tk, tn), lambda i, j, k: (k, j))],
        out_specs=pl.BlockSpec((tm, tn), lambda i, j, k: (i, j)),
        out_shape=jax.ShapeDtypeStruct((m, n), F32), compiler_params=_cparams())(a, b)


class _Exchange(NamedTuple):
    operands: list
    out_shapes: list
    sem_shapes: list
    build: Callable


def _call_with_exchange(ex, body, name, grid, in_specs, out_specs, out_shape, scratch_shapes, operands):
    if ex is None:
        return pl.pallas_call(body, name=name, grid=grid, in_specs=in_specs, out_specs=out_specs, out_shape=out_shape,
                              scratch_shapes=scratch_shapes, compiler_params=_cparams())(*operands)
    n_in, n_out, n_scr = len(in_specs), len(out_specs), len(scratch_shapes)
    k_in, k_out = len(ex.operands), len(ex.out_shapes)

    def carried(*refs):
        a, b = n_in, n_in + k_in
        c, e = b + n_out, b + n_out + k_out
        f = e + n_scr
        start, finish = ex.build(refs[a:b], refs[c:e], refs[f:])
        steps = [pl.program_id(ax) for ax in range(len(grid))]
        first = functools.reduce(jnp.logical_and, [s == 0 for s in steps])
        last = functools.reduce(jnp.logical_and, [s == n - 1 for s, n in zip(steps, grid)])
        pl.when(first)(start)
        body(*refs[:a], *refs[b:c], *refs[e:f])
        pl.when(last)(finish)

    return pl.pallas_call(
        carried, name=name, grid=grid, in_specs=list(in_specs) + [ANY] * k_in,
        out_specs=list(out_specs) + [ANY] * k_out, out_shape=list(out_shape) + list(ex.out_shapes),
        scratch_shapes=list(scratch_shapes) + list(ex.sem_shapes), compiler_params=_cparams(),
    )(*operands, *ex.operands)


def _run_exchange(ex, name):
    k_in, k_out = len(ex.operands), len(ex.out_shapes)

    def body(*refs):
        start, finish = ex.build(refs[:k_in], refs[k_in:k_in + k_out], refs[k_in + k_out:])
        start()
        finish()

    return pl.pallas_call(body, name=name, in_specs=[ANY] * k_in, out_specs=[ANY] * k_out,
                          out_shape=list(ex.out_shapes), scratch_shapes=list(ex.sem_shapes))(*ex.operands)


def _ffn_fwd(x, g, wgu4, wd2, name, ex=None):
    t, d = x.shape
    tm = _row_tile(t, 512)

    def body(x_ref, g_ref, wg_ref, wu_ref, wd_ref, o_ref, gg_ref, uu_ref, xn_scr, acc_scr):
        j = pl.program_id(1)

        @pl.when(j == 0)
        def _():
            xv = x_ref[...]
            xn_scr[...] = (xv * _rstd(xv, d) * g_ref[...]).astype(BF16)
            acc_scr[...] = jnp.zeros_like(acc_scr)

        xn = xn_scr[...]
        gg = _dot(xn, wg_ref[0])
        uu = _dot(xn, wu_ref[0])
        gg_ref[...] = gg.astype(BF16)
        uu_ref[...] = uu.astype(BF16)
        act = gg * jax.nn.sigmoid(gg) * uu
        acc_scr[...] += _dot(act.astype(BF16), wd_ref[0])

        @pl.when(j == 1)
        def _():
            o_ref[...] = x_ref[...] + 0.5 * acc_scr[...]

    row = pl.BlockSpec((tm, d), lambda i, j: (i, 0))
    ffb = pl.BlockSpec((tm, FF_TILE), lambda i, j: (i, j))
    return _call_with_exchange(
        ex, body, name, (t // tm, 2),
        [row, pl.BlockSpec((1, d), lambda i, j: (0, 0)),
         pl.BlockSpec((1, d, FF_TILE), lambda i, j: (j, 0, 0)),
         pl.BlockSpec((1, d, FF_TILE), lambda i, j: (j + 2, 0, 0)),
         pl.BlockSpec((1, FF_TILE, d), lambda i, j: (j, 0, 0))],
        [row, ffb, ffb],
        [jax.ShapeDtypeStruct((t, d), F32), jax.ShapeDtypeStruct((t, D_FF), BF16), jax.ShapeDtypeStruct((t, D_FF), BF16)],
        [pltpu.VMEM((tm, d), BF16), pltpu.VMEM((tm, d), F32)], (x, g, wgu4, wgu4, wd2))


def _ffn_bwd(x, g, dy, gpre, upre, wgu4, wd2, name, ex=None):
    t, d = x.shape
    tm = _row_tile(t, 256)

    def body(dy_ref, gg_ref, uu_ref, wg_ref, wu_ref, wd_ref, dg_ref, du_ref, act_ref, part_ref):
        gg = gg_ref[...].astype(F32)
        uu = uu_ref[...].astype(F32)
        sg = jax.nn.sigmoid(gg)
        silu = gg * sg
        act_ref[...] = (silu * uu).astype(BF16)
        dyh = (0.5 * dy_ref[...]).astype(BF16)
        dact = _dot_nt(dyh, wd_ref[0])
        du = (dact * silu).astype(BF16)
        dgt = (dact * uu * (sg * (1.0 + gg * (1.0 - sg)))).astype(BF16)
        du_ref[...] = du
        dg_ref[...] = dgt
        part_ref[0] = _dot_nt(dgt, wg_ref[0]) + _dot_nt(du, wu_ref[0])

    row = pl.BlockSpec((tm, d), lambda j, i: (i, 0))
    ffb = pl.BlockSpec((tm, FF_TILE), lambda j, i: (i, j))
    dgt, dup, act, parts, *got = _call_with_exchange(
        ex, body, name, (2, t // tm),
        [row, ffb, ffb,
         pl.BlockSpec((1, d, FF_TILE), lambda j, i: (j, 0, 0)),
         pl.BlockSpec((1, d, FF_TILE), lambda j, i: (j + 2, 0, 0)),
         pl.BlockSpec((1, FF_TILE, d), lambda j, i: (j, 0, 0))],
        [ffb, ffb, ffb, pl.BlockSpec((1, tm, d), lambda j, i: (j, i, 0))],
        [jax.ShapeDtypeStruct((t, D_FF), BF16)] * 3 + [jax.ShapeDtypeStruct((2, t, d), F32)],
        [], (dy, gpre, upre, wgu4, wgu4, wd2))

    def norm_body(x_ref, g_ref, p_ref, dy_ref, dx_ref, dgain_ref, xn_ref):
        @pl.when(pl.program_id(0) == 0)
        def _():
            dgain_ref[...] = jnp.zeros_like(dgain_ref)

        xv = x_ref[...]
        r = _rstd(xv, d)
        xn_ref[...] = (xv * r * g_ref[...]).astype(BF16)
        dx, dgr = _rms_vjp(xv, r, g_ref[...], p_ref[0] + p_ref[1], d)
        dx_ref[...] = dy_ref[...] + dx
        dgain_ref[...] += jnp.sum(dgr, axis=0, keepdims=True)

    nrow = pl.BlockSpec((tm, d), lambda i: (i, 0))
    vec = pl.BlockSpec((1, d), lambda i: (0, 0))
    dx, dgain, xn = pl.pallas_call(
        norm_body, name=name + "_norm", grid=(t // tm,),
        in_specs=[nrow, vec, pl.BlockSpec((2, tm, d), lambda i: (0, i, 0)), nrow],
        out_specs=[nrow, vec, nrow],
        out_shape=[jax.ShapeDtypeStruct((t, d), F32), jax.ShapeDtypeStruct((1, d), F32),
                   jax.ShapeDtypeStruct((t, d), BF16)],
        compiler_params=_cparams())(x, g, parts, dy)
    return [dx, dgain, xn, dgt, dup, act] + got


def _loss_head(y, tgt, name):
    t, d = y.shape
    tm = _row_tile(t, 512)

    def body(y_ref, t_ref, dy_ref, loss_ref):
        @pl.when(pl.program_id(0) == 0)
        def _():
            loss_ref[...] = jnp.zeros_like(loss_ref)

        e = y_ref[...] - t_ref[...]
        dy_ref[...] = e * (1.0 / d)
        part = 0.5 * jnp.sum(jnp.sum(e * e, axis=-1, keepdims=True) * (1.0 / d), axis=0, keepdims=True)
        loss_ref[...] += jnp.broadcast_to(part, loss_ref.shape)

    row = pl.BlockSpec((tm, d), lambda i: (i, 0))
    return pl.pallas_call(
        body, name=name, grid=(t // tm,), in_specs=[row, row],
        out_specs=[row, pl.BlockSpec((1, LANES), lambda i: (0, 0))],
        out_shape=[jax.ShapeDtypeStruct((t, d), F32), jax.ShapeDtypeStruct((1, LANES), F32)],
        compiler_params=_cparams())(y, tgt)


def _sgu_layernorm(vpre, lg, lb):
    v = _gelu(vpre)
    mu = jnp.mean(v, axis=-1, keepdims=True)
    xc = v - mu
    rstd = lax.rsqrt(jnp.mean(xc * xc, axis=-1, keepdims=True) + EPS)
    xhat = xc * rstd
    return xhat, rstd, xhat * lg + lb


def _sgu_fwd(zuv, lg, lb, wt, bias_l, name):
    t = zuv.shape[0]
    tm = _row_tile(t, 512)

    def body(u_ref, v_ref, lg_ref, lb_ref, wt_ref, bl_ref, o_ref, vln_scr):
        _, _, vln = _sgu_layernorm(v_ref[...], lg_ref[...], lb_ref[...])
        vln_scr[...] = vln.astype(BF16)
        lo = lax.broadcasted_iota(jnp.int32, (CHUNK, LANES), 1) < 64
        for c in range(tm // CHUNK):
            rows = slice(c * CHUNK, (c + 1) * CHUNK)
            for p in range(SG_GROUPS // 2):
                cols = slice(p * LANES, (p + 1) * LANES)
                vp = vln_scr[rows, cols]
                mixed = jnp.where(lo, _dot(wt_ref[2 * p], vp), _dot(wt_ref[2 * p + 1], vp)) + bl_ref[:, cols]
                o_ref[rows, cols] = (_gelu(u_ref[rows, cols]) * mixed).astype(BF16)

    half = lambda k: pl.BlockSpec((tm, SG_WIDTH), lambda i: (i, k))
    vec = pl.BlockSpec((1, SG_WIDTH), lambda i: (0, 0))
    return pl.pallas_call(
        body, name=name, grid=(t // tm,),
        in_specs=[half(0), half(1), vec, vec,
                  pl.BlockSpec((SG_GROUPS, CHUNK, CHUNK), lambda i: (0, 0, 0)),
                  pl.BlockSpec((CHUNK, SG_WIDTH), lambda i: (0, 0))],
        out_specs=pl.BlockSpec((tm, SG_WIDTH), lambda i: (i, 0)),
        out_shape=jax.ShapeDtypeStruct((t, SG_WIDTH), BF16),
        scratch_shapes=[pltpu.VMEM((tm, SG_WIDTH), BF16)],
        compiler_params=_cparams())(zuv, zuv, lg, lb, wt, bias_l)


def _sgu_bwd(zuv, dya, lg, lb, wt, wt_t, bias_l, name):
    t = zuv.shape[0]
    tm = _row_tile(t, 256)
    nsteps = t // tm

    def body(u_ref, v_ref, dy_ref, lg_ref, lb_ref, wt_ref, wtt_ref, bl_ref,
             dz_ref, dwt_ref, dbl_ref, dlg_ref, dlb_ref, vln_scr, dvln_scr, dbacc_scr):
        step = pl.program_id(0)

        @pl.when(step == 0)
        def _():
            dwt_ref[...] = jnp.zeros_like(dwt_ref)
            dlg_ref[...] = jnp.zeros_like(dlg_ref)
            dlb_ref[...] = jnp.zeros_like(dlb_ref)
            dbl_ref[...] = jnp.zeros_like(dbl_ref)
            dbacc_scr[...] = jnp.zeros_like(dbacc_scr)

        vpre = v_ref[...]
        lgv = lg_ref[...]
        xhat, rstd, vln = _sgu_layernorm(vpre, lgv, lb_ref[...])
        vln_scr[...] = vln.astype(BF16)
        lo = lax.broadcasted_iota(jnp.int32, (CHUNK, LANES), 1) < 64
        for c in range(tm // CHUNK):
            rows = slice(c * CHUNK, (c + 1) * CHUNK)
            for p in range(SG_GROUPS // 2):
                cols = slice(p * LANES, (p + 1) * LANES)
                vp = vln_scr[rows, cols]
                mixed = jnp.where(lo, _dot(wt_ref[2 * p], vp), _dot(wt_ref[2 * p + 1], vp)) + bl_ref[:, cols]
                upre = u_ref[rows, cols]
                dyp = dy_ref[rows, cols]
                dz_ref[rows, cols] = (dyp * mixed * _gelu_grad(upre)).astype(BF16)
                dm = dyp * _gelu(upre)
                dbacc_scr[:, cols] += dm
                dlo = jnp.where(lo, dm, 0.0).astype(BF16)
                dhi = jnp.where(lo, 0.0, dm).astype(BF16)
                dvln_scr[rows, cols] = _dot(wtt_ref[2 * p], dlo) + _dot(wtt_ref[2 * p + 1], dhi)
                dwt_ref[2 * p] += _dot_nt(dlo, vp)
                dwt_ref[2 * p + 1] += _dot_nt(dhi, vp)
        dvln = dvln_scr[...]
        dlg_ref[...] += jnp.sum(dvln * xhat, axis=0, keepdims=True)
        dlb_ref[...] += jnp.sum(dvln, axis=0, keepdims=True)
        dxh = dvln * lgv
        dv = rstd * (dxh - jnp.mean(dxh, axis=-1, keepdims=True)
                     - xhat * jnp.mean(dxh * xhat, axis=-1, keepdims=True))
        dz_ref[:, SG_WIDTH:] = (dv * _gelu_grad(vpre)).astype(BF16)

        @pl.when(step == nsteps - 1)
        def _():
            rr = lax.broadcasted_iota(jnp.int32, (CHUNK, CHUNK), 0)
            cc = lax.broadcasted_iota(jnp.int32, (CHUNK, CHUNK), 1)
            tril = (cc <= rr).astype(F32)
            for gidx in range(SG_GROUPS):
                dwt_ref[gidx] = dwt_ref[gidx] * tril
            kk = lax.broadcasted_iota(jnp.int32, (SG_WIDTH, LANES), 0)
            gg = lax.broadcasted_iota(jnp.int32, (SG_WIDTH, LANES), 1)
            sel = ((kk // 64) == gg).astype(F32)
            dbl_ref[...] = jnp.dot(dbacc_scr[...], sel, preferred_element_type=F32,
                                   precision=lax.Precision.HIGHEST)

    half = lambda k: pl.BlockSpec((tm, SG_WIDTH), lambda i: (i, k))
    vec = pl.BlockSpec((1, SG_WIDTH), lambda i: (0, 0))
    wspec = pl.BlockSpec((SG_GROUPS, CHUNK, CHUNK), lambda i: (0, 0, 0))
    return pl.pallas_call(
        body, name=name, grid=(nsteps,),
        in_specs=[half(0), half(1), pl.BlockSpec((tm, SG_WIDTH), lambda i: (i, 0)), vec, vec,
                  wspec, wspec, pl.BlockSpec((CHUNK, SG_WIDTH), lambda i: (0, 0))],
        out_specs=[pl.BlockSpec((tm, 2 * SG_WIDTH), lambda i: (i, 0)), wspec,
                   pl.BlockSpec((CHUNK, LANES), lambda i: (0, 0)), vec, vec],
        out_shape=[jax.ShapeDtypeStruct((t, 2 * SG_WIDTH), BF16),
                   jax.ShapeDtypeStruct((SG_GROUPS, CHUNK, CHUNK), F32),
                   jax.ShapeDtypeStruct((CHUNK, LANES), F32),
                   jax.ShapeDtypeStruct((1, SG_WIDTH), F32), jax.ShapeDtypeStruct((1, SG_WIDTH), F32)],
        scratch_shapes=[pltpu.VMEM((tm, SG_WIDTH), BF16), pltpu.VMEM((tm, SG_WIDTH), F32),
                        pltpu.VMEM((CHUNK, SG_WIDTH), F32)],
        compiler_params=_cparams())(zuv, zuv, dya, lg, lb, wt, wt_t, bias_l)


def _rope(x, c, s1, s2):
    return x * c + pltpu.roll(x, LANES - 16, 1) * s1 + pltpu.roll(x, 16, 1) * s2


def _rope_t(dy, c, s1, s2):
    return dy * c + pltpu.roll(dy * s1, 16, 1) + pltpu.roll(dy * s2, LANES - 16, 1)


def _mla_prep_fwd(zcq, zckv, zkr, gcq, gckv, qg, kg, wuq, wuk, wuv, rc, rs1, rs2, name, ex=None):
    t = zcq.shape[0]
    tm = _row_tile(t, 256)
    hd = MLA_HEADS * LANES

    def body(zcq_ref, zckv_ref, zkr_ref, gcq_ref, gckv_ref, qg_ref, kg_ref, wuq_ref, wuk_ref, wuv_ref,
             c_ref, s1_ref, s2_ref, q_ref, k_ref, v_ref, cqn_ref, ckvn_ref):
        c, s1, s2 = c_ref[...], s1_ref[...], s2_ref[...]
        xq = zcq_ref[...]
        cqn = (xq * _rstd(xq, MLA_Q_RANK) * gcq_ref[...]).astype(BF16)
        cqn_ref[...] = cqn
        ql = _dot(cqn, wuq_ref[...])
        xk = zckv_ref[...]
        ckvn = (xk * _rstd(xk, MLA_KV_RANK) * gckv_ref[...]).astype(BF16)
        ckvn_ref[...] = ckvn
        kl = _dot(ckvn, wuk_ref[...])
        slot_lane = lax.broadcasted_iota(jnp.int32, (tm, hd), 1) % LANES
        v_ref[...] = jnp.where(slot_lane == V_ONES_LANE, 1.0, _dot(ckvn, wuv_ref[...])).astype(BF16)
        kr = zkr_ref[...]
        for h in range(MLA_HEADS):
            sl = slice(h * LANES, (h + 1) * LANES)
            qh = ql[:, sl]
            q_ref[:, sl] = (_rope(qh * _rstd(qh, MLA_QK) * qg_ref[...], c, s1, s2) * ATTN_SCALE2).astype(BF16)
            kh = kl[:, sl] + kr
            k_ref[:, sl] = _rope(kh * _rstd(kh, MLA_QK) * kg_ref[...], c, s1, s2).astype(BF16)

    row = lambda n: pl.BlockSpec((tm, n), lambda i: (i, 0))
    full = lambda a: pl.BlockSpec(a.shape, lambda i: (0, 0))
    return _call_with_exchange(
        ex, body, name, (t // tm,),
        [row(MLA_Q_RANK), row(MLA_KV_RANK), row(LANES), full(gcq), full(gckv), full(qg), full(kg),
         full(wuq), full(wuk), full(wuv), row(LANES), row(LANES), row(LANES)],
        [row(hd), row(hd), row(hd), row(MLA_Q_RANK), row(MLA_KV_RANK)],
        [jax.ShapeDtypeStruct((t, hd), BF16)] * 3
        + [jax.ShapeDtypeStruct((t, MLA_Q_RANK), BF16), jax.ShapeDtypeStruct((t, MLA_KV_RANK), BF16)],
        [], (zcq, zckv, zkr, gcq, gckv, qg, kg, wuq, wuk, wuv, rc, rs1, rs2))


def _mla_prep_bwd(zcq, zckv, zkr, gcq, gckv, qg, kg, wuq, wuk, wuv, rc, rs1, rs2, dq, dk, dv, name):
    t = zcq.shape[0]
    tm = _row_tile(t, 256)
    hd = MLA_HEADS * LANES

    def body(zcq_ref, zckv_ref, zkr_ref, gcq_ref, gckv_ref, qg_ref, kg_ref, wuq_ref, wuk_ref, wuv_ref,
             c_ref, s1_ref, s2_ref, dq_ref, dk_ref, dv_ref,
             dzcq_ref, dzckv_ref, dzkr_ref, dql_ref, dkl_ref, dgcq_ref, dgckv_ref, dqg_ref, dkg_ref):
        @pl.when(pl.program_id(0) == 0)
        def _():
            for ref in (dgcq_ref, dgckv_ref, dqg_ref, dkg_ref):
                ref[...] = jnp.zeros_like(ref)

        c, s1, s2 = c_ref[...], s1_ref[...], s2_ref[...]
        qgv, kgv = qg_ref[...], kg_ref[...]
        xq = zcq_ref[...]
        rq = _rstd(xq, MLA_Q_RANK)
        ql = _dot((xq * rq * gcq_ref[...]).astype(BF16), wuq_ref[...])
        xk = zckv_ref[...]
        rk = _rstd(xk, MLA_KV_RANK)
        kl = _dot((xk * rk * gckv_ref[...]).astype(BF16), wuk_ref[...])
        kr = zkr_ref[...]
        dqg_acc = jnp.zeros((tm, LANES), F32)
        dkg_acc = jnp.zeros((tm, LANES), F32)
        dkr = jnp.zeros((tm, LANES), F32)
        for h in range(MLA_HEADS):
            sl = slice(h * LANES, (h + 1) * LANES)
            qh = ql[:, sl]
            dqh, dgr = _rms_vjp(qh, _rstd(qh, MLA_QK), qgv, _rope_t(dq_ref[:, sl], c, s1, s2), MLA_QK)
            dql_ref[:, sl] = dqh.astype(BF16)
            dqg_acc += dgr
            kh = kl[:, sl] + kr
            dkh, dgr = _rms_vjp(kh, _rstd(kh, MLA_QK), kgv, _rope_t(dk_ref[:, sl], c, s1, s2), MLA_QK)
            dkl_ref[:, sl] = dkh.astype(BF16)
            dkg_acc += dgr
            dkr += dkh
        dqg_ref[...] += jnp.sum(dqg_acc, axis=0, keepdims=True)
        dkg_ref[...] += jnp.sum(dkg_acc, axis=0, keepdims=True)
        lane = lax.broadcasted_iota(jnp.int32, (tm, LANES), 1)
        dzkr_ref[...] = jnp.where((lane >= MLA_NOPE) & (lane < MLA_QK), dkr, 0.0).astype(BF16)
        dcqn = _dot_nt(dql_ref[...], wuq_ref[...])
        dx, dgr = _rms_vjp(xq, rq, gcq_ref[...], dcqn, MLA_Q_RANK)
        dzcq_ref[...] = dx.astype(BF16)
        dgcq_ref[...] += jnp.sum(dgr, axis=0, keepdims=True)
        dckvn = _dot_nt(dkl_ref[...], wuk_ref[...]) + _dot_nt(dv_ref[...].astype(BF16), wuv_ref[...])
        dx, dgr = _rms_vjp(xk, rk, gckv_ref[...], dckvn, MLA_KV_RANK)
        dzckv_ref[...] = dx.astype(BF16)
        dgckv_ref[...] += jnp.sum(dgr, axis=0, keepdims=True)

    row = lambda n: pl.BlockSpec((tm, n), lambda i: (i, 0))
    full = lambda a: pl.BlockSpec(a.shape, lambda i: (0, 0))
    vec = lambda n: pl.BlockSpec((1, n), lambda i: (0, 0))
    return pl.pallas_call(
        body, name=name, grid=(t // tm,),
        in_specs=[row(MLA_Q_RANK), row(MLA_KV_RANK), row(LANES), full(gcq), full(gckv), full(qg), full(kg),
                  full(wuq), full(wuk), full(wuv), row(LANES), row(LANES), row(LANES), row(hd), row(hd), row(hd)],
        out_specs=[row(MLA_Q_RANK), row(MLA_KV_RANK), row(LANES), row(hd), row(hd),
                   vec(MLA_Q_RANK), vec(MLA_KV_RANK), vec(LANES), vec(LANES)],
        out_shape=[jax.ShapeDtypeStruct((t, MLA_Q_RANK), BF16), jax.ShapeDtypeStruct((t, MLA_KV_RANK), BF16),
                   jax.ShapeDtypeStruct((t, LANES), BF16), jax.ShapeDtypeStruct((t, hd), BF16),
                   jax.ShapeDtypeStruct((t, hd), BF16), jax.ShapeDtypeStruct((1, MLA_Q_RANK), F32),
                   jax.ShapeDtypeStruct((1, MLA_KV_RANK), F32), jax.ShapeDtypeStruct((1, LANES), F32),
                   jax.ShapeDtypeStruct((1, LANES), F32)],
        compiler_params=_cparams(),
    )(zcq, zckv, zkr, gcq, gckv, qg, kg, wuq, wuk, wuv, rc, rs1, rs2, dq, dk, dv)


def _attn_tiles(t):
    tq = 512 if t >= 2048 else 128
    return tq, min(t, 4 * tq), min(t, 2 * tq)


def _causal_keep(tq, tk, i, j):
    row = lax.broadcasted_iota(jnp.int32, (tq, tk), 0)
    col = lax.broadcasted_iota(jnp.int32, (tq, tk), 1)
    return (col - row) <= (i * tq - j * tk)


def _causal_keep_t(tq, tk, i, j):
    key = lax.broadcasted_iota(jnp.int32, (tk, tq), 0)
    qry = lax.broadcasted_iota(jnp.int32, (tk, tq), 1)
    return (key - qry) <= (i * tq - j * tk)


ATTN_FWD_HEADS_PER_STEP = 2
ATTN_BWD_HEADS_PER_STEP = 2


def _attn_fwd(q, k, v, name):
    t, hd = q.shape
    hp = ATTN_FWD_HEADS_PER_STEP
    tq, tk, _ = _attn_tiles(t)
    pairs = [(i, j) for i in range(t // tq) for j in range(((i + 1) * tq - 1) // tk + 1)]
    ii = np.array([p[0] for p in pairs], np.int32)
    jj = np.array([p[1] for p in pairs], np.int32)

    def body(ii_ref, jj_ref, q_ref, k_ref, v_ref, o_ref, lse_ref, m_scr, acc_scr):
        s_id = pl.program_id(1)
        i, j = ii_ref[s_id], jj_ref[s_id]
        last = j == ((i + 1) * tq - 1) // tk
        ones_lane = lax.broadcasted_iota(jnp.int32, (tq, LANES), 1) == V_ONES_LANE

        @pl.when(j == 0)
        def _():
            m_scr[...] = jnp.full_like(m_scr, NEG)
            acc_scr[...] = jnp.zeros_like(acc_scr)

        def step(masked):
            for hh in range(hp):
                sl = slice(hh * LANES, (hh + 1) * LANES)
                s = _dot_nt(q_ref[:, sl], k_ref[:, sl])
                if masked:
                    s = jnp.where(_causal_keep(tq, tk, i, j), s, NEG)
                m_prev = m_scr[hh]
                m_new = jnp.maximum(m_prev, jnp.max(s, axis=1, keepdims=True))
                p = jnp.exp2(s - m_new)
                alpha = jnp.exp2(m_prev - m_new)
                acc = alpha * acc_scr[:, sl] + _dot(p.astype(BF16), v_ref[:, sl])
                if masked:
                    l_new = jnp.sum(jnp.where(ones_lane, acc, 0.0), axis=1, keepdims=True)
                    o_ref[:, sl] = (acc / l_new).astype(BF16)
                    lse_ref[:, sl] = jnp.broadcast_to(m_new + jnp.log(l_new) * LOG2E, (tq, LANES))
                else:
                    acc_scr[:, sl] = acc
                    m_scr[hh] = m_new

        @pl.when(jnp.logical_not(last))
        def _():
            step(False)

        @pl.when(last)
        def _():
            step(True)

    w = hp * LANES
    qspec = pl.BlockSpec((tq, w), lambda h, s, ii_r, jj_r: (ii_r[s], h))
    kspec = pl.BlockSpec((tk, w), lambda h, s, ii_r, jj_r: (jj_r[s], h))
    return pl.pallas_call(
        body, name=name,
        grid_spec=pltpu.PrefetchScalarGridSpec(
            num_scalar_prefetch=2, grid=(hd // w, len(pairs)), in_specs=[qspec, kspec, kspec],
            out_specs=[qspec, qspec],
            scratch_shapes=[pltpu.VMEM((hp, tq, 1), F32), pltpu.VMEM((tq, w), F32)]),
        out_shape=[jax.ShapeDtypeStruct((t, hd), BF16), jax.ShapeDtypeStruct((t, hd), F32)],
        compiler_params=_cparams())(jnp.asarray(ii), jnp.asarray(jj), q, k, v)


def _attn_bwd_rows(o, lse, do, name):
    t, hd = o.shape
    heads = hd // LANES
    tm = _row_tile(t, 512)

    def body(o_ref, lse_ref, do_ref, out_ref):
        lane = lax.broadcasted_iota(jnp.int32, (tm, LANES), 1)
        acc = jnp.zeros((tm, LANES), F32)
        for h in range(heads):
            sl = slice(h * LANES, (h + 1) * LANES)
            delta = jnp.sum(do_ref[:, sl].astype(F32) * o_ref[:, sl].astype(F32), axis=1, keepdims=True)
            acc = jnp.where(lane == h, delta, acc)
            acc = jnp.where(lane == heads + h, lse_ref[:, sl], acc)
        out_ref[...] = acc

    row = pl.BlockSpec((tm, hd), lambda i: (i, 0))
    cols = pl.pallas_call(
        body, name=name, grid=(t // tm,), in_specs=[row, row, row],
        out_specs=pl.BlockSpec((tm, LANES), lambda i: (i, 0)),
        out_shape=jax.ShapeDtypeStruct((t, LANES), F32), compiler_params=_cparams())(o, lse, do)
    rows = cols.T
    return rows[:heads].reshape(heads, 1, t), rows[heads:2 * heads].reshape(heads, 1, t)


def _attn_bwd(q, k, v, delta_rows, lse_rows, do, name):
    t, hd = q.shape
    hp = ATTN_BWD_HEADS_PER_STEP
    tq, _, tk = _attn_tiles(t)
    nq = t // tq
    pairs = [(i, j) for j in range(t // tk) for i in range((j * tk) // tq, nq)]
    ii = np.array([p[0] for p in pairs], np.int32)
    jj = np.array([p[1] for p in pairs], np.int32)

    def body(jj_ref, ii_ref, q_ref, k_ref, v_ref, delta_ref, lse_ref, do_ref, dq_ref, dk_ref, dv_ref,
             dk_scr, dv_scr):
        s_id = pl.program_id(1)
        i, j = ii_ref[s_id], jj_ref[s_id]

        @pl.when(s_id == 0)
        def _():
            dq_ref[...] = jnp.zeros_like(dq_ref)

        @pl.when(i == (j * tk) // tq)
        def _():
            dk_scr[...] = jnp.zeros_like(dk_scr)
            dv_scr[...] = jnp.zeros_like(dv_scr)

        rows = pl.ds(pl.multiple_of(i * tq, tq), tq)

        def step(masked):
            for hh in range(hp):
                sl = slice(hh * LANES, (hh + 1) * LANES)
                qv, kv, dov = q_ref[:, sl], k_ref[:, sl], do_ref[:, sl]
                st = _dot_nt(kv, qv)
                if masked:
                    st = jnp.where(_causal_keep_t(tq, tk, i, j), st, NEG)
                pt = jnp.exp2(st - lse_ref[hh])
                dv_scr[:, sl] += _dot(pt.astype(BF16), dov)
                dpt = _dot_nt(v_ref[:, sl], dov)
                dst = (pt * (dpt - delta_ref[hh]) * ATTN_SCALE).astype(BF16)
                dk_scr[:, sl] += _dot(dst, qv)
                dq_ref[rows, sl] += _dot_tn(dst, kv)

        crosses = (j + 1) * tk - 1 > i * tq

        @pl.when(jnp.logical_not(crosses))
        def _():
            step(False)

        @pl.when(crosses)
        def _():
            step(True)

        @pl.when(i == nq - 1)
        def _():
            dk_ref[...] = dk_scr[...] * (1.0 / ATTN_SCALE2)
            dv_ref[...] = dv_scr[...]

    w = hp * LANES
    qspec = pl.BlockSpec((tq, w), lambda h, s, jj_r, ii_r: (ii_r[s], h))
    kspec = pl.BlockSpec((tk, w), lambda h, s, jj_r, ii_r: (jj_r[s], h))
    rspec = pl.BlockSpec((hp, 1, tq), lambda h, s, jj_r, ii_r: (h, 0, ii_r[s]))
    return pl.pallas_call(
        body, name=name,
        grid_spec=pltpu.PrefetchScalarGridSpec(
            num_scalar_prefetch=2, grid=(hd // w, len(pairs)),
            in_specs=[qspec, kspec, kspec, rspec, rspec, qspec],
            out_specs=[pl.BlockSpec((t, w), lambda h, s, jj_r, ii_r: (0, h)), kspec, kspec],
            scratch_shapes=[pltpu.VMEM((tk, w), F32), pltpu.VMEM((tk, w), F32)]),
        out_shape=[jax.ShapeDtypeStruct((t, hd), F32)] * 3,
        compiler_params=_cparams())(jnp.asarray(jj), jnp.asarray(ii), q, k, v, delta_rows, lse_rows, do)


MEM_W = MEM_HEADS * LANES


def _mem_kv_fwd(mem, gmem, wkv, kg, name):
    m, d = mem.shape

    def body(mem_ref, g_ref, w_ref, kg_ref, k_ref, v_ref, mn_ref):
        xv = mem_ref[...]
        mn = (xv * _rstd(xv, d) * g_ref[...]).astype(BF16)
        mn_ref[...] = mn
        kvm = _dot(mn, w_ref[...])
        v_ref[...] = kvm[:, MEM_W:].astype(BF16)
        for h in range(MEM_HEADS):
            sl = slice(h * LANES, (h + 1) * LANES)
            kh = kvm[:, sl]
            k_ref[:, sl] = (kh * _rstd(kh, LANES) * kg_ref[...]).astype(BF16)

    full = lambda a: pl.BlockSpec(a.shape, lambda i: (0, 0))
    return pl.pallas_call(
        body, name=name, grid=(1,), in_specs=[full(mem), full(gmem), full(wkv), full(kg)],
        out_specs=[pl.BlockSpec((m, MEM_W), lambda i: (0, 0)), pl.BlockSpec((m, MEM_W), lambda i: (0, 0)),
                   pl.BlockSpec((m, d), lambda i: (0, 0))],
        out_shape=[jax.ShapeDtypeStruct((m, MEM_W), BF16), jax.ShapeDtypeStruct((m, MEM_W), BF16),
                   jax.ShapeDtypeStruct((m, d), BF16)],
        compiler_params=_cparams())(mem, gmem, wkv, kg)


def _mem_softmax(qn, kh):
    s = _dot_nt(qn, kh) * (LANES ** -0.5)
    e = jnp.exp(s - jnp.max(s, axis=1, keepdims=True))
    return e / jnp.sum(e, axis=1, keepdims=True)


def _mem_attn_fwd(zqm, qg, km, vm, name):
    t = zqm.shape[0]
    tm = _row_tile(t, 512)

    def body(q_ref, qg_ref, k_ref, v_ref, o_ref):
        for h in range(MEM_HEADS):
            sl = slice(h * LANES, (h + 1) * LANES)
            qh = q_ref[:, sl]
            qn = (qh * _rstd(qh, LANES) * qg_ref[...]).astype(BF16)
            p = _mem_softmax(qn, k_ref[:, sl])
            o_ref[:, sl] = _dot(p.astype(BF16), v_ref[:, sl]).astype(BF16)

    row = pl.BlockSpec((tm, MEM_W), lambda i: (i, 0))
    full = lambda a: pl.BlockSpec(a.shape, lambda i: (0, 0))
    return pl.pallas_call(
        body, name=name, grid=(t // tm,), in_specs=[row, full(qg), full(km), full(vm)], out_specs=row,
        out_shape=jax.ShapeDtypeStruct((t, MEM_W), BF16), compiler_params=_cparams())(zqm, qg, km, vm)


def _mem_attn_bwd(zqm, dyc, qg, km, vm, name):
    t = zqm.shape[0]
    m = km.shape[0]
    tm = _row_tile(t, 256)

    def body(q_ref, dy_ref, qg_ref, k_ref, v_ref, dz_ref, dk_ref, dv_ref, dqg_ref):
        @pl.when(pl.program_id(0) == 0)
        def _():
            dk_ref[...] = jnp.zeros_like(dk_ref)
            dv_ref[...] = jnp.zeros_like(dv_ref)
            dqg_ref[...] = jnp.zeros_like(dqg_ref)

        qgv = qg_ref[...]
        dqg_acc = jnp.zeros((tm, LANES), F32)
        for h in range(MEM_HEADS):
            sl = slice(h * LANES, (h + 1) * LANES)
            qh = q_ref[:, sl]
            r = _rstd(qh, LANES)
            qn = (qh * r * qgv).astype(BF16)
            kh = k_ref[:, sl]
            p = _mem_softmax(qn, kh)
            dov = dy_ref[:, sl]
            dv_ref[:, sl] += _dot_tn(p.astype(BF16), dov)
            dp = _dot_nt(dov, v_ref[:, sl])
            ds = (p * (dp - jnp.sum(dp * p, axis=1, keepdims=True)) * (LANES ** -0.5)).astype(BF16)
            dk_ref[:, sl] += _dot_tn(ds, qn)
            dqh, dgr = _rms_vjp(qh, r, qgv, _dot(ds, kh), LANES)
            dz_ref[:, sl] = dqh.astype(BF16)
            dqg_acc += dgr
        dqg_ref[...] += jnp.sum(dqg_acc, axis=0, keepdims=True)

    row = pl.BlockSpec((tm, MEM_W), lambda i: (i, 0))
    full = lambda a: pl.BlockSpec(a.shape, lambda i: (0, 0))
    acc = pl.BlockSpec((m, MEM_W), lambda i: (0, 0))
    return pl.pallas_call(
        body, name=name, grid=(t // tm,), in_specs=[row, row, full(qg), full(km), full(vm)],
        out_specs=[row, acc, acc, pl.BlockSpec((1, LANES), lambda i: (0, 0))],
        out_shape=[jax.ShapeDtypeStruct((t, MEM_W), BF16), jax.ShapeDtypeStruct((m, MEM_W), F32),
                   jax.ShapeDtypeStruct((m, MEM_W), F32), jax.ShapeDtypeStruct((1, LANES), F32)],
        compiler_params=_cparams())(zqm, dyc, qg, km, vm)


def _mem_kv_bwd(mem, gmem, wkv, kg, dkn, dvm, name):
    m, d = mem.shape

    def body(mem_ref, g_ref, w_ref, kg_ref, dk_ref, dv_ref, dw_ref, dkg_ref, dg_ref, dkv_scr):
        xv = mem_ref[...]
        r = _rstd(xv, d)
        mn = (xv * r * g_ref[...]).astype(BF16)
        kvm = _dot(mn, w_ref[...])
        dkv_scr[:, MEM_W:] = dv_ref[...].astype(BF16)
        dkg_acc = jnp.zeros((m, LANES), F32)
        for h in range(MEM_HEADS):
            sl = slice(h * LANES, (h + 1) * LANES)
            kh = kvm[:, sl]
            dkh, dgr = _rms_vjp(kh, _rstd(kh, LANES), kg_ref[...], dk_ref[:, sl], LANES)
            dkv_scr[:, sl] = dkh.astype(BF16)
            dkg_acc += dgr
        dkg_ref[...] = jnp.sum(dkg_acc, axis=0, keepdims=True)
        dkv = dkv_scr[...]
        dw_ref[...] = _dot_tn(mn, dkv)
        dmn = _dot_nt(dkv, w_ref[...])
        dg_ref[...] = jnp.sum(dmn * xv * r, axis=0, keepdims=True)

    full = lambda a: pl.BlockSpec(a.shape, lambda i: (0, 0))
    return pl.pallas_call(
        body, name=name, grid=(1,),
        in_specs=[full(mem), full(gmem), full(wkv), full(kg), full(dkn), full(dvm)],
        out_specs=[pl.BlockSpec((d, 2 * MEM_W), lambda i: (0, 0)), pl.BlockSpec((1, LANES), lambda i: (0, 0)),
                   pl.BlockSpec((1, d), lambda i: (0, 0))],
        out_shape=[jax.ShapeDtypeStruct((d, 2 * MEM_W), F32), jax.ShapeDtypeStruct((1, LANES), F32),
                   jax.ShapeDtypeStruct((1, d), F32)],
        scratch_shapes=[pltpu.VMEM((m, 2 * MEM_W), BF16)],
        compiler_params=_cparams())(mem, gmem, wkv, kg, dkn, dvm)


def _merge_fwd(x1, ya, yb, yc, zg, bg, wa, wb, wc, wo, name):
    t, d = x1.shape
    tm = _row_tile(t, 256)

    def body(x_ref, ya_ref, yb_ref, yc_ref, zg_ref, bg_ref, wa_ref, wb_ref, wc_ref, wo_ref,
             x2_ref, mg_ref, pa_ref, pb_ref, pc_ref):
        merged = None
        for k, (y_ref, w_ref, p_ref) in enumerate(
                ((ya_ref, wa_ref, pa_ref), (yb_ref, wb_ref, pb_ref), (yc_ref, wc_ref, pc_ref))):
            sl = slice(k * d, (k + 1) * d)
            pr = _dot(y_ref[...], w_ref[...])
            p_ref[...] = pr.astype(BF16)
            term = jax.nn.sigmoid(zg_ref[:, sl] + bg_ref[:, sl]) * pr
            merged = term if merged is None else merged + term
        mb = merged.astype(BF16)
        mg_ref[...] = mb
        x2_ref[...] = x_ref[...] + _dot(mb, wo_ref[...])

    row = lambda n: pl.BlockSpec((tm, n), lambda i: (i, 0))
    full = lambda a: pl.BlockSpec(a.shape, lambda i: (0, 0))
    return pl.pallas_call(
        body, name=name, grid=(t // tm,),
        in_specs=[row(d), row(ya.shape[1]), row(yb.shape[1]), row(yc.shape[1]), row(3 * d), full(bg),
                  full(wa), full(wb), full(wc), full(wo)],
        out_specs=[row(d)] * 5,
        out_shape=[jax.ShapeDtypeStruct((t, d), F32)] + [jax.ShapeDtypeStruct((t, d), BF16)] * 4,
        compiler_params=_cparams())(x1, ya, yb, yc, zg, bg, wa, wb, wc, wo)


def _merge_bwd(dx2, pa, pb, pc, zg, bg, wa, wb, wc, wo, name, ex=None):
    t, d = dx2.shape
    tm = _row_tile(t, 256)

    def body(dx_ref, pa_ref, pb_ref, pc_ref, zg_ref, bg_ref, wa_ref, wb_ref, wc_ref, wo_ref,
             dpa_ref, dpb_ref, dpc_ref, dzg_ref, dbg_ref, dya_ref, dyb_ref, dyc_ref):
        @pl.when(pl.program_id(0) == 0)
        def _():
            dbg_ref[...] = jnp.zeros_like(dbg_ref)

        dm = _dot_nt(dx_ref[...].astype(BF16), wo_ref[...])
        for k, (p_ref, w_ref, dp_ref, dy_ref) in enumerate(
                ((pa_ref, wa_ref, dpa_ref, dya_ref), (pb_ref, wb_ref, dpb_ref, dyb_ref),
                 (pc_ref, wc_ref, dpc_ref, dyc_ref))):
            sl = slice(k * d, (k + 1) * d)
            gate = jax.nn.sigmoid(zg_ref[:, sl] + bg_ref[:, sl])
            dpr = (dm * gate).astype(BF16)
            dp_ref[...] = dpr
            dzg = dm * p_ref[...].astype(F32) * gate * (1.0 - gate)
            dzg_ref[:, sl] = dzg.astype(BF16)
            dbg_ref[:, sl] += jnp.sum(dzg, axis=0, keepdims=True)
            dy_ref[...] = _dot_nt(dpr, w_ref[...]).astype(dy_ref.dtype)

    row = lambda n: pl.BlockSpec((tm, n), lambda i: (i, 0))
    full = lambda a: pl.BlockSpec(a.shape, lambda i: (0, 0))
    na, nb, nc = wa.shape[0], wb.shape[0], wc.shape[0]
    return _call_with_exchange(
        ex, body, name, (t // tm,),
        [row(d), row(d), row(d), row(d), row(3 * d), full(bg), full(wa), full(wb), full(wc), full(wo)],
        [row(d), row(d), row(d), row(3 * d), pl.BlockSpec((1, 3 * d), lambda i: (0, 0)), row(na), row(nb), row(nc)],
        [jax.ShapeDtypeStruct((t, d), BF16)] * 3
        + [jax.ShapeDtypeStruct((t, 3 * d), BF16), jax.ShapeDtypeStruct((1, 3 * d), F32),
           jax.ShapeDtypeStruct((t, na), F32), jax.ShapeDtypeStruct((t, nb), BF16),
           jax.ShapeDtypeStruct((t, nc), BF16)],
        [], (dx2, pa, pb, pc, zg, bg, wa, wb, wc, wo))


def _adamw_math(w, g, m, v):
    bc1 = 1.0 - ADAM_B1 ** ADAM_STEP
    bc2 = 1.0 - ADAM_B2 ** ADAM_STEP
    nm = ADAM_B1 * m + (1.0 - ADAM_B1) * g
    nv = ADAM_B2 * v + (1.0 - ADAM_B2) * (g * g)
    delta = -ADAM_LR * ((nm / bc1) / (jnp.sqrt(nv / bc2) + ADAM_EPS) + ADAM_WD * w)
    return delta, nm, nv


def _div_tile(n, cap, mult):
    best = None
    for cand in range(mult, min(n, cap) + 1, mult):
        if n % cand == 0:
            best = cand
    assert best is not None, (n, cap, mult)
    return best


def _adamw(w, g, m, v, name):
    rows, cols = w.shape
    tr = rows if rows * cols <= 256 * 1024 else _div_tile(rows, 256, 8)

    def body(w_ref, g_ref, m_ref, v_ref, d_ref, nm_ref, nv_ref):
        d_ref[...], nm_ref[...], nv_ref[...] = _adamw_math(w_ref[...], g_ref[...], m_ref[...], v_ref[...])

    blk = pl.BlockSpec((tr, cols), lambda i: (i, 0))
    return pl.pallas_call(
        body, name=name, grid=(rows // tr,), in_specs=[blk] * 4, out_specs=[blk] * 3,
        out_shape=[jax.ShapeDtypeStruct((rows, cols), F32)] * 3, compiler_params=_cparams())(w, g, m, v)


def _adamw_slots(w, slots, m, v, name):
    _, hr, cols = w.shape
    tr = _div_tile(hr, 128, 16)

    def body(w_ref, s_ref, m_ref, v_ref, g_ref, d_ref, nm_ref, nv_ref):
        g = s_ref[0, 0].astype(F32)
        for k in range(1, N_CHIPS):
            g = g + s_ref[0, k].astype(F32)
        g_ref[0] = g
        d_ref[0], nm_ref[0], nv_ref[0] = _adamw_math(w_ref[0], g, m_ref[0], v_ref[0])

    blk = pl.BlockSpec((1, tr, cols), lambda h, i: (h, i, 0))
    return pl.pallas_call(
        body, name=name, grid=(2, hr // tr),
        in_specs=[blk, pl.BlockSpec((1, N_CHIPS, tr, cols), lambda h, i: (h, 0, i, 0)), blk, blk],
        out_specs=[blk] * 4, out_shape=[jax.ShapeDtypeStruct((2, hr, cols), F32)] * 4,
        compiler_params=_cparams())(w, slots, m, v)


ANY = pl.BlockSpec(memory_space=pl.ANY)


def _place():
    x, y, c = lax.axis_index("x"), lax.axis_index("y"), lax.axis_index("c")
    other_chips = [(1 - x, y), (x, 1 - y), (1 - x, 1 - y)]
    return x, y, c, other_chips


def _remote(src, dst, send_sem, recv_sem, to):
    return pltpu.make_async_remote_copy(src_ref=src, dst_ref=dst, send_sem=send_sem, recv_sem=recv_sem,
                                        device_id=to, device_id_type=MESH)


def _gather_exchange(shards):
    nw = len(shards)

    def build(s_refs, g_refs, sems):
        send_sems, recv_sems, local_sems = sems
        x, y, c, chips = _place()
        me = 2 * x + y
        sibling = (x, y, 1 - c)
        mine = [pltpu.make_async_copy(s_refs[w], g_refs[w].at[me], local_sems.at[w]) for w in range(nw)]
        first = [_remote(s_refs[w].at[c], g_refs[w].at[me, c], send_sems.at[k, w], recv_sems.at[k, w], (cx, cy, c))
                 for k, (cx, cy) in enumerate(chips) for w in range(nw)]

        def start():
            for cp in mine + first:
                cp.start()

        def finish():
            passed = []
            for k, (cx, cy) in enumerate(chips):
                for w in range(nw):
                    slab = g_refs[w].at[2 * cx + cy, c]
                    _remote(slab, slab, send_sems.at[k, w], recv_sems.at[k, w], (cx, cy, c)).wait_recv()
                    fwd = _remote(slab, slab, send_sems.at[3 + k, w], recv_sems.at[3 + k, w], sibling)
                    fwd.start()
                    passed.append(fwd)
            for k, (cx, cy) in enumerate(chips):
                for w in range(nw):
                    slab = g_refs[w].at[2 * cx + cy, 1 - c]
                    _remote(slab, slab, send_sems.at[3 + k, w], recv_sems.at[3 + k, w], sibling).wait_recv()
            for cp in first + passed:
                cp.wait_send()
            for cp in mine:
                cp.wait()

        return start, finish

    return _Exchange(list(shards), [jax.ShapeDtypeStruct((N_CHIPS,) + s.shape, BF16) for s in shards],
                     [pltpu.SemaphoreType.DMA((6, nw)), pltpu.SemaphoreType.DMA((6, nw)),
                      pltpu.SemaphoreType.DMA((nw,))], build)


def _swap_halves(grads, name):
    nw = len(grads)

    def body(*refs):
        g_refs, sib_refs = refs[:nw], refs[nw:2 * nw]
        send_sems, recv_sems = refs[2 * nw:]
        x, y, c, _ = _place()
        copies = [_remote(g_refs[w].at[s, 1 - c], sib_refs[w].at[s], send_sems.at[s, w], recv_sems.at[s, w],
                          (x, y, 1 - c)) for w in range(nw) for s in range(N_CHIPS)]
        for cp in copies:
            cp.start()
        for cp in copies:
            cp.wait_recv()
        for cp in copies:
            cp.wait_send()

    return pl.pallas_call(
        body, name=name, in_specs=[ANY] * nw, out_specs=[ANY] * nw,
        out_shape=[jax.ShapeDtypeStruct((N_CHIPS,) + g.shape[2:], BF16) for g in grads],
        scratch_shapes=[pltpu.SemaphoreType.DMA((N_CHIPS, nw)), pltpu.SemaphoreType.DMA((N_CHIPS, nw))],
    )(*grads)


def _pair_sum(grad, sib, core, name):
    nchip, _, hr, cols = grad.shape
    tr = _div_tile(hr, 256, 16)

    def body(core_ref, a_ref, b_ref, o_ref):
        o_ref[...] = (a_ref[0].astype(F32) + b_ref[...].astype(F32)).astype(BF16)

    return pl.pallas_call(
        body, name=name,
        grid_spec=pltpu.PrefetchScalarGridSpec(
            num_scalar_prefetch=1, grid=(nchip, hr // tr),
            in_specs=[pl.BlockSpec((1, 1, tr, cols), lambda s, i, core_r: (s, core_r[0], i, 0)),
                      pl.BlockSpec((1, tr, cols), lambda s, i, core_r: (s, i, 0))],
            out_specs=pl.BlockSpec((1, tr, cols), lambda s, i, core_r: (s, i, 0))),
        out_shape=jax.ShapeDtypeStruct((nchip, hr, cols), BF16), compiler_params=_cparams())(core, grad, sib)


def _pair_sum_exchange(sums):
    nw = len(sums)

    def build(p_refs, o_refs, sems):
        send_sems, recv_sems, local_sems = sems
        x, y, c, chips = _place()
        me = 2 * x + y
        sibling = (x, y, 1 - c)
        mine = [pltpu.make_async_copy(p_refs[w].at[me], o_refs[w].at[c, 3], local_sems.at[w]) for w in range(nw)]
        first = [_remote(p_refs[w].at[2 * cx + cy], o_refs[w].at[c, k], send_sems.at[k, w], recv_sems.at[k, w],
                         (cx, cy, c)) for k, (cx, cy) in enumerate(chips) for w in range(nw)]

        def start():
            for cp in mine + first:
                cp.start()

        def finish():
            passed = []
            for k in range(N_CHIPS):
                for w in range(nw):
                    slab = o_refs[w].at[c, k]
                    if k < 3:
                        first[k * nw + w].wait_recv()
                    else:
                        mine[w].wait()
                    fwd = _remote(slab, slab, send_sems.at[3 + k, w], recv_sems.at[3 + k, w], sibling)
                    fwd.start()
                    passed.append(fwd)
            for k in range(N_CHIPS):
                for w in range(nw):
                    slab = o_refs[w].at[1 - c, k]
                    _remote(slab, slab, send_sems.at[3 + k, w], recv_sems.at[3 + k, w], sibling).wait_recv()
            for cp in first + passed:
                cp.wait_send()

        return start, finish

    return _Exchange(list(sums), [jax.ShapeDtypeStruct((2,) + p.shape, BF16) for p in sums],
                     [pltpu.SemaphoreType.DMA((7, nw)), pltpu.SemaphoreType.DMA((7, nw)),
                      pltpu.SemaphoreType.DMA((nw,))], build)


def _allreduce_small(vec):
    m_per, n = vec.shape

    def body(x_ref, out_ref, gath_ref, send_sems, recv_sems, local_sem):
        x, y, c, chips = _place()
        me, sibling = (x, y, c), (x, y, 1 - c)

        def rows(px, py, pc):
            return gath_ref.at[pl.ds((4 * px + 2 * py + pc) * m_per, m_per), :]

        def copy(k, block, to, src=None):
            return pltpu.make_async_remote_copy(
                src_ref=rows(*block) if src is None else src, dst_ref=rows(*block),
                send_sem=send_sems.at[k], recv_sem=recv_sems.at[k], device_id=to, device_id_type=MESH)

        mine = pltpu.make_async_copy(x_ref, rows(*me), local_sem)
        mine.start()
        first = [copy(0, me, sibling, src=x_ref)]
        first += [copy(1 + j, me, (*chip, c), src=x_ref) for j, chip in enumerate(chips)]
        for cp in first:
            cp.start()
        passed = [copy(4 + j, (*chip, c), sibling) for j, chip in enumerate(chips)]
        for j, chip in enumerate(chips):
            copy(1 + j, (*chip, c), me).wait_recv()
            passed[j].start()
        copy(0, sibling, me).wait_recv()
        for j, chip in enumerate(chips):
            copy(4 + j, (*chip, 1 - c), me).wait_recv()
        for cp in first + passed:
            cp.wait_send()
        mine.wait()
        acc = gath_ref[pl.ds(0, m_per), :]
        for k in range(1, N_DEV):
            acc = acc + gath_ref[pl.ds(k * m_per, m_per), :]
        out_ref[...] = acc

    vm = pl.BlockSpec(memory_space=pltpu.VMEM)
    return pl.pallas_call(
        body, name="allreduce_small", in_specs=[vm], out_specs=vm,
        out_shape=jax.ShapeDtypeStruct((m_per, n), F32),
        scratch_shapes=[pltpu.VMEM((N_DEV * m_per, n), F32), pltpu.SemaphoreType.DMA((7,)),
                        pltpu.SemaphoreType.DMA((7,)), pltpu.SemaphoreType.DMA],
    )(vec)


def _pack_small(vals):
    flat = jnp.concatenate([vals[name].reshape(-1).astype(F32) for name, _ in SMALL])
    flat = jnp.pad(flat, (0, SMALL_ROWS * LANES - flat.shape[0]))
    return flat.reshape(SMALL_ROWS, LANES)


def _unpack_small(packed):
    flat = packed.reshape(-1)
    out, off = {}, 0
    for name, shape in SMALL:
        n = int(np.prod(shape))
        out[name] = flat[off:off + n].reshape(shape)
        off += n
    return out


def _head_pad_cols(w, heads, real):
    k = w.shape[0]
    return jnp.pad(w.reshape(k, heads, real), ((0, 0), (0, 0), (0, LANES - real))).reshape(k, heads * LANES)


def _rope_tables(positions):
    half = MLA_ROPE // 2
    inv = ROPE_BASE ** (-jnp.arange(half, dtype=F32) / half)
    ang = positions.astype(F32)[:, None] * inv
    cos, sin = jnp.cos(ang), jnp.sin(ang)
    t = positions.shape[0]
    z = lambda n: jnp.zeros((t, n), F32)
    rc = jnp.concatenate([jnp.ones((t, MLA_NOPE), F32), cos, cos, z(LANES - MLA_QK)], axis=1)
    rs1 = jnp.concatenate([z(MLA_NOPE), -sin, z(LANES - MLA_NOPE - half)], axis=1)
    rs2 = jnp.concatenate([z(MLA_NOPE + half), sin, z(LANES - MLA_QK)], axis=1)
    return rc, rs1, rs2


FFN1_WEIGHTS = ("ffn1_w_gu", "ffn1_w_down")
FFN2_WEIGHTS = ("ffn2_w_gu", "ffn2_w_down")
MIXER_WEIGHTS = tuple(n for n, *_ in SHARDED if n not in FFN1_WEIGHTS + FFN2_WEIGHTS)
SHARD_SHAPE = {n: (r, c, kind) for n, r, c, kind in SHARDED}


def _from_blocks(name, gathered):
    r, c, kind = SHARD_SHAPE[name]
    blk = gathered.reshape(N_CHIPS, r, c)
    return blk, (blk.transpose(1, 0, 2).reshape(r, N_CHIPS * c) if kind == "col" else blk.reshape(N_CHIPS * r, c))


def _grad_pair_sums(names, gw, core, tag):
    by_owner = []
    for name in names:
        r, c, kind = SHARD_SHAPE[name]
        blk = gw[name].reshape(r, N_CHIPS, c).transpose(1, 0, 2) if kind == "col" else gw[name].reshape(N_CHIPS, r, c)
        by_owner.append(blk.astype(BF16).reshape(N_CHIPS, 2, r // 2, c))
    received = _swap_halves(by_owner, "grad_swap_" + tag)
    return [_pair_sum(g, s, core, "pair_sum_" + n) for g, s, n in zip(by_owner, received, names)]


def _device_step(x, mem, positions, tgt, small, shards, core):
    d = D_MODEL
    g_ffn1, g_mix, g_ffn2 = small["ffn1_norm"], small["mix_norm"], small["ffn2_norm"]
    big = {}
    for name, g in zip(FFN1_WEIGHTS, _run_exchange(_gather_exchange([shards[n] for n in FFN1_WEIGHTS]), "gather_ffn1")):
        big[name + "#blocks"], big[name] = _from_blocks(name, g)
    wgu1, wd1 = big["ffn1_w_gu#blocks"], big["ffn1_w_down"].reshape(2, FF_TILE, d)
    x1, gpre1, upre1, *rest = _ffn_fwd(x, g_ffn1, wgu1, wd1, "ffn1_fwd",
                                       ex=_gather_exchange([shards[n] for n in MIXER_WEIGHTS]))
    for name, g in zip(MIXER_WEIGHTS, rest):
        big[name + "#blocks"], big[name] = _from_blocks(name, g)
    w_in = big["w_in"]
    w_uv_, w_cq, w_ckv = w_in[:, :COL_CQ], w_in[:, COL_CQ:COL_CKV], w_in[:, COL_CKV:COL_KR]
    w_kr = jnp.pad(w_in[:, COL_KR:COL_QM], ((0, 0), (MLA_NOPE, LANES - MLA_QK)))
    w_qm, w_g = w_in[:, COL_QM:COL_GATE], w_in[:, COL_GATE:]
    segs = (w_uv_, w_cq, w_ckv, w_kr, w_qm, w_g)
    wuq = _head_pad_cols(big["mla_w_uq"], MLA_HEADS, MLA_QK)
    ukv = big["mla_w_ukv"].reshape(MLA_KV_RANK, MLA_HEADS, 2, MLA_NOPE)
    wuk = _head_pad_cols(ukv[:, :, 0].reshape(MLA_KV_RANK, -1), MLA_HEADS, MLA_NOPE)
    wuv = _head_pad_cols(ukv[:, :, 1].reshape(MLA_KV_RANK, -1), MLA_HEADS, MLA_NOPE)
    wkv = big["mem_w_kv"]
    wa, wc, wo = big["w_branch_a"], big["w_branch_c"], big["w_out"]
    wb = jnp.pad(big["w_branch_b"].reshape(MLA_HEADS, MLA_NOPE, d),
                 ((0, 0), (0, LANES - MLA_NOPE), (0, 0))).reshape(MLA_HEADS * LANES, d)
    qg = jnp.pad(small["mla_q_norm"], ((0, 0), (0, LANES - MLA_QK)))
    kg = jnp.pad(small["mla_k_norm"], ((0, 0), (0, LANES - MLA_QK)))
    causal = jnp.tril(jnp.ones((CHUNK, CHUNK), bool))
    wt_f = jnp.where(causal[None], small["sg_w"][0], 0.0)
    wt, wt_t = wt_f.astype(BF16), wt_f.transpose(0, 2, 1).astype(BF16)
    bias_l = jnp.repeat(small["sg_b"][0].T, 64, axis=1)
    rc, rs1, rs2 = _rope_tables(positions)

    h = _rms_fwd(x1, g_mix, "mix_norm_fwd")
    zuv, zcq, zckv, zkr, zqm, zg = _mm_cols(h, segs, [F32] * 5 + [BF16], "in_proj")
    ya = _sgu_fwd(zuv, small["sg_ln_g"], small["sg_ln_b"], wt, bias_l, "sgu_fwd")
    q, k, v, cqn, ckvn, *rest = _mla_prep_fwd(zcq, zckv, zkr, small["mla_cq_norm"], small["mla_ckv_norm"], qg, kg,
                                              wuq, wuk, wuv, rc, rs1, rs2, "mla_prep_fwd",
                                              ex=_gather_exchange([shards[n] for n in FFN2_WEIGHTS]))
    for name, g in zip(FFN2_WEIGHTS, rest):
        big[name + "#blocks"], big[name] = _from_blocks(name, g)
    wgu2, wd2 = big["ffn2_w_gu#blocks"], big["ffn2_w_down"].reshape(2, FF_TILE, d)
    yb, lse = _attn_fwd(q, k, v, "mla_attn_fwd")
    km, vm, memn = _mem_kv_fwd(mem, small["mem_norm"], wkv, small["mem_k_norm"], "mem_kv_fwd")
    yc = _mem_attn_fwd(zqm, small["mem_q_norm"], km, vm, "mem_attn_fwd")
    x2, merged, pa, pb, pc = _merge_fwd(x1, ya, yb, yc, zg, small["b_gate"], wa, wb, wc, wo, "merge_fwd")
    x3, gpre2, upre2 = _ffn_fwd(x2, g_ffn2, wgu2, wd2, "ffn2_fwd")
    dy, loss_row = _loss_head(x3, tgt, "loss_head")

    gw, gs, slots = {}, {}, {}

    def ffn_grads(prefix, xin, gain, dyin, gpre, upre, wgu, wd, ex=None, ex_names=()):
        dx, dgain, xn, dgt, dup, act, *got = _ffn_bwd(xin, gain, dyin, gpre, upre, wgu, wd, prefix + "_bwd", ex=ex)
        slots.update(zip(ex_names, got))
        gw[prefix + "_w_gu"] = jnp.concatenate(
            [_mm_tn(xn, dgt, prefix + "_dwg"), _mm_tn(xn, dup, prefix + "_dwu")], axis=1)
        gw[prefix + "_w_down"] = _mm_tn(act, dyin, prefix + "_dwd", scale=0.5)
        gs[prefix + "_norm"] = dgain
        return dx

    dx2 = ffn_grads("ffn2", x2, g_ffn2, dy, gpre2, upre2, wgu2, wd2)
    ffn2_sums = _pair_sum_exchange(_grad_pair_sums(FFN2_WEIGHTS, gw, core, "ffn2"))
    dpa, dpb, dpc, dzg, dbg, dya, dyb, dyc, *got = _merge_bwd(dx2, pa, pb, pc, zg, small["b_gate"], wa, wb, wc, wo,
                                                              "merge_bwd", ex=ffn2_sums)
    slots.update(zip(FFN2_WEIGHTS, got))
    gs["b_gate"] = dbg
    gw["w_out"] = _mm_tn(merged, dx2, "dw_out")
    gw["w_branch_a"] = _mm_tn(ya, dpa, "dw_branch_a")
    gw["w_branch_b"] = _mm_tn(yb, dpb, "dw_branch_b").reshape(MLA_HEADS, LANES, d)[:, :MLA_NOPE].reshape(-1, d)
    gw["w_branch_c"] = _mm_tn(yc, dpc, "dw_branch_c")

    dzuv, dwt, dbl, dlg, dlb = _sgu_bwd(zuv, dya, small["sg_ln_g"], small["sg_ln_b"], wt, wt_t, bias_l, "sgu_bwd")
    gs["sg_w"], gs["sg_b"] = dwt[None], dbl[:, :SG_GROUPS].T[None]
    gs["sg_ln_g"], gs["sg_ln_b"] = dlg, dlb

    delta_rows, lse_rows = _attn_bwd_rows(yb, lse, dyb, "mla_attn_bwd_rows")
    dq, dk, dv = _attn_bwd(q, k, v, delta_rows, lse_rows, dyb, "mla_attn_bwd")
    dzcq, dzckv, dzkr, dql, dkl, dgcq, dgckv, dqg, dkg = _mla_prep_bwd(
        zcq, zckv, zkr, small["mla_cq_norm"], small["mla_ckv_norm"], qg, kg, wuq, wuk, wuv, rc, rs1, rs2,
        dq, dk, dv, "mla_prep_bwd")
    gs["mla_cq_norm"], gs["mla_ckv_norm"] = dgcq, dgckv
    gs["mla_q_norm"], gs["mla_k_norm"] = dqg[:, :MLA_QK], dkg[:, :MLA_QK]
    gw["mla_w_uq"] = _mm_tn(cqn, dql, "dw_uq").reshape(MLA_Q_RANK, MLA_HEADS, LANES)[:, :, :MLA_QK].reshape(
        MLA_Q_RANK, -1)
    dwuk = _mm_tn(ckvn, dkl, "dw_uk").reshape(MLA_KV_RANK, MLA_HEADS, LANES)[:, :, :MLA_NOPE]
    dwuv = _mm_tn(ckvn, dv, "dw_uv").reshape(MLA_KV_RANK, MLA_HEADS, LANES)[:, :, :MLA_NOPE]
    gw["mla_w_ukv"] = jnp.concatenate([dwuk, dwuv], axis=2).reshape(MLA_KV_RANK, -1)

    dzqm, dkn, dvm, dmqg = _mem_attn_bwd(zqm, dyc, small["mem_q_norm"], km, vm, "mem_attn_bwd")
    gs["mem_q_norm"] = dmqg
    gw["mem_w_kv"], gs["mem_k_norm"], gs["mem_norm"] = _mem_kv_bwd(
        mem, small["mem_norm"], wkv, small["mem_k_norm"], dkn, dvm, "mem_kv_bwd")

    dzs = (dzuv, dzcq, dzckv, dzkr, dzqm, dzg)
    dh = _mm([(dz, w.T) for dz, w in zip(dzs, segs)], F32, "in_proj_bwd")
    dws = list(_mm_tn_cols(h, dzs[:5], "dw_in_narrow")) + [_mm_tn(h, dzg, "dw_in_gate")]
    dws[3] = dws[3][:, MLA_NOPE:MLA_QK]
    gw["w_in"] = jnp.concatenate(dws, axis=1)
    dx1, gs["mix_norm"] = _rms_bwd(x1, g_mix, dh, dx2, "mix_norm_bwd")
    mixer_sums = _pair_sum_exchange(_grad_pair_sums(MIXER_WEIGHTS, gw, core, "mixer"))
    dx = ffn_grads("ffn1", x, g_ffn1, dx1, gpre1, upre1, wgu1, wd1, ex=mixer_sums, ex_names=MIXER_WEIGHTS)
    ffn1_sums = _pair_sum_exchange(_grad_pair_sums(FFN1_WEIGHTS, gw, core, "ffn1"))
    slots.update(zip(FFN1_WEIGHTS, _run_exchange(ffn1_sums, "grad_exchange_ffn1")))
    return loss_row, dx, slots, gs


def kernel(x, mem, positions, ffn1_norm, ffn1_w_gu, ffn1_w_down, mix_norm, w_in, b_gate, sg_ln_g, sg_ln_b, sg_w, sg_b, mla_cq_norm, mla_w_uq, mla_ckv_norm, mla_w_ukv, mla_q_norm, mla_k_norm, mem_norm, mem_w_kv, mem_q_norm, mem_k_norm, w_branch_a, w_branch_b, w_branch_c, w_out, ffn2_norm, ffn2_w_gu, ffn2_w_down, loss_target, m_ffn1_norm, m_ffn1_w_gu, m_ffn1_w_down, m_mix_norm, m_w_in, m_b_gate, m_sg_ln_g, m_sg_ln_b, m_sg_w, m_sg_b, m_mla_cq_norm, m_mla_w_uq, m_mla_ckv_norm, m_mla_w_ukv, m_mla_q_norm, m_mla_k_norm, m_mem_norm, m_mem_w_kv, m_mem_q_norm, m_mem_k_norm, m_w_branch_a, m_w_branch_b, m_w_branch_c, m_w_out, m_ffn2_norm, m_ffn2_w_gu, m_ffn2_w_down, v_ffn1_norm, v_ffn1_w_gu, v_ffn1_w_down, v_mix_norm, v_w_in, v_b_gate, v_sg_ln_g, v_sg_ln_b, v_sg_w, v_sg_b, v_mla_cq_norm, v_mla_w_uq, v_mla_ckv_norm, v_mla_w_ukv, v_mla_q_norm, v_mla_k_norm, v_mem_norm, v_mem_w_kv, v_mem_q_norm, v_mem_k_norm, v_w_branch_a, v_w_branch_b, v_w_branch_c, v_w_out, v_ffn2_norm, v_ffn2_w_gu, v_ffn2_w_down):
    args = dict(locals())
    weights = {n: args[n] for n in WEIGHT_ORDER}
    mom_m = {n: args["m_" + n] for n in WEIGHT_ORDER}
    mom_v = {n: args["v_" + n] for n in WEIGHT_ORDER}
    small = {n: weights[n] for n, _ in SMALL}
    halves = lambda a, r, c: a.reshape(2, r // 2, c)

    shards = {n: halves(weights[n][0].astype(BF16), r, c) for n, r, c, _ in SHARDED}
    core = lax.axis_index("c").astype(jnp.int32).reshape(1)
    loss_row, dx, slots, gs = _device_step(x[0], mem[0], positions[0], loss_target[0], small, shards, core)
    loss = lax.psum(loss_row[0, 0], ("x", "y", "c"))
    small_grads = _unpack_small(_allreduce_small(_pack_small(gs)))

    grads, deltas, new_m, new_v = {}, {}, {}, {}
    for name, r, c, _ in SHARDED:
        outs = _adamw_slots(halves(weights[name][0], r, c), slots[name], halves(mom_m[name][0], r, c),
                            halves(mom_v[name][0], r, c), "adamw_" + name)
        shape = weights[name].shape
        grads[name], deltas[name], new_m[name], new_v[name] = [o.reshape(shape) for o in outs]
    dlt, nm, nv = _adamw(_pack_small(small), _pack_small(small_grads), _pack_small({n: mom_m[n] for n, _ in SMALL}),
                         _pack_small({n: mom_v[n] for n, _ in SMALL}), "adamw_small")
    for name, _ in SMALL:
        grads[name] = small_grads[name]
    deltas.update(_unpack_small(dlt))
    new_m.update(_unpack_small(nm))
    new_v.update(_unpack_small(nv))

    return (loss, dx[None], *[grads[n] for n in WEIGHT_ORDER], *[deltas[n] for n in WEIGHT_ORDER],
            *[new_m[n] for n in WEIGHT_ORDER], *[new_v[n] for n in WEIGHT_ORDER])
```

```python
import functools
from typing import Callable, NamedTuple

import numpy as np
import jax
import jax.numpy as jnp
from jax import lax
from jax.experimental import pallas as pl
from jax.experimental.pallas import tpu as pltpu

F32 = jnp.float32
BF16 = jnp.bfloat16

D_MODEL = 1024
D_FF = 2816
FF_TILE = 1408
SG_WIDTH = 512
SG_GROUPS = 8
CHUNK = 128
MLA_HEADS = 8
MLA_QK = 96
MLA_NOPE = 64
MLA_ROPE = 32
MLA_Q_RANK = 384
MLA_KV_RANK = 256
MEM_HEADS = 4
MEM_LEN = 256
LANES = 128
EPS = 1e-6
NEG = -1e30
ROPE_BASE = 10000.0
N_CHIPS = 4
N_DEV = 8

ADAM_LR = 0.001
ADAM_B1 = 0.9
ADAM_B2 = 0.999
ADAM_EPS = 1e-08
ADAM_WD = 0.01
ADAM_STEP = 10

COL_V = 512
COL_CQ = 1024
COL_CKV = 1408
COL_KR = 1664
COL_QM = 1696
COL_GATE = 2208
IN_COLS = 5280

VMEM_LIMIT_BYTES = 56 * 1024 * 1024
INV_SQRT2 = 0.7071067811865476
INV_SQRT_2PI = 0.3989422804014327
LOG2E = 1.4426950408889634
ATTN_SCALE = MLA_QK ** -0.5
V_ONES_LANE = 64
ATTN_SCALE2 = ATTN_SCALE * LOG2E

SHARDED = (
    ("ffn1_w_gu", 1024, 1408, "col"),
    ("ffn1_w_down", 704, 1024, "row"),
    ("w_in", 1024, 1320, "col"),
    ("mla_w_uq", 384, 192, "col"),
    ("mla_w_ukv", 256, 256, "col"),
    ("mem_w_kv", 256, 1024, "row"),
    ("w_branch_a", 512, 256, "col"),
    ("w_branch_b", 512, 256, "col"),
    ("w_branch_c", 512, 256, "col"),
    ("w_out", 256, 1024, "row"),
    ("ffn2_w_gu", 1024, 1408, "col"),
    ("ffn2_w_down", 704, 1024, "row"),
)
SMALL = (
    ("ffn1_norm", (1, 1024)), ("mix_norm", (1, 1024)), ("b_gate", (1, 3072)),
    ("sg_ln_g", (1, 512)), ("sg_ln_b", (1, 512)), ("sg_w", (1, 8, 128, 128)),
    ("sg_b", (1, 8, 128)), ("mla_cq_norm", (1, 384)), ("mla_ckv_norm", (1, 256)),
    ("mla_q_norm", (1, 96)), ("mla_k_norm", (1, 96)), ("mem_norm", (1, 1024)),
    ("mem_q_norm", (1, 128)), ("mem_k_norm", (1, 128)), ("ffn2_norm", (1, 1024)),
)
WEIGHT_ORDER = (
    "ffn1_norm", "ffn1_w_gu", "ffn1_w_down", "mix_norm", "w_in", "b_gate", "sg_ln_g", "sg_ln_b",
    "sg_w", "sg_b", "mla_cq_norm", "mla_w_uq", "mla_ckv_norm", "mla_w_ukv", "mla_q_norm",
    "mla_k_norm", "mem_norm", "mem_w_kv", "mem_q_norm", "mem_k_norm", "w_branch_a", "w_branch_b",
    "w_branch_c", "w_out", "ffn2_norm", "ffn2_w_gu", "ffn2_w_down",
)

_N_SMALL = sum(int(np.prod(s)) for _, s in SMALL)
SMALL_ROWS = -(-_N_SMALL // (LANES * 8)) * 8

MESH = pl.DeviceIdType.MESH


def _cparams():
    return pltpu.CompilerParams(vmem_limit_bytes=VMEM_LIMIT_BYTES)


def _dot(a, b):
    return jnp.dot(a, b, preferred_element_type=F32)


def _dot_nt(a, b):
    return lax.dot_general(a, b, (((1,), (1,)), ((), ())), preferred_element_type=F32)


def _dot_tn(a, b):
    return lax.dot_general(a, b, (((0,), (0,)), ((), ())), preferred_element_type=F32)


def _gelu(x):
    return 0.5 * x * (1.0 + lax.erf(x * INV_SQRT2))


def _gelu_grad(x):
    return 0.5 * (1.0 + lax.erf(x * INV_SQRT2)) + x * jnp.exp(-0.5 * x * x) * INV_SQRT_2PI


def _rstd(x, n):
    return lax.rsqrt(jnp.sum(x * x, axis=-1, keepdims=True) * (1.0 / n) + EPS)


def _rms_vjp(x, r, g, dy, n):
    dxh = dy * g
    dx = r * dxh - x * (r * r * r) * (jnp.sum(dxh * x, axis=-1, keepdims=True) * (1.0 / n))
    return dx, dy * x * r


def _row_tile(t, want):
    return min(t, want)


def _wide_tile(n):
    if n <= 1024:
        return n
    if n % 1024 == 0:
        return 1024
    assert n % FF_TILE == 0, n
    return FF_TILE


def _rms_fwd(x, g, name):
    t, d = x.shape
    tm = _row_tile(t, 512)

    def body(x_ref, g_ref, o_ref):
        xv = x_ref[...]
        o_ref[...] = (xv * _rstd(xv, d) * g_ref[...]).astype(BF16)

    return pl.pallas_call(
        body, name=name, grid=(t // tm,),
        in_specs=[pl.BlockSpec((tm, d), lambda i: (i, 0)), pl.BlockSpec((1, d), lambda i: (0, 0))],
        out_specs=pl.BlockSpec((tm, d), lambda i: (i, 0)),
        out_shape=jax.ShapeDtypeStruct((t, d), BF16), compiler_params=_cparams())(x, g)


def _rms_bwd(x, g, dxn, dres, name):
    t, d = x.shape
    tm = _row_tile(t, 256)

    def body(x_ref, g_ref, d_ref, r_ref, dx_ref, dg_ref):
        @pl.when(pl.program_id(0) == 0)
        def _():
            dg_ref[...] = jnp.zeros_like(dg_ref)

        xv = x_ref[...]
        r = _rstd(xv, d)
        dx, dgr = _rms_vjp(xv, r, g_ref[...], d_ref[...].astype(F32), d)
        dx_ref[...] = r_ref[...] + dx
        dg_ref[...] += jnp.sum(dgr, axis=0, keepdims=True)

    row = pl.BlockSpec((tm, d), lambda i: (i, 0))
    vec = pl.BlockSpec((1, d), lambda i: (0, 0))
    return pl.pallas_call(
        body, name=name, grid=(t // tm,), in_specs=[row, vec, row, row], out_specs=[row, vec],
        out_shape=[jax.ShapeDtypeStruct((t, d), F32), jax.ShapeDtypeStruct((1, d), F32)],
        compiler_params=_cparams())(x, g, dxn, dres)


def _mm(pairs, out_dtype, name):
    t = pairs[0][0].shape[0]
    n = pairs[0][1].shape[1]
    tm = _row_tile(t, 512)
    tn = _wide_tile(n)
    np_ = len(pairs)

    def body(*refs):
        o_ref = refs[2 * np_]
        acc = None
        for a_ref, w_ref in zip(refs[:np_], refs[np_:2 * np_]):
            part = _dot(a_ref[...].astype(BF16), w_ref[...])
            acc = part if acc is None else acc + part
        o_ref[...] = acc.astype(out_dtype)

    in_specs = [pl.BlockSpec((tm, a.shape[1]), lambda i, j: (i, 0)) for a, _ in pairs]
    in_specs += [pl.BlockSpec((w.shape[0], tn), lambda i, j: (0, j)) for _, w in pairs]
    return pl.pallas_call(
        body, name=name, grid=(t // tm, n // tn), in_specs=in_specs,
        out_specs=pl.BlockSpec((tm, tn), lambda i, j: (i, j)),
        out_shape=jax.ShapeDtypeStruct((t, n), out_dtype), compiler_params=_cparams(),
    )(*[a for a, _ in pairs], *[w for _, w in pairs])


def _mm_cols(a, ws, out_dtypes, name, ex=None):
    t, kdim = a.shape
    tm = _row_tile(t, 256)
    n = len(ws)

    def body(*refs):
        av = refs[0][...]
        for w_ref, o_ref in zip(refs[1:1 + n], refs[1 + n:]):
            o_ref[...] = _dot(av, w_ref[...]).astype(o_ref.dtype)

    row = lambda width: pl.BlockSpec((tm, width), lambda i: (i, 0))
    return _call_with_exchange(
        ex, body, name, (t // tm,),
        [row(kdim)] + [pl.BlockSpec(w.shape, lambda i: (0, 0)) for w in ws],
        [row(w.shape[1]) for w in ws],
        [jax.ShapeDtypeStruct((t, w.shape[1]), dt) for w, dt in zip(ws, out_dtypes)], [], (a, *ws))


def _mm_tn_cols(a, bs, name):
    t, m = a.shape
    tk = _row_tile(t, 512)
    n = len(bs)

    def body(*refs):
        @pl.when(pl.program_id(0) == 0)
        def _():
            for o_ref in refs[1 + n:]:
                o_ref[...] = jnp.zeros_like(o_ref)

        av = refs[0][...].astype(BF16)
        for b_ref, o_ref in zip(refs[1:1 + n], refs[1 + n:]):
            o_ref[...] += _dot_tn(av, b_ref[...].astype(BF16))

    row = lambda width: pl.BlockSpec((tk, width), lambda k: (k, 0))
    return pl.pallas_call(
        body, name=name, grid=(t // tk,), in_specs=[row(m)] + [row(b.shape[1]) for b in bs],
        out_specs=[pl.BlockSpec((m, b.shape[1]), lambda k: (0, 0)) for b in bs],
        out_shape=[jax.ShapeDtypeStruct((m, b.shape[1]), F32) for b in bs],
        compiler_params=_cparams())(a, *bs)


def _mm_tn(a, b, name, scale=1.0):
    t, m = a.shape
    n = b.shape[1]
    tm, tn = _wide_tile(m), _wide_tile(n)
    tk = _row_tile(t, 1024)
    nk = t // tk

    def body(a_ref, b_ref, o_ref):
        k = pl.program_id(2)

        @pl.when(k == 0)
        def _():
            o_ref[...] = jnp.zeros_like(o_ref)

        o_ref[...] += _dot_tn(a_ref[...].astype(BF16), b_ref[...].astype(BF16))
        if scale != 1.0:
            @pl.when(k == nk - 1)
            def _():
                o_ref[...] = o_ref[...] * scale

    return pl.pallas_call(
        body, name=name, grid=(m // tm, n // tn, nk),
        in_specs=[pl.BlockSpec((tk, tm), lambda i, j, k: (k, i)),
                  pl.BlockSpec((tk, tn), lambda i, j, k: (k, j))],
        out_specs=pl.BlockSpec((tm, tn), lambda i, j, k: (i, j)),
        out_shape=jax.ShapeDtypeStruct((m, n), F32), compiler_params=_cparams())(a, b)


class _Exchange(NamedTuple):
    operands: list
    out_shapes: list
    sem_shapes: list
    build: Callable


def _call_with_exchange(ex, body, name, grid, in_specs, out_specs, out_shape, scratch_shapes, operands):
    if ex is None:
        return pl.pallas_call(body, name=name, grid=grid, in_specs=in_specs, out_specs=out_specs, out_shape=out_shape,
                              scratch_shapes=scratch_shapes, compiler_params=_cparams())(*operands)
    n_in, n_out, n_scr = len(in_specs), len(out_specs), len(scratch_shapes)
    k_in, k_out = len(ex.operands), len(ex.out_shapes)

    def carried(*refs):
        a, b = n_in, n_in + k_in
        c, e = b + n_out, b + n_out + k_out
        f = e + n_scr
        start, finish = ex.build(refs[a:b], refs[c:e], refs[f:])
        steps = [pl.program_id(ax) for ax in range(len(grid))]
        first = functools.reduce(jnp.logical_and, [s == 0 for s in steps])
        last = functools.reduce(jnp.logical_and, [s == n - 1 for s, n in zip(steps, grid)])
        pl.when(first)(start)
        body(*refs[:a], *refs[b:c], *refs[e:f])
        pl.when(last)(finish)

    return pl.pallas_call(
        carried, name=name, grid=grid, in_specs=list(in_specs) + [ANY] * k_in,
        out_specs=list(out_specs) + [ANY] * k_out, out_shape=list(out_shape) + list(ex.out_shapes),
        scratch_shapes=list(scratch_shapes) + list(ex.sem_shapes), compiler_params=_cparams(),
    )(*operands, *ex.operands)


def _run_exchange(ex, name):
    k_in, k_out = len(ex.operands), len(ex.out_shapes)

    def body(*refs):
        start, finish = ex.build(refs[:k_in], refs[k_in:k_in + k_out], refs[k_in + k_out:])
        start()
        finish()

    return pl.pallas_call(body, name=name, in_specs=[ANY] * k_in, out_specs=[ANY] * k_out,
                          out_shape=list(ex.out_shapes), scratch_shapes=list(ex.sem_shapes))(*ex.operands)


def _ffn_fwd(x, g, wgu4, wd2, name, ex=None):
    t, d = x.shape
    tm = _row_tile(t, 512)

    def body(x_ref, g_ref, wg_ref, wu_ref, wd_ref, o_ref, gg_ref, uu_ref, xn_scr, acc_scr):
        j = pl.program_id(1)

        @pl.when(j == 0)
        def _():
            xv = x_ref[...]
            xn_scr[...] = (xv * _rstd(xv, d) * g_ref[...]).astype(BF16)
            acc_scr[...] = jnp.zeros_like(acc_scr)

        xn = xn_scr[...]
        gg = _dot(xn, wg_ref[0])
        uu = _dot(xn, wu_ref[0])
        gg_ref[...] = gg.astype(BF16)
        uu_ref[...] = uu.astype(BF16)
        act = gg * jax.nn.sigmoid(gg) * uu
        acc_scr[...] += _dot(act.astype(BF16), wd_ref[0])

        @pl.when(j == 1)
        def _():
            o_ref[...] = x_ref[...] + 0.5 * acc_scr[...]

    row = pl.BlockSpec((tm, d), lambda i, j: (i, 0))
    ffb = pl.BlockSpec((tm, FF_TILE), lambda i, j: (i, j))
    return _call_with_exchange(
        ex, body, name, (t // tm, 2),
        [row, pl.BlockSpec((1, d), lambda i, j: (0, 0)),
         pl.BlockSpec((1, d, FF_TILE), lambda i, j: (j, 0, 0)),
         pl.BlockSpec((1, d, FF_TILE), lambda i, j: (j + 2, 0, 0)),
         pl.BlockSpec((1, FF_TILE, d), lambda i, j: (j, 0, 0))],
        [row, ffb, ffb],
        [jax.ShapeDtypeStruct((t, d), F32), jax.ShapeDtypeStruct((t, D_FF), BF16), jax.ShapeDtypeStruct((t, D_FF), BF16)],
        [pltpu.VMEM((tm, d), BF16), pltpu.VMEM((tm, d), F32)], (x, g, wgu4, wgu4, wd2))


def _ffn_bwd(x, g, dy, gpre, upre, wgu4, wd2, name, ex=None):
    t, d = x.shape
    tm = _row_tile(t, 512)

    def body(dy_ref, gg_ref, uu_ref, wgu_hbm, wd_hbm, dg_ref, du_ref, act_ref, part_ref, wg_ref, wu_ref, wd_ref):
        j = pl.program_id(0)

        @pl.when(pl.program_id(1) == 0)
        def _():
            pltpu.sync_copy(wgu_hbm.at[j], wg_ref.at[0])
            pltpu.sync_copy(wgu_hbm.at[j + 2], wu_ref.at[0])
            pltpu.sync_copy(wd_hbm.at[j], wd_ref.at[0])

        gg = gg_ref[...].astype(F32)
        uu = uu_ref[...].astype(F32)
        sg = jax.nn.sigmoid(gg)
        silu = gg * sg
        act_ref[...] = (silu * uu).astype(BF16)
        dyh = (0.5 * dy_ref[...]).astype(BF16)
        dact = _dot_nt(dyh, wd_ref[0])
        du = (dact * silu).astype(BF16)
        dgt = (dact * uu * (sg * (1.0 + gg * (1.0 - sg)))).astype(BF16)
        du_ref[...] = du
        dg_ref[...] = dgt
        part_ref[0] = _dot_nt(dgt, wg_ref[0]) + _dot_nt(du, wu_ref[0])

    row = pl.BlockSpec((tm, d), lambda j, i: (i, 0))
    ffb = pl.BlockSpec((tm, FF_TILE), lambda j, i: (i, j))
    dgt, dup, act, parts, *got = _call_with_exchange(
        ex, body, name, (2, t // tm),
        [row, ffb, ffb, ANY, ANY],
        [ffb, ffb, ffb, pl.BlockSpec((1, tm, d), lambda j, i: (j, i, 0))],
        [jax.ShapeDtypeStruct((t, D_FF), BF16)] * 3 + [jax.ShapeDtypeStruct((2, t, d), F32)],
        [pltpu.VMEM((1, d, FF_TILE), BF16), pltpu.VMEM((1, d, FF_TILE), BF16), pltpu.VMEM((1, FF_TILE, d), BF16)],
        (dy, gpre, upre, wgu4, wd2))

    def norm_body(x_ref, g_ref, p_ref, dy_ref, dx_ref, dgain_ref, xn_ref):
        @pl.when(pl.program_id(0) == 0)
        def _():
            dgain_ref[...] = jnp.zeros_like(dgain_ref)

        xv = x_ref[...]
        r = _rstd(xv, d)
        xn_ref[...] = (xv * r * g_ref[...]).astype(BF16)
        dx, dgr = _rms_vjp(xv, r, g_ref[...], p_ref[0] + p_ref[1], d)
        dx_ref[...] = dy_ref[...] + dx
        dgain_ref[...] += jnp.sum(dgr, axis=0, keepdims=True)

    tn = _row_tile(t, 256)
    nrow = pl.BlockSpec((tn, d), lambda i: (i, 0))
    vec = pl.BlockSpec((1, d), lambda i: (0, 0))
    dx, dgain, xn = pl.pallas_call(
        norm_body, name=name + "_norm", grid=(t // tn,),
        in_specs=[nrow, vec, pl.BlockSpec((2, tn, d), lambda i: (0, i, 0)), nrow],
        out_specs=[nrow, vec, nrow],
        out_shape=[jax.ShapeDtypeStruct((t, d), F32), jax.ShapeDtypeStruct((1, d), F32),
                   jax.ShapeDtypeStruct((t, d), BF16)],
        compiler_params=_cparams())(x, g, parts, dy)
    return [dx, dgain, xn, dgt, dup, act] + got


def _loss_head(y, tgt, name):
    t, d = y.shape
    tm = _row_tile(t, 512)

    def body(y_ref, t_ref, dy_ref, loss_ref):
        @pl.when(pl.program_id(0) == 0)
        def _():
            loss_ref[...] = jnp.zeros_like(loss_ref)

        e = y_ref[...] - t_ref[...]
        dy_ref[...] = e * (1.0 / d)
        part = 0.5 * jnp.sum(jnp.sum(e * e, axis=-1, keepdims=True) * (1.0 / d), axis=0, keepdims=True)
        loss_ref[...] += jnp.broadcast_to(part, loss_ref.shape)

    row = pl.BlockSpec((tm, d), lambda i: (i, 0))
    return pl.pallas_call(
        body, name=name, grid=(t // tm,), in_specs=[row, row],
        out_specs=[row, pl.BlockSpec((1, LANES), lambda i: (0, 0))],
        out_shape=[jax.ShapeDtypeStruct((t, d), F32), jax.ShapeDtypeStruct((1, LANES), F32)],
        compiler_params=_cparams())(y, tgt)


def _sgu_layernorm(vpre, lg, lb):
    v = _gelu(vpre)
    mu = jnp.mean(v, axis=-1, keepdims=True)
    xc = v - mu
    rstd = lax.rsqrt(jnp.mean(xc * xc, axis=-1, keepdims=True) + EPS)
    xhat = xc * rstd
    return xhat, rstd, xhat * lg + lb


def _sgu_fwd(zuv, lg, lb, wt, bias_l, name):
    t = zuv.shape[0]
    tm = _row_tile(t, 512)

    def body(u_ref, v_ref, lg_ref, lb_ref, wt_ref, bl_ref, o_ref, vln_scr):
        _, _, vln = _sgu_layernorm(v_ref[...], lg_ref[...], lb_ref[...])
        vln_scr[...] = vln.astype(BF16)
        lo = lax.broadcasted_iota(jnp.int32, (CHUNK, LANES), 1) < 64
        for c in range(tm // CHUNK):
            rows = slice(c * CHUNK, (c + 1) * CHUNK)
            for p in range(SG_GROUPS // 2):
                cols = slice(p * LANES, (p + 1) * LANES)
                vp = vln_scr[rows, cols]
                mixed = jnp.where(lo, _dot(wt_ref[2 * p], vp), _dot(wt_ref[2 * p + 1], vp)) + bl_ref[:, cols]
                o_ref[rows, cols] = (_gelu(u_ref[rows, cols]) * mixed).astype(BF16)

    half = lambda k: pl.BlockSpec((tm, SG_WIDTH), lambda i: (i, k))
    vec = pl.BlockSpec((1, SG_WIDTH), lambda i: (0, 0))
    return pl.pallas_call(
        body, name=name, grid=(t // tm,),
        in_specs=[half(0), half(1), vec, vec,
                  pl.BlockSpec((SG_GROUPS, CHUNK, CHUNK), lambda i: (0, 0, 0)),
                  pl.BlockSpec((CHUNK, SG_WIDTH), lambda i: (0, 0))],
        out_specs=pl.BlockSpec((tm, SG_WIDTH), lambda i: (i, 0)),
        out_shape=jax.ShapeDtypeStruct((t, SG_WIDTH), BF16),
        scratch_shapes=[pltpu.VMEM((tm, SG_WIDTH), BF16)],
        compiler_params=_cparams())(zuv, zuv, lg, lb, wt, bias_l)


def _sgu_bwd(zuv, dya, lg, lb, wt, wt_t, bias_l, name):
    t = zuv.shape[0]
    tm = _row_tile(t, 256)
    nsteps = t // tm

    def body(u_ref, v_ref, dy_ref, lg_ref, lb_ref, wt_ref, wtt_ref, bl_ref,
             dz_ref, dwt_ref, dbl_ref, dlg_ref, dlb_ref, vln_scr, dvln_scr, dbacc_scr):
        step = pl.program_id(0)

        @pl.when(step == 0)
        def _():
            dwt_ref[...] = jnp.zeros_like(dwt_ref)
            dlg_ref[...] = jnp.zeros_like(dlg_ref)
            dlb_ref[...] = jnp.zeros_like(dlb_ref)
            dbl_ref[...] = jnp.zeros_like(dbl_ref)
            dbacc_scr[...] = jnp.zeros_like(dbacc_scr)

        vpre = v_ref[...]
        lgv = lg_ref[...]
        xhat, rstd, vln = _sgu_layernorm(vpre, lgv, lb_ref[...])
        vln_scr[...] = vln.astype(BF16)
        lo = lax.broadcasted_iota(jnp.int32, (CHUNK, LANES), 1) < 64
        for c in range(tm // CHUNK):
            rows = slice(c * CHUNK, (c + 1) * CHUNK)
            for p in range(SG_GROUPS // 2):
                cols = slice(p * LANES, (p + 1) * LANES)
                vp = vln_scr[rows, cols]
                mixed = jnp.where(lo, _dot(wt_ref[2 * p], vp), _dot(wt_ref[2 * p + 1], vp)) + bl_ref[:, cols]
                upre = u_ref[rows, cols]
                dyp = dy_ref[rows, cols]
                dz_ref[rows, cols] = (dyp * mixed * _gelu_grad(upre)).astype(BF16)
                dm = dyp * _gelu(upre)
                dbacc_scr[:, cols] += dm
                dlo = jnp.where(lo, dm, 0.0).astype(BF16)
                dhi = jnp.where(lo, 0.0, dm).astype(BF16)
                dvln_scr[rows, cols] = _dot(wtt_ref[2 * p], dlo) + _dot(wtt_ref[2 * p + 1], dhi)
                dwt_ref[2 * p] += _dot_nt(dlo, vp)
                dwt_ref[2 * p + 1] += _dot_nt(dhi, vp)
        dvln = dvln_scr[...]
        dlg_ref[...] += jnp.sum(dvln * xhat, axis=0, keepdims=True)
        dlb_ref[...] += jnp.sum(dvln, axis=0, keepdims=True)
        dxh = dvln * lgv
        dv = rstd * (dxh - jnp.mean(dxh, axis=-1, keepdims=True)
                     - xhat * jnp.mean(dxh * xhat, axis=-1, keepdims=True))
        dz_ref[:, SG_WIDTH:] = (dv * _gelu_grad(vpre)).astype(BF16)

        @pl.when(step == nsteps - 1)
        def _():
            rr = lax.broadcasted_iota(jnp.int32, (CHUNK, CHUNK), 0)
            cc = lax.broadcasted_iota(jnp.int32, (CHUNK, CHUNK), 1)
            tril = (cc <= rr).astype(F32)
            for gidx in range(SG_GROUPS):
                dwt_ref[gidx] = dwt_ref[gidx] * tril
            kk = lax.broadcasted_iota(jnp.int32, (SG_WIDTH, LANES), 0)
            gg = lax.broadcasted_iota(jnp.int32, (SG_WIDTH, LANES), 1)
            sel = ((kk // 64) == gg).astype(F32)
            dbl_ref[...] = jnp.dot(dbacc_scr[...], sel, preferred_element_type=F32,
                                   precision=lax.Precision.HIGHEST)

    half = lambda k: pl.BlockSpec((tm, SG_WIDTH), lambda i: (i, k))
    vec = pl.BlockSpec((1, SG_WIDTH), lambda i: (0, 0))
    wspec = pl.BlockSpec((SG_GROUPS, CHUNK, CHUNK), lambda i: (0, 0, 0))
    return pl.pallas_call(
        body, name=name, grid=(nsteps,),
        in_specs=[half(0), half(1), pl.BlockSpec((tm, SG_WIDTH), lambda i: (i, 0)), vec, vec,
                  wspec, wspec, pl.BlockSpec((CHUNK, SG_WIDTH), lambda i: (0, 0))],
        out_specs=[pl.BlockSpec((tm, 2 * SG_WIDTH), lambda i: (i, 0)), wspec,
                   pl.BlockSpec((CHUNK, LANES), lambda i: (0, 0)), vec, vec],
        out_shape=[jax.ShapeDtypeStruct((t, 2 * SG_WIDTH), BF16),
                   jax.ShapeDtypeStruct((SG_GROUPS, CHUNK, CHUNK), F32),
                   jax.ShapeDtypeStruct((CHUNK, LANES), F32),
                   jax.ShapeDtypeStruct((1, SG_WIDTH), F32), jax.ShapeDtypeStruct((1, SG_WIDTH), F32)],
        scratch_shapes=[pltpu.VMEM((tm, SG_WIDTH), BF16), pltpu.VMEM((tm, SG_WIDTH), F32),
                        pltpu.VMEM((CHUNK, SG_WIDTH), F32)],
        compiler_params=_cparams())(zuv, zuv, dya, lg, lb, wt, wt_t, bias_l)


def _rope(x, c, s1, s2):
    return x * c + pltpu.roll(x, LANES - 16, 1) * s1 + pltpu.roll(x, 16, 1) * s2


def _rope_t(dy, c, s1, s2):
    return dy * c + pltpu.roll(dy * s1, 16, 1) + pltpu.roll(dy * s2, LANES - 16, 1)


def _mla_prep_fwd(zcq, zckv, zkr, gcq, gckv, qg, kg, wuq, wuk, wuv, rc, rs1, rs2, name, ex=None):
    t = zcq.shape[0]
    tm = _row_tile(t, 256)
    hd = MLA_HEADS * LANES

    def body(zcq_ref, zckv_ref, zkr_ref, gcq_ref, gckv_ref, qg_ref, kg_ref, wuq_ref, wuk_ref, wuv_ref,
             c_ref, s1_ref, s2_ref, q_ref, k_ref, v_ref, cqn_ref, ckvn_ref):
        c, s1, s2 = c_ref[...], s1_ref[...], s2_ref[...]
        xq = zcq_ref[...]
        cqn = (xq * _rstd(xq, MLA_Q_RANK) * gcq_ref[...]).astype(BF16)
        cqn_ref[...] = cqn
        ql = _dot(cqn, wuq_ref[...])
        xk = zckv_ref[...]
        ckvn = (xk * _rstd(xk, MLA_KV_RANK) * gckv_ref[...]).astype(BF16)
        ckvn_ref[...] = ckvn
        kl = _dot(ckvn, wuk_ref[...])
        slot_lane = lax.broadcasted_iota(jnp.int32, (tm, hd), 1) % LANES
        v_ref[...] = jnp.where(slot_lane == V_ONES_LANE, 1.0, _dot(ckvn, wuv_ref[...])).astype(BF16)
        kr = zkr_ref[...]
        for h in range(MLA_HEADS):
            sl = slice(h * LANES, (h + 1) * LANES)
            qh = ql[:, sl]
            q_ref[:, sl] = (_rope(qh * _rstd(qh, MLA_QK) * qg_ref[...], c, s1, s2) * ATTN_SCALE2).astype(BF16)
            kh = kl[:, sl] + kr
            k_ref[:, sl] = _rope(kh * _rstd(kh, MLA_QK) * kg_ref[...], c, s1, s2).astype(BF16)

    row = lambda n: pl.BlockSpec((tm, n), lambda i: (i, 0))
    full = lambda a: pl.BlockSpec(a.shape, lambda i: (0, 0))
    return _call_with_exchange(
        ex, body, name, (t // tm,),
        [row(MLA_Q_RANK), row(MLA_KV_RANK), row(LANES), full(gcq), full(gckv), full(qg), full(kg),
         full(wuq), full(wuk), full(wuv), row(LANES), row(LANES), row(LANES)],
        [row(hd), row(hd), row(hd), row(MLA_Q_RANK), row(MLA_KV_RANK)],
        [jax.ShapeDtypeStruct((t, hd), BF16)] * 3
        + [jax.ShapeDtypeStruct((t, MLA_Q_RANK), BF16), jax.ShapeDtypeStruct((t, MLA_KV_RANK), BF16)],
        [], (zcq, zckv, zkr, gcq, gckv, qg, kg, wuq, wuk, wuv, rc, rs1, rs2))


def _mla_prep_bwd(zcq, zckv, zkr, gcq, gckv, qg, kg, wuq, wuk, wuv, rc, rs1, rs2, dq, dk, dv, name):
    t = zcq.shape[0]
    tm = _row_tile(t, 256)
    hd = MLA_HEADS * LANES

    def body(zcq_ref, zckv_ref, zkr_ref, gcq_ref, gckv_ref, qg_ref, kg_ref, wuq_ref, wuk_ref, wuv_ref,
             c_ref, s1_ref, s2_ref, dq_ref, dk_ref, dv_ref,
             dzcq_ref, dzckv_ref, dzkr_ref, dql_ref, dkl_ref, dgcq_ref, dgckv_ref, dqg_ref, dkg_ref):
        @pl.when(pl.program_id(0) == 0)
        def _():
            for ref in (dgcq_ref, dgckv_ref, dqg_ref, dkg_ref):
                ref[...] = jnp.zeros_like(ref)

        c, s1, s2 = c_ref[...], s1_ref[...], s2_ref[...]
        qgv, kgv = qg_ref[...], kg_ref[...]
        xq = zcq_ref[...]
        rq = _rstd(xq, MLA_Q_RANK)
        ql = _dot((xq * rq * gcq_ref[...]).astype(BF16), wuq_ref[...])
        xk = zckv_ref[...]
        rk = _rstd(xk, MLA_KV_RANK)
        kl = _dot((xk * rk * gckv_ref[...]).astype(BF16), wuk_ref[...])
        kr = zkr_ref[...]
        dqg_acc = jnp.zeros((tm, LANES), F32)
        dkg_acc = jnp.zeros((tm, LANES), F32)
        dkr = jnp.zeros((tm, LANES), F32)
        for h in range(MLA_HEADS):
            sl = slice(h * LANES, (h + 1) * LANES)
            qh = ql[:, sl]
            dqh, dgr = _rms_vjp(qh, _rstd(qh, MLA_QK), qgv, _rope_t(dq_ref[:, sl], c, s1, s2), MLA_QK)
            dql_ref[:, sl] = dqh.astype(BF16)
            dqg_acc += dgr
            kh = kl[:, sl] + kr
            dkh, dgr = _rms_vjp(kh, _rstd(kh, MLA_QK), kgv, _rope_t(dk_ref[:, sl], c, s1, s2), MLA_QK)
            dkl_ref[:, sl] = dkh.astype(BF16)
            dkg_acc += dgr
            dkr += dkh
        dqg_ref[...] += jnp.sum(dqg_acc, axis=0, keepdims=True)
        dkg_ref[...] += jnp.sum(dkg_acc, axis=0, keepdims=True)
        lane = lax.broadcasted_iota(jnp.int32, (tm, LANES), 1)
        dzkr_ref[...] = jnp.where((lane >= MLA_NOPE) & (lane < MLA_QK), dkr, 0.0).astype(BF16)
        dcqn = _dot_nt(dql_ref[...], wuq_ref[...])
        dx, dgr = _rms_vjp(xq, rq, gcq_ref[...], dcqn, MLA_Q_RANK)
        dzcq_ref[...] = dx.astype(BF16)
        dgcq_ref[...] += jnp.sum(dgr, axis=0, keepdims=True)
        dckvn = _dot_nt(dkl_ref[...], wuk_ref[...]) + _dot_nt(dv_ref[...].astype(BF16), wuv_ref[...])
        dx, dgr = _rms_vjp(xk, rk, gckv_ref[...], dckvn, MLA_KV_RANK)
        dzckv_ref[...] = dx.astype(BF16)
        dgckv_ref[...] += jnp.sum(dgr, axis=0, keepdims=True)

    row = lambda n: pl.BlockSpec((tm, n), lambda i: (i, 0))
    full = lambda a: pl.BlockSpec(a.shape, lambda i: (0, 0))
    vec = lambda n: pl.BlockSpec((1, n), lambda i: (0, 0))
    return pl.pallas_call(
        body, name=name, grid=(t // tm,),
        in_specs=[row(MLA_Q_RANK), row(MLA_KV_RANK), row(LANES), full(gcq), full(gckv), full(qg), full(kg),
                  full(wuq), full(wuk), full(wuv), row(LANES), row(LANES), row(LANES), row(hd), row(hd), row(hd)],
        out_specs=[row(MLA_Q_RANK), row(MLA_KV_RANK), row(LANES), row(hd), row(hd),
                   vec(MLA_Q_RANK), vec(MLA_KV_RANK), vec(LANES), vec(LANES)],
        out_shape=[jax.ShapeDtypeStruct((t, MLA_Q_RANK), BF16), jax.ShapeDtypeStruct((t, MLA_KV_RANK), BF16),
                   jax.ShapeDtypeStruct((t, LANES), BF16), jax.ShapeDtypeStruct((t, hd), BF16),
                   jax.ShapeDtypeStruct((t, hd), BF16), jax.ShapeDtypeStruct((1, MLA_Q_RANK), F32),
                   jax.ShapeDtypeStruct((1, MLA_KV_RANK), F32), jax.ShapeDtypeStruct((1, LANES), F32),
                   jax.ShapeDtypeStruct((1, LANES), F32)],
        compiler_params=_cparams(),
    )(zcq, zckv, zkr, gcq, gckv, qg, kg, wuq, wuk, wuv, rc, rs1, rs2, dq, dk, dv)


def _attn_tiles(t):
    tq = 512 if t >= 2048 else 128
    return tq, min(t, 4 * tq), min(t, 2 * tq)


def _causal_keep(tq, tk, i, j):
    row = lax.broadcasted_iota(jnp.int32, (tq, tk), 0)
    col = lax.broadcasted_iota(jnp.int32, (tq, tk), 1)
    return (col - row) <= (i * tq - j * tk)


def _causal_keep_t(tq, tk, i, j):
    key = lax.broadcasted_iota(jnp.int32, (tk, tq), 0)
    qry = lax.broadcasted_iota(jnp.int32, (tk, tq), 1)
    return (key - qry) <= (i * tq - j * tk)


ATTN_FWD_HEADS_PER_STEP = 2
ATTN_BWD_HEADS_PER_STEP = 2


def _attn_fwd(q, k, v, name):
    t, hd = q.shape
    hp = ATTN_FWD_HEADS_PER_STEP
    tq, tk, _ = _attn_tiles(t)
    pairs = [(i, j) for i in range(t // tq) for j in range(((i + 1) * tq - 1) // tk + 1)]
    ii = np.array([p[0] for p in pairs], np.int32)
    jj = np.array([p[1] for p in pairs], np.int32)

    def body(ii_ref, jj_ref, q_ref, k_ref, v_ref, o_ref, lse_ref, m_scr, acc_scr):
        s_id = pl.program_id(1)
        i, j = ii_ref[s_id], jj_ref[s_id]
        last = j == ((i + 1) * tq - 1) // tk
        ones_lane = lax.broadcasted_iota(jnp.int32, (tq, LANES), 1) == V_ONES_LANE

        @pl.when(j == 0)
        def _():
            m_scr[...] = jnp.full_like(m_scr, NEG)
            acc_scr[...] = jnp.zeros_like(acc_scr)

        def step(masked):
            for hh in range(hp):
                sl = slice(hh * LANES, (hh + 1) * LANES)
                s = _dot_nt(q_ref[:, sl], k_ref[:, sl])
                if masked:
                    s = jnp.where(_causal_keep(tq, tk, i, j), s, NEG)
                m_prev = m_scr[hh]
                m_new = jnp.maximum(m_prev, jnp.max(s, axis=1, keepdims=True))
                p = jnp.exp2(s - m_new)
                alpha = jnp.exp2(m_prev - m_new)
                acc = alpha * acc_scr[:, sl] + _dot(p.astype(BF16), v_ref[:, sl])
                if masked:
                    l_new = jnp.sum(jnp.where(ones_lane, acc, 0.0), axis=1, keepdims=True)
                    o_ref[:, sl] = (acc / l_new).astype(BF16)
                    lse_ref[:, sl] = jnp.broadcast_to(m_new + jnp.log(l_new) * LOG2E, (tq, LANES))
                else:
                    acc_scr[:, sl] = acc
                    m_scr[hh] = m_new

        @pl.when(jnp.logical_not(last))
        def _():
            step(False)

        @pl.when(last)
        def _():
            step(True)

    w = hp * LANES
    qspec = pl.BlockSpec((tq, w), lambda h, s, ii_r, jj_r: (ii_r[s], h))
    kspec = pl.BlockSpec((tk, w), lambda h, s, ii_r, jj_r: (jj_r[s], h))
    return pl.pallas_call(
        body, name=name,
        grid_spec=pltpu.PrefetchScalarGridSpec(
            num_scalar_prefetch=2, grid=(hd // w, len(pairs)), in_specs=[qspec, kspec, kspec],
            out_specs=[qspec, qspec],
            scratch_shapes=[pltpu.VMEM((hp, tq, 1), F32), pltpu.VMEM((tq, w), F32)]),
        out_shape=[jax.ShapeDtypeStruct((t, hd), BF16), jax.ShapeDtypeStruct((t, hd), F32)],
        compiler_params=_cparams())(jnp.asarray(ii), jnp.asarray(jj), q, k, v)


def _attn_bwd_rows(o, lse, do, name):
    t, hd = o.shape
    heads = hd // LANES
    tm = _row_tile(t, 512)

    def body(o_ref, lse_ref, do_ref, out_ref):
        lane = lax.broadcasted_iota(jnp.int32, (tm, LANES), 1)
        acc = jnp.zeros((tm, LANES), F32)
        for h in range(heads):
            sl = slice(h * LANES, (h + 1) * LANES)
            delta = jnp.sum(do_ref[:, sl].astype(F32) * o_ref[:, sl].astype(F32), axis=1, keepdims=True)
            acc = jnp.where(lane == h, delta, acc)
            acc = jnp.where(lane == heads + h, lse_ref[:, sl], acc)
        out_ref[...] = acc

    row = pl.BlockSpec((tm, hd), lambda i: (i, 0))
    cols = pl.pallas_call(
        body, name=name, grid=(t // tm,), in_specs=[row, row, row],
        out_specs=pl.BlockSpec((tm, LANES), lambda i: (i, 0)),
        out_shape=jax.ShapeDtypeStruct((t, LANES), F32), compiler_params=_cparams())(o, lse, do)
    rows = cols.T
    return rows[:heads].reshape(heads, 1, t), rows[heads:2 * heads].reshape(heads, 1, t)


def _attn_bwd(q, k, v, delta_rows, lse_rows, do, name):
    t, hd = q.shape
    hp = ATTN_BWD_HEADS_PER_STEP
    tq, _, tk = _attn_tiles(t)
    nq = t // tq
    pairs = [(i, j) for j in range(t // tk) for i in range((j * tk) // tq, nq)]
    ii = np.array([p[0] for p in pairs], np.int32)
    jj = np.array([p[1] for p in pairs], np.int32)

    def body(jj_ref, ii_ref, q_ref, k_ref, v_ref, delta_ref, lse_ref, do_ref, dq_ref, dk_ref, dv_ref,
             dk_scr, dv_scr):
        s_id = pl.program_id(1)
        i, j = ii_ref[s_id], jj_ref[s_id]

        @pl.when(s_id == 0)
        def _():
            dq_ref[...] = jnp.zeros_like(dq_ref)

        @pl.when(i == (j * tk) // tq)
        def _():
            dk_scr[...] = jnp.zeros_like(dk_scr)
            dv_scr[...] = jnp.zeros_like(dv_scr)

        rows = pl.ds(pl.multiple_of(i * tq, tq), tq)

        def step(masked):
            for hh in range(hp):
                sl = slice(hh * LANES, (hh + 1) * LANES)
                qv, kv, dov = q_ref[:, sl], k_ref[:, sl], do_ref[:, sl]
                st = _dot_nt(kv, qv)
                if masked:
                    st = jnp.where(_causal_keep_t(tq, tk, i, j), st, NEG)
                pt = jnp.exp2(st - lse_ref[hh])
                dv_scr[:, sl] += _dot(pt.astype(BF16), dov)
                dpt = _dot_nt(v_ref[:, sl], dov)
                dst = (pt * (dpt - delta_ref[hh]) * ATTN_SCALE).astype(BF16)
                dk_scr[:, sl] += _dot(dst, qv)
                dq_ref[rows, sl] += _dot_tn(dst, kv)

        crosses = (j + 1) * tk - 1 > i * tq

        @pl.when(jnp.logical_not(crosses))
        def _():
            step(False)

        @pl.when(crosses)
        def _():
            step(True)

        @pl.when(i == nq - 1)
        def _():
            dk_ref[...] = dk_scr[...] * (1.0 / ATTN_SCALE2)
            dv_ref[...] = dv_scr[...]

    w = hp * LANES
    qspec = pl.BlockSpec((tq, w), lambda h, s, jj_r, ii_r: (ii_r[s], h))
    kspec = pl.BlockSpec((tk, w), lambda h, s, jj_r, ii_r: (jj_r[s], h))
    rspec = pl.BlockSpec((hp, 1, tq), lambda h, s, jj_r, ii_r: (h, 0, ii_r[s]))
    return pl.pallas_call(
        body, name=name,
        grid_spec=pltpu.PrefetchScalarGridSpec(
            num_scalar_prefetch=2, grid=(hd // w, len(pairs)),
            in_specs=[qspec, kspec, kspec, rspec, rspec, qspec],
            out_specs=[pl.BlockSpec((t, w), lambda h, s, jj_r, ii_r: (0, h)), kspec, kspec],
            scratch_shapes=[pltpu.VMEM((tk, w), F32), pltpu.VMEM((tk, w), F32)]),
        out_shape=[jax.ShapeDtypeStruct((t, hd), F32)] * 3,
        compiler_params=_cparams())(jnp.asarray(jj), jnp.asarray(ii), q, k, v, delta_rows, lse_rows, do)


MEM_W = MEM_HEADS * LANES


def _mem_kv_fwd(mem, gmem, wkv, kg, name):
    m, d = mem.shape

    def body(mem_ref, g_ref, w_ref, kg_ref, k_ref, v_ref, mn_ref):
        xv = mem_ref[...]
        mn = (xv * _rstd(xv, d) * g_ref[...]).astype(BF16)
        mn_ref[...] = mn
        kvm = _dot(mn, w_ref[...])
        v_ref[...] = kvm[:, MEM_W:].astype(BF16)
        for h in range(MEM_HEADS):
            sl = slice(h * LANES, (h + 1) * LANES)
            kh = kvm[:, sl]
            k_ref[:, sl] = (kh * _rstd(kh, LANES) * kg_ref[...]).astype(BF16)

    full = lambda a: pl.BlockSpec(a.shape, lambda i: (0, 0))
    return pl.pallas_call(
        body, name=name, grid=(1,), in_specs=[full(mem), full(gmem), full(wkv), full(kg)],
        out_specs=[pl.BlockSpec((m, MEM_W), lambda i: (0, 0)), pl.BlockSpec((m, MEM_W), lambda i: (0, 0)),
                   pl.BlockSpec((m, d), lambda i: (0, 0))],
        out_shape=[jax.ShapeDtypeStruct((m, MEM_W), BF16), jax.ShapeDtypeStruct((m, MEM_W), BF16),
                   jax.ShapeDtypeStruct((m, d), BF16)],
        compiler_params=_cparams())(mem, gmem, wkv, kg)


def _mem_softmax(qn, kh):
    s = _dot_nt(qn, kh) * (LANES ** -0.5)
    e = jnp.exp(s - jnp.max(s, axis=1, keepdims=True))
    return e / jnp.sum(e, axis=1, keepdims=True)


def _mem_attn_fwd(zqm, qg, km, vm, name):
    t = zqm.shape[0]
    tm = _row_tile(t, 512)

    def body(q_ref, qg_ref, k_ref, v_ref, o_ref):
        for h in range(MEM_HEADS):
            sl = slice(h * LANES, (h + 1) * LANES)
            qh = q_ref[:, sl]
            qn = (qh * _rstd(qh, LANES) * qg_ref[...]).astype(BF16)
            p = _mem_softmax(qn, k_ref[:, sl])
            o_ref[:, sl] = _dot(p.astype(BF16), v_ref[:, sl]).astype(BF16)

    row = pl.BlockSpec((tm, MEM_W), lambda i: (i, 0))
    full = lambda a: pl.BlockSpec(a.shape, lambda i: (0, 0))
    return pl.pallas_call(
        body, name=name, grid=(t // tm,), in_specs=[row, full(qg), full(km), full(vm)], out_specs=row,
        out_shape=jax.ShapeDtypeStruct((t, MEM_W), BF16), compiler_params=_cparams())(zqm, qg, km, vm)


def _mem_attn_bwd(zqm, dyc, qg, km, vm, name):
    t = zqm.shape[0]
    m = km.shape[0]
    tm = _row_tile(t, 256)

    def body(q_ref, dy_ref, qg_ref, k_ref, v_ref, dz_ref, dk_ref, dv_ref, dqg_ref):
        @pl.when(pl.program_id(0) == 0)
        def _():
            dk_ref[...] = jnp.zeros_like(dk_ref)
            dv_ref[...] = jnp.zeros_like(dv_ref)
            dqg_ref[...] = jnp.zeros_like(dqg_ref)

        qgv = qg_ref[...]
        dqg_acc = jnp.zeros((tm, LANES), F32)
        for h in range(MEM_HEADS):
            sl = slice(h * LANES, (h + 1) * LANES)
            qh = q_ref[:, sl]
            r = _rstd(qh, LANES)
            qn = (qh * r * qgv).astype(BF16)
            kh = k_ref[:, sl]
            p = _mem_softmax(qn, kh)
            dov = dy_ref[:, sl]
            dv_ref[:, sl] += _dot_tn(p.astype(BF16), dov)
            dp = _dot_nt(dov, v_ref[:, sl])
            ds = (p * (dp - jnp.sum(dp * p, axis=1, keepdims=True)) * (LANES ** -0.5)).astype(BF16)
            dk_ref[:, sl] += _dot_tn(ds, qn)
            dqh, dgr = _rms_vjp(qh, r, qgv, _dot(ds, kh), LANES)
            dz_ref[:, sl] = dqh.astype(BF16)
            dqg_acc += dgr
        dqg_ref[...] += jnp.sum(dqg_acc, axis=0, keepdims=True)

    row = pl.BlockSpec((tm, MEM_W), lambda i: (i, 0))
    full = lambda a: pl.BlockSpec(a.shape, lambda i: (0, 0))
    acc = pl.BlockSpec((m, MEM_W), lambda i: (0, 0))
    return pl.pallas_call(
        body, name=name, grid=(t // tm,), in_specs=[row, row, full(qg), full(km), full(vm)],
        out_specs=[row, acc, acc, pl.BlockSpec((1, LANES), lambda i: (0, 0))],
        out_shape=[jax.ShapeDtypeStruct((t, MEM_W), BF16), jax.ShapeDtypeStruct((m, MEM_W), F32),
                   jax.ShapeDtypeStruct((m, MEM_W), F32), jax.ShapeDtypeStruct((1, LANES), F32)],
        compiler_params=_cparams())(zqm, dyc, qg, km, vm)


def _mem_kv_bwd(mem, gmem, wkv, kg, dkn, dvm, name):
    m, d = mem.shape

    def body(mem_ref, g_ref, w_ref, kg_ref, dk_ref, dv_ref, dw_ref, dkg_ref, dg_ref, dkv_scr):
        xv = mem_ref[...]
        r = _rstd(xv, d)
        mn = (xv * r * g_ref[...]).astype(BF16)
        kvm = _dot(mn, w_ref[...])
        dkv_scr[:, MEM_W:] = dv_ref[...].astype(BF16)
        dkg_acc = jnp.zeros((m, LANES), F32)
        for h in range(MEM_HEADS):
            sl = slice(h * LANES, (h + 1) * LANES)
            kh = kvm[:, sl]
            dkh, dgr = _rms_vjp(kh, _rstd(kh, LANES), kg_ref[...], dk_ref[:, sl], LANES)
            dkv_scr[:, sl] = dkh.astype(BF16)
            dkg_acc += dgr
        dkg_ref[...] = jnp.sum(dkg_acc, axis=0, keepdims=True)
        dkv = dkv_scr[...]
        dw_ref[...] = _dot_tn(mn, dkv)
        dmn = _dot_nt(dkv, w_ref[...])
        dg_ref[...] = jnp.sum(dmn * xv * r, axis=0, keepdims=True)

    full = lambda a: pl.BlockSpec(a.shape, lambda i: (0, 0))
    return pl.pallas_call(
        body, name=name, grid=(1,),
        in_specs=[full(mem), full(gmem), full(wkv), full(kg), full(dkn), full(dvm)],
        out_specs=[pl.BlockSpec((d, 2 * MEM_W), lambda i: (0, 0)), pl.BlockSpec((1, LANES), lambda i: (0, 0)),
                   pl.BlockSpec((1, d), lambda i: (0, 0))],
        out_shape=[jax.ShapeDtypeStruct((d, 2 * MEM_W), F32), jax.ShapeDtypeStruct((1, LANES), F32),
                   jax.ShapeDtypeStruct((1, d), F32)],
        scratch_shapes=[pltpu.VMEM((m, 2 * MEM_W), BF16)],
        compiler_params=_cparams())(mem, gmem, wkv, kg, dkn, dvm)


def _merge_fwd(x1, ya, yb, yc, zg, bg, wa, wb, wc, wo, name):
    t, d = x1.shape
    tm = _row_tile(t, 256)

    def body(x_ref, ya_ref, yb_ref, yc_ref, zg_ref, bg_ref, wa_ref, wb_ref, wc_ref, wo_ref,
             x2_ref, mg_ref, pa_ref, pb_ref, pc_ref):
        merged = None
        for k, (y_ref, w_ref, p_ref) in enumerate(
                ((ya_ref, wa_ref, pa_ref), (yb_ref, wb_ref, pb_ref), (yc_ref, wc_ref, pc_ref))):
            sl = slice(k * d, (k + 1) * d)
            pr = _dot(y_ref[...], w_ref[...])
            p_ref[...] = pr.astype(BF16)
            term = jax.nn.sigmoid(zg_ref[:, sl] + bg_ref[:, sl]) * pr
            merged = term if merged is None else merged + term
        mb = merged.astype(BF16)
        mg_ref[...] = mb
        x2_ref[...] = x_ref[...] + _dot(mb, wo_ref[...])

    row = lambda n: pl.BlockSpec((tm, n), lambda i: (i, 0))
    full = lambda a: pl.BlockSpec(a.shape, lambda i: (0, 0))
    return pl.pallas_call(
        body, name=name, grid=(t // tm,),
        in_specs=[row(d), row(ya.shape[1]), row(yb.shape[1]), row(yc.shape[1]), row(3 * d), full(bg),
                  full(wa), full(wb), full(wc), full(wo)],
        out_specs=[row(d)] * 5,
        out_shape=[jax.ShapeDtypeStruct((t, d), F32)] + [jax.ShapeDtypeStruct((t, d), BF16)] * 4,
        compiler_params=_cparams())(x1, ya, yb, yc, zg, bg, wa, wb, wc, wo)


def _merge_bwd(dx2, pa, pb, pc, zg, bg, wa, wb, wc, wo, name, ex=None):
    t, d = dx2.shape
    tm = _row_tile(t, 256)

    def body(dx_ref, pa_ref, pb_ref, pc_ref, zg_ref, bg_ref, wa_ref, wb_ref, wc_ref, wo_ref,
             dpa_ref, dpb_ref, dpc_ref, dzg_ref, dbg_ref, dya_ref, dyb_ref, dyc_ref):
        @pl.when(pl.program_id(0) == 0)
        def _():
            dbg_ref[...] = jnp.zeros_like(dbg_ref)

        dm = _dot_nt(dx_ref[...].astype(BF16), wo_ref[...])
        for k, (p_ref, w_ref, dp_ref, dy_ref) in enumerate(
                ((pa_ref, wa_ref, dpa_ref, dya_ref), (pb_ref, wb_ref, dpb_ref, dyb_ref),
                 (pc_ref, wc_ref, dpc_ref, dyc_ref))):
            sl = slice(k * d, (k + 1) * d)
            gate = jax.nn.sigmoid(zg_ref[:, sl] + bg_ref[:, sl])
            dpr = (dm * gate).astype(BF16)
            dp_ref[...] = dpr
            dzg = dm * p_ref[...].astype(F32) * gate * (1.0 - gate)
            dzg_ref[:, sl] = dzg.astype(BF16)
            dbg_ref[:, sl] += jnp.sum(dzg, axis=0, keepdims=True)
            dy_ref[...] = _dot_nt(dpr, w_ref[...]).astype(dy_ref.dtype)

    row = lambda n: pl.BlockSpec((tm, n), lambda i: (i, 0))
    full = lambda a: pl.BlockSpec(a.shape, lambda i: (0, 0))
    na, nb, nc = wa.shape[0], wb.shape[0], wc.shape[0]
    return _call_with_exchange(
        ex, body, name, (t // tm,),
        [row(d), row(d), row(d), row(d), row(3 * d), full(bg), full(wa), full(wb), full(wc), full(wo)],
        [row(d), row(d), row(d), row(3 * d), pl.BlockSpec((1, 3 * d), lambda i: (0, 0)), row(na), row(nb), row(nc)],
        [jax.ShapeDtypeStruct((t, d), BF16)] * 3
        + [jax.ShapeDtypeStruct((t, 3 * d), BF16), jax.ShapeDtypeStruct((1, 3 * d), F32),
           jax.ShapeDtypeStruct((t, na), F32), jax.ShapeDtypeStruct((t, nb), BF16),
           jax.ShapeDtypeStruct((t, nc), BF16)],
        [], (dx2, pa, pb, pc, zg, bg, wa, wb, wc, wo))


def _adamw_math(w, g, m, v):
    bc1 = 1.0 - ADAM_B1 ** ADAM_STEP
    bc2 = 1.0 - ADAM_B2 ** ADAM_STEP
    nm = ADAM_B1 * m + (1.0 - ADAM_B1) * g
    nv = ADAM_B2 * v + (1.0 - ADAM_B2) * (g * g)
    delta = -ADAM_LR * ((nm / bc1) / (jnp.sqrt(nv / bc2) + ADAM_EPS) + ADAM_WD * w)
    return delta, nm, nv


def _div_tile(n, cap, mult):
    best = None
    for cand in range(mult, min(n, cap) + 1, mult):
        if n % cand == 0:
            best = cand
    assert best is not None, (n, cap, mult)
    return best


def _adamw(w, g, m, v, name):
    rows, cols = w.shape
    tr = rows if rows * cols <= 256 * 1024 else _div_tile(rows, 256, 8)

    def body(w_ref, g_ref, m_ref, v_ref, d_ref, nm_ref, nv_ref):
        d_ref[...], nm_ref[...], nv_ref[...] = _adamw_math(w_ref[...], g_ref[...], m_ref[...], v_ref[...])

    blk = pl.BlockSpec((tr, cols), lambda i: (i, 0))
    return pl.pallas_call(
        body, name=name, grid=(rows // tr,), in_specs=[blk] * 4, out_specs=[blk] * 3,
        out_shape=[jax.ShapeDtypeStruct((rows, cols), F32)] * 3, compiler_params=_cparams())(w, g, m, v)


def _adamw_slots(w, slots, m, v, name):
    _, hr, cols = w.shape
    tr = _div_tile(hr, 128, 16)

    def body(w_ref, s_ref, m_ref, v_ref, g_ref, d_ref, nm_ref, nv_ref):
        g = s_ref[0, 0].astype(F32)
        for k in range(1, N_CHIPS):
            g = g + s_ref[0, k].astype(F32)
        g_ref[0] = g
        d_ref[0], nm_ref[0], nv_ref[0] = _adamw_math(w_ref[0], g, m_ref[0], v_ref[0])

    blk = pl.BlockSpec((1, tr, cols), lambda h, i: (h, i, 0))
    return pl.pallas_call(
        body, name=name, grid=(2, hr // tr),
        in_specs=[blk, pl.BlockSpec((1, N_CHIPS, tr, cols), lambda h, i: (h, 0, i, 0)), blk, blk],
        out_specs=[blk] * 4, out_shape=[jax.ShapeDtypeStruct((2, hr, cols), F32)] * 4,
        compiler_params=_cparams())(w, slots, m, v)


ANY = pl.BlockSpec(memory_space=pl.ANY)


def _place():
    x, y, c = lax.axis_index("x"), lax.axis_index("y"), lax.axis_index("c")
    other_chips = [(1 - x, y), (x, 1 - y), (1 - x, 1 - y)]
    return x, y, c, other_chips


def _remote(src, dst, send_sem, recv_sem, to):
    return pltpu.make_async_remote_copy(src_ref=src, dst_ref=dst, send_sem=send_sem, recv_sem=recv_sem,
                                        device_id=to, device_id_type=MESH)


def _gather_exchange(shards):
    nw = len(shards)

    def build(s_refs, g_refs, sems):
        send_sems, recv_sems, local_sems = sems
        x, y, c, chips = _place()
        me = 2 * x + y
        sibling = (x, y, 1 - c)
        mine = [pltpu.make_async_copy(s_refs[w], g_refs[w].at[me], local_sems.at[w]) for w in range(nw)]
        first = [_remote(s_refs[w].at[c], g_refs[w].at[me, c], send_sems.at[k, w], recv_sems.at[k, w], (cx, cy, c))
                 for k, (cx, cy) in enumerate(chips) for w in range(nw)]

        def start():
            for cp in mine + first:
                cp.start()

        def finish():
            passed = []
            for k, (cx, cy) in enumerate(chips):
                for w in range(nw):
                    slab = g_refs[w].at[2 * cx + cy, c]
                    _remote(slab, slab, send_sems.at[k, w], recv_sems.at[k, w], (cx, cy, c)).wait_recv()
                    fwd = _remote(slab, slab, send_sems.at[3 + k, w], recv_sems.at[3 + k, w], sibling)
                    fwd.start()
                    passed.append(fwd)
            for k, (cx, cy) in enumerate(chips):
                for w in range(nw):
                    slab = g_refs[w].at[2 * cx + cy, 1 - c]
                    _remote(slab, slab, send_sems.at[3 + k, w], recv_sems.at[3 + k, w], sibling).wait_recv()
            for cp in first + passed:
                cp.wait_send()
            for cp in mine:
                cp.wait()

        return start, finish

    return _Exchange(list(shards), [jax.ShapeDtypeStruct((N_CHIPS,) + s.shape, BF16) for s in shards],
                     [pltpu.SemaphoreType.DMA((6, nw)), pltpu.SemaphoreType.DMA((6, nw)),
                      pltpu.SemaphoreType.DMA((nw,))], build)


def _swap_halves(grads, name):
    nw = len(grads)

    def body(*refs):
        g_refs, sib_refs = refs[:nw], refs[nw:2 * nw]
        send_sems, recv_sems = refs[2 * nw:]
        x, y, c, _ = _place()
        copies = [_remote(g_refs[w].at[s, 1 - c], sib_refs[w].at[s], send_sems.at[s, w], recv_sems.at[s, w],
                          (x, y, 1 - c)) for w in range(nw) for s in range(N_CHIPS)]
        for cp in copies:
            cp.start()
        for cp in copies:
            cp.wait_recv()
        for cp in copies:
            cp.wait_send()

    return pl.pallas_call(
        body, name=name, in_specs=[ANY] * nw, out_specs=[ANY] * nw,
        out_shape=[jax.ShapeDtypeStruct((N_CHIPS,) + g.shape[2:], BF16) for g in grads],
        scratch_shapes=[pltpu.SemaphoreType.DMA((N_CHIPS, nw)), pltpu.SemaphoreType.DMA((N_CHIPS, nw))],
    )(*grads)


def _pair_sum(grad, sib, core, name):
    nchip, _, hr, cols = grad.shape
    tr = _div_tile(hr, 256, 16)

    def body(core_ref, a_ref, b_ref, o_ref):
        o_ref[...] = (a_ref[0].astype(F32) + b_ref[...].astype(F32)).astype(BF16)

    return pl.pallas_call(
        body, name=name,
        grid_spec=pltpu.PrefetchScalarGridSpec(
            num_scalar_prefetch=1, grid=(nchip, hr // tr),
            in_specs=[pl.BlockSpec((1, 1, tr, cols), lambda s, i, core_r: (s, core_r[0], i, 0)),
                      pl.BlockSpec((1, tr, cols), lambda s, i, core_r: (s, i, 0))],
            out_specs=pl.BlockSpec((1, tr, cols), lambda s, i, core_r: (s, i, 0))),
        out_shape=jax.ShapeDtypeStruct((nchip, hr, cols), BF16), compiler_params=_cparams())(core, grad, sib)


def _pair_sum_exchange(sums):
    nw = len(sums)

    def build(p_refs, o_refs, sems):
        send_sems, recv_sems, local_sems = sems
        x, y, c, chips = _place()
        me = 2 * x + y
        sibling = (x, y, 1 - c)
        mine = [pltpu.make_async_copy(p_refs[w].at[me], o_refs[w].at[c, 3], local_sems.at[w]) for w in range(nw)]
        first = [_remote(p_refs[w].at[2 * cx + cy], o_refs[w].at[c, k], send_sems.at[k, w], recv_sems.at[k, w],
                         (cx, cy, c)) for k, (cx, cy) in enumerate(chips) for w in range(nw)]

        def start():
            for cp in mine + first:
                cp.start()

        def finish():
            passed = []
            for k in range(N_CHIPS):
                for w in range(nw):
                    slab = o_refs[w].at[c, k]
                    if k < 3:
                        first[k * nw + w].wait_recv()
                    else:
                        mine[w].wait()
                    fwd = _remote(slab, slab, send_sems.at[3 + k, w], recv_sems.at[3 + k, w], sibling)
                    fwd.start()
                    passed.append(fwd)
            for k in range(N_CHIPS):
                for w in range(nw):
                    slab = o_refs[w].at[1 - c, k]
                    _remote(slab, slab, send_sems.at[3 + k, w], recv_sems.at[3 + k, w], sibling).wait_recv()
            for cp in first + passed:
                cp.wait_send()

        return start, finish

    return _Exchange(list(sums), [jax.ShapeDtypeStruct((2,) + p.shape, BF16) for p in sums],
                     [pltpu.SemaphoreType.DMA((7, nw)), pltpu.SemaphoreType.DMA((7, nw)),
                      pltpu.SemaphoreType.DMA((nw,))], build)


def _allreduce_small(vec):
    m_per, n = vec.shape

    def body(x_ref, out_ref, gath_ref, send_sems, recv_sems, local_sem):
        x, y, c, chips = _place()
        me, sibling = (x, y, c), (x, y, 1 - c)

        def rows(px, py, pc):
            return gath_ref.at[pl.ds((4 * px + 2 * py + pc) * m_per, m_per), :]

        def copy(k, block, to, src=None):
            return pltpu.make_async_remote_copy(
                src_ref=rows(*block) if src is None else src, dst_ref=rows(*block),
                send_sem=send_sems.at[k], recv_sem=recv_sems.at[k], device_id=to, device_id_type=MESH)

        mine = pltpu.make_async_copy(x_ref, rows(*me), local_sem)
        mine.start()
        first = [copy(0, me, sibling, src=x_ref)]
        first += [copy(1 + j, me, (*chip, c), src=x_ref) for j, chip in enumerate(chips)]
        for cp in first:
            cp.start()
        passed = [copy(4 + j, (*chip, c), sibling) for j, chip in enumerate(chips)]
        for j, chip in enumerate(chips):
            copy(1 + j, (*chip, c), me).wait_recv()
            passed[j].start()
        copy(0, sibling, me).wait_recv()
        for j, chip in enumerate(chips):
            copy(4 + j, (*chip, 1 - c), me).wait_recv()
        for cp in first + passed:
            cp.wait_send()
        mine.wait()
        acc = gath_ref[pl.ds(0, m_per), :]
        for k in range(1, N_DEV):
            acc = acc + gath_ref[pl.ds(k * m_per, m_per), :]
        out_ref[...] = acc

    vm = pl.BlockSpec(memory_space=pltpu.VMEM)
    return pl.pallas_call(
        body, name="allreduce_small", in_specs=[vm], out_specs=vm,
        out_shape=jax.ShapeDtypeStruct((m_per, n), F32),
        scratch_shapes=[pltpu.VMEM((N_DEV * m_per, n), F32), pltpu.SemaphoreType.DMA((7,)),
                        pltpu.SemaphoreType.DMA((7,)), pltpu.SemaphoreType.DMA],
    )(vec)


def _pack_small(vals, tail=()):
    flat = jnp.concatenate([vals[name].reshape(-1).astype(F32) for name, _ in SMALL] + [v.reshape(1) for v in tail])
    flat = jnp.pad(flat, (0, SMALL_ROWS * LANES - flat.shape[0]))
    return flat.reshape(SMALL_ROWS, LANES)


def _unpack_small(packed):
    flat = packed.reshape(-1)
    out, off = {}, 0
    for name, shape in SMALL:
        n = int(np.prod(shape))
        out[name] = flat[off:off + n].reshape(shape)
        off += n
    return out


def _head_pad_cols(w, heads, real):
    k = w.shape[0]
    return jnp.pad(w.reshape(k, heads, real), ((0, 0), (0, 0), (0, LANES - real))).reshape(k, heads * LANES)


def _rope_tables(positions):
    half = MLA_ROPE // 2
    inv = ROPE_BASE ** (-jnp.arange(half, dtype=F32) / half)
    ang = positions.astype(F32)[:, None] * inv
    cos, sin = jnp.cos(ang), jnp.sin(ang)
    t = positions.shape[0]
    z = lambda n: jnp.zeros((t, n), F32)
    rc = jnp.concatenate([jnp.ones((t, MLA_NOPE), F32), cos, cos, z(LANES - MLA_QK)], axis=1)
    rs1 = jnp.concatenate([z(MLA_NOPE), -sin, z(LANES - MLA_NOPE - half)], axis=1)
    rs2 = jnp.concatenate([z(MLA_NOPE + half), sin, z(LANES - MLA_QK)], axis=1)
    return rc, rs1, rs2


FFN1_WEIGHTS = ("ffn1_w_gu", "ffn1_w_down")
FFN2_WEIGHTS = ("ffn2_w_gu", "ffn2_w_down")
MIXER_WEIGHTS = tuple(n for n, *_ in SHARDED if n not in FFN1_WEIGHTS + FFN2_WEIGHTS)
SHARD_SHAPE = {n: (r, c, kind) for n, r, c, kind in SHARDED}


def _from_blocks(name, gathered):
    r, c, kind = SHARD_SHAPE[name]
    blk = gathered.reshape(N_CHIPS, r, c)
    return blk, (blk.transpose(1, 0, 2).reshape(r, N_CHIPS * c) if kind == "col" else blk.reshape(N_CHIPS * r, c))


def _grad_pair_sums(names, gw, core, tag):
    by_owner = []
    for name in names:
        r, c, kind = SHARD_SHAPE[name]
        blk = gw[name].reshape(r, N_CHIPS, c).transpose(1, 0, 2) if kind == "col" else gw[name].reshape(N_CHIPS, r, c)
        by_owner.append(blk.astype(BF16).reshape(N_CHIPS, 2, r // 2, c))
    received = _swap_halves(by_owner, "grad_swap_" + tag)
    return [_pair_sum(g, s, core, "pair_sum_" + n) for g, s, n in zip(by_owner, received, names)]


def _device_step(x, mem, positions, tgt, small, shards, core):
    d = D_MODEL
    g_ffn1, g_mix, g_ffn2 = small["ffn1_norm"], small["mix_norm"], small["ffn2_norm"]
    big = {}
    for name, g in zip(FFN1_WEIGHTS, _run_exchange(_gather_exchange([shards[n] for n in FFN1_WEIGHTS]), "gather_ffn1")):
        big[name + "#blocks"], big[name] = _from_blocks(name, g)
    wgu1, wd1 = big["ffn1_w_gu#blocks"], big["ffn1_w_down"].reshape(2, FF_TILE, d)
    x1, gpre1, upre1, *rest = _ffn_fwd(x, g_ffn1, wgu1, wd1, "ffn1_fwd",
                                       ex=_gather_exchange([shards[n] for n in MIXER_WEIGHTS]))
    for name, g in zip(MIXER_WEIGHTS, rest):
        big[name + "#blocks"], big[name] = _from_blocks(name, g)
    w_in = big["w_in"]
    w_uv_, w_cq, w_ckv = w_in[:, :COL_CQ], w_in[:, COL_CQ:COL_CKV], w_in[:, COL_CKV:COL_KR]
    w_kr = jnp.pad(w_in[:, COL_KR:COL_QM], ((0, 0), (MLA_NOPE, LANES - MLA_QK)))
    w_qm, w_g = w_in[:, COL_QM:COL_GATE], w_in[:, COL_GATE:]
    segs = (w_uv_, w_cq, w_ckv, w_kr, w_qm, w_g)
    wuq = _head_pad_cols(big["mla_w_uq"], MLA_HEADS, MLA_QK)
    ukv = big["mla_w_ukv"].reshape(MLA_KV_RANK, MLA_HEADS, 2, MLA_NOPE)
    wuk = _head_pad_cols(ukv[:, :, 0].reshape(MLA_KV_RANK, -1), MLA_HEADS, MLA_NOPE)
    wuv = _head_pad_cols(ukv[:, :, 1].reshape(MLA_KV_RANK, -1), MLA_HEADS, MLA_NOPE)
    wkv = big["mem_w_kv"]
    wa, wc, wo = big["w_branch_a"], big["w_branch_c"], big["w_out"]
    wb = jnp.pad(big["w_branch_b"].reshape(MLA_HEADS, MLA_NOPE, d),
                 ((0, 0), (0, LANES - MLA_NOPE), (0, 0))).reshape(MLA_HEADS * LANES, d)
    qg = jnp.pad(small["mla_q_norm"], ((0, 0), (0, LANES - MLA_QK)))
    kg = jnp.pad(small["mla_k_norm"], ((0, 0), (0, LANES - MLA_QK)))
    causal = jnp.tril(jnp.ones((CHUNK, CHUNK), bool))
    wt_f = jnp.where(causal[None], small["sg_w"][0], 0.0)
    wt, wt_t = wt_f.astype(BF16), wt_f.transpose(0, 2, 1).astype(BF16)
    bias_l = jnp.repeat(small["sg_b"][0].T, 64, axis=1)
    rc, rs1, rs2 = _rope_tables(positions)

    h = _rms_fwd(x1, g_mix, "mix_norm_fwd")
    zuv, zcq, zckv, zkr, zqm, zg, *rest = _mm_cols(h, segs, [F32] * 5 + [BF16], "in_proj",
                                                   ex=_gather_exchange([shards[n] for n in FFN2_WEIGHTS]))
    for name, g in zip(FFN2_WEIGHTS, rest):
        big[name + "#blocks"], big[name] = _from_blocks(name, g)
    wgu2, wd2 = big["ffn2_w_gu#blocks"], big["ffn2_w_down"].reshape(2, FF_TILE, d)
    ya = _sgu_fwd(zuv, small["sg_ln_g"], small["sg_ln_b"], wt, bias_l, "sgu_fwd")
    q, k, v, cqn, ckvn = _mla_prep_fwd(zcq, zckv, zkr, small["mla_cq_norm"], small["mla_ckv_norm"], qg, kg,
                                       wuq, wuk, wuv, rc, rs1, rs2, "mla_prep_fwd")
    yb, lse = _attn_fwd(q, k, v, "mla_attn_fwd")
    km, vm, memn = _mem_kv_fwd(mem, small["mem_norm"], wkv, small["mem_k_norm"], "mem_kv_fwd")
    yc = _mem_attn_fwd(zqm, small["mem_q_norm"], km, vm, "mem_attn_fwd")
    x2, merged, pa, pb, pc = _merge_fwd(x1, ya, yb, yc, zg, small["b_gate"], wa, wb, wc, wo, "merge_fwd")
    x3, gpre2, upre2 = _ffn_fwd(x2, g_ffn2, wgu2, wd2, "ffn2_fwd")
    dy, loss_row = _loss_head(x3, tgt, "loss_head")

    gw, gs, slots = {}, {}, {}

    def ffn_grads(prefix, xin, gain, dyin, gpre, upre, wgu, wd, ex=None, ex_names=()):
        dx, dgain, xn, dgt, dup, act, *got = _ffn_bwd(xin, gain, dyin, gpre, upre, wgu, wd, prefix + "_bwd", ex=ex)
        slots.update(zip(ex_names, got))
        gw[prefix + "_w_gu"] = jnp.concatenate(
            [_mm_tn(xn, dgt, prefix + "_dwg"), _mm_tn(xn, dup, prefix + "_dwu")], axis=1)
        gw[prefix + "_w_down"] = _mm_tn(act, dyin, prefix + "_dwd", scale=0.5)
        gs[prefix + "_norm"] = dgain
        return dx

    dx2 = ffn_grads("ffn2", x2, g_ffn2, dy, gpre2, upre2, wgu2, wd2)
    ffn2_sums = _pair_sum_exchange(_grad_pair_sums(FFN2_WEIGHTS, gw, core, "ffn2"))
    dpa, dpb, dpc, dzg, dbg, dya, dyb, dyc, *got = _merge_bwd(dx2, pa, pb, pc, zg, small["b_gate"], wa, wb, wc, wo,
                                                              "merge_bwd", ex=ffn2_sums)
    slots.update(zip(FFN2_WEIGHTS, got))
    gs["b_gate"] = dbg
    gw["w_out"] = _mm_tn(merged, dx2, "dw_out")
    gw["w_branch_a"] = _mm_tn(ya, dpa, "dw_branch_a")
    gw["w_branch_b"] = _mm_tn(yb, dpb, "dw_branch_b").reshape(MLA_HEADS, LANES, d)[:, :MLA_NOPE].reshape(-1, d)
    gw["w_branch_c"] = _mm_tn(yc, dpc, "dw_branch_c")

    dzuv, dwt, dbl, dlg, dlb = _sgu_bwd(zuv, dya, small["sg_ln_g"], small["sg_ln_b"], wt, wt_t, bias_l, "sgu_bwd")
    gs["sg_w"], gs["sg_b"] = dwt[None], dbl[:, :SG_GROUPS].T[None]
    gs["sg_ln_g"], gs["sg_ln_b"] = dlg, dlb

    delta_rows, lse_rows = _attn_bwd_rows(yb, lse, dyb, "mla_attn_bwd_rows")
    dq, dk, dv = _attn_bwd(q, k, v, delta_rows, lse_rows, dyb, "mla_attn_bwd")
    dzcq, dzckv, dzkr, dql, dkl, dgcq, dgckv, dqg, dkg = _mla_prep_bwd(
        zcq, zckv, zkr, small["mla_cq_norm"], small["mla_ckv_norm"], qg, kg, wuq, wuk, wuv, rc, rs1, rs2,
        dq, dk, dv, "mla_prep_bwd")
    gs["mla_cq_norm"], gs["mla_ckv_norm"] = dgcq, dgckv
    gs["mla_q_norm"], gs["mla_k_norm"] = dqg[:, :MLA_QK], dkg[:, :MLA_QK]
    gw["mla_w_uq"] = _mm_tn(cqn, dql, "dw_uq").reshape(MLA_Q_RANK, MLA_HEADS, LANES)[:, :, :MLA_QK].reshape(
        MLA_Q_RANK, -1)
    dwuk = _mm_tn(ckvn, dkl, "dw_uk").reshape(MLA_KV_RANK, MLA_HEADS, LANES)[:, :, :MLA_NOPE]
    dwuv = _mm_tn(ckvn, dv, "dw_uv").reshape(MLA_KV_RANK, MLA_HEADS, LANES)[:, :, :MLA_NOPE]
    gw["mla_w_ukv"] = jnp.concatenate([dwuk, dwuv], axis=2).reshape(MLA_KV_RANK, -1)

    dzqm, dkn, dvm, dmqg = _mem_attn_bwd(zqm, dyc, small["mem_q_norm"], km, vm, "mem_attn_bwd")
    gs["mem_q_norm"] = dmqg
    gw["mem_w_kv"], gs["mem_k_norm"], gs["mem_norm"] = _mem_kv_bwd(
        mem, small["mem_norm"], wkv, small["mem_k_norm"], dkn, dvm, "mem_kv_bwd")

    dzs = (dzuv, dzcq, dzckv, dzkr, dzqm, dzg)
    dh = _mm([(dz, w.T) for dz, w in zip(dzs, segs)], F32, "in_proj_bwd")
    dws = list(_mm_tn_cols(h, dzs[:5], "dw_in_narrow")) + [_mm_tn(h, dzg, "dw_in_gate")]
    dws[3] = dws[3][:, MLA_NOPE:MLA_QK]
    gw["w_in"] = jnp.concatenate(dws, axis=1)
    dx1, gs["mix_norm"] = _rms_bwd(x1, g_mix, dh, dx2, "mix_norm_bwd")
    mixer_sums = _pair_sum_exchange(_grad_pair_sums(MIXER_WEIGHTS, gw, core, "mixer"))
    dx = ffn_grads("ffn1", x, g_ffn1, dx1, gpre1, upre1, wgu1, wd1, ex=mixer_sums, ex_names=MIXER_WEIGHTS)
    ffn1_sums = _pair_sum_exchange(_grad_pair_sums(FFN1_WEIGHTS, gw, core, "ffn1"))
    slots.update(zip(FFN1_WEIGHTS, _run_exchange(ffn1_sums, "grad_exchange_ffn1")))
    return loss_row, dx, slots, gs


def kernel(x, mem, positions, ffn1_norm, ffn1_w_gu, ffn1_w_down, mix_norm, w_in, b_gate, sg_ln_g, sg_ln_b, sg_w, sg_b, mla_cq_norm, mla_w_uq, mla_ckv_norm, mla_w_ukv, mla_q_norm, mla_k_norm, mem_norm, mem_w_kv, mem_q_norm, mem_k_norm, w_branch_a, w_branch_b, w_branch_c, w_out, ffn2_norm, ffn2_w_gu, ffn2_w_down, loss_target, m_ffn1_norm, m_ffn1_w_gu, m_ffn1_w_down, m_mix_norm, m_w_in, m_b_gate, m_sg_ln_g, m_sg_ln_b, m_sg_w, m_sg_b, m_mla_cq_norm, m_mla_w_uq, m_mla_ckv_norm, m_mla_w_ukv, m_mla_q_norm, m_mla_k_norm, m_mem_norm, m_mem_w_kv, m_mem_q_norm, m_mem_k_norm, m_w_branch_a, m_w_branch_b, m_w_branch_c, m_w_out, m_ffn2_norm, m_ffn2_w_gu, m_ffn2_w_down, v_ffn1_norm, v_ffn1_w_gu, v_ffn1_w_down, v_mix_norm, v_w_in, v_b_gate, v_sg_ln_g, v_sg_ln_b, v_sg_w, v_sg_b, v_mla_cq_norm, v_mla_w_uq, v_mla_ckv_norm, v_mla_w_ukv, v_mla_q_norm, v_mla_k_norm, v_mem_norm, v_mem_w_kv, v_mem_q_norm, v_mem_k_norm, v_w_branch_a, v_w_branch_b, v_w_branch_c, v_w_out, v_ffn2_norm, v_ffn2_w_gu, v_ffn2_w_down):
    args = dict(locals())
    weights = {n: args[n] for n in WEIGHT_ORDER}
    mom_m = {n: args["m_" + n] for n in WEIGHT_ORDER}
    mom_v = {n: args["v_" + n] for n in WEIGHT_ORDER}
    small = {n: weights[n] for n, _ in SMALL}
    halves = lambda a, r, c: a.reshape(2, r // 2, c)

    shards = {n: halves(weights[n][0].astype(BF16), r, c) for n, r, c, _ in SHARDED}
    core = lax.axis_index("c").astype(jnp.int32).reshape(1)
    loss_row, dx, slots, gs = _device_step(x[0], mem[0], positions[0], loss_target[0], small, shards, core)
    summed = _allreduce_small(_pack_small(gs, tail=[loss_row[0, 0]]))
    loss = summed.reshape(-1)[_N_SMALL]
    small_grads = _unpack_small(summed)

    grads, deltas, new_m, new_v = {}, {}, {}, {}
    for name, r, c, _ in SHARDED:
        outs = _adamw_slots(halves(weights[name][0], r, c), slots[name], halves(mom_m[name][0], r, c),
                            halves(mom_v[name][0], r, c), "adamw_" + name)
        shape = weights[name].shape
        grads[name], deltas[name], new_m[name], new_v[name] = [o.reshape(shape) for o in outs]
    dlt, nm, nv = _adamw(_pack_small(small), _pack_small(small_grads), _pack_small({n: mom_m[n] for n, _ in SMALL}),
                         _pack_small({n: mom_v[n] for n, _ in SMALL}), "adamw_small")
    for name, _ in SMALL:
        grads[name] = small_grads[name]
    deltas.update(_unpack_small(dlt))
    new_m.update(_unpack_small(nm))
    new_v.update(_unpack_small(nv))

    return (loss, dx[None], *[grads[n] for n in WEIGHT_ORDER], *[deltas[n] for n in WEIGHT_ORDER],
            *[new_m[n] for n in WEIGHT_ORDER], *[new_v[n] for n in WEIGHT_ORDER])
```

```python
import functools
from typing import Callable, NamedTuple

import numpy as np
import jax
import jax.numpy as jnp
from jax import lax
from jax.experimental import pallas as pl
from jax.experimental.pallas import tpu as pltpu

F32 = jnp.float32
BF16 = jnp.bfloat16

D_MODEL = 1024
D_FF = 2816
FF_TILE = 1408
SG_WIDTH = 512
SG_GROUPS = 8
CHUNK = 128
MLA_HEADS = 8
MLA_QK = 96
MLA_NOPE = 64
MLA_ROPE = 32
MLA_Q_RANK = 384
MLA_KV_RANK = 256
MEM_HEADS = 4
MEM_LEN = 256
LANES = 128
EPS = 1e-6
NEG = -1e30
ROPE_BASE = 10000.0
N_CHIPS = 4
N_DEV = 8

ADAM_LR = 0.001
ADAM_B1 = 0.9
ADAM_B2 = 0.999
ADAM_EPS = 1e-08
ADAM_WD = 0.01
ADAM_STEP = 10

COL_V = 512
COL_CQ = 1024
COL_CKV = 1408
COL_KR = 1664
COL_QM = 1696
COL_GATE = 2208
IN_COLS = 5280

VMEM_LIMIT_BYTES = 56 * 1024 * 1024
INV_SQRT2 = 0.7071067811865476
INV_SQRT_2PI = 0.3989422804014327
LOG2E = 1.4426950408889634
ATTN_SCALE = MLA_QK ** -0.5
V_ONES_LANE = 64
ATTN_SCALE2 = ATTN_SCALE * LOG2E

SHARDED = (
    ("ffn1_w_gu", 1024, 1408, "col"),
    ("ffn1_w_down", 704, 1024, "row"),
    ("w_in", 1024, 1320, "col"),
    ("mla_w_uq", 384, 192, "col"),
    ("mla_w_ukv", 256, 256, "col"),
    ("mem_w_kv", 256, 1024, "row"),
    ("w_branch_a", 512, 256, "col"),
    ("w_branch_b", 512, 256, "col"),
    ("w_branch_c", 512, 256, "col"),
    ("w_out", 256, 1024, "row"),
    ("ffn2_w_gu", 1024, 1408, "col"),
    ("ffn2_w_down", 704, 1024, "row"),
)
SMALL = (
    ("ffn1_norm", (1, 1024)), ("mix_norm", (1, 1024)), ("b_gate", (1, 3072)),
    ("sg_ln_g", (1, 512)), ("sg_ln_b", (1, 512)), ("sg_w", (1, 8, 128, 128)),
    ("sg_b", (1, 8, 128)), ("mla_cq_norm", (1, 384)), ("mla_ckv_norm", (1, 256)),
    ("mla_q_norm", (1, 96)), ("mla_k_norm", (1, 96)), ("mem_norm", (1, 1024)),
    ("mem_q_norm", (1, 128)), ("mem_k_norm", (1, 128)), ("ffn2_norm", (1, 1024)),
)
WEIGHT_ORDER = (
    "ffn1_norm", "ffn1_w_gu", "ffn1_w_down", "mix_norm", "w_in", "b_gate", "sg_ln_g", "sg_ln_b",
    "sg_w", "sg_b", "mla_cq_norm", "mla_w_uq", "mla_ckv_norm", "mla_w_ukv", "mla_q_norm",
    "mla_k_norm", "mem_norm", "mem_w_kv", "mem_q_norm", "mem_k_norm", "w_branch_a", "w_branch_b",
    "w_branch_c", "w_out", "ffn2_norm", "ffn2_w_gu", "ffn2_w_down",
)

_N_SMALL = sum(int(np.prod(s)) for _, s in SMALL)
SMALL_ROWS = -(-_N_SMALL // (LANES * 8)) * 8

MESH = pl.DeviceIdType.MESH


def _cparams():
    return pltpu.CompilerParams(vmem_limit_bytes=VMEM_LIMIT_BYTES)


def _dot(a, b):
    return jnp.dot(a, b, preferred_element_type=F32)


def _dot_nt(a, b):
    return lax.dot_general(a, b, (((1,), (1,)), ((), ())), preferred_element_type=F32)


def _dot_tn(a, b):
    return lax.dot_general(a, b, (((0,), (0,)), ((), ())), preferred_element_type=F32)


def _gelu(x):
    return 0.5 * x * (1.0 + lax.erf(x * INV_SQRT2))


def _gelu_grad(x):
    return 0.5 * (1.0 + lax.erf(x * INV_SQRT2)) + x * jnp.exp(-0.5 * x * x) * INV_SQRT_2PI


def _rstd(x, n):
    return lax.rsqrt(jnp.sum(x * x, axis=-1, keepdims=True) * (1.0 / n) + EPS)


def _rms_vjp(x, r, g, dy, n):
    dxh = dy * g
    dx = r * dxh - x * (r * r * r) * (jnp.sum(dxh * x, axis=-1, keepdims=True) * (1.0 / n))
    return dx, dy * x * r


def _row_tile(t, want):
    return min(t, want)


def _wide_tile(n):
    if n <= 1024:
        return n
    if n % 1024 == 0:
        return 1024
    assert n % FF_TILE == 0, n
    return FF_TILE


def _rms_fwd(x, g, name):
    t, d = x.shape
    tm = _row_tile(t, 512)

    def body(x_ref, g_ref, o_ref):
        xv = x_ref[...]
        o_ref[...] = (xv * _rstd(xv, d) * g_ref[...]).astype(BF16)

    return pl.pallas_call(
        body, name=name, grid=(t // tm,),
        in_specs=[pl.BlockSpec((tm, d), lambda i: (i, 0)), pl.BlockSpec((1, d), lambda i: (0, 0))],
        out_specs=pl.BlockSpec((tm, d), lambda i: (i, 0)),
        out_shape=jax.ShapeDtypeStruct((t, d), BF16), compiler_params=_cparams())(x, g)


def _rms_bwd(x, g, dxn, dres, name):
    t, d = x.shape
    tm = _row_tile(t, 256)

    def body(x_ref, g_ref, d_ref, r_ref, dx_ref, dg_ref):
        @pl.when(pl.program_id(0) == 0)
        def _():
            dg_ref[...] = jnp.zeros_like(dg_ref)

        xv = x_ref[...]
        r = _rstd(xv, d)
        dx, dgr = _rms_vjp(xv, r, g_ref[...], d_ref[...].astype(F32), d)
        dx_ref[...] = r_ref[...] + dx
        dg_ref[...] += jnp.sum(dgr, axis=0, keepdims=True)

    row = pl.BlockSpec((tm, d), lambda i: (i, 0))
    vec = pl.BlockSpec((1, d), lambda i: (0, 0))
    return pl.pallas_call(
        body, name=name, grid=(t // tm,), in_specs=[row, vec, row, row], out_specs=[row, vec],
        out_shape=[jax.ShapeDtypeStruct((t, d), F32), jax.ShapeDtypeStruct((1, d), F32)],
        compiler_params=_cparams())(x, g, dxn, dres)


def _mm(pairs, out_dtype, name):
    t = pairs[0][0].shape[0]
    n = pairs[0][1].shape[1]
    tm = _row_tile(t, 512)
    tn = _wide_tile(n)
    np_ = len(pairs)

    def body(*refs):
        o_ref = refs[2 * np_]
        acc = None
        for a_ref, w_ref in zip(refs[:np_], refs[np_:2 * np_]):
            part = _dot(a_ref[...].astype(BF16), w_ref[...])
            acc = part if acc is None else acc + part
        o_ref[...] = acc.astype(out_dtype)

    in_specs = [pl.BlockSpec((tm, a.shape[1]), lambda i, j: (i, 0)) for a, _ in pairs]
    in_specs += [pl.BlockSpec((w.shape[0], tn), lambda i, j: (0, j)) for _, w in pairs]
    return pl.pallas_call(
        body, name=name, grid=(t // tm, n // tn), in_specs=in_specs,
        out_specs=pl.BlockSpec((tm, tn), lambda i, j: (i, j)),
        out_shape=jax.ShapeDtypeStruct((t, n), out_dtype), compiler_params=_cparams(),
    )(*[a for a, _ in pairs], *[w for _, w in pairs])


def _mm_cols(a, ws, out_dtypes, name, ex=None):
    t, kdim = a.shape
    tm = _row_tile(t, 256)
    n = len(ws)

    def body(*refs):
        av = refs[0][...]
        for w_ref, o_ref in zip(refs[1:1 + n], refs[1 + n:]):
            o_ref[...] = _dot(av, w_ref[...]).astype(o_ref.dtype)

    row = lambda width: pl.BlockSpec((tm, width), lambda i: (i, 0))
    return _call_with_exchange(
        ex, body, name, (t // tm,),
        [row(kdim)] + [pl.BlockSpec(w.shape, lambda i: (0, 0)) for w in ws],
        [row(w.shape[1]) for w in ws],
        [jax.ShapeDtypeStruct((t, w.shape[1]), dt) for w, dt in zip(ws, out_dtypes)], [], (a, *ws))


def _mm_tn_cols(a, bs, name):
    t, m = a.shape
    tk = _row_tile(t, 512)
    n = len(bs)

    def body(*refs):
        @pl.when(pl.program_id(0) == 0)
        def _():
            for o_ref in refs[1 + n:]:
                o_ref[...] = jnp.zeros_like(o_ref)

        av = refs[0][...].astype(BF16)
        for b_ref, o_ref in zip(refs[1:1 + n], refs[1 + n:]):
            o_ref[...] += _dot_tn(av, b_ref[...].astype(BF16))

    row = lambda width: pl.BlockSpec((tk, width), lambda k: (k, 0))
    return pl.pallas_call(
        body, name=name, grid=(t // tk,), in_specs=[row(m)] + [row(b.shape[1]) for b in bs],
        out_specs=[pl.BlockSpec((m, b.shape[1]), lambda k: (0, 0)) for b in bs],
        out_shape=[jax.ShapeDtypeStruct((m, b.shape[1]), F32) for b in bs],
        compiler_params=_cparams())(a, *bs)


def _mm_tn(a, b, name, scale=1.0):
    t, m = a.shape
    n = b.shape[1]
    tm, tn = _wide_tile(m), _wide_tile(n)
    tk = _row_tile(t, 1024)
    nk = t // tk

    def body(a_ref, b_ref, o_ref):
        k = pl.program_id(2)

        @pl.when(k == 0)
        def _():
            o_ref[...] = jnp.zeros_like(o_ref)

        o_ref[...] += _dot_tn(a_ref[...].astype(BF16), b_ref[...].astype(BF16))
        if scale != 1.0:
            @pl.when(k == nk - 1)
            def _():
                o_ref[...] = o_ref[...] * scale

    return pl.pallas_call(
        body, name=name, grid=(m // tm, n // tn, nk),
        in_specs=[pl.BlockSpec((tk, tm), lambda i, j, k: (k, i)),
                  pl.BlockSpec((tk, tn), lambda i, j, k: (k, j))],
        out_specs=pl.BlockSpec((tm, tn), lambda i, j, k: (i, j)),
        out_shape=jax.ShapeDtypeStruct((m, n), F32), compiler_params=_cparams())(a, b)


class _Exchange(NamedTuple):
    operands: list
    out_shapes: list
    sem_shapes: list
    build: Callable


def _call_with_exchange(ex, body, name, grid, in_specs, out_specs, out_shape, scratch_shapes, operands):
    if ex is None:
        return pl.pallas_call(body, name=name, grid=grid, in_specs=in_specs, out_specs=out_specs, out_shape=out_shape,
                              scratch_shapes=scratch_shapes, compiler_params=_cparams())(*operands)
    n_in, n_out, n_scr = len(in_specs), len(out_specs), len(scratch_shapes)
    k_in, k_out = len(ex.operands), len(ex.out_shapes)

    def carried(*refs):
        a, b = n_in, n_in + k_in
        c, e = b + n_out, b + n_out + k_out
        f = e + n_scr
        start, finish = ex.build(refs[a:b], refs[c:e], refs[f:])
        steps = [pl.program_id(ax) for ax in range(len(grid))]
        first = functools.reduce(jnp.logical_and, [s == 0 for s in steps])
        last = functools.reduce(jnp.logical_and, [s == n - 1 for s, n in zip(steps, grid)])
        pl.when(first)(start)
        body(*refs[:a], *refs[b:c], *refs[e:f])
        pl.when(last)(finish)

    return pl.pallas_call(
        carried, name=name, grid=grid, in_specs=list(in_specs) + [ANY] * k_in,
        out_specs=list(out_specs) + [ANY] * k_out, out_shape=list(out_shape) + list(ex.out_shapes),
        scratch_shapes=list(scratch_shapes) + list(ex.sem_shapes), compiler_params=_cparams(),
    )(*operands, *ex.operands)


def _run_exchange(ex, name):
    k_in, k_out = len(ex.operands), len(ex.out_shapes)

    def body(*refs):
        start, finish = ex.build(refs[:k_in], refs[k_in:k_in + k_out], refs[k_in + k_out:])
        start()
        finish()

    return pl.pallas_call(body, name=name, in_specs=[ANY] * k_in, out_specs=[ANY] * k_out,
                          out_shape=list(ex.out_shapes), scratch_shapes=list(ex.sem_shapes))(*ex.operands)


def _ffn_fwd(x, g, wgu4, wd2, name, ex=None):
    t, d = x.shape
    tm = _row_tile(t, 512)

    def body(x_ref, g_ref, wg_ref, wu_ref, wd_ref, o_ref, gg_ref, uu_ref, xn_scr, acc_scr):
        j = pl.program_id(1)

        @pl.when(j == 0)
        def _():
            xv = x_ref[...]
            xn_scr[...] = (xv * _rstd(xv, d) * g_ref[...]).astype(BF16)
            acc_scr[...] = jnp.zeros_like(acc_scr)

        xn = xn_scr[...]
        gg = _dot(xn, wg_ref[0])
        uu = _dot(xn, wu_ref[0])
        gg_ref[...] = gg.astype(BF16)
        uu_ref[...] = uu.astype(BF16)
        act = gg * jax.nn.sigmoid(gg) * uu
        acc_scr[...] += _dot(act.astype(BF16), wd_ref[0])

        @pl.when(j == 1)
        def _():
            o_ref[...] = x_ref[...] + 0.5 * acc_scr[...]

    row = pl.BlockSpec((tm, d), lambda i, j: (i, 0))
    ffb = pl.BlockSpec((tm, FF_TILE), lambda i, j: (i, j))
    return _call_with_exchange(
        ex, body, name, (t // tm, 2),
        [row, pl.BlockSpec((1, d), lambda i, j: (0, 0)),
         pl.BlockSpec((1, d, FF_TILE), lambda i, j: (j, 0, 0)),
         pl.BlockSpec((1, d, FF_TILE), lambda i, j: (j + 2, 0, 0)),
         pl.BlockSpec((1, FF_TILE, d), lambda i, j: (j, 0, 0))],
        [row, ffb, ffb],
        [jax.ShapeDtypeStruct((t, d), F32), jax.ShapeDtypeStruct((t, D_FF), BF16), jax.ShapeDtypeStruct((t, D_FF), BF16)],
        [pltpu.VMEM((tm, d), BF16), pltpu.VMEM((tm, d), F32)], (x, g, wgu4, wgu4, wd2))


def _ffn_bwd(x, g, dy, gpre, upre, wgu4, wd2, name, ex=None):
    t, d = x.shape
    tm = _row_tile(t, 512)

    def body(dy_ref, gg_ref, uu_ref, wgu_hbm, wd_hbm, dg_ref, du_ref, act_ref, part_ref, wg_ref, wu_ref, wd_ref):
        j = pl.program_id(0)

        @pl.when(pl.program_id(1) == 0)
        def _():
            pltpu.sync_copy(wgu_hbm.at[j], wg_ref.at[0])
            pltpu.sync_copy(wgu_hbm.at[j + 2], wu_ref.at[0])
            pltpu.sync_copy(wd_hbm.at[j], wd_ref.at[0])

        gg = gg_ref[...].astype(F32)
        uu = uu_ref[...].astype(F32)
        sg = jax.nn.sigmoid(gg)
        silu = gg * sg
        act_ref[...] = (silu * uu).astype(BF16)
        dyh = (0.5 * dy_ref[...]).astype(BF16)
        dact = _dot_nt(dyh, wd_ref[0])
        du = (dact * silu).astype(BF16)
        dgt = (dact * uu * (sg * (1.0 + gg * (1.0 - sg)))).astype(BF16)
        du_ref[...] = du
        dg_ref[...] = dgt
        part_ref[0] = _dot_nt(dgt, wg_ref[0]) + _dot_nt(du, wu_ref[0])

    row = pl.BlockSpec((tm, d), lambda j, i: (i, 0))
    ffb = pl.BlockSpec((tm, FF_TILE), lambda j, i: (i, j))
    dgt, dup, act, parts, *got = _call_with_exchange(
        ex, body, name, (2, t // tm),
        [row, ffb, ffb, ANY, ANY],
        [ffb, ffb, ffb, pl.BlockSpec((1, tm, d), lambda j, i: (j, i, 0))],
        [jax.ShapeDtypeStruct((t, D_FF), BF16)] * 3 + [jax.ShapeDtypeStruct((2, t, d), F32)],
        [pltpu.VMEM((1, d, FF_TILE), BF16), pltpu.VMEM((1, d, FF_TILE), BF16), pltpu.VMEM((1, FF_TILE, d), BF16)],
        (dy, gpre, upre, wgu4, wd2))

    def norm_body(x_ref, g_ref, p_ref, dy_ref, dx_ref, dgain_ref, xn_ref):
        @pl.when(pl.program_id(0) == 0)
        def _():
            dgain_ref[...] = jnp.zeros_like(dgain_ref)

        xv = x_ref[...]
        r = _rstd(xv, d)
        xn_ref[...] = (xv * r * g_ref[...]).astype(BF16)
        dx, dgr = _rms_vjp(xv, r, g_ref[...], p_ref[0] + p_ref[1], d)
        dx_ref[...] = dy_ref[...] + dx
        dgain_ref[...] += jnp.sum(dgr, axis=0, keepdims=True)

    tn = _row_tile(t, 256)
    nrow = pl.BlockSpec((tn, d), lambda i: (i, 0))
    vec = pl.BlockSpec((1, d), lambda i: (0, 0))
    dx, dgain, xn = pl.pallas_call(
        norm_body, name=name + "_norm", grid=(t // tn,),
        in_specs=[nrow, vec, pl.BlockSpec((2, tn, d), lambda i: (0, i, 0)), nrow],
        out_specs=[nrow, vec, nrow],
        out_shape=[jax.ShapeDtypeStruct((t, d), F32), jax.ShapeDtypeStruct((1, d), F32),
                   jax.ShapeDtypeStruct((t, d), BF16)],
        compiler_params=_cparams())(x, g, parts, dy)
    return [dx, dgain, xn, dgt, dup, act] + got


def _loss_head(y, tgt, name):
    t, d = y.shape
    tm = _row_tile(t, 512)

    def body(y_ref, t_ref, dy_ref, loss_ref):
        @pl.when(pl.program_id(0) == 0)
        def _():
            loss_ref[...] = jnp.zeros_like(loss_ref)

        e = y_ref[...] - t_ref[...]
        dy_ref[...] = e * (1.0 / d)
        part = 0.5 * jnp.sum(jnp.sum(e * e, axis=-1, keepdims=True) * (1.0 / d), axis=0, keepdims=True)
        loss_ref[...] += jnp.broadcast_to(part, loss_ref.shape)

    row = pl.BlockSpec((tm, d), lambda i: (i, 0))
    return pl.pallas_call(
        body, name=name, grid=(t // tm,), in_specs=[row, row],
        out_specs=[row, pl.BlockSpec((1, LANES), lambda i: (0, 0))],
        out_shape=[jax.ShapeDtypeStruct((t, d), F32), jax.ShapeDtypeStruct((1, LANES), F32)],
        compiler_params=_cparams())(y, tgt)


def _sgu_layernorm(vpre, lg, lb):
    v = _gelu(vpre)
    mu = jnp.mean(v, axis=-1, keepdims=True)
    xc = v - mu
    rstd = lax.rsqrt(jnp.mean(xc * xc, axis=-1, keepdims=True) + EPS)
    xhat = xc * rstd
    return xhat, rstd, xhat * lg + lb


def _sgu_fwd(zuv, lg, lb, wt, bias_l, name):
    t = zuv.shape[0]
    tm = _row_tile(t, 512)

    def body(u_ref, v_ref, lg_ref, lb_ref, wt_ref, bl_ref, o_ref, vln_scr):
        _, _, vln = _sgu_layernorm(v_ref[...], lg_ref[...], lb_ref[...])
        vln_scr[...] = vln.astype(BF16)
        lo = lax.broadcasted_iota(jnp.int32, (CHUNK, LANES), 1) < 64
        for c in range(tm // CHUNK):
            rows = slice(c * CHUNK, (c + 1) * CHUNK)
            for p in range(SG_GROUPS // 2):
                cols = slice(p * LANES, (p + 1) * LANES)
                vp = vln_scr[rows, cols]
                mixed = jnp.where(lo, _dot(wt_ref[2 * p], vp), _dot(wt_ref[2 * p + 1], vp)) + bl_ref[:, cols]
                o_ref[rows, cols] = (_gelu(u_ref[rows, cols]) * mixed).astype(BF16)

    half = lambda k: pl.BlockSpec((tm, SG_WIDTH), lambda i: (i, k))
    vec = pl.BlockSpec((1, SG_WIDTH), lambda i: (0, 0))
    return pl.pallas_call(
        body, name=name, grid=(t // tm,),
        in_specs=[half(0), half(1), vec, vec,
                  pl.BlockSpec((SG_GROUPS, CHUNK, CHUNK), lambda i: (0, 0, 0)),
                  pl.BlockSpec((CHUNK, SG_WIDTH), lambda i: (0, 0))],
        out_specs=pl.BlockSpec((tm, SG_WIDTH), lambda i: (i, 0)),
        out_shape=jax.ShapeDtypeStruct((t, SG_WIDTH), BF16),
        scratch_shapes=[pltpu.VMEM((tm, SG_WIDTH), BF16)],
        compiler_params=_cparams())(zuv, zuv, lg, lb, wt, bias_l)


def _sgu_bwd(zuv, dya, lg, lb, wt, wt_t, bias_l, name):
    t = zuv.shape[0]
    tm = _row_tile(t, 256)
    nsteps = t // tm

    def body(u_ref, v_ref, dy_ref, lg_ref, lb_ref, wt_ref, wtt_ref, bl_ref,
             dz_ref, dwt_ref, dbl_ref, dlg_ref, dlb_ref, vln_scr, dvln_scr, dbacc_scr):
        step = pl.program_id(0)

        @pl.when(step == 0)
        def _():
            dwt_ref[...] = jnp.zeros_like(dwt_ref)
            dlg_ref[...] = jnp.zeros_like(dlg_ref)
            dlb_ref[...] = jnp.zeros_like(dlb_ref)
            dbl_ref[...] = jnp.zeros_like(dbl_ref)
            dbacc_scr[...] = jnp.zeros_like(dbacc_scr)

        vpre = v_ref[...]
        lgv = lg_ref[...]
        xhat, rstd, vln = _sgu_layernorm(vpre, lgv, lb_ref[...])
        vln_scr[...] = vln.astype(BF16)
        lo = lax.broadcasted_iota(jnp.int32, (CHUNK, LANES), 1) < 64
        for c in range(tm // CHUNK):
            rows = slice(c * CHUNK, (c + 1) * CHUNK)
            for p in range(SG_GROUPS // 2):
                cols = slice(p * LANES, (p + 1) * LANES)
                vp = vln_scr[rows, cols]
                mixed = jnp.where(lo, _dot(wt_ref[2 * p], vp), _dot(wt_ref[2 * p + 1], vp)) + bl_ref[:, cols]
                upre = u_ref[rows, cols]
                dyp = dy_ref[rows, cols]
                dz_ref[rows, cols] = (dyp * mixed * _gelu_grad(upre)).astype(BF16)
                dm = dyp * _gelu(upre)
                dbacc_scr[:, cols] += dm
                dlo = jnp.where(lo, dm, 0.0).astype(BF16)
                dhi = jnp.where(lo, 0.0, dm).astype(BF16)
                dvln_scr[rows, cols] = _dot(wtt_ref[2 * p], dlo) + _dot(wtt_ref[2 * p + 1], dhi)
                dwt_ref[2 * p] += _dot_nt(dlo, vp)
                dwt_ref[2 * p + 1] += _dot_nt(dhi, vp)
        dvln = dvln_scr[...]
        dlg_ref[...] += jnp.sum(dvln * xhat, axis=0, keepdims=True)
        dlb_ref[...] += jnp.sum(dvln, axis=0, keepdims=True)
        dxh = dvln * lgv
        dv = rstd * (dxh - jnp.mean(dxh, axis=-1, keepdims=True)
                     - xhat * jnp.mean(dxh * xhat, axis=-1, keepdims=True))
        dz_ref[:, SG_WIDTH:] = (dv * _gelu_grad(vpre)).astype(BF16)

        @pl.when(step == nsteps - 1)
        def _():
            rr = lax.broadcasted_iota(jnp.int32, (CHUNK, CHUNK), 0)
            cc = lax.broadcasted_iota(jnp.int32, (CHUNK, CHUNK), 1)
            tril = (cc <= rr).astype(F32)
            for gidx in range(SG_GROUPS):
                dwt_ref[gidx] = dwt_ref[gidx] * tril
            kk = lax.broadcasted_iota(jnp.int32, (SG_WIDTH, LANES), 0)
            gg = lax.broadcasted_iota(jnp.int32, (SG_WIDTH, LANES), 1)
            sel = ((kk // 64) == gg).astype(F32)
            dbl_ref[...] = jnp.dot(dbacc_scr[...], sel, preferred_element_type=F32,
                                   precision=lax.Precision.HIGHEST)

    half = lambda k: pl.BlockSpec((tm, SG_WIDTH), lambda i: (i, k))
    vec = pl.BlockSpec((1, SG_WIDTH), lambda i: (0, 0))
    wspec = pl.BlockSpec((SG_GROUPS, CHUNK, CHUNK), lambda i: (0, 0, 0))
    return pl.pallas_call(
        body, name=name, grid=(nsteps,),
        in_specs=[half(0), half(1), pl.BlockSpec((tm, SG_WIDTH), lambda i: (i, 0)), vec, vec,
                  wspec, wspec, pl.BlockSpec((CHUNK, SG_WIDTH), lambda i: (0, 0))],
        out_specs=[pl.BlockSpec((tm, 2 * SG_WIDTH), lambda i: (i, 0)), wspec,
                   pl.BlockSpec((CHUNK, LANES), lambda i: (0, 0)), vec, vec],
        out_shape=[jax.ShapeDtypeStruct((t, 2 * SG_WIDTH), BF16),
                   jax.ShapeDtypeStruct((SG_GROUPS, CHUNK, CHUNK), F32),
                   jax.ShapeDtypeStruct((CHUNK, LANES), F32),
                   jax.ShapeDtypeStruct((1, SG_WIDTH), F32), jax.ShapeDtypeStruct((1, SG_WIDTH), F32)],
        scratch_shapes=[pltpu.VMEM((tm, SG_WIDTH), BF16), pltpu.VMEM((tm, SG_WIDTH), F32),
                        pltpu.VMEM((CHUNK, SG_WIDTH), F32)],
        compiler_params=_cparams())(zuv, zuv, dya, lg, lb, wt, wt_t, bias_l)


def _rope(x, c, s1, s2):
    return x * c + pltpu.roll(x, LANES - 16, 1) * s1 + pltpu.roll(x, 16, 1) * s2


def _rope_t(dy, c, s1, s2):
    return dy * c + pltpu.roll(dy * s1, 16, 1) + pltpu.roll(dy * s2, LANES - 16, 1)


def _mla_prep_fwd(zcq, zckv, zkr, gcq, gckv, qg, kg, wuq, wuk, wuv, rc, rs1, rs2, name, ex=None):
    t = zcq.shape[0]
    tm = _row_tile(t, 256)
    hd = MLA_HEADS * LANES

    def body(zcq_ref, zckv_ref, zkr_ref, gcq_ref, gckv_ref, qg_ref, kg_ref, wuq_ref, wuk_ref, wuv_ref,
             c_ref, s1_ref, s2_ref, q_ref, k_ref, v_ref, cqn_ref, ckvn_ref):
        c, s1, s2 = c_ref[...], s1_ref[...], s2_ref[...]
        xq = zcq_ref[...]
        cqn = (xq * _rstd(xq, MLA_Q_RANK) * gcq_ref[...]).astype(BF16)
        cqn_ref[...] = cqn
        ql = _dot(cqn, wuq_ref[...])
        xk = zckv_ref[...]
        ckvn = (xk * _rstd(xk, MLA_KV_RANK) * gckv_ref[...]).astype(BF16)
        ckvn_ref[...] = ckvn
        kl = _dot(ckvn, wuk_ref[...])
        slot_lane = lax.broadcasted_iota(jnp.int32, (tm, hd), 1) % LANES
        v_ref[...] = jnp.where(slot_lane == V_ONES_LANE, 1.0, _dot(ckvn, wuv_ref[...])).astype(BF16)
        kr = zkr_ref[...]
        for h in range(MLA_HEADS):
            sl = slice(h * LANES, (h + 1) * LANES)
            qh = ql[:, sl]
            q_ref[:, sl] = (_rope(qh * _rstd(qh, MLA_QK) * qg_ref[...], c, s1, s2) * ATTN_SCALE2).astype(BF16)
            kh = kl[:, sl] + kr
            k_ref[:, sl] = _rope(kh * _rstd(kh, MLA_QK) * kg_ref[...], c, s1, s2).astype(BF16)

    row = lambda n: pl.BlockSpec((tm, n), lambda i: (i, 0))
    full = lambda a: pl.BlockSpec(a.shape, lambda i: (0, 0))
    return _call_with_exchange(
        ex, body, name, (t // tm,),
        [row(MLA_Q_RANK), row(MLA_KV_RANK), row(LANES), full(gcq), full(gckv), full(qg), full(kg),
         full(wuq), full(wuk), full(wuv), row(LANES), row(LANES), row(LANES)],
        [row(hd), row(hd), row(hd), row(MLA_Q_RANK), row(MLA_KV_RANK)],
        [jax.ShapeDtypeStruct((t, hd), BF16)] * 3
        + [jax.ShapeDtypeStruct((t, MLA_Q_RANK), BF16), jax.ShapeDtypeStruct((t, MLA_KV_RANK), BF16)],
        [], (zcq, zckv, zkr, gcq, gckv, qg, kg, wuq, wuk, wuv, rc, rs1, rs2))


def _mla_prep_bwd(zcq, zckv, zkr, gcq, gckv, qg, kg, wuq, wuk, wuv, rc, rs1, rs2, dq, dk, dv, name):
    t = zcq.shape[0]
    tm = _row_tile(t, 256)
    hd = MLA_HEADS * LANES

    def body(zcq_ref, zckv_ref, zkr_ref, gcq_ref, gckv_ref, qg_ref, kg_ref, wuq_ref, wuk_ref, wuv_ref,
             c_ref, s1_ref, s2_ref, dq_ref, dk_ref, dv_ref,
             dzcq_ref, dzckv_ref, dzkr_ref, dql_ref, dkl_ref, dgcq_ref, dgckv_ref, dqg_ref, dkg_ref):
        @pl.when(pl.program_id(0) == 0)
        def _():
            for ref in (dgcq_ref, dgckv_ref, dqg_ref, dkg_ref):
                ref[...] = jnp.zeros_like(ref)

        c, s1, s2 = c_ref[...], s1_ref[...], s2_ref[...]
        qgv, kgv = qg_ref[...], kg_ref[...]
        xq = zcq_ref[...]
        rq = _rstd(xq, MLA_Q_RANK)
        ql = _dot((xq * rq * gcq_ref[...]).astype(BF16), wuq_ref[...])
        xk = zckv_ref[...]
        rk = _rstd(xk, MLA_KV_RANK)
        kl = _dot((xk * rk * gckv_ref[...]).astype(BF16), wuk_ref[...])
        kr = zkr_ref[...]
        dqg_acc = jnp.zeros((tm, LANES), F32)
        dkg_acc = jnp.zeros((tm, LANES), F32)
        dkr = jnp.zeros((tm, LANES), F32)
        for h in range(MLA_HEADS):
            sl = slice(h * LANES, (h + 1) * LANES)
            qh = ql[:, sl]
            dqh, dgr = _rms_vjp(qh, _rstd(qh, MLA_QK), qgv, _rope_t(dq_ref[:, sl], c, s1, s2), MLA_QK)
            dql_ref[:, sl] = dqh.astype(BF16)
            dqg_acc += dgr
            kh = kl[:, sl] + kr
            dkh, dgr = _rms_vjp(kh, _rstd(kh, MLA_QK), kgv, _rope_t(dk_ref[:, sl], c, s1, s2), MLA_QK)
            dkl_ref[:, sl] = dkh.astype(BF16)
            dkg_acc += dgr
            dkr += dkh
        dqg_ref[...] += jnp.sum(dqg_acc, axis=0, keepdims=True)
        dkg_ref[...] += jnp.sum(dkg_acc, axis=0, keepdims=True)
        lane = lax.broadcasted_iota(jnp.int32, (tm, LANES), 1)
        dzkr_ref[...] = jnp.where((lane >= MLA_NOPE) & (lane < MLA_QK), dkr, 0.0).astype(BF16)
        dcqn = _dot_nt(dql_ref[...], wuq_ref[...])
        dx, dgr = _rms_vjp(xq, rq, gcq_ref[...], dcqn, MLA_Q_RANK)
        dzcq_ref[...] = dx.astype(BF16)
        dgcq_ref[...] += jnp.sum(dgr, axis=0, keepdims=True)
        dckvn = _dot_nt(dkl_ref[...], wuk_ref[...]) + _dot_nt(dv_ref[...].astype(BF16), wuv_ref[...])
        dx, dgr = _rms_vjp(xk, rk, gckv_ref[...], dckvn, MLA_KV_RANK)
        dzckv_ref[...] = dx.astype(BF16)
        dgckv_ref[...] += jnp.sum(dgr, axis=0, keepdims=True)

    row = lambda n: pl.BlockSpec((tm, n), lambda i: (i, 0))
    full = lambda a: pl.BlockSpec(a.shape, lambda i: (0, 0))
    vec = lambda n: pl.BlockSpec((1, n), lambda i: (0, 0))
    return pl.pallas_call(
        body, name=name, grid=(t // tm,),
        in_specs=[row(MLA_Q_RANK), row(MLA_KV_RANK), row(LANES), full(gcq), full(gckv), full(qg), full(kg),
                  full(wuq), full(wuk), full(wuv), row(LANES), row(LANES), row(LANES), row(hd), row(hd), row(hd)],
        out_specs=[row(MLA_Q_RANK), row(MLA_KV_RANK), row(LANES), row(hd), row(hd),
                   vec(MLA_Q_RANK), vec(MLA_KV_RANK), vec(LANES), vec(LANES)],
        out_shape=[jax.ShapeDtypeStruct((t, MLA_Q_RANK), BF16), jax.ShapeDtypeStruct((t, MLA_KV_RANK), BF16),
                   jax.ShapeDtypeStruct((t, LANES), BF16), jax.ShapeDtypeStruct((t, hd), BF16),
                   jax.ShapeDtypeStruct((t, hd), BF16), jax.ShapeDtypeStruct((1, MLA_Q_RANK), F32),
                   jax.ShapeDtypeStruct((1, MLA_KV_RANK), F32), jax.ShapeDtypeStruct((1, LANES), F32),
                   jax.ShapeDtypeStruct((1, LANES), F32)],
        compiler_params=_cparams(),
    )(zcq, zckv, zkr, gcq, gckv, qg, kg, wuq, wuk, wuv, rc, rs1, rs2, dq, dk, dv)


def _attn_tiles(t):
    tq = 512 if t >= 2048 else 128
    return tq, min(t, 4 * tq), min(t, 2 * tq)


def _causal_keep(tq, tk, i, j):
    row = lax.broadcasted_iota(jnp.int32, (tq, tk), 0)
    col = lax.broadcasted_iota(jnp.int32, (tq, tk), 1)
    return (col - row) <= (i * tq - j * tk)


def _causal_keep_t(tq, tk, i, j):
    key = lax.broadcasted_iota(jnp.int32, (tk, tq), 0)
    qry = lax.broadcasted_iota(jnp.int32, (tk, tq), 1)
    return (key - qry) <= (i * tq - j * tk)


ATTN_FWD_HEADS_PER_STEP = 2
ATTN_BWD_HEADS_PER_STEP = 2


def _attn_fwd(q, k, v, name):
    t, hd = q.shape
    hp = ATTN_FWD_HEADS_PER_STEP
    tq, tk, _ = _attn_tiles(t)
    pairs = [(i, j) for i in range(t // tq) for j in range(((i + 1) * tq - 1) // tk + 1)]
    ii = np.array([p[0] for p in pairs], np.int32)
    jj = np.array([p[1] for p in pairs], np.int32)

    def body(ii_ref, jj_ref, q_ref, k_ref, v_ref, o_ref, lse_ref, m_scr, acc_scr):
        s_id = pl.program_id(1)
        i, j = ii_ref[s_id], jj_ref[s_id]
        last = j == ((i + 1) * tq - 1) // tk
        ones_lane = lax.broadcasted_iota(jnp.int32, (tq, LANES), 1) == V_ONES_LANE

        @pl.when(j == 0)
        def _():
            m_scr[...] = jnp.full_like(m_scr, NEG)
            acc_scr[...] = jnp.zeros_like(acc_scr)

        def step(masked):
            for hh in range(hp):
                sl = slice(hh * LANES, (hh + 1) * LANES)
                s = _dot_nt(q_ref[:, sl], k_ref[:, sl])
                if masked:
                    s = jnp.where(_causal_keep(tq, tk, i, j), s, NEG)
                m_prev = m_scr[hh]
                m_new = jnp.maximum(m_prev, jnp.max(s, axis=1, keepdims=True))
                p = jnp.exp2(s - m_new)
                alpha = jnp.exp2(m_prev - m_new)
                acc = alpha * acc_scr[:, sl] + _dot(p.astype(BF16), v_ref[:, sl])
                if masked:
                    l_new = jnp.sum(jnp.where(ones_lane, acc, 0.0), axis=1, keepdims=True)
                    o_ref[:, sl] = (acc / l_new).astype(BF16)
                    lse_ref[:, sl] = jnp.broadcast_to(m_new + jnp.log(l_new) * LOG2E, (tq, LANES))
                else:
                    acc_scr[:, sl] = acc
                    m_scr[hh] = m_new

        @pl.when(jnp.logical_not(last))
        def _():
            step(False)

        @pl.when(last)
        def _():
            step(True)

    w = hp * LANES
    qspec = pl.BlockSpec((tq, w), lambda h, s, ii_r, jj_r: (ii_r[s], h))
    kspec = pl.BlockSpec((tk, w), lambda h, s, ii_r, jj_r: (jj_r[s], h))
    return pl.pallas_call(
        body, name=name,
        grid_spec=pltpu.PrefetchScalarGridSpec(
            num_scalar_prefetch=2, grid=(hd // w, len(pairs)), in_specs=[qspec, kspec, kspec],
            out_specs=[qspec, qspec],
            scratch_shapes=[pltpu.VMEM((hp, tq, 1), F32), pltpu.VMEM((tq, w), F32)]),
        out_shape=[jax.ShapeDtypeStruct((t, hd), BF16), jax.ShapeDtypeStruct((t, hd), F32)],
        compiler_params=_cparams())(jnp.asarray(ii), jnp.asarray(jj), q, k, v)


def _attn_bwd_rows(o, lse, do, name):
    t, hd = o.shape
    heads = hd // LANES
    tm = _row_tile(t, 512)

    def body(o_ref, lse_ref, do_ref, out_ref):
        lane = lax.broadcasted_iota(jnp.int32, (tm, LANES), 1)
        acc = jnp.zeros((tm, LANES), F32)
        for h in range(heads):
            sl = slice(h * LANES, (h + 1) * LANES)
            delta = jnp.sum(do_ref[:, sl].astype(F32) * o_ref[:, sl].astype(F32), axis=1, keepdims=True)
            acc = jnp.where(lane == h, delta, acc)
            acc = jnp.where(lane == heads + h, lse_ref[:, sl], acc)
        out_ref[...] = acc

    row = pl.BlockSpec((tm, hd), lambda i: (i, 0))
    cols = pl.pallas_call(
        body, name=name, grid=(t // tm,), in_specs=[row, row, row],
        out_specs=pl.BlockSpec((tm, LANES), lambda i: (i, 0)),
        out_shape=jax.ShapeDtypeStruct((t, LANES), F32), compiler_params=_cparams())(o, lse, do)
    rows = cols.T
    return rows[:heads].reshape(heads, 1, t), rows[heads:2 * heads].reshape(heads, 1, t)


def _attn_bwd(q, k, v, delta_rows, lse_rows, do, name):
    t, hd = q.shape
    hp = ATTN_BWD_HEADS_PER_STEP
    tq, _, tk = _attn_tiles(t)
    nq = t // tq
    pairs = [(i, j) for j in range(t // tk) for i in range((j * tk) // tq, nq)]
    ii = np.array([p[0] for p in pairs], np.int32)
    jj = np.array([p[1] for p in pairs], np.int32)

    def body(jj_ref, ii_ref, q_ref, k_ref, v_ref, delta_ref, lse_ref, do_ref, dq_ref, dk_ref, dv_ref,
             dk_scr, dv_scr):
        s_id = pl.program_id(1)
        i, j = ii_ref[s_id], jj_ref[s_id]

        @pl.when(s_id == 0)
        def _():
            dq_ref[...] = jnp.zeros_like(dq_ref)

        @pl.when(i == (j * tk) // tq)
        def _():
            dk_scr[...] = jnp.zeros_like(dk_scr)
            dv_scr[...] = jnp.zeros_like(dv_scr)

        rows = pl.ds(pl.multiple_of(i * tq, tq), tq)

        def step(masked):
            for hh in range(hp):
                sl = slice(hh * LANES, (hh + 1) * LANES)
                qv, kv, dov = q_ref[:, sl], k_ref[:, sl], do_ref[:, sl]
                st = _dot_nt(kv, qv)
                if masked:
                    st = jnp.where(_causal_keep_t(tq, tk, i, j), st, NEG)
                pt = jnp.exp2(st - lse_ref[hh])
                dv_scr[:, sl] += _dot(pt.astype(BF16), dov)
                dpt = _dot_nt(v_ref[:, sl], dov)
                dst = (pt * (dpt - delta_ref[hh]) * ATTN_SCALE).astype(BF16)
                dk_scr[:, sl] += _dot(dst, qv)
                dq_ref[rows, sl] += _dot_tn(dst, kv)

        crosses = (j + 1) * tk - 1 > i * tq

        @pl.when(jnp.logical_not(crosses))
        def _():
            step(False)

        @pl.when(crosses)
        def _():
            step(True)

        @pl.when(i == nq - 1)
        def _():
            dk_ref[...] = dk_scr[...] * (1.0 / ATTN_SCALE2)
            dv_ref[...] = dv_scr[...]

    w = hp * LANES
    qspec = pl.BlockSpec((tq, w), lambda h, s, jj_r, ii_r: (ii_r[s], h))
    kspec = pl.BlockSpec((tk, w), lambda h, s, jj_r, ii_r: (jj_r[s], h))
    rspec = pl.BlockSpec((hp, 1, tq), lambda h, s, jj_r, ii_r: (h, 0, ii_r[s]))
    return pl.pallas_call(
        body, name=name,
        grid_spec=pltpu.PrefetchScalarGridSpec(
            num_scalar_prefetch=2, grid=(hd // w, len(pairs)),
            in_specs=[qspec, kspec, kspec, rspec, rspec, qspec],
            out_specs=[pl.BlockSpec((t, w), lambda h, s, jj_r, ii_r: (0, h)), kspec, kspec],
            scratch_shapes=[pltpu.VMEM((tk, w), F32), pltpu.VMEM((tk, w), F32)]),
        out_shape=[jax.ShapeDtypeStruct((t, hd), F32)] * 3,
        compiler_params=_cparams())(jnp.asarray(jj), jnp.asarray(ii), q, k, v, delta_rows, lse_rows, do)


MEM_W = MEM_HEADS * LANES


def _mem_kv_fwd(mem, gmem, wkv, kg, name):
    m, d = mem.shape

    def body(mem_ref, g_ref, w_ref, kg_ref, k_ref, v_ref, mn_ref):
        xv = mem_ref[...]
        mn = (xv * _rstd(xv, d) * g_ref[...]).astype(BF16)
        mn_ref[...] = mn
        kvm = _dot(mn, w_ref[...])
        v_ref[...] = kvm[:, MEM_W:].astype(BF16)
        for h in range(MEM_HEADS):
            sl = slice(h * LANES, (h + 1) * LANES)
            kh = kvm[:, sl]
            k_ref[:, sl] = (kh * _rstd(kh, LANES) * kg_ref[...]).astype(BF16)

    full = lambda a: pl.BlockSpec(a.shape, lambda i: (0, 0))
    return pl.pallas_call(
        body, name=name, grid=(1,), in_specs=[full(mem), full(gmem), full(wkv), full(kg)],
        out_specs=[pl.BlockSpec((m, MEM_W), lambda i: (0, 0)), pl.BlockSpec((m, MEM_W), lambda i: (0, 0)),
                   pl.BlockSpec((m, d), lambda i: (0, 0))],
        out_shape=[jax.ShapeDtypeStruct((m, MEM_W), BF16), jax.ShapeDtypeStruct((m, MEM_W), BF16),
                   jax.ShapeDtypeStruct((m, d), BF16)],
        compiler_params=_cparams())(mem, gmem, wkv, kg)


def _mem_softmax(qn, kh):
    s = _dot_nt(qn, kh) * (LANES ** -0.5)
    e = jnp.exp(s - jnp.max(s, axis=1, keepdims=True))
    return e / jnp.sum(e, axis=1, keepdims=True)


def _mem_attn_fwd(zqm, qg, km, vm, name):
    t = zqm.shape[0]
    tm = _row_tile(t, 512)

    def body(q_ref, qg_ref, k_ref, v_ref, o_ref):
        for h in range(MEM_HEADS):
            sl = slice(h * LANES, (h + 1) * LANES)
            qh = q_ref[:, sl]
            qn = (qh * _rstd(qh, LANES) * qg_ref[...]).astype(BF16)
            p = _mem_softmax(qn, k_ref[:, sl])
            o_ref[:, sl] = _dot(p.astype(BF16), v_ref[:, sl]).astype(BF16)

    row = pl.BlockSpec((tm, MEM_W), lambda i: (i, 0))
    full = lambda a: pl.BlockSpec(a.shape, lambda i: (0, 0))
    return pl.pallas_call(
        body, name=name, grid=(t // tm,), in_specs=[row, full(qg), full(km), full(vm)], out_specs=row,
        out_shape=jax.ShapeDtypeStruct((t, MEM_W), BF16), compiler_params=_cparams())(zqm, qg, km, vm)


def _mem_attn_bwd(zqm, dyc, qg, km, vm, name):
    t = zqm.shape[0]
    m = km.shape[0]
    tm = _row_tile(t, 256)

    def body(q_ref, dy_ref, qg_ref, k_ref, v_ref, dz_ref, dk_ref, dv_ref, dqg_ref):
        @pl.when(pl.program_id(0) == 0)
        def _():
            dk_ref[...] = jnp.zeros_like(dk_ref)
            dv_ref[...] = jnp.zeros_like(dv_ref)
            dqg_ref[...] = jnp.zeros_like(dqg_ref)

        qgv = qg_ref[...]
        dqg_acc = jnp.zeros((tm, LANES), F32)
        for h in range(MEM_HEADS):
            sl = slice(h * LANES, (h + 1) * LANES)
            qh = q_ref[:, sl]
            r = _rstd(qh, LANES)
            qn = (qh * r * qgv).astype(BF16)
            kh = k_ref[:, sl]
            p = _mem_softmax(qn, kh)
            dov = dy_ref[:, sl]
            dv_ref[:, sl] += _dot_tn(p.astype(BF16), dov)
            dp = _dot_nt(dov, v_ref[:, sl])
            ds = (p * (dp - jnp.sum(dp * p, axis=1, keepdims=True)) * (LANES ** -0.5)).astype(BF16)
            dk_ref[:, sl] += _dot_tn(ds, qn)
            dqh, dgr = _rms_vjp(qh, r, qgv, _dot(ds, kh), LANES)
            dz_ref[:, sl] = dqh.astype(BF16)
            dqg_acc += dgr
        dqg_ref[...] += jnp.sum(dqg_acc, axis=0, keepdims=True)

    row = pl.BlockSpec((tm, MEM_W), lambda i: (i, 0))
    full = lambda a: pl.BlockSpec(a.shape, lambda i: (0, 0))
    acc = pl.BlockSpec((m, MEM_W), lambda i: (0, 0))
    return pl.pallas_call(
        body, name=name, grid=(t // tm,), in_specs=[row, row, full(qg), full(km), full(vm)],
        out_specs=[row, acc, acc, pl.BlockSpec((1, LANES), lambda i: (0, 0))],
        out_shape=[jax.ShapeDtypeStruct((t, MEM_W), BF16), jax.ShapeDtypeStruct((m, MEM_W), F32),
                   jax.ShapeDtypeStruct((m, MEM_W), F32), jax.ShapeDtypeStruct((1, LANES), F32)],
        compiler_params=_cparams())(zqm, dyc, qg, km, vm)


def _mem_kv_bwd(mem, gmem, wkv, kg, dkn, dvm, name):
    m, d = mem.shape

    def body(mem_ref, g_ref, w_ref, kg_ref, dk_ref, dv_ref, dw_ref, dkg_ref, dg_ref, dkv_scr):
        xv = mem_ref[...]
        r = _rstd(xv, d)
        mn = (xv * r * g_ref[...]).astype(BF16)
        kvm = _dot(mn, w_ref[...])
        dkv_scr[:, MEM_W:] = dv_ref[...].astype(BF16)
        dkg_acc = jnp.zeros((m, LANES), F32)
        for h in range(MEM_HEADS):
            sl = slice(h * LANES, (h + 1) * LANES)
            kh = kvm[:, sl]
            dkh, dgr = _rms_vjp(kh, _rstd(kh, LANES), kg_ref[...], dk_ref[:, sl], LANES)
            dkv_scr[:, sl] = dkh.astype(BF16)
            dkg_acc += dgr
        dkg_ref[...] = jnp.sum(dkg_acc, axis=0, keepdims=True)
        dkv = dkv_scr[...]
        dw_ref[...] = _dot_tn(mn, dkv)
        dmn = _dot_nt(dkv, w_ref[...])
        dg_ref[...] = jnp.sum(dmn * xv * r, axis=0, keepdims=True)

    full = lambda a: pl.BlockSpec(a.shape, lambda i: (0, 0))
    return pl.pallas_call(
        body, name=name, grid=(1,),
        in_specs=[full(mem), full(gmem), full(wkv), full(kg), full(dkn), full(dvm)],
        out_specs=[pl.BlockSpec((d, 2 * MEM_W), lambda i: (0, 0)), pl.BlockSpec((1, LANES), lambda i: (0, 0)),
                   pl.BlockSpec((1, d), lambda i: (0, 0))],
        out_shape=[jax.ShapeDtypeStruct((d, 2 * MEM_W), F32), jax.ShapeDtypeStruct((1, LANES), F32),
                   jax.ShapeDtypeStruct((1, d), F32)],
        scratch_shapes=[pltpu.VMEM((m, 2 * MEM_W), BF16)],
        compiler_params=_cparams())(mem, gmem, wkv, kg, dkn, dvm)


def _merge_fwd(x1, ya, yb, yc, zg, bg, wa, wb, wc, wo, name):
    t, d = x1.shape
    tm = _row_tile(t, 256)

    def body(x_ref, ya_ref, yb_ref, yc_ref, zg_ref, bg_ref, wa_ref, wb_ref, wc_ref, wo_ref,
             x2_ref, mg_ref, pa_ref, pb_ref, pc_ref):
        merged = None
        for k, (y_ref, w_ref, p_ref) in enumerate(
                ((ya_ref, wa_ref, pa_ref), (yb_ref, wb_ref, pb_ref), (yc_ref, wc_ref, pc_ref))):
            sl = slice(k * d, (k + 1) * d)
            pr = _dot(y_ref[...], w_ref[...])
            p_ref[...] = pr.astype(BF16)
            term = jax.nn.sigmoid(zg_ref[:, sl] + bg_ref[:, sl]) * pr
            merged = term if merged is None else merged + term
        mb = merged.astype(BF16)
        mg_ref[...] = mb
        x2_ref[...] = x_ref[...] + _dot(mb, wo_ref[...])

    row = lambda n: pl.BlockSpec((tm, n), lambda i: (i, 0))
    full = lambda a: pl.BlockSpec(a.shape, lambda i: (0, 0))
    return pl.pallas_call(
        body, name=name, grid=(t // tm,),
        in_specs=[row(d), row(ya.shape[1]), row(yb.shape[1]), row(yc.shape[1]), row(3 * d), full(bg),
                  full(wa), full(wb), full(wc), full(wo)],
        out_specs=[row(d)] * 5,
        out_shape=[jax.ShapeDtypeStruct((t, d), F32)] + [jax.ShapeDtypeStruct((t, d), BF16)] * 4,
        compiler_params=_cparams())(x1, ya, yb, yc, zg, bg, wa, wb, wc, wo)


def _merge_bwd(dx2, pa, pb, pc, zg, bg, wa, wb, wc, wo, name, ex=None):
    t, d = dx2.shape
    tm = _row_tile(t, 256)

    def body(dx_ref, pa_ref, pb_ref, pc_ref, zg_ref, bg_ref, wa_ref, wb_ref, wc_ref, wo_ref,
             dpa_ref, dpb_ref, dpc_ref, dzg_ref, dbg_ref, dya_ref, dyb_ref, dyc_ref):
        @pl.when(pl.program_id(0) == 0)
        def _():
            dbg_ref[...] = jnp.zeros_like(dbg_ref)

        dm = _dot_nt(dx_ref[...].astype(BF16), wo_ref[...])
        for k, (p_ref, w_ref, dp_ref, dy_ref) in enumerate(
                ((pa_ref, wa_ref, dpa_ref, dya_ref), (pb_ref, wb_ref, dpb_ref, dyb_ref),
                 (pc_ref, wc_ref, dpc_ref, dyc_ref))):
            sl = slice(k * d, (k + 1) * d)
            gate = jax.nn.sigmoid(zg_ref[:, sl] + bg_ref[:, sl])
            dpr = (dm * gate).astype(BF16)
            dp_ref[...] = dpr
            dzg = dm * p_ref[...].astype(F32) * gate * (1.0 - gate)
            dzg_ref[:, sl] = dzg.astype(BF16)
            dbg_ref[:, sl] += jnp.sum(dzg, axis=0, keepdims=True)
            dy_ref[...] = _dot_nt(dpr, w_ref[...]).astype(dy_ref.dtype)

    row = lambda n: pl.BlockSpec((tm, n), lambda i: (i, 0))
    full = lambda a: pl.BlockSpec(a.shape, lambda i: (0, 0))
    na, nb, nc = wa.shape[0], wb.shape[0], wc.shape[0]
    return _call_with_exchange(
        ex, body, name, (t // tm,),
        [row(d), row(d), row(d), row(d), row(3 * d), full(bg), full(wa), full(wb), full(wc), full(wo)],
        [row(d), row(d), row(d), row(3 * d), pl.BlockSpec((1, 3 * d), lambda i: (0, 0)), row(na), row(nb), row(nc)],
        [jax.ShapeDtypeStruct((t, d), BF16)] * 3
        + [jax.ShapeDtypeStruct((t, 3 * d), BF16), jax.ShapeDtypeStruct((1, 3 * d), F32),
           jax.ShapeDtypeStruct((t, na), F32), jax.ShapeDtypeStruct((t, nb), BF16),
           jax.ShapeDtypeStruct((t, nc), BF16)],
        [], (dx2, pa, pb, pc, zg, bg, wa, wb, wc, wo))


def _adamw_math(w, g, m, v):
    bc1 = 1.0 - ADAM_B1 ** ADAM_STEP
    bc2 = 1.0 - ADAM_B2 ** ADAM_STEP
    nm = ADAM_B1 * m + (1.0 - ADAM_B1) * g
    nv = ADAM_B2 * v + (1.0 - ADAM_B2) * (g * g)
    delta = -ADAM_LR * ((nm / bc1) / (jnp.sqrt(nv / bc2) + ADAM_EPS) + ADAM_WD * w)
    return delta, nm, nv


def _div_tile(n, cap, mult):
    best = None
    for cand in range(mult, min(n, cap) + 1, mult):
        if n % cand == 0:
            best = cand
    assert best is not None, (n, cap, mult)
    return best


def _adamw(w, g, m, v, name):
    rows, cols = w.shape
    tr = rows if rows * cols <= 256 * 1024 else _div_tile(rows, 256, 8)

    def body(w_ref, g_ref, m_ref, v_ref, d_ref, nm_ref, nv_ref):
        d_ref[...], nm_ref[...], nv_ref[...] = _adamw_math(w_ref[...], g_ref[...], m_ref[...], v_ref[...])

    blk = pl.BlockSpec((tr, cols), lambda i: (i, 0))
    return pl.pallas_call(
        body, name=name, grid=(rows // tr,), in_specs=[blk] * 4, out_specs=[blk] * 3,
        out_shape=[jax.ShapeDtypeStruct((rows, cols), F32)] * 3, compiler_params=_cparams())(w, g, m, v)


def _adamw_slots(w, slots, m, v, name):
    _, hr, cols = w.shape
    tr = _div_tile(hr, 128, 16)

    def body(w_ref, s_ref, m_ref, v_ref, g_ref, d_ref, nm_ref, nv_ref):
        g = s_ref[0, 0].astype(F32)
        for k in range(1, N_CHIPS):
            g = g + s_ref[0, k].astype(F32)
        g_ref[0] = g
        d_ref[0], nm_ref[0], nv_ref[0] = _adamw_math(w_ref[0], g, m_ref[0], v_ref[0])

    blk = pl.BlockSpec((1, tr, cols), lambda h, i: (h, i, 0))
    return pl.pallas_call(
        body, name=name, grid=(2, hr // tr),
        in_specs=[blk, pl.BlockSpec((1, N_CHIPS, tr, cols), lambda h, i: (h, 0, i, 0)), blk, blk],
        out_specs=[blk] * 4, out_shape=[jax.ShapeDtypeStruct((2, hr, cols), F32)] * 4,
        compiler_params=_cparams())(w, slots, m, v)


ANY = pl.BlockSpec(memory_space=pl.ANY)


def _place():
    x, y, c = lax.axis_index("x"), lax.axis_index("y"), lax.axis_index("c")
    other_chips = [(1 - x, y), (x, 1 - y), (1 - x, 1 - y)]
    return x, y, c, other_chips


def _remote(src, dst, send_sem, recv_sem, to):
    return pltpu.make_async_remote_copy(src_ref=src, dst_ref=dst, send_sem=send_sem, recv_sem=recv_sem,
                                        device_id=to, device_id_type=MESH)


PIECE_BYTES = 384 * 1024


def _row_pieces(half_rows, cols):
    for n in (4, 2):
        if half_rows % (16 * n) == 0 and half_rows * cols * 2 // n >= PIECE_BYTES:
            return [pl.ds(k * (half_rows // n), half_rows // n) for k in range(n)]
    return [pl.ds(0, half_rows)]


def _pieces(arrays, rows_axis):
    return [(w, rows) for w, a in enumerate(arrays) for rows in _row_pieces(a.shape[rows_axis], a.shape[-1])]


def _gather_exchange(shards):
    nw = len(shards)
    pieces = _pieces(shards, 1)
    npc = len(pieces)

    def build(s_refs, g_refs, sems):
        send_sems, recv_sems, local_sems = sems
        x, y, c, chips = _place()
        me = 2 * x + y
        sibling = (x, y, 1 - c)
        mine = [pltpu.make_async_copy(s_refs[w], g_refs[w].at[me], local_sems.at[w]) for w in range(nw)]
        first = [_remote(s_refs[w].at[c, rows], g_refs[w].at[me, c, rows], send_sems.at[k, p], recv_sems.at[k, p],
                         (cx, cy, c)) for k, (cx, cy) in enumerate(chips) for p, (w, rows) in enumerate(pieces)]

        def start():
            for cp in mine + first:
                cp.start()

        def finish():
            passed = []
            for k, (cx, cy) in enumerate(chips):
                for p, (w, rows) in enumerate(pieces):
                    slab = g_refs[w].at[2 * cx + cy, c, rows]
                    _remote(slab, slab, send_sems.at[k, p], recv_sems.at[k, p], (cx, cy, c)).wait_recv()
                    fwd = _remote(slab, slab, send_sems.at[3 + k, p], recv_sems.at[3 + k, p], sibling)
                    fwd.start()
                    passed.append(fwd)
            for k, (cx, cy) in enumerate(chips):
                for p, (w, rows) in enumerate(pieces):
                    slab = g_refs[w].at[2 * cx + cy, 1 - c, rows]
                    _remote(slab, slab, send_sems.at[3 + k, p], recv_sems.at[3 + k, p], sibling).wait_recv()
            for cp in first + passed:
                cp.wait_send()
            for cp in mine:
                cp.wait()

        return start, finish

    return _Exchange(list(shards), [jax.ShapeDtypeStruct((N_CHIPS,) + s.shape, BF16) for s in shards],
                     [pltpu.SemaphoreType.DMA((6, npc)), pltpu.SemaphoreType.DMA((6, npc)),
                      pltpu.SemaphoreType.DMA((nw,))], build)


def _swap_halves(grads, name):
    nw = len(grads)

    def body(*refs):
        g_refs, sib_refs = refs[:nw], refs[nw:2 * nw]
        send_sems, recv_sems = refs[2 * nw:]
        x, y, c, _ = _place()
        copies = [_remote(g_refs[w].at[s, 1 - c], sib_refs[w].at[s], send_sems.at[s, w], recv_sems.at[s, w],
                          (x, y, 1 - c)) for w in range(nw) for s in range(N_CHIPS)]
        for cp in copies:
            cp.start()
        for cp in copies:
            cp.wait_recv()
        for cp in copies:
            cp.wait_send()

    return pl.pallas_call(
        body, name=name, in_specs=[ANY] * nw, out_specs=[ANY] * nw,
        out_shape=[jax.ShapeDtypeStruct((N_CHIPS,) + g.shape[2:], BF16) for g in grads],
        scratch_shapes=[pltpu.SemaphoreType.DMA((N_CHIPS, nw)), pltpu.SemaphoreType.DMA((N_CHIPS, nw))],
    )(*grads)


def _pair_sum(grad, sib, core, name):
    nchip, _, hr, cols = grad.shape
    tr = _div_tile(hr, 256, 16)

    def body(core_ref, a_ref, b_ref, o_ref):
        o_ref[...] = (a_ref[0].astype(F32) + b_ref[...].astype(F32)).astype(BF16)

    return pl.pallas_call(
        body, name=name,
        grid_spec=pltpu.PrefetchScalarGridSpec(
            num_scalar_prefetch=1, grid=(nchip, hr // tr),
            in_specs=[pl.BlockSpec((1, 1, tr, cols), lambda s, i, core_r: (s, core_r[0], i, 0)),
                      pl.BlockSpec((1, tr, cols), lambda s, i, core_r: (s, i, 0))],
            out_specs=pl.BlockSpec((1, tr, cols), lambda s, i, core_r: (s, i, 0))),
        out_shape=jax.ShapeDtypeStruct((nchip, hr, cols), BF16), compiler_params=_cparams())(core, grad, sib)


def _pair_sum_exchange(sums):
    nw = len(sums)
    pieces = _pieces(sums, 1)
    npc = len(pieces)

    def build(p_refs, o_refs, sems):
        send_sems, recv_sems, local_sems = sems
        x, y, c, chips = _place()
        me = 2 * x + y
        sibling = (x, y, 1 - c)
        mine = [pltpu.make_async_copy(p_refs[w].at[me], o_refs[w].at[c, 3], local_sems.at[w]) for w in range(nw)]
        first = [_remote(p_refs[w].at[2 * cx + cy, rows], o_refs[w].at[c, k, rows], send_sems.at[k, p],
                         recv_sems.at[k, p], (cx, cy, c))
                 for k, (cx, cy) in enumerate(chips) for p, (w, rows) in enumerate(pieces)]

        def start():
            for cp in mine + first:
                cp.start()

        def finish():
            passed = []
            for k in range(N_CHIPS):
                own_waited = set()
                for p, (w, rows) in enumerate(pieces):
                    slab = o_refs[w].at[c, k, rows]
                    if k < 3:
                        first[k * npc + p].wait_recv()
                    elif w not in own_waited:
                        mine[w].wait()
                        own_waited.add(w)
                    fwd = _remote(slab, slab, send_sems.at[3 + k, p], recv_sems.at[3 + k, p], sibling)
                    fwd.start()
                    passed.append(fwd)
            for k in range(N_CHIPS):
                for p, (w, rows) in enumerate(pieces):
                    slab = o_refs[w].at[1 - c, k, rows]
                    _remote(slab, slab, send_sems.at[3 + k, p], recv_sems.at[3 + k, p], sibling).wait_recv()
            for cp in first + passed:
                cp.wait_send()

        return start, finish

    return _Exchange(list(sums), [jax.ShapeDtypeStruct((2,) + p.shape, BF16) for p in sums],
                     [pltpu.SemaphoreType.DMA((7, npc)), pltpu.SemaphoreType.DMA((7, npc)),
                      pltpu.SemaphoreType.DMA((nw,))], build)


def _allreduce_small(vec):
    m_per, n = vec.shape

    def body(x_ref, out_ref, gath_ref, send_sems, recv_sems, local_sem):
        x, y, c, chips = _place()
        me, sibling = (x, y, c), (x, y, 1 - c)

        def rows(px, py, pc):
            return gath_ref.at[pl.ds((4 * px + 2 * py + pc) * m_per, m_per), :]

        def copy(k, block, to, src=None):
            return pltpu.make_async_remote_copy(
                src_ref=rows(*block) if src is None else src, dst_ref=rows(*block),
                send_sem=send_sems.at[k], recv_sem=recv_sems.at[k], device_id=to, device_id_type=MESH)

        mine = pltpu.make_async_copy(x_ref, rows(*me), local_sem)
        mine.start()
        first = [copy(0, me, sibling, src=x_ref)]
        first += [copy(1 + j, me, (*chip, c), src=x_ref) for j, chip in enumerate(chips)]
        for cp in first:
            cp.start()
        passed = [copy(4 + j, (*chip, c), sibling) for j, chip in enumerate(chips)]
        for j, chip in enumerate(chips):
            copy(1 + j, (*chip, c), me).wait_recv()
            passed[j].start()
        copy(0, sibling, me).wait_recv()
        for j, chip in enumerate(chips):
            copy(4 + j, (*chip, 1 - c), me).wait_recv()
        for cp in first + passed:
            cp.wait_send()
        mine.wait()
        acc = gath_ref[pl.ds(0, m_per), :]
        for k in range(1, N_DEV):
            acc = acc + gath_ref[pl.ds(k * m_per, m_per), :]
        out_ref[...] = acc

    vm = pl.BlockSpec(memory_space=pltpu.VMEM)
    return pl.pallas_call(
        body, name="allreduce_small", in_specs=[vm], out_specs=vm,
        out_shape=jax.ShapeDtypeStruct((m_per, n), F32),
        scratch_shapes=[pltpu.VMEM((N_DEV * m_per, n), F32), pltpu.SemaphoreType.DMA((7,)),
                        pltpu.SemaphoreType.DMA((7,)), pltpu.SemaphoreType.DMA],
    )(vec)


def _pack_small(vals, tail=()):
    flat = jnp.concatenate([vals[name].reshape(-1).astype(F32) for name, _ in SMALL] + [v.reshape(1) for v in tail])
    flat = jnp.pad(flat, (0, SMALL_ROWS * LANES - flat.shape[0]))
    return flat.reshape(SMALL_ROWS, LANES)


def _unpack_small(packed):
    flat = packed.reshape(-1)
    out, off = {}, 0
    for name, shape in SMALL:
        n = int(np.prod(shape))
        out[name] = flat[off:off + n].reshape(shape)
        off += n
    return out


def _head_pad_cols(w, heads, real):
    k = w.shape[0]
    return jnp.pad(w.reshape(k, heads, real), ((0, 0), (0, 0), (0, LANES - real))).reshape(k, heads * LANES)


def _rope_tables(positions):
    half = MLA_ROPE // 2
    inv = ROPE_BASE ** (-jnp.arange(half, dtype=F32) / half)
    ang = positions.astype(F32)[:, None] * inv
    cos, sin = jnp.cos(ang), jnp.sin(ang)
    t = positions.shape[0]
    z = lambda n: jnp.zeros((t, n), F32)
    rc = jnp.concatenate([jnp.ones((t, MLA_NOPE), F32), cos, cos, z(LANES - MLA_QK)], axis=1)
    rs1 = jnp.concatenate([z(MLA_NOPE), -sin, z(LANES - MLA_NOPE - half)], axis=1)
    rs2 = jnp.concatenate([z(MLA_NOPE + half), sin, z(LANES - MLA_QK)], axis=1)
    return rc, rs1, rs2


FFN1_WEIGHTS = ("ffn1_w_gu", "ffn1_w_down")
FFN2_WEIGHTS = ("ffn2_w_gu", "ffn2_w_down")
MIXER_WEIGHTS = tuple(n for n, *_ in SHARDED if n not in FFN1_WEIGHTS + FFN2_WEIGHTS)
SHARD_SHAPE = {n: (r, c, kind) for n, r, c, kind in SHARDED}


def _from_blocks(name, gathered):
    r, c, kind = SHARD_SHAPE[name]
    blk = gathered.reshape(N_CHIPS, r, c)
    return blk, (blk.transpose(1, 0, 2).reshape(r, N_CHIPS * c) if kind == "col" else blk.reshape(N_CHIPS * r, c))


def _grad_pair_sums(names, gw, core, tag):
    by_owner = []
    for name in names:
        r, c, kind = SHARD_SHAPE[name]
        blk = gw[name].reshape(r, N_CHIPS, c).transpose(1, 0, 2) if kind == "col" else gw[name].reshape(N_CHIPS, r, c)
        by_owner.append(blk.astype(BF16).reshape(N_CHIPS, 2, r // 2, c))
    received = _swap_halves(by_owner, "grad_swap_" + tag)
    return [_pair_sum(g, s, core, "pair_sum_" + n) for g, s, n in zip(by_owner, received, names)]


def _device_step(x, mem, positions, tgt, small, shards, core):
    d = D_MODEL
    g_ffn1, g_mix, g_ffn2 = small["ffn1_norm"], small["mix_norm"], small["ffn2_norm"]
    big = {}
    for name, g in zip(FFN1_WEIGHTS, _run_exchange(_gather_exchange([shards[n] for n in FFN1_WEIGHTS]), "gather_ffn1")):
        big[name + "#blocks"], big[name] = _from_blocks(name, g)
    wgu1, wd1 = big["ffn1_w_gu#blocks"], big["ffn1_w_down"].reshape(2, FF_TILE, d)
    x1, gpre1, upre1, *rest = _ffn_fwd(x, g_ffn1, wgu1, wd1, "ffn1_fwd",
                                       ex=_gather_exchange([shards[n] for n in MIXER_WEIGHTS]))
    for name, g in zip(MIXER_WEIGHTS, rest):
        big[name + "#blocks"], big[name] = _from_blocks(name, g)
    w_in = big["w_in"]
    w_uv_, w_cq, w_ckv = w_in[:, :COL_CQ], w_in[:, COL_CQ:COL_CKV], w_in[:, COL_CKV:COL_KR]
    w_kr = jnp.pad(w_in[:, COL_KR:COL_QM], ((0, 0), (MLA_NOPE, LANES - MLA_QK)))
    w_qm, w_g = w_in[:, COL_QM:COL_GATE], w_in[:, COL_GATE:]
    segs = (w_uv_, w_cq, w_ckv, w_kr, w_qm, w_g)
    wuq = _head_pad_cols(big["mla_w_uq"], MLA_HEADS, MLA_QK)
    ukv = big["mla_w_ukv"].reshape(MLA_KV_RANK, MLA_HEADS, 2, MLA_NOPE)
    wuk = _head_pad_cols(ukv[:, :, 0].reshape(MLA_KV_RANK, -1), MLA_HEADS, MLA_NOPE)
    wuv = _head_pad_cols(ukv[:, :, 1].reshape(MLA_KV_RANK, -1), MLA_HEADS, MLA_NOPE)
    wkv = big["mem_w_kv"]
    wa, wc, wo = big["w_branch_a"], big["w_branch_c"], big["w_out"]
    wb = jnp.pad(big["w_branch_b"].reshape(MLA_HEADS, MLA_NOPE, d),
                 ((0, 0), (0, LANES - MLA_NOPE), (0, 0))).reshape(MLA_HEADS * LANES, d)
    qg = jnp.pad(small["mla_q_norm"], ((0, 0), (0, LANES - MLA_QK)))
    kg = jnp.pad(small["mla_k_norm"], ((0, 0), (0, LANES - MLA_QK)))
    causal = jnp.tril(jnp.ones((CHUNK, CHUNK), bool))
    wt_f = jnp.where(causal[None], small["sg_w"][0], 0.0)
    wt, wt_t = wt_f.astype(BF16), wt_f.transpose(0, 2, 1).astype(BF16)
    bias_l = jnp.repeat(small["sg_b"][0].T, 64, axis=1)
    rc, rs1, rs2 = _rope_tables(positions)

    h = _rms_fwd(x1, g_mix, "mix_norm_fwd")
    zuv, zcq, zckv, zkr, zqm, zg, *rest = _mm_cols(h, segs, [F32] * 5 + [BF16], "in_proj",
                                                   ex=_gather_exchange([shards[n] for n in FFN2_WEIGHTS]))
    for name, g in zip(FFN2_WEIGHTS, rest):
        big[name + "#blocks"], big[name] = _from_blocks(name, g)
    wgu2, wd2 = big["ffn2_w_gu#blocks"], big["ffn2_w_down"].reshape(2, FF_TILE, d)
    ya = _sgu_fwd(zuv, small["sg_ln_g"], small["sg_ln_b"], wt, bias_l, "sgu_fwd")
    q, k, v, cqn, ckvn = _mla_prep_fwd(zcq, zckv, zkr, small["mla_cq_norm"], small["mla_ckv_norm"], qg, kg,
                                       wuq, wuk, wuv, rc, rs1, rs2, "mla_prep_fwd")
    yb, lse = _attn_fwd(q, k, v, "mla_attn_fwd")
    km, vm, memn = _mem_kv_fwd(mem, small["mem_norm"], wkv, small["mem_k_norm"], "mem_kv_fwd")
    yc = _mem_attn_fwd(zqm, small["mem_q_norm"], km, vm, "mem_attn_fwd")
    x2, merged, pa, pb, pc = _merge_fwd(x1, ya, yb, yc, zg, small["b_gate"], wa, wb, wc, wo, "merge_fwd")
    x3, gpre2, upre2 = _ffn_fwd(x2, g_ffn2, wgu2, wd2, "ffn2_fwd")
    dy, loss_row = _loss_head(x3, tgt, "loss_head")

    gw, gs, slots = {}, {}, {}

    def ffn_grads(prefix, xin, gain, dyin, gpre, upre, wgu, wd, ex=None, ex_names=()):
        dx, dgain, xn, dgt, dup, act, *got = _ffn_bwd(xin, gain, dyin, gpre, upre, wgu, wd, prefix + "_bwd", ex=ex)
        slots.update(zip(ex_names, got))
        gw[prefix + "_w_gu"] = jnp.concatenate(
            [_mm_tn(xn, dgt, prefix + "_dwg"), _mm_tn(xn, dup, prefix + "_dwu")], axis=1)
        gw[prefix + "_w_down"] = _mm_tn(act, dyin, prefix + "_dwd", scale=0.5)
        gs[prefix + "_norm"] = dgain
        return dx

    dx2 = ffn_grads("ffn2", x2, g_ffn2, dy, gpre2, upre2, wgu2, wd2)
    ffn2_sums = _pair_sum_exchange(_grad_pair_sums(FFN2_WEIGHTS, gw, core, "ffn2"))
    dpa, dpb, dpc, dzg, dbg, dya, dyb, dyc, *got = _merge_bwd(dx2, pa, pb, pc, zg, small["b_gate"], wa, wb, wc, wo,
                                                              "merge_bwd", ex=ffn2_sums)
    slots.update(zip(FFN2_WEIGHTS, got))
    gs["b_gate"] = dbg
    gw["w_out"] = _mm_tn(merged, dx2, "dw_out")
    gw["w_branch_a"] = _mm_tn(ya, dpa, "dw_branch_a")
    gw["w_branch_b"] = _mm_tn(yb, dpb, "dw_branch_b").reshape(MLA_HEADS, LANES, d)[:, :MLA_NOPE].reshape(-1, d)
    gw["w_branch_c"] = _mm_tn(yc, dpc, "dw_branch_c")

    dzuv, dwt, dbl, dlg, dlb = _sgu_bwd(zuv, dya, small["sg_ln_g"], small["sg_ln_b"], wt, wt_t, bias_l, "sgu_bwd")
    gs["sg_w"], gs["sg_b"] = dwt[None], dbl[:, :SG_GROUPS].T[None]
    gs["sg_ln_g"], gs["sg_ln_b"] = dlg, dlb

    delta_rows, lse_rows = _attn_bwd_rows(yb, lse, dyb, "mla_attn_bwd_rows")
    dq, dk, dv = _attn_bwd(q, k, v, delta_rows, lse_rows, dyb, "mla_attn_bwd")
    dzcq, dzckv, dzkr, dql, dkl, dgcq, dgckv, dqg, dkg = _mla_prep_bwd(
        zcq, zckv, zkr, small["mla_cq_norm"], small["mla_ckv_norm"], qg, kg, wuq, wuk, wuv, rc, rs1, rs2,
        dq, dk, dv, "mla_prep_bwd")
    gs["mla_cq_norm"], gs["mla_ckv_norm"] = dgcq, dgckv
    gs["mla_q_norm"], gs["mla_k_norm"] = dqg[:, :MLA_QK], dkg[:, :MLA_QK]
    gw["mla_w_uq"] = _mm_tn(cqn, dql, "dw_uq").reshape(MLA_Q_RANK, MLA_HEADS, LANES)[:, :, :MLA_QK].reshape(
        MLA_Q_RANK, -1)
    dwuk = _mm_tn(ckvn, dkl, "dw_uk").reshape(MLA_KV_RANK, MLA_HEADS, LANES)[:, :, :MLA_NOPE]
    dwuv = _mm_tn(ckvn, dv, "dw_uv").reshape(MLA_KV_RANK, MLA_HEADS, LANES)[:, :, :MLA_NOPE]
    gw["mla_w_ukv"] = jnp.concatenate([dwuk, dwuv], axis=2).reshape(MLA_KV_RANK, -1)

    dzqm, dkn, dvm, dmqg = _mem_attn_bwd(zqm, dyc, small["mem_q_norm"], km, vm, "mem_attn_bwd")
    gs["mem_q_norm"] = dmqg
    gw["mem_w_kv"], gs["mem_k_norm"], gs["mem_norm"] = _mem_kv_bwd(
        mem, small["mem_norm"], wkv, small["mem_k_norm"], dkn, dvm, "mem_kv_bwd")

    dzs = (dzuv, dzcq, dzckv, dzkr, dzqm, dzg)
    dh = _mm([(dz, w.T) for dz, w in zip(dzs, segs)], F32, "in_proj_bwd")
    dws = list(_mm_tn_cols(h, dzs[:5], "dw_in_narrow")) + [_mm_tn(h, dzg, "dw_in_gate")]
    dws[3] = dws[3][:, MLA_NOPE:MLA_QK]
    gw["w_in"] = jnp.concatenate(dws, axis=1)
    dx1, gs["mix_norm"] = _rms_bwd(x1, g_mix, dh, dx2, "mix_norm_bwd")
    mixer_sums = _pair_sum_exchange(_grad_pair_sums(MIXER_WEIGHTS, gw, core, "mixer"))
    dx = ffn_grads("ffn1", x, g_ffn1, dx1, gpre1, upre1, wgu1, wd1, ex=mixer_sums, ex_names=MIXER_WEIGHTS)
    ffn1_sums = _pair_sum_exchange(_grad_pair_sums(FFN1_WEIGHTS, gw, core, "ffn1"))
    slots.update(zip(FFN1_WEIGHTS, _run_exchange(ffn1_sums, "grad_exchange_ffn1")))
    return loss_row, dx, slots, gs


def kernel(x, mem, positions, ffn1_norm, ffn1_w_gu, ffn1_w_down, mix_norm, w_in, b_gate, sg_ln_g, sg_ln_b, sg_w, sg_b, mla_cq_norm, mla_w_uq, mla_ckv_norm, mla_w_ukv, mla_q_norm, mla_k_norm, mem_norm, mem_w_kv, mem_q_norm, mem_k_norm, w_branch_a, w_branch_b, w_branch_c, w_out, ffn2_norm, ffn2_w_gu, ffn2_w_down, loss_target, m_ffn1_norm, m_ffn1_w_gu, m_ffn1_w_down, m_mix_norm, m_w_in, m_b_gate, m_sg_ln_g, m_sg_ln_b, m_sg_w, m_sg_b, m_mla_cq_norm, m_mla_w_uq, m_mla_ckv_norm, m_mla_w_ukv, m_mla_q_norm, m_mla_k_norm, m_mem_norm, m_mem_w_kv, m_mem_q_norm, m_mem_k_norm, m_w_branch_a, m_w_branch_b, m_w_branch_c, m_w_out, m_ffn2_norm, m_ffn2_w_gu, m_ffn2_w_down, v_ffn1_norm, v_ffn1_w_gu, v_ffn1_w_down, v_mix_norm, v_w_in, v_b_gate, v_sg_ln_g, v_sg_ln_b, v_sg_w, v_sg_b, v_mla_cq_norm, v_mla_w_uq, v_mla_ckv_norm, v_mla_w_ukv, v_mla_q_norm, v_mla_k_norm, v_mem_norm, v_mem_w_kv, v_mem_q_norm, v_mem_k_norm, v_w_branch_a, v_w_branch_b, v_w_branch_c, v_w_out, v_ffn2_norm, v_ffn2_w_gu, v_ffn2_w_down):
    args = dict(locals())
    weights = {n: args[n] for n in WEIGHT_ORDER}
    mom_m = {n: args["m_" + n] for n in WEIGHT_ORDER}
    mom_v = {n: args["v_" + n] for n in WEIGHT_ORDER}
    small = {n: weights[n] for n, _ in SMALL}
    halves = lambda a, r, c: a.reshape(2, r // 2, c)

    shards = {n: halves(weights[n][0].astype(BF16), r, c) for n, r, c, _ in SHARDED}
    core = lax.axis_index("c").astype(jnp.int32).reshape(1)
    loss_row, dx, slots, gs = _device_step(x[0], mem[0], positions[0], loss_target[0], small, shards, core)
    summed = _allreduce_small(_pack_small(gs, tail=[loss_row[0, 0]]))
    loss = summed.reshape(-1)[_N_SMALL]
    small_grads = _unpack_small(summed)

    grads, deltas, new_m, new_v = {}, {}, {}, {}
    for name, r, c, _ in SHARDED:
        outs = _adamw_slots(halves(weights[name][0], r, c), slots[name], halves(mom_m[name][0], r, c),
                            halves(mom_v[name][0], r, c), "adamw_" + name)
        shape = weights[name].shape
        grads[name], deltas[name], new_m[name], new_v[name] = [o.reshape(shape) for o in outs]
    dlt, nm, nv = _adamw(_pack_small(small), _pack_small(small_grads), _pack_small({n: mom_m[n] for n, _ in SMALL}),
                         _pack_small({n: mom_v[n] for n, _ in SMALL}), "adamw_small")
    for name, _ in SMALL:
        grads[name] = small_grads[name]
    deltas.update(_unpack_small(dlt))
    new_m.update(_unpack_small(nm))
    new_v.update(_unpack_small(nv))

    return (loss, dx[None], *[grads[n] for n in WEIGHT_ORDER], *[deltas[n] for n in WEIGHT_ORDER],
            *[new_m[n] for n in WEIGHT_ORDER], *[new_v[n] for n in WEIGHT_ORDER])
```

```python
import functools
from typing import Callable, NamedTuple

import numpy as np
import jax
import jax.numpy as jnp
from jax import lax
from jax.experimental import pallas as pl
from jax.experimental.pallas import tpu as pltpu

F32 = jnp.float32
BF16 = jnp.bfloat16

D_MODEL = 1024
D_FF = 2816
FF_TILE = 1408
SG_WIDTH = 512
SG_GROUPS = 8
CHUNK = 128
MLA_HEADS = 8
MLA_QK = 96
MLA_NOPE = 64
MLA_ROPE = 32
MLA_Q_RANK = 384
MLA_KV_RANK = 256
MEM_HEADS = 4
MEM_LEN = 256
LANES = 128
EPS = 1e-6
NEG = -1e30
ROPE_BASE = 10000.0
N_CHIPS = 4
N_DEV = 8

ADAM_LR = 0.001
ADAM_B1 = 0.9
ADAM_B2 = 0.999
ADAM_EPS = 1e-08
ADAM_WD = 0.01
ADAM_STEP = 10

COL_V = 512
COL_CQ = 1024
COL_CKV = 1408
COL_KR = 1664
COL_QM = 1696
COL_GATE = 2208
IN_COLS = 5280

VMEM_LIMIT_BYTES = 56 * 1024 * 1024
INV_SQRT2 = 0.7071067811865476
INV_SQRT_2PI = 0.3989422804014327
LOG2E = 1.4426950408889634
ATTN_SCALE = MLA_QK ** -0.5
V_ONES_LANE = 64
ATTN_SCALE2 = ATTN_SCALE * LOG2E

SHARDED = (
    ("ffn1_w_gu", 1024, 1408, "col"),
    ("ffn1_w_down", 704, 1024, "row"),
    ("w_in", 1024, 1320, "col"),
    ("mla_w_uq", 384, 192, "col"),
    ("mla_w_ukv", 256, 256, "col"),
    ("mem_w_kv", 256, 1024, "row"),
    ("w_branch_a", 512, 256, "col"),
    ("w_branch_b", 512, 256, "col"),
    ("w_branch_c", 512, 256, "col"),
    ("w_out", 256, 1024, "row"),
    ("ffn2_w_gu", 1024, 1408, "col"),
    ("ffn2_w_down", 704, 1024, "row"),
)
SMALL = (
    ("ffn1_norm", (1, 1024)), ("mix_norm", (1, 1024)), ("b_gate", (1, 3072)),
    ("sg_ln_g", (1, 512)), ("sg_ln_b", (1, 512)), ("sg_w", (1, 8, 128, 128)),
    ("sg_b", (1, 8, 128)), ("mla_cq_norm", (1, 384)), ("mla_ckv_norm", (1, 256)),
    ("mla_q_norm", (1, 96)), ("mla_k_norm", (1, 96)), ("mem_norm", (1, 1024)),
    ("mem_q_norm", (1, 128)), ("mem_k_norm", (1, 128)), ("ffn2_norm", (1, 1024)),
)
WEIGHT_ORDER = (
    "ffn1_norm", "ffn1_w_gu", "ffn1_w_down", "mix_norm", "w_in", "b_gate", "sg_ln_g", "sg_ln_b",
    "sg_w", "sg_b", "mla_cq_norm", "mla_w_uq", "mla_ckv_norm", "mla_w_ukv", "mla_q_norm",
    "mla_k_norm", "mem_norm", "mem_w_kv", "mem_q_norm", "mem_k_norm", "w_branch_a", "w_branch_b",
    "w_branch_c", "w_out", "ffn2_norm", "ffn2_w_gu", "ffn2_w_down",
)

_N_SMALL = sum(int(np.prod(s)) for _, s in SMALL)
SMALL_ROWS = -(-_N_SMALL // (LANES * 8)) * 8

MESH = pl.DeviceIdType.MESH


def _cparams():
    return pltpu.CompilerParams(vmem_limit_bytes=VMEM_LIMIT_BYTES)


def _dot(a, b):
    return jnp.dot(a, b, preferred_element_type=F32)


def _dot_nt(a, b):
    return lax.dot_general(a, b, (((1,), (1,)), ((), ())), preferred_element_type=F32)


def _dot_tn(a, b):
    return lax.dot_general(a, b, (((0,), (0,)), ((), ())), preferred_element_type=F32)


def _gelu(x):
    return 0.5 * x * (1.0 + lax.erf(x * INV_SQRT2))


def _gelu_grad(x):
    return 0.5 * (1.0 + lax.erf(x * INV_SQRT2)) + x * jnp.exp(-0.5 * x * x) * INV_SQRT_2PI


def _rstd(x, n):
    return lax.rsqrt(jnp.sum(x * x, axis=-1, keepdims=True) * (1.0 / n) + EPS)


def _rms_vjp(x, r, g, dy, n):
    dxh = dy * g
    dx = r * dxh - x * (r * r * r) * (jnp.sum(dxh * x, axis=-1, keepdims=True) * (1.0 / n))
    return dx, dy * x * r


def _row_tile(t, want):
    return min(t, want)


def _wide_tile(n):
    if n <= 1024:
        return n
    if n % 1024 == 0:
        return 1024
    assert n % FF_TILE == 0, n
    return FF_TILE


def _mm_cols(a, ws, out_dtypes, name, ex=None):
    t, kdim = a.shape
    tm = _row_tile(t, 256)
    n = len(ws)

    def body(*refs):
        av = refs[0][...]
        for w_ref, o_ref in zip(refs[1:1 + n], refs[1 + n:]):
            o_ref[...] = _dot(av, w_ref[...]).astype(o_ref.dtype)

    row = lambda width: pl.BlockSpec((tm, width), lambda i: (i, 0))
    return _call_with_exchange(
        ex, body, name, (t // tm,),
        [row(kdim)] + [pl.BlockSpec(w.shape, lambda i: (0, 0)) for w in ws],
        [row(w.shape[1]) for w in ws],
        [jax.ShapeDtypeStruct((t, w.shape[1]), dt) for w, dt in zip(ws, out_dtypes)], [], (a, *ws))


def _proj_norm_bwd(dzs, wts, x, g, dres, name):
    t, d = x.shape
    tm = _row_tile(t, 256)
    n = len(dzs)

    def body(*refs):
        x_ref, g_ref, r_ref, dx_ref, dg_ref = refs[2 * n:]

        @pl.when(pl.program_id(0) == 0)
        def _():
            dg_ref[...] = jnp.zeros_like(dg_ref)

        dh = None
        for dz_ref, w_ref in zip(refs[:n], refs[n:2 * n]):
            part = _dot(dz_ref[...], w_ref[...])
            dh = part if dh is None else dh + part
        xv = x_ref[...]
        dx, dgr = _rms_vjp(xv, _rstd(xv, d), g_ref[...], dh, d)
        dx_ref[...] = r_ref[...] + dx
        dg_ref[...] += jnp.sum(dgr, axis=0, keepdims=True)

    row = lambda width: pl.BlockSpec((tm, width), lambda i: (i, 0))
    vec = pl.BlockSpec((1, d), lambda i: (0, 0))
    return pl.pallas_call(
        body, name=name, grid=(t // tm,),
        in_specs=[row(dz.shape[1]) for dz in dzs] + [pl.BlockSpec(w.shape, lambda i: (0, 0)) for w in wts]
        + [row(d), vec, row(d)],
        out_specs=[row(d), vec],
        out_shape=[jax.ShapeDtypeStruct((t, d), F32), jax.ShapeDtypeStruct((1, d), F32)],
        compiler_params=_cparams())(*dzs, *wts, x, g, dres)


def _mm_tn_cols(a, bs, name):
    t, m = a.shape
    tk = _row_tile(t, 512)
    n = len(bs)

    def body(*refs):
        @pl.when(pl.program_id(0) == 0)
        def _():
            for o_ref in refs[1 + n:]:
                o_ref[...] = jnp.zeros_like(o_ref)

        av = refs[0][...].astype(BF16)
        for b_ref, o_ref in zip(refs[1:1 + n], refs[1 + n:]):
            o_ref[...] += _dot_tn(av, b_ref[...].astype(BF16))

    row = lambda width: pl.BlockSpec((tk, width), lambda k: (k, 0))
    return pl.pallas_call(
        body, name=name, grid=(t // tk,), in_specs=[row(m)] + [row(b.shape[1]) for b in bs],
        out_specs=[pl.BlockSpec((m, b.shape[1]), lambda k: (0, 0)) for b in bs],
        out_shape=[jax.ShapeDtypeStruct((m, b.shape[1]), F32) for b in bs],
        compiler_params=_cparams())(a, *bs)


def _mm_tn(a, b, name, scale=1.0):
    t, m = a.shape
    n = b.shape[1]
    tm, tn = _wide_tile(m), _wide_tile(n)
    tk = _row_tile(t, 1024)
    nk = t // tk

    def body(a_ref, b_ref, o_ref):
        k = pl.program_id(2)

        @pl.when(k == 0)
        def _():
            o_ref[...] = jnp.zeros_like(o_ref)

        o_ref[...] += _dot_tn(a_ref[...].astype(BF16), b_ref[...].astype(BF16))
        if scale != 1.0:
            @pl.when(k == nk - 1)
            def _():
                o_ref[...] = o_ref[...] * scale

    return pl.pallas_call(
        body, name=name, grid=(m // tm, n // tn, nk),
        in_specs=[pl.BlockSpec((tk, tm), lambda i, j, k: (k, i)),
                  pl.BlockSpec((tk, tn), lambda i, j, k: (k, j))],
        out_specs=pl.BlockSpec((tm, tn), lambda i, j, k: (i, j)),
        out_shape=jax.ShapeDtypeStruct((m, n), F32), compiler_params=_cparams())(a, b)


class _Exchange(NamedTuple):
    operands: list
    out_shapes: list
    sem_shapes: list
    build: Callable


def _call_with_exchange(ex, body, name, grid, in_specs, out_specs, out_shape, scratch_shapes, operands):
    if ex is None:
        return pl.pallas_call(body, name=name, grid=grid, in_specs=in_specs, out_specs=out_specs, out_shape=out_shape,
                              scratch_shapes=scratch_shapes, compiler_params=_cparams())(*operands)
    n_in, n_out, n_scr = len(in_specs), len(out_specs), len(scratch_shapes)
    k_in, k_out = len(ex.operands), len(ex.out_shapes)

    def carried(*refs):
        a, b = n_in, n_in + k_in
        c, e = b + n_out, b + n_out + k_out
        f = e + n_scr
        start, finish = ex.build(refs[a:b], refs[c:e], refs[f:])
        steps = [pl.program_id(ax) for ax in range(len(grid))]
        first = functools.reduce(jnp.logical_and, [s == 0 for s in steps])
        last = functools.reduce(jnp.logical_and, [s == n - 1 for s, n in zip(steps, grid)])
        pl.when(first)(start)
        body(*refs[:a], *refs[b:c], *refs[e:f])
        pl.when(last)(finish)

    return pl.pallas_call(
        carried, name=name, grid=grid, in_specs=list(in_specs) + [ANY] * k_in,
        out_specs=list(out_specs) + [ANY] * k_out, out_shape=list(out_shape) + list(ex.out_shapes),
        scratch_shapes=list(scratch_shapes) + list(ex.sem_shapes), compiler_params=_cparams(),
    )(*operands, *ex.operands)


def _run_exchange(ex, name):
    k_in, k_out = len(ex.operands), len(ex.out_shapes)

    def body(*refs):
        start, finish = ex.build(refs[:k_in], refs[k_in:k_in + k_out], refs[k_in + k_out:])
        start()
        finish()

    return pl.pallas_call(body, name=name, in_specs=[ANY] * k_in, out_specs=[ANY] * k_out,
                          out_shape=list(ex.out_shapes), scratch_shapes=list(ex.sem_shapes))(*ex.operands)


def _ffn_fwd(x, g, wgu4, wd2, name, ex=None, next_gain=None, target=None):
    t, d = x.shape
    tm = _row_tile(t, 512)
    assert next_gain is None or target is None
    extra = [a for a in (next_gain, target) if a is not None]

    def body(*refs):
        x_ref, g_ref, wg_ref, wu_ref, wd_ref = refs[:5]
        e_ref = refs[5] if extra else None
        outs, (xn_scr, acc_scr) = refs[5 + len(extra):-2], refs[-2:]
        if target is not None:
            dy_ref, loss_ref, gg_ref, uu_ref = outs
        elif next_gain is not None:
            o_ref, gg_ref, uu_ref, h_ref = outs
        else:
            o_ref, gg_ref, uu_ref = outs
        i, j = pl.program_id(0), pl.program_id(1)

        @pl.when(j == 0)
        def _():
            xv = x_ref[...]
            xn_scr[...] = (xv * _rstd(xv, d) * g_ref[...]).astype(BF16)
            acc_scr[...] = jnp.zeros_like(acc_scr)

        if target is not None:
            @pl.when((i == 0) & (j == 0))
            def _():
                loss_ref[...] = jnp.zeros_like(loss_ref)

        xn = xn_scr[...]
        gg = _dot(xn, wg_ref[0])
        uu = _dot(xn, wu_ref[0])
        gg_ref[...] = gg.astype(BF16)
        uu_ref[...] = uu.astype(BF16)
        act = gg * jax.nn.sigmoid(gg) * uu
        acc_scr[...] += _dot(act.astype(BF16), wd_ref[0])

        @pl.when(j == 1)
        def _():
            y = x_ref[...] + 0.5 * acc_scr[...]
            if target is not None:
                e = y - e_ref[...]
                dy_ref[...] = e * (1.0 / d)
                part = 0.5 * jnp.sum(jnp.sum(e * e, axis=-1, keepdims=True) * (1.0 / d), axis=0, keepdims=True)
                loss_ref[...] += jnp.broadcast_to(part, loss_ref.shape)
            else:
                o_ref[...] = y
                if next_gain is not None:
                    h_ref[...] = (y * _rstd(y, d) * e_ref[...]).astype(BF16)

    row = pl.BlockSpec((tm, d), lambda i, j: (i, 0))
    vec = pl.BlockSpec((1, d), lambda i, j: (0, 0))
    ffb = pl.BlockSpec((tm, FF_TILE), lambda i, j: (i, j))
    f32_rows, bf16_ff = jax.ShapeDtypeStruct((t, d), F32), jax.ShapeDtypeStruct((t, D_FF), BF16)
    if target is not None:
        extra_spec, out_specs = [row], [row, pl.BlockSpec((1, LANES), lambda i, j: (0, 0)), ffb, ffb]
        out_shape = [f32_rows, jax.ShapeDtypeStruct((1, LANES), F32), bf16_ff, bf16_ff]
    elif next_gain is not None:
        extra_spec, out_specs = [vec], [row, ffb, ffb, row]
        out_shape = [f32_rows, bf16_ff, bf16_ff, jax.ShapeDtypeStruct((t, d), BF16)]
    else:
        extra_spec, out_specs, out_shape = [], [row, ffb, ffb], [f32_rows, bf16_ff, bf16_ff]
    return _call_with_exchange(
        ex, body, name, (t // tm, 2),
        [row, vec,
         pl.BlockSpec((1, d, FF_TILE), lambda i, j: (j, 0, 0)),
         pl.BlockSpec((1, d, FF_TILE), lambda i, j: (j + 2, 0, 0)),
         pl.BlockSpec((1, FF_TILE, d), lambda i, j: (j, 0, 0))] + extra_spec,
        out_specs, out_shape,
        [pltpu.VMEM((tm, d), BF16), pltpu.VMEM((tm, d), F32)], (x, g, wgu4, wgu4, wd2, *extra))


def _ffn_bwd(x, g, dy, gpre, upre, wgu4, wd2, name, ex=None):
    t, d = x.shape
    tm = _row_tile(t, 512)

    def body(dy_ref, gg_ref, uu_ref, wgu_hbm, wd_hbm, dg_ref, du_ref, act_ref, part_ref, wg_ref, wu_ref, wd_ref):
        j = pl.program_id(0)

        @pl.when(pl.program_id(1) == 0)
        def _():
            pltpu.sync_copy(wgu_hbm.at[j], wg_ref.at[0])
            pltpu.sync_copy(wgu_hbm.at[j + 2], wu_ref.at[0])
            pltpu.sync_copy(wd_hbm.at[j], wd_ref.at[0])

        gg = gg_ref[...].astype(F32)
        uu = uu_ref[...].astype(F32)
        sg = jax.nn.sigmoid(gg)
        silu = gg * sg
        act_ref[...] = (silu * uu).astype(BF16)
        dyh = (0.5 * dy_ref[...]).astype(BF16)
        dact = _dot_nt(dyh, wd_ref[0])
        du = (dact * silu).astype(BF16)
        dgt = (dact * uu * (sg * (1.0 + gg * (1.0 - sg)))).astype(BF16)
        du_ref[...] = du
        dg_ref[...] = dgt
        part_ref[0] = _dot_nt(dgt, wg_ref[0]) + _dot_nt(du, wu_ref[0])

    row = pl.BlockSpec((tm, d), lambda j, i: (i, 0))
    ffb = pl.BlockSpec((tm, FF_TILE), lambda j, i: (i, j))
    dgt, dup, act, parts, *got = _call_with_exchange(
        ex, body, name, (2, t // tm),
        [row, ffb, ffb, ANY, ANY],
        [ffb, ffb, ffb, pl.BlockSpec((1, tm, d), lambda j, i: (j, i, 0))],
        [jax.ShapeDtypeStruct((t, D_FF), BF16)] * 3 + [jax.ShapeDtypeStruct((2, t, d), F32)],
        [pltpu.VMEM((1, d, FF_TILE), BF16), pltpu.VMEM((1, d, FF_TILE), BF16), pltpu.VMEM((1, FF_TILE, d), BF16)],
        (dy, gpre, upre, wgu4, wd2))

    def norm_body(x_ref, g_ref, p_ref, dy_ref, dx_ref, dgain_ref, xn_ref):
        @pl.when(pl.program_id(0) == 0)
        def _():
            dgain_ref[...] = jnp.zeros_like(dgain_ref)

        xv = x_ref[...]
        r = _rstd(xv, d)
        xn_ref[...] = (xv * r * g_ref[...]).astype(BF16)
        dx, dgr = _rms_vjp(xv, r, g_ref[...], p_ref[0] + p_ref[1], d)
        dx_ref[...] = dy_ref[...] + dx
        dgain_ref[...] += jnp.sum(dgr, axis=0, keepdims=True)

    tn = _row_tile(t, 256)
    nrow = pl.BlockSpec((tn, d), lambda i: (i, 0))
    vec = pl.BlockSpec((1, d), lambda i: (0, 0))
    dx, dgain, xn = pl.pallas_call(
        norm_body, name=name + "_norm", grid=(t // tn,),
        in_specs=[nrow, vec, pl.BlockSpec((2, tn, d), lambda i: (0, i, 0)), nrow],
        out_specs=[nrow, vec, nrow],
        out_shape=[jax.ShapeDtypeStruct((t, d), F32), jax.ShapeDtypeStruct((1, d), F32),
                   jax.ShapeDtypeStruct((t, d), BF16)],
        compiler_params=_cparams())(x, g, parts, dy)
    return [dx, dgain, xn, dgt, dup, act] + got


def _sgu_layernorm(vpre, lg, lb):
    v = _gelu(vpre)
    mu = jnp.mean(v, axis=-1, keepdims=True)
    xc = v - mu
    rstd = lax.rsqrt(jnp.mean(xc * xc, axis=-1, keepdims=True) + EPS)
    xhat = xc * rstd
    return xhat, rstd, xhat * lg + lb


def _sgu_fwd(zuv, lg, lb, wt, bias_l, name):
    t = zuv.shape[0]
    tm = _row_tile(t, 512)

    def body(u_ref, v_ref, lg_ref, lb_ref, wt_ref, bl_ref, o_ref, vln_scr):
        _, _, vln = _sgu_layernorm(v_ref[...], lg_ref[...], lb_ref[...])
        vln_scr[...] = vln.astype(BF16)
        lo = lax.broadcasted_iota(jnp.int32, (CHUNK, LANES), 1) < 64
        for c in range(tm // CHUNK):
            rows = slice(c * CHUNK, (c + 1) * CHUNK)
            for p in range(SG_GROUPS // 2):
                cols = slice(p * LANES, (p + 1) * LANES)
                vp = vln_scr[rows, cols]
                mixed = jnp.where(lo, _dot(wt_ref[2 * p], vp), _dot(wt_ref[2 * p + 1], vp)) + bl_ref[:, cols]
                o_ref[rows, cols] = (_gelu(u_ref[rows, cols]) * mixed).astype(BF16)

    half = lambda k: pl.BlockSpec((tm, SG_WIDTH), lambda i: (i, k))
    vec = pl.BlockSpec((1, SG_WIDTH), lambda i: (0, 0))
    return pl.pallas_call(
        body, name=name, grid=(t // tm,),
        in_specs=[half(0), half(1), vec, vec,
                  pl.BlockSpec((SG_GROUPS, CHUNK, CHUNK), lambda i: (0, 0, 0)),
                  pl.BlockSpec((CHUNK, SG_WIDTH), lambda i: (0, 0))],
        out_specs=pl.BlockSpec((tm, SG_WIDTH), lambda i: (i, 0)),
        out_shape=jax.ShapeDtypeStruct((t, SG_WIDTH), BF16),
        scratch_shapes=[pltpu.VMEM((tm, SG_WIDTH), BF16)],
        compiler_params=_cparams())(zuv, zuv, lg, lb, wt, bias_l)


def _sgu_bwd(zuv, dya, lg, lb, wt, wt_t, bias_l, name):
    t = zuv.shape[0]
    tm = _row_tile(t, 256)
    nsteps = t // tm

    def body(u_ref, v_ref, dy_ref, lg_ref, lb_ref, wt_ref, wtt_ref, bl_ref,
             dz_ref, dwt_ref, dbl_ref, dlg_ref, dlb_ref, vln_scr, dvln_scr, dbacc_scr):
        step = pl.program_id(0)

        @pl.when(step == 0)
        def _():
            dwt_ref[...] = jnp.zeros_like(dwt_ref)
            dlg_ref[...] = jnp.zeros_like(dlg_ref)
            dlb_ref[...] = jnp.zeros_like(dlb_ref)
            dbl_ref[...] = jnp.zeros_like(dbl_ref)
            dbacc_scr[...] = jnp.zeros_like(dbacc_scr)

        vpre = v_ref[...]
        lgv = lg_ref[...]
        xhat, rstd, vln = _sgu_layernorm(vpre, lgv, lb_ref[...])
        vln_scr[...] = vln.astype(BF16)
        lo = lax.broadcasted_iota(jnp.int32, (CHUNK, LANES), 1) < 64
        for c in range(tm // CHUNK):
            rows = slice(c * CHUNK, (c + 1) * CHUNK)
            for p in range(SG_GROUPS // 2):
                cols = slice(p * LANES, (p + 1) * LANES)
                vp = vln_scr[rows, cols]
                mixed = jnp.where(lo, _dot(wt_ref[2 * p], vp), _dot(wt_ref[2 * p + 1], vp)) + bl_ref[:, cols]
                upre = u_ref[rows, cols]
                dyp = dy_ref[rows, cols]
                dz_ref[rows, cols] = (dyp * mixed * _gelu_grad(upre)).astype(BF16)
                dm = dyp * _gelu(upre)
                dbacc_scr[:, cols] += dm
                dlo = jnp.where(lo, dm, 0.0).astype(BF16)
                dhi = jnp.where(lo, 0.0, dm).astype(BF16)
                dvln_scr[rows, cols] = _dot(wtt_ref[2 * p], dlo) + _dot(wtt_ref[2 * p + 1], dhi)
                dwt_ref[2 * p] += _dot_nt(dlo, vp)
                dwt_ref[2 * p + 1] += _dot_nt(dhi, vp)
        dvln = dvln_scr[...]
        dlg_ref[...] += jnp.sum(dvln * xhat, axis=0, keepdims=True)
        dlb_ref[...] += jnp.sum(dvln, axis=0, keepdims=True)
        dxh = dvln * lgv
        dv = rstd * (dxh - jnp.mean(dxh, axis=-1, keepdims=True)
                     - xhat * jnp.mean(dxh * xhat, axis=-1, keepdims=True))
        dz_ref[:, SG_WIDTH:] = (dv * _gelu_grad(vpre)).astype(BF16)

        @pl.when(step == nsteps - 1)
        def _():
            rr = lax.broadcasted_iota(jnp.int32, (CHUNK, CHUNK), 0)
            cc = lax.broadcasted_iota(jnp.int32, (CHUNK, CHUNK), 1)
            tril = (cc <= rr).astype(F32)
            for gidx in range(SG_GROUPS):
                dwt_ref[gidx] = dwt_ref[gidx] * tril
            kk = lax.broadcasted_iota(jnp.int32, (SG_WIDTH, LANES), 0)
            gg = lax.broadcasted_iota(jnp.int32, (SG_WIDTH, LANES), 1)
            sel = ((kk // 64) == gg).astype(F32)
            dbl_ref[...] = jnp.dot(dbacc_scr[...], sel, preferred_element_type=F32,
                                   precision=lax.Precision.HIGHEST)

    half = lambda k: pl.BlockSpec((tm, SG_WIDTH), lambda i: (i, k))
    vec = pl.BlockSpec((1, SG_WIDTH), lambda i: (0, 0))
    wspec = pl.BlockSpec((SG_GROUPS, CHUNK, CHUNK), lambda i: (0, 0, 0))
    return pl.pallas_call(
        body, name=name, grid=(nsteps,),
        in_specs=[half(0), half(1), pl.BlockSpec((tm, SG_WIDTH), lambda i: (i, 0)), vec, vec,
                  wspec, wspec, pl.BlockSpec((CHUNK, SG_WIDTH), lambda i: (0, 0))],
        out_specs=[pl.BlockSpec((tm, 2 * SG_WIDTH), lambda i: (i, 0)), wspec,
                   pl.BlockSpec((CHUNK, LANES), lambda i: (0, 0)), vec, vec],
        out_shape=[jax.ShapeDtypeStruct((t, 2 * SG_WIDTH), BF16),
                   jax.ShapeDtypeStruct((SG_GROUPS, CHUNK, CHUNK), F32),
                   jax.ShapeDtypeStruct((CHUNK, LANES), F32),
                   jax.ShapeDtypeStruct((1, SG_WIDTH), F32), jax.ShapeDtypeStruct((1, SG_WIDTH), F32)],
        scratch_shapes=[pltpu.VMEM((tm, SG_WIDTH), BF16), pltpu.VMEM((tm, SG_WIDTH), F32),
                        pltpu.VMEM((CHUNK, SG_WIDTH), F32)],
        compiler_params=_cparams())(zuv, zuv, dya, lg, lb, wt, wt_t, bias_l)


def _rope(x, c, s1, s2):
    return x * c + pltpu.roll(x, LANES - 16, 1) * s1 + pltpu.roll(x, 16, 1) * s2


def _rope_t(dy, c, s1, s2):
    return dy * c + pltpu.roll(dy * s1, 16, 1) + pltpu.roll(dy * s2, LANES - 16, 1)


def _mla_prep_fwd(zcq, zckv, zkr, gcq, gckv, qg, kg, wuq, wuk, wuv, rc, rs1, rs2, name, ex=None):
    t = zcq.shape[0]
    tm = _row_tile(t, 256)
    hd = MLA_HEADS * LANES

    def body(zcq_ref, zckv_ref, zkr_ref, gcq_ref, gckv_ref, qg_ref, kg_ref, wuq_ref, wuk_ref, wuv_ref,
             c_ref, s1_ref, s2_ref, q_ref, k_ref, v_ref, cqn_ref, ckvn_ref):
        c, s1, s2 = c_ref[...], s1_ref[...], s2_ref[...]
        xq = zcq_ref[...]
        cqn = (xq * _rstd(xq, MLA_Q_RANK) * gcq_ref[...]).astype(BF16)
        cqn_ref[...] = cqn
        ql = _dot(cqn, wuq_ref[...])
        xk = zckv_ref[...]
        ckvn = (xk * _rstd(xk, MLA_KV_RANK) * gckv_ref[...]).astype(BF16)
        ckvn_ref[...] = ckvn
        kl = _dot(ckvn, wuk_ref[...])
        slot_lane = lax.broadcasted_iota(jnp.int32, (tm, hd), 1) % LANES
        v_ref[...] = jnp.where(slot_lane == V_ONES_LANE, 1.0, _dot(ckvn, wuv_ref[...])).astype(BF16)
        kr = zkr_ref[...]
        for h in range(MLA_HEADS):
            sl = slice(h * LANES, (h + 1) * LANES)
            qh = ql[:, sl]
            q_ref[:, sl] = (_rope(qh * _rstd(qh, MLA_QK) * qg_ref[...], c, s1, s2) * ATTN_SCALE2).astype(BF16)
            kh = kl[:, sl] + kr
            k_ref[:, sl] = _rope(kh * _rstd(kh, MLA_QK) * kg_ref[...], c, s1, s2).astype(BF16)

    row = lambda n: pl.BlockSpec((tm, n), lambda i: (i, 0))
    full = lambda a: pl.BlockSpec(a.shape, lambda i: (0, 0))
    return _call_with_exchange(
        ex, body, name, (t // tm,),
        [row(MLA_Q_RANK), row(MLA_KV_RANK), row(LANES), full(gcq), full(gckv), full(qg), full(kg),
         full(wuq), full(wuk), full(wuv), row(LANES), row(LANES), row(LANES)],
        [row(hd), row(hd), row(hd), row(MLA_Q_RANK), row(MLA_KV_RANK)],
        [jax.ShapeDtypeStruct((t, hd), BF16)] * 3
        + [jax.ShapeDtypeStruct((t, MLA_Q_RANK), BF16), jax.ShapeDtypeStruct((t, MLA_KV_RANK), BF16)],
        [], (zcq, zckv, zkr, gcq, gckv, qg, kg, wuq, wuk, wuv, rc, rs1, rs2))


def _mla_prep_bwd(zcq, zckv, zkr, gcq, gckv, qg, kg, wuq, wuk, wuv, rc, rs1, rs2, dq, dk, dv, name):
    t = zcq.shape[0]
    tm = _row_tile(t, 256)
    hd = MLA_HEADS * LANES

    def body(zcq_ref, zckv_ref, zkr_ref, gcq_ref, gckv_ref, qg_ref, kg_ref, wuq_ref, wuk_ref, wuv_ref,
             c_ref, s1_ref, s2_ref, dq_ref, dk_ref, dv_ref,
             dzcq_ref, dzckv_ref, dzkr_ref, dql_ref, dkl_ref, dgcq_ref, dgckv_ref, dqg_ref, dkg_ref):
        @pl.when(pl.program_id(0) == 0)
        def _():
            for ref in (dgcq_ref, dgckv_ref, dqg_ref, dkg_ref):
                ref[...] = jnp.zeros_like(ref)

        c, s1, s2 = c_ref[...], s1_ref[...], s2_ref[...]
        qgv, kgv = qg_ref[...], kg_ref[...]
        xq = zcq_ref[...]
        rq = _rstd(xq, MLA_Q_RANK)
        ql = _dot((xq * rq * gcq_ref[...]).astype(BF16), wuq_ref[...])
        xk = zckv_ref[...]
        rk = _rstd(xk, MLA_KV_RANK)
        kl = _dot((xk * rk * gckv_ref[...]).astype(BF16), wuk_ref[...])
        kr = zkr_ref[...]
        dqg_acc = jnp.zeros((tm, LANES), F32)
        dkg_acc = jnp.zeros((tm, LANES), F32)
        dkr = jnp.zeros((tm, LANES), F32)
        for h in range(MLA_HEADS):
            sl = slice(h * LANES, (h + 1) * LANES)
            qh = ql[:, sl]
            dqh, dgr = _rms_vjp(qh, _rstd(qh, MLA_QK), qgv, _rope_t(dq_ref[:, sl], c, s1, s2), MLA_QK)
            dql_ref[:, sl] = dqh.astype(BF16)
            dqg_acc += dgr
            kh = kl[:, sl] + kr
            dkh, dgr = _rms_vjp(kh, _rstd(kh, MLA_QK), kgv, _rope_t(dk_ref[:, sl], c, s1, s2), MLA_QK)
            dkl_ref[:, sl] = dkh.astype(BF16)
            dkg_acc += dgr
            dkr += dkh
        dqg_ref[...] += jnp.sum(dqg_acc, axis=0, keepdims=True)
        dkg_ref[...] += jnp.sum(dkg_acc, axis=0, keepdims=True)
        lane = lax.broadcasted_iota(jnp.int32, (tm, LANES), 1)
        dzkr_ref[...] = jnp.where((lane >= MLA_NOPE) & (lane < MLA_QK), dkr, 0.0).astype(BF16)
        dcqn = _dot_nt(dql_ref[...], wuq_ref[...])
        dx, dgr = _rms_vjp(xq, rq, gcq_ref[...], dcqn, MLA_Q_RANK)
        dzcq_ref[...] = dx.astype(BF16)
        dgcq_ref[...] += jnp.sum(dgr, axis=0, keepdims=True)
        dckvn = _dot_nt(dkl_ref[...], wuk_ref[...]) + _dot_nt(dv_ref[...].astype(BF16), wuv_ref[...])
        dx, dgr = _rms_vjp(xk, rk, gckv_ref[...], dckvn, MLA_KV_RANK)
        dzckv_ref[...] = dx.astype(BF16)
        dgckv_ref[...] += jnp.sum(dgr, axis=0, keepdims=True)

    row = lambda n: pl.BlockSpec((tm, n), lambda i: (i, 0))
    full = lambda a: pl.BlockSpec(a.shape, lambda i: (0, 0))
    vec = lambda n: pl.BlockSpec((1, n), lambda i: (0, 0))
    return pl.pallas_call(
        body, name=name, grid=(t // tm,),
        in_specs=[row(MLA_Q_RANK), row(MLA_KV_RANK), row(LANES), full(gcq), full(gckv), full(qg), full(kg),
                  full(wuq), full(wuk), full(wuv), row(LANES), row(LANES), row(LANES), row(hd), row(hd), row(hd)],
        out_specs=[row(MLA_Q_RANK), row(MLA_KV_RANK), row(LANES), row(hd), row(hd),
                   vec(MLA_Q_RANK), vec(MLA_KV_RANK), vec(LANES), vec(LANES)],
        out_shape=[jax.ShapeDtypeStruct((t, MLA_Q_RANK), BF16), jax.ShapeDtypeStruct((t, MLA_KV_RANK), BF16),
                   jax.ShapeDtypeStruct((t, LANES), BF16), jax.ShapeDtypeStruct((t, hd), BF16),
                   jax.ShapeDtypeStruct((t, hd), BF16), jax.ShapeDtypeStruct((1, MLA_Q_RANK), F32),
                   jax.ShapeDtypeStruct((1, MLA_KV_RANK), F32), jax.ShapeDtypeStruct((1, LANES), F32),
                   jax.ShapeDtypeStruct((1, LANES), F32)],
        compiler_params=_cparams(),
    )(zcq, zckv, zkr, gcq, gckv, qg, kg, wuq, wuk, wuv, rc, rs1, rs2, dq, dk, dv)


def _attn_tiles(t):
    tq = 512 if t >= 2048 else 128
    return tq, min(t, 4 * tq), min(t, 2 * tq)


def _causal_keep(tq, tk, i, j):
    row = lax.broadcasted_iota(jnp.int32, (tq, tk), 0)
    col = lax.broadcasted_iota(jnp.int32, (tq, tk), 1)
    return (col - row) <= (i * tq - j * tk)


def _causal_keep_t(tq, tk, i, j):
    key = lax.broadcasted_iota(jnp.int32, (tk, tq), 0)
    qry = lax.broadcasted_iota(jnp.int32, (tk, tq), 1)
    return (key - qry) <= (i * tq - j * tk)


ATTN_FWD_HEADS_PER_STEP = 2
ATTN_BWD_HEADS_PER_STEP = 2


def _attn_fwd(q, k, v, name):
    t, hd = q.shape
    hp = ATTN_FWD_HEADS_PER_STEP
    tq, tk, _ = _attn_tiles(t)
    pairs = [(i, j) for i in range(t // tq) for j in range(((i + 1) * tq - 1) // tk + 1)]
    ii = np.array([p[0] for p in pairs], np.int32)
    jj = np.array([p[1] for p in pairs], np.int32)

    def body(ii_ref, jj_ref, q_ref, k_ref, v_ref, o_ref, lse_ref, m_scr, acc_scr):
        s_id = pl.program_id(1)
        i, j = ii_ref[s_id], jj_ref[s_id]
        last = j == ((i + 1) * tq - 1) // tk
        ones_lane = lax.broadcasted_iota(jnp.int32, (tq, LANES), 1) == V_ONES_LANE

        @pl.when(j == 0)
        def _():
            m_scr[...] = jnp.full_like(m_scr, NEG)
            acc_scr[...] = jnp.zeros_like(acc_scr)

        def step(masked):
            for hh in range(hp):
                sl = slice(hh * LANES, (hh + 1) * LANES)
                s = _dot_nt(q_ref[:, sl], k_ref[:, sl])
                if masked:
                    s = jnp.where(_causal_keep(tq, tk, i, j), s, NEG)
                m_prev = m_scr[hh]
                m_new = jnp.maximum(m_prev, jnp.max(s, axis=1, keepdims=True))
                p = jnp.exp2(s - m_new)
                alpha = jnp.exp2(m_prev - m_new)
                acc = alpha * acc_scr[:, sl] + _dot(p.astype(BF16), v_ref[:, sl])
                if masked:
                    l_new = jnp.sum(jnp.where(ones_lane, acc, 0.0), axis=1, keepdims=True)
                    o_ref[:, sl] = (acc / l_new).astype(BF16)
                    lse_ref[:, sl] = jnp.broadcast_to(m_new + jnp.log(l_new) * LOG2E, (tq, LANES))
                else:
                    acc_scr[:, sl] = acc
                    m_scr[hh] = m_new

        @pl.when(jnp.logical_not(last))
        def _():
            step(False)

        @pl.when(last)
        def _():
            step(True)

    w = hp * LANES
    qspec = pl.BlockSpec((tq, w), lambda h, s, ii_r, jj_r: (ii_r[s], h))
    kspec = pl.BlockSpec((tk, w), lambda h, s, ii_r, jj_r: (jj_r[s], h))
    return pl.pallas_call(
        body, name=name,
        grid_spec=pltpu.PrefetchScalarGridSpec(
            num_scalar_prefetch=2, grid=(hd // w, len(pairs)), in_specs=[qspec, kspec, kspec],
            out_specs=[qspec, qspec],
            scratch_shapes=[pltpu.VMEM((hp, tq, 1), F32), pltpu.VMEM((tq, w), F32)]),
        out_shape=[jax.ShapeDtypeStruct((t, hd), BF16), jax.ShapeDtypeStruct((t, hd), F32)],
        compiler_params=_cparams())(jnp.asarray(ii), jnp.asarray(jj), q, k, v)


def _attn_bwd_rows(o, lse, do, name):
    t, hd = o.shape
    heads = hd // LANES
    tm = _row_tile(t, 512)

    def body(o_ref, lse_ref, do_ref, out_ref):
        lane = lax.broadcasted_iota(jnp.int32, (tm, LANES), 1)
        acc = jnp.zeros((tm, LANES), F32)
        for h in range(heads):
            sl = slice(h * LANES, (h + 1) * LANES)
            delta = jnp.sum(do_ref[:, sl].astype(F32) * o_ref[:, sl].astype(F32), axis=1, keepdims=True)
            acc = jnp.where(lane == h, delta, acc)
            acc = jnp.where(lane == heads + h, lse_ref[:, sl], acc)
        out_ref[...] = acc

    row = pl.BlockSpec((tm, hd), lambda i: (i, 0))
    cols = pl.pallas_call(
        body, name=name, grid=(t // tm,), in_specs=[row, row, row],
        out_specs=pl.BlockSpec((tm, LANES), lambda i: (i, 0)),
        out_shape=jax.ShapeDtypeStruct((t, LANES), F32), compiler_params=_cparams())(o, lse, do)
    rows = cols.T
    return rows[:heads].reshape(heads, 1, t), rows[heads:2 * heads].reshape(heads, 1, t)


def _attn_bwd(q, k, v, delta_rows, lse_rows, do, name):
    t, hd = q.shape
    hp = ATTN_BWD_HEADS_PER_STEP
    tq, _, tk = _attn_tiles(t)
    nq = t // tq
    pairs = [(i, j) for j in range(t // tk) for i in range((j * tk) // tq, nq)]
    ii = np.array([p[0] for p in pairs], np.int32)
    jj = np.array([p[1] for p in pairs], np.int32)

    def body(jj_ref, ii_ref, q_ref, k_ref, v_ref, delta_ref, lse_ref, do_ref, dq_ref, dk_ref, dv_ref,
             dk_scr, dv_scr):
        s_id = pl.program_id(1)
        i, j = ii_ref[s_id], jj_ref[s_id]

        @pl.when(s_id == 0)
        def _():
            dq_ref[...] = jnp.zeros_like(dq_ref)

        @pl.when(i == (j * tk) // tq)
        def _():
            dk_scr[...] = jnp.zeros_like(dk_scr)
            dv_scr[...] = jnp.zeros_like(dv_scr)

        rows = pl.ds(pl.multiple_of(i * tq, tq), tq)

        def step(masked):
            for hh in range(hp):
                sl = slice(hh * LANES, (hh + 1) * LANES)
                qv, kv, dov = q_ref[:, sl], k_ref[:, sl], do_ref[:, sl]
                st = _dot_nt(kv, qv)
                if masked:
                    st = jnp.where(_causal_keep_t(tq, tk, i, j), st, NEG)
                pt = jnp.exp2(st - lse_ref[hh])
                dv_scr[:, sl] += _dot(pt.astype(BF16), dov)
                dpt = _dot_nt(v_ref[:, sl], dov)
                dst = (pt * (dpt - delta_ref[hh]) * ATTN_SCALE).astype(BF16)
                dk_scr[:, sl] += _dot(dst, qv)
                dq_ref[rows, sl] += _dot_tn(dst, kv)

        crosses = (j + 1) * tk - 1 > i * tq

        @pl.when(jnp.logical_not(crosses))
        def _():
            step(False)

        @pl.when(crosses)
        def _():
            step(True)

        @pl.when(i == nq - 1)
        def _():
            dk_ref[...] = dk_scr[...] * (1.0 / ATTN_SCALE2)
            dv_ref[...] = dv_scr[...]

    w = hp * LANES
    qspec = pl.BlockSpec((tq, w), lambda h, s, jj_r, ii_r: (ii_r[s], h))
    kspec = pl.BlockSpec((tk, w), lambda h, s, jj_r, ii_r: (jj_r[s], h))
    rspec = pl.BlockSpec((hp, 1, tq), lambda h, s, jj_r, ii_r: (h, 0, ii_r[s]))
    return pl.pallas_call(
        body, name=name,
        grid_spec=pltpu.PrefetchScalarGridSpec(
            num_scalar_prefetch=2, grid=(hd // w, len(pairs)),
            in_specs=[qspec, kspec, kspec, rspec, rspec, qspec],
            out_specs=[pl.BlockSpec((t, w), lambda h, s, jj_r, ii_r: (0, h)), kspec, kspec],
            scratch_shapes=[pltpu.VMEM((tk, w), F32), pltpu.VMEM((tk, w), F32)]),
        out_shape=[jax.ShapeDtypeStruct((t, hd), F32)] * 3,
        compiler_params=_cparams())(jnp.asarray(jj), jnp.asarray(ii), q, k, v, delta_rows, lse_rows, do)


MEM_W = MEM_HEADS * LANES


def _mem_kv_fwd(mem, gmem, wkv, kg, name):
    m, d = mem.shape

    def body(mem_ref, g_ref, w_ref, kg_ref, k_ref, v_ref, mn_ref):
        xv = mem_ref[...]
        mn = (xv * _rstd(xv, d) * g_ref[...]).astype(BF16)
        mn_ref[...] = mn
        kvm = _dot(mn, w_ref[...])
        v_ref[...] = kvm[:, MEM_W:].astype(BF16)
        for h in range(MEM_HEADS):
            sl = slice(h * LANES, (h + 1) * LANES)
            kh = kvm[:, sl]
            k_ref[:, sl] = (kh * _rstd(kh, LANES) * kg_ref[...]).astype(BF16)

    full = lambda a: pl.BlockSpec(a.shape, lambda i: (0, 0))
    return pl.pallas_call(
        body, name=name, grid=(1,), in_specs=[full(mem), full(gmem), full(wkv), full(kg)],
        out_specs=[pl.BlockSpec((m, MEM_W), lambda i: (0, 0)), pl.BlockSpec((m, MEM_W), lambda i: (0, 0)),
                   pl.BlockSpec((m, d), lambda i: (0, 0))],
        out_shape=[jax.ShapeDtypeStruct((m, MEM_W), BF16), jax.ShapeDtypeStruct((m, MEM_W), BF16),
                   jax.ShapeDtypeStruct((m, d), BF16)],
        compiler_params=_cparams())(mem, gmem, wkv, kg)


def _mem_softmax(qn, kh):
    s = _dot_nt(qn, kh) * (LANES ** -0.5)
    e = jnp.exp(s - jnp.max(s, axis=1, keepdims=True))
    return e / jnp.sum(e, axis=1, keepdims=True)


def _mem_attn_fwd(zqm, qg, km, vm, name):
    t = zqm.shape[0]
    tm = _row_tile(t, 512)

    def body(q_ref, qg_ref, k_ref, v_ref, o_ref):
        for h in range(MEM_HEADS):
            sl = slice(h * LANES, (h + 1) * LANES)
            qh = q_ref[:, sl]
            qn = (qh * _rstd(qh, LANES) * qg_ref[...]).astype(BF16)
            p = _mem_softmax(qn, k_ref[:, sl])
            o_ref[:, sl] = _dot(p.astype(BF16), v_ref[:, sl]).astype(BF16)

    row = pl.BlockSpec((tm, MEM_W), lambda i: (i, 0))
    full = lambda a: pl.BlockSpec(a.shape, lambda i: (0, 0))
    return pl.pallas_call(
        body, name=name, grid=(t // tm,), in_specs=[row, full(qg), full(km), full(vm)], out_specs=row,
        out_shape=jax.ShapeDtypeStruct((t, MEM_W), BF16), compiler_params=_cparams())(zqm, qg, km, vm)


def _mem_attn_bwd(zqm, dyc, qg, km, vm, name):
    t = zqm.shape[0]
    m = km.shape[0]
    tm = _row_tile(t, 256)

    def body(q_ref, dy_ref, qg_ref, k_ref, v_ref, dz_ref, dk_ref, dv_ref, dqg_ref):
        @pl.when(pl.program_id(0) == 0)
        def _():
            dk_ref[...] = jnp.zeros_like(dk_ref)
            dv_ref[...] = jnp.zeros_like(dv_ref)
            dqg_ref[...] = jnp.zeros_like(dqg_ref)

        qgv = qg_ref[...]
        dqg_acc = jnp.zeros((tm, LANES), F32)
        for h in range(MEM_HEADS):
            sl = slice(h * LANES, (h + 1) * LANES)
            qh = q_ref[:, sl]
            r = _rstd(qh, LANES)
            qn = (qh * r * qgv).astype(BF16)
            kh = k_ref[:, sl]
            p = _mem_softmax(qn, kh)
            dov = dy_ref[:, sl]
            dv_ref[:, sl] += _dot_tn(p.astype(BF16), dov)
            dp = _dot_nt(dov, v_ref[:, sl])
            ds = (p * (dp - jnp.sum(dp * p, axis=1, keepdims=True)) * (LANES ** -0.5)).astype(BF16)
            dk_ref[:, sl] += _dot_tn(ds, qn)
            dqh, dgr = _rms_vjp(qh, r, qgv, _dot(ds, kh), LANES)
            dz_ref[:, sl] = dqh.astype(BF16)
            dqg_acc += dgr
        dqg_ref[...] += jnp.sum(dqg_acc, axis=0, keepdims=True)

    row = pl.BlockSpec((tm, MEM_W), lambda i: (i, 0))
    full = lambda a: pl.BlockSpec(a.shape, lambda i: (0, 0))
    acc = pl.BlockSpec((m, MEM_W), lambda i: (0, 0))
    return pl.pallas_call(
        body, name=name, grid=(t // tm,), in_specs=[row, row, full(qg), full(km), full(vm)],
        out_specs=[row, acc, acc, pl.BlockSpec((1, LANES), lambda i: (0, 0))],
        out_shape=[jax.ShapeDtypeStruct((t, MEM_W), BF16), jax.ShapeDtypeStruct((m, MEM_W), F32),
                   jax.ShapeDtypeStruct((m, MEM_W), F32), jax.ShapeDtypeStruct((1, LANES), F32)],
        compiler_params=_cparams())(zqm, dyc, qg, km, vm)


def _mem_kv_bwd(mem, gmem, wkv, kg, dkn, dvm, name):
    m, d = mem.shape

    def body(mem_ref, g_ref, w_ref, kg_ref, dk_ref, dv_ref, dw_ref, dkg_ref, dg_ref, dkv_scr):
        xv = mem_ref[...]
        r = _rstd(xv, d)
        mn = (xv * r * g_ref[...]).astype(BF16)
        kvm = _dot(mn, w_ref[...])
        dkv_scr[:, MEM_W:] = dv_ref[...].astype(BF16)
        dkg_acc = jnp.zeros((m, LANES), F32)
        for h in range(MEM_HEADS):
            sl = slice(h * LANES, (h + 1) * LANES)
            kh = kvm[:, sl]
            dkh, dgr = _rms_vjp(kh, _rstd(kh, LANES), kg_ref[...], dk_ref[:, sl], LANES)
            dkv_scr[:, sl] = dkh.astype(BF16)
            dkg_acc += dgr
        dkg_ref[...] = jnp.sum(dkg_acc, axis=0, keepdims=True)
        dkv = dkv_scr[...]
        dw_ref[...] = _dot_tn(mn, dkv)
        dmn = _dot_nt(dkv, w_ref[...])
        dg_ref[...] = jnp.sum(dmn * xv * r, axis=0, keepdims=True)

    full = lambda a: pl.BlockSpec(a.shape, lambda i: (0, 0))
    return pl.pallas_call(
        body, name=name, grid=(1,),
        in_specs=[full(mem), full(gmem), full(wkv), full(kg), full(dkn), full(dvm)],
        out_specs=[pl.BlockSpec((d, 2 * MEM_W), lambda i: (0, 0)), pl.BlockSpec((1, LANES), lambda i: (0, 0)),
                   pl.BlockSpec((1, d), lambda i: (0, 0))],
        out_shape=[jax.ShapeDtypeStruct((d, 2 * MEM_W), F32), jax.ShapeDtypeStruct((1, LANES), F32),
                   jax.ShapeDtypeStruct((1, d), F32)],
        scratch_shapes=[pltpu.VMEM((m, 2 * MEM_W), BF16)],
        compiler_params=_cparams())(mem, gmem, wkv, kg, dkn, dvm)


def _merge_fwd(x1, ya, yb, yc, zg, bg, wa, wb, wc, wo, name):
    t, d = x1.shape
    tm = _row_tile(t, 256)

    def body(x_ref, ya_ref, yb_ref, yc_ref, zg_ref, bg_ref, wa_ref, wb_ref, wc_ref, wo_ref,
             x2_ref, mg_ref, pa_ref, pb_ref, pc_ref):
        merged = None
        for k, (y_ref, w_ref, p_ref) in enumerate(
                ((ya_ref, wa_ref, pa_ref), (yb_ref, wb_ref, pb_ref), (yc_ref, wc_ref, pc_ref))):
            sl = slice(k * d, (k + 1) * d)
            pr = _dot(y_ref[...], w_ref[...])
            p_ref[...] = pr.astype(BF16)
            term = jax.nn.sigmoid(zg_ref[:, sl] + bg_ref[:, sl]) * pr
            merged = term if merged is None else merged + term
        mb = merged.astype(BF16)
        mg_ref[...] = mb
        x2_ref[...] = x_ref[...] + _dot(mb, wo_ref[...])

    row = lambda n: pl.BlockSpec((tm, n), lambda i: (i, 0))
    full = lambda a: pl.BlockSpec(a.shape, lambda i: (0, 0))
    return pl.pallas_call(
        body, name=name, grid=(t // tm,),
        in_specs=[row(d), row(ya.shape[1]), row(yb.shape[1]), row(yc.shape[1]), row(3 * d), full(bg),
                  full(wa), full(wb), full(wc), full(wo)],
        out_specs=[row(d)] * 5,
        out_shape=[jax.ShapeDtypeStruct((t, d), F32)] + [jax.ShapeDtypeStruct((t, d), BF16)] * 4,
        compiler_params=_cparams())(x1, ya, yb, yc, zg, bg, wa, wb, wc, wo)


def _merge_bwd(dx2, pa, pb, pc, zg, bg, wa, wb, wc, wo, name, ex=None):
    t, d = dx2.shape
    tm = _row_tile(t, 256)

    def body(dx_ref, pa_ref, pb_ref, pc_ref, zg_ref, bg_ref, wa_ref, wb_ref, wc_ref, wo_ref,
             dpa_ref, dpb_ref, dpc_ref, dzg_ref, dbg_ref, dya_ref, dyb_ref, dyc_ref):
        @pl.when(pl.program_id(0) == 0)
        def _():
            dbg_ref[...] = jnp.zeros_like(dbg_ref)

        dm = _dot_nt(dx_ref[...].astype(BF16), wo_ref[...])
        for k, (p_ref, w_ref, dp_ref, dy_ref) in enumerate(
                ((pa_ref, wa_ref, dpa_ref, dya_ref), (pb_ref, wb_ref, dpb_ref, dyb_ref),
                 (pc_ref, wc_ref, dpc_ref, dyc_ref))):
            sl = slice(k * d, (k + 1) * d)
            gate = jax.nn.sigmoid(zg_ref[:, sl] + bg_ref[:, sl])
            dpr = (dm * gate).astype(BF16)
            dp_ref[...] = dpr
            dzg = dm * p_ref[...].astype(F32) * gate * (1.0 - gate)
            dzg_ref[:, sl] = dzg.astype(BF16)
            dbg_ref[:, sl] += jnp.sum(dzg, axis=0, keepdims=True)
            dy_ref[...] = _dot_nt(dpr, w_ref[...]).astype(dy_ref.dtype)

    row = lambda n: pl.BlockSpec((tm, n), lambda i: (i, 0))
    full = lambda a: pl.BlockSpec(a.shape, lambda i: (0, 0))
    na, nb, nc = wa.shape[0], wb.shape[0], wc.shape[0]
    return _call_with_exchange(
        ex, body, name, (t // tm,),
        [row(d), row(d), row(d), row(d), row(3 * d), full(bg), full(wa), full(wb), full(wc), full(wo)],
        [row(d), row(d), row(d), row(3 * d), pl.BlockSpec((1, 3 * d), lambda i: (0, 0)), row(na), row(nb), row(nc)],
        [jax.ShapeDtypeStruct((t, d), BF16)] * 3
        + [jax.ShapeDtypeStruct((t, 3 * d), BF16), jax.ShapeDtypeStruct((1, 3 * d), F32),
           jax.ShapeDtypeStruct((t, na), F32), jax.ShapeDtypeStruct((t, nb), BF16),
           jax.ShapeDtypeStruct((t, nc), BF16)],
        [], (dx2, pa, pb, pc, zg, bg, wa, wb, wc, wo))


def _adamw_math(w, g, m, v):
    bc1 = 1.0 - ADAM_B1 ** ADAM_STEP
    bc2 = 1.0 - ADAM_B2 ** ADAM_STEP
    nm = ADAM_B1 * m + (1.0 - ADAM_B1) * g
    nv = ADAM_B2 * v + (1.0 - ADAM_B2) * (g * g)
    delta = -ADAM_LR * ((nm / bc1) / (jnp.sqrt(nv / bc2) + ADAM_EPS) + ADAM_WD * w)
    return delta, nm, nv


def _div_tile(n, cap, mult):
    best = None
    for cand in range(mult, min(n, cap) + 1, mult):
        if n % cand == 0:
            best = cand
    assert best is not None, (n, cap, mult)
    return best


def _adamw(w, g, m, v, name):
    rows, cols = w.shape
    tr = rows if rows * cols <= 256 * 1024 else _div_tile(rows, 256, 8)

    def body(w_ref, g_ref, m_ref, v_ref, d_ref, nm_ref, nv_ref):
        d_ref[...], nm_ref[...], nv_ref[...] = _adamw_math(w_ref[...], g_ref[...], m_ref[...], v_ref[...])

    blk = pl.BlockSpec((tr, cols), lambda i: (i, 0))
    return pl.pallas_call(
        body, name=name, grid=(rows // tr,), in_specs=[blk] * 4, out_specs=[blk] * 3,
        out_shape=[jax.ShapeDtypeStruct((rows, cols), F32)] * 3, compiler_params=_cparams())(w, g, m, v)


def _adamw_slots(w, slots, m, v, name):
    _, hr, cols = w.shape
    tr = _div_tile(hr, 128, 16)

    def body(w_ref, s_ref, m_ref, v_ref, g_ref, d_ref, nm_ref, nv_ref):
        g = s_ref[0, 0].astype(F32)
        for k in range(1, N_CHIPS):
            g = g + s_ref[0, k].astype(F32)
        g_ref[0] = g
        d_ref[0], nm_ref[0], nv_ref[0] = _adamw_math(w_ref[0], g, m_ref[0], v_ref[0])

    blk = pl.BlockSpec((1, tr, cols), lambda h, i: (h, i, 0))
    return pl.pallas_call(
        body, name=name, grid=(2, hr // tr),
        in_specs=[blk, pl.BlockSpec((1, N_CHIPS, tr, cols), lambda h, i: (h, 0, i, 0)), blk, blk],
        out_specs=[blk] * 4, out_shape=[jax.ShapeDtypeStruct((2, hr, cols), F32)] * 4,
        compiler_params=_cparams())(w, slots, m, v)


ANY = pl.BlockSpec(memory_space=pl.ANY)


def _place():
    x, y, c = lax.axis_index("x"), lax.axis_index("y"), lax.axis_index("c")
    other_chips = [(1 - x, y), (x, 1 - y), (1 - x, 1 - y)]
    return x, y, c, other_chips


def _remote(src, dst, send_sem, recv_sem, to):
    return pltpu.make_async_remote_copy(src_ref=src, dst_ref=dst, send_sem=send_sem, recv_sem=recv_sem,
                                        device_id=to, device_id_type=MESH)


PIECE_BYTES = 384 * 1024


def _row_pieces(half_rows, cols):
    for n in (4, 2):
        if half_rows % (16 * n) == 0 and half_rows * cols * 2 // n >= PIECE_BYTES:
            return [pl.ds(k * (half_rows // n), half_rows // n) for k in range(n)]
    return [pl.ds(0, half_rows)]


def _pieces(arrays, rows_axis):
    return [(w, rows) for w, a in enumerate(arrays) for rows in _row_pieces(a.shape[rows_axis], a.shape[-1])]


def _gather_exchange(shards):
    nw = len(shards)
    pieces = _pieces(shards, 1)
    npc = len(pieces)

    def build(s_refs, g_refs, sems):
        send_sems, recv_sems, local_sems = sems
        x, y, c, chips = _place()
        me = 2 * x + y
        sibling = (x, y, 1 - c)
        mine = [pltpu.make_async_copy(s_refs[w], g_refs[w].at[me], local_sems.at[w]) for w in range(nw)]
        first = [_remote(s_refs[w].at[c, rows], g_refs[w].at[me, c, rows], send_sems.at[k, p], recv_sems.at[k, p],
                         (cx, cy, c)) for k, (cx, cy) in enumerate(chips) for p, (w, rows) in enumerate(pieces)]

        def start():
            for cp in mine + first:
                cp.start()

        def finish():
            passed = []
            for k, (cx, cy) in enumerate(chips):
                for p, (w, rows) in enumerate(pieces):
                    slab = g_refs[w].at[2 * cx + cy, c, rows]
                    _remote(slab, slab, send_sems.at[k, p], recv_sems.at[k, p], (cx, cy, c)).wait_recv()
                    fwd = _remote(slab, slab, send_sems.at[3 + k, p], recv_sems.at[3 + k, p], sibling)
                    fwd.start()
                    passed.append(fwd)
            for k, (cx, cy) in enumerate(chips):
                for p, (w, rows) in enumerate(pieces):
                    slab = g_refs[w].at[2 * cx + cy, 1 - c, rows]
                    _remote(slab, slab, send_sems.at[3 + k, p], recv_sems.at[3 + k, p], sibling).wait_recv()
            for cp in first + passed:
                cp.wait_send()
            for cp in mine:
                cp.wait()

        return start, finish

    return _Exchange(list(shards), [jax.ShapeDtypeStruct((N_CHIPS,) + s.shape, BF16) for s in shards],
                     [pltpu.SemaphoreType.DMA((6, npc)), pltpu.SemaphoreType.DMA((6, npc)),
                      pltpu.SemaphoreType.DMA((nw,))], build)


def _swap_halves(grads, name):
    nw = len(grads)

    def body(*refs):
        g_refs, sib_refs = refs[:nw], refs[nw:2 * nw]
        send_sems, recv_sems = refs[2 * nw:]
        x, y, c, _ = _place()
        copies = [_remote(g_refs[w].at[s, 1 - c], sib_refs[w].at[s], send_sems.at[s, w], recv_sems.at[s, w],
                          (x, y, 1 - c)) for w in range(nw) for s in range(N_CHIPS)]
        for cp in copies:
            cp.start()
        for cp in copies:
            cp.wait_recv()
        for cp in copies:
            cp.wait_send()

    return pl.pallas_call(
        body, name=name, in_specs=[ANY] * nw, out_specs=[ANY] * nw,
        out_shape=[jax.ShapeDtypeStruct((N_CHIPS,) + g.shape[2:], BF16) for g in grads],
        scratch_shapes=[pltpu.SemaphoreType.DMA((N_CHIPS, nw)), pltpu.SemaphoreType.DMA((N_CHIPS, nw))],
    )(*grads)


def _pair_sum(grad, sib, core, name):
    nchip, _, hr, cols = grad.shape
    tr = _div_tile(hr, 256, 16)

    def body(core_ref, a_ref, b_ref, o_ref):
        o_ref[...] = (a_ref[0].astype(F32) + b_ref[...].astype(F32)).astype(BF16)

    return pl.pallas_call(
        body, name=name,
        grid_spec=pltpu.PrefetchScalarGridSpec(
            num_scalar_prefetch=1, grid=(nchip, hr // tr),
            in_specs=[pl.BlockSpec((1, 1, tr, cols), lambda s, i, core_r: (s, core_r[0], i, 0)),
                      pl.BlockSpec((1, tr, cols), lambda s, i, core_r: (s, i, 0))],
            out_specs=pl.BlockSpec((1, tr, cols), lambda s, i, core_r: (s, i, 0))),
        out_shape=jax.ShapeDtypeStruct((nchip, hr, cols), BF16), compiler_params=_cparams())(core, grad, sib)


def _pair_sum_exchange(sums):
    nw = len(sums)
    pieces = _pieces(sums, 1)
    npc = len(pieces)

    def build(p_refs, o_refs, sems):
        send_sems, recv_sems, local_sems = sems
        x, y, c, chips = _place()
        me = 2 * x + y
        sibling = (x, y, 1 - c)
        mine = [pltpu.make_async_copy(p_refs[w].at[me], o_refs[w].at[c, 3], local_sems.at[w]) for w in range(nw)]
        first = [_remote(p_refs[w].at[2 * cx + cy, rows], o_refs[w].at[c, k, rows], send_sems.at[k, p],
                         recv_sems.at[k, p], (cx, cy, c))
                 for k, (cx, cy) in enumerate(chips) for p, (w, rows) in enumerate(pieces)]

        def start():
            for cp in mine + first:
                cp.start()

        def finish():
            passed = []
            for k in range(N_CHIPS):
                own_waited = set()
                for p, (w, rows) in enumerate(pieces):
                    slab = o_refs[w].at[c, k, rows]
                    if k < 3:
                        first[k * npc + p].wait_recv()
                    elif w not in own_waited:
                        mine[w].wait()
                        own_waited.add(w)
                    fwd = _remote(slab, slab, send_sems.at[3 + k, p], recv_sems.at[3 + k, p], sibling)
                    fwd.start()
                    passed.append(fwd)
            for k in range(N_CHIPS):
                for p, (w, rows) in enumerate(pieces):
                    slab = o_refs[w].at[1 - c, k, rows]
                    _remote(slab, slab, send_sems.at[3 + k, p], recv_sems.at[3 + k, p], sibling).wait_recv()
            for cp in first + passed:
                cp.wait_send()

        return start, finish

    return _Exchange(list(sums), [jax.ShapeDtypeStruct((2,) + p.shape, BF16) for p in sums],
                     [pltpu.SemaphoreType.DMA((7, npc)), pltpu.SemaphoreType.DMA((7, npc)),
                      pltpu.SemaphoreType.DMA((nw,))], build)


def _allreduce_small(vec):
    m_per, n = vec.shape

    def body(x_ref, out_ref, gath_ref, send_sems, recv_sems, local_sem):
        x, y, c, chips = _place()
        me, sibling = (x, y, c), (x, y, 1 - c)

        def rows(px, py, pc):
            return gath_ref.at[pl.ds((4 * px + 2 * py + pc) * m_per, m_per), :]

        def copy(k, block, to, src=None):
            return pltpu.make_async_remote_copy(
                src_ref=rows(*block) if src is None else src, dst_ref=rows(*block),
                send_sem=send_sems.at[k], recv_sem=recv_sems.at[k], device_id=to, device_id_type=MESH)

        mine = pltpu.make_async_copy(x_ref, rows(*me), local_sem)
        mine.start()
        first = [copy(0, me, sibling, src=x_ref)]
        first += [copy(1 + j, me, (*chip, c), src=x_ref) for j, chip in enumerate(chips)]
        for cp in first:
            cp.start()
        passed = [copy(4 + j, (*chip, c), sibling) for j, chip in enumerate(chips)]
        for j, chip in enumerate(chips):
            copy(1 + j, (*chip, c), me).wait_recv()
            passed[j].start()
        copy(0, sibling, me).wait_recv()
        for j, chip in enumerate(chips):
            copy(4 + j, (*chip, 1 - c), me).wait_recv()
        for cp in first + passed:
            cp.wait_send()
        mine.wait()
        acc = gath_ref[pl.ds(0, m_per), :]
        for k in range(1, N_DEV):
            acc = acc + gath_ref[pl.ds(k * m_per, m_per), :]
        out_ref[...] = acc

    vm = pl.BlockSpec(memory_space=pltpu.VMEM)
    return pl.pallas_call(
        body, name="allreduce_small", in_specs=[vm], out_specs=vm,
        out_shape=jax.ShapeDtypeStruct((m_per, n), F32),
        scratch_shapes=[pltpu.VMEM((N_DEV * m_per, n), F32), pltpu.SemaphoreType.DMA((7,)),
                        pltpu.SemaphoreType.DMA((7,)), pltpu.SemaphoreType.DMA],
    )(vec)


def _pack_small(vals, tail=()):
    flat = jnp.concatenate([vals[name].reshape(-1).astype(F32) for name, _ in SMALL] + [v.reshape(1) for v in tail])
    flat = jnp.pad(flat, (0, SMALL_ROWS * LANES - flat.shape[0]))
    return flat.reshape(SMALL_ROWS, LANES)


def _unpack_small(packed):
    flat = packed.reshape(-1)
    out, off = {}, 0
    for name, shape in SMALL:
        n = int(np.prod(shape))
        out[name] = flat[off:off + n].reshape(shape)
        off += n
    return out


def _head_pad_cols(w, heads, real):
    k = w.shape[0]
    return jnp.pad(w.reshape(k, heads, real), ((0, 0), (0, 0), (0, LANES - real))).reshape(k, heads * LANES)


def _rope_tables(positions):
    half = MLA_ROPE // 2
    inv = ROPE_BASE ** (-jnp.arange(half, dtype=F32) / half)
    ang = positions.astype(F32)[:, None] * inv
    cos, sin = jnp.cos(ang), jnp.sin(ang)
    t = positions.shape[0]
    z = lambda n: jnp.zeros((t, n), F32)
    rc = jnp.concatenate([jnp.ones((t, MLA_NOPE), F32), cos, cos, z(LANES - MLA_QK)], axis=1)
    rs1 = jnp.concatenate([z(MLA_NOPE), -sin, z(LANES - MLA_NOPE - half)], axis=1)
    rs2 = jnp.concatenate([z(MLA_NOPE + half), sin, z(LANES - MLA_QK)], axis=1)
    return rc, rs1, rs2


FFN1_WEIGHTS = ("ffn1_w_gu", "ffn1_w_down")
FFN2_WEIGHTS = ("ffn2_w_gu", "ffn2_w_down")
MIXER_WEIGHTS = tuple(n for n, *_ in SHARDED if n not in FFN1_WEIGHTS + FFN2_WEIGHTS)
SHARD_SHAPE = {n: (r, c, kind) for n, r, c, kind in SHARDED}


def _from_blocks(name, gathered):
    r, c, kind = SHARD_SHAPE[name]
    blk = gathered.reshape(N_CHIPS, r, c)
    return blk, (blk.transpose(1, 0, 2).reshape(r, N_CHIPS * c) if kind == "col" else blk.reshape(N_CHIPS * r, c))


def _grad_pair_sums(names, gw, core, tag):
    by_owner = []
    for name in names:
        r, c, kind = SHARD_SHAPE[name]
        blk = gw[name].reshape(r, N_CHIPS, c).transpose(1, 0, 2) if kind == "col" else gw[name].reshape(N_CHIPS, r, c)
        by_owner.append(blk.astype(BF16).reshape(N_CHIPS, 2, r // 2, c))
    received = _swap_halves(by_owner, "grad_swap_" + tag)
    return [_pair_sum(g, s, core, "pair_sum_" + n) for g, s, n in zip(by_owner, received, names)]


def _device_step(x, mem, positions, tgt, small, shards, core):
    d = D_MODEL
    g_ffn1, g_mix, g_ffn2 = small["ffn1_norm"], small["mix_norm"], small["ffn2_norm"]
    big = {}
    for name, g in zip(FFN1_WEIGHTS, _run_exchange(_gather_exchange([shards[n] for n in FFN1_WEIGHTS]), "gather_ffn1")):
        big[name + "#blocks"], big[name] = _from_blocks(name, g)
    wgu1, wd1 = big["ffn1_w_gu#blocks"], big["ffn1_w_down"].reshape(2, FF_TILE, d)
    x1, gpre1, upre1, h, *rest = _ffn_fwd(x, g_ffn1, wgu1, wd1, "ffn1_fwd", next_gain=g_mix,
                                          ex=_gather_exchange([shards[n] for n in MIXER_WEIGHTS]))
    for name, g in zip(MIXER_WEIGHTS, rest):
        big[name + "#blocks"], big[name] = _from_blocks(name, g)
    w_in = big["w_in"]
    w_uv_, w_cq, w_ckv = w_in[:, :COL_CQ], w_in[:, COL_CQ:COL_CKV], w_in[:, COL_CKV:COL_KR]
    w_kr = jnp.pad(w_in[:, COL_KR:COL_QM], ((0, 0), (MLA_NOPE, LANES - MLA_QK)))
    w_qm, w_g = w_in[:, COL_QM:COL_GATE], w_in[:, COL_GATE:]
    segs = (w_uv_, w_cq, w_ckv, w_kr, w_qm, w_g)
    wuq = _head_pad_cols(big["mla_w_uq"], MLA_HEADS, MLA_QK)
    ukv = big["mla_w_ukv"].reshape(MLA_KV_RANK, MLA_HEADS, 2, MLA_NOPE)
    wuk = _head_pad_cols(ukv[:, :, 0].reshape(MLA_KV_RANK, -1), MLA_HEADS, MLA_NOPE)
    wuv = _head_pad_cols(ukv[:, :, 1].reshape(MLA_KV_RANK, -1), MLA_HEADS, MLA_NOPE)
    wkv = big["mem_w_kv"]
    wa, wc, wo = big["w_branch_a"], big["w_branch_c"], big["w_out"]
    wb = jnp.pad(big["w_branch_b"].reshape(MLA_HEADS, MLA_NOPE, d),
                 ((0, 0), (0, LANES - MLA_NOPE), (0, 0))).reshape(MLA_HEADS * LANES, d)
    qg = jnp.pad(small["mla_q_norm"], ((0, 0), (0, LANES - MLA_QK)))
    kg = jnp.pad(small["mla_k_norm"], ((0, 0), (0, LANES - MLA_QK)))
    causal = jnp.tril(jnp.ones((CHUNK, CHUNK), bool))
    wt_f = jnp.where(causal[None], small["sg_w"][0], 0.0)
    wt, wt_t = wt_f.astype(BF16), wt_f.transpose(0, 2, 1).astype(BF16)
    bias_l = jnp.repeat(small["sg_b"][0].T, 64, axis=1)
    rc, rs1, rs2 = _rope_tables(positions)

    zuv, zcq, zckv, zkr, zqm, zg, *rest = _mm_cols(h, segs, [F32] * 5 + [BF16], "in_proj",
                                                   ex=_gather_exchange([shards[n] for n in FFN2_WEIGHTS]))
    for name, g in zip(FFN2_WEIGHTS, rest):
        big[name + "#blocks"], big[name] = _from_blocks(name, g)
    wgu2, wd2 = big["ffn2_w_gu#blocks"], big["ffn2_w_down"].reshape(2, FF_TILE, d)
    ya = _sgu_fwd(zuv, small["sg_ln_g"], small["sg_ln_b"], wt, bias_l, "sgu_fwd")
    q, k, v, cqn, ckvn = _mla_prep_fwd(zcq, zckv, zkr, small["mla_cq_norm"], small["mla_ckv_norm"], qg, kg,
                                       wuq, wuk, wuv, rc, rs1, rs2, "mla_prep_fwd")
    yb, lse = _attn_fwd(q, k, v, "mla_attn_fwd")
    km, vm, memn = _mem_kv_fwd(mem, small["mem_norm"], wkv, small["mem_k_norm"], "mem_kv_fwd")
    yc = _mem_attn_fwd(zqm, small["mem_q_norm"], km, vm, "mem_attn_fwd")
    x2, merged, pa, pb, pc = _merge_fwd(x1, ya, yb, yc, zg, small["b_gate"], wa, wb, wc, wo, "merge_fwd")
    dy, loss_row, gpre2, upre2 = _ffn_fwd(x2, g_ffn2, wgu2, wd2, "ffn2_fwd", target=tgt)

    gw, gs, slots = {}, {}, {}

    def ffn_grads(prefix, xin, gain, dyin, gpre, upre, wgu, wd, ex=None, ex_names=()):
        dx, dgain, xn, dgt, dup, act, *got = _ffn_bwd(xin, gain, dyin, gpre, upre, wgu, wd, prefix + "_bwd", ex=ex)
        slots.update(zip(ex_names, got))
        gw[prefix + "_w_gu"] = jnp.concatenate(
            [_mm_tn(xn, dgt, prefix + "_dwg"), _mm_tn(xn, dup, prefix + "_dwu")], axis=1)
        gw[prefix + "_w_down"] = _mm_tn(act, dyin, prefix + "_dwd", scale=0.5)
        gs[prefix + "_norm"] = dgain
        return dx

    dx2 = ffn_grads("ffn2", x2, g_ffn2, dy, gpre2, upre2, wgu2, wd2)
    ffn2_sums = _pair_sum_exchange(_grad_pair_sums(FFN2_WEIGHTS, gw, core, "ffn2"))
    dpa, dpb, dpc, dzg, dbg, dya, dyb, dyc, *got = _merge_bwd(dx2, pa, pb, pc, zg, small["b_gate"], wa, wb, wc, wo,
                                                              "merge_bwd", ex=ffn2_sums)
    slots.update(zip(FFN2_WEIGHTS, got))
    gs["b_gate"] = dbg
    gw["w_out"] = _mm_tn(merged, dx2, "dw_out")
    gw["w_branch_a"] = _mm_tn(ya, dpa, "dw_branch_a")
    gw["w_branch_b"] = _mm_tn(yb, dpb, "dw_branch_b").reshape(MLA_HEADS, LANES, d)[:, :MLA_NOPE].reshape(-1, d)
    gw["w_branch_c"] = _mm_tn(yc, dpc, "dw_branch_c")

    dzuv, dwt, dbl, dlg, dlb = _sgu_bwd(zuv, dya, small["sg_ln_g"], small["sg_ln_b"], wt, wt_t, bias_l, "sgu_bwd")
    gs["sg_w"], gs["sg_b"] = dwt[None], dbl[:, :SG_GROUPS].T[None]
    gs["sg_ln_g"], gs["sg_ln_b"] = dlg, dlb

    delta_rows, lse_rows = _attn_bwd_rows(yb, lse, dyb, "mla_attn_bwd_rows")
    dq, dk, dv = _attn_bwd(q, k, v, delta_rows, lse_rows, dyb, "mla_attn_bwd")
    dzcq, dzckv, dzkr, dql, dkl, dgcq, dgckv, dqg, dkg = _mla_prep_bwd(
        zcq, zckv, zkr, small["mla_cq_norm"], small["mla_ckv_norm"], qg, kg, wuq, wuk, wuv, rc, rs1, rs2,
        dq, dk, dv, "mla_prep_bwd")
    gs["mla_cq_norm"], gs["mla_ckv_norm"] = dgcq, dgckv
    gs["mla_q_norm"], gs["mla_k_norm"] = dqg[:, :MLA_QK], dkg[:, :MLA_QK]
    gw["mla_w_uq"] = _mm_tn(cqn, dql, "dw_uq").reshape(MLA_Q_RANK, MLA_HEADS, LANES)[:, :, :MLA_QK].reshape(
        MLA_Q_RANK, -1)
    dwuk = _mm_tn(ckvn, dkl, "dw_uk").reshape(MLA_KV_RANK, MLA_HEADS, LANES)[:, :, :MLA_NOPE]
    dwuv = _mm_tn(ckvn, dv, "dw_uv").reshape(MLA_KV_RANK, MLA_HEADS, LANES)[:, :, :MLA_NOPE]
    gw["mla_w_ukv"] = jnp.concatenate([dwuk, dwuv], axis=2).reshape(MLA_KV_RANK, -1)

    dzqm, dkn, dvm, dmqg = _mem_attn_bwd(zqm, dyc, small["mem_q_norm"], km, vm, "mem_attn_bwd")
    gs["mem_q_norm"] = dmqg
    gw["mem_w_kv"], gs["mem_k_norm"], gs["mem_norm"] = _mem_kv_bwd(
        mem, small["mem_norm"], wkv, small["mem_k_norm"], dkn, dvm, "mem_kv_bwd")

    dzs = (dzuv, dzcq, dzckv, dzkr, dzqm, dzg)
    dws = list(_mm_tn_cols(h, dzs[:5], "dw_in_narrow")) + [_mm_tn(h, dzg, "dw_in_gate")]
    dws[3] = dws[3][:, MLA_NOPE:MLA_QK]
    gw["w_in"] = jnp.concatenate(dws, axis=1)
    dx1, gs["mix_norm"] = _proj_norm_bwd(dzs, [w.T for w in segs], x1, g_mix, dx2, "in_proj_bwd")
    mixer_sums = _pair_sum_exchange(_grad_pair_sums(MIXER_WEIGHTS, gw, core, "mixer"))
    dx = ffn_grads("ffn1", x, g_ffn1, dx1, gpre1, upre1, wgu1, wd1, ex=mixer_sums, ex_names=MIXER_WEIGHTS)
    ffn1_sums = _pair_sum_exchange(_grad_pair_sums(FFN1_WEIGHTS, gw, core, "ffn1"))
    slots.update(zip(FFN1_WEIGHTS, _run_exchange(ffn1_sums, "grad_exchange_ffn1")))
    return loss_row, dx, slots, gs


def kernel(x, mem, positions, ffn1_norm, ffn1_w_gu, ffn1_w_down, mix_norm, w_in, b_gate, sg_ln_g, sg_ln_b, sg_w, sg_b, mla_cq_norm, mla_w_uq, mla_ckv_norm, mla_w_ukv, mla_q_norm, mla_k_norm, mem_norm, mem_w_kv, mem_q_norm, mem_k_norm, w_branch_a, w_branch_b, w_branch_c, w_out, ffn2_norm, ffn2_w_gu, ffn2_w_down, loss_target, m_ffn1_norm, m_ffn1_w_gu, m_ffn1_w_down, m_mix_norm, m_w_in, m_b_gate, m_sg_ln_g, m_sg_ln_b, m_sg_w, m_sg_b, m_mla_cq_norm, m_mla_w_uq, m_mla_ckv_norm, m_mla_w_ukv, m_mla_q_norm, m_mla_k_norm, m_mem_norm, m_mem_w_kv, m_mem_q_norm, m_mem_k_norm, m_w_branch_a, m_w_branch_b, m_w_branch_c, m_w_out, m_ffn2_norm, m_ffn2_w_gu, m_ffn2_w_down, v_ffn1_norm, v_ffn1_w_gu, v_ffn1_w_down, v_mix_norm, v_w_in, v_b_gate, v_sg_ln_g, v_sg_ln_b, v_sg_w, v_sg_b, v_mla_cq_norm, v_mla_w_uq, v_mla_ckv_norm, v_mla_w_ukv, v_mla_q_norm, v_mla_k_norm, v_mem_norm, v_mem_w_kv, v_mem_q_norm, v_mem_k_norm, v_w_branch_a, v_w_branch_b, v_w_branch_c, v_w_out, v_ffn2_norm, v_ffn2_w_gu, v_ffn2_w_down):
    args = dict(locals())
    weights = {n: args[n] for n in WEIGHT_ORDER}
    mom_m = {n: args["m_" + n] for n in WEIGHT_ORDER}
    mom_v = {n: args["v_" + n] for n in WEIGHT_ORDER}
    small = {n: weights[n] for n, _ in SMALL}
    halves = lambda a, r, c: a.reshape(2, r // 2, c)

    shards = {n: halves(weights[n][0].astype(BF16), r, c) for n, r, c, _ in SHARDED}
    core = lax.axis_index("c").astype(jnp.int32).reshape(1)
    loss_row, dx, slots, gs = _device_step(x[0], mem[0], positions[0], loss_target[0], small, shards, core)
    summed = _allreduce_small(_pack_small(gs, tail=[loss_row[0, 0]]))
    loss = summed.reshape(-1)[_N_SMALL]
    small_grads = _unpack_small(summed)

    grads, deltas, new_m, new_v = {}, {}, {}, {}
    for name, r, c, _ in SHARDED:
        outs = _adamw_slots(halves(weights[name][0], r, c), slots[name], halves(mom_m[name][0], r, c),
                            halves(mom_v[name][0], r, c), "adamw_" + name)
        shape = weights[name].shape
        grads[name], deltas[name], new_m[name], new_v[name] = [o.reshape(shape) for o in outs]
    dlt, nm, nv = _adamw(_pack_small(small), _pack_small(small_grads), _pack_small({n: mom_m[n] for n, _ in SMALL}),
                         _pack_small({n: mom_v[n] for n, _ in SMALL}), "adamw_small")
    for name, _ in SMALL:
        grads[name] = small_grads[name]
    deltas.update(_unpack_small(dlt))
    new_m.update(_unpack_small(nm))
    new_v.update(_unpack_small(nv))

    return (loss, dx[None], *[grads[n] for n in WEIGHT_ORDER], *[deltas[n] for n in WEIGHT_ORDER],
            *[new_m[n] for n in WEIGHT_ORDER], *[new_v[n] for n in WEIGHT_ORDER])
```

```python
import functools
from typing import Callable, NamedTuple

import numpy as np
import jax
import jax.numpy as jnp
from jax import lax
from jax.experimental import pallas as pl
from jax.experimental.pallas import tpu as pltpu

F32 = jnp.float32
BF16 = jnp.bfloat16

D_MODEL = 1024
D_FF = 2816
FF_TILE = 1408
SG_WIDTH = 512
SG_GROUPS = 8
CHUNK = 128
MLA_HEADS = 8
MLA_QK = 96
MLA_NOPE = 64
MLA_ROPE = 32
MLA_Q_RANK = 384
MLA_KV_RANK = 256
MEM_HEADS = 4
MEM_LEN = 256
LANES = 128
EPS = 1e-6
NEG = -1e30
ROPE_BASE = 10000.0
N_CHIPS = 4
N_DEV = 8

ADAM_LR = 0.001
ADAM_B1 = 0.9
ADAM_B2 = 0.999
ADAM_EPS = 1e-08
ADAM_WD = 0.01
ADAM_STEP = 10

COL_V = 512
COL_CQ = 1024
COL_CKV = 1408
COL_KR = 1664
COL_QM = 1696
COL_GATE = 2208
IN_COLS = 5280

VMEM_LIMIT_BYTES = 56 * 1024 * 1024
INV_SQRT2 = 0.7071067811865476
INV_SQRT_2PI = 0.3989422804014327
LOG2E = 1.4426950408889634
ATTN_SCALE = MLA_QK ** -0.5
V_ONES_LANE = 64
ATTN_SCALE2 = ATTN_SCALE * LOG2E

SHARDED = (
    ("ffn1_w_gu", 1024, 1408, "col"),
    ("ffn1_w_down", 704, 1024, "row"),
    ("w_in", 1024, 1320, "col"),
    ("mla_w_uq", 384, 192, "col"),
    ("mla_w_ukv", 256, 256, "col"),
    ("mem_w_kv", 256, 1024, "row"),
    ("w_branch_a", 512, 256, "col"),
    ("w_branch_b", 512, 256, "col"),
    ("w_branch_c", 512, 256, "col"),
    ("w_out", 256, 1024, "row"),
    ("ffn2_w_gu", 1024, 1408, "col"),
    ("ffn2_w_down", 704, 1024, "row"),
)
SMALL = (
    ("ffn1_norm", (1, 1024)), ("mix_norm", (1, 1024)), ("b_gate", (1, 3072)),
    ("sg_ln_g", (1, 512)), ("sg_ln_b", (1, 512)), ("sg_w", (1, 8, 128, 128)),
    ("sg_b", (1, 8, 128)), ("mla_cq_norm", (1, 384)), ("mla_ckv_norm", (1, 256)),
    ("mla_q_norm", (1, 96)), ("mla_k_norm", (1, 96)), ("mem_norm", (1, 1024)),
    ("mem_q_norm", (1, 128)), ("mem_k_norm", (1, 128)), ("ffn2_norm", (1, 1024)),
)
WEIGHT_ORDER = (
    "ffn1_norm", "ffn1_w_gu", "ffn1_w_down", "mix_norm", "w_in", "b_gate", "sg_ln_g", "sg_ln_b",
    "sg_w", "sg_b", "mla_cq_norm", "mla_w_uq", "mla_ckv_norm", "mla_w_ukv", "mla_q_norm",
    "mla_k_norm", "mem_norm", "mem_w_kv", "mem_q_norm", "mem_k_norm", "w_branch_a", "w_branch_b",
    "w_branch_c", "w_out", "ffn2_norm", "ffn2_w_gu", "ffn2_w_down",
)

_N_SMALL = sum(int(np.prod(s)) for _, s in SMALL)
SMALL_ROWS = -(-_N_SMALL // (LANES * 8)) * 8

MESH = pl.DeviceIdType.MESH


def _cparams():
    return pltpu.CompilerParams(vmem_limit_bytes=VMEM_LIMIT_BYTES)


def _dot(a, b):
    return jnp.dot(a, b, preferred_element_type=F32)


def _dot_nt(a, b):
    return lax.dot_general(a, b, (((1,), (1,)), ((), ())), preferred_element_type=F32)


def _dot_tn(a, b):
    return lax.dot_general(a, b, (((0,), (0,)), ((), ())), preferred_element_type=F32)


def _gelu(x):
    return 0.5 * x * (1.0 + lax.erf(x * INV_SQRT2))


def _gelu_grad(x):
    return 0.5 * (1.0 + lax.erf(x * INV_SQRT2)) + x * jnp.exp(-0.5 * x * x) * INV_SQRT_2PI


def _rstd(x, n):
    return lax.rsqrt(jnp.sum(x * x, axis=-1, keepdims=True) * (1.0 / n) + EPS)


def _rms_vjp(x, r, g, dy, n):
    dxh = dy * g
    dx = r * dxh - x * (r * r * r) * (jnp.sum(dxh * x, axis=-1, keepdims=True) * (1.0 / n))
    return dx, dy * x * r


def _row_tile(t, want):
    return min(t, want)


def _wide_tile(n):
    if n <= 1024:
        return n
    if n % 1024 == 0:
        return 1024
    assert n % FF_TILE == 0, n
    return FF_TILE


def _mm_cols(a, ws, out_dtypes, name, ex=None):
    t, kdim = a.shape
    tm = _row_tile(t, 256)
    n = len(ws)

    def body(*refs):
        av = refs[0][...]
        for w_ref, o_ref in zip(refs[1:1 + n], refs[1 + n:]):
            o_ref[...] = _dot(av, w_ref[...]).astype(o_ref.dtype)

    row = lambda width: pl.BlockSpec((tm, width), lambda i: (i, 0))
    return _call_with_exchange(
        ex, body, name, (t // tm,),
        [row(kdim)] + [pl.BlockSpec(w.shape, lambda i: (0, 0)) for w in ws],
        [row(w.shape[1]) for w in ws],
        [jax.ShapeDtypeStruct((t, w.shape[1]), dt) for w, dt in zip(ws, out_dtypes)], [], (a, *ws))


def _proj_norm_bwd(dzs, wts, x, g, dres, name):
    t, d = x.shape
    tm = _row_tile(t, 256)
    n = len(dzs)

    def body(*refs):
        x_ref, g_ref, r_ref, dx_ref, dg_ref = refs[2 * n:]

        @pl.when(pl.program_id(0) == 0)
        def _():
            dg_ref[...] = jnp.zeros_like(dg_ref)

        dh = None
        for dz_ref, w_ref in zip(refs[:n], refs[n:2 * n]):
            part = _dot(dz_ref[...], w_ref[...])
            dh = part if dh is None else dh + part
        xv = x_ref[...]
        dx, dgr = _rms_vjp(xv, _rstd(xv, d), g_ref[...], dh, d)
        dx_ref[...] = r_ref[...] + dx
        dg_ref[...] += jnp.sum(dgr, axis=0, keepdims=True)

    row = lambda width: pl.BlockSpec((tm, width), lambda i: (i, 0))
    vec = pl.BlockSpec((1, d), lambda i: (0, 0))
    return pl.pallas_call(
        body, name=name, grid=(t // tm,),
        in_specs=[row(dz.shape[1]) for dz in dzs] + [pl.BlockSpec(w.shape, lambda i: (0, 0)) for w in wts]
        + [row(d), vec, row(d)],
        out_specs=[row(d), vec],
        out_shape=[jax.ShapeDtypeStruct((t, d), F32), jax.ShapeDtypeStruct((1, d), F32)],
        compiler_params=_cparams())(*dzs, *wts, x, g, dres)


def _mm_tn_cols(a, bs, name):
    t, m = a.shape
    tk = _row_tile(t, 512)
    n = len(bs)

    def body(*refs):
        @pl.when(pl.program_id(0) == 0)
        def _():
            for o_ref in refs[1 + n:]:
                o_ref[...] = jnp.zeros_like(o_ref)

        av = refs[0][...].astype(BF16)
        for b_ref, o_ref in zip(refs[1:1 + n], refs[1 + n:]):
            o_ref[...] += _dot_tn(av, b_ref[...].astype(BF16))

    row = lambda width: pl.BlockSpec((tk, width), lambda k: (k, 0))
    return pl.pallas_call(
        body, name=name, grid=(t // tk,), in_specs=[row(m)] + [row(b.shape[1]) for b in bs],
        out_specs=[pl.BlockSpec((m, b.shape[1]), lambda k: (0, 0)) for b in bs],
        out_shape=[jax.ShapeDtypeStruct((m, b.shape[1]), F32) for b in bs],
        compiler_params=_cparams())(a, *bs)


def _mm_tn(a, b, name, scale=1.0, ex=None):
    t, m = a.shape
    n = b.shape[1]
    tm, tn = _wide_tile(m), _wide_tile(n)
    tk = _row_tile(t, 1024)
    nk = t // tk

    def body(a_ref, b_ref, o_ref):
        k = pl.program_id(2)

        @pl.when(k == 0)
        def _():
            o_ref[...] = jnp.zeros_like(o_ref)

        o_ref[...] += _dot_tn(a_ref[...].astype(BF16), b_ref[...].astype(BF16))
        if scale != 1.0:
            @pl.when(k == nk - 1)
            def _():
                o_ref[...] = o_ref[...] * scale

    outs = _call_with_exchange(
        ex, body, name, (m // tm, n // tn, nk),
        [pl.BlockSpec((tk, tm), lambda i, j, k: (k, i)), pl.BlockSpec((tk, tn), lambda i, j, k: (k, j))],
        [pl.BlockSpec((tm, tn), lambda i, j, k: (i, j))], [jax.ShapeDtypeStruct((m, n), F32)], [], (a, b))
    return outs[0] if ex is None else outs


class _Exchange(NamedTuple):
    operands: list
    out_shapes: list
    sem_shapes: list
    build: Callable


def _call_with_exchange(ex, body, name, grid, in_specs, out_specs, out_shape, scratch_shapes, operands):
    if ex is None:
        return pl.pallas_call(body, name=name, grid=grid, in_specs=in_specs, out_specs=out_specs, out_shape=out_shape,
                              scratch_shapes=scratch_shapes, compiler_params=_cparams())(*operands)
    n_in, n_out, n_scr = len(in_specs), len(out_specs), len(scratch_shapes)
    k_in, k_out = len(ex.operands), len(ex.out_shapes)

    def carried(*refs):
        a, b = n_in, n_in + k_in
        c, e = b + n_out, b + n_out + k_out
        f = e + n_scr
        start, finish = ex.build(refs[a:b], refs[c:e], refs[f:])
        steps = [pl.program_id(ax) for ax in range(len(grid))]
        first = functools.reduce(jnp.logical_and, [s == 0 for s in steps])
        last = functools.reduce(jnp.logical_and, [s == n - 1 for s, n in zip(steps, grid)])
        pl.when(first)(start)
        body(*refs[:a], *refs[b:c], *refs[e:f])
        pl.when(last)(finish)

    return pl.pallas_call(
        carried, name=name, grid=grid, in_specs=list(in_specs) + [ANY] * k_in,
        out_specs=list(out_specs) + [ANY] * k_out, out_shape=list(out_shape) + list(ex.out_shapes),
        scratch_shapes=list(scratch_shapes) + list(ex.sem_shapes), compiler_params=_cparams(),
    )(*operands, *ex.operands)


def _run_exchange(ex, name):
    k_in, k_out = len(ex.operands), len(ex.out_shapes)

    def body(*refs):
        start, finish = ex.build(refs[:k_in], refs[k_in:k_in + k_out], refs[k_in + k_out:])
        start()
        finish()

    return pl.pallas_call(body, name=name, in_specs=[ANY] * k_in, out_specs=[ANY] * k_out,
                          out_shape=list(ex.out_shapes), scratch_shapes=list(ex.sem_shapes))(*ex.operands)


def _ffn_fwd(x, g, wgu4, wd2, name, ex=None, next_gain=None, target=None):
    t, d = x.shape
    tm = _row_tile(t, 512)
    assert next_gain is None or target is None
    extra = [a for a in (next_gain, target) if a is not None]

    def body(*refs):
        x_ref, g_ref, wg_ref, wu_ref, wd_ref = refs[:5]
        e_ref = refs[5] if extra else None
        outs, (xn_scr, acc_scr) = refs[5 + len(extra):-2], refs[-2:]
        if target is not None:
            dy_ref, loss_ref, gg_ref, uu_ref = outs
        elif next_gain is not None:
            o_ref, gg_ref, uu_ref, h_ref = outs
        else:
            o_ref, gg_ref, uu_ref = outs
        i, j = pl.program_id(0), pl.program_id(1)

        @pl.when(j == 0)
        def _():
            xv = x_ref[...]
            xn_scr[...] = (xv * _rstd(xv, d) * g_ref[...]).astype(BF16)
            acc_scr[...] = jnp.zeros_like(acc_scr)

        if target is not None:
            @pl.when((i == 0) & (j == 0))
            def _():
                loss_ref[...] = jnp.zeros_like(loss_ref)

        xn = xn_scr[...]
        gg = _dot(xn, wg_ref[0])
        uu = _dot(xn, wu_ref[0])
        gg_ref[...] = gg.astype(BF16)
        uu_ref[...] = uu.astype(BF16)
        act = gg * jax.nn.sigmoid(gg) * uu
        acc_scr[...] += _dot(act.astype(BF16), wd_ref[0])

        @pl.when(j == 1)
        def _():
            y = x_ref[...] + 0.5 * acc_scr[...]
            if target is not None:
                e = y - e_ref[...]
                dy_ref[...] = e * (1.0 / d)
                part = 0.5 * jnp.sum(jnp.sum(e * e, axis=-1, keepdims=True) * (1.0 / d), axis=0, keepdims=True)
                loss_ref[...] += jnp.broadcast_to(part, loss_ref.shape)
            else:
                o_ref[...] = y
                if next_gain is not None:
                    h_ref[...] = (y * _rstd(y, d) * e_ref[...]).astype(BF16)

    row = pl.BlockSpec((tm, d), lambda i, j: (i, 0))
    vec = pl.BlockSpec((1, d), lambda i, j: (0, 0))
    ffb = pl.BlockSpec((tm, FF_TILE), lambda i, j: (i, j))
    f32_rows, bf16_ff = jax.ShapeDtypeStruct((t, d), F32), jax.ShapeDtypeStruct((t, D_FF), BF16)
    if target is not None:
        extra_spec, out_specs = [row], [row, pl.BlockSpec((1, LANES), lambda i, j: (0, 0)), ffb, ffb]
        out_shape = [f32_rows, jax.ShapeDtypeStruct((1, LANES), F32), bf16_ff, bf16_ff]
    elif next_gain is not None:
        extra_spec, out_specs = [vec], [row, ffb, ffb, row]
        out_shape = [f32_rows, bf16_ff, bf16_ff, jax.ShapeDtypeStruct((t, d), BF16)]
    else:
        extra_spec, out_specs, out_shape = [], [row, ffb, ffb], [f32_rows, bf16_ff, bf16_ff]
    return _call_with_exchange(
        ex, body, name, (t // tm, 2),
        [row, vec,
         pl.BlockSpec((1, d, FF_TILE), lambda i, j: (j, 0, 0)),
         pl.BlockSpec((1, d, FF_TILE), lambda i, j: (j + 2, 0, 0)),
         pl.BlockSpec((1, FF_TILE, d), lambda i, j: (j, 0, 0))] + extra_spec,
        out_specs, out_shape,
        [pltpu.VMEM((tm, d), BF16), pltpu.VMEM((tm, d), F32)], (x, g, wgu4, wgu4, wd2, *extra))


def _ffn_bwd(x, g, dy, gpre, upre, wgu4, wd2, name, ex=None):
    t, d = x.shape
    tm = _row_tile(t, 512)

    def body(dy_ref, gg_ref, uu_ref, wgu_hbm, wd_hbm, dg_ref, du_ref, act_ref, part_ref, wg_ref, wu_ref, wd_ref):
        j = pl.program_id(0)

        @pl.when(pl.program_id(1) == 0)
        def _():
            pltpu.sync_copy(wgu_hbm.at[j], wg_ref.at[0])
            pltpu.sync_copy(wgu_hbm.at[j + 2], wu_ref.at[0])
            pltpu.sync_copy(wd_hbm.at[j], wd_ref.at[0])

        gg = gg_ref[...].astype(F32)
        uu = uu_ref[...].astype(F32)
        sg = jax.nn.sigmoid(gg)
        silu = gg * sg
        act_ref[...] = (silu * uu).astype(BF16)
        dyh = (0.5 * dy_ref[...]).astype(BF16)
        dact = _dot_nt(dyh, wd_ref[0])
        du = (dact * silu).astype(BF16)
        dgt = (dact * uu * (sg * (1.0 + gg * (1.0 - sg)))).astype(BF16)
        du_ref[...] = du
        dg_ref[...] = dgt
        part_ref[0] = _dot_nt(dgt, wg_ref[0]) + _dot_nt(du, wu_ref[0])

    row = pl.BlockSpec((tm, d), lambda j, i: (i, 0))
    ffb = pl.BlockSpec((tm, FF_TILE), lambda j, i: (i, j))
    dgt, dup, act, parts, *got = _call_with_exchange(
        ex, body, name, (2, t // tm),
        [row, ffb, ffb, ANY, ANY],
        [ffb, ffb, ffb, pl.BlockSpec((1, tm, d), lambda j, i: (j, i, 0))],
        [jax.ShapeDtypeStruct((t, D_FF), BF16)] * 3 + [jax.ShapeDtypeStruct((2, t, d), F32)],
        [pltpu.VMEM((1, d, FF_TILE), BF16), pltpu.VMEM((1, d, FF_TILE), BF16), pltpu.VMEM((1, FF_TILE, d), BF16)],
        (dy, gpre, upre, wgu4, wd2))

    def norm_body(x_ref, g_ref, p_ref, dy_ref, dx_ref, dgain_ref, xn_ref):
        @pl.when(pl.program_id(0) == 0)
        def _():
            dgain_ref[...] = jnp.zeros_like(dgain_ref)

        xv = x_ref[...]
        r = _rstd(xv, d)
        xn_ref[...] = (xv * r * g_ref[...]).astype(BF16)
        dx, dgr = _rms_vjp(xv, r, g_ref[...], p_ref[0] + p_ref[1], d)
        dx_ref[...] = dy_ref[...] + dx
        dgain_ref[...] += jnp.sum(dgr, axis=0, keepdims=True)

    tn = _row_tile(t, 256)
    nrow = pl.BlockSpec((tn, d), lambda i: (i, 0))
    vec = pl.BlockSpec((1, d), lambda i: (0, 0))
    dx, dgain, xn = pl.pallas_call(
        norm_body, name=name + "_norm", grid=(t // tn,),
        in_specs=[nrow, vec, pl.BlockSpec((2, tn, d), lambda i: (0, i, 0)), nrow],
        out_specs=[nrow, vec, nrow],
        out_shape=[jax.ShapeDtypeStruct((t, d), F32), jax.ShapeDtypeStruct((1, d), F32),
                   jax.ShapeDtypeStruct((t, d), BF16)],
        compiler_params=_cparams())(x, g, parts, dy)
    return [dx, dgain, xn, dgt, dup, act] + got


def _sgu_layernorm(vpre, lg, lb):
    v = _gelu(vpre)
    mu = jnp.mean(v, axis=-1, keepdims=True)
    xc = v - mu
    rstd = lax.rsqrt(jnp.mean(xc * xc, axis=-1, keepdims=True) + EPS)
    xhat = xc * rstd
    return xhat, rstd, xhat * lg + lb


def _sgu_fwd(zuv, lg, lb, wt, bias_l, name):
    t = zuv.shape[0]
    tm = _row_tile(t, 512)

    def body(u_ref, v_ref, lg_ref, lb_ref, wt_ref, bl_ref, o_ref, vln_scr):
        _, _, vln = _sgu_layernorm(v_ref[...], lg_ref[...], lb_ref[...])
        vln_scr[...] = vln.astype(BF16)
        lo = lax.broadcasted_iota(jnp.int32, (CHUNK, LANES), 1) < 64
        for c in range(tm // CHUNK):
            rows = slice(c * CHUNK, (c + 1) * CHUNK)
            for p in range(SG_GROUPS // 2):
                cols = slice(p * LANES, (p + 1) * LANES)
                vp = vln_scr[rows, cols]
                mixed = jnp.where(lo, _dot(wt_ref[2 * p], vp), _dot(wt_ref[2 * p + 1], vp)) + bl_ref[:, cols]
                o_ref[rows, cols] = (_gelu(u_ref[rows, cols]) * mixed).astype(BF16)

    half = lambda k: pl.BlockSpec((tm, SG_WIDTH), lambda i: (i, k))
    vec = pl.BlockSpec((1, SG_WIDTH), lambda i: (0, 0))
    return pl.pallas_call(
        body, name=name, grid=(t // tm,),
        in_specs=[half(0), half(1), vec, vec,
                  pl.BlockSpec((SG_GROUPS, CHUNK, CHUNK), lambda i: (0, 0, 0)),
                  pl.BlockSpec((CHUNK, SG_WIDTH), lambda i: (0, 0))],
        out_specs=pl.BlockSpec((tm, SG_WIDTH), lambda i: (i, 0)),
        out_shape=jax.ShapeDtypeStruct((t, SG_WIDTH), BF16),
        scratch_shapes=[pltpu.VMEM((tm, SG_WIDTH), BF16)],
        compiler_params=_cparams())(zuv, zuv, lg, lb, wt, bias_l)


def _sgu_bwd(zuv, dya, lg, lb, wt, wt_t, bias_l, name):
    t = zuv.shape[0]
    tm = _row_tile(t, 256)
    nsteps = t // tm

    def body(u_ref, v_ref, dy_ref, lg_ref, lb_ref, wt_ref, wtt_ref, bl_ref,
             dz_ref, dwt_ref, dbl_ref, dlg_ref, dlb_ref, vln_scr, dvln_scr, dbacc_scr):
        step = pl.program_id(0)

        @pl.when(step == 0)
        def _():
            dwt_ref[...] = jnp.zeros_like(dwt_ref)
            dlg_ref[...] = jnp.zeros_like(dlg_ref)
            dlb_ref[...] = jnp.zeros_like(dlb_ref)
            dbl_ref[...] = jnp.zeros_like(dbl_ref)
            dbacc_scr[...] = jnp.zeros_like(dbacc_scr)

        vpre = v_ref[...]
        lgv = lg_ref[...]
        xhat, rstd, vln = _sgu_layernorm(vpre, lgv, lb_ref[...])
        vln_scr[...] = vln.astype(BF16)
        lo = lax.broadcasted_iota(jnp.int32, (CHUNK, LANES), 1) < 64
        for c in range(tm // CHUNK):
            rows = slice(c * CHUNK, (c + 1) * CHUNK)
            for p in range(SG_GROUPS // 2):
                cols = slice(p * LANES, (p + 1) * LANES)
                vp = vln_scr[rows, cols]
                mixed = jnp.where(lo, _dot(wt_ref[2 * p], vp), _dot(wt_ref[2 * p + 1], vp)) + bl_ref[:, cols]
                upre = u_ref[rows, cols]
                dyp = dy_ref[rows, cols]
                dz_ref[rows, cols] = (dyp * mixed * _gelu_grad(upre)).astype(BF16)
                dm = dyp * _gelu(upre)
                dbacc_scr[:, cols] += dm
                dlo = jnp.where(lo, dm, 0.0).astype(BF16)
                dhi = jnp.where(lo, 0.0, dm).astype(BF16)
                dvln_scr[rows, cols] = _dot(wtt_ref[2 * p], dlo) + _dot(wtt_ref[2 * p + 1], dhi)
                dwt_ref[2 * p] += _dot_nt(dlo, vp)
                dwt_ref[2 * p + 1] += _dot_nt(dhi, vp)
        dvln = dvln_scr[...]
        dlg_ref[...] += jnp.sum(dvln * xhat, axis=0, keepdims=True)
        dlb_ref[...] += jnp.sum(dvln, axis=0, keepdims=True)
        dxh = dvln * lgv
        dv = rstd * (dxh - jnp.mean(dxh, axis=-1, keepdims=True)
                     - xhat * jnp.mean(dxh * xhat, axis=-1, keepdims=True))
        dz_ref[:, SG_WIDTH:] = (dv * _gelu_grad(vpre)).astype(BF16)

        @pl.when(step == nsteps - 1)
        def _():
            rr = lax.broadcasted_iota(jnp.int32, (CHUNK, CHUNK), 0)
            cc = lax.broadcasted_iota(jnp.int32, (CHUNK, CHUNK), 1)
            tril = (cc <= rr).astype(F32)
            for gidx in range(SG_GROUPS):
                dwt_ref[gidx] = dwt_ref[gidx] * tril
            kk = lax.broadcasted_iota(jnp.int32, (SG_WIDTH, LANES), 0)
            gg = lax.broadcasted_iota(jnp.int32, (SG_WIDTH, LANES), 1)
            sel = ((kk // 64) == gg).astype(F32)
            dbl_ref[...] = jnp.dot(dbacc_scr[...], sel, preferred_element_type=F32,
                                   precision=lax.Precision.HIGHEST)

    half = lambda k: pl.BlockSpec((tm, SG_WIDTH), lambda i: (i, k))
    vec = pl.BlockSpec((1, SG_WIDTH), lambda i: (0, 0))
    wspec = pl.BlockSpec((SG_GROUPS, CHUNK, CHUNK), lambda i: (0, 0, 0))
    return pl.pallas_call(
        body, name=name, grid=(nsteps,),
        in_specs=[half(0), half(1), pl.BlockSpec((tm, SG_WIDTH), lambda i: (i, 0)), vec, vec,
                  wspec, wspec, pl.BlockSpec((CHUNK, SG_WIDTH), lambda i: (0, 0))],
        out_specs=[pl.BlockSpec((tm, 2 * SG_WIDTH), lambda i: (i, 0)), wspec,
                   pl.BlockSpec((CHUNK, LANES), lambda i: (0, 0)), vec, vec],
        out_shape=[jax.ShapeDtypeStruct((t, 2 * SG_WIDTH), BF16),
                   jax.ShapeDtypeStruct((SG_GROUPS, CHUNK, CHUNK), F32),
                   jax.ShapeDtypeStruct((CHUNK, LANES), F32),
                   jax.ShapeDtypeStruct((1, SG_WIDTH), F32), jax.ShapeDtypeStruct((1, SG_WIDTH), F32)],
        scratch_shapes=[pltpu.VMEM((tm, SG_WIDTH), BF16), pltpu.VMEM((tm, SG_WIDTH), F32),
                        pltpu.VMEM((CHUNK, SG_WIDTH), F32)],
        compiler_params=_cparams())(zuv, zuv, dya, lg, lb, wt, wt_t, bias_l)


def _rope(x, c, s1, s2):
    return x * c + pltpu.roll(x, LANES - 16, 1) * s1 + pltpu.roll(x, 16, 1) * s2


def _rope_t(dy, c, s1, s2):
    return dy * c + pltpu.roll(dy * s1, 16, 1) + pltpu.roll(dy * s2, LANES - 16, 1)


def _mla_prep_fwd(zcq, zckv, zkr, gcq, gckv, qg, kg, wuq, wuk, wuv, rc, rs1, rs2, name, ex=None):
    t = zcq.shape[0]
    tm = _row_tile(t, 256)
    hd = MLA_HEADS * LANES

    def body(zcq_ref, zckv_ref, zkr_ref, gcq_ref, gckv_ref, qg_ref, kg_ref, wuq_ref, wuk_ref, wuv_ref,
             c_ref, s1_ref, s2_ref, q_ref, k_ref, v_ref, cqn_ref, ckvn_ref):
        c, s1, s2 = c_ref[...], s1_ref[...], s2_ref[...]
        xq = zcq_ref[...]
        cqn = (xq * _rstd(xq, MLA_Q_RANK) * gcq_ref[...]).astype(BF16)
        cqn_ref[...] = cqn
        ql = _dot(cqn, wuq_ref[...])
        xk = zckv_ref[...]
        ckvn = (xk * _rstd(xk, MLA_KV_RANK) * gckv_ref[...]).astype(BF16)
        ckvn_ref[...] = ckvn
        kl = _dot(ckvn, wuk_ref[...])
        slot_lane = lax.broadcasted_iota(jnp.int32, (tm, hd), 1) % LANES
        v_ref[...] = jnp.where(slot_lane == V_ONES_LANE, 1.0, _dot(ckvn, wuv_ref[...])).astype(BF16)
        kr = zkr_ref[...]
        for h in range(MLA_HEADS):
            sl = slice(h * LANES, (h + 1) * LANES)
            qh = ql[:, sl]
            q_ref[:, sl] = (_rope(qh * _rstd(qh, MLA_QK) * qg_ref[...], c, s1, s2) * ATTN_SCALE2).astype(BF16)
            kh = kl[:, sl] + kr
            k_ref[:, sl] = _rope(kh * _rstd(kh, MLA_QK) * kg_ref[...], c, s1, s2).astype(BF16)

    row = lambda n: pl.BlockSpec((tm, n), lambda i: (i, 0))
    full = lambda a: pl.BlockSpec(a.shape, lambda i: (0, 0))
    return _call_with_exchange(
        ex, body, name, (t // tm,),
        [row(MLA_Q_RANK), row(MLA_KV_RANK), row(LANES), full(gcq), full(gckv), full(qg), full(kg),
         full(wuq), full(wuk), full(wuv), row(LANES), row(LANES), row(LANES)],
        [row(hd), row(hd), row(hd), row(MLA_Q_RANK), row(MLA_KV_RANK)],
        [jax.ShapeDtypeStruct((t, hd), BF16)] * 3
        + [jax.ShapeDtypeStruct((t, MLA_Q_RANK), BF16), jax.ShapeDtypeStruct((t, MLA_KV_RANK), BF16)],
        [], (zcq, zckv, zkr, gcq, gckv, qg, kg, wuq, wuk, wuv, rc, rs1, rs2))


def _mla_prep_bwd(zcq, zckv, zkr, gcq, gckv, qg, kg, wuq, wuk, wuv, rc, rs1, rs2, dq, dk, dv, name):
    t = zcq.shape[0]
    tm = _row_tile(t, 256)
    hd = MLA_HEADS * LANES

    def body(zcq_ref, zckv_ref, zkr_ref, gcq_ref, gckv_ref, qg_ref, kg_ref, wuq_ref, wuk_ref, wuv_ref,
             c_ref, s1_ref, s2_ref, dq_ref, dk_ref, dv_ref,
             dzcq_ref, dzckv_ref, dzkr_ref, dql_ref, dkl_ref, dgcq_ref, dgckv_ref, dqg_ref, dkg_ref):
        @pl.when(pl.program_id(0) == 0)
        def _():
            for ref in (dgcq_ref, dgckv_ref, dqg_ref, dkg_ref):
                ref[...] = jnp.zeros_like(ref)

        c, s1, s2 = c_ref[...], s1_ref[...], s2_ref[...]
        qgv, kgv = qg_ref[...], kg_ref[...]
        xq = zcq_ref[...]
        rq = _rstd(xq, MLA_Q_RANK)
        ql = _dot((xq * rq * gcq_ref[...]).astype(BF16), wuq_ref[...])
        xk = zckv_ref[...]
        rk = _rstd(xk, MLA_KV_RANK)
        kl = _dot((xk * rk * gckv_ref[...]).astype(BF16), wuk_ref[...])
        kr = zkr_ref[...]
        dqg_acc = jnp.zeros((tm, LANES), F32)
        dkg_acc = jnp.zeros((tm, LANES), F32)
        dkr = jnp.zeros((tm, LANES), F32)
        for h in range(MLA_HEADS):
            sl = slice(h * LANES, (h + 1) * LANES)
            qh = ql[:, sl]
            dqh, dgr = _rms_vjp(qh, _rstd(qh, MLA_QK), qgv, _rope_t(dq_ref[:, sl], c, s1, s2), MLA_QK)
            dql_ref[:, sl] = dqh.astype(BF16)
            dqg_acc += dgr
            kh = kl[:, sl] + kr
            dkh, dgr = _rms_vjp(kh, _rstd(kh, MLA_QK), kgv, _rope_t(dk_ref[:, sl], c, s1, s2), MLA_QK)
            dkl_ref[:, sl] = dkh.astype(BF16)
            dkg_acc += dgr
            dkr += dkh
        dqg_ref[...] += jnp.sum(dqg_acc, axis=0, keepdims=True)
        dkg_ref[...] += jnp.sum(dkg_acc, axis=0, keepdims=True)
        lane = lax.broadcasted_iota(jnp.int32, (tm, LANES), 1)
        dzkr_ref[...] = jnp.where((lane >= MLA_NOPE) & (lane < MLA_QK), dkr, 0.0).astype(BF16)
        dcqn = _dot_nt(dql_ref[...], wuq_ref[...])
        dx, dgr = _rms_vjp(xq, rq, gcq_ref[...], dcqn, MLA_Q_RANK)
        dzcq_ref[...] = dx.astype(BF16)
        dgcq_ref[...] += jnp.sum(dgr, axis=0, keepdims=True)
        dckvn = _dot_nt(dkl_ref[...], wuk_ref[...]) + _dot_nt(dv_ref[...].astype(BF16), wuv_ref[...])
        dx, dgr = _rms_vjp(xk, rk, gckv_ref[...], dckvn, MLA_KV_RANK)
        dzckv_ref[...] = dx.astype(BF16)
        dgckv_ref[...] += jnp.sum(dgr, axis=0, keepdims=True)

    row = lambda n: pl.BlockSpec((tm, n), lambda i: (i, 0))
    full = lambda a: pl.BlockSpec(a.shape, lambda i: (0, 0))
    vec = lambda n: pl.BlockSpec((1, n), lambda i: (0, 0))
    return pl.pallas_call(
        body, name=name, grid=(t // tm,),
        in_specs=[row(MLA_Q_RANK), row(MLA_KV_RANK), row(LANES), full(gcq), full(gckv), full(qg), full(kg),
                  full(wuq), full(wuk), full(wuv), row(LANES), row(LANES), row(LANES), row(hd), row(hd), row(hd)],
        out_specs=[row(MLA_Q_RANK), row(MLA_KV_RANK), row(LANES), row(hd), row(hd),
                   vec(MLA_Q_RANK), vec(MLA_KV_RANK), vec(LANES), vec(LANES)],
        out_shape=[jax.ShapeDtypeStruct((t, MLA_Q_RANK), BF16), jax.ShapeDtypeStruct((t, MLA_KV_RANK), BF16),
                   jax.ShapeDtypeStruct((t, LANES), BF16), jax.ShapeDtypeStruct((t, hd), BF16),
                   jax.ShapeDtypeStruct((t, hd), BF16), jax.ShapeDtypeStruct((1, MLA_Q_RANK), F32),
                   jax.ShapeDtypeStruct((1, MLA_KV_RANK), F32), jax.ShapeDtypeStruct((1, LANES), F32),
                   jax.ShapeDtypeStruct((1, LANES), F32)],
        compiler_params=_cparams(),
    )(zcq, zckv, zkr, gcq, gckv, qg, kg, wuq, wuk, wuv, rc, rs1, rs2, dq, dk, dv)


def _attn_tiles(t):
    tq = 512 if t >= 2048 else 128
    return tq, min(t, 4 * tq), min(t, 2 * tq)


def _causal_keep(tq, tk, i, j):
    row = lax.broadcasted_iota(jnp.int32, (tq, tk), 0)
    col = lax.broadcasted_iota(jnp.int32, (tq, tk), 1)
    return (col - row) <= (i * tq - j * tk)


def _causal_keep_t(tq, tk, i, j):
    key = lax.broadcasted_iota(jnp.int32, (tk, tq), 0)
    qry = lax.broadcasted_iota(jnp.int32, (tk, tq), 1)
    return (key - qry) <= (i * tq - j * tk)


ATTN_FWD_HEADS_PER_STEP = 2
ATTN_BWD_HEADS_PER_STEP = 2


def _attn_fwd(q, k, v, name):
    t, hd = q.shape
    hp = ATTN_FWD_HEADS_PER_STEP
    tq, tk, _ = _attn_tiles(t)
    pairs = [(i, j) for i in range(t // tq) for j in range(((i + 1) * tq - 1) // tk + 1)]
    ii = np.array([p[0] for p in pairs], np.int32)
    jj = np.array([p[1] for p in pairs], np.int32)

    def body(ii_ref, jj_ref, q_ref, k_ref, v_ref, o_ref, lse_ref, m_scr, acc_scr):
        s_id = pl.program_id(1)
        i, j = ii_ref[s_id], jj_ref[s_id]
        last = j == ((i + 1) * tq - 1) // tk
        ones_lane = lax.broadcasted_iota(jnp.int32, (tq, LANES), 1) == V_ONES_LANE

        @pl.when(j == 0)
        def _():
            m_scr[...] = jnp.full_like(m_scr, NEG)
            acc_scr[...] = jnp.zeros_like(acc_scr)

        def step(masked):
            for hh in range(hp):
                sl = slice(hh * LANES, (hh + 1) * LANES)
                s = _dot_nt(q_ref[:, sl], k_ref[:, sl])
                if masked:
                    s = jnp.where(_causal_keep(tq, tk, i, j), s, NEG)
                m_prev = m_scr[hh]
                m_new = jnp.maximum(m_prev, jnp.max(s, axis=1, keepdims=True))
                p = jnp.exp2(s - m_new)
                alpha = jnp.exp2(m_prev - m_new)
                acc = alpha * acc_scr[:, sl] + _dot(p.astype(BF16), v_ref[:, sl])
                if masked:
                    l_new = jnp.sum(jnp.where(ones_lane, acc, 0.0), axis=1, keepdims=True)
                    o_ref[:, sl] = (acc / l_new).astype(BF16)
                    lse_ref[:, sl] = jnp.broadcast_to(m_new + jnp.log(l_new) * LOG2E, (tq, LANES))
                else:
                    acc_scr[:, sl] = acc
                    m_scr[hh] = m_new

        @pl.when(jnp.logical_not(last))
        def _():
            step(False)

        @pl.when(last)
        def _():
            step(True)

    w = hp * LANES
    qspec = pl.BlockSpec((tq, w), lambda h, s, ii_r, jj_r: (ii_r[s], h))
    kspec = pl.BlockSpec((tk, w), lambda h, s, ii_r, jj_r: (jj_r[s], h))
    return pl.pallas_call(
        body, name=name,
        grid_spec=pltpu.PrefetchScalarGridSpec(
            num_scalar_prefetch=2, grid=(hd // w, len(pairs)), in_specs=[qspec, kspec, kspec],
            out_specs=[qspec, qspec],
            scratch_shapes=[pltpu.VMEM((hp, tq, 1), F32), pltpu.VMEM((tq, w), F32)]),
        out_shape=[jax.ShapeDtypeStruct((t, hd), BF16), jax.ShapeDtypeStruct((t, hd), F32)],
        compiler_params=_cparams())(jnp.asarray(ii), jnp.asarray(jj), q, k, v)


def _attn_bwd_rows(o, lse, do, name):
    t, hd = o.shape
    heads = hd // LANES
    tm = _row_tile(t, 512)

    def body(o_ref, lse_ref, do_ref, out_ref):
        lane = lax.broadcasted_iota(jnp.int32, (tm, LANES), 1)
        acc = jnp.zeros((tm, LANES), F32)
        for h in range(heads):
            sl = slice(h * LANES, (h + 1) * LANES)
            delta = jnp.sum(do_ref[:, sl].astype(F32) * o_ref[:, sl].astype(F32), axis=1, keepdims=True)
            acc = jnp.where(lane == h, delta, acc)
            acc = jnp.where(lane == heads + h, lse_ref[:, sl], acc)
        out_ref[...] = acc

    row = pl.BlockSpec((tm, hd), lambda i: (i, 0))
    cols = pl.pallas_call(
        body, name=name, grid=(t // tm,), in_specs=[row, row, row],
        out_specs=pl.BlockSpec((tm, LANES), lambda i: (i, 0)),
        out_shape=jax.ShapeDtypeStruct((t, LANES), F32), compiler_params=_cparams())(o, lse, do)
    rows = cols.T
    return rows[:heads].reshape(heads, 1, t), rows[heads:2 * heads].reshape(heads, 1, t)


def _attn_bwd(q, k, v, delta_rows, lse_rows, do, name):
    t, hd = q.shape
    hp = ATTN_BWD_HEADS_PER_STEP
    tq, _, tk = _attn_tiles(t)
    nq = t // tq
    pairs = [(i, j) for j in range(t // tk) for i in range((j * tk) // tq, nq)]
    ii = np.array([p[0] for p in pairs], np.int32)
    jj = np.array([p[1] for p in pairs], np.int32)

    def body(jj_ref, ii_ref, q_ref, k_ref, v_ref, delta_ref, lse_ref, do_ref, dq_ref, dk_ref, dv_ref,
             dk_scr, dv_scr):
        s_id = pl.program_id(1)
        i, j = ii_ref[s_id], jj_ref[s_id]

        @pl.when(s_id == 0)
        def _():
            dq_ref[...] = jnp.zeros_like(dq_ref)

        @pl.when(i == (j * tk) // tq)
        def _():
            dk_scr[...] = jnp.zeros_like(dk_scr)
            dv_scr[...] = jnp.zeros_like(dv_scr)

        rows = pl.ds(pl.multiple_of(i * tq, tq), tq)

        def step(masked):
            for hh in range(hp):
                sl = slice(hh * LANES, (hh + 1) * LANES)
                qv, kv, dov = q_ref[:, sl], k_ref[:, sl], do_ref[:, sl]
                st = _dot_nt(kv, qv)
                if masked:
                    st = jnp.where(_causal_keep_t(tq, tk, i, j), st, NEG)
                pt = jnp.exp2(st - lse_ref[hh])
                dv_scr[:, sl] += _dot(pt.astype(BF16), dov)
                dpt = _dot_nt(v_ref[:, sl], dov)
                dst = (pt * (dpt - delta_ref[hh]) * ATTN_SCALE).astype(BF16)
                dk_scr[:, sl] += _dot(dst, qv)
                dq_ref[rows, sl] += _dot_tn(dst, kv)

        crosses = (j + 1) * tk - 1 > i * tq

        @pl.when(jnp.logical_not(crosses))
        def _():
            step(False)

        @pl.when(crosses)
        def _():
            step(True)

        @pl.when(i == nq - 1)
        def _():
            dk_ref[...] = dk_scr[...] * (1.0 / ATTN_SCALE2)
            dv_ref[...] = dv_scr[...]

    w = hp * LANES
    qspec = pl.BlockSpec((tq, w), lambda h, s, jj_r, ii_r: (ii_r[s], h))
    kspec = pl.BlockSpec((tk, w), lambda h, s, jj_r, ii_r: (jj_r[s], h))
    rspec = pl.BlockSpec((hp, 1, tq), lambda h, s, jj_r, ii_r: (h, 0, ii_r[s]))
    return pl.pallas_call(
        body, name=name,
        grid_spec=pltpu.PrefetchScalarGridSpec(
            num_scalar_prefetch=2, grid=(hd // w, len(pairs)),
            in_specs=[qspec, kspec, kspec, rspec, rspec, qspec],
            out_specs=[pl.BlockSpec((t, w), lambda h, s, jj_r, ii_r: (0, h)), kspec, kspec],
            scratch_shapes=[pltpu.VMEM((tk, w), F32), pltpu.VMEM((tk, w), F32)]),
        out_shape=[jax.ShapeDtypeStruct((t, hd), F32)] * 3,
        compiler_params=_cparams())(jnp.asarray(jj), jnp.asarray(ii), q, k, v, delta_rows, lse_rows, do)


MEM_W = MEM_HEADS * LANES


def _mem_kv_fwd(mem, gmem, wkv, kg, name):
    m, d = mem.shape

    def body(mem_ref, g_ref, w_ref, kg_ref, k_ref, v_ref, mn_ref):
        xv = mem_ref[...]
        mn = (xv * _rstd(xv, d) * g_ref[...]).astype(BF16)
        mn_ref[...] = mn
        kvm = _dot(mn, w_ref[...])
        v_ref[...] = kvm[:, MEM_W:].astype(BF16)
        for h in range(MEM_HEADS):
            sl = slice(h * LANES, (h + 1) * LANES)
            kh = kvm[:, sl]
            k_ref[:, sl] = (kh * _rstd(kh, LANES) * kg_ref[...]).astype(BF16)

    full = lambda a: pl.BlockSpec(a.shape, lambda i: (0, 0))
    return pl.pallas_call(
        body, name=name, grid=(1,), in_specs=[full(mem), full(gmem), full(wkv), full(kg)],
        out_specs=[pl.BlockSpec((m, MEM_W), lambda i: (0, 0)), pl.BlockSpec((m, MEM_W), lambda i: (0, 0)),
                   pl.BlockSpec((m, d), lambda i: (0, 0))],
        out_shape=[jax.ShapeDtypeStruct((m, MEM_W), BF16), jax.ShapeDtypeStruct((m, MEM_W), BF16),
                   jax.ShapeDtypeStruct((m, d), BF16)],
        compiler_params=_cparams())(mem, gmem, wkv, kg)


def _mem_softmax(qn, kh):
    s = _dot_nt(qn, kh) * (LANES ** -0.5)
    e = jnp.exp(s - jnp.max(s, axis=1, keepdims=True))
    return e / jnp.sum(e, axis=1, keepdims=True)


def _mem_attn_fwd(zqm, qg, km, vm, name):
    t = zqm.shape[0]
    tm = _row_tile(t, 512)

    def body(q_ref, qg_ref, k_ref, v_ref, o_ref):
        for h in range(MEM_HEADS):
            sl = slice(h * LANES, (h + 1) * LANES)
            qh = q_ref[:, sl]
            qn = (qh * _rstd(qh, LANES) * qg_ref[...]).astype(BF16)
            p = _mem_softmax(qn, k_ref[:, sl])
            o_ref[:, sl] = _dot(p.astype(BF16), v_ref[:, sl]).astype(BF16)

    row = pl.BlockSpec((tm, MEM_W), lambda i: (i, 0))
    full = lambda a: pl.BlockSpec(a.shape, lambda i: (0, 0))
    return pl.pallas_call(
        body, name=name, grid=(t // tm,), in_specs=[row, full(qg), full(km), full(vm)], out_specs=row,
        out_shape=jax.ShapeDtypeStruct((t, MEM_W), BF16), compiler_params=_cparams())(zqm, qg, km, vm)


def _mem_attn_bwd(zqm, dyc, qg, km, vm, name):
    t = zqm.shape[0]
    m = km.shape[0]
    tm = _row_tile(t, 256)

    def body(q_ref, dy_ref, qg_ref, k_ref, v_ref, dz_ref, dk_ref, dv_ref, dqg_ref):
        @pl.when(pl.program_id(0) == 0)
        def _():
            dk_ref[...] = jnp.zeros_like(dk_ref)
            dv_ref[...] = jnp.zeros_like(dv_ref)
            dqg_ref[...] = jnp.zeros_like(dqg_ref)

        qgv = qg_ref[...]
        dqg_acc = jnp.zeros((tm, LANES), F32)
        for h in range(MEM_HEADS):
            sl = slice(h * LANES, (h + 1) * LANES)
            qh = q_ref[:, sl]
            r = _rstd(qh, LANES)
            qn = (qh * r * qgv).astype(BF16)
            kh = k_ref[:, sl]
            p = _mem_softmax(qn, kh)
            dov = dy_ref[:, sl]
            dv_ref[:, sl] += _dot_tn(p.astype(BF16), dov)
            dp = _dot_nt(dov, v_ref[:, sl])
            ds = (p * (dp - jnp.sum(dp * p, axis=1, keepdims=True)) * (LANES ** -0.5)).astype(BF16)
            dk_ref[:, sl] += _dot_tn(ds, qn)
            dqh, dgr = _rms_vjp(qh, r, qgv, _dot(ds, kh), LANES)
            dz_ref[:, sl] = dqh.astype(BF16)
            dqg_acc += dgr
        dqg_ref[...] += jnp.sum(dqg_acc, axis=0, keepdims=True)

    row = pl.BlockSpec((tm, MEM_W), lambda i: (i, 0))
    full = lambda a: pl.BlockSpec(a.shape, lambda i: (0, 0))
    acc = pl.BlockSpec((m, MEM_W), lambda i: (0, 0))
    return pl.pallas_call(
        body, name=name, grid=(t // tm,), in_specs=[row, row, full(qg), full(km), full(vm)],
        out_specs=[row, acc, acc, pl.BlockSpec((1, LANES), lambda i: (0, 0))],
        out_shape=[jax.ShapeDtypeStruct((t, MEM_W), BF16), jax.ShapeDtypeStruct((m, MEM_W), F32),
                   jax.ShapeDtypeStruct((m, MEM_W), F32), jax.ShapeDtypeStruct((1, LANES), F32)],
        compiler_params=_cparams())(zqm, dyc, qg, km, vm)


def _mem_kv_bwd(mem, gmem, wkv, kg, dkn, dvm, name):
    m, d = mem.shape

    def body(mem_ref, g_ref, w_ref, kg_ref, dk_ref, dv_ref, dw_ref, dkg_ref, dg_ref, dkv_scr):
        xv = mem_ref[...]
        r = _rstd(xv, d)
        mn = (xv * r * g_ref[...]).astype(BF16)
        kvm = _dot(mn, w_ref[...])
        dkv_scr[:, MEM_W:] = dv_ref[...].astype(BF16)
        dkg_acc = jnp.zeros((m, LANES), F32)
        for h in range(MEM_HEADS):
            sl = slice(h * LANES, (h + 1) * LANES)
            kh = kvm[:, sl]
            dkh, dgr = _rms_vjp(kh, _rstd(kh, LANES), kg_ref[...], dk_ref[:, sl], LANES)
            dkv_scr[:, sl] = dkh.astype(BF16)
            dkg_acc += dgr
        dkg_ref[...] = jnp.sum(dkg_acc, axis=0, keepdims=True)
        dkv = dkv_scr[...]
        dw_ref[...] = _dot_tn(mn, dkv)
        dmn = _dot_nt(dkv, w_ref[...])
        dg_ref[...] = jnp.sum(dmn * xv * r, axis=0, keepdims=True)

    full = lambda a: pl.BlockSpec(a.shape, lambda i: (0, 0))
    return pl.pallas_call(
        body, name=name, grid=(1,),
        in_specs=[full(mem), full(gmem), full(wkv), full(kg), full(dkn), full(dvm)],
        out_specs=[pl.BlockSpec((d, 2 * MEM_W), lambda i: (0, 0)), pl.BlockSpec((1, LANES), lambda i: (0, 0)),
                   pl.BlockSpec((1, d), lambda i: (0, 0))],
        out_shape=[jax.ShapeDtypeStruct((d, 2 * MEM_W), F32), jax.ShapeDtypeStruct((1, LANES), F32),
                   jax.ShapeDtypeStruct((1, d), F32)],
        scratch_shapes=[pltpu.VMEM((m, 2 * MEM_W), BF16)],
        compiler_params=_cparams())(mem, gmem, wkv, kg, dkn, dvm)


def _merge_fwd(x1, ya, yb, yc, zg, bg, wa, wb, wc, wo, name):
    t, d = x1.shape
    tm = _row_tile(t, 256)

    def body(x_ref, ya_ref, yb_ref, yc_ref, zg_ref, bg_ref, wa_ref, wb_ref, wc_ref, wo_ref,
             x2_ref, mg_ref, pa_ref, pb_ref, pc_ref):
        merged = None
        for k, (y_ref, w_ref, p_ref) in enumerate(
                ((ya_ref, wa_ref, pa_ref), (yb_ref, wb_ref, pb_ref), (yc_ref, wc_ref, pc_ref))):
            sl = slice(k * d, (k + 1) * d)
            pr = _dot(y_ref[...], w_ref[...])
            p_ref[...] = pr.astype(BF16)
            term = jax.nn.sigmoid(zg_ref[:, sl] + bg_ref[:, sl]) * pr
            merged = term if merged is None else merged + term
        mb = merged.astype(BF16)
        mg_ref[...] = mb
        x2_ref[...] = x_ref[...] + _dot(mb, wo_ref[...])

    row = lambda n: pl.BlockSpec((tm, n), lambda i: (i, 0))
    full = lambda a: pl.BlockSpec(a.shape, lambda i: (0, 0))
    return pl.pallas_call(
        body, name=name, grid=(t // tm,),
        in_specs=[row(d), row(ya.shape[1]), row(yb.shape[1]), row(yc.shape[1]), row(3 * d), full(bg),
                  full(wa), full(wb), full(wc), full(wo)],
        out_specs=[row(d)] * 5,
        out_shape=[jax.ShapeDtypeStruct((t, d), F32)] + [jax.ShapeDtypeStruct((t, d), BF16)] * 4,
        compiler_params=_cparams())(x1, ya, yb, yc, zg, bg, wa, wb, wc, wo)


def _merge_bwd(dx2, pa, pb, pc, zg, bg, wa, wb, wc, wo, name, ex=None):
    t, d = dx2.shape
    tm = _row_tile(t, 256)

    def body(dx_ref, pa_ref, pb_ref, pc_ref, zg_ref, bg_ref, wa_ref, wb_ref, wc_ref, wo_ref,
             dpa_ref, dpb_ref, dpc_ref, dzg_ref, dbg_ref, dya_ref, dyb_ref, dyc_ref):
        @pl.when(pl.program_id(0) == 0)
        def _():
            dbg_ref[...] = jnp.zeros_like(dbg_ref)

        dm = _dot_nt(dx_ref[...].astype(BF16), wo_ref[...])
        for k, (p_ref, w_ref, dp_ref, dy_ref) in enumerate(
                ((pa_ref, wa_ref, dpa_ref, dya_ref), (pb_ref, wb_ref, dpb_ref, dyb_ref),
                 (pc_ref, wc_ref, dpc_ref, dyc_ref))):
            sl = slice(k * d, (k + 1) * d)
            gate = jax.nn.sigmoid(zg_ref[:, sl] + bg_ref[:, sl])
            dpr = (dm * gate).astype(BF16)
            dp_ref[...] = dpr
            dzg = dm * p_ref[...].astype(F32) * gate * (1.0 - gate)
            dzg_ref[:, sl] = dzg.astype(BF16)
            dbg_ref[:, sl] += jnp.sum(dzg, axis=0, keepdims=True)
            dy_ref[...] = _dot_nt(dpr, w_ref[...]).astype(dy_ref.dtype)

    row = lambda n: pl.BlockSpec((tm, n), lambda i: (i, 0))
    full = lambda a: pl.BlockSpec(a.shape, lambda i: (0, 0))
    na, nb, nc = wa.shape[0], wb.shape[0], wc.shape[0]
    return _call_with_exchange(
        ex, body, name, (t // tm,),
        [row(d), row(d), row(d), row(d), row(3 * d), full(bg), full(wa), full(wb), full(wc), full(wo)],
        [row(d), row(d), row(d), row(3 * d), pl.BlockSpec((1, 3 * d), lambda i: (0, 0)), row(na), row(nb), row(nc)],
        [jax.ShapeDtypeStruct((t, d), BF16)] * 3
        + [jax.ShapeDtypeStruct((t, 3 * d), BF16), jax.ShapeDtypeStruct((1, 3 * d), F32),
           jax.ShapeDtypeStruct((t, na), F32), jax.ShapeDtypeStruct((t, nb), BF16),
           jax.ShapeDtypeStruct((t, nc), BF16)],
        [], (dx2, pa, pb, pc, zg, bg, wa, wb, wc, wo))


def _adamw_math(w, g, m, v):
    bc1 = 1.0 - ADAM_B1 ** ADAM_STEP
    bc2 = 1.0 - ADAM_B2 ** ADAM_STEP
    nm = ADAM_B1 * m + (1.0 - ADAM_B1) * g
    nv = ADAM_B2 * v + (1.0 - ADAM_B2) * (g * g)
    delta = -ADAM_LR * ((nm / bc1) / (jnp.sqrt(nv / bc2) + ADAM_EPS) + ADAM_WD * w)
    return delta, nm, nv


def _div_tile(n, cap, mult):
    best = None
    for cand in range(mult, min(n, cap) + 1, mult):
        if n % cand == 0:
            best = cand
    assert best is not None, (n, cap, mult)
    return best


def _adamw(w, g, m, v, name):
    rows, cols = w.shape
    tr = rows if rows * cols <= 256 * 1024 else _div_tile(rows, 256, 8)

    def body(w_ref, g_ref, m_ref, v_ref, d_ref, nm_ref, nv_ref):
        d_ref[...], nm_ref[...], nv_ref[...] = _adamw_math(w_ref[...], g_ref[...], m_ref[...], v_ref[...])

    blk = pl.BlockSpec((tr, cols), lambda i: (i, 0))
    return pl.pallas_call(
        body, name=name, grid=(rows // tr,), in_specs=[blk] * 4, out_specs=[blk] * 3,
        out_shape=[jax.ShapeDtypeStruct((rows, cols), F32)] * 3, compiler_params=_cparams())(w, g, m, v)


def _adamw_slots(w, slots, m, v, name):
    _, hr, cols = w.shape
    tr = _div_tile(hr, 128, 16)

    def body(w_ref, s_ref, m_ref, v_ref, g_ref, d_ref, nm_ref, nv_ref):
        g = s_ref[0, 0].astype(F32)
        for k in range(1, N_CHIPS):
            g = g + s_ref[0, k].astype(F32)
        g_ref[0] = g
        d_ref[0], nm_ref[0], nv_ref[0] = _adamw_math(w_ref[0], g, m_ref[0], v_ref[0])

    blk = pl.BlockSpec((1, tr, cols), lambda h, i: (h, i, 0))
    return pl.pallas_call(
        body, name=name, grid=(2, hr // tr),
        in_specs=[blk, pl.BlockSpec((1, N_CHIPS, tr, cols), lambda h, i: (h, 0, i, 0)), blk, blk],
        out_specs=[blk] * 4, out_shape=[jax.ShapeDtypeStruct((2, hr, cols), F32)] * 4,
        compiler_params=_cparams())(w, slots, m, v)


ANY = pl.BlockSpec(memory_space=pl.ANY)


def _place():
    x, y, c = lax.axis_index("x"), lax.axis_index("y"), lax.axis_index("c")
    other_chips = [(1 - x, y), (x, 1 - y), (1 - x, 1 - y)]
    return x, y, c, other_chips


def _remote(src, dst, send_sem, recv_sem, to):
    return pltpu.make_async_remote_copy(src_ref=src, dst_ref=dst, send_sem=send_sem, recv_sem=recv_sem,
                                        device_id=to, device_id_type=MESH)


PIECE_BYTES = 384 * 1024


def _row_pieces(half_rows, cols):
    for n in (4, 2):
        if half_rows % (16 * n) == 0 and half_rows * cols * 2 // n >= PIECE_BYTES:
            return [pl.ds(k * (half_rows // n), half_rows // n) for k in range(n)]
    return [pl.ds(0, half_rows)]


def _pieces(arrays, rows_axis):
    return [(w, rows) for w, a in enumerate(arrays) for rows in _row_pieces(a.shape[rows_axis], a.shape[-1])]


def _gather_exchange(shards):
    nw = len(shards)
    pieces = _pieces(shards, 1)
    npc = len(pieces)

    def build(s_refs, g_refs, sems):
        send_sems, recv_sems, local_sems = sems
        x, y, c, chips = _place()
        me = 2 * x + y
        sibling = (x, y, 1 - c)
        mine = [pltpu.make_async_copy(s_refs[w], g_refs[w].at[me], local_sems.at[w]) for w in range(nw)]
        first = [_remote(s_refs[w].at[c, rows], g_refs[w].at[me, c, rows], send_sems.at[k, p], recv_sems.at[k, p],
                         (cx, cy, c)) for k, (cx, cy) in enumerate(chips) for p, (w, rows) in enumerate(pieces)]

        def start():
            for cp in mine + first:
                cp.start()

        def finish():
            passed = []
            for k, (cx, cy) in enumerate(chips):
                for p, (w, rows) in enumerate(pieces):
                    slab = g_refs[w].at[2 * cx + cy, c, rows]
                    _remote(slab, slab, send_sems.at[k, p], recv_sems.at[k, p], (cx, cy, c)).wait_recv()
                    fwd = _remote(slab, slab, send_sems.at[3 + k, p], recv_sems.at[3 + k, p], sibling)
                    fwd.start()
                    passed.append(fwd)
            for k, (cx, cy) in enumerate(chips):
                for p, (w, rows) in enumerate(pieces):
                    slab = g_refs[w].at[2 * cx + cy, 1 - c, rows]
                    _remote(slab, slab, send_sems.at[3 + k, p], recv_sems.at[3 + k, p], sibling).wait_recv()
            for cp in first + passed:
                cp.wait_send()
            for cp in mine:
                cp.wait()

        return start, finish

    return _Exchange(list(shards), [jax.ShapeDtypeStruct((N_CHIPS,) + s.shape, BF16) for s in shards],
                     [pltpu.SemaphoreType.DMA((6, npc)), pltpu.SemaphoreType.DMA((6, npc)),
                      pltpu.SemaphoreType.DMA((nw,))], build)


def _swap_halves(grads, name):
    nw = len(grads)

    def body(*refs):
        g_refs, sib_refs = refs[:nw], refs[nw:2 * nw]
        send_sems, recv_sems = refs[2 * nw:]
        x, y, c, _ = _place()
        copies = [_remote(g_refs[w].at[s, 1 - c], sib_refs[w].at[s], send_sems.at[s, w], recv_sems.at[s, w],
                          (x, y, 1 - c)) for w in range(nw) for s in range(N_CHIPS)]
        for cp in copies:
            cp.start()
        for cp in copies:
            cp.wait_recv()
        for cp in copies:
            cp.wait_send()

    return pl.pallas_call(
        body, name=name, in_specs=[ANY] * nw, out_specs=[ANY] * nw,
        out_shape=[jax.ShapeDtypeStruct((N_CHIPS,) + g.shape[2:], BF16) for g in grads],
        scratch_shapes=[pltpu.SemaphoreType.DMA((N_CHIPS, nw)), pltpu.SemaphoreType.DMA((N_CHIPS, nw))],
    )(*grads)


def _pair_sum(grad, sib, core, name):
    nchip, _, hr, cols = grad.shape
    tr = _div_tile(hr, 256, 16)

    def body(core_ref, a_ref, b_ref, o_ref):
        o_ref[...] = (a_ref[0].astype(F32) + b_ref[...].astype(F32)).astype(BF16)

    return pl.pallas_call(
        body, name=name,
        grid_spec=pltpu.PrefetchScalarGridSpec(
            num_scalar_prefetch=1, grid=(nchip, hr // tr),
            in_specs=[pl.BlockSpec((1, 1, tr, cols), lambda s, i, core_r: (s, core_r[0], i, 0)),
                      pl.BlockSpec((1, tr, cols), lambda s, i, core_r: (s, i, 0))],
            out_specs=pl.BlockSpec((1, tr, cols), lambda s, i, core_r: (s, i, 0))),
        out_shape=jax.ShapeDtypeStruct((nchip, hr, cols), BF16), compiler_params=_cparams())(core, grad, sib)


def _pair_sum_exchange(sums):
    nw = len(sums)
    pieces = _pieces(sums, 1)
    npc = len(pieces)

    def build(p_refs, o_refs, sems):
        send_sems, recv_sems, local_sems = sems
        x, y, c, chips = _place()
        me = 2 * x + y
        sibling = (x, y, 1 - c)
        mine = [pltpu.make_async_copy(p_refs[w].at[me], o_refs[w].at[c, 3], local_sems.at[w]) for w in range(nw)]
        first = [_remote(p_refs[w].at[2 * cx + cy, rows], o_refs[w].at[c, k, rows], send_sems.at[k, p],
                         recv_sems.at[k, p], (cx, cy, c))
                 for k, (cx, cy) in enumerate(chips) for p, (w, rows) in enumerate(pieces)]

        def start():
            for cp in mine + first:
                cp.start()

        def finish():
            passed = []
            for k in range(N_CHIPS):
                own_waited = set()
                for p, (w, rows) in enumerate(pieces):
                    slab = o_refs[w].at[c, k, rows]
                    if k < 3:
                        first[k * npc + p].wait_recv()
                    elif w not in own_waited:
                        mine[w].wait()
                        own_waited.add(w)
                    fwd = _remote(slab, slab, send_sems.at[3 + k, p], recv_sems.at[3 + k, p], sibling)
                    fwd.start()
                    passed.append(fwd)
            for k in range(N_CHIPS):
                for p, (w, rows) in enumerate(pieces):
                    slab = o_refs[w].at[1 - c, k, rows]
                    _remote(slab, slab, send_sems.at[3 + k, p], recv_sems.at[3 + k, p], sibling).wait_recv()
            for cp in first + passed:
                cp.wait_send()

        return start, finish

    return _Exchange(list(sums), [jax.ShapeDtypeStruct((2,) + p.shape, BF16) for p in sums],
                     [pltpu.SemaphoreType.DMA((7, npc)), pltpu.SemaphoreType.DMA((7, npc)),
                      pltpu.SemaphoreType.DMA((nw,))], build)


def _small_sum_exchange(vec):
    m_per, n = vec.shape

    def build(ins, outs, scr):
        (x_ref,), (out_ref,) = ins, outs
        gath_ref, sum_ref, send_sems, recv_sems, local_sem, out_sem = scr
        x, y, c, chips = _place()
        me, sibling = (x, y, c), (x, y, 1 - c)

        def rows(px, py, pc):
            return gath_ref.at[pl.ds((4 * px + 2 * py + pc) * m_per, m_per), :]

        def copy(k, block, to, src=None):
            return pltpu.make_async_remote_copy(
                src_ref=rows(*block) if src is None else src, dst_ref=rows(*block),
                send_sem=send_sems.at[k], recv_sem=recv_sems.at[k], device_id=to, device_id_type=MESH)

        mine = pltpu.make_async_copy(x_ref, rows(*me), local_sem)
        first = [copy(0, me, sibling, src=x_ref)] + [copy(1 + j, me, (*chip, c), src=x_ref)
                                                     for j, chip in enumerate(chips)]

        def start():
            for cp in [mine] + first:
                cp.start()

        def finish():
            passed = [copy(4 + j, (*chip, c), sibling) for j, chip in enumerate(chips)]
            for j, chip in enumerate(chips):
                copy(1 + j, (*chip, c), me).wait_recv()
                passed[j].start()
            copy(0, sibling, me).wait_recv()
            for j, chip in enumerate(chips):
                copy(4 + j, (*chip, 1 - c), me).wait_recv()
            for cp in first + passed:
                cp.wait_send()
            mine.wait()
            acc = gath_ref[pl.ds(0, m_per), :]
            for k in range(1, N_DEV):
                acc = acc + gath_ref[pl.ds(k * m_per, m_per), :]
            sum_ref[...] = acc
            done = pltpu.make_async_copy(sum_ref, out_ref, out_sem)
            done.start()
            done.wait()

        return start, finish

    return _Exchange([vec], [jax.ShapeDtypeStruct((m_per, n), F32)],
                     [pltpu.VMEM((N_DEV * m_per, n), F32), pltpu.VMEM((m_per, n), F32), pltpu.SemaphoreType.DMA((7,)),
                      pltpu.SemaphoreType.DMA((7,)), pltpu.SemaphoreType.DMA, pltpu.SemaphoreType.DMA], build)


def _pack_small(vals, tail=()):
    flat = jnp.concatenate([vals[name].reshape(-1).astype(F32) for name, _ in SMALL] + [v.reshape(1) for v in tail])
    flat = jnp.pad(flat, (0, SMALL_ROWS * LANES - flat.shape[0]))
    return flat.reshape(SMALL_ROWS, LANES)


def _unpack_small(packed):
    flat = packed.reshape(-1)
    out, off = {}, 0
    for name, shape in SMALL:
        n = int(np.prod(shape))
        out[name] = flat[off:off + n].reshape(shape)
        off += n
    return out


def _head_pad_cols(w, heads, real):
    k = w.shape[0]
    return jnp.pad(w.reshape(k, heads, real), ((0, 0), (0, 0), (0, LANES - real))).reshape(k, heads * LANES)


def _rope_tables(positions):
    half = MLA_ROPE // 2
    inv = ROPE_BASE ** (-jnp.arange(half, dtype=F32) / half)
    ang = positions.astype(F32)[:, None] * inv
    cos, sin = jnp.cos(ang), jnp.sin(ang)
    t = positions.shape[0]
    z = lambda n: jnp.zeros((t, n), F32)
    rc = jnp.concatenate([jnp.ones((t, MLA_NOPE), F32), cos, cos, z(LANES - MLA_QK)], axis=1)
    rs1 = jnp.concatenate([z(MLA_NOPE), -sin, z(LANES - MLA_NOPE - half)], axis=1)
    rs2 = jnp.concatenate([z(MLA_NOPE + half), sin, z(LANES - MLA_QK)], axis=1)
    return rc, rs1, rs2


FFN1_WEIGHTS = ("ffn1_w_gu", "ffn1_w_down")
FFN2_WEIGHTS = ("ffn2_w_gu", "ffn2_w_down")
MIXER_WEIGHTS = tuple(n for n, *_ in SHARDED if n not in FFN1_WEIGHTS + FFN2_WEIGHTS)
SHARD_SHAPE = {n: (r, c, kind) for n, r, c, kind in SHARDED}


def _from_blocks(name, gathered):
    r, c, kind = SHARD_SHAPE[name]
    blk = gathered.reshape(N_CHIPS, r, c)
    return blk, (blk.transpose(1, 0, 2).reshape(r, N_CHIPS * c) if kind == "col" else blk.reshape(N_CHIPS * r, c))


def _grad_pair_sums(names, gw, core, tag):
    by_owner = []
    for name in names:
        r, c, kind = SHARD_SHAPE[name]
        blk = gw[name].reshape(r, N_CHIPS, c).transpose(1, 0, 2) if kind == "col" else gw[name].reshape(N_CHIPS, r, c)
        by_owner.append(blk.astype(BF16).reshape(N_CHIPS, 2, r // 2, c))
    received = _swap_halves(by_owner, "grad_swap_" + tag)
    return [_pair_sum(g, s, core, "pair_sum_" + n) for g, s, n in zip(by_owner, received, names)]


def _device_step(x, mem, positions, tgt, small, shards, core):
    d = D_MODEL
    g_ffn1, g_mix, g_ffn2 = small["ffn1_norm"], small["mix_norm"], small["ffn2_norm"]
    big = {}
    for name, g in zip(FFN1_WEIGHTS, _run_exchange(_gather_exchange([shards[n] for n in FFN1_WEIGHTS]), "gather_ffn1")):
        big[name + "#blocks"], big[name] = _from_blocks(name, g)
    wgu1, wd1 = big["ffn1_w_gu#blocks"], big["ffn1_w_down"].reshape(2, FF_TILE, d)
    x1, gpre1, upre1, h, *rest = _ffn_fwd(x, g_ffn1, wgu1, wd1, "ffn1_fwd", next_gain=g_mix,
                                          ex=_gather_exchange([shards[n] for n in MIXER_WEIGHTS]))
    for name, g in zip(MIXER_WEIGHTS, rest):
        big[name + "#blocks"], big[name] = _from_blocks(name, g)
    w_in = big["w_in"]
    w_uv_, w_cq, w_ckv = w_in[:, :COL_CQ], w_in[:, COL_CQ:COL_CKV], w_in[:, COL_CKV:COL_KR]
    w_kr = jnp.pad(w_in[:, COL_KR:COL_QM], ((0, 0), (MLA_NOPE, LANES - MLA_QK)))
    w_qm, w_g = w_in[:, COL_QM:COL_GATE], w_in[:, COL_GATE:]
    segs = (w_uv_, w_cq, w_ckv, w_kr, w_qm, w_g)
    wuq = _head_pad_cols(big["mla_w_uq"], MLA_HEADS, MLA_QK)
    ukv = big["mla_w_ukv"].reshape(MLA_KV_RANK, MLA_HEADS, 2, MLA_NOPE)
    wuk = _head_pad_cols(ukv[:, :, 0].reshape(MLA_KV_RANK, -1), MLA_HEADS, MLA_NOPE)
    wuv = _head_pad_cols(ukv[:, :, 1].reshape(MLA_KV_RANK, -1), MLA_HEADS, MLA_NOPE)
    wkv = big["mem_w_kv"]
    wa, wc, wo = big["w_branch_a"], big["w_branch_c"], big["w_out"]
    wb = jnp.pad(big["w_branch_b"].reshape(MLA_HEADS, MLA_NOPE, d),
                 ((0, 0), (0, LANES - MLA_NOPE), (0, 0))).reshape(MLA_HEADS * LANES, d)
    qg = jnp.pad(small["mla_q_norm"], ((0, 0), (0, LANES - MLA_QK)))
    kg = jnp.pad(small["mla_k_norm"], ((0, 0), (0, LANES - MLA_QK)))
    causal = jnp.tril(jnp.ones((CHUNK, CHUNK), bool))
    wt_f = jnp.where(causal[None], small["sg_w"][0], 0.0)
    wt, wt_t = wt_f.astype(BF16), wt_f.transpose(0, 2, 1).astype(BF16)
    bias_l = jnp.repeat(small["sg_b"][0].T, 64, axis=1)
    rc, rs1, rs2 = _rope_tables(positions)

    zuv, zcq, zckv, zkr, zqm, zg, *rest = _mm_cols(h, segs, [F32] * 5 + [BF16], "in_proj",
                                                   ex=_gather_exchange([shards[n] for n in FFN2_WEIGHTS]))
    for name, g in zip(FFN2_WEIGHTS, rest):
        big[name + "#blocks"], big[name] = _from_blocks(name, g)
    wgu2, wd2 = big["ffn2_w_gu#blocks"], big["ffn2_w_down"].reshape(2, FF_TILE, d)
    ya = _sgu_fwd(zuv, small["sg_ln_g"], small["sg_ln_b"], wt, bias_l, "sgu_fwd")
    q, k, v, cqn, ckvn = _mla_prep_fwd(zcq, zckv, zkr, small["mla_cq_norm"], small["mla_ckv_norm"], qg, kg,
                                       wuq, wuk, wuv, rc, rs1, rs2, "mla_prep_fwd")
    yb, lse = _attn_fwd(q, k, v, "mla_attn_fwd")
    km, vm, memn = _mem_kv_fwd(mem, small["mem_norm"], wkv, small["mem_k_norm"], "mem_kv_fwd")
    yc = _mem_attn_fwd(zqm, small["mem_q_norm"], km, vm, "mem_attn_fwd")
    x2, merged, pa, pb, pc = _merge_fwd(x1, ya, yb, yc, zg, small["b_gate"], wa, wb, wc, wo, "merge_fwd")
    dy, loss_row, gpre2, upre2 = _ffn_fwd(x2, g_ffn2, wgu2, wd2, "ffn2_fwd", target=tgt)

    gw, gs, slots = {}, {}, {}

    def ffn_grads(prefix, xin, gain, dyin, gpre, upre, wgu, wd, ex=None, ex_names=(), last=False):
        dx, dgain, xn, dgt, dup, act, *got = _ffn_bwd(xin, gain, dyin, gpre, upre, wgu, wd, prefix + "_bwd", ex=ex)
        slots.update(zip(ex_names, got))
        gs[prefix + "_norm"] = dgain
        gw[prefix + "_w_gu"] = jnp.concatenate(
            [_mm_tn(xn, dgt, prefix + "_dwg"), _mm_tn(xn, dup, prefix + "_dwu")], axis=1)
        if last:
            small_sum = _small_sum_exchange(_pack_small(gs, tail=[loss_row[0, 0]]))
            gw[prefix + "_w_down"], summed = _mm_tn(act, dyin, prefix + "_dwd", scale=0.5, ex=small_sum)
            return dx, summed
        gw[prefix + "_w_down"] = _mm_tn(act, dyin, prefix + "_dwd", scale=0.5)
        return dx

    dx2 = ffn_grads("ffn2", x2, g_ffn2, dy, gpre2, upre2, wgu2, wd2)
    ffn2_sums = _pair_sum_exchange(_grad_pair_sums(FFN2_WEIGHTS, gw, core, "ffn2"))
    dpa, dpb, dpc, dzg, dbg, dya, dyb, dyc, *got = _merge_bwd(dx2, pa, pb, pc, zg, small["b_gate"], wa, wb, wc, wo,
                                                              "merge_bwd", ex=ffn2_sums)
    slots.update(zip(FFN2_WEIGHTS, got))
    gs["b_gate"] = dbg
    gw["w_out"] = _mm_tn(merged, dx2, "dw_out")
    gw["w_branch_a"] = _mm_tn(ya, dpa, "dw_branch_a")
    gw["w_branch_b"] = _mm_tn(yb, dpb, "dw_branch_b").reshape(MLA_HEADS, LANES, d)[:, :MLA_NOPE].reshape(-1, d)
    gw["w_branch_c"] = _mm_tn(yc, dpc, "dw_branch_c")

    dzuv, dwt, dbl, dlg, dlb = _sgu_bwd(zuv, dya, small["sg_ln_g"], small["sg_ln_b"], wt, wt_t, bias_l, "sgu_bwd")
    gs["sg_w"], gs["sg_b"] = dwt[None], dbl[:, :SG_GROUPS].T[None]
    gs["sg_ln_g"], gs["sg_ln_b"] = dlg, dlb

    delta_rows, lse_rows = _attn_bwd_rows(yb, lse, dyb, "mla_attn_bwd_rows")
    dq, dk, dv = _attn_bwd(q, k, v, delta_rows, lse_rows, dyb, "mla_attn_bwd")
    dzcq, dzckv, dzkr, dql, dkl, dgcq, dgckv, dqg, dkg = _mla_prep_bwd(
        zcq, zckv, zkr, small["mla_cq_norm"], small["mla_ckv_norm"], qg, kg, wuq, wuk, wuv, rc, rs1, rs2,
        dq, dk, dv, "mla_prep_bwd")
    gs["mla_cq_norm"], gs["mla_ckv_norm"] = dgcq, dgckv
    gs["mla_q_norm"], gs["mla_k_norm"] = dqg[:, :MLA_QK], dkg[:, :MLA_QK]
    gw["mla_w_uq"] = _mm_tn(cqn, dql, "dw_uq").reshape(MLA_Q_RANK, MLA_HEADS, LANES)[:, :, :MLA_QK].reshape(
        MLA_Q_RANK, -1)
    dwuk = _mm_tn(ckvn, dkl, "dw_uk").reshape(MLA_KV_RANK, MLA_HEADS, LANES)[:, :, :MLA_NOPE]
    dwuv = _mm_tn(ckvn, dv, "dw_uv").reshape(MLA_KV_RANK, MLA_HEADS, LANES)[:, :, :MLA_NOPE]
    gw["mla_w_ukv"] = jnp.concatenate([dwuk, dwuv], axis=2).reshape(MLA_KV_RANK, -1)

    dzqm, dkn, dvm, dmqg = _mem_attn_bwd(zqm, dyc, small["mem_q_norm"], km, vm, "mem_attn_bwd")
    gs["mem_q_norm"] = dmqg
    gw["mem_w_kv"], gs["mem_k_norm"], gs["mem_norm"] = _mem_kv_bwd(
        mem, small["mem_norm"], wkv, small["mem_k_norm"], dkn, dvm, "mem_kv_bwd")

    dzs = (dzuv, dzcq, dzckv, dzkr, dzqm, dzg)
    dws = list(_mm_tn_cols(h, dzs[:5], "dw_in_narrow")) + [_mm_tn(h, dzg, "dw_in_gate")]
    dws[3] = dws[3][:, MLA_NOPE:MLA_QK]
    gw["w_in"] = jnp.concatenate(dws, axis=1)
    dx1, gs["mix_norm"] = _proj_norm_bwd(dzs, [w.T for w in segs], x1, g_mix, dx2, "in_proj_bwd")
    mixer_sums = _pair_sum_exchange(_grad_pair_sums(MIXER_WEIGHTS, gw, core, "mixer"))
    dx, summed = ffn_grads("ffn1", x, g_ffn1, dx1, gpre1, upre1, wgu1, wd1, ex=mixer_sums, ex_names=MIXER_WEIGHTS,
                           last=True)
    ffn1_sums = _pair_sum_exchange(_grad_pair_sums(FFN1_WEIGHTS, gw, core, "ffn1"))
    slots.update(zip(FFN1_WEIGHTS, _run_exchange(ffn1_sums, "grad_exchange_ffn1")))
    return dx, slots, summed


def kernel(x, mem, positions, ffn1_norm, ffn1_w_gu, ffn1_w_down, mix_norm, w_in, b_gate, sg_ln_g, sg_ln_b, sg_w, sg_b, mla_cq_norm, mla_w_uq, mla_ckv_norm, mla_w_ukv, mla_q_norm, mla_k_norm, mem_norm, mem_w_kv, mem_q_norm, mem_k_norm, w_branch_a, w_branch_b, w_branch_c, w_out, ffn2_norm, ffn2_w_gu, ffn2_w_down, loss_target, m_ffn1_norm, m_ffn1_w_gu, m_ffn1_w_down, m_mix_norm, m_w_in, m_b_gate, m_sg_ln_g, m_sg_ln_b, m_sg_w, m_sg_b, m_mla_cq_norm, m_mla_w_uq, m_mla_ckv_norm, m_mla_w_ukv, m_mla_q_norm, m_mla_k_norm, m_mem_norm, m_mem_w_kv, m_mem_q_norm, m_mem_k_norm, m_w_branch_a, m_w_branch_b, m_w_branch_c, m_w_out, m_ffn2_norm, m_ffn2_w_gu, m_ffn2_w_down, v_ffn1_norm, v_ffn1_w_gu, v_ffn1_w_down, v_mix_norm, v_w_in, v_b_gate, v_sg_ln_g, v_sg_ln_b, v_sg_w, v_sg_b, v_mla_cq_norm, v_mla_w_uq, v_mla_ckv_norm, v_mla_w_ukv, v_mla_q_norm, v_mla_k_norm, v_mem_norm, v_mem_w_kv, v_mem_q_norm, v_mem_k_norm, v_w_branch_a, v_w_branch_b, v_w_branch_c, v_w_out, v_ffn2_norm, v_ffn2_w_gu, v_ffn2_w_down):
    args = dict(locals())
    weights = {n: args[n] for n in WEIGHT_ORDER}
    mom_m = {n: args["m_" + n] for n in WEIGHT_ORDER}
    mom_v = {n: args["v_" + n] for n in WEIGHT_ORDER}
    small = {n: weights[n] for n, _ in SMALL}
    halves = lambda a, r, c: a.reshape(2, r // 2, c)

    shards = {n: halves(weights[n][0].astype(BF16), r, c) for n, r, c, _ in SHARDED}
    core = lax.axis_index("c").astype(jnp.int32).reshape(1)
    dx, slots, summed = _device_step(x[0], mem[0], positions[0], loss_target[0], small, shards, core)
    loss = summed.reshape(-1)[_N_SMALL]
    small_grads = _unpack_small(summed)

    grads, deltas, new_m, new_v = {}, {}, {}, {}
    for name, r, c, _ in SHARDED:
        outs = _adamw_slots(halves(weights[name][0], r, c), slots[name], halves(mom_m[name][0], r, c),
                            halves(mom_v[name][0], r, c), "adamw_" + name)
        shape = weights[name].shape
        grads[name], deltas[name], new_m[name], new_v[name] = [o.reshape(shape) for o in outs]
    dlt, nm, nv = _adamw(_pack_small(small), _pack_small(small_grads), _pack_small({n: mom_m[n] for n, _ in SMALL}),
                         _pack_small({n: mom_v[n] for n, _ in SMALL}), "adamw_small")
    for name, _ in SMALL:
        grads[name] = small_grads[name]
    deltas.update(_unpack_small(dlt))
    new_m.update(_unpack_small(nm))
    new_v.update(_unpack_small(nv))

    return (loss, dx[None], *[grads[n] for n in WEIGHT_ORDER], *[deltas[n] for n in WEIGHT_ORDER],
            *[new_m[n] for n in WEIGHT_ORDER], *[new_v[n] for n in WEIGHT_ORDER])
```

```python
import functools
from typing import Callable, NamedTuple

import numpy as np
import jax
import jax.numpy as jnp
from jax import lax
from jax.experimental import pallas as pl
from jax.experimental.pallas import tpu as pltpu

F32 = jnp.float32
BF16 = jnp.bfloat16

D_MODEL = 1024
D_FF = 2816
FF_TILE = 1408
SG_WIDTH = 512
SG_GROUPS = 8
CHUNK = 128
MLA_HEADS = 8
MLA_QK = 96
MLA_NOPE = 64
MLA_ROPE = 32
MLA_Q_RANK = 384
MLA_KV_RANK = 256
MEM_HEADS = 4
MEM_LEN = 256
LANES = 128
EPS = 1e-6
NEG = -1e30
ROPE_BASE = 10000.0
N_CHIPS = 4
N_DEV = 8

ADAM_LR = 0.001
ADAM_B1 = 0.9
ADAM_B2 = 0.999
ADAM_EPS = 1e-08
ADAM_WD = 0.01
ADAM_STEP = 10

COL_V = 512
COL_CQ = 1024
COL_CKV = 1408
COL_KR = 1664
COL_QM = 1696
COL_GATE = 2208
IN_COLS = 5280

VMEM_LIMIT_BYTES = 56 * 1024 * 1024
INV_SQRT2 = 0.7071067811865476
INV_SQRT_2PI = 0.3989422804014327
LOG2E = 1.4426950408889634
ATTN_SCALE = MLA_QK ** -0.5
V_ONES_LANE = 64
ATTN_SCALE2 = ATTN_SCALE * LOG2E

SHARDED = (
    ("ffn1_w_gu", 1024, 1408, "col"),
    ("ffn1_w_down", 704, 1024, "row"),
    ("w_in", 1024, 1320, "col"),
    ("mla_w_uq", 384, 192, "col"),
    ("mla_w_ukv", 256, 256, "col"),
    ("mem_w_kv", 256, 1024, "row"),
    ("w_branch_a", 512, 256, "col"),
    ("w_branch_b", 512, 256, "col"),
    ("w_branch_c", 512, 256, "col"),
    ("w_out", 256, 1024, "row"),
    ("ffn2_w_gu", 1024, 1408, "col"),
    ("ffn2_w_down", 704, 1024, "row"),
)
SMALL = (
    ("ffn1_norm", (1, 1024)), ("mix_norm", (1, 1024)), ("b_gate", (1, 3072)),
    ("sg_ln_g", (1, 512)), ("sg_ln_b", (1, 512)), ("sg_w", (1, 8, 128, 128)),
    ("sg_b", (1, 8, 128)), ("mla_cq_norm", (1, 384)), ("mla_ckv_norm", (1, 256)),
    ("mla_q_norm", (1, 96)), ("mla_k_norm", (1, 96)), ("mem_norm", (1, 1024)),
    ("mem_q_norm", (1, 128)), ("mem_k_norm", (1, 128)), ("ffn2_norm", (1, 1024)),
)
WEIGHT_ORDER = (
    "ffn1_norm", "ffn1_w_gu", "ffn1_w_down", "mix_norm", "w_in", "b_gate", "sg_ln_g", "sg_ln_b",
    "sg_w", "sg_b", "mla_cq_norm", "mla_w_uq", "mla_ckv_norm", "mla_w_ukv", "mla_q_norm",
    "mla_k_norm", "mem_norm", "mem_w_kv", "mem_q_norm", "mem_k_norm", "w_branch_a", "w_branch_b",
    "w_branch_c", "w_out", "ffn2_norm", "ffn2_w_gu", "ffn2_w_down",
)

_N_SMALL = sum(int(np.prod(s)) for _, s in SMALL)
SMALL_ROWS = -(-_N_SMALL // (LANES * 8)) * 8

MESH = pl.DeviceIdType.MESH


def _cparams():
    return pltpu.CompilerParams(vmem_limit_bytes=VMEM_LIMIT_BYTES)


def _dot(a, b):
    return jnp.dot(a, b, preferred_element_type=F32)


def _dot_nt(a, b):
    return lax.dot_general(a, b, (((1,), (1,)), ((), ())), preferred_element_type=F32)


def _dot_tn(a, b):
    return lax.dot_general(a, b, (((0,), (0,)), ((), ())), preferred_element_type=F32)


def _gelu(x):
    return 0.5 * x * (1.0 + lax.erf(x * INV_SQRT2))


def _gelu_grad(x):
    return 0.5 * (1.0 + lax.erf(x * INV_SQRT2)) + x * jnp.exp(-0.5 * x * x) * INV_SQRT_2PI


def _rstd(x, n):
    return lax.rsqrt(jnp.sum(x * x, axis=-1, keepdims=True) * (1.0 / n) + EPS)


def _rms_vjp(x, r, g, dy, n):
    dxh = dy * g
    dx = r * dxh - x * (r * r * r) * (jnp.sum(dxh * x, axis=-1, keepdims=True) * (1.0 / n))
    return dx, dy * x * r


def _row_tile(t, want):
    return min(t, want)


def _wide_tile(n):
    if n <= 1024:
        return n
    if n % 1024 == 0:
        return 1024
    assert n % FF_TILE == 0, n
    return FF_TILE


def _mm_cols(a, ws, out_dtypes, name, ex=None):
    t, kdim = a.shape
    tm = _row_tile(t, 256)
    n = len(ws)

    def body(*refs):
        av = refs[0][...]
        for w_ref, o_ref in zip(refs[1:1 + n], refs[1 + n:]):
            o_ref[...] = _dot(av, w_ref[...]).astype(o_ref.dtype)

    row = lambda width: pl.BlockSpec((tm, width), lambda i: (i, 0))
    return _call_with_exchange(
        ex, body, name, (t // tm,),
        [row(kdim)] + [pl.BlockSpec(w.shape, lambda i: (0, 0)) for w in ws],
        [row(w.shape[1]) for w in ws],
        [jax.ShapeDtypeStruct((t, w.shape[1]), dt) for w, dt in zip(ws, out_dtypes)], [], (a, *ws))


def _proj_norm_bwd(dzs, wts, x, g, dres, name):
    t, d = x.shape
    tm = _row_tile(t, 256)
    n = len(dzs)

    def body(*refs):
        x_ref, g_ref, r_ref, dx_ref, dg_ref = refs[2 * n:]

        @pl.when(pl.program_id(0) == 0)
        def _():
            dg_ref[...] = jnp.zeros_like(dg_ref)

        dh = None
        for dz_ref, w_ref in zip(refs[:n], refs[n:2 * n]):
            part = _dot(dz_ref[...], w_ref[...])
            dh = part if dh is None else dh + part
        xv = x_ref[...]
        dx, dgr = _rms_vjp(xv, _rstd(xv, d), g_ref[...], dh, d)
        dx_ref[...] = r_ref[...] + dx
        dg_ref[...] += jnp.sum(dgr, axis=0, keepdims=True)

    row = lambda width: pl.BlockSpec((tm, width), lambda i: (i, 0))
    vec = pl.BlockSpec((1, d), lambda i: (0, 0))
    return pl.pallas_call(
        body, name=name, grid=(t // tm,),
        in_specs=[row(dz.shape[1]) for dz in dzs] + [pl.BlockSpec(w.shape, lambda i: (0, 0)) for w in wts]
        + [row(d), vec, row(d)],
        out_specs=[row(d), vec],
        out_shape=[jax.ShapeDtypeStruct((t, d), F32), jax.ShapeDtypeStruct((1, d), F32)],
        compiler_params=_cparams())(*dzs, *wts, x, g, dres)


def _mm_tn_cols(a, bs, name):
    t, m = a.shape
    tk = _row_tile(t, 512)
    n = len(bs)

    def body(*refs):
        @pl.when(pl.program_id(0) == 0)
        def _():
            for o_ref in refs[1 + n:]:
                o_ref[...] = jnp.zeros_like(o_ref)

        av = refs[0][...].astype(BF16)
        for b_ref, o_ref in zip(refs[1:1 + n], refs[1 + n:]):
            o_ref[...] += _dot_tn(av, b_ref[...].astype(BF16))

    row = lambda width: pl.BlockSpec((tk, width), lambda k: (k, 0))
    return pl.pallas_call(
        body, name=name, grid=(t // tk,), in_specs=[row(m)] + [row(b.shape[1]) for b in bs],
        out_specs=[pl.BlockSpec((m, b.shape[1]), lambda k: (0, 0)) for b in bs],
        out_shape=[jax.ShapeDtypeStruct((m, b.shape[1]), F32) for b in bs],
        compiler_params=_cparams())(a, *bs)


def _mm_tn(a, b, name, scale=1.0, ex=None, col_blocks=False, out_dtype=F32):
    t, m = a.shape
    n = b.shape[1]
    tm, tn = _wide_tile(m), _wide_tile(n)
    tk = _row_tile(t, 1024)
    nk = t // tk
    in_place = out_dtype == F32

    def body(a_ref, b_ref, o_ref, *scr):
        k = pl.program_id(2)
        acc_ref = o_ref if in_place else scr[0]

        @pl.when(k == 0)
        def _():
            acc_ref[...] = jnp.zeros_like(acc_ref)

        prod = _dot_tn(a_ref[...].astype(BF16), b_ref[...].astype(BF16))
        acc_ref[...] += prod.reshape(acc_ref.shape)
        if scale != 1.0 or not in_place:
            @pl.when(k == nk - 1)
            def _():
                o_ref[...] = (acc_ref[...] * scale).astype(out_dtype).reshape(o_ref.shape)

    if col_blocks:
        out_spec = pl.BlockSpec((1, tm, tn), lambda i, j, k: (j, i, 0))
        out_shape = jax.ShapeDtypeStruct((n // tn, m, tn), out_dtype)
    else:
        out_spec = pl.BlockSpec((tm, tn), lambda i, j, k: (i, j))
        out_shape = jax.ShapeDtypeStruct((m, n), out_dtype)
    outs = _call_with_exchange(
        ex, body, name, (m // tm, n // tn, nk),
        [pl.BlockSpec((tk, tm), lambda i, j, k: (k, i)), pl.BlockSpec((tk, tn), lambda i, j, k: (k, j))],
        [out_spec], [out_shape], [] if in_place else [pltpu.VMEM((tm, tn), F32)], (a, b))
    return outs[0] if ex is None else outs


class _Exchange(NamedTuple):
    operands: list
    out_shapes: list
    sem_shapes: list
    build: Callable


def _call_with_exchange(ex, body, name, grid, in_specs, out_specs, out_shape, scratch_shapes, operands):
    if ex is None:
        return pl.pallas_call(body, name=name, grid=grid, in_specs=in_specs, out_specs=out_specs, out_shape=out_shape,
                              scratch_shapes=scratch_shapes, compiler_params=_cparams())(*operands)
    n_in, n_out, n_scr = len(in_specs), len(out_specs), len(scratch_shapes)
    k_in, k_out = len(ex.operands), len(ex.out_shapes)

    def carried(*refs):
        a, b = n_in, n_in + k_in
        c, e = b + n_out, b + n_out + k_out
        f = e + n_scr
        start, finish = ex.build(refs[a:b], refs[c:e], refs[f:])
        steps = [pl.program_id(ax) for ax in range(len(grid))]
        first = functools.reduce(jnp.logical_and, [s == 0 for s in steps])
        last = functools.reduce(jnp.logical_and, [s == n - 1 for s, n in zip(steps, grid)])
        pl.when(first)(start)
        body(*refs[:a], *refs[b:c], *refs[e:f])
        pl.when(last)(finish)

    return pl.pallas_call(
        carried, name=name, grid=grid, in_specs=list(in_specs) + [ANY] * k_in,
        out_specs=list(out_specs) + [ANY] * k_out, out_shape=list(out_shape) + list(ex.out_shapes),
        scratch_shapes=list(scratch_shapes) + list(ex.sem_shapes), compiler_params=_cparams(),
    )(*operands, *ex.operands)


def _run_exchange(ex, name):
    k_in, k_out = len(ex.operands), len(ex.out_shapes)

    def body(*refs):
        start, finish = ex.build(refs[:k_in], refs[k_in:k_in + k_out], refs[k_in + k_out:])
        start()
        finish()

    return pl.pallas_call(body, name=name, in_specs=[ANY] * k_in, out_specs=[ANY] * k_out,
                          out_shape=list(ex.out_shapes), scratch_shapes=list(ex.sem_shapes))(*ex.operands)


def _ffn_fwd(x, g, wgu4, wd2, name, ex=None, next_gain=None, target=None):
    t, d = x.shape
    tm = _row_tile(t, 512)
    assert next_gain is None or target is None
    extra = [a for a in (next_gain, target) if a is not None]

    def body(*refs):
        x_ref, g_ref, wg_ref, wu_ref, wd_ref = refs[:5]
        e_ref = refs[5] if extra else None
        outs, (xn_scr, acc_scr) = refs[5 + len(extra):-2], refs[-2:]
        if target is not None:
            dy_ref, loss_ref, gg_ref, uu_ref = outs
        elif next_gain is not None:
            o_ref, gg_ref, uu_ref, h_ref = outs
        else:
            o_ref, gg_ref, uu_ref = outs
        i, j = pl.program_id(0), pl.program_id(1)

        @pl.when(j == 0)
        def _():
            xv = x_ref[...]
            xn_scr[...] = (xv * _rstd(xv, d) * g_ref[...]).astype(BF16)
            acc_scr[...] = jnp.zeros_like(acc_scr)

        if target is not None:
            @pl.when((i == 0) & (j == 0))
            def _():
                loss_ref[...] = jnp.zeros_like(loss_ref)

        xn = xn_scr[...]
        gg = _dot(xn, wg_ref[0])
        uu = _dot(xn, wu_ref[0])
        gg_ref[...] = gg.astype(BF16)
        uu_ref[...] = uu.astype(BF16)
        act = gg * jax.nn.sigmoid(gg) * uu
        acc_scr[...] += _dot(act.astype(BF16), wd_ref[0])

        @pl.when(j == 1)
        def _():
            y = x_ref[...] + 0.5 * acc_scr[...]
            if target is not None:
                e = y - e_ref[...]
                dy_ref[...] = e * (1.0 / d)
                part = 0.5 * jnp.sum(jnp.sum(e * e, axis=-1, keepdims=True) * (1.0 / d), axis=0, keepdims=True)
                loss_ref[...] += jnp.broadcast_to(part, loss_ref.shape)
            else:
                o_ref[...] = y
                if next_gain is not None:
                    h_ref[...] = (y * _rstd(y, d) * e_ref[...]).astype(BF16)

    row = pl.BlockSpec((tm, d), lambda i, j: (i, 0))
    vec = pl.BlockSpec((1, d), lambda i, j: (0, 0))
    ffb = pl.BlockSpec((tm, FF_TILE), lambda i, j: (i, j))
    f32_rows, bf16_ff = jax.ShapeDtypeStruct((t, d), F32), jax.ShapeDtypeStruct((t, D_FF), BF16)
    if target is not None:
        extra_spec, out_specs = [row], [row, pl.BlockSpec((1, LANES), lambda i, j: (0, 0)), ffb, ffb]
        out_shape = [f32_rows, jax.ShapeDtypeStruct((1, LANES), F32), bf16_ff, bf16_ff]
    elif next_gain is not None:
        extra_spec, out_specs = [vec], [row, ffb, ffb, row]
        out_shape = [f32_rows, bf16_ff, bf16_ff, jax.ShapeDtypeStruct((t, d), BF16)]
    else:
        extra_spec, out_specs, out_shape = [], [row, ffb, ffb], [f32_rows, bf16_ff, bf16_ff]
    return _call_with_exchange(
        ex, body, name, (t // tm, 2),
        [row, vec,
         pl.BlockSpec((1, d, FF_TILE), lambda i, j: (j, 0, 0)),
         pl.BlockSpec((1, d, FF_TILE), lambda i, j: (j + 2, 0, 0)),
         pl.BlockSpec((1, FF_TILE, d), lambda i, j: (j, 0, 0))] + extra_spec,
        out_specs, out_shape,
        [pltpu.VMEM((tm, d), BF16), pltpu.VMEM((tm, d), F32)], (x, g, wgu4, wgu4, wd2, *extra))


def _ffn_bwd(x, g, dy, gpre, upre, wgu4, wd2, name, ex=None):
    t, d = x.shape
    tm = _row_tile(t, 512)

    def body(dy_ref, gg_ref, uu_ref, wgu_hbm, wd_hbm, dg_ref, du_ref, act_ref, part_ref, wg_ref, wu_ref, wd_ref):
        j = pl.program_id(0)

        @pl.when(pl.program_id(1) == 0)
        def _():
            pltpu.sync_copy(wgu_hbm.at[j], wg_ref.at[0])
            pltpu.sync_copy(wgu_hbm.at[j + 2], wu_ref.at[0])
            pltpu.sync_copy(wd_hbm.at[j], wd_ref.at[0])

        gg = gg_ref[...].astype(F32)
        uu = uu_ref[...].astype(F32)
        sg = jax.nn.sigmoid(gg)
        silu = gg * sg
        act_ref[...] = (silu * uu).astype(BF16)
        dyh = (0.5 * dy_ref[...]).astype(BF16)
        dact = _dot_nt(dyh, wd_ref[0])
        du = (dact * silu).astype(BF16)
        dgt = (dact * uu * (sg * (1.0 + gg * (1.0 - sg)))).astype(BF16)
        du_ref[...] = du
        dg_ref[...] = dgt
        part_ref[0] = _dot_nt(dgt, wg_ref[0]) + _dot_nt(du, wu_ref[0])

    row = pl.BlockSpec((tm, d), lambda j, i: (i, 0))
    ffb = pl.BlockSpec((tm, FF_TILE), lambda j, i: (i, j))
    dgt, dup, act, parts, *got = _call_with_exchange(
        ex, body, name, (2, t // tm),
        [row, ffb, ffb, ANY, ANY],
        [ffb, ffb, ffb, pl.BlockSpec((1, tm, d), lambda j, i: (j, i, 0))],
        [jax.ShapeDtypeStruct((t, D_FF), BF16)] * 3 + [jax.ShapeDtypeStruct((2, t, d), F32)],
        [pltpu.VMEM((1, d, FF_TILE), BF16), pltpu.VMEM((1, d, FF_TILE), BF16), pltpu.VMEM((1, FF_TILE, d), BF16)],
        (dy, gpre, upre, wgu4, wd2))

    def norm_body(x_ref, g_ref, p_ref, dy_ref, dx_ref, dgain_ref, xn_ref):
        @pl.when(pl.program_id(0) == 0)
        def _():
            dgain_ref[...] = jnp.zeros_like(dgain_ref)

        xv = x_ref[...]
        r = _rstd(xv, d)
        xn_ref[...] = (xv * r * g_ref[...]).astype(BF16)
        dx, dgr = _rms_vjp(xv, r, g_ref[...], p_ref[0] + p_ref[1], d)
        dx_ref[...] = dy_ref[...] + dx
        dgain_ref[...] += jnp.sum(dgr, axis=0, keepdims=True)

    tn = _row_tile(t, 256)
    nrow = pl.BlockSpec((tn, d), lambda i: (i, 0))
    vec = pl.BlockSpec((1, d), lambda i: (0, 0))
    dx, dgain, xn = pl.pallas_call(
        norm_body, name=name + "_norm", grid=(t // tn,),
        in_specs=[nrow, vec, pl.BlockSpec((2, tn, d), lambda i: (0, i, 0)), nrow],
        out_specs=[nrow, vec, nrow],
        out_shape=[jax.ShapeDtypeStruct((t, d), F32), jax.ShapeDtypeStruct((1, d), F32),
                   jax.ShapeDtypeStruct((t, d), BF16)],
        compiler_params=_cparams())(x, g, parts, dy)
    return [dx, dgain, xn, dgt, dup, act] + got


def _sgu_layernorm(vpre, lg, lb):
    v = _gelu(vpre)
    mu = jnp.mean(v, axis=-1, keepdims=True)
    xc = v - mu
    rstd = lax.rsqrt(jnp.mean(xc * xc, axis=-1, keepdims=True) + EPS)
    xhat = xc * rstd
    return xhat, rstd, xhat * lg + lb


def _sgu_fwd(zuv, lg, lb, wt, bias_l, name):
    t = zuv.shape[0]
    tm = _row_tile(t, 512)

    def body(u_ref, v_ref, lg_ref, lb_ref, wt_ref, bl_ref, o_ref, vln_scr):
        _, _, vln = _sgu_layernorm(v_ref[...], lg_ref[...], lb_ref[...])
        vln_scr[...] = vln.astype(BF16)
        lo = lax.broadcasted_iota(jnp.int32, (CHUNK, LANES), 1) < 64
        for c in range(tm // CHUNK):
            rows = slice(c * CHUNK, (c + 1) * CHUNK)
            for p in range(SG_GROUPS // 2):
                cols = slice(p * LANES, (p + 1) * LANES)
                vp = vln_scr[rows, cols]
                mixed = jnp.where(lo, _dot(wt_ref[2 * p], vp), _dot(wt_ref[2 * p + 1], vp)) + bl_ref[:, cols]
                o_ref[rows, cols] = (_gelu(u_ref[rows, cols]) * mixed).astype(BF16)

    half = lambda k: pl.BlockSpec((tm, SG_WIDTH), lambda i: (i, k))
    vec = pl.BlockSpec((1, SG_WIDTH), lambda i: (0, 0))
    return pl.pallas_call(
        body, name=name, grid=(t // tm,),
        in_specs=[half(0), half(1), vec, vec,
                  pl.BlockSpec((SG_GROUPS, CHUNK, CHUNK), lambda i: (0, 0, 0)),
                  pl.BlockSpec((CHUNK, SG_WIDTH), lambda i: (0, 0))],
        out_specs=pl.BlockSpec((tm, SG_WIDTH), lambda i: (i, 0)),
        out_shape=jax.ShapeDtypeStruct((t, SG_WIDTH), BF16),
        scratch_shapes=[pltpu.VMEM((tm, SG_WIDTH), BF16)],
        compiler_params=_cparams())(zuv, zuv, lg, lb, wt, bias_l)


def _sgu_bwd(zuv, dya, lg, lb, wt, wt_t, bias_l, name):
    t = zuv.shape[0]
    tm = _row_tile(t, 256)
    nsteps = t // tm

    def body(u_ref, v_ref, dy_ref, lg_ref, lb_ref, wt_ref, wtt_ref, bl_ref,
             dz_ref, dwt_ref, dbl_ref, dlg_ref, dlb_ref, vln_scr, dvln_scr, dbacc_scr):
        step = pl.program_id(0)

        @pl.when(step == 0)
        def _():
            dwt_ref[...] = jnp.zeros_like(dwt_ref)
            dlg_ref[...] = jnp.zeros_like(dlg_ref)
            dlb_ref[...] = jnp.zeros_like(dlb_ref)
            dbl_ref[...] = jnp.zeros_like(dbl_ref)
            dbacc_scr[...] = jnp.zeros_like(dbacc_scr)

        vpre = v_ref[...]
        lgv = lg_ref[...]
        xhat, rstd, vln = _sgu_layernorm(vpre, lgv, lb_ref[...])
        vln_scr[...] = vln.astype(BF16)
        lo = lax.broadcasted_iota(jnp.int32, (CHUNK, LANES), 1) < 64
        for c in range(tm // CHUNK):
            rows = slice(c * CHUNK, (c + 1) * CHUNK)
            for p in range(SG_GROUPS // 2):
                cols = slice(p * LANES, (p + 1) * LANES)
                vp = vln_scr[rows, cols]
                mixed = jnp.where(lo, _dot(wt_ref[2 * p], vp), _dot(wt_ref[2 * p + 1], vp)) + bl_ref[:, cols]
                upre = u_ref[rows, cols]
                dyp = dy_ref[rows, cols]
                dz_ref[rows, cols] = (dyp * mixed * _gelu_grad(upre)).astype(BF16)
                dm = dyp * _gelu(upre)
                dbacc_scr[:, cols] += dm
                dlo = jnp.where(lo, dm, 0.0).astype(BF16)
                dhi = jnp.where(lo, 0.0, dm).astype(BF16)
                dvln_scr[rows, cols] = _dot(wtt_ref[2 * p], dlo) + _dot(wtt_ref[2 * p + 1], dhi)
                dwt_ref[2 * p] += _dot_nt(dlo, vp)
                dwt_ref[2 * p + 1] += _dot_nt(dhi, vp)
        dvln = dvln_scr[...]
        dlg_ref[...] += jnp.sum(dvln * xhat, axis=0, keepdims=True)
        dlb_ref[...] += jnp.sum(dvln, axis=0, keepdims=True)
        dxh = dvln * lgv
        dv = rstd * (dxh - jnp.mean(dxh, axis=-1, keepdims=True)
                     - xhat * jnp.mean(dxh * xhat, axis=-1, keepdims=True))
        dz_ref[:, SG_WIDTH:] = (dv * _gelu_grad(vpre)).astype(BF16)

        @pl.when(step == nsteps - 1)
        def _():
            rr = lax.broadcasted_iota(jnp.int32, (CHUNK, CHUNK), 0)
            cc = lax.broadcasted_iota(jnp.int32, (CHUNK, CHUNK), 1)
            tril = (cc <= rr).astype(F32)
            for gidx in range(SG_GROUPS):
                dwt_ref[gidx] = dwt_ref[gidx] * tril
            kk = lax.broadcasted_iota(jnp.int32, (SG_WIDTH, LANES), 0)
            gg = lax.broadcasted_iota(jnp.int32, (SG_WIDTH, LANES), 1)
            sel = ((kk // 64) == gg).astype(F32)
            dbl_ref[...] = jnp.dot(dbacc_scr[...], sel, preferred_element_type=F32,
                                   precision=lax.Precision.HIGHEST)

    half = lambda k: pl.BlockSpec((tm, SG_WIDTH), lambda i: (i, k))
    vec = pl.BlockSpec((1, SG_WIDTH), lambda i: (0, 0))
    wspec = pl.BlockSpec((SG_GROUPS, CHUNK, CHUNK), lambda i: (0, 0, 0))
    return pl.pallas_call(
        body, name=name, grid=(nsteps,),
        in_specs=[half(0), half(1), pl.BlockSpec((tm, SG_WIDTH), lambda i: (i, 0)), vec, vec,
                  wspec, wspec, pl.BlockSpec((CHUNK, SG_WIDTH), lambda i: (0, 0))],
        out_specs=[pl.BlockSpec((tm, 2 * SG_WIDTH), lambda i: (i, 0)), wspec,
                   pl.BlockSpec((CHUNK, LANES), lambda i: (0, 0)), vec, vec],
        out_shape=[jax.ShapeDtypeStruct((t, 2 * SG_WIDTH), BF16),
                   jax.ShapeDtypeStruct((SG_GROUPS, CHUNK, CHUNK), F32),
                   jax.ShapeDtypeStruct((CHUNK, LANES), F32),
                   jax.ShapeDtypeStruct((1, SG_WIDTH), F32), jax.ShapeDtypeStruct((1, SG_WIDTH), F32)],
        scratch_shapes=[pltpu.VMEM((tm, SG_WIDTH), BF16), pltpu.VMEM((tm, SG_WIDTH), F32),
                        pltpu.VMEM((CHUNK, SG_WIDTH), F32)],
        compiler_params=_cparams())(zuv, zuv, dya, lg, lb, wt, wt_t, bias_l)


def _rope(x, c, s1, s2):
    return x * c + pltpu.roll(x, LANES - 16, 1) * s1 + pltpu.roll(x, 16, 1) * s2


def _rope_t(dy, c, s1, s2):
    return dy * c + pltpu.roll(dy * s1, 16, 1) + pltpu.roll(dy * s2, LANES - 16, 1)


def _mla_prep_fwd(zcq, zckv, zkr, gcq, gckv, qg, kg, wuq, wuk, wuv, rc, rs1, rs2, name, ex=None):
    t = zcq.shape[0]
    tm = _row_tile(t, 256)
    hd = MLA_HEADS * LANES

    def body(zcq_ref, zckv_ref, zkr_ref, gcq_ref, gckv_ref, qg_ref, kg_ref, wuq_ref, wuk_ref, wuv_ref,
             c_ref, s1_ref, s2_ref, q_ref, k_ref, v_ref, cqn_ref, ckvn_ref):
        c, s1, s2 = c_ref[...], s1_ref[...], s2_ref[...]
        xq = zcq_ref[...]
        cqn = (xq * _rstd(xq, MLA_Q_RANK) * gcq_ref[...]).astype(BF16)
        cqn_ref[...] = cqn
        ql = _dot(cqn, wuq_ref[...])
        xk = zckv_ref[...]
        ckvn = (xk * _rstd(xk, MLA_KV_RANK) * gckv_ref[...]).astype(BF16)
        ckvn_ref[...] = ckvn
        kl = _dot(ckvn, wuk_ref[...])
        slot_lane = lax.broadcasted_iota(jnp.int32, (tm, hd), 1) % LANES
        v_ref[...] = jnp.where(slot_lane == V_ONES_LANE, 1.0, _dot(ckvn, wuv_ref[...])).astype(BF16)
        kr = zkr_ref[...]
        for h in range(MLA_HEADS):
            sl = slice(h * LANES, (h + 1) * LANES)
            qh = ql[:, sl]
            q_ref[:, sl] = (_rope(qh * _rstd(qh, MLA_QK) * qg_ref[...], c, s1, s2) * ATTN_SCALE2).astype(BF16)
            kh = kl[:, sl] + kr
            k_ref[:, sl] = _rope(kh * _rstd(kh, MLA_QK) * kg_ref[...], c, s1, s2).astype(BF16)

    row = lambda n: pl.BlockSpec((tm, n), lambda i: (i, 0))
    full = lambda a: pl.BlockSpec(a.shape, lambda i: (0, 0))
    return _call_with_exchange(
        ex, body, name, (t // tm,),
        [row(MLA_Q_RANK), row(MLA_KV_RANK), row(LANES), full(gcq), full(gckv), full(qg), full(kg),
         full(wuq), full(wuk), full(wuv), row(LANES), row(LANES), row(LANES)],
        [row(hd), row(hd), row(hd), row(MLA_Q_RANK), row(MLA_KV_RANK)],
        [jax.ShapeDtypeStruct((t, hd), BF16)] * 3
        + [jax.ShapeDtypeStruct((t, MLA_Q_RANK), BF16), jax.ShapeDtypeStruct((t, MLA_KV_RANK), BF16)],
        [], (zcq, zckv, zkr, gcq, gckv, qg, kg, wuq, wuk, wuv, rc, rs1, rs2))


def _mla_prep_bwd(zcq, zckv, zkr, gcq, gckv, qg, kg, wuq, wuk, wuv, rc, rs1, rs2, dq, dk, dv, name):
    t = zcq.shape[0]
    tm = _row_tile(t, 256)
    hd = MLA_HEADS * LANES

    def body(zcq_ref, zckv_ref, zkr_ref, gcq_ref, gckv_ref, qg_ref, kg_ref, wuq_ref, wuk_ref, wuv_ref,
             c_ref, s1_ref, s2_ref, dq_ref, dk_ref, dv_ref,
             dzcq_ref, dzckv_ref, dzkr_ref, dql_ref, dkl_ref, dgcq_ref, dgckv_ref, dqg_ref, dkg_ref):
        @pl.when(pl.program_id(0) == 0)
        def _():
            for ref in (dgcq_ref, dgckv_ref, dqg_ref, dkg_ref):
                ref[...] = jnp.zeros_like(ref)

        c, s1, s2 = c_ref[...], s1_ref[...], s2_ref[...]
        qgv, kgv = qg_ref[...], kg_ref[...]
        xq = zcq_ref[...]
        rq = _rstd(xq, MLA_Q_RANK)
        ql = _dot((xq * rq * gcq_ref[...]).astype(BF16), wuq_ref[...])
        xk = zckv_ref[...]
        rk = _rstd(xk, MLA_KV_RANK)
        kl = _dot((xk * rk * gckv_ref[...]).astype(BF16), wuk_ref[...])
        kr = zkr_ref[...]
        dqg_acc = jnp.zeros((tm, LANES), F32)
        dkg_acc = jnp.zeros((tm, LANES), F32)
        dkr = jnp.zeros((tm, LANES), F32)
        for h in range(MLA_HEADS):
            sl = slice(h * LANES, (h + 1) * LANES)
            qh = ql[:, sl]
            dqh, dgr = _rms_vjp(qh, _rstd(qh, MLA_QK), qgv, _rope_t(dq_ref[:, sl], c, s1, s2), MLA_QK)
            dql_ref[:, sl] = dqh.astype(BF16)
            dqg_acc += dgr
            kh = kl[:, sl] + kr
            dkh, dgr = _rms_vjp(kh, _rstd(kh, MLA_QK), kgv, _rope_t(dk_ref[:, sl], c, s1, s2), MLA_QK)
            dkl_ref[:, sl] = dkh.astype(BF16)
            dkg_acc += dgr
            dkr += dkh
        dqg_ref[...] += jnp.sum(dqg_acc, axis=0, keepdims=True)
        dkg_ref[...] += jnp.sum(dkg_acc, axis=0, keepdims=True)
        lane = lax.broadcasted_iota(jnp.int32, (tm, LANES), 1)
        dzkr_ref[...] = jnp.where((lane >= MLA_NOPE) & (lane < MLA_QK), dkr, 0.0).astype(BF16)
        dcqn = _dot_nt(dql_ref[...], wuq_ref[...])
        dx, dgr = _rms_vjp(xq, rq, gcq_ref[...], dcqn, MLA_Q_RANK)
        dzcq_ref[...] = dx.astype(BF16)
        dgcq_ref[...] += jnp.sum(dgr, axis=0, keepdims=True)
        dckvn = _dot_nt(dkl_ref[...], wuk_ref[...]) + _dot_nt(dv_ref[...].astype(BF16), wuv_ref[...])
        dx, dgr = _rms_vjp(xk, rk, gckv_ref[...], dckvn, MLA_KV_RANK)
        dzckv_ref[...] = dx.astype(BF16)
        dgckv_ref[...] += jnp.sum(dgr, axis=0, keepdims=True)

    row = lambda n: pl.BlockSpec((tm, n), lambda i: (i, 0))
    full = lambda a: pl.BlockSpec(a.shape, lambda i: (0, 0))
    vec = lambda n: pl.BlockSpec((1, n), lambda i: (0, 0))
    return pl.pallas_call(
        body, name=name, grid=(t // tm,),
        in_specs=[row(MLA_Q_RANK), row(MLA_KV_RANK), row(LANES), full(gcq), full(gckv), full(qg), full(kg),
                  full(wuq), full(wuk), full(wuv), row(LANES), row(LANES), row(LANES), row(hd), row(hd), row(hd)],
        out_specs=[row(MLA_Q_RANK), row(MLA_KV_RANK), row(LANES), row(hd), row(hd),
                   vec(MLA_Q_RANK), vec(MLA_KV_RANK), vec(LANES), vec(LANES)],
        out_shape=[jax.ShapeDtypeStruct((t, MLA_Q_RANK), BF16), jax.ShapeDtypeStruct((t, MLA_KV_RANK), BF16),
                   jax.ShapeDtypeStruct((t, LANES), BF16), jax.ShapeDtypeStruct((t, hd), BF16),
                   jax.ShapeDtypeStruct((t, hd), BF16), jax.ShapeDtypeStruct((1, MLA_Q_RANK), F32),
                   jax.ShapeDtypeStruct((1, MLA_KV_RANK), F32), jax.ShapeDtypeStruct((1, LANES), F32),
                   jax.ShapeDtypeStruct((1, LANES), F32)],
        compiler_params=_cparams(),
    )(zcq, zckv, zkr, gcq, gckv, qg, kg, wuq, wuk, wuv, rc, rs1, rs2, dq, dk, dv)


def _attn_tiles(t):
    tq = 512 if t >= 2048 else 128
    return tq, min(t, 4 * tq), min(t, 2 * tq)


def _causal_keep(tq, tk, i, j):
    row = lax.broadcasted_iota(jnp.int32, (tq, tk), 0)
    col = lax.broadcasted_iota(jnp.int32, (tq, tk), 1)
    return (col - row) <= (i * tq - j * tk)


def _causal_keep_t(tq, tk, i, j):
    key = lax.broadcasted_iota(jnp.int32, (tk, tq), 0)
    qry = lax.broadcasted_iota(jnp.int32, (tk, tq), 1)
    return (key - qry) <= (i * tq - j * tk)


ATTN_FWD_HEADS_PER_STEP = 2
ATTN_BWD_HEADS_PER_STEP = 2


def _attn_fwd(q, k, v, name):
    t, hd = q.shape
    hp = ATTN_FWD_HEADS_PER_STEP
    tq, tk, _ = _attn_tiles(t)
    pairs = [(i, j) for i in range(t // tq) for j in range(((i + 1) * tq - 1) // tk + 1)]
    ii = np.array([p[0] for p in pairs], np.int32)
    jj = np.array([p[1] for p in pairs], np.int32)

    def body(ii_ref, jj_ref, q_ref, k_ref, v_ref, o_ref, lse_ref, m_scr, acc_scr):
        s_id = pl.program_id(1)
        i, j = ii_ref[s_id], jj_ref[s_id]
        last = j == ((i + 1) * tq - 1) // tk
        ones_lane = lax.broadcasted_iota(jnp.int32, (tq, LANES), 1) == V_ONES_LANE

        @pl.when(j == 0)
        def _():
            m_scr[...] = jnp.full_like(m_scr, NEG)
            acc_scr[...] = jnp.zeros_like(acc_scr)

        def step(masked):
            for hh in range(hp):
                sl = slice(hh * LANES, (hh + 1) * LANES)
                s = _dot_nt(q_ref[:, sl], k_ref[:, sl])
                if masked:
                    s = jnp.where(_causal_keep(tq, tk, i, j), s, NEG)
                m_prev = m_scr[hh]
                m_new = jnp.maximum(m_prev, jnp.max(s, axis=1, keepdims=True))
                p = jnp.exp2(s - m_new)
                alpha = jnp.exp2(m_prev - m_new)
                acc = alpha * acc_scr[:, sl] + _dot(p.astype(BF16), v_ref[:, sl])
                if masked:
                    l_new = jnp.sum(jnp.where(ones_lane, acc, 0.0), axis=1, keepdims=True)
                    o_ref[:, sl] = (acc / l_new).astype(BF16)
                    lse_ref[:, sl] = jnp.broadcast_to(m_new + jnp.log(l_new) * LOG2E, (tq, LANES))
                else:
                    acc_scr[:, sl] = acc
                    m_scr[hh] = m_new

        @pl.when(jnp.logical_not(last))
        def _():
            step(False)

        @pl.when(last)
        def _():
            step(True)

    w = hp * LANES
    qspec = pl.BlockSpec((tq, w), lambda h, s, ii_r, jj_r: (ii_r[s], h))
    kspec = pl.BlockSpec((tk, w), lambda h, s, ii_r, jj_r: (jj_r[s], h))
    return pl.pallas_call(
        body, name=name,
        grid_spec=pltpu.PrefetchScalarGridSpec(
            num_scalar_prefetch=2, grid=(hd // w, len(pairs)), in_specs=[qspec, kspec, kspec],
            out_specs=[qspec, qspec],
            scratch_shapes=[pltpu.VMEM((hp, tq, 1), F32), pltpu.VMEM((tq, w), F32)]),
        out_shape=[jax.ShapeDtypeStruct((t, hd), BF16), jax.ShapeDtypeStruct((t, hd), F32)],
        compiler_params=_cparams())(jnp.asarray(ii), jnp.asarray(jj), q, k, v)


def _attn_bwd_rows(o, lse, do, name):
    t, hd = o.shape
    heads = hd // LANES
    tm = _row_tile(t, 512)

    def body(o_ref, lse_ref, do_ref, out_ref):
        lane = lax.broadcasted_iota(jnp.int32, (tm, LANES), 1)
        acc = jnp.zeros((tm, LANES), F32)
        for h in range(heads):
            sl = slice(h * LANES, (h + 1) * LANES)
            delta = jnp.sum(do_ref[:, sl].astype(F32) * o_ref[:, sl].astype(F32), axis=1, keepdims=True)
            acc = jnp.where(lane == h, delta, acc)
            acc = jnp.where(lane == heads + h, lse_ref[:, sl], acc)
        out_ref[...] = acc

    row = pl.BlockSpec((tm, hd), lambda i: (i, 0))
    cols = pl.pallas_call(
        body, name=name, grid=(t // tm,), in_specs=[row, row, row],
        out_specs=pl.BlockSpec((tm, LANES), lambda i: (i, 0)),
        out_shape=jax.ShapeDtypeStruct((t, LANES), F32), compiler_params=_cparams())(o, lse, do)
    rows = cols.T
    return rows[:heads].reshape(heads, 1, t), rows[heads:2 * heads].reshape(heads, 1, t)


def _attn_bwd(q, k, v, delta_rows, lse_rows, do, name):
    t, hd = q.shape
    hp = ATTN_BWD_HEADS_PER_STEP
    tq, _, tk = _attn_tiles(t)
    nq = t // tq
    pairs = [(i, j) for j in range(t // tk) for i in range((j * tk) // tq, nq)]
    ii = np.array([p[0] for p in pairs], np.int32)
    jj = np.array([p[1] for p in pairs], np.int32)

    def body(jj_ref, ii_ref, q_ref, k_ref, v_ref, delta_ref, lse_ref, do_ref, dq_ref, dk_ref, dv_ref,
             dk_scr, dv_scr):
        s_id = pl.program_id(1)
        i, j = ii_ref[s_id], jj_ref[s_id]

        @pl.when(s_id == 0)
        def _():
            dq_ref[...] = jnp.zeros_like(dq_ref)

        @pl.when(i == (j * tk) // tq)
        def _():
            dk_scr[...] = jnp.zeros_like(dk_scr)
            dv_scr[...] = jnp.zeros_like(dv_scr)

        rows = pl.ds(pl.multiple_of(i * tq, tq), tq)

        def step(masked):
            for hh in range(hp):
                sl = slice(hh * LANES, (hh + 1) * LANES)
                qv, kv, dov = q_ref[:, sl], k_ref[:, sl], do_ref[:, sl]
                st = _dot_nt(kv, qv)
                if masked:
                    st = jnp.where(_causal_keep_t(tq, tk, i, j), st, NEG)
                pt = jnp.exp2(st - lse_ref[hh])
                dv_scr[:, sl] += _dot(pt.astype(BF16), dov)
                dpt = _dot_nt(v_ref[:, sl], dov)
                dst = (pt * (dpt - delta_ref[hh]) * ATTN_SCALE).astype(BF16)
                dk_scr[:, sl] += _dot(dst, qv)
                dq_ref[rows, sl] += _dot_tn(dst, kv)

        crosses = (j + 1) * tk - 1 > i * tq

        @pl.when(jnp.logical_not(crosses))
        def _():
            step(False)

        @pl.when(crosses)
        def _():
            step(True)

        @pl.when(i == nq - 1)
        def _():
            dk_ref[...] = dk_scr[...] * (1.0 / ATTN_SCALE2)
            dv_ref[...] = dv_scr[...]

    w = hp * LANES
    qspec = pl.BlockSpec((tq, w), lambda h, s, jj_r, ii_r: (ii_r[s], h))
    kspec = pl.BlockSpec((tk, w), lambda h, s, jj_r, ii_r: (jj_r[s], h))
    rspec = pl.BlockSpec((hp, 1, tq), lambda h, s, jj_r, ii_r: (h, 0, ii_r[s]))
    return pl.pallas_call(
        body, name=name,
        grid_spec=pltpu.PrefetchScalarGridSpec(
            num_scalar_prefetch=2, grid=(hd // w, len(pairs)),
            in_specs=[qspec, kspec, kspec, rspec, rspec, qspec],
            out_specs=[pl.BlockSpec((t, w), lambda h, s, jj_r, ii_r: (0, h)), kspec, kspec],
            scratch_shapes=[pltpu.VMEM((tk, w), F32), pltpu.VMEM((tk, w), F32)]),
        out_shape=[jax.ShapeDtypeStruct((t, hd), F32)] * 3,
        compiler_params=_cparams())(jnp.asarray(jj), jnp.asarray(ii), q, k, v, delta_rows, lse_rows, do)


MEM_W = MEM_HEADS * LANES


def _mem_kv_fwd(mem, gmem, wkv, kg, name):
    m, d = mem.shape

    def body(mem_ref, g_ref, w_ref, kg_ref, k_ref, v_ref, mn_ref):
        xv = mem_ref[...]
        mn = (xv * _rstd(xv, d) * g_ref[...]).astype(BF16)
        mn_ref[...] = mn
        kvm = _dot(mn, w_ref[...])
        v_ref[...] = kvm[:, MEM_W:].astype(BF16)
        for h in range(MEM_HEADS):
            sl = slice(h * LANES, (h + 1) * LANES)
            kh = kvm[:, sl]
            k_ref[:, sl] = (kh * _rstd(kh, LANES) * kg_ref[...]).astype(BF16)

    full = lambda a: pl.BlockSpec(a.shape, lambda i: (0, 0))
    return pl.pallas_call(
        body, name=name, grid=(1,), in_specs=[full(mem), full(gmem), full(wkv), full(kg)],
        out_specs=[pl.BlockSpec((m, MEM_W), lambda i: (0, 0)), pl.BlockSpec((m, MEM_W), lambda i: (0, 0)),
                   pl.BlockSpec((m, d), lambda i: (0, 0))],
        out_shape=[jax.ShapeDtypeStruct((m, MEM_W), BF16), jax.ShapeDtypeStruct((m, MEM_W), BF16),
                   jax.ShapeDtypeStruct((m, d), BF16)],
        compiler_params=_cparams())(mem, gmem, wkv, kg)


def _mem_softmax(qn, kh):
    s = _dot_nt(qn, kh) * (LANES ** -0.5)
    e = jnp.exp(s - jnp.max(s, axis=1, keepdims=True))
    return e / jnp.sum(e, axis=1, keepdims=True)


def _mem_attn_fwd(zqm, qg, km, vm, name):
    t = zqm.shape[0]
    tm = _row_tile(t, 512)

    def body(q_ref, qg_ref, k_ref, v_ref, o_ref):
        for h in range(MEM_HEADS):
            sl = slice(h * LANES, (h + 1) * LANES)
            qh = q_ref[:, sl]
            qn = (qh * _rstd(qh, LANES) * qg_ref[...]).astype(BF16)
            p = _mem_softmax(qn, k_ref[:, sl])
            o_ref[:, sl] = _dot(p.astype(BF16), v_ref[:, sl]).astype(BF16)

    row = pl.BlockSpec((tm, MEM_W), lambda i: (i, 0))
    full = lambda a: pl.BlockSpec(a.shape, lambda i: (0, 0))
    return pl.pallas_call(
        body, name=name, grid=(t // tm,), in_specs=[row, full(qg), full(km), full(vm)], out_specs=row,
        out_shape=jax.ShapeDtypeStruct((t, MEM_W), BF16), compiler_params=_cparams())(zqm, qg, km, vm)


def _mem_attn_bwd(zqm, dyc, qg, km, vm, name):
    t = zqm.shape[0]
    m = km.shape[0]
    tm = _row_tile(t, 256)

    def body(q_ref, dy_ref, qg_ref, k_ref, v_ref, dz_ref, dk_ref, dv_ref, dqg_ref):
        @pl.when(pl.program_id(0) == 0)
        def _():
            dk_ref[...] = jnp.zeros_like(dk_ref)
            dv_ref[...] = jnp.zeros_like(dv_ref)
            dqg_ref[...] = jnp.zeros_like(dqg_ref)

        qgv = qg_ref[...]
        dqg_acc = jnp.zeros((tm, LANES), F32)
        for h in range(MEM_HEADS):
            sl = slice(h * LANES, (h + 1) * LANES)
            qh = q_ref[:, sl]
            r = _rstd(qh, LANES)
            qn = (qh * r * qgv).astype(BF16)
            kh = k_ref[:, sl]
            p = _mem_softmax(qn, kh)
            dov = dy_ref[:, sl]
            dv_ref[:, sl] += _dot_tn(p.astype(BF16), dov)
            dp = _dot_nt(dov, v_ref[:, sl])
            ds = (p * (dp - jnp.sum(dp * p, axis=1, keepdims=True)) * (LANES ** -0.5)).astype(BF16)
            dk_ref[:, sl] += _dot_tn(ds, qn)
            dqh, dgr = _rms_vjp(qh, r, qgv, _dot(ds, kh), LANES)
            dz_ref[:, sl] = dqh.astype(BF16)
            dqg_acc += dgr
        dqg_ref[...] += jnp.sum(dqg_acc, axis=0, keepdims=True)

    row = pl.BlockSpec((tm, MEM_W), lambda i: (i, 0))
    full = lambda a: pl.BlockSpec(a.shape, lambda i: (0, 0))
    acc = pl.BlockSpec((m, MEM_W), lambda i: (0, 0))
    return pl.pallas_call(
        body, name=name, grid=(t // tm,), in_specs=[row, row, full(qg), full(km), full(vm)],
        out_specs=[row, acc, acc, pl.BlockSpec((1, LANES), lambda i: (0, 0))],
        out_shape=[jax.ShapeDtypeStruct((t, MEM_W), BF16), jax.ShapeDtypeStruct((m, MEM_W), F32),
                   jax.ShapeDtypeStruct((m, MEM_W), F32), jax.ShapeDtypeStruct((1, LANES), F32)],
        compiler_params=_cparams())(zqm, dyc, qg, km, vm)


def _mem_kv_bwd(mem, gmem, wkv, kg, dkn, dvm, name):
    m, d = mem.shape

    def body(mem_ref, g_ref, w_ref, kg_ref, dk_ref, dv_ref, dw_ref, dkg_ref, dg_ref, dkv_scr):
        xv = mem_ref[...]
        r = _rstd(xv, d)
        mn = (xv * r * g_ref[...]).astype(BF16)
        kvm = _dot(mn, w_ref[...])
        dkv_scr[:, MEM_W:] = dv_ref[...].astype(BF16)
        dkg_acc = jnp.zeros((m, LANES), F32)
        for h in range(MEM_HEADS):
            sl = slice(h * LANES, (h + 1) * LANES)
            kh = kvm[:, sl]
            dkh, dgr = _rms_vjp(kh, _rstd(kh, LANES), kg_ref[...], dk_ref[:, sl], LANES)
            dkv_scr[:, sl] = dkh.astype(BF16)
            dkg_acc += dgr
        dkg_ref[...] = jnp.sum(dkg_acc, axis=0, keepdims=True)
        dkv = dkv_scr[...]
        dw_ref[...] = _dot_tn(mn, dkv)
        dmn = _dot_nt(dkv, w_ref[...])
        dg_ref[...] = jnp.sum(dmn * xv * r, axis=0, keepdims=True)

    full = lambda a: pl.BlockSpec(a.shape, lambda i: (0, 0))
    return pl.pallas_call(
        body, name=name, grid=(1,),
        in_specs=[full(mem), full(gmem), full(wkv), full(kg), full(dkn), full(dvm)],
        out_specs=[pl.BlockSpec((d, 2 * MEM_W), lambda i: (0, 0)), pl.BlockSpec((1, LANES), lambda i: (0, 0)),
                   pl.BlockSpec((1, d), lambda i: (0, 0))],
        out_shape=[jax.ShapeDtypeStruct((d, 2 * MEM_W), F32), jax.ShapeDtypeStruct((1, LANES), F32),
                   jax.ShapeDtypeStruct((1, d), F32)],
        scratch_shapes=[pltpu.VMEM((m, 2 * MEM_W), BF16)],
        compiler_params=_cparams())(mem, gmem, wkv, kg, dkn, dvm)


def _merge_fwd(x1, ya, yb, yc, zg, bg, wa, wb, wc, wo, name):
    t, d = x1.shape
    tm = _row_tile(t, 256)

    def body(x_ref, ya_ref, yb_ref, yc_ref, zg_ref, bg_ref, wa_ref, wb_ref, wc_ref, wo_ref,
             x2_ref, mg_ref, pa_ref, pb_ref, pc_ref):
        merged = None
        for k, (y_ref, w_ref, p_ref) in enumerate(
                ((ya_ref, wa_ref, pa_ref), (yb_ref, wb_ref, pb_ref), (yc_ref, wc_ref, pc_ref))):
            sl = slice(k * d, (k + 1) * d)
            pr = _dot(y_ref[...], w_ref[...])
            p_ref[...] = pr.astype(BF16)
            term = jax.nn.sigmoid(zg_ref[:, sl] + bg_ref[:, sl]) * pr
            merged = term if merged is None else merged + term
        mb = merged.astype(BF16)
        mg_ref[...] = mb
        x2_ref[...] = x_ref[...] + _dot(mb, wo_ref[...])

    row = lambda n: pl.BlockSpec((tm, n), lambda i: (i, 0))
    full = lambda a: pl.BlockSpec(a.shape, lambda i: (0, 0))
    return pl.pallas_call(
        body, name=name, grid=(t // tm,),
        in_specs=[row(d), row(ya.shape[1]), row(yb.shape[1]), row(yc.shape[1]), row(3 * d), full(bg),
                  full(wa), full(wb), full(wc), full(wo)],
        out_specs=[row(d)] * 5,
        out_shape=[jax.ShapeDtypeStruct((t, d), F32)] + [jax.ShapeDtypeStruct((t, d), BF16)] * 4,
        compiler_params=_cparams())(x1, ya, yb, yc, zg, bg, wa, wb, wc, wo)


def _merge_bwd(dx2, pa, pb, pc, zg, bg, wa, wb, wc, wo, name, ex=None):
    t, d = dx2.shape
    tm = _row_tile(t, 256)

    def body(dx_ref, pa_ref, pb_ref, pc_ref, zg_ref, bg_ref, wa_ref, wb_ref, wc_ref, wo_ref,
             dpa_ref, dpb_ref, dpc_ref, dzg_ref, dbg_ref, dya_ref, dyb_ref, dyc_ref):
        @pl.when(pl.program_id(0) == 0)
        def _():
            dbg_ref[...] = jnp.zeros_like(dbg_ref)

        dm = _dot_nt(dx_ref[...].astype(BF16), wo_ref[...])
        for k, (p_ref, w_ref, dp_ref, dy_ref) in enumerate(
                ((pa_ref, wa_ref, dpa_ref, dya_ref), (pb_ref, wb_ref, dpb_ref, dyb_ref),
                 (pc_ref, wc_ref, dpc_ref, dyc_ref))):
            sl = slice(k * d, (k + 1) * d)
            gate = jax.nn.sigmoid(zg_ref[:, sl] + bg_ref[:, sl])
            dpr = (dm * gate).astype(BF16)
            dp_ref[...] = dpr
            dzg = dm * p_ref[...].astype(F32) * gate * (1.0 - gate)
            dzg_ref[:, sl] = dzg.astype(BF16)
            dbg_ref[:, sl] += jnp.sum(dzg, axis=0, keepdims=True)
            dy_ref[...] = _dot_nt(dpr, w_ref[...]).astype(dy_ref.dtype)

    row = lambda n: pl.BlockSpec((tm, n), lambda i: (i, 0))
    full = lambda a: pl.BlockSpec(a.shape, lambda i: (0, 0))
    na, nb, nc = wa.shape[0], wb.shape[0], wc.shape[0]
    return _call_with_exchange(
        ex, body, name, (t // tm,),
        [row(d), row(d), row(d), row(d), row(3 * d), full(bg), full(wa), full(wb), full(wc), full(wo)],
        [row(d), row(d), row(d), row(3 * d), pl.BlockSpec((1, 3 * d), lambda i: (0, 0)), row(na), row(nb), row(nc)],
        [jax.ShapeDtypeStruct((t, d), BF16)] * 3
        + [jax.ShapeDtypeStruct((t, 3 * d), BF16), jax.ShapeDtypeStruct((1, 3 * d), F32),
           jax.ShapeDtypeStruct((t, na), F32), jax.ShapeDtypeStruct((t, nb), BF16),
           jax.ShapeDtypeStruct((t, nc), BF16)],
        [], (dx2, pa, pb, pc, zg, bg, wa, wb, wc, wo))


def _adamw_math(w, g, m, v):
    bc1 = 1.0 - ADAM_B1 ** ADAM_STEP
    bc2 = 1.0 - ADAM_B2 ** ADAM_STEP
    nm = ADAM_B1 * m + (1.0 - ADAM_B1) * g
    nv = ADAM_B2 * v + (1.0 - ADAM_B2) * (g * g)
    delta = -ADAM_LR * ((nm / bc1) / (jnp.sqrt(nv / bc2) + ADAM_EPS) + ADAM_WD * w)
    return delta, nm, nv


def _div_tile(n, cap, mult):
    best = None
    for cand in range(mult, min(n, cap) + 1, mult):
        if n % cand == 0:
            best = cand
    assert best is not None, (n, cap, mult)
    return best


def _adamw(w, g, m, v, name):
    rows, cols = w.shape
    tr = rows if rows * cols <= 256 * 1024 else _div_tile(rows, 256, 8)

    def body(w_ref, g_ref, m_ref, v_ref, d_ref, nm_ref, nv_ref):
        d_ref[...], nm_ref[...], nv_ref[...] = _adamw_math(w_ref[...], g_ref[...], m_ref[...], v_ref[...])

    blk = pl.BlockSpec((tr, cols), lambda i: (i, 0))
    return pl.pallas_call(
        body, name=name, grid=(rows // tr,), in_specs=[blk] * 4, out_specs=[blk] * 3,
        out_shape=[jax.ShapeDtypeStruct((rows, cols), F32)] * 3, compiler_params=_cparams())(w, g, m, v)


def _adamw_slots(w, slots, m, v, name):
    _, hr, cols = w.shape
    tr = _div_tile(hr, 128, 16)

    def body(w_ref, s_ref, m_ref, v_ref, g_ref, d_ref, nm_ref, nv_ref):
        g = s_ref[0, 0].astype(F32)
        for k in range(1, N_CHIPS):
            g = g + s_ref[0, k].astype(F32)
        g_ref[0] = g
        d_ref[0], nm_ref[0], nv_ref[0] = _adamw_math(w_ref[0], g, m_ref[0], v_ref[0])

    blk = pl.BlockSpec((1, tr, cols), lambda h, i: (h, i, 0))
    return pl.pallas_call(
        body, name=name, grid=(2, hr // tr),
        in_specs=[blk, pl.BlockSpec((1, N_CHIPS, tr, cols), lambda h, i: (h, 0, i, 0)), blk, blk],
        out_specs=[blk] * 4, out_shape=[jax.ShapeDtypeStruct((2, hr, cols), F32)] * 4,
        compiler_params=_cparams())(w, slots, m, v)


ANY = pl.BlockSpec(memory_space=pl.ANY)


def _place():
    x, y, c = lax.axis_index("x"), lax.axis_index("y"), lax.axis_index("c")
    other_chips = [(1 - x, y), (x, 1 - y), (1 - x, 1 - y)]
    return x, y, c, other_chips


def _remote(src, dst, send_sem, recv_sem, to):
    return pltpu.make_async_remote_copy(src_ref=src, dst_ref=dst, send_sem=send_sem, recv_sem=recv_sem,
                                        device_id=to, device_id_type=MESH)


PIECE_BYTES = 384 * 1024


def _row_pieces(half_rows, cols):
    for n in (4, 2):
        if half_rows % (16 * n) == 0 and half_rows * cols * 2 // n >= PIECE_BYTES:
            return [pl.ds(k * (half_rows // n), half_rows // n) for k in range(n)]
    return [pl.ds(0, half_rows)]


def _pieces(arrays, rows_axis):
    return [(w, rows) for w, a in enumerate(arrays) for rows in _row_pieces(a.shape[rows_axis], a.shape[-1])]


def _gather_exchange(shards):
    nw = len(shards)
    pieces = _pieces(shards, 1)
    npc = len(pieces)

    def build(s_refs, g_refs, sems):
        send_sems, recv_sems, local_sems = sems
        x, y, c, chips = _place()
        me = 2 * x + y
        sibling = (x, y, 1 - c)
        mine = [pltpu.make_async_copy(s_refs[w], g_refs[w].at[me], local_sems.at[w]) for w in range(nw)]
        first = [_remote(s_refs[w].at[c, rows], g_refs[w].at[me, c, rows], send_sems.at[k, p], recv_sems.at[k, p],
                         (cx, cy, c)) for k, (cx, cy) in enumerate(chips) for p, (w, rows) in enumerate(pieces)]

        def start():
            for cp in mine + first:
                cp.start()

        def finish():
            passed = []
            for k, (cx, cy) in enumerate(chips):
                for p, (w, rows) in enumerate(pieces):
                    slab = g_refs[w].at[2 * cx + cy, c, rows]
                    _remote(slab, slab, send_sems.at[k, p], recv_sems.at[k, p], (cx, cy, c)).wait_recv()
                    fwd = _remote(slab, slab, send_sems.at[3 + k, p], recv_sems.at[3 + k, p], sibling)
                    fwd.start()
                    passed.append(fwd)
            for k, (cx, cy) in enumerate(chips):
                for p, (w, rows) in enumerate(pieces):
                    slab = g_refs[w].at[2 * cx + cy, 1 - c, rows]
                    _remote(slab, slab, send_sems.at[3 + k, p], recv_sems.at[3 + k, p], sibling).wait_recv()
            for cp in first + passed:
                cp.wait_send()
            for cp in mine:
                cp.wait()

        return start, finish

    return _Exchange(list(shards), [jax.ShapeDtypeStruct((N_CHIPS,) + s.shape, BF16) for s in shards],
                     [pltpu.SemaphoreType.DMA((6, npc)), pltpu.SemaphoreType.DMA((6, npc)),
                      pltpu.SemaphoreType.DMA((nw,))], build)


def _swap_halves(grads, name):
    nw = len(grads)

    def body(*refs):
        g_refs, sib_refs = refs[:nw], refs[nw:2 * nw]
        send_sems, recv_sems = refs[2 * nw:]
        x, y, c, _ = _place()
        copies = [_remote(g_refs[w].at[s, 1 - c], sib_refs[w].at[s], send_sems.at[s, w], recv_sems.at[s, w],
                          (x, y, 1 - c)) for w in range(nw) for s in range(N_CHIPS)]
        for cp in copies:
            cp.start()
        for cp in copies:
            cp.wait_recv()
        for cp in copies:
            cp.wait_send()

    return pl.pallas_call(
        body, name=name, in_specs=[ANY] * nw, out_specs=[ANY] * nw,
        out_shape=[jax.ShapeDtypeStruct((N_CHIPS,) + g.shape[2:], BF16) for g in grads],
        scratch_shapes=[pltpu.SemaphoreType.DMA((N_CHIPS, nw)), pltpu.SemaphoreType.DMA((N_CHIPS, nw))],
    )(*grads)


def _pair_sum(grad, sib, core, name):
    nchip, _, hr, cols = grad.shape
    tr = _div_tile(hr, 256, 16)

    def body(core_ref, a_ref, b_ref, o_ref):
        o_ref[...] = (a_ref[0].astype(F32) + b_ref[...].astype(F32)).astype(BF16)

    return pl.pallas_call(
        body, name=name,
        grid_spec=pltpu.PrefetchScalarGridSpec(
            num_scalar_prefetch=1, grid=(nchip, hr // tr),
            in_specs=[pl.BlockSpec((1, 1, tr, cols), lambda s, i, core_r: (s, core_r[0], i, 0)),
                      pl.BlockSpec((1, tr, cols), lambda s, i, core_r: (s, i, 0))],
            out_specs=pl.BlockSpec((1, tr, cols), lambda s, i, core_r: (s, i, 0))),
        out_shape=jax.ShapeDtypeStruct((nchip, hr, cols), BF16), compiler_params=_cparams())(core, grad, sib)


def _pair_sum_exchange(sums):
    nw = len(sums)
    pieces = _pieces(sums, 1)
    npc = len(pieces)

    def build(p_refs, o_refs, sems):
        send_sems, recv_sems, local_sems = sems
        x, y, c, chips = _place()
        me = 2 * x + y
        sibling = (x, y, 1 - c)
        mine = [pltpu.make_async_copy(p_refs[w].at[me], o_refs[w].at[c, 3], local_sems.at[w]) for w in range(nw)]
        first = [_remote(p_refs[w].at[2 * cx + cy, rows], o_refs[w].at[c, k, rows], send_sems.at[k, p],
                         recv_sems.at[k, p], (cx, cy, c))
                 for k, (cx, cy) in enumerate(chips) for p, (w, rows) in enumerate(pieces)]

        def start():
            for cp in mine + first:
                cp.start()

        def finish():
            passed = []
            for k in range(N_CHIPS):
                own_waited = set()
                for p, (w, rows) in enumerate(pieces):
                    slab = o_refs[w].at[c, k, rows]
                    if k < 3:
                        first[k * npc + p].wait_recv()
                    elif w not in own_waited:
                        mine[w].wait()
                        own_waited.add(w)
                    fwd = _remote(slab, slab, send_sems.at[3 + k, p], recv_sems.at[3 + k, p], sibling)
                    fwd.start()
                    passed.append(fwd)
            for k in range(N_CHIPS):
                for p, (w, rows) in enumerate(pieces):
                    slab = o_refs[w].at[1 - c, k, rows]
                    _remote(slab, slab, send_sems.at[3 + k, p], recv_sems.at[3 + k, p], sibling).wait_recv()
            for cp in first + passed:
                cp.wait_send()

        return start, finish

    return _Exchange(list(sums), [jax.ShapeDtypeStruct((2,) + p.shape, BF16) for p in sums],
                     [pltpu.SemaphoreType.DMA((7, npc)), pltpu.SemaphoreType.DMA((7, npc)),
                      pltpu.SemaphoreType.DMA((nw,))], build)


def _small_sum_exchange(vec):
    m_per, n = vec.shape

    def build(ins, outs, scr):
        (x_ref,), (out_ref,) = ins, outs
        gath_ref, sum_ref, send_sems, recv_sems, local_sem, out_sem = scr
        x, y, c, chips = _place()
        me, sibling = (x, y, c), (x, y, 1 - c)

        def rows(px, py, pc):
            return gath_ref.at[pl.ds((4 * px + 2 * py + pc) * m_per, m_per), :]

        def copy(k, block, to, src=None):
            return pltpu.make_async_remote_copy(
                src_ref=rows(*block) if src is None else src, dst_ref=rows(*block),
                send_sem=send_sems.at[k], recv_sem=recv_sems.at[k], device_id=to, device_id_type=MESH)

        mine = pltpu.make_async_copy(x_ref, rows(*me), local_sem)
        first = [copy(0, me, sibling, src=x_ref)] + [copy(1 + j, me, (*chip, c), src=x_ref)
                                                     for j, chip in enumerate(chips)]

        def start():
            for cp in [mine] + first:
                cp.start()

        def finish():
            passed = [copy(4 + j, (*chip, c), sibling) for j, chip in enumerate(chips)]
            for j, chip in enumerate(chips):
                copy(1 + j, (*chip, c), me).wait_recv()
                passed[j].start()
            copy(0, sibling, me).wait_recv()
            for j, chip in enumerate(chips):
                copy(4 + j, (*chip, 1 - c), me).wait_recv()
            for cp in first + passed:
                cp.wait_send()
            mine.wait()
            acc = gath_ref[pl.ds(0, m_per), :]
            for k in range(1, N_DEV):
                acc = acc + gath_ref[pl.ds(k * m_per, m_per), :]
            sum_ref[...] = acc
            done = pltpu.make_async_copy(sum_ref, out_ref, out_sem)
            done.start()
            done.wait()

        return start, finish

    return _Exchange([vec], [jax.ShapeDtypeStruct((m_per, n), F32)],
                     [pltpu.VMEM((N_DEV * m_per, n), F32), pltpu.VMEM((m_per, n), F32), pltpu.SemaphoreType.DMA((7,)),
                      pltpu.SemaphoreType.DMA((7,)), pltpu.SemaphoreType.DMA, pltpu.SemaphoreType.DMA], build)


def _pack_small(vals, tail=()):
    flat = jnp.concatenate([vals[name].reshape(-1).astype(F32) for name, _ in SMALL] + [v.reshape(1) for v in tail])
    flat = jnp.pad(flat, (0, SMALL_ROWS * LANES - flat.shape[0]))
    return flat.reshape(SMALL_ROWS, LANES)


def _unpack_small(packed):
    flat = packed.reshape(-1)
    out, off = {}, 0
    for name, shape in SMALL:
        n = int(np.prod(shape))
        out[name] = flat[off:off + n].reshape(shape)
        off += n
    return out


def _head_pad_cols(w, heads, real):
    k = w.shape[0]
    return jnp.pad(w.reshape(k, heads, real), ((0, 0), (0, 0), (0, LANES - real))).reshape(k, heads * LANES)


def _rope_tables(positions):
    half = MLA_ROPE // 2
    inv = ROPE_BASE ** (-jnp.arange(half, dtype=F32) / half)
    ang = positions.astype(F32)[:, None] * inv
    cos, sin = jnp.cos(ang), jnp.sin(ang)
    t = positions.shape[0]
    z = lambda n: jnp.zeros((t, n), F32)
    rc = jnp.concatenate([jnp.ones((t, MLA_NOPE), F32), cos, cos, z(LANES - MLA_QK)], axis=1)
    rs1 = jnp.concatenate([z(MLA_NOPE), -sin, z(LANES - MLA_NOPE - half)], axis=1)
    rs2 = jnp.concatenate([z(MLA_NOPE + half), sin, z(LANES - MLA_QK)], axis=1)
    return rc, rs1, rs2


FFN1_WEIGHTS = ("ffn1_w_gu", "ffn1_w_down")
FFN2_WEIGHTS = ("ffn2_w_gu", "ffn2_w_down")
MIXER_WEIGHTS = tuple(n for n, *_ in SHARDED if n not in FFN1_WEIGHTS + FFN2_WEIGHTS)
SHARD_SHAPE = {n: (r, c, kind) for n, r, c, kind in SHARDED}


def _from_blocks(name, gathered):
    r, c, kind = SHARD_SHAPE[name]
    blk = gathered.reshape(N_CHIPS, r, c)
    return blk, (blk.transpose(1, 0, 2).reshape(r, N_CHIPS * c) if kind == "col" else blk.reshape(N_CHIPS * r, c))


def _grad_pair_sums(names, gw, core, tag):
    by_owner = []
    for name in names:
        r, c, kind = SHARD_SHAPE[name]
        if gw[name].dtype == BF16:
            blk = gw[name]
        elif kind == "col":
            blk = gw[name].reshape(r, N_CHIPS, c).transpose(1, 0, 2)
        else:
            blk = gw[name].reshape(N_CHIPS, r, c)
        by_owner.append(blk.astype(BF16).reshape(N_CHIPS, 2, r // 2, c))
    received = _swap_halves(by_owner, "grad_swap_" + tag)
    return [_pair_sum(g, s, core, "pair_sum_" + n) for g, s, n in zip(by_owner, received, names)]


def _device_step(x, mem, positions, tgt, small, shards, core):
    d = D_MODEL
    g_ffn1, g_mix, g_ffn2 = small["ffn1_norm"], small["mix_norm"], small["ffn2_norm"]
    big = {}
    for name, g in zip(FFN1_WEIGHTS, _run_exchange(_gather_exchange([shards[n] for n in FFN1_WEIGHTS]), "gather_ffn1")):
        big[name + "#blocks"], big[name] = _from_blocks(name, g)
    wgu1, wd1 = big["ffn1_w_gu#blocks"], big["ffn1_w_down"].reshape(2, FF_TILE, d)
    x1, gpre1, upre1, h, *rest = _ffn_fwd(x, g_ffn1, wgu1, wd1, "ffn1_fwd", next_gain=g_mix,
                                          ex=_gather_exchange([shards[n] for n in MIXER_WEIGHTS]))
    for name, g in zip(MIXER_WEIGHTS, rest):
        big[name + "#blocks"], big[name] = _from_blocks(name, g)
    w_in = big["w_in"]
    w_uv_, w_cq, w_ckv = w_in[:, :COL_CQ], w_in[:, COL_CQ:COL_CKV], w_in[:, COL_CKV:COL_KR]
    w_kr = jnp.pad(w_in[:, COL_KR:COL_QM], ((0, 0), (MLA_NOPE, LANES - MLA_QK)))
    w_qm, w_g = w_in[:, COL_QM:COL_GATE], w_in[:, COL_GATE:]
    segs = (w_uv_, w_cq, w_ckv, w_kr, w_qm, w_g)
    wuq = _head_pad_cols(big["mla_w_uq"], MLA_HEADS, MLA_QK)
    ukv = big["mla_w_ukv"].reshape(MLA_KV_RANK, MLA_HEADS, 2, MLA_NOPE)
    wuk = _head_pad_cols(ukv[:, :, 0].reshape(MLA_KV_RANK, -1), MLA_HEADS, MLA_NOPE)
    wuv = _head_pad_cols(ukv[:, :, 1].reshape(MLA_KV_RANK, -1), MLA_HEADS, MLA_NOPE)
    wkv = big["mem_w_kv"]
    wa, wc, wo = big["w_branch_a"], big["w_branch_c"], big["w_out"]
    wb = jnp.pad(big["w_branch_b"].reshape(MLA_HEADS, MLA_NOPE, d),
                 ((0, 0), (0, LANES - MLA_NOPE), (0, 0))).reshape(MLA_HEADS * LANES, d)
    qg = jnp.pad(small["mla_q_norm"], ((0, 0), (0, LANES - MLA_QK)))
    kg = jnp.pad(small["mla_k_norm"], ((0, 0), (0, LANES - MLA_QK)))
    causal = jnp.tril(jnp.ones((CHUNK, CHUNK), bool))
    wt_f = jnp.where(causal[None], small["sg_w"][0], 0.0)
    wt, wt_t = wt_f.astype(BF16), wt_f.transpose(0, 2, 1).astype(BF16)
    bias_l = jnp.repeat(small["sg_b"][0].T, 64, axis=1)
    rc, rs1, rs2 = _rope_tables(positions)

    zuv, zcq, zckv, zkr, zqm, zg, *rest = _mm_cols(h, segs, [F32] * 5 + [BF16], "in_proj",
                                                   ex=_gather_exchange([shards[n] for n in FFN2_WEIGHTS]))
    for name, g in zip(FFN2_WEIGHTS, rest):
        big[name + "#blocks"], big[name] = _from_blocks(name, g)
    wgu2, wd2 = big["ffn2_w_gu#blocks"], big["ffn2_w_down"].reshape(2, FF_TILE, d)
    ya = _sgu_fwd(zuv, small["sg_ln_g"], small["sg_ln_b"], wt, bias_l, "sgu_fwd")
    q, k, v, cqn, ckvn = _mla_prep_fwd(zcq, zckv, zkr, small["mla_cq_norm"], small["mla_ckv_norm"], qg, kg,
                                       wuq, wuk, wuv, rc, rs1, rs2, "mla_prep_fwd")
    yb, lse = _attn_fwd(q, k, v, "mla_attn_fwd")
    km, vm, memn = _mem_kv_fwd(mem, small["mem_norm"], wkv, small["mem_k_norm"], "mem_kv_fwd")
    yc = _mem_attn_fwd(zqm, small["mem_q_norm"], km, vm, "mem_attn_fwd")
    x2, merged, pa, pb, pc = _merge_fwd(x1, ya, yb, yc, zg, small["b_gate"], wa, wb, wc, wo, "merge_fwd")
    dy, loss_row, gpre2, upre2 = _ffn_fwd(x2, g_ffn2, wgu2, wd2, "ffn2_fwd", target=tgt)

    gw, gs, slots = {}, {}, {}

    def ffn_grads(prefix, xin, gain, dyin, gpre, upre, wgu, wd, ex=None, ex_names=(), last=False):
        dx, dgain, xn, dgt, dup, act, *got = _ffn_bwd(xin, gain, dyin, gpre, upre, wgu, wd, prefix + "_bwd", ex=ex)
        slots.update(zip(ex_names, got))
        gs[prefix + "_norm"] = dgain
        gw[prefix + "_w_gu"] = jnp.concatenate(
            [_mm_tn(xn, dgt, prefix + "_dwg", col_blocks=True, out_dtype=BF16),
             _mm_tn(xn, dup, prefix + "_dwu", col_blocks=True, out_dtype=BF16)], axis=0)
        rows_down = SHARD_SHAPE[prefix + "_w_down"][0]
        if last:
            small_sum = _small_sum_exchange(_pack_small(gs, tail=[loss_row[0, 0]]))
            dwd, summed = _mm_tn(act, dyin, prefix + "_dwd", scale=0.5, ex=small_sum, out_dtype=BF16)
            gw[prefix + "_w_down"] = dwd.reshape(N_CHIPS, rows_down, d)
            return dx, summed
        gw[prefix + "_w_down"] = _mm_tn(act, dyin, prefix + "_dwd", scale=0.5, out_dtype=BF16).reshape(
            N_CHIPS, rows_down, d)
        return dx

    dx2 = ffn_grads("ffn2", x2, g_ffn2, dy, gpre2, upre2, wgu2, wd2)
    ffn2_sums = _pair_sum_exchange(_grad_pair_sums(FFN2_WEIGHTS, gw, core, "ffn2"))
    dpa, dpb, dpc, dzg, dbg, dya, dyb, dyc, *got = _merge_bwd(dx2, pa, pb, pc, zg, small["b_gate"], wa, wb, wc, wo,
                                                              "merge_bwd", ex=ffn2_sums)
    slots.update(zip(FFN2_WEIGHTS, got))
    gs["b_gate"] = dbg
    gw["w_out"] = _mm_tn(merged, dx2, "dw_out")
    gw["w_branch_a"] = _mm_tn(ya, dpa, "dw_branch_a")
    gw["w_branch_b"] = _mm_tn(yb, dpb, "dw_branch_b").reshape(MLA_HEADS, LANES, d)[:, :MLA_NOPE].reshape(-1, d)
    gw["w_branch_c"] = _mm_tn(yc, dpc, "dw_branch_c")

    dzuv, dwt, dbl, dlg, dlb = _sgu_bwd(zuv, dya, small["sg_ln_g"], small["sg_ln_b"], wt, wt_t, bias_l, "sgu_bwd")
    gs["sg_w"], gs["sg_b"] = dwt[None], dbl[:, :SG_GROUPS].T[None]
    gs["sg_ln_g"], gs["sg_ln_b"] = dlg, dlb

    delta_rows, lse_rows = _attn_bwd_rows(yb, lse, dyb, "mla_attn_bwd_rows")
    dq, dk, dv = _attn_bwd(q, k, v, delta_rows, lse_rows, dyb, "mla_attn_bwd")
    dzcq, dzckv, dzkr, dql, dkl, dgcq, dgckv, dqg, dkg = _mla_prep_bwd(
        zcq, zckv, zkr, small["mla_cq_norm"], small["mla_ckv_norm"], qg, kg, wuq, wuk, wuv, rc, rs1, rs2,
        dq, dk, dv, "mla_prep_bwd")
    gs["mla_cq_norm"], gs["mla_ckv_norm"] = dgcq, dgckv
    gs["mla_q_norm"], gs["mla_k_norm"] = dqg[:, :MLA_QK], dkg[:, :MLA_QK]
    gw["mla_w_uq"] = _mm_tn(cqn, dql, "dw_uq").reshape(MLA_Q_RANK, MLA_HEADS, LANES)[:, :, :MLA_QK].reshape(
        MLA_Q_RANK, -1)
    dwuk = _mm_tn(ckvn, dkl, "dw_uk").reshape(MLA_KV_RANK, MLA_HEADS, LANES)[:, :, :MLA_NOPE]
    dwuv = _mm_tn(ckvn, dv, "dw_uv").reshape(MLA_KV_RANK, MLA_HEADS, LANES)[:, :, :MLA_NOPE]
    gw["mla_w_ukv"] = jnp.concatenate([dwuk, dwuv], axis=2).reshape(MLA_KV_RANK, -1)

    dzqm, dkn, dvm, dmqg = _mem_attn_bwd(zqm, dyc, small["mem_q_norm"], km, vm, "mem_attn_bwd")
    gs["mem_q_norm"] = dmqg
    gw["mem_w_kv"], gs["mem_k_norm"], gs["mem_norm"] = _mem_kv_bwd(
        mem, small["mem_norm"], wkv, small["mem_k_norm"], dkn, dvm, "mem_kv_bwd")

    dzs = (dzuv, dzcq, dzckv, dzkr, dzqm, dzg)
    dws = list(_mm_tn_cols(h, dzs[:5], "dw_in_narrow")) + [_mm_tn(h, dzg, "dw_in_gate")]
    dws[3] = dws[3][:, MLA_NOPE:MLA_QK]
    gw["w_in"] = jnp.concatenate(dws, axis=1)
    dx1, gs["mix_norm"] = _proj_norm_bwd(dzs, [w.T for w in segs], x1, g_mix, dx2, "in_proj_bwd")
    mixer_sums = _pair_sum_exchange(_grad_pair_sums(MIXER_WEIGHTS, gw, core, "mixer"))
    dx, summed = ffn_grads("ffn1", x, g_ffn1, dx1, gpre1, upre1, wgu1, wd1, ex=mixer_sums, ex_names=MIXER_WEIGHTS,
                           last=True)
    ffn1_sums = _pair_sum_exchange(_grad_pair_sums(FFN1_WEIGHTS, gw, core, "ffn1"))
    slots.update(zip(FFN1_WEIGHTS, _run_exchange(ffn1_sums, "grad_exchange_ffn1")))
    return dx, slots, summed


def kernel(x, mem, positions, ffn1_norm, ffn1_w_gu, ffn1_w_down, mix_norm, w_in, b_gate, sg_ln_g, sg_ln_b, sg_w, sg_b, mla_cq_norm, mla_w_uq, mla_ckv_norm, mla_w_ukv, mla_q_norm, mla_k_norm, mem_norm, mem_w_kv, mem_q_norm, mem_k_norm, w_branch_a, w_branch_b, w_branch_c, w_out, ffn2_norm, ffn2_w_gu, ffn2_w_down, loss_target, m_ffn1_norm, m_ffn1_w_gu, m_ffn1_w_down, m_mix_norm, m_w_in, m_b_gate, m_sg_ln_g, m_sg_ln_b, m_sg_w, m_sg_b, m_mla_cq_norm, m_mla_w_uq, m_mla_ckv_norm, m_mla_w_ukv, m_mla_q_norm, m_mla_k_norm, m_mem_norm, m_mem_w_kv, m_mem_q_norm, m_mem_k_norm, m_w_branch_a, m_w_branch_b, m_w_branch_c, m_w_out, m_ffn2_norm, m_ffn2_w_gu, m_ffn2_w_down, v_ffn1_norm, v_ffn1_w_gu, v_ffn1_w_down, v_mix_norm, v_w_in, v_b_gate, v_sg_ln_g, v_sg_ln_b, v_sg_w, v_sg_b, v_mla_cq_norm, v_mla_w_uq, v_mla_ckv_norm, v_mla_w_ukv, v_mla_q_norm, v_mla_k_norm, v_mem_norm, v_mem_w_kv, v_mem_q_norm, v_mem_k_norm, v_w_branch_a, v_w_branch_b, v_w_branch_c, v_w_out, v_ffn2_norm, v_ffn2_w_gu, v_ffn2_w_down):
    args = dict(locals())
    weights = {n: args[n] for n in WEIGHT_ORDER}
    mom_m = {n: args["m_" + n] for n in WEIGHT_ORDER}
    mom_v = {n: args["v_" + n] for n in WEIGHT_ORDER}
    small = {n: weights[n] for n, _ in SMALL}
    halves = lambda a, r, c: a.reshape(2, r // 2, c)

    shards = {n: halves(weights[n][0].astype(BF16), r, c) for n, r, c, _ in SHARDED}
    core = lax.axis_index("c").astype(jnp.int32).reshape(1)
    dx, slots, summed = _device_step(x[0], mem[0], positions[0], loss_target[0], small, shards, core)
    loss = summed.reshape(-1)[_N_SMALL]
    small_grads = _unpack_small(summed)

    grads, deltas, new_m, new_v = {}, {}, {}, {}
    for name, r, c, _ in SHARDED:
        outs = _adamw_slots(halves(weights[name][0], r, c), slots[name], halves(mom_m[name][0], r, c),
                            halves(mom_v[name][0], r, c), "adamw_" + name)
        shape = weights[name].shape
        grads[name], deltas[name], new_m[name], new_v[name] = [o.reshape(shape) for o in outs]
    dlt, nm, nv = _adamw(_pack_small(small), _pack_small(small_grads), _pack_small({n: mom_m[n] for n, _ in SMALL}),
                         _pack_small({n: mom_v[n] for n, _ in SMALL}), "adamw_small")
    for name, _ in SMALL:
        grads[name] = small_grads[name]
    deltas.update(_unpack_small(dlt))
    new_m.update(_unpack_small(nm))
    new_v.update(_unpack_small(nv))

    return (loss, dx[None], *[grads[n] for n in WEIGHT_ORDER], *[deltas[n] for n in WEIGHT_ORDER],
            *[new_m[n] for n in WEIGHT_ORDER], *[new_v[n] for n in WEIGHT_ORDER])
```

```python
import functools
from typing import Callable, NamedTuple

import numpy as np
import jax
import jax.numpy as jnp
from jax import lax
from jax.experimental import pallas as pl
from jax.experimental.pallas import tpu as pltpu

F32 = jnp.float32
BF16 = jnp.bfloat16

D_MODEL = 1024
D_FF = 2816
FF_TILE = 1408
SG_WIDTH = 512
SG_GROUPS = 8
CHUNK = 128
MLA_HEADS = 8
MLA_QK = 96
MLA_NOPE = 64
MLA_ROPE = 32
MLA_Q_RANK = 384
MLA_KV_RANK = 256
MEM_HEADS = 4
MEM_LEN = 256
LANES = 128
EPS = 1e-6
NEG = -1e30
ROPE_BASE = 10000.0
N_CHIPS = 4
N_DEV = 8

ADAM_LR = 0.001
ADAM_B1 = 0.9
ADAM_B2 = 0.999
ADAM_EPS = 1e-08
ADAM_WD = 0.01
ADAM_STEP = 10

COL_V = 512
COL_CQ = 1024
COL_CKV = 1408
COL_KR = 1664
COL_QM = 1696
COL_GATE = 2208
IN_COLS = 5280

VMEM_LIMIT_BYTES = 56 * 1024 * 1024
INV_SQRT2 = 0.7071067811865476
INV_SQRT_2PI = 0.3989422804014327
LOG2E = 1.4426950408889634
ATTN_SCALE = MLA_QK ** -0.5
V_ONES_LANE = 64
ATTN_SCALE2 = ATTN_SCALE * LOG2E

SHARDED = (
    ("ffn1_w_gu", 1024, 1408, "col"),
    ("ffn1_w_down", 704, 1024, "row"),
    ("w_in", 1024, 1320, "col"),
    ("mla_w_uq", 384, 192, "col"),
    ("mla_w_ukv", 256, 256, "col"),
    ("mem_w_kv", 256, 1024, "row"),
    ("w_branch_a", 512, 256, "col"),
    ("w_branch_b", 512, 256, "col"),
    ("w_branch_c", 512, 256, "col"),
    ("w_out", 256, 1024, "row"),
    ("ffn2_w_gu", 1024, 1408, "col"),
    ("ffn2_w_down", 704, 1024, "row"),
)
SMALL = (
    ("ffn1_norm", (1, 1024)), ("mix_norm", (1, 1024)), ("b_gate", (1, 3072)),
    ("sg_ln_g", (1, 512)), ("sg_ln_b", (1, 512)), ("sg_w", (1, 8, 128, 128)),
    ("sg_b", (1, 8, 128)), ("mla_cq_norm", (1, 384)), ("mla_ckv_norm", (1, 256)),
    ("mla_q_norm", (1, 96)), ("mla_k_norm", (1, 96)), ("mem_norm", (1, 1024)),
    ("mem_q_norm", (1, 128)), ("mem_k_norm", (1, 128)), ("ffn2_norm", (1, 1024)),
)
WEIGHT_ORDER = (
    "ffn1_norm", "ffn1_w_gu", "ffn1_w_down", "mix_norm", "w_in", "b_gate", "sg_ln_g", "sg_ln_b",
    "sg_w", "sg_b", "mla_cq_norm", "mla_w_uq", "mla_ckv_norm", "mla_w_ukv", "mla_q_norm",
    "mla_k_norm", "mem_norm", "mem_w_kv", "mem_q_norm", "mem_k_norm", "w_branch_a", "w_branch_b",
    "w_branch_c", "w_out", "ffn2_norm", "ffn2_w_gu", "ffn2_w_down",
)

_N_SMALL = sum(int(np.prod(s)) for _, s in SMALL)
SMALL_ROWS = -(-_N_SMALL // (LANES * 8)) * 8

MESH = pl.DeviceIdType.MESH


def _cparams():
    return pltpu.CompilerParams(vmem_limit_bytes=VMEM_LIMIT_BYTES)


def _dot(a, b):
    return jnp.dot(a, b, preferred_element_type=F32)


def _dot_nt(a, b):
    return lax.dot_general(a, b, (((1,), (1,)), ((), ())), preferred_element_type=F32)


def _dot_tn(a, b):
    return lax.dot_general(a, b, (((0,), (0,)), ((), ())), preferred_element_type=F32)


def _gelu(x):
    return 0.5 * x * (1.0 + lax.erf(x * INV_SQRT2))


def _gelu_grad(x):
    return 0.5 * (1.0 + lax.erf(x * INV_SQRT2)) + x * jnp.exp(-0.5 * x * x) * INV_SQRT_2PI


def _rstd(x, n):
    return lax.rsqrt(jnp.sum(x * x, axis=-1, keepdims=True) * (1.0 / n) + EPS)


def _rms_vjp(x, r, g, dy, n):
    dxh = dy * g
    dx = r * dxh - x * (r * r * r) * (jnp.sum(dxh * x, axis=-1, keepdims=True) * (1.0 / n))
    return dx, dy * x * r


def _row_tile(t, want):
    return min(t, want)


def _wide_tile(n):
    if n <= 1024:
        return n
    if n % 1024 == 0:
        return 1024
    assert n % FF_TILE == 0, n
    return FF_TILE


def _mm_cols(a, ws, out_dtypes, name, ex=None):
    t, kdim = a.shape
    tm = _row_tile(t, 256)
    n = len(ws)

    def body(*refs):
        av = refs[0][...]
        for w_ref, o_ref in zip(refs[1:1 + n], refs[1 + n:]):
            o_ref[...] = _dot(av, w_ref[...]).astype(o_ref.dtype)

    row = lambda width: pl.BlockSpec((tm, width), lambda i: (i, 0))
    return _call_with_exchange(
        ex, body, name, (t // tm,),
        [row(kdim)] + [pl.BlockSpec(w.shape, lambda i: (0, 0)) for w in ws],
        [row(w.shape[1]) for w in ws],
        [jax.ShapeDtypeStruct((t, w.shape[1]), dt) for w, dt in zip(ws, out_dtypes)], [], (a, *ws))


def _proj_norm_bwd(dzs, wts, x, g, dres, name):
    t, d = x.shape
    tm = _row_tile(t, 256)
    n = len(dzs)

    def body(*refs):
        x_ref, g_ref, r_ref, dx_ref, dg_ref = refs[2 * n:]

        @pl.when(pl.program_id(0) == 0)
        def _():
            dg_ref[...] = jnp.zeros_like(dg_ref)

        dh = None
        for dz_ref, w_ref in zip(refs[:n], refs[n:2 * n]):
            part = _dot(dz_ref[...], w_ref[...])
            dh = part if dh is None else dh + part
        xv = x_ref[...]
        dx, dgr = _rms_vjp(xv, _rstd(xv, d), g_ref[...], dh, d)
        dx_ref[...] = r_ref[...] + dx
        dg_ref[...] += jnp.sum(dgr, axis=0, keepdims=True)

    row = lambda width: pl.BlockSpec((tm, width), lambda i: (i, 0))
    vec = pl.BlockSpec((1, d), lambda i: (0, 0))
    return pl.pallas_call(
        body, name=name, grid=(t // tm,),
        in_specs=[row(dz.shape[1]) for dz in dzs] + [pl.BlockSpec(w.shape, lambda i: (0, 0)) for w in wts]
        + [row(d), vec, row(d)],
        out_specs=[row(d), vec],
        out_shape=[jax.ShapeDtypeStruct((t, d), F32), jax.ShapeDtypeStruct((1, d), F32)],
        compiler_params=_cparams())(*dzs, *wts, x, g, dres)


def _mm_tn_cols(a, bs, name):
    t, m = a.shape
    tk = _row_tile(t, 512)
    n = len(bs)

    def body(*refs):
        @pl.when(pl.program_id(0) == 0)
        def _():
            for o_ref in refs[1 + n:]:
                o_ref[...] = jnp.zeros_like(o_ref)

        av = refs[0][...].astype(BF16)
        for b_ref, o_ref in zip(refs[1:1 + n], refs[1 + n:]):
            o_ref[...] += _dot_tn(av, b_ref[...].astype(BF16))

    row = lambda width: pl.BlockSpec((tk, width), lambda k: (k, 0))
    return pl.pallas_call(
        body, name=name, grid=(t // tk,), in_specs=[row(m)] + [row(b.shape[1]) for b in bs],
        out_specs=[pl.BlockSpec((m, b.shape[1]), lambda k: (0, 0)) for b in bs],
        out_shape=[jax.ShapeDtypeStruct((m, b.shape[1]), F32) for b in bs],
        compiler_params=_cparams())(a, *bs)


def _mm_tn(a, b, name, scale=1.0, ex=None, col_blocks=False, out_dtype=F32):
    t, m = a.shape
    n = b.shape[1]
    tm, tn = _wide_tile(m), _wide_tile(n)
    tk = _row_tile(t, 1024)
    nk = t // tk
    in_place = out_dtype == F32

    def body(a_ref, b_ref, o_ref, *scr):
        k = pl.program_id(2)
        acc_ref = o_ref if in_place else scr[0]

        @pl.when(k == 0)
        def _():
            acc_ref[...] = jnp.zeros_like(acc_ref)

        prod = _dot_tn(a_ref[...].astype(BF16), b_ref[...].astype(BF16))
        acc_ref[...] += prod.reshape(acc_ref.shape)
        if scale != 1.0 or not in_place:
            @pl.when(k == nk - 1)
            def _():
                o_ref[...] = (acc_ref[...] * scale).astype(out_dtype).reshape(o_ref.shape)

    if col_blocks:
        out_spec = pl.BlockSpec((1, tm, tn), lambda i, j, k: (j, i, 0))
        out_shape = jax.ShapeDtypeStruct((n // tn, m, tn), out_dtype)
    else:
        out_spec = pl.BlockSpec((tm, tn), lambda i, j, k: (i, j))
        out_shape = jax.ShapeDtypeStruct((m, n), out_dtype)
    outs = _call_with_exchange(
        ex, body, name, (m // tm, n // tn, nk),
        [pl.BlockSpec((tk, tm), lambda i, j, k: (k, i)), pl.BlockSpec((tk, tn), lambda i, j, k: (k, j))],
        [out_spec], [out_shape], [] if in_place else [pltpu.VMEM((tm, tn), F32)], (a, b))
    return outs[0] if ex is None else outs


class _Exchange(NamedTuple):
    operands: list
    out_shapes: list
    sem_shapes: list
    build: Callable


def _call_with_exchange(ex, body, name, grid, in_specs, out_specs, out_shape, scratch_shapes, operands, prefetch=()):
    n_pre = len(prefetch)

    def call(kernel, ins, outs, shapes, scratch):
        if n_pre:
            spec = pltpu.PrefetchScalarGridSpec(num_scalar_prefetch=n_pre, grid=grid, in_specs=ins, out_specs=outs,
                                                scratch_shapes=scratch)
            return pl.pallas_call(kernel, name=name, grid_spec=spec, out_shape=shapes, compiler_params=_cparams())
        return pl.pallas_call(kernel, name=name, grid=grid, in_specs=ins, out_specs=outs, out_shape=shapes,
                              scratch_shapes=scratch, compiler_params=_cparams())

    if ex is None:
        return call(body, in_specs, out_specs, out_shape, scratch_shapes)(*prefetch, *operands)
    n_in, n_out, n_scr = len(in_specs), len(out_specs), len(scratch_shapes)
    k_in, k_out = len(ex.operands), len(ex.out_shapes)

    def carried(*refs):
        pre, refs = refs[:n_pre], refs[n_pre:]
        a, b = n_in, n_in + k_in
        c, e = b + n_out, b + n_out + k_out
        f = e + n_scr
        start, finish = ex.build(refs[a:b], refs[c:e], refs[f:])
        steps = [pl.program_id(ax) for ax in range(len(grid))]
        first = functools.reduce(jnp.logical_and, [s == 0 for s in steps])
        last = functools.reduce(jnp.logical_and, [s == n - 1 for s, n in zip(steps, grid)])
        pl.when(first)(start)
        body(*pre, *refs[:a], *refs[b:c], *refs[e:f])
        pl.when(last)(finish)

    return call(carried, list(in_specs) + [ANY] * k_in, list(out_specs) + [ANY] * k_out,
                list(out_shape) + list(ex.out_shapes), list(scratch_shapes) + list(ex.sem_shapes),
                )(*prefetch, *operands, *ex.operands)


def _run_exchange(ex, name):
    k_in, k_out = len(ex.operands), len(ex.out_shapes)

    def body(*refs):
        start, finish = ex.build(refs[:k_in], refs[k_in:k_in + k_out], refs[k_in + k_out:])
        start()
        finish()

    return pl.pallas_call(body, name=name, in_specs=[ANY] * k_in, out_specs=[ANY] * k_out,
                          out_shape=list(ex.out_shapes), scratch_shapes=list(ex.sem_shapes))(*ex.operands)


def _ffn_fwd(x, g, wgu4, wd2, name, ex=None, next_gain=None, target=None):
    t, d = x.shape
    tm = _row_tile(t, 512)
    assert next_gain is None or target is None
    extra = [a for a in (next_gain, target) if a is not None]

    def body(*refs):
        x_ref, g_ref, wg_ref, wu_ref, wd_ref = refs[:5]
        e_ref = refs[5] if extra else None
        outs, (xn_scr, acc_scr) = refs[5 + len(extra):-2], refs[-2:]
        if target is not None:
            dy_ref, loss_ref, gg_ref, uu_ref = outs
        elif next_gain is not None:
            o_ref, gg_ref, uu_ref, h_ref = outs
        else:
            o_ref, gg_ref, uu_ref = outs
        i, j = pl.program_id(0), pl.program_id(1)

        @pl.when(j == 0)
        def _():
            xv = x_ref[...]
            xn_scr[...] = (xv * _rstd(xv, d) * g_ref[...]).astype(BF16)
            acc_scr[...] = jnp.zeros_like(acc_scr)

        if target is not None:
            @pl.when((i == 0) & (j == 0))
            def _():
                loss_ref[...] = jnp.zeros_like(loss_ref)

        xn = xn_scr[...]
        gg = _dot(xn, wg_ref[0])
        uu = _dot(xn, wu_ref[0])
        gg_ref[...] = gg.astype(BF16)
        uu_ref[...] = uu.astype(BF16)
        act = gg * jax.nn.sigmoid(gg) * uu
        acc_scr[...] += _dot(act.astype(BF16), wd_ref[0])

        @pl.when(j == 1)
        def _():
            y = x_ref[...] + 0.5 * acc_scr[...]
            if target is not None:
                e = y - e_ref[...]
                dy_ref[...] = e * (1.0 / d)
                part = 0.5 * jnp.sum(jnp.sum(e * e, axis=-1, keepdims=True) * (1.0 / d), axis=0, keepdims=True)
                loss_ref[...] += jnp.broadcast_to(part, loss_ref.shape)
            else:
                o_ref[...] = y
                if next_gain is not None:
                    h_ref[...] = (y * _rstd(y, d) * e_ref[...]).astype(BF16)

    row = pl.BlockSpec((tm, d), lambda i, j: (i, 0))
    vec = pl.BlockSpec((1, d), lambda i, j: (0, 0))
    ffb = pl.BlockSpec((tm, FF_TILE), lambda i, j: (i, j))
    f32_rows, bf16_ff = jax.ShapeDtypeStruct((t, d), F32), jax.ShapeDtypeStruct((t, D_FF), BF16)
    if target is not None:
        extra_spec, out_specs = [row], [row, pl.BlockSpec((1, LANES), lambda i, j: (0, 0)), ffb, ffb]
        out_shape = [f32_rows, jax.ShapeDtypeStruct((1, LANES), F32), bf16_ff, bf16_ff]
    elif next_gain is not None:
        extra_spec, out_specs = [vec], [row, ffb, ffb, row]
        out_shape = [f32_rows, bf16_ff, bf16_ff, jax.ShapeDtypeStruct((t, d), BF16)]
    else:
        extra_spec, out_specs, out_shape = [], [row, ffb, ffb], [f32_rows, bf16_ff, bf16_ff]
    return _call_with_exchange(
        ex, body, name, (t // tm, 2),
        [row, vec,
         pl.BlockSpec((1, d, FF_TILE), lambda i, j: (j, 0, 0)),
         pl.BlockSpec((1, d, FF_TILE), lambda i, j: (j + 2, 0, 0)),
         pl.BlockSpec((1, FF_TILE, d), lambda i, j: (j, 0, 0))] + extra_spec,
        out_specs, out_shape,
        [pltpu.VMEM((tm, d), BF16), pltpu.VMEM((tm, d), F32)], (x, g, wgu4, wgu4, wd2, *extra))


def _ffn_bwd(x, g, dy, gpre, upre, wgu4, wd2, name, ex=None):
    t, d = x.shape
    tm = _row_tile(t, 512)

    def body(dy_ref, gg_ref, uu_ref, wgu_hbm, wd_hbm, dg_ref, du_ref, act_ref, part_ref, wg_ref, wu_ref, wd_ref):
        j = pl.program_id(0)

        @pl.when(pl.program_id(1) == 0)
        def _():
            pltpu.sync_copy(wgu_hbm.at[j], wg_ref.at[0])
            pltpu.sync_copy(wgu_hbm.at[j + 2], wu_ref.at[0])
            pltpu.sync_copy(wd_hbm.at[j], wd_ref.at[0])

        gg = gg_ref[...].astype(F32)
        uu = uu_ref[...].astype(F32)
        sg = jax.nn.sigmoid(gg)
        silu = gg * sg
        act_ref[...] = (silu * uu).astype(BF16)
        dyh = (0.5 * dy_ref[...]).astype(BF16)
        dact = _dot_nt(dyh, wd_ref[0])
        du = (dact * silu).astype(BF16)
        dgt = (dact * uu * (sg * (1.0 + gg * (1.0 - sg)))).astype(BF16)
        du_ref[...] = du
        dg_ref[...] = dgt
        part_ref[0] = _dot_nt(dgt, wg_ref[0]) + _dot_nt(du, wu_ref[0])

    row = pl.BlockSpec((tm, d), lambda j, i: (i, 0))
    ffb = pl.BlockSpec((tm, FF_TILE), lambda j, i: (i, j))
    dgt, dup, act, parts, *got = _call_with_exchange(
        ex, body, name, (2, t // tm),
        [row, ffb, ffb, ANY, ANY],
        [ffb, ffb, ffb, pl.BlockSpec((1, tm, d), lambda j, i: (j, i, 0))],
        [jax.ShapeDtypeStruct((t, D_FF), BF16)] * 3 + [jax.ShapeDtypeStruct((2, t, d), F32)],
        [pltpu.VMEM((1, d, FF_TILE), BF16), pltpu.VMEM((1, d, FF_TILE), BF16), pltpu.VMEM((1, FF_TILE, d), BF16)],
        (dy, gpre, upre, wgu4, wd2))

    def norm_body(x_ref, g_ref, p_ref, dy_ref, dx_ref, dgain_ref, xn_ref):
        @pl.when(pl.program_id(0) == 0)
        def _():
            dgain_ref[...] = jnp.zeros_like(dgain_ref)

        xv = x_ref[...]
        r = _rstd(xv, d)
        xn_ref[...] = (xv * r * g_ref[...]).astype(BF16)
        dx, dgr = _rms_vjp(xv, r, g_ref[...], p_ref[0] + p_ref[1], d)
        dx_ref[...] = dy_ref[...] + dx
        dgain_ref[...] += jnp.sum(dgr, axis=0, keepdims=True)

    tn = _row_tile(t, 256)
    nrow = pl.BlockSpec((tn, d), lambda i: (i, 0))
    vec = pl.BlockSpec((1, d), lambda i: (0, 0))
    dx, dgain, xn = pl.pallas_call(
        norm_body, name=name + "_norm", grid=(t // tn,),
        in_specs=[nrow, vec, pl.BlockSpec((2, tn, d), lambda i: (0, i, 0)), nrow],
        out_specs=[nrow, vec, nrow],
        out_shape=[jax.ShapeDtypeStruct((t, d), F32), jax.ShapeDtypeStruct((1, d), F32),
                   jax.ShapeDtypeStruct((t, d), BF16)],
        compiler_params=_cparams())(x, g, parts, dy)
    return [dx, dgain, xn, dgt, dup, act] + got


def _sgu_layernorm(vpre, lg, lb):
    v = _gelu(vpre)
    mu = jnp.mean(v, axis=-1, keepdims=True)
    xc = v - mu
    rstd = lax.rsqrt(jnp.mean(xc * xc, axis=-1, keepdims=True) + EPS)
    xhat = xc * rstd
    return xhat, rstd, xhat * lg + lb


def _sgu_fwd(zuv, lg, lb, wt, bias_l, name):
    t = zuv.shape[0]
    tm = _row_tile(t, 512)

    def body(u_ref, v_ref, lg_ref, lb_ref, wt_ref, bl_ref, o_ref, vln_scr):
        _, _, vln = _sgu_layernorm(v_ref[...], lg_ref[...], lb_ref[...])
        vln_scr[...] = vln.astype(BF16)
        lo = lax.broadcasted_iota(jnp.int32, (CHUNK, LANES), 1) < 64
        for c in range(tm // CHUNK):
            rows = slice(c * CHUNK, (c + 1) * CHUNK)
            for p in range(SG_GROUPS // 2):
                cols = slice(p * LANES, (p + 1) * LANES)
                vp = vln_scr[rows, cols]
                mixed = jnp.where(lo, _dot(wt_ref[2 * p], vp), _dot(wt_ref[2 * p + 1], vp)) + bl_ref[:, cols]
                o_ref[rows, cols] = (_gelu(u_ref[rows, cols]) * mixed).astype(BF16)

    half = lambda k: pl.BlockSpec((tm, SG_WIDTH), lambda i: (i, k))
    vec = pl.BlockSpec((1, SG_WIDTH), lambda i: (0, 0))
    return pl.pallas_call(
        body, name=name, grid=(t // tm,),
        in_specs=[half(0), half(1), vec, vec,
                  pl.BlockSpec((SG_GROUPS, CHUNK, CHUNK), lambda i: (0, 0, 0)),
                  pl.BlockSpec((CHUNK, SG_WIDTH), lambda i: (0, 0))],
        out_specs=pl.BlockSpec((tm, SG_WIDTH), lambda i: (i, 0)),
        out_shape=jax.ShapeDtypeStruct((t, SG_WIDTH), BF16),
        scratch_shapes=[pltpu.VMEM((tm, SG_WIDTH), BF16)],
        compiler_params=_cparams())(zuv, zuv, lg, lb, wt, bias_l)


def _sgu_bwd(zuv, dya, lg, lb, wt, wt_t, bias_l, name):
    t = zuv.shape[0]
    tm = _row_tile(t, 256)
    nsteps = t // tm

    def body(u_ref, v_ref, dy_ref, lg_ref, lb_ref, wt_ref, wtt_ref, bl_ref,
             dz_ref, dwt_ref, dbl_ref, dlg_ref, dlb_ref, vln_scr, dvln_scr, dbacc_scr):
        step = pl.program_id(0)

        @pl.when(step == 0)
        def _():
            dwt_ref[...] = jnp.zeros_like(dwt_ref)
            dlg_ref[...] = jnp.zeros_like(dlg_ref)
            dlb_ref[...] = jnp.zeros_like(dlb_ref)
            dbl_ref[...] = jnp.zeros_like(dbl_ref)
            dbacc_scr[...] = jnp.zeros_like(dbacc_scr)

        vpre = v_ref[...]
        lgv = lg_ref[...]
        xhat, rstd, vln = _sgu_layernorm(vpre, lgv, lb_ref[...])
        vln_scr[...] = vln.astype(BF16)
        lo = lax.broadcasted_iota(jnp.int32, (CHUNK, LANES), 1) < 64
        for c in range(tm // CHUNK):
            rows = slice(c * CHUNK, (c + 1) * CHUNK)
            for p in range(SG_GROUPS // 2):
                cols = slice(p * LANES, (p + 1) * LANES)
                vp = vln_scr[rows, cols]
                mixed = jnp.where(lo, _dot(wt_ref[2 * p], vp), _dot(wt_ref[2 * p + 1], vp)) + bl_ref[:, cols]
                upre = u_ref[rows, cols]
                dyp = dy_ref[rows, cols]
                dz_ref[rows, cols] = (dyp * mixed * _gelu_grad(upre)).astype(BF16)
                dm = dyp * _gelu(upre)
                dbacc_scr[:, cols] += dm
                dlo = jnp.where(lo, dm, 0.0).astype(BF16)
                dhi = jnp.where(lo, 0.0, dm).astype(BF16)
                dvln_scr[rows, cols] = _dot(wtt_ref[2 * p], dlo) + _dot(wtt_ref[2 * p + 1], dhi)
                dwt_ref[2 * p] += _dot_nt(dlo, vp)
                dwt_ref[2 * p + 1] += _dot_nt(dhi, vp)
        dvln = dvln_scr[...]
        dlg_ref[...] += jnp.sum(dvln * xhat, axis=0, keepdims=True)
        dlb_ref[...] += jnp.sum(dvln, axis=0, keepdims=True)
        dxh = dvln * lgv
        dv = rstd * (dxh - jnp.mean(dxh, axis=-1, keepdims=True)
                     - xhat * jnp.mean(dxh * xhat, axis=-1, keepdims=True))
        dz_ref[:, SG_WIDTH:] = (dv * _gelu_grad(vpre)).astype(BF16)

        @pl.when(step == nsteps - 1)
        def _():
            rr = lax.broadcasted_iota(jnp.int32, (CHUNK, CHUNK), 0)
            cc = lax.broadcasted_iota(jnp.int32, (CHUNK, CHUNK), 1)
            tril = (cc <= rr).astype(F32)
            for gidx in range(SG_GROUPS):
                dwt_ref[gidx] = dwt_ref[gidx] * tril
            kk = lax.broadcasted_iota(jnp.int32, (SG_WIDTH, LANES), 0)
            gg = lax.broadcasted_iota(jnp.int32, (SG_WIDTH, LANES), 1)
            sel = ((kk // 64) == gg).astype(F32)
            dbl_ref[...] = jnp.dot(dbacc_scr[...], sel, preferred_element_type=F32,
                                   precision=lax.Precision.HIGHEST)

    half = lambda k: pl.BlockSpec((tm, SG_WIDTH), lambda i: (i, k))
    vec = pl.BlockSpec((1, SG_WIDTH), lambda i: (0, 0))
    wspec = pl.BlockSpec((SG_GROUPS, CHUNK, CHUNK), lambda i: (0, 0, 0))
    return pl.pallas_call(
        body, name=name, grid=(nsteps,),
        in_specs=[half(0), half(1), pl.BlockSpec((tm, SG_WIDTH), lambda i: (i, 0)), vec, vec,
                  wspec, wspec, pl.BlockSpec((CHUNK, SG_WIDTH), lambda i: (0, 0))],
        out_specs=[pl.BlockSpec((tm, 2 * SG_WIDTH), lambda i: (i, 0)), wspec,
                   pl.BlockSpec((CHUNK, LANES), lambda i: (0, 0)), vec, vec],
        out_shape=[jax.ShapeDtypeStruct((t, 2 * SG_WIDTH), BF16),
                   jax.ShapeDtypeStruct((SG_GROUPS, CHUNK, CHUNK), F32),
                   jax.ShapeDtypeStruct((CHUNK, LANES), F32),
                   jax.ShapeDtypeStruct((1, SG_WIDTH), F32), jax.ShapeDtypeStruct((1, SG_WIDTH), F32)],
        scratch_shapes=[pltpu.VMEM((tm, SG_WIDTH), BF16), pltpu.VMEM((tm, SG_WIDTH), F32),
                        pltpu.VMEM((CHUNK, SG_WIDTH), F32)],
        compiler_params=_cparams())(zuv, zuv, dya, lg, lb, wt, wt_t, bias_l)


def _rope(x, c, s1, s2):
    return x * c + pltpu.roll(x, LANES - 16, 1) * s1 + pltpu.roll(x, 16, 1) * s2


def _rope_t(dy, c, s1, s2):
    return dy * c + pltpu.roll(dy * s1, 16, 1) + pltpu.roll(dy * s2, LANES - 16, 1)


def _mla_prep_fwd(zcq, zckv, zkr, gcq, gckv, qg, kg, wuq, wuk, wuv, rc, rs1, rs2, name, ex=None):
    t = zcq.shape[0]
    tm = _row_tile(t, 256)
    hd = MLA_HEADS * LANES

    def body(zcq_ref, zckv_ref, zkr_ref, gcq_ref, gckv_ref, qg_ref, kg_ref, wuq_ref, wuk_ref, wuv_ref,
             c_ref, s1_ref, s2_ref, q_ref, k_ref, v_ref, cqn_ref, ckvn_ref):
        c, s1, s2 = c_ref[...], s1_ref[...], s2_ref[...]
        xq = zcq_ref[...]
        cqn = (xq * _rstd(xq, MLA_Q_RANK) * gcq_ref[...]).astype(BF16)
        cqn_ref[...] = cqn
        ql = _dot(cqn, wuq_ref[...])
        xk = zckv_ref[...]
        ckvn = (xk * _rstd(xk, MLA_KV_RANK) * gckv_ref[...]).astype(BF16)
        ckvn_ref[...] = ckvn
        kl = _dot(ckvn, wuk_ref[...])
        slot_lane = lax.broadcasted_iota(jnp.int32, (tm, hd), 1) % LANES
        v_ref[...] = jnp.where(slot_lane == V_ONES_LANE, 1.0, _dot(ckvn, wuv_ref[...])).astype(BF16)
        kr = zkr_ref[...]
        for h in range(MLA_HEADS):
            sl = slice(h * LANES, (h + 1) * LANES)
            qh = ql[:, sl]
            q_ref[:, sl] = (_rope(qh * _rstd(qh, MLA_QK) * qg_ref[...], c, s1, s2) * ATTN_SCALE2).astype(BF16)
            kh = kl[:, sl] + kr
            k_ref[:, sl] = _rope(kh * _rstd(kh, MLA_QK) * kg_ref[...], c, s1, s2).astype(BF16)

    row = lambda n: pl.BlockSpec((tm, n), lambda i: (i, 0))
    full = lambda a: pl.BlockSpec(a.shape, lambda i: (0, 0))
    return _call_with_exchange(
        ex, body, name, (t // tm,),
        [row(MLA_Q_RANK), row(MLA_KV_RANK), row(LANES), full(gcq), full(gckv), full(qg), full(kg),
         full(wuq), full(wuk), full(wuv), row(LANES), row(LANES), row(LANES)],
        [row(hd), row(hd), row(hd), row(MLA_Q_RANK), row(MLA_KV_RANK)],
        [jax.ShapeDtypeStruct((t, hd), BF16)] * 3
        + [jax.ShapeDtypeStruct((t, MLA_Q_RANK), BF16), jax.ShapeDtypeStruct((t, MLA_KV_RANK), BF16)],
        [], (zcq, zckv, zkr, gcq, gckv, qg, kg, wuq, wuk, wuv, rc, rs1, rs2))


def _mla_prep_bwd(zcq, zckv, zkr, gcq, gckv, qg, kg, wuq, wuk, wuv, rc, rs1, rs2, dq, dk, dv, name):
    t = zcq.shape[0]
    tm = _row_tile(t, 256)
    hd = MLA_HEADS * LANES

    def body(zcq_ref, zckv_ref, zkr_ref, gcq_ref, gckv_ref, qg_ref, kg_ref, wuq_ref, wuk_ref, wuv_ref,
             c_ref, s1_ref, s2_ref, dq_ref, dk_ref, dv_ref,
             dzcq_ref, dzckv_ref, dzkr_ref, dql_ref, dkl_ref, dgcq_ref, dgckv_ref, dqg_ref, dkg_ref):
        @pl.when(pl.program_id(0) == 0)
        def _():
            for ref in (dgcq_ref, dgckv_ref, dqg_ref, dkg_ref):
                ref[...] = jnp.zeros_like(ref)

        c, s1, s2 = c_ref[...], s1_ref[...], s2_ref[...]
        qgv, kgv = qg_ref[...], kg_ref[...]
        xq = zcq_ref[...]
        rq = _rstd(xq, MLA_Q_RANK)
        ql = _dot((xq * rq * gcq_ref[...]).astype(BF16), wuq_ref[...])
        xk = zckv_ref[...]
        rk = _rstd(xk, MLA_KV_RANK)
        kl = _dot((xk * rk * gckv_ref[...]).astype(BF16), wuk_ref[...])
        kr = zkr_ref[...]
        dqg_acc = jnp.zeros((tm, LANES), F32)
        dkg_acc = jnp.zeros((tm, LANES), F32)
        dkr = jnp.zeros((tm, LANES), F32)
        for h in range(MLA_HEADS):
            sl = slice(h * LANES, (h + 1) * LANES)
            qh = ql[:, sl]
            dqh, dgr = _rms_vjp(qh, _rstd(qh, MLA_QK), qgv, _rope_t(dq_ref[:, sl], c, s1, s2), MLA_QK)
            dql_ref[:, sl] = dqh.astype(BF16)
            dqg_acc += dgr
            kh = kl[:, sl] + kr
            dkh, dgr = _rms_vjp(kh, _rstd(kh, MLA_QK), kgv, _rope_t(dk_ref[:, sl], c, s1, s2), MLA_QK)
            dkl_ref[:, sl] = dkh.astype(BF16)
            dkg_acc += dgr
            dkr += dkh
        dqg_ref[...] += jnp.sum(dqg_acc, axis=0, keepdims=True)
        dkg_ref[...] += jnp.sum(dkg_acc, axis=0, keepdims=True)
        lane = lax.broadcasted_iota(jnp.int32, (tm, LANES), 1)
        dzkr_ref[...] = jnp.where((lane >= MLA_NOPE) & (lane < MLA_QK), dkr, 0.0).astype(BF16)
        dcqn = _dot_nt(dql_ref[...], wuq_ref[...])
        dx, dgr = _rms_vjp(xq, rq, gcq_ref[...], dcqn, MLA_Q_RANK)
        dzcq_ref[...] = dx.astype(BF16)
        dgcq_ref[...] += jnp.sum(dgr, axis=0, keepdims=True)
        dckvn = _dot_nt(dkl_ref[...], wuk_ref[...]) + _dot_nt(dv_ref[...].astype(BF16), wuv_ref[...])
        dx, dgr = _rms_vjp(xk, rk, gckv_ref[...], dckvn, MLA_KV_RANK)
        dzckv_ref[...] = dx.astype(BF16)
        dgckv_ref[...] += jnp.sum(dgr, axis=0, keepdims=True)

    row = lambda n: pl.BlockSpec((tm, n), lambda i: (i, 0))
    full = lambda a: pl.BlockSpec(a.shape, lambda i: (0, 0))
    vec = lambda n: pl.BlockSpec((1, n), lambda i: (0, 0))
    return pl.pallas_call(
        body, name=name, grid=(t // tm,),
        in_specs=[row(MLA_Q_RANK), row(MLA_KV_RANK), row(LANES), full(gcq), full(gckv), full(qg), full(kg),
                  full(wuq), full(wuk), full(wuv), row(LANES), row(LANES), row(LANES), row(hd), row(hd), row(hd)],
        out_specs=[row(MLA_Q_RANK), row(MLA_KV_RANK), row(LANES), row(hd), row(hd),
                   vec(MLA_Q_RANK), vec(MLA_KV_RANK), vec(LANES), vec(LANES)],
        out_shape=[jax.ShapeDtypeStruct((t, MLA_Q_RANK), BF16), jax.ShapeDtypeStruct((t, MLA_KV_RANK), BF16),
                   jax.ShapeDtypeStruct((t, LANES), BF16), jax.ShapeDtypeStruct((t, hd), BF16),
                   jax.ShapeDtypeStruct((t, hd), BF16), jax.ShapeDtypeStruct((1, MLA_Q_RANK), F32),
                   jax.ShapeDtypeStruct((1, MLA_KV_RANK), F32), jax.ShapeDtypeStruct((1, LANES), F32),
                   jax.ShapeDtypeStruct((1, LANES), F32)],
        compiler_params=_cparams(),
    )(zcq, zckv, zkr, gcq, gckv, qg, kg, wuq, wuk, wuv, rc, rs1, rs2, dq, dk, dv)


def _attn_tiles(t):
    tq = 512 if t >= 2048 else 128
    return tq, min(t, 4 * tq), min(t, 2 * tq)


def _causal_keep(tq, tk, i, j):
    row = lax.broadcasted_iota(jnp.int32, (tq, tk), 0)
    col = lax.broadcasted_iota(jnp.int32, (tq, tk), 1)
    return (col - row) <= (i * tq - j * tk)


def _causal_keep_t(tq, tk, i, j):
    key = lax.broadcasted_iota(jnp.int32, (tk, tq), 0)
    qry = lax.broadcasted_iota(jnp.int32, (tk, tq), 1)
    return (key - qry) <= (i * tq - j * tk)


ATTN_FWD_HEADS_PER_STEP = 2
ATTN_BWD_HEADS_PER_STEP = 2


def _attn_fwd(q, k, v, name, ex=None):
    t, hd = q.shape
    hp = ATTN_FWD_HEADS_PER_STEP
    tq, tk, _ = _attn_tiles(t)
    pairs = [(i, j) for i in range(t // tq) for j in range(((i + 1) * tq - 1) // tk + 1)]
    ii = np.array([p[0] for p in pairs], np.int32)
    jj = np.array([p[1] for p in pairs], np.int32)

    def body(ii_ref, jj_ref, q_ref, k_ref, v_ref, o_ref, lse_ref, m_scr, acc_scr):
        s_id = pl.program_id(1)
        i, j = ii_ref[s_id], jj_ref[s_id]
        last = j == ((i + 1) * tq - 1) // tk
        ones_lane = lax.broadcasted_iota(jnp.int32, (tq, LANES), 1) == V_ONES_LANE

        @pl.when(j == 0)
        def _():
            m_scr[...] = jnp.full_like(m_scr, NEG)
            acc_scr[...] = jnp.zeros_like(acc_scr)

        def step(masked):
            for hh in range(hp):
                sl = slice(hh * LANES, (hh + 1) * LANES)
                s = _dot_nt(q_ref[:, sl], k_ref[:, sl])
                if masked:
                    s = jnp.where(_causal_keep(tq, tk, i, j), s, NEG)
                m_prev = m_scr[hh]
                m_new = jnp.maximum(m_prev, jnp.max(s, axis=1, keepdims=True))
                p = jnp.exp2(s - m_new)
                alpha = jnp.exp2(m_prev - m_new)
                acc = alpha * acc_scr[:, sl] + _dot(p.astype(BF16), v_ref[:, sl])
                if masked:
                    l_new = jnp.sum(jnp.where(ones_lane, acc, 0.0), axis=1, keepdims=True)
                    o_ref[:, sl] = (acc / l_new).astype(BF16)
                    lse_ref[:, sl] = jnp.broadcast_to(m_new + jnp.log(l_new) * LOG2E, (tq, LANES))
                else:
                    acc_scr[:, sl] = acc
                    m_scr[hh] = m_new

        @pl.when(jnp.logical_not(last))
        def _():
            step(False)

        @pl.when(last)
        def _():
            step(True)

    w = hp * LANES
    qspec = pl.BlockSpec((tq, w), lambda h, s, ii_r, jj_r: (ii_r[s], h))
    kspec = pl.BlockSpec((tk, w), lambda h, s, ii_r, jj_r: (jj_r[s], h))
    return _call_with_exchange(
        ex, body, name, (hd // w, len(pairs)), [qspec, kspec, kspec], [qspec, qspec],
        [jax.ShapeDtypeStruct((t, hd), BF16), jax.ShapeDtypeStruct((t, hd), F32)],
        [pltpu.VMEM((hp, tq, 1), F32), pltpu.VMEM((tq, w), F32)], (q, k, v),
        prefetch=(jnp.asarray(ii), jnp.asarray(jj)))


def _attn_bwd_rows(o, lse, do, name):
    t, hd = o.shape
    heads = hd // LANES
    tm = _row_tile(t, 512)

    def body(o_ref, lse_ref, do_ref, out_ref):
        lane = lax.broadcasted_iota(jnp.int32, (tm, LANES), 1)
        acc = jnp.zeros((tm, LANES), F32)
        for h in range(heads):
            sl = slice(h * LANES, (h + 1) * LANES)
            delta = jnp.sum(do_ref[:, sl].astype(F32) * o_ref[:, sl].astype(F32), axis=1, keepdims=True)
            acc = jnp.where(lane == h, delta, acc)
            acc = jnp.where(lane == heads + h, lse_ref[:, sl], acc)
        out_ref[...] = acc

    row = pl.BlockSpec((tm, hd), lambda i: (i, 0))
    cols = pl.pallas_call(
        body, name=name, grid=(t // tm,), in_specs=[row, row, row],
        out_specs=pl.BlockSpec((tm, LANES), lambda i: (i, 0)),
        out_shape=jax.ShapeDtypeStruct((t, LANES), F32), compiler_params=_cparams())(o, lse, do)
    rows = cols.T
    return rows[:heads].reshape(heads, 1, t), rows[heads:2 * heads].reshape(heads, 1, t)


def _attn_bwd(q, k, v, delta_rows, lse_rows, do, name):
    t, hd = q.shape
    hp = ATTN_BWD_HEADS_PER_STEP
    tq, _, tk = _attn_tiles(t)
    nq = t // tq
    pairs = [(i, j) for j in range(t // tk) for i in range((j * tk) // tq, nq)]
    ii = np.array([p[0] for p in pairs], np.int32)
    jj = np.array([p[1] for p in pairs], np.int32)

    def body(jj_ref, ii_ref, q_ref, k_ref, v_ref, delta_ref, lse_ref, do_ref, dq_ref, dk_ref, dv_ref,
             dk_scr, dv_scr):
        s_id = pl.program_id(1)
        i, j = ii_ref[s_id], jj_ref[s_id]

        @pl.when(s_id == 0)
        def _():
            dq_ref[...] = jnp.zeros_like(dq_ref)

        @pl.when(i == (j * tk) // tq)
        def _():
            dk_scr[...] = jnp.zeros_like(dk_scr)
            dv_scr[...] = jnp.zeros_like(dv_scr)

        rows = pl.ds(pl.multiple_of(i * tq, tq), tq)

        def step(masked):
            for hh in range(hp):
                sl = slice(hh * LANES, (hh + 1) * LANES)
                qv, kv, dov = q_ref[:, sl], k_ref[:, sl], do_ref[:, sl]
                st = _dot_nt(kv, qv)
                if masked:
                    st = jnp.where(_causal_keep_t(tq, tk, i, j), st, NEG)
                pt = jnp.exp2(st - lse_ref[hh])
                dv_scr[:, sl] += _dot(pt.astype(BF16), dov)
                dpt = _dot_nt(v_ref[:, sl], dov)
                dst = (pt * (dpt - delta_ref[hh]) * ATTN_SCALE).astype(BF16)
                dk_scr[:, sl] += _dot(dst, qv)
                dq_ref[rows, sl] += _dot_tn(dst, kv)

        crosses = (j + 1) * tk - 1 > i * tq

        @pl.when(jnp.logical_not(crosses))
        def _():
            step(False)

        @pl.when(crosses)
        def _():
            step(True)

        @pl.when(i == nq - 1)
        def _():
            dk_ref[...] = dk_scr[...] * (1.0 / ATTN_SCALE2)
            dv_ref[...] = dv_scr[...]

    w = hp * LANES
    qspec = pl.BlockSpec((tq, w), lambda h, s, jj_r, ii_r: (ii_r[s], h))
    kspec = pl.BlockSpec((tk, w), lambda h, s, jj_r, ii_r: (jj_r[s], h))
    rspec = pl.BlockSpec((hp, 1, tq), lambda h, s, jj_r, ii_r: (h, 0, ii_r[s]))
    return pl.pallas_call(
        body, name=name,
        grid_spec=pltpu.PrefetchScalarGridSpec(
            num_scalar_prefetch=2, grid=(hd // w, len(pairs)),
            in_specs=[qspec, kspec, kspec, rspec, rspec, qspec],
            out_specs=[pl.BlockSpec((t, w), lambda h, s, jj_r, ii_r: (0, h)), kspec, kspec],
            scratch_shapes=[pltpu.VMEM((tk, w), F32), pltpu.VMEM((tk, w), F32)]),
        out_shape=[jax.ShapeDtypeStruct((t, hd), F32)] * 3,
        compiler_params=_cparams())(jnp.asarray(jj), jnp.asarray(ii), q, k, v, delta_rows, lse_rows, do)


MEM_W = MEM_HEADS * LANES


def _mem_kv_fwd(mem, gmem, wkv, kg, name):
    m, d = mem.shape

    def body(mem_ref, g_ref, w_ref, kg_ref, k_ref, v_ref, mn_ref):
        xv = mem_ref[...]
        mn = (xv * _rstd(xv, d) * g_ref[...]).astype(BF16)
        mn_ref[...] = mn
        kvm = _dot(mn, w_ref[...])
        v_ref[...] = kvm[:, MEM_W:].astype(BF16)
        for h in range(MEM_HEADS):
            sl = slice(h * LANES, (h + 1) * LANES)
            kh = kvm[:, sl]
            k_ref[:, sl] = (kh * _rstd(kh, LANES) * kg_ref[...]).astype(BF16)

    full = lambda a: pl.BlockSpec(a.shape, lambda i: (0, 0))
    return pl.pallas_call(
        body, name=name, grid=(1,), in_specs=[full(mem), full(gmem), full(wkv), full(kg)],
        out_specs=[pl.BlockSpec((m, MEM_W), lambda i: (0, 0)), pl.BlockSpec((m, MEM_W), lambda i: (0, 0)),
                   pl.BlockSpec((m, d), lambda i: (0, 0))],
        out_shape=[jax.ShapeDtypeStruct((m, MEM_W), BF16), jax.ShapeDtypeStruct((m, MEM_W), BF16),
                   jax.ShapeDtypeStruct((m, d), BF16)],
        compiler_params=_cparams())(mem, gmem, wkv, kg)


def _mem_softmax(qn, kh):
    s = _dot_nt(qn, kh) * (LANES ** -0.5)
    e = jnp.exp(s - jnp.max(s, axis=1, keepdims=True))
    return e / jnp.sum(e, axis=1, keepdims=True)


def _mem_attn_fwd(zqm, qg, km, vm, name):
    t = zqm.shape[0]
    tm = _row_tile(t, 512)

    def body(q_ref, qg_ref, k_ref, v_ref, o_ref):
        for h in range(MEM_HEADS):
            sl = slice(h * LANES, (h + 1) * LANES)
            qh = q_ref[:, sl]
            qn = (qh * _rstd(qh, LANES) * qg_ref[...]).astype(BF16)
            p = _mem_softmax(qn, k_ref[:, sl])
            o_ref[:, sl] = _dot(p.astype(BF16), v_ref[:, sl]).astype(BF16)

    row = pl.BlockSpec((tm, MEM_W), lambda i: (i, 0))
    full = lambda a: pl.BlockSpec(a.shape, lambda i: (0, 0))
    return pl.pallas_call(
        body, name=name, grid=(t // tm,), in_specs=[row, full(qg), full(km), full(vm)], out_specs=row,
        out_shape=jax.ShapeDtypeStruct((t, MEM_W), BF16), compiler_params=_cparams())(zqm, qg, km, vm)


def _mem_attn_bwd(zqm, dyc, qg, km, vm, name):
    t = zqm.shape[0]
    m = km.shape[0]
    tm = _row_tile(t, 256)

    def body(q_ref, dy_ref, qg_ref, k_ref, v_ref, dz_ref, dk_ref, dv_ref, dqg_ref):
        @pl.when(pl.program_id(0) == 0)
        def _():
            dk_ref[...] = jnp.zeros_like(dk_ref)
            dv_ref[...] = jnp.zeros_like(dv_ref)
            dqg_ref[...] = jnp.zeros_like(dqg_ref)

        qgv = qg_ref[...]
        dqg_acc = jnp.zeros((tm, LANES), F32)
        for h in range(MEM_HEADS):
            sl = slice(h * LANES, (h + 1) * LANES)
            qh = q_ref[:, sl]
            r = _rstd(qh, LANES)
            qn = (qh * r * qgv).astype(BF16)
            kh = k_ref[:, sl]
            p = _mem_softmax(qn, kh)
            dov = dy_ref[:, sl]
            dv_ref[:, sl] += _dot_tn(p.astype(BF16), dov)
            dp = _dot_nt(dov, v_ref[:, sl])
            ds = (p * (dp - jnp.sum(dp * p, axis=1, keepdims=True)) * (LANES ** -0.5)).astype(BF16)
            dk_ref[:, sl] += _dot_tn(ds, qn)
            dqh, dgr = _rms_vjp(qh, r, qgv, _dot(ds, kh), LANES)
            dz_ref[:, sl] = dqh.astype(BF16)
            dqg_acc += dgr
        dqg_ref[...] += jnp.sum(dqg_acc, axis=0, keepdims=True)

    row = pl.BlockSpec((tm, MEM_W), lambda i: (i, 0))
    full = lambda a: pl.BlockSpec(a.shape, lambda i: (0, 0))
    acc = pl.BlockSpec((m, MEM_W), lambda i: (0, 0))
    return pl.pallas_call(
        body, name=name, grid=(t // tm,), in_specs=[row, row, full(qg), full(km), full(vm)],
        out_specs=[row, acc, acc, pl.BlockSpec((1, LANES), lambda i: (0, 0))],
        out_shape=[jax.ShapeDtypeStruct((t, MEM_W), BF16), jax.ShapeDtypeStruct((m, MEM_W), F32),
                   jax.ShapeDtypeStruct((m, MEM_W), F32), jax.ShapeDtypeStruct((1, LANES), F32)],
        compiler_params=_cparams())(zqm, dyc, qg, km, vm)


def _mem_kv_bwd(mem, gmem, wkv, kg, dkn, dvm, name):
    m, d = mem.shape

    def body(mem_ref, g_ref, w_ref, kg_ref, dk_ref, dv_ref, dw_ref, dkg_ref, dg_ref, dkv_scr):
        xv = mem_ref[...]
        r = _rstd(xv, d)
        mn = (xv * r * g_ref[...]).astype(BF16)
        kvm = _dot(mn, w_ref[...])
        dkv_scr[:, MEM_W:] = dv_ref[...].astype(BF16)
        dkg_acc = jnp.zeros((m, LANES), F32)
        for h in range(MEM_HEADS):
            sl = slice(h * LANES, (h + 1) * LANES)
            kh = kvm[:, sl]
            dkh, dgr = _rms_vjp(kh, _rstd(kh, LANES), kg_ref[...], dk_ref[:, sl], LANES)
            dkv_scr[:, sl] = dkh.astype(BF16)
            dkg_acc += dgr
        dkg_ref[...] = jnp.sum(dkg_acc, axis=0, keepdims=True)
        dkv = dkv_scr[...]
        dw_ref[...] = _dot_tn(mn, dkv)
        dmn = _dot_nt(dkv, w_ref[...])
        dg_ref[...] = jnp.sum(dmn * xv * r, axis=0, keepdims=True)

    full = lambda a: pl.BlockSpec(a.shape, lambda i: (0, 0))
    return pl.pallas_call(
        body, name=name, grid=(1,),
        in_specs=[full(mem), full(gmem), full(wkv), full(kg), full(dkn), full(dvm)],
        out_specs=[pl.BlockSpec((d, 2 * MEM_W), lambda i: (0, 0)), pl.BlockSpec((1, LANES), lambda i: (0, 0)),
                   pl.BlockSpec((1, d), lambda i: (0, 0))],
        out_shape=[jax.ShapeDtypeStruct((d, 2 * MEM_W), F32), jax.ShapeDtypeStruct((1, LANES), F32),
                   jax.ShapeDtypeStruct((1, d), F32)],
        scratch_shapes=[pltpu.VMEM((m, 2 * MEM_W), BF16)],
        compiler_params=_cparams())(mem, gmem, wkv, kg, dkn, dvm)


def _merge_fwd(x1, ya, yb, yc, zg, bg, wa, wb, wc, wo, name):
    t, d = x1.shape
    tm = _row_tile(t, 256)

    def body(x_ref, ya_ref, yb_ref, yc_ref, zg_ref, bg_ref, wa_ref, wb_ref, wc_ref, wo_ref,
             x2_ref, mg_ref, pa_ref, pb_ref, pc_ref):
        merged = None
        for k, (y_ref, w_ref, p_ref) in enumerate(
                ((ya_ref, wa_ref, pa_ref), (yb_ref, wb_ref, pb_ref), (yc_ref, wc_ref, pc_ref))):
            sl = slice(k * d, (k + 1) * d)
            pr = _dot(y_ref[...], w_ref[...])
            p_ref[...] = pr.astype(BF16)
            term = jax.nn.sigmoid(zg_ref[:, sl] + bg_ref[:, sl]) * pr
            merged = term if merged is None else merged + term
        mb = merged.astype(BF16)
        mg_ref[...] = mb
        x2_ref[...] = x_ref[...] + _dot(mb, wo_ref[...])

    row = lambda n: pl.BlockSpec((tm, n), lambda i: (i, 0))
    full = lambda a: pl.BlockSpec(a.shape, lambda i: (0, 0))
    return pl.pallas_call(
        body, name=name, grid=(t // tm,),
        in_specs=[row(d), row(ya.shape[1]), row(yb.shape[1]), row(yc.shape[1]), row(3 * d), full(bg),
                  full(wa), full(wb), full(wc), full(wo)],
        out_specs=[row(d)] * 5,
        out_shape=[jax.ShapeDtypeStruct((t, d), F32)] + [jax.ShapeDtypeStruct((t, d), BF16)] * 4,
        compiler_params=_cparams())(x1, ya, yb, yc, zg, bg, wa, wb, wc, wo)


def _merge_bwd(dx2, pa, pb, pc, zg, bg, wa, wb, wc, wo, name, ex=None):
    t, d = dx2.shape
    tm = _row_tile(t, 256)

    def body(dx_ref, pa_ref, pb_ref, pc_ref, zg_ref, bg_ref, wa_ref, wb_ref, wc_ref, wo_ref,
             dpa_ref, dpb_ref, dpc_ref, dzg_ref, dbg_ref, dya_ref, dyb_ref, dyc_ref):
        @pl.when(pl.program_id(0) == 0)
        def _():
            dbg_ref[...] = jnp.zeros_like(dbg_ref)

        dm = _dot_nt(dx_ref[...].astype(BF16), wo_ref[...])
        for k, (p_ref, w_ref, dp_ref, dy_ref) in enumerate(
                ((pa_ref, wa_ref, dpa_ref, dya_ref), (pb_ref, wb_ref, dpb_ref, dyb_ref),
                 (pc_ref, wc_ref, dpc_ref, dyc_ref))):
            sl = slice(k * d, (k + 1) * d)
            gate = jax.nn.sigmoid(zg_ref[:, sl] + bg_ref[:, sl])
            dpr = (dm * gate).astype(BF16)
            dp_ref[...] = dpr
            dzg = dm * p_ref[...].astype(F32) * gate * (1.0 - gate)
            dzg_ref[:, sl] = dzg.astype(BF16)
            dbg_ref[:, sl] += jnp.sum(dzg, axis=0, keepdims=True)
            dy_ref[...] = _dot_nt(dpr, w_ref[...]).astype(dy_ref.dtype)

    row = lambda n: pl.BlockSpec((tm, n), lambda i: (i, 0))
    full = lambda a: pl.BlockSpec(a.shape, lambda i: (0, 0))
    na, nb, nc = wa.shape[0], wb.shape[0], wc.shape[0]
    return _call_with_exchange(
        ex, body, name, (t // tm,),
        [row(d), row(d), row(d), row(d), row(3 * d), full(bg), full(wa), full(wb), full(wc), full(wo)],
        [row(d), row(d), row(d), row(3 * d), pl.BlockSpec((1, 3 * d), lambda i: (0, 0)), row(na), row(nb), row(nc)],
        [jax.ShapeDtypeStruct((t, d), BF16)] * 3
        + [jax.ShapeDtypeStruct((t, 3 * d), BF16), jax.ShapeDtypeStruct((1, 3 * d), F32),
           jax.ShapeDtypeStruct((t, na), F32), jax.ShapeDtypeStruct((t, nb), BF16),
           jax.ShapeDtypeStruct((t, nc), BF16)],
        [], (dx2, pa, pb, pc, zg, bg, wa, wb, wc, wo))


def _adamw_math(w, g, m, v):
    bc1 = 1.0 - ADAM_B1 ** ADAM_STEP
    bc2 = 1.0 - ADAM_B2 ** ADAM_STEP
    nm = ADAM_B1 * m + (1.0 - ADAM_B1) * g
    nv = ADAM_B2 * v + (1.0 - ADAM_B2) * (g * g)
    delta = -ADAM_LR * ((nm / bc1) / (jnp.sqrt(nv / bc2) + ADAM_EPS) + ADAM_WD * w)
    return delta, nm, nv


def _div_tile(n, cap, mult):
    best = None
    for cand in range(mult, min(n, cap) + 1, mult):
        if n % cand == 0:
            best = cand
    assert best is not None, (n, cap, mult)
    return best


def _adamw(w, g, m, v, name):
    rows, cols = w.shape
    tr = rows if rows * cols <= 256 * 1024 else _div_tile(rows, 256, 8)

    def body(w_ref, g_ref, m_ref, v_ref, d_ref, nm_ref, nv_ref):
        d_ref[...], nm_ref[...], nv_ref[...] = _adamw_math(w_ref[...], g_ref[...], m_ref[...], v_ref[...])

    blk = pl.BlockSpec((tr, cols), lambda i: (i, 0))
    return pl.pallas_call(
        body, name=name, grid=(rows // tr,), in_specs=[blk] * 4, out_specs=[blk] * 3,
        out_shape=[jax.ShapeDtypeStruct((rows, cols), F32)] * 3, compiler_params=_cparams())(w, g, m, v)


def _adamw_slots(w, slots, m, v, name):
    _, hr, cols = w.shape
    tr = _div_tile(hr, 128, 16)

    def body(w_ref, s_ref, m_ref, v_ref, g_ref, d_ref, nm_ref, nv_ref):
        g = s_ref[0, 0].astype(F32)
        for k in range(1, N_CHIPS):
            g = g + s_ref[0, k].astype(F32)
        g_ref[0] = g
        d_ref[0], nm_ref[0], nv_ref[0] = _adamw_math(w_ref[0], g, m_ref[0], v_ref[0])

    blk = pl.BlockSpec((1, tr, cols), lambda h, i: (h, i, 0))
    return pl.pallas_call(
        body, name=name, grid=(2, hr // tr),
        in_specs=[blk, pl.BlockSpec((1, N_CHIPS, tr, cols), lambda h, i: (h, 0, i, 0)), blk, blk],
        out_specs=[blk] * 4, out_shape=[jax.ShapeDtypeStruct((2, hr, cols), F32)] * 4,
        compiler_params=_cparams())(w, slots, m, v)


ANY = pl.BlockSpec(memory_space=pl.ANY)


def _place():
    x, y, c = lax.axis_index("x"), lax.axis_index("y"), lax.axis_index("c")
    other_chips = [(1 - x, y), (x, 1 - y), (1 - x, 1 - y)]
    return x, y, c, other_chips


def _remote(src, dst, send_sem, recv_sem, to):
    return pltpu.make_async_remote_copy(src_ref=src, dst_ref=dst, send_sem=send_sem, recv_sem=recv_sem,
                                        device_id=to, device_id_type=MESH)


PIECE_BYTES = 384 * 1024


def _row_pieces(half_rows, cols):
    for n in (4, 2):
        if half_rows % (16 * n) == 0 and half_rows * cols * 2 // n >= PIECE_BYTES:
            return [pl.ds(k * (half_rows // n), half_rows // n) for k in range(n)]
    return [pl.ds(0, half_rows)]


def _pieces(arrays, rows_axis):
    return [(w, rows) for w, a in enumerate(arrays) for rows in _row_pieces(a.shape[rows_axis], a.shape[-1])]


def _gather_exchange(shards):
    nw = len(shards)
    pieces = _pieces(shards, 1)
    npc = len(pieces)

    def build(s_refs, g_refs, sems):
        send_sems, recv_sems, local_sems = sems
        x, y, c, chips = _place()
        me = 2 * x + y
        sibling = (x, y, 1 - c)
        mine = [pltpu.make_async_copy(s_refs[w], g_refs[w].at[me], local_sems.at[w]) for w in range(nw)]
        first = [_remote(s_refs[w].at[c, rows], g_refs[w].at[me, c, rows], send_sems.at[k, p], recv_sems.at[k, p],
                         (cx, cy, c)) for k, (cx, cy) in enumerate(chips) for p, (w, rows) in enumerate(pieces)]

        def start():
            for cp in mine + first:
                cp.start()

        def finish():
            passed = []
            for k, (cx, cy) in enumerate(chips):
                for p, (w, rows) in enumerate(pieces):
                    slab = g_refs[w].at[2 * cx + cy, c, rows]
                    _remote(slab, slab, send_sems.at[k, p], recv_sems.at[k, p], (cx, cy, c)).wait_recv()
                    fwd = _remote(slab, slab, send_sems.at[3 + k, p], recv_sems.at[3 + k, p], sibling)
                    fwd.start()
                    passed.append(fwd)
            for k, (cx, cy) in enumerate(chips):
                for p, (w, rows) in enumerate(pieces):
                    slab = g_refs[w].at[2 * cx + cy, 1 - c, rows]
                    _remote(slab, slab, send_sems.at[3 + k, p], recv_sems.at[3 + k, p], sibling).wait_recv()
            for cp in first + passed:
                cp.wait_send()
            for cp in mine:
                cp.wait()

        return start, finish

    return _Exchange(list(shards), [jax.ShapeDtypeStruct((N_CHIPS,) + s.shape, BF16) for s in shards],
                     [pltpu.SemaphoreType.DMA((6, npc)), pltpu.SemaphoreType.DMA((6, npc)),
                      pltpu.SemaphoreType.DMA((nw,))], build)


def _swap_halves(grads, name):
    nw = len(grads)

    def body(*refs):
        g_refs, sib_refs = refs[:nw], refs[nw:2 * nw]
        send_sems, recv_sems = refs[2 * nw:]
        x, y, c, _ = _place()
        copies = [_remote(g_refs[w].at[s, 1 - c], sib_refs[w].at[s], send_sems.at[s, w], recv_sems.at[s, w],
                          (x, y, 1 - c)) for w in range(nw) for s in range(N_CHIPS)]
        for cp in copies:
            cp.start()
        for cp in copies:
            cp.wait_recv()
        for cp in copies:
            cp.wait_send()

    return pl.pallas_call(
        body, name=name, in_specs=[ANY] * nw, out_specs=[ANY] * nw,
        out_shape=[jax.ShapeDtypeStruct((N_CHIPS,) + g.shape[2:], BF16) for g in grads],
        scratch_shapes=[pltpu.SemaphoreType.DMA((N_CHIPS, nw)), pltpu.SemaphoreType.DMA((N_CHIPS, nw))],
    )(*grads)


def _pair_sum(grad, sib, core, name):
    nchip, _, hr, cols = grad.shape
    tr = _div_tile(hr, 256, 16)

    def body(core_ref, a_ref, b_ref, o_ref):
        o_ref[...] = (a_ref[0].astype(F32) + b_ref[...].astype(F32)).astype(BF16)

    return pl.pallas_call(
        body, name=name,
        grid_spec=pltpu.PrefetchScalarGridSpec(
            num_scalar_prefetch=1, grid=(nchip, hr // tr),
            in_specs=[pl.BlockSpec((1, 1, tr, cols), lambda s, i, core_r: (s, core_r[0], i, 0)),
                      pl.BlockSpec((1, tr, cols), lambda s, i, core_r: (s, i, 0))],
            out_specs=pl.BlockSpec((1, tr, cols), lambda s, i, core_r: (s, i, 0))),
        out_shape=jax.ShapeDtypeStruct((nchip, hr, cols), BF16), compiler_params=_cparams())(core, grad, sib)


def _pair_sum_exchange(sums):
    nw = len(sums)
    pieces = _pieces(sums, 1)
    npc = len(pieces)

    def build(p_refs, o_refs, sems):
        send_sems, recv_sems, local_sems = sems
        x, y, c, chips = _place()
        me = 2 * x + y
        sibling = (x, y, 1 - c)
        mine = [pltpu.make_async_copy(p_refs[w].at[me], o_refs[w].at[c, 3], local_sems.at[w]) for w in range(nw)]
        first = [_remote(p_refs[w].at[2 * cx + cy, rows], o_refs[w].at[c, k, rows], send_sems.at[k, p],
                         recv_sems.at[k, p], (cx, cy, c))
                 for k, (cx, cy) in enumerate(chips) for p, (w, rows) in enumerate(pieces)]

        def start():
            for cp in mine + first:
                cp.start()

        def finish():
            passed = []
            for k in range(N_CHIPS):
                own_waited = set()
                for p, (w, rows) in enumerate(pieces):
                    slab = o_refs[w].at[c, k, rows]
                    if k < 3:
                        first[k * npc + p].wait_recv()
                    elif w not in own_waited:
                        mine[w].wait()
                        own_waited.add(w)
                    fwd = _remote(slab, slab, send_sems.at[3 + k, p], recv_sems.at[3 + k, p], sibling)
                    fwd.start()
                    passed.append(fwd)
            for k in range(N_CHIPS):
                for p, (w, rows) in enumerate(pieces):
                    slab = o_refs[w].at[1 - c, k, rows]
                    _remote(slab, slab, send_sems.at[3 + k, p], recv_sems.at[3 + k, p], sibling).wait_recv()
            for cp in first + passed:
                cp.wait_send()

        return start, finish

    return _Exchange(list(sums), [jax.ShapeDtypeStruct((2,) + p.shape, BF16) for p in sums],
                     [pltpu.SemaphoreType.DMA((7, npc)), pltpu.SemaphoreType.DMA((7, npc)),
                      pltpu.SemaphoreType.DMA((nw,))], build)


def _small_sum_exchange(vec):
    m_per, n = vec.shape

    def build(ins, outs, scr):
        (x_ref,), (out_ref,) = ins, outs
        gath_ref, sum_ref, send_sems, recv_sems, local_sem, out_sem = scr
        x, y, c, chips = _place()
        me, sibling = (x, y, c), (x, y, 1 - c)

        def rows(px, py, pc):
            return gath_ref.at[pl.ds((4 * px + 2 * py + pc) * m_per, m_per), :]

        def copy(k, block, to, src=None):
            return pltpu.make_async_remote_copy(
                src_ref=rows(*block) if src is None else src, dst_ref=rows(*block),
                send_sem=send_sems.at[k], recv_sem=recv_sems.at[k], device_id=to, device_id_type=MESH)

        mine = pltpu.make_async_copy(x_ref, rows(*me), local_sem)
        first = [copy(0, me, sibling, src=x_ref)] + [copy(1 + j, me, (*chip, c), src=x_ref)
                                                     for j, chip in enumerate(chips)]

        def start():
            for cp in [mine] + first:
                cp.start()

        def finish():
            passed = [copy(4 + j, (*chip, c), sibling) for j, chip in enumerate(chips)]
            for j, chip in enumerate(chips):
                copy(1 + j, (*chip, c), me).wait_recv()
                passed[j].start()
            copy(0, sibling, me).wait_recv()
            for j, chip in enumerate(chips):
                copy(4 + j, (*chip, 1 - c), me).wait_recv()
            for cp in first + passed:
                cp.wait_send()
            mine.wait()
            acc = gath_ref[pl.ds(0, m_per), :]
            for k in range(1, N_DEV):
                acc = acc + gath_ref[pl.ds(k * m_per, m_per), :]
            sum_ref[...] = acc
            done = pltpu.make_async_copy(sum_ref, out_ref, out_sem)
            done.start()
            done.wait()

        return start, finish

    return _Exchange([vec], [jax.ShapeDtypeStruct((m_per, n), F32)],
                     [pltpu.VMEM((N_DEV * m_per, n), F32), pltpu.VMEM((m_per, n), F32), pltpu.SemaphoreType.DMA((7,)),
                      pltpu.SemaphoreType.DMA((7,)), pltpu.SemaphoreType.DMA, pltpu.SemaphoreType.DMA], build)


def _pack_small(vals, tail=()):
    flat = jnp.concatenate([vals[name].reshape(-1).astype(F32) for name, _ in SMALL] + [v.reshape(1) for v in tail])
    flat = jnp.pad(flat, (0, SMALL_ROWS * LANES - flat.shape[0]))
    return flat.reshape(SMALL_ROWS, LANES)


def _unpack_small(packed):
    flat = packed.reshape(-1)
    out, off = {}, 0
    for name, shape in SMALL:
        n = int(np.prod(shape))
        out[name] = flat[off:off + n].reshape(shape)
        off += n
    return out


def _head_pad_cols(w, heads, real):
    k = w.shape[0]
    return jnp.pad(w.reshape(k, heads, real), ((0, 0), (0, 0), (0, LANES - real))).reshape(k, heads * LANES)


def _rope_tables(positions):
    half = MLA_ROPE // 2
    inv = ROPE_BASE ** (-jnp.arange(half, dtype=F32) / half)
    ang = positions.astype(F32)[:, None] * inv
    cos, sin = jnp.cos(ang), jnp.sin(ang)
    t = positions.shape[0]
    z = lambda n: jnp.zeros((t, n), F32)
    rc = jnp.concatenate([jnp.ones((t, MLA_NOPE), F32), cos, cos, z(LANES - MLA_QK)], axis=1)
    rs1 = jnp.concatenate([z(MLA_NOPE), -sin, z(LANES - MLA_NOPE - half)], axis=1)
    rs2 = jnp.concatenate([z(MLA_NOPE + half), sin, z(LANES - MLA_QK)], axis=1)
    return rc, rs1, rs2


FFN1_WEIGHTS = ("ffn1_w_gu", "ffn1_w_down")
FFN2_WEIGHTS = ("ffn2_w_gu", "ffn2_w_down")
MIXER_WEIGHTS = tuple(n for n, *_ in SHARDED if n not in FFN1_WEIGHTS + FFN2_WEIGHTS)
SHARD_SHAPE = {n: (r, c, kind) for n, r, c, kind in SHARDED}


def _from_blocks(name, gathered):
    r, c, kind = SHARD_SHAPE[name]
    blk = gathered.reshape(N_CHIPS, r, c)
    return blk, (blk.transpose(1, 0, 2).reshape(r, N_CHIPS * c) if kind == "col" else blk.reshape(N_CHIPS * r, c))


def _grad_pair_sums(names, gw, core, tag):
    by_owner = []
    for name in names:
        r, c, kind = SHARD_SHAPE[name]
        if gw[name].dtype == BF16:
            blk = gw[name]
        elif kind == "col":
            blk = gw[name].reshape(r, N_CHIPS, c).transpose(1, 0, 2)
        else:
            blk = gw[name].reshape(N_CHIPS, r, c)
        by_owner.append(blk.astype(BF16).reshape(N_CHIPS, 2, r // 2, c))
    received = _swap_halves(by_owner, "grad_swap_" + tag)
    return [_pair_sum(g, s, core, "pair_sum_" + n) for g, s, n in zip(by_owner, received, names)]


def _device_step(x, mem, positions, tgt, small, shards, core):
    d = D_MODEL
    g_ffn1, g_mix, g_ffn2 = small["ffn1_norm"], small["mix_norm"], small["ffn2_norm"]
    big = {}
    for name, g in zip(FFN1_WEIGHTS, _run_exchange(_gather_exchange([shards[n] for n in FFN1_WEIGHTS]), "gather_ffn1")):
        big[name + "#blocks"], big[name] = _from_blocks(name, g)
    wgu1, wd1 = big["ffn1_w_gu#blocks"], big["ffn1_w_down"].reshape(2, FF_TILE, d)
    x1, gpre1, upre1, h, *rest = _ffn_fwd(x, g_ffn1, wgu1, wd1, "ffn1_fwd", next_gain=g_mix,
                                          ex=_gather_exchange([shards[n] for n in MIXER_WEIGHTS]))
    for name, g in zip(MIXER_WEIGHTS, rest):
        big[name + "#blocks"], big[name] = _from_blocks(name, g)
    w_in = big["w_in"]
    w_uv_, w_cq, w_ckv = w_in[:, :COL_CQ], w_in[:, COL_CQ:COL_CKV], w_in[:, COL_CKV:COL_KR]
    w_kr = jnp.pad(w_in[:, COL_KR:COL_QM], ((0, 0), (MLA_NOPE, LANES - MLA_QK)))
    w_qm, w_g = w_in[:, COL_QM:COL_GATE], w_in[:, COL_GATE:]
    segs = (w_uv_, w_cq, w_ckv, w_kr, w_qm, w_g)
    wuq = _head_pad_cols(big["mla_w_uq"], MLA_HEADS, MLA_QK)
    ukv = big["mla_w_ukv"].reshape(MLA_KV_RANK, MLA_HEADS, 2, MLA_NOPE)
    wuk = _head_pad_cols(ukv[:, :, 0].reshape(MLA_KV_RANK, -1), MLA_HEADS, MLA_NOPE)
    wuv = _head_pad_cols(ukv[:, :, 1].reshape(MLA_KV_RANK, -1), MLA_HEADS, MLA_NOPE)
    wkv = big["mem_w_kv"]
    wa, wc, wo = big["w_branch_a"], big["w_branch_c"], big["w_out"]
    wb = jnp.pad(big["w_branch_b"].reshape(MLA_HEADS, MLA_NOPE, d),
                 ((0, 0), (0, LANES - MLA_NOPE), (0, 0))).reshape(MLA_HEADS * LANES, d)
    qg = jnp.pad(small["mla_q_norm"], ((0, 0), (0, LANES - MLA_QK)))
    kg = jnp.pad(small["mla_k_norm"], ((0, 0), (0, LANES - MLA_QK)))
    causal = jnp.tril(jnp.ones((CHUNK, CHUNK), bool))
    wt_f = jnp.where(causal[None], small["sg_w"][0], 0.0)
    wt, wt_t = wt_f.astype(BF16), wt_f.transpose(0, 2, 1).astype(BF16)
    bias_l = jnp.repeat(small["sg_b"][0].T, 64, axis=1)
    rc, rs1, rs2 = _rope_tables(positions)

    zuv, zcq, zckv, zkr, zqm, zg = _mm_cols(h, segs, [F32] * 5 + [BF16], "in_proj")
    ya = _sgu_fwd(zuv, small["sg_ln_g"], small["sg_ln_b"], wt, bias_l, "sgu_fwd")
    q, k, v, cqn, ckvn = _mla_prep_fwd(zcq, zckv, zkr, small["mla_cq_norm"], small["mla_ckv_norm"], qg, kg,
                                       wuq, wuk, wuv, rc, rs1, rs2, "mla_prep_fwd")
    yb, lse, *rest = _attn_fwd(q, k, v, "mla_attn_fwd", ex=_gather_exchange([shards[n] for n in FFN2_WEIGHTS]))
    for name, g in zip(FFN2_WEIGHTS, rest):
        big[name + "#blocks"], big[name] = _from_blocks(name, g)
    wgu2, wd2 = big["ffn2_w_gu#blocks"], big["ffn2_w_down"].reshape(2, FF_TILE, d)
    km, vm, memn = _mem_kv_fwd(mem, small["mem_norm"], wkv, small["mem_k_norm"], "mem_kv_fwd")
    yc = _mem_attn_fwd(zqm, small["mem_q_norm"], km, vm, "mem_attn_fwd")
    x2, merged, pa, pb, pc = _merge_fwd(x1, ya, yb, yc, zg, small["b_gate"], wa, wb, wc, wo, "merge_fwd")
    dy, loss_row, gpre2, upre2 = _ffn_fwd(x2, g_ffn2, wgu2, wd2, "ffn2_fwd", target=tgt)

    gw, gs, slots = {}, {}, {}

    def ffn_grads(prefix, xin, gain, dyin, gpre, upre, wgu, wd, ex=None, ex_names=(), last=False):
        dx, dgain, xn, dgt, dup, act, *got = _ffn_bwd(xin, gain, dyin, gpre, upre, wgu, wd, prefix + "_bwd", ex=ex)
        slots.update(zip(ex_names, got))
        gs[prefix + "_norm"] = dgain
        gw[prefix + "_w_gu"] = jnp.concatenate(
            [_mm_tn(xn, dgt, prefix + "_dwg", col_blocks=True, out_dtype=BF16),
             _mm_tn(xn, dup, prefix + "_dwu", col_blocks=True, out_dtype=BF16)], axis=0)
        rows_down = SHARD_SHAPE[prefix + "_w_down"][0]
        if last:
            small_sum = _small_sum_exchange(_pack_small(gs, tail=[loss_row[0, 0]]))
            dwd, summed = _mm_tn(act, dyin, prefix + "_dwd", scale=0.5, ex=small_sum, out_dtype=BF16)
            gw[prefix + "_w_down"] = dwd.reshape(N_CHIPS, rows_down, d)
            return dx, summed
        gw[prefix + "_w_down"] = _mm_tn(act, dyin, prefix + "_dwd", scale=0.5, out_dtype=BF16).reshape(
            N_CHIPS, rows_down, d)
        return dx

    dx2 = ffn_grads("ffn2", x2, g_ffn2, dy, gpre2, upre2, wgu2, wd2)
    ffn2_sums = _pair_sum_exchange(_grad_pair_sums(FFN2_WEIGHTS, gw, core, "ffn2"))
    dpa, dpb, dpc, dzg, dbg, dya, dyb, dyc, *got = _merge_bwd(dx2, pa, pb, pc, zg, small["b_gate"], wa, wb, wc, wo,
                                                              "merge_bwd", ex=ffn2_sums)
    slots.update(zip(FFN2_WEIGHTS, got))
    gs["b_gate"] = dbg
    gw["w_out"] = _mm_tn(merged, dx2, "dw_out")
    gw["w_branch_a"] = _mm_tn(ya, dpa, "dw_branch_a")
    gw["w_branch_b"] = _mm_tn(yb, dpb, "dw_branch_b").reshape(MLA_HEADS, LANES, d)[:, :MLA_NOPE].reshape(-1, d)
    gw["w_branch_c"] = _mm_tn(yc, dpc, "dw_branch_c")

    dzuv, dwt, dbl, dlg, dlb = _sgu_bwd(zuv, dya, small["sg_ln_g"], small["sg_ln_b"], wt, wt_t, bias_l, "sgu_bwd")
    gs["sg_w"], gs["sg_b"] = dwt[None], dbl[:, :SG_GROUPS].T[None]
    gs["sg_ln_g"], gs["sg_ln_b"] = dlg, dlb

    delta_rows, lse_rows = _attn_bwd_rows(yb, lse, dyb, "mla_attn_bwd_rows")
    dq, dk, dv = _attn_bwd(q, k, v, delta_rows, lse_rows, dyb, "mla_attn_bwd")
    dzcq, dzckv, dzkr, dql, dkl, dgcq, dgckv, dqg, dkg = _mla_prep_bwd(
        zcq, zckv, zkr, small["mla_cq_norm"], small["mla_ckv_norm"], qg, kg, wuq, wuk, wuv, rc, rs1, rs2,
        dq, dk, dv, "mla_prep_bwd")
    gs["mla_cq_norm"], gs["mla_ckv_norm"] = dgcq, dgckv
    gs["mla_q_norm"], gs["mla_k_norm"] = dqg[:, :MLA_QK], dkg[:, :MLA_QK]
    gw["mla_w_uq"] = _mm_tn(cqn, dql, "dw_uq").reshape(MLA_Q_RANK, MLA_HEADS, LANES)[:, :, :MLA_QK].reshape(
        MLA_Q_RANK, -1)
    dwuk = _mm_tn(ckvn, dkl, "dw_uk").reshape(MLA_KV_RANK, MLA_HEADS, LANES)[:, :, :MLA_NOPE]
    dwuv = _mm_tn(ckvn, dv, "dw_uv").reshape(MLA_KV_RANK, MLA_HEADS, LANES)[:, :, :MLA_NOPE]
    gw["mla_w_ukv"] = jnp.concatenate([dwuk, dwuv], axis=2).reshape(MLA_KV_RANK, -1)

    dzqm, dkn, dvm, dmqg = _mem_attn_bwd(zqm, dyc, small["mem_q_norm"], km, vm, "mem_attn_bwd")
    gs["mem_q_norm"] = dmqg
    gw["mem_w_kv"], gs["mem_k_norm"], gs["mem_norm"] = _mem_kv_bwd(
        mem, small["mem_norm"], wkv, small["mem_k_norm"], dkn, dvm, "mem_kv_bwd")

    dzs = (dzuv, dzcq, dzckv, dzkr, dzqm, dzg)
    dws = list(_mm_tn_cols(h, dzs[:5], "dw_in_narrow")) + [_mm_tn(h, dzg, "dw_in_gate")]
    dws[3] = dws[3][:, MLA_NOPE:MLA_QK]
    gw["w_in"] = jnp.concatenate(dws, axis=1)
    dx1, gs["mix_norm"] = _proj_norm_bwd(dzs, [w.T for w in segs], x1, g_mix, dx2, "in_proj_bwd")
    mixer_sums = _pair_sum_exchange(_grad_pair_sums(MIXER_WEIGHTS, gw, core, "mixer"))
    dx, summed = ffn_grads("ffn1", x, g_ffn1, dx1, gpre1, upre1, wgu1, wd1, ex=mixer_sums, ex_names=MIXER_WEIGHTS,
                           last=True)
    ffn1_sums = _pair_sum_exchange(_grad_pair_sums(FFN1_WEIGHTS, gw, core, "ffn1"))
    slots.update(zip(FFN1_WEIGHTS, _run_exchange(ffn1_sums, "grad_exchange_ffn1")))
    return dx, slots, summed


def kernel(x, mem, positions, ffn1_norm, ffn1_w_gu, ffn1_w_down, mix_norm, w_in, b_gate, sg_ln_g, sg_ln_b, sg_w, sg_b, mla_cq_norm, mla_w_uq, mla_ckv_norm, mla_w_ukv, mla_q_norm, mla_k_norm, mem_norm, mem_w_kv, mem_q_norm, mem_k_norm, w_branch_a, w_branch_b, w_branch_c, w_out, ffn2_norm, ffn2_w_gu, ffn2_w_down, loss_target, m_ffn1_norm, m_ffn1_w_gu, m_ffn1_w_down, m_mix_norm, m_w_in, m_b_gate, m_sg_ln_g, m_sg_ln_b, m_sg_w, m_sg_b, m_mla_cq_norm, m_mla_w_uq, m_mla_ckv_norm, m_mla_w_ukv, m_mla_q_norm, m_mla_k_norm, m_mem_norm, m_mem_w_kv, m_mem_q_norm, m_mem_k_norm, m_w_branch_a, m_w_branch_b, m_w_branch_c, m_w_out, m_ffn2_norm, m_ffn2_w_gu, m_ffn2_w_down, v_ffn1_norm, v_ffn1_w_gu, v_ffn1_w_down, v_mix_norm, v_w_in, v_b_gate, v_sg_ln_g, v_sg_ln_b, v_sg_w, v_sg_b, v_mla_cq_norm, v_mla_w_uq, v_mla_ckv_norm, v_mla_w_ukv, v_mla_q_norm, v_mla_k_norm, v_mem_norm, v_mem_w_kv, v_mem_q_norm, v_mem_k_norm, v_w_branch_a, v_w_branch_b, v_w_branch_c, v_w_out, v_ffn2_norm, v_ffn2_w_gu, v_ffn2_w_down):
    args = dict(locals())
    weights = {n: args[n] for n in WEIGHT_ORDER}
    mom_m = {n: args["m_" + n] for n in WEIGHT_ORDER}
    mom_v = {n: args["v_" + n] for n in WEIGHT_ORDER}
    small = {n: weights[n] for n, _ in SMALL}
    halves = lambda a, r, c: a.reshape(2, r // 2, c)

    shards = {n: halves(weights[n][0].astype(BF16), r, c) for n, r, c, _ in SHARDED}
    core = lax.axis_index("c").astype(jnp.int32).reshape(1)
    dx, slots, summed = _device_step(x[0], mem[0], positions[0], loss_target[0], small, shards, core)
    loss = summed.reshape(-1)[_N_SMALL]
    small_grads = _unpack_small(summed)

    grads, deltas, new_m, new_v = {}, {}, {}, {}
    for name, r, c, _ in SHARDED:
        outs = _adamw_slots(halves(weights[name][0], r, c), slots[name], halves(mom_m[name][0], r, c),
                            halves(mom_v[name][0], r, c), "adamw_" + name)
        shape = weights[name].shape
        grads[name], deltas[name], new_m[name], new_v[name] = [o.reshape(shape) for o in outs]
    dlt, nm, nv = _adamw(_pack_small(small), _pack_small(small_grads), _pack_small({n: mom_m[n] for n, _ in SMALL}),
                         _pack_small({n: mom_v[n] for n, _ in SMALL}), "adamw_small")
    for name, _ in SMALL:
        grads[name] = small_grads[name]
    deltas.update(_unpack_small(dlt))
    new_m.update(_unpack_small(nm))
    new_v.update(_unpack_small(nv))

    return (loss, dx[None], *[grads[n] for n in WEIGHT_ORDER], *[deltas[n] for n in WEIGHT_ORDER],
            *[new_m[n] for n in WEIGHT_ORDER], *[new_v[n] for n in WEIGHT_ORDER])
```

```python
import functools
from typing import Callable, NamedTuple

import numpy as np
import jax
import jax.numpy as jnp
from jax import lax
from jax.experimental import pallas as pl
from jax.experimental.pallas import tpu as pltpu

F32 = jnp.float32
BF16 = jnp.bfloat16

D_MODEL = 1024
D_FF = 2816
FF_TILE = 1408
SG_WIDTH = 512
SG_GROUPS = 8
CHUNK = 128
MLA_HEADS = 8
MLA_QK = 96
MLA_NOPE = 64
MLA_ROPE = 32
MLA_Q_RANK = 384
MLA_KV_RANK = 256
MEM_HEADS = 4
MEM_LEN = 256
LANES = 128
EPS = 1e-6
NEG = -1e30
ROPE_BASE = 10000.0
N_CHIPS = 4
N_DEV = 8

ADAM_LR = 0.001
ADAM_B1 = 0.9
ADAM_B2 = 0.999
ADAM_EPS = 1e-08
ADAM_WD = 0.01
ADAM_STEP = 10

COL_V = 512
COL_CQ = 1024
COL_CKV = 1408
COL_KR = 1664
COL_QM = 1696
COL_GATE = 2208
IN_COLS = 5280

VMEM_LIMIT_BYTES = 56 * 1024 * 1024
INV_SQRT2 = 0.7071067811865476
INV_SQRT_2PI = 0.3989422804014327
LOG2E = 1.4426950408889634
ATTN_SCALE = MLA_QK ** -0.5
V_ONES_LANE = 64
ATTN_SCALE2 = ATTN_SCALE * LOG2E

SHARDED = (
    ("ffn1_w_gu", 1024, 1408, "col"),
    ("ffn1_w_down", 704, 1024, "row"),
    ("w_in", 1024, 1320, "col"),
    ("mla_w_uq", 384, 192, "col"),
    ("mla_w_ukv", 256, 256, "col"),
    ("mem_w_kv", 256, 1024, "row"),
    ("w_branch_a", 512, 256, "col"),
    ("w_branch_b", 512, 256, "col"),
    ("w_branch_c", 512, 256, "col"),
    ("w_out", 256, 1024, "row"),
    ("ffn2_w_gu", 1024, 1408, "col"),
    ("ffn2_w_down", 704, 1024, "row"),
)
SMALL = (
    ("ffn1_norm", (1, 1024)), ("mix_norm", (1, 1024)), ("b_gate", (1, 3072)),
    ("sg_ln_g", (1, 512)), ("sg_ln_b", (1, 512)), ("sg_w", (1, 8, 128, 128)),
    ("sg_b", (1, 8, 128)), ("mla_cq_norm", (1, 384)), ("mla_ckv_norm", (1, 256)),
    ("mla_q_norm", (1, 96)), ("mla_k_norm", (1, 96)), ("mem_norm", (1, 1024)),
    ("mem_q_norm", (1, 128)), ("mem_k_norm", (1, 128)), ("ffn2_norm", (1, 1024)),
)
WEIGHT_ORDER = (
    "ffn1_norm", "ffn1_w_gu", "ffn1_w_down", "mix_norm", "w_in", "b_gate", "sg_ln_g", "sg_ln_b",
    "sg_w", "sg_b", "mla_cq_norm", "mla_w_uq", "mla_ckv_norm", "mla_w_ukv", "mla_q_norm",
    "mla_k_norm", "mem_norm", "mem_w_kv", "mem_q_norm", "mem_k_norm", "w_branch_a", "w_branch_b",
    "w_branch_c", "w_out", "ffn2_norm", "ffn2_w_gu", "ffn2_w_down",
)

_N_SMALL = sum(int(np.prod(s)) for _, s in SMALL)
SMALL_ROWS = -(-_N_SMALL // (LANES * 8)) * 8

MESH = pl.DeviceIdType.MESH


def _cparams():
    return pltpu.CompilerParams(vmem_limit_bytes=VMEM_LIMIT_BYTES)


def _dot(a, b):
    return jnp.dot(a, b, preferred_element_type=F32)


def _dot_nt(a, b):
    return lax.dot_general(a, b, (((1,), (1,)), ((), ())), preferred_element_type=F32)


def _dot_tn(a, b):
    return lax.dot_general(a, b, (((0,), (0,)), ((), ())), preferred_element_type=F32)


def _gelu(x):
    return 0.5 * x * (1.0 + lax.erf(x * INV_SQRT2))


def _gelu_grad(x):
    return 0.5 * (1.0 + lax.erf(x * INV_SQRT2)) + x * jnp.exp(-0.5 * x * x) * INV_SQRT_2PI


def _rstd(x, n):
    return lax.rsqrt(jnp.sum(x * x, axis=-1, keepdims=True) * (1.0 / n) + EPS)


def _rms_vjp(x, r, g, dy, n):
    dxh = dy * g
    dx = r * dxh - x * (r * r * r) * (jnp.sum(dxh * x, axis=-1, keepdims=True) * (1.0 / n))
    return dx, dy * x * r


def _row_tile(t, want):
    return min(t, want)


def _wide_tile(n):
    if n <= 1024:
        return n
    if n % 1024 == 0:
        return 1024
    assert n % FF_TILE == 0, n
    return FF_TILE


def _mm_cols(a, ws, out_dtypes, name, ex=None):
    t, kdim = a.shape
    tm = _row_tile(t, 256)
    n = len(ws)

    def body(*refs):
        av = refs[0][...]
        for w_ref, o_ref in zip(refs[1:1 + n], refs[1 + n:]):
            o_ref[...] = _dot(av, w_ref[...]).astype(o_ref.dtype)

    row = lambda width: pl.BlockSpec((tm, width), lambda i: (i, 0))
    return _call_with_exchange(
        ex, body, name, (t // tm,),
        [row(kdim)] + [pl.BlockSpec(w.shape, lambda i: (0, 0)) for w in ws],
        [row(w.shape[1]) for w in ws],
        [jax.ShapeDtypeStruct((t, w.shape[1]), dt) for w, dt in zip(ws, out_dtypes)], [], (a, *ws))


def _proj_norm_bwd(dzs, wts, x, g, dres, name):
    t, d = x.shape
    tm = _row_tile(t, 256)
    n = len(dzs)

    def body(*refs):
        x_ref, g_ref, r_ref, dx_ref, dg_ref = refs[2 * n:]

        @pl.when(pl.program_id(0) == 0)
        def _():
            dg_ref[...] = jnp.zeros_like(dg_ref)

        dh = None
        for dz_ref, w_ref in zip(refs[:n], refs[n:2 * n]):
            part = _dot(dz_ref[...], w_ref[...])
            dh = part if dh is None else dh + part
        xv = x_ref[...]
        dx, dgr = _rms_vjp(xv, _rstd(xv, d), g_ref[...], dh, d)
        dx_ref[...] = r_ref[...] + dx
        dg_ref[...] += jnp.sum(dgr, axis=0, keepdims=True)

    row = lambda width: pl.BlockSpec((tm, width), lambda i: (i, 0))
    vec = pl.BlockSpec((1, d), lambda i: (0, 0))
    return pl.pallas_call(
        body, name=name, grid=(t // tm,),
        in_specs=[row(dz.shape[1]) for dz in dzs] + [pl.BlockSpec(w.shape, lambda i: (0, 0)) for w in wts]
        + [row(d), vec, row(d)],
        out_specs=[row(d), vec],
        out_shape=[jax.ShapeDtypeStruct((t, d), F32), jax.ShapeDtypeStruct((1, d), F32)],
        compiler_params=_cparams())(*dzs, *wts, x, g, dres)


def _mm_tn_cols(a, bs, name):
    t, m = a.shape
    tk = _row_tile(t, 512)
    n = len(bs)

    def body(*refs):
        @pl.when(pl.program_id(0) == 0)
        def _():
            for o_ref in refs[1 + n:]:
                o_ref[...] = jnp.zeros_like(o_ref)

        av = refs[0][...].astype(BF16)
        for b_ref, o_ref in zip(refs[1:1 + n], refs[1 + n:]):
            o_ref[...] += _dot_tn(av, b_ref[...].astype(BF16))

    row = lambda width: pl.BlockSpec((tk, width), lambda k: (k, 0))
    return pl.pallas_call(
        body, name=name, grid=(t // tk,), in_specs=[row(m)] + [row(b.shape[1]) for b in bs],
        out_specs=[pl.BlockSpec((m, b.shape[1]), lambda k: (0, 0)) for b in bs],
        out_shape=[jax.ShapeDtypeStruct((m, b.shape[1]), F32) for b in bs],
        compiler_params=_cparams())(a, *bs)


def _mm_tn(a, b, name, scale=1.0, ex=None, col_blocks=False, out_dtype=F32):
    t, m = a.shape
    n = b.shape[1]
    tm, tn = _wide_tile(m), _wide_tile(n)
    tk = _row_tile(t, 1024)
    nk = t // tk
    in_place = out_dtype == F32

    def body(a_ref, b_ref, o_ref, *scr):
        k = pl.program_id(2)
        acc_ref = o_ref if in_place else scr[0]

        @pl.when(k == 0)
        def _():
            acc_ref[...] = jnp.zeros_like(acc_ref)

        prod = _dot_tn(a_ref[...].astype(BF16), b_ref[...].astype(BF16))
        acc_ref[...] += prod.reshape(acc_ref.shape)
        if scale != 1.0 or not in_place:
            @pl.when(k == nk - 1)
            def _():
                o_ref[...] = (acc_ref[...] * scale).astype(out_dtype).reshape(o_ref.shape)

    if col_blocks:
        out_spec = pl.BlockSpec((1, tm, tn), lambda i, j, k: (j, i, 0))
        out_shape = jax.ShapeDtypeStruct((n // tn, m, tn), out_dtype)
    else:
        out_spec = pl.BlockSpec((tm, tn), lambda i, j, k: (i, j))
        out_shape = jax.ShapeDtypeStruct((m, n), out_dtype)
    outs = _call_with_exchange(
        ex, body, name, (m // tm, n // tn, nk),
        [pl.BlockSpec((tk, tm), lambda i, j, k: (k, i)), pl.BlockSpec((tk, tn), lambda i, j, k: (k, j))],
        [out_spec], [out_shape], [] if in_place else [pltpu.VMEM((tm, tn), F32)], (a, b))
    return outs[0] if ex is None else outs


PASS_ON_STEPS_BEFORE_END = 8


class _Exchange(NamedTuple):
    operands: list
    out_shapes: list
    sem_shapes: list
    build: Callable


def _call_with_exchange(ex, body, name, grid, in_specs, out_specs, out_shape, scratch_shapes, operands, prefetch=()):
    n_pre = len(prefetch)
    total = int(np.prod(grid))
    pass_step = max(total // 2, total - PASS_ON_STEPS_BEFORE_END)

    def call(kernel, ins, outs, shapes, scratch):
        if n_pre:
            spec = pltpu.PrefetchScalarGridSpec(num_scalar_prefetch=n_pre, grid=grid, in_specs=ins, out_specs=outs,
                                                scratch_shapes=scratch)
            return pl.pallas_call(kernel, name=name, grid_spec=spec, out_shape=shapes, compiler_params=_cparams())
        return pl.pallas_call(kernel, name=name, grid=grid, in_specs=ins, out_specs=outs, out_shape=shapes,
                              scratch_shapes=scratch, compiler_params=_cparams())

    if ex is None:
        return call(body, in_specs, out_specs, out_shape, scratch_shapes)(*prefetch, *operands)
    n_in, n_out, n_scr = len(in_specs), len(out_specs), len(scratch_shapes)
    k_in, k_out = len(ex.operands), len(ex.out_shapes)

    def carried(*refs):
        pre, refs = refs[:n_pre], refs[n_pre:]
        a, b = n_in, n_in + k_in
        c, e = b + n_out, b + n_out + k_out
        f = e + n_scr
        start, pass_on, finish = ex.build(refs[a:b], refs[c:e], refs[f:])
        step = functools.reduce(lambda lin, ax: lin * grid[ax] + pl.program_id(ax), range(len(grid)), 0)
        pl.when(step == 0)(start)
        body(*pre, *refs[:a], *refs[b:c], *refs[e:f])
        pl.when(step == pass_step)(pass_on)
        pl.when(step == total - 1)(finish)

    return call(carried, list(in_specs) + [ANY] * k_in, list(out_specs) + [ANY] * k_out,
                list(out_shape) + list(ex.out_shapes), list(scratch_shapes) + list(ex.sem_shapes),
                )(*prefetch, *operands, *ex.operands)


def _run_exchange(ex, name):
    k_in, k_out = len(ex.operands), len(ex.out_shapes)

    def body(*refs):
        start, pass_on, finish = ex.build(refs[:k_in], refs[k_in:k_in + k_out], refs[k_in + k_out:])
        start()
        pass_on()
        finish()

    return pl.pallas_call(body, name=name, in_specs=[ANY] * k_in, out_specs=[ANY] * k_out,
                          out_shape=list(ex.out_shapes), scratch_shapes=list(ex.sem_shapes))(*ex.operands)


def _ffn_fwd(x, g, wgu4, wd2, name, ex=None, next_gain=None, target=None):
    t, d = x.shape
    tm = _row_tile(t, 512)
    assert next_gain is None or target is None
    extra = [a for a in (next_gain, target) if a is not None]

    def body(*refs):
        x_ref, g_ref, wg_ref, wu_ref, wd_ref = refs[:5]
        e_ref = refs[5] if extra else None
        outs, (xn_scr, acc_scr) = refs[5 + len(extra):-2], refs[-2:]
        if target is not None:
            dy_ref, loss_ref, gg_ref, uu_ref = outs
        elif next_gain is not None:
            o_ref, gg_ref, uu_ref, h_ref = outs
        else:
            o_ref, gg_ref, uu_ref = outs
        i, j = pl.program_id(0), pl.program_id(1)

        @pl.when(j == 0)
        def _():
            xv = x_ref[...]
            xn_scr[...] = (xv * _rstd(xv, d) * g_ref[...]).astype(BF16)
            acc_scr[...] = jnp.zeros_like(acc_scr)

        if target is not None:
            @pl.when((i == 0) & (j == 0))
            def _():
                loss_ref[...] = jnp.zeros_like(loss_ref)

        xn = xn_scr[...]
        gg = _dot(xn, wg_ref[0])
        uu = _dot(xn, wu_ref[0])
        gg_ref[...] = gg.astype(BF16)
        uu_ref[...] = uu.astype(BF16)
        act = gg * jax.nn.sigmoid(gg) * uu
        acc_scr[...] += _dot(act.astype(BF16), wd_ref[0])

        @pl.when(j == 1)
        def _():
            y = x_ref[...] + 0.5 * acc_scr[...]
            if target is not None:
                e = y - e_ref[...]
                dy_ref[...] = e * (1.0 / d)
                part = 0.5 * jnp.sum(jnp.sum(e * e, axis=-1, keepdims=True) * (1.0 / d), axis=0, keepdims=True)
                loss_ref[...] += jnp.broadcast_to(part, loss_ref.shape)
            else:
                o_ref[...] = y
                if next_gain is not None:
                    h_ref[...] = (y * _rstd(y, d) * e_ref[...]).astype(BF16)

    row = pl.BlockSpec((tm, d), lambda i, j: (i, 0))
    vec = pl.BlockSpec((1, d), lambda i, j: (0, 0))
    ffb = pl.BlockSpec((tm, FF_TILE), lambda i, j: (i, j))
    f32_rows, bf16_ff = jax.ShapeDtypeStruct((t, d), F32), jax.ShapeDtypeStruct((t, D_FF), BF16)
    if target is not None:
        extra_spec, out_specs = [row], [row, pl.BlockSpec((1, LANES), lambda i, j: (0, 0)), ffb, ffb]
        out_shape = [f32_rows, jax.ShapeDtypeStruct((1, LANES), F32), bf16_ff, bf16_ff]
    elif next_gain is not None:
        extra_spec, out_specs = [vec], [row, ffb, ffb, row]
        out_shape = [f32_rows, bf16_ff, bf16_ff, jax.ShapeDtypeStruct((t, d), BF16)]
    else:
        extra_spec, out_specs, out_shape = [], [row, ffb, ffb], [f32_rows, bf16_ff, bf16_ff]
    return _call_with_exchange(
        ex, body, name, (t // tm, 2),
        [row, vec,
         pl.BlockSpec((1, d, FF_TILE), lambda i, j: (j, 0, 0)),
         pl.BlockSpec((1, d, FF_TILE), lambda i, j: (j + 2, 0, 0)),
         pl.BlockSpec((1, FF_TILE, d), lambda i, j: (j, 0, 0))] + extra_spec,
        out_specs, out_shape,
        [pltpu.VMEM((tm, d), BF16), pltpu.VMEM((tm, d), F32)], (x, g, wgu4, wgu4, wd2, *extra))


def _ffn_bwd(x, g, dy, gpre, upre, wgu4, wd2, name, ex=None):
    t, d = x.shape
    tm = _row_tile(t, 512)

    def body(dy_ref, gg_ref, uu_ref, wgu_hbm, wd_hbm, dg_ref, du_ref, act_ref, part_ref, wg_ref, wu_ref, wd_ref):
        j = pl.program_id(0)

        @pl.when(pl.program_id(1) == 0)
        def _():
            pltpu.sync_copy(wgu_hbm.at[j], wg_ref.at[0])
            pltpu.sync_copy(wgu_hbm.at[j + 2], wu_ref.at[0])
            pltpu.sync_copy(wd_hbm.at[j], wd_ref.at[0])

        gg = gg_ref[...].astype(F32)
        uu = uu_ref[...].astype(F32)
        sg = jax.nn.sigmoid(gg)
        silu = gg * sg
        act_ref[...] = (silu * uu).astype(BF16)
        dyh = (0.5 * dy_ref[...]).astype(BF16)
        dact = _dot_nt(dyh, wd_ref[0])
        du = (dact * silu).astype(BF16)
        dgt = (dact * uu * (sg * (1.0 + gg * (1.0 - sg)))).astype(BF16)
        du_ref[...] = du
        dg_ref[...] = dgt
        part_ref[0] = _dot_nt(dgt, wg_ref[0]) + _dot_nt(du, wu_ref[0])

    row = pl.BlockSpec((tm, d), lambda j, i: (i, 0))
    ffb = pl.BlockSpec((tm, FF_TILE), lambda j, i: (i, j))
    dgt, dup, act, parts, *got = _call_with_exchange(
        ex, body, name, (2, t // tm),
        [row, ffb, ffb, ANY, ANY],
        [ffb, ffb, ffb, pl.BlockSpec((1, tm, d), lambda j, i: (j, i, 0))],
        [jax.ShapeDtypeStruct((t, D_FF), BF16)] * 3 + [jax.ShapeDtypeStruct((2, t, d), F32)],
        [pltpu.VMEM((1, d, FF_TILE), BF16), pltpu.VMEM((1, d, FF_TILE), BF16), pltpu.VMEM((1, FF_TILE, d), BF16)],
        (dy, gpre, upre, wgu4, wd2))

    def norm_body(x_ref, g_ref, p_ref, dy_ref, dx_ref, dgain_ref, xn_ref):
        @pl.when(pl.program_id(0) == 0)
        def _():
            dgain_ref[...] = jnp.zeros_like(dgain_ref)

        xv = x_ref[...]
        r = _rstd(xv, d)
        xn_ref[...] = (xv * r * g_ref[...]).astype(BF16)
        dx, dgr = _rms_vjp(xv, r, g_ref[...], p_ref[0] + p_ref[1], d)
        dx_ref[...] = dy_ref[...] + dx
        dgain_ref[...] += jnp.sum(dgr, axis=0, keepdims=True)

    tn = _row_tile(t, 256)
    nrow = pl.BlockSpec((tn, d), lambda i: (i, 0))
    vec = pl.BlockSpec((1, d), lambda i: (0, 0))
    dx, dgain, xn = pl.pallas_call(
        norm_body, name=name + "_norm", grid=(t // tn,),
        in_specs=[nrow, vec, pl.BlockSpec((2, tn, d), lambda i: (0, i, 0)), nrow],
        out_specs=[nrow, vec, nrow],
        out_shape=[jax.ShapeDtypeStruct((t, d), F32), jax.ShapeDtypeStruct((1, d), F32),
                   jax.ShapeDtypeStruct((t, d), BF16)],
        compiler_params=_cparams())(x, g, parts, dy)
    return [dx, dgain, xn, dgt, dup, act] + got


def _sgu_layernorm(vpre, lg, lb):
    v = _gelu(vpre)
    mu = jnp.mean(v, axis=-1, keepdims=True)
    xc = v - mu
    rstd = lax.rsqrt(jnp.mean(xc * xc, axis=-1, keepdims=True) + EPS)
    xhat = xc * rstd
    return xhat, rstd, xhat * lg + lb


def _sgu_fwd(zuv, lg, lb, wt, bias_l, name):
    t = zuv.shape[0]
    tm = _row_tile(t, 512)

    def body(u_ref, v_ref, lg_ref, lb_ref, wt_ref, bl_ref, o_ref, vln_scr):
        _, _, vln = _sgu_layernorm(v_ref[...], lg_ref[...], lb_ref[...])
        vln_scr[...] = vln.astype(BF16)
        lo = lax.broadcasted_iota(jnp.int32, (CHUNK, LANES), 1) < 64
        for c in range(tm // CHUNK):
            rows = slice(c * CHUNK, (c + 1) * CHUNK)
            for p in range(SG_GROUPS // 2):
                cols = slice(p * LANES, (p + 1) * LANES)
                vp = vln_scr[rows, cols]
                mixed = jnp.where(lo, _dot(wt_ref[2 * p], vp), _dot(wt_ref[2 * p + 1], vp)) + bl_ref[:, cols]
                o_ref[rows, cols] = (_gelu(u_ref[rows, cols]) * mixed).astype(BF16)

    half = lambda k: pl.BlockSpec((tm, SG_WIDTH), lambda i: (i, k))
    vec = pl.BlockSpec((1, SG_WIDTH), lambda i: (0, 0))
    return pl.pallas_call(
        body, name=name, grid=(t // tm,),
        in_specs=[half(0), half(1), vec, vec,
                  pl.BlockSpec((SG_GROUPS, CHUNK, CHUNK), lambda i: (0, 0, 0)),
                  pl.BlockSpec((CHUNK, SG_WIDTH), lambda i: (0, 0))],
        out_specs=pl.BlockSpec((tm, SG_WIDTH), lambda i: (i, 0)),
        out_shape=jax.ShapeDtypeStruct((t, SG_WIDTH), BF16),
        scratch_shapes=[pltpu.VMEM((tm, SG_WIDTH), BF16)],
        compiler_params=_cparams())(zuv, zuv, lg, lb, wt, bias_l)


def _sgu_bwd(zuv, dya, lg, lb, wt, wt_t, bias_l, name):
    t = zuv.shape[0]
    tm = _row_tile(t, 256)
    nsteps = t // tm

    def body(u_ref, v_ref, dy_ref, lg_ref, lb_ref, wt_ref, wtt_ref, bl_ref,
             dz_ref, dwt_ref, dbl_ref, dlg_ref, dlb_ref, vln_scr, dvln_scr, dbacc_scr):
        step = pl.program_id(0)

        @pl.when(step == 0)
        def _():
            dwt_ref[...] = jnp.zeros_like(dwt_ref)
            dlg_ref[...] = jnp.zeros_like(dlg_ref)
            dlb_ref[...] = jnp.zeros_like(dlb_ref)
            dbl_ref[...] = jnp.zeros_like(dbl_ref)
            dbacc_scr[...] = jnp.zeros_like(dbacc_scr)

        vpre = v_ref[...]
        lgv = lg_ref[...]
        xhat, rstd, vln = _sgu_layernorm(vpre, lgv, lb_ref[...])
        vln_scr[...] = vln.astype(BF16)
        lo = lax.broadcasted_iota(jnp.int32, (CHUNK, LANES), 1) < 64
        for c in range(tm // CHUNK):
            rows = slice(c * CHUNK, (c + 1) * CHUNK)
            for p in range(SG_GROUPS // 2):
                cols = slice(p * LANES, (p + 1) * LANES)
                vp = vln_scr[rows, cols]
                mixed = jnp.where(lo, _dot(wt_ref[2 * p], vp), _dot(wt_ref[2 * p + 1], vp)) + bl_ref[:, cols]
                upre = u_ref[rows, cols]
                dyp = dy_ref[rows, cols]
                dz_ref[rows, cols] = (dyp * mixed * _gelu_grad(upre)).astype(BF16)
                dm = dyp * _gelu(upre)
                dbacc_scr[:, cols] += dm
                dlo = jnp.where(lo, dm, 0.0).astype(BF16)
                dhi = jnp.where(lo, 0.0, dm).astype(BF16)
                dvln_scr[rows, cols] = _dot(wtt_ref[2 * p], dlo) + _dot(wtt_ref[2 * p + 1], dhi)
                dwt_ref[2 * p] += _dot_nt(dlo, vp)
                dwt_ref[2 * p + 1] += _dot_nt(dhi, vp)
        dvln = dvln_scr[...]
        dlg_ref[...] += jnp.sum(dvln * xhat, axis=0, keepdims=True)
        dlb_ref[...] += jnp.sum(dvln, axis=0, keepdims=True)
        dxh = dvln * lgv
        dv = rstd * (dxh - jnp.mean(dxh, axis=-1, keepdims=True)
                     - xhat * jnp.mean(dxh * xhat, axis=-1, keepdims=True))
        dz_ref[:, SG_WIDTH:] = (dv * _gelu_grad(vpre)).astype(BF16)

        @pl.when(step == nsteps - 1)
        def _():
            rr = lax.broadcasted_iota(jnp.int32, (CHUNK, CHUNK), 0)
            cc = lax.broadcasted_iota(jnp.int32, (CHUNK, CHUNK), 1)
            tril = (cc <= rr).astype(F32)
            for gidx in range(SG_GROUPS):
                dwt_ref[gidx] = dwt_ref[gidx] * tril
            kk = lax.broadcasted_iota(jnp.int32, (SG_WIDTH, LANES), 0)
            gg = lax.broadcasted_iota(jnp.int32, (SG_WIDTH, LANES), 1)
            sel = ((kk // 64) == gg).astype(F32)
            dbl_ref[...] = jnp.dot(dbacc_scr[...], sel, preferred_element_type=F32,
                                   precision=lax.Precision.HIGHEST)

    half = lambda k: pl.BlockSpec((tm, SG_WIDTH), lambda i: (i, k))
    vec = pl.BlockSpec((1, SG_WIDTH), lambda i: (0, 0))
    wspec = pl.BlockSpec((SG_GROUPS, CHUNK, CHUNK), lambda i: (0, 0, 0))
    return pl.pallas_call(
        body, name=name, grid=(nsteps,),
        in_specs=[half(0), half(1), pl.BlockSpec((tm, SG_WIDTH), lambda i: (i, 0)), vec, vec,
                  wspec, wspec, pl.BlockSpec((CHUNK, SG_WIDTH), lambda i: (0, 0))],
        out_specs=[pl.BlockSpec((tm, 2 * SG_WIDTH), lambda i: (i, 0)), wspec,
                   pl.BlockSpec((CHUNK, LANES), lambda i: (0, 0)), vec, vec],
        out_shape=[jax.ShapeDtypeStruct((t, 2 * SG_WIDTH), BF16),
                   jax.ShapeDtypeStruct((SG_GROUPS, CHUNK, CHUNK), F32),
                   jax.ShapeDtypeStruct((CHUNK, LANES), F32),
                   jax.ShapeDtypeStruct((1, SG_WIDTH), F32), jax.ShapeDtypeStruct((1, SG_WIDTH), F32)],
        scratch_shapes=[pltpu.VMEM((tm, SG_WIDTH), BF16), pltpu.VMEM((tm, SG_WIDTH), F32),
                        pltpu.VMEM((CHUNK, SG_WIDTH), F32)],
        compiler_params=_cparams())(zuv, zuv, dya, lg, lb, wt, wt_t, bias_l)


def _rope(x, c, s1, s2):
    return x * c + pltpu.roll(x, LANES - 16, 1) * s1 + pltpu.roll(x, 16, 1) * s2


def _rope_t(dy, c, s1, s2):
    return dy * c + pltpu.roll(dy * s1, 16, 1) + pltpu.roll(dy * s2, LANES - 16, 1)


def _mla_prep_fwd(zcq, zckv, zkr, gcq, gckv, qg, kg, wuq, wuk, wuv, rc, rs1, rs2, name, ex=None):
    t = zcq.shape[0]
    tm = _row_tile(t, 256)
    hd = MLA_HEADS * LANES

    def body(zcq_ref, zckv_ref, zkr_ref, gcq_ref, gckv_ref, qg_ref, kg_ref, wuq_ref, wuk_ref, wuv_ref,
             c_ref, s1_ref, s2_ref, q_ref, k_ref, v_ref, cqn_ref, ckvn_ref):
        c, s1, s2 = c_ref[...], s1_ref[...], s2_ref[...]
        xq = zcq_ref[...]
        cqn = (xq * _rstd(xq, MLA_Q_RANK) * gcq_ref[...]).astype(BF16)
        cqn_ref[...] = cqn
        ql = _dot(cqn, wuq_ref[...])
        xk = zckv_ref[...]
        ckvn = (xk * _rstd(xk, MLA_KV_RANK) * gckv_ref[...]).astype(BF16)
        ckvn_ref[...] = ckvn
        kl = _dot(ckvn, wuk_ref[...])
        slot_lane = lax.broadcasted_iota(jnp.int32, (tm, hd), 1) % LANES
        v_ref[...] = jnp.where(slot_lane == V_ONES_LANE, 1.0, _dot(ckvn, wuv_ref[...])).astype(BF16)
        kr = zkr_ref[...]
        for h in range(MLA_HEADS):
            sl = slice(h * LANES, (h + 1) * LANES)
            qh = ql[:, sl]
            q_ref[:, sl] = (_rope(qh * _rstd(qh, MLA_QK) * qg_ref[...], c, s1, s2) * ATTN_SCALE2).astype(BF16)
            kh = kl[:, sl] + kr
            k_ref[:, sl] = _rope(kh * _rstd(kh, MLA_QK) * kg_ref[...], c, s1, s2).astype(BF16)

    row = lambda n: pl.BlockSpec((tm, n), lambda i: (i, 0))
    full = lambda a: pl.BlockSpec(a.shape, lambda i: (0, 0))
    return _call_with_exchange(
        ex, body, name, (t // tm,),
        [row(MLA_Q_RANK), row(MLA_KV_RANK), row(LANES), full(gcq), full(gckv), full(qg), full(kg),
         full(wuq), full(wuk), full(wuv), row(LANES), row(LANES), row(LANES)],
        [row(hd), row(hd), row(hd), row(MLA_Q_RANK), row(MLA_KV_RANK)],
        [jax.ShapeDtypeStruct((t, hd), BF16)] * 3
        + [jax.ShapeDtypeStruct((t, MLA_Q_RANK), BF16), jax.ShapeDtypeStruct((t, MLA_KV_RANK), BF16)],
        [], (zcq, zckv, zkr, gcq, gckv, qg, kg, wuq, wuk, wuv, rc, rs1, rs2))


def _mla_prep_bwd(zcq, zckv, zkr, gcq, gckv, qg, kg, wuq, wuk, wuv, rc, rs1, rs2, dq, dk, dv, name):
    t = zcq.shape[0]
    tm = _row_tile(t, 256)
    hd = MLA_HEADS * LANES

    def body(zcq_ref, zckv_ref, zkr_ref, gcq_ref, gckv_ref, qg_ref, kg_ref, wuq_ref, wuk_ref, wuv_ref,
             c_ref, s1_ref, s2_ref, dq_ref, dk_ref, dv_ref,
             dzcq_ref, dzckv_ref, dzkr_ref, dql_ref, dkl_ref, dgcq_ref, dgckv_ref, dqg_ref, dkg_ref):
        @pl.when(pl.program_id(0) == 0)
        def _():
            for ref in (dgcq_ref, dgckv_ref, dqg_ref, dkg_ref):
                ref[...] = jnp.zeros_like(ref)

        c, s1, s2 = c_ref[...], s1_ref[...], s2_ref[...]
        qgv, kgv = qg_ref[...], kg_ref[...]
        xq = zcq_ref[...]
        rq = _rstd(xq, MLA_Q_RANK)
        ql = _dot((xq * rq * gcq_ref[...]).astype(BF16), wuq_ref[...])
        xk = zckv_ref[...]
        rk = _rstd(xk, MLA_KV_RANK)
        kl = _dot((xk * rk * gckv_ref[...]).astype(BF16), wuk_ref[...])
        kr = zkr_ref[...]
        dqg_acc = jnp.zeros((tm, LANES), F32)
        dkg_acc = jnp.zeros((tm, LANES), F32)
        dkr = jnp.zeros((tm, LANES), F32)
        for h in range(MLA_HEADS):
            sl = slice(h * LANES, (h + 1) * LANES)
            qh = ql[:, sl]
            dqh, dgr = _rms_vjp(qh, _rstd(qh, MLA_QK), qgv, _rope_t(dq_ref[:, sl], c, s1, s2), MLA_QK)
            dql_ref[:, sl] = dqh.astype(BF16)
            dqg_acc += dgr
            kh = kl[:, sl] + kr
            dkh, dgr = _rms_vjp(kh, _rstd(kh, MLA_QK), kgv, _rope_t(dk_ref[:, sl], c, s1, s2), MLA_QK)
            dkl_ref[:, sl] = dkh.astype(BF16)
            dkg_acc += dgr
            dkr += dkh
        dqg_ref[...] += jnp.sum(dqg_acc, axis=0, keepdims=True)
        dkg_ref[...] += jnp.sum(dkg_acc, axis=0, keepdims=True)
        lane = lax.broadcasted_iota(jnp.int32, (tm, LANES), 1)
        dzkr_ref[...] = jnp.where((lane >= MLA_NOPE) & (lane < MLA_QK), dkr, 0.0).astype(BF16)
        dcqn = _dot_nt(dql_ref[...], wuq_ref[...])
        dx, dgr = _rms_vjp(xq, rq, gcq_ref[...], dcqn, MLA_Q_RANK)
        dzcq_ref[...] = dx.astype(BF16)
        dgcq_ref[...] += jnp.sum(dgr, axis=0, keepdims=True)
        dckvn = _dot_nt(dkl_ref[...], wuk_ref[...]) + _dot_nt(dv_ref[...].astype(BF16), wuv_ref[...])
        dx, dgr = _rms_vjp(xk, rk, gckv_ref[...], dckvn, MLA_KV_RANK)
        dzckv_ref[...] = dx.astype(BF16)
        dgckv_ref[...] += jnp.sum(dgr, axis=0, keepdims=True)

    row = lambda n: pl.BlockSpec((tm, n), lambda i: (i, 0))
    full = lambda a: pl.BlockSpec(a.shape, lambda i: (0, 0))
    vec = lambda n: pl.BlockSpec((1, n), lambda i: (0, 0))
    return pl.pallas_call(
        body, name=name, grid=(t // tm,),
        in_specs=[row(MLA_Q_RANK), row(MLA_KV_RANK), row(LANES), full(gcq), full(gckv), full(qg), full(kg),
                  full(wuq), full(wuk), full(wuv), row(LANES), row(LANES), row(LANES), row(hd), row(hd), row(hd)],
        out_specs=[row(MLA_Q_RANK), row(MLA_KV_RANK), row(LANES), row(hd), row(hd),
                   vec(MLA_Q_RANK), vec(MLA_KV_RANK), vec(LANES), vec(LANES)],
        out_shape=[jax.ShapeDtypeStruct((t, MLA_Q_RANK), BF16), jax.ShapeDtypeStruct((t, MLA_KV_RANK), BF16),
                   jax.ShapeDtypeStruct((t, LANES), BF16), jax.ShapeDtypeStruct((t, hd), BF16),
                   jax.ShapeDtypeStruct((t, hd), BF16), jax.ShapeDtypeStruct((1, MLA_Q_RANK), F32),
                   jax.ShapeDtypeStruct((1, MLA_KV_RANK), F32), jax.ShapeDtypeStruct((1, LANES), F32),
                   jax.ShapeDtypeStruct((1, LANES), F32)],
        compiler_params=_cparams(),
    )(zcq, zckv, zkr, gcq, gckv, qg, kg, wuq, wuk, wuv, rc, rs1, rs2, dq, dk, dv)


def _attn_tiles(t):
    tq = 512 if t >= 2048 else 128
    return tq, min(t, 4 * tq), min(t, 2 * tq)


def _causal_keep(tq, tk, i, j):
    row = lax.broadcasted_iota(jnp.int32, (tq, tk), 0)
    col = lax.broadcasted_iota(jnp.int32, (tq, tk), 1)
    return (col - row) <= (i * tq - j * tk)


def _causal_keep_t(tq, tk, i, j):
    key = lax.broadcasted_iota(jnp.int32, (tk, tq), 0)
    qry = lax.broadcasted_iota(jnp.int32, (tk, tq), 1)
    return (key - qry) <= (i * tq - j * tk)


ATTN_FWD_HEADS_PER_STEP = 2
ATTN_BWD_HEADS_PER_STEP = 2


def _attn_fwd(q, k, v, name, ex=None):
    t, hd = q.shape
    hp = ATTN_FWD_HEADS_PER_STEP
    tq, tk, _ = _attn_tiles(t)
    pairs = [(i, j) for i in range(t // tq) for j in range(((i + 1) * tq - 1) // tk + 1)]
    ii = np.array([p[0] for p in pairs], np.int32)
    jj = np.array([p[1] for p in pairs], np.int32)

    def body(ii_ref, jj_ref, q_ref, k_ref, v_ref, o_ref, lse_ref, m_scr, acc_scr):
        s_id = pl.program_id(1)
        i, j = ii_ref[s_id], jj_ref[s_id]
        last = j == ((i + 1) * tq - 1) // tk
        ones_lane = lax.broadcasted_iota(jnp.int32, (tq, LANES), 1) == V_ONES_LANE

        @pl.when(j == 0)
        def _():
            m_scr[...] = jnp.full_like(m_scr, NEG)
            acc_scr[...] = jnp.zeros_like(acc_scr)

        def step(masked):
            for hh in range(hp):
                sl = slice(hh * LANES, (hh + 1) * LANES)
                s = _dot_nt(q_ref[:, sl], k_ref[:, sl])
                if masked:
                    s = jnp.where(_causal_keep(tq, tk, i, j), s, NEG)
                m_prev = m_scr[hh]
                m_new = jnp.maximum(m_prev, jnp.max(s, axis=1, keepdims=True))
                p = jnp.exp2(s - m_new)
                alpha = jnp.exp2(m_prev - m_new)
                acc = alpha * acc_scr[:, sl] + _dot(p.astype(BF16), v_ref[:, sl])
                if masked:
                    l_new = jnp.sum(jnp.where(ones_lane, acc, 0.0), axis=1, keepdims=True)
                    o_ref[:, sl] = (acc / l_new).astype(BF16)
                    lse_ref[:, sl] = jnp.broadcast_to(m_new + jnp.log(l_new) * LOG2E, (tq, LANES))
                else:
                    acc_scr[:, sl] = acc
                    m_scr[hh] = m_new

        @pl.when(jnp.logical_not(last))
        def _():
            step(False)

        @pl.when(last)
        def _():
            step(True)

    w = hp * LANES
    qspec = pl.BlockSpec((tq, w), lambda h, s, ii_r, jj_r: (ii_r[s], h))
    kspec = pl.BlockSpec((tk, w), lambda h, s, ii_r, jj_r: (jj_r[s], h))
    return _call_with_exchange(
        ex, body, name, (hd // w, len(pairs)), [qspec, kspec, kspec], [qspec, qspec],
        [jax.ShapeDtypeStruct((t, hd), BF16), jax.ShapeDtypeStruct((t, hd), F32)],
        [pltpu.VMEM((hp, tq, 1), F32), pltpu.VMEM((tq, w), F32)], (q, k, v),
        prefetch=(jnp.asarray(ii), jnp.asarray(jj)))


def _attn_bwd_rows(o, lse, do, name):
    t, hd = o.shape
    heads = hd // LANES
    tm = _row_tile(t, 512)

    def body(o_ref, lse_ref, do_ref, out_ref):
        lane = lax.broadcasted_iota(jnp.int32, (tm, LANES), 1)
        acc = jnp.zeros((tm, LANES), F32)
        for h in range(heads):
            sl = slice(h * LANES, (h + 1) * LANES)
            delta = jnp.sum(do_ref[:, sl].astype(F32) * o_ref[:, sl].astype(F32), axis=1, keepdims=True)
            acc = jnp.where(lane == h, delta, acc)
            acc = jnp.where(lane == heads + h, lse_ref[:, sl], acc)
        out_ref[...] = acc

    row = pl.BlockSpec((tm, hd), lambda i: (i, 0))
    cols = pl.pallas_call(
        body, name=name, grid=(t // tm,), in_specs=[row, row, row],
        out_specs=pl.BlockSpec((tm, LANES), lambda i: (i, 0)),
        out_shape=jax.ShapeDtypeStruct((t, LANES), F32), compiler_params=_cparams())(o, lse, do)
    rows = cols.T
    return rows[:heads].reshape(heads, 1, t), rows[heads:2 * heads].reshape(heads, 1, t)


def _attn_bwd(q, k, v, delta_rows, lse_rows, do, name):
    t, hd = q.shape
    hp = ATTN_BWD_HEADS_PER_STEP
    tq, _, tk = _attn_tiles(t)
    nq = t // tq
    pairs = [(i, j) for j in range(t // tk) for i in range((j * tk) // tq, nq)]
    ii = np.array([p[0] for p in pairs], np.int32)
    jj = np.array([p[1] for p in pairs], np.int32)

    def body(jj_ref, ii_ref, q_ref, k_ref, v_ref, delta_ref, lse_ref, do_ref, dq_ref, dk_ref, dv_ref,
             dk_scr, dv_scr):
        s_id = pl.program_id(1)
        i, j = ii_ref[s_id], jj_ref[s_id]

        @pl.when(s_id == 0)
        def _():
            dq_ref[...] = jnp.zeros_like(dq_ref)

        @pl.when(i == (j * tk) // tq)
        def _():
            dk_scr[...] = jnp.zeros_like(dk_scr)
            dv_scr[...] = jnp.zeros_like(dv_scr)

        rows = pl.ds(pl.multiple_of(i * tq, tq), tq)

        def step(masked):
            for hh in range(hp):
                sl = slice(hh * LANES, (hh + 1) * LANES)
                qv, kv, dov = q_ref[:, sl], k_ref[:, sl], do_ref[:, sl]
                st = _dot_nt(kv, qv)
                if masked:
                    st = jnp.where(_causal_keep_t(tq, tk, i, j), st, NEG)
                pt = jnp.exp2(st - lse_ref[hh])
                dv_scr[:, sl] += _dot(pt.astype(BF16), dov)
                dpt = _dot_nt(v_ref[:, sl], dov)
                dst = (pt * (dpt - delta_ref[hh]) * ATTN_SCALE).astype(BF16)
                dk_scr[:, sl] += _dot(dst, qv)
                dq_ref[rows, sl] += _dot_tn(dst, kv)

        crosses = (j + 1) * tk - 1 > i * tq

        @pl.when(jnp.logical_not(crosses))
        def _():
            step(False)

        @pl.when(crosses)
        def _():
            step(True)

        @pl.when(i == nq - 1)
        def _():
            dk_ref[...] = dk_scr[...] * (1.0 / ATTN_SCALE2)
            dv_ref[...] = dv_scr[...]

    w = hp * LANES
    qspec = pl.BlockSpec((tq, w), lambda h, s, jj_r, ii_r: (ii_r[s], h))
    kspec = pl.BlockSpec((tk, w), lambda h, s, jj_r, ii_r: (jj_r[s], h))
    rspec = pl.BlockSpec((hp, 1, tq), lambda h, s, jj_r, ii_r: (h, 0, ii_r[s]))
    return pl.pallas_call(
        body, name=name,
        grid_spec=pltpu.PrefetchScalarGridSpec(
            num_scalar_prefetch=2, grid=(hd // w, len(pairs)),
            in_specs=[qspec, kspec, kspec, rspec, rspec, qspec],
            out_specs=[pl.BlockSpec((t, w), lambda h, s, jj_r, ii_r: (0, h)), kspec, kspec],
            scratch_shapes=[pltpu.VMEM((tk, w), F32), pltpu.VMEM((tk, w), F32)]),
        out_shape=[jax.ShapeDtypeStruct((t, hd), F32)] * 3,
        compiler_params=_cparams())(jnp.asarray(jj), jnp.asarray(ii), q, k, v, delta_rows, lse_rows, do)


MEM_W = MEM_HEADS * LANES


def _mem_kv_fwd(mem, gmem, wkv, kg, name):
    m, d = mem.shape

    def body(mem_ref, g_ref, w_ref, kg_ref, k_ref, v_ref, mn_ref):
        xv = mem_ref[...]
        mn = (xv * _rstd(xv, d) * g_ref[...]).astype(BF16)
        mn_ref[...] = mn
        kvm = _dot(mn, w_ref[...])
        v_ref[...] = kvm[:, MEM_W:].astype(BF16)
        for h in range(MEM_HEADS):
            sl = slice(h * LANES, (h + 1) * LANES)
            kh = kvm[:, sl]
            k_ref[:, sl] = (kh * _rstd(kh, LANES) * kg_ref[...]).astype(BF16)

    full = lambda a: pl.BlockSpec(a.shape, lambda i: (0, 0))
    return pl.pallas_call(
        body, name=name, grid=(1,), in_specs=[full(mem), full(gmem), full(wkv), full(kg)],
        out_specs=[pl.BlockSpec((m, MEM_W), lambda i: (0, 0)), pl.BlockSpec((m, MEM_W), lambda i: (0, 0)),
                   pl.BlockSpec((m, d), lambda i: (0, 0))],
        out_shape=[jax.ShapeDtypeStruct((m, MEM_W), BF16), jax.ShapeDtypeStruct((m, MEM_W), BF16),
                   jax.ShapeDtypeStruct((m, d), BF16)],
        compiler_params=_cparams())(mem, gmem, wkv, kg)


def _mem_softmax(qn, kh):
    s = _dot_nt(qn, kh) * (LANES ** -0.5)
    e = jnp.exp(s - jnp.max(s, axis=1, keepdims=True))
    return e / jnp.sum(e, axis=1, keepdims=True)


def _mem_attn_fwd(zqm, qg, km, vm, name):
    t = zqm.shape[0]
    tm = _row_tile(t, 512)

    def body(q_ref, qg_ref, k_ref, v_ref, o_ref):
        for h in range(MEM_HEADS):
            sl = slice(h * LANES, (h + 1) * LANES)
            qh = q_ref[:, sl]
            qn = (qh * _rstd(qh, LANES) * qg_ref[...]).astype(BF16)
            p = _mem_softmax(qn, k_ref[:, sl])
            o_ref[:, sl] = _dot(p.astype(BF16), v_ref[:, sl]).astype(BF16)

    row = pl.BlockSpec((tm, MEM_W), lambda i: (i, 0))
    full = lambda a: pl.BlockSpec(a.shape, lambda i: (0, 0))
    return pl.pallas_call(
        body, name=name, grid=(t // tm,), in_specs=[row, full(qg), full(km), full(vm)], out_specs=row,
        out_shape=jax.ShapeDtypeStruct((t, MEM_W), BF16), compiler_params=_cparams())(zqm, qg, km, vm)


def _mem_attn_bwd(zqm, dyc, qg, km, vm, name):
    t = zqm.shape[0]
    m = km.shape[0]
    tm = _row_tile(t, 256)

    def body(q_ref, dy_ref, qg_ref, k_ref, v_ref, dz_ref, dk_ref, dv_ref, dqg_ref):
        @pl.when(pl.program_id(0) == 0)
        def _():
            dk_ref[...] = jnp.zeros_like(dk_ref)
            dv_ref[...] = jnp.zeros_like(dv_ref)
            dqg_ref[...] = jnp.zeros_like(dqg_ref)

        qgv = qg_ref[...]
        dqg_acc = jnp.zeros((tm, LANES), F32)
        for h in range(MEM_HEADS):
            sl = slice(h * LANES, (h + 1) * LANES)
            qh = q_ref[:, sl]
            r = _rstd(qh, LANES)
            qn = (qh * r * qgv).astype(BF16)
            kh = k_ref[:, sl]
            p = _mem_softmax(qn, kh)
            dov = dy_ref[:, sl]
            dv_ref[:, sl] += _dot_tn(p.astype(BF16), dov)
            dp = _dot_nt(dov, v_ref[:, sl])
            ds = (p * (dp - jnp.sum(dp * p, axis=1, keepdims=True)) * (LANES ** -0.5)).astype(BF16)
            dk_ref[:, sl] += _dot_tn(ds, qn)
            dqh, dgr = _rms_vjp(qh, r, qgv, _dot(ds, kh), LANES)
            dz_ref[:, sl] = dqh.astype(BF16)
            dqg_acc += dgr
        dqg_ref[...] += jnp.sum(dqg_acc, axis=0, keepdims=True)

    row = pl.BlockSpec((tm, MEM_W), lambda i: (i, 0))
    full = lambda a: pl.BlockSpec(a.shape, lambda i: (0, 0))
    acc = pl.BlockSpec((m, MEM_W), lambda i: (0, 0))
    return pl.pallas_call(
        body, name=name, grid=(t // tm,), in_specs=[row, row, full(qg), full(km), full(vm)],
        out_specs=[row, acc, acc, pl.BlockSpec((1, LANES), lambda i: (0, 0))],
        out_shape=[jax.ShapeDtypeStruct((t, MEM_W), BF16), jax.ShapeDtypeStruct((m, MEM_W), F32),
                   jax.ShapeDtypeStruct((m, MEM_W), F32), jax.ShapeDtypeStruct((1, LANES), F32)],
        compiler_params=_cparams())(zqm, dyc, qg, km, vm)


def _mem_kv_bwd(mem, gmem, wkv, kg, dkn, dvm, name):
    m, d = mem.shape

    def body(mem_ref, g_ref, w_ref, kg_ref, dk_ref, dv_ref, dw_ref, dkg_ref, dg_ref, dkv_scr):
        xv = mem_ref[...]
        r = _rstd(xv, d)
        mn = (xv * r * g_ref[...]).astype(BF16)
        kvm = _dot(mn, w_ref[...])
        dkv_scr[:, MEM_W:] = dv_ref[...].astype(BF16)
        dkg_acc = jnp.zeros((m, LANES), F32)
        for h in range(MEM_HEADS):
            sl = slice(h * LANES, (h + 1) * LANES)
            kh = kvm[:, sl]
            dkh, dgr = _rms_vjp(kh, _rstd(kh, LANES), kg_ref[...], dk_ref[:, sl], LANES)
            dkv_scr[:, sl] = dkh.astype(BF16)
            dkg_acc += dgr
        dkg_ref[...] = jnp.sum(dkg_acc, axis=0, keepdims=True)
        dkv = dkv_scr[...]
        dw_ref[...] = _dot_tn(mn, dkv)
        dmn = _dot_nt(dkv, w_ref[...])
        dg_ref[...] = jnp.sum(dmn * xv * r, axis=0, keepdims=True)

    full = lambda a: pl.BlockSpec(a.shape, lambda i: (0, 0))
    return pl.pallas_call(
        body, name=name, grid=(1,),
        in_specs=[full(mem), full(gmem), full(wkv), full(kg), full(dkn), full(dvm)],
        out_specs=[pl.BlockSpec((d, 2 * MEM_W), lambda i: (0, 0)), pl.BlockSpec((1, LANES), lambda i: (0, 0)),
                   pl.BlockSpec((1, d), lambda i: (0, 0))],
        out_shape=[jax.ShapeDtypeStruct((d, 2 * MEM_W), F32), jax.ShapeDtypeStruct((1, LANES), F32),
                   jax.ShapeDtypeStruct((1, d), F32)],
        scratch_shapes=[pltpu.VMEM((m, 2 * MEM_W), BF16)],
        compiler_params=_cparams())(mem, gmem, wkv, kg, dkn, dvm)


def _merge_fwd(x1, ya, yb, yc, zg, bg, wa, wb, wc, wo, name):
    t, d = x1.shape
    tm = _row_tile(t, 256)

    def body(x_ref, ya_ref, yb_ref, yc_ref, zg_ref, bg_ref, wa_ref, wb_ref, wc_ref, wo_ref,
             x2_ref, mg_ref, pa_ref, pb_ref, pc_ref):
        merged = None
        for k, (y_ref, w_ref, p_ref) in enumerate(
                ((ya_ref, wa_ref, pa_ref), (yb_ref, wb_ref, pb_ref), (yc_ref, wc_ref, pc_ref))):
            sl = slice(k * d, (k + 1) * d)
            pr = _dot(y_ref[...], w_ref[...])
            p_ref[...] = pr.astype(BF16)
            term = jax.nn.sigmoid(zg_ref[:, sl] + bg_ref[:, sl]) * pr
            merged = term if merged is None else merged + term
        mb = merged.astype(BF16)
        mg_ref[...] = mb
        x2_ref[...] = x_ref[...] + _dot(mb, wo_ref[...])

    row = lambda n: pl.BlockSpec((tm, n), lambda i: (i, 0))
    full = lambda a: pl.BlockSpec(a.shape, lambda i: (0, 0))
    return pl.pallas_call(
        body, name=name, grid=(t // tm,),
        in_specs=[row(d), row(ya.shape[1]), row(yb.shape[1]), row(yc.shape[1]), row(3 * d), full(bg),
                  full(wa), full(wb), full(wc), full(wo)],
        out_specs=[row(d)] * 5,
        out_shape=[jax.ShapeDtypeStruct((t, d), F32)] + [jax.ShapeDtypeStruct((t, d), BF16)] * 4,
        compiler_params=_cparams())(x1, ya, yb, yc, zg, bg, wa, wb, wc, wo)


def _merge_bwd(dx2, pa, pb, pc, zg, bg, wa, wb, wc, wo, name, ex=None):
    t, d = dx2.shape
    tm = _row_tile(t, 256)

    def body(dx_ref, pa_ref, pb_ref, pc_ref, zg_ref, bg_ref, wa_ref, wb_ref, wc_ref, wo_ref,
             dpa_ref, dpb_ref, dpc_ref, dzg_ref, dbg_ref, dya_ref, dyb_ref, dyc_ref):
        @pl.when(pl.program_id(0) == 0)
        def _():
            dbg_ref[...] = jnp.zeros_like(dbg_ref)

        dm = _dot_nt(dx_ref[...].astype(BF16), wo_ref[...])
        for k, (p_ref, w_ref, dp_ref, dy_ref) in enumerate(
                ((pa_ref, wa_ref, dpa_ref, dya_ref), (pb_ref, wb_ref, dpb_ref, dyb_ref),
                 (pc_ref, wc_ref, dpc_ref, dyc_ref))):
            sl = slice(k * d, (k + 1) * d)
            gate = jax.nn.sigmoid(zg_ref[:, sl] + bg_ref[:, sl])
            dpr = (dm * gate).astype(BF16)
            dp_ref[...] = dpr
            dzg = dm * p_ref[...].astype(F32) * gate * (1.0 - gate)
            dzg_ref[:, sl] = dzg.astype(BF16)
            dbg_ref[:, sl] += jnp.sum(dzg, axis=0, keepdims=True)
            dy_ref[...] = _dot_nt(dpr, w_ref[...]).astype(dy_ref.dtype)

    row = lambda n: pl.BlockSpec((tm, n), lambda i: (i, 0))
    full = lambda a: pl.BlockSpec(a.shape, lambda i: (0, 0))
    na, nb, nc = wa.shape[0], wb.shape[0], wc.shape[0]
    return _call_with_exchange(
        ex, body, name, (t // tm,),
        [row(d), row(d), row(d), row(d), row(3 * d), full(bg), full(wa), full(wb), full(wc), full(wo)],
        [row(d), row(d), row(d), row(3 * d), pl.BlockSpec((1, 3 * d), lambda i: (0, 0)), row(na), row(nb), row(nc)],
        [jax.ShapeDtypeStruct((t, d), BF16)] * 3
        + [jax.ShapeDtypeStruct((t, 3 * d), BF16), jax.ShapeDtypeStruct((1, 3 * d), F32),
           jax.ShapeDtypeStruct((t, na), F32), jax.ShapeDtypeStruct((t, nb), BF16),
           jax.ShapeDtypeStruct((t, nc), BF16)],
        [], (dx2, pa, pb, pc, zg, bg, wa, wb, wc, wo))


def _adamw_math(w, g, m, v):
    bc1 = 1.0 - ADAM_B1 ** ADAM_STEP
    bc2 = 1.0 - ADAM_B2 ** ADAM_STEP
    nm = ADAM_B1 * m + (1.0 - ADAM_B1) * g
    nv = ADAM_B2 * v + (1.0 - ADAM_B2) * (g * g)
    delta = -ADAM_LR * ((nm / bc1) / (jnp.sqrt(nv / bc2) + ADAM_EPS) + ADAM_WD * w)
    return delta, nm, nv


def _div_tile(n, cap, mult):
    best = None
    for cand in range(mult, min(n, cap) + 1, mult):
        if n % cand == 0:
            best = cand
    assert best is not None, (n, cap, mult)
    return best


def _adamw(w, g, m, v, name):
    rows, cols = w.shape
    tr = rows if rows * cols <= 256 * 1024 else _div_tile(rows, 256, 8)

    def body(w_ref, g_ref, m_ref, v_ref, d_ref, nm_ref, nv_ref):
        d_ref[...], nm_ref[...], nv_ref[...] = _adamw_math(w_ref[...], g_ref[...], m_ref[...], v_ref[...])

    blk = pl.BlockSpec((tr, cols), lambda i: (i, 0))
    return pl.pallas_call(
        body, name=name, grid=(rows // tr,), in_specs=[blk] * 4, out_specs=[blk] * 3,
        out_shape=[jax.ShapeDtypeStruct((rows, cols), F32)] * 3, compiler_params=_cparams())(w, g, m, v)


def _adamw_slots(w, slots, m, v, name):
    _, hr, cols = w.shape
    tr = _div_tile(hr, 128, 16)

    def body(w_ref, s_ref, m_ref, v_ref, g_ref, d_ref, nm_ref, nv_ref):
        g = s_ref[0, 0].astype(F32)
        for k in range(1, N_CHIPS):
            g = g + s_ref[0, k].astype(F32)
        g_ref[0] = g
        d_ref[0], nm_ref[0], nv_ref[0] = _adamw_math(w_ref[0], g, m_ref[0], v_ref[0])

    blk = pl.BlockSpec((1, tr, cols), lambda h, i: (h, i, 0))
    return pl.pallas_call(
        body, name=name, grid=(2, hr // tr),
        in_specs=[blk, pl.BlockSpec((1, N_CHIPS, tr, cols), lambda h, i: (h, 0, i, 0)), blk, blk],
        out_specs=[blk] * 4, out_shape=[jax.ShapeDtypeStruct((2, hr, cols), F32)] * 4,
        compiler_params=_cparams())(w, slots, m, v)


ANY = pl.BlockSpec(memory_space=pl.ANY)


def _place():
    x, y, c = lax.axis_index("x"), lax.axis_index("y"), lax.axis_index("c")
    other_chips = [(1 - x, y), (x, 1 - y), (1 - x, 1 - y)]
    return x, y, c, other_chips


def _remote(src, dst, send_sem, recv_sem, to):
    return pltpu.make_async_remote_copy(src_ref=src, dst_ref=dst, send_sem=send_sem, recv_sem=recv_sem,
                                        device_id=to, device_id_type=MESH)


PIECE_BYTES = 384 * 1024


def _row_pieces(half_rows, cols):
    for n in (4, 2):
        if half_rows % (16 * n) == 0 and half_rows * cols * 2 // n >= PIECE_BYTES:
            return [pl.ds(k * (half_rows // n), half_rows // n) for k in range(n)]
    return [pl.ds(0, half_rows)]


def _pieces(arrays, rows_axis):
    return [(w, rows) for w, a in enumerate(arrays) for rows in _row_pieces(a.shape[rows_axis], a.shape[-1])]


def _gather_exchange(shards):
    nw = len(shards)
    pieces = _pieces(shards, 1)
    npc = len(pieces)

    def build(s_refs, g_refs, sems):
        send_sems, recv_sems, local_sems = sems
        x, y, c, chips = _place()
        me = 2 * x + y
        sibling = (x, y, 1 - c)
        mine = [pltpu.make_async_copy(s_refs[w], g_refs[w].at[me], local_sems.at[w]) for w in range(nw)]
        first = [_remote(s_refs[w].at[c, rows], g_refs[w].at[me, c, rows], send_sems.at[k, p], recv_sems.at[k, p],
                         (cx, cy, c)) for k, (cx, cy) in enumerate(chips) for p, (w, rows) in enumerate(pieces)]

        def start():
            for cp in mine + first:
                cp.start()

        arrived = [g_refs[w].at[2 * cx + cy, c, rows] for cx, cy in chips for w, rows in pieces]
        passed = [_remote(slab, slab, send_sems.at[3 + q // npc, q % npc], recv_sems.at[3 + q // npc, q % npc], sibling)
                  for q, slab in enumerate(arrived)]

        def pass_on():
            for q, slab in enumerate(arrived):
                k, p = q // npc, q % npc
                _remote(slab, slab, send_sems.at[k, p], recv_sems.at[k, p], (*chips[k], c)).wait_recv()
                passed[q].start()

        def finish():
            for k, (cx, cy) in enumerate(chips):
                for p, (w, rows) in enumerate(pieces):
                    slab = g_refs[w].at[2 * cx + cy, 1 - c, rows]
                    _remote(slab, slab, send_sems.at[3 + k, p], recv_sems.at[3 + k, p], sibling).wait_recv()
            for cp in first + passed:
                cp.wait_send()
            for cp in mine:
                cp.wait()

        return start, pass_on, finish

    return _Exchange(list(shards), [jax.ShapeDtypeStruct((N_CHIPS,) + s.shape, BF16) for s in shards],
                     [pltpu.SemaphoreType.DMA((6, npc)), pltpu.SemaphoreType.DMA((6, npc)),
                      pltpu.SemaphoreType.DMA((nw,))], build)


def _swap_halves(grads, name):
    nw = len(grads)

    def body(*refs):
        g_refs, sib_refs = refs[:nw], refs[nw:2 * nw]
        send_sems, recv_sems = refs[2 * nw:]
        x, y, c, _ = _place()
        copies = [_remote(g_refs[w].at[s, 1 - c], sib_refs[w].at[s], send_sems.at[s, w], recv_sems.at[s, w],
                          (x, y, 1 - c)) for w in range(nw) for s in range(N_CHIPS)]
        for cp in copies:
            cp.start()
        for cp in copies:
            cp.wait_recv()
        for cp in copies:
            cp.wait_send()

    return pl.pallas_call(
        body, name=name, in_specs=[ANY] * nw, out_specs=[ANY] * nw,
        out_shape=[jax.ShapeDtypeStruct((N_CHIPS,) + g.shape[2:], BF16) for g in grads],
        scratch_shapes=[pltpu.SemaphoreType.DMA((N_CHIPS, nw)), pltpu.SemaphoreType.DMA((N_CHIPS, nw))],
    )(*grads)


def _pair_sum(grad, sib, core, name):
    nchip, _, hr, cols = grad.shape
    tr = _div_tile(hr, 256, 16)

    def body(core_ref, a_ref, b_ref, o_ref):
        o_ref[...] = (a_ref[0].astype(F32) + b_ref[...].astype(F32)).astype(BF16)

    return pl.pallas_call(
        body, name=name,
        grid_spec=pltpu.PrefetchScalarGridSpec(
            num_scalar_prefetch=1, grid=(nchip, hr // tr),
            in_specs=[pl.BlockSpec((1, 1, tr, cols), lambda s, i, core_r: (s, core_r[0], i, 0)),
                      pl.BlockSpec((1, tr, cols), lambda s, i, core_r: (s, i, 0))],
            out_specs=pl.BlockSpec((1, tr, cols), lambda s, i, core_r: (s, i, 0))),
        out_shape=jax.ShapeDtypeStruct((nchip, hr, cols), BF16), compiler_params=_cparams())(core, grad, sib)


def _pair_sum_exchange(sums):
    nw = len(sums)
    pieces = _pieces(sums, 1)
    npc = len(pieces)

    def build(p_refs, o_refs, sems):
        send_sems, recv_sems, local_sems = sems
        x, y, c, chips = _place()
        me = 2 * x + y
        sibling = (x, y, 1 - c)
        mine = [pltpu.make_async_copy(p_refs[w].at[me], o_refs[w].at[c, 3], local_sems.at[w]) for w in range(nw)]
        first = [_remote(p_refs[w].at[2 * cx + cy, rows], o_refs[w].at[c, k, rows], send_sems.at[k, p],
                         recv_sems.at[k, p], (cx, cy, c))
                 for k, (cx, cy) in enumerate(chips) for p, (w, rows) in enumerate(pieces)]

        def start():
            for cp in mine + first:
                cp.start()

        passed = [_remote(o_refs[w].at[c, k, rows], o_refs[w].at[c, k, rows], send_sems.at[3 + k, p],
                          recv_sems.at[3 + k, p], sibling) for k in range(N_CHIPS) for p, (w, rows) in enumerate(pieces)]

        def pass_on():
            for k in range(N_CHIPS):
                own_waited = set()
                for p, (w, rows) in enumerate(pieces):
                    if k < 3:
                        first[k * npc + p].wait_recv()
                    elif w not in own_waited:
                        mine[w].wait()
                        own_waited.add(w)
                    passed[k * npc + p].start()

        def finish():
            for k in range(N_CHIPS):
                for p, (w, rows) in enumerate(pieces):
                    slab = o_refs[w].at[1 - c, k, rows]
                    _remote(slab, slab, send_sems.at[3 + k, p], recv_sems.at[3 + k, p], sibling).wait_recv()
            for cp in first + passed:
                cp.wait_send()

        return start, pass_on, finish

    return _Exchange(list(sums), [jax.ShapeDtypeStruct((2,) + p.shape, BF16) for p in sums],
                     [pltpu.SemaphoreType.DMA((7, npc)), pltpu.SemaphoreType.DMA((7, npc)),
                      pltpu.SemaphoreType.DMA((nw,))], build)


def _small_sum_exchange(vec):
    m_per, n = vec.shape

    def build(ins, outs, scr):
        (x_ref,), (out_ref,) = ins, outs
        gath_ref, sum_ref, send_sems, recv_sems, local_sem, out_sem = scr
        x, y, c, chips = _place()
        me, sibling = (x, y, c), (x, y, 1 - c)

        def rows(px, py, pc):
            return gath_ref.at[pl.ds((4 * px + 2 * py + pc) * m_per, m_per), :]

        def copy(k, block, to, src=None):
            return pltpu.make_async_remote_copy(
                src_ref=rows(*block) if src is None else src, dst_ref=rows(*block),
                send_sem=send_sems.at[k], recv_sem=recv_sems.at[k], device_id=to, device_id_type=MESH)

        mine = pltpu.make_async_copy(x_ref, rows(*me), local_sem)
        first = [copy(0, me, sibling, src=x_ref)] + [copy(1 + j, me, (*chip, c), src=x_ref)
                                                     for j, chip in enumerate(chips)]

        def start():
            for cp in [mine] + first:
                cp.start()

        passed = [copy(4 + j, (*chip, c), sibling) for j, chip in enumerate(chips)]

        def pass_on():
            for j, chip in enumerate(chips):
                copy(1 + j, (*chip, c), me).wait_recv()
                passed[j].start()

        def finish():
            copy(0, sibling, me).wait_recv()
            for j, chip in enumerate(chips):
                copy(4 + j, (*chip, 1 - c), me).wait_recv()
            for cp in first + passed:
                cp.wait_send()
            mine.wait()
            acc = gath_ref[pl.ds(0, m_per), :]
            for k in range(1, N_DEV):
                acc = acc + gath_ref[pl.ds(k * m_per, m_per), :]
            sum_ref[...] = acc
            done = pltpu.make_async_copy(sum_ref, out_ref, out_sem)
            done.start()
            done.wait()

        return start, pass_on, finish

    return _Exchange([vec], [jax.ShapeDtypeStruct((m_per, n), F32)],
                     [pltpu.VMEM((N_DEV * m_per, n), F32), pltpu.VMEM((m_per, n), F32), pltpu.SemaphoreType.DMA((7,)),
                      pltpu.SemaphoreType.DMA((7,)), pltpu.SemaphoreType.DMA, pltpu.SemaphoreType.DMA], build)


def _pack_small(vals, tail=()):
    flat = jnp.concatenate([vals[name].reshape(-1).astype(F32) for name, _ in SMALL] + [v.reshape(1) for v in tail])
    flat = jnp.pad(flat, (0, SMALL_ROWS * LANES - flat.shape[0]))
    return flat.reshape(SMALL_ROWS, LANES)


def _unpack_small(packed):
    flat = packed.reshape(-1)
    out, off = {}, 0
    for name, shape in SMALL:
        n = int(np.prod(shape))
        out[name] = flat[off:off + n].reshape(shape)
        off += n
    return out


def _head_pad_cols(w, heads, real):
    k = w.shape[0]
    return jnp.pad(w.reshape(k, heads, real), ((0, 0), (0, 0), (0, LANES - real))).reshape(k, heads * LANES)


def _rope_tables(positions):
    half = MLA_ROPE // 2
    inv = ROPE_BASE ** (-jnp.arange(half, dtype=F32) / half)
    ang = positions.astype(F32)[:, None] * inv
    cos, sin = jnp.cos(ang), jnp.sin(ang)
    t = positions.shape[0]
    z = lambda n: jnp.zeros((t, n), F32)
    rc = jnp.concatenate([jnp.ones((t, MLA_NOPE), F32), cos, cos, z(LANES - MLA_QK)], axis=1)
    rs1 = jnp.concatenate([z(MLA_NOPE), -sin, z(LANES - MLA_NOPE - half)], axis=1)
    rs2 = jnp.concatenate([z(MLA_NOPE + half), sin, z(LANES - MLA_QK)], axis=1)
    return rc, rs1, rs2


FFN1_WEIGHTS = ("ffn1_w_gu", "ffn1_w_down")
FFN2_WEIGHTS = ("ffn2_w_gu", "ffn2_w_down")
MIXER_WEIGHTS = tuple(n for n, *_ in SHARDED if n not in FFN1_WEIGHTS + FFN2_WEIGHTS)
SHARD_SHAPE = {n: (r, c, kind) for n, r, c, kind in SHARDED}


def _from_blocks(name, gathered):
    r, c, kind = SHARD_SHAPE[name]
    blk = gathered.reshape(N_CHIPS, r, c)
    return blk, (blk.transpose(1, 0, 2).reshape(r, N_CHIPS * c) if kind == "col" else blk.reshape(N_CHIPS * r, c))


def _grad_pair_sums(names, gw, core, tag):
    by_owner = []
    for name in names:
        r, c, kind = SHARD_SHAPE[name]
        if gw[name].dtype == BF16:
            blk = gw[name]
        elif kind == "col":
            blk = gw[name].reshape(r, N_CHIPS, c).transpose(1, 0, 2)
        else:
            blk = gw[name].reshape(N_CHIPS, r, c)
        by_owner.append(blk.astype(BF16).reshape(N_CHIPS, 2, r // 2, c))
    received = _swap_halves(by_owner, "grad_swap_" + tag)
    return [_pair_sum(g, s, core, "pair_sum_" + n) for g, s, n in zip(by_owner, received, names)]


def _device_step(x, mem, positions, tgt, small, shards, core):
    d = D_MODEL
    g_ffn1, g_mix, g_ffn2 = small["ffn1_norm"], small["mix_norm"], small["ffn2_norm"]
    big = {}
    for name, g in zip(FFN1_WEIGHTS, _run_exchange(_gather_exchange([shards[n] for n in FFN1_WEIGHTS]), "gather_ffn1")):
        big[name + "#blocks"], big[name] = _from_blocks(name, g)
    wgu1, wd1 = big["ffn1_w_gu#blocks"], big["ffn1_w_down"].reshape(2, FF_TILE, d)
    x1, gpre1, upre1, h, *rest = _ffn_fwd(x, g_ffn1, wgu1, wd1, "ffn1_fwd", next_gain=g_mix,
                                          ex=_gather_exchange([shards[n] for n in MIXER_WEIGHTS]))
    for name, g in zip(MIXER_WEIGHTS, rest):
        big[name + "#blocks"], big[name] = _from_blocks(name, g)
    w_in = big["w_in"]
    w_uv_, w_cq, w_ckv = w_in[:, :COL_CQ], w_in[:, COL_CQ:COL_CKV], w_in[:, COL_CKV:COL_KR]
    w_kr = jnp.pad(w_in[:, COL_KR:COL_QM], ((0, 0), (MLA_NOPE, LANES - MLA_QK)))
    w_qm, w_g = w_in[:, COL_QM:COL_GATE], w_in[:, COL_GATE:]
    segs = (w_uv_, w_cq, w_ckv, w_kr, w_qm, w_g)
    wuq = _head_pad_cols(big["mla_w_uq"], MLA_HEADS, MLA_QK)
    ukv = big["mla_w_ukv"].reshape(MLA_KV_RANK, MLA_HEADS, 2, MLA_NOPE)
    wuk = _head_pad_cols(ukv[:, :, 0].reshape(MLA_KV_RANK, -1), MLA_HEADS, MLA_NOPE)
    wuv = _head_pad_cols(ukv[:, :, 1].reshape(MLA_KV_RANK, -1), MLA_HEADS, MLA_NOPE)
    wkv = big["mem_w_kv"]
    wa, wc, wo = big["w_branch_a"], big["w_branch_c"], big["w_out"]
    wb = jnp.pad(big["w_branch_b"].reshape(MLA_HEADS, MLA_NOPE, d),
                 ((0, 0), (0, LANES - MLA_NOPE), (0, 0))).reshape(MLA_HEADS * LANES, d)
    qg = jnp.pad(small["mla_q_norm"], ((0, 0), (0, LANES - MLA_QK)))
    kg = jnp.pad(small["mla_k_norm"], ((0, 0), (0, LANES - MLA_QK)))
    causal = jnp.tril(jnp.ones((CHUNK, CHUNK), bool))
    wt_f = jnp.where(causal[None], small["sg_w"][0], 0.0)
    wt, wt_t = wt_f.astype(BF16), wt_f.transpose(0, 2, 1).astype(BF16)
    bias_l = jnp.repeat(small["sg_b"][0].T, 64, axis=1)
    rc, rs1, rs2 = _rope_tables(positions)

    zuv, zcq, zckv, zkr, zqm, zg = _mm_cols(h, segs, [F32] * 5 + [BF16], "in_proj")
    ya = _sgu_fwd(zuv, small["sg_ln_g"], small["sg_ln_b"], wt, bias_l, "sgu_fwd")
    q, k, v, cqn, ckvn = _mla_prep_fwd(zcq, zckv, zkr, small["mla_cq_norm"], small["mla_ckv_norm"], qg, kg,
                                       wuq, wuk, wuv, rc, rs1, rs2, "mla_prep_fwd")
    yb, lse, *rest = _attn_fwd(q, k, v, "mla_attn_fwd", ex=_gather_exchange([shards[n] for n in FFN2_WEIGHTS]))
    for name, g in zip(FFN2_WEIGHTS, rest):
        big[name + "#blocks"], big[name] = _from_blocks(name, g)
    wgu2, wd2 = big["ffn2_w_gu#blocks"], big["ffn2_w_down"].reshape(2, FF_TILE, d)
    km, vm, memn = _mem_kv_fwd(mem, small["mem_norm"], wkv, small["mem_k_norm"], "mem_kv_fwd")
    yc = _mem_attn_fwd(zqm, small["mem_q_norm"], km, vm, "mem_attn_fwd")
    x2, merged, pa, pb, pc = _merge_fwd(x1, ya, yb, yc, zg, small["b_gate"], wa, wb, wc, wo, "merge_fwd")
    dy, loss_row, gpre2, upre2 = _ffn_fwd(x2, g_ffn2, wgu2, wd2, "ffn2_fwd", target=tgt)

    gw, gs, slots = {}, {}, {}

    def ffn_grads(prefix, xin, gain, dyin, gpre, upre, wgu, wd, ex=None, ex_names=(), last=False):
        dx, dgain, xn, dgt, dup, act, *got = _ffn_bwd(xin, gain, dyin, gpre, upre, wgu, wd, prefix + "_bwd", ex=ex)
        slots.update(zip(ex_names, got))
        gs[prefix + "_norm"] = dgain
        gw[prefix + "_w_gu"] = jnp.concatenate(
            [_mm_tn(xn, dgt, prefix + "_dwg", col_blocks=True, out_dtype=BF16),
             _mm_tn(xn, dup, prefix + "_dwu", col_blocks=True, out_dtype=BF16)], axis=0)
        rows_down = SHARD_SHAPE[prefix + "_w_down"][0]
        if last:
            small_sum = _small_sum_exchange(_pack_small(gs, tail=[loss_row[0, 0]]))
            dwd, summed = _mm_tn(act, dyin, prefix + "_dwd", scale=0.5, ex=small_sum, out_dtype=BF16)
            gw[prefix + "_w_down"] = dwd.reshape(N_CHIPS, rows_down, d)
            return dx, summed
        gw[prefix + "_w_down"] = _mm_tn(act, dyin, prefix + "_dwd", scale=0.5, out_dtype=BF16).reshape(
            N_CHIPS, rows_down, d)
        return dx

    dx2 = ffn_grads("ffn2", x2, g_ffn2, dy, gpre2, upre2, wgu2, wd2)
    ffn2_sums = _pair_sum_exchange(_grad_pair_sums(FFN2_WEIGHTS, gw, core, "ffn2"))
    dpa, dpb, dpc, dzg, dbg, dya, dyb, dyc, *got = _merge_bwd(dx2, pa, pb, pc, zg, small["b_gate"], wa, wb, wc, wo,
                                                              "merge_bwd", ex=ffn2_sums)
    slots.update(zip(FFN2_WEIGHTS, got))
    gs["b_gate"] = dbg
    gw["w_out"] = _mm_tn(merged, dx2, "dw_out")
    gw["w_branch_a"] = _mm_tn(ya, dpa, "dw_branch_a")
    gw["w_branch_b"] = _mm_tn(yb, dpb, "dw_branch_b").reshape(MLA_HEADS, LANES, d)[:, :MLA_NOPE].reshape(-1, d)
    gw["w_branch_c"] = _mm_tn(yc, dpc, "dw_branch_c")

    dzuv, dwt, dbl, dlg, dlb = _sgu_bwd(zuv, dya, small["sg_ln_g"], small["sg_ln_b"], wt, wt_t, bias_l, "sgu_bwd")
    gs["sg_w"], gs["sg_b"] = dwt[None], dbl[:, :SG_GROUPS].T[None]
    gs["sg_ln_g"], gs["sg_ln_b"] = dlg, dlb

    delta_rows, lse_rows = _attn_bwd_rows(yb, lse, dyb, "mla_attn_bwd_rows")
    dq, dk, dv = _attn_bwd(q, k, v, delta_rows, lse_rows, dyb, "mla_attn_bwd")
    dzcq, dzckv, dzkr, dql, dkl, dgcq, dgckv, dqg, dkg = _mla_prep_bwd(
        zcq, zckv, zkr, small["mla_cq_norm"], small["mla_ckv_norm"], qg, kg, wuq, wuk, wuv, rc, rs1, rs2,
        dq, dk, dv, "mla_prep_bwd")
    gs["mla_cq_norm"], gs["mla_ckv_norm"] = dgcq, dgckv
    gs["mla_q_norm"], gs["mla_k_norm"] = dqg[:, :MLA_QK], dkg[:, :MLA_QK]
    gw["mla_w_uq"] = _mm_tn(cqn, dql, "dw_uq").reshape(MLA_Q_RANK, MLA_HEADS, LANES)[:, :, :MLA_QK].reshape(
        MLA_Q_RANK, -1)
    dwuk = _mm_tn(ckvn, dkl, "dw_uk").reshape(MLA_KV_RANK, MLA_HEADS, LANES)[:, :, :MLA_NOPE]
    dwuv = _mm_tn(ckvn, dv, "dw_uv").reshape(MLA_KV_RANK, MLA_HEADS, LANES)[:, :, :MLA_NOPE]
    gw["mla_w_ukv"] = jnp.concatenate([dwuk, dwuv], axis=2).reshape(MLA_KV_RANK, -1)

    dzqm, dkn, dvm, dmqg = _mem_attn_bwd(zqm, dyc, small["mem_q_norm"], km, vm, "mem_attn_bwd")
    gs["mem_q_norm"] = dmqg
    gw["mem_w_kv"], gs["mem_k_norm"], gs["mem_norm"] = _mem_kv_bwd(
        mem, small["mem_norm"], wkv, small["mem_k_norm"], dkn, dvm, "mem_kv_bwd")

    dzs = (dzuv, dzcq, dzckv, dzkr, dzqm, dzg)
    dws = list(_mm_tn_cols(h, dzs[:5], "dw_in_narrow")) + [_mm_tn(h, dzg, "dw_in_gate")]
    dws[3] = dws[3][:, MLA_NOPE:MLA_QK]
    gw["w_in"] = jnp.concatenate(dws, axis=1)
    dx1, gs["mix_norm"] = _proj_norm_bwd(dzs, [w.T for w in segs], x1, g_mix, dx2, "in_proj_bwd")
    mixer_sums = _pair_sum_exchange(_grad_pair_sums(MIXER_WEIGHTS, gw, core, "mixer"))
    dx, summed = ffn_grads("ffn1", x, g_ffn1, dx1, gpre1, upre1, wgu1, wd1, ex=mixer_sums, ex_names=MIXER_WEIGHTS,
                           last=True)
    ffn1_sums = _pair_sum_exchange(_grad_pair_sums(FFN1_WEIGHTS, gw, core, "ffn1"))
    slots.update(zip(FFN1_WEIGHTS, _run_exchange(ffn1_sums, "grad_exchange_ffn1")))
    return dx, slots, summed


def kernel(x, mem, positions, ffn1_norm, ffn1_w_gu, ffn1_w_down, mix_norm, w_in, b_gate, sg_ln_g, sg_ln_b, sg_w, sg_b, mla_cq_norm, mla_w_uq, mla_ckv_norm, mla_w_ukv, mla_q_norm, mla_k_norm, mem_norm, mem_w_kv, mem_q_norm, mem_k_norm, w_branch_a, w_branch_b, w_branch_c, w_out, ffn2_norm, ffn2_w_gu, ffn2_w_down, loss_target, m_ffn1_norm, m_ffn1_w_gu, m_ffn1_w_down, m_mix_norm, m_w_in, m_b_gate, m_sg_ln_g, m_sg_ln_b, m_sg_w, m_sg_b, m_mla_cq_norm, m_mla_w_uq, m_mla_ckv_norm, m_mla_w_ukv, m_mla_q_norm, m_mla_k_norm, m_mem_norm, m_mem_w_kv, m_mem_q_norm, m_mem_k_norm, m_w_branch_a, m_w_branch_b, m_w_branch_c, m_w_out, m_ffn2_norm, m_ffn2_w_gu, m_ffn2_w_down, v_ffn1_norm, v_ffn1_w_gu, v_ffn1_w_down, v_mix_norm, v_w_in, v_b_gate, v_sg_ln_g, v_sg_ln_b, v_sg_w, v_sg_b, v_mla_cq_norm, v_mla_w_uq, v_mla_ckv_norm, v_mla_w_ukv, v_mla_q_norm, v_mla_k_norm, v_mem_norm, v_mem_w_kv, v_mem_q_norm, v_mem_k_norm, v_w_branch_a, v_w_branch_b, v_w_branch_c, v_w_out, v_ffn2_norm, v_ffn2_w_gu, v_ffn2_w_down):
    args = dict(locals())
    weights = {n: args[n] for n in WEIGHT_ORDER}
    mom_m = {n: args["m_" + n] for n in WEIGHT_ORDER}
    mom_v = {n: args["v_" + n] for n in WEIGHT_ORDER}
    small = {n: weights[n] for n, _ in SMALL}
    halves = lambda a, r, c: a.reshape(2, r // 2, c)

    shards = {n: halves(weights[n][0].astype(BF16), r, c) for n, r, c, _ in SHARDED}
    core = lax.axis_index("c").astype(jnp.int32).reshape(1)
    dx, slots, summed = _device_step(x[0], mem[0], positions[0], loss_target[0], small, shards, core)
    loss = summed.reshape(-1)[_N_SMALL]
    small_grads = _unpack_small(summed)

    grads, deltas, new_m, new_v = {}, {}, {}, {}
    for name, r, c, _ in SHARDED:
        outs = _adamw_slots(halves(weights[name][0], r, c), slots[name], halves(mom_m[name][0], r, c),
                            halves(mom_v[name][0], r, c), "adamw_" + name)
        shape = weights[name].shape
        grads[name], deltas[name], new_m[name], new_v[name] = [o.reshape(shape) for o in outs]
    dlt, nm, nv = _adamw(_pack_small(small), _pack_small(small_grads), _pack_small({n: mom_m[n] for n, _ in SMALL}),
                         _pack_small({n: mom_v[n] for n, _ in SMALL}), "adamw_small")
    for name, _ in SMALL:
        grads[name] = small_grads[name]
    deltas.update(_unpack_small(dlt))
    new_m.update(_unpack_small(nm))
    new_v.update(_unpack_small(nv))

    return (loss, dx[None], *[grads[n] for n in WEIGHT_ORDER], *[deltas[n] for n in WEIGHT_ORDER],
            *[new_m[n] for n in WEIGHT_ORDER], *[new_v[n] for n in WEIGHT_ORDER])
```

```python
import functools
from typing import Callable, NamedTuple

import numpy as np
import jax
import jax.numpy as jnp
from jax import lax
from jax.experimental import pallas as pl
from jax.experimental.pallas import tpu as pltpu

F32 = jnp.float32
BF16 = jnp.bfloat16

D_MODEL = 1024
D_FF = 2816
FF_TILE = 1408
SG_WIDTH = 512
SG_GROUPS = 8
CHUNK = 128
MLA_HEADS = 8
MLA_QK = 96
MLA_NOPE = 64
MLA_ROPE = 32
MLA_Q_RANK = 384
MLA_KV_RANK = 256
MEM_HEADS = 4
MEM_LEN = 256
LANES = 128
EPS = 1e-6
NEG = -1e30
ROPE_BASE = 10000.0
N_CHIPS = 4
N_DEV = 8

ADAM_LR = 0.001
ADAM_B1 = 0.9
ADAM_B2 = 0.999
ADAM_EPS = 1e-08
ADAM_WD = 0.01
ADAM_STEP = 10

COL_V = 512
COL_CQ = 1024
COL_CKV = 1408
COL_KR = 1664
COL_QM = 1696
COL_GATE = 2208
IN_COLS = 5280

VMEM_LIMIT_BYTES = 56 * 1024 * 1024
INV_SQRT2 = 0.7071067811865476
INV_SQRT_2PI = 0.3989422804014327
LOG2E = 1.4426950408889634
ATTN_SCALE = MLA_QK ** -0.5
V_ONES_LANE = 64
ATTN_SCALE2 = ATTN_SCALE * LOG2E

SHARDED = (
    ("ffn1_w_gu", 1024, 1408, "col"),
    ("ffn1_w_down", 704, 1024, "row"),
    ("w_in", 1024, 1320, "col"),
    ("mla_w_uq", 384, 192, "col"),
    ("mla_w_ukv", 256, 256, "col"),
    ("mem_w_kv", 256, 1024, "row"),
    ("w_branch_a", 512, 256, "col"),
    ("w_branch_b", 512, 256, "col"),
    ("w_branch_c", 512, 256, "col"),
    ("w_out", 256, 1024, "row"),
    ("ffn2_w_gu", 1024, 1408, "col"),
    ("ffn2_w_down", 704, 1024, "row"),
)
SMALL = (
    ("ffn1_norm", (1, 1024)), ("mix_norm", (1, 1024)), ("b_gate", (1, 3072)),
    ("sg_ln_g", (1, 512)), ("sg_ln_b", (1, 512)), ("sg_w", (1, 8, 128, 128)),
    ("sg_b", (1, 8, 128)), ("mla_cq_norm", (1, 384)), ("mla_ckv_norm", (1, 256)),
    ("mla_q_norm", (1, 96)), ("mla_k_norm", (1, 96)), ("mem_norm", (1, 1024)),
    ("mem_q_norm", (1, 128)), ("mem_k_norm", (1, 128)), ("ffn2_norm", (1, 1024)),
)
WEIGHT_ORDER = (
    "ffn1_norm", "ffn1_w_gu", "ffn1_w_down", "mix_norm", "w_in", "b_gate", "sg_ln_g", "sg_ln_b",
    "sg_w", "sg_b", "mla_cq_norm", "mla_w_uq", "mla_ckv_norm", "mla_w_ukv", "mla_q_norm",
    "mla_k_norm", "mem_norm", "mem_w_kv", "mem_q_norm", "mem_k_norm", "w_branch_a", "w_branch_b",
    "w_branch_c", "w_out", "ffn2_norm", "ffn2_w_gu", "ffn2_w_down",
)

_N_SMALL = sum(int(np.prod(s)) for _, s in SMALL)
SMALL_ROWS = -(-_N_SMALL // (LANES * 8)) * 8

MESH = pl.DeviceIdType.MESH


def _cparams():
    return pltpu.CompilerParams(vmem_limit_bytes=VMEM_LIMIT_BYTES)


def _dot(a, b):
    return jnp.dot(a, b, preferred_element_type=F32)


def _dot_nt(a, b):
    return lax.dot_general(a, b, (((1,), (1,)), ((), ())), preferred_element_type=F32)


def _dot_tn(a, b):
    return lax.dot_general(a, b, (((0,), (0,)), ((), ())), preferred_element_type=F32)


def _gelu(x):
    return 0.5 * x * (1.0 + lax.erf(x * INV_SQRT2))


def _gelu_grad(x):
    return 0.5 * (1.0 + lax.erf(x * INV_SQRT2)) + x * jnp.exp(-0.5 * x * x) * INV_SQRT_2PI


def _rstd(x, n):
    return lax.rsqrt(jnp.sum(x * x, axis=-1, keepdims=True) * (1.0 / n) + EPS)


def _rms_vjp(x, r, g, dy, n):
    dxh = dy * g
    dx = r * dxh - x * (r * r * r) * (jnp.sum(dxh * x, axis=-1, keepdims=True) * (1.0 / n))
    return dx, dy * x * r


def _row_tile(t, want):
    return min(t, want)


def _wide_tile(n):
    if n <= 1024:
        return n
    if n % 1024 == 0:
        return 1024
    assert n % FF_TILE == 0, n
    return FF_TILE


def _mm_cols(a, ws, out_dtypes, name, ex=None):
    t, kdim = a.shape
    tm = _row_tile(t, 256)
    n = len(ws)

    def body(*refs):
        av = refs[0][...]
        for w_ref, o_ref in zip(refs[1:1 + n], refs[1 + n:]):
            o_ref[...] = _dot(av, w_ref[...]).astype(o_ref.dtype)

    row = lambda width: pl.BlockSpec((tm, width), lambda i: (i, 0))
    return _call_with_exchange(
        ex, body, name, (t // tm,),
        [row(kdim)] + [pl.BlockSpec(w.shape, lambda i: (0, 0)) for w in ws],
        [row(w.shape[1]) for w in ws],
        [jax.ShapeDtypeStruct((t, w.shape[1]), dt) for w, dt in zip(ws, out_dtypes)], [], (a, *ws))


def _proj_norm_bwd(dzs, wts, x, g, dres, name):
    t, d = x.shape
    tm = _row_tile(t, 256)
    n = len(dzs)

    def body(*refs):
        x_ref, g_ref, r_ref, dx_ref, dg_ref = refs[2 * n:]

        @pl.when(pl.program_id(0) == 0)
        def _():
            dg_ref[...] = jnp.zeros_like(dg_ref)

        dh = None
        for dz_ref, w_ref in zip(refs[:n], refs[n:2 * n]):
            part = _dot(dz_ref[...], w_ref[...])
            dh = part if dh is None else dh + part
        xv = x_ref[...]
        dx, dgr = _rms_vjp(xv, _rstd(xv, d), g_ref[...], dh, d)
        dx_ref[...] = r_ref[...] + dx
        dg_ref[...] += jnp.sum(dgr, axis=0, keepdims=True)

    row = lambda width: pl.BlockSpec((tm, width), lambda i: (i, 0))
    vec = pl.BlockSpec((1, d), lambda i: (0, 0))
    return pl.pallas_call(
        body, name=name, grid=(t // tm,),
        in_specs=[row(dz.shape[1]) for dz in dzs] + [pl.BlockSpec(w.shape, lambda i: (0, 0)) for w in wts]
        + [row(d), vec, row(d)],
        out_specs=[row(d), vec],
        out_shape=[jax.ShapeDtypeStruct((t, d), F32), jax.ShapeDtypeStruct((1, d), F32)],
        compiler_params=_cparams())(*dzs, *wts, x, g, dres)


def _mm_tn_cols(a, bs, name):
    t, m = a.shape
    tk = _row_tile(t, 512)
    n = len(bs)

    def body(*refs):
        @pl.when(pl.program_id(0) == 0)
        def _():
            for o_ref in refs[1 + n:]:
                o_ref[...] = jnp.zeros_like(o_ref)

        av = refs[0][...].astype(BF16)
        for b_ref, o_ref in zip(refs[1:1 + n], refs[1 + n:]):
            o_ref[...] += _dot_tn(av, b_ref[...].astype(BF16))

    row = lambda width: pl.BlockSpec((tk, width), lambda k: (k, 0))
    return pl.pallas_call(
        body, name=name, grid=(t // tk,), in_specs=[row(m)] + [row(b.shape[1]) for b in bs],
        out_specs=[pl.BlockSpec((m, b.shape[1]), lambda k: (0, 0)) for b in bs],
        out_shape=[jax.ShapeDtypeStruct((m, b.shape[1]), F32) for b in bs],
        compiler_params=_cparams())(a, *bs)


def _mm_tn(a, b, name, scale=1.0, ex=None, col_blocks=False, out_dtype=F32):
    t, m = a.shape
    n = b.shape[1]
    tm, tn = _wide_tile(m), _wide_tile(n)
    tk = _row_tile(t, 1024)
    nk = t // tk
    in_place = out_dtype == F32

    def body(a_ref, b_ref, o_ref, *scr):
        k = pl.program_id(2)
        acc_ref = o_ref if in_place else scr[0]

        @pl.when(k == 0)
        def _():
            acc_ref[...] = jnp.zeros_like(acc_ref)

        prod = _dot_tn(a_ref[...].astype(BF16), b_ref[...].astype(BF16))
        acc_ref[...] += prod.reshape(acc_ref.shape)
        if scale != 1.0 or not in_place:
            @pl.when(k == nk - 1)
            def _():
                o_ref[...] = (acc_ref[...] * scale).astype(out_dtype).reshape(o_ref.shape)

    if col_blocks:
        out_spec = pl.BlockSpec((1, tm, tn), lambda i, j, k: (j, i, 0))
        out_shape = jax.ShapeDtypeStruct((n // tn, m, tn), out_dtype)
    else:
        out_spec = pl.BlockSpec((tm, tn), lambda i, j, k: (i, j))
        out_shape = jax.ShapeDtypeStruct((m, n), out_dtype)
    outs = _call_with_exchange(
        ex, body, name, (m // tm, n // tn, nk),
        [pl.BlockSpec((tk, tm), lambda i, j, k: (k, i)), pl.BlockSpec((tk, tn), lambda i, j, k: (k, j))],
        [out_spec], [out_shape], [] if in_place else [pltpu.VMEM((tm, tn), F32)], (a, b))
    return outs[0] if ex is None else outs


PASS_ON_STEPS_BEFORE_END = 8


class _Exchange(NamedTuple):
    operands: list
    out_shapes: list
    sem_shapes: list
    build: Callable


def _call_with_exchange(ex, body, name, grid, in_specs, out_specs, out_shape, scratch_shapes, operands, prefetch=()):
    n_pre = len(prefetch)
    total = int(np.prod(grid))
    pass_step = max(total // 2, total - PASS_ON_STEPS_BEFORE_END)

    def call(kernel, ins, outs, shapes, scratch):
        if n_pre:
            spec = pltpu.PrefetchScalarGridSpec(num_scalar_prefetch=n_pre, grid=grid, in_specs=ins, out_specs=outs,
                                                scratch_shapes=scratch)
            return pl.pallas_call(kernel, name=name, grid_spec=spec, out_shape=shapes, compiler_params=_cparams())
        return pl.pallas_call(kernel, name=name, grid=grid, in_specs=ins, out_specs=outs, out_shape=shapes,
                              scratch_shapes=scratch, compiler_params=_cparams())

    if ex is None:
        return call(body, in_specs, out_specs, out_shape, scratch_shapes)(*prefetch, *operands)
    n_in, n_out, n_scr = len(in_specs), len(out_specs), len(scratch_shapes)
    k_in, k_out = len(ex.operands), len(ex.out_shapes)

    def carried(*refs):
        pre, refs = refs[:n_pre], refs[n_pre:]
        a, b = n_in, n_in + k_in
        c, e = b + n_out, b + n_out + k_out
        f = e + n_scr
        start, pass_on, finish = ex.build(refs[a:b], refs[c:e], refs[f:])
        step = functools.reduce(lambda lin, ax: lin * grid[ax] + pl.program_id(ax), range(len(grid)), 0)
        pl.when(step == 0)(start)
        body(*pre, *refs[:a], *refs[b:c], *refs[e:f])
        pl.when(step == pass_step)(pass_on)
        pl.when(step == total - 1)(finish)

    return call(carried, list(in_specs) + [ANY] * k_in, list(out_specs) + [ANY] * k_out,
                list(out_shape) + list(ex.out_shapes), list(scratch_shapes) + list(ex.sem_shapes),
                )(*prefetch, *operands, *ex.operands)


def _run_exchange(ex, name):
    k_in, k_out = len(ex.operands), len(ex.out_shapes)

    def body(*refs):
        start, pass_on, finish = ex.build(refs[:k_in], refs[k_in:k_in + k_out], refs[k_in + k_out:])
        start()
        pass_on()
        finish()

    return pl.pallas_call(body, name=name, in_specs=[ANY] * k_in, out_specs=[ANY] * k_out,
                          out_shape=list(ex.out_shapes), scratch_shapes=list(ex.sem_shapes))(*ex.operands)


def _ffn_fwd(x, g, wgu4, wd2, name, ex=None, next_gain=None, target=None):
    t, d = x.shape
    tm = _row_tile(t, 512)
    assert next_gain is None or target is None
    extra = [a for a in (next_gain, target) if a is not None]

    def body(*refs):
        x_ref, g_ref, wg_ref, wu_ref, wd_ref = refs[:5]
        e_ref = refs[5] if extra else None
        outs, (xn_scr, acc_scr) = refs[5 + len(extra):-2], refs[-2:]
        if target is not None:
            dy_ref, loss_ref, gg_ref, uu_ref = outs
        elif next_gain is not None:
            o_ref, gg_ref, uu_ref, h_ref = outs
        else:
            o_ref, gg_ref, uu_ref = outs
        i, j = pl.program_id(0), pl.program_id(1)

        @pl.when(j == 0)
        def _():
            xv = x_ref[...]
            xn_scr[...] = (xv * _rstd(xv, d) * g_ref[...]).astype(BF16)
            acc_scr[...] = jnp.zeros_like(acc_scr)

        if target is not None:
            @pl.when((i == 0) & (j == 0))
            def _():
                loss_ref[...] = jnp.zeros_like(loss_ref)

        xn = xn_scr[...]
        gg = _dot(xn, wg_ref[0])
        uu = _dot(xn, wu_ref[0])
        gg_ref[...] = gg.astype(BF16)
        uu_ref[...] = uu.astype(BF16)
        act = gg * jax.nn.sigmoid(gg) * uu
        acc_scr[...] += _dot(act.astype(BF16), wd_ref[0])

        @pl.when(j == 1)
        def _():
            y = x_ref[...] + 0.5 * acc_scr[...]
            if target is not None:
                e = y - e_ref[...]
                dy_ref[...] = e * (1.0 / d)
                part = 0.5 * jnp.sum(jnp.sum(e * e, axis=-1, keepdims=True) * (1.0 / d), axis=0, keepdims=True)
                loss_ref[...] += jnp.broadcast_to(part, loss_ref.shape)
            else:
                o_ref[...] = y
                if next_gain is not None:
                    h_ref[...] = (y * _rstd(y, d) * e_ref[...]).astype(BF16)

    row = pl.BlockSpec((tm, d), lambda i, j: (i, 0))
    vec = pl.BlockSpec((1, d), lambda i, j: (0, 0))
    ffb = pl.BlockSpec((tm, FF_TILE), lambda i, j: (i, j))
    f32_rows, bf16_ff = jax.ShapeDtypeStruct((t, d), F32), jax.ShapeDtypeStruct((t, D_FF), BF16)
    if target is not None:
        extra_spec, out_specs = [row], [row, pl.BlockSpec((1, LANES), lambda i, j: (0, 0)), ffb, ffb]
        out_shape = [f32_rows, jax.ShapeDtypeStruct((1, LANES), F32), bf16_ff, bf16_ff]
    elif next_gain is not None:
        extra_spec, out_specs = [vec], [row, ffb, ffb, row]
        out_shape = [f32_rows, bf16_ff, bf16_ff, jax.ShapeDtypeStruct((t, d), BF16)]
    else:
        extra_spec, out_specs, out_shape = [], [row, ffb, ffb], [f32_rows, bf16_ff, bf16_ff]
    return _call_with_exchange(
        ex, body, name, (t // tm, 2),
        [row, vec,
         pl.BlockSpec((1, d, FF_TILE), lambda i, j: (j, 0, 0)),
         pl.BlockSpec((1, d, FF_TILE), lambda i, j: (j + 2, 0, 0)),
         pl.BlockSpec((1, FF_TILE, d), lambda i, j: (j, 0, 0))] + extra_spec,
        out_specs, out_shape,
        [pltpu.VMEM((tm, d), BF16), pltpu.VMEM((tm, d), F32)], (x, g, wgu4, wgu4, wd2, *extra))


def _ffn_bwd(x, g, dy, gpre, upre, wgu4, wd2, name, ex=None):
    t, d = x.shape
    tm = _row_tile(t, 512)

    def body(dy_ref, gg_ref, uu_ref, wgu_hbm, wd_hbm, dg_ref, du_ref, act_ref, part_ref, wg_ref, wu_ref, wd_ref):
        j = pl.program_id(0)

        @pl.when(pl.program_id(1) == 0)
        def _():
            pltpu.sync_copy(wgu_hbm.at[j], wg_ref.at[0])
            pltpu.sync_copy(wgu_hbm.at[j + 2], wu_ref.at[0])
            pltpu.sync_copy(wd_hbm.at[j], wd_ref.at[0])

        gg = gg_ref[...].astype(F32)
        uu = uu_ref[...].astype(F32)
        sg = jax.nn.sigmoid(gg)
        silu = gg * sg
        act_ref[...] = (silu * uu).astype(BF16)
        dyh = (0.5 * dy_ref[...]).astype(BF16)
        dact = _dot_nt(dyh, wd_ref[0])
        du = (dact * silu).astype(BF16)
        dgt = (dact * uu * (sg * (1.0 + gg * (1.0 - sg)))).astype(BF16)
        du_ref[...] = du
        dg_ref[...] = dgt
        part_ref[0] = _dot_nt(dgt, wg_ref[0]) + _dot_nt(du, wu_ref[0])

    row = pl.BlockSpec((tm, d), lambda j, i: (i, 0))
    ffb = pl.BlockSpec((tm, FF_TILE), lambda j, i: (i, j))
    dgt, dup, act, parts, *got = _call_with_exchange(
        ex, body, name, (2, t // tm),
        [row, ffb, ffb, ANY, ANY],
        [ffb, ffb, ffb, pl.BlockSpec((1, tm, d), lambda j, i: (j, i, 0))],
        [jax.ShapeDtypeStruct((t, D_FF), BF16)] * 3 + [jax.ShapeDtypeStruct((2, t, d), F32)],
        [pltpu.VMEM((1, d, FF_TILE), BF16), pltpu.VMEM((1, d, FF_TILE), BF16), pltpu.VMEM((1, FF_TILE, d), BF16)],
        (dy, gpre, upre, wgu4, wd2))

    def norm_body(x_ref, g_ref, p_ref, dy_ref, dx_ref, dgain_ref, xn_ref):
        @pl.when(pl.program_id(0) == 0)
        def _():
            dgain_ref[...] = jnp.zeros_like(dgain_ref)

        xv = x_ref[...]
        r = _rstd(xv, d)
        xn_ref[...] = (xv * r * g_ref[...]).astype(BF16)
        dx, dgr = _rms_vjp(xv, r, g_ref[...], p_ref[0] + p_ref[1], d)
        dx_ref[...] = dy_ref[...] + dx
        dgain_ref[...] += jnp.sum(dgr, axis=0, keepdims=True)

    tn = _row_tile(t, 256)
    nrow = pl.BlockSpec((tn, d), lambda i: (i, 0))
    vec = pl.BlockSpec((1, d), lambda i: (0, 0))
    dx, dgain, xn = pl.pallas_call(
        norm_body, name=name + "_norm", grid=(t // tn,),
        in_specs=[nrow, vec, pl.BlockSpec((2, tn, d), lambda i: (0, i, 0)), nrow],
        out_specs=[nrow, vec, nrow],
        out_shape=[jax.ShapeDtypeStruct((t, d), F32), jax.ShapeDtypeStruct((1, d), F32),
                   jax.ShapeDtypeStruct((t, d), BF16)],
        compiler_params=_cparams())(x, g, parts, dy)
    return [dx, dgain, xn, dgt, dup, act] + got


def _sgu_layernorm(vpre, lg, lb):
    v = _gelu(vpre)
    mu = jnp.mean(v, axis=-1, keepdims=True)
    xc = v - mu
    rstd = lax.rsqrt(jnp.mean(xc * xc, axis=-1, keepdims=True) + EPS)
    xhat = xc * rstd
    return xhat, rstd, xhat * lg + lb


def _sgu_fwd(zuv, lg, lb, wt, bias_l, name):
    t = zuv.shape[0]
    tm = _row_tile(t, 512)

    def body(u_ref, v_ref, lg_ref, lb_ref, wt_ref, bl_ref, o_ref, vln_scr):
        _, _, vln = _sgu_layernorm(v_ref[...], lg_ref[...], lb_ref[...])
        vln_scr[...] = vln.astype(BF16)
        lo = lax.broadcasted_iota(jnp.int32, (CHUNK, LANES), 1) < 64
        for c in range(tm // CHUNK):
            rows = slice(c * CHUNK, (c + 1) * CHUNK)
            for p in range(SG_GROUPS // 2):
                cols = slice(p * LANES, (p + 1) * LANES)
                vp = vln_scr[rows, cols]
                mixed = jnp.where(lo, _dot(wt_ref[2 * p], vp), _dot(wt_ref[2 * p + 1], vp)) + bl_ref[:, cols]
                o_ref[rows, cols] = (_gelu(u_ref[rows, cols]) * mixed).astype(BF16)

    half = lambda k: pl.BlockSpec((tm, SG_WIDTH), lambda i: (i, k))
    vec = pl.BlockSpec((1, SG_WIDTH), lambda i: (0, 0))
    return pl.pallas_call(
        body, name=name, grid=(t // tm,),
        in_specs=[half(0), half(1), vec, vec,
                  pl.BlockSpec((SG_GROUPS, CHUNK, CHUNK), lambda i: (0, 0, 0)),
                  pl.BlockSpec((CHUNK, SG_WIDTH), lambda i: (0, 0))],
        out_specs=pl.BlockSpec((tm, SG_WIDTH), lambda i: (i, 0)),
        out_shape=jax.ShapeDtypeStruct((t, SG_WIDTH), BF16),
        scratch_shapes=[pltpu.VMEM((tm, SG_WIDTH), BF16)],
        compiler_params=_cparams())(zuv, zuv, lg, lb, wt, bias_l)


def _sgu_bwd(zuv, dya, lg, lb, wt, wt_t, bias_l, name):
    t = zuv.shape[0]
    tm = _row_tile(t, 256)
    nsteps = t // tm

    def body(u_ref, v_ref, dy_ref, lg_ref, lb_ref, wt_ref, wtt_ref, bl_ref,
             dz_ref, dwt_ref, dbl_ref, dlg_ref, dlb_ref, vln_scr, dvln_scr, dbacc_scr):
        step = pl.program_id(0)

        @pl.when(step == 0)
        def _():
            dwt_ref[...] = jnp.zeros_like(dwt_ref)
            dlg_ref[...] = jnp.zeros_like(dlg_ref)
            dlb_ref[...] = jnp.zeros_like(dlb_ref)
            dbl_ref[...] = jnp.zeros_like(dbl_ref)
            dbacc_scr[...] = jnp.zeros_like(dbacc_scr)

        vpre = v_ref[...]
        lgv = lg_ref[...]
        xhat, rstd, vln = _sgu_layernorm(vpre, lgv, lb_ref[...])
        vln_scr[...] = vln.astype(BF16)
        lo = lax.broadcasted_iota(jnp.int32, (CHUNK, LANES), 1) < 64
        for c in range(tm // CHUNK):
            rows = slice(c * CHUNK, (c + 1) * CHUNK)
            for p in range(SG_GROUPS // 2):
                cols = slice(p * LANES, (p + 1) * LANES)
                vp = vln_scr[rows, cols]
                mixed = jnp.where(lo, _dot(wt_ref[2 * p], vp), _dot(wt_ref[2 * p + 1], vp)) + bl_ref[:, cols]
                upre = u_ref[rows, cols]
                dyp = dy_ref[rows, cols]
                dz_ref[rows, cols] = (dyp * mixed * _gelu_grad(upre)).astype(BF16)
                dm = dyp * _gelu(upre)
                dbacc_scr[:, cols] += dm
                dlo = jnp.where(lo, dm, 0.0).astype(BF16)
                dhi = jnp.where(lo, 0.0, dm).astype(BF16)
                dvln_scr[rows, cols] = _dot(wtt_ref[2 * p], dlo) + _dot(wtt_ref[2 * p + 1], dhi)
                dwt_ref[2 * p] += _dot_nt(dlo, vp)
                dwt_ref[2 * p + 1] += _dot_nt(dhi, vp)
        dvln = dvln_scr[...]
        dlg_ref[...] += jnp.sum(dvln * xhat, axis=0, keepdims=True)
        dlb_ref[...] += jnp.sum(dvln, axis=0, keepdims=True)
        dxh = dvln * lgv
        dv = rstd * (dxh - jnp.mean(dxh, axis=-1, keepdims=True)
                     - xhat * jnp.mean(dxh * xhat, axis=-1, keepdims=True))
        dz_ref[:, SG_WIDTH:] = (dv * _gelu_grad(vpre)).astype(BF16)

        @pl.when(step == nsteps - 1)
        def _():
            rr = lax.broadcasted_iota(jnp.int32, (CHUNK, CHUNK), 0)
            cc = lax.broadcasted_iota(jnp.int32, (CHUNK, CHUNK), 1)
            tril = (cc <= rr).astype(F32)
            for gidx in range(SG_GROUPS):
                dwt_ref[gidx] = dwt_ref[gidx] * tril
            kk = lax.broadcasted_iota(jnp.int32, (SG_WIDTH, LANES), 0)
            gg = lax.broadcasted_iota(jnp.int32, (SG_WIDTH, LANES), 1)
            sel = ((kk // 64) == gg).astype(F32)
            dbl_ref[...] = jnp.dot(dbacc_scr[...], sel, preferred_element_type=F32,
                                   precision=lax.Precision.HIGHEST)

    half = lambda k: pl.BlockSpec((tm, SG_WIDTH), lambda i: (i, k))
    vec = pl.BlockSpec((1, SG_WIDTH), lambda i: (0, 0))
    wspec = pl.BlockSpec((SG_GROUPS, CHUNK, CHUNK), lambda i: (0, 0, 0))
    return pl.pallas_call(
        body, name=name, grid=(nsteps,),
        in_specs=[half(0), half(1), pl.BlockSpec((tm, SG_WIDTH), lambda i: (i, 0)), vec, vec,
                  wspec, wspec, pl.BlockSpec((CHUNK, SG_WIDTH), lambda i: (0, 0))],
        out_specs=[pl.BlockSpec((tm, 2 * SG_WIDTH), lambda i: (i, 0)), wspec,
                   pl.BlockSpec((CHUNK, LANES), lambda i: (0, 0)), vec, vec],
        out_shape=[jax.ShapeDtypeStruct((t, 2 * SG_WIDTH), BF16),
                   jax.ShapeDtypeStruct((SG_GROUPS, CHUNK, CHUNK), F32),
                   jax.ShapeDtypeStruct((CHUNK, LANES), F32),
                   jax.ShapeDtypeStruct((1, SG_WIDTH), F32), jax.ShapeDtypeStruct((1, SG_WIDTH), F32)],
        scratch_shapes=[pltpu.VMEM((tm, SG_WIDTH), BF16), pltpu.VMEM((tm, SG_WIDTH), F32),
                        pltpu.VMEM((CHUNK, SG_WIDTH), F32)],
        compiler_params=_cparams())(zuv, zuv, dya, lg, lb, wt, wt_t, bias_l)


def _rope(x, c, s1, s2):
    return x * c + pltpu.roll(x, LANES - 16, 1) * s1 + pltpu.roll(x, 16, 1) * s2


def _rope_t(dy, c, s1, s2):
    return dy * c + pltpu.roll(dy * s1, 16, 1) + pltpu.roll(dy * s2, LANES - 16, 1)


def _mla_prep_fwd(zcq, zckv, zkr, gcq, gckv, qg, kg, wuq, wuk, wuv, rc, rs1, rs2, name, ex=None):
    t = zcq.shape[0]
    tm = _row_tile(t, 256)
    hd = MLA_HEADS * LANES

    def body(zcq_ref, zckv_ref, zkr_ref, gcq_ref, gckv_ref, qg_ref, kg_ref, wuq_ref, wuk_ref, wuv_ref,
             c_ref, s1_ref, s2_ref, q_ref, k_ref, v_ref, cqn_ref, ckvn_ref):
        c, s1, s2 = c_ref[...], s1_ref[...], s2_ref[...]
        xq = zcq_ref[...]
        cqn = (xq * _rstd(xq, MLA_Q_RANK) * gcq_ref[...]).astype(BF16)
        cqn_ref[...] = cqn
        ql = _dot(cqn, wuq_ref[...])
        xk = zckv_ref[...]
        ckvn = (xk * _rstd(xk, MLA_KV_RANK) * gckv_ref[...]).astype(BF16)
        ckvn_ref[...] = ckvn
        kl = _dot(ckvn, wuk_ref[...])
        slot_lane = lax.broadcasted_iota(jnp.int32, (tm, hd), 1) % LANES
        v_ref[...] = jnp.where(slot_lane == V_ONES_LANE, 1.0, _dot(ckvn, wuv_ref[...])).astype(BF16)
        kr = zkr_ref[...]
        for h in range(MLA_HEADS):
            sl = slice(h * LANES, (h + 1) * LANES)
            qh = ql[:, sl]
            q_ref[:, sl] = (_rope(qh * _rstd(qh, MLA_QK) * qg_ref[...], c, s1, s2) * ATTN_SCALE2).astype(BF16)
            kh = kl[:, sl] + kr
            k_ref[:, sl] = _rope(kh * _rstd(kh, MLA_QK) * kg_ref[...], c, s1, s2).astype(BF16)

    row = lambda n: pl.BlockSpec((tm, n), lambda i: (i, 0))
    full = lambda a: pl.BlockSpec(a.shape, lambda i: (0, 0))
    return _call_with_exchange(
        ex, body, name, (t // tm,),
        [row(MLA_Q_RANK), row(MLA_KV_RANK), row(LANES), full(gcq), full(gckv), full(qg), full(kg),
         full(wuq), full(wuk), full(wuv), row(LANES), row(LANES), row(LANES)],
        [row(hd), row(hd), row(hd), row(MLA_Q_RANK), row(MLA_KV_RANK)],
        [jax.ShapeDtypeStruct((t, hd), BF16)] * 3
        + [jax.ShapeDtypeStruct((t, MLA_Q_RANK), BF16), jax.ShapeDtypeStruct((t, MLA_KV_RANK), BF16)],
        [], (zcq, zckv, zkr, gcq, gckv, qg, kg, wuq, wuk, wuv, rc, rs1, rs2))


def _mla_prep_bwd(zcq, zckv, zkr, gcq, gckv, qg, kg, wuq, wuk, wuv, rc, rs1, rs2, dq, dk, dv, name):
    t = zcq.shape[0]
    tm = _row_tile(t, 256)
    hd = MLA_HEADS * LANES

    def body(zcq_ref, zckv_ref, zkr_ref, gcq_ref, gckv_ref, qg_ref, kg_ref, wuq_ref, wuk_ref, wuv_ref,
             c_ref, s1_ref, s2_ref, dq_ref, dk_ref, dv_ref,
             dzcq_ref, dzckv_ref, dzkr_ref, dql_ref, dkl_ref, dgcq_ref, dgckv_ref, dqg_ref, dkg_ref):
        @pl.when(pl.program_id(0) == 0)
        def _():
            for ref in (dgcq_ref, dgckv_ref, dqg_ref, dkg_ref):
                ref[...] = jnp.zeros_like(ref)

        c, s1, s2 = c_ref[...], s1_ref[...], s2_ref[...]
        qgv, kgv = qg_ref[...], kg_ref[...]
        xq = zcq_ref[...]
        rq = _rstd(xq, MLA_Q_RANK)
        ql = _dot((xq * rq * gcq_ref[...]).astype(BF16), wuq_ref[...])
        xk = zckv_ref[...]
        rk = _rstd(xk, MLA_KV_RANK)
        kl = _dot((xk * rk * gckv_ref[...]).astype(BF16), wuk_ref[...])
        kr = zkr_ref[...]
        dqg_acc = jnp.zeros((tm, LANES), F32)
        dkg_acc = jnp.zeros((tm, LANES), F32)
        dkr = jnp.zeros((tm, LANES), F32)
        for h in range(MLA_HEADS):
            sl = slice(h * LANES, (h + 1) * LANES)
            qh = ql[:, sl]
            dqh, dgr = _rms_vjp(qh, _rstd(qh, MLA_QK), qgv, _rope_t(dq_ref[:, sl], c, s1, s2), MLA_QK)
            dql_ref[:, sl] = dqh.astype(BF16)
            dqg_acc += dgr
            kh = kl[:, sl] + kr
            dkh, dgr = _rms_vjp(kh, _rstd(kh, MLA_QK), kgv, _rope_t(dk_ref[:, sl], c, s1, s2), MLA_QK)
            dkl_ref[:, sl] = dkh.astype(BF16)
            dkg_acc += dgr
            dkr += dkh
        dqg_ref[...] += jnp.sum(dqg_acc, axis=0, keepdims=True)
        dkg_ref[...] += jnp.sum(dkg_acc, axis=0, keepdims=True)
        lane = lax.broadcasted_iota(jnp.int32, (tm, LANES), 1)
        dzkr_ref[...] = jnp.where((lane >= MLA_NOPE) & (lane < MLA_QK), dkr, 0.0).astype(BF16)
        dcqn = _dot_nt(dql_ref[...], wuq_ref[...])
        dx, dgr = _rms_vjp(xq, rq, gcq_ref[...], dcqn, MLA_Q_RANK)
        dzcq_ref[...] = dx.astype(BF16)
        dgcq_ref[...] += jnp.sum(dgr, axis=0, keepdims=True)
        dckvn = _dot_nt(dkl_ref[...], wuk_ref[...]) + _dot_nt(dv_ref[...].astype(BF16), wuv_ref[...])
        dx, dgr = _rms_vjp(xk, rk, gckv_ref[...], dckvn, MLA_KV_RANK)
        dzckv_ref[...] = dx.astype(BF16)
        dgckv_ref[...] += jnp.sum(dgr, axis=0, keepdims=True)

    row = lambda n: pl.BlockSpec((tm, n), lambda i: (i, 0))
    full = lambda a: pl.BlockSpec(a.shape, lambda i: (0, 0))
    vec = lambda n: pl.BlockSpec((1, n), lambda i: (0, 0))
    return pl.pallas_call(
        body, name=name, grid=(t // tm,),
        in_specs=[row(MLA_Q_RANK), row(MLA_KV_RANK), row(LANES), full(gcq), full(gckv), full(qg), full(kg),
                  full(wuq), full(wuk), full(wuv), row(LANES), row(LANES), row(LANES), row(hd), row(hd), row(hd)],
        out_specs=[row(MLA_Q_RANK), row(MLA_KV_RANK), row(LANES), row(hd), row(hd),
                   vec(MLA_Q_RANK), vec(MLA_KV_RANK), vec(LANES), vec(LANES)],
        out_shape=[jax.ShapeDtypeStruct((t, MLA_Q_RANK), BF16), jax.ShapeDtypeStruct((t, MLA_KV_RANK), BF16),
                   jax.ShapeDtypeStruct((t, LANES), BF16), jax.ShapeDtypeStruct((t, hd), BF16),
                   jax.ShapeDtypeStruct((t, hd), BF16), jax.ShapeDtypeStruct((1, MLA_Q_RANK), F32),
                   jax.ShapeDtypeStruct((1, MLA_KV_RANK), F32), jax.ShapeDtypeStruct((1, LANES), F32),
                   jax.ShapeDtypeStruct((1, LANES), F32)],
        compiler_params=_cparams(),
    )(zcq, zckv, zkr, gcq, gckv, qg, kg, wuq, wuk, wuv, rc, rs1, rs2, dq, dk, dv)


def _attn_tiles(t):
    tq = 512 if t >= 2048 else 128
    return tq, min(t, 4 * tq), min(t, 2 * tq)


def _causal_keep(tq, nk, i, j, tk):
    row = lax.broadcasted_iota(jnp.int32, (tq, nk), 0)
    col = lax.broadcasted_iota(jnp.int32, (tq, nk), 1)
    return (col - row) <= (i * tq - j * tk)


def _causal_keep_t(tq, nk, i, j, tk):
    key = lax.broadcasted_iota(jnp.int32, (nk, tq), 0)
    qry = lax.broadcasted_iota(jnp.int32, (nk, tq), 1)
    return (key - qry) <= (i * tq - j * tk)


ATTN_FWD_HEADS_PER_STEP = 2
ATTN_BWD_HEADS_PER_STEP = 2


def _attn_fwd(q, k, v, name, ex=None):
    t, hd = q.shape
    hp = ATTN_FWD_HEADS_PER_STEP
    tq, tk, _ = _attn_tiles(t)
    pairs = [(i, j) for i in range(t // tq) for j in range(((i + 1) * tq - 1) // tk + 1)]
    ii = np.array([p[0] for p in pairs], np.int32)
    jj = np.array([p[1] for p in pairs], np.int32)

    def body(ii_ref, jj_ref, q_ref, k_ref, v_ref, o_ref, lse_ref, m_scr, acc_scr):
        s_id = pl.program_id(1)
        i, j = ii_ref[s_id], jj_ref[s_id]
        last = j == ((i + 1) * tq - 1) // tk
        ones_lane = lax.broadcasted_iota(jnp.int32, (tq, LANES), 1) == V_ONES_LANE

        @pl.when(j == 0)
        def _():
            m_scr[...] = jnp.full_like(m_scr, NEG)
            acc_scr[...] = jnp.zeros_like(acc_scr)

        def step(masked, nk):
            for hh in range(hp):
                sl = slice(hh * LANES, (hh + 1) * LANES)
                s = _dot_nt(q_ref[:, sl], k_ref[:nk, sl])
                if masked:
                    s = jnp.where(_causal_keep(tq, nk, i, j, tk), s, NEG)
                m_prev = m_scr[hh]
                m_new = jnp.maximum(m_prev, jnp.max(s, axis=1, keepdims=True))
                p = jnp.exp2(s - m_new)
                alpha = jnp.exp2(m_prev - m_new)
                acc = alpha * acc_scr[:, sl] + _dot(p.astype(BF16), v_ref[:nk, sl])
                if masked:
                    l_new = jnp.sum(jnp.where(ones_lane, acc, 0.0), axis=1, keepdims=True)
                    o_ref[:, sl] = (acc / l_new).astype(BF16)
                    lse_ref[:, sl] = jnp.broadcast_to(m_new + jnp.log(l_new) * LOG2E, (tq, LANES))
                else:
                    acc_scr[:, sl] = acc
                    m_scr[hh] = m_new

        @pl.when(jnp.logical_not(last))
        def _():
            step(False, tk)

        r = (((i + 1) * tq - 1) % tk) // tq
        for rr in range(tk // tq):
            @pl.when(last & (r == rr))
            def _():
                step(True, (rr + 1) * tq)

    w = hp * LANES
    qspec = pl.BlockSpec((tq, w), lambda h, s, ii_r, jj_r: (ii_r[s], h))
    kspec = pl.BlockSpec((tk, w), lambda h, s, ii_r, jj_r: (jj_r[s], h))
    return _call_with_exchange(
        ex, body, name, (hd // w, len(pairs)), [qspec, kspec, kspec], [qspec, qspec],
        [jax.ShapeDtypeStruct((t, hd), BF16), jax.ShapeDtypeStruct((t, hd), F32)],
        [pltpu.VMEM((hp, tq, 1), F32), pltpu.VMEM((tq, w), F32)], (q, k, v),
        prefetch=(jnp.asarray(ii), jnp.asarray(jj)))


def _attn_bwd_rows(o, lse, do, name):
    t, hd = o.shape
    heads = hd // LANES
    tm = _row_tile(t, 512)

    def body(o_ref, lse_ref, do_ref, out_ref):
        lane = lax.broadcasted_iota(jnp.int32, (tm, LANES), 1)
        acc = jnp.zeros((tm, LANES), F32)
        for h in range(heads):
            sl = slice(h * LANES, (h + 1) * LANES)
            delta = jnp.sum(do_ref[:, sl].astype(F32) * o_ref[:, sl].astype(F32), axis=1, keepdims=True)
            acc = jnp.where(lane == h, delta, acc)
            acc = jnp.where(lane == heads + h, lse_ref[:, sl], acc)
        out_ref[...] = acc

    row = pl.BlockSpec((tm, hd), lambda i: (i, 0))
    cols = pl.pallas_call(
        body, name=name, grid=(t // tm,), in_specs=[row, row, row],
        out_specs=pl.BlockSpec((tm, LANES), lambda i: (i, 0)),
        out_shape=jax.ShapeDtypeStruct((t, LANES), F32), compiler_params=_cparams())(o, lse, do)
    rows = cols.T
    return rows[:heads].reshape(heads, 1, t), rows[heads:2 * heads].reshape(heads, 1, t)


def _attn_bwd(q, k, v, delta_rows, lse_rows, do, name):
    t, hd = q.shape
    hp = ATTN_BWD_HEADS_PER_STEP
    tq, _, tk = _attn_tiles(t)
    nq = t // tq
    pairs = [(i, j) for j in range(t // tk) for i in range((j * tk) // tq, nq)]
    ii = np.array([p[0] for p in pairs], np.int32)
    jj = np.array([p[1] for p in pairs], np.int32)

    def body(jj_ref, ii_ref, q_ref, k_ref, v_ref, delta_ref, lse_ref, do_ref, dq_ref, dk_ref, dv_ref,
             dk_scr, dv_scr):
        s_id = pl.program_id(1)
        i, j = ii_ref[s_id], jj_ref[s_id]

        @pl.when(s_id == 0)
        def _():
            dq_ref[...] = jnp.zeros_like(dq_ref)

        @pl.when(i == (j * tk) // tq)
        def _():
            dk_scr[...] = jnp.zeros_like(dk_scr)
            dv_scr[...] = jnp.zeros_like(dv_scr)

        rows = pl.ds(pl.multiple_of(i * tq, tq), tq)

        def step(masked, nk):
            for hh in range(hp):
                sl = slice(hh * LANES, (hh + 1) * LANES)
                qv, kv, dov = q_ref[:, sl], k_ref[:nk, sl], do_ref[:, sl]
                st = _dot_nt(kv, qv)
                if masked:
                    st = jnp.where(_causal_keep_t(tq, nk, i, j, tk), st, NEG)
                pt = jnp.exp2(st - lse_ref[hh])
                dv_scr[:nk, sl] += _dot(pt.astype(BF16), dov)
                dpt = _dot_nt(v_ref[:nk, sl], dov)
                dst = (pt * (dpt - delta_ref[hh]) * ATTN_SCALE).astype(BF16)
                dk_scr[:nk, sl] += _dot(dst, qv)
                dq_ref[rows, sl] += _dot_tn(dst, kv)

        seen = jnp.minimum((i + 1) * tq - j * tk, tk)
        for nk in range(tq, tk + 1, tq):
            @pl.when((seen == nk) & ((i + 1) * tq - j * tk <= tk))
            def _():
                step(True, nk)

        @pl.when((i + 1) * tq - j * tk > tk)
        def _():
            step(False, tk)

        @pl.when(i == nq - 1)
        def _():
            dk_ref[...] = dk_scr[...] * (1.0 / ATTN_SCALE2)
            dv_ref[...] = dv_scr[...]

    w = hp * LANES
    qspec = pl.BlockSpec((tq, w), lambda h, s, jj_r, ii_r: (ii_r[s], h))
    kspec = pl.BlockSpec((tk, w), lambda h, s, jj_r, ii_r: (jj_r[s], h))
    rspec = pl.BlockSpec((hp, 1, tq), lambda h, s, jj_r, ii_r: (h, 0, ii_r[s]))
    return pl.pallas_call(
        body, name=name,
        grid_spec=pltpu.PrefetchScalarGridSpec(
            num_scalar_prefetch=2, grid=(hd // w, len(pairs)),
            in_specs=[qspec, kspec, kspec, rspec, rspec, qspec],
            out_specs=[pl.BlockSpec((t, w), lambda h, s, jj_r, ii_r: (0, h)), kspec, kspec],
            scratch_shapes=[pltpu.VMEM((tk, w), F32), pltpu.VMEM((tk, w), F32)]),
        out_shape=[jax.ShapeDtypeStruct((t, hd), F32)] * 3,
        compiler_params=_cparams())(jnp.asarray(jj), jnp.asarray(ii), q, k, v, delta_rows, lse_rows, do)


MEM_W = MEM_HEADS * LANES


def _mem_kv_fwd(mem, gmem, wkv, kg, name):
    m, d = mem.shape

    def body(mem_ref, g_ref, w_ref, kg_ref, k_ref, v_ref, mn_ref):
        xv = mem_ref[...]
        mn = (xv * _rstd(xv, d) * g_ref[...]).astype(BF16)
        mn_ref[...] = mn
        kvm = _dot(mn, w_ref[...])
        v_ref[...] = kvm[:, MEM_W:].astype(BF16)
        for h in range(MEM_HEADS):
            sl = slice(h * LANES, (h + 1) * LANES)
            kh = kvm[:, sl]
            k_ref[:, sl] = (kh * _rstd(kh, LANES) * kg_ref[...]).astype(BF16)

    full = lambda a: pl.BlockSpec(a.shape, lambda i: (0, 0))
    return pl.pallas_call(
        body, name=name, grid=(1,), in_specs=[full(mem), full(gmem), full(wkv), full(kg)],
        out_specs=[pl.BlockSpec((m, MEM_W), lambda i: (0, 0)), pl.BlockSpec((m, MEM_W), lambda i: (0, 0)),
                   pl.BlockSpec((m, d), lambda i: (0, 0))],
        out_shape=[jax.ShapeDtypeStruct((m, MEM_W), BF16), jax.ShapeDtypeStruct((m, MEM_W), BF16),
                   jax.ShapeDtypeStruct((m, d), BF16)],
        compiler_params=_cparams())(mem, gmem, wkv, kg)


def _mem_softmax(qn, kh):
    s = _dot_nt(qn, kh) * (LANES ** -0.5)
    e = jnp.exp(s - jnp.max(s, axis=1, keepdims=True))
    return e / jnp.sum(e, axis=1, keepdims=True)


def _mem_attn_fwd(zqm, qg, km, vm, name):
    t = zqm.shape[0]
    tm = _row_tile(t, 512)

    def body(q_ref, qg_ref, k_ref, v_ref, o_ref):
        for h in range(MEM_HEADS):
            sl = slice(h * LANES, (h + 1) * LANES)
            qh = q_ref[:, sl]
            qn = (qh * _rstd(qh, LANES) * qg_ref[...]).astype(BF16)
            p = _mem_softmax(qn, k_ref[:, sl])
            o_ref[:, sl] = _dot(p.astype(BF16), v_ref[:, sl]).astype(BF16)

    row = pl.BlockSpec((tm, MEM_W), lambda i: (i, 0))
    full = lambda a: pl.BlockSpec(a.shape, lambda i: (0, 0))
    return pl.pallas_call(
        body, name=name, grid=(t // tm,), in_specs=[row, full(qg), full(km), full(vm)], out_specs=row,
        out_shape=jax.ShapeDtypeStruct((t, MEM_W), BF16), compiler_params=_cparams())(zqm, qg, km, vm)


def _mem_attn_bwd(zqm, dyc, qg, km, vm, name):
    t = zqm.shape[0]
    m = km.shape[0]
    tm = _row_tile(t, 256)

    def body(q_ref, dy_ref, qg_ref, k_ref, v_ref, dz_ref, dk_ref, dv_ref, dqg_ref):
        @pl.when(pl.program_id(0) == 0)
        def _():
            dk_ref[...] = jnp.zeros_like(dk_ref)
            dv_ref[...] = jnp.zeros_like(dv_ref)
            dqg_ref[...] = jnp.zeros_like(dqg_ref)

        qgv = qg_ref[...]
        dqg_acc = jnp.zeros((tm, LANES), F32)
        for h in range(MEM_HEADS):
            sl = slice(h * LANES, (h + 1) * LANES)
            qh = q_ref[:, sl]
            r = _rstd(qh, LANES)
            qn = (qh * r * qgv).astype(BF16)
            kh = k_ref[:, sl]
            p = _mem_softmax(qn, kh)
            dov = dy_ref[:, sl]
            dv_ref[:, sl] += _dot_tn(p.astype(BF16), dov)
            dp = _dot_nt(dov, v_ref[:, sl])
            ds = (p * (dp - jnp.sum(dp * p, axis=1, keepdims=True)) * (LANES ** -0.5)).astype(BF16)
            dk_ref[:, sl] += _dot_tn(ds, qn)
            dqh, dgr = _rms_vjp(qh, r, qgv, _dot(ds, kh), LANES)
            dz_ref[:, sl] = dqh.astype(BF16)
            dqg_acc += dgr
        dqg_ref[...] += jnp.sum(dqg_acc, axis=0, keepdims=True)

    row = pl.BlockSpec((tm, MEM_W), lambda i: (i, 0))
    full = lambda a: pl.BlockSpec(a.shape, lambda i: (0, 0))
    acc = pl.BlockSpec((m, MEM_W), lambda i: (0, 0))
    return pl.pallas_call(
        body, name=name, grid=(t // tm,), in_specs=[row, row, full(qg), full(km), full(vm)],
        out_specs=[row, acc, acc, pl.BlockSpec((1, LANES), lambda i: (0, 0))],
        out_shape=[jax.ShapeDtypeStruct((t, MEM_W), BF16), jax.ShapeDtypeStruct((m, MEM_W), F32),
                   jax.ShapeDtypeStruct((m, MEM_W), F32), jax.ShapeDtypeStruct((1, LANES), F32)],
        compiler_params=_cparams())(zqm, dyc, qg, km, vm)


def _mem_kv_bwd(mem, gmem, wkv, kg, dkn, dvm, name):
    m, d = mem.shape

    def body(mem_ref, g_ref, w_ref, kg_ref, dk_ref, dv_ref, dw_ref, dkg_ref, dg_ref, dkv_scr):
        xv = mem_ref[...]
        r = _rstd(xv, d)
        mn = (xv * r * g_ref[...]).astype(BF16)
        kvm = _dot(mn, w_ref[...])
        dkv_scr[:, MEM_W:] = dv_ref[...].astype(BF16)
        dkg_acc = jnp.zeros((m, LANES), F32)
        for h in range(MEM_HEADS):
            sl = slice(h * LANES, (h + 1) * LANES)
            kh = kvm[:, sl]
            dkh, dgr = _rms_vjp(kh, _rstd(kh, LANES), kg_ref[...], dk_ref[:, sl], LANES)
            dkv_scr[:, sl] = dkh.astype(BF16)
            dkg_acc += dgr
        dkg_ref[...] = jnp.sum(dkg_acc, axis=0, keepdims=True)
        dkv = dkv_scr[...]
        dw_ref[...] = _dot_tn(mn, dkv)
        dmn = _dot_nt(dkv, w_ref[...])
        dg_ref[...] = jnp.sum(dmn * xv * r, axis=0, keepdims=True)

    full = lambda a: pl.BlockSpec(a.shape, lambda i: (0, 0))
    return pl.pallas_call(
        body, name=name, grid=(1,),
        in_specs=[full(mem), full(gmem), full(wkv), full(kg), full(dkn), full(dvm)],
        out_specs=[pl.BlockSpec((d, 2 * MEM_W), lambda i: (0, 0)), pl.BlockSpec((1, LANES), lambda i: (0, 0)),
                   pl.BlockSpec((1, d), lambda i: (0, 0))],
        out_shape=[jax.ShapeDtypeStruct((d, 2 * MEM_W), F32), jax.ShapeDtypeStruct((1, LANES), F32),
                   jax.ShapeDtypeStruct((1, d), F32)],
        scratch_shapes=[pltpu.VMEM((m, 2 * MEM_W), BF16)],
        compiler_params=_cparams())(mem, gmem, wkv, kg, dkn, dvm)


def _merge_fwd(x1, ya, yb, yc, zg, bg, wa, wb, wc, wo, name):
    t, d = x1.shape
    tm = _row_tile(t, 256)

    def body(x_ref, ya_ref, yb_ref, yc_ref, zg_ref, bg_ref, wa_ref, wb_ref, wc_ref, wo_ref,
             x2_ref, mg_ref, pa_ref, pb_ref, pc_ref):
        merged = None
        for k, (y_ref, w_ref, p_ref) in enumerate(
                ((ya_ref, wa_ref, pa_ref), (yb_ref, wb_ref, pb_ref), (yc_ref, wc_ref, pc_ref))):
            sl = slice(k * d, (k + 1) * d)
            pr = _dot(y_ref[...], w_ref[...])
            p_ref[...] = pr.astype(BF16)
            term = jax.nn.sigmoid(zg_ref[:, sl] + bg_ref[:, sl]) * pr
            merged = term if merged is None else merged + term
        mb = merged.astype(BF16)
        mg_ref[...] = mb
        x2_ref[...] = x_ref[...] + _dot(mb, wo_ref[...])

    row = lambda n: pl.BlockSpec((tm, n), lambda i: (i, 0))
    full = lambda a: pl.BlockSpec(a.shape, lambda i: (0, 0))
    return pl.pallas_call(
        body, name=name, grid=(t // tm,),
        in_specs=[row(d), row(ya.shape[1]), row(yb.shape[1]), row(yc.shape[1]), row(3 * d), full(bg),
                  full(wa), full(wb), full(wc), full(wo)],
        out_specs=[row(d)] * 5,
        out_shape=[jax.ShapeDtypeStruct((t, d), F32)] + [jax.ShapeDtypeStruct((t, d), BF16)] * 4,
        compiler_params=_cparams())(x1, ya, yb, yc, zg, bg, wa, wb, wc, wo)


def _merge_bwd(dx2, pa, pb, pc, zg, bg, wa, wb, wc, wo, name, ex=None):
    t, d = dx2.shape
    tm = _row_tile(t, 256)

    def body(dx_ref, pa_ref, pb_ref, pc_ref, zg_ref, bg_ref, wa_ref, wb_ref, wc_ref, wo_ref,
             dpa_ref, dpb_ref, dpc_ref, dzg_ref, dbg_ref, dya_ref, dyb_ref, dyc_ref):
        @pl.when(pl.program_id(0) == 0)
        def _():
            dbg_ref[...] = jnp.zeros_like(dbg_ref)

        dm = _dot_nt(dx_ref[...].astype(BF16), wo_ref[...])
        for k, (p_ref, w_ref, dp_ref, dy_ref) in enumerate(
                ((pa_ref, wa_ref, dpa_ref, dya_ref), (pb_ref, wb_ref, dpb_ref, dyb_ref),
                 (pc_ref, wc_ref, dpc_ref, dyc_ref))):
            sl = slice(k * d, (k + 1) * d)
            gate = jax.nn.sigmoid(zg_ref[:, sl] + bg_ref[:, sl])
            dpr = (dm * gate).astype(BF16)
            dp_ref[...] = dpr
            dzg = dm * p_ref[...].astype(F32) * gate * (1.0 - gate)
            dzg_ref[:, sl] = dzg.astype(BF16)
            dbg_ref[:, sl] += jnp.sum(dzg, axis=0, keepdims=True)
            dy_ref[...] = _dot_nt(dpr, w_ref[...]).astype(dy_ref.dtype)

    row = lambda n: pl.BlockSpec((tm, n), lambda i: (i, 0))
    full = lambda a: pl.BlockSpec(a.shape, lambda i: (0, 0))
    na, nb, nc = wa.shape[0], wb.shape[0], wc.shape[0]
    return _call_with_exchange(
        ex, body, name, (t // tm,),
        [row(d), row(d), row(d), row(d), row(3 * d), full(bg), full(wa), full(wb), full(wc), full(wo)],
        [row(d), row(d), row(d), row(3 * d), pl.BlockSpec((1, 3 * d), lambda i: (0, 0)), row(na), row(nb), row(nc)],
        [jax.ShapeDtypeStruct((t, d), BF16)] * 3
        + [jax.ShapeDtypeStruct((t, 3 * d), BF16), jax.ShapeDtypeStruct((1, 3 * d), F32),
           jax.ShapeDtypeStruct((t, na), F32), jax.ShapeDtypeStruct((t, nb), BF16),
           jax.ShapeDtypeStruct((t, nc), BF16)],
        [], (dx2, pa, pb, pc, zg, bg, wa, wb, wc, wo))


def _adamw_math(w, g, m, v):
    bc1 = 1.0 - ADAM_B1 ** ADAM_STEP
    bc2 = 1.0 - ADAM_B2 ** ADAM_STEP
    nm = ADAM_B1 * m + (1.0 - ADAM_B1) * g
    nv = ADAM_B2 * v + (1.0 - ADAM_B2) * (g * g)
    delta = -ADAM_LR * ((nm / bc1) / (jnp.sqrt(nv / bc2) + ADAM_EPS) + ADAM_WD * w)
    return delta, nm, nv


def _div_tile(n, cap, mult):
    best = None
    for cand in range(mult, min(n, cap) + 1, mult):
        if n % cand == 0:
            best = cand
    assert best is not None, (n, cap, mult)
    return best


def _adamw(w, g, m, v, name):
    rows, cols = w.shape
    tr = rows if rows * cols <= 256 * 1024 else _div_tile(rows, 256, 8)

    def body(w_ref, g_ref, m_ref, v_ref, d_ref, nm_ref, nv_ref):
        d_ref[...], nm_ref[...], nv_ref[...] = _adamw_math(w_ref[...], g_ref[...], m_ref[...], v_ref[...])

    blk = pl.BlockSpec((tr, cols), lambda i: (i, 0))
    return pl.pallas_call(
        body, name=name, grid=(rows // tr,), in_specs=[blk] * 4, out_specs=[blk] * 3,
        out_shape=[jax.ShapeDtypeStruct((rows, cols), F32)] * 3, compiler_params=_cparams())(w, g, m, v)


def _adamw_slots(w, slots, m, v, name):
    _, hr, cols = w.shape
    tr = _div_tile(hr, 128, 16)

    def body(w_ref, s_ref, m_ref, v_ref, g_ref, d_ref, nm_ref, nv_ref):
        g = s_ref[0, 0].astype(F32)
        for k in range(1, N_CHIPS):
            g = g + s_ref[0, k].astype(F32)
        g_ref[0] = g
        d_ref[0], nm_ref[0], nv_ref[0] = _adamw_math(w_ref[0], g, m_ref[0], v_ref[0])

    blk = pl.BlockSpec((1, tr, cols), lambda h, i: (h, i, 0))
    return pl.pallas_call(
        body, name=name, grid=(2, hr // tr),
        in_specs=[blk, pl.BlockSpec((1, N_CHIPS, tr, cols), lambda h, i: (h, 0, i, 0)), blk, blk],
        out_specs=[blk] * 4, out_shape=[jax.ShapeDtypeStruct((2, hr, cols), F32)] * 4,
        compiler_params=_cparams())(w, slots, m, v)


ANY = pl.BlockSpec(memory_space=pl.ANY)


def _place():
    x, y, c = lax.axis_index("x"), lax.axis_index("y"), lax.axis_index("c")
    other_chips = [(1 - x, y), (x, 1 - y), (1 - x, 1 - y)]
    return x, y, c, other_chips


def _remote(src, dst, send_sem, recv_sem, to):
    return pltpu.make_async_remote_copy(src_ref=src, dst_ref=dst, send_sem=send_sem, recv_sem=recv_sem,
                                        device_id=to, device_id_type=MESH)


PIECE_BYTES = 384 * 1024


def _row_pieces(half_rows, cols):
    for n in (4, 2):
        if half_rows % (16 * n) == 0 and half_rows * cols * 2 // n >= PIECE_BYTES:
            return [pl.ds(k * (half_rows // n), half_rows // n) for k in range(n)]
    return [pl.ds(0, half_rows)]


def _pieces(arrays, rows_axis):
    return [(w, rows) for w, a in enumerate(arrays) for rows in _row_pieces(a.shape[rows_axis], a.shape[-1])]


def _gather_exchange(shards):
    nw = len(shards)
    pieces = _pieces(shards, 1)
    npc = len(pieces)

    def build(s_refs, g_refs, sems):
        send_sems, recv_sems, local_sems = sems
        x, y, c, chips = _place()
        me = 2 * x + y
        sibling = (x, y, 1 - c)
        mine = [pltpu.make_async_copy(s_refs[w], g_refs[w].at[me], local_sems.at[w]) for w in range(nw)]
        first = [_remote(s_refs[w].at[c, rows], g_refs[w].at[me, c, rows], send_sems.at[k, p], recv_sems.at[k, p],
                         (cx, cy, c)) for k, (cx, cy) in enumerate(chips) for p, (w, rows) in enumerate(pieces)]

        def start():
            for cp in mine + first:
                cp.start()

        arrived = [g_refs[w].at[2 * cx + cy, c, rows] for cx, cy in chips for w, rows in pieces]
        passed = [_remote(slab, slab, send_sems.at[3 + q // npc, q % npc], recv_sems.at[3 + q // npc, q % npc], sibling)
                  for q, slab in enumerate(arrived)]

        def pass_on():
            for q, slab in enumerate(arrived):
                k, p = q // npc, q % npc
                _remote(slab, slab, send_sems.at[k, p], recv_sems.at[k, p], (*chips[k], c)).wait_recv()
                passed[q].start()

        def finish():
            for k, (cx, cy) in enumerate(chips):
                for p, (w, rows) in enumerate(pieces):
                    slab = g_refs[w].at[2 * cx + cy, 1 - c, rows]
                    _remote(slab, slab, send_sems.at[3 + k, p], recv_sems.at[3 + k, p], sibling).wait_recv()
            for cp in first + passed:
                cp.wait_send()
            for cp in mine:
                cp.wait()

        return start, pass_on, finish

    return _Exchange(list(shards), [jax.ShapeDtypeStruct((N_CHIPS,) + s.shape, BF16) for s in shards],
                     [pltpu.SemaphoreType.DMA((6, npc)), pltpu.SemaphoreType.DMA((6, npc)),
                      pltpu.SemaphoreType.DMA((nw,))], build)


def _swap_halves(grads, name):
    nw = len(grads)

    def body(*refs):
        g_refs, sib_refs = refs[:nw], refs[nw:2 * nw]
        send_sems, recv_sems = refs[2 * nw:]
        x, y, c, _ = _place()
        copies = [_remote(g_refs[w].at[s, 1 - c], sib_refs[w].at[s], send_sems.at[s, w], recv_sems.at[s, w],
                          (x, y, 1 - c)) for w in range(nw) for s in range(N_CHIPS)]
        for cp in copies:
            cp.start()
        for cp in copies:
            cp.wait_recv()
        for cp in copies:
            cp.wait_send()

    return pl.pallas_call(
        body, name=name, in_specs=[ANY] * nw, out_specs=[ANY] * nw,
        out_shape=[jax.ShapeDtypeStruct((N_CHIPS,) + g.shape[2:], BF16) for g in grads],
        scratch_shapes=[pltpu.SemaphoreType.DMA((N_CHIPS, nw)), pltpu.SemaphoreType.DMA((N_CHIPS, nw))],
    )(*grads)


def _pair_sum(grad, sib, core, name):
    nchip, _, hr, cols = grad.shape
    tr = _div_tile(hr, 256, 16)

    def body(core_ref, a_ref, b_ref, o_ref):
        o_ref[...] = (a_ref[0].astype(F32) + b_ref[...].astype(F32)).astype(BF16)

    return pl.pallas_call(
        body, name=name,
        grid_spec=pltpu.PrefetchScalarGridSpec(
            num_scalar_prefetch=1, grid=(nchip, hr // tr),
            in_specs=[pl.BlockSpec((1, 1, tr, cols), lambda s, i, core_r: (s, core_r[0], i, 0)),
                      pl.BlockSpec((1, tr, cols), lambda s, i, core_r: (s, i, 0))],
            out_specs=pl.BlockSpec((1, tr, cols), lambda s, i, core_r: (s, i, 0))),
        out_shape=jax.ShapeDtypeStruct((nchip, hr, cols), BF16), compiler_params=_cparams())(core, grad, sib)


def _pair_sum_exchange(sums):
    nw = len(sums)
    pieces = _pieces(sums, 1)
    npc = len(pieces)

    def build(p_refs, o_refs, sems):
        send_sems, recv_sems, local_sems = sems
        x, y, c, chips = _place()
        me = 2 * x + y
        sibling = (x, y, 1 - c)
        mine = [pltpu.make_async_copy(p_refs[w].at[me], o_refs[w].at[c, 3], local_sems.at[w]) for w in range(nw)]
        first = [_remote(p_refs[w].at[2 * cx + cy, rows], o_refs[w].at[c, k, rows], send_sems.at[k, p],
                         recv_sems.at[k, p], (cx, cy, c))
                 for k, (cx, cy) in enumerate(chips) for p, (w, rows) in enumerate(pieces)]

        def start():
            for cp in mine + first:
                cp.start()

        passed = [_remote(o_refs[w].at[c, k, rows], o_refs[w].at[c, k, rows], send_sems.at[3 + k, p],
                          recv_sems.at[3 + k, p], sibling) for k in range(N_CHIPS) for p, (w, rows) in enumerate(pieces)]

        def pass_on():
            for k in range(N_CHIPS):
                own_waited = set()
                for p, (w, rows) in enumerate(pieces):
                    if k < 3:
                        first[k * npc + p].wait_recv()
                    elif w not in own_waited:
                        mine[w].wait()
                        own_waited.add(w)
                    passed[k * npc + p].start()

        def finish():
            for k in range(N_CHIPS):
                for p, (w, rows) in enumerate(pieces):
                    slab = o_refs[w].at[1 - c, k, rows]
                    _remote(slab, slab, send_sems.at[3 + k, p], recv_sems.at[3 + k, p], sibling).wait_recv()
            for cp in first + passed:
                cp.wait_send()

        return start, pass_on, finish

    return _Exchange(list(sums), [jax.ShapeDtypeStruct((2,) + p.shape, BF16) for p in sums],
                     [pltpu.SemaphoreType.DMA((7, npc)), pltpu.SemaphoreType.DMA((7, npc)),
                      pltpu.SemaphoreType.DMA((nw,))], build)


def _small_sum_exchange(vec):
    m_per, n = vec.shape

    def build(ins, outs, scr):
        (x_ref,), (out_ref,) = ins, outs
        gath_ref, sum_ref, send_sems, recv_sems, local_sem, out_sem = scr
        x, y, c, chips = _place()
        me, sibling = (x, y, c), (x, y, 1 - c)

        def rows(px, py, pc):
            return gath_ref.at[pl.ds((4 * px + 2 * py + pc) * m_per, m_per), :]

        def copy(k, block, to, src=None):
            return pltpu.make_async_remote_copy(
                src_ref=rows(*block) if src is None else src, dst_ref=rows(*block),
                send_sem=send_sems.at[k], recv_sem=recv_sems.at[k], device_id=to, device_id_type=MESH)

        mine = pltpu.make_async_copy(x_ref, rows(*me), local_sem)
        first = [copy(0, me, sibling, src=x_ref)] + [copy(1 + j, me, (*chip, c), src=x_ref)
                                                     for j, chip in enumerate(chips)]

        def start():
            for cp in [mine] + first:
                cp.start()

        passed = [copy(4 + j, (*chip, c), sibling) for j, chip in enumerate(chips)]

        def pass_on():
            for j, chip in enumerate(chips):
                copy(1 + j, (*chip, c), me).wait_recv()
                passed[j].start()

        def finish():
            copy(0, sibling, me).wait_recv()
            for j, chip in enumerate(chips):
                copy(4 + j, (*chip, 1 - c), me).wait_recv()
            for cp in first + passed:
                cp.wait_send()
            mine.wait()
            acc = gath_ref[pl.ds(0, m_per), :]
            for k in range(1, N_DEV):
                acc = acc + gath_ref[pl.ds(k * m_per, m_per), :]
            sum_ref[...] = acc
            done = pltpu.make_async_copy(sum_ref, out_ref, out_sem)
            done.start()
            done.wait()

        return start, pass_on, finish

    return _Exchange([vec], [jax.ShapeDtypeStruct((m_per, n), F32)],
                     [pltpu.VMEM((N_DEV * m_per, n), F32), pltpu.VMEM((m_per, n), F32), pltpu.SemaphoreType.DMA((7,)),
                      pltpu.SemaphoreType.DMA((7,)), pltpu.SemaphoreType.DMA, pltpu.SemaphoreType.DMA], build)


def _pack_small(vals, tail=()):
    flat = jnp.concatenate([vals[name].reshape(-1).astype(F32) for name, _ in SMALL] + [v.reshape(1) for v in tail])
    flat = jnp.pad(flat, (0, SMALL_ROWS * LANES - flat.shape[0]))
    return flat.reshape(SMALL_ROWS, LANES)


def _unpack_small(packed):
    flat = packed.reshape(-1)
    out, off = {}, 0
    for name, shape in SMALL:
        n = int(np.prod(shape))
        out[name] = flat[off:off + n].reshape(shape)
        off += n
    return out


def _head_pad_cols(w, heads, real):
    k = w.shape[0]
    return jnp.pad(w.reshape(k, heads, real), ((0, 0), (0, 0), (0, LANES - real))).reshape(k, heads * LANES)


def _rope_tables(positions):
    half = MLA_ROPE // 2
    inv = ROPE_BASE ** (-jnp.arange(half, dtype=F32) / half)
    ang = positions.astype(F32)[:, None] * inv
    cos, sin = jnp.cos(ang), jnp.sin(ang)
    t = positions.shape[0]
    z = lambda n: jnp.zeros((t, n), F32)
    rc = jnp.concatenate([jnp.ones((t, MLA_NOPE), F32), cos, cos, z(LANES - MLA_QK)], axis=1)
    rs1 = jnp.concatenate([z(MLA_NOPE), -sin, z(LANES - MLA_NOPE - half)], axis=1)
    rs2 = jnp.concatenate([z(MLA_NOPE + half), sin, z(LANES - MLA_QK)], axis=1)
    return rc, rs1, rs2


FFN1_WEIGHTS = ("ffn1_w_gu", "ffn1_w_down")
FFN2_WEIGHTS = ("ffn2_w_gu", "ffn2_w_down")
MIXER_WEIGHTS = tuple(n for n, *_ in SHARDED if n not in FFN1_WEIGHTS + FFN2_WEIGHTS)
SHARD_SHAPE = {n: (r, c, kind) for n, r, c, kind in SHARDED}


def _from_blocks(name, gathered):
    r, c, kind = SHARD_SHAPE[name]
    blk = gathered.reshape(N_CHIPS, r, c)
    return blk, (blk.transpose(1, 0, 2).reshape(r, N_CHIPS * c) if kind == "col" else blk.reshape(N_CHIPS * r, c))


def _grad_pair_sums(names, gw, core, tag):
    by_owner = []
    for name in names:
        r, c, kind = SHARD_SHAPE[name]
        if gw[name].dtype == BF16:
            blk = gw[name]
        elif kind == "col":
            blk = gw[name].reshape(r, N_CHIPS, c).transpose(1, 0, 2)
        else:
            blk = gw[name].reshape(N_CHIPS, r, c)
        by_owner.append(blk.astype(BF16).reshape(N_CHIPS, 2, r // 2, c))
    received = _swap_halves(by_owner, "grad_swap_" + tag)
    return [_pair_sum(g, s, core, "pair_sum_" + n) for g, s, n in zip(by_owner, received, names)]


def _device_step(x, mem, positions, tgt, small, shards, core):
    d = D_MODEL
    g_ffn1, g_mix, g_ffn2 = small["ffn1_norm"], small["mix_norm"], small["ffn2_norm"]
    big = {}
    for name, g in zip(FFN1_WEIGHTS, _run_exchange(_gather_exchange([shards[n] for n in FFN1_WEIGHTS]), "gather_ffn1")):
        big[name + "#blocks"], big[name] = _from_blocks(name, g)
    wgu1, wd1 = big["ffn1_w_gu#blocks"], big["ffn1_w_down"].reshape(2, FF_TILE, d)
    x1, gpre1, upre1, h, *rest = _ffn_fwd(x, g_ffn1, wgu1, wd1, "ffn1_fwd", next_gain=g_mix,
                                          ex=_gather_exchange([shards[n] for n in MIXER_WEIGHTS]))
    for name, g in zip(MIXER_WEIGHTS, rest):
        big[name + "#blocks"], big[name] = _from_blocks(name, g)
    w_in = big["w_in"]
    w_uv_, w_cq, w_ckv = w_in[:, :COL_CQ], w_in[:, COL_CQ:COL_CKV], w_in[:, COL_CKV:COL_KR]
    w_kr = jnp.pad(w_in[:, COL_KR:COL_QM], ((0, 0), (MLA_NOPE, LANES - MLA_QK)))
    w_qm, w_g = w_in[:, COL_QM:COL_GATE], w_in[:, COL_GATE:]
    segs = (w_uv_, w_cq, w_ckv, w_kr, w_qm, w_g)
    wuq = _head_pad_cols(big["mla_w_uq"], MLA_HEADS, MLA_QK)
    ukv = big["mla_w_ukv"].reshape(MLA_KV_RANK, MLA_HEADS, 2, MLA_NOPE)
    wuk = _head_pad_cols(ukv[:, :, 0].reshape(MLA_KV_RANK, -1), MLA_HEADS, MLA_NOPE)
    wuv = _head_pad_cols(ukv[:, :, 1].reshape(MLA_KV_RANK, -1), MLA_HEADS, MLA_NOPE)
    wkv = big["mem_w_kv"]
    wa, wc, wo = big["w_branch_a"], big["w_branch_c"], big["w_out"]
    wb = jnp.pad(big["w_branch_b"].reshape(MLA_HEADS, MLA_NOPE, d),
                 ((0, 0), (0, LANES - MLA_NOPE), (0, 0))).reshape(MLA_HEADS * LANES, d)
    qg = jnp.pad(small["mla_q_norm"], ((0, 0), (0, LANES - MLA_QK)))
    kg = jnp.pad(small["mla_k_norm"], ((0, 0), (0, LANES - MLA_QK)))
    causal = jnp.tril(jnp.ones((CHUNK, CHUNK), bool))
    wt_f = jnp.where(causal[None], small["sg_w"][0], 0.0)
    wt, wt_t = wt_f.astype(BF16), wt_f.transpose(0, 2, 1).astype(BF16)
    bias_l = jnp.repeat(small["sg_b"][0].T, 64, axis=1)
    rc, rs1, rs2 = _rope_tables(positions)

    zuv, zcq, zckv, zkr, zqm, zg = _mm_cols(h, segs, [F32] * 5 + [BF16], "in_proj")
    ya = _sgu_fwd(zuv, small["sg_ln_g"], small["sg_ln_b"], wt, bias_l, "sgu_fwd")
    q, k, v, cqn, ckvn = _mla_prep_fwd(zcq, zckv, zkr, small["mla_cq_norm"], small["mla_ckv_norm"], qg, kg,
                                       wuq, wuk, wuv, rc, rs1, rs2, "mla_prep_fwd")
    yb, lse, *rest = _attn_fwd(q, k, v, "mla_attn_fwd", ex=_gather_exchange([shards[n] for n in FFN2_WEIGHTS]))
    for name, g in zip(FFN2_WEIGHTS, rest):
        big[name + "#blocks"], big[name] = _from_blocks(name, g)
    wgu2, wd2 = big["ffn2_w_gu#blocks"], big["ffn2_w_down"].reshape(2, FF_TILE, d)
    km, vm, memn = _mem_kv_fwd(mem, small["mem_norm"], wkv, small["mem_k_norm"], "mem_kv_fwd")
    yc = _mem_attn_fwd(zqm, small["mem_q_norm"], km, vm, "mem_attn_fwd")
    x2, merged, pa, pb, pc = _merge_fwd(x1, ya, yb, yc, zg, small["b_gate"], wa, wb, wc, wo, "merge_fwd")
    dy, loss_row, gpre2, upre2 = _ffn_fwd(x2, g_ffn2, wgu2, wd2, "ffn2_fwd", target=tgt)

    gw, gs, slots = {}, {}, {}

    def ffn_grads(prefix, xin, gain, dyin, gpre, upre, wgu, wd, ex=None, ex_names=(), last=False):
        dx, dgain, xn, dgt, dup, act, *got = _ffn_bwd(xin, gain, dyin, gpre, upre, wgu, wd, prefix + "_bwd", ex=ex)
        slots.update(zip(ex_names, got))
        gs[prefix + "_norm"] = dgain
        gw[prefix + "_w_gu"] = jnp.concatenate(
            [_mm_tn(xn, dgt, prefix + "_dwg", col_blocks=True, out_dtype=BF16),
             _mm_tn(xn, dup, prefix + "_dwu", col_blocks=True, out_dtype=BF16)], axis=0)
        rows_down = SHARD_SHAPE[prefix + "_w_down"][0]
        if last:
            small_sum = _small_sum_exchange(_pack_small(gs, tail=[loss_row[0, 0]]))
            dwd, summed = _mm_tn(act, dyin, prefix + "_dwd", scale=0.5, ex=small_sum, out_dtype=BF16)
            gw[prefix + "_w_down"] = dwd.reshape(N_CHIPS, rows_down, d)
            return dx, summed
        gw[prefix + "_w_down"] = _mm_tn(act, dyin, prefix + "_dwd", scale=0.5, out_dtype=BF16).reshape(
            N_CHIPS, rows_down, d)
        return dx

    dx2 = ffn_grads("ffn2", x2, g_ffn2, dy, gpre2, upre2, wgu2, wd2)
    ffn2_sums = _pair_sum_exchange(_grad_pair_sums(FFN2_WEIGHTS, gw, core, "ffn2"))
    dpa, dpb, dpc, dzg, dbg, dya, dyb, dyc, *got = _merge_bwd(dx2, pa, pb, pc, zg, small["b_gate"], wa, wb, wc, wo,
                                                              "merge_bwd", ex=ffn2_sums)
    slots.update(zip(FFN2_WEIGHTS, got))
    gs["b_gate"] = dbg
    gw["w_out"] = _mm_tn(merged, dx2, "dw_out")
    gw["w_branch_a"] = _mm_tn(ya, dpa, "dw_branch_a")
    gw["w_branch_b"] = _mm_tn(yb, dpb, "dw_branch_b").reshape(MLA_HEADS, LANES, d)[:, :MLA_NOPE].reshape(-1, d)
    gw["w_branch_c"] = _mm_tn(yc, dpc, "dw_branch_c")

    dzuv, dwt, dbl, dlg, dlb = _sgu_bwd(zuv, dya, small["sg_ln_g"], small["sg_ln_b"], wt, wt_t, bias_l, "sgu_bwd")
    gs["sg_w"], gs["sg_b"] = dwt[None], dbl[:, :SG_GROUPS].T[None]
    gs["sg_ln_g"], gs["sg_ln_b"] = dlg, dlb

    delta_rows, lse_rows = _attn_bwd_rows(yb, lse, dyb, "mla_attn_bwd_rows")
    dq, dk, dv = _attn_bwd(q, k, v, delta_rows, lse_rows, dyb, "mla_attn_bwd")
    dzcq, dzckv, dzkr, dql, dkl, dgcq, dgckv, dqg, dkg = _mla_prep_bwd(
        zcq, zckv, zkr, small["mla_cq_norm"], small["mla_ckv_norm"], qg, kg, wuq, wuk, wuv, rc, rs1, rs2,
        dq, dk, dv, "mla_prep_bwd")
    gs["mla_cq_norm"], gs["mla_ckv_norm"] = dgcq, dgckv
    gs["mla_q_norm"], gs["mla_k_norm"] = dqg[:, :MLA_QK], dkg[:, :MLA_QK]
    gw["mla_w_uq"] = _mm_tn(cqn, dql, "dw_uq").reshape(MLA_Q_RANK, MLA_HEADS, LANES)[:, :, :MLA_QK].reshape(
        MLA_Q_RANK, -1)
    dwuk = _mm_tn(ckvn, dkl, "dw_uk").reshape(MLA_KV_RANK, MLA_HEADS, LANES)[:, :, :MLA_NOPE]
    dwuv = _mm_tn(ckvn, dv, "dw_uv").reshape(MLA_KV_RANK, MLA_HEADS, LANES)[:, :, :MLA_NOPE]
    gw["mla_w_ukv"] = jnp.concatenate([dwuk, dwuv], axis=2).reshape(MLA_KV_RANK, -1)

    dzqm, dkn, dvm, dmqg = _mem_attn_bwd(zqm, dyc, small["mem_q_norm"], km, vm, "mem_attn_bwd")
    gs["mem_q_norm"] = dmqg
    gw["mem_w_kv"], gs["mem_k_norm"], gs["mem_norm"] = _mem_kv_bwd(
        mem, small["mem_norm"], wkv, small["mem_k_norm"], dkn, dvm, "mem_kv_bwd")

    dzs = (dzuv, dzcq, dzckv, dzkr, dzqm, dzg)
    dws = list(_mm_tn_cols(h, dzs[:5], "dw_in_narrow")) + [_mm_tn(h, dzg, "dw_in_gate")]
    dws[3] = dws[3][:, MLA_NOPE:MLA_QK]
    gw["w_in"] = jnp.concatenate(dws, axis=1)
    dx1, gs["mix_norm"] = _proj_norm_bwd(dzs, [w.T for w in segs], x1, g_mix, dx2, "in_proj_bwd")
    mixer_sums = _pair_sum_exchange(_grad_pair_sums(MIXER_WEIGHTS, gw, core, "mixer"))
    dx, summed = ffn_grads("ffn1", x, g_ffn1, dx1, gpre1, upre1, wgu1, wd1, ex=mixer_sums, ex_names=MIXER_WEIGHTS,
                           last=True)
    ffn1_sums = _pair_sum_exchange(_grad_pair_sums(FFN1_WEIGHTS, gw, core, "ffn1"))
    slots.update(zip(FFN1_WEIGHTS, _run_exchange(ffn1_sums, "grad_exchange_ffn1")))
    return dx, slots, summed


def kernel(x, mem, positions, ffn1_norm, ffn1_w_gu, ffn1_w_down, mix_norm, w_in, b_gate, sg_ln_g, sg_ln_b, sg_w, sg_b, mla_cq_norm, mla_w_uq, mla_ckv_norm, mla_w_ukv, mla_q_norm, mla_k_norm, mem_norm, mem_w_kv, mem_q_norm, mem_k_norm, w_branch_a, w_branch_b, w_branch_c, w_out, ffn2_norm, ffn2_w_gu, ffn2_w_down, loss_target, m_ffn1_norm, m_ffn1_w_gu, m_ffn1_w_down, m_mix_norm, m_w_in, m_b_gate, m_sg_ln_g, m_sg_ln_b, m_sg_w, m_sg_b, m_mla_cq_norm, m_mla_w_uq, m_mla_ckv_norm, m_mla_w_ukv, m_mla_q_norm, m_mla_k_norm, m_mem_norm, m_mem_w_kv, m_mem_q_norm, m_mem_k_norm, m_w_branch_a, m_w_branch_b, m_w_branch_c, m_w_out, m_ffn2_norm, m_ffn2_w_gu, m_ffn2_w_down, v_ffn1_norm, v_ffn1_w_gu, v_ffn1_w_down, v_mix_norm, v_w_in, v_b_gate, v_sg_ln_g, v_sg_ln_b, v_sg_w, v_sg_b, v_mla_cq_norm, v_mla_w_uq, v_mla_ckv_norm, v_mla_w_ukv, v_mla_q_norm, v_mla_k_norm, v_mem_norm, v_mem_w_kv, v_mem_q_norm, v_mem_k_norm, v_w_branch_a, v_w_branch_b, v_w_branch_c, v_w_out, v_ffn2_norm, v_ffn2_w_gu, v_ffn2_w_down):
    args = dict(locals())
    weights = {n: args[n] for n in WEIGHT_ORDER}
    mom_m = {n: args["m_" + n] for n in WEIGHT_ORDER}
    mom_v = {n: args["v_" + n] for n in WEIGHT_ORDER}
    small = {n: weights[n] for n, _ in SMALL}
    halves = lambda a, r, c: a.reshape(2, r // 2, c)

    shards = {n: halves(weights[n][0].astype(BF16), r, c) for n, r, c, _ in SHARDED}
    core = lax.axis_index("c").astype(jnp.int32).reshape(1)
    dx, slots, summed = _device_step(x[0], mem[0], positions[0], loss_target[0], small, shards, core)
    loss = summed.reshape(-1)[_N_SMALL]
    small_grads = _unpack_small(summed)

    grads, deltas, new_m, new_v = {}, {}, {}, {}
    for name, r, c, _ in SHARDED:
        outs = _adamw_slots(halves(weights[name][0], r, c), slots[name], halves(mom_m[name][0], r, c),
                            halves(mom_v[name][0], r, c), "adamw_" + name)
        shape = weights[name].shape
        grads[name], deltas[name], new_m[name], new_v[name] = [o.reshape(shape) for o in outs]
    dlt, nm, nv = _adamw(_pack_small(small), _pack_small(small_grads), _pack_small({n: mom_m[n] for n, _ in SMALL}),
                         _pack_small({n: mom_v[n] for n, _ in SMALL}), "adamw_small")
    for name, _ in SMALL:
        grads[name] = small_grads[name]
    deltas.update(_unpack_small(dlt))
    new_m.update(_unpack_small(nm))
    new_v.update(_unpack_small(nv))

    return (loss, dx[None], *[grads[n] for n in WEIGHT_ORDER], *[deltas[n] for n in WEIGHT_ORDER],
            *[new_m[n] for n in WEIGHT_ORDER], *[new_v[n] for n in WEIGHT_ORDER])
```

```python
import functools
from typing import Callable, NamedTuple

import numpy as np
import jax
import jax.numpy as jnp
from jax import lax
from jax.experimental import pallas as pl
from jax.experimental.pallas import tpu as pltpu

F32 = jnp.float32
BF16 = jnp.bfloat16

D_MODEL = 1024
D_FF = 2816
FF_TILE = 1408
SG_WIDTH = 512
SG_GROUPS = 8
CHUNK = 128
MLA_HEADS = 8
MLA_QK = 96
MLA_NOPE = 64
MLA_ROPE = 32
MLA_Q_RANK = 384
MLA_KV_RANK = 256
MEM_HEADS = 4
MEM_LEN = 256
LANES = 128
EPS = 1e-6
NEG = -1e30
ROPE_BASE = 10000.0
N_CHIPS = 4
N_DEV = 8

ADAM_LR = 0.001
ADAM_B1 = 0.9
ADAM_B2 = 0.999
ADAM_EPS = 1e-08
ADAM_WD = 0.01
ADAM_STEP = 10

COL_V = 512
COL_CQ = 1024
COL_CKV = 1408
COL_KR = 1664
COL_QM = 1696
COL_GATE = 2208
IN_COLS = 5280

VMEM_LIMIT_BYTES = 56 * 1024 * 1024
INV_SQRT2 = 0.7071067811865476
INV_SQRT_2PI = 0.3989422804014327
LOG2E = 1.4426950408889634
ATTN_SCALE = MLA_QK ** -0.5
V_ONES_LANE = 64
ATTN_SCALE2 = ATTN_SCALE * LOG2E

SHARDED = (
    ("ffn1_w_gu", 1024, 1408, "col"),
    ("ffn1_w_down", 704, 1024, "row"),
    ("w_in", 1024, 1320, "col"),
    ("mla_w_uq", 384, 192, "col"),
    ("mla_w_ukv", 256, 256, "col"),
    ("mem_w_kv", 256, 1024, "row"),
    ("w_branch_a", 512, 256, "col"),
    ("w_branch_b", 512, 256, "col"),
    ("w_branch_c", 512, 256, "col"),
    ("w_out", 256, 1024, "row"),
    ("ffn2_w_gu", 1024, 1408, "col"),
    ("ffn2_w_down", 704, 1024, "row"),
)
SMALL = (
    ("ffn1_norm", (1, 1024)), ("mix_norm", (1, 1024)), ("b_gate", (1, 3072)),
    ("sg_ln_g", (1, 512)), ("sg_ln_b", (1, 512)), ("sg_w", (1, 8, 128, 128)),
    ("sg_b", (1, 8, 128)), ("mla_cq_norm", (1, 384)), ("mla_ckv_norm", (1, 256)),
    ("mla_q_norm", (1, 96)), ("mla_k_norm", (1, 96)), ("mem_norm", (1, 1024)),
    ("mem_q_norm", (1, 128)), ("mem_k_norm", (1, 128)), ("ffn2_norm", (1, 1024)),
)
WEIGHT_ORDER = (
    "ffn1_norm", "ffn1_w_gu", "ffn1_w_down", "mix_norm", "w_in", "b_gate", "sg_ln_g", "sg_ln_b",
    "sg_w", "sg_b", "mla_cq_norm", "mla_w_uq", "mla_ckv_norm", "mla_w_ukv", "mla_q_norm",
    "mla_k_norm", "mem_norm", "mem_w_kv", "mem_q_norm", "mem_k_norm", "w_branch_a", "w_branch_b",
    "w_branch_c", "w_out", "ffn2_norm", "ffn2_w_gu", "ffn2_w_down",
)

_N_SMALL = sum(int(np.prod(s)) for _, s in SMALL)
SMALL_ROWS = -(-_N_SMALL // (LANES * 8)) * 8

MESH = pl.DeviceIdType.MESH


def _cparams():
    return pltpu.CompilerParams(vmem_limit_bytes=VMEM_LIMIT_BYTES)


def _dot(a, b):
    return jnp.dot(a, b, preferred_element_type=F32)


def _dot_nt(a, b):
    return lax.dot_general(a, b, (((1,), (1,)), ((), ())), preferred_element_type=F32)


def _dot_tn(a, b):
    return lax.dot_general(a, b, (((0,), (0,)), ((), ())), preferred_element_type=F32)


def _gelu(x):
    return 0.5 * x * (1.0 + lax.erf(x * INV_SQRT2))


def _gelu_grad(x):
    return 0.5 * (1.0 + lax.erf(x * INV_SQRT2)) + x * jnp.exp(-0.5 * x * x) * INV_SQRT_2PI


def _rstd(x, n):
    return lax.rsqrt(jnp.sum(x * x, axis=-1, keepdims=True) * (1.0 / n) + EPS)


def _rms_vjp(x, r, g, dy, n):
    dxh = dy * g
    dx = r * dxh - x * (r * r * r) * (jnp.sum(dxh * x, axis=-1, keepdims=True) * (1.0 / n))
    return dx, dy * x * r


def _row_tile(t, want):
    return min(t, want)


def _wide_tile(n):
    if n <= 1024:
        return n
    if n % 1024 == 0:
        return 1024
    assert n % FF_TILE == 0, n
    return FF_TILE


def _mm_cols(a, ws, out_dtypes, name, ex=None):
    t, kdim = a.shape
    tm = _row_tile(t, 256)
    n = len(ws)

    def body(*refs):
        av = refs[0][...]
        for w_ref, o_ref in zip(refs[1:1 + n], refs[1 + n:]):
            o_ref[...] = _dot(av, w_ref[...]).astype(o_ref.dtype)

    row = lambda width: pl.BlockSpec((tm, width), lambda i: (i, 0))
    return _call_with_exchange(
        ex, body, name, (t // tm,),
        [row(kdim)] + [pl.BlockSpec(w.shape, lambda i: (0, 0)) for w in ws],
        [row(w.shape[1]) for w in ws],
        [jax.ShapeDtypeStruct((t, w.shape[1]), dt) for w, dt in zip(ws, out_dtypes)], [], (a, *ws))


def _proj_norm_bwd(dzs, wts, x, g, dres, name):
    t, d = x.shape
    tm = _row_tile(t, 256)
    n = len(dzs)

    def body(*refs):
        x_ref, g_ref, r_ref, dx_ref, dg_ref = refs[2 * n:]

        @pl.when(pl.program_id(0) == 0)
        def _():
            dg_ref[...] = jnp.zeros_like(dg_ref)

        dh = None
        for dz_ref, w_ref in zip(refs[:n], refs[n:2 * n]):
            part = _dot(dz_ref[...], w_ref[...])
            dh = part if dh is None else dh + part
        xv = x_ref[...]
        dx, dgr = _rms_vjp(xv, _rstd(xv, d), g_ref[...], dh, d)
        dx_ref[...] = r_ref[...] + dx
        dg_ref[...] += jnp.sum(dgr, axis=0, keepdims=True)

    row = lambda width: pl.BlockSpec((tm, width), lambda i: (i, 0))
    vec = pl.BlockSpec((1, d), lambda i: (0, 0))
    return pl.pallas_call(
        body, name=name, grid=(t // tm,),
        in_specs=[row(dz.shape[1]) for dz in dzs] + [pl.BlockSpec(w.shape, lambda i: (0, 0)) for w in wts]
        + [row(d), vec, row(d)],
        out_specs=[row(d), vec],
        out_shape=[jax.ShapeDtypeStruct((t, d), F32), jax.ShapeDtypeStruct((1, d), F32)],
        compiler_params=_cparams())(*dzs, *wts, x, g, dres)


def _mm_tn_cols(a, bs, name):
    t, m = a.shape
    tk = _row_tile(t, 512)
    n = len(bs)

    def body(*refs):
        @pl.when(pl.program_id(0) == 0)
        def _():
            for o_ref in refs[1 + n:]:
                o_ref[...] = jnp.zeros_like(o_ref)

        av = refs[0][...].astype(BF16)
        for b_ref, o_ref in zip(refs[1:1 + n], refs[1 + n:]):
            o_ref[...] += _dot_tn(av, b_ref[...].astype(BF16))

    row = lambda width: pl.BlockSpec((tk, width), lambda k: (k, 0))
    return pl.pallas_call(
        body, name=name, grid=(t // tk,), in_specs=[row(m)] + [row(b.shape[1]) for b in bs],
        out_specs=[pl.BlockSpec((m, b.shape[1]), lambda k: (0, 0)) for b in bs],
        out_shape=[jax.ShapeDtypeStruct((m, b.shape[1]), F32) for b in bs],
        compiler_params=_cparams())(a, *bs)


def _mm_tn(a, b, name, scale=1.0, ex=None, col_blocks=False, out_dtype=F32):
    t, m = a.shape
    n = b.shape[1]
    tm, tn = _wide_tile(m), _wide_tile(n)
    tk = _row_tile(t, 1024)
    nk = t // tk
    in_place = out_dtype == F32

    def body(a_ref, b_ref, o_ref, *scr):
        k = pl.program_id(2)
        acc_ref = o_ref if in_place else scr[0]

        @pl.when(k == 0)
        def _():
            acc_ref[...] = jnp.zeros_like(acc_ref)

        prod = _dot_tn(a_ref[...].astype(BF16), b_ref[...].astype(BF16))
        acc_ref[...] += prod.reshape(acc_ref.shape)
        if scale != 1.0 or not in_place:
            @pl.when(k == nk - 1)
            def _():
                o_ref[...] = (acc_ref[...] * scale).astype(out_dtype).reshape(o_ref.shape)

    if col_blocks:
        out_spec = pl.BlockSpec((1, tm, tn), lambda i, j, k: (j, i, 0))
        out_shape = jax.ShapeDtypeStruct((n // tn, m, tn), out_dtype)
    else:
        out_spec = pl.BlockSpec((tm, tn), lambda i, j, k: (i, j))
        out_shape = jax.ShapeDtypeStruct((m, n), out_dtype)
    outs = _call_with_exchange(
        ex, body, name, (m // tm, n // tn, nk),
        [pl.BlockSpec((tk, tm), lambda i, j, k: (k, i)), pl.BlockSpec((tk, tn), lambda i, j, k: (k, j))],
        [out_spec], [out_shape], [] if in_place else [pltpu.VMEM((tm, tn), F32)], (a, b))
    return outs[0] if ex is None else outs


PASS_ON_STEPS_BEFORE_END = 8


class _Exchange(NamedTuple):
    operands: list
    out_shapes: list
    sem_shapes: list
    build: Callable


def _call_with_exchange(ex, body, name, grid, in_specs, out_specs, out_shape, scratch_shapes, operands, prefetch=()):
    n_pre = len(prefetch)
    total = int(np.prod(grid))
    pass_step = max(total // 2, total - PASS_ON_STEPS_BEFORE_END)

    def call(kernel, ins, outs, shapes, scratch):
        if n_pre:
            spec = pltpu.PrefetchScalarGridSpec(num_scalar_prefetch=n_pre, grid=grid, in_specs=ins, out_specs=outs,
                                                scratch_shapes=scratch)
            return pl.pallas_call(kernel, name=name, grid_spec=spec, out_shape=shapes, compiler_params=_cparams())
        return pl.pallas_call(kernel, name=name, grid=grid, in_specs=ins, out_specs=outs, out_shape=shapes,
                              scratch_shapes=scratch, compiler_params=_cparams())

    if ex is None:
        return call(body, in_specs, out_specs, out_shape, scratch_shapes)(*prefetch, *operands)
    n_in, n_out, n_scr = len(in_specs), len(out_specs), len(scratch_shapes)
    k_in, k_out = len(ex.operands), len(ex.out_shapes)

    def carried(*refs):
        pre, refs = refs[:n_pre], refs[n_pre:]
        a, b = n_in, n_in + k_in
        c, e = b + n_out, b + n_out + k_out
        f = e + n_scr
        start, pass_on, finish = ex.build(refs[a:b], refs[c:e], refs[f:])
        step = functools.reduce(lambda lin, ax: lin * grid[ax] + pl.program_id(ax), range(len(grid)), 0)
        pl.when(step == 0)(start)
        body(*pre, *refs[:a], *refs[b:c], *refs[e:f])
        pl.when(step == pass_step)(pass_on)
        pl.when(step == total - 1)(finish)

    return call(carried, list(in_specs) + [ANY] * k_in, list(out_specs) + [ANY] * k_out,
                list(out_shape) + list(ex.out_shapes), list(scratch_shapes) + list(ex.sem_shapes),
                )(*prefetch, *operands, *ex.operands)


def _run_exchange(ex, name):
    k_in, k_out = len(ex.operands), len(ex.out_shapes)

    def body(*refs):
        start, pass_on, finish = ex.build(refs[:k_in], refs[k_in:k_in + k_out], refs[k_in + k_out:])
        start()
        pass_on()
        finish()

    return pl.pallas_call(body, name=name, in_specs=[ANY] * k_in, out_specs=[ANY] * k_out,
                          out_shape=list(ex.out_shapes), scratch_shapes=list(ex.sem_shapes))(*ex.operands)


def _ffn_fwd(x, g, wgu4, wd2, name, ex=None, next_gain=None, target=None):
    t, d = x.shape
    tm = _row_tile(t, 512)
    assert next_gain is None or target is None
    extra = [a for a in (next_gain, target) if a is not None]

    def body(*refs):
        x_ref, g_ref, wg_ref, wu_ref, wd_ref = refs[:5]
        e_ref = refs[5] if extra else None
        outs, (xn_scr, acc_scr) = refs[5 + len(extra):-2], refs[-2:]
        if target is not None:
            dy_ref, loss_ref, gg_ref, uu_ref = outs
        elif next_gain is not None:
            o_ref, gg_ref, uu_ref, h_ref = outs
        else:
            o_ref, gg_ref, uu_ref = outs
        i, j = pl.program_id(0), pl.program_id(1)

        @pl.when(j == 0)
        def _():
            xv = x_ref[...]
            xn_scr[...] = (xv * _rstd(xv, d) * g_ref[...]).astype(BF16)
            acc_scr[...] = jnp.zeros_like(acc_scr)

        if target is not None:
            @pl.when((i == 0) & (j == 0))
            def _():
                loss_ref[...] = jnp.zeros_like(loss_ref)

        xn = xn_scr[...]
        gg = _dot(xn, wg_ref[0])
        uu = _dot(xn, wu_ref[0])
        gg_ref[...] = gg.astype(BF16)
        uu_ref[...] = uu.astype(BF16)
        act = gg * jax.nn.sigmoid(gg) * uu
        acc_scr[...] += _dot(act.astype(BF16), wd_ref[0])

        @pl.when(j == 1)
        def _():
            y = x_ref[...] + 0.5 * acc_scr[...]
            if target is not None:
                e = y - e_ref[...]
                dy_ref[...] = e * (1.0 / d)
                part = 0.5 * jnp.sum(jnp.sum(e * e, axis=-1, keepdims=True) * (1.0 / d), axis=0, keepdims=True)
                loss_ref[...] += jnp.broadcast_to(part, loss_ref.shape)
            else:
                o_ref[...] = y
                if next_gain is not None:
                    h_ref[...] = (y * _rstd(y, d) * e_ref[...]).astype(BF16)

    row = pl.BlockSpec((tm, d), lambda i, j: (i, 0))
    vec = pl.BlockSpec((1, d), lambda i, j: (0, 0))
    ffb = pl.BlockSpec((tm, FF_TILE), lambda i, j: (i, j))
    f32_rows, bf16_ff = jax.ShapeDtypeStruct((t, d), F32), jax.ShapeDtypeStruct((t, D_FF), BF16)
    if target is not None:
        extra_spec, out_specs = [row], [row, pl.BlockSpec((1, LANES), lambda i, j: (0, 0)), ffb, ffb]
        out_shape = [f32_rows, jax.ShapeDtypeStruct((1, LANES), F32), bf16_ff, bf16_ff]
    elif next_gain is not None:
        extra_spec, out_specs = [vec], [row, ffb, ffb, row]
        out_shape = [f32_rows, bf16_ff, bf16_ff, jax.ShapeDtypeStruct((t, d), BF16)]
    else:
        extra_spec, out_specs, out_shape = [], [row, ffb, ffb], [f32_rows, bf16_ff, bf16_ff]
    return _call_with_exchange(
        ex, body, name, (t // tm, 2),
        [row, vec,
         pl.BlockSpec((1, d, FF_TILE), lambda i, j: (j, 0, 0)),
         pl.BlockSpec((1, d, FF_TILE), lambda i, j: (j + 2, 0, 0)),
         pl.BlockSpec((1, FF_TILE, d), lambda i, j: (j, 0, 0))] + extra_spec,
        out_specs, out_shape,
        [pltpu.VMEM((tm, d), BF16), pltpu.VMEM((tm, d), F32)], (x, g, wgu4, wgu4, wd2, *extra))


def _ffn_bwd(x, g, dy, gpre, upre, wgu4, wd2, name, ex=None):
    t, d = x.shape
    tm = _row_tile(t, 512)

    def body(dy_ref, gg_ref, uu_ref, wgu_hbm, wd_hbm, dg_ref, du_ref, act_ref, part_ref, wg_ref, wu_ref, wd_ref):
        j = pl.program_id(0)

        @pl.when(pl.program_id(1) == 0)
        def _():
            pltpu.sync_copy(wgu_hbm.at[j], wg_ref.at[0])
            pltpu.sync_copy(wgu_hbm.at[j + 2], wu_ref.at[0])
            pltpu.sync_copy(wd_hbm.at[j], wd_ref.at[0])

        gg = gg_ref[...].astype(F32)
        uu = uu_ref[...].astype(F32)
        sg = jax.nn.sigmoid(gg)
        silu = gg * sg
        act_ref[...] = (silu * uu).astype(BF16)
        dyh = (0.5 * dy_ref[...]).astype(BF16)
        dact = _dot_nt(dyh, wd_ref[0])
        du = (dact * silu).astype(BF16)
        dgt = (dact * uu * (sg * (1.0 + gg * (1.0 - sg)))).astype(BF16)
        du_ref[...] = du
        dg_ref[...] = dgt
        part_ref[0] = _dot_nt(dgt, wg_ref[0]) + _dot_nt(du, wu_ref[0])

    row = pl.BlockSpec((tm, d), lambda j, i: (i, 0))
    ffb = pl.BlockSpec((tm, FF_TILE), lambda j, i: (i, j))
    dgt, dup, act, parts, *got = _call_with_exchange(
        ex, body, name, (2, t // tm),
        [row, ffb, ffb, ANY, ANY],
        [ffb, ffb, ffb, pl.BlockSpec((1, tm, d), lambda j, i: (j, i, 0))],
        [jax.ShapeDtypeStruct((t, D_FF), BF16)] * 3 + [jax.ShapeDtypeStruct((2, t, d), F32)],
        [pltpu.VMEM((1, d, FF_TILE), BF16), pltpu.VMEM((1, d, FF_TILE), BF16), pltpu.VMEM((1, FF_TILE, d), BF16)],
        (dy, gpre, upre, wgu4, wd2))

    def norm_body(x_ref, g_ref, p_ref, dy_ref, dx_ref, dgain_ref, xn_ref):
        @pl.when(pl.program_id(0) == 0)
        def _():
            dgain_ref[...] = jnp.zeros_like(dgain_ref)

        xv = x_ref[...]
        r = _rstd(xv, d)
        xn_ref[...] = (xv * r * g_ref[...]).astype(BF16)
        dx, dgr = _rms_vjp(xv, r, g_ref[...], p_ref[0] + p_ref[1], d)
        dx_ref[...] = dy_ref[...] + dx
        dgain_ref[...] += jnp.sum(dgr, axis=0, keepdims=True)

    tn = _row_tile(t, 256)
    nrow = pl.BlockSpec((tn, d), lambda i: (i, 0))
    vec = pl.BlockSpec((1, d), lambda i: (0, 0))
    dx, dgain, xn = pl.pallas_call(
        norm_body, name=name + "_norm", grid=(t // tn,),
        in_specs=[nrow, vec, pl.BlockSpec((2, tn, d), lambda i: (0, i, 0)), nrow],
        out_specs=[nrow, vec, nrow],
        out_shape=[jax.ShapeDtypeStruct((t, d), F32), jax.ShapeDtypeStruct((1, d), F32),
                   jax.ShapeDtypeStruct((t, d), BF16)],
        compiler_params=_cparams())(x, g, parts, dy)
    return [dx, dgain, xn, dgt, dup, act] + got


def _sgu_layernorm(vpre, lg, lb):
    v = _gelu(vpre)
    mu = jnp.mean(v, axis=-1, keepdims=True)
    xc = v - mu
    rstd = lax.rsqrt(jnp.mean(xc * xc, axis=-1, keepdims=True) + EPS)
    xhat = xc * rstd
    return xhat, rstd, xhat * lg + lb


def _sgu_fwd(zuv, lg, lb, wt, bias_l, name):
    t = zuv.shape[0]
    tm = _row_tile(t, 512)

    def body(u_ref, v_ref, lg_ref, lb_ref, wt_ref, bl_ref, o_ref, vln_scr):
        _, _, vln = _sgu_layernorm(v_ref[...], lg_ref[...], lb_ref[...])
        vln_scr[...] = vln.astype(BF16)
        lo = lax.broadcasted_iota(jnp.int32, (CHUNK, LANES), 1) < 64
        for c in range(tm // CHUNK):
            rows = slice(c * CHUNK, (c + 1) * CHUNK)
            for p in range(SG_GROUPS // 2):
                cols = slice(p * LANES, (p + 1) * LANES)
                vp = vln_scr[rows, cols]
                mixed = jnp.where(lo, _dot(wt_ref[2 * p], vp), _dot(wt_ref[2 * p + 1], vp)) + bl_ref[:, cols]
                o_ref[rows, cols] = (_gelu(u_ref[rows, cols]) * mixed).astype(BF16)

    half = lambda k: pl.BlockSpec((tm, SG_WIDTH), lambda i: (i, k))
    vec = pl.BlockSpec((1, SG_WIDTH), lambda i: (0, 0))
    return pl.pallas_call(
        body, name=name, grid=(t // tm,),
        in_specs=[half(0), half(1), vec, vec,
                  pl.BlockSpec((SG_GROUPS, CHUNK, CHUNK), lambda i: (0, 0, 0)),
                  pl.BlockSpec((CHUNK, SG_WIDTH), lambda i: (0, 0))],
        out_specs=pl.BlockSpec((tm, SG_WIDTH), lambda i: (i, 0)),
        out_shape=jax.ShapeDtypeStruct((t, SG_WIDTH), BF16),
        scratch_shapes=[pltpu.VMEM((tm, SG_WIDTH), BF16)],
        compiler_params=_cparams())(zuv, zuv, lg, lb, wt, bias_l)


def _sgu_bwd(zuv, dya, lg, lb, wt, wt_t, bias_l, name):
    t = zuv.shape[0]
    tm = _row_tile(t, 256)
    nsteps = t // tm

    def body(u_ref, v_ref, dy_ref, lg_ref, lb_ref, wt_ref, wtt_ref, bl_ref,
             dz_ref, dwt_ref, dbl_ref, dlg_ref, dlb_ref, vln_scr, dvln_scr, dbacc_scr):
        step = pl.program_id(0)

        @pl.when(step == 0)
        def _():
            dwt_ref[...] = jnp.zeros_like(dwt_ref)
            dlg_ref[...] = jnp.zeros_like(dlg_ref)
            dlb_ref[...] = jnp.zeros_like(dlb_ref)
            dbl_ref[...] = jnp.zeros_like(dbl_ref)
            dbacc_scr[...] = jnp.zeros_like(dbacc_scr)

        vpre = v_ref[...]
        lgv = lg_ref[...]
        xhat, rstd, vln = _sgu_layernorm(vpre, lgv, lb_ref[...])
        vln_scr[...] = vln.astype(BF16)
        lo = lax.broadcasted_iota(jnp.int32, (CHUNK, LANES), 1) < 64
        for c in range(tm // CHUNK):
            rows = slice(c * CHUNK, (c + 1) * CHUNK)
            for p in range(SG_GROUPS // 2):
                cols = slice(p * LANES, (p + 1) * LANES)
                vp = vln_scr[rows, cols]
                mixed = jnp.where(lo, _dot(wt_ref[2 * p], vp), _dot(wt_ref[2 * p + 1], vp)) + bl_ref[:, cols]
                upre = u_ref[rows, cols]
                dyp = dy_ref[rows, cols]
                dz_ref[rows, cols] = (dyp * mixed * _gelu_grad(upre)).astype(BF16)
                dm = dyp * _gelu(upre)
                dbacc_scr[:, cols] += dm
                dlo = jnp.where(lo, dm, 0.0).astype(BF16)
                dhi = jnp.where(lo, 0.0, dm).astype(BF16)
                dvln_scr[rows, cols] = _dot(wtt_ref[2 * p], dlo) + _dot(wtt_ref[2 * p + 1], dhi)
                dwt_ref[2 * p] += _dot_nt(dlo, vp)
                dwt_ref[2 * p + 1] += _dot_nt(dhi, vp)
        dvln = dvln_scr[...]
        dlg_ref[...] += jnp.sum(dvln * xhat, axis=0, keepdims=True)
        dlb_ref[...] += jnp.sum(dvln, axis=0, keepdims=True)
        dxh = dvln * lgv
        dv = rstd * (dxh - jnp.mean(dxh, axis=-1, keepdims=True)
                     - xhat * jnp.mean(dxh * xhat, axis=-1, keepdims=True))
        dz_ref[:, SG_WIDTH:] = (dv * _gelu_grad(vpre)).astype(BF16)

        @pl.when(step == nsteps - 1)
        def _():
            rr = lax.broadcasted_iota(jnp.int32, (CHUNK, CHUNK), 0)
            cc = lax.broadcasted_iota(jnp.int32, (CHUNK, CHUNK), 1)
            tril = (cc <= rr).astype(F32)
            for gidx in range(SG_GROUPS):
                dwt_ref[gidx] = dwt_ref[gidx] * tril
            kk = lax.broadcasted_iota(jnp.int32, (SG_WIDTH, LANES), 0)
            gg = lax.broadcasted_iota(jnp.int32, (SG_WIDTH, LANES), 1)
            sel = ((kk // 64) == gg).astype(F32)
            dbl_ref[...] = jnp.dot(dbacc_scr[...], sel, preferred_element_type=F32,
                                   precision=lax.Precision.HIGHEST)

    half = lambda k: pl.BlockSpec((tm, SG_WIDTH), lambda i: (i, k))
    vec = pl.BlockSpec((1, SG_WIDTH), lambda i: (0, 0))
    wspec = pl.BlockSpec((SG_GROUPS, CHUNK, CHUNK), lambda i: (0, 0, 0))
    return pl.pallas_call(
        body, name=name, grid=(nsteps,),
        in_specs=[half(0), half(1), pl.BlockSpec((tm, SG_WIDTH), lambda i: (i, 0)), vec, vec,
                  wspec, wspec, pl.BlockSpec((CHUNK, SG_WIDTH), lambda i: (0, 0))],
        out_specs=[pl.BlockSpec((tm, 2 * SG_WIDTH), lambda i: (i, 0)), wspec,
                   pl.BlockSpec((CHUNK, LANES), lambda i: (0, 0)), vec, vec],
        out_shape=[jax.ShapeDtypeStruct((t, 2 * SG_WIDTH), BF16),
                   jax.ShapeDtypeStruct((SG_GROUPS, CHUNK, CHUNK), F32),
                   jax.ShapeDtypeStruct((CHUNK, LANES), F32),
                   jax.ShapeDtypeStruct((1, SG_WIDTH), F32), jax.ShapeDtypeStruct((1, SG_WIDTH), F32)],
        scratch_shapes=[pltpu.VMEM((tm, SG_WIDTH), BF16), pltpu.VMEM((tm, SG_WIDTH), F32),
                        pltpu.VMEM((CHUNK, SG_WIDTH), F32)],
        compiler_params=_cparams())(zuv, zuv, dya, lg, lb, wt, wt_t, bias_l)


def _rope(x, c, s1, s2):
    return x * c + pltpu.roll(x, LANES - 16, 1) * s1 + pltpu.roll(x, 16, 1) * s2


def _rope_t(dy, c, s1, s2):
    return dy * c + pltpu.roll(dy * s1, 16, 1) + pltpu.roll(dy * s2, LANES - 16, 1)


def _mla_prep_fwd(zcq, zckv, zkr, gcq, gckv, qg, kg, wuq, wuk, wuv, rc, rs1, rs2, name, ex=None):
    t = zcq.shape[0]
    tm = _row_tile(t, 256)
    hd = MLA_HEADS * LANES

    def body(zcq_ref, zckv_ref, zkr_ref, gcq_ref, gckv_ref, qg_ref, kg_ref, wuq_ref, wuk_ref, wuv_ref,
             c_ref, s1_ref, s2_ref, q_ref, k_ref, v_ref, cqn_ref, ckvn_ref):
        c, s1, s2 = c_ref[...], s1_ref[...], s2_ref[...]
        xq = zcq_ref[...]
        cqn = (xq * _rstd(xq, MLA_Q_RANK) * gcq_ref[...]).astype(BF16)
        cqn_ref[...] = cqn
        ql = _dot(cqn, wuq_ref[...])
        xk = zckv_ref[...]
        ckvn = (xk * _rstd(xk, MLA_KV_RANK) * gckv_ref[...]).astype(BF16)
        ckvn_ref[...] = ckvn
        kl = _dot(ckvn, wuk_ref[...])
        slot_lane = lax.broadcasted_iota(jnp.int32, (tm, hd), 1) % LANES
        v_ref[...] = jnp.where(slot_lane == V_ONES_LANE, 1.0, _dot(ckvn, wuv_ref[...])).astype(BF16)
        kr = zkr_ref[...]
        for h in range(MLA_HEADS):
            sl = slice(h * LANES, (h + 1) * LANES)
            qh = ql[:, sl]
            q_ref[:, sl] = (_rope(qh * _rstd(qh, MLA_QK) * qg_ref[...], c, s1, s2) * ATTN_SCALE2).astype(BF16)
            kh = kl[:, sl] + kr
            k_ref[:, sl] = _rope(kh * _rstd(kh, MLA_QK) * kg_ref[...], c, s1, s2).astype(BF16)

    row = lambda n: pl.BlockSpec((tm, n), lambda i: (i, 0))
    full = lambda a: pl.BlockSpec(a.shape, lambda i: (0, 0))
    return _call_with_exchange(
        ex, body, name, (t // tm,),
        [row(MLA_Q_RANK), row(MLA_KV_RANK), row(LANES), full(gcq), full(gckv), full(qg), full(kg),
         full(wuq), full(wuk), full(wuv), row(LANES), row(LANES), row(LANES)],
        [row(hd), row(hd), row(hd), row(MLA_Q_RANK), row(MLA_KV_RANK)],
        [jax.ShapeDtypeStruct((t, hd), BF16)] * 3
        + [jax.ShapeDtypeStruct((t, MLA_Q_RANK), BF16), jax.ShapeDtypeStruct((t, MLA_KV_RANK), BF16)],
        [], (zcq, zckv, zkr, gcq, gckv, qg, kg, wuq, wuk, wuv, rc, rs1, rs2))


def _mla_prep_bwd(zcq, zckv, zkr, gcq, gckv, qg, kg, wuq, wuk, wuv, rc, rs1, rs2, dq, dk, dv, name):
    t = zcq.shape[0]
    tm = _row_tile(t, 256)
    hd = MLA_HEADS * LANES

    def body(zcq_ref, zckv_ref, zkr_ref, gcq_ref, gckv_ref, qg_ref, kg_ref, wuq_ref, wuk_ref, wuv_ref,
             c_ref, s1_ref, s2_ref, dq_ref, dk_ref, dv_ref,
             dzcq_ref, dzckv_ref, dzkr_ref, dql_ref, dkl_ref, dgcq_ref, dgckv_ref, dqg_ref, dkg_ref):
        @pl.when(pl.program_id(0) == 0)
        def _():
            for ref in (dgcq_ref, dgckv_ref, dqg_ref, dkg_ref):
                ref[...] = jnp.zeros_like(ref)

        c, s1, s2 = c_ref[...], s1_ref[...], s2_ref[...]
        qgv, kgv = qg_ref[...], kg_ref[...]
        xq = zcq_ref[...]
        rq = _rstd(xq, MLA_Q_RANK)
        ql = _dot((xq * rq * gcq_ref[...]).astype(BF16), wuq_ref[...])
        xk = zckv_ref[...]
        rk = _rstd(xk, MLA_KV_RANK)
        kl = _dot((xk * rk * gckv_ref[...]).astype(BF16), wuk_ref[...])
        kr = zkr_ref[...]
        dqg_acc = jnp.zeros((tm, LANES), F32)
        dkg_acc = jnp.zeros((tm, LANES), F32)
        dkr = jnp.zeros((tm, LANES), F32)
        for h in range(MLA_HEADS):
            sl = slice(h * LANES, (h + 1) * LANES)
            qh = ql[:, sl]
            dqh, dgr = _rms_vjp(qh, _rstd(qh, MLA_QK), qgv, _rope_t(dq_ref[:, sl], c, s1, s2), MLA_QK)
            dql_ref[:, sl] = dqh.astype(BF16)
            dqg_acc += dgr
            kh = kl[:, sl] + kr
            dkh, dgr = _rms_vjp(kh, _rstd(kh, MLA_QK), kgv, _rope_t(dk_ref[:, sl], c, s1, s2), MLA_QK)
            dkl_ref[:, sl] = dkh.astype(BF16)
            dkg_acc += dgr
            dkr += dkh
        dqg_ref[...] += jnp.sum(dqg_acc, axis=0, keepdims=True)
        dkg_ref[...] += jnp.sum(dkg_acc, axis=0, keepdims=True)
        lane = lax.broadcasted_iota(jnp.int32, (tm, LANES), 1)
        dzkr_ref[...] = jnp.where((lane >= MLA_NOPE) & (lane < MLA_QK), dkr, 0.0).astype(BF16)
        dcqn = _dot_nt(dql_ref[...], wuq_ref[...])
        dx, dgr = _rms_vjp(xq, rq, gcq_ref[...], dcqn, MLA_Q_RANK)
        dzcq_ref[...] = dx.astype(BF16)
        dgcq_ref[...] += jnp.sum(dgr, axis=0, keepdims=True)
        dckvn = _dot_nt(dkl_ref[...], wuk_ref[...]) + _dot_nt(dv_ref[...].astype(BF16), wuv_ref[...])
        dx, dgr = _rms_vjp(xk, rk, gckv_ref[...], dckvn, MLA_KV_RANK)
        dzckv_ref[...] = dx.astype(BF16)
        dgckv_ref[...] += jnp.sum(dgr, axis=0, keepdims=True)

    row = lambda n: pl.BlockSpec((tm, n), lambda i: (i, 0))
    full = lambda a: pl.BlockSpec(a.shape, lambda i: (0, 0))
    vec = lambda n: pl.BlockSpec((1, n), lambda i: (0, 0))
    return pl.pallas_call(
        body, name=name, grid=(t // tm,),
        in_specs=[row(MLA_Q_RANK), row(MLA_KV_RANK), row(LANES), full(gcq), full(gckv), full(qg), full(kg),
                  full(wuq), full(wuk), full(wuv), row(LANES), row(LANES), row(LANES), row(hd), row(hd), row(hd)],
        out_specs=[row(MLA_Q_RANK), row(MLA_KV_RANK), row(LANES), row(hd), row(hd),
                   vec(MLA_Q_RANK), vec(MLA_KV_RANK), vec(LANES), vec(LANES)],
        out_shape=[jax.ShapeDtypeStruct((t, MLA_Q_RANK), BF16), jax.ShapeDtypeStruct((t, MLA_KV_RANK), BF16),
                   jax.ShapeDtypeStruct((t, LANES), BF16), jax.ShapeDtypeStruct((t, hd), BF16),
                   jax.ShapeDtypeStruct((t, hd), BF16), jax.ShapeDtypeStruct((1, MLA_Q_RANK), F32),
                   jax.ShapeDtypeStruct((1, MLA_KV_RANK), F32), jax.ShapeDtypeStruct((1, LANES), F32),
                   jax.ShapeDtypeStruct((1, LANES), F32)],
        compiler_params=_cparams(),
    )(zcq, zckv, zkr, gcq, gckv, qg, kg, wuq, wuk, wuv, rc, rs1, rs2, dq, dk, dv)


def _attn_tiles(t):
    tq = 512 if t >= 2048 else 128
    return tq, min(t, 8 * tq), min(t, 4 * tq)


def _causal_keep(tq, nk, i, j, tk):
    row = lax.broadcasted_iota(jnp.int32, (tq, nk), 0)
    col = lax.broadcasted_iota(jnp.int32, (tq, nk), 1)
    return (col - row) <= (i * tq - j * tk)


def _causal_keep_t(tq, nk, i, j, tk):
    key = lax.broadcasted_iota(jnp.int32, (nk, tq), 0)
    qry = lax.broadcasted_iota(jnp.int32, (nk, tq), 1)
    return (key - qry) <= (i * tq - j * tk)


ATTN_FWD_HEADS_PER_STEP = 2
ATTN_BWD_HEADS_PER_STEP = 2


def _attn_fwd(q, k, v, name, ex=None):
    t, hd = q.shape
    hp = ATTN_FWD_HEADS_PER_STEP
    tq, tk, _ = _attn_tiles(t)
    pairs = [(i, j) for i in range(t // tq) for j in range(((i + 1) * tq - 1) // tk + 1)]
    ii = np.array([p[0] for p in pairs], np.int32)
    jj = np.array([p[1] for p in pairs], np.int32)

    def body(ii_ref, jj_ref, q_ref, k_ref, v_ref, o_ref, lse_ref, m_scr, acc_scr):
        s_id = pl.program_id(1)
        i, j = ii_ref[s_id], jj_ref[s_id]
        last = j == ((i + 1) * tq - 1) // tk
        ones_lane = lax.broadcasted_iota(jnp.int32, (tq, LANES), 1) == V_ONES_LANE

        @pl.when(j == 0)
        def _():
            m_scr[...] = jnp.full_like(m_scr, NEG)
            acc_scr[...] = jnp.zeros_like(acc_scr)

        def step(masked, nk):
            for hh in range(hp):
                sl = slice(hh * LANES, (hh + 1) * LANES)
                s = _dot_nt(q_ref[:, sl], k_ref[:nk, sl])
                if masked:
                    s = jnp.where(_causal_keep(tq, nk, i, j, tk), s, NEG)
                m_prev = m_scr[hh]
                m_new = jnp.maximum(m_prev, jnp.max(s, axis=1, keepdims=True))
                p = jnp.exp2(s - m_new)
                alpha = jnp.exp2(m_prev - m_new)
                acc = alpha * acc_scr[:, sl] + _dot(p.astype(BF16), v_ref[:nk, sl])
                if masked:
                    l_new = jnp.sum(jnp.where(ones_lane, acc, 0.0), axis=1, keepdims=True)
                    o_ref[:, sl] = (acc / l_new).astype(BF16)
                    lse_ref[:, sl] = jnp.broadcast_to(m_new + jnp.log(l_new) * LOG2E, (tq, LANES))
                else:
                    acc_scr[:, sl] = acc
                    m_scr[hh] = m_new

        @pl.when(jnp.logical_not(last))
        def _():
            step(False, tk)

        r = (((i + 1) * tq - 1) % tk) // tq
        for rr in range(tk // tq):
            @pl.when(last & (r == rr))
            def _():
                step(True, (rr + 1) * tq)

    w = hp * LANES
    qspec = pl.BlockSpec((tq, w), lambda h, s, ii_r, jj_r: (ii_r[s], h))
    kspec = pl.BlockSpec((tk, w), lambda h, s, ii_r, jj_r: (jj_r[s], h))
    return _call_with_exchange(
        ex, body, name, (hd // w, len(pairs)), [qspec, kspec, kspec], [qspec, qspec],
        [jax.ShapeDtypeStruct((t, hd), BF16), jax.ShapeDtypeStruct((t, hd), F32)],
        [pltpu.VMEM((hp, tq, 1), F32), pltpu.VMEM((tq, w), F32)], (q, k, v),
        prefetch=(jnp.asarray(ii), jnp.asarray(jj)))


def _attn_bwd_rows(o, lse, do, name):
    t, hd = o.shape
    heads = hd // LANES
    tm = _row_tile(t, 512)

    def body(o_ref, lse_ref, do_ref, out_ref):
        lane = lax.broadcasted_iota(jnp.int32, (tm, LANES), 1)
        acc = jnp.zeros((tm, LANES), F32)
        for h in range(heads):
            sl = slice(h * LANES, (h + 1) * LANES)
            delta = jnp.sum(do_ref[:, sl].astype(F32) * o_ref[:, sl].astype(F32), axis=1, keepdims=True)
            acc = jnp.where(lane == h, delta, acc)
            acc = jnp.where(lane == heads + h, lse_ref[:, sl], acc)
        out_ref[...] = acc

    row = pl.BlockSpec((tm, hd), lambda i: (i, 0))
    cols = pl.pallas_call(
        body, name=name, grid=(t // tm,), in_specs=[row, row, row],
        out_specs=pl.BlockSpec((tm, LANES), lambda i: (i, 0)),
        out_shape=jax.ShapeDtypeStruct((t, LANES), F32), compiler_params=_cparams())(o, lse, do)
    rows = cols.T
    return rows[:heads].reshape(heads, 1, t), rows[heads:2 * heads].reshape(heads, 1, t)


def _attn_bwd(q, k, v, delta_rows, lse_rows, do, name):
    t, hd = q.shape
    hp = ATTN_BWD_HEADS_PER_STEP
    tq, _, tk = _attn_tiles(t)
    nq = t // tq
    pairs = [(i, j) for j in range(t // tk) for i in range((j * tk) // tq, nq)]
    ii = np.array([p[0] for p in pairs], np.int32)
    jj = np.array([p[1] for p in pairs], np.int32)

    def body(jj_ref, ii_ref, q_ref, k_ref, v_ref, delta_ref, lse_ref, do_ref, dq_ref, dk_ref, dv_ref,
             dk_scr, dv_scr):
        s_id = pl.program_id(1)
        i, j = ii_ref[s_id], jj_ref[s_id]

        @pl.when(s_id == 0)
        def _():
            dq_ref[...] = jnp.zeros_like(dq_ref)

        @pl.when(i == (j * tk) // tq)
        def _():
            dk_scr[...] = jnp.zeros_like(dk_scr)
            dv_scr[...] = jnp.zeros_like(dv_scr)

        rows = pl.ds(pl.multiple_of(i * tq, tq), tq)

        def step(masked, nk):
            for hh in range(hp):
                sl = slice(hh * LANES, (hh + 1) * LANES)
                qv, kv, dov = q_ref[:, sl], k_ref[:nk, sl], do_ref[:, sl]
                st = _dot_nt(kv, qv)
                if masked:
                    st = jnp.where(_causal_keep_t(tq, nk, i, j, tk), st, NEG)
                pt = jnp.exp2(st - lse_ref[hh])
                dv_scr[:nk, sl] += _dot(pt.astype(BF16), dov)
                dpt = _dot_nt(v_ref[:nk, sl], dov)
                dst = (pt * (dpt - delta_ref[hh]) * ATTN_SCALE).astype(BF16)
                dk_scr[:nk, sl] += _dot(dst, qv)
                dq_ref[rows, sl] += _dot_tn(dst, kv)

        seen = jnp.minimum((i + 1) * tq - j * tk, tk)
        for nk in range(tq, tk + 1, tq):
            @pl.when((seen == nk) & ((i + 1) * tq - j * tk <= tk))
            def _():
                step(True, nk)

        @pl.when((i + 1) * tq - j * tk > tk)
        def _():
            step(False, tk)

        @pl.when(i == nq - 1)
        def _():
            dk_ref[...] = dk_scr[...] * (1.0 / ATTN_SCALE2)
            dv_ref[...] = dv_scr[...]

    w = hp * LANES
    qspec = pl.BlockSpec((tq, w), lambda h, s, jj_r, ii_r: (ii_r[s], h))
    kspec = pl.BlockSpec((tk, w), lambda h, s, jj_r, ii_r: (jj_r[s], h))
    rspec = pl.BlockSpec((hp, 1, tq), lambda h, s, jj_r, ii_r: (h, 0, ii_r[s]))
    return pl.pallas_call(
        body, name=name,
        grid_spec=pltpu.PrefetchScalarGridSpec(
            num_scalar_prefetch=2, grid=(hd // w, len(pairs)),
            in_specs=[qspec, kspec, kspec, rspec, rspec, qspec],
            out_specs=[pl.BlockSpec((t, w), lambda h, s, jj_r, ii_r: (0, h)), kspec, kspec],
            scratch_shapes=[pltpu.VMEM((tk, w), F32), pltpu.VMEM((tk, w), F32)]),
        out_shape=[jax.ShapeDtypeStruct((t, hd), F32)] * 3,
        compiler_params=_cparams())(jnp.asarray(jj), jnp.asarray(ii), q, k, v, delta_rows, lse_rows, do)


MEM_W = MEM_HEADS * LANES


def _mem_kv_fwd(mem, gmem, wkv, kg, name):
    m, d = mem.shape

    def body(mem_ref, g_ref, w_ref, kg_ref, k_ref, v_ref, mn_ref):
        xv = mem_ref[...]
        mn = (xv * _rstd(xv, d) * g_ref[...]).astype(BF16)
        mn_ref[...] = mn
        kvm = _dot(mn, w_ref[...])
        v_ref[...] = kvm[:, MEM_W:].astype(BF16)
        for h in range(MEM_HEADS):
            sl = slice(h * LANES, (h + 1) * LANES)
            kh = kvm[:, sl]
            k_ref[:, sl] = (kh * _rstd(kh, LANES) * kg_ref[...]).astype(BF16)

    full = lambda a: pl.BlockSpec(a.shape, lambda i: (0, 0))
    return pl.pallas_call(
        body, name=name, grid=(1,), in_specs=[full(mem), full(gmem), full(wkv), full(kg)],
        out_specs=[pl.BlockSpec((m, MEM_W), lambda i: (0, 0)), pl.BlockSpec((m, MEM_W), lambda i: (0, 0)),
                   pl.BlockSpec((m, d), lambda i: (0, 0))],
        out_shape=[jax.ShapeDtypeStruct((m, MEM_W), BF16), jax.ShapeDtypeStruct((m, MEM_W), BF16),
                   jax.ShapeDtypeStruct((m, d), BF16)],
        compiler_params=_cparams())(mem, gmem, wkv, kg)


def _mem_softmax(qn, kh):
    s = _dot_nt(qn, kh) * (LANES ** -0.5)
    e = jnp.exp(s - jnp.max(s, axis=1, keepdims=True))
    return e / jnp.sum(e, axis=1, keepdims=True)


def _mem_attn_fwd(zqm, qg, km, vm, name):
    t = zqm.shape[0]
    tm = _row_tile(t, 512)

    def body(q_ref, qg_ref, k_ref, v_ref, o_ref):
        for h in range(MEM_HEADS):
            sl = slice(h * LANES, (h + 1) * LANES)
            qh = q_ref[:, sl]
            qn = (qh * _rstd(qh, LANES) * qg_ref[...]).astype(BF16)
            p = _mem_softmax(qn, k_ref[:, sl])
            o_ref[:, sl] = _dot(p.astype(BF16), v_ref[:, sl]).astype(BF16)

    row = pl.BlockSpec((tm, MEM_W), lambda i: (i, 0))
    full = lambda a: pl.BlockSpec(a.shape, lambda i: (0, 0))
    return pl.pallas_call(
        body, name=name, grid=(t // tm,), in_specs=[row, full(qg), full(km), full(vm)], out_specs=row,
        out_shape=jax.ShapeDtypeStruct((t, MEM_W), BF16), compiler_params=_cparams())(zqm, qg, km, vm)


def _mem_attn_bwd(zqm, dyc, qg, km, vm, name):
    t = zqm.shape[0]
    m = km.shape[0]
    tm = _row_tile(t, 256)

    def body(q_ref, dy_ref, qg_ref, k_ref, v_ref, dz_ref, dk_ref, dv_ref, dqg_ref):
        @pl.when(pl.program_id(0) == 0)
        def _():
            dk_ref[...] = jnp.zeros_like(dk_ref)
            dv_ref[...] = jnp.zeros_like(dv_ref)
            dqg_ref[...] = jnp.zeros_like(dqg_ref)

        qgv = qg_ref[...]
        dqg_acc = jnp.zeros((tm, LANES), F32)
        for h in range(MEM_HEADS):
            sl = slice(h * LANES, (h + 1) * LANES)
            qh = q_ref[:, sl]
            r = _rstd(qh, LANES)
            qn = (qh * r * qgv).astype(BF16)
            kh = k_ref[:, sl]
            p = _mem_softmax(qn, kh)
            dov = dy_ref[:, sl]
            dv_ref[:, sl] += _dot_tn(p.astype(BF16), dov)
            dp = _dot_nt(dov, v_ref[:, sl])
            ds = (p * (dp - jnp.sum(dp * p, axis=1, keepdims=True)) * (LANES ** -0.5)).astype(BF16)
            dk_ref[:, sl] += _dot_tn(ds, qn)
            dqh, dgr = _rms_vjp(qh, r, qgv, _dot(ds, kh), LANES)
            dz_ref[:, sl] = dqh.astype(BF16)
            dqg_acc += dgr
        dqg_ref[...] += jnp.sum(dqg_acc, axis=0, keepdims=True)

    row = pl.BlockSpec((tm, MEM_W), lambda i: (i, 0))
    full = lambda a: pl.BlockSpec(a.shape, lambda i: (0, 0))
    acc = pl.BlockSpec((m, MEM_W), lambda i: (0, 0))
    return pl.pallas_call(
        body, name=name, grid=(t // tm,), in_specs=[row, row, full(qg), full(km), full(vm)],
        out_specs=[row, acc, acc, pl.BlockSpec((1, LANES), lambda i: (0, 0))],
        out_shape=[jax.ShapeDtypeStruct((t, MEM_W), BF16), jax.ShapeDtypeStruct((m, MEM_W), F32),
                   jax.ShapeDtypeStruct((m, MEM_W), F32), jax.ShapeDtypeStruct((1, LANES), F32)],
        compiler_params=_cparams())(zqm, dyc, qg, km, vm)


def _mem_kv_bwd(mem, gmem, wkv, kg, dkn, dvm, name):
    m, d = mem.shape

    def body(mem_ref, g_ref, w_ref, kg_ref, dk_ref, dv_ref, dw_ref, dkg_ref, dg_ref, dkv_scr):
        xv = mem_ref[...]
        r = _rstd(xv, d)
        mn = (xv * r * g_ref[...]).astype(BF16)
        kvm = _dot(mn, w_ref[...])
        dkv_scr[:, MEM_W:] = dv_ref[...].astype(BF16)
        dkg_acc = jnp.zeros((m, LANES), F32)
        for h in range(MEM_HEADS):
            sl = slice(h * LANES, (h + 1) * LANES)
            kh = kvm[:, sl]
            dkh, dgr = _rms_vjp(kh, _rstd(kh, LANES), kg_ref[...], dk_ref[:, sl], LANES)
            dkv_scr[:, sl] = dkh.astype(BF16)
            dkg_acc += dgr
        dkg_ref[...] = jnp.sum(dkg_acc, axis=0, keepdims=True)
        dkv = dkv_scr[...]
        dw_ref[...] = _dot_tn(mn, dkv)
        dmn = _dot_nt(dkv, w_ref[...])
        dg_ref[...] = jnp.sum(dmn * xv * r, axis=0, keepdims=True)

    full = lambda a: pl.BlockSpec(a.shape, lambda i: (0, 0))
    return pl.pallas_call(
        body, name=name, grid=(1,),
        in_specs=[full(mem), full(gmem), full(wkv), full(kg), full(dkn), full(dvm)],
        out_specs=[pl.BlockSpec((d, 2 * MEM_W), lambda i: (0, 0)), pl.BlockSpec((1, LANES), lambda i: (0, 0)),
                   pl.BlockSpec((1, d), lambda i: (0, 0))],
        out_shape=[jax.ShapeDtypeStruct((d, 2 * MEM_W), F32), jax.ShapeDtypeStruct((1, LANES), F32),
                   jax.ShapeDtypeStruct((1, d), F32)],
        scratch_shapes=[pltpu.VMEM((m, 2 * MEM_W), BF16)],
        compiler_params=_cparams())(mem, gmem, wkv, kg, dkn, dvm)


def _merge_fwd(x1, ya, yb, yc, zg, bg, wa, wb, wc, wo, name):
    t, d = x1.shape
    tm = _row_tile(t, 256)

    def body(x_ref, ya_ref, yb_ref, yc_ref, zg_ref, bg_ref, wa_ref, wb_ref, wc_ref, wo_ref,
             x2_ref, mg_ref, pa_ref, pb_ref, pc_ref):
        merged = None
        for k, (y_ref, w_ref, p_ref) in enumerate(
                ((ya_ref, wa_ref, pa_ref), (yb_ref, wb_ref, pb_ref), (yc_ref, wc_ref, pc_ref))):
            sl = slice(k * d, (k + 1) * d)
            pr = _dot(y_ref[...], w_ref[...])
            p_ref[...] = pr.astype(BF16)
            term = jax.nn.sigmoid(zg_ref[:, sl] + bg_ref[:, sl]) * pr
            merged = term if merged is None else merged + term
        mb = merged.astype(BF16)
        mg_ref[...] = mb
        x2_ref[...] = x_ref[...] + _dot(mb, wo_ref[...])

    row = lambda n: pl.BlockSpec((tm, n), lambda i: (i, 0))
    full = lambda a: pl.BlockSpec(a.shape, lambda i: (0, 0))
    return pl.pallas_call(
        body, name=name, grid=(t // tm,),
        in_specs=[row(d), row(ya.shape[1]), row(yb.shape[1]), row(yc.shape[1]), row(3 * d), full(bg),
                  full(wa), full(wb), full(wc), full(wo)],
        out_specs=[row(d)] * 5,
        out_shape=[jax.ShapeDtypeStruct((t, d), F32)] + [jax.ShapeDtypeStruct((t, d), BF16)] * 4,
        compiler_params=_cparams())(x1, ya, yb, yc, zg, bg, wa, wb, wc, wo)


def _merge_bwd(dx2, pa, pb, pc, zg, bg, wa, wb, wc, wo, name, ex=None):
    t, d = dx2.shape
    tm = _row_tile(t, 256)

    def body(dx_ref, pa_ref, pb_ref, pc_ref, zg_ref, bg_ref, wa_ref, wb_ref, wc_ref, wo_ref,
             dpa_ref, dpb_ref, dpc_ref, dzg_ref, dbg_ref, dya_ref, dyb_ref, dyc_ref):
        @pl.when(pl.program_id(0) == 0)
        def _():
            dbg_ref[...] = jnp.zeros_like(dbg_ref)

        dm = _dot_nt(dx_ref[...].astype(BF16), wo_ref[...])
        for k, (p_ref, w_ref, dp_ref, dy_ref) in enumerate(
                ((pa_ref, wa_ref, dpa_ref, dya_ref), (pb_ref, wb_ref, dpb_ref, dyb_ref),
                 (pc_ref, wc_ref, dpc_ref, dyc_ref))):
            sl = slice(k * d, (k + 1) * d)
            gate = jax.nn.sigmoid(zg_ref[:, sl] + bg_ref[:, sl])
            dpr = (dm * gate).astype(BF16)
            dp_ref[...] = dpr
            dzg = dm * p_ref[...].astype(F32) * gate * (1.0 - gate)
            dzg_ref[:, sl] = dzg.astype(BF16)
            dbg_ref[:, sl] += jnp.sum(dzg, axis=0, keepdims=True)
            dy_ref[...] = _dot_nt(dpr, w_ref[...]).astype(dy_ref.dtype)

    row = lambda n: pl.BlockSpec((tm, n), lambda i: (i, 0))
    full = lambda a: pl.BlockSpec(a.shape, lambda i: (0, 0))
    na, nb, nc = wa.shape[0], wb.shape[0], wc.shape[0]
    return _call_with_exchange(
        ex, body, name, (t // tm,),
        [row(d), row(d), row(d), row(d), row(3 * d), full(bg), full(wa), full(wb), full(wc), full(wo)],
        [row(d), row(d), row(d), row(3 * d), pl.BlockSpec((1, 3 * d), lambda i: (0, 0)), row(na), row(nb), row(nc)],
        [jax.ShapeDtypeStruct((t, d), BF16)] * 3
        + [jax.ShapeDtypeStruct((t, 3 * d), BF16), jax.ShapeDtypeStruct((1, 3 * d), F32),
           jax.ShapeDtypeStruct((t, na), F32), jax.ShapeDtypeStruct((t, nb), BF16),
           jax.ShapeDtypeStruct((t, nc), BF16)],
        [], (dx2, pa, pb, pc, zg, bg, wa, wb, wc, wo))


def _adamw_math(w, g, m, v):
    bc1 = 1.0 - ADAM_B1 ** ADAM_STEP
    bc2 = 1.0 - ADAM_B2 ** ADAM_STEP
    nm = ADAM_B1 * m + (1.0 - ADAM_B1) * g
    nv = ADAM_B2 * v + (1.0 - ADAM_B2) * (g * g)
    delta = -ADAM_LR * ((nm / bc1) / (jnp.sqrt(nv / bc2) + ADAM_EPS) + ADAM_WD * w)
    return delta, nm, nv


def _div_tile(n, cap, mult):
    best = None
    for cand in range(mult, min(n, cap) + 1, mult):
        if n % cand == 0:
            best = cand
    assert best is not None, (n, cap, mult)
    return best


def _adamw(w, g, m, v, name):
    rows, cols = w.shape
    tr = rows if rows * cols <= 256 * 1024 else _div_tile(rows, 256, 8)

    def body(w_ref, g_ref, m_ref, v_ref, d_ref, nm_ref, nv_ref):
        d_ref[...], nm_ref[...], nv_ref[...] = _adamw_math(w_ref[...], g_ref[...], m_ref[...], v_ref[...])

    blk = pl.BlockSpec((tr, cols), lambda i: (i, 0))
    return pl.pallas_call(
        body, name=name, grid=(rows // tr,), in_specs=[blk] * 4, out_specs=[blk] * 3,
        out_shape=[jax.ShapeDtypeStruct((rows, cols), F32)] * 3, compiler_params=_cparams())(w, g, m, v)


def _adamw_slots(w, slots, m, v, name):
    _, hr, cols = w.shape
    tr = _div_tile(hr, 128, 16)

    def body(w_ref, s_ref, m_ref, v_ref, g_ref, d_ref, nm_ref, nv_ref):
        g = s_ref[0, 0].astype(F32)
        for k in range(1, N_CHIPS):
            g = g + s_ref[0, k].astype(F32)
        g_ref[0] = g
        d_ref[0], nm_ref[0], nv_ref[0] = _adamw_math(w_ref[0], g, m_ref[0], v_ref[0])

    blk = pl.BlockSpec((1, tr, cols), lambda h, i: (h, i, 0))
    return pl.pallas_call(
        body, name=name, grid=(2, hr // tr),
        in_specs=[blk, pl.BlockSpec((1, N_CHIPS, tr, cols), lambda h, i: (h, 0, i, 0)), blk, blk],
        out_specs=[blk] * 4, out_shape=[jax.ShapeDtypeStruct((2, hr, cols), F32)] * 4,
        compiler_params=_cparams())(w, slots, m, v)


ANY = pl.BlockSpec(memory_space=pl.ANY)


def _place():
    x, y, c = lax.axis_index("x"), lax.axis_index("y"), lax.axis_index("c")
    other_chips = [(1 - x, y), (x, 1 - y), (1 - x, 1 - y)]
    return x, y, c, other_chips


def _remote(src, dst, send_sem, recv_sem, to):
    return pltpu.make_async_remote_copy(src_ref=src, dst_ref=dst, send_sem=send_sem, recv_sem=recv_sem,
                                        device_id=to, device_id_type=MESH)


PIECE_BYTES = 384 * 1024


def _row_pieces(half_rows, cols):
    for n in (4, 2):
        if half_rows % (16 * n) == 0 and half_rows * cols * 2 // n >= PIECE_BYTES:
            return [pl.ds(k * (half_rows // n), half_rows // n) for k in range(n)]
    return [pl.ds(0, half_rows)]


def _pieces(arrays, rows_axis):
    return [(w, rows) for w, a in enumerate(arrays) for rows in _row_pieces(a.shape[rows_axis], a.shape[-1])]


def _gather_exchange(shards):
    nw = len(shards)
    pieces = _pieces(shards, 1)
    npc = len(pieces)

    def build(s_refs, g_refs, sems):
        send_sems, recv_sems, local_sems = sems
        x, y, c, chips = _place()
        me = 2 * x + y
        sibling = (x, y, 1 - c)
        mine = [pltpu.make_async_copy(s_refs[w], g_refs[w].at[me], local_sems.at[w]) for w in range(nw)]
        first = [_remote(s_refs[w].at[c, rows], g_refs[w].at[me, c, rows], send_sems.at[k, p], recv_sems.at[k, p],
                         (cx, cy, c)) for k, (cx, cy) in enumerate(chips) for p, (w, rows) in enumerate(pieces)]

        def start():
            for cp in mine + first:
                cp.start()

        arrived = [g_refs[w].at[2 * cx + cy, c, rows] for cx, cy in chips for w, rows in pieces]
        passed = [_remote(slab, slab, send_sems.at[3 + q // npc, q % npc], recv_sems.at[3 + q // npc, q % npc], sibling)
                  for q, slab in enumerate(arrived)]

        def pass_on():
            for q, slab in enumerate(arrived):
                k, p = q // npc, q % npc
                _remote(slab, slab, send_sems.at[k, p], recv_sems.at[k, p], (*chips[k], c)).wait_recv()
                passed[q].start()

        def finish():
            for k, (cx, cy) in enumerate(chips):
                for p, (w, rows) in enumerate(pieces):
                    slab = g_refs[w].at[2 * cx + cy, 1 - c, rows]
                    _remote(slab, slab, send_sems.at[3 + k, p], recv_sems.at[3 + k, p], sibling).wait_recv()
            for cp in first + passed:
                cp.wait_send()
            for cp in mine:
                cp.wait()

        return start, pass_on, finish

    return _Exchange(list(shards), [jax.ShapeDtypeStruct((N_CHIPS,) + s.shape, BF16) for s in shards],
                     [pltpu.SemaphoreType.DMA((6, npc)), pltpu.SemaphoreType.DMA((6, npc)),
                      pltpu.SemaphoreType.DMA((nw,))], build)


def _swap_halves(grads, name):
    nw = len(grads)

    def body(*refs):
        g_refs, sib_refs = refs[:nw], refs[nw:2 * nw]
        send_sems, recv_sems = refs[2 * nw:]
        x, y, c, _ = _place()
        copies = [_remote(g_refs[w].at[s, 1 - c], sib_refs[w].at[s], send_sems.at[s, w], recv_sems.at[s, w],
                          (x, y, 1 - c)) for w in range(nw) for s in range(N_CHIPS)]
        for cp in copies:
            cp.start()
        for cp in copies:
            cp.wait_recv()
        for cp in copies:
            cp.wait_send()

    return pl.pallas_call(
        body, name=name, in_specs=[ANY] * nw, out_specs=[ANY] * nw,
        out_shape=[jax.ShapeDtypeStruct((N_CHIPS,) + g.shape[2:], BF16) for g in grads],
        scratch_shapes=[pltpu.SemaphoreType.DMA((N_CHIPS, nw)), pltpu.SemaphoreType.DMA((N_CHIPS, nw))],
    )(*grads)


def _pair_sum(grad, sib, core, name):
    nchip, _, hr, cols = grad.shape
    tr = _div_tile(hr, 256, 16)

    def body(core_ref, a_ref, b_ref, o_ref):
        o_ref[...] = (a_ref[0].astype(F32) + b_ref[...].astype(F32)).astype(BF16)

    return pl.pallas_call(
        body, name=name,
        grid_spec=pltpu.PrefetchScalarGridSpec(
            num_scalar_prefetch=1, grid=(nchip, hr // tr),
            in_specs=[pl.BlockSpec((1, 1, tr, cols), lambda s, i, core_r: (s, core_r[0], i, 0)),
                      pl.BlockSpec((1, tr, cols), lambda s, i, core_r: (s, i, 0))],
            out_specs=pl.BlockSpec((1, tr, cols), lambda s, i, core_r: (s, i, 0))),
        out_shape=jax.ShapeDtypeStruct((nchip, hr, cols), BF16), compiler_params=_cparams())(core, grad, sib)


def _pair_sum_exchange(sums):
    nw = len(sums)
    pieces = _pieces(sums, 1)
    npc = len(pieces)

    def build(p_refs, o_refs, sems):
        send_sems, recv_sems, local_sems = sems
        x, y, c, chips = _place()
        me = 2 * x + y
        sibling = (x, y, 1 - c)
        mine = [pltpu.make_async_copy(p_refs[w].at[me], o_refs[w].at[c, 3], local_sems.at[w]) for w in range(nw)]
        first = [_remote(p_refs[w].at[2 * cx + cy, rows], o_refs[w].at[c, k, rows], send_sems.at[k, p],
                         recv_sems.at[k, p], (cx, cy, c))
                 for k, (cx, cy) in enumerate(chips) for p, (w, rows) in enumerate(pieces)]

        def start():
            for cp in mine + first:
                cp.start()

        passed = [_remote(o_refs[w].at[c, k, rows], o_refs[w].at[c, k, rows], send_sems.at[3 + k, p],
                          recv_sems.at[3 + k, p], sibling) for k in range(N_CHIPS) for p, (w, rows) in enumerate(pieces)]

        def pass_on():
            for k in range(N_CHIPS):
                own_waited = set()
                for p, (w, rows) in enumerate(pieces):
                    if k < 3:
                        first[k * npc + p].wait_recv()
                    elif w not in own_waited:
                        mine[w].wait()
                        own_waited.add(w)
                    passed[k * npc + p].start()

        def finish():
            for k in range(N_CHIPS):
                for p, (w, rows) in enumerate(pieces):
                    slab = o_refs[w].at[1 - c, k, rows]
                    _remote(slab, slab, send_sems.at[3 + k, p], recv_sems.at[3 + k, p], sibling).wait_recv()
            for cp in first + passed:
                cp.wait_send()

        return start, pass_on, finish

    return _Exchange(list(sums), [jax.ShapeDtypeStruct((2,) + p.shape, BF16) for p in sums],
                     [pltpu.SemaphoreType.DMA((7, npc)), pltpu.SemaphoreType.DMA((7, npc)),
                      pltpu.SemaphoreType.DMA((nw,))], build)


def _small_sum_exchange(vec):
    m_per, n = vec.shape

    def build(ins, outs, scr):
        (x_ref,), (out_ref,) = ins, outs
        gath_ref, sum_ref, send_sems, recv_sems, local_sem, out_sem = scr
        x, y, c, chips = _place()
        me, sibling = (x, y, c), (x, y, 1 - c)

        def rows(px, py, pc):
            return gath_ref.at[pl.ds((4 * px + 2 * py + pc) * m_per, m_per), :]

        def copy(k, block, to, src=None):
            return pltpu.make_async_remote_copy(
                src_ref=rows(*block) if src is None else src, dst_ref=rows(*block),
                send_sem=send_sems.at[k], recv_sem=recv_sems.at[k], device_id=to, device_id_type=MESH)

        mine = pltpu.make_async_copy(x_ref, rows(*me), local_sem)
        first = [copy(0, me, sibling, src=x_ref)] + [copy(1 + j, me, (*chip, c), src=x_ref)
                                                     for j, chip in enumerate(chips)]

        def start():
            for cp in [mine] + first:
                cp.start()

        passed = [copy(4 + j, (*chip, c), sibling) for j, chip in enumerate(chips)]

        def pass_on():
            for j, chip in enumerate(chips):
                copy(1 + j, (*chip, c), me).wait_recv()
                passed[j].start()

        def finish():
            copy(0, sibling, me).wait_recv()
            for j, chip in enumerate(chips):
                copy(4 + j, (*chip, 1 - c), me).wait_recv()
            for cp in first + passed:
                cp.wait_send()
            mine.wait()
            acc = gath_ref[pl.ds(0, m_per), :]
            for k in range(1, N_DEV):
                acc = acc + gath_ref[pl.ds(k * m_per, m_per), :]
            sum_ref[...] = acc
            done = pltpu.make_async_copy(sum_ref, out_ref, out_sem)
            done.start()
            done.wait()

        return start, pass_on, finish

    return _Exchange([vec], [jax.ShapeDtypeStruct((m_per, n), F32)],
                     [pltpu.VMEM((N_DEV * m_per, n), F32), pltpu.VMEM((m_per, n), F32), pltpu.SemaphoreType.DMA((7,)),
                      pltpu.SemaphoreType.DMA((7,)), pltpu.SemaphoreType.DMA, pltpu.SemaphoreType.DMA], build)


def _pack_small(vals, tail=()):
    flat = jnp.concatenate([vals[name].reshape(-1).astype(F32) for name, _ in SMALL] + [v.reshape(1) for v in tail])
    flat = jnp.pad(flat, (0, SMALL_ROWS * LANES - flat.shape[0]))
    return flat.reshape(SMALL_ROWS, LANES)


def _unpack_small(packed):
    flat = packed.reshape(-1)
    out, off = {}, 0
    for name, shape in SMALL:
        n = int(np.prod(shape))
        out[name] = flat[off:off + n].reshape(shape)
        off += n
    return out


def _head_pad_cols(w, heads, real):
    k = w.shape[0]
    return jnp.pad(w.reshape(k, heads, real), ((0, 0), (0, 0), (0, LANES - real))).reshape(k, heads * LANES)


def _rope_tables(positions):
    half = MLA_ROPE // 2
    inv = ROPE_BASE ** (-jnp.arange(half, dtype=F32) / half)
    ang = positions.astype(F32)[:, None] * inv
    cos, sin = jnp.cos(ang), jnp.sin(ang)
    t = positions.shape[0]
    z = lambda n: jnp.zeros((t, n), F32)
    rc = jnp.concatenate([jnp.ones((t, MLA_NOPE), F32), cos, cos, z(LANES - MLA_QK)], axis=1)
    rs1 = jnp.concatenate([z(MLA_NOPE), -sin, z(LANES - MLA_NOPE - half)], axis=1)
    rs2 = jnp.concatenate([z(MLA_NOPE + half), sin, z(LANES - MLA_QK)], axis=1)
    return rc, rs1, rs2


FFN1_WEIGHTS = ("ffn1_w_gu", "ffn1_w_down")
FFN2_WEIGHTS = ("ffn2_w_gu", "ffn2_w_down")
MIXER_WEIGHTS = tuple(n for n, *_ in SHARDED if n not in FFN1_WEIGHTS + FFN2_WEIGHTS)
SHARD_SHAPE = {n: (r, c, kind) for n, r, c, kind in SHARDED}


def _from_blocks(name, gathered):
    r, c, kind = SHARD_SHAPE[name]
    blk = gathered.reshape(N_CHIPS, r, c)
    return blk, (blk.transpose(1, 0, 2).reshape(r, N_CHIPS * c) if kind == "col" else blk.reshape(N_CHIPS * r, c))


def _grad_pair_sums(names, gw, core, tag):
    by_owner = []
    for name in names:
        r, c, kind = SHARD_SHAPE[name]
        if gw[name].dtype == BF16:
            blk = gw[name]
        elif kind == "col":
            blk = gw[name].reshape(r, N_CHIPS, c).transpose(1, 0, 2)
        else:
            blk = gw[name].reshape(N_CHIPS, r, c)
        by_owner.append(blk.astype(BF16).reshape(N_CHIPS, 2, r // 2, c))
    received = _swap_halves(by_owner, "grad_swap_" + tag)
    return [_pair_sum(g, s, core, "pair_sum_" + n) for g, s, n in zip(by_owner, received, names)]


def _device_step(x, mem, positions, tgt, small, shards, core):
    d = D_MODEL
    g_ffn1, g_mix, g_ffn2 = small["ffn1_norm"], small["mix_norm"], small["ffn2_norm"]
    big = {}
    for name, g in zip(FFN1_WEIGHTS, _run_exchange(_gather_exchange([shards[n] for n in FFN1_WEIGHTS]), "gather_ffn1")):
        big[name + "#blocks"], big[name] = _from_blocks(name, g)
    wgu1, wd1 = big["ffn1_w_gu#blocks"], big["ffn1_w_down"].reshape(2, FF_TILE, d)
    x1, gpre1, upre1, h, *rest = _ffn_fwd(x, g_ffn1, wgu1, wd1, "ffn1_fwd", next_gain=g_mix,
                                          ex=_gather_exchange([shards[n] for n in MIXER_WEIGHTS]))
    for name, g in zip(MIXER_WEIGHTS, rest):
        big[name + "#blocks"], big[name] = _from_blocks(name, g)
    w_in = big["w_in"]
    w_uv_, w_cq, w_ckv = w_in[:, :COL_CQ], w_in[:, COL_CQ:COL_CKV], w_in[:, COL_CKV:COL_KR]
    w_kr = jnp.pad(w_in[:, COL_KR:COL_QM], ((0, 0), (MLA_NOPE, LANES - MLA_QK)))
    w_qm, w_g = w_in[:, COL_QM:COL_GATE], w_in[:, COL_GATE:]
    segs = (w_uv_, w_cq, w_ckv, w_kr, w_qm, w_g)
    wuq = _head_pad_cols(big["mla_w_uq"], MLA_HEADS, MLA_QK)
    ukv = big["mla_w_ukv"].reshape(MLA_KV_RANK, MLA_HEADS, 2, MLA_NOPE)
    wuk = _head_pad_cols(ukv[:, :, 0].reshape(MLA_KV_RANK, -1), MLA_HEADS, MLA_NOPE)
    wuv = _head_pad_cols(ukv[:, :, 1].reshape(MLA_KV_RANK, -1), MLA_HEADS, MLA_NOPE)
    wkv = big["mem_w_kv"]
    wa, wc, wo = big["w_branch_a"], big["w_branch_c"], big["w_out"]
    wb = jnp.pad(big["w_branch_b"].reshape(MLA_HEADS, MLA_NOPE, d),
                 ((0, 0), (0, LANES - MLA_NOPE), (0, 0))).reshape(MLA_HEADS * LANES, d)
    qg = jnp.pad(small["mla_q_norm"], ((0, 0), (0, LANES - MLA_QK)))
    kg = jnp.pad(small["mla_k_norm"], ((0, 0), (0, LANES - MLA_QK)))
    causal = jnp.tril(jnp.ones((CHUNK, CHUNK), bool))
    wt_f = jnp.where(causal[None], small["sg_w"][0], 0.0)
    wt, wt_t = wt_f.astype(BF16), wt_f.transpose(0, 2, 1).astype(BF16)
    bias_l = jnp.repeat(small["sg_b"][0].T, 64, axis=1)
    rc, rs1, rs2 = _rope_tables(positions)

    zuv, zcq, zckv, zkr, zqm, zg = _mm_cols(h, segs, [F32] * 5 + [BF16], "in_proj")
    ya = _sgu_fwd(zuv, small["sg_ln_g"], small["sg_ln_b"], wt, bias_l, "sgu_fwd")
    q, k, v, cqn, ckvn = _mla_prep_fwd(zcq, zckv, zkr, small["mla_cq_norm"], small["mla_ckv_norm"], qg, kg,
                                       wuq, wuk, wuv, rc, rs1, rs2, "mla_prep_fwd")
    yb, lse, *rest = _attn_fwd(q, k, v, "mla_attn_fwd", ex=_gather_exchange([shards[n] for n in FFN2_WEIGHTS]))
    for name, g in zip(FFN2_WEIGHTS, rest):
        big[name + "#blocks"], big[name] = _from_blocks(name, g)
    wgu2, wd2 = big["ffn2_w_gu#blocks"], big["ffn2_w_down"].reshape(2, FF_TILE, d)
    km, vm, memn = _mem_kv_fwd(mem, small["mem_norm"], wkv, small["mem_k_norm"], "mem_kv_fwd")
    yc = _mem_attn_fwd(zqm, small["mem_q_norm"], km, vm, "mem_attn_fwd")
    x2, merged, pa, pb, pc = _merge_fwd(x1, ya, yb, yc, zg, small["b_gate"], wa, wb, wc, wo, "merge_fwd")
    dy, loss_row, gpre2, upre2 = _ffn_fwd(x2, g_ffn2, wgu2, wd2, "ffn2_fwd", target=tgt)

    gw, gs, slots = {}, {}, {}

    def ffn_grads(prefix, xin, gain, dyin, gpre, upre, wgu, wd, ex=None, ex_names=(), last=False):
        dx, dgain, xn, dgt, dup, act, *got = _ffn_bwd(xin, gain, dyin, gpre, upre, wgu, wd, prefix + "_bwd", ex=ex)
        slots.update(zip(ex_names, got))
        gs[prefix + "_norm"] = dgain
        gw[prefix + "_w_gu"] = jnp.concatenate(
            [_mm_tn(xn, dgt, prefix + "_dwg", col_blocks=True, out_dtype=BF16),
             _mm_tn(xn, dup, prefix + "_dwu", col_blocks=True, out_dtype=BF16)], axis=0)
        rows_down = SHARD_SHAPE[prefix + "_w_down"][0]
        if last:
            small_sum = _small_sum_exchange(_pack_small(gs, tail=[loss_row[0, 0]]))
            dwd, summed = _mm_tn(act, dyin, prefix + "_dwd", scale=0.5, ex=small_sum, out_dtype=BF16)
            gw[prefix + "_w_down"] = dwd.reshape(N_CHIPS, rows_down, d)
            return dx, summed
        gw[prefix + "_w_down"] = _mm_tn(act, dyin, prefix + "_dwd", scale=0.5, out_dtype=BF16).reshape(
            N_CHIPS, rows_down, d)
        return dx

    dx2 = ffn_grads("ffn2", x2, g_ffn2, dy, gpre2, upre2, wgu2, wd2)
    ffn2_sums = _pair_sum_exchange(_grad_pair_sums(FFN2_WEIGHTS, gw, core, "ffn2"))
    dpa, dpb, dpc, dzg, dbg, dya, dyb, dyc, *got = _merge_bwd(dx2, pa, pb, pc, zg, small["b_gate"], wa, wb, wc, wo,
                                                              "merge_bwd", ex=ffn2_sums)
    slots.update(zip(FFN2_WEIGHTS, got))
    gs["b_gate"] = dbg
    gw["w_out"] = _mm_tn(merged, dx2, "dw_out")
    gw["w_branch_a"] = _mm_tn(ya, dpa, "dw_branch_a")
    gw["w_branch_b"] = _mm_tn(yb, dpb, "dw_branch_b").reshape(MLA_HEADS, LANES, d)[:, :MLA_NOPE].reshape(-1, d)
    gw["w_branch_c"] = _mm_tn(yc, dpc, "dw_branch_c")

    dzuv, dwt, dbl, dlg, dlb = _sgu_bwd(zuv, dya, small["sg_ln_g"], small["sg_ln_b"], wt, wt_t, bias_l, "sgu_bwd")
    gs["sg_w"], gs["sg_b"] = dwt[None], dbl[:, :SG_GROUPS].T[None]
    gs["sg_ln_g"], gs["sg_ln_b"] = dlg, dlb

    delta_rows, lse_rows = _attn_bwd_rows(yb, lse, dyb, "mla_attn_bwd_rows")
    dq, dk, dv = _attn_bwd(q, k, v, delta_rows, lse_rows, dyb, "mla_attn_bwd")
    dzcq, dzckv, dzkr, dql, dkl, dgcq, dgckv, dqg, dkg = _mla_prep_bwd(
        zcq, zckv, zkr, small["mla_cq_norm"], small["mla_ckv_norm"], qg, kg, wuq, wuk, wuv, rc, rs1, rs2,
        dq, dk, dv, "mla_prep_bwd")
    gs["mla_cq_norm"], gs["mla_ckv_norm"] = dgcq, dgckv
    gs["mla_q_norm"], gs["mla_k_norm"] = dqg[:, :MLA_QK], dkg[:, :MLA_QK]
    gw["mla_w_uq"] = _mm_tn(cqn, dql, "dw_uq").reshape(MLA_Q_RANK, MLA_HEADS, LANES)[:, :, :MLA_QK].reshape(
        MLA_Q_RANK, -1)
    dwuk = _mm_tn(ckvn, dkl, "dw_uk").reshape(MLA_KV_RANK, MLA_HEADS, LANES)[:, :, :MLA_NOPE]
    dwuv = _mm_tn(ckvn, dv, "dw_uv").reshape(MLA_KV_RANK, MLA_HEADS, LANES)[:, :, :MLA_NOPE]
    gw["mla_w_ukv"] = jnp.concatenate([dwuk, dwuv], axis=2).reshape(MLA_KV_RANK, -1)

    dzqm, dkn, dvm, dmqg = _mem_attn_bwd(zqm, dyc, small["mem_q_norm"], km, vm, "mem_attn_bwd")
    gs["mem_q_norm"] = dmqg
    gw["mem_w_kv"], gs["mem_k_norm"], gs["mem_norm"] = _mem_kv_bwd(
        mem, small["mem_norm"], wkv, small["mem_k_norm"], dkn, dvm, "mem_kv_bwd")

    dzs = (dzuv, dzcq, dzckv, dzkr, dzqm, dzg)
    dws = list(_mm_tn_cols(h, dzs[:5], "dw_in_narrow")) + [_mm_tn(h, dzg, "dw_in_gate")]
    dws[3] = dws[3][:, MLA_NOPE:MLA_QK]
    gw["w_in"] = jnp.concatenate(dws, axis=1)
    dx1, gs["mix_norm"] = _proj_norm_bwd(dzs, [w.T for w in segs], x1, g_mix, dx2, "in_proj_bwd")
    mixer_sums = _pair_sum_exchange(_grad_pair_sums(MIXER_WEIGHTS, gw, core, "mixer"))
    dx, summed = ffn_grads("ffn1", x, g_ffn1, dx1, gpre1, upre1, wgu1, wd1, ex=mixer_sums, ex_names=MIXER_WEIGHTS,
                           last=True)
    ffn1_sums = _pair_sum_exchange(_grad_pair_sums(FFN1_WEIGHTS, gw, core, "ffn1"))
    slots.update(zip(FFN1_WEIGHTS, _run_exchange(ffn1_sums, "grad_exchange_ffn1")))
    return dx, slots, summed


def kernel(x, mem, positions, ffn1_norm, ffn1_w_gu, ffn1_w_down, mix_norm, w_in, b_gate, sg_ln_g, sg_ln_b, sg_w, sg_b, mla_cq_norm, mla_w_uq, mla_ckv_norm, mla_w_ukv, mla_q_norm, mla_k_norm, mem_norm, mem_w_kv, mem_q_norm, mem_k_norm, w_branch_a, w_branch_b, w_branch_c, w_out, ffn2_norm, ffn2_w_gu, ffn2_w_down, loss_target, m_ffn1_norm, m_ffn1_w_gu, m_ffn1_w_down, m_mix_norm, m_w_in, m_b_gate, m_sg_ln_g, m_sg_ln_b, m_sg_w, m_sg_b, m_mla_cq_norm, m_mla_w_uq, m_mla_ckv_norm, m_mla_w_ukv, m_mla_q_norm, m_mla_k_norm, m_mem_norm, m_mem_w_kv, m_mem_q_norm, m_mem_k_norm, m_w_branch_a, m_w_branch_b, m_w_branch_c, m_w_out, m_ffn2_norm, m_ffn2_w_gu, m_ffn2_w_down, v_ffn1_norm, v_ffn1_w_gu, v_ffn1_w_down, v_mix_norm, v_w_in, v_b_gate, v_sg_ln_g, v_sg_ln_b, v_sg_w, v_sg_b, v_mla_cq_norm, v_mla_w_uq, v_mla_ckv_norm, v_mla_w_ukv, v_mla_q_norm, v_mla_k_norm, v_mem_norm, v_mem_w_kv, v_mem_q_norm, v_mem_k_norm, v_w_branch_a, v_w_branch_b, v_w_branch_c, v_w_out, v_ffn2_norm, v_ffn2_w_gu, v_ffn2_w_down):
    args = dict(locals())
    weights = {n: args[n] for n in WEIGHT_ORDER}
    mom_m = {n: args["m_" + n] for n in WEIGHT_ORDER}
    mom_v = {n: args["v_" + n] for n in WEIGHT_ORDER}
    small = {n: weights[n] for n, _ in SMALL}
    halves = lambda a, r, c: a.reshape(2, r // 2, c)

    shards = {n: halves(weights[n][0].astype(BF16), r, c) for n, r, c, _ in SHARDED}
    core = lax.axis_index("c").astype(jnp.int32).reshape(1)
    dx, slots, summed = _device_step(x[0], mem[0], positions[0], loss_target[0], small, shards, core)
    loss = summed.reshape(-1)[_N_SMALL]
    small_grads = _unpack_small(summed)

    grads, deltas, new_m, new_v = {}, {}, {}, {}
    for name, r, c, _ in SHARDED:
        outs = _adamw_slots(halves(weights[name][0], r, c), slots[name], halves(mom_m[name][0], r, c),
                            halves(mom_v[name][0], r, c), "adamw_" + name)
        shape = weights[name].shape
        grads[name], deltas[name], new_m[name], new_v[name] = [o.reshape(shape) for o in outs]
    dlt, nm, nv = _adamw(_pack_small(small), _pack_small(small_grads), _pack_small({n: mom_m[n] for n, _ in SMALL}),
                         _pack_small({n: mom_v[n] for n, _ in SMALL}), "adamw_small")
    for name, _ in SMALL:
        grads[name] = small_grads[name]
    deltas.update(_unpack_small(dlt))
    new_m.update(_unpack_small(nm))
    new_v.update(_unpack_small(nv))

    return (loss, dx[None], *[grads[n] for n in WEIGHT_ORDER], *[deltas[n] for n in WEIGHT_ORDER],
            *[new_m[n] for n in WEIGHT_ORDER], *[new_v[n] for n in WEIGHT_ORDER])
```

```python
import functools
from typing import Callable, NamedTuple

import numpy as np
import jax
import jax.numpy as jnp
from jax import lax
from jax.experimental import pallas as pl
from jax.experimental.pallas import tpu as pltpu

F32 = jnp.float32
BF16 = jnp.bfloat16

D_MODEL = 1024
D_FF = 2816
FF_TILE = 1408
SG_WIDTH = 512
SG_GROUPS = 8
CHUNK = 128
MLA_HEADS = 8
MLA_QK = 96
MLA_NOPE = 64
MLA_ROPE = 32
MLA_Q_RANK = 384
MLA_KV_RANK = 256
MEM_HEADS = 4
MEM_LEN = 256
LANES = 128
EPS = 1e-6
NEG = -1e30
ROPE_BASE = 10000.0
N_CHIPS = 4
N_DEV = 8

ADAM_LR = 0.001
ADAM_B1 = 0.9
ADAM_B2 = 0.999
ADAM_EPS = 1e-08
ADAM_WD = 0.01
ADAM_STEP = 10

COL_V = 512
COL_CQ = 1024
COL_CKV = 1408
COL_KR = 1664
COL_QM = 1696
COL_GATE = 2208
IN_COLS = 5280

VMEM_LIMIT_BYTES = 56 * 1024 * 1024
INV_SQRT2 = 0.7071067811865476
INV_SQRT_2PI = 0.3989422804014327
LOG2E = 1.4426950408889634
ATTN_SCALE = MLA_QK ** -0.5
V_ONES_LANE = 64
ATTN_SCALE2 = ATTN_SCALE * LOG2E

SHARDED = (
    ("ffn1_w_gu", 1024, 1408, "col"),
    ("ffn1_w_down", 704, 1024, "row"),
    ("w_in", 1024, 1320, "col"),
    ("mla_w_uq", 384, 192, "col"),
    ("mla_w_ukv", 256, 256, "col"),
    ("mem_w_kv", 256, 1024, "row"),
    ("w_branch_a", 512, 256, "col"),
    ("w_branch_b", 512, 256, "col"),
    ("w_branch_c", 512, 256, "col"),
    ("w_out", 256, 1024, "row"),
    ("ffn2_w_gu", 1024, 1408, "col"),
    ("ffn2_w_down", 704, 1024, "row"),
)
SMALL = (
    ("ffn1_norm", (1, 1024)), ("mix_norm", (1, 1024)), ("b_gate", (1, 3072)),
    ("sg_ln_g", (1, 512)), ("sg_ln_b", (1, 512)), ("sg_w", (1, 8, 128, 128)),
    ("sg_b", (1, 8, 128)), ("mla_cq_norm", (1, 384)), ("mla_ckv_norm", (1, 256)),
    ("mla_q_norm", (1, 96)), ("mla_k_norm", (1, 96)), ("mem_norm", (1, 1024)),
    ("mem_q_norm", (1, 128)), ("mem_k_norm", (1, 128)), ("ffn2_norm", (1, 1024)),
)
WEIGHT_ORDER = (
    "ffn1_norm", "ffn1_w_gu", "ffn1_w_down", "mix_norm", "w_in", "b_gate", "sg_ln_g", "sg_ln_b",
    "sg_w", "sg_b", "mla_cq_norm", "mla_w_uq", "mla_ckv_norm", "mla_w_ukv", "mla_q_norm",
    "mla_k_norm", "mem_norm", "mem_w_kv", "mem_q_norm", "mem_k_norm", "w_branch_a", "w_branch_b",
    "w_branch_c", "w_out", "ffn2_norm", "ffn2_w_gu", "ffn2_w_down",
)

_N_SMALL = sum(int(np.prod(s)) for _, s in SMALL)
SMALL_ROWS = -(-_N_SMALL // (LANES * 8)) * 8

MESH = pl.DeviceIdType.MESH


def _cparams():
    return pltpu.CompilerParams(vmem_limit_bytes=VMEM_LIMIT_BYTES)


def _dot(a, b):
    return jnp.dot(a, b, preferred_element_type=F32)


def _dot_nt(a, b):
    return lax.dot_general(a, b, (((1,), (1,)), ((), ())), preferred_element_type=F32)


def _dot_tn(a, b):
    return lax.dot_general(a, b, (((0,), (0,)), ((), ())), preferred_element_type=F32)


def _gelu(x):
    return 0.5 * x * (1.0 + lax.erf(x * INV_SQRT2))


def _gelu_grad(x):
    return 0.5 * (1.0 + lax.erf(x * INV_SQRT2)) + x * jnp.exp(-0.5 * x * x) * INV_SQRT_2PI


def _rstd(x, n):
    return lax.rsqrt(jnp.sum(x * x, axis=-1, keepdims=True) * (1.0 / n) + EPS)


def _rms_vjp(x, r, g, dy, n):
    dxh = dy * g
    dx = r * dxh - x * (r * r * r) * (jnp.sum(dxh * x, axis=-1, keepdims=True) * (1.0 / n))
    return dx, dy * x * r


def _row_tile(t, want):
    return min(t, want)


def _wide_tile(n):
    if n <= 1024:
        return n
    if n % 1024 == 0:
        return 1024
    assert n % FF_TILE == 0, n
    return FF_TILE


def _mm_cols(a, ws, out_dtypes, name, ex=None):
    t, kdim = a.shape
    tm = _row_tile(t, 256)
    n = len(ws)

    def body(*refs):
        av = refs[0][...]
        for w_ref, o_ref in zip(refs[1:1 + n], refs[1 + n:]):
            o_ref[...] = _dot(av, w_ref[...]).astype(o_ref.dtype)

    row = lambda width: pl.BlockSpec((tm, width), lambda i: (i, 0))
    return _call_with_exchange(
        ex, body, name, (t // tm,),
        [row(kdim)] + [pl.BlockSpec(w.shape, lambda i: (0, 0)) for w in ws],
        [row(w.shape[1]) for w in ws],
        [jax.ShapeDtypeStruct((t, w.shape[1]), dt) for w, dt in zip(ws, out_dtypes)], [], (a, *ws))


def _proj_norm_bwd(dzs, wts, x, g, dres, name):
    t, d = x.shape
    tm = _row_tile(t, 256)
    n = len(dzs)

    def body(*refs):
        x_ref, g_ref, r_ref, dx_ref, dg_ref = refs[2 * n:]

        @pl.when(pl.program_id(0) == 0)
        def _():
            dg_ref[...] = jnp.zeros_like(dg_ref)

        dh = None
        for dz_ref, w_ref in zip(refs[:n], refs[n:2 * n]):
            part = _dot(dz_ref[...], w_ref[...])
            dh = part if dh is None else dh + part
        xv = x_ref[...]
        dx, dgr = _rms_vjp(xv, _rstd(xv, d), g_ref[...], dh, d)
        dx_ref[...] = r_ref[...] + dx
        dg_ref[...] += jnp.sum(dgr, axis=0, keepdims=True)

    row = lambda width: pl.BlockSpec((tm, width), lambda i: (i, 0))
    vec = pl.BlockSpec((1, d), lambda i: (0, 0))
    return pl.pallas_call(
        body, name=name, grid=(t // tm,),
        in_specs=[row(dz.shape[1]) for dz in dzs] + [pl.BlockSpec(w.shape, lambda i: (0, 0)) for w in wts]
        + [row(d), vec, row(d)],
        out_specs=[row(d), vec],
        out_shape=[jax.ShapeDtypeStruct((t, d), F32), jax.ShapeDtypeStruct((1, d), F32)],
        compiler_params=_cparams())(*dzs, *wts, x, g, dres)


def _mm_tn_cols(a, bs, name):
    t, m = a.shape
    tk = _row_tile(t, 512)
    n = len(bs)

    def body(*refs):
        @pl.when(pl.program_id(0) == 0)
        def _():
            for o_ref in refs[1 + n:]:
                o_ref[...] = jnp.zeros_like(o_ref)

        av = refs[0][...].astype(BF16)
        for b_ref, o_ref in zip(refs[1:1 + n], refs[1 + n:]):
            o_ref[...] += _dot_tn(av, b_ref[...].astype(BF16))

    row = lambda width: pl.BlockSpec((tk, width), lambda k: (k, 0))
    return pl.pallas_call(
        body, name=name, grid=(t // tk,), in_specs=[row(m)] + [row(b.shape[1]) for b in bs],
        out_specs=[pl.BlockSpec((m, b.shape[1]), lambda k: (0, 0)) for b in bs],
        out_shape=[jax.ShapeDtypeStruct((m, b.shape[1]), F32) for b in bs],
        compiler_params=_cparams())(a, *bs)


def _mm_tn(a, b, name, scale=1.0, ex=None, col_blocks=False, out_dtype=F32):
    t, m = a.shape
    n = b.shape[1]
    tm, tn = _wide_tile(m), _wide_tile(n)
    tk = _row_tile(t, 1024)
    nk = t // tk
    in_place = out_dtype == F32

    def body(a_ref, b_ref, o_ref, *scr):
        k = pl.program_id(2)
        acc_ref = o_ref if in_place else scr[0]

        @pl.when(k == 0)
        def _():
            acc_ref[...] = jnp.zeros_like(acc_ref)

        prod = _dot_tn(a_ref[...].astype(BF16), b_ref[...].astype(BF16))
        acc_ref[...] += prod.reshape(acc_ref.shape)
        if scale != 1.0 or not in_place:
            @pl.when(k == nk - 1)
            def _():
                o_ref[...] = (acc_ref[...] * scale).astype(out_dtype).reshape(o_ref.shape)

    if col_blocks:
        out_spec = pl.BlockSpec((1, tm, tn), lambda i, j, k: (j, i, 0))
        out_shape = jax.ShapeDtypeStruct((n // tn, m, tn), out_dtype)
    else:
        out_spec = pl.BlockSpec((tm, tn), lambda i, j, k: (i, j))
        out_shape = jax.ShapeDtypeStruct((m, n), out_dtype)
    outs = _call_with_exchange(
        ex, body, name, (m // tm, n // tn, nk),
        [pl.BlockSpec((tk, tm), lambda i, j, k: (k, i)), pl.BlockSpec((tk, tn), lambda i, j, k: (k, j))],
        [out_spec], [out_shape], [] if in_place else [pltpu.VMEM((tm, tn), F32)], (a, b))
    return outs[0] if ex is None else outs


PASS_ON_STEPS_BEFORE_END = 8


class _Exchange(NamedTuple):
    operands: list
    out_shapes: list
    sem_shapes: list
    build: Callable


def _call_with_exchange(ex, body, name, grid, in_specs, out_specs, out_shape, scratch_shapes, operands, prefetch=()):
    n_pre = len(prefetch)
    total = int(np.prod(grid))
    pass_step = max(total // 2, total - PASS_ON_STEPS_BEFORE_END)

    def call(kernel, ins, outs, shapes, scratch):
        if n_pre:
            spec = pltpu.PrefetchScalarGridSpec(num_scalar_prefetch=n_pre, grid=grid, in_specs=ins, out_specs=outs,
                                                scratch_shapes=scratch)
            return pl.pallas_call(kernel, name=name, grid_spec=spec, out_shape=shapes, compiler_params=_cparams())
        return pl.pallas_call(kernel, name=name, grid=grid, in_specs=ins, out_specs=outs, out_shape=shapes,
                              scratch_shapes=scratch, compiler_params=_cparams())

    if ex is None:
        return call(body, in_specs, out_specs, out_shape, scratch_shapes)(*prefetch, *operands)
    n_in, n_out, n_scr = len(in_specs), len(out_specs), len(scratch_shapes)
    k_in, k_out = len(ex.operands), len(ex.out_shapes)

    def carried(*refs):
        pre, refs = refs[:n_pre], refs[n_pre:]
        a, b = n_in, n_in + k_in
        c, e = b + n_out, b + n_out + k_out
        f = e + n_scr
        start, pass_on, finish = ex.build(refs[a:b], refs[c:e], refs[f:])
        step = functools.reduce(lambda lin, ax: lin * grid[ax] + pl.program_id(ax), range(len(grid)), 0)
        pl.when(step == 0)(start)
        body(*pre, *refs[:a], *refs[b:c], *refs[e:f])
        pl.when(step == pass_step)(pass_on)
        pl.when(step == total - 1)(finish)

    return call(carried, list(in_specs) + [ANY] * k_in, list(out_specs) + [ANY] * k_out,
                list(out_shape) + list(ex.out_shapes), list(scratch_shapes) + list(ex.sem_shapes),
                )(*prefetch, *operands, *ex.operands)


def _run_exchange(ex, name):
    k_in, k_out = len(ex.operands), len(ex.out_shapes)

    def body(*refs):
        start, pass_on, finish = ex.build(refs[:k_in], refs[k_in:k_in + k_out], refs[k_in + k_out:])
        start()
        pass_on()
        finish()

    return pl.pallas_call(body, name=name, in_specs=[ANY] * k_in, out_specs=[ANY] * k_out,
                          out_shape=list(ex.out_shapes), scratch_shapes=list(ex.sem_shapes))(*ex.operands)


def _ffn_fwd(x, g, wgu4, wd2, name, ex=None, next_gain=None, target=None):
    t, d = x.shape
    tm = _row_tile(t, 512)
    assert next_gain is None or target is None
    extra = [a for a in (next_gain, target) if a is not None]

    def body(*refs):
        x_ref, g_ref, wg_ref, wu_ref, wd_ref = refs[:5]
        e_ref = refs[5] if extra else None
        outs, (xn_scr, acc_scr) = refs[5 + len(extra):-2], refs[-2:]
        if target is not None:
            dy_ref, loss_ref, gg_ref, uu_ref = outs
        elif next_gain is not None:
            o_ref, gg_ref, uu_ref, h_ref = outs
        else:
            o_ref, gg_ref, uu_ref = outs
        i, j = pl.program_id(0), pl.program_id(1)

        @pl.when(j == 0)
        def _():
            xv = x_ref[...]
            xn_scr[...] = (xv * _rstd(xv, d) * g_ref[...]).astype(BF16)
            acc_scr[...] = jnp.zeros_like(acc_scr)

        if target is not None:
            @pl.when((i == 0) & (j == 0))
            def _():
                loss_ref[...] = jnp.zeros_like(loss_ref)

        xn = xn_scr[...]
        gg = _dot(xn, wg_ref[0])
        uu = _dot(xn, wu_ref[0])
        gg_ref[...] = gg.astype(BF16)
        uu_ref[...] = uu.astype(BF16)
        act = gg * jax.nn.sigmoid(gg) * uu
        acc_scr[...] += _dot(act.astype(BF16), wd_ref[0])

        @pl.when(j == 1)
        def _():
            y = x_ref[...] + 0.5 * acc_scr[...]
            if target is not None:
                e = y - e_ref[...]
                dy_ref[...] = e * (1.0 / d)
                part = 0.5 * jnp.sum(jnp.sum(e * e, axis=-1, keepdims=True) * (1.0 / d), axis=0, keepdims=True)
                loss_ref[...] += jnp.broadcast_to(part, loss_ref.shape)
            else:
                o_ref[...] = y
                if next_gain is not None:
                    h_ref[...] = (y * _rstd(y, d) * e_ref[...]).astype(BF16)

    row = pl.BlockSpec((tm, d), lambda i, j: (i, 0))
    vec = pl.BlockSpec((1, d), lambda i, j: (0, 0))
    ffb = pl.BlockSpec((tm, FF_TILE), lambda i, j: (i, j))
    f32_rows, bf16_ff = jax.ShapeDtypeStruct((t, d), F32), jax.ShapeDtypeStruct((t, D_FF), BF16)
    if target is not None:
        extra_spec, out_specs = [row], [row, pl.BlockSpec((1, LANES), lambda i, j: (0, 0)), ffb, ffb]
        out_shape = [f32_rows, jax.ShapeDtypeStruct((1, LANES), F32), bf16_ff, bf16_ff]
    elif next_gain is not None:
        extra_spec, out_specs = [vec], [row, ffb, ffb, row]
        out_shape = [f32_rows, bf16_ff, bf16_ff, jax.ShapeDtypeStruct((t, d), BF16)]
    else:
        extra_spec, out_specs, out_shape = [], [row, ffb, ffb], [f32_rows, bf16_ff, bf16_ff]
    return _call_with_exchange(
        ex, body, name, (t // tm, 2),
        [row, vec,
         pl.BlockSpec((1, d, FF_TILE), lambda i, j: (j, 0, 0)),
         pl.BlockSpec((1, d, FF_TILE), lambda i, j: (j + 2, 0, 0)),
         pl.BlockSpec((1, FF_TILE, d), lambda i, j: (j, 0, 0))] + extra_spec,
        out_specs, out_shape,
        [pltpu.VMEM((tm, d), BF16), pltpu.VMEM((tm, d), F32)], (x, g, wgu4, wgu4, wd2, *extra))


def _ffn_bwd(x, g, dy, gpre, upre, wgu4, wd2, name, ex=None):
    t, d = x.shape
    tm = _row_tile(t, 512)

    def body(dy_ref, gg_ref, uu_ref, wgu_hbm, wd_hbm, dg_ref, du_ref, act_ref, part_ref, wg_ref, wu_ref, wd_ref):
        j = pl.program_id(0)

        @pl.when(pl.program_id(1) == 0)
        def _():
            pltpu.sync_copy(wgu_hbm.at[j], wg_ref.at[0])
            pltpu.sync_copy(wgu_hbm.at[j + 2], wu_ref.at[0])
            pltpu.sync_copy(wd_hbm.at[j], wd_ref.at[0])

        gg = gg_ref[...].astype(F32)
        uu = uu_ref[...].astype(F32)
        sg = jax.nn.sigmoid(gg)
        silu = gg * sg
        act_ref[...] = (silu * uu).astype(BF16)
        dyh = (0.5 * dy_ref[...]).astype(BF16)
        dact = _dot_nt(dyh, wd_ref[0])
        du = (dact * silu).astype(BF16)
        dgt = (dact * uu * (sg * (1.0 + gg * (1.0 - sg)))).astype(BF16)
        du_ref[...] = du
        dg_ref[...] = dgt
        part_ref[0] = _dot_nt(dgt, wg_ref[0]) + _dot_nt(du, wu_ref[0])

    row = pl.BlockSpec((tm, d), lambda j, i: (i, 0))
    ffb = pl.BlockSpec((tm, FF_TILE), lambda j, i: (i, j))
    dgt, dup, act, parts, *got = _call_with_exchange(
        ex, body, name, (2, t // tm),
        [row, ffb, ffb, ANY, ANY],
        [ffb, ffb, ffb, pl.BlockSpec((1, tm, d), lambda j, i: (j, i, 0))],
        [jax.ShapeDtypeStruct((t, D_FF), BF16)] * 3 + [jax.ShapeDtypeStruct((2, t, d), F32)],
        [pltpu.VMEM((1, d, FF_TILE), BF16), pltpu.VMEM((1, d, FF_TILE), BF16), pltpu.VMEM((1, FF_TILE, d), BF16)],
        (dy, gpre, upre, wgu4, wd2))

    def norm_body(x_ref, g_ref, p_ref, dy_ref, dx_ref, dgain_ref, xn_ref):
        @pl.when(pl.program_id(0) == 0)
        def _():
            dgain_ref[...] = jnp.zeros_like(dgain_ref)

        xv = x_ref[...]
        r = _rstd(xv, d)
        xn_ref[...] = (xv * r * g_ref[...]).astype(BF16)
        dx, dgr = _rms_vjp(xv, r, g_ref[...], p_ref[0] + p_ref[1], d)
        dx_ref[...] = dy_ref[...] + dx
        dgain_ref[...] += jnp.sum(dgr, axis=0, keepdims=True)

    tn = _row_tile(t, 256)
    nrow = pl.BlockSpec((tn, d), lambda i: (i, 0))
    vec = pl.BlockSpec((1, d), lambda i: (0, 0))
    dx, dgain, xn = pl.pallas_call(
        norm_body, name=name + "_norm", grid=(t // tn,),
        in_specs=[nrow, vec, pl.BlockSpec((2, tn, d), lambda i: (0, i, 0)), nrow],
        out_specs=[nrow, vec, nrow],
        out_shape=[jax.ShapeDtypeStruct((t, d), F32), jax.ShapeDtypeStruct((1, d), F32),
                   jax.ShapeDtypeStruct((t, d), BF16)],
        compiler_params=_cparams())(x, g, parts, dy)
    return [dx, dgain, xn, dgt, dup, act] + got


def _sgu_layernorm(vpre, lg, lb):
    v = _gelu(vpre)
    mu = jnp.mean(v, axis=-1, keepdims=True)
    xc = v - mu
    rstd = lax.rsqrt(jnp.mean(xc * xc, axis=-1, keepdims=True) + EPS)
    xhat = xc * rstd
    return xhat, rstd, xhat * lg + lb


def _sgu_fwd(zuv, lg, lb, wt, bias_l, name):
    t = zuv.shape[0]
    tm = _row_tile(t, 512)

    def body(u_ref, v_ref, lg_ref, lb_ref, wt_ref, bl_ref, o_ref, vln_scr):
        _, _, vln = _sgu_layernorm(v_ref[...], lg_ref[...], lb_ref[...])
        vln_scr[...] = vln.astype(BF16)
        lo = lax.broadcasted_iota(jnp.int32, (CHUNK, LANES), 1) < 64
        for c in range(tm // CHUNK):
            rows = slice(c * CHUNK, (c + 1) * CHUNK)
            for p in range(SG_GROUPS // 2):
                cols = slice(p * LANES, (p + 1) * LANES)
                vp = vln_scr[rows, cols]
                mixed = jnp.where(lo, _dot(wt_ref[2 * p], vp), _dot(wt_ref[2 * p + 1], vp)) + bl_ref[:, cols]
                o_ref[rows, cols] = (_gelu(u_ref[rows, cols]) * mixed).astype(BF16)

    half = lambda k: pl.BlockSpec((tm, SG_WIDTH), lambda i: (i, k))
    vec = pl.BlockSpec((1, SG_WIDTH), lambda i: (0, 0))
    return pl.pallas_call(
        body, name=name, grid=(t // tm,),
        in_specs=[half(0), half(1), vec, vec,
                  pl.BlockSpec((SG_GROUPS, CHUNK, CHUNK), lambda i: (0, 0, 0)),
                  pl.BlockSpec((CHUNK, SG_WIDTH), lambda i: (0, 0))],
        out_specs=pl.BlockSpec((tm, SG_WIDTH), lambda i: (i, 0)),
        out_shape=jax.ShapeDtypeStruct((t, SG_WIDTH), BF16),
        scratch_shapes=[pltpu.VMEM((tm, SG_WIDTH), BF16)],
        compiler_params=_cparams())(zuv, zuv, lg, lb, wt, bias_l)


def _sgu_bwd(zuv, dya, lg, lb, wt, wt_t, bias_l, name):
    t = zuv.shape[0]
    tm = _row_tile(t, 256)
    nsteps = t // tm

    def body(u_ref, v_ref, dy_ref, lg_ref, lb_ref, wt_ref, wtt_ref, bl_ref,
             dz_ref, dwt_ref, dbl_ref, dlg_ref, dlb_ref, vln_scr, dvln_scr, dbacc_scr):
        step = pl.program_id(0)

        @pl.when(step == 0)
        def _():
            dwt_ref[...] = jnp.zeros_like(dwt_ref)
            dlg_ref[...] = jnp.zeros_like(dlg_ref)
            dlb_ref[...] = jnp.zeros_like(dlb_ref)
            dbl_ref[...] = jnp.zeros_like(dbl_ref)
            dbacc_scr[...] = jnp.zeros_like(dbacc_scr)

        vpre = v_ref[...]
        lgv = lg_ref[...]
        xhat, rstd, vln = _sgu_layernorm(vpre, lgv, lb_ref[...])
        vln_scr[...] = vln.astype(BF16)
        lo = lax.broadcasted_iota(jnp.int32, (CHUNK, LANES), 1) < 64
        for c in range(tm // CHUNK):
            rows = slice(c * CHUNK, (c + 1) * CHUNK)
            for p in range(SG_GROUPS // 2):
                cols = slice(p * LANES, (p + 1) * LANES)
                vp = vln_scr[rows, cols]
                mixed = jnp.where(lo, _dot(wt_ref[2 * p], vp), _dot(wt_ref[2 * p + 1], vp)) + bl_ref[:, cols]
                upre = u_ref[rows, cols]
                dyp = dy_ref[rows, cols]
                dz_ref[rows, cols] = (dyp * mixed * _gelu_grad(upre)).astype(BF16)
                dm = dyp * _gelu(upre)
                dbacc_scr[:, cols] += dm
                dlo = jnp.where(lo, dm, 0.0).astype(BF16)
                dhi = jnp.where(lo, 0.0, dm).astype(BF16)
                dvln_scr[rows, cols] = _dot(wtt_ref[2 * p], dlo) + _dot(wtt_ref[2 * p + 1], dhi)
                dwt_ref[2 * p] += _dot_nt(dlo, vp)
                dwt_ref[2 * p + 1] += _dot_nt(dhi, vp)
        dvln = dvln_scr[...]
        dlg_ref[...] += jnp.sum(dvln * xhat, axis=0, keepdims=True)
        dlb_ref[...] += jnp.sum(dvln, axis=0, keepdims=True)
        dxh = dvln * lgv
        dv = rstd * (dxh - jnp.mean(dxh, axis=-1, keepdims=True)
                     - xhat * jnp.mean(dxh * xhat, axis=-1, keepdims=True))
        dz_ref[:, SG_WIDTH:] = (dv * _gelu_grad(vpre)).astype(BF16)

        @pl.when(step == nsteps - 1)
        def _():
            rr = lax.broadcasted_iota(jnp.int32, (CHUNK, CHUNK), 0)
            cc = lax.broadcasted_iota(jnp.int32, (CHUNK, CHUNK), 1)
            tril = (cc <= rr).astype(F32)
            for gidx in range(SG_GROUPS):
                dwt_ref[gidx] = dwt_ref[gidx] * tril
            kk = lax.broadcasted_iota(jnp.int32, (SG_WIDTH, LANES), 0)
            gg = lax.broadcasted_iota(jnp.int32, (SG_WIDTH, LANES), 1)
            sel = ((kk // 64) == gg).astype(F32)
            dbl_ref[...] = jnp.dot(dbacc_scr[...], sel, preferred_element_type=F32,
                                   precision=lax.Precision.HIGHEST)

    half = lambda k: pl.BlockSpec((tm, SG_WIDTH), lambda i: (i, k))
    vec = pl.BlockSpec((1, SG_WIDTH), lambda i: (0, 0))
    wspec = pl.BlockSpec((SG_GROUPS, CHUNK, CHUNK), lambda i: (0, 0, 0))
    return pl.pallas_call(
        body, name=name, grid=(nsteps,),
        in_specs=[half(0), half(1), pl.BlockSpec((tm, SG_WIDTH), lambda i: (i, 0)), vec, vec,
                  wspec, wspec, pl.BlockSpec((CHUNK, SG_WIDTH), lambda i: (0, 0))],
        out_specs=[pl.BlockSpec((tm, 2 * SG_WIDTH), lambda i: (i, 0)), wspec,
                   pl.BlockSpec((CHUNK, LANES), lambda i: (0, 0)), vec, vec],
        out_shape=[jax.ShapeDtypeStruct((t, 2 * SG_WIDTH), BF16),
                   jax.ShapeDtypeStruct((SG_GROUPS, CHUNK, CHUNK), F32),
                   jax.ShapeDtypeStruct((CHUNK, LANES), F32),
                   jax.ShapeDtypeStruct((1, SG_WIDTH), F32), jax.ShapeDtypeStruct((1, SG_WIDTH), F32)],
        scratch_shapes=[pltpu.VMEM((tm, SG_WIDTH), BF16), pltpu.VMEM((tm, SG_WIDTH), F32),
                        pltpu.VMEM((CHUNK, SG_WIDTH), F32)],
        compiler_params=_cparams())(zuv, zuv, dya, lg, lb, wt, wt_t, bias_l)


def _rope(x, c, s1, s2):
    return x * c + pltpu.roll(x, LANES - 16, 1) * s1 + pltpu.roll(x, 16, 1) * s2


def _rope_t(dy, c, s1, s2):
    return dy * c + pltpu.roll(dy * s1, 16, 1) + pltpu.roll(dy * s2, LANES - 16, 1)


def _mla_prep_fwd(zcq, zckv, zkr, gcq, gckv, qg, kg, wuq, wuk, wuv, rc, rs1, rs2, name, ex=None):
    t = zcq.shape[0]
    tm = _row_tile(t, 256)
    hd = MLA_HEADS * LANES

    def body(zcq_ref, zckv_ref, zkr_ref, gcq_ref, gckv_ref, qg_ref, kg_ref, wuq_ref, wuk_ref, wuv_ref,
             c_ref, s1_ref, s2_ref, q_ref, k_ref, v_ref, cqn_ref, ckvn_ref):
        c, s1, s2 = c_ref[...], s1_ref[...], s2_ref[...]
        xq = zcq_ref[...]
        cqn = (xq * _rstd(xq, MLA_Q_RANK) * gcq_ref[...]).astype(BF16)
        cqn_ref[...] = cqn
        ql = _dot(cqn, wuq_ref[...])
        xk = zckv_ref[...]
        ckvn = (xk * _rstd(xk, MLA_KV_RANK) * gckv_ref[...]).astype(BF16)
        ckvn_ref[...] = ckvn
        kl = _dot(ckvn, wuk_ref[...])
        slot_lane = lax.broadcasted_iota(jnp.int32, (tm, hd), 1) % LANES
        v_ref[...] = jnp.where(slot_lane == V_ONES_LANE, 1.0, _dot(ckvn, wuv_ref[...])).astype(BF16)
        kr = zkr_ref[...]
        for h in range(MLA_HEADS):
            sl = slice(h * LANES, (h + 1) * LANES)
            qh = ql[:, sl]
            q_ref[:, sl] = (_rope(qh * _rstd(qh, MLA_QK) * qg_ref[...], c, s1, s2) * ATTN_SCALE2).astype(BF16)
            kh = kl[:, sl] + kr
            k_ref[:, sl] = _rope(kh * _rstd(kh, MLA_QK) * kg_ref[...], c, s1, s2).astype(BF16)

    row = lambda n: pl.BlockSpec((tm, n), lambda i: (i, 0))
    full = lambda a: pl.BlockSpec(a.shape, lambda i: (0, 0))
    return _call_with_exchange(
        ex, body, name, (t // tm,),
        [row(MLA_Q_RANK), row(MLA_KV_RANK), row(LANES), full(gcq), full(gckv), full(qg), full(kg),
         full(wuq), full(wuk), full(wuv), row(LANES), row(LANES), row(LANES)],
        [row(hd), row(hd), row(hd), row(MLA_Q_RANK), row(MLA_KV_RANK)],
        [jax.ShapeDtypeStruct((t, hd), BF16)] * 3
        + [jax.ShapeDtypeStruct((t, MLA_Q_RANK), BF16), jax.ShapeDtypeStruct((t, MLA_KV_RANK), BF16)],
        [], (zcq, zckv, zkr, gcq, gckv, qg, kg, wuq, wuk, wuv, rc, rs1, rs2))


def _mla_prep_bwd(zcq, zckv, zkr, gcq, gckv, qg, kg, wuq, wuk, wuv, rc, rs1, rs2, dq, dk, dv, name):
    t = zcq.shape[0]
    tm = _row_tile(t, 256)
    hd = MLA_HEADS * LANES

    def body(zcq_ref, zckv_ref, zkr_ref, gcq_ref, gckv_ref, qg_ref, kg_ref, wuq_ref, wuk_ref, wuv_ref,
             c_ref, s1_ref, s2_ref, dq_ref, dk_ref, dv_ref,
             dzcq_ref, dzckv_ref, dzkr_ref, dql_ref, dkl_ref, dgcq_ref, dgckv_ref, dqg_ref, dkg_ref):
        @pl.when(pl.program_id(0) == 0)
        def _():
            for ref in (dgcq_ref, dgckv_ref, dqg_ref, dkg_ref):
                ref[...] = jnp.zeros_like(ref)

        c, s1, s2 = c_ref[...], s1_ref[...], s2_ref[...]
        qgv, kgv = qg_ref[...], kg_ref[...]
        xq = zcq_ref[...]
        rq = _rstd(xq, MLA_Q_RANK)
        ql = _dot((xq * rq * gcq_ref[...]).astype(BF16), wuq_ref[...])
        xk = zckv_ref[...]
        rk = _rstd(xk, MLA_KV_RANK)
        kl = _dot((xk * rk * gckv_ref[...]).astype(BF16), wuk_ref[...])
        kr = zkr_ref[...]
        dqg_acc = jnp.zeros((tm, LANES), F32)
        dkg_acc = jnp.zeros((tm, LANES), F32)
        dkr = jnp.zeros((tm, LANES), F32)
        for h in range(MLA_HEADS):
            sl = slice(h * LANES, (h + 1) * LANES)
            qh = ql[:, sl]
            dqh, dgr = _rms_vjp(qh, _rstd(qh, MLA_QK), qgv, _rope_t(dq_ref[:, sl], c, s1, s2), MLA_QK)
            dql_ref[:, sl] = dqh.astype(BF16)
            dqg_acc += dgr
            kh = kl[:, sl] + kr
            dkh, dgr = _rms_vjp(kh, _rstd(kh, MLA_QK), kgv, _rope_t(dk_ref[:, sl], c, s1, s2), MLA_QK)
            dkl_ref[:, sl] = dkh.astype(BF16)
            dkg_acc += dgr
            dkr += dkh
        dqg_ref[...] += jnp.sum(dqg_acc, axis=0, keepdims=True)
        dkg_ref[...] += jnp.sum(dkg_acc, axis=0, keepdims=True)
        lane = lax.broadcasted_iota(jnp.int32, (tm, LANES), 1)
        dzkr_ref[...] = jnp.where((lane >= MLA_NOPE) & (lane < MLA_QK), dkr, 0.0).astype(BF16)
        dcqn = _dot_nt(dql_ref[...], wuq_ref[...])
        dx, dgr = _rms_vjp(xq, rq, gcq_ref[...], dcqn, MLA_Q_RANK)
        dzcq_ref[...] = dx.astype(BF16)
        dgcq_ref[...] += jnp.sum(dgr, axis=0, keepdims=True)
        dckvn = _dot_nt(dkl_ref[...], wuk_ref[...]) + _dot_nt(dv_ref[...].astype(BF16), wuv_ref[...])
        dx, dgr = _rms_vjp(xk, rk, gckv_ref[...], dckvn, MLA_KV_RANK)
        dzckv_ref[...] = dx.astype(BF16)
        dgckv_ref[...] += jnp.sum(dgr, axis=0, keepdims=True)

    row = lambda n: pl.BlockSpec((tm, n), lambda i: (i, 0))
    full = lambda a: pl.BlockSpec(a.shape, lambda i: (0, 0))
    vec = lambda n: pl.BlockSpec((1, n), lambda i: (0, 0))
    return pl.pallas_call(
        body, name=name, grid=(t // tm,),
        in_specs=[row(MLA_Q_RANK), row(MLA_KV_RANK), row(LANES), full(gcq), full(gckv), full(qg), full(kg),
                  full(wuq), full(wuk), full(wuv), row(LANES), row(LANES), row(LANES), row(hd), row(hd), row(hd)],
        out_specs=[row(MLA_Q_RANK), row(MLA_KV_RANK), row(LANES), row(hd), row(hd),
                   vec(MLA_Q_RANK), vec(MLA_KV_RANK), vec(LANES), vec(LANES)],
        out_shape=[jax.ShapeDtypeStruct((t, MLA_Q_RANK), BF16), jax.ShapeDtypeStruct((t, MLA_KV_RANK), BF16),
                   jax.ShapeDtypeStruct((t, LANES), BF16), jax.ShapeDtypeStruct((t, hd), BF16),
                   jax.ShapeDtypeStruct((t, hd), BF16), jax.ShapeDtypeStruct((1, MLA_Q_RANK), F32),
                   jax.ShapeDtypeStruct((1, MLA_KV_RANK), F32), jax.ShapeDtypeStruct((1, LANES), F32),
                   jax.ShapeDtypeStruct((1, LANES), F32)],
        compiler_params=_cparams(),
    )(zcq, zckv, zkr, gcq, gckv, qg, kg, wuq, wuk, wuv, rc, rs1, rs2, dq, dk, dv)


def _attn_tiles(t):
    tq = 512 if t >= 2048 else 128
    return tq, min(t, 4 * tq), min(t, 4 * tq)


def _causal_keep(tq, nk, i, j, tk):
    row = lax.broadcasted_iota(jnp.int32, (tq, nk), 0)
    col = lax.broadcasted_iota(jnp.int32, (tq, nk), 1)
    return (col - row) <= (i * tq - j * tk)


def _causal_keep_t(tq, nk, i, j, tk):
    key = lax.broadcasted_iota(jnp.int32, (nk, tq), 0)
    qry = lax.broadcasted_iota(jnp.int32, (nk, tq), 1)
    return (key - qry) <= (i * tq - j * tk)


ATTN_FWD_HEADS_PER_STEP = 2
ATTN_BWD_HEADS_PER_STEP = 2


def _attn_fwd(q, k, v, name, ex=None):
    t, hd = q.shape
    hp = ATTN_FWD_HEADS_PER_STEP
    tq, tk, _ = _attn_tiles(t)
    pairs = [(i, j) for i in range(t // tq) for j in range(((i + 1) * tq - 1) // tk + 1)]
    ii = np.array([p[0] for p in pairs], np.int32)
    jj = np.array([p[1] for p in pairs], np.int32)

    def body(ii_ref, jj_ref, q_ref, k_ref, v_ref, o_ref, lse_ref, m_scr, acc_scr):
        s_id = pl.program_id(1)
        i, j = ii_ref[s_id], jj_ref[s_id]
        last = j == ((i + 1) * tq - 1) // tk
        ones_lane = lax.broadcasted_iota(jnp.int32, (tq, LANES), 1) == V_ONES_LANE

        @pl.when(j == 0)
        def _():
            m_scr[...] = jnp.full_like(m_scr, NEG)
            acc_scr[...] = jnp.zeros_like(acc_scr)

        def step(masked, nk):
            for hh in range(hp):
                sl = slice(hh * LANES, (hh + 1) * LANES)
                s = _dot_nt(q_ref[:, sl], k_ref[:nk, sl])
                if masked:
                    s = jnp.where(_causal_keep(tq, nk, i, j, tk), s, NEG)
                m_prev = m_scr[hh]
                m_new = jnp.maximum(m_prev, jnp.max(s, axis=1, keepdims=True))
                p = jnp.exp2(s - m_new)
                alpha = jnp.exp2(m_prev - m_new)
                acc = alpha * acc_scr[:, sl] + _dot(p.astype(BF16), v_ref[:nk, sl])
                if masked:
                    l_new = jnp.sum(jnp.where(ones_lane, acc, 0.0), axis=1, keepdims=True)
                    o_ref[:, sl] = (acc / l_new).astype(BF16)
                    lse_ref[:, sl] = jnp.broadcast_to(m_new + jnp.log(l_new) * LOG2E, (tq, LANES))
                else:
                    acc_scr[:, sl] = acc
                    m_scr[hh] = m_new

        @pl.when(jnp.logical_not(last))
        def _():
            step(False, tk)

        r = (((i + 1) * tq - 1) % tk) // tq
        for rr in range(tk // tq):
            @pl.when(last & (r == rr))
            def _():
                step(True, (rr + 1) * tq)

    w = hp * LANES
    qspec = pl.BlockSpec((tq, w), lambda h, s, ii_r, jj_r: (ii_r[s], h))
    kspec = pl.BlockSpec((tk, w), lambda h, s, ii_r, jj_r: (jj_r[s], h))
    return _call_with_exchange(
        ex, body, name, (hd // w, len(pairs)), [qspec, kspec, kspec], [qspec, qspec],
        [jax.ShapeDtypeStruct((t, hd), BF16), jax.ShapeDtypeStruct((t, hd), F32)],
        [pltpu.VMEM((hp, tq, 1), F32), pltpu.VMEM((tq, w), F32)], (q, k, v),
        prefetch=(jnp.asarray(ii), jnp.asarray(jj)))


def _attn_bwd_rows(o, lse, do, name):
    t, hd = o.shape
    heads = hd // LANES
    tm = _row_tile(t, 512)

    def body(o_ref, lse_ref, do_ref, out_ref):
        lane = lax.broadcasted_iota(jnp.int32, (tm, LANES), 1)
        acc = jnp.zeros((tm, LANES), F32)
        for h in range(heads):
            sl = slice(h * LANES, (h + 1) * LANES)
            delta = jnp.sum(do_ref[:, sl].astype(F32) * o_ref[:, sl].astype(F32), axis=1, keepdims=True)
            acc = jnp.where(lane == h, delta, acc)
            acc = jnp.where(lane == heads + h, lse_ref[:, sl], acc)
        out_ref[...] = acc

    row = pl.BlockSpec((tm, hd), lambda i: (i, 0))
    cols = pl.pallas_call(
        body, name=name, grid=(t // tm,), in_specs=[row, row, row],
        out_specs=pl.BlockSpec((tm, LANES), lambda i: (i, 0)),
        out_shape=jax.ShapeDtypeStruct((t, LANES), F32), compiler_params=_cparams())(o, lse, do)
    rows = cols.T
    return rows[:heads].reshape(heads, 1, t), rows[heads:2 * heads].reshape(heads, 1, t)


def _attn_bwd(q, k, v, delta_rows, lse_rows, do, name):
    t, hd = q.shape
    hp = ATTN_BWD_HEADS_PER_STEP
    tq, _, tk = _attn_tiles(t)
    nq = t // tq
    pairs = [(i, j) for j in range(t // tk) for i in range((j * tk) // tq, nq)]
    ii = np.array([p[0] for p in pairs], np.int32)
    jj = np.array([p[1] for p in pairs], np.int32)

    def body(jj_ref, ii_ref, q_ref, k_ref, v_ref, delta_ref, lse_ref, do_ref, dq_ref, dk_ref, dv_ref,
             dk_scr, dv_scr):
        s_id = pl.program_id(1)
        i, j = ii_ref[s_id], jj_ref[s_id]

        @pl.when(s_id == 0)
        def _():
            dq_ref[...] = jnp.zeros_like(dq_ref)

        @pl.when(i == (j * tk) // tq)
        def _():
            dk_scr[...] = jnp.zeros_like(dk_scr)
            dv_scr[...] = jnp.zeros_like(dv_scr)

        rows = pl.ds(pl.multiple_of(i * tq, tq), tq)

        def step(masked, nk):
            for hh in range(hp):
                sl = slice(hh * LANES, (hh + 1) * LANES)
                qv, kv, dov = q_ref[:, sl], k_ref[:nk, sl], do_ref[:, sl]
                st = _dot_nt(kv, qv)
                if masked:
                    st = jnp.where(_causal_keep_t(tq, nk, i, j, tk), st, NEG)
                pt = jnp.exp2(st - lse_ref[hh])
                dv_scr[:nk, sl] += _dot(pt.astype(BF16), dov)
                dpt = _dot_nt(v_ref[:nk, sl], dov)
                dst = (pt * (dpt - delta_ref[hh]) * ATTN_SCALE).astype(BF16)
                dk_scr[:nk, sl] += _dot(dst, qv)
                dq_ref[rows, sl] += _dot_tn(dst, kv)

        seen = jnp.minimum((i + 1) * tq - j * tk, tk)
        for nk in range(tq, tk + 1, tq):
            @pl.when((seen == nk) & ((i + 1) * tq - j * tk <= tk))
            def _():
                step(True, nk)

        @pl.when((i + 1) * tq - j * tk > tk)
        def _():
            step(False, tk)

        @pl.when(i == nq - 1)
        def _():
            dk_ref[...] = dk_scr[...] * (1.0 / ATTN_SCALE2)
            dv_ref[...] = dv_scr[...]

    w = hp * LANES
    qspec = pl.BlockSpec((tq, w), lambda h, s, jj_r, ii_r: (ii_r[s], h))
    kspec = pl.BlockSpec((tk, w), lambda h, s, jj_r, ii_r: (jj_r[s], h))
    rspec = pl.BlockSpec((hp, 1, tq), lambda h, s, jj_r, ii_r: (h, 0, ii_r[s]))
    return pl.pallas_call(
        body, name=name,
        grid_spec=pltpu.PrefetchScalarGridSpec(
            num_scalar_prefetch=2, grid=(hd // w, len(pairs)),
            in_specs=[qspec, kspec, kspec, rspec, rspec, qspec],
            out_specs=[pl.BlockSpec((t, w), lambda h, s, jj_r, ii_r: (0, h)), kspec, kspec],
            scratch_shapes=[pltpu.VMEM((tk, w), F32), pltpu.VMEM((tk, w), F32)]),
        out_shape=[jax.ShapeDtypeStruct((t, hd), F32)] * 3,
        compiler_params=_cparams())(jnp.asarray(jj), jnp.asarray(ii), q, k, v, delta_rows, lse_rows, do)


MEM_W = MEM_HEADS * LANES


def _mem_kv_fwd(mem, gmem, wkv, kg, name):
    m, d = mem.shape

    def body(mem_ref, g_ref, w_ref, kg_ref, k_ref, v_ref, mn_ref):
        xv = mem_ref[...]
        mn = (xv * _rstd(xv, d) * g_ref[...]).astype(BF16)
        mn_ref[...] = mn
        kvm = _dot(mn, w_ref[...])
        v_ref[...] = kvm[:, MEM_W:].astype(BF16)
        for h in range(MEM_HEADS):
            sl = slice(h * LANES, (h + 1) * LANES)
            kh = kvm[:, sl]
            k_ref[:, sl] = (kh * _rstd(kh, LANES) * kg_ref[...]).astype(BF16)

    full = lambda a: pl.BlockSpec(a.shape, lambda i: (0, 0))
    return pl.pallas_call(
        body, name=name, grid=(1,), in_specs=[full(mem), full(gmem), full(wkv), full(kg)],
        out_specs=[pl.BlockSpec((m, MEM_W), lambda i: (0, 0)), pl.BlockSpec((m, MEM_W), lambda i: (0, 0)),
                   pl.BlockSpec((m, d), lambda i: (0, 0))],
        out_shape=[jax.ShapeDtypeStruct((m, MEM_W), BF16), jax.ShapeDtypeStruct((m, MEM_W), BF16),
                   jax.ShapeDtypeStruct((m, d), BF16)],
        compiler_params=_cparams())(mem, gmem, wkv, kg)


def _mem_softmax(qn, kh):
    s = _dot_nt(qn, kh) * (LANES ** -0.5)
    e = jnp.exp(s - jnp.max(s, axis=1, keepdims=True))
    return e / jnp.sum(e, axis=1, keepdims=True)


def _mem_attn_fwd(zqm, qg, km, vm, name):
    t = zqm.shape[0]
    tm = _row_tile(t, 512)

    def body(q_ref, qg_ref, k_ref, v_ref, o_ref):
        for h in range(MEM_HEADS):
            sl = slice(h * LANES, (h + 1) * LANES)
            qh = q_ref[:, sl]
            qn = (qh * _rstd(qh, LANES) * qg_ref[...]).astype(BF16)
            p = _mem_softmax(qn, k_ref[:, sl])
            o_ref[:, sl] = _dot(p.astype(BF16), v_ref[:, sl]).astype(BF16)

    row = pl.BlockSpec((tm, MEM_W), lambda i: (i, 0))
    full = lambda a: pl.BlockSpec(a.shape, lambda i: (0, 0))
    return pl.pallas_call(
        body, name=name, grid=(t // tm,), in_specs=[row, full(qg), full(km), full(vm)], out_specs=row,
        out_shape=jax.ShapeDtypeStruct((t, MEM_W), BF16), compiler_params=_cparams())(zqm, qg, km, vm)


def _mem_attn_bwd(zqm, dyc, qg, km, vm, name):
    t = zqm.shape[0]
    m = km.shape[0]
    tm = _row_tile(t, 256)

    def body(q_ref, dy_ref, qg_ref, k_ref, v_ref, dz_ref, dk_ref, dv_ref, dqg_ref):
        @pl.when(pl.program_id(0) == 0)
        def _():
            dk_ref[...] = jnp.zeros_like(dk_ref)
            dv_ref[...] = jnp.zeros_like(dv_ref)
            dqg_ref[...] = jnp.zeros_like(dqg_ref)

        qgv = qg_ref[...]
        dqg_acc = jnp.zeros((tm, LANES), F32)
        for h in range(MEM_HEADS):
            sl = slice(h * LANES, (h + 1) * LANES)
            qh = q_ref[:, sl]
            r = _rstd(qh, LANES)
            qn = (qh * r * qgv).astype(BF16)
            kh = k_ref[:, sl]
            p = _mem_softmax(qn, kh)
            dov = dy_ref[:, sl]
            dv_ref[:, sl] += _dot_tn(p.astype(BF16), dov)
            dp = _dot_nt(dov, v_ref[:, sl])
            ds = (p * (dp - jnp.sum(dp * p, axis=1, keepdims=True)) * (LANES ** -0.5)).astype(BF16)
            dk_ref[:, sl] += _dot_tn(ds, qn)
            dqh, dgr = _rms_vjp(qh, r, qgv, _dot(ds, kh), LANES)
            dz_ref[:, sl] = dqh.astype(BF16)
            dqg_acc += dgr
        dqg_ref[...] += jnp.sum(dqg_acc, axis=0, keepdims=True)

    row = pl.BlockSpec((tm, MEM_W), lambda i: (i, 0))
    full = lambda a: pl.BlockSpec(a.shape, lambda i: (0, 0))
    acc = pl.BlockSpec((m, MEM_W), lambda i: (0, 0))
    return pl.pallas_call(
        body, name=name, grid=(t // tm,), in_specs=[row, row, full(qg), full(km), full(vm)],
        out_specs=[row, acc, acc, pl.BlockSpec((1, LANES), lambda i: (0, 0))],
        out_shape=[jax.ShapeDtypeStruct((t, MEM_W), BF16), jax.ShapeDtypeStruct((m, MEM_W), F32),
                   jax.ShapeDtypeStruct((m, MEM_W), F32), jax.ShapeDtypeStruct((1, LANES), F32)],
        compiler_params=_cparams())(zqm, dyc, qg, km, vm)


def _mem_kv_bwd(mem, gmem, wkv, kg, dkn, dvm, name):
    m, d = mem.shape

    def body(mem_ref, g_ref, w_ref, kg_ref, dk_ref, dv_ref, dw_ref, dkg_ref, dg_ref, dkv_scr):
        xv = mem_ref[...]
        r = _rstd(xv, d)
        mn = (xv * r * g_ref[...]).astype(BF16)
        kvm = _dot(mn, w_ref[...])
        dkv_scr[:, MEM_W:] = dv_ref[...].astype(BF16)
        dkg_acc = jnp.zeros((m, LANES), F32)
        for h in range(MEM_HEADS):
            sl = slice(h * LANES, (h + 1) * LANES)
            kh = kvm[:, sl]
            dkh, dgr = _rms_vjp(kh, _rstd(kh, LANES), kg_ref[...], dk_ref[:, sl], LANES)
            dkv_scr[:, sl] = dkh.astype(BF16)
            dkg_acc += dgr
        dkg_ref[...] = jnp.sum(dkg_acc, axis=0, keepdims=True)
        dkv = dkv_scr[...]
        dw_ref[...] = _dot_tn(mn, dkv)
        dmn = _dot_nt(dkv, w_ref[...])
        dg_ref[...] = jnp.sum(dmn * xv * r, axis=0, keepdims=True)

    full = lambda a: pl.BlockSpec(a.shape, lambda i: (0, 0))
    return pl.pallas_call(
        body, name=name, grid=(1,),
        in_specs=[full(mem), full(gmem), full(wkv), full(kg), full(dkn), full(dvm)],
        out_specs=[pl.BlockSpec((d, 2 * MEM_W), lambda i: (0, 0)), pl.BlockSpec((1, LANES), lambda i: (0, 0)),
                   pl.BlockSpec((1, d), lambda i: (0, 0))],
        out_shape=[jax.ShapeDtypeStruct((d, 2 * MEM_W), F32), jax.ShapeDtypeStruct((1, LANES), F32),
                   jax.ShapeDtypeStruct((1, d), F32)],
        scratch_shapes=[pltpu.VMEM((m, 2 * MEM_W), BF16)],
        compiler_params=_cparams())(mem, gmem, wkv, kg, dkn, dvm)


def _merge_fwd(x1, ya, yb, yc, zg, bg, wa, wb, wc, wo, name):
    t, d = x1.shape
    tm = _row_tile(t, 256)

    def body(x_ref, ya_ref, yb_ref, yc_ref, zg_ref, bg_ref, wa_ref, wb_ref, wc_ref, wo_ref,
             x2_ref, mg_ref, pa_ref, pb_ref, pc_ref):
        merged = None
        for k, (y_ref, w_ref, p_ref) in enumerate(
                ((ya_ref, wa_ref, pa_ref), (yb_ref, wb_ref, pb_ref), (yc_ref, wc_ref, pc_ref))):
            sl = slice(k * d, (k + 1) * d)
            pr = _dot(y_ref[...], w_ref[...])
            p_ref[...] = pr.astype(BF16)
            term = jax.nn.sigmoid(zg_ref[:, sl] + bg_ref[:, sl]) * pr
            merged = term if merged is None else merged + term
        mb = merged.astype(BF16)
        mg_ref[...] = mb
        x2_ref[...] = x_ref[...] + _dot(mb, wo_ref[...])

    row = lambda n: pl.BlockSpec((tm, n), lambda i: (i, 0))
    full = lambda a: pl.BlockSpec(a.shape, lambda i: (0, 0))
    return pl.pallas_call(
        body, name=name, grid=(t // tm,),
        in_specs=[row(d), row(ya.shape[1]), row(yb.shape[1]), row(yc.shape[1]), row(3 * d), full(bg),
                  full(wa), full(wb), full(wc), full(wo)],
        out_specs=[row(d)] * 5,
        out_shape=[jax.ShapeDtypeStruct((t, d), F32)] + [jax.ShapeDtypeStruct((t, d), BF16)] * 4,
        compiler_params=_cparams())(x1, ya, yb, yc, zg, bg, wa, wb, wc, wo)


def _merge_bwd(dx2, pa, pb, pc, zg, bg, wa, wb, wc, wo, name, ex=None):
    t, d = dx2.shape
    tm = _row_tile(t, 256)

    def body(dx_ref, pa_ref, pb_ref, pc_ref, zg_ref, bg_ref, wa_ref, wb_ref, wc_ref, wo_ref,
             dpa_ref, dpb_ref, dpc_ref, dzg_ref, dbg_ref, dya_ref, dyb_ref, dyc_ref):
        @pl.when(pl.program_id(0) == 0)
        def _():
            dbg_ref[...] = jnp.zeros_like(dbg_ref)

        dm = _dot_nt(dx_ref[...].astype(BF16), wo_ref[...])
        for k, (p_ref, w_ref, dp_ref, dy_ref) in enumerate(
                ((pa_ref, wa_ref, dpa_ref, dya_ref), (pb_ref, wb_ref, dpb_ref, dyb_ref),
                 (pc_ref, wc_ref, dpc_ref, dyc_ref))):
            sl = slice(k * d, (k + 1) * d)
            gate = jax.nn.sigmoid(zg_ref[:, sl] + bg_ref[:, sl])
            dpr = (dm * gate).astype(BF16)
            dp_ref[...] = dpr
            dzg = dm * p_ref[...].astype(F32) * gate * (1.0 - gate)
            dzg_ref[:, sl] = dzg.astype(BF16)
            dbg_ref[:, sl] += jnp.sum(dzg, axis=0, keepdims=True)
            dy_ref[...] = _dot_nt(dpr, w_ref[...]).astype(dy_ref.dtype)

    row = lambda n: pl.BlockSpec((tm, n), lambda i: (i, 0))
    full = lambda a: pl.BlockSpec(a.shape, lambda i: (0, 0))
    na, nb, nc = wa.shape[0], wb.shape[0], wc.shape[0]
    return _call_with_exchange(
        ex, body, name, (t // tm,),
        [row(d), row(d), row(d), row(d), row(3 * d), full(bg), full(wa), full(wb), full(wc), full(wo)],
        [row(d), row(d), row(d), row(3 * d), pl.BlockSpec((1, 3 * d), lambda i: (0, 0)), row(na), row(nb), row(nc)],
        [jax.ShapeDtypeStruct((t, d), BF16)] * 3
        + [jax.ShapeDtypeStruct((t, 3 * d), BF16), jax.ShapeDtypeStruct((1, 3 * d), F32),
           jax.ShapeDtypeStruct((t, na), F32), jax.ShapeDtypeStruct((t, nb), BF16),
           jax.ShapeDtypeStruct((t, nc), BF16)],
        [], (dx2, pa, pb, pc, zg, bg, wa, wb, wc, wo))


def _adamw_math(w, g, m, v):
    bc1 = 1.0 - ADAM_B1 ** ADAM_STEP
    bc2 = 1.0 - ADAM_B2 ** ADAM_STEP
    nm = ADAM_B1 * m + (1.0 - ADAM_B1) * g
    nv = ADAM_B2 * v + (1.0 - ADAM_B2) * (g * g)
    delta = -ADAM_LR * ((nm / bc1) / (jnp.sqrt(nv / bc2) + ADAM_EPS) + ADAM_WD * w)
    return delta, nm, nv


def _div_tile(n, cap, mult):
    best = None
    for cand in range(mult, min(n, cap) + 1, mult):
        if n % cand == 0:
            best = cand
    assert best is not None, (n, cap, mult)
    return best


def _adamw(w, g, m, v, name):
    rows, cols = w.shape
    tr = rows if rows * cols <= 256 * 1024 else _div_tile(rows, 256, 8)

    def body(w_ref, g_ref, m_ref, v_ref, d_ref, nm_ref, nv_ref):
        d_ref[...], nm_ref[...], nv_ref[...] = _adamw_math(w_ref[...], g_ref[...], m_ref[...], v_ref[...])

    blk = pl.BlockSpec((tr, cols), lambda i: (i, 0))
    return pl.pallas_call(
        body, name=name, grid=(rows // tr,), in_specs=[blk] * 4, out_specs=[blk] * 3,
        out_shape=[jax.ShapeDtypeStruct((rows, cols), F32)] * 3, compiler_params=_cparams())(w, g, m, v)


def _adamw_slots(w, slots, m, v, name):
    _, hr, cols = w.shape
    tr = _div_tile(hr, 128, 16)

    def body(w_ref, s_ref, m_ref, v_ref, g_ref, d_ref, nm_ref, nv_ref):
        g = s_ref[0, 0].astype(F32)
        for k in range(1, N_CHIPS):
            g = g + s_ref[0, k].astype(F32)
        g_ref[0] = g
        d_ref[0], nm_ref[0], nv_ref[0] = _adamw_math(w_ref[0], g, m_ref[0], v_ref[0])

    blk = pl.BlockSpec((1, tr, cols), lambda h, i: (h, i, 0))
    return pl.pallas_call(
        body, name=name, grid=(2, hr // tr),
        in_specs=[blk, pl.BlockSpec((1, N_CHIPS, tr, cols), lambda h, i: (h, 0, i, 0)), blk, blk],
        out_specs=[blk] * 4, out_shape=[jax.ShapeDtypeStruct((2, hr, cols), F32)] * 4,
        compiler_params=_cparams())(w, slots, m, v)


ANY = pl.BlockSpec(memory_space=pl.ANY)


def _place():
    x, y, c = lax.axis_index("x"), lax.axis_index("y"), lax.axis_index("c")
    other_chips = [(1 - x, y), (x, 1 - y), (1 - x, 1 - y)]
    return x, y, c, other_chips


def _remote(src, dst, send_sem, recv_sem, to):
    return pltpu.make_async_remote_copy(src_ref=src, dst_ref=dst, send_sem=send_sem, recv_sem=recv_sem,
                                        device_id=to, device_id_type=MESH)


PIECE_BYTES = 384 * 1024


def _row_pieces(half_rows, cols):
    for n in (4, 2):
        if half_rows % (16 * n) == 0 and half_rows * cols * 2 // n >= PIECE_BYTES:
            return [pl.ds(k * (half_rows // n), half_rows // n) for k in range(n)]
    return [pl.ds(0, half_rows)]


def _pieces(arrays, rows_axis):
    return [(w, rows) for w, a in enumerate(arrays) for rows in _row_pieces(a.shape[rows_axis], a.shape[-1])]


def _gather_exchange(shards):
    nw = len(shards)
    pieces = _pieces(shards, 1)
    npc = len(pieces)

    def build(s_refs, g_refs, sems):
        send_sems, recv_sems, local_sems = sems
        x, y, c, chips = _place()
        me = 2 * x + y
        sibling = (x, y, 1 - c)
        mine = [pltpu.make_async_copy(s_refs[w], g_refs[w].at[me], local_sems.at[w]) for w in range(nw)]
        first = [_remote(s_refs[w].at[c, rows], g_refs[w].at[me, c, rows], send_sems.at[k, p], recv_sems.at[k, p],
                         (cx, cy, c)) for k, (cx, cy) in enumerate(chips) for p, (w, rows) in enumerate(pieces)]

        def start():
            for cp in mine + first:
                cp.start()

        arrived = [g_refs[w].at[2 * cx + cy, c, rows] for cx, cy in chips for w, rows in pieces]
        passed = [_remote(slab, slab, send_sems.at[3 + q // npc, q % npc], recv_sems.at[3 + q // npc, q % npc], sibling)
                  for q, slab in enumerate(arrived)]

        def pass_on():
            for q, slab in enumerate(arrived):
                k, p = q // npc, q % npc
                _remote(slab, slab, send_sems.at[k, p], recv_sems.at[k, p], (*chips[k], c)).wait_recv()
                passed[q].start()

        def finish():
            for k, (cx, cy) in enumerate(chips):
                for p, (w, rows) in enumerate(pieces):
                    slab = g_refs[w].at[2 * cx + cy, 1 - c, rows]
                    _remote(slab, slab, send_sems.at[3 + k, p], recv_sems.at[3 + k, p], sibling).wait_recv()
            for cp in first + passed:
                cp.wait_send()
            for cp in mine:
                cp.wait()

        return start, pass_on, finish

    return _Exchange(list(shards), [jax.ShapeDtypeStruct((N_CHIPS,) + s.shape, BF16) for s in shards],
                     [pltpu.SemaphoreType.DMA((6, npc)), pltpu.SemaphoreType.DMA((6, npc)),
                      pltpu.SemaphoreType.DMA((nw,))], build)


def _swap_halves(grads, name):
    nw = len(grads)

    def body(*refs):
        g_refs, sib_refs = refs[:nw], refs[nw:2 * nw]
        send_sems, recv_sems = refs[2 * nw:]
        x, y, c, _ = _place()
        copies = [_remote(g_refs[w].at[s, 1 - c], sib_refs[w].at[s], send_sems.at[s, w], recv_sems.at[s, w],
                          (x, y, 1 - c)) for w in range(nw) for s in range(N_CHIPS)]
        for cp in copies:
            cp.start()
        for cp in copies:
            cp.wait_recv()
        for cp in copies:
            cp.wait_send()

    return pl.pallas_call(
        body, name=name, in_specs=[ANY] * nw, out_specs=[ANY] * nw,
        out_shape=[jax.ShapeDtypeStruct((N_CHIPS,) + g.shape[2:], BF16) for g in grads],
        scratch_shapes=[pltpu.SemaphoreType.DMA((N_CHIPS, nw)), pltpu.SemaphoreType.DMA((N_CHIPS, nw))],
    )(*grads)


def _pair_sum(grad, sib, core, name):
    nchip, _, hr, cols = grad.shape
    tr = _div_tile(hr, 256, 16)

    def body(core_ref, a_ref, b_ref, o_ref):
        o_ref[...] = (a_ref[0].astype(F32) + b_ref[...].astype(F32)).astype(BF16)

    return pl.pallas_call(
        body, name=name,
        grid_spec=pltpu.PrefetchScalarGridSpec(
            num_scalar_prefetch=1, grid=(nchip, hr // tr),
            in_specs=[pl.BlockSpec((1, 1, tr, cols), lambda s, i, core_r: (s, core_r[0], i, 0)),
                      pl.BlockSpec((1, tr, cols), lambda s, i, core_r: (s, i, 0))],
            out_specs=pl.BlockSpec((1, tr, cols), lambda s, i, core_r: (s, i, 0))),
        out_shape=jax.ShapeDtypeStruct((nchip, hr, cols), BF16), compiler_params=_cparams())(core, grad, sib)


def _pair_sum_exchange(sums):
    nw = len(sums)
    pieces = _pieces(sums, 1)
    npc = len(pieces)

    def build(p_refs, o_refs, sems):
        send_sems, recv_sems, local_sems = sems
        x, y, c, chips = _place()
        me = 2 * x + y
        sibling = (x, y, 1 - c)
        mine = [pltpu.make_async_copy(p_refs[w].at[me], o_refs[w].at[c, 3], local_sems.at[w]) for w in range(nw)]
        first = [_remote(p_refs[w].at[2 * cx + cy, rows], o_refs[w].at[c, k, rows], send_sems.at[k, p],
                         recv_sems.at[k, p], (cx, cy, c))
                 for k, (cx, cy) in enumerate(chips) for p, (w, rows) in enumerate(pieces)]

        def start():
            for cp in mine + first:
                cp.start()

        passed = [_remote(o_refs[w].at[c, k, rows], o_refs[w].at[c, k, rows], send_sems.at[3 + k, p],
                          recv_sems.at[3 + k, p], sibling) for k in range(N_CHIPS) for p, (w, rows) in enumerate(pieces)]

        def pass_on():
            for k in range(N_CHIPS):
                own_waited = set()
                for p, (w, rows) in enumerate(pieces):
                    if k < 3:
                        first[k * npc + p].wait_recv()
                    elif w not in own_waited:
                        mine[w].wait()
                        own_waited.add(w)
                    passed[k * npc + p].start()

        def finish():
            for k in range(N_CHIPS):
                for p, (w, rows) in enumerate(pieces):
                    slab = o_refs[w].at[1 - c, k, rows]
                    _remote(slab, slab, send_sems.at[3 + k, p], recv_sems.at[3 + k, p], sibling).wait_recv()
            for cp in first + passed:
                cp.wait_send()

        return start, pass_on, finish

    return _Exchange(list(sums), [jax.ShapeDtypeStruct((2,) + p.shape, BF16) for p in sums],
                     [pltpu.SemaphoreType.DMA((7, npc)), pltpu.SemaphoreType.DMA((7, npc)),
                      pltpu.SemaphoreType.DMA((nw,))], build)


def _small_sum_exchange(vec):
    m_per, n = vec.shape

    def build(ins, outs, scr):
        (x_ref,), (out_ref,) = ins, outs
        gath_ref, sum_ref, send_sems, recv_sems, local_sem, out_sem = scr
        x, y, c, chips = _place()
        me, sibling = (x, y, c), (x, y, 1 - c)

        def rows(px, py, pc):
            return gath_ref.at[pl.ds((4 * px + 2 * py + pc) * m_per, m_per), :]

        def copy(k, block, to, src=None):
            return pltpu.make_async_remote_copy(
                src_ref=rows(*block) if src is None else src, dst_ref=rows(*block),
                send_sem=send_sems.at[k], recv_sem=recv_sems.at[k], device_id=to, device_id_type=MESH)

        mine = pltpu.make_async_copy(x_ref, rows(*me), local_sem)
        first = [copy(0, me, sibling, src=x_ref)] + [copy(1 + j, me, (*chip, c), src=x_ref)
                                                     for j, chip in enumerate(chips)]

        def start():
            for cp in [mine] + first:
                cp.start()

        passed = [copy(4 + j, (*chip, c), sibling) for j, chip in enumerate(chips)]

        def pass_on():
            for j, chip in enumerate(chips):
                copy(1 + j, (*chip, c), me).wait_recv()
                passed[j].start()

        def finish():
            copy(0, sibling, me).wait_recv()
            for j, chip in enumerate(chips):
                copy(4 + j, (*chip, 1 - c), me).wait_recv()
            for cp in first + passed:
                cp.wait_send()
            mine.wait()
            acc = gath_ref[pl.ds(0, m_per), :]
            for k in range(1, N_DEV):
                acc = acc + gath_ref[pl.ds(k * m_per, m_per), :]
            sum_ref[...] = acc
            done = pltpu.make_async_copy(sum_ref, out_ref, out_sem)
            done.start()
            done.wait()

        return start, pass_on, finish

    return _Exchange([vec], [jax.ShapeDtypeStruct((m_per, n), F32)],
                     [pltpu.VMEM((N_DEV * m_per, n), F32), pltpu.VMEM((m_per, n), F32), pltpu.SemaphoreType.DMA((7,)),
                      pltpu.SemaphoreType.DMA((7,)), pltpu.SemaphoreType.DMA, pltpu.SemaphoreType.DMA], build)


def _pack_small(vals, tail=()):
    flat = jnp.concatenate([vals[name].reshape(-1).astype(F32) for name, _ in SMALL] + [v.reshape(1) for v in tail])
    flat = jnp.pad(flat, (0, SMALL_ROWS * LANES - flat.shape[0]))
    return flat.reshape(SMALL_ROWS, LANES)


def _unpack_small(packed):
    flat = packed.reshape(-1)
    out, off = {}, 0
    for name, shape in SMALL:
        n = int(np.prod(shape))
        out[name] = flat[off:off + n].reshape(shape)
        off += n
    return out


def _head_pad_cols(w, heads, real):
    k = w.shape[0]
    return jnp.pad(w.reshape(k, heads, real), ((0, 0), (0, 0), (0, LANES - real))).reshape(k, heads * LANES)


def _rope_tables(positions):
    half = MLA_ROPE // 2
    inv = ROPE_BASE ** (-jnp.arange(half, dtype=F32) / half)
    ang = positions.astype(F32)[:, None] * inv
    cos, sin = jnp.cos(ang), jnp.sin(ang)
    t = positions.shape[0]
    z = lambda n: jnp.zeros((t, n), F32)
    rc = jnp.concatenate([jnp.ones((t, MLA_NOPE), F32), cos, cos, z(LANES - MLA_QK)], axis=1)
    rs1 = jnp.concatenate([z(MLA_NOPE), -sin, z(LANES - MLA_NOPE - half)], axis=1)
    rs2 = jnp.concatenate([z(MLA_NOPE + half), sin, z(LANES - MLA_QK)], axis=1)
    return rc, rs1, rs2


FFN1_WEIGHTS = ("ffn1_w_gu", "ffn1_w_down")
FFN2_WEIGHTS = ("ffn2_w_gu", "ffn2_w_down")
MIXER_WEIGHTS = tuple(n for n, *_ in SHARDED if n not in FFN1_WEIGHTS + FFN2_WEIGHTS)
SHARD_SHAPE = {n: (r, c, kind) for n, r, c, kind in SHARDED}


def _from_blocks(name, gathered):
    r, c, kind = SHARD_SHAPE[name]
    blk = gathered.reshape(N_CHIPS, r, c)
    return blk, (blk.transpose(1, 0, 2).reshape(r, N_CHIPS * c) if kind == "col" else blk.reshape(N_CHIPS * r, c))


def _grad_pair_sums(names, gw, core, tag):
    by_owner = []
    for name in names:
        r, c, kind = SHARD_SHAPE[name]
        if gw[name].dtype == BF16:
            blk = gw[name]
        elif kind == "col":
            blk = gw[name].reshape(r, N_CHIPS, c).transpose(1, 0, 2)
        else:
            blk = gw[name].reshape(N_CHIPS, r, c)
        by_owner.append(blk.astype(BF16).reshape(N_CHIPS, 2, r // 2, c))
    received = _swap_halves(by_owner, "grad_swap_" + tag)
    return [_pair_sum(g, s, core, "pair_sum_" + n) for g, s, n in zip(by_owner, received, names)]


def _device_step(x, mem, positions, tgt, small, shards, core):
    d = D_MODEL
    g_ffn1, g_mix, g_ffn2 = small["ffn1_norm"], small["mix_norm"], small["ffn2_norm"]
    big = {}
    for name, g in zip(FFN1_WEIGHTS, _run_exchange(_gather_exchange([shards[n] for n in FFN1_WEIGHTS]), "gather_ffn1")):
        big[name + "#blocks"], big[name] = _from_blocks(name, g)
    wgu1, wd1 = big["ffn1_w_gu#blocks"], big["ffn1_w_down"].reshape(2, FF_TILE, d)
    x1, gpre1, upre1, h, *rest = _ffn_fwd(x, g_ffn1, wgu1, wd1, "ffn1_fwd", next_gain=g_mix,
                                          ex=_gather_exchange([shards[n] for n in MIXER_WEIGHTS]))
    for name, g in zip(MIXER_WEIGHTS, rest):
        big[name + "#blocks"], big[name] = _from_blocks(name, g)
    w_in = big["w_in"]
    w_uv_, w_cq, w_ckv = w_in[:, :COL_CQ], w_in[:, COL_CQ:COL_CKV], w_in[:, COL_CKV:COL_KR]
    w_kr = jnp.pad(w_in[:, COL_KR:COL_QM], ((0, 0), (MLA_NOPE, LANES - MLA_QK)))
    w_qm, w_g = w_in[:, COL_QM:COL_GATE], w_in[:, COL_GATE:]
    segs = (w_uv_, w_cq, w_ckv, w_kr, w_qm, w_g)
    wuq = _head_pad_cols(big["mla_w_uq"], MLA_HEADS, MLA_QK)
    ukv = big["mla_w_ukv"].reshape(MLA_KV_RANK, MLA_HEADS, 2, MLA_NOPE)
    wuk = _head_pad_cols(ukv[:, :, 0].reshape(MLA_KV_RANK, -1), MLA_HEADS, MLA_NOPE)
    wuv = _head_pad_cols(ukv[:, :, 1].reshape(MLA_KV_RANK, -1), MLA_HEADS, MLA_NOPE)
    wkv = big["mem_w_kv"]
    wa, wc, wo = big["w_branch_a"], big["w_branch_c"], big["w_out"]
    wb = jnp.pad(big["w_branch_b"].reshape(MLA_HEADS, MLA_NOPE, d),
                 ((0, 0), (0, LANES - MLA_NOPE), (0, 0))).reshape(MLA_HEADS * LANES, d)
    qg = jnp.pad(small["mla_q_norm"], ((0, 0), (0, LANES - MLA_QK)))
    kg = jnp.pad(small["mla_k_norm"], ((0, 0), (0, LANES - MLA_QK)))
    causal = jnp.tril(jnp.ones((CHUNK, CHUNK), bool))
    wt_f = jnp.where(causal[None], small["sg_w"][0], 0.0)
    wt, wt_t = wt_f.astype(BF16), wt_f.transpose(0, 2, 1).astype(BF16)
    bias_l = jnp.repeat(small["sg_b"][0].T, 64, axis=1)
    rc, rs1, rs2 = _rope_tables(positions)

    zuv, zcq, zckv, zkr, zqm, zg = _mm_cols(h, segs, [F32] * 5 + [BF16], "in_proj")
    ya = _sgu_fwd(zuv, small["sg_ln_g"], small["sg_ln_b"], wt, bias_l, "sgu_fwd")
    q, k, v, cqn, ckvn = _mla_prep_fwd(zcq, zckv, zkr, small["mla_cq_norm"], small["mla_ckv_norm"], qg, kg,
                                       wuq, wuk, wuv, rc, rs1, rs2, "mla_prep_fwd")
    yb, lse, *rest = _attn_fwd(q, k, v, "mla_attn_fwd", ex=_gather_exchange([shards[n] for n in FFN2_WEIGHTS]))
    for name, g in zip(FFN2_WEIGHTS, rest):
        big[name + "#blocks"], big[name] = _from_blocks(name, g)
    wgu2, wd2 = big["ffn2_w_gu#blocks"], big["ffn2_w_down"].reshape(2, FF_TILE, d)
    km, vm, memn = _mem_kv_fwd(mem, small["mem_norm"], wkv, small["mem_k_norm"], "mem_kv_fwd")
    yc = _mem_attn_fwd(zqm, small["mem_q_norm"], km, vm, "mem_attn_fwd")
    x2, merged, pa, pb, pc = _merge_fwd(x1, ya, yb, yc, zg, small["b_gate"], wa, wb, wc, wo, "merge_fwd")
    dy, loss_row, gpre2, upre2 = _ffn_fwd(x2, g_ffn2, wgu2, wd2, "ffn2_fwd", target=tgt)

    gw, gs, slots = {}, {}, {}

    def ffn_grads(prefix, xin, gain, dyin, gpre, upre, wgu, wd, ex=None, ex_names=(), last=False):
        dx, dgain, xn, dgt, dup, act, *got = _ffn_bwd(xin, gain, dyin, gpre, upre, wgu, wd, prefix + "_bwd", ex=ex)
        slots.update(zip(ex_names, got))
        gs[prefix + "_norm"] = dgain
        gw[prefix + "_w_gu"] = jnp.concatenate(
            [_mm_tn(xn, dgt, prefix + "_dwg", col_blocks=True, out_dtype=BF16),
             _mm_tn(xn, dup, prefix + "_dwu", col_blocks=True, out_dtype=BF16)], axis=0)
        rows_down = SHARD_SHAPE[prefix + "_w_down"][0]
        if last:
            small_sum = _small_sum_exchange(_pack_small(gs, tail=[loss_row[0, 0]]))
            dwd, summed = _mm_tn(act, dyin, prefix + "_dwd", scale=0.5, ex=small_sum, out_dtype=BF16)
            gw[prefix + "_w_down"] = dwd.reshape(N_CHIPS, rows_down, d)
            return dx, summed
        gw[prefix + "_w_down"] = _mm_tn(act, dyin, prefix + "_dwd", scale=0.5, out_dtype=BF16).reshape(
            N_CHIPS, rows_down, d)
        return dx

    dx2 = ffn_grads("ffn2", x2, g_ffn2, dy, gpre2, upre2, wgu2, wd2)
    ffn2_sums = _pair_sum_exchange(_grad_pair_sums(FFN2_WEIGHTS, gw, core, "ffn2"))
    dpa, dpb, dpc, dzg, dbg, dya, dyb, dyc, *got = _merge_bwd(dx2, pa, pb, pc, zg, small["b_gate"], wa, wb, wc, wo,
                                                              "merge_bwd", ex=ffn2_sums)
    slots.update(zip(FFN2_WEIGHTS, got))
    gs["b_gate"] = dbg
    gw["w_out"] = _mm_tn(merged, dx2, "dw_out")
    gw["w_branch_a"] = _mm_tn(ya, dpa, "dw_branch_a")
    gw["w_branch_b"] = _mm_tn(yb, dpb, "dw_branch_b").reshape(MLA_HEADS, LANES, d)[:, :MLA_NOPE].reshape(-1, d)
    gw["w_branch_c"] = _mm_tn(yc, dpc, "dw_branch_c")

    dzuv, dwt, dbl, dlg, dlb = _sgu_bwd(zuv, dya, small["sg_ln_g"], small["sg_ln_b"], wt, wt_t, bias_l, "sgu_bwd")
    gs["sg_w"], gs["sg_b"] = dwt[None], dbl[:, :SG_GROUPS].T[None]
    gs["sg_ln_g"], gs["sg_ln_b"] = dlg, dlb

    delta_rows, lse_rows = _attn_bwd_rows(yb, lse, dyb, "mla_attn_bwd_rows")
    dq, dk, dv = _attn_bwd(q, k, v, delta_rows, lse_rows, dyb, "mla_attn_bwd")
    dzcq, dzckv, dzkr, dql, dkl, dgcq, dgckv, dqg, dkg = _mla_prep_bwd(
        zcq, zckv, zkr, small["mla_cq_norm"], small["mla_ckv_norm"], qg, kg, wuq, wuk, wuv, rc, rs1, rs2,
        dq, dk, dv, "mla_prep_bwd")
    gs["mla_cq_norm"], gs["mla_ckv_norm"] = dgcq, dgckv
    gs["mla_q_norm"], gs["mla_k_norm"] = dqg[:, :MLA_QK], dkg[:, :MLA_QK]
    gw["mla_w_uq"] = _mm_tn(cqn, dql, "dw_uq").reshape(MLA_Q_RANK, MLA_HEADS, LANES)[:, :, :MLA_QK].reshape(
        MLA_Q_RANK, -1)
    dwuk = _mm_tn(ckvn, dkl, "dw_uk").reshape(MLA_KV_RANK, MLA_HEADS, LANES)[:, :, :MLA_NOPE]
    dwuv = _mm_tn(ckvn, dv, "dw_uv").reshape(MLA_KV_RANK, MLA_HEADS, LANES)[:, :, :MLA_NOPE]
    gw["mla_w_ukv"] = jnp.concatenate([dwuk, dwuv], axis=2).reshape(MLA_KV_RANK, -1)

    dzqm, dkn, dvm, dmqg = _mem_attn_bwd(zqm, dyc, small["mem_q_norm"], km, vm, "mem_attn_bwd")
    gs["mem_q_norm"] = dmqg
    gw["mem_w_kv"], gs["mem_k_norm"], gs["mem_norm"] = _mem_kv_bwd(
        mem, small["mem_norm"], wkv, small["mem_k_norm"], dkn, dvm, "mem_kv_bwd")

    dzs = (dzuv, dzcq, dzckv, dzkr, dzqm, dzg)
    dws = list(_mm_tn_cols(h, dzs[:5], "dw_in_narrow")) + [_mm_tn(h, dzg, "dw_in_gate")]
    dws[3] = dws[3][:, MLA_NOPE:MLA_QK]
    gw["w_in"] = jnp.concatenate(dws, axis=1)
    dx1, gs["mix_norm"] = _proj_norm_bwd(dzs, [w.T for w in segs], x1, g_mix, dx2, "in_proj_bwd")
    mixer_sums = _pair_sum_exchange(_grad_pair_sums(MIXER_WEIGHTS, gw, core, "mixer"))
    dx, summed = ffn_grads("ffn1", x, g_ffn1, dx1, gpre1, upre1, wgu1, wd1, ex=mixer_sums, ex_names=MIXER_WEIGHTS,
                           last=True)
    ffn1_sums = _pair_sum_exchange(_grad_pair_sums(FFN1_WEIGHTS, gw, core, "ffn1"))
    slots.update(zip(FFN1_WEIGHTS, _run_exchange(ffn1_sums, "grad_exchange_ffn1")))
    return dx, slots, summed


def kernel(x, mem, positions, ffn1_norm, ffn1_w_gu, ffn1_w_down, mix_norm, w_in, b_gate, sg_ln_g, sg_ln_b, sg_w, sg_b, mla_cq_norm, mla_w_uq, mla_ckv_norm, mla_w_ukv, mla_q_norm, mla_k_norm, mem_norm, mem_w_kv, mem_q_norm, mem_k_norm, w_branch_a, w_branch_b, w_branch_c, w_out, ffn2_norm, ffn2_w_gu, ffn2_w_down, loss_target, m_ffn1_norm, m_ffn1_w_gu, m_ffn1_w_down, m_mix_norm, m_w_in, m_b_gate, m_sg_ln_g, m_sg_ln_b, m_sg_w, m_sg_b, m_mla_cq_norm, m_mla_w_uq, m_mla_ckv_norm, m_mla_w_ukv, m_mla_q_norm, m_mla_k_norm, m_mem_norm, m_mem_w_kv, m_mem_q_norm, m_mem_k_norm, m_w_branch_a, m_w_branch_b, m_w_branch_c, m_w_out, m_ffn2_norm, m_ffn2_w_gu, m_ffn2_w_down, v_ffn1_norm, v_ffn1_w_gu, v_ffn1_w_down, v_mix_norm, v_w_in, v_b_gate, v_sg_ln_g, v_sg_ln_b, v_sg_w, v_sg_b, v_mla_cq_norm, v_mla_w_uq, v_mla_ckv_norm, v_mla_w_ukv, v_mla_q_norm, v_mla_k_norm, v_mem_norm, v_mem_w_kv, v_mem_q_norm, v_mem_k_norm, v_w_branch_a, v_w_branch_b, v_w_branch_c, v_w_out, v_ffn2_norm, v_ffn2_w_gu, v_ffn2_w_down):
    args = dict(locals())
    weights = {n: args[n] for n in WEIGHT_ORDER}
    mom_m = {n: args["m_" + n] for n in WEIGHT_ORDER}
    mom_v = {n: args["v_" + n] for n in WEIGHT_ORDER}
    small = {n: weights[n] for n, _ in SMALL}
    halves = lambda a, r, c: a.reshape(2, r // 2, c)

    shards = {n: halves(weights[n][0].astype(BF16), r, c) for n, r, c, _ in SHARDED}
    core = lax.axis_index("c").astype(jnp.int32).reshape(1)
    dx, slots, summed = _device_step(x[0], mem[0], positions[0], loss_target[0], small, shards, core)
    loss = summed.reshape(-1)[_N_SMALL]
    small_grads = _unpack_small(summed)

    grads, deltas, new_m, new_v = {}, {}, {}, {}
    for name, r, c, _ in SHARDED:
        outs = _adamw_slots(halves(weights[name][0], r, c), slots[name], halves(mom_m[name][0], r, c),
                            halves(mom_v[name][0], r, c), "adamw_" + name)
        shape = weights[name].shape
        grads[name], deltas[name], new_m[name], new_v[name] = [o.reshape(shape) for o in outs]
    dlt, nm, nv = _adamw(_pack_small(small), _pack_small(small_grads), _pack_small({n: mom_m[n] for n, _ in SMALL}),
                         _pack_small({n: mom_v[n] for n, _ in SMALL}), "adamw_small")
    for name, _ in SMALL:
        grads[name] = small_grads[name]
    deltas.update(_unpack_small(dlt))
    new_m.update(_unpack_small(nm))
    new_v.update(_unpack_small(nv))

    return (loss, dx[None], *[grads[n] for n in WEIGHT_ORDER], *[deltas[n] for n in WEIGHT_ORDER],
            *[new_m[n] for n in WEIGHT_ORDER], *[new_v[n] for n in WEIGHT_ORDER])
```

```python
import functools
from typing import Callable, NamedTuple

import numpy as np
import jax
import jax.numpy as jnp
from jax import lax
from jax.experimental import pallas as pl
from jax.experimental.pallas import tpu as pltpu

F32 = jnp.float32
BF16 = jnp.bfloat16

D_MODEL = 1024
D_FF = 2816
FF_TILE = 1408
SG_WIDTH = 512
SG_GROUPS = 8
CHUNK = 128
MLA_HEADS = 8
MLA_QK = 96
MLA_NOPE = 64
MLA_ROPE = 32
MLA_Q_RANK = 384
MLA_KV_RANK = 256
MEM_HEADS = 4
MEM_LEN = 256
LANES = 128
EPS = 1e-6
NEG = -1e30
ROPE_BASE = 10000.0
N_CHIPS = 4
N_DEV = 8

ADAM_LR = 0.001
ADAM_B1 = 0.9
ADAM_B2 = 0.999
ADAM_EPS = 1e-08
ADAM_WD = 0.01
ADAM_STEP = 10

COL_V = 512
COL_CQ = 1024
COL_CKV = 1408
COL_KR = 1664
COL_QM = 1696
COL_GATE = 2208
IN_COLS = 5280

VMEM_LIMIT_BYTES = 56 * 1024 * 1024
INV_SQRT2 = 0.7071067811865476
INV_SQRT_2PI = 0.3989422804014327
LOG2E = 1.4426950408889634
ATTN_SCALE = MLA_QK ** -0.5
V_ONES_LANE = 64
ATTN_SCALE2 = ATTN_SCALE * LOG2E

SHARDED = (
    ("ffn1_w_gu", 1024, 1408, "col"),
    ("ffn1_w_down", 704, 1024, "row"),
    ("w_in", 1024, 1320, "col"),
    ("mla_w_uq", 384, 192, "col"),
    ("mla_w_ukv", 256, 256, "col"),
    ("mem_w_kv", 256, 1024, "row"),
    ("w_branch_a", 512, 256, "col"),
    ("w_branch_b", 512, 256, "col"),
    ("w_branch_c", 512, 256, "col"),
    ("w_out", 256, 1024, "row"),
    ("ffn2_w_gu", 1024, 1408, "col"),
    ("ffn2_w_down", 704, 1024, "row"),
)
SMALL = (
    ("ffn1_norm", (1, 1024)), ("mix_norm", (1, 1024)), ("b_gate", (1, 3072)),
    ("sg_ln_g", (1, 512)), ("sg_ln_b", (1, 512)), ("sg_w", (1, 8, 128, 128)),
    ("sg_b", (1, 8, 128)), ("mla_cq_norm", (1, 384)), ("mla_ckv_norm", (1, 256)),
    ("mla_q_norm", (1, 96)), ("mla_k_norm", (1, 96)), ("mem_norm", (1, 1024)),
    ("mem_q_norm", (1, 128)), ("mem_k_norm", (1, 128)), ("ffn2_norm", (1, 1024)),
)
WEIGHT_ORDER = (
    "ffn1_norm", "ffn1_w_gu", "ffn1_w_down", "mix_norm", "w_in", "b_gate", "sg_ln_g", "sg_ln_b",
    "sg_w", "sg_b", "mla_cq_norm", "mla_w_uq", "mla_ckv_norm", "mla_w_ukv", "mla_q_norm",
    "mla_k_norm", "mem_norm", "mem_w_kv", "mem_q_norm", "mem_k_norm", "w_branch_a", "w_branch_b",
    "w_branch_c", "w_out", "ffn2_norm", "ffn2_w_gu", "ffn2_w_down",
)

_N_SMALL = sum(int(np.prod(s)) for _, s in SMALL)
SMALL_ROWS = -(-_N_SMALL // (LANES * 8)) * 8

MESH = pl.DeviceIdType.MESH


def _cparams():
    return pltpu.CompilerParams(vmem_limit_bytes=VMEM_LIMIT_BYTES)


def _dot(a, b):
    return jnp.dot(a, b, preferred_element_type=F32)


def _dot_nt(a, b):
    return lax.dot_general(a, b, (((1,), (1,)), ((), ())), preferred_element_type=F32)


def _dot_tn(a, b):
    return lax.dot_general(a, b, (((0,), (0,)), ((), ())), preferred_element_type=F32)


def _gelu(x):
    return 0.5 * x * (1.0 + lax.erf(x * INV_SQRT2))


def _gelu_grad(x):
    return 0.5 * (1.0 + lax.erf(x * INV_SQRT2)) + x * jnp.exp(-0.5 * x * x) * INV_SQRT_2PI


def _rstd(x, n):
    return lax.rsqrt(jnp.sum(x * x, axis=-1, keepdims=True) * (1.0 / n) + EPS)


def _rms_vjp(x, r, g, dy, n):
    dxh = dy * g
    dx = r * dxh - x * (r * r * r) * (jnp.sum(dxh * x, axis=-1, keepdims=True) * (1.0 / n))
    return dx, dy * x * r


def _row_tile(t, want):
    return min(t, want)


def _wide_tile(n):
    if n <= 1024:
        return n
    if n % 1024 == 0:
        return 1024
    assert n % FF_TILE == 0, n
    return FF_TILE


def _mm_cols(a, ws, out_dtypes, name, ex=None):
    t, kdim = a.shape
    tm = _row_tile(t, 512)
    n = len(ws)

    def body(*refs):
        av = refs[0][...]
        for w_ref, o_ref in zip(refs[1:1 + n], refs[1 + n:]):
            o_ref[...] = _dot(av, w_ref[...]).astype(o_ref.dtype)

    row = lambda width: pl.BlockSpec((tm, width), lambda i: (i, 0))
    return _call_with_exchange(
        ex, body, name, (t // tm,),
        [row(kdim)] + [pl.BlockSpec(w.shape, lambda i: (0, 0)) for w in ws],
        [row(w.shape[1]) for w in ws],
        [jax.ShapeDtypeStruct((t, w.shape[1]), dt) for w, dt in zip(ws, out_dtypes)], [], (a, *ws))


def _proj_norm_bwd(dzs, wts, x, g, dres, name):
    t, d = x.shape
    tm = _row_tile(t, 256)
    n = len(dzs)

    def body(*refs):
        x_ref, g_ref, r_ref, dx_ref, dg_ref = refs[2 * n:]

        @pl.when(pl.program_id(0) == 0)
        def _():
            dg_ref[...] = jnp.zeros_like(dg_ref)

        dh = None
        for dz_ref, w_ref in zip(refs[:n], refs[n:2 * n]):
            part = _dot(dz_ref[...], w_ref[...])
            dh = part if dh is None else dh + part
        xv = x_ref[...]
        dx, dgr = _rms_vjp(xv, _rstd(xv, d), g_ref[...], dh, d)
        dx_ref[...] = r_ref[...] + dx
        dg_ref[...] += jnp.sum(dgr, axis=0, keepdims=True)

    row = lambda width: pl.BlockSpec((tm, width), lambda i: (i, 0))
    vec = pl.BlockSpec((1, d), lambda i: (0, 0))
    return pl.pallas_call(
        body, name=name, grid=(t // tm,),
        in_specs=[row(dz.shape[1]) for dz in dzs] + [pl.BlockSpec(w.shape, lambda i: (0, 0)) for w in wts]
        + [row(d), vec, row(d)],
        out_specs=[row(d), vec],
        out_shape=[jax.ShapeDtypeStruct((t, d), F32), jax.ShapeDtypeStruct((1, d), F32)],
        compiler_params=_cparams())(*dzs, *wts, x, g, dres)


def _mm_tn_cols(a, bs, name):
    t, m = a.shape
    tk = _row_tile(t, 1024)
    n = len(bs)

    def body(*refs):
        @pl.when(pl.program_id(0) == 0)
        def _():
            for o_ref in refs[1 + n:]:
                o_ref[...] = jnp.zeros_like(o_ref)

        av = refs[0][...].astype(BF16)
        for b_ref, o_ref in zip(refs[1:1 + n], refs[1 + n:]):
            o_ref[...] += _dot_tn(av, b_ref[...].astype(BF16))

    row = lambda width: pl.BlockSpec((tk, width), lambda k: (k, 0))
    return pl.pallas_call(
        body, name=name, grid=(t // tk,), in_specs=[row(m)] + [row(b.shape[1]) for b in bs],
        out_specs=[pl.BlockSpec((m, b.shape[1]), lambda k: (0, 0)) for b in bs],
        out_shape=[jax.ShapeDtypeStruct((m, b.shape[1]), F32) for b in bs],
        compiler_params=_cparams())(a, *bs)


def _mm_tn(a, b, name, scale=1.0, ex=None, col_blocks=False, out_dtype=F32):
    t, m = a.shape
    n = b.shape[1]
    tm, tn = _wide_tile(m), _wide_tile(n)
    tk = _row_tile(t, 2048)
    nk = t // tk
    in_place = out_dtype == F32

    def body(a_ref, b_ref, o_ref, *scr):
        k = pl.program_id(2)
        acc_ref = o_ref if in_place else scr[0]

        @pl.when(k == 0)
        def _():
            acc_ref[...] = jnp.zeros_like(acc_ref)

        prod = _dot_tn(a_ref[...].astype(BF16), b_ref[...].astype(BF16))
        acc_ref[...] += prod.reshape(acc_ref.shape)
        if scale != 1.0 or not in_place:
            @pl.when(k == nk - 1)
            def _():
                o_ref[...] = (acc_ref[...] * scale).astype(out_dtype).reshape(o_ref.shape)

    if col_blocks:
        out_spec = pl.BlockSpec((1, tm, tn), lambda i, j, k: (j, i, 0))
        out_shape = jax.ShapeDtypeStruct((n // tn, m, tn), out_dtype)
    else:
        out_spec = pl.BlockSpec((tm, tn), lambda i, j, k: (i, j))
        out_shape = jax.ShapeDtypeStruct((m, n), out_dtype)
    outs = _call_with_exchange(
        ex, body, name, (m // tm, n // tn, nk),
        [pl.BlockSpec((tk, tm), lambda i, j, k: (k, i)), pl.BlockSpec((tk, tn), lambda i, j, k: (k, j))],
        [out_spec], [out_shape], [] if in_place else [pltpu.VMEM((tm, tn), F32)], (a, b))
    return outs[0] if ex is None else outs


PASS_ON_STEPS_BEFORE_END = 8


class _Exchange(NamedTuple):
    operands: list
    out_shapes: list
    sem_shapes: list
    build: Callable


def _call_with_exchange(ex, body, name, grid, in_specs, out_specs, out_shape, scratch_shapes, operands, prefetch=()):
    n_pre = len(prefetch)
    total = int(np.prod(grid))
    pass_step = max(total // 2, total - PASS_ON_STEPS_BEFORE_END)

    def call(kernel, ins, outs, shapes, scratch):
        if n_pre:
            spec = pltpu.PrefetchScalarGridSpec(num_scalar_prefetch=n_pre, grid=grid, in_specs=ins, out_specs=outs,
                                                scratch_shapes=scratch)
            return pl.pallas_call(kernel, name=name, grid_spec=spec, out_shape=shapes, compiler_params=_cparams())
        return pl.pallas_call(kernel, name=name, grid=grid, in_specs=ins, out_specs=outs, out_shape=shapes,
                              scratch_shapes=scratch, compiler_params=_cparams())

    if ex is None:
        return call(body, in_specs, out_specs, out_shape, scratch_shapes)(*prefetch, *operands)
    n_in, n_out, n_scr = len(in_specs), len(out_specs), len(scratch_shapes)
    k_in, k_out = len(ex.operands), len(ex.out_shapes)

    def carried(*refs):
        pre, refs = refs[:n_pre], refs[n_pre:]
        a, b = n_in, n_in + k_in
        c, e = b + n_out, b + n_out + k_out
        f = e + n_scr
        start, pass_on, finish = ex.build(refs[a:b], refs[c:e], refs[f:])
        step = functools.reduce(lambda lin, ax: lin * grid[ax] + pl.program_id(ax), range(len(grid)), 0)
        pl.when(step == 0)(start)
        body(*pre, *refs[:a], *refs[b:c], *refs[e:f])
        pl.when(step == pass_step)(pass_on)
        pl.when(step == total - 1)(finish)

    return call(carried, list(in_specs) + [ANY] * k_in, list(out_specs) + [ANY] * k_out,
                list(out_shape) + list(ex.out_shapes), list(scratch_shapes) + list(ex.sem_shapes),
                )(*prefetch, *operands, *ex.operands)


def _run_exchange(ex, name):
    k_in, k_out = len(ex.operands), len(ex.out_shapes)

    def body(*refs):
        start, pass_on, finish = ex.build(refs[:k_in], refs[k_in:k_in + k_out], refs[k_in + k_out:])
        start()
        pass_on()
        finish()

    return pl.pallas_call(body, name=name, in_specs=[ANY] * k_in, out_specs=[ANY] * k_out,
                          out_shape=list(ex.out_shapes), scratch_shapes=list(ex.sem_shapes))(*ex.operands)


def _ffn_fwd(x, g, wgu4, wd2, name, ex=None, next_gain=None, target=None):
    t, d = x.shape
    tm = _row_tile(t, 512)
    assert next_gain is None or target is None
    extra = [a for a in (next_gain, target) if a is not None]

    def body(*refs):
        x_ref, g_ref, wg_ref, wu_ref, wd_ref = refs[:5]
        e_ref = refs[5] if extra else None
        outs, (xn_scr, acc_scr) = refs[5 + len(extra):-2], refs[-2:]
        if target is not None:
            dy_ref, loss_ref, gg_ref, uu_ref = outs
        elif next_gain is not None:
            o_ref, gg_ref, uu_ref, h_ref = outs
        else:
            o_ref, gg_ref, uu_ref = outs
        i, j = pl.program_id(0), pl.program_id(1)

        @pl.when(j == 0)
        def _():
            xv = x_ref[...]
            xn_scr[...] = (xv * _rstd(xv, d) * g_ref[...]).astype(BF16)
            acc_scr[...] = jnp.zeros_like(acc_scr)

        if target is not None:
            @pl.when((i == 0) & (j == 0))
            def _():
                loss_ref[...] = jnp.zeros_like(loss_ref)

        xn = xn_scr[...]
        gg = _dot(xn, wg_ref[0])
        uu = _dot(xn, wu_ref[0])
        gg_ref[...] = gg.astype(BF16)
        uu_ref[...] = uu.astype(BF16)
        act = gg * jax.nn.sigmoid(gg) * uu
        acc_scr[...] += _dot(act.astype(BF16), wd_ref[0])

        @pl.when(j == 1)
        def _():
            y = x_ref[...] + 0.5 * acc_scr[...]
            if target is not None:
                e = y - e_ref[...]
                dy_ref[...] = e * (1.0 / d)
                part = 0.5 * jnp.sum(jnp.sum(e * e, axis=-1, keepdims=True) * (1.0 / d), axis=0, keepdims=True)
                loss_ref[...] += jnp.broadcast_to(part, loss_ref.shape)
            else:
                o_ref[...] = y
                if next_gain is not None:
                    h_ref[...] = (y * _rstd(y, d) * e_ref[...]).astype(BF16)

    row = pl.BlockSpec((tm, d), lambda i, j: (i, 0))
    vec = pl.BlockSpec((1, d), lambda i, j: (0, 0))
    ffb = pl.BlockSpec((tm, FF_TILE), lambda i, j: (i, j))
    f32_rows, bf16_ff = jax.ShapeDtypeStruct((t, d), F32), jax.ShapeDtypeStruct((t, D_FF), BF16)
    if target is not None:
        extra_spec, out_specs = [row], [row, pl.BlockSpec((1, LANES), lambda i, j: (0, 0)), ffb, ffb]
        out_shape = [f32_rows, jax.ShapeDtypeStruct((1, LANES), F32), bf16_ff, bf16_ff]
    elif next_gain is not None:
        extra_spec, out_specs = [vec], [row, ffb, ffb, row]
        out_shape = [f32_rows, bf16_ff, bf16_ff, jax.ShapeDtypeStruct((t, d), BF16)]
    else:
        extra_spec, out_specs, out_shape = [], [row, ffb, ffb], [f32_rows, bf16_ff, bf16_ff]
    return _call_with_exchange(
        ex, body, name, (t // tm, 2),
        [row, vec,
         pl.BlockSpec((1, d, FF_TILE), lambda i, j: (j, 0, 0)),
         pl.BlockSpec((1, d, FF_TILE), lambda i, j: (j + 2, 0, 0)),
         pl.BlockSpec((1, FF_TILE, d), lambda i, j: (j, 0, 0))] + extra_spec,
        out_specs, out_shape,
        [pltpu.VMEM((tm, d), BF16), pltpu.VMEM((tm, d), F32)], (x, g, wgu4, wgu4, wd2, *extra))


def _ffn_bwd(x, g, dy, gpre, upre, wgu4, wd2, name, ex=None):
    t, d = x.shape
    tm = _row_tile(t, 512)

    def body(dy_ref, gg_ref, uu_ref, wgu_hbm, wd_hbm, dg_ref, du_ref, act_ref, part_ref, wg_ref, wu_ref, wd_ref):
        j = pl.program_id(0)

        @pl.when(pl.program_id(1) == 0)
        def _():
            pltpu.sync_copy(wgu_hbm.at[j], wg_ref.at[0])
            pltpu.sync_copy(wgu_hbm.at[j + 2], wu_ref.at[0])
            pltpu.sync_copy(wd_hbm.at[j], wd_ref.at[0])

        gg = gg_ref[...].astype(F32)
        uu = uu_ref[...].astype(F32)
        sg = jax.nn.sigmoid(gg)
        silu = gg * sg
        act_ref[...] = (silu * uu).astype(BF16)
        dyh = (0.5 * dy_ref[...]).astype(BF16)
        dact = _dot_nt(dyh, wd_ref[0])
        du = (dact * silu).astype(BF16)
        dgt = (dact * uu * (sg * (1.0 + gg * (1.0 - sg)))).astype(BF16)
        du_ref[...] = du
        dg_ref[...] = dgt
        part_ref[0] = _dot_nt(dgt, wg_ref[0]) + _dot_nt(du, wu_ref[0])

    row = pl.BlockSpec((tm, d), lambda j, i: (i, 0))
    ffb = pl.BlockSpec((tm, FF_TILE), lambda j, i: (i, j))
    dgt, dup, act, parts, *got = _call_with_exchange(
        ex, body, name, (2, t // tm),
        [row, ffb, ffb, ANY, ANY],
        [ffb, ffb, ffb, pl.BlockSpec((1, tm, d), lambda j, i: (j, i, 0))],
        [jax.ShapeDtypeStruct((t, D_FF), BF16)] * 3 + [jax.ShapeDtypeStruct((2, t, d), F32)],
        [pltpu.VMEM((1, d, FF_TILE), BF16), pltpu.VMEM((1, d, FF_TILE), BF16), pltpu.VMEM((1, FF_TILE, d), BF16)],
        (dy, gpre, upre, wgu4, wd2))

    def norm_body(x_ref, g_ref, p_ref, dy_ref, dx_ref, dgain_ref, xn_ref):
        @pl.when(pl.program_id(0) == 0)
        def _():
            dgain_ref[...] = jnp.zeros_like(dgain_ref)

        xv = x_ref[...]
        r = _rstd(xv, d)
        xn_ref[...] = (xv * r * g_ref[...]).astype(BF16)
        dx, dgr = _rms_vjp(xv, r, g_ref[...], p_ref[0] + p_ref[1], d)
        dx_ref[...] = dy_ref[...] + dx
        dgain_ref[...] += jnp.sum(dgr, axis=0, keepdims=True)

    tn = _row_tile(t, 256)
    nrow = pl.BlockSpec((tn, d), lambda i: (i, 0))
    vec = pl.BlockSpec((1, d), lambda i: (0, 0))
    dx, dgain, xn = pl.pallas_call(
        norm_body, name=name + "_norm", grid=(t // tn,),
        in_specs=[nrow, vec, pl.BlockSpec((2, tn, d), lambda i: (0, i, 0)), nrow],
        out_specs=[nrow, vec, nrow],
        out_shape=[jax.ShapeDtypeStruct((t, d), F32), jax.ShapeDtypeStruct((1, d), F32),
                   jax.ShapeDtypeStruct((t, d), BF16)],
        compiler_params=_cparams())(x, g, parts, dy)
    return [dx, dgain, xn, dgt, dup, act] + got


def _sgu_layernorm(vpre, lg, lb):
    v = _gelu(vpre)
    mu = jnp.mean(v, axis=-1, keepdims=True)
    xc = v - mu
    rstd = lax.rsqrt(jnp.mean(xc * xc, axis=-1, keepdims=True) + EPS)
    xhat = xc * rstd
    return xhat, rstd, xhat * lg + lb


def _sgu_fwd(zuv, lg, lb, wt, bias_l, name):
    t = zuv.shape[0]
    tm = _row_tile(t, 512)

    def body(u_ref, v_ref, lg_ref, lb_ref, wt_ref, bl_ref, o_ref, vln_scr):
        _, _, vln = _sgu_layernorm(v_ref[...], lg_ref[...], lb_ref[...])
        vln_scr[...] = vln.astype(BF16)
        lo = lax.broadcasted_iota(jnp.int32, (CHUNK, LANES), 1) < 64
        for c in range(tm // CHUNK):
            rows = slice(c * CHUNK, (c + 1) * CHUNK)
            for p in range(SG_GROUPS // 2):
                cols = slice(p * LANES, (p + 1) * LANES)
                vp = vln_scr[rows, cols]
                mixed = jnp.where(lo, _dot(wt_ref[2 * p], vp), _dot(wt_ref[2 * p + 1], vp)) + bl_ref[:, cols]
                o_ref[rows, cols] = (_gelu(u_ref[rows, cols]) * mixed).astype(BF16)

    half = lambda k: pl.BlockSpec((tm, SG_WIDTH), lambda i: (i, k))
    vec = pl.BlockSpec((1, SG_WIDTH), lambda i: (0, 0))
    return pl.pallas_call(
        body, name=name, grid=(t // tm,),
        in_specs=[half(0), half(1), vec, vec,
                  pl.BlockSpec((SG_GROUPS, CHUNK, CHUNK), lambda i: (0, 0, 0)),
                  pl.BlockSpec((CHUNK, SG_WIDTH), lambda i: (0, 0))],
        out_specs=pl.BlockSpec((tm, SG_WIDTH), lambda i: (i, 0)),
        out_shape=jax.ShapeDtypeStruct((t, SG_WIDTH), BF16),
        scratch_shapes=[pltpu.VMEM((tm, SG_WIDTH), BF16)],
        compiler_params=_cparams())(zuv, zuv, lg, lb, wt, bias_l)


def _sgu_bwd(zuv, dya, lg, lb, wt, wt_t, bias_l, name):
    t = zuv.shape[0]
    tm = _row_tile(t, 256)
    nsteps = t // tm

    def body(u_ref, v_ref, dy_ref, lg_ref, lb_ref, wt_ref, wtt_ref, bl_ref,
             dz_ref, dwt_ref, dbl_ref, dlg_ref, dlb_ref, vln_scr, dvln_scr, dbacc_scr):
        step = pl.program_id(0)

        @pl.when(step == 0)
        def _():
            dwt_ref[...] = jnp.zeros_like(dwt_ref)
            dlg_ref[...] = jnp.zeros_like(dlg_ref)
            dlb_ref[...] = jnp.zeros_like(dlb_ref)
            dbl_ref[...] = jnp.zeros_like(dbl_ref)
            dbacc_scr[...] = jnp.zeros_like(dbacc_scr)

        vpre = v_ref[...]
        lgv = lg_ref[...]
        xhat, rstd, vln = _sgu_layernorm(vpre, lgv, lb_ref[...])
        vln_scr[...] = vln.astype(BF16)
        lo = lax.broadcasted_iota(jnp.int32, (CHUNK, LANES), 1) < 64
        for c in range(tm // CHUNK):
            rows = slice(c * CHUNK, (c + 1) * CHUNK)
            for p in range(SG_GROUPS // 2):
                cols = slice(p * LANES, (p + 1) * LANES)
                vp = vln_scr[rows, cols]
                mixed = jnp.where(lo, _dot(wt_ref[2 * p], vp), _dot(wt_ref[2 * p + 1], vp)) + bl_ref[:, cols]
                upre = u_ref[rows, cols]
                dyp = dy_ref[rows, cols]
                dz_ref[rows, cols] = (dyp * mixed * _gelu_grad(upre)).astype(BF16)
                dm = dyp * _gelu(upre)
                dbacc_scr[:, cols] += dm
                dlo = jnp.where(lo, dm, 0.0).astype(BF16)
                dhi = jnp.where(lo, 0.0, dm).astype(BF16)
                dvln_scr[rows, cols] = _dot(wtt_ref[2 * p], dlo) + _dot(wtt_ref[2 * p + 1], dhi)
                dwt_ref[2 * p] += _dot_nt(dlo, vp)
                dwt_ref[2 * p + 1] += _dot_nt(dhi, vp)
        dvln = dvln_scr[...]
        dlg_ref[...] += jnp.sum(dvln * xhat, axis=0, keepdims=True)
        dlb_ref[...] += jnp.sum(dvln, axis=0, keepdims=True)
        dxh = dvln * lgv
        dv = rstd * (dxh - jnp.mean(dxh, axis=-1, keepdims=True)
                     - xhat * jnp.mean(dxh * xhat, axis=-1, keepdims=True))
        dz_ref[:, SG_WIDTH:] = (dv * _gelu_grad(vpre)).astype(BF16)

        @pl.when(step == nsteps - 1)
        def _():
            rr = lax.broadcasted_iota(jnp.int32, (CHUNK, CHUNK), 0)
            cc = lax.broadcasted_iota(jnp.int32, (CHUNK, CHUNK), 1)
            tril = (cc <= rr).astype(F32)
            for gidx in range(SG_GROUPS):
                dwt_ref[gidx] = dwt_ref[gidx] * tril
            kk = lax.broadcasted_iota(jnp.int32, (SG_WIDTH, LANES), 0)
            gg = lax.broadcasted_iota(jnp.int32, (SG_WIDTH, LANES), 1)
            sel = ((kk // 64) == gg).astype(F32)
            dbl_ref[...] = jnp.dot(dbacc_scr[...], sel, preferred_element_type=F32,
                                   precision=lax.Precision.HIGHEST)

    half = lambda k: pl.BlockSpec((tm, SG_WIDTH), lambda i: (i, k))
    vec = pl.BlockSpec((1, SG_WIDTH), lambda i: (0, 0))
    wspec = pl.BlockSpec((SG_GROUPS, CHUNK, CHUNK), lambda i: (0, 0, 0))
    return pl.pallas_call(
        body, name=name, grid=(nsteps,),
        in_specs=[half(0), half(1), pl.BlockSpec((tm, SG_WIDTH), lambda i: (i, 0)), vec, vec,
                  wspec, wspec, pl.BlockSpec((CHUNK, SG_WIDTH), lambda i: (0, 0))],
        out_specs=[pl.BlockSpec((tm, 2 * SG_WIDTH), lambda i: (i, 0)), wspec,
                   pl.BlockSpec((CHUNK, LANES), lambda i: (0, 0)), vec, vec],
        out_shape=[jax.ShapeDtypeStruct((t, 2 * SG_WIDTH), BF16),
                   jax.ShapeDtypeStruct((SG_GROUPS, CHUNK, CHUNK), F32),
                   jax.ShapeDtypeStruct((CHUNK, LANES), F32),
                   jax.ShapeDtypeStruct((1, SG_WIDTH), F32), jax.ShapeDtypeStruct((1, SG_WIDTH), F32)],
        scratch_shapes=[pltpu.VMEM((tm, SG_WIDTH), BF16), pltpu.VMEM((tm, SG_WIDTH), F32),
                        pltpu.VMEM((CHUNK, SG_WIDTH), F32)],
        compiler_params=_cparams())(zuv, zuv, dya, lg, lb, wt, wt_t, bias_l)


def _rope(x, c, s1, s2):
    return x * c + pltpu.roll(x, LANES - 16, 1) * s1 + pltpu.roll(x, 16, 1) * s2


def _rope_t(dy, c, s1, s2):
    return dy * c + pltpu.roll(dy * s1, 16, 1) + pltpu.roll(dy * s2, LANES - 16, 1)


def _mla_prep_fwd(zcq, zckv, zkr, gcq, gckv, qg, kg, wuq, wuk, wuv, rc, rs1, rs2, name, ex=None):
    t = zcq.shape[0]
    tm = _row_tile(t, 256)
    hd = MLA_HEADS * LANES

    def body(zcq_ref, zckv_ref, zkr_ref, gcq_ref, gckv_ref, qg_ref, kg_ref, wuq_ref, wuk_ref, wuv_ref,
             c_ref, s1_ref, s2_ref, q_ref, k_ref, v_ref, cqn_ref, ckvn_ref):
        c, s1, s2 = c_ref[...], s1_ref[...], s2_ref[...]
        xq = zcq_ref[...]
        cqn = (xq * _rstd(xq, MLA_Q_RANK) * gcq_ref[...]).astype(BF16)
        cqn_ref[...] = cqn
        ql = _dot(cqn, wuq_ref[...])
        xk = zckv_ref[...]
        ckvn = (xk * _rstd(xk, MLA_KV_RANK) * gckv_ref[...]).astype(BF16)
        ckvn_ref[...] = ckvn
        kl = _dot(ckvn, wuk_ref[...])
        slot_lane = lax.broadcasted_iota(jnp.int32, (tm, hd), 1) % LANES
        v_ref[...] = jnp.where(slot_lane == V_ONES_LANE, 1.0, _dot(ckvn, wuv_ref[...])).astype(BF16)
        kr = zkr_ref[...]
        for h in range(MLA_HEADS):
            sl = slice(h * LANES, (h + 1) * LANES)
            qh = ql[:, sl]
            q_ref[:, sl] = (_rope(qh * _rstd(qh, MLA_QK) * qg_ref[...], c, s1, s2) * ATTN_SCALE2).astype(BF16)
            kh = kl[:, sl] + kr
            k_ref[:, sl] = _rope(kh * _rstd(kh, MLA_QK) * kg_ref[...], c, s1, s2).astype(BF16)

    row = lambda n: pl.BlockSpec((tm, n), lambda i: (i, 0))
    full = lambda a: pl.BlockSpec(a.shape, lambda i: (0, 0))
    return _call_with_exchange(
        ex, body, name, (t // tm,),
        [row(MLA_Q_RANK), row(MLA_KV_RANK), row(LANES), full(gcq), full(gckv), full(qg), full(kg),
         full(wuq), full(wuk), full(wuv), row(LANES), row(LANES), row(LANES)],
        [row(hd), row(hd), row(hd), row(MLA_Q_RANK), row(MLA_KV_RANK)],
        [jax.ShapeDtypeStruct((t, hd), BF16)] * 3
        + [jax.ShapeDtypeStruct((t, MLA_Q_RANK), BF16), jax.ShapeDtypeStruct((t, MLA_KV_RANK), BF16)],
        [], (zcq, zckv, zkr, gcq, gckv, qg, kg, wuq, wuk, wuv, rc, rs1, rs2))


def _mla_prep_bwd(zcq, zckv, zkr, gcq, gckv, qg, kg, wuq, wuk, wuv, rc, rs1, rs2, dq, dk, dv, name):
    t = zcq.shape[0]
    tm = _row_tile(t, 256)
    hd = MLA_HEADS * LANES

    def body(zcq_ref, zckv_ref, zkr_ref, gcq_ref, gckv_ref, qg_ref, kg_ref, wuq_ref, wuk_ref, wuv_ref,
             c_ref, s1_ref, s2_ref, dq_ref, dk_ref, dv_ref,
             dzcq_ref, dzckv_ref, dzkr_ref, dql_ref, dkl_ref, dgcq_ref, dgckv_ref, dqg_ref, dkg_ref):
        @pl.when(pl.program_id(0) == 0)
        def _():
            for ref in (dgcq_ref, dgckv_ref, dqg_ref, dkg_ref):
                ref[...] = jnp.zeros_like(ref)

        c, s1, s2 = c_ref[...], s1_ref[...], s2_ref[...]
        qgv, kgv = qg_ref[...], kg_ref[...]
        xq = zcq_ref[...]
        rq = _rstd(xq, MLA_Q_RANK)
        ql = _dot((xq * rq * gcq_ref[...]).astype(BF16), wuq_ref[...])
        xk = zckv_ref[...]
        rk = _rstd(xk, MLA_KV_RANK)
        kl = _dot((xk * rk * gckv_ref[...]).astype(BF16), wuk_ref[...])
        kr = zkr_ref[...]
        dqg_acc = jnp.zeros((tm, LANES), F32)
        dkg_acc = jnp.zeros((tm, LANES), F32)
        dkr = jnp.zeros((tm, LANES), F32)
        for h in range(MLA_HEADS):
            sl = slice(h * LANES, (h + 1) * LANES)
            qh = ql[:, sl]
            dqh, dgr = _rms_vjp(qh, _rstd(qh, MLA_QK), qgv, _rope_t(dq_ref[:, sl], c, s1, s2), MLA_QK)
            dql_ref[:, sl] = dqh.astype(BF16)
            dqg_acc += dgr
            kh = kl[:, sl] + kr
            dkh, dgr = _rms_vjp(kh, _rstd(kh, MLA_QK), kgv, _rope_t(dk_ref[:, sl], c, s1, s2), MLA_QK)
            dkl_ref[:, sl] = dkh.astype(BF16)
            dkg_acc += dgr
            dkr += dkh
        dqg_ref[...] += jnp.sum(dqg_acc, axis=0, keepdims=True)
        dkg_ref[...] += jnp.sum(dkg_acc, axis=0, keepdims=True)
        lane = lax.broadcasted_iota(jnp.int32, (tm, LANES), 1)
        dzkr_ref[...] = jnp.where((lane >= MLA_NOPE) & (lane < MLA_QK), dkr, 0.0).astype(BF16)
        dcqn = _dot_nt(dql_ref[...], wuq_ref[...])
        dx, dgr = _rms_vjp(xq, rq, gcq_ref[...], dcqn, MLA_Q_RANK)
        dzcq_ref[...] = dx.astype(BF16)
        dgcq_ref[...] += jnp.sum(dgr, axis=0, keepdims=True)
        dckvn = _dot_nt(dkl_ref[...], wuk_ref[...]) + _dot_nt(dv_ref[...].astype(BF16), wuv_ref[...])
        dx, dgr = _rms_vjp(xk, rk, gckv_ref[...], dckvn, MLA_KV_RANK)
        dzckv_ref[...] = dx.astype(BF16)
        dgckv_ref[...] += jnp.sum(dgr, axis=0, keepdims=True)

    row = lambda n: pl.BlockSpec((tm, n), lambda i: (i, 0))
    full = lambda a: pl.BlockSpec(a.shape, lambda i: (0, 0))
    vec = lambda n: pl.BlockSpec((1, n), lambda i: (0, 0))
    return pl.pallas_call(
        body, name=name, grid=(t // tm,),
        in_specs=[row(MLA_Q_RANK), row(MLA_KV_RANK), row(LANES), full(gcq), full(gckv), full(qg), full(kg),
                  full(wuq), full(wuk), full(wuv), row(LANES), row(LANES), row(LANES), row(hd), row(hd), row(hd)],
        out_specs=[row(MLA_Q_RANK), row(MLA_KV_RANK), row(LANES), row(hd), row(hd),
                   vec(MLA_Q_RANK), vec(MLA_KV_RANK), vec(LANES), vec(LANES)],
        out_shape=[jax.ShapeDtypeStruct((t, MLA_Q_RANK), BF16), jax.ShapeDtypeStruct((t, MLA_KV_RANK), BF16),
                   jax.ShapeDtypeStruct((t, LANES), BF16), jax.ShapeDtypeStruct((t, hd), BF16),
                   jax.ShapeDtypeStruct((t, hd), BF16), jax.ShapeDtypeStruct((1, MLA_Q_RANK), F32),
                   jax.ShapeDtypeStruct((1, MLA_KV_RANK), F32), jax.ShapeDtypeStruct((1, LANES), F32),
                   jax.ShapeDtypeStruct((1, LANES), F32)],
        compiler_params=_cparams(),
    )(zcq, zckv, zkr, gcq, gckv, qg, kg, wuq, wuk, wuv, rc, rs1, rs2, dq, dk, dv)


def _attn_tiles(t):
    tq = 512 if t >= 2048 else 128
    return tq, min(t, 4 * tq), min(t, 4 * tq)


def _causal_keep(tq, nk, i, j, tk):
    row = lax.broadcasted_iota(jnp.int32, (tq, nk), 0)
    col = lax.broadcasted_iota(jnp.int32, (tq, nk), 1)
    return (col - row) <= (i * tq - j * tk)


def _causal_keep_t(tq, nk, i, j, tk):
    key = lax.broadcasted_iota(jnp.int32, (nk, tq), 0)
    qry = lax.broadcasted_iota(jnp.int32, (nk, tq), 1)
    return (key - qry) <= (i * tq - j * tk)


ATTN_FWD_HEADS_PER_STEP = 2
ATTN_BWD_HEADS_PER_STEP = 2


def _attn_fwd(q, k, v, name, ex=None):
    t, hd = q.shape
    hp = ATTN_FWD_HEADS_PER_STEP
    tq, tk, _ = _attn_tiles(t)
    pairs = [(i, j) for i in range(t // tq) for j in range(((i + 1) * tq - 1) // tk + 1)]
    ii = np.array([p[0] for p in pairs], np.int32)
    jj = np.array([p[1] for p in pairs], np.int32)

    def body(ii_ref, jj_ref, q_ref, k_ref, v_ref, o_ref, lse_ref, m_scr, acc_scr):
        s_id = pl.program_id(1)
        i, j = ii_ref[s_id], jj_ref[s_id]
        last = j == ((i + 1) * tq - 1) // tk
        ones_lane = lax.broadcasted_iota(jnp.int32, (tq, LANES), 1) == V_ONES_LANE

        @pl.when(j == 0)
        def _():
            m_scr[...] = jnp.full_like(m_scr, NEG)
            acc_scr[...] = jnp.zeros_like(acc_scr)

        def step(masked, nk):
            for hh in range(hp):
                sl = slice(hh * LANES, (hh + 1) * LANES)
                s = _dot_nt(q_ref[:, sl], k_ref[:nk, sl])
                if masked:
                    s = jnp.where(_causal_keep(tq, nk, i, j, tk), s, NEG)
                m_prev = m_scr[hh]
                m_new = jnp.maximum(m_prev, jnp.max(s, axis=1, keepdims=True))
                p = jnp.exp2(s - m_new)
                alpha = jnp.exp2(m_prev - m_new)
                acc = alpha * acc_scr[:, sl] + _dot(p.astype(BF16), v_ref[:nk, sl])
                if masked:
                    l_new = jnp.sum(jnp.where(ones_lane, acc, 0.0), axis=1, keepdims=True)
                    o_ref[:, sl] = (acc / l_new).astype(BF16)
                    lse_ref[:, sl] = jnp.broadcast_to(m_new + jnp.log(l_new) * LOG2E, (tq, LANES))
                else:
                    acc_scr[:, sl] = acc
                    m_scr[hh] = m_new

        @pl.when(jnp.logical_not(last))
        def _():
            step(False, tk)

        r = (((i + 1) * tq - 1) % tk) // tq
        for rr in range(tk // tq):
            @pl.when(last & (r == rr))
            def _():
                step(True, (rr + 1) * tq)

    w = hp * LANES
    qspec = pl.BlockSpec((tq, w), lambda h, s, ii_r, jj_r: (ii_r[s], h))
    kspec = pl.BlockSpec((tk, w), lambda h, s, ii_r, jj_r: (jj_r[s], h))
    return _call_with_exchange(
        ex, body, name, (hd // w, len(pairs)), [qspec, kspec, kspec], [qspec, qspec],
        [jax.ShapeDtypeStruct((t, hd), BF16), jax.ShapeDtypeStruct((t, hd), F32)],
        [pltpu.VMEM((hp, tq, 1), F32), pltpu.VMEM((tq, w), F32)], (q, k, v),
        prefetch=(jnp.asarray(ii), jnp.asarray(jj)))


def _attn_bwd_rows(o, lse, do, name):
    t, hd = o.shape
    heads = hd // LANES
    tm = _row_tile(t, 512)

    def body(o_ref, lse_ref, do_ref, out_ref):
        lane = lax.broadcasted_iota(jnp.int32, (tm, LANES), 1)
        acc = jnp.zeros((tm, LANES), F32)
        for h in range(heads):
            sl = slice(h * LANES, (h + 1) * LANES)
            delta = jnp.sum(do_ref[:, sl].astype(F32) * o_ref[:, sl].astype(F32), axis=1, keepdims=True)
            acc = jnp.where(lane == h, delta, acc)
            acc = jnp.where(lane == heads + h, lse_ref[:, sl], acc)
        out_ref[...] = acc

    row = pl.BlockSpec((tm, hd), lambda i: (i, 0))
    cols = pl.pallas_call(
        body, name=name, grid=(t // tm,), in_specs=[row, row, row],
        out_specs=pl.BlockSpec((tm, LANES), lambda i: (i, 0)),
        out_shape=jax.ShapeDtypeStruct((t, LANES), F32), compiler_params=_cparams())(o, lse, do)
    rows = cols.T
    return rows[:heads].reshape(heads, 1, t), rows[heads:2 * heads].reshape(heads, 1, t)


def _attn_bwd(q, k, v, delta_rows, lse_rows, do, name):
    t, hd = q.shape
    hp = ATTN_BWD_HEADS_PER_STEP
    tq, _, tk = _attn_tiles(t)
    nq = t // tq
    pairs = [(i, j) for j in range(t // tk) for i in range((j * tk) // tq, nq)]
    ii = np.array([p[0] for p in pairs], np.int32)
    jj = np.array([p[1] for p in pairs], np.int32)

    def body(jj_ref, ii_ref, q_ref, k_ref, v_ref, delta_ref, lse_ref, do_ref, dq_ref, dk_ref, dv_ref,
             dk_scr, dv_scr):
        s_id = pl.program_id(1)
        i, j = ii_ref[s_id], jj_ref[s_id]

        @pl.when(s_id == 0)
        def _():
            dq_ref[...] = jnp.zeros_like(dq_ref)

        @pl.when(i == (j * tk) // tq)
        def _():
            dk_scr[...] = jnp.zeros_like(dk_scr)
            dv_scr[...] = jnp.zeros_like(dv_scr)

        rows = pl.ds(pl.multiple_of(i * tq, tq), tq)

        def step(masked, nk):
            for hh in range(hp):
                sl = slice(hh * LANES, (hh + 1) * LANES)
                qv, kv, dov = q_ref[:, sl], k_ref[:nk, sl], do_ref[:, sl]
                st = _dot_nt(kv, qv)
                if masked:
                    st = jnp.where(_causal_keep_t(tq, nk, i, j, tk), st, NEG)
                pt = jnp.exp2(st - lse_ref[hh])
                dv_scr[:nk, sl] += _dot(pt.astype(BF16), dov)
                dpt = _dot_nt(v_ref[:nk, sl], dov)
                dst = (pt * (dpt - delta_ref[hh]) * ATTN_SCALE).astype(BF16)
                dk_scr[:nk, sl] += _dot(dst, qv)
                dq_ref[rows, sl] += _dot_tn(dst, kv)

        seen = jnp.minimum((i + 1) * tq - j * tk, tk)
        for nk in range(tq, tk + 1, tq):
            @pl.when((seen == nk) & ((i + 1) * tq - j * tk <= tk))
            def _():
                step(True, nk)

        @pl.when((i + 1) * tq - j * tk > tk)
        def _():
            step(False, tk)

        @pl.when(i == nq - 1)
        def _():
            dk_ref[...] = dk_scr[...] * (1.0 / ATTN_SCALE2)
            dv_ref[...] = dv_scr[...]

    w = hp * LANES
    qspec = pl.BlockSpec((tq, w), lambda h, s, jj_r, ii_r: (ii_r[s], h))
    kspec = pl.BlockSpec((tk, w), lambda h, s, jj_r, ii_r: (jj_r[s], h))
    rspec = pl.BlockSpec((hp, 1, tq), lambda h, s, jj_r, ii_r: (h, 0, ii_r[s]))
    return pl.pallas_call(
        body, name=name,
        grid_spec=pltpu.PrefetchScalarGridSpec(
            num_scalar_prefetch=2, grid=(hd // w, len(pairs)),
            in_specs=[qspec, kspec, kspec, rspec, rspec, qspec],
            out_specs=[pl.BlockSpec((t, w), lambda h, s, jj_r, ii_r: (0, h)), kspec, kspec],
            scratch_shapes=[pltpu.VMEM((tk, w), F32), pltpu.VMEM((tk, w), F32)]),
        out_shape=[jax.ShapeDtypeStruct((t, hd), F32)] * 3,
        compiler_params=_cparams())(jnp.asarray(jj), jnp.asarray(ii), q, k, v, delta_rows, lse_rows, do)


MEM_W = MEM_HEADS * LANES


def _mem_kv_fwd(mem, gmem, wkv, kg, name):
    m, d = mem.shape

    def body(mem_ref, g_ref, w_ref, kg_ref, k_ref, v_ref, mn_ref):
        xv = mem_ref[...]
        mn = (xv * _rstd(xv, d) * g_ref[...]).astype(BF16)
        mn_ref[...] = mn
        kvm = _dot(mn, w_ref[...])
        v_ref[...] = kvm[:, MEM_W:].astype(BF16)
        for h in range(MEM_HEADS):
            sl = slice(h * LANES, (h + 1) * LANES)
            kh = kvm[:, sl]
            k_ref[:, sl] = (kh * _rstd(kh, LANES) * kg_ref[...]).astype(BF16)

    full = lambda a: pl.BlockSpec(a.shape, lambda i: (0, 0))
    return pl.pallas_call(
        body, name=name, grid=(1,), in_specs=[full(mem), full(gmem), full(wkv), full(kg)],
        out_specs=[pl.BlockSpec((m, MEM_W), lambda i: (0, 0)), pl.BlockSpec((m, MEM_W), lambda i: (0, 0)),
                   pl.BlockSpec((m, d), lambda i: (0, 0))],
        out_shape=[jax.ShapeDtypeStruct((m, MEM_W), BF16), jax.ShapeDtypeStruct((m, MEM_W), BF16),
                   jax.ShapeDtypeStruct((m, d), BF16)],
        compiler_params=_cparams())(mem, gmem, wkv, kg)


def _mem_softmax(qn, kh):
    s = _dot_nt(qn, kh) * (LANES ** -0.5)
    e = jnp.exp(s - jnp.max(s, axis=1, keepdims=True))
    return e / jnp.sum(e, axis=1, keepdims=True)


def _mem_attn_fwd(zqm, qg, km, vm, name):
    t = zqm.shape[0]
    tm = _row_tile(t, 512)

    def body(q_ref, qg_ref, k_ref, v_ref, o_ref):
        for h in range(MEM_HEADS):
            sl = slice(h * LANES, (h + 1) * LANES)
            qh = q_ref[:, sl]
            qn = (qh * _rstd(qh, LANES) * qg_ref[...]).astype(BF16)
            p = _mem_softmax(qn, k_ref[:, sl])
            o_ref[:, sl] = _dot(p.astype(BF16), v_ref[:, sl]).astype(BF16)

    row = pl.BlockSpec((tm, MEM_W), lambda i: (i, 0))
    full = lambda a: pl.BlockSpec(a.shape, lambda i: (0, 0))
    return pl.pallas_call(
        body, name=name, grid=(t // tm,), in_specs=[row, full(qg), full(km), full(vm)], out_specs=row,
        out_shape=jax.ShapeDtypeStruct((t, MEM_W), BF16), compiler_params=_cparams())(zqm, qg, km, vm)


def _mem_attn_bwd(zqm, dyc, qg, km, vm, name):
    t = zqm.shape[0]
    m = km.shape[0]
    tm = _row_tile(t, 256)

    def body(q_ref, dy_ref, qg_ref, k_ref, v_ref, dz_ref, dk_ref, dv_ref, dqg_ref):
        @pl.when(pl.program_id(0) == 0)
        def _():
            dk_ref[...] = jnp.zeros_like(dk_ref)
            dv_ref[...] = jnp.zeros_like(dv_ref)
            dqg_ref[...] = jnp.zeros_like(dqg_ref)

        qgv = qg_ref[...]
        dqg_acc = jnp.zeros((tm, LANES), F32)
        for h in range(MEM_HEADS):
            sl = slice(h * LANES, (h + 1) * LANES)
            qh = q_ref[:, sl]
            r = _rstd(qh, LANES)
            qn = (qh * r * qgv).astype(BF16)
            kh = k_ref[:, sl]
            p = _mem_softmax(qn, kh)
            dov = dy_ref[:, sl]
            dv_ref[:, sl] += _dot_tn(p.astype(BF16), dov)
            dp = _dot_nt(dov, v_ref[:, sl])
            ds = (p * (dp - jnp.sum(dp * p, axis=1, keepdims=True)) * (LANES ** -0.5)).astype(BF16)
            dk_ref[:, sl] += _dot_tn(ds, qn)
            dqh, dgr = _rms_vjp(qh, r, qgv, _dot(ds, kh), LANES)
            dz_ref[:, sl] = dqh.astype(BF16)
            dqg_acc += dgr
        dqg_ref[...] += jnp.sum(dqg_acc, axis=0, keepdims=True)

    row = pl.BlockSpec((tm, MEM_W), lambda i: (i, 0))
    full = lambda a: pl.BlockSpec(a.shape, lambda i: (0, 0))
    acc = pl.BlockSpec((m, MEM_W), lambda i: (0, 0))
    return pl.pallas_call(
        body, name=name, grid=(t // tm,), in_specs=[row, row, full(qg), full(km), full(vm)],
        out_specs=[row, acc, acc, pl.BlockSpec((1, LANES), lambda i: (0, 0))],
        out_shape=[jax.ShapeDtypeStruct((t, MEM_W), BF16), jax.ShapeDtypeStruct((m, MEM_W), F32),
                   jax.ShapeDtypeStruct((m, MEM_W), F32), jax.ShapeDtypeStruct((1, LANES), F32)],
        compiler_params=_cparams())(zqm, dyc, qg, km, vm)


def _mem_kv_bwd(mem, gmem, wkv, kg, dkn, dvm, name):
    m, d = mem.shape

    def body(mem_ref, g_ref, w_ref, kg_ref, dk_ref, dv_ref, dw_ref, dkg_ref, dg_ref, dkv_scr):
        xv = mem_ref[...]
        r = _rstd(xv, d)
        mn = (xv * r * g_ref[...]).astype(BF16)
        kvm = _dot(mn, w_ref[...])
        dkv_scr[:, MEM_W:] = dv_ref[...].astype(BF16)
        dkg_acc = jnp.zeros((m, LANES), F32)
        for h in range(MEM_HEADS):
            sl = slice(h * LANES, (h + 1) * LANES)
            kh = kvm[:, sl]
            dkh, dgr = _rms_vjp(kh, _rstd(kh, LANES), kg_ref[...], dk_ref[:, sl], LANES)
            dkv_scr[:, sl] = dkh.astype(BF16)
            dkg_acc += dgr
        dkg_ref[...] = jnp.sum(dkg_acc, axis=0, keepdims=True)
        dkv = dkv_scr[...]
        dw_ref[...] = _dot_tn(mn, dkv)
        dmn = _dot_nt(dkv, w_ref[...])
        dg_ref[...] = jnp.sum(dmn * xv * r, axis=0, keepdims=True)

    full = lambda a: pl.BlockSpec(a.shape, lambda i: (0, 0))
    return pl.pallas_call(
        body, name=name, grid=(1,),
        in_specs=[full(mem), full(gmem), full(wkv), full(kg), full(dkn), full(dvm)],
        out_specs=[pl.BlockSpec((d, 2 * MEM_W), lambda i: (0, 0)), pl.BlockSpec((1, LANES), lambda i: (0, 0)),
                   pl.BlockSpec((1, d), lambda i: (0, 0))],
        out_shape=[jax.ShapeDtypeStruct((d, 2 * MEM_W), F32), jax.ShapeDtypeStruct((1, LANES), F32),
                   jax.ShapeDtypeStruct((1, d), F32)],
        scratch_shapes=[pltpu.VMEM((m, 2 * MEM_W), BF16)],
        compiler_params=_cparams())(mem, gmem, wkv, kg, dkn, dvm)


def _merge_fwd(x1, ya, yb, yc, zg, bg, wa, wb, wc, wo, name):
    t, d = x1.shape
    tm = _row_tile(t, 256)

    def body(x_ref, ya_ref, yb_ref, yc_ref, zg_ref, bg_ref, wa_ref, wb_ref, wc_ref, wo_ref,
             x2_ref, mg_ref, pa_ref, pb_ref, pc_ref):
        merged = None
        for k, (y_ref, w_ref, p_ref) in enumerate(
                ((ya_ref, wa_ref, pa_ref), (yb_ref, wb_ref, pb_ref), (yc_ref, wc_ref, pc_ref))):
            sl = slice(k * d, (k + 1) * d)
            pr = _dot(y_ref[...], w_ref[...])
            p_ref[...] = pr.astype(BF16)
            term = jax.nn.sigmoid(zg_ref[:, sl] + bg_ref[:, sl]) * pr
            merged = term if merged is None else merged + term
        mb = merged.astype(BF16)
        mg_ref[...] = mb
        x2_ref[...] = x_ref[...] + _dot(mb, wo_ref[...])

    row = lambda n: pl.BlockSpec((tm, n), lambda i: (i, 0))
    full = lambda a: pl.BlockSpec(a.shape, lambda i: (0, 0))
    return pl.pallas_call(
        body, name=name, grid=(t // tm,),
        in_specs=[row(d), row(ya.shape[1]), row(yb.shape[1]), row(yc.shape[1]), row(3 * d), full(bg),
                  full(wa), full(wb), full(wc), full(wo)],
        out_specs=[row(d)] * 5,
        out_shape=[jax.ShapeDtypeStruct((t, d), F32)] + [jax.ShapeDtypeStruct((t, d), BF16)] * 4,
        compiler_params=_cparams())(x1, ya, yb, yc, zg, bg, wa, wb, wc, wo)


def _merge_bwd(dx2, pa, pb, pc, zg, bg, wa, wb, wc, wo, name, ex=None):
    t, d = dx2.shape
    tm = _row_tile(t, 256)

    def body(dx_ref, pa_ref, pb_ref, pc_ref, zg_ref, bg_ref, wa_ref, wb_ref, wc_ref, wo_ref,
             dpa_ref, dpb_ref, dpc_ref, dzg_ref, dbg_ref, dya_ref, dyb_ref, dyc_ref):
        @pl.when(pl.program_id(0) == 0)
        def _():
            dbg_ref[...] = jnp.zeros_like(dbg_ref)

        dm = _dot_nt(dx_ref[...].astype(BF16), wo_ref[...])
        for k, (p_ref, w_ref, dp_ref, dy_ref) in enumerate(
                ((pa_ref, wa_ref, dpa_ref, dya_ref), (pb_ref, wb_ref, dpb_ref, dyb_ref),
                 (pc_ref, wc_ref, dpc_ref, dyc_ref))):
            sl = slice(k * d, (k + 1) * d)
            gate = jax.nn.sigmoid(zg_ref[:, sl] + bg_ref[:, sl])
            dpr = (dm * gate).astype(BF16)
            dp_ref[...] = dpr
            dzg = dm * p_ref[...].astype(F32) * gate * (1.0 - gate)
            dzg_ref[:, sl] = dzg.astype(BF16)
            dbg_ref[:, sl] += jnp.sum(dzg, axis=0, keepdims=True)
            dy_ref[...] = _dot_nt(dpr, w_ref[...]).astype(dy_ref.dtype)

    row = lambda n: pl.BlockSpec((tm, n), lambda i: (i, 0))
    full = lambda a: pl.BlockSpec(a.shape, lambda i: (0, 0))
    na, nb, nc = wa.shape[0], wb.shape[0], wc.shape[0]
    return _call_with_exchange(
        ex, body, name, (t // tm,),
        [row(d), row(d), row(d), row(d), row(3 * d), full(bg), full(wa), full(wb), full(wc), full(wo)],
        [row(d), row(d), row(d), row(3 * d), pl.BlockSpec((1, 3 * d), lambda i: (0, 0)), row(na), row(nb), row(nc)],
        [jax.ShapeDtypeStruct((t, d), BF16)] * 3
        + [jax.ShapeDtypeStruct((t, 3 * d), BF16), jax.ShapeDtypeStruct((1, 3 * d), F32),
           jax.ShapeDtypeStruct((t, na), F32), jax.ShapeDtypeStruct((t, nb), BF16),
           jax.ShapeDtypeStruct((t, nc), BF16)],
        [], (dx2, pa, pb, pc, zg, bg, wa, wb, wc, wo))


def _adamw_math(w, g, m, v):
    bc1 = 1.0 - ADAM_B1 ** ADAM_STEP
    bc2 = 1.0 - ADAM_B2 ** ADAM_STEP
    nm = ADAM_B1 * m + (1.0 - ADAM_B1) * g
    nv = ADAM_B2 * v + (1.0 - ADAM_B2) * (g * g)
    delta = -ADAM_LR * ((nm / bc1) / (jnp.sqrt(nv / bc2) + ADAM_EPS) + ADAM_WD * w)
    return delta, nm, nv


def _div_tile(n, cap, mult):
    best = None
    for cand in range(mult, min(n, cap) + 1, mult):
        if n % cand == 0:
            best = cand
    assert best is not None, (n, cap, mult)
    return best


def _adamw(w, g, m, v, name):
    rows, cols = w.shape
    tr = rows if rows * cols <= 256 * 1024 else _div_tile(rows, 256, 8)

    def body(w_ref, g_ref, m_ref, v_ref, d_ref, nm_ref, nv_ref):
        d_ref[...], nm_ref[...], nv_ref[...] = _adamw_math(w_ref[...], g_ref[...], m_ref[...], v_ref[...])

    blk = pl.BlockSpec((tr, cols), lambda i: (i, 0))
    return pl.pallas_call(
        body, name=name, grid=(rows // tr,), in_specs=[blk] * 4, out_specs=[blk] * 3,
        out_shape=[jax.ShapeDtypeStruct((rows, cols), F32)] * 3, compiler_params=_cparams())(w, g, m, v)


def _adamw_slots(w, slots, m, v, name):
    _, hr, cols = w.shape
    tr = _div_tile(hr, 128, 16)

    def body(w_ref, s_ref, m_ref, v_ref, g_ref, d_ref, nm_ref, nv_ref):
        g = s_ref[0, 0].astype(F32)
        for k in range(1, N_CHIPS):
            g = g + s_ref[0, k].astype(F32)
        g_ref[0] = g
        d_ref[0], nm_ref[0], nv_ref[0] = _adamw_math(w_ref[0], g, m_ref[0], v_ref[0])

    blk = pl.BlockSpec((1, tr, cols), lambda h, i: (h, i, 0))
    return pl.pallas_call(
        body, name=name, grid=(2, hr // tr),
        in_specs=[blk, pl.BlockSpec((1, N_CHIPS, tr, cols), lambda h, i: (h, 0, i, 0)), blk, blk],
        out_specs=[blk] * 4, out_shape=[jax.ShapeDtypeStruct((2, hr, cols), F32)] * 4,
        compiler_params=_cparams())(w, slots, m, v)


ANY = pl.BlockSpec(memory_space=pl.ANY)


def _place():
    x, y, c = lax.axis_index("x"), lax.axis_index("y"), lax.axis_index("c")
    other_chips = [(1 - x, y), (x, 1 - y), (1 - x, 1 - y)]
    return x, y, c, other_chips


def _remote(src, dst, send_sem, recv_sem, to):
    return pltpu.make_async_remote_copy(src_ref=src, dst_ref=dst, send_sem=send_sem, recv_sem=recv_sem,
                                        device_id=to, device_id_type=MESH)


PIECE_BYTES = 384 * 1024


def _row_pieces(half_rows, cols):
    for n in (4, 2):
        if half_rows % (16 * n) == 0 and half_rows * cols * 2 // n >= PIECE_BYTES:
            return [pl.ds(k * (half_rows // n), half_rows // n) for k in range(n)]
    return [pl.ds(0, half_rows)]


def _pieces(arrays, rows_axis):
    return [(w, rows) for w, a in enumerate(arrays) for rows in _row_pieces(a.shape[rows_axis], a.shape[-1])]


def _gather_exchange(shards):
    nw = len(shards)
    pieces = _pieces(shards, 1)
    npc = len(pieces)

    def build(s_refs, g_refs, sems):
        send_sems, recv_sems, local_sems = sems
        x, y, c, chips = _place()
        me = 2 * x + y
        sibling = (x, y, 1 - c)
        mine = [pltpu.make_async_copy(s_refs[w], g_refs[w].at[me], local_sems.at[w]) for w in range(nw)]
        first = [_remote(s_refs[w].at[c, rows], g_refs[w].at[me, c, rows], send_sems.at[k, p], recv_sems.at[k, p],
                         (cx, cy, c)) for k, (cx, cy) in enumerate(chips) for p, (w, rows) in enumerate(pieces)]

        def start():
            for cp in mine + first:
                cp.start()

        arrived = [g_refs[w].at[2 * cx + cy, c, rows] for cx, cy in chips for w, rows in pieces]
        passed = [_remote(slab, slab, send_sems.at[3 + q // npc, q % npc], recv_sems.at[3 + q // npc, q % npc], sibling)
                  for q, slab in enumerate(arrived)]

        def pass_on():
            for q, slab in enumerate(arrived):
                k, p = q // npc, q % npc
                _remote(slab, slab, send_sems.at[k, p], recv_sems.at[k, p], (*chips[k], c)).wait_recv()
                passed[q].start()

        def finish():
            for k, (cx, cy) in enumerate(chips):
                for p, (w, rows) in enumerate(pieces):
                    slab = g_refs[w].at[2 * cx + cy, 1 - c, rows]
                    _remote(slab, slab, send_sems.at[3 + k, p], recv_sems.at[3 + k, p], sibling).wait_recv()
            for cp in first + passed:
                cp.wait_send()
            for cp in mine:
                cp.wait()

        return start, pass_on, finish

    return _Exchange(list(shards), [jax.ShapeDtypeStruct((N_CHIPS,) + s.shape, BF16) for s in shards],
                     [pltpu.SemaphoreType.DMA((6, npc)), pltpu.SemaphoreType.DMA((6, npc)),
                      pltpu.SemaphoreType.DMA((nw,))], build)


def _swap_halves(grads, name):
    nw = len(grads)

    def body(*refs):
        g_refs, sib_refs = refs[:nw], refs[nw:2 * nw]
        send_sems, recv_sems = refs[2 * nw:]
        x, y, c, _ = _place()
        copies = [_remote(g_refs[w].at[s, 1 - c], sib_refs[w].at[s], send_sems.at[s, w], recv_sems.at[s, w],
                          (x, y, 1 - c)) for w in range(nw) for s in range(N_CHIPS)]
        for cp in copies:
            cp.start()
        for cp in copies:
            cp.wait_recv()
        for cp in copies:
            cp.wait_send()

    return pl.pallas_call(
        body, name=name, in_specs=[ANY] * nw, out_specs=[ANY] * nw,
        out_shape=[jax.ShapeDtypeStruct((N_CHIPS,) + g.shape[2:], BF16) for g in grads],
        scratch_shapes=[pltpu.SemaphoreType.DMA((N_CHIPS, nw)), pltpu.SemaphoreType.DMA((N_CHIPS, nw))],
    )(*grads)


def _pair_sum(grad, sib, core, name):
    nchip, _, hr, cols = grad.shape
    tr = _div_tile(hr, 256, 16)

    def body(core_ref, a_ref, b_ref, o_ref):
        o_ref[...] = (a_ref[0].astype(F32) + b_ref[...].astype(F32)).astype(BF16)

    return pl.pallas_call(
        body, name=name,
        grid_spec=pltpu.PrefetchScalarGridSpec(
            num_scalar_prefetch=1, grid=(nchip, hr // tr),
            in_specs=[pl.BlockSpec((1, 1, tr, cols), lambda s, i, core_r: (s, core_r[0], i, 0)),
                      pl.BlockSpec((1, tr, cols), lambda s, i, core_r: (s, i, 0))],
            out_specs=pl.BlockSpec((1, tr, cols), lambda s, i, core_r: (s, i, 0))),
        out_shape=jax.ShapeDtypeStruct((nchip, hr, cols), BF16), compiler_params=_cparams())(core, grad, sib)


def _pair_sum_exchange(sums):
    nw = len(sums)
    pieces = _pieces(sums, 1)
    npc = len(pieces)

    def build(p_refs, o_refs, sems):
        send_sems, recv_sems, local_sems = sems
        x, y, c, chips = _place()
        me = 2 * x + y
        sibling = (x, y, 1 - c)
        mine = [pltpu.make_async_copy(p_refs[w].at[me], o_refs[w].at[c, 3], local_sems.at[w]) for w in range(nw)]
        first = [_remote(p_refs[w].at[2 * cx + cy, rows], o_refs[w].at[c, k, rows], send_sems.at[k, p],
                         recv_sems.at[k, p], (cx, cy, c))
                 for k, (cx, cy) in enumerate(chips) for p, (w, rows) in enumerate(pieces)]

        def start():
            for cp in mine + first:
                cp.start()

        passed = [_remote(o_refs[w].at[c, k, rows], o_refs[w].at[c, k, rows], send_sems.at[3 + k, p],
                          recv_sems.at[3 + k, p], sibling) for k in range(N_CHIPS) for p, (w, rows) in enumerate(pieces)]

        def pass_on():
            for k in range(N_CHIPS):
                own_waited = set()
                for p, (w, rows) in enumerate(pieces):
                    if k < 3:
                        first[k * npc + p].wait_recv()
                    elif w not in own_waited:
                        mine[w].wait()
                        own_waited.add(w)
                    passed[k * npc + p].start()

        def finish():
            for k in range(N_CHIPS):
                for p, (w, rows) in enumerate(pieces):
                    slab = o_refs[w].at[1 - c, k, rows]
                    _remote(slab, slab, send_sems.at[3 + k, p], recv_sems.at[3 + k, p], sibling).wait_recv()
            for cp in first + passed:
                cp.wait_send()

        return start, pass_on, finish

    return _Exchange(list(sums), [jax.ShapeDtypeStruct((2,) + p.shape, BF16) for p in sums],
                     [pltpu.SemaphoreType.DMA((7, npc)), pltpu.SemaphoreType.DMA((7, npc)),
                      pltpu.SemaphoreType.DMA((nw,))], build)


def _small_sum_exchange(vec):
    m_per, n = vec.shape

    def build(ins, outs, scr):
        (x_ref,), (out_ref,) = ins, outs
        gath_ref, sum_ref, send_sems, recv_sems, local_sem, out_sem = scr
        x, y, c, chips = _place()
        me, sibling = (x, y, c), (x, y, 1 - c)

        def rows(px, py, pc):
            return gath_ref.at[pl.ds((4 * px + 2 * py + pc) * m_per, m_per), :]

        def copy(k, block, to, src=None):
            return pltpu.make_async_remote_copy(
                src_ref=rows(*block) if src is None else src, dst_ref=rows(*block),
                send_sem=send_sems.at[k], recv_sem=recv_sems.at[k], device_id=to, device_id_type=MESH)

        mine = pltpu.make_async_copy(x_ref, rows(*me), local_sem)
        first = [copy(0, me, sibling, src=x_ref)] + [copy(1 + j, me, (*chip, c), src=x_ref)
                                                     for j, chip in enumerate(chips)]

        def start():
            for cp in [mine] + first:
                cp.start()

        passed = [copy(4 + j, (*chip, c), sibling) for j, chip in enumerate(chips)]

        def pass_on():
            for j, chip in enumerate(chips):
                copy(1 + j, (*chip, c), me).wait_recv()
                passed[j].start()

        def finish():
            copy(0, sibling, me).wait_recv()
            for j, chip in enumerate(chips):
                copy(4 + j, (*chip, 1 - c), me).wait_recv()
            for cp in first + passed:
                cp.wait_send()
            mine.wait()
            acc = gath_ref[pl.ds(0, m_per), :]
            for k in range(1, N_DEV):
                acc = acc + gath_ref[pl.ds(k * m_per, m_per), :]
            sum_ref[...] = acc
            done = pltpu.make_async_copy(sum_ref, out_ref, out_sem)
            done.start()
            done.wait()

        return start, pass_on, finish

    return _Exchange([vec], [jax.ShapeDtypeStruct((m_per, n), F32)],
                     [pltpu.VMEM((N_DEV * m_per, n), F32), pltpu.VMEM((m_per, n), F32), pltpu.SemaphoreType.DMA((7,)),
                      pltpu.SemaphoreType.DMA((7,)), pltpu.SemaphoreType.DMA, pltpu.SemaphoreType.DMA], build)


def _pack_small(vals, tail=()):
    flat = jnp.concatenate([vals[name].reshape(-1).astype(F32) for name, _ in SMALL] + [v.reshape(1) for v in tail])
    flat = jnp.pad(flat, (0, SMALL_ROWS * LANES - flat.shape[0]))
    return flat.reshape(SMALL_ROWS, LANES)


def _unpack_small(packed):
    flat = packed.reshape(-1)
    out, off = {}, 0
    for name, shape in SMALL:
        n = int(np.prod(shape))
        out[name] = flat[off:off + n].reshape(shape)
        off += n
    return out


def _head_pad_cols(w, heads, real):
    k = w.shape[0]
    return jnp.pad(w.reshape(k, heads, real), ((0, 0), (0, 0), (0, LANES - real))).reshape(k, heads * LANES)


def _rope_tables(positions):
    half = MLA_ROPE // 2
    inv = ROPE_BASE ** (-jnp.arange(half, dtype=F32) / half)
    ang = positions.astype(F32)[:, None] * inv
    cos, sin = jnp.cos(ang), jnp.sin(ang)
    t = positions.shape[0]
    z = lambda n: jnp.zeros((t, n), F32)
    rc = jnp.concatenate([jnp.ones((t, MLA_NOPE), F32), cos, cos, z(LANES - MLA_QK)], axis=1)
    rs1 = jnp.concatenate([z(MLA_NOPE), -sin, z(LANES - MLA_NOPE - half)], axis=1)
    rs2 = jnp.concatenate([z(MLA_NOPE + half), sin, z(LANES - MLA_QK)], axis=1)
    return rc, rs1, rs2


FFN1_WEIGHTS = ("ffn1_w_gu", "ffn1_w_down")
FFN2_WEIGHTS = ("ffn2_w_gu", "ffn2_w_down")
MIXER_WEIGHTS = tuple(n for n, *_ in SHARDED if n not in FFN1_WEIGHTS + FFN2_WEIGHTS)
SHARD_SHAPE = {n: (r, c, kind) for n, r, c, kind in SHARDED}


def _from_blocks(name, gathered):
    r, c, kind = SHARD_SHAPE[name]
    blk = gathered.reshape(N_CHIPS, r, c)
    return blk, (blk.transpose(1, 0, 2).reshape(r, N_CHIPS * c) if kind == "col" else blk.reshape(N_CHIPS * r, c))


def _grad_pair_sums(names, gw, core, tag):
    by_owner = []
    for name in names:
        r, c, kind = SHARD_SHAPE[name]
        if gw[name].dtype == BF16:
            blk = gw[name]
        elif kind == "col":
            blk = gw[name].reshape(r, N_CHIPS, c).transpose(1, 0, 2)
        else:
            blk = gw[name].reshape(N_CHIPS, r, c)
        by_owner.append(blk.astype(BF16).reshape(N_CHIPS, 2, r // 2, c))
    received = _swap_halves(by_owner, "grad_swap_" + tag)
    return [_pair_sum(g, s, core, "pair_sum_" + n) for g, s, n in zip(by_owner, received, names)]


def _device_step(x, mem, positions, tgt, small, shards, core):
    d = D_MODEL
    g_ffn1, g_mix, g_ffn2 = small["ffn1_norm"], small["mix_norm"], small["ffn2_norm"]
    big = {}
    for name, g in zip(FFN1_WEIGHTS, _run_exchange(_gather_exchange([shards[n] for n in FFN1_WEIGHTS]), "gather_ffn1")):
        big[name + "#blocks"], big[name] = _from_blocks(name, g)
    wgu1, wd1 = big["ffn1_w_gu#blocks"], big["ffn1_w_down"].reshape(2, FF_TILE, d)
    x1, gpre1, upre1, h, *rest = _ffn_fwd(x, g_ffn1, wgu1, wd1, "ffn1_fwd", next_gain=g_mix,
                                          ex=_gather_exchange([shards[n] for n in MIXER_WEIGHTS]))
    for name, g in zip(MIXER_WEIGHTS, rest):
        big[name + "#blocks"], big[name] = _from_blocks(name, g)
    w_in = big["w_in"]
    w_uv_, w_cq, w_ckv = w_in[:, :COL_CQ], w_in[:, COL_CQ:COL_CKV], w_in[:, COL_CKV:COL_KR]
    w_kr = jnp.pad(w_in[:, COL_KR:COL_QM], ((0, 0), (MLA_NOPE, LANES - MLA_QK)))
    w_qm, w_g = w_in[:, COL_QM:COL_GATE], w_in[:, COL_GATE:]
    segs = (w_uv_, w_cq, w_ckv, w_kr, w_qm, w_g)
    wuq = _head_pad_cols(big["mla_w_uq"], MLA_HEADS, MLA_QK)
    ukv = big["mla_w_ukv"].reshape(MLA_KV_RANK, MLA_HEADS, 2, MLA_NOPE)
    wuk = _head_pad_cols(ukv[:, :, 0].reshape(MLA_KV_RANK, -1), MLA_HEADS, MLA_NOPE)
    wuv = _head_pad_cols(ukv[:, :, 1].reshape(MLA_KV_RANK, -1), MLA_HEADS, MLA_NOPE)
    wkv = big["mem_w_kv"]
    wa, wc, wo = big["w_branch_a"], big["w_branch_c"], big["w_out"]
    wb = jnp.pad(big["w_branch_b"].reshape(MLA_HEADS, MLA_NOPE, d),
                 ((0, 0), (0, LANES - MLA_NOPE), (0, 0))).reshape(MLA_HEADS * LANES, d)
    qg = jnp.pad(small["mla_q_norm"], ((0, 0), (0, LANES - MLA_QK)))
    kg = jnp.pad(small["mla_k_norm"], ((0, 0), (0, LANES - MLA_QK)))
    causal = jnp.tril(jnp.ones((CHUNK, CHUNK), bool))
    wt_f = jnp.where(causal[None], small["sg_w"][0], 0.0)
    wt, wt_t = wt_f.astype(BF16), wt_f.transpose(0, 2, 1).astype(BF16)
    bias_l = jnp.repeat(small["sg_b"][0].T, 64, axis=1)
    rc, rs1, rs2 = _rope_tables(positions)

    zuv, zcq, zckv, zkr, zqm, zg = _mm_cols(h, segs, [F32] * 5 + [BF16], "in_proj")
    ya = _sgu_fwd(zuv, small["sg_ln_g"], small["sg_ln_b"], wt, bias_l, "sgu_fwd")
    q, k, v, cqn, ckvn = _mla_prep_fwd(zcq, zckv, zkr, small["mla_cq_norm"], small["mla_ckv_norm"], qg, kg,
                                       wuq, wuk, wuv, rc, rs1, rs2, "mla_prep_fwd")
    yb, lse, *rest = _attn_fwd(q, k, v, "mla_attn_fwd", ex=_gather_exchange([shards[n] for n in FFN2_WEIGHTS]))
    for name, g in zip(FFN2_WEIGHTS, rest):
        big[name + "#blocks"], big[name] = _from_blocks(name, g)
    wgu2, wd2 = big["ffn2_w_gu#blocks"], big["ffn2_w_down"].reshape(2, FF_TILE, d)
    km, vm, memn = _mem_kv_fwd(mem, small["mem_norm"], wkv, small["mem_k_norm"], "mem_kv_fwd")
    yc = _mem_attn_fwd(zqm, small["mem_q_norm"], km, vm, "mem_attn_fwd")
    x2, merged, pa, pb, pc = _merge_fwd(x1, ya, yb, yc, zg, small["b_gate"], wa, wb, wc, wo, "merge_fwd")
    dy, loss_row, gpre2, upre2 = _ffn_fwd(x2, g_ffn2, wgu2, wd2, "ffn2_fwd", target=tgt)

    gw, gs, slots = {}, {}, {}

    def ffn_grads(prefix, xin, gain, dyin, gpre, upre, wgu, wd, ex=None, ex_names=(), last=False):
        dx, dgain, xn, dgt, dup, act, *got = _ffn_bwd(xin, gain, dyin, gpre, upre, wgu, wd, prefix + "_bwd", ex=ex)
        slots.update(zip(ex_names, got))
        gs[prefix + "_norm"] = dgain
        gw[prefix + "_w_gu"] = jnp.concatenate(
            [_mm_tn(xn, dgt, prefix + "_dwg", col_blocks=True, out_dtype=BF16),
             _mm_tn(xn, dup, prefix + "_dwu", col_blocks=True, out_dtype=BF16)], axis=0)
        rows_down = SHARD_SHAPE[prefix + "_w_down"][0]
        if last:
            small_sum = _small_sum_exchange(_pack_small(gs, tail=[loss_row[0, 0]]))
            dwd, summed = _mm_tn(act, dyin, prefix + "_dwd", scale=0.5, ex=small_sum, out_dtype=BF16)
            gw[prefix + "_w_down"] = dwd.reshape(N_CHIPS, rows_down, d)
            return dx, summed
        gw[prefix + "_w_down"] = _mm_tn(act, dyin, prefix + "_dwd", scale=0.5, out_dtype=BF16).reshape(
            N_CHIPS, rows_down, d)
        return dx

    dx2 = ffn_grads("ffn2", x2, g_ffn2, dy, gpre2, upre2, wgu2, wd2)
    ffn2_sums = _pair_sum_exchange(_grad_pair_sums(FFN2_WEIGHTS, gw, core, "ffn2"))
    dpa, dpb, dpc, dzg, dbg, dya, dyb, dyc, *got = _merge_bwd(dx2, pa, pb, pc, zg, small["b_gate"], wa, wb, wc, wo,
                                                              "merge_bwd", ex=ffn2_sums)
    slots.update(zip(FFN2_WEIGHTS, got))
    gs["b_gate"] = dbg
    gw["w_out"] = _mm_tn(merged, dx2, "dw_out")
    gw["w_branch_a"] = _mm_tn(ya, dpa, "dw_branch_a")
    gw["w_branch_b"] = _mm_tn(yb, dpb, "dw_branch_b").reshape(MLA_HEADS, LANES, d)[:, :MLA_NOPE].reshape(-1, d)
    gw["w_branch_c"] = _mm_tn(yc, dpc, "dw_branch_c")

    dzuv, dwt, dbl, dlg, dlb = _sgu_bwd(zuv, dya, small["sg_ln_g"], small["sg_ln_b"], wt, wt_t, bias_l, "sgu_bwd")
    gs["sg_w"], gs["sg_b"] = dwt[None], dbl[:, :SG_GROUPS].T[None]
    gs["sg_ln_g"], gs["sg_ln_b"] = dlg, dlb

    delta_rows, lse_rows = _attn_bwd_rows(yb, lse, dyb, "mla_attn_bwd_rows")
    dq, dk, dv = _attn_bwd(q, k, v, delta_rows, lse_rows, dyb, "mla_attn_bwd")
    dzcq, dzckv, dzkr, dql, dkl, dgcq, dgckv, dqg, dkg = _mla_prep_bwd(
        zcq, zckv, zkr, small["mla_cq_norm"], small["mla_ckv_norm"], qg, kg, wuq, wuk, wuv, rc, rs1, rs2,
        dq, dk, dv, "mla_prep_bwd")
    gs["mla_cq_norm"], gs["mla_ckv_norm"] = dgcq, dgckv
    gs["mla_q_norm"], gs["mla_k_norm"] = dqg[:, :MLA_QK], dkg[:, :MLA_QK]
    gw["mla_w_uq"] = _mm_tn(cqn, dql, "dw_uq").reshape(MLA_Q_RANK, MLA_HEADS, LANES)[:, :, :MLA_QK].reshape(
        MLA_Q_RANK, -1)
    dwuk = _mm_tn(ckvn, dkl, "dw_uk").reshape(MLA_KV_RANK, MLA_HEADS, LANES)[:, :, :MLA_NOPE]
    dwuv = _mm_tn(ckvn, dv, "dw_uv").reshape(MLA_KV_RANK, MLA_HEADS, LANES)[:, :, :MLA_NOPE]
    gw["mla_w_ukv"] = jnp.concatenate([dwuk, dwuv], axis=2).reshape(MLA_KV_RANK, -1)

    dzqm, dkn, dvm, dmqg = _mem_attn_bwd(zqm, dyc, small["mem_q_norm"], km, vm, "mem_attn_bwd")
    gs["mem_q_norm"] = dmqg
    gw["mem_w_kv"], gs["mem_k_norm"], gs["mem_norm"] = _mem_kv_bwd(
        mem, small["mem_norm"], wkv, small["mem_k_norm"], dkn, dvm, "mem_kv_bwd")

    dzs = (dzuv, dzcq, dzckv, dzkr, dzqm, dzg)
    dws = list(_mm_tn_cols(h, dzs[:5], "dw_in_narrow")) + [_mm_tn(h, dzg, "dw_in_gate")]
    dws[3] = dws[3][:, MLA_NOPE:MLA_QK]
    gw["w_in"] = jnp.concatenate(dws, axis=1)
    dx1, gs["mix_norm"] = _proj_norm_bwd(dzs, [w.T for w in segs], x1, g_mix, dx2, "in_proj_bwd")
    mixer_sums = _pair_sum_exchange(_grad_pair_sums(MIXER_WEIGHTS, gw, core, "mixer"))
    dx, summed = ffn_grads("ffn1", x, g_ffn1, dx1, gpre1, upre1, wgu1, wd1, ex=mixer_sums, ex_names=MIXER_WEIGHTS,
                           last=True)
    ffn1_sums = _pair_sum_exchange(_grad_pair_sums(FFN1_WEIGHTS, gw, core, "ffn1"))
    slots.update(zip(FFN1_WEIGHTS, _run_exchange(ffn1_sums, "grad_exchange_ffn1")))
    return dx, slots, summed


def kernel(x, mem, positions, ffn1_norm, ffn1_w_gu, ffn1_w_down, mix_norm, w_in, b_gate, sg_ln_g, sg_ln_b, sg_w, sg_b, mla_cq_norm, mla_w_uq, mla_ckv_norm, mla_w_ukv, mla_q_norm, mla_k_norm, mem_norm, mem_w_kv, mem_q_norm, mem_k_norm, w_branch_a, w_branch_b, w_branch_c, w_out, ffn2_norm, ffn2_w_gu, ffn2_w_down, loss_target, m_ffn1_norm, m_ffn1_w_gu, m_ffn1_w_down, m_mix_norm, m_w_in, m_b_gate, m_sg_ln_g, m_sg_ln_b, m_sg_w, m_sg_b, m_mla_cq_norm, m_mla_w_uq, m_mla_ckv_norm, m_mla_w_ukv, m_mla_q_norm, m_mla_k_norm, m_mem_norm, m_mem_w_kv, m_mem_q_norm, m_mem_k_norm, m_w_branch_a, m_w_branch_b, m_w_branch_c, m_w_out, m_ffn2_norm, m_ffn2_w_gu, m_ffn2_w_down, v_ffn1_norm, v_ffn1_w_gu, v_ffn1_w_down, v_mix_norm, v_w_in, v_b_gate, v_sg_ln_g, v_sg_ln_b, v_sg_w, v_sg_b, v_mla_cq_norm, v_mla_w_uq, v_mla_ckv_norm, v_mla_w_ukv, v_mla_q_norm, v_mla_k_norm, v_mem_norm, v_mem_w_kv, v_mem_q_norm, v_mem_k_norm, v_w_branch_a, v_w_branch_b, v_w_branch_c, v_w_out, v_ffn2_norm, v_ffn2_w_gu, v_ffn2_w_down):
    args = dict(locals())
    weights = {n: args[n] for n in WEIGHT_ORDER}
    mom_m = {n: args["m_" + n] for n in WEIGHT_ORDER}
    mom_v = {n: args["v_" + n] for n in WEIGHT_ORDER}
    small = {n: weights[n] for n, _ in SMALL}
    halves = lambda a, r, c: a.reshape(2, r // 2, c)

    shards = {n: halves(weights[n][0].astype(BF16), r, c) for n, r, c, _ in SHARDED}
    core = lax.axis_index("c").astype(jnp.int32).reshape(1)
    dx, slots, summed = _device_step(x[0], mem[0], positions[0], loss_target[0], small, shards, core)
    loss = summed.reshape(-1)[_N_SMALL]
    small_grads = _unpack_small(summed)

    grads, deltas, new_m, new_v = {}, {}, {}, {}
    for name, r, c, _ in SHARDED:
        outs = _adamw_slots(halves(weights[name][0], r, c), slots[name], halves(mom_m[name][0], r, c),
                            halves(mom_v[name][0], r, c), "adamw_" + name)
        shape = weights[name].shape
        grads[name], deltas[name], new_m[name], new_v[name] = [o.reshape(shape) for o in outs]
    dlt, nm, nv = _adamw(_pack_small(small), _pack_small(small_grads), _pack_small({n: mom_m[n] for n, _ in SMALL}),
                         _pack_small({n: mom_v[n] for n, _ in SMALL}), "adamw_small")
    for name, _ in SMALL:
        grads[name] = small_grads[name]
    deltas.update(_unpack_small(dlt))
    new_m.update(_unpack_small(nm))
    new_v.update(_unpack_small(nv))

    return (loss, dx[None], *[grads[n] for n in WEIGHT_ORDER], *[deltas[n] for n in WEIGHT_ORDER],
            *[new_m[n] for n in WEIGHT_ORDER], *[new_v[n] for n in WEIGHT_ORDER])
```

```python
import functools
from typing import Callable, NamedTuple

import numpy as np
import jax
import jax.numpy as jnp
from jax import lax
from jax.experimental import pallas as pl
from jax.experimental.pallas import tpu as pltpu

F32 = jnp.float32
BF16 = jnp.bfloat16

D_MODEL = 1024
D_FF = 2816
FF_TILE = 1408
SG_WIDTH = 512
SG_GROUPS = 8
CHUNK = 128
MLA_HEADS = 8
MLA_QK = 96
MLA_NOPE = 64
MLA_ROPE = 32
MLA_Q_RANK = 384
MLA_KV_RANK = 256
MEM_HEADS = 4
LANES = 128
EPS = 1e-6
NEG = -1e30
ROPE_BASE = 10000.0
N_CHIPS = 4
N_DEV = 8

ADAM_LR = 0.001
ADAM_B1 = 0.9
ADAM_B2 = 0.999
ADAM_EPS = 1e-08
ADAM_WD = 0.01
ADAM_STEP = 10

COL_CQ = 1024
COL_CKV = 1408
COL_KR = 1664
COL_QM = 1696
COL_GATE = 2208

VMEM_LIMIT_BYTES = 56 * 1024 * 1024
INV_SQRT2 = 0.7071067811865476
INV_SQRT_2PI = 0.3989422804014327
LOG2E = 1.4426950408889634
ATTN_SCALE = MLA_QK ** -0.5
V_ONES_LANE = 64
ATTN_SCALE2 = ATTN_SCALE * LOG2E

SHARDED = (
    ("ffn1_w_gu", 1024, 1408, "col"),
    ("ffn1_w_down", 704, 1024, "row"),
    ("w_in", 1024, 1320, "col"),
    ("mla_w_uq", 384, 192, "col"),
    ("mla_w_ukv", 256, 256, "col"),
    ("mem_w_kv", 256, 1024, "row"),
    ("w_branch_a", 512, 256, "col"),
    ("w_branch_b", 512, 256, "col"),
    ("w_branch_c", 512, 256, "col"),
    ("w_out", 256, 1024, "row"),
    ("ffn2_w_gu", 1024, 1408, "col"),
    ("ffn2_w_down", 704, 1024, "row"),
)
SMALL = (
    ("ffn1_norm", (1, 1024)), ("mix_norm", (1, 1024)), ("b_gate", (1, 3072)),
    ("sg_ln_g", (1, 512)), ("sg_ln_b", (1, 512)), ("sg_w", (1, 8, 128, 128)),
    ("sg_b", (1, 8, 128)), ("mla_cq_norm", (1, 384)), ("mla_ckv_norm", (1, 256)),
    ("mla_q_norm", (1, 96)), ("mla_k_norm", (1, 96)), ("mem_norm", (1, 1024)),
    ("mem_q_norm", (1, 128)), ("mem_k_norm", (1, 128)), ("ffn2_norm", (1, 1024)),
)
WEIGHT_ORDER = (
    "ffn1_norm", "ffn1_w_gu", "ffn1_w_down", "mix_norm", "w_in", "b_gate", "sg_ln_g", "sg_ln_b",
    "sg_w", "sg_b", "mla_cq_norm", "mla_w_uq", "mla_ckv_norm", "mla_w_ukv", "mla_q_norm",
    "mla_k_norm", "mem_norm", "mem_w_kv", "mem_q_norm", "mem_k_norm", "w_branch_a", "w_branch_b",
    "w_branch_c", "w_out", "ffn2_norm", "ffn2_w_gu", "ffn2_w_down",
)

_N_SMALL = sum(int(np.prod(s)) for _, s in SMALL)
SMALL_ROWS = -(-_N_SMALL // (LANES * 8)) * 8

MESH = pl.DeviceIdType.MESH


def _cparams():
    return pltpu.CompilerParams(vmem_limit_bytes=VMEM_LIMIT_BYTES)


def _dot(a, b):
    return jnp.dot(a, b, preferred_element_type=F32)


def _dot_nt(a, b):
    return lax.dot_general(a, b, (((1,), (1,)), ((), ())), preferred_element_type=F32)


def _dot_tn(a, b):
    return lax.dot_general(a, b, (((0,), (0,)), ((), ())), preferred_element_type=F32)


def _gelu(x):
    return 0.5 * x * (1.0 + lax.erf(x * INV_SQRT2))


def _gelu_grad(x):
    return 0.5 * (1.0 + lax.erf(x * INV_SQRT2)) + x * jnp.exp(-0.5 * x * x) * INV_SQRT_2PI


def _rstd(x, n):
    return lax.rsqrt(jnp.sum(x * x, axis=-1, keepdims=True) * (1.0 / n) + EPS)


def _rms_vjp(x, r, g, dy, n):
    dxh = dy * g
    dx = r * dxh - x * (r * r * r) * (jnp.sum(dxh * x, axis=-1, keepdims=True) * (1.0 / n))
    return dx, dy * x * r


def _row_tile(t, want):
    return min(t, want)


def _wide_tile(n):
    if n <= 1024:
        return n
    if n % 1024 == 0:
        return 1024
    assert n % FF_TILE == 0, n
    return FF_TILE


def _mm_cols(a, ws, out_dtypes, name, ex=None):
    t, kdim = a.shape
    tm = _row_tile(t, 512)
    n = len(ws)

    def body(*refs):
        av = refs[0][...]
        for w_ref, o_ref in zip(refs[1:1 + n], refs[1 + n:]):
            o_ref[...] = _dot(av, w_ref[...]).astype(o_ref.dtype)

    row = lambda width: pl.BlockSpec((tm, width), lambda i: (i, 0))
    return _call_with_exchange(
        ex, body, name, (t // tm,),
        [row(kdim)] + [pl.BlockSpec(w.shape, lambda i: (0, 0)) for w in ws],
        [row(w.shape[1]) for w in ws],
        [jax.ShapeDtypeStruct((t, w.shape[1]), dt) for w, dt in zip(ws, out_dtypes)], [], (a, *ws))


def _proj_norm_bwd(dzs, wts, x, g, dres, name):
    t, d = x.shape
    tm = _row_tile(t, 256)
    n = len(dzs)

    def body(*refs):
        x_ref, g_ref, r_ref, dx_ref, dg_ref = refs[2 * n:]

        @pl.when(pl.program_id(0) == 0)
        def _():
            dg_ref[...] = jnp.zeros_like(dg_ref)

        dh = None
        for dz_ref, w_ref in zip(refs[:n], refs[n:2 * n]):
            part = _dot(dz_ref[...], w_ref[...])
            dh = part if dh is None else dh + part
        xv = x_ref[...]
        dx, dgr = _rms_vjp(xv, _rstd(xv, d), g_ref[...], dh, d)
        dx_ref[...] = r_ref[...] + dx
        dg_ref[...] += jnp.sum(dgr, axis=0, keepdims=True)

    row = lambda width: pl.BlockSpec((tm, width), lambda i: (i, 0))
    vec = pl.BlockSpec((1, d), lambda i: (0, 0))
    return pl.pallas_call(
        body, name=name, grid=(t // tm,),
        in_specs=[row(dz.shape[1]) for dz in dzs] + [pl.BlockSpec(w.shape, lambda i: (0, 0)) for w in wts]
        + [row(d), vec, row(d)],
        out_specs=[row(d), vec],
        out_shape=[jax.ShapeDtypeStruct((t, d), F32), jax.ShapeDtypeStruct((1, d), F32)],
        compiler_params=_cparams())(*dzs, *wts, x, g, dres)


def _mm_tn_cols(a, bs, name):
    t, m = a.shape
    tk = _row_tile(t, 1024)
    n = len(bs)

    def body(*refs):
        @pl.when(pl.program_id(0) == 0)
        def _():
            for o_ref in refs[1 + n:]:
                o_ref[...] = jnp.zeros_like(o_ref)

        av = refs[0][...].astype(BF16)
        for b_ref, o_ref in zip(refs[1:1 + n], refs[1 + n:]):
            o_ref[...] += _dot_tn(av, b_ref[...].astype(BF16))

    row = lambda width: pl.BlockSpec((tk, width), lambda k: (k, 0))
    return pl.pallas_call(
        body, name=name, grid=(t // tk,), in_specs=[row(m)] + [row(b.shape[1]) for b in bs],
        out_specs=[pl.BlockSpec((m, b.shape[1]), lambda k: (0, 0)) for b in bs],
        out_shape=[jax.ShapeDtypeStruct((m, b.shape[1]), F32) for b in bs],
        compiler_params=_cparams())(a, *bs)


def _mm_tn(a, b, name, scale=1.0, ex=None, col_blocks=False, out_dtype=F32):
    t, m = a.shape
    n = b.shape[1]
    tm, tn = _wide_tile(m), _wide_tile(n)
    tk = _row_tile(t, 2048)
    nk = t // tk
    in_place = out_dtype == F32

    def body(a_ref, b_ref, o_ref, *scr):
        k = pl.program_id(2)
        acc_ref = o_ref if in_place else scr[0]

        @pl.when(k == 0)
        def _():
            acc_ref[...] = jnp.zeros_like(acc_ref)

        prod = _dot_tn(a_ref[...].astype(BF16), b_ref[...].astype(BF16))
        acc_ref[...] += prod.reshape(acc_ref.shape)
        if scale != 1.0 or not in_place:
            @pl.when(k == nk - 1)
            def _():
                o_ref[...] = (acc_ref[...] * scale).astype(out_dtype).reshape(o_ref.shape)

    if col_blocks:
        out_spec = pl.BlockSpec((1, tm, tn), lambda i, j, k: (j, i, 0))
        out_shape = jax.ShapeDtypeStruct((n // tn, m, tn), out_dtype)
    else:
        out_spec = pl.BlockSpec((tm, tn), lambda i, j, k: (i, j))
        out_shape = jax.ShapeDtypeStruct((m, n), out_dtype)
    outs = _call_with_exchange(
        ex, body, name, (m // tm, n // tn, nk),
        [pl.BlockSpec((tk, tm), lambda i, j, k: (k, i)), pl.BlockSpec((tk, tn), lambda i, j, k: (k, j))],
        [out_spec], [out_shape], [] if in_place else [pltpu.VMEM((tm, tn), F32)], (a, b))
    return outs[0] if ex is None else outs


PASS_ON_STEPS_BEFORE_END = 8


class _Exchange(NamedTuple):
    operands: list
    out_shapes: list
    sem_shapes: list
    build: Callable


def _call_with_exchange(ex, body, name, grid, in_specs, out_specs, out_shape, scratch_shapes, operands, prefetch=()):
    n_pre = len(prefetch)
    total = int(np.prod(grid))
    pass_step = max(total // 2, total - PASS_ON_STEPS_BEFORE_END)

    def call(kernel, ins, outs, shapes, scratch):
        if n_pre:
            spec = pltpu.PrefetchScalarGridSpec(num_scalar_prefetch=n_pre, grid=grid, in_specs=ins, out_specs=outs,
                                                scratch_shapes=scratch)
            return pl.pallas_call(kernel, name=name, grid_spec=spec, out_shape=shapes, compiler_params=_cparams())
        return pl.pallas_call(kernel, name=name, grid=grid, in_specs=ins, out_specs=outs, out_shape=shapes,
                              scratch_shapes=scratch, compiler_params=_cparams())

    if ex is None:
        return call(body, in_specs, out_specs, out_shape, scratch_shapes)(*prefetch, *operands)
    n_in, n_out, n_scr = len(in_specs), len(out_specs), len(scratch_shapes)
    k_in, k_out = len(ex.operands), len(ex.out_shapes)

    def carried(*refs):
        pre, refs = refs[:n_pre], refs[n_pre:]
        a, b = n_in, n_in + k_in
        c, e = b + n_out, b + n_out + k_out
        f = e + n_scr
        start, pass_on, finish = ex.build(refs[a:b], refs[c:e], refs[f:])
        step = functools.reduce(lambda lin, ax: lin * grid[ax] + pl.program_id(ax), range(len(grid)), 0)
        pl.when(step == 0)(start)
        body(*pre, *refs[:a], *refs[b:c], *refs[e:f])
        pl.when(step == pass_step)(pass_on)
        pl.when(step == total - 1)(finish)

    return call(carried, list(in_specs) + [ANY] * k_in, list(out_specs) + [ANY] * k_out,
                list(out_shape) + list(ex.out_shapes), list(scratch_shapes) + list(ex.sem_shapes),
                )(*prefetch, *operands, *ex.operands)


def _run_exchange(ex, name):
    k_in, k_out = len(ex.operands), len(ex.out_shapes)

    def body(*refs):
        start, pass_on, finish = ex.build(refs[:k_in], refs[k_in:k_in + k_out], refs[k_in + k_out:])
        start()
        pass_on()
        finish()

    return pl.pallas_call(body, name=name, in_specs=[ANY] * k_in, out_specs=[ANY] * k_out,
                          out_shape=list(ex.out_shapes), scratch_shapes=list(ex.sem_shapes))(*ex.operands)


def _ffn_fwd(x, g, wgu4, wd2, name, ex=None, next_gain=None, target=None):
    t, d = x.shape
    tm = _row_tile(t, 512)
    assert next_gain is None or target is None
    extra = [a for a in (next_gain, target) if a is not None]

    def body(*refs):
        x_ref, g_ref, wg_ref, wu_ref, wd_ref = refs[:5]
        e_ref = refs[5] if extra else None
        outs, (xn_scr, acc_scr) = refs[5 + len(extra):-2], refs[-2:]
        if target is not None:
            dy_ref, loss_ref, gg_ref, uu_ref = outs
        elif next_gain is not None:
            o_ref, gg_ref, uu_ref, h_ref = outs
        else:
            o_ref, gg_ref, uu_ref = outs
        i, j = pl.program_id(0), pl.program_id(1)

        @pl.when(j == 0)
        def _():
            xv = x_ref[...]
            xn_scr[...] = (xv * _rstd(xv, d) * g_ref[...]).astype(BF16)
            acc_scr[...] = jnp.zeros_like(acc_scr)

        if target is not None:
            @pl.when((i == 0) & (j == 0))
            def _():
                loss_ref[...] = jnp.zeros_like(loss_ref)

        xn = xn_scr[...]
        gg = _dot(xn, wg_ref[0])
        uu = _dot(xn, wu_ref[0])
        gg_ref[...] = gg.astype(BF16)
        uu_ref[...] = uu.astype(BF16)
        act = gg * jax.nn.sigmoid(gg) * uu
        acc_scr[...] += _dot(act.astype(BF16), wd_ref[0])

        @pl.when(j == 1)
        def _():
            y = x_ref[...] + 0.5 * acc_scr[...]
            if target is not None:
                e = y - e_ref[...]
                dy_ref[...] = e * (1.0 / d)
                part = 0.5 * jnp.sum(jnp.sum(e * e, axis=-1, keepdims=True) * (1.0 / d), axis=0, keepdims=True)
                loss_ref[...] += jnp.broadcast_to(part, loss_ref.shape)
            else:
                o_ref[...] = y
                if next_gain is not None:
                    h_ref[...] = (y * _rstd(y, d) * e_ref[...]).astype(BF16)

    row = pl.BlockSpec((tm, d), lambda i, j: (i, 0))
    vec = pl.BlockSpec((1, d), lambda i, j: (0, 0))
    ffb = pl.BlockSpec((tm, FF_TILE), lambda i, j: (i, j))
    f32_rows, bf16_ff = jax.ShapeDtypeStruct((t, d), F32), jax.ShapeDtypeStruct((t, D_FF), BF16)
    if target is not None:
        extra_spec, out_specs = [row], [row, pl.BlockSpec((1, LANES), lambda i, j: (0, 0)), ffb, ffb]
        out_shape = [f32_rows, jax.ShapeDtypeStruct((1, LANES), F32), bf16_ff, bf16_ff]
    elif next_gain is not None:
        extra_spec, out_specs = [vec], [row, ffb, ffb, row]
        out_shape = [f32_rows, bf16_ff, bf16_ff, jax.ShapeDtypeStruct((t, d), BF16)]
    else:
        extra_spec, out_specs, out_shape = [], [row, ffb, ffb], [f32_rows, bf16_ff, bf16_ff]
    return _call_with_exchange(
        ex, body, name, (t // tm, 2),
        [row, vec,
         pl.BlockSpec((1, d, FF_TILE), lambda i, j: (j, 0, 0)),
         pl.BlockSpec((1, d, FF_TILE), lambda i, j: (j + 2, 0, 0)),
         pl.BlockSpec((1, FF_TILE, d), lambda i, j: (j, 0, 0))] + extra_spec,
        out_specs, out_shape,
        [pltpu.VMEM((tm, d), BF16), pltpu.VMEM((tm, d), F32)], (x, g, wgu4, wgu4, wd2, *extra))


def _ffn_bwd(x, g, dy, gpre, upre, wgu4, wd2, name, ex=None):
    t, d = x.shape
    tm = _row_tile(t, 512)

    def body(dy_ref, gg_ref, uu_ref, wgu_hbm, wd_hbm, dg_ref, du_ref, act_ref, part_ref, wg_ref, wu_ref, wd_ref):
        j = pl.program_id(0)

        @pl.when(pl.program_id(1) == 0)
        def _():
            pltpu.sync_copy(wgu_hbm.at[j], wg_ref.at[0])
            pltpu.sync_copy(wgu_hbm.at[j + 2], wu_ref.at[0])
            pltpu.sync_copy(wd_hbm.at[j], wd_ref.at[0])

        gg = gg_ref[...].astype(F32)
        uu = uu_ref[...].astype(F32)
        sg = jax.nn.sigmoid(gg)
        silu = gg * sg
        act_ref[...] = (silu * uu).astype(BF16)
        dyh = (0.5 * dy_ref[...]).astype(BF16)
        dact = _dot_nt(dyh, wd_ref[0])
        du = (dact * silu).astype(BF16)
        dgt = (dact * uu * (sg * (1.0 + gg * (1.0 - sg)))).astype(BF16)
        du_ref[...] = du
        dg_ref[...] = dgt
        part_ref[0] = (_dot_nt(dgt, wg_ref[0]) + _dot_nt(du, wu_ref[0])).astype(BF16)

    row = pl.BlockSpec((tm, d), lambda j, i: (i, 0))
    ffb = pl.BlockSpec((tm, FF_TILE), lambda j, i: (i, j))
    dgt, dup, act, parts, *got = _call_with_exchange(
        ex, body, name, (2, t // tm),
        [row, ffb, ffb, ANY, ANY],
        [ffb, ffb, ffb, pl.BlockSpec((1, tm, d), lambda j, i: (j, i, 0))],
        [jax.ShapeDtypeStruct((t, D_FF), BF16)] * 3 + [jax.ShapeDtypeStruct((2, t, d), BF16)],
        [pltpu.VMEM((1, d, FF_TILE), BF16), pltpu.VMEM((1, d, FF_TILE), BF16), pltpu.VMEM((1, FF_TILE, d), BF16)],
        (dy, gpre, upre, wgu4, wd2))

    def norm_body(x_ref, g_ref, p_ref, dy_ref, dx_ref, dgain_ref, xn_ref):
        @pl.when(pl.program_id(0) == 0)
        def _():
            dgain_ref[...] = jnp.zeros_like(dgain_ref)

        xv = x_ref[...]
        r = _rstd(xv, d)
        xn_ref[...] = (xv * r * g_ref[...]).astype(BF16)
        dx, dgr = _rms_vjp(xv, r, g_ref[...], p_ref[0].astype(F32) + p_ref[1].astype(F32), d)
        dx_ref[...] = dy_ref[...] + dx
        dgain_ref[...] += jnp.sum(dgr, axis=0, keepdims=True)

    tn = _row_tile(t, 256)
    nrow = pl.BlockSpec((tn, d), lambda i: (i, 0))
    vec = pl.BlockSpec((1, d), lambda i: (0, 0))
    dx, dgain, xn = pl.pallas_call(
        norm_body, name=name + "_norm", grid=(t // tn,),
        in_specs=[nrow, vec, pl.BlockSpec((2, tn, d), lambda i: (0, i, 0)), nrow],
        out_specs=[nrow, vec, nrow],
        out_shape=[jax.ShapeDtypeStruct((t, d), F32), jax.ShapeDtypeStruct((1, d), F32),
                   jax.ShapeDtypeStruct((t, d), BF16)],
        compiler_params=_cparams())(x, g, parts, dy)
    return [dx, dgain, xn, dgt, dup, act] + got


def _sgu_layernorm(vpre, lg, lb):
    v = _gelu(vpre)
    mu = jnp.mean(v, axis=-1, keepdims=True)
    xc = v - mu
    rstd = lax.rsqrt(jnp.mean(xc * xc, axis=-1, keepdims=True) + EPS)
    xhat = xc * rstd
    return xhat, rstd, xhat * lg + lb


def _sgu_fwd(zuv, lg, lb, wt, bias_l, name):
    t = zuv.shape[0]
    tm = _row_tile(t, 512)

    def body(u_ref, v_ref, lg_ref, lb_ref, wt_ref, bl_ref, o_ref, vln_scr):
        _, _, vln = _sgu_layernorm(v_ref[...], lg_ref[...], lb_ref[...])
        vln_scr[...] = vln.astype(BF16)
        lo = lax.broadcasted_iota(jnp.int32, (CHUNK, LANES), 1) < 64
        for c in range(tm // CHUNK):
            rows = slice(c * CHUNK, (c + 1) * CHUNK)
            for p in range(SG_GROUPS // 2):
                cols = slice(p * LANES, (p + 1) * LANES)
                vp = vln_scr[rows, cols]
                mixed = jnp.where(lo, _dot(wt_ref[2 * p], vp), _dot(wt_ref[2 * p + 1], vp)) + bl_ref[:, cols]
                o_ref[rows, cols] = (_gelu(u_ref[rows, cols]) * mixed).astype(BF16)

    half = lambda k: pl.BlockSpec((tm, SG_WIDTH), lambda i: (i, k))
    vec = pl.BlockSpec((1, SG_WIDTH), lambda i: (0, 0))
    return pl.pallas_call(
        body, name=name, grid=(t // tm,),
        in_specs=[half(0), half(1), vec, vec,
                  pl.BlockSpec((SG_GROUPS, CHUNK, CHUNK), lambda i: (0, 0, 0)),
                  pl.BlockSpec((CHUNK, SG_WIDTH), lambda i: (0, 0))],
        out_specs=pl.BlockSpec((tm, SG_WIDTH), lambda i: (i, 0)),
        out_shape=jax.ShapeDtypeStruct((t, SG_WIDTH), BF16),
        scratch_shapes=[pltpu.VMEM((tm, SG_WIDTH), BF16)],
        compiler_params=_cparams())(zuv, zuv, lg, lb, wt, bias_l)


def _sgu_bwd(zuv, dya, lg, lb, wt, wt_t, bias_l, name):
    t = zuv.shape[0]
    tm = _row_tile(t, 256)
    nsteps = t // tm

    def body(u_ref, v_ref, dy_ref, lg_ref, lb_ref, wt_ref, wtt_ref, bl_ref,
             dz_ref, dwt_ref, dbl_ref, dlg_ref, dlb_ref, vln_scr, dvln_scr, dbacc_scr):
        step = pl.program_id(0)

        @pl.when(step == 0)
        def _():
            dwt_ref[...] = jnp.zeros_like(dwt_ref)
            dlg_ref[...] = jnp.zeros_like(dlg_ref)
            dlb_ref[...] = jnp.zeros_like(dlb_ref)
            dbl_ref[...] = jnp.zeros_like(dbl_ref)
            dbacc_scr[...] = jnp.zeros_like(dbacc_scr)

        vpre = v_ref[...]
        lgv = lg_ref[...]
        xhat, rstd, vln = _sgu_layernorm(vpre, lgv, lb_ref[...])
        vln_scr[...] = vln.astype(BF16)
        lo = lax.broadcasted_iota(jnp.int32, (CHUNK, LANES), 1) < 64
        for c in range(tm // CHUNK):
            rows = slice(c * CHUNK, (c + 1) * CHUNK)
            for p in range(SG_GROUPS // 2):
                cols = slice(p * LANES, (p + 1) * LANES)
                vp = vln_scr[rows, cols]
                mixed = jnp.where(lo, _dot(wt_ref[2 * p], vp), _dot(wt_ref[2 * p + 1], vp)) + bl_ref[:, cols]
                upre = u_ref[rows, cols]
                dyp = dy_ref[rows, cols]
                dz_ref[rows, cols] = (dyp * mixed * _gelu_grad(upre)).astype(BF16)
                dm = dyp * _gelu(upre)
                dbacc_scr[:, cols] += dm
                dlo = jnp.where(lo, dm, 0.0).astype(BF16)
                dhi = jnp.where(lo, 0.0, dm).astype(BF16)
                dvln_scr[rows, cols] = _dot(wtt_ref[2 * p], dlo) + _dot(wtt_ref[2 * p + 1], dhi)
                dwt_ref[2 * p] += _dot_nt(dlo, vp)
                dwt_ref[2 * p + 1] += _dot_nt(dhi, vp)
        dvln = dvln_scr[...]
        dlg_ref[...] += jnp.sum(dvln * xhat, axis=0, keepdims=True)
        dlb_ref[...] += jnp.sum(dvln, axis=0, keepdims=True)
        dxh = dvln * lgv
        dv = rstd * (dxh - jnp.mean(dxh, axis=-1, keepdims=True)
                     - xhat * jnp.mean(dxh * xhat, axis=-1, keepdims=True))
        dz_ref[:, SG_WIDTH:] = (dv * _gelu_grad(vpre)).astype(BF16)

        @pl.when(step == nsteps - 1)
        def _():
            rr = lax.broadcasted_iota(jnp.int32, (CHUNK, CHUNK), 0)
            cc = lax.broadcasted_iota(jnp.int32, (CHUNK, CHUNK), 1)
            tril = (cc <= rr).astype(F32)
            for gidx in range(SG_GROUPS):
                dwt_ref[gidx] = dwt_ref[gidx] * tril
            kk = lax.broadcasted_iota(jnp.int32, (SG_WIDTH, LANES), 0)
            gg = lax.broadcasted_iota(jnp.int32, (SG_WIDTH, LANES), 1)
            sel = ((kk // 64) == gg).astype(F32)
            dbl_ref[...] = jnp.dot(dbacc_scr[...], sel, preferred_element_type=F32,
                                   precision=lax.Precision.HIGHEST)

    half = lambda k: pl.BlockSpec((tm, SG_WIDTH), lambda i: (i, k))
    vec = pl.BlockSpec((1, SG_WIDTH), lambda i: (0, 0))
    wspec = pl.BlockSpec((SG_GROUPS, CHUNK, CHUNK), lambda i: (0, 0, 0))
    return pl.pallas_call(
        body, name=name, grid=(nsteps,),
        in_specs=[half(0), half(1), pl.BlockSpec((tm, SG_WIDTH), lambda i: (i, 0)), vec, vec,
                  wspec, wspec, pl.BlockSpec((CHUNK, SG_WIDTH), lambda i: (0, 0))],
        out_specs=[pl.BlockSpec((tm, 2 * SG_WIDTH), lambda i: (i, 0)), wspec,
                   pl.BlockSpec((CHUNK, LANES), lambda i: (0, 0)), vec, vec],
        out_shape=[jax.ShapeDtypeStruct((t, 2 * SG_WIDTH), BF16),
                   jax.ShapeDtypeStruct((SG_GROUPS, CHUNK, CHUNK), F32),
                   jax.ShapeDtypeStruct((CHUNK, LANES), F32),
                   jax.ShapeDtypeStruct((1, SG_WIDTH), F32), jax.ShapeDtypeStruct((1, SG_WIDTH), F32)],
        scratch_shapes=[pltpu.VMEM((tm, SG_WIDTH), BF16), pltpu.VMEM((tm, SG_WIDTH), F32),
                        pltpu.VMEM((CHUNK, SG_WIDTH), F32)],
        compiler_params=_cparams())(zuv, zuv, dya, lg, lb, wt, wt_t, bias_l)


def _rope(x, c, s1, s2):
    return x * c + pltpu.roll(x, LANES - 16, 1) * s1 + pltpu.roll(x, 16, 1) * s2


def _rope_t(dy, c, s1, s2):
    return dy * c + pltpu.roll(dy * s1, 16, 1) + pltpu.roll(dy * s2, LANES - 16, 1)


def _mla_prep_fwd(zcq, zckv, zkr, gcq, gckv, qg, kg, wuq, wuk, wuv, rc, rs1, rs2, name, ex=None):
    t = zcq.shape[0]
    tm = _row_tile(t, 256)
    hd = MLA_HEADS * LANES

    def body(zcq_ref, zckv_ref, zkr_ref, gcq_ref, gckv_ref, qg_ref, kg_ref, wuq_ref, wuk_ref, wuv_ref,
             c_ref, s1_ref, s2_ref, q_ref, k_ref, v_ref, cqn_ref, ckvn_ref):
        c, s1, s2 = c_ref[...], s1_ref[...], s2_ref[...]
        xq = zcq_ref[...]
        cqn = (xq * _rstd(xq, MLA_Q_RANK) * gcq_ref[...]).astype(BF16)
        cqn_ref[...] = cqn
        ql = _dot(cqn, wuq_ref[...])
        xk = zckv_ref[...]
        ckvn = (xk * _rstd(xk, MLA_KV_RANK) * gckv_ref[...]).astype(BF16)
        ckvn_ref[...] = ckvn
        kl = _dot(ckvn, wuk_ref[...])
        slot_lane = lax.broadcasted_iota(jnp.int32, (tm, hd), 1) % LANES
        v_ref[...] = jnp.where(slot_lane == V_ONES_LANE, 1.0, _dot(ckvn, wuv_ref[...])).astype(BF16)
        kr = zkr_ref[...]
        for h in range(MLA_HEADS):
            sl = slice(h * LANES, (h + 1) * LANES)
            qh = ql[:, sl]
            q_ref[:, sl] = (_rope(qh * _rstd(qh, MLA_QK) * qg_ref[...], c, s1, s2) * ATTN_SCALE2).astype(BF16)
            kh = kl[:, sl] + kr
            k_ref[:, sl] = _rope(kh * _rstd(kh, MLA_QK) * kg_ref[...], c, s1, s2).astype(BF16)

    row = lambda n: pl.BlockSpec((tm, n), lambda i: (i, 0))
    full = lambda a: pl.BlockSpec(a.shape, lambda i: (0, 0))
    return _call_with_exchange(
        ex, body, name, (t // tm,),
        [row(MLA_Q_RANK), row(MLA_KV_RANK), row(LANES), full(gcq), full(gckv), full(qg), full(kg),
         full(wuq), full(wuk), full(wuv), row(LANES), row(LANES), row(LANES)],
        [row(hd), row(hd), row(hd), row(MLA_Q_RANK), row(MLA_KV_RANK)],
        [jax.ShapeDtypeStruct((t, hd), BF16)] * 3
        + [jax.ShapeDtypeStruct((t, MLA_Q_RANK), BF16), jax.ShapeDtypeStruct((t, MLA_KV_RANK), BF16)],
        [], (zcq, zckv, zkr, gcq, gckv, qg, kg, wuq, wuk, wuv, rc, rs1, rs2))


def _mla_prep_bwd(zcq, zckv, zkr, gcq, gckv, qg, kg, wuq, wuk, wuv, rc, rs1, rs2, dq, dk, dv, name):
    t = zcq.shape[0]
    tm = _row_tile(t, 256)
    hd = MLA_HEADS * LANES

    def body(zcq_ref, zckv_ref, zkr_ref, gcq_ref, gckv_ref, qg_ref, kg_ref, wuq_ref, wuk_ref, wuv_ref,
             c_ref, s1_ref, s2_ref, dq_ref, dk_ref, dv_ref,
             dzcq_ref, dzckv_ref, dzkr_ref, dql_ref, dkl_ref, dgcq_ref, dgckv_ref, dqg_ref, dkg_ref):
        @pl.when(pl.program_id(0) == 0)
        def _():
            for ref in (dgcq_ref, dgckv_ref, dqg_ref, dkg_ref):
                ref[...] = jnp.zeros_like(ref)

        c, s1, s2 = c_ref[...], s1_ref[...], s2_ref[...]
        qgv, kgv = qg_ref[...], kg_ref[...]
        xq = zcq_ref[...]
        rq = _rstd(xq, MLA_Q_RANK)
        ql = _dot((xq * rq * gcq_ref[...]).astype(BF16), wuq_ref[...])
        xk = zckv_ref[...]
        rk = _rstd(xk, MLA_KV_RANK)
        kl = _dot((xk * rk * gckv_ref[...]).astype(BF16), wuk_ref[...])
        kr = zkr_ref[...]
        dqg_acc = jnp.zeros((tm, LANES), F32)
        dkg_acc = jnp.zeros((tm, LANES), F32)
        dkr = jnp.zeros((tm, LANES), F32)
        for h in range(MLA_HEADS):
            sl = slice(h * LANES, (h + 1) * LANES)
            qh = ql[:, sl]
            dqh, dgr = _rms_vjp(qh, _rstd(qh, MLA_QK), qgv, _rope_t(dq_ref[:, sl], c, s1, s2), MLA_QK)
            dql_ref[:, sl] = dqh.astype(BF16)
            dqg_acc += dgr
            kh = kl[:, sl] + kr
            dkh, dgr = _rms_vjp(kh, _rstd(kh, MLA_QK), kgv, _rope_t(dk_ref[:, sl], c, s1, s2), MLA_QK)
            dkl_ref[:, sl] = dkh.astype(BF16)
            dkg_acc += dgr
            dkr += dkh
        dqg_ref[...] += jnp.sum(dqg_acc, axis=0, keepdims=True)
        dkg_ref[...] += jnp.sum(dkg_acc, axis=0, keepdims=True)
        lane = lax.broadcasted_iota(jnp.int32, (tm, LANES), 1)
        dzkr_ref[...] = jnp.where((lane >= MLA_NOPE) & (lane < MLA_QK), dkr, 0.0).astype(BF16)
        dcqn = _dot_nt(dql_ref[...], wuq_ref[...])
        dx, dgr = _rms_vjp(xq, rq, gcq_ref[...], dcqn, MLA_Q_RANK)
        dzcq_ref[...] = dx.astype(BF16)
        dgcq_ref[...] += jnp.sum(dgr, axis=0, keepdims=True)
        dckvn = _dot_nt(dkl_ref[...], wuk_ref[...]) + _dot_nt(dv_ref[...].astype(BF16), wuv_ref[...])
        dx, dgr = _rms_vjp(xk, rk, gckv_ref[...], dckvn, MLA_KV_RANK)
        dzckv_ref[...] = dx.astype(BF16)
        dgckv_ref[...] += jnp.sum(dgr, axis=0, keepdims=True)

    row = lambda n: pl.BlockSpec((tm, n), lambda i: (i, 0))
    full = lambda a: pl.BlockSpec(a.shape, lambda i: (0, 0))
    vec = lambda n: pl.BlockSpec((1, n), lambda i: (0, 0))
    return pl.pallas_call(
        body, name=name, grid=(t // tm,),
        in_specs=[row(MLA_Q_RANK), row(MLA_KV_RANK), row(LANES), full(gcq), full(gckv), full(qg), full(kg),
                  full(wuq), full(wuk), full(wuv), row(LANES), row(LANES), row(LANES), row(hd), row(hd), row(hd)],
        out_specs=[row(MLA_Q_RANK), row(MLA_KV_RANK), row(LANES), row(hd), row(hd),
                   vec(MLA_Q_RANK), vec(MLA_KV_RANK), vec(LANES), vec(LANES)],
        out_shape=[jax.ShapeDtypeStruct((t, MLA_Q_RANK), BF16), jax.ShapeDtypeStruct((t, MLA_KV_RANK), BF16),
                   jax.ShapeDtypeStruct((t, LANES), BF16), jax.ShapeDtypeStruct((t, hd), BF16),
                   jax.ShapeDtypeStruct((t, hd), BF16), jax.ShapeDtypeStruct((1, MLA_Q_RANK), F32),
                   jax.ShapeDtypeStruct((1, MLA_KV_RANK), F32), jax.ShapeDtypeStruct((1, LANES), F32),
                   jax.ShapeDtypeStruct((1, LANES), F32)],
        compiler_params=_cparams(),
    )(zcq, zckv, zkr, gcq, gckv, qg, kg, wuq, wuk, wuv, rc, rs1, rs2, dq, dk, dv)


def _attn_tiles(t):
    tq = 512 if t >= 2048 else 128
    return tq, min(t, 4 * tq), min(t, 4 * tq)


def _causal_keep(tq, nk, i, j, tk):
    row = lax.broadcasted_iota(jnp.int32, (tq, nk), 0)
    col = lax.broadcasted_iota(jnp.int32, (tq, nk), 1)
    return (col - row) <= (i * tq - j * tk)


def _causal_keep_t(tq, nk, i, j, tk):
    key = lax.broadcasted_iota(jnp.int32, (nk, tq), 0)
    qry = lax.broadcasted_iota(jnp.int32, (nk, tq), 1)
    return (key - qry) <= (i * tq - j * tk)


ATTN_FWD_HEADS_PER_STEP = 2
ATTN_BWD_HEADS_PER_STEP = 2


def _attn_fwd(q, k, v, name, ex=None):
    t, hd = q.shape
    hp = ATTN_FWD_HEADS_PER_STEP
    tq, tk, _ = _attn_tiles(t)
    pairs = [(i, j) for i in range(t // tq) for j in range(((i + 1) * tq - 1) // tk + 1)]
    ii = np.array([p[0] for p in pairs], np.int32)
    jj = np.array([p[1] for p in pairs], np.int32)

    def body(ii_ref, jj_ref, q_ref, k_ref, v_ref, o_ref, lse_ref, m_scr, acc_scr):
        s_id = pl.program_id(1)
        i, j = ii_ref[s_id], jj_ref[s_id]
        last = j == ((i + 1) * tq - 1) // tk
        ones_lane = lax.broadcasted_iota(jnp.int32, (tq, LANES), 1) == V_ONES_LANE

        @pl.when(j == 0)
        def _():
            m_scr[...] = jnp.full_like(m_scr, NEG)
            acc_scr[...] = jnp.zeros_like(acc_scr)

        def step(masked, nk):
            for hh in range(hp):
                sl = slice(hh * LANES, (hh + 1) * LANES)
                s = _dot_nt(q_ref[:, sl], k_ref[:nk, sl])
                if masked:
                    s = jnp.where(_causal_keep(tq, nk, i, j, tk), s, NEG)
                m_prev = m_scr[hh]
                m_new = jnp.maximum(m_prev, jnp.max(s, axis=1, keepdims=True))
                p = jnp.exp2(s - m_new)
                alpha = jnp.exp2(m_prev - m_new)
                acc = alpha * acc_scr[:, sl] + _dot(p.astype(BF16), v_ref[:nk, sl])
                if masked:
                    l_new = jnp.sum(jnp.where(ones_lane, acc, 0.0), axis=1, keepdims=True)
                    o_ref[:, sl] = (acc / l_new).astype(BF16)
                    lse_ref[:, sl] = jnp.broadcast_to(m_new + jnp.log(l_new) * LOG2E, (tq, LANES))
                else:
                    acc_scr[:, sl] = acc
                    m_scr[hh] = m_new

        @pl.when(jnp.logical_not(last))
        def _():
            step(False, tk)

        r = (((i + 1) * tq - 1) % tk) // tq
        for rr in range(tk // tq):
            @pl.when(last & (r == rr))
            def _():
                step(True, (rr + 1) * tq)

    w = hp * LANES
    qspec = pl.BlockSpec((tq, w), lambda h, s, ii_r, jj_r: (ii_r[s], h))
    kspec = pl.BlockSpec((tk, w), lambda h, s, ii_r, jj_r: (jj_r[s], h))
    return _call_with_exchange(
        ex, body, name, (hd // w, len(pairs)), [qspec, kspec, kspec], [qspec, qspec],
        [jax.ShapeDtypeStruct((t, hd), BF16), jax.ShapeDtypeStruct((t, hd), F32)],
        [pltpu.VMEM((hp, tq, 1), F32), pltpu.VMEM((tq, w), F32)], (q, k, v),
        prefetch=(jnp.asarray(ii), jnp.asarray(jj)))


def _attn_bwd_rows(o, lse, do, name):
    t, hd = o.shape
    heads = hd // LANES
    tm = _row_tile(t, 512)

    def body(o_ref, lse_ref, do_ref, out_ref):
        lane = lax.broadcasted_iota(jnp.int32, (tm, LANES), 1)
        acc = jnp.zeros((tm, LANES), F32)
        for h in range(heads):
            sl = slice(h * LANES, (h + 1) * LANES)
            delta = jnp.sum(do_ref[:, sl].astype(F32) * o_ref[:, sl].astype(F32), axis=1, keepdims=True)
            acc = jnp.where(lane == h, delta, acc)
            acc = jnp.where(lane == heads + h, lse_ref[:, sl], acc)
        out_ref[...] = acc

    row = pl.BlockSpec((tm, hd), lambda i: (i, 0))
    cols = pl.pallas_call(
        body, name=name, grid=(t // tm,), in_specs=[row, row, row],
        out_specs=pl.BlockSpec((tm, LANES), lambda i: (i, 0)),
        out_shape=jax.ShapeDtypeStruct((t, LANES), F32), compiler_params=_cparams())(o, lse, do)
    rows = cols.T
    return rows[:heads].reshape(heads, 1, t), rows[heads:2 * heads].reshape(heads, 1, t)


def _attn_bwd(q, k, v, delta_rows, lse_rows, do, name):
    t, hd = q.shape
    hp = ATTN_BWD_HEADS_PER_STEP
    tq, _, tk = _attn_tiles(t)
    nq = t // tq
    pairs = [(i, j) for j in range(t // tk) for i in range((j * tk) // tq, nq)]
    ii = np.array([p[0] for p in pairs], np.int32)
    jj = np.array([p[1] for p in pairs], np.int32)

    def body(jj_ref, ii_ref, q_ref, k_ref, v_ref, delta_ref, lse_ref, do_ref, dq_ref, dk_ref, dv_ref,
             dk_scr, dv_scr):
        s_id = pl.program_id(1)
        i, j = ii_ref[s_id], jj_ref[s_id]

        @pl.when(s_id == 0)
        def _():
            dq_ref[...] = jnp.zeros_like(dq_ref)

        @pl.when(i == (j * tk) // tq)
        def _():
            dk_scr[...] = jnp.zeros_like(dk_scr)
            dv_scr[...] = jnp.zeros_like(dv_scr)

        rows = pl.ds(pl.multiple_of(i * tq, tq), tq)

        def step(masked, nk):
            for hh in range(hp):
                sl = slice(hh * LANES, (hh + 1) * LANES)
                qv, kv, dov = q_ref[:, sl], k_ref[:nk, sl], do_ref[:, sl]
                st = _dot_nt(kv, qv)
                if masked:
                    st = jnp.where(_causal_keep_t(tq, nk, i, j, tk), st, NEG)
                pt = jnp.exp2(st - lse_ref[hh])
                dv_scr[:nk, sl] += _dot(pt.astype(BF16), dov)
                dpt = _dot_nt(v_ref[:nk, sl], dov)
                dst = (pt * (dpt - delta_ref[hh]) * ATTN_SCALE).astype(BF16)
                dk_scr[:nk, sl] += _dot(dst, qv)
                dq_ref[rows, sl] += _dot_tn(dst, kv)

        seen = jnp.minimum((i + 1) * tq - j * tk, tk)
        for nk in range(tq, tk + 1, tq):
            @pl.when((seen == nk) & ((i + 1) * tq - j * tk <= tk))
            def _():
                step(True, nk)

        @pl.when((i + 1) * tq - j * tk > tk)
        def _():
            step(False, tk)

        @pl.when(i == nq - 1)
        def _():
            dk_ref[...] = dk_scr[...] * (1.0 / ATTN_SCALE2)
            dv_ref[...] = dv_scr[...]

    w = hp * LANES
    qspec = pl.BlockSpec((tq, w), lambda h, s, jj_r, ii_r: (ii_r[s], h))
    kspec = pl.BlockSpec((tk, w), lambda h, s, jj_r, ii_r: (jj_r[s], h))
    rspec = pl.BlockSpec((hp, 1, tq), lambda h, s, jj_r, ii_r: (h, 0, ii_r[s]))
    return pl.pallas_call(
        body, name=name,
        grid_spec=pltpu.PrefetchScalarGridSpec(
            num_scalar_prefetch=2, grid=(hd // w, len(pairs)),
            in_specs=[qspec, kspec, kspec, rspec, rspec, qspec],
            out_specs=[pl.BlockSpec((t, w), lambda h, s, jj_r, ii_r: (0, h)), kspec, kspec],
            scratch_shapes=[pltpu.VMEM((tk, w), F32), pltpu.VMEM((tk, w), F32)]),
        out_shape=[jax.ShapeDtypeStruct((t, hd), F32)] * 3,
        compiler_params=_cparams())(jnp.asarray(jj), jnp.asarray(ii), q, k, v, delta_rows, lse_rows, do)


MEM_W = MEM_HEADS * LANES


def _mem_kv_fwd(mem, gmem, wkv, kg, name):
    m, d = mem.shape

    def body(mem_ref, g_ref, w_ref, kg_ref, k_ref, v_ref, mn_ref):
        xv = mem_ref[...]
        mn = (xv * _rstd(xv, d) * g_ref[...]).astype(BF16)
        mn_ref[...] = mn
        kvm = _dot(mn, w_ref[...])
        v_ref[...] = kvm[:, MEM_W:].astype(BF16)
        for h in range(MEM_HEADS):
            sl = slice(h * LANES, (h + 1) * LANES)
            kh = kvm[:, sl]
            k_ref[:, sl] = (kh * _rstd(kh, LANES) * kg_ref[...]).astype(BF16)

    full = lambda a: pl.BlockSpec(a.shape, lambda i: (0, 0))
    return pl.pallas_call(
        body, name=name, grid=(1,), in_specs=[full(mem), full(gmem), full(wkv), full(kg)],
        out_specs=[pl.BlockSpec((m, MEM_W), lambda i: (0, 0)), pl.BlockSpec((m, MEM_W), lambda i: (0, 0)),
                   pl.BlockSpec((m, d), lambda i: (0, 0))],
        out_shape=[jax.ShapeDtypeStruct((m, MEM_W), BF16), jax.ShapeDtypeStruct((m, MEM_W), BF16),
                   jax.ShapeDtypeStruct((m, d), BF16)],
        compiler_params=_cparams())(mem, gmem, wkv, kg)


def _mem_softmax(qn, kh):
    s = _dot_nt(qn, kh) * (LANES ** -0.5)
    e = jnp.exp(s - jnp.max(s, axis=1, keepdims=True))
    return e / jnp.sum(e, axis=1, keepdims=True)


def _mem_attn_fwd(zqm, qg, km, vm, name):
    t = zqm.shape[0]
    tm = _row_tile(t, 512)

    def body(q_ref, qg_ref, k_ref, v_ref, o_ref):
        for h in range(MEM_HEADS):
            sl = slice(h * LANES, (h + 1) * LANES)
            qh = q_ref[:, sl]
            qn = (qh * _rstd(qh, LANES) * qg_ref[...]).astype(BF16)
            p = _mem_softmax(qn, k_ref[:, sl])
            o_ref[:, sl] = _dot(p.astype(BF16), v_ref[:, sl]).astype(BF16)

    row = pl.BlockSpec((tm, MEM_W), lambda i: (i, 0))
    full = lambda a: pl.BlockSpec(a.shape, lambda i: (0, 0))
    return pl.pallas_call(
        body, name=name, grid=(t // tm,), in_specs=[row, full(qg), full(km), full(vm)], out_specs=row,
        out_shape=jax.ShapeDtypeStruct((t, MEM_W), BF16), compiler_params=_cparams())(zqm, qg, km, vm)


def _mem_attn_bwd(zqm, dyc, qg, km, vm, name):
    t = zqm.shape[0]
    m = km.shape[0]
    tm = _row_tile(t, 256)

    def body(q_ref, dy_ref, qg_ref, k_ref, v_ref, dz_ref, dk_ref, dv_ref, dqg_ref):
        @pl.when(pl.program_id(0) == 0)
        def _():
            dk_ref[...] = jnp.zeros_like(dk_ref)
            dv_ref[...] = jnp.zeros_like(dv_ref)
            dqg_ref[...] = jnp.zeros_like(dqg_ref)

        qgv = qg_ref[...]
        dqg_acc = jnp.zeros((tm, LANES), F32)
        for h in range(MEM_HEADS):
            sl = slice(h * LANES, (h + 1) * LANES)
            qh = q_ref[:, sl]
            r = _rstd(qh, LANES)
            qn = (qh * r * qgv).astype(BF16)
            kh = k_ref[:, sl]
            p = _mem_softmax(qn, kh)
            dov = dy_ref[:, sl]
            dv_ref[:, sl] += _dot_tn(p.astype(BF16), dov)
            dp = _dot_nt(dov, v_ref[:, sl])
            ds = (p * (dp - jnp.sum(dp * p, axis=1, keepdims=True)) * (LANES ** -0.5)).astype(BF16)
            dk_ref[:, sl] += _dot_tn(ds, qn)
            dqh, dgr = _rms_vjp(qh, r, qgv, _dot(ds, kh), LANES)
            dz_ref[:, sl] = dqh.astype(BF16)
            dqg_acc += dgr
        dqg_ref[...] += jnp.sum(dqg_acc, axis=0, keepdims=True)

    row = pl.BlockSpec((tm, MEM_W), lambda i: (i, 0))
    full = lambda a: pl.BlockSpec(a.shape, lambda i: (0, 0))
    acc = pl.BlockSpec((m, MEM_W), lambda i: (0, 0))
    return pl.pallas_call(
        body, name=name, grid=(t // tm,), in_specs=[row, row, full(qg), full(km), full(vm)],
        out_specs=[row, acc, acc, pl.BlockSpec((1, LANES), lambda i: (0, 0))],
        out_shape=[jax.ShapeDtypeStruct((t, MEM_W), BF16), jax.ShapeDtypeStruct((m, MEM_W), F32),
                   jax.ShapeDtypeStruct((m, MEM_W), F32), jax.ShapeDtypeStruct((1, LANES), F32)],
        compiler_params=_cparams())(zqm, dyc, qg, km, vm)


def _mem_kv_bwd(mem, gmem, wkv, kg, dkn, dvm, name):
    m, d = mem.shape

    def body(mem_ref, g_ref, w_ref, kg_ref, dk_ref, dv_ref, dw_ref, dkg_ref, dg_ref, dkv_scr):
        xv = mem_ref[...]
        r = _rstd(xv, d)
        mn = (xv * r * g_ref[...]).astype(BF16)
        kvm = _dot(mn, w_ref[...])
        dkv_scr[:, MEM_W:] = dv_ref[...].astype(BF16)
        dkg_acc = jnp.zeros((m, LANES), F32)
        for h in range(MEM_HEADS):
            sl = slice(h * LANES, (h + 1) * LANES)
            kh = kvm[:, sl]
            dkh, dgr = _rms_vjp(kh, _rstd(kh, LANES), kg_ref[...], dk_ref[:, sl], LANES)
            dkv_scr[:, sl] = dkh.astype(BF16)
            dkg_acc += dgr
        dkg_ref[...] = jnp.sum(dkg_acc, axis=0, keepdims=True)
        dkv = dkv_scr[...]
        dw_ref[...] = _dot_tn(mn, dkv)
        dmn = _dot_nt(dkv, w_ref[...])
        dg_ref[...] = jnp.sum(dmn * xv * r, axis=0, keepdims=True)

    full = lambda a: pl.BlockSpec(a.shape, lambda i: (0, 0))
    return pl.pallas_call(
        body, name=name, grid=(1,),
        in_specs=[full(mem), full(gmem), full(wkv), full(kg), full(dkn), full(dvm)],
        out_specs=[pl.BlockSpec((d, 2 * MEM_W), lambda i: (0, 0)), pl.BlockSpec((1, LANES), lambda i: (0, 0)),
                   pl.BlockSpec((1, d), lambda i: (0, 0))],
        out_shape=[jax.ShapeDtypeStruct((d, 2 * MEM_W), F32), jax.ShapeDtypeStruct((1, LANES), F32),
                   jax.ShapeDtypeStruct((1, d), F32)],
        scratch_shapes=[pltpu.VMEM((m, 2 * MEM_W), BF16)],
        compiler_params=_cparams())(mem, gmem, wkv, kg, dkn, dvm)


def _merge_fwd(x1, ya, yb, yc, zg, bg, wa, wb, wc, wo, name):
    t, d = x1.shape
    tm = _row_tile(t, 256)

    def body(x_ref, ya_ref, yb_ref, yc_ref, zg_ref, bg_ref, wa_ref, wb_ref, wc_ref, wo_ref,
             x2_ref, mg_ref, pa_ref, pb_ref, pc_ref):
        merged = None
        for k, (y_ref, w_ref, p_ref) in enumerate(
                ((ya_ref, wa_ref, pa_ref), (yb_ref, wb_ref, pb_ref), (yc_ref, wc_ref, pc_ref))):
            sl = slice(k * d, (k + 1) * d)
            pr = _dot(y_ref[...], w_ref[...])
            p_ref[...] = pr.astype(BF16)
            term = jax.nn.sigmoid(zg_ref[:, sl] + bg_ref[:, sl]) * pr
            merged = term if merged is None else merged + term
        mb = merged.astype(BF16)
        mg_ref[...] = mb
        x2_ref[...] = x_ref[...] + _dot(mb, wo_ref[...])

    row = lambda n: pl.BlockSpec((tm, n), lambda i: (i, 0))
    full = lambda a: pl.BlockSpec(a.shape, lambda i: (0, 0))
    return pl.pallas_call(
        body, name=name, grid=(t // tm,),
        in_specs=[row(d), row(ya.shape[1]), row(yb.shape[1]), row(yc.shape[1]), row(3 * d), full(bg),
                  full(wa), full(wb), full(wc), full(wo)],
        out_specs=[row(d)] * 5,
        out_shape=[jax.ShapeDtypeStruct((t, d), F32)] + [jax.ShapeDtypeStruct((t, d), BF16)] * 4,
        compiler_params=_cparams())(x1, ya, yb, yc, zg, bg, wa, wb, wc, wo)


def _merge_bwd(dx2, pa, pb, pc, zg, bg, wa, wb, wc, wo, name, ex=None):
    t, d = dx2.shape
    tm = _row_tile(t, 256)

    def body(dx_ref, pa_ref, pb_ref, pc_ref, zg_ref, bg_ref, wa_ref, wb_ref, wc_ref, wo_ref,
             dpa_ref, dpb_ref, dpc_ref, dzg_ref, dbg_ref, dya_ref, dyb_ref, dyc_ref):
        @pl.when(pl.program_id(0) == 0)
        def _():
            dbg_ref[...] = jnp.zeros_like(dbg_ref)

        dm = _dot_nt(dx_ref[...].astype(BF16), wo_ref[...])
        for k, (p_ref, w_ref, dp_ref, dy_ref) in enumerate(
                ((pa_ref, wa_ref, dpa_ref, dya_ref), (pb_ref, wb_ref, dpb_ref, dyb_ref),
                 (pc_ref, wc_ref, dpc_ref, dyc_ref))):
            sl = slice(k * d, (k + 1) * d)
            gate = jax.nn.sigmoid(zg_ref[:, sl] + bg_ref[:, sl])
            dpr = (dm * gate).astype(BF16)
            dp_ref[...] = dpr
            dzg = dm * p_ref[...].astype(F32) * gate * (1.0 - gate)
            dzg_ref[:, sl] = dzg.astype(BF16)
            dbg_ref[:, sl] += jnp.sum(dzg, axis=0, keepdims=True)
            dy_ref[...] = _dot_nt(dpr, w_ref[...]).astype(dy_ref.dtype)

    row = lambda n: pl.BlockSpec((tm, n), lambda i: (i, 0))
    full = lambda a: pl.BlockSpec(a.shape, lambda i: (0, 0))
    na, nb, nc = wa.shape[0], wb.shape[0], wc.shape[0]
    return _call_with_exchange(
        ex, body, name, (t // tm,),
        [row(d), row(d), row(d), row(d), row(3 * d), full(bg), full(wa), full(wb), full(wc), full(wo)],
        [row(d), row(d), row(d), row(3 * d), pl.BlockSpec((1, 3 * d), lambda i: (0, 0)), row(na), row(nb), row(nc)],
        [jax.ShapeDtypeStruct((t, d), BF16)] * 3
        + [jax.ShapeDtypeStruct((t, 3 * d), BF16), jax.ShapeDtypeStruct((1, 3 * d), F32),
           jax.ShapeDtypeStruct((t, na), F32), jax.ShapeDtypeStruct((t, nb), BF16),
           jax.ShapeDtypeStruct((t, nc), BF16)],
        [], (dx2, pa, pb, pc, zg, bg, wa, wb, wc, wo))


def _adamw_math(w, g, m, v):
    bc1 = 1.0 - ADAM_B1 ** ADAM_STEP
    bc2 = 1.0 - ADAM_B2 ** ADAM_STEP
    nm = ADAM_B1 * m + (1.0 - ADAM_B1) * g
    nv = ADAM_B2 * v + (1.0 - ADAM_B2) * (g * g)
    delta = -ADAM_LR * ((nm / bc1) / (jnp.sqrt(nv / bc2) + ADAM_EPS) + ADAM_WD * w)
    return delta, nm, nv


def _div_tile(n, cap, mult):
    best = None
    for cand in range(mult, min(n, cap) + 1, mult):
        if n % cand == 0:
            best = cand
    assert best is not None, (n, cap, mult)
    return best


def _adamw(w, g, m, v, name):
    rows, cols = w.shape
    tr = rows if rows * cols <= 256 * 1024 else _div_tile(rows, 256, 8)

    def body(w_ref, g_ref, m_ref, v_ref, d_ref, nm_ref, nv_ref):
        d_ref[...], nm_ref[...], nv_ref[...] = _adamw_math(w_ref[...], g_ref[...], m_ref[...], v_ref[...])

    blk = pl.BlockSpec((tr, cols), lambda i: (i, 0))
    return pl.pallas_call(
        body, name=name, grid=(rows // tr,), in_specs=[blk] * 4, out_specs=[blk] * 3,
        out_shape=[jax.ShapeDtypeStruct((rows, cols), F32)] * 3, compiler_params=_cparams())(w, g, m, v)


def _adamw_slots(w, slots, m, v, name):
    _, hr, cols = w.shape
    tr = _div_tile(hr, 128, 16)

    def body(w_ref, s_ref, m_ref, v_ref, g_ref, d_ref, nm_ref, nv_ref):
        g = s_ref[0, 0].astype(F32)
        for k in range(1, N_CHIPS):
            g = g + s_ref[0, k].astype(F32)
        g_ref[0] = g
        d_ref[0], nm_ref[0], nv_ref[0] = _adamw_math(w_ref[0], g, m_ref[0], v_ref[0])

    blk = pl.BlockSpec((1, tr, cols), lambda h, i: (h, i, 0))
    return pl.pallas_call(
        body, name=name, grid=(2, hr // tr),
        in_specs=[blk, pl.BlockSpec((1, N_CHIPS, tr, cols), lambda h, i: (h, 0, i, 0)), blk, blk],
        out_specs=[blk] * 4, out_shape=[jax.ShapeDtypeStruct((2, hr, cols), F32)] * 4,
        compiler_params=_cparams())(w, slots, m, v)


ANY = pl.BlockSpec(memory_space=pl.ANY)


def _place():
    x, y, c = lax.axis_index("x"), lax.axis_index("y"), lax.axis_index("c")
    other_chips = [(1 - x, y), (x, 1 - y), (1 - x, 1 - y)]
    return x, y, c, other_chips


def _remote(src, dst, send_sem, recv_sem, to):
    return pltpu.make_async_remote_copy(src_ref=src, dst_ref=dst, send_sem=send_sem, recv_sem=recv_sem,
                                        device_id=to, device_id_type=MESH)


PIECE_BYTES = 384 * 1024


def _row_pieces(half_rows, cols):
    for n in (4, 2):
        if half_rows % (16 * n) == 0 and half_rows * cols * 2 // n >= PIECE_BYTES:
            return [pl.ds(k * (half_rows // n), half_rows // n) for k in range(n)]
    return [pl.ds(0, half_rows)]


def _pieces(arrays, rows_axis):
    return [(w, rows) for w, a in enumerate(arrays) for rows in _row_pieces(a.shape[rows_axis], a.shape[-1])]


def _gather_exchange(shards):
    nw = len(shards)
    pieces = _pieces(shards, 1)
    npc = len(pieces)

    def build(s_refs, g_refs, sems):
        send_sems, recv_sems, local_sems = sems
        x, y, c, chips = _place()
        me = 2 * x + y
        sibling = (x, y, 1 - c)
        mine = [pltpu.make_async_copy(s_refs[w], g_refs[w].at[me], local_sems.at[w]) for w in range(nw)]
        first = [_remote(s_refs[w].at[c, rows], g_refs[w].at[me, c, rows], send_sems.at[k, p], recv_sems.at[k, p],
                         (cx, cy, c)) for k, (cx, cy) in enumerate(chips) for p, (w, rows) in enumerate(pieces)]

        def start():
            for cp in mine + first:
                cp.start()

        arrived = [g_refs[w].at[2 * cx + cy, c, rows] for cx, cy in chips for w, rows in pieces]
        passed = [_remote(slab, slab, send_sems.at[3 + q // npc, q % npc], recv_sems.at[3 + q // npc, q % npc], sibling)
                  for q, slab in enumerate(arrived)]

        def pass_on():
            for q, slab in enumerate(arrived):
                k, p = q // npc, q % npc
                _remote(slab, slab, send_sems.at[k, p], recv_sems.at[k, p], (*chips[k], c)).wait_recv()
                passed[q].start()

        def finish():
            for k, (cx, cy) in enumerate(chips):
                for p, (w, rows) in enumerate(pieces):
                    slab = g_refs[w].at[2 * cx + cy, 1 - c, rows]
                    _remote(slab, slab, send_sems.at[3 + k, p], recv_sems.at[3 + k, p], sibling).wait_recv()
            for cp in first + passed:
                cp.wait_send()
            for cp in mine:
                cp.wait()

        return start, pass_on, finish

    return _Exchange(list(shards), [jax.ShapeDtypeStruct((N_CHIPS,) + s.shape, BF16) for s in shards],
                     [pltpu.SemaphoreType.DMA((6, npc)), pltpu.SemaphoreType.DMA((6, npc)),
                      pltpu.SemaphoreType.DMA((nw,))], build)


def _swap_halves(grads, name):
    nw = len(grads)

    def body(*refs):
        g_refs, sib_refs = refs[:nw], refs[nw:2 * nw]
        send_sems, recv_sems = refs[2 * nw:]
        x, y, c, _ = _place()
        copies = [_remote(g_refs[w].at[s, 1 - c], sib_refs[w].at[s], send_sems.at[s, w], recv_sems.at[s, w],
                          (x, y, 1 - c)) for w in range(nw) for s in range(N_CHIPS)]
        for cp in copies:
            cp.start()
        for cp in copies:
            cp.wait_recv()
        for cp in copies:
            cp.wait_send()

    return pl.pallas_call(
        body, name=name, in_specs=[ANY] * nw, out_specs=[ANY] * nw,
        out_shape=[jax.ShapeDtypeStruct((N_CHIPS,) + g.shape[2:], BF16) for g in grads],
        scratch_shapes=[pltpu.SemaphoreType.DMA((N_CHIPS, nw)), pltpu.SemaphoreType.DMA((N_CHIPS, nw))],
    )(*grads)


def _pair_sum(grad, sib, core, name):
    nchip, _, hr, cols = grad.shape
    tr = _div_tile(hr, 256, 16)

    def body(core_ref, a_ref, b_ref, o_ref):
        o_ref[...] = (a_ref[0].astype(F32) + b_ref[...].astype(F32)).astype(BF16)

    return pl.pallas_call(
        body, name=name,
        grid_spec=pltpu.PrefetchScalarGridSpec(
            num_scalar_prefetch=1, grid=(nchip, hr // tr),
            in_specs=[pl.BlockSpec((1, 1, tr, cols), lambda s, i, core_r: (s, core_r[0], i, 0)),
                      pl.BlockSpec((1, tr, cols), lambda s, i, core_r: (s, i, 0))],
            out_specs=pl.BlockSpec((1, tr, cols), lambda s, i, core_r: (s, i, 0))),
        out_shape=jax.ShapeDtypeStruct((nchip, hr, cols), BF16), compiler_params=_cparams())(core, grad, sib)


def _pair_sum_exchange(sums):
    nw = len(sums)
    pieces = _pieces(sums, 1)
    npc = len(pieces)

    def build(p_refs, o_refs, sems):
        send_sems, recv_sems, local_sems = sems
        x, y, c, chips = _place()
        me = 2 * x + y
        sibling = (x, y, 1 - c)
        mine = [pltpu.make_async_copy(p_refs[w].at[me], o_refs[w].at[c, 3], local_sems.at[w]) for w in range(nw)]
        first = [_remote(p_refs[w].at[2 * cx + cy, rows], o_refs[w].at[c, k, rows], send_sems.at[k, p],
                         recv_sems.at[k, p], (cx, cy, c))
                 for k, (cx, cy) in enumerate(chips) for p, (w, rows) in enumerate(pieces)]

        def start():
            for cp in mine + first:
                cp.start()

        passed = [_remote(o_refs[w].at[c, k, rows], o_refs[w].at[c, k, rows], send_sems.at[3 + k, p],
                          recv_sems.at[3 + k, p], sibling) for k in range(N_CHIPS) for p, (w, rows) in enumerate(pieces)]

        def pass_on():
            for k in range(N_CHIPS):
                own_waited = set()
                for p, (w, rows) in enumerate(pieces):
                    if k < 3:
                        first[k * npc + p].wait_recv()
                    elif w not in own_waited:
                        mine[w].wait()
                        own_waited.add(w)
                    passed[k * npc + p].start()

        def finish():
            for k in range(N_CHIPS):
                for p, (w, rows) in enumerate(pieces):
                    slab = o_refs[w].at[1 - c, k, rows]
                    _remote(slab, slab, send_sems.at[3 + k, p], recv_sems.at[3 + k, p], sibling).wait_recv()
            for cp in first + passed:
                cp.wait_send()

        return start, pass_on, finish

    return _Exchange(list(sums), [jax.ShapeDtypeStruct((2,) + p.shape, BF16) for p in sums],
                     [pltpu.SemaphoreType.DMA((7, npc)), pltpu.SemaphoreType.DMA((7, npc)),
                      pltpu.SemaphoreType.DMA((nw,))], build)


def _small_sum_exchange(vec):
    m_per, n = vec.shape

    def build(ins, outs, scr):
        (x_ref,), (out_ref,) = ins, outs
        gath_ref, sum_ref, send_sems, recv_sems, local_sem, out_sem = scr
        x, y, c, chips = _place()
        me, sibling = (x, y, c), (x, y, 1 - c)

        def rows(px, py, pc):
            return gath_ref.at[pl.ds((4 * px + 2 * py + pc) * m_per, m_per), :]

        def copy(k, block, to, src=None):
            return pltpu.make_async_remote_copy(
                src_ref=rows(*block) if src is None else src, dst_ref=rows(*block),
                send_sem=send_sems.at[k], recv_sem=recv_sems.at[k], device_id=to, device_id_type=MESH)

        mine = pltpu.make_async_copy(x_ref, rows(*me), local_sem)
        first = [copy(0, me, sibling, src=x_ref)] + [copy(1 + j, me, (*chip, c), src=x_ref)
                                                     for j, chip in enumerate(chips)]

        def start():
            for cp in [mine] + first:
                cp.start()

        passed = [copy(4 + j, (*chip, c), sibling) for j, chip in enumerate(chips)]

        def pass_on():
            for j, chip in enumerate(chips):
                copy(1 + j, (*chip, c), me).wait_recv()
                passed[j].start()

        def finish():
            copy(0, sibling, me).wait_recv()
            for j, chip in enumerate(chips):
                copy(4 + j, (*chip, 1 - c), me).wait_recv()
            for cp in first + passed:
                cp.wait_send()
            mine.wait()
            acc = gath_ref[pl.ds(0, m_per), :]
            for k in range(1, N_DEV):
                acc = acc + gath_ref[pl.ds(k * m_per, m_per), :]
            sum_ref[...] = acc
            done = pltpu.make_async_copy(sum_ref, out_ref, out_sem)
            done.start()
            done.wait()

        return start, pass_on, finish

    return _Exchange([vec], [jax.ShapeDtypeStruct((m_per, n), F32)],
                     [pltpu.VMEM((N_DEV * m_per, n), F32), pltpu.VMEM((m_per, n), F32), pltpu.SemaphoreType.DMA((7,)),
                      pltpu.SemaphoreType.DMA((7,)), pltpu.SemaphoreType.DMA, pltpu.SemaphoreType.DMA], build)


def _pack_small(vals, tail=()):
    flat = jnp.concatenate([vals[name].reshape(-1).astype(F32) for name, _ in SMALL] + [v.reshape(1) for v in tail])
    flat = jnp.pad(flat, (0, SMALL_ROWS * LANES - flat.shape[0]))
    return flat.reshape(SMALL_ROWS, LANES)


def _unpack_small(packed):
    flat = packed.reshape(-1)
    out, off = {}, 0
    for name, shape in SMALL:
        n = int(np.prod(shape))
        out[name] = flat[off:off + n].reshape(shape)
        off += n
    return out


def _head_pad_cols(w, heads, real):
    k = w.shape[0]
    return jnp.pad(w.reshape(k, heads, real), ((0, 0), (0, 0), (0, LANES - real))).reshape(k, heads * LANES)


def _rope_tables(positions):
    half = MLA_ROPE // 2
    inv = ROPE_BASE ** (-jnp.arange(half, dtype=F32) / half)
    ang = positions.astype(F32)[:, None] * inv
    cos, sin = jnp.cos(ang), jnp.sin(ang)
    t = positions.shape[0]
    z = lambda n: jnp.zeros((t, n), F32)
    rc = jnp.concatenate([jnp.ones((t, MLA_NOPE), F32), cos, cos, z(LANES - MLA_QK)], axis=1)
    rs1 = jnp.concatenate([z(MLA_NOPE), -sin, z(LANES - MLA_NOPE - half)], axis=1)
    rs2 = jnp.concatenate([z(MLA_NOPE + half), sin, z(LANES - MLA_QK)], axis=1)
    return rc, rs1, rs2


FFN1_WEIGHTS = ("ffn1_w_gu", "ffn1_w_down")
FFN2_WEIGHTS = ("ffn2_w_gu", "ffn2_w_down")
MIXER_WEIGHTS = tuple(n for n, *_ in SHARDED if n not in FFN1_WEIGHTS + FFN2_WEIGHTS)
SHARD_SHAPE = {n: (r, c, kind) for n, r, c, kind in SHARDED}


def _from_blocks(name, gathered):
    r, c, kind = SHARD_SHAPE[name]
    blk = gathered.reshape(N_CHIPS, r, c)
    return blk, (blk.transpose(1, 0, 2).reshape(r, N_CHIPS * c) if kind == "col" else blk.reshape(N_CHIPS * r, c))


def _grad_pair_sums(names, gw, core, tag):
    by_owner = []
    for name in names:
        r, c, kind = SHARD_SHAPE[name]
        if gw[name].dtype == BF16:
            blk = gw[name]
        elif kind == "col":
            blk = gw[name].reshape(r, N_CHIPS, c).transpose(1, 0, 2)
        else:
            blk = gw[name].reshape(N_CHIPS, r, c)
        by_owner.append(blk.astype(BF16).reshape(N_CHIPS, 2, r // 2, c))
    received = _swap_halves(by_owner, "grad_swap_" + tag)
    return [_pair_sum(g, s, core, "pair_sum_" + n) for g, s, n in zip(by_owner, received, names)]


def _device_step(x, mem, positions, tgt, small, shards, core):
    d = D_MODEL
    g_ffn1, g_mix, g_ffn2 = small["ffn1_norm"], small["mix_norm"], small["ffn2_norm"]
    big = {}
    for name, g in zip(FFN1_WEIGHTS, _run_exchange(_gather_exchange([shards[n] for n in FFN1_WEIGHTS]), "gather_ffn1")):
        big[name + "#blocks"], big[name] = _from_blocks(name, g)
    wgu1, wd1 = big["ffn1_w_gu#blocks"], big["ffn1_w_down"].reshape(2, FF_TILE, d)
    x1, gpre1, upre1, h, *rest = _ffn_fwd(x, g_ffn1, wgu1, wd1, "ffn1_fwd", next_gain=g_mix,
                                          ex=_gather_exchange([shards[n] for n in MIXER_WEIGHTS]))
    for name, g in zip(MIXER_WEIGHTS, rest):
        big[name + "#blocks"], big[name] = _from_blocks(name, g)
    w_in = big["w_in"]
    w_uv_, w_cq, w_ckv = w_in[:, :COL_CQ], w_in[:, COL_CQ:COL_CKV], w_in[:, COL_CKV:COL_KR]
    w_kr = jnp.pad(w_in[:, COL_KR:COL_QM], ((0, 0), (MLA_NOPE, LANES - MLA_QK)))
    w_qm, w_g = w_in[:, COL_QM:COL_GATE], w_in[:, COL_GATE:]
    segs = (w_uv_, w_cq, w_ckv, w_kr, w_qm, w_g)
    wuq = _head_pad_cols(big["mla_w_uq"], MLA_HEADS, MLA_QK)
    ukv = big["mla_w_ukv"].reshape(MLA_KV_RANK, MLA_HEADS, 2, MLA_NOPE)
    wuk = _head_pad_cols(ukv[:, :, 0].reshape(MLA_KV_RANK, -1), MLA_HEADS, MLA_NOPE)
    wuv = _head_pad_cols(ukv[:, :, 1].reshape(MLA_KV_RANK, -1), MLA_HEADS, MLA_NOPE)
    wkv = big["mem_w_kv"]
    wa, wc, wo = big["w_branch_a"], big["w_branch_c"], big["w_out"]
    wb = jnp.pad(big["w_branch_b"].reshape(MLA_HEADS, MLA_NOPE, d),
                 ((0, 0), (0, LANES - MLA_NOPE), (0, 0))).reshape(MLA_HEADS * LANES, d)
    qg = jnp.pad(small["mla_q_norm"], ((0, 0), (0, LANES - MLA_QK)))
    kg = jnp.pad(small["mla_k_norm"], ((0, 0), (0, LANES - MLA_QK)))
    causal = jnp.tril(jnp.ones((CHUNK, CHUNK), bool))
    wt_f = jnp.where(causal[None], small["sg_w"][0], 0.0)
    wt, wt_t = wt_f.astype(BF16), wt_f.transpose(0, 2, 1).astype(BF16)
    bias_l = jnp.repeat(small["sg_b"][0].T, 64, axis=1)
    rc, rs1, rs2 = _rope_tables(positions)

    zuv, zcq, zckv, zkr, zqm, zg = _mm_cols(h, segs, [F32] * 5 + [BF16], "in_proj")
    ya = _sgu_fwd(zuv, small["sg_ln_g"], small["sg_ln_b"], wt, bias_l, "sgu_fwd")
    q, k, v, cqn, ckvn = _mla_prep_fwd(zcq, zckv, zkr, small["mla_cq_norm"], small["mla_ckv_norm"], qg, kg,
                                       wuq, wuk, wuv, rc, rs1, rs2, "mla_prep_fwd")
    yb, lse, *rest = _attn_fwd(q, k, v, "mla_attn_fwd", ex=_gather_exchange([shards[n] for n in FFN2_WEIGHTS]))
    for name, g in zip(FFN2_WEIGHTS, rest):
        big[name + "#blocks"], big[name] = _from_blocks(name, g)
    wgu2, wd2 = big["ffn2_w_gu#blocks"], big["ffn2_w_down"].reshape(2, FF_TILE, d)
    km, vm, memn = _mem_kv_fwd(mem, small["mem_norm"], wkv, small["mem_k_norm"], "mem_kv_fwd")
    yc = _mem_attn_fwd(zqm, small["mem_q_norm"], km, vm, "mem_attn_fwd")
    x2, merged, pa, pb, pc = _merge_fwd(x1, ya, yb, yc, zg, small["b_gate"], wa, wb, wc, wo, "merge_fwd")
    dy, loss_row, gpre2, upre2 = _ffn_fwd(x2, g_ffn2, wgu2, wd2, "ffn2_fwd", target=tgt)

    gw, gs, slots = {}, {}, {}

    def ffn_grads(prefix, xin, gain, dyin, gpre, upre, wgu, wd, ex=None, ex_names=(), last=False):
        dx, dgain, xn, dgt, dup, act, *got = _ffn_bwd(xin, gain, dyin, gpre, upre, wgu, wd, prefix + "_bwd", ex=ex)
        slots.update(zip(ex_names, got))
        gs[prefix + "_norm"] = dgain
        gw[prefix + "_w_gu"] = jnp.concatenate(
            [_mm_tn(xn, dgt, prefix + "_dwg", col_blocks=True, out_dtype=BF16),
             _mm_tn(xn, dup, prefix + "_dwu", col_blocks=True, out_dtype=BF16)], axis=0)
        rows_down = SHARD_SHAPE[prefix + "_w_down"][0]
        if last:
            small_sum = _small_sum_exchange(_pack_small(gs, tail=[loss_row[0, 0]]))
            dwd, summed = _mm_tn(act, dyin, prefix + "_dwd", scale=0.5, ex=small_sum, out_dtype=BF16)
            gw[prefix + "_w_down"] = dwd.reshape(N_CHIPS, rows_down, d)
            return dx, summed
        gw[prefix + "_w_down"] = _mm_tn(act, dyin, prefix + "_dwd", scale=0.5, out_dtype=BF16).reshape(
            N_CHIPS, rows_down, d)
        return dx

    dx2 = ffn_grads("ffn2", x2, g_ffn2, dy, gpre2, upre2, wgu2, wd2)
    ffn2_sums = _pair_sum_exchange(_grad_pair_sums(FFN2_WEIGHTS, gw, core, "ffn2"))
    dpa, dpb, dpc, dzg, dbg, dya, dyb, dyc, *got = _merge_bwd(dx2, pa, pb, pc, zg, small["b_gate"], wa, wb, wc, wo,
                                                              "merge_bwd", ex=ffn2_sums)
    slots.update(zip(FFN2_WEIGHTS, got))
    gs["b_gate"] = dbg
    gw["w_out"] = _mm_tn(merged, dx2, "dw_out")
    gw["w_branch_a"] = _mm_tn(ya, dpa, "dw_branch_a")
    gw["w_branch_b"] = _mm_tn(yb, dpb, "dw_branch_b").reshape(MLA_HEADS, LANES, d)[:, :MLA_NOPE].reshape(-1, d)
    gw["w_branch_c"] = _mm_tn(yc, dpc, "dw_branch_c")

    dzuv, dwt, dbl, dlg, dlb = _sgu_bwd(zuv, dya, small["sg_ln_g"], small["sg_ln_b"], wt, wt_t, bias_l, "sgu_bwd")
    gs["sg_w"], gs["sg_b"] = dwt[None], dbl[:, :SG_GROUPS].T[None]
    gs["sg_ln_g"], gs["sg_ln_b"] = dlg, dlb

    delta_rows, lse_rows = _attn_bwd_rows(yb, lse, dyb, "mla_attn_bwd_rows")
    dq, dk, dv = _attn_bwd(q, k, v, delta_rows, lse_rows, dyb, "mla_attn_bwd")
    dzcq, dzckv, dzkr, dql, dkl, dgcq, dgckv, dqg, dkg = _mla_prep_bwd(
        zcq, zckv, zkr, small["mla_cq_norm"], small["mla_ckv_norm"], qg, kg, wuq, wuk, wuv, rc, rs1, rs2,
        dq, dk, dv, "mla_prep_bwd")
    gs["mla_cq_norm"], gs["mla_ckv_norm"] = dgcq, dgckv
    gs["mla_q_norm"], gs["mla_k_norm"] = dqg[:, :MLA_QK], dkg[:, :MLA_QK]
    gw["mla_w_uq"] = _mm_tn(cqn, dql, "dw_uq").reshape(MLA_Q_RANK, MLA_HEADS, LANES)[:, :, :MLA_QK].reshape(
        MLA_Q_RANK, -1)
    dwuk = _mm_tn(ckvn, dkl, "dw_uk").reshape(MLA_KV_RANK, MLA_HEADS, LANES)[:, :, :MLA_NOPE]
    dwuv = _mm_tn(ckvn, dv, "dw_uv").reshape(MLA_KV_RANK, MLA_HEADS, LANES)[:, :, :MLA_NOPE]
    gw["mla_w_ukv"] = jnp.concatenate([dwuk, dwuv], axis=2).reshape(MLA_KV_RANK, -1)

    dzqm, dkn, dvm, dmqg = _mem_attn_bwd(zqm, dyc, small["mem_q_norm"], km, vm, "mem_attn_bwd")
    gs["mem_q_norm"] = dmqg
    gw["mem_w_kv"], gs["mem_k_norm"], gs["mem_norm"] = _mem_kv_bwd(
        mem, small["mem_norm"], wkv, small["mem_k_norm"], dkn, dvm, "mem_kv_bwd")

    dzs = (dzuv, dzcq, dzckv, dzkr, dzqm, dzg)
    dws = list(_mm_tn_cols(h, dzs[:5], "dw_in_narrow")) + [_mm_tn(h, dzg, "dw_in_gate")]
    dws[3] = dws[3][:, MLA_NOPE:MLA_QK]
    gw["w_in"] = jnp.concatenate(dws, axis=1)
    dx1, gs["mix_norm"] = _proj_norm_bwd(dzs, [w.T for w in segs], x1, g_mix, dx2, "in_proj_bwd")
    mixer_sums = _pair_sum_exchange(_grad_pair_sums(MIXER_WEIGHTS, gw, core, "mixer"))
    dx, summed = ffn_grads("ffn1", x, g_ffn1, dx1, gpre1, upre1, wgu1, wd1, ex=mixer_sums, ex_names=MIXER_WEIGHTS,
                           last=True)
    ffn1_sums = _pair_sum_exchange(_grad_pair_sums(FFN1_WEIGHTS, gw, core, "ffn1"))
    slots.update(zip(FFN1_WEIGHTS, _run_exchange(ffn1_sums, "grad_exchange_ffn1")))
    return dx, slots, summed


def kernel(x, mem, positions, ffn1_norm, ffn1_w_gu, ffn1_w_down, mix_norm, w_in, b_gate, sg_ln_g, sg_ln_b, sg_w, sg_b, mla_cq_norm, mla_w_uq, mla_ckv_norm, mla_w_ukv, mla_q_norm, mla_k_norm, mem_norm, mem_w_kv, mem_q_norm, mem_k_norm, w_branch_a, w_branch_b, w_branch_c, w_out, ffn2_norm, ffn2_w_gu, ffn2_w_down, loss_target, m_ffn1_norm, m_ffn1_w_gu, m_ffn1_w_down, m_mix_norm, m_w_in, m_b_gate, m_sg_ln_g, m_sg_ln_b, m_sg_w, m_sg_b, m_mla_cq_norm, m_mla_w_uq, m_mla_ckv_norm, m_mla_w_ukv, m_mla_q_norm, m_mla_k_norm, m_mem_norm, m_mem_w_kv, m_mem_q_norm, m_mem_k_norm, m_w_branch_a, m_w_branch_b, m_w_branch_c, m_w_out, m_ffn2_norm, m_ffn2_w_gu, m_ffn2_w_down, v_ffn1_norm, v_ffn1_w_gu, v_ffn1_w_down, v_mix_norm, v_w_in, v_b_gate, v_sg_ln_g, v_sg_ln_b, v_sg_w, v_sg_b, v_mla_cq_norm, v_mla_w_uq, v_mla_ckv_norm, v_mla_w_ukv, v_mla_q_norm, v_mla_k_norm, v_mem_norm, v_mem_w_kv, v_mem_q_norm, v_mem_k_norm, v_w_branch_a, v_w_branch_b, v_w_branch_c, v_w_out, v_ffn2_norm, v_ffn2_w_gu, v_ffn2_w_down):
    args = dict(locals())
    weights = {n: args[n] for n in WEIGHT_ORDER}
    mom_m = {n: args["m_" + n] for n in WEIGHT_ORDER}
    mom_v = {n: args["v_" + n] for n in WEIGHT_ORDER}
    small = {n: weights[n] for n, _ in SMALL}
    halves = lambda a, r, c: a.reshape(2, r // 2, c)

    shards = {n: halves(weights[n][0].astype(BF16), r, c) for n, r, c, _ in SHARDED}
    core = lax.axis_index("c").astype(jnp.int32).reshape(1)
    dx, slots, summed = _device_step(x[0], mem[0], positions[0], loss_target[0], small, shards, core)
    loss = summed.reshape(-1)[_N_SMALL]
    small_grads = _unpack_small(summed)

    grads, deltas, new_m, new_v = {}, {}, {}, {}
    for name, r, c, _ in SHARDED:
        outs = _adamw_slots(halves(weights[name][0], r, c), slots[name], halves(mom_m[name][0], r, c),
                            halves(mom_v[name][0], r, c), "adamw_" + name)
        shape = weights[name].shape
        grads[name], deltas[name], new_m[name], new_v[name] = [o.reshape(shape) for o in outs]
    dlt, nm, nv = _adamw(_pack_small(small), _pack_small(small_grads), _pack_small({n: mom_m[n] for n, _ in SMALL}),
                         _pack_small({n: mom_v[n] for n, _ in SMALL}), "adamw_small")
    for name, _ in SMALL:
        grads[name] = small_grads[name]
    deltas.update(_unpack_small(dlt))
    new_m.update(_unpack_small(nm))
    new_v.update(_unpack_small(nv))

    return (loss, dx[None], *[grads[n] for n in WEIGHT_ORDER], *[deltas[n] for n in WEIGHT_ORDER],
            *[new_m[n] for n in WEIGHT_ORDER], *[new_v[n] for n in WEIGHT_ORDER])
```

```python
import functools
from typing import Callable, NamedTuple

import numpy as np
import jax
import jax.numpy as jnp
from jax import lax
from jax.experimental import pallas as pl
from jax.experimental.pallas import tpu as pltpu

F32 = jnp.float32
BF16 = jnp.bfloat16

D_MODEL = 1024
D_FF = 2816
FF_TILE = 1408
SG_WIDTH = 512
SG_GROUPS = 8
CHUNK = 128
MLA_HEADS = 8
MLA_QK = 96
MLA_NOPE = 64
MLA_ROPE = 32
MLA_Q_RANK = 384
MLA_KV_RANK = 256
MEM_HEADS = 4
LANES = 128
EPS = 1e-6
NEG = -1e30
ROPE_BASE = 10000.0
N_CHIPS = 4
N_DEV = 8

ADAM_LR = 0.001
ADAM_B1 = 0.9
ADAM_B2 = 0.999
ADAM_EPS = 1e-08
ADAM_WD = 0.01
ADAM_STEP = 10

COL_CQ = 1024
COL_CKV = 1408
COL_KR = 1664
COL_QM = 1696
COL_GATE = 2208

VMEM_LIMIT_BYTES = 56 * 1024 * 1024
INV_SQRT2 = 0.7071067811865476
INV_SQRT_2PI = 0.3989422804014327
LOG2E = 1.4426950408889634
ATTN_SCALE = MLA_QK ** -0.5
V_ONES_LANE = 64
ATTN_SCALE2 = ATTN_SCALE * LOG2E

SHARDED = (
    ("ffn1_w_gu", 1024, 1408, "col"),
    ("ffn1_w_down", 704, 1024, "row"),
    ("w_in", 1024, 1320, "col"),
    ("mla_w_uq", 384, 192, "col"),
    ("mla_w_ukv", 256, 256, "col"),
    ("mem_w_kv", 256, 1024, "row"),
    ("w_branch_a", 512, 256, "col"),
    ("w_branch_b", 512, 256, "col"),
    ("w_branch_c", 512, 256, "col"),
    ("w_out", 256, 1024, "row"),
    ("ffn2_w_gu", 1024, 1408, "col"),
    ("ffn2_w_down", 704, 1024, "row"),
)
SMALL = (
    ("ffn1_norm", (1, 1024)), ("mix_norm", (1, 1024)), ("b_gate", (1, 3072)),
    ("sg_ln_g", (1, 512)), ("sg_ln_b", (1, 512)), ("sg_w", (1, 8, 128, 128)),
    ("sg_b", (1, 8, 128)), ("mla_cq_norm", (1, 384)), ("mla_ckv_norm", (1, 256)),
    ("mla_q_norm", (1, 96)), ("mla_k_norm", (1, 96)), ("mem_norm", (1, 1024)),
    ("mem_q_norm", (1, 128)), ("mem_k_norm", (1, 128)), ("ffn2_norm", (1, 1024)),
)
WEIGHT_ORDER = (
    "ffn1_norm", "ffn1_w_gu", "ffn1_w_down", "mix_norm", "w_in", "b_gate", "sg_ln_g", "sg_ln_b",
    "sg_w", "sg_b", "mla_cq_norm", "mla_w_uq", "mla_ckv_norm", "mla_w_ukv", "mla_q_norm",
    "mla_k_norm", "mem_norm", "mem_w_kv", "mem_q_norm", "mem_k_norm", "w_branch_a", "w_branch_b",
    "w_branch_c", "w_out", "ffn2_norm", "ffn2_w_gu", "ffn2_w_down",
)

_N_SMALL = sum(int(np.prod(s)) for _, s in SMALL)
SMALL_ROWS = -(-_N_SMALL // (LANES * 8)) * 8

MESH = pl.DeviceIdType.MESH


def _cparams():
    return pltpu.CompilerParams(vmem_limit_bytes=VMEM_LIMIT_BYTES)


def _dot(a, b):
    return jnp.dot(a, b, preferred_element_type=F32)


def _dot_nt(a, b):
    return lax.dot_general(a, b, (((1,), (1,)), ((), ())), preferred_element_type=F32)


def _dot_tn(a, b):
    return lax.dot_general(a, b, (((0,), (0,)), ((), ())), preferred_element_type=F32)


def _gelu(x):
    return 0.5 * x * (1.0 + lax.erf(x * INV_SQRT2))


def _gelu_grad(x):
    return 0.5 * (1.0 + lax.erf(x * INV_SQRT2)) + x * jnp.exp(-0.5 * x * x) * INV_SQRT_2PI


def _rstd(x, n):
    return lax.rsqrt(jnp.sum(x * x, axis=-1, keepdims=True) * (1.0 / n) + EPS)


def _rms_vjp(x, r, g, dy, n):
    dxh = dy * g
    dx = r * dxh - x * (r * r * r) * (jnp.sum(dxh * x, axis=-1, keepdims=True) * (1.0 / n))
    return dx, dy * x * r


def _row_tile(t, want):
    return min(t, want)


def _wide_tile(n):
    if n <= 1024:
        return n
    if n % 1024 == 0:
        return 1024
    assert n % FF_TILE == 0, n
    return FF_TILE


def _mm_cols(a, ws, out_dtypes, name, ex=None):
    t, kdim = a.shape
    tm = _row_tile(t, 512)
    n = len(ws)

    def body(*refs):
        av = refs[0][...]
        for w_ref, o_ref in zip(refs[1:1 + n], refs[1 + n:]):
            o_ref[...] = _dot(av, w_ref[...]).astype(o_ref.dtype)

    row = lambda width: pl.BlockSpec((tm, width), lambda i: (i, 0))
    return _call_with_exchange(
        ex, body, name, (t // tm,),
        [row(kdim)] + [pl.BlockSpec(w.shape, lambda i: (0, 0)) for w in ws],
        [row(w.shape[1]) for w in ws],
        [jax.ShapeDtypeStruct((t, w.shape[1]), dt) for w, dt in zip(ws, out_dtypes)], [], (a, *ws))


def _proj_norm_bwd(dzs, wts, x, g, dres, name):
    t, d = x.shape
    tm = _row_tile(t, 256)
    n = len(dzs)

    def body(*refs):
        x_ref, g_ref, r_ref, dx_ref, dg_ref = refs[2 * n:]

        @pl.when(pl.program_id(0) == 0)
        def _():
            dg_ref[...] = jnp.zeros_like(dg_ref)

        dh = None
        for dz_ref, w_ref in zip(refs[:n], refs[n:2 * n]):
            part = _dot(dz_ref[...], w_ref[...])
            dh = part if dh is None else dh + part
        xv = x_ref[...]
        dx, dgr = _rms_vjp(xv, _rstd(xv, d), g_ref[...], dh, d)
        dx_ref[...] = r_ref[...] + dx
        dg_ref[...] += jnp.sum(dgr, axis=0, keepdims=True)

    row = lambda width: pl.BlockSpec((tm, width), lambda i: (i, 0))
    vec = pl.BlockSpec((1, d), lambda i: (0, 0))
    return pl.pallas_call(
        body, name=name, grid=(t // tm,),
        in_specs=[row(dz.shape[1]) for dz in dzs] + [pl.BlockSpec(w.shape, lambda i: (0, 0)) for w in wts]
        + [row(d), vec, row(d)],
        out_specs=[row(d), vec],
        out_shape=[jax.ShapeDtypeStruct((t, d), F32), jax.ShapeDtypeStruct((1, d), F32)],
        compiler_params=_cparams())(*dzs, *wts, x, g, dres)


def _mm_tn_cols(a, bs, name):
    t, m = a.shape
    tk = _row_tile(t, 1024)
    n = len(bs)

    def body(*refs):
        @pl.when(pl.program_id(0) == 0)
        def _():
            for o_ref in refs[1 + n:]:
                o_ref[...] = jnp.zeros_like(o_ref)

        av = refs[0][...].astype(BF16)
        for b_ref, o_ref in zip(refs[1:1 + n], refs[1 + n:]):
            o_ref[...] += _dot_tn(av, b_ref[...].astype(BF16))

    row = lambda width: pl.BlockSpec((tk, width), lambda k: (k, 0))
    return pl.pallas_call(
        body, name=name, grid=(t // tk,), in_specs=[row(m)] + [row(b.shape[1]) for b in bs],
        out_specs=[pl.BlockSpec((m, b.shape[1]), lambda k: (0, 0)) for b in bs],
        out_shape=[jax.ShapeDtypeStruct((m, b.shape[1]), F32) for b in bs],
        compiler_params=_cparams())(a, *bs)


def _mm_tn(a, b, name, scale=1.0, ex=None, col_blocks=False, out_dtype=F32):
    t, m = a.shape
    n = b.shape[1]
    tm, tn = _wide_tile(m), _wide_tile(n)
    tk = _row_tile(t, 2048)
    nk = t // tk
    in_place = out_dtype == F32

    def body(a_ref, b_ref, o_ref, *scr):
        k = pl.program_id(2)
        acc_ref = o_ref if in_place else scr[0]

        @pl.when(k == 0)
        def _():
            acc_ref[...] = jnp.zeros_like(acc_ref)

        prod = _dot_tn(a_ref[...].astype(BF16), b_ref[...].astype(BF16))
        acc_ref[...] += prod.reshape(acc_ref.shape)
        if scale != 1.0 or not in_place:
            @pl.when(k == nk - 1)
            def _():
                o_ref[...] = (acc_ref[...] * scale).astype(out_dtype).reshape(o_ref.shape)

    if col_blocks:
        out_spec = pl.BlockSpec((1, tm, tn), lambda i, j, k: (j, i, 0))
        out_shape = jax.ShapeDtypeStruct((n // tn, m, tn), out_dtype)
    else:
        out_spec = pl.BlockSpec((tm, tn), lambda i, j, k: (i, j))
        out_shape = jax.ShapeDtypeStruct((m, n), out_dtype)
    outs = _call_with_exchange(
        ex, body, name, (m // tm, n // tn, nk),
        [pl.BlockSpec((tk, tm), lambda i, j, k: (k, i)), pl.BlockSpec((tk, tn), lambda i, j, k: (k, j))],
        [out_spec], [out_shape], [] if in_place else [pltpu.VMEM((tm, tn), F32)], (a, b))
    return outs[0] if ex is None else outs


PASS_ON_STEPS_BEFORE_END = 8


class _Exchange(NamedTuple):
    operands: list
    out_shapes: list
    sem_shapes: list
    build: Callable


def _call_with_exchange(ex, body, name, grid, in_specs, out_specs, out_shape, scratch_shapes, operands, prefetch=()):
    n_pre = len(prefetch)
    total = int(np.prod(grid))
    pass_step = max(total // 2, total - PASS_ON_STEPS_BEFORE_END)

    def call(kernel, ins, outs, shapes, scratch):
        if n_pre:
            spec = pltpu.PrefetchScalarGridSpec(num_scalar_prefetch=n_pre, grid=grid, in_specs=ins, out_specs=outs,
                                                scratch_shapes=scratch)
            return pl.pallas_call(kernel, name=name, grid_spec=spec, out_shape=shapes, compiler_params=_cparams())
        return pl.pallas_call(kernel, name=name, grid=grid, in_specs=ins, out_specs=outs, out_shape=shapes,
                              scratch_shapes=scratch, compiler_params=_cparams())

    if ex is None:
        return call(body, in_specs, out_specs, out_shape, scratch_shapes)(*prefetch, *operands)
    n_in, n_out, n_scr = len(in_specs), len(out_specs), len(scratch_shapes)
    k_in, k_out = len(ex.operands), len(ex.out_shapes)

    def carried(*refs):
        pre, refs = refs[:n_pre], refs[n_pre:]
        a, b = n_in, n_in + k_in
        c, e = b + n_out, b + n_out + k_out
        f = e + n_scr
        start, pass_on, finish = ex.build(refs[a:b], refs[c:e], refs[f:])
        step = functools.reduce(lambda lin, ax: lin * grid[ax] + pl.program_id(ax), range(len(grid)), 0)
        pl.when(step == 0)(start)
        body(*pre, *refs[:a], *refs[b:c], *refs[e:f])
        pl.when(step == pass_step)(pass_on)
        pl.when(step == total - 1)(finish)

    return call(carried, list(in_specs) + [ANY] * k_in, list(out_specs) + [ANY] * k_out,
                list(out_shape) + list(ex.out_shapes), list(scratch_shapes) + list(ex.sem_shapes),
                )(*prefetch, *operands, *ex.operands)


def _run_exchange(ex, name):
    k_in, k_out = len(ex.operands), len(ex.out_shapes)

    def body(*refs):
        start, pass_on, finish = ex.build(refs[:k_in], refs[k_in:k_in + k_out], refs[k_in + k_out:])
        start()
        pass_on()
        finish()

    return pl.pallas_call(body, name=name, in_specs=[ANY] * k_in, out_specs=[ANY] * k_out,
                          out_shape=list(ex.out_shapes), scratch_shapes=list(ex.sem_shapes))(*ex.operands)


def _ffn_fwd(x, g, wgu4, wd2, name, ex=None, next_gain=None, target=None):
    t, d = x.shape
    tm = _row_tile(t, 512)
    assert next_gain is None or target is None
    extra = [a for a in (next_gain, target) if a is not None]

    def body(*refs):
        x_ref, g_ref, wg_ref, wu_ref, wd_ref = refs[:5]
        e_ref = refs[5] if extra else None
        outs, (xn_scr, acc_scr) = refs[5 + len(extra):-2], refs[-2:]
        if target is not None:
            dy_ref, loss_ref, gg_ref, uu_ref = outs
        elif next_gain is not None:
            o_ref, gg_ref, uu_ref, h_ref = outs
        else:
            o_ref, gg_ref, uu_ref = outs
        i, j = pl.program_id(0), pl.program_id(1)

        @pl.when(j == 0)
        def _():
            xv = x_ref[...]
            xn_scr[...] = (xv * _rstd(xv, d) * g_ref[...]).astype(BF16)
            acc_scr[...] = jnp.zeros_like(acc_scr)

        if target is not None:
            @pl.when((i == 0) & (j == 0))
            def _():
                loss_ref[...] = jnp.zeros_like(loss_ref)

        xn = xn_scr[...]
        gg = _dot(xn, wg_ref[0])
        uu = _dot(xn, wu_ref[0])
        gg_ref[...] = gg.astype(BF16)
        uu_ref[...] = uu.astype(BF16)
        act = gg * jax.nn.sigmoid(gg) * uu
        acc_scr[...] += _dot(act.astype(BF16), wd_ref[0])

        @pl.when(j == 1)
        def _():
            y = x_ref[...] + 0.5 * acc_scr[...]
            if target is not None:
                e = y - e_ref[...]
                dy_ref[...] = e * (1.0 / d)
                part = 0.5 * jnp.sum(jnp.sum(e * e, axis=-1, keepdims=True) * (1.0 / d), axis=0, keepdims=True)
                loss_ref[...] += jnp.broadcast_to(part, loss_ref.shape)
            else:
                o_ref[...] = y
                if next_gain is not None:
                    h_ref[...] = (y * _rstd(y, d) * e_ref[...]).astype(BF16)

    row = pl.BlockSpec((tm, d), lambda i, j: (i, 0))
    vec = pl.BlockSpec((1, d), lambda i, j: (0, 0))
    ffb = pl.BlockSpec((tm, FF_TILE), lambda i, j: (i, j))
    f32_rows, bf16_ff = jax.ShapeDtypeStruct((t, d), F32), jax.ShapeDtypeStruct((t, D_FF), BF16)
    if target is not None:
        extra_spec, out_specs = [row], [row, pl.BlockSpec((1, LANES), lambda i, j: (0, 0)), ffb, ffb]
        out_shape = [f32_rows, jax.ShapeDtypeStruct((1, LANES), F32), bf16_ff, bf16_ff]
    elif next_gain is not None:
        extra_spec, out_specs = [vec], [row, ffb, ffb, row]
        out_shape = [f32_rows, bf16_ff, bf16_ff, jax.ShapeDtypeStruct((t, d), BF16)]
    else:
        extra_spec, out_specs, out_shape = [], [row, ffb, ffb], [f32_rows, bf16_ff, bf16_ff]
    return _call_with_exchange(
        ex, body, name, (t // tm, 2),
        [row, vec,
         pl.BlockSpec((1, d, FF_TILE), lambda i, j: (j, 0, 0)),
         pl.BlockSpec((1, d, FF_TILE), lambda i, j: (j + 2, 0, 0)),
         pl.BlockSpec((1, FF_TILE, d), lambda i, j: (j, 0, 0))] + extra_spec,
        out_specs, out_shape,
        [pltpu.VMEM((tm, d), BF16), pltpu.VMEM((tm, d), F32)], (x, g, wgu4, wgu4, wd2, *extra))


def _ffn_bwd(x, g, dy, gpre, upre, wgu4, wd2, name, ex=None):
    t, d = x.shape
    tm = _row_tile(t, 512)

    def body(dy_ref, gg_ref, uu_ref, wgu_hbm, wd_hbm, dg_ref, du_ref, act_ref, part_ref, wg_ref, wu_ref, wd_ref):
        j = pl.program_id(0)

        @pl.when(pl.program_id(1) == 0)
        def _():
            pltpu.sync_copy(wgu_hbm.at[j], wg_ref.at[0])
            pltpu.sync_copy(wgu_hbm.at[j + 2], wu_ref.at[0])
            pltpu.sync_copy(wd_hbm.at[j], wd_ref.at[0])

        gg = gg_ref[...].astype(F32)
        uu = uu_ref[...].astype(F32)
        sg = jax.nn.sigmoid(gg)
        silu = gg * sg
        act_ref[...] = (silu * uu).astype(BF16)
        dyh = (0.5 * dy_ref[...]).astype(BF16)
        dact = _dot_nt(dyh, wd_ref[0])
        du = (dact * silu).astype(BF16)
        dgt = (dact * uu * (sg * (1.0 + gg * (1.0 - sg)))).astype(BF16)
        du_ref[...] = du
        dg_ref[...] = dgt
        part_ref[0] = (_dot_nt(dgt, wg_ref[0]) + _dot_nt(du, wu_ref[0])).astype(BF16)

    row = pl.BlockSpec((tm, d), lambda j, i: (i, 0))
    ffb = pl.BlockSpec((tm, FF_TILE), lambda j, i: (i, j))
    dgt, dup, act, parts, *got = _call_with_exchange(
        ex, body, name, (2, t // tm),
        [row, ffb, ffb, ANY, ANY],
        [ffb, ffb, ffb, pl.BlockSpec((1, tm, d), lambda j, i: (j, i, 0))],
        [jax.ShapeDtypeStruct((t, D_FF), BF16)] * 3 + [jax.ShapeDtypeStruct((2, t, d), BF16)],
        [pltpu.VMEM((1, d, FF_TILE), BF16), pltpu.VMEM((1, d, FF_TILE), BF16), pltpu.VMEM((1, FF_TILE, d), BF16)],
        (dy, gpre, upre, wgu4, wd2))

    def norm_body(x_ref, g_ref, p_ref, dy_ref, dx_ref, dgain_ref, xn_ref):
        @pl.when(pl.program_id(0) == 0)
        def _():
            dgain_ref[...] = jnp.zeros_like(dgain_ref)

        xv = x_ref[...]
        r = _rstd(xv, d)
        xn_ref[...] = (xv * r * g_ref[...]).astype(BF16)
        dx, dgr = _rms_vjp(xv, r, g_ref[...], p_ref[0].astype(F32) + p_ref[1].astype(F32), d)
        dx_ref[...] = dy_ref[...] + dx
        dgain_ref[...] += jnp.sum(dgr, axis=0, keepdims=True)

    tn = _row_tile(t, 256)
    nrow = pl.BlockSpec((tn, d), lambda i: (i, 0))
    vec = pl.BlockSpec((1, d), lambda i: (0, 0))
    dx, dgain, xn = pl.pallas_call(
        norm_body, name=name + "_norm", grid=(t // tn,),
        in_specs=[nrow, vec, pl.BlockSpec((2, tn, d), lambda i: (0, i, 0)), nrow],
        out_specs=[nrow, vec, nrow],
        out_shape=[jax.ShapeDtypeStruct((t, d), F32), jax.ShapeDtypeStruct((1, d), F32),
                   jax.ShapeDtypeStruct((t, d), BF16)],
        compiler_params=_cparams())(x, g, parts, dy)
    return [dx, dgain, xn, dgt, dup, act] + got


def _sgu_layernorm(vpre, lg, lb):
    v = _gelu(vpre)
    mu = jnp.mean(v, axis=-1, keepdims=True)
    xc = v - mu
    rstd = lax.rsqrt(jnp.mean(xc * xc, axis=-1, keepdims=True) + EPS)
    xhat = xc * rstd
    return xhat, rstd, xhat * lg + lb


def _sgu_fwd(zuv, lg, lb, wt, bias_l, name):
    t = zuv.shape[0]
    tm = _row_tile(t, 512)

    def body(u_ref, v_ref, lg_ref, lb_ref, wt_ref, bl_ref, o_ref, vln_scr):
        _, _, vln = _sgu_layernorm(v_ref[...], lg_ref[...], lb_ref[...])
        vln_scr[...] = vln.astype(BF16)
        lo = lax.broadcasted_iota(jnp.int32, (CHUNK, LANES), 1) < 64
        for c in range(tm // CHUNK):
            rows = slice(c * CHUNK, (c + 1) * CHUNK)
            for p in range(SG_GROUPS // 2):
                cols = slice(p * LANES, (p + 1) * LANES)
                vp = vln_scr[rows, cols]
                mixed = jnp.where(lo, _dot(wt_ref[2 * p], vp), _dot(wt_ref[2 * p + 1], vp)) + bl_ref[:, cols]
                o_ref[rows, cols] = (_gelu(u_ref[rows, cols]) * mixed).astype(BF16)

    half = lambda k: pl.BlockSpec((tm, SG_WIDTH), lambda i: (i, k))
    vec = pl.BlockSpec((1, SG_WIDTH), lambda i: (0, 0))
    return pl.pallas_call(
        body, name=name, grid=(t // tm,),
        in_specs=[half(0), half(1), vec, vec,
                  pl.BlockSpec((SG_GROUPS, CHUNK, CHUNK), lambda i: (0, 0, 0)),
                  pl.BlockSpec((CHUNK, SG_WIDTH), lambda i: (0, 0))],
        out_specs=pl.BlockSpec((tm, SG_WIDTH), lambda i: (i, 0)),
        out_shape=jax.ShapeDtypeStruct((t, SG_WIDTH), BF16),
        scratch_shapes=[pltpu.VMEM((tm, SG_WIDTH), BF16)],
        compiler_params=_cparams())(zuv, zuv, lg, lb, wt, bias_l)


def _sgu_bwd(zuv, dya, lg, lb, wt, wt_t, bias_l, name):
    t = zuv.shape[0]
    tm = _row_tile(t, 256)
    nsteps = t // tm

    def body(u_ref, v_ref, dy_ref, lg_ref, lb_ref, wt_ref, wtt_ref, bl_ref,
             dz_ref, dwt_ref, dbl_ref, dlg_ref, dlb_ref, vln_scr, dvln_scr, dbacc_scr):
        step = pl.program_id(0)

        @pl.when(step == 0)
        def _():
            dwt_ref[...] = jnp.zeros_like(dwt_ref)
            dlg_ref[...] = jnp.zeros_like(dlg_ref)
            dlb_ref[...] = jnp.zeros_like(dlb_ref)
            dbl_ref[...] = jnp.zeros_like(dbl_ref)
            dbacc_scr[...] = jnp.zeros_like(dbacc_scr)

        vpre = v_ref[...]
        lgv = lg_ref[...]
        xhat, rstd, vln = _sgu_layernorm(vpre, lgv, lb_ref[...])
        vln_scr[...] = vln.astype(BF16)
        lo = lax.broadcasted_iota(jnp.int32, (CHUNK, LANES), 1) < 64
        for c in range(tm // CHUNK):
            rows = slice(c * CHUNK, (c + 1) * CHUNK)
            for p in range(SG_GROUPS // 2):
                cols = slice(p * LANES, (p + 1) * LANES)
                vp = vln_scr[rows, cols]
                mixed = jnp.where(lo, _dot(wt_ref[2 * p], vp), _dot(wt_ref[2 * p + 1], vp)) + bl_ref[:, cols]
                upre = u_ref[rows, cols]
                dyp = dy_ref[rows, cols]
                dz_ref[rows, cols] = (dyp * mixed * _gelu_grad(upre)).astype(BF16)
                dm = dyp * _gelu(upre)
                dbacc_scr[:, cols] += dm
                dlo = jnp.where(lo, dm, 0.0).astype(BF16)
                dhi = jnp.where(lo, 0.0, dm).astype(BF16)
                dvln_scr[rows, cols] = _dot(wtt_ref[2 * p], dlo) + _dot(wtt_ref[2 * p + 1], dhi)
                dwt_ref[2 * p] += _dot_nt(dlo, vp)
                dwt_ref[2 * p + 1] += _dot_nt(dhi, vp)
        dvln = dvln_scr[...]
        dlg_ref[...] += jnp.sum(dvln * xhat, axis=0, keepdims=True)
        dlb_ref[...] += jnp.sum(dvln, axis=0, keepdims=True)
        dxh = dvln * lgv
        dv = rstd * (dxh - jnp.mean(dxh, axis=-1, keepdims=True)
                     - xhat * jnp.mean(dxh * xhat, axis=-1, keepdims=True))
        dz_ref[:, SG_WIDTH:] = (dv * _gelu_grad(vpre)).astype(BF16)

        @pl.when(step == nsteps - 1)
        def _():
            rr = lax.broadcasted_iota(jnp.int32, (CHUNK, CHUNK), 0)
            cc = lax.broadcasted_iota(jnp.int32, (CHUNK, CHUNK), 1)
            tril = (cc <= rr).astype(F32)
            for gidx in range(SG_GROUPS):
                dwt_ref[gidx] = dwt_ref[gidx] * tril
            kk = lax.broadcasted_iota(jnp.int32, (SG_WIDTH, LANES), 0)
            gg = lax.broadcasted_iota(jnp.int32, (SG_WIDTH, LANES), 1)
            sel = ((kk // 64) == gg).astype(F32)
            dbl_ref[...] = jnp.dot(dbacc_scr[...], sel, preferred_element_type=F32,
                                   precision=lax.Precision.HIGHEST)

    half = lambda k: pl.BlockSpec((tm, SG_WIDTH), lambda i: (i, k))
    vec = pl.BlockSpec((1, SG_WIDTH), lambda i: (0, 0))
    wspec = pl.BlockSpec((SG_GROUPS, CHUNK, CHUNK), lambda i: (0, 0, 0))
    return pl.pallas_call(
        body, name=name, grid=(nsteps,),
        in_specs=[half(0), half(1), pl.BlockSpec((tm, SG_WIDTH), lambda i: (i, 0)), vec, vec,
                  wspec, wspec, pl.BlockSpec((CHUNK, SG_WIDTH), lambda i: (0, 0))],
        out_specs=[pl.BlockSpec((tm, 2 * SG_WIDTH), lambda i: (i, 0)), wspec,
                   pl.BlockSpec((CHUNK, LANES), lambda i: (0, 0)), vec, vec],
        out_shape=[jax.ShapeDtypeStruct((t, 2 * SG_WIDTH), BF16),
                   jax.ShapeDtypeStruct((SG_GROUPS, CHUNK, CHUNK), F32),
                   jax.ShapeDtypeStruct((CHUNK, LANES), F32),
                   jax.ShapeDtypeStruct((1, SG_WIDTH), F32), jax.ShapeDtypeStruct((1, SG_WIDTH), F32)],
        scratch_shapes=[pltpu.VMEM((tm, SG_WIDTH), BF16), pltpu.VMEM((tm, SG_WIDTH), F32),
                        pltpu.VMEM((CHUNK, SG_WIDTH), F32)],
        compiler_params=_cparams())(zuv, zuv, dya, lg, lb, wt, wt_t, bias_l)


def _rope(x, c, s1, s2):
    return x * c + pltpu.roll(x, LANES - 16, 1) * s1 + pltpu.roll(x, 16, 1) * s2


def _rope_t(dy, c, s1, s2):
    return dy * c + pltpu.roll(dy * s1, 16, 1) + pltpu.roll(dy * s2, LANES - 16, 1)


def _mla_prep_fwd(zcq, zckv, zkr, gcq, gckv, qg, kg, wuq, wuk, wuv, rc, rs1, rs2, name, ex=None):
    t = zcq.shape[0]
    tm = _row_tile(t, 256)
    hd = MLA_HEADS * LANES

    def body(zcq_ref, zckv_ref, zkr_ref, gcq_ref, gckv_ref, qg_ref, kg_ref, wuq_ref, wuk_ref, wuv_ref,
             c_ref, s1_ref, s2_ref, q_ref, k_ref, v_ref, cqn_ref, ckvn_ref):
        c, s1, s2 = c_ref[...], s1_ref[...], s2_ref[...]
        xq = zcq_ref[...]
        cqn = (xq * _rstd(xq, MLA_Q_RANK) * gcq_ref[...]).astype(BF16)
        cqn_ref[...] = cqn
        ql = _dot(cqn, wuq_ref[...])
        xk = zckv_ref[...]
        ckvn = (xk * _rstd(xk, MLA_KV_RANK) * gckv_ref[...]).astype(BF16)
        ckvn_ref[...] = ckvn
        kl = _dot(ckvn, wuk_ref[...])
        slot_lane = lax.broadcasted_iota(jnp.int32, (tm, hd), 1) % LANES
        v_ref[...] = jnp.where(slot_lane == V_ONES_LANE, 1.0, _dot(ckvn, wuv_ref[...])).astype(BF16)
        kr = zkr_ref[...]
        for h in range(MLA_HEADS):
            sl = slice(h * LANES, (h + 1) * LANES)
            qh = ql[:, sl]
            q_ref[:, sl] = (_rope(qh * _rstd(qh, MLA_QK) * qg_ref[...], c, s1, s2) * ATTN_SCALE2).astype(BF16)
            kh = kl[:, sl] + kr
            k_ref[:, sl] = _rope(kh * _rstd(kh, MLA_QK) * kg_ref[...], c, s1, s2).astype(BF16)

    row = lambda n: pl.BlockSpec((tm, n), lambda i: (i, 0))
    full = lambda a: pl.BlockSpec(a.shape, lambda i: (0, 0))
    return _call_with_exchange(
        ex, body, name, (t // tm,),
        [row(MLA_Q_RANK), row(MLA_KV_RANK), row(LANES), full(gcq), full(gckv), full(qg), full(kg),
         full(wuq), full(wuk), full(wuv), row(LANES), row(LANES), row(LANES)],
        [row(hd), row(hd), row(hd), row(MLA_Q_RANK), row(MLA_KV_RANK)],
        [jax.ShapeDtypeStruct((t, hd), BF16)] * 3
        + [jax.ShapeDtypeStruct((t, MLA_Q_RANK), BF16), jax.ShapeDtypeStruct((t, MLA_KV_RANK), BF16)],
        [], (zcq, zckv, zkr, gcq, gckv, qg, kg, wuq, wuk, wuv, rc, rs1, rs2))


def _mla_prep_bwd(zcq, zckv, zkr, gcq, gckv, qg, kg, wuq, wuk, wuv, rc, rs1, rs2, dq, dk, dv, name):
    t = zcq.shape[0]
    tm = _row_tile(t, 256)
    hd = MLA_HEADS * LANES

    def body(zcq_ref, zckv_ref, zkr_ref, gcq_ref, gckv_ref, qg_ref, kg_ref, wuq_ref, wuk_ref, wuv_ref,
             c_ref, s1_ref, s2_ref, dq_ref, dk_ref, dv_ref,
             dzcq_ref, dzckv_ref, dzkr_ref, dql_ref, dkl_ref, dgcq_ref, dgckv_ref, dqg_ref, dkg_ref):
        @pl.when(pl.program_id(0) == 0)
        def _():
            for ref in (dgcq_ref, dgckv_ref, dqg_ref, dkg_ref):
                ref[...] = jnp.zeros_like(ref)

        c, s1, s2 = c_ref[...], s1_ref[...], s2_ref[...]
        qgv, kgv = qg_ref[...], kg_ref[...]
        xq = zcq_ref[...]
        rq = _rstd(xq, MLA_Q_RANK)
        ql = _dot((xq * rq * gcq_ref[...]).astype(BF16), wuq_ref[...])
        xk = zckv_ref[...]
        rk = _rstd(xk, MLA_KV_RANK)
        kl = _dot((xk * rk * gckv_ref[...]).astype(BF16), wuk_ref[...])
        kr = zkr_ref[...]
        dqg_acc = jnp.zeros((tm, LANES), F32)
        dkg_acc = jnp.zeros((tm, LANES), F32)
        dkr = jnp.zeros((tm, LANES), F32)
        for h in range(MLA_HEADS):
            sl = slice(h * LANES, (h + 1) * LANES)
            qh = ql[:, sl]
            dqh, dgr = _rms_vjp(qh, _rstd(qh, MLA_QK), qgv, _rope_t(dq_ref[:, sl], c, s1, s2), MLA_QK)
            dql_ref[:, sl] = dqh.astype(BF16)
            dqg_acc += dgr
            kh = kl[:, sl] + kr
            dkh, dgr = _rms_vjp(kh, _rstd(kh, MLA_QK), kgv, _rope_t(dk_ref[:, sl], c, s1, s2), MLA_QK)
            dkl_ref[:, sl] = dkh.astype(BF16)
            dkg_acc += dgr
            dkr += dkh
        dqg_ref[...] += jnp.sum(dqg_acc, axis=0, keepdims=True)
        dkg_ref[...] += jnp.sum(dkg_acc, axis=0, keepdims=True)
        lane = lax.broadcasted_iota(jnp.int32, (tm, LANES), 1)
        dzkr_ref[...] = jnp.where((lane >= MLA_NOPE) & (lane < MLA_QK), dkr, 0.0).astype(BF16)
        dcqn = _dot_nt(dql_ref[...], wuq_ref[...])
        dx, dgr = _rms_vjp(xq, rq, gcq_ref[...], dcqn, MLA_Q_RANK)
        dzcq_ref[...] = dx.astype(BF16)
        dgcq_ref[...] += jnp.sum(dgr, axis=0, keepdims=True)
        dckvn = _dot_nt(dkl_ref[...], wuk_ref[...]) + _dot_nt(dv_ref[...].astype(BF16), wuv_ref[...])
        dx, dgr = _rms_vjp(xk, rk, gckv_ref[...], dckvn, MLA_KV_RANK)
        dzckv_ref[...] = dx.astype(BF16)
        dgckv_ref[...] += jnp.sum(dgr, axis=0, keepdims=True)

    row = lambda n: pl.BlockSpec((tm, n), lambda i: (i, 0))
    full = lambda a: pl.BlockSpec(a.shape, lambda i: (0, 0))
    vec = lambda n: pl.BlockSpec((1, n), lambda i: (0, 0))
    return pl.pallas_call(
        body, name=name, grid=(t // tm,),
        in_specs=[row(MLA_Q_RANK), row(MLA_KV_RANK), row(LANES), full(gcq), full(gckv), full(qg), full(kg),
                  full(wuq), full(wuk), full(wuv), row(LANES), row(LANES), row(LANES), row(hd), row(hd), row(hd)],
        out_specs=[row(MLA_Q_RANK), row(MLA_KV_RANK), row(LANES), row(hd), row(hd),
                   vec(MLA_Q_RANK), vec(MLA_KV_RANK), vec(LANES), vec(LANES)],
        out_shape=[jax.ShapeDtypeStruct((t, MLA_Q_RANK), BF16), jax.ShapeDtypeStruct((t, MLA_KV_RANK), BF16),
                   jax.ShapeDtypeStruct((t, LANES), BF16), jax.ShapeDtypeStruct((t, hd), BF16),
                   jax.ShapeDtypeStruct((t, hd), BF16), jax.ShapeDtypeStruct((1, MLA_Q_RANK), F32),
                   jax.ShapeDtypeStruct((1, MLA_KV_RANK), F32), jax.ShapeDtypeStruct((1, LANES), F32),
                   jax.ShapeDtypeStruct((1, LANES), F32)],
        compiler_params=_cparams(),
    )(zcq, zckv, zkr, gcq, gckv, qg, kg, wuq, wuk, wuv, rc, rs1, rs2, dq, dk, dv)


def _attn_tiles(t):
    tq = 512 if t >= 2048 else 128
    return tq, min(t, 4 * tq), min(t, 4 * tq)


def _causal_keep(tq, nk, i, j, tk):
    row = lax.broadcasted_iota(jnp.int32, (tq, nk), 0)
    col = lax.broadcasted_iota(jnp.int32, (tq, nk), 1)
    return (col - row) <= (i * tq - j * tk)


def _causal_keep_t(tq, nk, i, j, tk):
    key = lax.broadcasted_iota(jnp.int32, (nk, tq), 0)
    qry = lax.broadcasted_iota(jnp.int32, (nk, tq), 1)
    return (key - qry) <= (i * tq - j * tk)


ATTN_FWD_HEADS_PER_STEP = 2
ATTN_BWD_HEADS_PER_STEP = 2


def _attn_fwd(q, k, v, name, ex=None):
    t, hd = q.shape
    hp = ATTN_FWD_HEADS_PER_STEP
    tq, tk, _ = _attn_tiles(t)
    pairs = [(i, j) for i in range(t // tq) for j in range(((i + 1) * tq - 1) // tk + 1)]
    ii = np.array([p[0] for p in pairs], np.int32)
    jj = np.array([p[1] for p in pairs], np.int32)

    def body(ii_ref, jj_ref, q_ref, k_ref, v_ref, o_ref, lse_ref, m_scr, acc_scr):
        s_id = pl.program_id(1)
        i, j = ii_ref[s_id], jj_ref[s_id]
        last = j == ((i + 1) * tq - 1) // tk
        ones_lane = lax.broadcasted_iota(jnp.int32, (tq, LANES), 1) == V_ONES_LANE

        @pl.when(j == 0)
        def _():
            m_scr[...] = jnp.full_like(m_scr, NEG)
            acc_scr[...] = jnp.zeros_like(acc_scr)

        def step(masked, nk):
            for hh in range(hp):
                sl = slice(hh * LANES, (hh + 1) * LANES)
                s = _dot_nt(q_ref[:, sl], k_ref[:nk, sl])
                if masked:
                    s = jnp.where(_causal_keep(tq, nk, i, j, tk), s, NEG)
                m_prev = m_scr[hh]
                m_new = jnp.maximum(m_prev, jnp.max(s, axis=1, keepdims=True))
                p = jnp.exp2(s - m_new)
                alpha = jnp.exp2(m_prev - m_new)
                acc = alpha * acc_scr[:, sl] + _dot(p.astype(BF16), v_ref[:nk, sl])
                if masked:
                    l_new = jnp.sum(jnp.where(ones_lane, acc, 0.0), axis=1, keepdims=True)
                    o_ref[:, sl] = (acc / l_new).astype(BF16)
                    lse_ref[:, sl] = jnp.broadcast_to(m_new + jnp.log(l_new) * LOG2E, (tq, LANES))
                else:
                    acc_scr[:, sl] = acc
                    m_scr[hh] = m_new

        @pl.when(jnp.logical_not(last))
        def _():
            step(False, tk)

        r = (((i + 1) * tq - 1) % tk) // tq
        for rr in range(tk // tq):
            @pl.when(last & (r == rr))
            def _():
                step(True, (rr + 1) * tq)

    w = hp * LANES
    qspec = pl.BlockSpec((tq, w), lambda h, s, ii_r, jj_r: (ii_r[s], h))
    kspec = pl.BlockSpec((tk, w), lambda h, s, ii_r, jj_r: (jj_r[s], h))
    return _call_with_exchange(
        ex, body, name, (hd // w, len(pairs)), [qspec, kspec, kspec], [qspec, qspec],
        [jax.ShapeDtypeStruct((t, hd), BF16), jax.ShapeDtypeStruct((t, hd), F32)],
        [pltpu.VMEM((hp, tq, 1), F32), pltpu.VMEM((tq, w), F32)], (q, k, v),
        prefetch=(jnp.asarray(ii), jnp.asarray(jj)))


def _attn_bwd_rows(o, lse, do, name):
    t, hd = o.shape
    heads = hd // LANES
    tm = _row_tile(t, 512)

    def body(o_ref, lse_ref, do_ref, out_ref):
        lane = lax.broadcasted_iota(jnp.int32, (tm, LANES), 1)
        acc = jnp.zeros((tm, LANES), F32)
        for h in range(heads):
            sl = slice(h * LANES, (h + 1) * LANES)
            delta = jnp.sum(do_ref[:, sl].astype(F32) * o_ref[:, sl].astype(F32), axis=1, keepdims=True)
            acc = jnp.where(lane == h, delta, acc)
            acc = jnp.where(lane == heads + h, lse_ref[:, sl], acc)
        out_ref[...] = acc

    row = pl.BlockSpec((tm, hd), lambda i: (i, 0))
    cols = pl.pallas_call(
        body, name=name, grid=(t // tm,), in_specs=[row, row, row],
        out_specs=pl.BlockSpec((tm, LANES), lambda i: (i, 0)),
        out_shape=jax.ShapeDtypeStruct((t, LANES), F32), compiler_params=_cparams())(o, lse, do)
    rows = cols.T
    return rows[:heads].reshape(heads, 1, t), rows[heads:2 * heads].reshape(heads, 1, t)


def _attn_bwd(q, k, v, delta_rows, lse_rows, do, name):
    t, hd = q.shape
    hp = ATTN_BWD_HEADS_PER_STEP
    tq, _, tk = _attn_tiles(t)
    nq = t // tq
    pairs = [(i, j) for j in range(t // tk) for i in range((j * tk) // tq, nq)]
    ii = np.array([p[0] for p in pairs], np.int32)
    jj = np.array([p[1] for p in pairs], np.int32)

    def body(jj_ref, ii_ref, q_ref, k_ref, v_ref, delta_ref, lse_ref, do_ref, dq_ref, dk_ref, dv_ref,
             dk_scr, dv_scr, dq_scr):
        s_id = pl.program_id(1)
        i, j = ii_ref[s_id], jj_ref[s_id]

        @pl.when(s_id == 0)
        def _():
            dq_scr[...] = jnp.zeros_like(dq_scr)

        @pl.when(i == (j * tk) // tq)
        def _():
            dk_scr[...] = jnp.zeros_like(dk_scr)
            dv_scr[...] = jnp.zeros_like(dv_scr)

        rows = pl.ds(pl.multiple_of(i * tq, tq), tq)

        def step(masked, nk):
            for hh in range(hp):
                sl = slice(hh * LANES, (hh + 1) * LANES)
                qv, kv, dov = q_ref[:, sl], k_ref[:nk, sl], do_ref[:, sl]
                st = _dot_nt(kv, qv)
                if masked:
                    st = jnp.where(_causal_keep_t(tq, nk, i, j, tk), st, NEG)
                pt = jnp.exp2(st - lse_ref[hh])
                dv_scr[:nk, sl] += _dot(pt.astype(BF16), dov)
                dpt = _dot_nt(v_ref[:nk, sl], dov)
                dst = (pt * (dpt - delta_ref[hh]) * ATTN_SCALE).astype(BF16)
                dk_scr[:nk, sl] += _dot(dst, qv)
                dq_scr[rows, sl] += _dot_tn(dst, kv)

        seen = jnp.minimum((i + 1) * tq - j * tk, tk)
        for nk in range(tq, tk + 1, tq):
            @pl.when((seen == nk) & ((i + 1) * tq - j * tk <= tk))
            def _():
                step(True, nk)

        @pl.when((i + 1) * tq - j * tk > tk)
        def _():
            step(False, tk)

        @pl.when(i == nq - 1)
        def _():
            dk_ref[...] = (dk_scr[...] * (1.0 / ATTN_SCALE2)).astype(BF16)
            dv_ref[...] = dv_scr[...].astype(BF16)

        @pl.when(s_id == len(pairs) - 1)
        def _():
            dq_ref[...] = dq_scr[...].astype(BF16)

    w = hp * LANES
    qspec = pl.BlockSpec((tq, w), lambda h, s, jj_r, ii_r: (ii_r[s], h))
    kspec = pl.BlockSpec((tk, w), lambda h, s, jj_r, ii_r: (jj_r[s], h))
    rspec = pl.BlockSpec((hp, 1, tq), lambda h, s, jj_r, ii_r: (h, 0, ii_r[s]))
    return pl.pallas_call(
        body, name=name,
        grid_spec=pltpu.PrefetchScalarGridSpec(
            num_scalar_prefetch=2, grid=(hd // w, len(pairs)),
            in_specs=[qspec, kspec, kspec, rspec, rspec, qspec],
            out_specs=[pl.BlockSpec((t, w), lambda h, s, jj_r, ii_r: (0, h)), kspec, kspec],
            scratch_shapes=[pltpu.VMEM((tk, w), F32), pltpu.VMEM((tk, w), F32), pltpu.VMEM((t, w), F32)]),
        out_shape=[jax.ShapeDtypeStruct((t, hd), BF16)] * 3,
        compiler_params=_cparams())(jnp.asarray(jj), jnp.asarray(ii), q, k, v, delta_rows, lse_rows, do)


MEM_W = MEM_HEADS * LANES


def _mem_kv_fwd(mem, gmem, wkv, kg, name):
    m, d = mem.shape

    def body(mem_ref, g_ref, w_ref, kg_ref, k_ref, v_ref, mn_ref):
        xv = mem_ref[...]
        mn = (xv * _rstd(xv, d) * g_ref[...]).astype(BF16)
        mn_ref[...] = mn
        kvm = _dot(mn, w_ref[...])
        v_ref[...] = kvm[:, MEM_W:].astype(BF16)
        for h in range(MEM_HEADS):
            sl = slice(h * LANES, (h + 1) * LANES)
            kh = kvm[:, sl]
            k_ref[:, sl] = (kh * _rstd(kh, LANES) * kg_ref[...]).astype(BF16)

    full = lambda a: pl.BlockSpec(a.shape, lambda i: (0, 0))
    return pl.pallas_call(
        body, name=name, grid=(1,), in_specs=[full(mem), full(gmem), full(wkv), full(kg)],
        out_specs=[pl.BlockSpec((m, MEM_W), lambda i: (0, 0)), pl.BlockSpec((m, MEM_W), lambda i: (0, 0)),
                   pl.BlockSpec((m, d), lambda i: (0, 0))],
        out_shape=[jax.ShapeDtypeStruct((m, MEM_W), BF16), jax.ShapeDtypeStruct((m, MEM_W), BF16),
                   jax.ShapeDtypeStruct((m, d), BF16)],
        compiler_params=_cparams())(mem, gmem, wkv, kg)


def _mem_softmax(qn, kh):
    s = _dot_nt(qn, kh) * (LANES ** -0.5)
    e = jnp.exp(s - jnp.max(s, axis=1, keepdims=True))
    return e / jnp.sum(e, axis=1, keepdims=True)


def _mem_attn_fwd(zqm, qg, km, vm, name):
    t = zqm.shape[0]
    tm = _row_tile(t, 512)

    def body(q_ref, qg_ref, k_ref, v_ref, o_ref):
        for h in range(MEM_HEADS):
            sl = slice(h * LANES, (h + 1) * LANES)
            qh = q_ref[:, sl]
            qn = (qh * _rstd(qh, LANES) * qg_ref[...]).astype(BF16)
            p = _mem_softmax(qn, k_ref[:, sl])
            o_ref[:, sl] = _dot(p.astype(BF16), v_ref[:, sl]).astype(BF16)

    row = pl.BlockSpec((tm, MEM_W), lambda i: (i, 0))
    full = lambda a: pl.BlockSpec(a.shape, lambda i: (0, 0))
    return pl.pallas_call(
        body, name=name, grid=(t // tm,), in_specs=[row, full(qg), full(km), full(vm)], out_specs=row,
        out_shape=jax.ShapeDtypeStruct((t, MEM_W), BF16), compiler_params=_cparams())(zqm, qg, km, vm)


def _mem_attn_bwd(zqm, dyc, qg, km, vm, name):
    t = zqm.shape[0]
    m = km.shape[0]
    tm = _row_tile(t, 256)

    def body(q_ref, dy_ref, qg_ref, k_ref, v_ref, dz_ref, dk_ref, dv_ref, dqg_ref):
        @pl.when(pl.program_id(0) == 0)
        def _():
            dk_ref[...] = jnp.zeros_like(dk_ref)
            dv_ref[...] = jnp.zeros_like(dv_ref)
            dqg_ref[...] = jnp.zeros_like(dqg_ref)

        qgv = qg_ref[...]
        dqg_acc = jnp.zeros((tm, LANES), F32)
        for h in range(MEM_HEADS):
            sl = slice(h * LANES, (h + 1) * LANES)
            qh = q_ref[:, sl]
            r = _rstd(qh, LANES)
            qn = (qh * r * qgv).astype(BF16)
            kh = k_ref[:, sl]
            p = _mem_softmax(qn, kh)
            dov = dy_ref[:, sl]
            dv_ref[:, sl] += _dot_tn(p.astype(BF16), dov)
            dp = _dot_nt(dov, v_ref[:, sl])
            ds = (p * (dp - jnp.sum(dp * p, axis=1, keepdims=True)) * (LANES ** -0.5)).astype(BF16)
            dk_ref[:, sl] += _dot_tn(ds, qn)
            dqh, dgr = _rms_vjp(qh, r, qgv, _dot(ds, kh), LANES)
            dz_ref[:, sl] = dqh.astype(BF16)
            dqg_acc += dgr
        dqg_ref[...] += jnp.sum(dqg_acc, axis=0, keepdims=True)

    row = pl.BlockSpec((tm, MEM_W), lambda i: (i, 0))
    full = lambda a: pl.BlockSpec(a.shape, lambda i: (0, 0))
    acc = pl.BlockSpec((m, MEM_W), lambda i: (0, 0))
    return pl.pallas_call(
        body, name=name, grid=(t // tm,), in_specs=[row, row, full(qg), full(km), full(vm)],
        out_specs=[row, acc, acc, pl.BlockSpec((1, LANES), lambda i: (0, 0))],
        out_shape=[jax.ShapeDtypeStruct((t, MEM_W), BF16), jax.ShapeDtypeStruct((m, MEM_W), F32),
                   jax.ShapeDtypeStruct((m, MEM_W), F32), jax.ShapeDtypeStruct((1, LANES), F32)],
        compiler_params=_cparams())(zqm, dyc, qg, km, vm)


def _mem_kv_bwd(mem, gmem, wkv, kg, dkn, dvm, name):
    m, d = mem.shape

    def body(mem_ref, g_ref, w_ref, kg_ref, dk_ref, dv_ref, dw_ref, dkg_ref, dg_ref, dkv_scr):
        xv = mem_ref[...]
        r = _rstd(xv, d)
        mn = (xv * r * g_ref[...]).astype(BF16)
        kvm = _dot(mn, w_ref[...])
        dkv_scr[:, MEM_W:] = dv_ref[...].astype(BF16)
        dkg_acc = jnp.zeros((m, LANES), F32)
        for h in range(MEM_HEADS):
            sl = slice(h * LANES, (h + 1) * LANES)
            kh = kvm[:, sl]
            dkh, dgr = _rms_vjp(kh, _rstd(kh, LANES), kg_ref[...], dk_ref[:, sl], LANES)
            dkv_scr[:, sl] = dkh.astype(BF16)
            dkg_acc += dgr
        dkg_ref[...] = jnp.sum(dkg_acc, axis=0, keepdims=True)
        dkv = dkv_scr[...]
        dw_ref[...] = _dot_tn(mn, dkv)
        dmn = _dot_nt(dkv, w_ref[...])
        dg_ref[...] = jnp.sum(dmn * xv * r, axis=0, keepdims=True)

    full = lambda a: pl.BlockSpec(a.shape, lambda i: (0, 0))
    return pl.pallas_call(
        body, name=name, grid=(1,),
        in_specs=[full(mem), full(gmem), full(wkv), full(kg), full(dkn), full(dvm)],
        out_specs=[pl.BlockSpec((d, 2 * MEM_W), lambda i: (0, 0)), pl.BlockSpec((1, LANES), lambda i: (0, 0)),
                   pl.BlockSpec((1, d), lambda i: (0, 0))],
        out_shape=[jax.ShapeDtypeStruct((d, 2 * MEM_W), F32), jax.ShapeDtypeStruct((1, LANES), F32),
                   jax.ShapeDtypeStruct((1, d), F32)],
        scratch_shapes=[pltpu.VMEM((m, 2 * MEM_W), BF16)],
        compiler_params=_cparams())(mem, gmem, wkv, kg, dkn, dvm)


def _merge_fwd(x1, ya, yb, yc, zg, bg, wa, wb, wc, wo, name):
    t, d = x1.shape
    tm = _row_tile(t, 256)

    def body(x_ref, ya_ref, yb_ref, yc_ref, zg_ref, bg_ref, wa_ref, wb_ref, wc_ref, wo_ref,
             x2_ref, mg_ref, pa_ref, pb_ref, pc_ref):
        merged = None
        for k, (y_ref, w_ref, p_ref) in enumerate(
                ((ya_ref, wa_ref, pa_ref), (yb_ref, wb_ref, pb_ref), (yc_ref, wc_ref, pc_ref))):
            sl = slice(k * d, (k + 1) * d)
            pr = _dot(y_ref[...], w_ref[...])
            p_ref[...] = pr.astype(BF16)
            term = jax.nn.sigmoid(zg_ref[:, sl] + bg_ref[:, sl]) * pr
            merged = term if merged is None else merged + term
        mb = merged.astype(BF16)
        mg_ref[...] = mb
        x2_ref[...] = x_ref[...] + _dot(mb, wo_ref[...])

    row = lambda n: pl.BlockSpec((tm, n), lambda i: (i, 0))
    full = lambda a: pl.BlockSpec(a.shape, lambda i: (0, 0))
    return pl.pallas_call(
        body, name=name, grid=(t // tm,),
        in_specs=[row(d), row(ya.shape[1]), row(yb.shape[1]), row(yc.shape[1]), row(3 * d), full(bg),
                  full(wa), full(wb), full(wc), full(wo)],
        out_specs=[row(d)] * 5,
        out_shape=[jax.ShapeDtypeStruct((t, d), F32)] + [jax.ShapeDtypeStruct((t, d), BF16)] * 4,
        compiler_params=_cparams())(x1, ya, yb, yc, zg, bg, wa, wb, wc, wo)


def _merge_bwd(dx2, pa, pb, pc, zg, bg, wa, wb, wc, wo, name, ex=None):
    t, d = dx2.shape
    tm = _row_tile(t, 256)

    def body(dx_ref, pa_ref, pb_ref, pc_ref, zg_ref, bg_ref, wa_ref, wb_ref, wc_ref, wo_ref,
             dpa_ref, dpb_ref, dpc_ref, dzg_ref, dbg_ref, dya_ref, dyb_ref, dyc_ref):
        @pl.when(pl.program_id(0) == 0)
        def _():
            dbg_ref[...] = jnp.zeros_like(dbg_ref)

        dm = _dot_nt(dx_ref[...].astype(BF16), wo_ref[...])
        for k, (p_ref, w_ref, dp_ref, dy_ref) in enumerate(
                ((pa_ref, wa_ref, dpa_ref, dya_ref), (pb_ref, wb_ref, dpb_ref, dyb_ref),
                 (pc_ref, wc_ref, dpc_ref, dyc_ref))):
            sl = slice(k * d, (k + 1) * d)
            gate = jax.nn.sigmoid(zg_ref[:, sl] + bg_ref[:, sl])
            dpr = (dm * gate).astype(BF16)
            dp_ref[...] = dpr
            dzg = dm * p_ref[...].astype(F32) * gate * (1.0 - gate)
            dzg_ref[:, sl] = dzg.astype(BF16)
            dbg_ref[:, sl] += jnp.sum(dzg, axis=0, keepdims=True)
            dy_ref[...] = _dot_nt(dpr, w_ref[...]).astype(dy_ref.dtype)

    row = lambda n: pl.BlockSpec((tm, n), lambda i: (i, 0))
    full = lambda a: pl.BlockSpec(a.shape, lambda i: (0, 0))
    na, nb, nc = wa.shape[0], wb.shape[0], wc.shape[0]
    return _call_with_exchange(
        ex, body, name, (t // tm,),
        [row(d), row(d), row(d), row(d), row(3 * d), full(bg), full(wa), full(wb), full(wc), full(wo)],
        [row(d), row(d), row(d), row(3 * d), pl.BlockSpec((1, 3 * d), lambda i: (0, 0)), row(na), row(nb), row(nc)],
        [jax.ShapeDtypeStruct((t, d), BF16)] * 3
        + [jax.ShapeDtypeStruct((t, 3 * d), BF16), jax.ShapeDtypeStruct((1, 3 * d), F32),
           jax.ShapeDtypeStruct((t, na), F32), jax.ShapeDtypeStruct((t, nb), BF16),
           jax.ShapeDtypeStruct((t, nc), BF16)],
        [], (dx2, pa, pb, pc, zg, bg, wa, wb, wc, wo))


def _adamw_math(w, g, m, v):
    bc1 = 1.0 - ADAM_B1 ** ADAM_STEP
    bc2 = 1.0 - ADAM_B2 ** ADAM_STEP
    nm = ADAM_B1 * m + (1.0 - ADAM_B1) * g
    nv = ADAM_B2 * v + (1.0 - ADAM_B2) * (g * g)
    delta = -ADAM_LR * ((nm / bc1) / (jnp.sqrt(nv / bc2) + ADAM_EPS) + ADAM_WD * w)
    return delta, nm, nv


def _div_tile(n, cap, mult):
    best = None
    for cand in range(mult, min(n, cap) + 1, mult):
        if n % cand == 0:
            best = cand
    assert best is not None, (n, cap, mult)
    return best


def _adamw(w, g, m, v, name):
    rows, cols = w.shape
    tr = rows if rows * cols <= 256 * 1024 else _div_tile(rows, 256, 8)

    def body(w_ref, g_ref, m_ref, v_ref, d_ref, nm_ref, nv_ref):
        d_ref[...], nm_ref[...], nv_ref[...] = _adamw_math(w_ref[...], g_ref[...], m_ref[...], v_ref[...])

    blk = pl.BlockSpec((tr, cols), lambda i: (i, 0))
    return pl.pallas_call(
        body, name=name, grid=(rows // tr,), in_specs=[blk] * 4, out_specs=[blk] * 3,
        out_shape=[jax.ShapeDtypeStruct((rows, cols), F32)] * 3, compiler_params=_cparams())(w, g, m, v)


def _adamw_slots(w, slots, m, v, name):
    _, hr, cols = w.shape
    tr = _div_tile(hr, 128, 16)

    def body(w_ref, s_ref, m_ref, v_ref, g_ref, d_ref, nm_ref, nv_ref):
        g = s_ref[0, 0].astype(F32)
        for k in range(1, N_CHIPS):
            g = g + s_ref[0, k].astype(F32)
        g_ref[0] = g
        d_ref[0], nm_ref[0], nv_ref[0] = _adamw_math(w_ref[0], g, m_ref[0], v_ref[0])

    blk = pl.BlockSpec((1, tr, cols), lambda h, i: (h, i, 0))
    return pl.pallas_call(
        body, name=name, grid=(2, hr // tr),
        in_specs=[blk, pl.BlockSpec((1, N_CHIPS, tr, cols), lambda h, i: (h, 0, i, 0)), blk, blk],
        out_specs=[blk] * 4, out_shape=[jax.ShapeDtypeStruct((2, hr, cols), F32)] * 4,
        compiler_params=_cparams())(w, slots, m, v)


ANY = pl.BlockSpec(memory_space=pl.ANY)


def _place():
    x, y, c = lax.axis_index("x"), lax.axis_index("y"), lax.axis_index("c")
    other_chips = [(1 - x, y), (x, 1 - y), (1 - x, 1 - y)]
    return x, y, c, other_chips


def _remote(src, dst, send_sem, recv_sem, to):
    return pltpu.make_async_remote_copy(src_ref=src, dst_ref=dst, send_sem=send_sem, recv_sem=recv_sem,
                                        device_id=to, device_id_type=MESH)


PIECE_BYTES = 384 * 1024


def _row_pieces(half_rows, cols):
    for n in (4, 2):
        if half_rows % (16 * n) == 0 and half_rows * cols * 2 // n >= PIECE_BYTES:
            return [pl.ds(k * (half_rows // n), half_rows // n) for k in range(n)]
    return [pl.ds(0, half_rows)]


def _pieces(arrays, rows_axis):
    return [(w, rows) for w, a in enumerate(arrays) for rows in _row_pieces(a.shape[rows_axis], a.shape[-1])]


def _gather_exchange(shards):
    nw = len(shards)
    pieces = _pieces(shards, 1)
    npc = len(pieces)

    def build(s_refs, g_refs, sems):
        send_sems, recv_sems, local_sems = sems
        x, y, c, chips = _place()
        me = 2 * x + y
        sibling = (x, y, 1 - c)
        mine = [pltpu.make_async_copy(s_refs[w], g_refs[w].at[me], local_sems.at[w]) for w in range(nw)]
        first = [_remote(s_refs[w].at[c, rows], g_refs[w].at[me, c, rows], send_sems.at[k, p], recv_sems.at[k, p],
                         (cx, cy, c)) for k, (cx, cy) in enumerate(chips) for p, (w, rows) in enumerate(pieces)]

        def start():
            for cp in mine + first:
                cp.start()

        arrived = [g_refs[w].at[2 * cx + cy, c, rows] for cx, cy in chips for w, rows in pieces]
        passed = [_remote(slab, slab, send_sems.at[3 + q // npc, q % npc], recv_sems.at[3 + q // npc, q % npc], sibling)
                  for q, slab in enumerate(arrived)]

        def pass_on():
            for q, slab in enumerate(arrived):
                k, p = q // npc, q % npc
                _remote(slab, slab, send_sems.at[k, p], recv_sems.at[k, p], (*chips[k], c)).wait_recv()
                passed[q].start()

        def finish():
            for k, (cx, cy) in enumerate(chips):
                for p, (w, rows) in enumerate(pieces):
                    slab = g_refs[w].at[2 * cx + cy, 1 - c, rows]
                    _remote(slab, slab, send_sems.at[3 + k, p], recv_sems.at[3 + k, p], sibling).wait_recv()
            for cp in first + passed:
                cp.wait_send()
            for cp in mine:
                cp.wait()

        return start, pass_on, finish

    return _Exchange(list(shards), [jax.ShapeDtypeStruct((N_CHIPS,) + s.shape, BF16) for s in shards],
                     [pltpu.SemaphoreType.DMA((6, npc)), pltpu.SemaphoreType.DMA((6, npc)),
                      pltpu.SemaphoreType.DMA((nw,))], build)


def _swap_halves(grads, name):
    nw = len(grads)

    def body(*refs):
        g_refs, sib_refs = refs[:nw], refs[nw:2 * nw]
        send_sems, recv_sems = refs[2 * nw:]
        x, y, c, _ = _place()
        copies = [_remote(g_refs[w].at[s, 1 - c], sib_refs[w].at[s], send_sems.at[s, w], recv_sems.at[s, w],
                          (x, y, 1 - c)) for w in range(nw) for s in range(N_CHIPS)]
        for cp in copies:
            cp.start()
        for cp in copies:
            cp.wait_recv()
        for cp in copies:
            cp.wait_send()

    return pl.pallas_call(
        body, name=name, in_specs=[ANY] * nw, out_specs=[ANY] * nw,
        out_shape=[jax.ShapeDtypeStruct((N_CHIPS,) + g.shape[2:], BF16) for g in grads],
        scratch_shapes=[pltpu.SemaphoreType.DMA((N_CHIPS, nw)), pltpu.SemaphoreType.DMA((N_CHIPS, nw))],
    )(*grads)


def _pair_sum(grad, sib, core, name):
    nchip, _, hr, cols = grad.shape
    tr = _div_tile(hr, 256, 16)

    def body(core_ref, a_ref, b_ref, o_ref):
        o_ref[...] = (a_ref[0].astype(F32) + b_ref[...].astype(F32)).astype(BF16)

    return pl.pallas_call(
        body, name=name,
        grid_spec=pltpu.PrefetchScalarGridSpec(
            num_scalar_prefetch=1, grid=(nchip, hr // tr),
            in_specs=[pl.BlockSpec((1, 1, tr, cols), lambda s, i, core_r: (s, core_r[0], i, 0)),
                      pl.BlockSpec((1, tr, cols), lambda s, i, core_r: (s, i, 0))],
            out_specs=pl.BlockSpec((1, tr, cols), lambda s, i, core_r: (s, i, 0))),
        out_shape=jax.ShapeDtypeStruct((nchip, hr, cols), BF16), compiler_params=_cparams())(core, grad, sib)


def _pair_sum_exchange(sums):
    nw = len(sums)
    pieces = _pieces(sums, 1)
    npc = len(pieces)

    def build(p_refs, o_refs, sems):
        send_sems, recv_sems, local_sems = sems
        x, y, c, chips = _place()
        me = 2 * x + y
        sibling = (x, y, 1 - c)
        mine = [pltpu.make_async_copy(p_refs[w].at[me], o_refs[w].at[c, 3], local_sems.at[w]) for w in range(nw)]
        first = [_remote(p_refs[w].at[2 * cx + cy, rows], o_refs[w].at[c, k, rows], send_sems.at[k, p],
                         recv_sems.at[k, p], (cx, cy, c))
                 for k, (cx, cy) in enumerate(chips) for p, (w, rows) in enumerate(pieces)]

        def start():
            for cp in mine + first:
                cp.start()

        passed = [_remote(o_refs[w].at[c, k, rows], o_refs[w].at[c, k, rows], send_sems.at[3 + k, p],
                          recv_sems.at[3 + k, p], sibling) for k in range(N_CHIPS) for p, (w, rows) in enumerate(pieces)]

        def pass_on():
            for k in range(N_CHIPS):
                own_waited = set()
                for p, (w, rows) in enumerate(pieces):
                    if k < 3:
                        first[k * npc + p].wait_recv()
                    elif w not in own_waited:
                        mine[w].wait()
                        own_waited.add(w)
                    passed[k * npc + p].start()

        def finish():
            for k in range(N_CHIPS):
                for p, (w, rows) in enumerate(pieces):
                    slab = o_refs[w].at[1 - c, k, rows]
                    _remote(slab, slab, send_sems.at[3 + k, p], recv_sems.at[3 + k, p], sibling).wait_recv()
            for cp in first + passed:
                cp.wait_send()

        return start, pass_on, finish

    return _Exchange(list(sums), [jax.ShapeDtypeStruct((2,) + p.shape, BF16) for p in sums],
                     [pltpu.SemaphoreType.DMA((7, npc)), pltpu.SemaphoreType.DMA((7, npc)),
                      pltpu.SemaphoreType.DMA((nw,))], build)


def _small_sum_exchange(vec):
    m_per, n = vec.shape

    def build(ins, outs, scr):
        (x_ref,), (out_ref,) = ins, outs
        gath_ref, sum_ref, send_sems, recv_sems, local_sem, out_sem = scr
        x, y, c, chips = _place()
        me, sibling = (x, y, c), (x, y, 1 - c)

        def rows(px, py, pc):
            return gath_ref.at[pl.ds((4 * px + 2 * py + pc) * m_per, m_per), :]

        def copy(k, block, to, src=None):
            return pltpu.make_async_remote_copy(
                src_ref=rows(*block) if src is None else src, dst_ref=rows(*block),
                send_sem=send_sems.at[k], recv_sem=recv_sems.at[k], device_id=to, device_id_type=MESH)

        mine = pltpu.make_async_copy(x_ref, rows(*me), local_sem)
        first = [copy(0, me, sibling, src=x_ref)] + [copy(1 + j, me, (*chip, c), src=x_ref)
                                                     for j, chip in enumerate(chips)]

        def start():
            for cp in [mine] + first:
                cp.start()

        passed = [copy(4 + j, (*chip, c), sibling) for j, chip in enumerate(chips)]

        def pass_on():
            for j, chip in enumerate(chips):
                copy(1 + j, (*chip, c), me).wait_recv()
                passed[j].start()

        def finish():
            copy(0, sibling, me).wait_recv()
            for j, chip in enumerate(chips):
                copy(4 + j, (*chip, 1 - c), me).wait_recv()
            for cp in first + passed:
                cp.wait_send()
            mine.wait()
            acc = gath_ref[pl.ds(0, m_per), :]
            for k in range(1, N_DEV):
                acc = acc + gath_ref[pl.ds(k * m_per, m_per), :]
            sum_ref[...] = acc
            done = pltpu.make_async_copy(sum_ref, out_ref, out_sem)
            done.start()
            done.wait()

        return start, pass_on, finish

    return _Exchange([vec], [jax.ShapeDtypeStruct((m_per, n), F32)],
                     [pltpu.VMEM((N_DEV * m_per, n), F32), pltpu.VMEM((m_per, n), F32), pltpu.SemaphoreType.DMA((7,)),
                      pltpu.SemaphoreType.DMA((7,)), pltpu.SemaphoreType.DMA, pltpu.SemaphoreType.DMA], build)


def _pack_small(vals, tail=()):
    flat = jnp.concatenate([vals[name].reshape(-1).astype(F32) for name, _ in SMALL] + [v.reshape(1) for v in tail])
    flat = jnp.pad(flat, (0, SMALL_ROWS * LANES - flat.shape[0]))
    return flat.reshape(SMALL_ROWS, LANES)


def _unpack_small(packed):
    flat = packed.reshape(-1)
    out, off = {}, 0
    for name, shape in SMALL:
        n = int(np.prod(shape))
        out[name] = flat[off:off + n].reshape(shape)
        off += n
    return out


def _head_pad_cols(w, heads, real):
    k = w.shape[0]
    return jnp.pad(w.reshape(k, heads, real), ((0, 0), (0, 0), (0, LANES - real))).reshape(k, heads * LANES)


def _rope_tables(positions):
    half = MLA_ROPE // 2
    inv = ROPE_BASE ** (-jnp.arange(half, dtype=F32) / half)
    ang = positions.astype(F32)[:, None] * inv
    cos, sin = jnp.cos(ang), jnp.sin(ang)
    t = positions.shape[0]
    z = lambda n: jnp.zeros((t, n), F32)
    rc = jnp.concatenate([jnp.ones((t, MLA_NOPE), F32), cos, cos, z(LANES - MLA_QK)], axis=1)
    rs1 = jnp.concatenate([z(MLA_NOPE), -sin, z(LANES - MLA_NOPE - half)], axis=1)
    rs2 = jnp.concatenate([z(MLA_NOPE + half), sin, z(LANES - MLA_QK)], axis=1)
    return rc, rs1, rs2


FFN1_WEIGHTS = ("ffn1_w_gu", "ffn1_w_down")
FFN2_WEIGHTS = ("ffn2_w_gu", "ffn2_w_down")
MIXER_WEIGHTS = tuple(n for n, *_ in SHARDED if n not in FFN1_WEIGHTS + FFN2_WEIGHTS)
SHARD_SHAPE = {n: (r, c, kind) for n, r, c, kind in SHARDED}


def _from_blocks(name, gathered):
    r, c, kind = SHARD_SHAPE[name]
    blk = gathered.reshape(N_CHIPS, r, c)
    return blk, (blk.transpose(1, 0, 2).reshape(r, N_CHIPS * c) if kind == "col" else blk.reshape(N_CHIPS * r, c))


def _grad_pair_sums(names, gw, core, tag):
    by_owner = []
    for name in names:
        r, c, kind = SHARD_SHAPE[name]
        if gw[name].dtype == BF16:
            blk = gw[name]
        elif kind == "col":
            blk = gw[name].reshape(r, N_CHIPS, c).transpose(1, 0, 2)
        else:
            blk = gw[name].reshape(N_CHIPS, r, c)
        by_owner.append(blk.astype(BF16).reshape(N_CHIPS, 2, r // 2, c))
    received = _swap_halves(by_owner, "grad_swap_" + tag)
    return [_pair_sum(g, s, core, "pair_sum_" + n) for g, s, n in zip(by_owner, received, names)]


def _device_step(x, mem, positions, tgt, small, shards, core):
    d = D_MODEL
    g_ffn1, g_mix, g_ffn2 = small["ffn1_norm"], small["mix_norm"], small["ffn2_norm"]
    big = {}
    for name, g in zip(FFN1_WEIGHTS, _run_exchange(_gather_exchange([shards[n] for n in FFN1_WEIGHTS]), "gather_ffn1")):
        big[name + "#blocks"], big[name] = _from_blocks(name, g)
    wgu1, wd1 = big["ffn1_w_gu#blocks"], big["ffn1_w_down"].reshape(2, FF_TILE, d)
    x1, gpre1, upre1, h, *rest = _ffn_fwd(x, g_ffn1, wgu1, wd1, "ffn1_fwd", next_gain=g_mix,
                                          ex=_gather_exchange([shards[n] for n in MIXER_WEIGHTS]))
    for name, g in zip(MIXER_WEIGHTS, rest):
        big[name + "#blocks"], big[name] = _from_blocks(name, g)
    w_in = big["w_in"]
    w_uv_, w_cq, w_ckv = w_in[:, :COL_CQ], w_in[:, COL_CQ:COL_CKV], w_in[:, COL_CKV:COL_KR]
    w_kr = jnp.pad(w_in[:, COL_KR:COL_QM], ((0, 0), (MLA_NOPE, LANES - MLA_QK)))
    w_qm, w_g = w_in[:, COL_QM:COL_GATE], w_in[:, COL_GATE:]
    segs = (w_uv_, w_cq, w_ckv, w_kr, w_qm, w_g)
    wuq = _head_pad_cols(big["mla_w_uq"], MLA_HEADS, MLA_QK)
    ukv = big["mla_w_ukv"].reshape(MLA_KV_RANK, MLA_HEADS, 2, MLA_NOPE)
    wuk = _head_pad_cols(ukv[:, :, 0].reshape(MLA_KV_RANK, -1), MLA_HEADS, MLA_NOPE)
    wuv = _head_pad_cols(ukv[:, :, 1].reshape(MLA_KV_RANK, -1), MLA_HEADS, MLA_NOPE)
    wkv = big["mem_w_kv"]
    wa, wc, wo = big["w_branch_a"], big["w_branch_c"], big["w_out"]
    wb = jnp.pad(big["w_branch_b"].reshape(MLA_HEADS, MLA_NOPE, d),
                 ((0, 0), (0, LANES - MLA_NOPE), (0, 0))).reshape(MLA_HEADS * LANES, d)
    qg = jnp.pad(small["mla_q_norm"], ((0, 0), (0, LANES - MLA_QK)))
    kg = jnp.pad(small["mla_k_norm"], ((0, 0), (0, LANES - MLA_QK)))
    causal = jnp.tril(jnp.ones((CHUNK, CHUNK), bool))
    wt_f = jnp.where(causal[None], small["sg_w"][0], 0.0)
    wt, wt_t = wt_f.astype(BF16), wt_f.transpose(0, 2, 1).astype(BF16)
    bias_l = jnp.repeat(small["sg_b"][0].T, 64, axis=1)
    rc, rs1, rs2 = _rope_tables(positions)

    zuv, zcq, zckv, zkr, zqm, zg = _mm_cols(h, segs, [F32] * 5 + [BF16], "in_proj")
    ya = _sgu_fwd(zuv, small["sg_ln_g"], small["sg_ln_b"], wt, bias_l, "sgu_fwd")
    q, k, v, cqn, ckvn = _mla_prep_fwd(zcq, zckv, zkr, small["mla_cq_norm"], small["mla_ckv_norm"], qg, kg,
                                       wuq, wuk, wuv, rc, rs1, rs2, "mla_prep_fwd")
    yb, lse, *rest = _attn_fwd(q, k, v, "mla_attn_fwd", ex=_gather_exchange([shards[n] for n in FFN2_WEIGHTS]))
    for name, g in zip(FFN2_WEIGHTS, rest):
        big[name + "#blocks"], big[name] = _from_blocks(name, g)
    wgu2, wd2 = big["ffn2_w_gu#blocks"], big["ffn2_w_down"].reshape(2, FF_TILE, d)
    km, vm, memn = _mem_kv_fwd(mem, small["mem_norm"], wkv, small["mem_k_norm"], "mem_kv_fwd")
    yc = _mem_attn_fwd(zqm, small["mem_q_norm"], km, vm, "mem_attn_fwd")
    x2, merged, pa, pb, pc = _merge_fwd(x1, ya, yb, yc, zg, small["b_gate"], wa, wb, wc, wo, "merge_fwd")
    dy, loss_row, gpre2, upre2 = _ffn_fwd(x2, g_ffn2, wgu2, wd2, "ffn2_fwd", target=tgt)

    gw, gs, slots = {}, {}, {}

    def ffn_grads(prefix, xin, gain, dyin, gpre, upre, wgu, wd, ex=None, ex_names=(), last=False):
        dx, dgain, xn, dgt, dup, act, *got = _ffn_bwd(xin, gain, dyin, gpre, upre, wgu, wd, prefix + "_bwd", ex=ex)
        slots.update(zip(ex_names, got))
        gs[prefix + "_norm"] = dgain
        gw[prefix + "_w_gu"] = jnp.concatenate(
            [_mm_tn(xn, dgt, prefix + "_dwg", col_blocks=True, out_dtype=BF16),
             _mm_tn(xn, dup, prefix + "_dwu", col_blocks=True, out_dtype=BF16)], axis=0)
        rows_down = SHARD_SHAPE[prefix + "_w_down"][0]
        if last:
            small_sum = _small_sum_exchange(_pack_small(gs, tail=[loss_row[0, 0]]))
            dwd, summed = _mm_tn(act, dyin, prefix + "_dwd", scale=0.5, ex=small_sum, out_dtype=BF16)
            gw[prefix + "_w_down"] = dwd.reshape(N_CHIPS, rows_down, d)
            return dx, summed
        gw[prefix + "_w_down"] = _mm_tn(act, dyin, prefix + "_dwd", scale=0.5, out_dtype=BF16).reshape(
            N_CHIPS, rows_down, d)
        return dx

    dx2 = ffn_grads("ffn2", x2, g_ffn2, dy, gpre2, upre2, wgu2, wd2)
    ffn2_sums = _pair_sum_exchange(_grad_pair_sums(FFN2_WEIGHTS, gw, core, "ffn2"))
    dpa, dpb, dpc, dzg, dbg, dya, dyb, dyc, *got = _merge_bwd(dx2, pa, pb, pc, zg, small["b_gate"], wa, wb, wc, wo,
                                                              "merge_bwd", ex=ffn2_sums)
    slots.update(zip(FFN2_WEIGHTS, got))
    gs["b_gate"] = dbg
    gw["w_out"] = _mm_tn(merged, dx2, "dw_out")
    gw["w_branch_a"] = _mm_tn(ya, dpa, "dw_branch_a")
    gw["w_branch_b"] = _mm_tn(yb, dpb, "dw_branch_b").reshape(MLA_HEADS, LANES, d)[:, :MLA_NOPE].reshape(-1, d)
    gw["w_branch_c"] = _mm_tn(yc, dpc, "dw_branch_c")

    dzuv, dwt, dbl, dlg, dlb = _sgu_bwd(zuv, dya, small["sg_ln_g"], small["sg_ln_b"], wt, wt_t, bias_l, "sgu_bwd")
    gs["sg_w"], gs["sg_b"] = dwt[None], dbl[:, :SG_GROUPS].T[None]
    gs["sg_ln_g"], gs["sg_ln_b"] = dlg, dlb

    delta_rows, lse_rows = _attn_bwd_rows(yb, lse, dyb, "mla_attn_bwd_rows")
    dq, dk, dv = _attn_bwd(q, k, v, delta_rows, lse_rows, dyb, "mla_attn_bwd")
    dzcq, dzckv, dzkr, dql, dkl, dgcq, dgckv, dqg, dkg = _mla_prep_bwd(
        zcq, zckv, zkr, small["mla_cq_norm"], small["mla_ckv_norm"], qg, kg, wuq, wuk, wuv, rc, rs1, rs2,
        dq, dk, dv, "mla_prep_bwd")
    gs["mla_cq_norm"], gs["mla_ckv_norm"] = dgcq, dgckv
    gs["mla_q_norm"], gs["mla_k_norm"] = dqg[:, :MLA_QK], dkg[:, :MLA_QK]
    gw["mla_w_uq"] = _mm_tn(cqn, dql, "dw_uq").reshape(MLA_Q_RANK, MLA_HEADS, LANES)[:, :, :MLA_QK].reshape(
        MLA_Q_RANK, -1)
    dwuk = _mm_tn(ckvn, dkl, "dw_uk").reshape(MLA_KV_RANK, MLA_HEADS, LANES)[:, :, :MLA_NOPE]
    dwuv = _mm_tn(ckvn, dv, "dw_uv").reshape(MLA_KV_RANK, MLA_HEADS, LANES)[:, :, :MLA_NOPE]
    gw["mla_w_ukv"] = jnp.concatenate([dwuk, dwuv], axis=2).reshape(MLA_KV_RANK, -1)

    dzqm, dkn, dvm, dmqg = _mem_attn_bwd(zqm, dyc, small["mem_q_norm"], km, vm, "mem_attn_bwd")
    gs["mem_q_norm"] = dmqg
    gw["mem_w_kv"], gs["mem_k_norm"], gs["mem_norm"] = _mem_kv_bwd(
        mem, small["mem_norm"], wkv, small["mem_k_norm"], dkn, dvm, "mem_kv_bwd")

    dzs = (dzuv, dzcq, dzckv, dzkr, dzqm, dzg)
    dws = list(_mm_tn_cols(h, dzs[:5], "dw_in_narrow")) + [_mm_tn(h, dzg, "dw_in_gate")]
    dws[3] = dws[3][:, MLA_NOPE:MLA_QK]
    gw["w_in"] = jnp.concatenate(dws, axis=1)
    dx1, gs["mix_norm"] = _proj_norm_bwd(dzs, [w.T for w in segs], x1, g_mix, dx2, "in_proj_bwd")
    mixer_sums = _pair_sum_exchange(_grad_pair_sums(MIXER_WEIGHTS, gw, core, "mixer"))
    dx, summed = ffn_grads("ffn1", x, g_ffn1, dx1, gpre1, upre1, wgu1, wd1, ex=mixer_sums, ex_names=MIXER_WEIGHTS,
                           last=True)
    ffn1_sums = _pair_sum_exchange(_grad_pair_sums(FFN1_WEIGHTS, gw, core, "ffn1"))
    slots.update(zip(FFN1_WEIGHTS, _run_exchange(ffn1_sums, "grad_exchange_ffn1")))
    return dx, slots, summed


def kernel(x, mem, positions, ffn1_norm, ffn1_w_gu, ffn1_w_down, mix_norm, w_in, b_gate, sg_ln_g, sg_ln_b, sg_w, sg_b, mla_cq_norm, mla_w_uq, mla_ckv_norm, mla_w_ukv, mla_q_norm, mla_k_norm, mem_norm, mem_w_kv, mem_q_norm, mem_k_norm, w_branch_a, w_branch_b, w_branch_c, w_out, ffn2_norm, ffn2_w_gu, ffn2_w_down, loss_target, m_ffn1_norm, m_ffn1_w_gu, m_ffn1_w_down, m_mix_norm, m_w_in, m_b_gate, m_sg_ln_g, m_sg_ln_b, m_sg_w, m_sg_b, m_mla_cq_norm, m_mla_w_uq, m_mla_ckv_norm, m_mla_w_ukv, m_mla_q_norm, m_mla_k_norm, m_mem_norm, m_mem_w_kv, m_mem_q_norm, m_mem_k_norm, m_w_branch_a, m_w_branch_b, m_w_branch_c, m_w_out, m_ffn2_norm, m_ffn2_w_gu, m_ffn2_w_down, v_ffn1_norm, v_ffn1_w_gu, v_ffn1_w_down, v_mix_norm, v_w_in, v_b_gate, v_sg_ln_g, v_sg_ln_b, v_sg_w, v_sg_b, v_mla_cq_norm, v_mla_w_uq, v_mla_ckv_norm, v_mla_w_ukv, v_mla_q_norm, v_mla_k_norm, v_mem_norm, v_mem_w_kv, v_mem_q_norm, v_mem_k_norm, v_w_branch_a, v_w_branch_b, v_w_branch_c, v_w_out, v_ffn2_norm, v_ffn2_w_gu, v_ffn2_w_down):
    args = dict(locals())
    weights = {n: args[n] for n in WEIGHT_ORDER}
    mom_m = {n: args["m_" + n] for n in WEIGHT_ORDER}
    mom_v = {n: args["v_" + n] for n in WEIGHT_ORDER}
    small = {n: weights[n] for n, _ in SMALL}
    halves = lambda a, r, c: a.reshape(2, r // 2, c)

    shards = {n: halves(weights[n][0].astype(BF16), r, c) for n, r, c, _ in SHARDED}
    core = lax.axis_index("c").astype(jnp.int32).reshape(1)
    dx, slots, summed = _device_step(x[0], mem[0], positions[0], loss_target[0], small, shards, core)
    loss = summed.reshape(-1)[_N_SMALL]
    small_grads = _unpack_small(summed)

    grads, deltas, new_m, new_v = {}, {}, {}, {}
    for name, r, c, _ in SHARDED:
        outs = _adamw_slots(halves(weights[name][0], r, c), slots[name], halves(mom_m[name][0], r, c),
                            halves(mom_v[name][0], r, c), "adamw_" + name)
        shape = weights[name].shape
        grads[name], deltas[name], new_m[name], new_v[name] = [o.reshape(shape) for o in outs]
    dlt, nm, nv = _adamw(_pack_small(small), _pack_small(small_grads), _pack_small({n: mom_m[n] for n, _ in SMALL}),
                         _pack_small({n: mom_v[n] for n, _ in SMALL}), "adamw_small")
    for name, _ in SMALL:
        grads[name] = small_grads[name]
    deltas.update(_unpack_small(dlt))
    new_m.update(_unpack_small(nm))
    new_v.update(_unpack_small(nv))

    return (loss, dx[None], *[grads[n] for n in WEIGHT_ORDER], *[deltas[n] for n in WEIGHT_ORDER],
            *[new_m[n] for n in WEIGHT_ORDER], *[new_v[n] for n in WEIGHT_ORDER])
```

```python
import functools
from typing import Callable, NamedTuple

import numpy as np
import jax
import jax.numpy as jnp
from jax import lax
from jax.experimental import pallas as pl
from jax.experimental.pallas import tpu as pltpu

F32 = jnp.float32
BF16 = jnp.bfloat16

D_MODEL = 1024
D_FF = 2816
FF_TILE = 1408
SG_WIDTH = 512
SG_GROUPS = 8
CHUNK = 128
MLA_HEADS = 8
MLA_QK = 96
MLA_NOPE = 64
MLA_ROPE = 32
MLA_Q_RANK = 384
MLA_KV_RANK = 256
MEM_HEADS = 4
LANES = 128
EPS = 1e-6
NEG = -1e30
ROPE_BASE = 10000.0
N_CHIPS = 4
N_DEV = 8

ADAM_LR = 0.001
ADAM_B1 = 0.9
ADAM_B2 = 0.999
ADAM_EPS = 1e-08
ADAM_WD = 0.01
ADAM_STEP = 10

COL_CQ = 1024
COL_CKV = 1408
COL_KR = 1664
COL_QM = 1696
COL_GATE = 2208

VMEM_LIMIT_BYTES = 56 * 1024 * 1024
INV_SQRT2 = 0.7071067811865476
INV_SQRT_2PI = 0.3989422804014327
LOG2E = 1.4426950408889634
ATTN_SCALE = MLA_QK ** -0.5
V_ONES_LANE = 64
ATTN_SCALE2 = ATTN_SCALE * LOG2E

SHARDED = (
    ("ffn1_w_gu", 1024, 1408, "col"),
    ("ffn1_w_down", 704, 1024, "row"),
    ("w_in", 1024, 1320, "col"),
    ("mla_w_uq", 384, 192, "col"),
    ("mla_w_ukv", 256, 256, "col"),
    ("mem_w_kv", 256, 1024, "row"),
    ("w_branch_a", 512, 256, "col"),
    ("w_branch_b", 512, 256, "col"),
    ("w_branch_c", 512, 256, "col"),
    ("w_out", 256, 1024, "row"),
    ("ffn2_w_gu", 1024, 1408, "col"),
    ("ffn2_w_down", 704, 1024, "row"),
)
SMALL = (
    ("ffn1_norm", (1, 1024)), ("mix_norm", (1, 1024)), ("b_gate", (1, 3072)),
    ("sg_ln_g", (1, 512)), ("sg_ln_b", (1, 512)), ("sg_w", (1, 8, 128, 128)),
    ("sg_b", (1, 8, 128)), ("mla_cq_norm", (1, 384)), ("mla_ckv_norm", (1, 256)),
    ("mla_q_norm", (1, 96)), ("mla_k_norm", (1, 96)), ("mem_norm", (1, 1024)),
    ("mem_q_norm", (1, 128)), ("mem_k_norm", (1, 128)), ("ffn2_norm", (1, 1024)),
)
WEIGHT_ORDER = (
    "ffn1_norm", "ffn1_w_gu", "ffn1_w_down", "mix_norm", "w_in", "b_gate", "sg_ln_g", "sg_ln_b",
    "sg_w", "sg_b", "mla_cq_norm", "mla_w_uq", "mla_ckv_norm", "mla_w_ukv", "mla_q_norm",
    "mla_k_norm", "mem_norm", "mem_w_kv", "mem_q_norm", "mem_k_norm", "w_branch_a", "w_branch_b",
    "w_branch_c", "w_out", "ffn2_norm", "ffn2_w_gu", "ffn2_w_down",
)

_N_SMALL = sum(int(np.prod(s)) for _, s in SMALL)
SMALL_ROWS = -(-_N_SMALL // (LANES * 8)) * 8

MESH = pl.DeviceIdType.MESH


def _cparams():
    return pltpu.CompilerParams(vmem_limit_bytes=VMEM_LIMIT_BYTES)


def _dot(a, b):
    return jnp.dot(a, b, preferred_element_type=F32)


def _dot_nt(a, b):
    return lax.dot_general(a, b, (((1,), (1,)), ((), ())), preferred_element_type=F32)


def _dot_tn(a, b):
    return lax.dot_general(a, b, (((0,), (0,)), ((), ())), preferred_element_type=F32)


def _gelu(x):
    return 0.5 * x * (1.0 + lax.erf(x * INV_SQRT2))


def _gelu_grad(x):
    return 0.5 * (1.0 + lax.erf(x * INV_SQRT2)) + x * jnp.exp(-0.5 * x * x) * INV_SQRT_2PI


def _rstd(x, n):
    return lax.rsqrt(jnp.sum(x * x, axis=-1, keepdims=True) * (1.0 / n) + EPS)


def _rms_vjp(x, r, g, dy, n):
    dxh = dy * g
    dx = r * dxh - x * (r * r * r) * (jnp.sum(dxh * x, axis=-1, keepdims=True) * (1.0 / n))
    return dx, dy * x * r


def _row_tile(t, want):
    return min(t, want)


def _wide_tile(n):
    if n <= 1024:
        return n
    if n % 1024 == 0:
        return 1024
    assert n % FF_TILE == 0, n
    return FF_TILE


def _mm_cols(a, ws, out_dtypes, name, ex=None):
    t, kdim = a.shape
    tm = _row_tile(t, 512)
    n = len(ws)

    def body(*refs):
        av = refs[0][...]
        for w_ref, o_ref in zip(refs[1:1 + n], refs[1 + n:]):
            o_ref[...] = _dot(av, w_ref[...]).astype(o_ref.dtype)

    row = lambda width: pl.BlockSpec((tm, width), lambda i: (i, 0))
    return _call_with_exchange(
        ex, body, name, (t // tm,),
        [row(kdim)] + [pl.BlockSpec(w.shape, lambda i: (0, 0)) for w in ws],
        [row(w.shape[1]) for w in ws],
        [jax.ShapeDtypeStruct((t, w.shape[1]), dt) for w, dt in zip(ws, out_dtypes)], [], (a, *ws))


def _proj_norm_bwd(dzs, wts, x, g, dres, name):
    t, d = x.shape
    tm = _row_tile(t, 256)
    n = len(dzs)

    def body(*refs):
        x_ref, g_ref, r_ref, dx_ref, dg_ref = refs[2 * n:]

        @pl.when(pl.program_id(0) == 0)
        def _():
            dg_ref[...] = jnp.zeros_like(dg_ref)

        dh = None
        for dz_ref, w_ref in zip(refs[:n], refs[n:2 * n]):
            part = _dot(dz_ref[...], w_ref[...])
            dh = part if dh is None else dh + part
        xv = x_ref[...]
        dx, dgr = _rms_vjp(xv, _rstd(xv, d), g_ref[...], dh, d)
        dx_ref[...] = r_ref[...] + dx
        dg_ref[...] += jnp.sum(dgr, axis=0, keepdims=True)

    row = lambda width: pl.BlockSpec((tm, width), lambda i: (i, 0))
    vec = pl.BlockSpec((1, d), lambda i: (0, 0))
    return pl.pallas_call(
        body, name=name, grid=(t // tm,),
        in_specs=[row(dz.shape[1]) for dz in dzs] + [pl.BlockSpec(w.shape, lambda i: (0, 0)) for w in wts]
        + [row(d), vec, row(d)],
        out_specs=[row(d), vec],
        out_shape=[jax.ShapeDtypeStruct((t, d), F32), jax.ShapeDtypeStruct((1, d), F32)],
        compiler_params=_cparams())(*dzs, *wts, x, g, dres)


def _mm_tn_cols(a, bs, name):
    t, m = a.shape
    tk = _row_tile(t, 1024)
    n = len(bs)

    def body(*refs):
        @pl.when(pl.program_id(0) == 0)
        def _():
            for o_ref in refs[1 + n:]:
                o_ref[...] = jnp.zeros_like(o_ref)

        av = refs[0][...].astype(BF16)
        for b_ref, o_ref in zip(refs[1:1 + n], refs[1 + n:]):
            o_ref[...] += _dot_tn(av, b_ref[...].astype(BF16))

    row = lambda width: pl.BlockSpec((tk, width), lambda k: (k, 0))
    return pl.pallas_call(
        body, name=name, grid=(t // tk,), in_specs=[row(m)] + [row(b.shape[1]) for b in bs],
        out_specs=[pl.BlockSpec((m, b.shape[1]), lambda k: (0, 0)) for b in bs],
        out_shape=[jax.ShapeDtypeStruct((m, b.shape[1]), F32) for b in bs],
        compiler_params=_cparams())(a, *bs)


def _mm_tn(a, b, name, scale=1.0, ex=None, col_blocks=False, out_dtype=F32):
    t, m = a.shape
    n = b.shape[1]
    tm, tn = _wide_tile(m), _wide_tile(n)
    tk = _row_tile(t, 2048)
    nk = t // tk
    in_place = out_dtype == F32

    def body(a_ref, b_ref, o_ref, *scr):
        k = pl.program_id(2)
        acc_ref = o_ref if in_place else scr[0]

        @pl.when(k == 0)
        def _():
            acc_ref[...] = jnp.zeros_like(acc_ref)

        prod = _dot_tn(a_ref[...].astype(BF16), b_ref[...].astype(BF16))
        acc_ref[...] += prod.reshape(acc_ref.shape)
        if scale != 1.0 or not in_place:
            @pl.when(k == nk - 1)
            def _():
                o_ref[...] = (acc_ref[...] * scale).astype(out_dtype).reshape(o_ref.shape)

    if col_blocks:
        out_spec = pl.BlockSpec((1, tm, tn), lambda i, j, k: (j, i, 0))
        out_shape = jax.ShapeDtypeStruct((n // tn, m, tn), out_dtype)
    else:
        out_spec = pl.BlockSpec((tm, tn), lambda i, j, k: (i, j))
        out_shape = jax.ShapeDtypeStruct((m, n), out_dtype)
    outs = _call_with_exchange(
        ex, body, name, (m // tm, n // tn, nk),
        [pl.BlockSpec((tk, tm), lambda i, j, k: (k, i)), pl.BlockSpec((tk, tn), lambda i, j, k: (k, j))],
        [out_spec], [out_shape], [] if in_place else [pltpu.VMEM((tm, tn), F32)], (a, b))
    return outs[0] if ex is None else outs


PASS_ON_STEPS_BEFORE_END = 8


class _Exchange(NamedTuple):
    operands: list
    out_shapes: list
    sem_shapes: list
    build: Callable


def _call_with_exchange(ex, body, name, grid, in_specs, out_specs, out_shape, scratch_shapes, operands, prefetch=()):
    n_pre = len(prefetch)
    total = int(np.prod(grid))
    pass_step = max(total // 2, total - PASS_ON_STEPS_BEFORE_END)

    def call(kernel, ins, outs, shapes, scratch):
        if n_pre:
            spec = pltpu.PrefetchScalarGridSpec(num_scalar_prefetch=n_pre, grid=grid, in_specs=ins, out_specs=outs,
                                                scratch_shapes=scratch)
            return pl.pallas_call(kernel, name=name, grid_spec=spec, out_shape=shapes, compiler_params=_cparams())
        return pl.pallas_call(kernel, name=name, grid=grid, in_specs=ins, out_specs=outs, out_shape=shapes,
                              scratch_shapes=scratch, compiler_params=_cparams())

    if ex is None:
        return call(body, in_specs, out_specs, out_shape, scratch_shapes)(*prefetch, *operands)
    n_in, n_out, n_scr = len(in_specs), len(out_specs), len(scratch_shapes)
    k_in, k_out = len(ex.operands), len(ex.out_shapes)

    def carried(*refs):
        pre, refs = refs[:n_pre], refs[n_pre:]
        a, b = n_in, n_in + k_in
        c, e = b + n_out, b + n_out + k_out
        f = e + n_scr
        start, pass_on, finish = ex.build(refs[a:b], refs[c:e], refs[f:])
        step = functools.reduce(lambda lin, ax: lin * grid[ax] + pl.program_id(ax), range(len(grid)), 0)
        pl.when(step == 0)(start)
        body(*pre, *refs[:a], *refs[b:c], *refs[e:f])
        pl.when(step == pass_step)(pass_on)
        pl.when(step == total - 1)(finish)

    return call(carried, list(in_specs) + [ANY] * k_in, list(out_specs) + [ANY] * k_out,
                list(out_shape) + list(ex.out_shapes), list(scratch_shapes) + list(ex.sem_shapes),
                )(*prefetch, *operands, *ex.operands)


def _run_exchange(ex, name):
    k_in, k_out = len(ex.operands), len(ex.out_shapes)

    def body(*refs):
        start, pass_on, finish = ex.build(refs[:k_in], refs[k_in:k_in + k_out], refs[k_in + k_out:])
        start()
        pass_on()
        finish()

    return pl.pallas_call(body, name=name, in_specs=[ANY] * k_in, out_specs=[ANY] * k_out,
                          out_shape=list(ex.out_shapes), scratch_shapes=list(ex.sem_shapes))(*ex.operands)


def _ffn_fwd(x, g, wgu4, wd2, name, ex=None, next_gain=None, target=None):
    t, d = x.shape
    tm = _row_tile(t, 512)
    assert next_gain is None or target is None
    extra = [a for a in (next_gain, target) if a is not None]

    def body(*refs):
        x_ref, g_ref, wg_ref, wu_ref, wd_ref = refs[:5]
        e_ref = refs[5] if extra else None
        outs, (xn_scr, acc_scr) = refs[5 + len(extra):-2], refs[-2:]
        if target is not None:
            dy_ref, loss_ref, gg_ref, uu_ref = outs
        elif next_gain is not None:
            o_ref, gg_ref, uu_ref, h_ref = outs
        else:
            o_ref, gg_ref, uu_ref = outs
        i, j = pl.program_id(0), pl.program_id(1)

        @pl.when(j == 0)
        def _():
            xv = x_ref[...]
            xn_scr[...] = (xv * _rstd(xv, d) * g_ref[...]).astype(BF16)
            acc_scr[...] = jnp.zeros_like(acc_scr)

        if target is not None:
            @pl.when((i == 0) & (j == 0))
            def _():
                loss_ref[...] = jnp.zeros_like(loss_ref)

        xn = xn_scr[...]
        gg = _dot(xn, wg_ref[0])
        uu = _dot(xn, wu_ref[0])
        gg_ref[...] = gg.astype(BF16)
        uu_ref[...] = uu.astype(BF16)
        act = gg * jax.nn.sigmoid(gg) * uu
        acc_scr[...] += _dot(act.astype(BF16), wd_ref[0])

        @pl.when(j == 1)
        def _():
            y = x_ref[...] + 0.5 * acc_scr[...]
            if target is not None:
                e = y - e_ref[...]
                dy_ref[...] = e * (1.0 / d)
                part = 0.5 * jnp.sum(jnp.sum(e * e, axis=-1, keepdims=True) * (1.0 / d), axis=0, keepdims=True)
                loss_ref[...] += jnp.broadcast_to(part, loss_ref.shape)
            else:
                o_ref[...] = y
                if next_gain is not None:
                    h_ref[...] = (y * _rstd(y, d) * e_ref[...]).astype(BF16)

    row = pl.BlockSpec((tm, d), lambda i, j: (i, 0))
    vec = pl.BlockSpec((1, d), lambda i, j: (0, 0))
    ffb = pl.BlockSpec((tm, FF_TILE), lambda i, j: (i, j))
    f32_rows, bf16_ff = jax.ShapeDtypeStruct((t, d), F32), jax.ShapeDtypeStruct((t, D_FF), BF16)
    if target is not None:
        extra_spec, out_specs = [row], [row, pl.BlockSpec((1, LANES), lambda i, j: (0, 0)), ffb, ffb]
        out_shape = [f32_rows, jax.ShapeDtypeStruct((1, LANES), F32), bf16_ff, bf16_ff]
    elif next_gain is not None:
        extra_spec, out_specs = [vec], [row, ffb, ffb, row]
        out_shape = [f32_rows, bf16_ff, bf16_ff, jax.ShapeDtypeStruct((t, d), BF16)]
    else:
        extra_spec, out_specs, out_shape = [], [row, ffb, ffb], [f32_rows, bf16_ff, bf16_ff]
    return _call_with_exchange(
        ex, body, name, (t // tm, 2),
        [row, vec,
         pl.BlockSpec((1, d, FF_TILE), lambda i, j: (j, 0, 0)),
         pl.BlockSpec((1, d, FF_TILE), lambda i, j: (j + 2, 0, 0)),
         pl.BlockSpec((1, FF_TILE, d), lambda i, j: (j, 0, 0))] + extra_spec,
        out_specs, out_shape,
        [pltpu.VMEM((tm, d), BF16), pltpu.VMEM((tm, d), F32)], (x, g, wgu4, wgu4, wd2, *extra))


def _ffn_bwd(x, g, dy, gpre, upre, wgu4, wd2, name, ex=None):
    t, d = x.shape
    tm = _row_tile(t, 512)

    def body(dy_ref, gg_ref, uu_ref, wgu_hbm, wd_hbm, dg_ref, du_ref, act_ref, part_ref, wg_ref, wu_ref, wd_ref):
        j = pl.program_id(0)

        @pl.when(pl.program_id(1) == 0)
        def _():
            pltpu.sync_copy(wgu_hbm.at[j], wg_ref.at[0])
            pltpu.sync_copy(wgu_hbm.at[j + 2], wu_ref.at[0])
            pltpu.sync_copy(wd_hbm.at[j], wd_ref.at[0])

        gg = gg_ref[...].astype(F32)
        uu = uu_ref[...].astype(F32)
        sg = jax.nn.sigmoid(gg)
        silu = gg * sg
        act_ref[...] = (silu * uu).astype(BF16)
        dyh = (0.5 * dy_ref[...]).astype(BF16)
        dact = _dot_nt(dyh, wd_ref[0])
        du = (dact * silu).astype(BF16)
        dgt = (dact * uu * (sg * (1.0 + gg * (1.0 - sg)))).astype(BF16)
        du_ref[...] = du
        dg_ref[...] = dgt
        part_ref[0] = (_dot_nt(dgt, wg_ref[0]) + _dot_nt(du, wu_ref[0])).astype(BF16)

    row = pl.BlockSpec((tm, d), lambda j, i: (i, 0))
    ffb = pl.BlockSpec((tm, FF_TILE), lambda j, i: (i, j))
    dgt, dup, act, parts, *got = _call_with_exchange(
        ex, body, name, (2, t // tm),
        [row, ffb, ffb, ANY, ANY],
        [ffb, ffb, ffb, pl.BlockSpec((1, tm, d), lambda j, i: (j, i, 0))],
        [jax.ShapeDtypeStruct((t, D_FF), BF16)] * 3 + [jax.ShapeDtypeStruct((2, t, d), BF16)],
        [pltpu.VMEM((1, d, FF_TILE), BF16), pltpu.VMEM((1, d, FF_TILE), BF16), pltpu.VMEM((1, FF_TILE, d), BF16)],
        (dy, gpre, upre, wgu4, wd2))

    def norm_body(x_ref, g_ref, p_ref, dy_ref, dx_ref, dgain_ref, xn_ref):
        @pl.when(pl.program_id(0) == 0)
        def _():
            dgain_ref[...] = jnp.zeros_like(dgain_ref)

        xv = x_ref[...]
        r = _rstd(xv, d)
        xn_ref[...] = (xv * r * g_ref[...]).astype(BF16)
        dx, dgr = _rms_vjp(xv, r, g_ref[...], p_ref[0].astype(F32) + p_ref[1].astype(F32), d)
        dx_ref[...] = dy_ref[...] + dx
        dgain_ref[...] += jnp.sum(dgr, axis=0, keepdims=True)

    tn = _row_tile(t, 256)
    nrow = pl.BlockSpec((tn, d), lambda i: (i, 0))
    vec = pl.BlockSpec((1, d), lambda i: (0, 0))
    dx, dgain, xn = pl.pallas_call(
        norm_body, name=name + "_norm", grid=(t // tn,),
        in_specs=[nrow, vec, pl.BlockSpec((2, tn, d), lambda i: (0, i, 0)), nrow],
        out_specs=[nrow, vec, nrow],
        out_shape=[jax.ShapeDtypeStruct((t, d), F32), jax.ShapeDtypeStruct((1, d), F32),
                   jax.ShapeDtypeStruct((t, d), BF16)],
        compiler_params=_cparams())(x, g, parts, dy)
    return [dx, dgain, xn, dgt, dup, act] + got


def _sgu_layernorm(vpre, lg, lb):
    v = _gelu(vpre)
    mu = jnp.mean(v, axis=-1, keepdims=True)
    xc = v - mu
    rstd = lax.rsqrt(jnp.mean(xc * xc, axis=-1, keepdims=True) + EPS)
    xhat = xc * rstd
    return xhat, rstd, xhat * lg + lb


def _sgu_fwd(zuv, lg, lb, wt, bias_l, name):
    t = zuv.shape[0]
    tm = _row_tile(t, 512)

    def body(u_ref, v_ref, lg_ref, lb_ref, wt_ref, bl_ref, o_ref, vln_scr):
        _, _, vln = _sgu_layernorm(v_ref[...], lg_ref[...], lb_ref[...])
        vln_scr[...] = vln.astype(BF16)
        lo = lax.broadcasted_iota(jnp.int32, (CHUNK, LANES), 1) < 64
        for c in range(tm // CHUNK):
            rows = slice(c * CHUNK, (c + 1) * CHUNK)
            for p in range(SG_GROUPS // 2):
                cols = slice(p * LANES, (p + 1) * LANES)
                vp = vln_scr[rows, cols]
                mixed = jnp.where(lo, _dot(wt_ref[2 * p], vp), _dot(wt_ref[2 * p + 1], vp)) + bl_ref[:, cols]
                o_ref[rows, cols] = (_gelu(u_ref[rows, cols]) * mixed).astype(BF16)

    half = lambda k: pl.BlockSpec((tm, SG_WIDTH), lambda i: (i, k))
    vec = pl.BlockSpec((1, SG_WIDTH), lambda i: (0, 0))
    return pl.pallas_call(
        body, name=name, grid=(t // tm,),
        in_specs=[half(0), half(1), vec, vec,
                  pl.BlockSpec((SG_GROUPS, CHUNK, CHUNK), lambda i: (0, 0, 0)),
                  pl.BlockSpec((CHUNK, SG_WIDTH), lambda i: (0, 0))],
        out_specs=pl.BlockSpec((tm, SG_WIDTH), lambda i: (i, 0)),
        out_shape=jax.ShapeDtypeStruct((t, SG_WIDTH), BF16),
        scratch_shapes=[pltpu.VMEM((tm, SG_WIDTH), BF16)],
        compiler_params=_cparams())(zuv, zuv, lg, lb, wt, bias_l)


def _sgu_bwd(zuv, dya, lg, lb, wt, wt_t, bias_l, name):
    t = zuv.shape[0]
    tm = _row_tile(t, 256)
    nsteps = t // tm

    def body(u_ref, v_ref, dy_ref, lg_ref, lb_ref, wt_ref, wtt_ref, bl_ref,
             dz_ref, dwt_ref, dbl_ref, dlg_ref, dlb_ref, vln_scr, dvln_scr, dbacc_scr):
        step = pl.program_id(0)

        @pl.when(step == 0)
        def _():
            dwt_ref[...] = jnp.zeros_like(dwt_ref)
            dlg_ref[...] = jnp.zeros_like(dlg_ref)
            dlb_ref[...] = jnp.zeros_like(dlb_ref)
            dbl_ref[...] = jnp.zeros_like(dbl_ref)
            dbacc_scr[...] = jnp.zeros_like(dbacc_scr)

        vpre = v_ref[...]
        lgv = lg_ref[...]
        xhat, rstd, vln = _sgu_layernorm(vpre, lgv, lb_ref[...])
        vln_scr[...] = vln.astype(BF16)
        lo = lax.broadcasted_iota(jnp.int32, (CHUNK, LANES), 1) < 64
        for c in range(tm // CHUNK):
            rows = slice(c * CHUNK, (c + 1) * CHUNK)
            for p in range(SG_GROUPS // 2):
                cols = slice(p * LANES, (p + 1) * LANES)
                vp = vln_scr[rows, cols]
                mixed = jnp.where(lo, _dot(wt_ref[2 * p], vp), _dot(wt_ref[2 * p + 1], vp)) + bl_ref[:, cols]
                upre = u_ref[rows, cols]
                dyp = dy_ref[rows, cols]
                dz_ref[rows, cols] = (dyp * mixed * _gelu_grad(upre)).astype(BF16)
                dm = dyp * _gelu(upre)
                dbacc_scr[:, cols] += dm
                dlo = jnp.where(lo, dm, 0.0).astype(BF16)
                dhi = jnp.where(lo, 0.0, dm).astype(BF16)
                dvln_scr[rows, cols] = _dot(wtt_ref[2 * p], dlo) + _dot(wtt_ref[2 * p + 1], dhi)
                dwt_ref[2 * p] += _dot_nt(dlo, vp)
                dwt_ref[2 * p + 1] += _dot_nt(dhi, vp)
        dvln = dvln_scr[...]
        dlg_ref[...] += jnp.sum(dvln * xhat, axis=0, keepdims=True)
        dlb_ref[...] += jnp.sum(dvln, axis=0, keepdims=True)
        dxh = dvln * lgv
        dv = rstd * (dxh - jnp.mean(dxh, axis=-1, keepdims=True)
                     - xhat * jnp.mean(dxh * xhat, axis=-1, keepdims=True))
        dz_ref[:, SG_WIDTH:] = (dv * _gelu_grad(vpre)).astype(BF16)

        @pl.when(step == nsteps - 1)
        def _():
            rr = lax.broadcasted_iota(jnp.int32, (CHUNK, CHUNK), 0)
            cc = lax.broadcasted_iota(jnp.int32, (CHUNK, CHUNK), 1)
            tril = (cc <= rr).astype(F32)
            for gidx in range(SG_GROUPS):
                dwt_ref[gidx] = dwt_ref[gidx] * tril
            kk = lax.broadcasted_iota(jnp.int32, (SG_WIDTH, LANES), 0)
            gg = lax.broadcasted_iota(jnp.int32, (SG_WIDTH, LANES), 1)
            sel = ((kk // 64) == gg).astype(F32)
            dbl_ref[...] = jnp.dot(dbacc_scr[...], sel, preferred_element_type=F32,
                                   precision=lax.Precision.HIGHEST)

    half = lambda k: pl.BlockSpec((tm, SG_WIDTH), lambda i: (i, k))
    vec = pl.BlockSpec((1, SG_WIDTH), lambda i: (0, 0))
    wspec = pl.BlockSpec((SG_GROUPS, CHUNK, CHUNK), lambda i: (0, 0, 0))
    return pl.pallas_call(
        body, name=name, grid=(nsteps,),
        in_specs=[half(0), half(1), pl.BlockSpec((tm, SG_WIDTH), lambda i: (i, 0)), vec, vec,
                  wspec, wspec, pl.BlockSpec((CHUNK, SG_WIDTH), lambda i: (0, 0))],
        out_specs=[pl.BlockSpec((tm, 2 * SG_WIDTH), lambda i: (i, 0)), wspec,
                   pl.BlockSpec((CHUNK, LANES), lambda i: (0, 0)), vec, vec],
        out_shape=[jax.ShapeDtypeStruct((t, 2 * SG_WIDTH), BF16),
                   jax.ShapeDtypeStruct((SG_GROUPS, CHUNK, CHUNK), F32),
                   jax.ShapeDtypeStruct((CHUNK, LANES), F32),
                   jax.ShapeDtypeStruct((1, SG_WIDTH), F32), jax.ShapeDtypeStruct((1, SG_WIDTH), F32)],
        scratch_shapes=[pltpu.VMEM((tm, SG_WIDTH), BF16), pltpu.VMEM((tm, SG_WIDTH), F32),
                        pltpu.VMEM((CHUNK, SG_WIDTH), F32)],
        compiler_params=_cparams())(zuv, zuv, dya, lg, lb, wt, wt_t, bias_l)


def _rope(x, c, s1, s2):
    return x * c + pltpu.roll(x, LANES - 16, 1) * s1 + pltpu.roll(x, 16, 1) * s2


def _rope_t(dy, c, s1, s2):
    return dy * c + pltpu.roll(dy * s1, 16, 1) + pltpu.roll(dy * s2, LANES - 16, 1)


def _mla_prep_fwd(zcq, zckv, zkr, gcq, gckv, qg, kg, wuq, wuk, wuv, rc, rs1, rs2, name, ex=None):
    t = zcq.shape[0]
    tm = _row_tile(t, 256)
    hd = MLA_HEADS * LANES

    def body(zcq_ref, zckv_ref, zkr_ref, gcq_ref, gckv_ref, qg_ref, kg_ref, wuq_ref, wuk_ref, wuv_ref,
             c_ref, s1_ref, s2_ref, q_ref, k_ref, v_ref, cqn_ref, ckvn_ref):
        c, s1, s2 = c_ref[...], s1_ref[...], s2_ref[...]
        xq = zcq_ref[...]
        cqn = (xq * _rstd(xq, MLA_Q_RANK) * gcq_ref[...]).astype(BF16)
        cqn_ref[...] = cqn
        ql = _dot(cqn, wuq_ref[...])
        xk = zckv_ref[...]
        ckvn = (xk * _rstd(xk, MLA_KV_RANK) * gckv_ref[...]).astype(BF16)
        ckvn_ref[...] = ckvn
        kl = _dot(ckvn, wuk_ref[...])
        slot_lane = lax.broadcasted_iota(jnp.int32, (tm, hd), 1) % LANES
        v_ref[...] = jnp.where(slot_lane == V_ONES_LANE, 1.0, _dot(ckvn, wuv_ref[...])).astype(BF16)
        kr = zkr_ref[...]
        for h in range(MLA_HEADS):
            sl = slice(h * LANES, (h + 1) * LANES)
            qh = ql[:, sl]
            q_ref[:, sl] = (_rope(qh * _rstd(qh, MLA_QK) * qg_ref[...], c, s1, s2) * ATTN_SCALE2).astype(BF16)
            kh = kl[:, sl] + kr
            k_ref[:, sl] = _rope(kh * _rstd(kh, MLA_QK) * kg_ref[...], c, s1, s2).astype(BF16)

    row = lambda n: pl.BlockSpec((tm, n), lambda i: (i, 0))
    full = lambda a: pl.BlockSpec(a.shape, lambda i: (0, 0))
    return _call_with_exchange(
        ex, body, name, (t // tm,),
        [row(MLA_Q_RANK), row(MLA_KV_RANK), row(LANES), full(gcq), full(gckv), full(qg), full(kg),
         full(wuq), full(wuk), full(wuv), row(LANES), row(LANES), row(LANES)],
        [row(hd), row(hd), row(hd), row(MLA_Q_RANK), row(MLA_KV_RANK)],
        [jax.ShapeDtypeStruct((t, hd), BF16)] * 3
        + [jax.ShapeDtypeStruct((t, MLA_Q_RANK), BF16), jax.ShapeDtypeStruct((t, MLA_KV_RANK), BF16)],
        [], (zcq, zckv, zkr, gcq, gckv, qg, kg, wuq, wuk, wuv, rc, rs1, rs2))


def _mla_prep_bwd(zcq, zckv, zkr, gcq, gckv, qg, kg, wuq, wuk, wuv, rc, rs1, rs2, dq, dk, dv, name):
    t = zcq.shape[0]
    tm = _row_tile(t, 256)
    hd = MLA_HEADS * LANES

    def body(zcq_ref, zckv_ref, zkr_ref, gcq_ref, gckv_ref, qg_ref, kg_ref, wuq_ref, wuk_ref, wuv_ref,
             c_ref, s1_ref, s2_ref, dq_ref, dk_ref, dv_ref,
             dzcq_ref, dzckv_ref, dzkr_ref, dql_ref, dkl_ref, dgcq_ref, dgckv_ref, dqg_ref, dkg_ref):
        @pl.when(pl.program_id(0) == 0)
        def _():
            for ref in (dgcq_ref, dgckv_ref, dqg_ref, dkg_ref):
                ref[...] = jnp.zeros_like(ref)

        c, s1, s2 = c_ref[...], s1_ref[...], s2_ref[...]
        qgv, kgv = qg_ref[...], kg_ref[...]
        xq = zcq_ref[...]
        rq = _rstd(xq, MLA_Q_RANK)
        ql = _dot((xq * rq * gcq_ref[...]).astype(BF16), wuq_ref[...])
        xk = zckv_ref[...]
        rk = _rstd(xk, MLA_KV_RANK)
        kl = _dot((xk * rk * gckv_ref[...]).astype(BF16), wuk_ref[...])
        kr = zkr_ref[...]
        dqg_acc = jnp.zeros((tm, LANES), F32)
        dkg_acc = jnp.zeros((tm, LANES), F32)
        dkr = jnp.zeros((tm, LANES), F32)
        for h in range(MLA_HEADS):
            sl = slice(h * LANES, (h + 1) * LANES)
            qh = ql[:, sl]
            dqh, dgr = _rms_vjp(qh, _rstd(qh, MLA_QK), qgv, _rope_t(dq_ref[:, sl], c, s1, s2), MLA_QK)
            dql_ref[:, sl] = dqh.astype(BF16)
            dqg_acc += dgr
            kh = kl[:, sl] + kr
            dkh, dgr = _rms_vjp(kh, _rstd(kh, MLA_QK), kgv, _rope_t(dk_ref[:, sl], c, s1, s2), MLA_QK)
            dkl_ref[:, sl] = dkh.astype(BF16)
            dkg_acc += dgr
            dkr += dkh
        dqg_ref[...] += jnp.sum(dqg_acc, axis=0, keepdims=True)
        dkg_ref[...] += jnp.sum(dkg_acc, axis=0, keepdims=True)
        lane = lax.broadcasted_iota(jnp.int32, (tm, LANES), 1)
        dzkr_ref[...] = jnp.where((lane >= MLA_NOPE) & (lane < MLA_QK), dkr, 0.0).astype(BF16)
        dcqn = _dot_nt(dql_ref[...], wuq_ref[...])
        dx, dgr = _rms_vjp(xq, rq, gcq_ref[...], dcqn, MLA_Q_RANK)
        dzcq_ref[...] = dx.astype(BF16)
        dgcq_ref[...] += jnp.sum(dgr, axis=0, keepdims=True)
        dckvn = _dot_nt(dkl_ref[...], wuk_ref[...]) + _dot_nt(dv_ref[...].astype(BF16), wuv_ref[...])
        dx, dgr = _rms_vjp(xk, rk, gckv_ref[...], dckvn, MLA_KV_RANK)
        dzckv_ref[...] = dx.astype(BF16)
        dgckv_ref[...] += jnp.sum(dgr, axis=0, keepdims=True)

    row = lambda n: pl.BlockSpec((tm, n), lambda i: (i, 0))
    full = lambda a: pl.BlockSpec(a.shape, lambda i: (0, 0))
    vec = lambda n: pl.BlockSpec((1, n), lambda i: (0, 0))
    return pl.pallas_call(
        body, name=name, grid=(t // tm,),
        in_specs=[row(MLA_Q_RANK), row(MLA_KV_RANK), row(LANES), full(gcq), full(gckv), full(qg), full(kg),
                  full(wuq), full(wuk), full(wuv), row(LANES), row(LANES), row(LANES), row(hd), row(hd), row(hd)],
        out_specs=[row(MLA_Q_RANK), row(MLA_KV_RANK), row(LANES), row(hd), row(hd),
                   vec(MLA_Q_RANK), vec(MLA_KV_RANK), vec(LANES), vec(LANES)],
        out_shape=[jax.ShapeDtypeStruct((t, MLA_Q_RANK), BF16), jax.ShapeDtypeStruct((t, MLA_KV_RANK), BF16),
                   jax.ShapeDtypeStruct((t, LANES), BF16), jax.ShapeDtypeStruct((t, hd), BF16),
                   jax.ShapeDtypeStruct((t, hd), BF16), jax.ShapeDtypeStruct((1, MLA_Q_RANK), F32),
                   jax.ShapeDtypeStruct((1, MLA_KV_RANK), F32), jax.ShapeDtypeStruct((1, LANES), F32),
                   jax.ShapeDtypeStruct((1, LANES), F32)],
        compiler_params=_cparams(),
    )(zcq, zckv, zkr, gcq, gckv, qg, kg, wuq, wuk, wuv, rc, rs1, rs2, dq, dk, dv)


def _attn_tiles(t):
    tq = 512 if t >= 2048 else 128
    return tq, min(t, 4 * tq), min(t, 4 * tq)


def _causal_keep(tq, nk, i, j, tk):
    row = lax.broadcasted_iota(jnp.int32, (tq, nk), 0)
    col = lax.broadcasted_iota(jnp.int32, (tq, nk), 1)
    return (col - row) <= (i * tq - j * tk)


def _causal_keep_t(tq, nk, i, j, tk):
    key = lax.broadcasted_iota(jnp.int32, (nk, tq), 0)
    qry = lax.broadcasted_iota(jnp.int32, (nk, tq), 1)
    return (key - qry) <= (i * tq - j * tk)


ATTN_FWD_HEADS_PER_STEP = 2
ATTN_BWD_HEADS_PER_STEP = 2


def _attn_fwd(q, k, v, name, ex=None):
    t, hd = q.shape
    hp = ATTN_FWD_HEADS_PER_STEP
    tq, tk, _ = _attn_tiles(t)
    pairs = [(i, j) for i in range(t // tq) for j in range(((i + 1) * tq - 1) // tk + 1)]
    ii = np.array([p[0] for p in pairs], np.int32)
    jj = np.array([p[1] for p in pairs], np.int32)

    def body(ii_ref, jj_ref, q_ref, k_ref, v_ref, o_ref, lse_ref, m_scr, acc_scr):
        s_id = pl.program_id(1)
        i, j = ii_ref[s_id], jj_ref[s_id]
        last = j == ((i + 1) * tq - 1) // tk
        ones_lane = lax.broadcasted_iota(jnp.int32, (tq, LANES), 1) == V_ONES_LANE

        @pl.when(j == 0)
        def _():
            m_scr[...] = jnp.full_like(m_scr, NEG)
            acc_scr[...] = jnp.zeros_like(acc_scr)

        def step(masked, nk):
            scores = [_dot_nt(q_ref[:, hh * LANES:(hh + 1) * LANES], k_ref[:nk, hh * LANES:(hh + 1) * LANES])
                      for hh in range(hp)]
            for hh in range(hp):
                sl = slice(hh * LANES, (hh + 1) * LANES)
                s = scores[hh]
                if masked:
                    s = jnp.where(_causal_keep(tq, nk, i, j, tk), s, NEG)
                m_prev = m_scr[hh]
                m_new = jnp.maximum(m_prev, jnp.max(s, axis=1, keepdims=True))
                p = jnp.exp2(s - m_new)
                alpha = jnp.exp2(m_prev - m_new)
                acc = alpha * acc_scr[:, sl] + _dot(p.astype(BF16), v_ref[:nk, sl])
                if masked:
                    l_new = jnp.sum(jnp.where(ones_lane, acc, 0.0), axis=1, keepdims=True)
                    o_ref[:, sl] = (acc / l_new).astype(BF16)
                    lse_ref[:, sl] = jnp.broadcast_to(m_new + jnp.log(l_new) * LOG2E, (tq, LANES))
                else:
                    acc_scr[:, sl] = acc
                    m_scr[hh] = m_new

        @pl.when(jnp.logical_not(last))
        def _():
            step(False, tk)

        r = (((i + 1) * tq - 1) % tk) // tq
        for rr in range(tk // tq):
            @pl.when(last & (r == rr))
            def _():
                step(True, (rr + 1) * tq)

    w = hp * LANES
    qspec = pl.BlockSpec((tq, w), lambda h, s, ii_r, jj_r: (ii_r[s], h))
    kspec = pl.BlockSpec((tk, w), lambda h, s, ii_r, jj_r: (jj_r[s], h))
    return _call_with_exchange(
        ex, body, name, (hd // w, len(pairs)), [qspec, kspec, kspec], [qspec, qspec],
        [jax.ShapeDtypeStruct((t, hd), BF16), jax.ShapeDtypeStruct((t, hd), F32)],
        [pltpu.VMEM((hp, tq, 1), F32), pltpu.VMEM((tq, w), F32)], (q, k, v),
        prefetch=(jnp.asarray(ii), jnp.asarray(jj)))


def _attn_bwd_rows(o, lse, do, name):
    t, hd = o.shape
    heads = hd // LANES
    tm = _row_tile(t, 512)

    def body(o_ref, lse_ref, do_ref, out_ref):
        lane = lax.broadcasted_iota(jnp.int32, (tm, LANES), 1)
        acc = jnp.zeros((tm, LANES), F32)
        for h in range(heads):
            sl = slice(h * LANES, (h + 1) * LANES)
            delta = jnp.sum(do_ref[:, sl].astype(F32) * o_ref[:, sl].astype(F32), axis=1, keepdims=True)
            acc = jnp.where(lane == h, delta, acc)
            acc = jnp.where(lane == heads + h, lse_ref[:, sl], acc)
        out_ref[...] = acc

    row = pl.BlockSpec((tm, hd), lambda i: (i, 0))
    cols = pl.pallas_call(
        body, name=name, grid=(t // tm,), in_specs=[row, row, row],
        out_specs=pl.BlockSpec((tm, LANES), lambda i: (i, 0)),
        out_shape=jax.ShapeDtypeStruct((t, LANES), F32), compiler_params=_cparams())(o, lse, do)
    rows = cols.T
    return rows[:heads].reshape(heads, 1, t), rows[heads:2 * heads].reshape(heads, 1, t)


def _attn_bwd(q, k, v, delta_rows, lse_rows, do, name):
    t, hd = q.shape
    hp = ATTN_BWD_HEADS_PER_STEP
    tq, _, tk = _attn_tiles(t)
    nq = t // tq
    pairs = [(i, j) for j in range(t // tk) for i in range((j * tk) // tq, nq)]
    ii = np.array([p[0] for p in pairs], np.int32)
    jj = np.array([p[1] for p in pairs], np.int32)

    def body(jj_ref, ii_ref, q_ref, k_ref, v_ref, delta_ref, lse_ref, do_ref, dq_ref, dk_ref, dv_ref,
             dk_scr, dv_scr, dq_scr):
        s_id = pl.program_id(1)
        i, j = ii_ref[s_id], jj_ref[s_id]

        @pl.when(s_id == 0)
        def _():
            dq_scr[...] = jnp.zeros_like(dq_scr)

        @pl.when(i == (j * tk) // tq)
        def _():
            dk_scr[...] = jnp.zeros_like(dk_scr)
            dv_scr[...] = jnp.zeros_like(dv_scr)

        rows = pl.ds(pl.multiple_of(i * tq, tq), tq)

        def step(masked, nk):
            heads = [slice(hh * LANES, (hh + 1) * LANES) for hh in range(hp)]
            scores = [_dot_nt(k_ref[:nk, sl], q_ref[:, sl]) for sl in heads]
            for hh, sl in enumerate(heads):
                qv, kv, dov = q_ref[:, sl], k_ref[:nk, sl], do_ref[:, sl]
                st = scores[hh]
                if masked:
                    st = jnp.where(_causal_keep_t(tq, nk, i, j, tk), st, NEG)
                pt = jnp.exp2(st - lse_ref[hh])
                dv_scr[:nk, sl] += _dot(pt.astype(BF16), dov)
                dpt = _dot_nt(v_ref[:nk, sl], dov)
                dst = (pt * (dpt - delta_ref[hh]) * ATTN_SCALE).astype(BF16)
                dk_scr[:nk, sl] += _dot(dst, qv)
                dq_scr[rows, sl] += _dot_tn(dst, kv)

        seen = jnp.minimum((i + 1) * tq - j * tk, tk)
        for nk in range(tq, tk + 1, tq):
            @pl.when((seen == nk) & ((i + 1) * tq - j * tk <= tk))
            def _():
                step(True, nk)

        @pl.when((i + 1) * tq - j * tk > tk)
        def _():
            step(False, tk)

        @pl.when(i == nq - 1)
        def _():
            dk_ref[...] = (dk_scr[...] * (1.0 / ATTN_SCALE2)).astype(BF16)
            dv_ref[...] = dv_scr[...].astype(BF16)

        @pl.when(s_id == len(pairs) - 1)
        def _():
            dq_ref[...] = dq_scr[...].astype(BF16)

    w = hp * LANES
    qspec = pl.BlockSpec((tq, w), lambda h, s, jj_r, ii_r: (ii_r[s], h))
    kspec = pl.BlockSpec((tk, w), lambda h, s, jj_r, ii_r: (jj_r[s], h))
    rspec = pl.BlockSpec((hp, 1, tq), lambda h, s, jj_r, ii_r: (h, 0, ii_r[s]))
    return pl.pallas_call(
        body, name=name,
        grid_spec=pltpu.PrefetchScalarGridSpec(
            num_scalar_prefetch=2, grid=(hd // w, len(pairs)),
            in_specs=[qspec, kspec, kspec, rspec, rspec, qspec],
            out_specs=[pl.BlockSpec((t, w), lambda h, s, jj_r, ii_r: (0, h)), kspec, kspec],
            scratch_shapes=[pltpu.VMEM((tk, w), F32), pltpu.VMEM((tk, w), F32), pltpu.VMEM((t, w), F32)]),
        out_shape=[jax.ShapeDtypeStruct((t, hd), BF16)] * 3,
        compiler_params=_cparams())(jnp.asarray(jj), jnp.asarray(ii), q, k, v, delta_rows, lse_rows, do)


MEM_W = MEM_HEADS * LANES


def _mem_kv_fwd(mem, gmem, wkv, kg, name):
    m, d = mem.shape

    def body(mem_ref, g_ref, w_ref, kg_ref, k_ref, v_ref, mn_ref):
        xv = mem_ref[...]
        mn = (xv * _rstd(xv, d) * g_ref[...]).astype(BF16)
        mn_ref[...] = mn
        kvm = _dot(mn, w_ref[...])
        v_ref[...] = kvm[:, MEM_W:].astype(BF16)
        for h in range(MEM_HEADS):
            sl = slice(h * LANES, (h + 1) * LANES)
            kh = kvm[:, sl]
            k_ref[:, sl] = (kh * _rstd(kh, LANES) * kg_ref[...]).astype(BF16)

    full = lambda a: pl.BlockSpec(a.shape, lambda i: (0, 0))
    return pl.pallas_call(
        body, name=name, grid=(1,), in_specs=[full(mem), full(gmem), full(wkv), full(kg)],
        out_specs=[pl.BlockSpec((m, MEM_W), lambda i: (0, 0)), pl.BlockSpec((m, MEM_W), lambda i: (0, 0)),
                   pl.BlockSpec((m, d), lambda i: (0, 0))],
        out_shape=[jax.ShapeDtypeStruct((m, MEM_W), BF16), jax.ShapeDtypeStruct((m, MEM_W), BF16),
                   jax.ShapeDtypeStruct((m, d), BF16)],
        compiler_params=_cparams())(mem, gmem, wkv, kg)


def _mem_softmax(qn, kh):
    s = _dot_nt(qn, kh) * (LANES ** -0.5)
    e = jnp.exp(s - jnp.max(s, axis=1, keepdims=True))
    return e / jnp.sum(e, axis=1, keepdims=True)


def _mem_attn_fwd(zqm, qg, km, vm, name):
    t = zqm.shape[0]
    tm = _row_tile(t, 512)

    def body(q_ref, qg_ref, k_ref, v_ref, o_ref):
        for h in range(MEM_HEADS):
            sl = slice(h * LANES, (h + 1) * LANES)
            qh = q_ref[:, sl]
            qn = (qh * _rstd(qh, LANES) * qg_ref[...]).astype(BF16)
            p = _mem_softmax(qn, k_ref[:, sl])
            o_ref[:, sl] = _dot(p.astype(BF16), v_ref[:, sl]).astype(BF16)

    row = pl.BlockSpec((tm, MEM_W), lambda i: (i, 0))
    full = lambda a: pl.BlockSpec(a.shape, lambda i: (0, 0))
    return pl.pallas_call(
        body, name=name, grid=(t // tm,), in_specs=[row, full(qg), full(km), full(vm)], out_specs=row,
        out_shape=jax.ShapeDtypeStruct((t, MEM_W), BF16), compiler_params=_cparams())(zqm, qg, km, vm)


def _mem_attn_bwd(zqm, dyc, qg, km, vm, name):
    t = zqm.shape[0]
    m = km.shape[0]
    tm = _row_tile(t, 256)

    def body(q_ref, dy_ref, qg_ref, k_ref, v_ref, dz_ref, dk_ref, dv_ref, dqg_ref):
        @pl.when(pl.program_id(0) == 0)
        def _():
            dk_ref[...] = jnp.zeros_like(dk_ref)
            dv_ref[...] = jnp.zeros_like(dv_ref)
            dqg_ref[...] = jnp.zeros_like(dqg_ref)

        qgv = qg_ref[...]
        dqg_acc = jnp.zeros((tm, LANES), F32)
        for h in range(MEM_HEADS):
            sl = slice(h * LANES, (h + 1) * LANES)
            qh = q_ref[:, sl]
            r = _rstd(qh, LANES)
            qn = (qh * r * qgv).astype(BF16)
            kh = k_ref[:, sl]
            p = _mem_softmax(qn, kh)
            dov = dy_ref[:, sl]
            dv_ref[:, sl] += _dot_tn(p.astype(BF16), dov)
            dp = _dot_nt(dov, v_ref[:, sl])
            ds = (p * (dp - jnp.sum(dp * p, axis=1, keepdims=True)) * (LANES ** -0.5)).astype(BF16)
            dk_ref[:, sl] += _dot_tn(ds, qn)
            dqh, dgr = _rms_vjp(qh, r, qgv, _dot(ds, kh), LANES)
            dz_ref[:, sl] = dqh.astype(BF16)
            dqg_acc += dgr
        dqg_ref[...] += jnp.sum(dqg_acc, axis=0, keepdims=True)

    row = pl.BlockSpec((tm, MEM_W), lambda i: (i, 0))
    full = lambda a: pl.BlockSpec(a.shape, lambda i: (0, 0))
    acc = pl.BlockSpec((m, MEM_W), lambda i: (0, 0))
    return pl.pallas_call(
        body, name=name, grid=(t // tm,), in_specs=[row, row, full(qg), full(km), full(vm)],
        out_specs=[row, acc, acc, pl.BlockSpec((1, LANES), lambda i: (0, 0))],
        out_shape=[jax.ShapeDtypeStruct((t, MEM_W), BF16), jax.ShapeDtypeStruct((m, MEM_W), F32),
                   jax.ShapeDtypeStruct((m, MEM_W), F32), jax.ShapeDtypeStruct((1, LANES), F32)],
        compiler_params=_cparams())(zqm, dyc, qg, km, vm)


def _mem_kv_bwd(mem, gmem, wkv, kg, dkn, dvm, name):
    m, d = mem.shape

    def body(mem_ref, g_ref, w_ref, kg_ref, dk_ref, dv_ref, dw_ref, dkg_ref, dg_ref, dkv_scr):
        xv = mem_ref[...]
        r = _rstd(xv, d)
        mn = (xv * r * g_ref[...]).astype(BF16)
        kvm = _dot(mn, w_ref[...])
        dkv_scr[:, MEM_W:] = dv_ref[...].astype(BF16)
        dkg_acc = jnp.zeros((m, LANES), F32)
        for h in range(MEM_HEADS):
            sl = slice(h * LANES, (h + 1) * LANES)
            kh = kvm[:, sl]
            dkh, dgr = _rms_vjp(kh, _rstd(kh, LANES), kg_ref[...], dk_ref[:, sl], LANES)
            dkv_scr[:, sl] = dkh.astype(BF16)
            dkg_acc += dgr
        dkg_ref[...] = jnp.sum(dkg_acc, axis=0, keepdims=True)
        dkv = dkv_scr[...]
        dw_ref[...] = _dot_tn(mn, dkv)
        dmn = _dot_nt(dkv, w_ref[...])
        dg_ref[...] = jnp.sum(dmn * xv * r, axis=0, keepdims=True)

    full = lambda a: pl.BlockSpec(a.shape, lambda i: (0, 0))
    return pl.pallas_call(
        body, name=name, grid=(1,),
        in_specs=[full(mem), full(gmem), full(wkv), full(kg), full(dkn), full(dvm)],
        out_specs=[pl.BlockSpec((d, 2 * MEM_W), lambda i: (0, 0)), pl.BlockSpec((1, LANES), lambda i: (0, 0)),
                   pl.BlockSpec((1, d), lambda i: (0, 0))],
        out_shape=[jax.ShapeDtypeStruct((d, 2 * MEM_W), F32), jax.ShapeDtypeStruct((1, LANES), F32),
                   jax.ShapeDtypeStruct((1, d), F32)],
        scratch_shapes=[pltpu.VMEM((m, 2 * MEM_W), BF16)],
        compiler_params=_cparams())(mem, gmem, wkv, kg, dkn, dvm)


def _merge_fwd(x1, ya, yb, yc, zg, bg, wa, wb, wc, wo, name):
    t, d = x1.shape
    tm = _row_tile(t, 256)

    def body(x_ref, ya_ref, yb_ref, yc_ref, zg_ref, bg_ref, wa_ref, wb_ref, wc_ref, wo_ref,
             x2_ref, mg_ref, pa_ref, pb_ref, pc_ref):
        merged = None
        for k, (y_ref, w_ref, p_ref) in enumerate(
                ((ya_ref, wa_ref, pa_ref), (yb_ref, wb_ref, pb_ref), (yc_ref, wc_ref, pc_ref))):
            sl = slice(k * d, (k + 1) * d)
            pr = _dot(y_ref[...], w_ref[...])
            p_ref[...] = pr.astype(BF16)
            term = jax.nn.sigmoid(zg_ref[:, sl] + bg_ref[:, sl]) * pr
            merged = term if merged is None else merged + term
        mb = merged.astype(BF16)
        mg_ref[...] = mb
        x2_ref[...] = x_ref[...] + _dot(mb, wo_ref[...])

    row = lambda n: pl.BlockSpec((tm, n), lambda i: (i, 0))
    full = lambda a: pl.BlockSpec(a.shape, lambda i: (0, 0))
    return pl.pallas_call(
        body, name=name, grid=(t // tm,),
        in_specs=[row(d), row(ya.shape[1]), row(yb.shape[1]), row(yc.shape[1]), row(3 * d), full(bg),
                  full(wa), full(wb), full(wc), full(wo)],
        out_specs=[row(d)] * 5,
        out_shape=[jax.ShapeDtypeStruct((t, d), F32)] + [jax.ShapeDtypeStruct((t, d), BF16)] * 4,
        compiler_params=_cparams())(x1, ya, yb, yc, zg, bg, wa, wb, wc, wo)


def _merge_bwd(dx2, pa, pb, pc, zg, bg, wa, wb, wc, wo, name, ex=None):
    t, d = dx2.shape
    tm = _row_tile(t, 256)

    def body(dx_ref, pa_ref, pb_ref, pc_ref, zg_ref, bg_ref, wa_ref, wb_ref, wc_ref, wo_ref,
             dpa_ref, dpb_ref, dpc_ref, dzg_ref, dbg_ref, dya_ref, dyb_ref, dyc_ref):
        @pl.when(pl.program_id(0) == 0)
        def _():
            dbg_ref[...] = jnp.zeros_like(dbg_ref)

        dm = _dot_nt(dx_ref[...].astype(BF16), wo_ref[...])
        for k, (p_ref, w_ref, dp_ref, dy_ref) in enumerate(
                ((pa_ref, wa_ref, dpa_ref, dya_ref), (pb_ref, wb_ref, dpb_ref, dyb_ref),
                 (pc_ref, wc_ref, dpc_ref, dyc_ref))):
            sl = slice(k * d, (k + 1) * d)
            gate = jax.nn.sigmoid(zg_ref[:, sl] + bg_ref[:, sl])
            dpr = (dm * gate).astype(BF16)
            dp_ref[...] = dpr
            dzg = dm * p_ref[...].astype(F32) * gate * (1.0 - gate)
            dzg_ref[:, sl] = dzg.astype(BF16)
            dbg_ref[:, sl] += jnp.sum(dzg, axis=0, keepdims=True)
            dy_ref[...] = _dot_nt(dpr, w_ref[...]).astype(dy_ref.dtype)

    row = lambda n: pl.BlockSpec((tm, n), lambda i: (i, 0))
    full = lambda a: pl.BlockSpec(a.shape, lambda i: (0, 0))
    na, nb, nc = wa.shape[0], wb.shape[0], wc.shape[0]
    return _call_with_exchange(
        ex, body, name, (t // tm,),
        [row(d), row(d), row(d), row(d), row(3 * d), full(bg), full(wa), full(wb), full(wc), full(wo)],
        [row(d), row(d), row(d), row(3 * d), pl.BlockSpec((1, 3 * d), lambda i: (0, 0)), row(na), row(nb), row(nc)],
        [jax.ShapeDtypeStruct((t, d), BF16)] * 3
        + [jax.ShapeDtypeStruct((t, 3 * d), BF16), jax.ShapeDtypeStruct((1, 3 * d), F32),
           jax.ShapeDtypeStruct((t, na), F32), jax.ShapeDtypeStruct((t, nb), BF16),
           jax.ShapeDtypeStruct((t, nc), BF16)],
        [], (dx2, pa, pb, pc, zg, bg, wa, wb, wc, wo))


def _adamw_math(w, g, m, v):
    bc1 = 1.0 - ADAM_B1 ** ADAM_STEP
    bc2 = 1.0 - ADAM_B2 ** ADAM_STEP
    nm = ADAM_B1 * m + (1.0 - ADAM_B1) * g
    nv = ADAM_B2 * v + (1.0 - ADAM_B2) * (g * g)
    delta = -ADAM_LR * ((nm / bc1) / (jnp.sqrt(nv / bc2) + ADAM_EPS) + ADAM_WD * w)
    return delta, nm, nv


def _div_tile(n, cap, mult):
    best = None
    for cand in range(mult, min(n, cap) + 1, mult):
        if n % cand == 0:
            best = cand
    assert best is not None, (n, cap, mult)
    return best


def _adamw(w, g, m, v, name):
    rows, cols = w.shape
    tr = rows if rows * cols <= 256 * 1024 else _div_tile(rows, 256, 8)

    def body(w_ref, g_ref, m_ref, v_ref, d_ref, nm_ref, nv_ref):
        d_ref[...], nm_ref[...], nv_ref[...] = _adamw_math(w_ref[...], g_ref[...], m_ref[...], v_ref[...])

    blk = pl.BlockSpec((tr, cols), lambda i: (i, 0))
    return pl.pallas_call(
        body, name=name, grid=(rows // tr,), in_specs=[blk] * 4, out_specs=[blk] * 3,
        out_shape=[jax.ShapeDtypeStruct((rows, cols), F32)] * 3, compiler_params=_cparams())(w, g, m, v)


def _adamw_slots(w, slots, m, v, name):
    _, hr, cols = w.shape
    tr = _div_tile(hr, 128, 16)

    def body(w_ref, s_ref, m_ref, v_ref, g_ref, d_ref, nm_ref, nv_ref):
        g = s_ref[0, 0].astype(F32)
        for k in range(1, N_CHIPS):
            g = g + s_ref[0, k].astype(F32)
        g_ref[0] = g
        d_ref[0], nm_ref[0], nv_ref[0] = _adamw_math(w_ref[0], g, m_ref[0], v_ref[0])

    blk = pl.BlockSpec((1, tr, cols), lambda h, i: (h, i, 0))
    return pl.pallas_call(
        body, name=name, grid=(2, hr // tr),
        in_specs=[blk, pl.BlockSpec((1, N_CHIPS, tr, cols), lambda h, i: (h, 0, i, 0)), blk, blk],
        out_specs=[blk] * 4, out_shape=[jax.ShapeDtypeStruct((2, hr, cols), F32)] * 4,
        compiler_params=_cparams())(w, slots, m, v)


ANY = pl.BlockSpec(memory_space=pl.ANY)


def _place():
    x, y, c = lax.axis_index("x"), lax.axis_index("y"), lax.axis_index("c")
    other_chips = [(1 - x, y), (x, 1 - y), (1 - x, 1 - y)]
    return x, y, c, other_chips


def _remote(src, dst, send_sem, recv_sem, to):
    return pltpu.make_async_remote_copy(src_ref=src, dst_ref=dst, send_sem=send_sem, recv_sem=recv_sem,
                                        device_id=to, device_id_type=MESH)


PIECE_BYTES = 384 * 1024


def _row_pieces(half_rows, cols):
    for n in (4, 2):
        if half_rows % (16 * n) == 0 and half_rows * cols * 2 // n >= PIECE_BYTES:
            return [pl.ds(k * (half_rows // n), half_rows // n) for k in range(n)]
    return [pl.ds(0, half_rows)]


def _pieces(arrays, rows_axis):
    return [(w, rows) for w, a in enumerate(arrays) for rows in _row_pieces(a.shape[rows_axis], a.shape[-1])]


def _gather_exchange(shards):
    nw = len(shards)
    pieces = _pieces(shards, 1)
    npc = len(pieces)

    def build(s_refs, g_refs, sems):
        send_sems, recv_sems, local_sems = sems
        x, y, c, chips = _place()
        me = 2 * x + y
        sibling = (x, y, 1 - c)
        mine = [pltpu.make_async_copy(s_refs[w], g_refs[w].at[me], local_sems.at[w]) for w in range(nw)]
        first = [_remote(s_refs[w].at[c, rows], g_refs[w].at[me, c, rows], send_sems.at[k, p], recv_sems.at[k, p],
                         (cx, cy, c)) for k, (cx, cy) in enumerate(chips) for p, (w, rows) in enumerate(pieces)]

        def start():
            for cp in mine + first:
                cp.start()

        arrived = [g_refs[w].at[2 * cx + cy, c, rows] for cx, cy in chips for w, rows in pieces]
        passed = [_remote(slab, slab, send_sems.at[3 + q // npc, q % npc], recv_sems.at[3 + q // npc, q % npc], sibling)
                  for q, slab in enumerate(arrived)]

        def pass_on():
            for q, slab in enumerate(arrived):
                k, p = q // npc, q % npc
                _remote(slab, slab, send_sems.at[k, p], recv_sems.at[k, p], (*chips[k], c)).wait_recv()
                passed[q].start()

        def finish():
            for k, (cx, cy) in enumerate(chips):
                for p, (w, rows) in enumerate(pieces):
                    slab = g_refs[w].at[2 * cx + cy, 1 - c, rows]
                    _remote(slab, slab, send_sems.at[3 + k, p], recv_sems.at[3 + k, p], sibling).wait_recv()
            for cp in first + passed:
                cp.wait_send()
            for cp in mine:
                cp.wait()

        return start, pass_on, finish

    return _Exchange(list(shards), [jax.ShapeDtypeStruct((N_CHIPS,) + s.shape, BF16) for s in shards],
                     [pltpu.SemaphoreType.DMA((6, npc)), pltpu.SemaphoreType.DMA((6, npc)),
                      pltpu.SemaphoreType.DMA((nw,))], build)


def _swap_halves(grads, name):
    nw = len(grads)

    def body(*refs):
        g_refs, sib_refs = refs[:nw], refs[nw:2 * nw]
        send_sems, recv_sems = refs[2 * nw:]
        x, y, c, _ = _place()
        copies = [_remote(g_refs[w].at[s, 1 - c], sib_refs[w].at[s], send_sems.at[s, w], recv_sems.at[s, w],
                          (x, y, 1 - c)) for w in range(nw) for s in range(N_CHIPS)]
        for cp in copies:
            cp.start()
        for cp in copies:
            cp.wait_recv()
        for cp in copies:
            cp.wait_send()

    return pl.pallas_call(
        body, name=name, in_specs=[ANY] * nw, out_specs=[ANY] * nw,
        out_shape=[jax.ShapeDtypeStruct((N_CHIPS,) + g.shape[2:], BF16) for g in grads],
        scratch_shapes=[pltpu.SemaphoreType.DMA((N_CHIPS, nw)), pltpu.SemaphoreType.DMA((N_CHIPS, nw))],
    )(*grads)


def _pair_sum(grad, sib, core, name):
    nchip, _, hr, cols = grad.shape
    tr = _div_tile(hr, 256, 16)

    def body(core_ref, a_ref, b_ref, o_ref):
        o_ref[...] = (a_ref[0].astype(F32) + b_ref[...].astype(F32)).astype(BF16)

    return pl.pallas_call(
        body, name=name,
        grid_spec=pltpu.PrefetchScalarGridSpec(
            num_scalar_prefetch=1, grid=(nchip, hr // tr),
            in_specs=[pl.BlockSpec((1, 1, tr, cols), lambda s, i, core_r: (s, core_r[0], i, 0)),
                      pl.BlockSpec((1, tr, cols), lambda s, i, core_r: (s, i, 0))],
            out_specs=pl.BlockSpec((1, tr, cols), lambda s, i, core_r: (s, i, 0))),
        out_shape=jax.ShapeDtypeStruct((nchip, hr, cols), BF16), compiler_params=_cparams())(core, grad, sib)


def _pair_sum_exchange(sums):
    nw = len(sums)
    pieces = _pieces(sums, 1)
    npc = len(pieces)

    def build(p_refs, o_refs, sems):
        send_sems, recv_sems, local_sems = sems
        x, y, c, chips = _place()
        me = 2 * x + y
        sibling = (x, y, 1 - c)
        mine = [pltpu.make_async_copy(p_refs[w].at[me], o_refs[w].at[c, 3], local_sems.at[w]) for w in range(nw)]
        first = [_remote(p_refs[w].at[2 * cx + cy, rows], o_refs[w].at[c, k, rows], send_sems.at[k, p],
                         recv_sems.at[k, p], (cx, cy, c))
                 for k, (cx, cy) in enumerate(chips) for p, (w, rows) in enumerate(pieces)]

        def start():
            for cp in mine + first:
                cp.start()

        passed = [_remote(o_refs[w].at[c, k, rows], o_refs[w].at[c, k, rows], send_sems.at[3 + k, p],
                          recv_sems.at[3 + k, p], sibling) for k in range(N_CHIPS) for p, (w, rows) in enumerate(pieces)]

        def pass_on():
            for k in range(N_CHIPS):
                own_waited = set()
                for p, (w, rows) in enumerate(pieces):
                    if k < 3:
                        first[k * npc + p].wait_recv()
                    elif w not in own_waited:
                        mine[w].wait()
                        own_waited.add(w)
                    passed[k * npc + p].start()

        def finish():
            for k in range(N_CHIPS):
                for p, (w, rows) in enumerate(pieces):
                    slab = o_refs[w].at[1 - c, k, rows]
                    _remote(slab, slab, send_sems.at[3 + k, p], recv_sems.at[3 + k, p], sibling).wait_recv()
            for cp in first + passed:
                cp.wait_send()

        return start, pass_on, finish

    return _Exchange(list(sums), [jax.ShapeDtypeStruct((2,) + p.shape, BF16) for p in sums],
                     [pltpu.SemaphoreType.DMA((7, npc)), pltpu.SemaphoreType.DMA((7, npc)),
                      pltpu.SemaphoreType.DMA((nw,))], build)


def _small_sum_exchange(vec):
    m_per, n = vec.shape

    def build(ins, outs, scr):
        (x_ref,), (out_ref,) = ins, outs
        gath_ref, sum_ref, send_sems, recv_sems, local_sem, out_sem = scr
        x, y, c, chips = _place()
        me, sibling = (x, y, c), (x, y, 1 - c)

        def rows(px, py, pc):
            return gath_ref.at[pl.ds((4 * px + 2 * py + pc) * m_per, m_per), :]

        def copy(k, block, to, src=None):
            return pltpu.make_async_remote_copy(
                src_ref=rows(*block) if src is None else src, dst_ref=rows(*block),
                send_sem=send_sems.at[k], recv_sem=recv_sems.at[k], device_id=to, device_id_type=MESH)

        mine = pltpu.make_async_copy(x_ref, rows(*me), local_sem)
        first = [copy(0, me, sibling, src=x_ref)] + [copy(1 + j, me, (*chip, c), src=x_ref)
                                                     for j, chip in enumerate(chips)]

        def start():
            for cp in [mine] + first:
                cp.start()

        passed = [copy(4 + j, (*chip, c), sibling) for j, chip in enumerate(chips)]

        def pass_on():
            for j, chip in enumerate(chips):
                copy(1 + j, (*chip, c), me).wait_recv()
                passed[j].start()

        def finish():
            copy(0, sibling, me).wait_recv()
            for j, chip in enumerate(chips):
                copy(4 + j, (*chip, 1 - c), me).wait_recv()
            for cp in first + passed:
                cp.wait_send()
            mine.wait()
            acc = gath_ref[pl.ds(0, m_per), :]
            for k in range(1, N_DEV):
                acc = acc + gath_ref[pl.ds(k * m_per, m_per), :]
            sum_ref[...] = acc
            done = pltpu.make_async_copy(sum_ref, out_ref, out_sem)
            done.start()
            done.wait()

        return start, pass_on, finish

    return _Exchange([vec], [jax.ShapeDtypeStruct((m_per, n), F32)],
                     [pltpu.VMEM((N_DEV * m_per, n), F32), pltpu.VMEM((m_per, n), F32), pltpu.SemaphoreType.DMA((7,)),
                      pltpu.SemaphoreType.DMA((7,)), pltpu.SemaphoreType.DMA, pltpu.SemaphoreType.DMA], build)


def _pack_small(vals, tail=()):
    flat = jnp.concatenate([vals[name].reshape(-1).astype(F32) for name, _ in SMALL] + [v.reshape(1) for v in tail])
    flat = jnp.pad(flat, (0, SMALL_ROWS * LANES - flat.shape[0]))
    return flat.reshape(SMALL_ROWS, LANES)


def _unpack_small(packed):
    flat = packed.reshape(-1)
    out, off = {}, 0
    for name, shape in SMALL:
        n = int(np.prod(shape))
        out[name] = flat[off:off + n].reshape(shape)
        off += n
    return out


def _head_pad_cols(w, heads, real):
    k = w.shape[0]
    return jnp.pad(w.reshape(k, heads, real), ((0, 0), (0, 0), (0, LANES - real))).reshape(k, heads * LANES)


def _rope_tables(positions):
    half = MLA_ROPE // 2
    inv = ROPE_BASE ** (-jnp.arange(half, dtype=F32) / half)
    ang = positions.astype(F32)[:, None] * inv
    cos, sin = jnp.cos(ang), jnp.sin(ang)
    t = positions.shape[0]
    z = lambda n: jnp.zeros((t, n), F32)
    rc = jnp.concatenate([jnp.ones((t, MLA_NOPE), F32), cos, cos, z(LANES - MLA_QK)], axis=1)
    rs1 = jnp.concatenate([z(MLA_NOPE), -sin, z(LANES - MLA_NOPE - half)], axis=1)
    rs2 = jnp.concatenate([z(MLA_NOPE + half), sin, z(LANES - MLA_QK)], axis=1)
    return rc, rs1, rs2


FFN1_WEIGHTS = ("ffn1_w_gu", "ffn1_w_down")
FFN2_WEIGHTS = ("ffn2_w_gu", "ffn2_w_down")
MIXER_WEIGHTS = tuple(n for n, *_ in SHARDED if n not in FFN1_WEIGHTS + FFN2_WEIGHTS)
SHARD_SHAPE = {n: (r, c, kind) for n, r, c, kind in SHARDED}


def _from_blocks(name, gathered):
    r, c, kind = SHARD_SHAPE[name]
    blk = gathered.reshape(N_CHIPS, r, c)
    return blk, (blk.transpose(1, 0, 2).reshape(r, N_CHIPS * c) if kind == "col" else blk.reshape(N_CHIPS * r, c))


def _grad_pair_sums(names, gw, core, tag):
    by_owner = []
    for name in names:
        r, c, kind = SHARD_SHAPE[name]
        if gw[name].dtype == BF16:
            blk = gw[name]
        elif kind == "col":
            blk = gw[name].reshape(r, N_CHIPS, c).transpose(1, 0, 2)
        else:
            blk = gw[name].reshape(N_CHIPS, r, c)
        by_owner.append(blk.astype(BF16).reshape(N_CHIPS, 2, r // 2, c))
    received = _swap_halves(by_owner, "grad_swap_" + tag)
    return [_pair_sum(g, s, core, "pair_sum_" + n) for g, s, n in zip(by_owner, received, names)]


def _device_step(x, mem, positions, tgt, small, shards, core):
    d = D_MODEL
    g_ffn1, g_mix, g_ffn2 = small["ffn1_norm"], small["mix_norm"], small["ffn2_norm"]
    big = {}
    for name, g in zip(FFN1_WEIGHTS, _run_exchange(_gather_exchange([shards[n] for n in FFN1_WEIGHTS]), "gather_ffn1")):
        big[name + "#blocks"], big[name] = _from_blocks(name, g)
    wgu1, wd1 = big["ffn1_w_gu#blocks"], big["ffn1_w_down"].reshape(2, FF_TILE, d)
    x1, gpre1, upre1, h, *rest = _ffn_fwd(x, g_ffn1, wgu1, wd1, "ffn1_fwd", next_gain=g_mix,
                                          ex=_gather_exchange([shards[n] for n in MIXER_WEIGHTS]))
    for name, g in zip(MIXER_WEIGHTS, rest):
        big[name + "#blocks"], big[name] = _from_blocks(name, g)
    w_in = big["w_in"]
    w_uv_, w_cq, w_ckv = w_in[:, :COL_CQ], w_in[:, COL_CQ:COL_CKV], w_in[:, COL_CKV:COL_KR]
    w_kr = jnp.pad(w_in[:, COL_KR:COL_QM], ((0, 0), (MLA_NOPE, LANES - MLA_QK)))
    w_qm, w_g = w_in[:, COL_QM:COL_GATE], w_in[:, COL_GATE:]
    segs = (w_uv_, w_cq, w_ckv, w_kr, w_qm, w_g)
    wuq = _head_pad_cols(big["mla_w_uq"], MLA_HEADS, MLA_QK)
    ukv = big["mla_w_ukv"].reshape(MLA_KV_RANK, MLA_HEADS, 2, MLA_NOPE)
    wuk = _head_pad_cols(ukv[:, :, 0].reshape(MLA_KV_RANK, -1), MLA_HEADS, MLA_NOPE)
    wuv = _head_pad_cols(ukv[:, :, 1].reshape(MLA_KV_RANK, -1), MLA_HEADS, MLA_NOPE)
    wkv = big["mem_w_kv"]
    wa, wc, wo = big["w_branch_a"], big["w_branch_c"], big["w_out"]
    wb = jnp.pad(big["w_branch_b"].reshape(MLA_HEADS, MLA_NOPE, d),
                 ((0, 0), (0, LANES - MLA_NOPE), (0, 0))).reshape(MLA_HEADS * LANES, d)
    qg = jnp.pad(small["mla_q_norm"], ((0, 0), (0, LANES - MLA_QK)))
    kg = jnp.pad(small["mla_k_norm"], ((0, 0), (0, LANES - MLA_QK)))
    causal = jnp.tril(jnp.ones((CHUNK, CHUNK), bool))
    wt_f = jnp.where(causal[None], small["sg_w"][0], 0.0)
    wt, wt_t = wt_f.astype(BF16), wt_f.transpose(0, 2, 1).astype(BF16)
    bias_l = jnp.repeat(small["sg_b"][0].T, 64, axis=1)
    rc, rs1, rs2 = _rope_tables(positions)

    zuv, zcq, zckv, zkr, zqm, zg = _mm_cols(h, segs, [F32] * 5 + [BF16], "in_proj")
    ya = _sgu_fwd(zuv, small["sg_ln_g"], small["sg_ln_b"], wt, bias_l, "sgu_fwd")
    q, k, v, cqn, ckvn = _mla_prep_fwd(zcq, zckv, zkr, small["mla_cq_norm"], small["mla_ckv_norm"], qg, kg,
                                       wuq, wuk, wuv, rc, rs1, rs2, "mla_prep_fwd")
    yb, lse, *rest = _attn_fwd(q, k, v, "mla_attn_fwd", ex=_gather_exchange([shards[n] for n in FFN2_WEIGHTS]))
    for name, g in zip(FFN2_WEIGHTS, rest):
        big[name + "#blocks"], big[name] = _from_blocks(name, g)
    wgu2, wd2 = big["ffn2_w_gu#blocks"], big["ffn2_w_down"].reshape(2, FF_TILE, d)
    km, vm, memn = _mem_kv_fwd(mem, small["mem_norm"], wkv, small["mem_k_norm"], "mem_kv_fwd")
    yc = _mem_attn_fwd(zqm, small["mem_q_norm"], km, vm, "mem_attn_fwd")
    x2, merged, pa, pb, pc = _merge_fwd(x1, ya, yb, yc, zg, small["b_gate"], wa, wb, wc, wo, "merge_fwd")
    dy, loss_row, gpre2, upre2 = _ffn_fwd(x2, g_ffn2, wgu2, wd2, "ffn2_fwd", target=tgt)

    gw, gs, slots = {}, {}, {}

    def ffn_grads(prefix, xin, gain, dyin, gpre, upre, wgu, wd, ex=None, ex_names=(), last=False):
        dx, dgain, xn, dgt, dup, act, *got = _ffn_bwd(xin, gain, dyin, gpre, upre, wgu, wd, prefix + "_bwd", ex=ex)
        slots.update(zip(ex_names, got))
        gs[prefix + "_norm"] = dgain
        gw[prefix + "_w_gu"] = jnp.concatenate(
            [_mm_tn(xn, dgt, prefix + "_dwg", col_blocks=True, out_dtype=BF16),
             _mm_tn(xn, dup, prefix + "_dwu", col_blocks=True, out_dtype=BF16)], axis=0)
        rows_down = SHARD_SHAPE[prefix + "_w_down"][0]
        if last:
            small_sum = _small_sum_exchange(_pack_small(gs, tail=[loss_row[0, 0]]))
            dwd, summed = _mm_tn(act, dyin, prefix + "_dwd", scale=0.5, ex=small_sum, out_dtype=BF16)
            gw[prefix + "_w_down"] = dwd.reshape(N_CHIPS, rows_down, d)
            return dx, summed
        gw[prefix + "_w_down"] = _mm_tn(act, dyin, prefix + "_dwd", scale=0.5, out_dtype=BF16).reshape(
            N_CHIPS, rows_down, d)
        return dx

    dx2 = ffn_grads("ffn2", x2, g_ffn2, dy, gpre2, upre2, wgu2, wd2)
    ffn2_sums = _pair_sum_exchange(_grad_pair_sums(FFN2_WEIGHTS, gw, core, "ffn2"))
    dpa, dpb, dpc, dzg, dbg, dya, dyb, dyc, *got = _merge_bwd(dx2, pa, pb, pc, zg, small["b_gate"], wa, wb, wc, wo,
                                                              "merge_bwd", ex=ffn2_sums)
    slots.update(zip(FFN2_WEIGHTS, got))
    gs["b_gate"] = dbg
    gw["w_out"] = _mm_tn(merged, dx2, "dw_out")
    gw["w_branch_a"] = _mm_tn(ya, dpa, "dw_branch_a")
    gw["w_branch_b"] = _mm_tn(yb, dpb, "dw_branch_b").reshape(MLA_HEADS, LANES, d)[:, :MLA_NOPE].reshape(-1, d)
    gw["w_branch_c"] = _mm_tn(yc, dpc, "dw_branch_c")

    dzuv, dwt, dbl, dlg, dlb = _sgu_bwd(zuv, dya, small["sg_ln_g"], small["sg_ln_b"], wt, wt_t, bias_l, "sgu_bwd")
    gs["sg_w"], gs["sg_b"] = dwt[None], dbl[:, :SG_GROUPS].T[None]
    gs["sg_ln_g"], gs["sg_ln_b"] = dlg, dlb

    delta_rows, lse_rows = _attn_bwd_rows(yb, lse, dyb, "mla_attn_bwd_rows")
    dq, dk, dv = _attn_bwd(q, k, v, delta_rows, lse_rows, dyb, "mla_attn_bwd")
    dzcq, dzckv, dzkr, dql, dkl, dgcq, dgckv, dqg, dkg = _mla_prep_bwd(
        zcq, zckv, zkr, small["mla_cq_norm"], small["mla_ckv_norm"], qg, kg, wuq, wuk, wuv, rc, rs1, rs2,
        dq, dk, dv, "mla_prep_bwd")
    gs["mla_cq_norm"], gs["mla_ckv_norm"] = dgcq, dgckv
    gs["mla_q_norm"], gs["mla_k_norm"] = dqg[:, :MLA_QK], dkg[:, :MLA_QK]
    gw["mla_w_uq"] = _mm_tn(cqn, dql, "dw_uq").reshape(MLA_Q_RANK, MLA_HEADS, LANES)[:, :, :MLA_QK].reshape(
        MLA_Q_RANK, -1)
    dwuk = _mm_tn(ckvn, dkl, "dw_uk").reshape(MLA_KV_RANK, MLA_HEADS, LANES)[:, :, :MLA_NOPE]
    dwuv = _mm_tn(ckvn, dv, "dw_uv").reshape(MLA_KV_RANK, MLA_HEADS, LANES)[:, :, :MLA_NOPE]
    gw["mla_w_ukv"] = jnp.concatenate([dwuk, dwuv], axis=2).reshape(MLA_KV_RANK, -1)

    dzqm, dkn, dvm, dmqg = _mem_attn_bwd(zqm, dyc, small["mem_q_norm"], km, vm, "mem_attn_bwd")
    gs["mem_q_norm"] = dmqg
    gw["mem_w_kv"], gs["mem_k_norm"], gs["mem_norm"] = _mem_kv_bwd(
        mem, small["mem_norm"], wkv, small["mem_k_norm"], dkn, dvm, "mem_kv_bwd")

    dzs = (dzuv, dzcq, dzckv, dzkr, dzqm, dzg)
    dws = list(_mm_tn_cols(h, dzs[:5], "dw_in_narrow")) + [_mm_tn(h, dzg, "dw_in_gate")]
    dws[3] = dws[3][:, MLA_NOPE:MLA_QK]
    gw["w_in"] = jnp.concatenate(dws, axis=1)
    dx1, gs["mix_norm"] = _proj_norm_bwd(dzs, [w.T for w in segs], x1, g_mix, dx2, "in_proj_bwd")
    mixer_sums = _pair_sum_exchange(_grad_pair_sums(MIXER_WEIGHTS, gw, core, "mixer"))
    dx, summed = ffn_grads("ffn1", x, g_ffn1, dx1, gpre1, upre1, wgu1, wd1, ex=mixer_sums, ex_names=MIXER_WEIGHTS,
                           last=True)
    ffn1_sums = _pair_sum_exchange(_grad_pair_sums(FFN1_WEIGHTS, gw, core, "ffn1"))
    slots.update(zip(FFN1_WEIGHTS, _run_exchange(ffn1_sums, "grad_exchange_ffn1")))
    return dx, slots, summed


def kernel(x, mem, positions, ffn1_norm, ffn1_w_gu, ffn1_w_down, mix_norm, w_in, b_gate, sg_ln_g, sg_ln_b, sg_w, sg_b, mla_cq_norm, mla_w_uq, mla_ckv_norm, mla_w_ukv, mla_q_norm, mla_k_norm, mem_norm, mem_w_kv, mem_q_norm, mem_k_norm, w_branch_a, w_branch_b, w_branch_c, w_out, ffn2_norm, ffn2_w_gu, ffn2_w_down, loss_target, m_ffn1_norm, m_ffn1_w_gu, m_ffn1_w_down, m_mix_norm, m_w_in, m_b_gate, m_sg_ln_g, m_sg_ln_b, m_sg_w, m_sg_b, m_mla_cq_norm, m_mla_w_uq, m_mla_ckv_norm, m_mla_w_ukv, m_mla_q_norm, m_mla_k_norm, m_mem_norm, m_mem_w_kv, m_mem_q_norm, m_mem_k_norm, m_w_branch_a, m_w_branch_b, m_w_branch_c, m_w_out, m_ffn2_norm, m_ffn2_w_gu, m_ffn2_w_down, v_ffn1_norm, v_ffn1_w_gu, v_ffn1_w_down, v_mix_norm, v_w_in, v_b_gate, v_sg_ln_g, v_sg_ln_b, v_sg_w, v_sg_b, v_mla_cq_norm, v_mla_w_uq, v_mla_ckv_norm, v_mla_w_ukv, v_mla_q_norm, v_mla_k_norm, v_mem_norm, v_mem_w_kv, v_mem_q_norm, v_mem_k_norm, v_w_branch_a, v_w_branch_b, v_w_branch_c, v_w_out, v_ffn2_norm, v_ffn2_w_gu, v_ffn2_w_down):
    args = dict(locals())
    weights = {n: args[n] for n in WEIGHT_ORDER}
    mom_m = {n: args["m_" + n] for n in WEIGHT_ORDER}
    mom_v = {n: args["v_" + n] for n in WEIGHT_ORDER}
    small = {n: weights[n] for n, _ in SMALL}
    halves = lambda a, r, c: a.reshape(2, r // 2, c)

    shards = {n: halves(weights[n][0].astype(BF16), r, c) for n, r, c, _ in SHARDED}
    core = lax.axis_index("c").astype(jnp.int32).reshape(1)
    dx, slots, summed = _device_step(x[0], mem[0], positions[0], loss_target[0], small, shards, core)
    loss = summed.reshape(-1)[_N_SMALL]
    small_grads = _unpack_small(summed)

    grads, deltas, new_m, new_v = {}, {}, {}, {}
    for name, r, c, _ in SHARDED:
        outs = _adamw_slots(halves(weights[name][0], r, c), slots[name], halves(mom_m[name][0], r, c),
                            halves(mom_v[name][0], r, c), "adamw_" + name)
        shape = weights[name].shape
        grads[name], deltas[name], new_m[name], new_v[name] = [o.reshape(shape) for o in outs]
    dlt, nm, nv = _adamw(_pack_small(small), _pack_small(small_grads), _pack_small({n: mom_m[n] for n, _ in SMALL}),
                         _pack_small({n: mom_v[n] for n, _ in SMALL}), "adamw_small")
    for name, _ in SMALL:
        grads[name] = small_grads[name]
    deltas.update(_unpack_small(dlt))
    new_m.update(_unpack_small(nm))
    new_v.update(_unpack_small(nv))

    return (loss, dx[None], *[grads[n] for n in WEIGHT_ORDER], *[deltas[n] for n in WEIGHT_ORDER],
            *[new_m[n] for n in WEIGHT_ORDER], *[new_v[n] for n in WEIGHT_ORDER])
```

```python
import functools
from typing import Callable, NamedTuple

import numpy as np
import jax
import jax.numpy as jnp
from jax import lax
from jax.experimental import pallas as pl
from jax.experimental.pallas import tpu as pltpu

F32 = jnp.float32
BF16 = jnp.bfloat16

D_MODEL = 1024
D_FF = 2816
FF_TILE = 1408
SG_WIDTH = 512
SG_GROUPS = 8
CHUNK = 128
MLA_HEADS = 8
MLA_QK = 96
MLA_NOPE = 64
MLA_ROPE = 32
MLA_Q_RANK = 384
MLA_KV_RANK = 256
MEM_HEADS = 4
LANES = 128
EPS = 1e-6
NEG = -1e30
ROPE_BASE = 10000.0
N_CHIPS = 4
N_DEV = 8

ADAM_LR = 0.001
ADAM_B1 = 0.9
ADAM_B2 = 0.999
ADAM_EPS = 1e-08
ADAM_WD = 0.01
ADAM_STEP = 10

COL_CQ = 1024
COL_CKV = 1408
COL_KR = 1664
COL_QM = 1696
COL_GATE = 2208

VMEM_LIMIT_BYTES = 56 * 1024 * 1024
INV_SQRT2 = 0.7071067811865476
INV_SQRT_2PI = 0.3989422804014327
LOG2E = 1.4426950408889634
ATTN_SCALE = MLA_QK ** -0.5
V_ONES_LANE = 64
ATTN_SCALE2 = ATTN_SCALE * LOG2E

SHARDED = (
    ("ffn1_w_gu", 1024, 1408, "col"),
    ("ffn1_w_down", 704, 1024, "row"),
    ("w_in", 1024, 1320, "col"),
    ("mla_w_uq", 384, 192, "col"),
    ("mla_w_ukv", 256, 256, "col"),
    ("mem_w_kv", 256, 1024, "row"),
    ("w_branch_a", 512, 256, "col"),
    ("w_branch_b", 512, 256, "col"),
    ("w_branch_c", 512, 256, "col"),
    ("w_out", 256, 1024, "row"),
    ("ffn2_w_gu", 1024, 1408, "col"),
    ("ffn2_w_down", 704, 1024, "row"),
)
SMALL = (
    ("ffn1_norm", (1, 1024)), ("mix_norm", (1, 1024)), ("b_gate", (1, 3072)),
    ("sg_ln_g", (1, 512)), ("sg_ln_b", (1, 512)), ("sg_w", (1, 8, 128, 128)),
    ("sg_b", (1, 8, 128)), ("mla_cq_norm", (1, 384)), ("mla_ckv_norm", (1, 256)),
    ("mla_q_norm", (1, 96)), ("mla_k_norm", (1, 96)), ("mem_norm", (1, 1024)),
    ("mem_q_norm", (1, 128)), ("mem_k_norm", (1, 128)), ("ffn2_norm", (1, 1024)),
)
WEIGHT_ORDER = (
    "ffn1_norm", "ffn1_w_gu", "ffn1_w_down", "mix_norm", "w_in", "b_gate", "sg_ln_g", "sg_ln_b",
    "sg_w", "sg_b", "mla_cq_norm", "mla_w_uq", "mla_ckv_norm", "mla_w_ukv", "mla_q_norm",
    "mla_k_norm", "mem_norm", "mem_w_kv", "mem_q_norm", "mem_k_norm", "w_branch_a", "w_branch_b",
    "w_branch_c", "w_out", "ffn2_norm", "ffn2_w_gu", "ffn2_w_down",
)

_N_SMALL = sum(int(np.prod(s)) for _, s in SMALL)
SMALL_ROWS = -(-_N_SMALL // (LANES * 8)) * 8

MESH = pl.DeviceIdType.MESH


def _cparams():
    return pltpu.CompilerParams(vmem_limit_bytes=VMEM_LIMIT_BYTES)


def _dot(a, b):
    return jnp.dot(a, b, preferred_element_type=F32)


def _dot_nt(a, b):
    return lax.dot_general(a, b, (((1,), (1,)), ((), ())), preferred_element_type=F32)


def _dot_tn(a, b):
    return lax.dot_general(a, b, (((0,), (0,)), ((), ())), preferred_element_type=F32)


def _gelu(x):
    return 0.5 * x * (1.0 + lax.erf(x * INV_SQRT2))


def _gelu_grad(x):
    return 0.5 * (1.0 + lax.erf(x * INV_SQRT2)) + x * jnp.exp(-0.5 * x * x) * INV_SQRT_2PI


def _rstd(x, n):
    return lax.rsqrt(jnp.sum(x * x, axis=-1, keepdims=True) * (1.0 / n) + EPS)


def _rms_vjp(x, r, g, dy, n):
    dxh = dy * g
    dx = r * dxh - x * (r * r * r) * (jnp.sum(dxh * x, axis=-1, keepdims=True) * (1.0 / n))
    return dx, dy * x * r


def _row_tile(t, want):
    return min(t, want)


def _wide_tile(n):
    if n <= 1024:
        return n
    if n % 1024 == 0:
        return 1024
    assert n % FF_TILE == 0, n
    return FF_TILE


def _mm_cols(a, ws, out_dtypes, name, ex=None):
    t, kdim = a.shape
    tm = _row_tile(t, 512)
    n = len(ws)

    def body(*refs):
        av = refs[0][...]
        for w_ref, o_ref in zip(refs[1:1 + n], refs[1 + n:]):
            o_ref[...] = _dot(av, w_ref[...]).astype(o_ref.dtype)

    row = lambda width: pl.BlockSpec((tm, width), lambda i: (i, 0))
    return _call_with_exchange(
        ex, body, name, (t // tm,),
        [row(kdim)] + [pl.BlockSpec(w.shape, lambda i: (0, 0)) for w in ws],
        [row(w.shape[1]) for w in ws],
        [jax.ShapeDtypeStruct((t, w.shape[1]), dt) for w, dt in zip(ws, out_dtypes)], [], (a, *ws))


def _proj_norm_bwd(dzs, wts, x, g, dres, name):
    t, d = x.shape
    tm = _row_tile(t, 256)
    n = len(dzs)

    def body(*refs):
        x_ref, g_ref, r_ref, dx_ref, dg_ref = refs[2 * n:]

        @pl.when(pl.program_id(0) == 0)
        def _():
            dg_ref[...] = jnp.zeros_like(dg_ref)

        dh = None
        for dz_ref, w_ref in zip(refs[:n], refs[n:2 * n]):
            part = _dot(dz_ref[...], w_ref[...])
            dh = part if dh is None else dh + part
        xv = x_ref[...]
        dx, dgr = _rms_vjp(xv, _rstd(xv, d), g_ref[...], dh, d)
        dx_ref[...] = r_ref[...] + dx
        dg_ref[...] += jnp.sum(dgr, axis=0, keepdims=True)

    row = lambda width: pl.BlockSpec((tm, width), lambda i: (i, 0))
    vec = pl.BlockSpec((1, d), lambda i: (0, 0))
    return pl.pallas_call(
        body, name=name, grid=(t // tm,),
        in_specs=[row(dz.shape[1]) for dz in dzs] + [pl.BlockSpec(w.shape, lambda i: (0, 0)) for w in wts]
        + [row(d), vec, row(d)],
        out_specs=[row(d), vec],
        out_shape=[jax.ShapeDtypeStruct((t, d), F32), jax.ShapeDtypeStruct((1, d), F32)],
        compiler_params=_cparams())(*dzs, *wts, x, g, dres)


def _mm_tn_cols(a, bs, name):
    t, m = a.shape
    tk = _row_tile(t, 1024)
    n = len(bs)

    def body(*refs):
        @pl.when(pl.program_id(0) == 0)
        def _():
            for o_ref in refs[1 + n:]:
                o_ref[...] = jnp.zeros_like(o_ref)

        av = refs[0][...].astype(BF16)
        for b_ref, o_ref in zip(refs[1:1 + n], refs[1 + n:]):
            o_ref[...] += _dot_tn(av, b_ref[...].astype(BF16))

    row = lambda width: pl.BlockSpec((tk, width), lambda k: (k, 0))
    return pl.pallas_call(
        body, name=name, grid=(t // tk,), in_specs=[row(m)] + [row(b.shape[1]) for b in bs],
        out_specs=[pl.BlockSpec((m, b.shape[1]), lambda k: (0, 0)) for b in bs],
        out_shape=[jax.ShapeDtypeStruct((m, b.shape[1]), F32) for b in bs],
        compiler_params=_cparams())(a, *bs)


def _mm_tn(a, b, name, scale=1.0, ex=None, col_blocks=False, out_dtype=F32):
    t, m = a.shape
    n = b.shape[1]
    tm, tn = _wide_tile(m), _wide_tile(n)
    tk = _row_tile(t, 2048)
    nk = t // tk
    in_place = out_dtype == F32

    def body(a_ref, b_ref, o_ref, *scr):
        k = pl.program_id(2)
        acc_ref = o_ref if in_place else scr[0]

        @pl.when(k == 0)
        def _():
            acc_ref[...] = jnp.zeros_like(acc_ref)

        prod = _dot_tn(a_ref[...].astype(BF16), b_ref[...].astype(BF16))
        acc_ref[...] += prod.reshape(acc_ref.shape)
        if scale != 1.0 or not in_place:
            @pl.when(k == nk - 1)
            def _():
                o_ref[...] = (acc_ref[...] * scale).astype(out_dtype).reshape(o_ref.shape)

    if col_blocks:
        out_spec = pl.BlockSpec((1, tm, tn), lambda i, j, k: (j, i, 0))
        out_shape = jax.ShapeDtypeStruct((n // tn, m, tn), out_dtype)
    else:
        out_spec = pl.BlockSpec((tm, tn), lambda i, j, k: (i, j))
        out_shape = jax.ShapeDtypeStruct((m, n), out_dtype)
    outs = _call_with_exchange(
        ex, body, name, (m // tm, n // tn, nk),
        [pl.BlockSpec((tk, tm), lambda i, j, k: (k, i)), pl.BlockSpec((tk, tn), lambda i, j, k: (k, j))],
        [out_spec], [out_shape], [] if in_place else [pltpu.VMEM((tm, tn), F32)], (a, b))
    return outs[0] if ex is None else outs


PASS_ON_STEPS_BEFORE_END = 8


class _Exchange(NamedTuple):
    operands: list
    out_shapes: list
    sem_shapes: list
    build: Callable


def _call_with_exchange(ex, body, name, grid, in_specs, out_specs, out_shape, scratch_shapes, operands, prefetch=()):
    n_pre = len(prefetch)
    total = int(np.prod(grid))
    pass_step = max(total // 2, total - PASS_ON_STEPS_BEFORE_END)

    def call(kernel, ins, outs, shapes, scratch):
        if n_pre:
            spec = pltpu.PrefetchScalarGridSpec(num_scalar_prefetch=n_pre, grid=grid, in_specs=ins, out_specs=outs,
                                                scratch_shapes=scratch)
            return pl.pallas_call(kernel, name=name, grid_spec=spec, out_shape=shapes, compiler_params=_cparams())
        return pl.pallas_call(kernel, name=name, grid=grid, in_specs=ins, out_specs=outs, out_shape=shapes,
                              scratch_shapes=scratch, compiler_params=_cparams())

    if ex is None:
        return call(body, in_specs, out_specs, out_shape, scratch_shapes)(*prefetch, *operands)
    n_in, n_out, n_scr = len(in_specs), len(out_specs), len(scratch_shapes)
    k_in, k_out = len(ex.operands), len(ex.out_shapes)

    def carried(*refs):
        pre, refs = refs[:n_pre], refs[n_pre:]
        a, b = n_in, n_in + k_in
        c, e = b + n_out, b + n_out + k_out
        f = e + n_scr
        start, pass_on, finish = ex.build(refs[a:b], refs[c:e], refs[f:])
        step = functools.reduce(lambda lin, ax: lin * grid[ax] + pl.program_id(ax), range(len(grid)), 0)
        pl.when(step == 0)(start)
        body(*pre, *refs[:a], *refs[b:c], *refs[e:f])
        pl.when(step == pass_step)(pass_on)
        pl.when(step == total - 1)(finish)

    return call(carried, list(in_specs) + [ANY] * k_in, list(out_specs) + [ANY] * k_out,
                list(out_shape) + list(ex.out_shapes), list(scratch_shapes) + list(ex.sem_shapes),
                )(*prefetch, *operands, *ex.operands)


def _run_exchange(ex, name):
    k_in, k_out = len(ex.operands), len(ex.out_shapes)

    def body(*refs):
        start, pass_on, finish = ex.build(refs[:k_in], refs[k_in:k_in + k_out], refs[k_in + k_out:])
        start()
        pass_on()
        finish()

    return pl.pallas_call(body, name=name, in_specs=[ANY] * k_in, out_specs=[ANY] * k_out,
                          out_shape=list(ex.out_shapes), scratch_shapes=list(ex.sem_shapes))(*ex.operands)


def _ffn_fwd(x, g, wgu4, wd2, name, ex=None, next_gain=None, target=None):
    t, d = x.shape
    tm = _row_tile(t, 512)
    assert next_gain is None or target is None
    extra = [a for a in (next_gain, target) if a is not None]

    def body(*refs):
        x_ref, g_ref, wg_ref, wu_ref, wd_ref = refs[:5]
        e_ref = refs[5] if extra else None
        outs, (xn_scr, acc_scr) = refs[5 + len(extra):-2], refs[-2:]
        if target is not None:
            dy_ref, loss_ref, gg_ref, uu_ref = outs
        elif next_gain is not None:
            o_ref, gg_ref, uu_ref, h_ref = outs
        else:
            o_ref, gg_ref, uu_ref = outs
        i, j = pl.program_id(0), pl.program_id(1)

        @pl.when(j == 0)
        def _():
            xv = x_ref[...]
            xn_scr[...] = (xv * _rstd(xv, d) * g_ref[...]).astype(BF16)
            acc_scr[...] = jnp.zeros_like(acc_scr)

        if target is not None:
            @pl.when((i == 0) & (j == 0))
            def _():
                loss_ref[...] = jnp.zeros_like(loss_ref)

        xn = xn_scr[...]
        gg = _dot(xn, wg_ref[0])
        uu = _dot(xn, wu_ref[0])
        gg_ref[...] = gg.astype(BF16)
        uu_ref[...] = uu.astype(BF16)
        act = gg * jax.nn.sigmoid(gg) * uu
        acc_scr[...] += _dot(act.astype(BF16), wd_ref[0])

        @pl.when(j == 1)
        def _():
            y = x_ref[...] + 0.5 * acc_scr[...]
            if target is not None:
                e = y - e_ref[...]
                dy_ref[...] = e * (1.0 / d)
                part = 0.5 * jnp.sum(jnp.sum(e * e, axis=-1, keepdims=True) * (1.0 / d), axis=0, keepdims=True)
                loss_ref[...] += jnp.broadcast_to(part, loss_ref.shape)
            else:
                o_ref[...] = y
                if next_gain is not None:
                    h_ref[...] = (y * _rstd(y, d) * e_ref[...]).astype(BF16)

    row = pl.BlockSpec((tm, d), lambda i, j: (i, 0))
    vec = pl.BlockSpec((1, d), lambda i, j: (0, 0))
    ffb = pl.BlockSpec((tm, FF_TILE), lambda i, j: (i, j))
    f32_rows, bf16_ff = jax.ShapeDtypeStruct((t, d), F32), jax.ShapeDtypeStruct((t, D_FF), BF16)
    if target is not None:
        extra_spec, out_specs = [row], [row, pl.BlockSpec((1, LANES), lambda i, j: (0, 0)), ffb, ffb]
        out_shape = [f32_rows, jax.ShapeDtypeStruct((1, LANES), F32), bf16_ff, bf16_ff]
    elif next_gain is not None:
        extra_spec, out_specs = [vec], [row, ffb, ffb, row]
        out_shape = [f32_rows, bf16_ff, bf16_ff, jax.ShapeDtypeStruct((t, d), BF16)]
    else:
        extra_spec, out_specs, out_shape = [], [row, ffb, ffb], [f32_rows, bf16_ff, bf16_ff]
    return _call_with_exchange(
        ex, body, name, (t // tm, 2),
        [row, vec,
         pl.BlockSpec((1, d, FF_TILE), lambda i, j: (j, 0, 0)),
         pl.BlockSpec((1, d, FF_TILE), lambda i, j: (j + 2, 0, 0)),
         pl.BlockSpec((1, FF_TILE, d), lambda i, j: (j, 0, 0))] + extra_spec,
        out_specs, out_shape,
        [pltpu.VMEM((tm, d), BF16), pltpu.VMEM((tm, d), F32)], (x, g, wgu4, wgu4, wd2, *extra))


def _ffn_bwd(x, g, dy, gpre, upre, wgu4, wd2, name, ex=None):
    t, d = x.shape
    tm = _row_tile(t, 512)

    def body(dy_ref, gg_ref, uu_ref, wgu_hbm, wd_hbm, dg_ref, du_ref, act_ref, part_ref, wg_ref, wu_ref, wd_ref):
        j = pl.program_id(0)

        @pl.when(pl.program_id(1) == 0)
        def _():
            pltpu.sync_copy(wgu_hbm.at[j], wg_ref.at[0])
            pltpu.sync_copy(wgu_hbm.at[j + 2], wu_ref.at[0])
            pltpu.sync_copy(wd_hbm.at[j], wd_ref.at[0])

        gg = gg_ref[...].astype(F32)
        uu = uu_ref[...].astype(F32)
        sg = jax.nn.sigmoid(gg)
        silu = gg * sg
        act_ref[...] = (silu * uu).astype(BF16)
        dyh = (0.5 * dy_ref[...]).astype(BF16)
        dact = _dot_nt(dyh, wd_ref[0])
        du = (dact * silu).astype(BF16)
        dgt = (dact * uu * (sg * (1.0 + gg * (1.0 - sg)))).astype(BF16)
        du_ref[...] = du
        dg_ref[...] = dgt
        part_ref[0] = (_dot_nt(dgt, wg_ref[0]) + _dot_nt(du, wu_ref[0])).astype(BF16)

    row = pl.BlockSpec((tm, d), lambda j, i: (i, 0))
    ffb = pl.BlockSpec((tm, FF_TILE), lambda j, i: (i, j))
    dgt, dup, act, parts, *got = _call_with_exchange(
        ex, body, name, (2, t // tm),
        [row, ffb, ffb, ANY, ANY],
        [ffb, ffb, ffb, pl.BlockSpec((1, tm, d), lambda j, i: (j, i, 0))],
        [jax.ShapeDtypeStruct((t, D_FF), BF16)] * 3 + [jax.ShapeDtypeStruct((2, t, d), BF16)],
        [pltpu.VMEM((1, d, FF_TILE), BF16), pltpu.VMEM((1, d, FF_TILE), BF16), pltpu.VMEM((1, FF_TILE, d), BF16)],
        (dy, gpre, upre, wgu4, wd2))

    def norm_body(x_ref, g_ref, p_ref, dy_ref, dx_ref, dgain_ref, xn_ref):
        @pl.when(pl.program_id(0) == 0)
        def _():
            dgain_ref[...] = jnp.zeros_like(dgain_ref)

        xv = x_ref[...]
        r = _rstd(xv, d)
        xn_ref[...] = (xv * r * g_ref[...]).astype(BF16)
        dx, dgr = _rms_vjp(xv, r, g_ref[...], p_ref[0].astype(F32) + p_ref[1].astype(F32), d)
        dx_ref[...] = dy_ref[...] + dx
        dgain_ref[...] += jnp.sum(dgr, axis=0, keepdims=True)

    tn = _row_tile(t, 256)
    nrow = pl.BlockSpec((tn, d), lambda i: (i, 0))
    vec = pl.BlockSpec((1, d), lambda i: (0, 0))
    dx, dgain, xn = pl.pallas_call(
        norm_body, name=name + "_norm", grid=(t // tn,),
        in_specs=[nrow, vec, pl.BlockSpec((2, tn, d), lambda i: (0, i, 0)), nrow],
        out_specs=[nrow, vec, nrow],
        out_shape=[jax.ShapeDtypeStruct((t, d), F32), jax.ShapeDtypeStruct((1, d), F32),
                   jax.ShapeDtypeStruct((t, d), BF16)],
        compiler_params=_cparams())(x, g, parts, dy)
    return [dx, dgain, xn, dgt, dup, act] + got


def _sgu_layernorm(vpre, lg, lb):
    v = _gelu(vpre)
    mu = jnp.mean(v, axis=-1, keepdims=True)
    xc = v - mu
    rstd = lax.rsqrt(jnp.mean(xc * xc, axis=-1, keepdims=True) + EPS)
    xhat = xc * rstd
    return xhat, rstd, xhat * lg + lb


def _sgu_fwd(zuv, lg, lb, wt, bias_l, name):
    t = zuv.shape[0]
    tm = _row_tile(t, 512)

    def body(u_ref, v_ref, lg_ref, lb_ref, wt_ref, bl_ref, o_ref, vln_scr):
        _, _, vln = _sgu_layernorm(v_ref[...], lg_ref[...], lb_ref[...])
        vln_scr[...] = vln.astype(BF16)
        lo = lax.broadcasted_iota(jnp.int32, (CHUNK, LANES), 1) < 64
        for c in range(tm // CHUNK):
            rows = slice(c * CHUNK, (c + 1) * CHUNK)
            for p in range(SG_GROUPS // 2):
                cols = slice(p * LANES, (p + 1) * LANES)
                vp = vln_scr[rows, cols]
                mixed = jnp.where(lo, _dot(wt_ref[2 * p], vp), _dot(wt_ref[2 * p + 1], vp)) + bl_ref[:, cols]
                o_ref[rows, cols] = (_gelu(u_ref[rows, cols]) * mixed).astype(BF16)

    half = lambda k: pl.BlockSpec((tm, SG_WIDTH), lambda i: (i, k))
    vec = pl.BlockSpec((1, SG_WIDTH), lambda i: (0, 0))
    return pl.pallas_call(
        body, name=name, grid=(t // tm,),
        in_specs=[half(0), half(1), vec, vec,
                  pl.BlockSpec((SG_GROUPS, CHUNK, CHUNK), lambda i: (0, 0, 0)),
                  pl.BlockSpec((CHUNK, SG_WIDTH), lambda i: (0, 0))],
        out_specs=pl.BlockSpec((tm, SG_WIDTH), lambda i: (i, 0)),
        out_shape=jax.ShapeDtypeStruct((t, SG_WIDTH), BF16),
        scratch_shapes=[pltpu.VMEM((tm, SG_WIDTH), BF16)],
        compiler_params=_cparams())(zuv, zuv, lg, lb, wt, bias_l)


def _sgu_bwd(zuv, dya, lg, lb, wt, wt_t, bias_l, name):
    t = zuv.shape[0]
    tm = _row_tile(t, 256)
    nsteps = t // tm

    def body(u_ref, v_ref, dy_ref, lg_ref, lb_ref, wt_ref, wtt_ref, bl_ref,
             dz_ref, dwt_ref, dbl_ref, dlg_ref, dlb_ref, vln_scr, dvln_scr, dbacc_scr):
        step = pl.program_id(0)

        @pl.when(step == 0)
        def _():
            dwt_ref[...] = jnp.zeros_like(dwt_ref)
            dlg_ref[...] = jnp.zeros_like(dlg_ref)
            dlb_ref[...] = jnp.zeros_like(dlb_ref)
            dbl_ref[...] = jnp.zeros_like(dbl_ref)
            dbacc_scr[...] = jnp.zeros_like(dbacc_scr)

        vpre = v_ref[...]
        lgv = lg_ref[...]
        xhat, rstd, vln = _sgu_layernorm(vpre, lgv, lb_ref[...])
        vln_scr[...] = vln.astype(BF16)
        lo = lax.broadcasted_iota(jnp.int32, (CHUNK, LANES), 1) < 64
        for c in range(tm // CHUNK):
            rows = slice(c * CHUNK, (c + 1) * CHUNK)
            for p in range(SG_GROUPS // 2):
                cols = slice(p * LANES, (p + 1) * LANES)
                vp = vln_scr[rows, cols]
                mixed = jnp.where(lo, _dot(wt_ref[2 * p], vp), _dot(wt_ref[2 * p + 1], vp)) + bl_ref[:, cols]
                upre = u_ref[rows, cols]
                dyp = dy_ref[rows, cols]
                dz_ref[rows, cols] = (dyp * mixed * _gelu_grad(upre)).astype(BF16)
                dm = dyp * _gelu(upre)
                dbacc_scr[:, cols] += dm
                dlo = jnp.where(lo, dm, 0.0).astype(BF16)
                dhi = jnp.where(lo, 0.0, dm).astype(BF16)
                dvln_scr[rows, cols] = _dot(wtt_ref[2 * p], dlo) + _dot(wtt_ref[2 * p + 1], dhi)
                dwt_ref[2 * p] += _dot_nt(dlo, vp)
                dwt_ref[2 * p + 1] += _dot_nt(dhi, vp)
        dvln = dvln_scr[...]
        dlg_ref[...] += jnp.sum(dvln * xhat, axis=0, keepdims=True)
        dlb_ref[...] += jnp.sum(dvln, axis=0, keepdims=True)
        dxh = dvln * lgv
        dv = rstd * (dxh - jnp.mean(dxh, axis=-1, keepdims=True)
                     - xhat * jnp.mean(dxh * xhat, axis=-1, keepdims=True))
        dz_ref[:, SG_WIDTH:] = (dv * _gelu_grad(vpre)).astype(BF16)

        @pl.when(step == nsteps - 1)
        def _():
            rr = lax.broadcasted_iota(jnp.int32, (CHUNK, CHUNK), 0)
            cc = lax.broadcasted_iota(jnp.int32, (CHUNK, CHUNK), 1)
            tril = (cc <= rr).astype(F32)
            for gidx in range(SG_GROUPS):
                dwt_ref[gidx] = dwt_ref[gidx] * tril
            kk = lax.broadcasted_iota(jnp.int32, (SG_WIDTH, LANES), 0)
            gg = lax.broadcasted_iota(jnp.int32, (SG_WIDTH, LANES), 1)
            sel = ((kk // 64) == gg).astype(F32)
            dbl_ref[...] = jnp.dot(dbacc_scr[...], sel, preferred_element_type=F32,
                                   precision=lax.Precision.HIGHEST)

    half = lambda k: pl.BlockSpec((tm, SG_WIDTH), lambda i: (i, k))
    vec = pl.BlockSpec((1, SG_WIDTH), lambda i: (0, 0))
    wspec = pl.BlockSpec((SG_GROUPS, CHUNK, CHUNK), lambda i: (0, 0, 0))
    return pl.pallas_call(
        body, name=name, grid=(nsteps,),
        in_specs=[half(0), half(1), pl.BlockSpec((tm, SG_WIDTH), lambda i: (i, 0)), vec, vec,
                  wspec, wspec, pl.BlockSpec((CHUNK, SG_WIDTH), lambda i: (0, 0))],
        out_specs=[pl.BlockSpec((tm, 2 * SG_WIDTH), lambda i: (i, 0)), wspec,
                   pl.BlockSpec((CHUNK, LANES), lambda i: (0, 0)), vec, vec],
        out_shape=[jax.ShapeDtypeStruct((t, 2 * SG_WIDTH), BF16),
                   jax.ShapeDtypeStruct((SG_GROUPS, CHUNK, CHUNK), F32),
                   jax.ShapeDtypeStruct((CHUNK, LANES), F32),
                   jax.ShapeDtypeStruct((1, SG_WIDTH), F32), jax.ShapeDtypeStruct((1, SG_WIDTH), F32)],
        scratch_shapes=[pltpu.VMEM((tm, SG_WIDTH), BF16), pltpu.VMEM((tm, SG_WIDTH), F32),
                        pltpu.VMEM((CHUNK, SG_WIDTH), F32)],
        compiler_params=_cparams())(zuv, zuv, dya, lg, lb, wt, wt_t, bias_l)


def _rope(x, c, s1, s2):
    return x * c + pltpu.roll(x, LANES - 16, 1) * s1 + pltpu.roll(x, 16, 1) * s2


def _rope_t(dy, c, s1, s2):
    return dy * c + pltpu.roll(dy * s1, 16, 1) + pltpu.roll(dy * s2, LANES - 16, 1)


def _mla_prep_fwd(zcq, zckv, zkr, gcq, gckv, qg, kg, wuq, wuk, wuv, rc, rs1, rs2, name, ex=None):
    t = zcq.shape[0]
    tm = _row_tile(t, 256)
    hd = MLA_HEADS * LANES

    def body(zcq_ref, zckv_ref, zkr_ref, gcq_ref, gckv_ref, qg_ref, kg_ref, wuq_ref, wuk_ref, wuv_ref,
             c_ref, s1_ref, s2_ref, q_ref, k_ref, v_ref, cqn_ref, ckvn_ref):
        c, s1, s2 = c_ref[...], s1_ref[...], s2_ref[...]
        xq = zcq_ref[...]
        cqn = (xq * _rstd(xq, MLA_Q_RANK) * gcq_ref[...]).astype(BF16)
        cqn_ref[...] = cqn
        ql = _dot(cqn, wuq_ref[...])
        xk = zckv_ref[...]
        ckvn = (xk * _rstd(xk, MLA_KV_RANK) * gckv_ref[...]).astype(BF16)
        ckvn_ref[...] = ckvn
        kl = _dot(ckvn, wuk_ref[...])
        slot_lane = lax.broadcasted_iota(jnp.int32, (tm, hd), 1) % LANES
        v_ref[...] = jnp.where(slot_lane == V_ONES_LANE, 1.0, _dot(ckvn, wuv_ref[...])).astype(BF16)
        kr = zkr_ref[...]
        for h in range(MLA_HEADS):
            sl = slice(h * LANES, (h + 1) * LANES)
            qh = ql[:, sl]
            q_ref[:, sl] = (_rope(qh * _rstd(qh, MLA_QK) * qg_ref[...], c, s1, s2) * ATTN_SCALE2).astype(BF16)
            kh = kl[:, sl] + kr
            k_ref[:, sl] = _rope(kh * _rstd(kh, MLA_QK) * kg_ref[...], c, s1, s2).astype(BF16)

    row = lambda n: pl.BlockSpec((tm, n), lambda i: (i, 0))
    full = lambda a: pl.BlockSpec(a.shape, lambda i: (0, 0))
    return _call_with_exchange(
        ex, body, name, (t // tm,),
        [row(MLA_Q_RANK), row(MLA_KV_RANK), row(LANES), full(gcq), full(gckv), full(qg), full(kg),
         full(wuq), full(wuk), full(wuv), row(LANES), row(LANES), row(LANES)],
        [row(hd), row(hd), row(hd), row(MLA_Q_RANK), row(MLA_KV_RANK)],
        [jax.ShapeDtypeStruct((t, hd), BF16)] * 3
        + [jax.ShapeDtypeStruct((t, MLA_Q_RANK), BF16), jax.ShapeDtypeStruct((t, MLA_KV_RANK), BF16)],
        [], (zcq, zckv, zkr, gcq, gckv, qg, kg, wuq, wuk, wuv, rc, rs1, rs2))


def _mla_prep_bwd(zcq, zckv, zkr, gcq, gckv, qg, kg, wuq, wuk, wuv, rc, rs1, rs2, dq, dk, dv, name):
    t = zcq.shape[0]
    tm = _row_tile(t, 256)
    hd = MLA_HEADS * LANES

    def body(zcq_ref, zckv_ref, zkr_ref, gcq_ref, gckv_ref, qg_ref, kg_ref, wuq_ref, wuk_ref, wuv_ref,
             c_ref, s1_ref, s2_ref, dq_ref, dk_ref, dv_ref,
             dzcq_ref, dzckv_ref, dzkr_ref, dql_ref, dkl_ref, dgcq_ref, dgckv_ref, dqg_ref, dkg_ref):
        @pl.when(pl.program_id(0) == 0)
        def _():
            for ref in (dgcq_ref, dgckv_ref, dqg_ref, dkg_ref):
                ref[...] = jnp.zeros_like(ref)

        c, s1, s2 = c_ref[...], s1_ref[...], s2_ref[...]
        qgv, kgv = qg_ref[...], kg_ref[...]
        xq = zcq_ref[...]
        rq = _rstd(xq, MLA_Q_RANK)
        ql = _dot((xq * rq * gcq_ref[...]).astype(BF16), wuq_ref[...])
        xk = zckv_ref[...]
        rk = _rstd(xk, MLA_KV_RANK)
        kl = _dot((xk * rk * gckv_ref[...]).astype(BF16), wuk_ref[...])
        kr = zkr_ref[...]
        dqg_acc = jnp.zeros((tm, LANES), F32)
        dkg_acc = jnp.zeros((tm, LANES), F32)
        dkr = jnp.zeros((tm, LANES), F32)
        for h in range(MLA_HEADS):
            sl = slice(h * LANES, (h + 1) * LANES)
            qh = ql[:, sl]
            dqh, dgr = _rms_vjp(qh, _rstd(qh, MLA_QK), qgv, _rope_t(dq_ref[:, sl], c, s1, s2), MLA_QK)
            dql_ref[:, sl] = dqh.astype(BF16)
            dqg_acc += dgr
            kh = kl[:, sl] + kr
            dkh, dgr = _rms_vjp(kh, _rstd(kh, MLA_QK), kgv, _rope_t(dk_ref[:, sl], c, s1, s2), MLA_QK)
            dkl_ref[:, sl] = dkh.astype(BF16)
            dkg_acc += dgr
            dkr += dkh
        dqg_ref[...] += jnp.sum(dqg_acc, axis=0, keepdims=True)
        dkg_ref[...] += jnp.sum(dkg_acc, axis=0, keepdims=True)
        lane = lax.broadcasted_iota(jnp.int32, (tm, LANES), 1)
        dzkr_ref[...] = jnp.where((lane >= MLA_NOPE) & (lane < MLA_QK), dkr, 0.0).astype(BF16)
        dcqn = _dot_nt(dql_ref[...], wuq_ref[...])
        dx, dgr = _rms_vjp(xq, rq, gcq_ref[...], dcqn, MLA_Q_RANK)
        dzcq_ref[...] = dx.astype(BF16)
        dgcq_ref[...] += jnp.sum(dgr, axis=0, keepdims=True)
        dckvn = _dot_nt(dkl_ref[...], wuk_ref[...]) + _dot_nt(dv_ref[...].astype(BF16), wuv_ref[...])
        dx, dgr = _rms_vjp(xk, rk, gckv_ref[...], dckvn, MLA_KV_RANK)
        dzckv_ref[...] = dx.astype(BF16)
        dgckv_ref[...] += jnp.sum(dgr, axis=0, keepdims=True)

    row = lambda n: pl.BlockSpec((tm, n), lambda i: (i, 0))
    full = lambda a: pl.BlockSpec(a.shape, lambda i: (0, 0))
    vec = lambda n: pl.BlockSpec((1, n), lambda i: (0, 0))
    return pl.pallas_call(
        body, name=name, grid=(t // tm,),
        in_specs=[row(MLA_Q_RANK), row(MLA_KV_RANK), row(LANES), full(gcq), full(gckv), full(qg), full(kg),
                  full(wuq), full(wuk), full(wuv), row(LANES), row(LANES), row(LANES), row(hd), row(hd), row(hd)],
        out_specs=[row(MLA_Q_RANK), row(MLA_KV_RANK), row(LANES), row(hd), row(hd),
                   vec(MLA_Q_RANK), vec(MLA_KV_RANK), vec(LANES), vec(LANES)],
        out_shape=[jax.ShapeDtypeStruct((t, MLA_Q_RANK), BF16), jax.ShapeDtypeStruct((t, MLA_KV_RANK), BF16),
                   jax.ShapeDtypeStruct((t, LANES), BF16), jax.ShapeDtypeStruct((t, hd), BF16),
                   jax.ShapeDtypeStruct((t, hd), BF16), jax.ShapeDtypeStruct((1, MLA_Q_RANK), F32),
                   jax.ShapeDtypeStruct((1, MLA_KV_RANK), F32), jax.ShapeDtypeStruct((1, LANES), F32),
                   jax.ShapeDtypeStruct((1, LANES), F32)],
        compiler_params=_cparams(),
    )(zcq, zckv, zkr, gcq, gckv, qg, kg, wuq, wuk, wuv, rc, rs1, rs2, dq, dk, dv)


def _attn_tiles(t):
    tq = 512 if t >= 2048 else 128
    return tq, min(t, 4 * tq), min(t, 4 * tq)


def _causal_keep(tq, nk, i, j, tk):
    row = lax.broadcasted_iota(jnp.int32, (tq, nk), 0)
    col = lax.broadcasted_iota(jnp.int32, (tq, nk), 1)
    return (col - row) <= (i * tq - j * tk)


def _causal_keep_t(tq, nk, i, j, tk):
    key = lax.broadcasted_iota(jnp.int32, (nk, tq), 0)
    qry = lax.broadcasted_iota(jnp.int32, (nk, tq), 1)
    return (key - qry) <= (i * tq - j * tk)


ATTN_FWD_HEADS_PER_STEP = 2
ATTN_BWD_HEADS_PER_STEP = 2


def _attn_fwd(q, k, v, name, ex=None):
    t, hd = q.shape
    hp = ATTN_FWD_HEADS_PER_STEP
    tq, tk, _ = _attn_tiles(t)
    pairs = [(i, j) for i in range(t // tq) for j in range(((i + 1) * tq - 1) // tk + 1)]
    ii = np.array([p[0] for p in pairs], np.int32)
    jj = np.array([p[1] for p in pairs], np.int32)

    def body(ii_ref, jj_ref, q_ref, k_ref, v_ref, o_ref, lse_ref, m_scr, acc_scr):
        s_id = pl.program_id(1)
        i, j = ii_ref[s_id], jj_ref[s_id]
        last = j == ((i + 1) * tq - 1) // tk
        ones_lane = lax.broadcasted_iota(jnp.int32, (tq, LANES), 1) == V_ONES_LANE

        @pl.when(j == 0)
        def _():
            m_scr[...] = jnp.full_like(m_scr, NEG)
            acc_scr[...] = jnp.zeros_like(acc_scr)

        def step(masked, nk):
            scores = [_dot_nt(q_ref[:, hh * LANES:(hh + 1) * LANES], k_ref[:nk, hh * LANES:(hh + 1) * LANES])
                      for hh in range(hp)]
            for hh in range(hp):
                sl = slice(hh * LANES, (hh + 1) * LANES)
                s = scores[hh]
                if masked:
                    s = jnp.where(_causal_keep(tq, nk, i, j, tk), s, NEG)
                m_prev = m_scr[hh]
                m_new = jnp.maximum(m_prev, jnp.max(s, axis=1, keepdims=True))
                p = jnp.exp2(s - m_new)
                alpha = jnp.exp2(m_prev - m_new)
                acc = alpha * acc_scr[:, sl] + _dot(p.astype(BF16), v_ref[:nk, sl])
                if masked:
                    l_new = jnp.sum(jnp.where(ones_lane, acc, 0.0), axis=1, keepdims=True)
                    o_ref[:, sl] = (acc / l_new).astype(BF16)
                    lse_ref[:, sl] = jnp.broadcast_to(m_new + jnp.log(l_new) * LOG2E, (tq, LANES))
                else:
                    acc_scr[:, sl] = acc
                    m_scr[hh] = m_new

        @pl.when(jnp.logical_not(last))
        def _():
            step(False, tk)

        r = (((i + 1) * tq - 1) % tk) // tq
        for rr in range(tk // tq):
            @pl.when(last & (r == rr))
            def _():
                step(True, (rr + 1) * tq)

    w = hp * LANES
    qspec = pl.BlockSpec((tq, w), lambda h, s, ii_r, jj_r: (ii_r[s], h))
    kspec = pl.BlockSpec((tk, w), lambda h, s, ii_r, jj_r: (jj_r[s], h))
    return _call_with_exchange(
        ex, body, name, (hd // w, len(pairs)), [qspec, kspec, kspec], [qspec, qspec],
        [jax.ShapeDtypeStruct((t, hd), BF16), jax.ShapeDtypeStruct((t, hd), F32)],
        [pltpu.VMEM((hp, tq, 1), F32), pltpu.VMEM((tq, w), F32)], (q, k, v),
        prefetch=(jnp.asarray(ii), jnp.asarray(jj)))


def _attn_bwd_rows(o, lse, do, name):
    t, hd = o.shape
    heads = hd // LANES
    tm = _row_tile(t, 512)

    def body(o_ref, lse_ref, do_ref, out_ref):
        lane = lax.broadcasted_iota(jnp.int32, (tm, LANES), 1)
        acc = jnp.zeros((tm, LANES), F32)
        for h in range(heads):
            sl = slice(h * LANES, (h + 1) * LANES)
            delta = jnp.sum(do_ref[:, sl].astype(F32) * o_ref[:, sl].astype(F32), axis=1, keepdims=True)
            acc = jnp.where(lane == h, delta, acc)
            acc = jnp.where(lane == heads + h, lse_ref[:, sl], acc)
        out_ref[...] = acc

    row = pl.BlockSpec((tm, hd), lambda i: (i, 0))
    cols = pl.pallas_call(
        body, name=name, grid=(t // tm,), in_specs=[row, row, row],
        out_specs=pl.BlockSpec((tm, LANES), lambda i: (i, 0)),
        out_shape=jax.ShapeDtypeStruct((t, LANES), F32), compiler_params=_cparams())(o, lse, do)
    rows = cols.T
    return rows[:heads].reshape(heads, 1, t), rows[heads:2 * heads].reshape(heads, 1, t)


def _attn_bwd(q, k, v, delta_rows, lse_rows, do, name):
    t, hd = q.shape
    hp = ATTN_BWD_HEADS_PER_STEP
    tq, _, tk = _attn_tiles(t)
    nq = t // tq
    pairs = [(i, j) for j in range(t // tk) for i in range((j * tk) // tq, nq)]
    ii = np.array([p[0] for p in pairs], np.int32)
    jj = np.array([p[1] for p in pairs], np.int32)

    def body(jj_ref, ii_ref, q_ref, k_ref, v_ref, delta_ref, lse_ref, do_ref, dq_ref, dk_ref, dv_ref,
             dk_scr, dv_scr, dq_scr):
        s_id = pl.program_id(1)
        i, j = ii_ref[s_id], jj_ref[s_id]

        @pl.when(s_id == 0)
        def _():
            dq_scr[...] = jnp.zeros_like(dq_scr)

        @pl.when(i == (j * tk) // tq)
        def _():
            dk_scr[...] = jnp.zeros_like(dk_scr)
            dv_scr[...] = jnp.zeros_like(dv_scr)

        rows = pl.ds(pl.multiple_of(i * tq, tq), tq)

        def step(masked, nk):
            heads = [slice(hh * LANES, (hh + 1) * LANES) for hh in range(hp)]
            scores = [_dot_nt(k_ref[:nk, sl], q_ref[:, sl]) for sl in heads]
            for hh, sl in enumerate(heads):
                qv, kv, dov = q_ref[:, sl], k_ref[:nk, sl], do_ref[:, sl]
                st = scores[hh]
                if masked:
                    st = jnp.where(_causal_keep_t(tq, nk, i, j, tk), st, NEG)
                pt = jnp.exp2(st - lse_ref[hh])
                dv_scr[:nk, sl] += _dot(pt.astype(BF16), dov)
                dpt = _dot_nt(v_ref[:nk, sl], dov)
                dst = (pt * (dpt - delta_ref[hh]) * ATTN_SCALE).astype(BF16)
                dk_scr[:nk, sl] += _dot(dst, qv)
                dq_scr[rows, sl] += _dot_tn(dst, kv)

        seen = jnp.minimum((i + 1) * tq - j * tk, tk)
        for nk in range(tq, tk + 1, tq):
            @pl.when((seen == nk) & ((i + 1) * tq - j * tk <= tk))
            def _():
                step(True, nk)

        @pl.when((i + 1) * tq - j * tk > tk)
        def _():
            step(False, tk)

        @pl.when(i == nq - 1)
        def _():
            dk_ref[...] = (dk_scr[...] * (1.0 / ATTN_SCALE2)).astype(BF16)
            dv_ref[...] = dv_scr[...].astype(BF16)

        @pl.when(s_id == len(pairs) - 1)
        def _():
            dq_ref[...] = dq_scr[...].astype(BF16)

    w = hp * LANES
    qspec = pl.BlockSpec((tq, w), lambda h, s, jj_r, ii_r: (ii_r[s], h))
    kspec = pl.BlockSpec((tk, w), lambda h, s, jj_r, ii_r: (jj_r[s], h))
    rspec = pl.BlockSpec((hp, 1, tq), lambda h, s, jj_r, ii_r: (h, 0, ii_r[s]))
    return pl.pallas_call(
        body, name=name,
        grid_spec=pltpu.PrefetchScalarGridSpec(
            num_scalar_prefetch=2, grid=(hd // w, len(pairs)),
            in_specs=[qspec, kspec, kspec, rspec, rspec, qspec],
            out_specs=[pl.BlockSpec((t, w), lambda h, s, jj_r, ii_r: (0, h)), kspec, kspec],
            scratch_shapes=[pltpu.VMEM((tk, w), F32), pltpu.VMEM((tk, w), F32), pltpu.VMEM((t, w), F32)]),
        out_shape=[jax.ShapeDtypeStruct((t, hd), BF16)] * 3,
        compiler_params=_cparams())(jnp.asarray(jj), jnp.asarray(ii), q, k, v, delta_rows, lse_rows, do)


MEM_W = MEM_HEADS * LANES


def _mem_kv_fwd(mem, gmem, wkv, kg, name):
    m, d = mem.shape

    def body(mem_ref, g_ref, w_ref, kg_ref, k_ref, v_ref, mn_ref):
        xv = mem_ref[...]
        mn = (xv * _rstd(xv, d) * g_ref[...]).astype(BF16)
        mn_ref[...] = mn
        kvm = _dot(mn, w_ref[...])
        v_ref[...] = kvm[:, MEM_W:].astype(BF16)
        for h in range(MEM_HEADS):
            sl = slice(h * LANES, (h + 1) * LANES)
            kh = kvm[:, sl]
            k_ref[:, sl] = (kh * _rstd(kh, LANES) * kg_ref[...]).astype(BF16)

    full = lambda a: pl.BlockSpec(a.shape, lambda i: (0, 0))
    return pl.pallas_call(
        body, name=name, grid=(1,), in_specs=[full(mem), full(gmem), full(wkv), full(kg)],
        out_specs=[pl.BlockSpec((m, MEM_W), lambda i: (0, 0)), pl.BlockSpec((m, MEM_W), lambda i: (0, 0)),
                   pl.BlockSpec((m, d), lambda i: (0, 0))],
        out_shape=[jax.ShapeDtypeStruct((m, MEM_W), BF16), jax.ShapeDtypeStruct((m, MEM_W), BF16),
                   jax.ShapeDtypeStruct((m, d), BF16)],
        compiler_params=_cparams())(mem, gmem, wkv, kg)


def _mem_softmax(qn, kh):
    s = _dot_nt(qn, kh) * (LANES ** -0.5)
    e = jnp.exp(s - jnp.max(s, axis=1, keepdims=True))
    return e / jnp.sum(e, axis=1, keepdims=True)


def _mem_attn_fwd(zqm, qg, km, vm, name):
    t = zqm.shape[0]
    tm = _row_tile(t, 512)

    def body(q_ref, qg_ref, k_ref, v_ref, o_ref):
        for h in range(MEM_HEADS):
            sl = slice(h * LANES, (h + 1) * LANES)
            qh = q_ref[:, sl]
            qn = (qh * _rstd(qh, LANES) * qg_ref[...]).astype(BF16)
            p = _mem_softmax(qn, k_ref[:, sl])
            o_ref[:, sl] = _dot(p.astype(BF16), v_ref[:, sl]).astype(BF16)

    row = pl.BlockSpec((tm, MEM_W), lambda i: (i, 0))
    full = lambda a: pl.BlockSpec(a.shape, lambda i: (0, 0))
    return pl.pallas_call(
        body, name=name, grid=(t // tm,), in_specs=[row, full(qg), full(km), full(vm)], out_specs=row,
        out_shape=jax.ShapeDtypeStruct((t, MEM_W), BF16), compiler_params=_cparams())(zqm, qg, km, vm)


def _mem_attn_bwd(zqm, dyc, qg, km, vm, name):
    t = zqm.shape[0]
    m = km.shape[0]
    tm = _row_tile(t, 256)

    def body(q_ref, dy_ref, qg_ref, k_ref, v_ref, dz_ref, dk_ref, dv_ref, dqg_ref):
        @pl.when(pl.program_id(0) == 0)
        def _():
            dk_ref[...] = jnp.zeros_like(dk_ref)
            dv_ref[...] = jnp.zeros_like(dv_ref)
            dqg_ref[...] = jnp.zeros_like(dqg_ref)

        qgv = qg_ref[...]
        dqg_acc = jnp.zeros((tm, LANES), F32)
        heads = [slice(h * LANES, (h + 1) * LANES) for h in range(MEM_HEADS)]
        dps = [_dot_nt(dy_ref[:, sl], v_ref[:, sl]) for sl in heads]
        for h, sl in enumerate(heads):
            qh = q_ref[:, sl]
            r = _rstd(qh, LANES)
            qn = (qh * r * qgv).astype(BF16)
            kh = k_ref[:, sl]
            p = _mem_softmax(qn, kh)
            dov = dy_ref[:, sl]
            dv_ref[:, sl] += _dot_tn(p.astype(BF16), dov)
            dp = dps[h]
            ds = (p * (dp - jnp.sum(dp * p, axis=1, keepdims=True)) * (LANES ** -0.5)).astype(BF16)
            dk_ref[:, sl] += _dot_tn(ds, qn)
            dqh, dgr = _rms_vjp(qh, r, qgv, _dot(ds, kh), LANES)
            dz_ref[:, sl] = dqh.astype(BF16)
            dqg_acc += dgr
        dqg_ref[...] += jnp.sum(dqg_acc, axis=0, keepdims=True)

    row = pl.BlockSpec((tm, MEM_W), lambda i: (i, 0))
    full = lambda a: pl.BlockSpec(a.shape, lambda i: (0, 0))
    acc = pl.BlockSpec((m, MEM_W), lambda i: (0, 0))
    return pl.pallas_call(
        body, name=name, grid=(t // tm,), in_specs=[row, row, full(qg), full(km), full(vm)],
        out_specs=[row, acc, acc, pl.BlockSpec((1, LANES), lambda i: (0, 0))],
        out_shape=[jax.ShapeDtypeStruct((t, MEM_W), BF16), jax.ShapeDtypeStruct((m, MEM_W), F32),
                   jax.ShapeDtypeStruct((m, MEM_W), F32), jax.ShapeDtypeStruct((1, LANES), F32)],
        compiler_params=_cparams())(zqm, dyc, qg, km, vm)


def _mem_kv_bwd(mem, gmem, wkv, kg, dkn, dvm, name):
    m, d = mem.shape

    def body(mem_ref, g_ref, w_ref, kg_ref, dk_ref, dv_ref, dw_ref, dkg_ref, dg_ref, dkv_scr):
        xv = mem_ref[...]
        r = _rstd(xv, d)
        mn = (xv * r * g_ref[...]).astype(BF16)
        kvm = _dot(mn, w_ref[...])
        dkv_scr[:, MEM_W:] = dv_ref[...].astype(BF16)
        dkg_acc = jnp.zeros((m, LANES), F32)
        for h in range(MEM_HEADS):
            sl = slice(h * LANES, (h + 1) * LANES)
            kh = kvm[:, sl]
            dkh, dgr = _rms_vjp(kh, _rstd(kh, LANES), kg_ref[...], dk_ref[:, sl], LANES)
            dkv_scr[:, sl] = dkh.astype(BF16)
            dkg_acc += dgr
        dkg_ref[...] = jnp.sum(dkg_acc, axis=0, keepdims=True)
        dkv = dkv_scr[...]
        dw_ref[...] = _dot_tn(mn, dkv)
        dmn = _dot_nt(dkv, w_ref[...])
        dg_ref[...] = jnp.sum(dmn * xv * r, axis=0, keepdims=True)

    full = lambda a: pl.BlockSpec(a.shape, lambda i: (0, 0))
    return pl.pallas_call(
        body, name=name, grid=(1,),
        in_specs=[full(mem), full(gmem), full(wkv), full(kg), full(dkn), full(dvm)],
        out_specs=[pl.BlockSpec((d, 2 * MEM_W), lambda i: (0, 0)), pl.BlockSpec((1, LANES), lambda i: (0, 0)),
                   pl.BlockSpec((1, d), lambda i: (0, 0))],
        out_shape=[jax.ShapeDtypeStruct((d, 2 * MEM_W), F32), jax.ShapeDtypeStruct((1, LANES), F32),
                   jax.ShapeDtypeStruct((1, d), F32)],
        scratch_shapes=[pltpu.VMEM((m, 2 * MEM_W), BF16)],
        compiler_params=_cparams())(mem, gmem, wkv, kg, dkn, dvm)


def _merge_fwd(x1, ya, yb, yc, zg, bg, wa, wb, wc, wo, name):
    t, d = x1.shape
    tm = _row_tile(t, 256)

    def body(x_ref, ya_ref, yb_ref, yc_ref, zg_ref, bg_ref, wa_ref, wb_ref, wc_ref, wo_ref,
             x2_ref, mg_ref, pa_ref, pb_ref, pc_ref):
        merged = None
        for k, (y_ref, w_ref, p_ref) in enumerate(
                ((ya_ref, wa_ref, pa_ref), (yb_ref, wb_ref, pb_ref), (yc_ref, wc_ref, pc_ref))):
            sl = slice(k * d, (k + 1) * d)
            pr = _dot(y_ref[...], w_ref[...])
            p_ref[...] = pr.astype(BF16)
            term = jax.nn.sigmoid(zg_ref[:, sl] + bg_ref[:, sl]) * pr
            merged = term if merged is None else merged + term
        mb = merged.astype(BF16)
        mg_ref[...] = mb
        x2_ref[...] = x_ref[...] + _dot(mb, wo_ref[...])

    row = lambda n: pl.BlockSpec((tm, n), lambda i: (i, 0))
    full = lambda a: pl.BlockSpec(a.shape, lambda i: (0, 0))
    return pl.pallas_call(
        body, name=name, grid=(t // tm,),
        in_specs=[row(d), row(ya.shape[1]), row(yb.shape[1]), row(yc.shape[1]), row(3 * d), full(bg),
                  full(wa), full(wb), full(wc), full(wo)],
        out_specs=[row(d)] * 5,
        out_shape=[jax.ShapeDtypeStruct((t, d), F32)] + [jax.ShapeDtypeStruct((t, d), BF16)] * 4,
        compiler_params=_cparams())(x1, ya, yb, yc, zg, bg, wa, wb, wc, wo)


def _merge_bwd(dx2, pa, pb, pc, zg, bg, wa, wb, wc, wo, name, ex=None):
    t, d = dx2.shape
    tm = _row_tile(t, 256)

    def body(dx_ref, pa_ref, pb_ref, pc_ref, zg_ref, bg_ref, wa_ref, wb_ref, wc_ref, wo_ref,
             dpa_ref, dpb_ref, dpc_ref, dzg_ref, dbg_ref, dya_ref, dyb_ref, dyc_ref):
        @pl.when(pl.program_id(0) == 0)
        def _():
            dbg_ref[...] = jnp.zeros_like(dbg_ref)

        dm = _dot_nt(dx_ref[...].astype(BF16), wo_ref[...])
        for k, (p_ref, w_ref, dp_ref, dy_ref) in enumerate(
                ((pa_ref, wa_ref, dpa_ref, dya_ref), (pb_ref, wb_ref, dpb_ref, dyb_ref),
                 (pc_ref, wc_ref, dpc_ref, dyc_ref))):
            sl = slice(k * d, (k + 1) * d)
            gate = jax.nn.sigmoid(zg_ref[:, sl] + bg_ref[:, sl])
            dpr = (dm * gate).astype(BF16)
            dp_ref[...] = dpr
            dzg = dm * p_ref[...].astype(F32) * gate * (1.0 - gate)
            dzg_ref[:, sl] = dzg.astype(BF16)
            dbg_ref[:, sl] += jnp.sum(dzg, axis=0, keepdims=True)
            dy_ref[...] = _dot_nt(dpr, w_ref[...]).astype(dy_ref.dtype)

    row = lambda n: pl.BlockSpec((tm, n), lambda i: (i, 0))
    full = lambda a: pl.BlockSpec(a.shape, lambda i: (0, 0))
    na, nb, nc = wa.shape[0], wb.shape[0], wc.shape[0]
    return _call_with_exchange(
        ex, body, name, (t // tm,),
        [row(d), row(d), row(d), row(d), row(3 * d), full(bg), full(wa), full(wb), full(wc), full(wo)],
        [row(d), row(d), row(d), row(3 * d), pl.BlockSpec((1, 3 * d), lambda i: (0, 0)), row(na), row(nb), row(nc)],
        [jax.ShapeDtypeStruct((t, d), BF16)] * 3
        + [jax.ShapeDtypeStruct((t, 3 * d), BF16), jax.ShapeDtypeStruct((1, 3 * d), F32),
           jax.ShapeDtypeStruct((t, na), F32), jax.ShapeDtypeStruct((t, nb), BF16),
           jax.ShapeDtypeStruct((t, nc), BF16)],
        [], (dx2, pa, pb, pc, zg, bg, wa, wb, wc, wo))


def _adamw_math(w, g, m, v):
    bc1 = 1.0 - ADAM_B1 ** ADAM_STEP
    bc2 = 1.0 - ADAM_B2 ** ADAM_STEP
    nm = ADAM_B1 * m + (1.0 - ADAM_B1) * g
    nv = ADAM_B2 * v + (1.0 - ADAM_B2) * (g * g)
    delta = -ADAM_LR * ((nm / bc1) / (jnp.sqrt(nv / bc2) + ADAM_EPS) + ADAM_WD * w)
    return delta, nm, nv


def _div_tile(n, cap, mult):
    best = None
    for cand in range(mult, min(n, cap) + 1, mult):
        if n % cand == 0:
            best = cand
    assert best is not None, (n, cap, mult)
    return best


def _adamw(w, g, m, v, name):
    rows, cols = w.shape
    tr = rows if rows * cols <= 256 * 1024 else _div_tile(rows, 256, 8)

    def body(w_ref, g_ref, m_ref, v_ref, d_ref, nm_ref, nv_ref):
        d_ref[...], nm_ref[...], nv_ref[...] = _adamw_math(w_ref[...], g_ref[...], m_ref[...], v_ref[...])

    blk = pl.BlockSpec((tr, cols), lambda i: (i, 0))
    return pl.pallas_call(
        body, name=name, grid=(rows // tr,), in_specs=[blk] * 4, out_specs=[blk] * 3,
        out_shape=[jax.ShapeDtypeStruct((rows, cols), F32)] * 3, compiler_params=_cparams())(w, g, m, v)


def _adamw_slots(w, slots, m, v, name):
    _, hr, cols = w.shape
    tr = _div_tile(hr, 128, 16)

    def body(w_ref, s_ref, m_ref, v_ref, g_ref, d_ref, nm_ref, nv_ref):
        g = s_ref[0, 0].astype(F32)
        for k in range(1, N_CHIPS):
            g = g + s_ref[0, k].astype(F32)
        g_ref[0] = g
        d_ref[0], nm_ref[0], nv_ref[0] = _adamw_math(w_ref[0], g, m_ref[0], v_ref[0])

    blk = pl.BlockSpec((1, tr, cols), lambda h, i: (h, i, 0))
    return pl.pallas_call(
        body, name=name, grid=(2, hr // tr),
        in_specs=[blk, pl.BlockSpec((1, N_CHIPS, tr, cols), lambda h, i: (h, 0, i, 0)), blk, blk],
        out_specs=[blk] * 4, out_shape=[jax.ShapeDtypeStruct((2, hr, cols), F32)] * 4,
        compiler_params=_cparams())(w, slots, m, v)


ANY = pl.BlockSpec(memory_space=pl.ANY)


def _place():
    x, y, c = lax.axis_index("x"), lax.axis_index("y"), lax.axis_index("c")
    other_chips = [(1 - x, y), (x, 1 - y), (1 - x, 1 - y)]
    return x, y, c, other_chips


def _remote(src, dst, send_sem, recv_sem, to):
    return pltpu.make_async_remote_copy(src_ref=src, dst_ref=dst, send_sem=send_sem, recv_sem=recv_sem,
                                        device_id=to, device_id_type=MESH)


PIECE_BYTES = 384 * 1024


def _row_pieces(half_rows, cols):
    for n in (4, 2):
        if half_rows % (16 * n) == 0 and half_rows * cols * 2 // n >= PIECE_BYTES:
            return [pl.ds(k * (half_rows // n), half_rows // n) for k in range(n)]
    return [pl.ds(0, half_rows)]


def _pieces(arrays, rows_axis):
    return [(w, rows) for w, a in enumerate(arrays) for rows in _row_pieces(a.shape[rows_axis], a.shape[-1])]


def _gather_exchange(shards):
    nw = len(shards)
    pieces = _pieces(shards, 1)
    npc = len(pieces)

    def build(s_refs, g_refs, sems):
        send_sems, recv_sems, local_sems = sems
        x, y, c, chips = _place()
        me = 2 * x + y
        sibling = (x, y, 1 - c)
        mine = [pltpu.make_async_copy(s_refs[w], g_refs[w].at[me], local_sems.at[w]) for w in range(nw)]
        first = [_remote(s_refs[w].at[c, rows], g_refs[w].at[me, c, rows], send_sems.at[k, p], recv_sems.at[k, p],
                         (cx, cy, c)) for k, (cx, cy) in enumerate(chips) for p, (w, rows) in enumerate(pieces)]

        def start():
            for cp in mine + first:
                cp.start()

        arrived = [g_refs[w].at[2 * cx + cy, c, rows] for cx, cy in chips for w, rows in pieces]
        passed = [_remote(slab, slab, send_sems.at[3 + q // npc, q % npc], recv_sems.at[3 + q // npc, q % npc], sibling)
                  for q, slab in enumerate(arrived)]

        def pass_on():
            for q, slab in enumerate(arrived):
                k, p = q // npc, q % npc
                _remote(slab, slab, send_sems.at[k, p], recv_sems.at[k, p], (*chips[k], c)).wait_recv()
                passed[q].start()

        def finish():
            for k, (cx, cy) in enumerate(chips):
                for p, (w, rows) in enumerate(pieces):
                    slab = g_refs[w].at[2 * cx + cy, 1 - c, rows]
                    _remote(slab, slab, send_sems.at[3 + k, p], recv_sems.at[3 + k, p], sibling).wait_recv()
            for cp in first + passed:
                cp.wait_send()
            for cp in mine:
                cp.wait()

        return start, pass_on, finish

    return _Exchange(list(shards), [jax.ShapeDtypeStruct((N_CHIPS,) + s.shape, BF16) for s in shards],
                     [pltpu.SemaphoreType.DMA((6, npc)), pltpu.SemaphoreType.DMA((6, npc)),
                      pltpu.SemaphoreType.DMA((nw,))], build)


def _swap_halves(grads, name):
    nw = len(grads)

    def body(*refs):
        g_refs, sib_refs = refs[:nw], refs[nw:2 * nw]
        send_sems, recv_sems = refs[2 * nw:]
        x, y, c, _ = _place()
        copies = [_remote(g_refs[w].at[s, 1 - c], sib_refs[w].at[s], send_sems.at[s, w], recv_sems.at[s, w],
                          (x, y, 1 - c)) for w in range(nw) for s in range(N_CHIPS)]
        for cp in copies:
            cp.start()
        for cp in copies:
            cp.wait_recv()
        for cp in copies:
            cp.wait_send()

    return pl.pallas_call(
        body, name=name, in_specs=[ANY] * nw, out_specs=[ANY] * nw,
        out_shape=[jax.ShapeDtypeStruct((N_CHIPS,) + g.shape[2:], BF16) for g in grads],
        scratch_shapes=[pltpu.SemaphoreType.DMA((N_CHIPS, nw)), pltpu.SemaphoreType.DMA((N_CHIPS, nw))],
    )(*grads)


def _pair_sum(grad, sib, core, name):
    nchip, _, hr, cols = grad.shape
    tr = _div_tile(hr, 256, 16)

    def body(core_ref, a_ref, b_ref, o_ref):
        o_ref[...] = (a_ref[0].astype(F32) + b_ref[...].astype(F32)).astype(BF16)

    return pl.pallas_call(
        body, name=name,
        grid_spec=pltpu.PrefetchScalarGridSpec(
            num_scalar_prefetch=1, grid=(nchip, hr // tr),
            in_specs=[pl.BlockSpec((1, 1, tr, cols), lambda s, i, core_r: (s, core_r[0], i, 0)),
                      pl.BlockSpec((1, tr, cols), lambda s, i, core_r: (s, i, 0))],
            out_specs=pl.BlockSpec((1, tr, cols), lambda s, i, core_r: (s, i, 0))),
        out_shape=jax.ShapeDtypeStruct((nchip, hr, cols), BF16), compiler_params=_cparams())(core, grad, sib)


def _pair_sum_exchange(sums):
    nw = len(sums)
    pieces = _pieces(sums, 1)
    npc = len(pieces)

    def build(p_refs, o_refs, sems):
        send_sems, recv_sems, local_sems = sems
        x, y, c, chips = _place()
        me = 2 * x + y
        sibling = (x, y, 1 - c)
        mine = [pltpu.make_async_copy(p_refs[w].at[me], o_refs[w].at[c, 3], local_sems.at[w]) for w in range(nw)]
        first = [_remote(p_refs[w].at[2 * cx + cy, rows], o_refs[w].at[c, k, rows], send_sems.at[k, p],
                         recv_sems.at[k, p], (cx, cy, c))
                 for k, (cx, cy) in enumerate(chips) for p, (w, rows) in enumerate(pieces)]

        def start():
            for cp in mine + first:
                cp.start()

        passed = [_remote(o_refs[w].at[c, k, rows], o_refs[w].at[c, k, rows], send_sems.at[3 + k, p],
                          recv_sems.at[3 + k, p], sibling) for k in range(N_CHIPS) for p, (w, rows) in enumerate(pieces)]

        def pass_on():
            for k in range(N_CHIPS):
                own_waited = set()
                for p, (w, rows) in enumerate(pieces):
                    if k < 3:
                        first[k * npc + p].wait_recv()
                    elif w not in own_waited:
                        mine[w].wait()
                        own_waited.add(w)
                    passed[k * npc + p].start()

        def finish():
            for k in range(N_CHIPS):
                for p, (w, rows) in enumerate(pieces):
                    slab = o_refs[w].at[1 - c, k, rows]
                    _remote(slab, slab, send_sems.at[3 + k, p], recv_sems.at[3 + k, p], sibling).wait_recv()
            for cp in first + passed:
                cp.wait_send()

        return start, pass_on, finish

    return _Exchange(list(sums), [jax.ShapeDtypeStruct((2,) + p.shape, BF16) for p in sums],
                     [pltpu.SemaphoreType.DMA((7, npc)), pltpu.SemaphoreType.DMA((7, npc)),
                      pltpu.SemaphoreType.DMA((nw,))], build)


def _small_sum_exchange(vec):
    m_per, n = vec.shape

    def build(ins, outs, scr):
        (x_ref,), (out_ref,) = ins, outs
        gath_ref, sum_ref, send_sems, recv_sems, local_sem, out_sem = scr
        x, y, c, chips = _place()
        me, sibling = (x, y, c), (x, y, 1 - c)

        def rows(px, py, pc):
            return gath_ref.at[pl.ds((4 * px + 2 * py + pc) * m_per, m_per), :]

        def copy(k, block, to, src=None):
            return pltpu.make_async_remote_copy(
                src_ref=rows(*block) if src is None else src, dst_ref=rows(*block),
                send_sem=send_sems.at[k], recv_sem=recv_sems.at[k], device_id=to, device_id_type=MESH)

        mine = pltpu.make_async_copy(x_ref, rows(*me), local_sem)
        first = [copy(0, me, sibling, src=x_ref)] + [copy(1 + j, me, (*chip, c), src=x_ref)
                                                     for j, chip in enumerate(chips)]

        def start():
            for cp in [mine] + first:
                cp.start()

        passed = [copy(4 + j, (*chip, c), sibling) for j, chip in enumerate(chips)]

        def pass_on():
            for j, chip in enumerate(chips):
                copy(1 + j, (*chip, c), me).wait_recv()
                passed[j].start()

        def finish():
            copy(0, sibling, me).wait_recv()
            for j, chip in enumerate(chips):
                copy(4 + j, (*chip, 1 - c), me).wait_recv()
            for cp in first + passed:
                cp.wait_send()
            mine.wait()
            acc = gath_ref[pl.ds(0, m_per), :]
            for k in range(1, N_DEV):
                acc = acc + gath_ref[pl.ds(k * m_per, m_per), :]
            sum_ref[...] = acc
            done = pltpu.make_async_copy(sum_ref, out_ref, out_sem)
            done.start()
            done.wait()

        return start, pass_on, finish

    return _Exchange([vec], [jax.ShapeDtypeStruct((m_per, n), F32)],
                     [pltpu.VMEM((N_DEV * m_per, n), F32), pltpu.VMEM((m_per, n), F32), pltpu.SemaphoreType.DMA((7,)),
                      pltpu.SemaphoreType.DMA((7,)), pltpu.SemaphoreType.DMA, pltpu.SemaphoreType.DMA], build)


def _pack_small(vals, tail=()):
    flat = jnp.concatenate([vals[name].reshape(-1).astype(F32) for name, _ in SMALL] + [v.reshape(1) for v in tail])
    flat = jnp.pad(flat, (0, SMALL_ROWS * LANES - flat.shape[0]))
    return flat.reshape(SMALL_ROWS, LANES)


def _unpack_small(packed):
    flat = packed.reshape(-1)
    out, off = {}, 0
    for name, shape in SMALL:
        n = int(np.prod(shape))
        out[name] = flat[off:off + n].reshape(shape)
        off += n
    return out


def _head_pad_cols(w, heads, real):
    k = w.shape[0]
    return jnp.pad(w.reshape(k, heads, real), ((0, 0), (0, 0), (0, LANES - real))).reshape(k, heads * LANES)


def _rope_tables(positions):
    half = MLA_ROPE // 2
    inv = ROPE_BASE ** (-jnp.arange(half, dtype=F32) / half)
    ang = positions.astype(F32)[:, None] * inv
    cos, sin = jnp.cos(ang), jnp.sin(ang)
    t = positions.shape[0]
    z = lambda n: jnp.zeros((t, n), F32)
    rc = jnp.concatenate([jnp.ones((t, MLA_NOPE), F32), cos, cos, z(LANES - MLA_QK)], axis=1)
    rs1 = jnp.concatenate([z(MLA_NOPE), -sin, z(LANES - MLA_NOPE - half)], axis=1)
    rs2 = jnp.concatenate([z(MLA_NOPE + half), sin, z(LANES - MLA_QK)], axis=1)
    return rc, rs1, rs2


FFN1_WEIGHTS = ("ffn1_w_gu", "ffn1_w_down")
FFN2_WEIGHTS = ("ffn2_w_gu", "ffn2_w_down")
MIXER_WEIGHTS = tuple(n for n, *_ in SHARDED if n not in FFN1_WEIGHTS + FFN2_WEIGHTS)
SHARD_SHAPE = {n: (r, c, kind) for n, r, c, kind in SHARDED}


def _from_blocks(name, gathered):
    r, c, kind = SHARD_SHAPE[name]
    blk = gathered.reshape(N_CHIPS, r, c)
    return blk, (blk.transpose(1, 0, 2).reshape(r, N_CHIPS * c) if kind == "col" else blk.reshape(N_CHIPS * r, c))


def _grad_pair_sums(names, gw, core, tag):
    by_owner = []
    for name in names:
        r, c, kind = SHARD_SHAPE[name]
        if gw[name].dtype == BF16:
            blk = gw[name]
        elif kind == "col":
            blk = gw[name].reshape(r, N_CHIPS, c).transpose(1, 0, 2)
        else:
            blk = gw[name].reshape(N_CHIPS, r, c)
        by_owner.append(blk.astype(BF16).reshape(N_CHIPS, 2, r // 2, c))
    received = _swap_halves(by_owner, "grad_swap_" + tag)
    return [_pair_sum(g, s, core, "pair_sum_" + n) for g, s, n in zip(by_owner, received, names)]


def _device_step(x, mem, positions, tgt, small, shards, core):
    d = D_MODEL
    g_ffn1, g_mix, g_ffn2 = small["ffn1_norm"], small["mix_norm"], small["ffn2_norm"]
    big = {}
    for name, g in zip(FFN1_WEIGHTS, _run_exchange(_gather_exchange([shards[n] for n in FFN1_WEIGHTS]), "gather_ffn1")):
        big[name + "#blocks"], big[name] = _from_blocks(name, g)
    wgu1, wd1 = big["ffn1_w_gu#blocks"], big["ffn1_w_down"].reshape(2, FF_TILE, d)
    x1, gpre1, upre1, h, *rest = _ffn_fwd(x, g_ffn1, wgu1, wd1, "ffn1_fwd", next_gain=g_mix,
                                          ex=_gather_exchange([shards[n] for n in MIXER_WEIGHTS]))
    for name, g in zip(MIXER_WEIGHTS, rest):
        big[name + "#blocks"], big[name] = _from_blocks(name, g)
    w_in = big["w_in"]
    w_uv_, w_cq, w_ckv = w_in[:, :COL_CQ], w_in[:, COL_CQ:COL_CKV], w_in[:, COL_CKV:COL_KR]
    w_kr = jnp.pad(w_in[:, COL_KR:COL_QM], ((0, 0), (MLA_NOPE, LANES - MLA_QK)))
    w_qm, w_g = w_in[:, COL_QM:COL_GATE], w_in[:, COL_GATE:]
    segs = (w_uv_, w_cq, w_ckv, w_kr, w_qm, w_g)
    wuq = _head_pad_cols(big["mla_w_uq"], MLA_HEADS, MLA_QK)
    ukv = big["mla_w_ukv"].reshape(MLA_KV_RANK, MLA_HEADS, 2, MLA_NOPE)
    wuk = _head_pad_cols(ukv[:, :, 0].reshape(MLA_KV_RANK, -1), MLA_HEADS, MLA_NOPE)
    wuv = _head_pad_cols(ukv[:, :, 1].reshape(MLA_KV_RANK, -1), MLA_HEADS, MLA_NOPE)
    wkv = big["mem_w_kv"]
    wa, wc, wo = big["w_branch_a"], big["w_branch_c"], big["w_out"]
    wb = jnp.pad(big["w_branch_b"].reshape(MLA_HEADS, MLA_NOPE, d),
                 ((0, 0), (0, LANES - MLA_NOPE), (0, 0))).reshape(MLA_HEADS * LANES, d)
    qg = jnp.pad(small["mla_q_norm"], ((0, 0), (0, LANES - MLA_QK)))
    kg = jnp.pad(small["mla_k_norm"], ((0, 0), (0, LANES - MLA_QK)))
    causal = jnp.tril(jnp.ones((CHUNK, CHUNK), bool))
    wt_f = jnp.where(causal[None], small["sg_w"][0], 0.0)
    wt, wt_t = wt_f.astype(BF16), wt_f.transpose(0, 2, 1).astype(BF16)
    bias_l = jnp.repeat(small["sg_b"][0].T, 64, axis=1)
    rc, rs1, rs2 = _rope_tables(positions)

    zuv, zcq, zckv, zkr, zqm, zg = _mm_cols(h, segs, [F32] * 5 + [BF16], "in_proj")
    ya = _sgu_fwd(zuv, small["sg_ln_g"], small["sg_ln_b"], wt, bias_l, "sgu_fwd")
    q, k, v, cqn, ckvn = _mla_prep_fwd(zcq, zckv, zkr, small["mla_cq_norm"], small["mla_ckv_norm"], qg, kg,
                                       wuq, wuk, wuv, rc, rs1, rs2, "mla_prep_fwd")
    yb, lse, *rest = _attn_fwd(q, k, v, "mla_attn_fwd", ex=_gather_exchange([shards[n] for n in FFN2_WEIGHTS]))
    for name, g in zip(FFN2_WEIGHTS, rest):
        big[name + "#blocks"], big[name] = _from_blocks(name, g)
    wgu2, wd2 = big["ffn2_w_gu#blocks"], big["ffn2_w_down"].reshape(2, FF_TILE, d)
    km, vm, memn = _mem_kv_fwd(mem, small["mem_norm"], wkv, small["mem_k_norm"], "mem_kv_fwd")
    yc = _mem_attn_fwd(zqm, small["mem_q_norm"], km, vm, "mem_attn_fwd")
    x2, merged, pa, pb, pc = _merge_fwd(x1, ya, yb, yc, zg, small["b_gate"], wa, wb, wc, wo, "merge_fwd")
    dy, loss_row, gpre2, upre2 = _ffn_fwd(x2, g_ffn2, wgu2, wd2, "ffn2_fwd", target=tgt)

    gw, gs, slots = {}, {}, {}

    def ffn_grads(prefix, xin, gain, dyin, gpre, upre, wgu, wd, ex=None, ex_names=(), last=False):
        dx, dgain, xn, dgt, dup, act, *got = _ffn_bwd(xin, gain, dyin, gpre, upre, wgu, wd, prefix + "_bwd", ex=ex)
        slots.update(zip(ex_names, got))
        gs[prefix + "_norm"] = dgain
        gw[prefix + "_w_gu"] = jnp.concatenate(
            [_mm_tn(xn, dgt, prefix + "_dwg", col_blocks=True, out_dtype=BF16),
             _mm_tn(xn, dup, prefix + "_dwu", col_blocks=True, out_dtype=BF16)], axis=0)
        rows_down = SHARD_SHAPE[prefix + "_w_down"][0]
        if last:
            small_sum = _small_sum_exchange(_pack_small(gs, tail=[loss_row[0, 0]]))
            dwd, summed = _mm_tn(act, dyin, prefix + "_dwd", scale=0.5, ex=small_sum, out_dtype=BF16)
            gw[prefix + "_w_down"] = dwd.reshape(N_CHIPS, rows_down, d)
            return dx, summed
        gw[prefix + "_w_down"] = _mm_tn(act, dyin, prefix + "_dwd", scale=0.5, out_dtype=BF16).reshape(
            N_CHIPS, rows_down, d)
        return dx

    dx2 = ffn_grads("ffn2", x2, g_ffn2, dy, gpre2, upre2, wgu2, wd2)
    ffn2_sums = _pair_sum_exchange(_grad_pair_sums(FFN2_WEIGHTS, gw, core, "ffn2"))
    dpa, dpb, dpc, dzg, dbg, dya, dyb, dyc, *got = _merge_bwd(dx2, pa, pb, pc, zg, small["b_gate"], wa, wb, wc, wo,
                                                              "merge_bwd", ex=ffn2_sums)
    slots.update(zip(FFN2_WEIGHTS, got))
    gs["b_gate"] = dbg
    gw["w_out"] = _mm_tn(merged, dx2, "dw_out")
    gw["w_branch_a"] = _mm_tn(ya, dpa, "dw_branch_a")
    gw["w_branch_b"] = _mm_tn(yb, dpb, "dw_branch_b").reshape(MLA_HEADS, LANES, d)[:, :MLA_NOPE].reshape(-1, d)
    gw["w_branch_c"] = _mm_tn(yc, dpc, "dw_branch_c")

    dzuv, dwt, dbl, dlg, dlb = _sgu_bwd(zuv, dya, small["sg_ln_g"], small["sg_ln_b"], wt, wt_t, bias_l, "sgu_bwd")
    gs["sg_w"], gs["sg_b"] = dwt[None], dbl[:, :SG_GROUPS].T[None]
    gs["sg_ln_g"], gs["sg_ln_b"] = dlg, dlb

    delta_rows, lse_rows = _attn_bwd_rows(yb, lse, dyb, "mla_attn_bwd_rows")
    dq, dk, dv = _attn_bwd(q, k, v, delta_rows, lse_rows, dyb, "mla_attn_bwd")
    dzcq, dzckv, dzkr, dql, dkl, dgcq, dgckv, dqg, dkg = _mla_prep_bwd(
        zcq, zckv, zkr, small["mla_cq_norm"], small["mla_ckv_norm"], qg, kg, wuq, wuk, wuv, rc, rs1, rs2,
        dq, dk, dv, "mla_prep_bwd")
    gs["mla_cq_norm"], gs["mla_ckv_norm"] = dgcq, dgckv
    gs["mla_q_norm"], gs["mla_k_norm"] = dqg[:, :MLA_QK], dkg[:, :MLA_QK]
    gw["mla_w_uq"] = _mm_tn(cqn, dql, "dw_uq").reshape(MLA_Q_RANK, MLA_HEADS, LANES)[:, :, :MLA_QK].reshape(
        MLA_Q_RANK, -1)
    dwuk = _mm_tn(ckvn, dkl, "dw_uk").reshape(MLA_KV_RANK, MLA_HEADS, LANES)[:, :, :MLA_NOPE]
    dwuv = _mm_tn(ckvn, dv, "dw_uv").reshape(MLA_KV_RANK, MLA_HEADS, LANES)[:, :, :MLA_NOPE]
    gw["mla_w_ukv"] = jnp.concatenate([dwuk, dwuv], axis=2).reshape(MLA_KV_RANK, -1)

    dzqm, dkn, dvm, dmqg = _mem_attn_bwd(zqm, dyc, small["mem_q_norm"], km, vm, "mem_attn_bwd")
    gs["mem_q_norm"] = dmqg
    gw["mem_w_kv"], gs["mem_k_norm"], gs["mem_norm"] = _mem_kv_bwd(
        mem, small["mem_norm"], wkv, small["mem_k_norm"], dkn, dvm, "mem_kv_bwd")

    dzs = (dzuv, dzcq, dzckv, dzkr, dzqm, dzg)
    dws = list(_mm_tn_cols(h, dzs[:5], "dw_in_narrow")) + [_mm_tn(h, dzg, "dw_in_gate")]
    dws[3] = dws[3][:, MLA_NOPE:MLA_QK]
    gw["w_in"] = jnp.concatenate(dws, axis=1)
    dx1, gs["mix_norm"] = _proj_norm_bwd(dzs, [w.T for w in segs], x1, g_mix, dx2, "in_proj_bwd")
    mixer_sums = _pair_sum_exchange(_grad_pair_sums(MIXER_WEIGHTS, gw, core, "mixer"))
    dx, summed = ffn_grads("ffn1", x, g_ffn1, dx1, gpre1, upre1, wgu1, wd1, ex=mixer_sums, ex_names=MIXER_WEIGHTS,
                           last=True)
    ffn1_sums = _pair_sum_exchange(_grad_pair_sums(FFN1_WEIGHTS, gw, core, "ffn1"))
    slots.update(zip(FFN1_WEIGHTS, _run_exchange(ffn1_sums, "grad_exchange_ffn1")))
    return dx, slots, summed


def kernel(x, mem, positions, ffn1_norm, ffn1_w_gu, ffn1_w_down, mix_norm, w_in, b_gate, sg_ln_g, sg_ln_b, sg_w, sg_b, mla_cq_norm, mla_w_uq, mla_ckv_norm, mla_w_ukv, mla_q_norm, mla_k_norm, mem_norm, mem_w_kv, mem_q_norm, mem_k_norm, w_branch_a, w_branch_b, w_branch_c, w_out, ffn2_norm, ffn2_w_gu, ffn2_w_down, loss_target, m_ffn1_norm, m_ffn1_w_gu, m_ffn1_w_down, m_mix_norm, m_w_in, m_b_gate, m_sg_ln_g, m_sg_ln_b, m_sg_w, m_sg_b, m_mla_cq_norm, m_mla_w_uq, m_mla_ckv_norm, m_mla_w_ukv, m_mla_q_norm, m_mla_k_norm, m_mem_norm, m_mem_w_kv, m_mem_q_norm, m_mem_k_norm, m_w_branch_a, m_w_branch_b, m_w_branch_c, m_w_out, m_ffn2_norm, m_ffn2_w_gu, m_ffn2_w_down, v_ffn1_norm, v_ffn1_w_gu, v_ffn1_w_down, v_mix_norm, v_w_in, v_b_gate, v_sg_ln_g, v_sg_ln_b, v_sg_w, v_sg_b, v_mla_cq_norm, v_mla_w_uq, v_mla_ckv_norm, v_mla_w_ukv, v_mla_q_norm, v_mla_k_norm, v_mem_norm, v_mem_w_kv, v_mem_q_norm, v_mem_k_norm, v_w_branch_a, v_w_branch_b, v_w_branch_c, v_w_out, v_ffn2_norm, v_ffn2_w_gu, v_ffn2_w_down):
    args = dict(locals())
    weights = {n: args[n] for n in WEIGHT_ORDER}
    mom_m = {n: args["m_" + n] for n in WEIGHT_ORDER}
    mom_v = {n: args["v_" + n] for n in WEIGHT_ORDER}
    small = {n: weights[n] for n, _ in SMALL}
    halves = lambda a, r, c: a.reshape(2, r // 2, c)

    shards = {n: halves(weights[n][0].astype(BF16), r, c) for n, r, c, _ in SHARDED}
    core = lax.axis_index("c").astype(jnp.int32).reshape(1)
    dx, slots, summed = _device_step(x[0], mem[0], positions[0], loss_target[0], small, shards, core)
    loss = summed.reshape(-1)[_N_SMALL]
    small_grads = _unpack_small(summed)

    grads, deltas, new_m, new_v = {}, {}, {}, {}
    for name, r, c, _ in SHARDED:
        outs = _adamw_slots(halves(weights[name][0], r, c), slots[name], halves(mom_m[name][0], r, c),
                            halves(mom_v[name][0], r, c), "adamw_" + name)
        shape = weights[name].shape
        grads[name], deltas[name], new_m[name], new_v[name] = [o.reshape(shape) for o in outs]
    dlt, nm, nv = _adamw(_pack_small(small), _pack_small(small_grads), _pack_small({n: mom_m[n] for n, _ in SMALL}),
                         _pack_small({n: mom_v[n] for n, _ in SMALL}), "adamw_small")
    for name, _ in SMALL:
        grads[name] = small_grads[name]
    deltas.update(_unpack_small(dlt))
    new_m.update(_unpack_small(nm))
    new_v.update(_unpack_small(nv))

    return (loss, dx[None], *[grads[n] for n in WEIGHT_ORDER], *[deltas[n] for n in WEIGHT_ORDER],
            *[new_m[n] for n in WEIGHT_ORDER], *[new_v[n] for n in WEIGHT_ORDER])
```

```python
import functools
from typing import Callable, NamedTuple

import numpy as np
import jax
import jax.numpy as jnp
from jax import lax
from jax.experimental import pallas as pl
from jax.experimental.pallas import tpu as pltpu

F32 = jnp.float32
BF16 = jnp.bfloat16

D_MODEL = 1024
D_FF = 2816
FF_TILE = 1408
SG_WIDTH = 512
SG_GROUPS = 8
CHUNK = 128
MLA_HEADS = 8
MLA_QK = 96
MLA_NOPE = 64
MLA_ROPE = 32
MLA_Q_RANK = 384
MLA_KV_RANK = 256
MEM_HEADS = 4
LANES = 128
EPS = 1e-6
NEG = -1e30
ROPE_BASE = 10000.0
N_CHIPS = 4
N_DEV = 8

ADAM_LR = 0.001
ADAM_B1 = 0.9
ADAM_B2 = 0.999
ADAM_EPS = 1e-08
ADAM_WD = 0.01
ADAM_STEP = 10

COL_CQ = 1024
COL_CKV = 1408
COL_KR = 1664
COL_QM = 1696
COL_GATE = 2208

VMEM_LIMIT_BYTES = 56 * 1024 * 1024
INV_SQRT2 = 0.7071067811865476
INV_SQRT_2PI = 0.3989422804014327
LOG2E = 1.4426950408889634
ATTN_SCALE = MLA_QK ** -0.5
V_ONES_LANE = 64
ATTN_SCALE2 = ATTN_SCALE * LOG2E

SHARDED = (
    ("ffn1_w_gu", 1024, 1408, "col"),
    ("ffn1_w_down", 704, 1024, "row"),
    ("w_in", 1024, 1320, "col"),
    ("mla_w_uq", 384, 192, "col"),
    ("mla_w_ukv", 256, 256, "col"),
    ("mem_w_kv", 256, 1024, "row"),
    ("w_branch_a", 512, 256, "col"),
    ("w_branch_b", 512, 256, "col"),
    ("w_branch_c", 512, 256, "col"),
    ("w_out", 256, 1024, "row"),
    ("ffn2_w_gu", 1024, 1408, "col"),
    ("ffn2_w_down", 704, 1024, "row"),
)
SMALL = (
    ("ffn1_norm", (1, 1024)), ("mix_norm", (1, 1024)), ("b_gate", (1, 3072)),
    ("sg_ln_g", (1, 512)), ("sg_ln_b", (1, 512)), ("sg_w", (1, 8, 128, 128)),
    ("sg_b", (1, 8, 128)), ("mla_cq_norm", (1, 384)), ("mla_ckv_norm", (1, 256)),
    ("mla_q_norm", (1, 96)), ("mla_k_norm", (1, 96)), ("mem_norm", (1, 1024)),
    ("mem_q_norm", (1, 128)), ("mem_k_norm", (1, 128)), ("ffn2_norm", (1, 1024)),
)
WEIGHT_ORDER = (
    "ffn1_norm", "ffn1_w_gu", "ffn1_w_down", "mix_norm", "w_in", "b_gate", "sg_ln_g", "sg_ln_b",
    "sg_w", "sg_b", "mla_cq_norm", "mla_w_uq", "mla_ckv_norm", "mla_w_ukv", "mla_q_norm",
    "mla_k_norm", "mem_norm", "mem_w_kv", "mem_q_norm", "mem_k_norm", "w_branch_a", "w_branch_b",
    "w_branch_c", "w_out", "ffn2_norm", "ffn2_w_gu", "ffn2_w_down",
)

_N_SMALL = sum(int(np.prod(s)) for _, s in SMALL)
SMALL_ROWS = -(-_N_SMALL // (LANES * 8)) * 8

MESH = pl.DeviceIdType.MESH


def _cparams():
    return pltpu.CompilerParams(vmem_limit_bytes=VMEM_LIMIT_BYTES)


def _dot(a, b):
    return jnp.dot(a, b, preferred_element_type=F32)


def _dot_nt(a, b):
    return lax.dot_general(a, b, (((1,), (1,)), ((), ())), preferred_element_type=F32)


def _dot_tn(a, b):
    return lax.dot_general(a, b, (((0,), (0,)), ((), ())), preferred_element_type=F32)


def _gelu(x):
    return 0.5 * x * (1.0 + lax.erf(x * INV_SQRT2))


def _gelu_grad(x):
    return 0.5 * (1.0 + lax.erf(x * INV_SQRT2)) + x * jnp.exp(-0.5 * x * x) * INV_SQRT_2PI


def _rstd(x, n):
    return lax.rsqrt(jnp.sum(x * x, axis=-1, keepdims=True) * (1.0 / n) + EPS)


def _rms_vjp(x, r, g, dy, n):
    dxh = dy * g
    dx = r * dxh - x * (r * r * r) * (jnp.sum(dxh * x, axis=-1, keepdims=True) * (1.0 / n))
    return dx, dy * x * r


def _row_tile(t, want):
    return min(t, want)


def _wide_tile(n):
    if n <= 1024:
        return n
    if n % 1024 == 0:
        return 1024
    assert n % FF_TILE == 0, n
    return FF_TILE


def _mm_cols(a, ws, out_dtypes, name, ex=None):
    t, kdim = a.shape
    tm = _row_tile(t, 512)
    n = len(ws)

    def body(*refs):
        av = refs[0][...]
        for w_ref, o_ref in zip(refs[1:1 + n], refs[1 + n:]):
            o_ref[...] = _dot(av, w_ref[...]).astype(o_ref.dtype)

    row = lambda width: pl.BlockSpec((tm, width), lambda i: (i, 0))
    return _call_with_exchange(
        ex, body, name, (t // tm,),
        [row(kdim)] + [pl.BlockSpec(w.shape, lambda i: (0, 0)) for w in ws],
        [row(w.shape[1]) for w in ws],
        [jax.ShapeDtypeStruct((t, w.shape[1]), dt) for w, dt in zip(ws, out_dtypes)], [], (a, *ws))


def _proj_norm_bwd(dzs, wts, x, g, dres, name):
    t, d = x.shape
    tm = _row_tile(t, 256)
    n = len(dzs)

    def body(*refs):
        x_ref, g_ref, r_ref, dx_ref, dg_ref = refs[2 * n:]

        @pl.when(pl.program_id(0) == 0)
        def _():
            dg_ref[...] = jnp.zeros_like(dg_ref)

        dh = None
        for dz_ref, w_ref in zip(refs[:n], refs[n:2 * n]):
            part = _dot(dz_ref[...], w_ref[...])
            dh = part if dh is None else dh + part
        xv = x_ref[...]
        dx, dgr = _rms_vjp(xv, _rstd(xv, d), g_ref[...], dh, d)
        dx_ref[...] = r_ref[...] + dx
        dg_ref[...] += jnp.sum(dgr, axis=0, keepdims=True)

    row = lambda width: pl.BlockSpec((tm, width), lambda i: (i, 0))
    vec = pl.BlockSpec((1, d), lambda i: (0, 0))
    return pl.pallas_call(
        body, name=name, grid=(t // tm,),
        in_specs=[row(dz.shape[1]) for dz in dzs] + [pl.BlockSpec(w.shape, lambda i: (0, 0)) for w in wts]
        + [row(d), vec, row(d)],
        out_specs=[row(d), vec],
        out_shape=[jax.ShapeDtypeStruct((t, d), F32), jax.ShapeDtypeStruct((1, d), F32)],
        compiler_params=_cparams())(*dzs, *wts, x, g, dres)


def _mm_tn_cols(a, bs, name):
    t, m = a.shape
    tk = _row_tile(t, 1024)
    n = len(bs)

    def body(*refs):
        @pl.when(pl.program_id(0) == 0)
        def _():
            for o_ref in refs[1 + n:]:
                o_ref[...] = jnp.zeros_like(o_ref)

        av = refs[0][...].astype(BF16)
        for b_ref, o_ref in zip(refs[1:1 + n], refs[1 + n:]):
            o_ref[...] += _dot_tn(av, b_ref[...].astype(BF16))

    row = lambda width: pl.BlockSpec((tk, width), lambda k: (k, 0))
    return pl.pallas_call(
        body, name=name, grid=(t // tk,), in_specs=[row(m)] + [row(b.shape[1]) for b in bs],
        out_specs=[pl.BlockSpec((m, b.shape[1]), lambda k: (0, 0)) for b in bs],
        out_shape=[jax.ShapeDtypeStruct((m, b.shape[1]), F32) for b in bs],
        compiler_params=_cparams())(a, *bs)


def _mm_tn(a, b, name, scale=1.0, ex=None, col_blocks=False, out_dtype=F32):
    t, m = a.shape
    n = b.shape[1]
    tm, tn = _wide_tile(m), _wide_tile(n)
    tk = _row_tile(t, 2048)
    nk = t // tk
    in_place = out_dtype == F32

    def body(a_ref, b_ref, o_ref, *scr):
        k = pl.program_id(2)
        acc_ref = o_ref if in_place else scr[0]

        @pl.when(k == 0)
        def _():
            acc_ref[...] = jnp.zeros_like(acc_ref)

        prod = _dot_tn(a_ref[...].astype(BF16), b_ref[...].astype(BF16))
        acc_ref[...] += prod.reshape(acc_ref.shape)
        if scale != 1.0 or not in_place:
            @pl.when(k == nk - 1)
            def _():
                o_ref[...] = (acc_ref[...] * scale).astype(out_dtype).reshape(o_ref.shape)

    if col_blocks:
        out_spec = pl.BlockSpec((1, tm, tn), lambda i, j, k: (j, i, 0))
        out_shape = jax.ShapeDtypeStruct((n // tn, m, tn), out_dtype)
    else:
        out_spec = pl.BlockSpec((tm, tn), lambda i, j, k: (i, j))
        out_shape = jax.ShapeDtypeStruct((m, n), out_dtype)
    outs = _call_with_exchange(
        ex, body, name, (m // tm, n // tn, nk),
        [pl.BlockSpec((tk, tm), lambda i, j, k: (k, i)), pl.BlockSpec((tk, tn), lambda i, j, k: (k, j))],
        [out_spec], [out_shape], [] if in_place else [pltpu.VMEM((tm, tn), F32)], (a, b))
    return outs[0] if ex is None else outs


PASS_ON_STEPS_BEFORE_END = 8


class _Exchange(NamedTuple):
    operands: list
    out_shapes: list
    sem_shapes: list
    build: Callable


def _call_with_exchange(ex, body, name, grid, in_specs, out_specs, out_shape, scratch_shapes, operands, prefetch=()):
    n_pre = len(prefetch)
    total = int(np.prod(grid))
    pass_step = max(total // 2, total - PASS_ON_STEPS_BEFORE_END)

    def call(kernel, ins, outs, shapes, scratch):
        if n_pre:
            spec = pltpu.PrefetchScalarGridSpec(num_scalar_prefetch=n_pre, grid=grid, in_specs=ins, out_specs=outs,
                                                scratch_shapes=scratch)
            return pl.pallas_call(kernel, name=name, grid_spec=spec, out_shape=shapes, compiler_params=_cparams())
        return pl.pallas_call(kernel, name=name, grid=grid, in_specs=ins, out_specs=outs, out_shape=shapes,
                              scratch_shapes=scratch, compiler_params=_cparams())

    if ex is None:
        return call(body, in_specs, out_specs, out_shape, scratch_shapes)(*prefetch, *operands)
    n_in, n_out, n_scr = len(in_specs), len(out_specs), len(scratch_shapes)
    k_in, k_out = len(ex.operands), len(ex.out_shapes)

    def carried(*refs):
        pre, refs = refs[:n_pre], refs[n_pre:]
        a, b = n_in, n_in + k_in
        c, e = b + n_out, b + n_out + k_out
        f = e + n_scr
        start, pass_on, finish = ex.build(refs[a:b], refs[c:e], refs[f:])
        step = functools.reduce(lambda lin, ax: lin * grid[ax] + pl.program_id(ax), range(len(grid)), 0)
        pl.when(step == 0)(start)
        body(*pre, *refs[:a], *refs[b:c], *refs[e:f])
        pl.when(step == pass_step)(pass_on)
        pl.when(step == total - 1)(finish)

    return call(carried, list(in_specs) + [ANY] * k_in, list(out_specs) + [ANY] * k_out,
                list(out_shape) + list(ex.out_shapes), list(scratch_shapes) + list(ex.sem_shapes),
                )(*prefetch, *operands, *ex.operands)


def _run_exchange(ex, name):
    k_in, k_out = len(ex.operands), len(ex.out_shapes)

    def body(*refs):
        start, pass_on, finish = ex.build(refs[:k_in], refs[k_in:k_in + k_out], refs[k_in + k_out:])
        start()
        pass_on()
        finish()

    return pl.pallas_call(body, name=name, in_specs=[ANY] * k_in, out_specs=[ANY] * k_out,
                          out_shape=list(ex.out_shapes), scratch_shapes=list(ex.sem_shapes))(*ex.operands)


def _ffn_fwd(x, g, wgu4, wd2, name, ex=None, next_gain=None, target=None):
    t, d = x.shape
    tm = _row_tile(t, 512)
    assert next_gain is None or target is None
    extra = [a for a in (next_gain, target) if a is not None]

    def body(*refs):
        x_ref, g_ref, wg_ref, wu_ref, wd_ref = refs[:5]
        e_ref = refs[5] if extra else None
        outs, (xn_scr, acc_scr) = refs[5 + len(extra):-2], refs[-2:]
        if target is not None:
            dy_ref, loss_ref, gg_ref, uu_ref = outs
        elif next_gain is not None:
            o_ref, gg_ref, uu_ref, h_ref = outs
        else:
            o_ref, gg_ref, uu_ref = outs
        i, j = pl.program_id(0), pl.program_id(1)

        @pl.when(j == 0)
        def _():
            xv = x_ref[...]
            xn_scr[...] = (xv * _rstd(xv, d) * g_ref[...]).astype(BF16)
            acc_scr[...] = jnp.zeros_like(acc_scr)

        if target is not None:
            @pl.when((i == 0) & (j == 0))
            def _():
                loss_ref[...] = jnp.zeros_like(loss_ref)

        xn = xn_scr[...]
        gg = _dot(xn, wg_ref[0])
        uu = _dot(xn, wu_ref[0])
        gg_ref[...] = gg.astype(BF16)
        uu_ref[...] = uu.astype(BF16)
        act = gg * jax.nn.sigmoid(gg) * uu
        acc_scr[...] += _dot(act.astype(BF16), wd_ref[0])

        @pl.when(j == 1)
        def _():
            y = x_ref[...] + 0.5 * acc_scr[...]
            if target is not None:
                e = y - e_ref[...]
                dy_ref[...] = e * (1.0 / d)
                part = 0.5 * jnp.sum(jnp.sum(e * e, axis=-1, keepdims=True) * (1.0 / d), axis=0, keepdims=True)
                loss_ref[...] += jnp.broadcast_to(part, loss_ref.shape)
            else:
                o_ref[...] = y
                if next_gain is not None:
                    h_ref[...] = (y * _rstd(y, d) * e_ref[...]).astype(BF16)

    row = pl.BlockSpec((tm, d), lambda i, j: (i, 0))
    vec = pl.BlockSpec((1, d), lambda i, j: (0, 0))
    ffb = pl.BlockSpec((tm, FF_TILE), lambda i, j: (i, j))
    f32_rows, bf16_ff = jax.ShapeDtypeStruct((t, d), F32), jax.ShapeDtypeStruct((t, D_FF), BF16)
    if target is not None:
        extra_spec, out_specs = [row], [row, pl.BlockSpec((1, LANES), lambda i, j: (0, 0)), ffb, ffb]
        out_shape = [f32_rows, jax.ShapeDtypeStruct((1, LANES), F32), bf16_ff, bf16_ff]
    elif next_gain is not None:
        extra_spec, out_specs = [vec], [row, ffb, ffb, row]
        out_shape = [f32_rows, bf16_ff, bf16_ff, jax.ShapeDtypeStruct((t, d), BF16)]
    else:
        extra_spec, out_specs, out_shape = [], [row, ffb, ffb], [f32_rows, bf16_ff, bf16_ff]
    return _call_with_exchange(
        ex, body, name, (t // tm, 2),
        [row, vec,
         pl.BlockSpec((1, d, FF_TILE), lambda i, j: (j, 0, 0)),
         pl.BlockSpec((1, d, FF_TILE), lambda i, j: (j + 2, 0, 0)),
         pl.BlockSpec((1, FF_TILE, d), lambda i, j: (j, 0, 0))] + extra_spec,
        out_specs, out_shape,
        [pltpu.VMEM((tm, d), BF16), pltpu.VMEM((tm, d), F32)], (x, g, wgu4, wgu4, wd2, *extra))


def _ffn_bwd(x, g, dy, gpre, upre, wgu4, wd2, name, ex=None):
    t, d = x.shape
    tm = _row_tile(t, 512)

    def body(dy_ref, gg_ref, uu_ref, wgu_hbm, wd_hbm, dg_ref, du_ref, act_ref, part_ref, wg_ref, wu_ref, wd_ref):
        j = pl.program_id(0)

        @pl.when(pl.program_id(1) == 0)
        def _():
            pltpu.sync_copy(wgu_hbm.at[j], wg_ref.at[0])
            pltpu.sync_copy(wgu_hbm.at[j + 2], wu_ref.at[0])
            pltpu.sync_copy(wd_hbm.at[j], wd_ref.at[0])

        gg = gg_ref[...].astype(F32)
        uu = uu_ref[...].astype(F32)
        sg = jax.nn.sigmoid(gg)
        silu = gg * sg
        act_ref[...] = (silu * uu).astype(BF16)
        dyh = (0.5 * dy_ref[...]).astype(BF16)
        dact = _dot_nt(dyh, wd_ref[0])
        du = (dact * silu).astype(BF16)
        dgt = (dact * uu * (sg * (1.0 + gg * (1.0 - sg)))).astype(BF16)
        du_ref[...] = du
        dg_ref[...] = dgt
        part_ref[0] = (_dot_nt(dgt, wg_ref[0]) + _dot_nt(du, wu_ref[0])).astype(BF16)

    row = pl.BlockSpec((tm, d), lambda j, i: (i, 0))
    ffb = pl.BlockSpec((tm, FF_TILE), lambda j, i: (i, j))
    dgt, dup, act, parts, *got = _call_with_exchange(
        ex, body, name, (2, t // tm),
        [row, ffb, ffb, ANY, ANY],
        [ffb, ffb, ffb, pl.BlockSpec((1, tm, d), lambda j, i: (j, i, 0))],
        [jax.ShapeDtypeStruct((t, D_FF), BF16)] * 3 + [jax.ShapeDtypeStruct((2, t, d), BF16)],
        [pltpu.VMEM((1, d, FF_TILE), BF16), pltpu.VMEM((1, d, FF_TILE), BF16), pltpu.VMEM((1, FF_TILE, d), BF16)],
        (dy, gpre, upre, wgu4, wd2))

    def norm_body(x_ref, g_ref, p_ref, dy_ref, dx_ref, dgain_ref, xn_ref):
        @pl.when(pl.program_id(0) == 0)
        def _():
            dgain_ref[...] = jnp.zeros_like(dgain_ref)

        xv = x_ref[...]
        r = _rstd(xv, d)
        xn_ref[...] = (xv * r * g_ref[...]).astype(BF16)
        dx, dgr = _rms_vjp(xv, r, g_ref[...], p_ref[0].astype(F32) + p_ref[1].astype(F32), d)
        dx_ref[...] = dy_ref[...] + dx
        dgain_ref[...] += jnp.sum(dgr, axis=0, keepdims=True)

    tn = _row_tile(t, 256)
    nrow = pl.BlockSpec((tn, d), lambda i: (i, 0))
    vec = pl.BlockSpec((1, d), lambda i: (0, 0))
    dx, dgain, xn = pl.pallas_call(
        norm_body, name=name + "_norm", grid=(t // tn,),
        in_specs=[nrow, vec, pl.BlockSpec((2, tn, d), lambda i: (0, i, 0)), nrow],
        out_specs=[nrow, vec, nrow],
        out_shape=[jax.ShapeDtypeStruct((t, d), F32), jax.ShapeDtypeStruct((1, d), F32),
                   jax.ShapeDtypeStruct((t, d), BF16)],
        compiler_params=_cparams())(x, g, parts, dy)
    return [dx, dgain, xn, dgt, dup, act] + got


def _sgu_layernorm(vpre, lg, lb):
    v = _gelu(vpre)
    mu = jnp.mean(v, axis=-1, keepdims=True)
    xc = v - mu
    rstd = lax.rsqrt(jnp.mean(xc * xc, axis=-1, keepdims=True) + EPS)
    xhat = xc * rstd
    return xhat, rstd, xhat * lg + lb


def _sgu_fwd(zuv, lg, lb, wt, bias_l, name):
    t = zuv.shape[0]
    tm = _row_tile(t, 512)

    def body(u_ref, v_ref, lg_ref, lb_ref, wt_ref, bl_ref, o_ref, vln_scr):
        _, _, vln = _sgu_layernorm(v_ref[...], lg_ref[...], lb_ref[...])
        vln_scr[...] = vln.astype(BF16)
        lo = lax.broadcasted_iota(jnp.int32, (CHUNK, LANES), 1) < 64
        for c in range(tm // CHUNK):
            rows = slice(c * CHUNK, (c + 1) * CHUNK)
            for p in range(SG_GROUPS // 2):
                cols = slice(p * LANES, (p + 1) * LANES)
                vp = vln_scr[rows, cols]
                mixed = jnp.where(lo, _dot(wt_ref[2 * p], vp), _dot(wt_ref[2 * p + 1], vp)) + bl_ref[:, cols]
                o_ref[rows, cols] = (_gelu(u_ref[rows, cols]) * mixed).astype(BF16)

    half = lambda k: pl.BlockSpec((tm, SG_WIDTH), lambda i: (i, k))
    vec = pl.BlockSpec((1, SG_WIDTH), lambda i: (0, 0))
    return pl.pallas_call(
        body, name=name, grid=(t // tm,),
        in_specs=[half(0), half(1), vec, vec,
                  pl.BlockSpec((SG_GROUPS, CHUNK, CHUNK), lambda i: (0, 0, 0)),
                  pl.BlockSpec((CHUNK, SG_WIDTH), lambda i: (0, 0))],
        out_specs=pl.BlockSpec((tm, SG_WIDTH), lambda i: (i, 0)),
        out_shape=jax.ShapeDtypeStruct((t, SG_WIDTH), BF16),
        scratch_shapes=[pltpu.VMEM((tm, SG_WIDTH), BF16)],
        compiler_params=_cparams())(zuv, zuv, lg, lb, wt, bias_l)


def _sgu_bwd(zuv, dya, lg, lb, wt, wt_t, bias_l, name):
    t = zuv.shape[0]
    tm = _row_tile(t, 256)
    nsteps = t // tm

    def body(u_ref, v_ref, dy_ref, lg_ref, lb_ref, wt_ref, wtt_ref, bl_ref,
             dz_ref, dwt_ref, dbl_ref, dlg_ref, dlb_ref, vln_scr, dvln_scr, dbacc_scr):
        step = pl.program_id(0)

        @pl.when(step == 0)
        def _():
            dwt_ref[...] = jnp.zeros_like(dwt_ref)
            dlg_ref[...] = jnp.zeros_like(dlg_ref)
            dlb_ref[...] = jnp.zeros_like(dlb_ref)
            dbl_ref[...] = jnp.zeros_like(dbl_ref)
            dbacc_scr[...] = jnp.zeros_like(dbacc_scr)

        vpre = v_ref[...]
        lgv = lg_ref[...]
        xhat, rstd, vln = _sgu_layernorm(vpre, lgv, lb_ref[...])
        vln_scr[...] = vln.astype(BF16)
        lo = lax.broadcasted_iota(jnp.int32, (CHUNK, LANES), 1) < 64
        for c in range(tm // CHUNK):
            rows = slice(c * CHUNK, (c + 1) * CHUNK)
            for p in range(SG_GROUPS // 2):
                cols = slice(p * LANES, (p + 1) * LANES)
                vp = vln_scr[rows, cols]
                mixed = jnp.where(lo, _dot(wt_ref[2 * p], vp), _dot(wt_ref[2 * p + 1], vp)) + bl_ref[:, cols]
                upre = u_ref[rows, cols]
                dyp = dy_ref[rows, cols]
                dz_ref[rows, cols] = (dyp * mixed * _gelu_grad(upre)).astype(BF16)
                dm = dyp * _gelu(upre)
                dbacc_scr[:, cols] += dm
                dlo = jnp.where(lo, dm, 0.0).astype(BF16)
                dhi = jnp.where(lo, 0.0, dm).astype(BF16)
                dvln_scr[rows, cols] = _dot(wtt_ref[2 * p], dlo) + _dot(wtt_ref[2 * p + 1], dhi)
                dwt_ref[2 * p] += _dot_nt(dlo, vp)
                dwt_ref[2 * p + 1] += _dot_nt(dhi, vp)
        dvln = dvln_scr[...]
        dlg_ref[...] += jnp.sum(dvln * xhat, axis=0, keepdims=True)
        dlb_ref[...] += jnp.sum(dvln, axis=0, keepdims=True)
        dxh = dvln * lgv
        dv = rstd * (dxh - jnp.mean(dxh, axis=-1, keepdims=True)
                     - xhat * jnp.mean(dxh * xhat, axis=-1, keepdims=True))
        dz_ref[:, SG_WIDTH:] = (dv * _gelu_grad(vpre)).astype(BF16)

        @pl.when(step == nsteps - 1)
        def _():
            rr = lax.broadcasted_iota(jnp.int32, (CHUNK, CHUNK), 0)
            cc = lax.broadcasted_iota(jnp.int32, (CHUNK, CHUNK), 1)
            tril = (cc <= rr).astype(F32)
            for gidx in range(SG_GROUPS):
                dwt_ref[gidx] = dwt_ref[gidx] * tril
            kk = lax.broadcasted_iota(jnp.int32, (SG_WIDTH, LANES), 0)
            gg = lax.broadcasted_iota(jnp.int32, (SG_WIDTH, LANES), 1)
            sel = ((kk // 64) == gg).astype(F32)
            dbl_ref[...] = jnp.dot(dbacc_scr[...], sel, preferred_element_type=F32,
                                   precision=lax.Precision.HIGHEST)

    half = lambda k: pl.BlockSpec((tm, SG_WIDTH), lambda i: (i, k))
    vec = pl.BlockSpec((1, SG_WIDTH), lambda i: (0, 0))
    wspec = pl.BlockSpec((SG_GROUPS, CHUNK, CHUNK), lambda i: (0, 0, 0))
    return pl.pallas_call(
        body, name=name, grid=(nsteps,),
        in_specs=[half(0), half(1), pl.BlockSpec((tm, SG_WIDTH), lambda i: (i, 0)), vec, vec,
                  wspec, wspec, pl.BlockSpec((CHUNK, SG_WIDTH), lambda i: (0, 0))],
        out_specs=[pl.BlockSpec((tm, 2 * SG_WIDTH), lambda i: (i, 0)), wspec,
                   pl.BlockSpec((CHUNK, LANES), lambda i: (0, 0)), vec, vec],
        out_shape=[jax.ShapeDtypeStruct((t, 2 * SG_WIDTH), BF16),
                   jax.ShapeDtypeStruct((SG_GROUPS, CHUNK, CHUNK), F32),
                   jax.ShapeDtypeStruct((CHUNK, LANES), F32),
                   jax.ShapeDtypeStruct((1, SG_WIDTH), F32), jax.ShapeDtypeStruct((1, SG_WIDTH), F32)],
        scratch_shapes=[pltpu.VMEM((tm, SG_WIDTH), BF16), pltpu.VMEM((tm, SG_WIDTH), F32),
                        pltpu.VMEM((CHUNK, SG_WIDTH), F32)],
        compiler_params=_cparams())(zuv, zuv, dya, lg, lb, wt, wt_t, bias_l)


def _rope(x, c, s1, s2):
    return x * c + pltpu.roll(x, LANES - 16, 1) * s1 + pltpu.roll(x, 16, 1) * s2


def _rope_t(dy, c, s1, s2):
    return dy * c + pltpu.roll(dy * s1, 16, 1) + pltpu.roll(dy * s2, LANES - 16, 1)


def _mla_prep_fwd(zcq, zckv, zkr, gcq, gckv, qg, kg, wuq, wuk, wuv, rc, rs1, rs2, name, ex=None):
    t = zcq.shape[0]
    tm = _row_tile(t, 256)
    hd = MLA_HEADS * LANES

    def body(zcq_ref, zckv_ref, zkr_ref, gcq_ref, gckv_ref, qg_ref, kg_ref, wuq_ref, wuk_ref, wuv_ref,
             c_ref, s1_ref, s2_ref, q_ref, k_ref, v_ref, cqn_ref, ckvn_ref):
        c, s1, s2 = c_ref[...], s1_ref[...], s2_ref[...]
        xq = zcq_ref[...]
        cqn = (xq * _rstd(xq, MLA_Q_RANK) * gcq_ref[...]).astype(BF16)
        cqn_ref[...] = cqn
        ql = _dot(cqn, wuq_ref[...])
        xk = zckv_ref[...]
        ckvn = (xk * _rstd(xk, MLA_KV_RANK) * gckv_ref[...]).astype(BF16)
        ckvn_ref[...] = ckvn
        kl = _dot(ckvn, wuk_ref[...])
        slot_lane = lax.broadcasted_iota(jnp.int32, (tm, hd), 1) % LANES
        v_ref[...] = jnp.where(slot_lane == V_ONES_LANE, 1.0, _dot(ckvn, wuv_ref[...])).astype(BF16)
        kr = zkr_ref[...]
        for h in range(MLA_HEADS):
            sl = slice(h * LANES, (h + 1) * LANES)
            qh = ql[:, sl]
            q_ref[:, sl] = (_rope(qh * _rstd(qh, MLA_QK) * qg_ref[...], c, s1, s2) * ATTN_SCALE2).astype(BF16)
            kh = kl[:, sl] + kr
            k_ref[:, sl] = _rope(kh * _rstd(kh, MLA_QK) * kg_ref[...], c, s1, s2).astype(BF16)

    row = lambda n: pl.BlockSpec((tm, n), lambda i: (i, 0))
    full = lambda a: pl.BlockSpec(a.shape, lambda i: (0, 0))
    return _call_with_exchange(
        ex, body, name, (t // tm,),
        [row(MLA_Q_RANK), row(MLA_KV_RANK), row(LANES), full(gcq), full(gckv), full(qg), full(kg),
         full(wuq), full(wuk), full(wuv), row(LANES), row(LANES), row(LANES)],
        [row(hd), row(hd), row(hd), row(MLA_Q_RANK), row(MLA_KV_RANK)],
        [jax.ShapeDtypeStruct((t, hd), BF16)] * 3
        + [jax.ShapeDtypeStruct((t, MLA_Q_RANK), BF16), jax.ShapeDtypeStruct((t, MLA_KV_RANK), BF16)],
        [], (zcq, zckv, zkr, gcq, gckv, qg, kg, wuq, wuk, wuv, rc, rs1, rs2))


def _mla_prep_bwd(zcq, zckv, zkr, gcq, gckv, qg, kg, wuq, wuk, wuv, rc, rs1, rs2, dq, dk, dv, name):
    t = zcq.shape[0]
    tm = _row_tile(t, 256)
    hd = MLA_HEADS * LANES

    def body(zcq_ref, zckv_ref, zkr_ref, gcq_ref, gckv_ref, qg_ref, kg_ref, wuq_ref, wuk_ref, wuv_ref,
             c_ref, s1_ref, s2_ref, dq_ref, dk_ref, dv_ref,
             dzcq_ref, dzckv_ref, dzkr_ref, dql_ref, dkl_ref, dgcq_ref, dgckv_ref, dqg_ref, dkg_ref):
        @pl.when(pl.program_id(0) == 0)
        def _():
            for ref in (dgcq_ref, dgckv_ref, dqg_ref, dkg_ref):
                ref[...] = jnp.zeros_like(ref)

        c, s1, s2 = c_ref[...], s1_ref[...], s2_ref[...]
        qgv, kgv = qg_ref[...], kg_ref[...]
        xq = zcq_ref[...]
        rq = _rstd(xq, MLA_Q_RANK)
        ql = _dot((xq * rq * gcq_ref[...]).astype(BF16), wuq_ref[...])
        xk = zckv_ref[...]
        rk = _rstd(xk, MLA_KV_RANK)
        kl = _dot((xk * rk * gckv_ref[...]).astype(BF16), wuk_ref[...])
        kr = zkr_ref[...]
        dqg_acc = jnp.zeros((tm, LANES), F32)
        dkg_acc = jnp.zeros((tm, LANES), F32)
        dkr = jnp.zeros((tm, LANES), F32)
        for h in range(MLA_HEADS):
            sl = slice(h * LANES, (h + 1) * LANES)
            qh = ql[:, sl]
            dqh, dgr = _rms_vjp(qh, _rstd(qh, MLA_QK), qgv, _rope_t(dq_ref[:, sl], c, s1, s2), MLA_QK)
            dql_ref[:, sl] = dqh.astype(BF16)
            dqg_acc += dgr
            kh = kl[:, sl] + kr
            dkh, dgr = _rms_vjp(kh, _rstd(kh, MLA_QK), kgv, _rope_t(dk_ref[:, sl], c, s1, s2), MLA_QK)
            dkl_ref[:, sl] = dkh.astype(BF16)
            dkg_acc += dgr
            dkr += dkh
        dqg_ref[...] += jnp.sum(dqg_acc, axis=0, keepdims=True)
        dkg_ref[...] += jnp.sum(dkg_acc, axis=0, keepdims=True)
        lane = lax.broadcasted_iota(jnp.int32, (tm, LANES), 1)
        dzkr_ref[...] = jnp.where((lane >= MLA_NOPE) & (lane < MLA_QK), dkr, 0.0).astype(BF16)
        dcqn = _dot_nt(dql_ref[...], wuq_ref[...])
        dx, dgr = _rms_vjp(xq, rq, gcq_ref[...], dcqn, MLA_Q_RANK)
        dzcq_ref[...] = dx.astype(BF16)
        dgcq_ref[...] += jnp.sum(dgr, axis=0, keepdims=True)
        dckvn = _dot_nt(dkl_ref[...], wuk_ref[...]) + _dot_nt(dv_ref[...].astype(BF16), wuv_ref[...])
        dx, dgr = _rms_vjp(xk, rk, gckv_ref[...], dckvn, MLA_KV_RANK)
        dzckv_ref[...] = dx.astype(BF16)
        dgckv_ref[...] += jnp.sum(dgr, axis=0, keepdims=True)

    row = lambda n: pl.BlockSpec((tm, n), lambda i: (i, 0))
    full = lambda a: pl.BlockSpec(a.shape, lambda i: (0, 0))
    vec = lambda n: pl.BlockSpec((1, n), lambda i: (0, 0))
    return pl.pallas_call(
        body, name=name, grid=(t // tm,),
        in_specs=[row(MLA_Q_RANK), row(MLA_KV_RANK), row(LANES), full(gcq), full(gckv), full(qg), full(kg),
                  full(wuq), full(wuk), full(wuv), row(LANES), row(LANES), row(LANES), row(hd), row(hd), row(hd)],
        out_specs=[row(MLA_Q_RANK), row(MLA_KV_RANK), row(LANES), row(hd), row(hd),
                   vec(MLA_Q_RANK), vec(MLA_KV_RANK), vec(LANES), vec(LANES)],
        out_shape=[jax.ShapeDtypeStruct((t, MLA_Q_RANK), BF16), jax.ShapeDtypeStruct((t, MLA_KV_RANK), BF16),
                   jax.ShapeDtypeStruct((t, LANES), BF16), jax.ShapeDtypeStruct((t, hd), BF16),
                   jax.ShapeDtypeStruct((t, hd), BF16), jax.ShapeDtypeStruct((1, MLA_Q_RANK), F32),
                   jax.ShapeDtypeStruct((1, MLA_KV_RANK), F32), jax.ShapeDtypeStruct((1, LANES), F32),
                   jax.ShapeDtypeStruct((1, LANES), F32)],
        compiler_params=_cparams(),
    )(zcq, zckv, zkr, gcq, gckv, qg, kg, wuq, wuk, wuv, rc, rs1, rs2, dq, dk, dv)


def _attn_tiles(t):
    tq = 512 if t >= 2048 else 128
    return tq, min(t, 4 * tq), min(t, 4 * tq)


def _causal_keep(tq, nk, i, j, tk):
    row = lax.broadcasted_iota(jnp.int32, (tq, nk), 0)
    col = lax.broadcasted_iota(jnp.int32, (tq, nk), 1)
    return (col - row) <= (i * tq - j * tk)


def _causal_keep_t(tq, nk, i, j, tk):
    key = lax.broadcasted_iota(jnp.int32, (nk, tq), 0)
    qry = lax.broadcasted_iota(jnp.int32, (nk, tq), 1)
    return (key - qry) <= (i * tq - j * tk)


ATTN_FWD_HEADS_PER_STEP = 4
ATTN_BWD_HEADS_PER_STEP = 2


def _attn_fwd(q, k, v, name, ex=None):
    t, hd = q.shape
    hp = ATTN_FWD_HEADS_PER_STEP
    tq, tk, _ = _attn_tiles(t)
    pairs = [(i, j) for i in range(t // tq) for j in range(((i + 1) * tq - 1) // tk + 1)]
    ii = np.array([p[0] for p in pairs], np.int32)
    jj = np.array([p[1] for p in pairs], np.int32)

    def body(ii_ref, jj_ref, q_ref, k_ref, v_ref, o_ref, lse_ref, m_scr, acc_scr):
        s_id = pl.program_id(1)
        i, j = ii_ref[s_id], jj_ref[s_id]
        last = j == ((i + 1) * tq - 1) // tk
        ones_lane = lax.broadcasted_iota(jnp.int32, (tq, LANES), 1) == V_ONES_LANE

        @pl.when(j == 0)
        def _():
            m_scr[...] = jnp.full_like(m_scr, NEG)
            acc_scr[...] = jnp.zeros_like(acc_scr)

        def step(masked, nk):
            scores = [_dot_nt(q_ref[:, hh * LANES:(hh + 1) * LANES], k_ref[:nk, hh * LANES:(hh + 1) * LANES])
                      for hh in range(hp)]
            for hh in range(hp):
                sl = slice(hh * LANES, (hh + 1) * LANES)
                s = scores[hh]
                if masked:
                    s = jnp.where(_causal_keep(tq, nk, i, j, tk), s, NEG)
                m_prev = m_scr[hh]
                m_new = jnp.maximum(m_prev, jnp.max(s, axis=1, keepdims=True))
                p = jnp.exp2(s - m_new)
                alpha = jnp.exp2(m_prev - m_new)
                acc = alpha * acc_scr[:, sl] + _dot(p.astype(BF16), v_ref[:nk, sl])
                if masked:
                    l_new = jnp.sum(jnp.where(ones_lane, acc, 0.0), axis=1, keepdims=True)
                    o_ref[:, sl] = (acc / l_new).astype(BF16)
                    lse_ref[:, sl] = jnp.broadcast_to(m_new + jnp.log(l_new) * LOG2E, (tq, LANES))
                else:
                    acc_scr[:, sl] = acc
                    m_scr[hh] = m_new

        @pl.when(jnp.logical_not(last))
        def _():
            step(False, tk)

        r = (((i + 1) * tq - 1) % tk) // tq
        for rr in range(tk // tq):
            @pl.when(last & (r == rr))
            def _():
                step(True, (rr + 1) * tq)

    w = hp * LANES
    qspec = pl.BlockSpec((tq, w), lambda h, s, ii_r, jj_r: (ii_r[s], h))
    kspec = pl.BlockSpec((tk, w), lambda h, s, ii_r, jj_r: (jj_r[s], h))
    return _call_with_exchange(
        ex, body, name, (hd // w, len(pairs)), [qspec, kspec, kspec], [qspec, qspec],
        [jax.ShapeDtypeStruct((t, hd), BF16), jax.ShapeDtypeStruct((t, hd), F32)],
        [pltpu.VMEM((hp, tq, 1), F32), pltpu.VMEM((tq, w), F32)], (q, k, v),
        prefetch=(jnp.asarray(ii), jnp.asarray(jj)))


def _attn_bwd_rows(o, lse, do, name):
    t, hd = o.shape
    heads = hd // LANES
    tm = _row_tile(t, 512)

    def body(o_ref, lse_ref, do_ref, out_ref):
        lane = lax.broadcasted_iota(jnp.int32, (tm, LANES), 1)
        acc = jnp.zeros((tm, LANES), F32)
        for h in range(heads):
            sl = slice(h * LANES, (h + 1) * LANES)
            delta = jnp.sum(do_ref[:, sl].astype(F32) * o_ref[:, sl].astype(F32), axis=1, keepdims=True)
            acc = jnp.where(lane == h, delta, acc)
            acc = jnp.where(lane == heads + h, lse_ref[:, sl], acc)
        out_ref[...] = acc

    row = pl.BlockSpec((tm, hd), lambda i: (i, 0))
    cols = pl.pallas_call(
        body, name=name, grid=(t // tm,), in_specs=[row, row, row],
        out_specs=pl.BlockSpec((tm, LANES), lambda i: (i, 0)),
        out_shape=jax.ShapeDtypeStruct((t, LANES), F32), compiler_params=_cparams())(o, lse, do)
    rows = cols.T
    return rows[:heads].reshape(heads, 1, t), rows[heads:2 * heads].reshape(heads, 1, t)


def _attn_bwd(q, k, v, delta_rows, lse_rows, do, name):
    t, hd = q.shape
    hp = ATTN_BWD_HEADS_PER_STEP
    tq, _, tk = _attn_tiles(t)
    nq = t // tq
    pairs = [(i, j) for j in range(t // tk) for i in range((j * tk) // tq, nq)]
    ii = np.array([p[0] for p in pairs], np.int32)
    jj = np.array([p[1] for p in pairs], np.int32)

    def body(jj_ref, ii_ref, q_ref, k_ref, v_ref, delta_ref, lse_ref, do_ref, dq_ref, dk_ref, dv_ref,
             dk_scr, dv_scr, dq_scr):
        s_id = pl.program_id(1)
        i, j = ii_ref[s_id], jj_ref[s_id]

        @pl.when(s_id == 0)
        def _():
            dq_scr[...] = jnp.zeros_like(dq_scr)

        @pl.when(i == (j * tk) // tq)
        def _():
            dk_scr[...] = jnp.zeros_like(dk_scr)
            dv_scr[...] = jnp.zeros_like(dv_scr)

        rows = pl.ds(pl.multiple_of(i * tq, tq), tq)

        def step(masked, nk):
            heads = [slice(hh * LANES, (hh + 1) * LANES) for hh in range(hp)]
            scores = [_dot_nt(k_ref[:nk, sl], q_ref[:, sl]) for sl in heads]
            for hh, sl in enumerate(heads):
                qv, kv, dov = q_ref[:, sl], k_ref[:nk, sl], do_ref[:, sl]
                st = scores[hh]
                if masked:
                    st = jnp.where(_causal_keep_t(tq, nk, i, j, tk), st, NEG)
                pt = jnp.exp2(st - lse_ref[hh])
                dv_scr[:nk, sl] += _dot(pt.astype(BF16), dov)
                dpt = _dot_nt(v_ref[:nk, sl], dov)
                dst = (pt * (dpt - delta_ref[hh]) * ATTN_SCALE).astype(BF16)
                dk_scr[:nk, sl] += _dot(dst, qv)
                dq_scr[rows, sl] += _dot_tn(dst, kv)

        seen = jnp.minimum((i + 1) * tq - j * tk, tk)
        for nk in range(tq, tk + 1, tq):
            @pl.when((seen == nk) & ((i + 1) * tq - j * tk <= tk))
            def _():
                step(True, nk)

        @pl.when((i + 1) * tq - j * tk > tk)
        def _():
            step(False, tk)

        @pl.when(i == nq - 1)
        def _():
            dk_ref[...] = (dk_scr[...] * (1.0 / ATTN_SCALE2)).astype(BF16)
            dv_ref[...] = dv_scr[...].astype(BF16)

        @pl.when(s_id == len(pairs) - 1)
        def _():
            dq_ref[...] = dq_scr[...].astype(BF16)

    w = hp * LANES
    qspec = pl.BlockSpec((tq, w), lambda h, s, jj_r, ii_r: (ii_r[s], h))
    kspec = pl.BlockSpec((tk, w), lambda h, s, jj_r, ii_r: (jj_r[s], h))
    rspec = pl.BlockSpec((hp, 1, tq), lambda h, s, jj_r, ii_r: (h, 0, ii_r[s]))
    return pl.pallas_call(
        body, name=name,
        grid_spec=pltpu.PrefetchScalarGridSpec(
            num_scalar_prefetch=2, grid=(hd // w, len(pairs)),
            in_specs=[qspec, kspec, kspec, rspec, rspec, qspec],
            out_specs=[pl.BlockSpec((t, w), lambda h, s, jj_r, ii_r: (0, h)), kspec, kspec],
            scratch_shapes=[pltpu.VMEM((tk, w), F32), pltpu.VMEM((tk, w), F32), pltpu.VMEM((t, w), F32)]),
        out_shape=[jax.ShapeDtypeStruct((t, hd), BF16)] * 3,
        compiler_params=_cparams())(jnp.asarray(jj), jnp.asarray(ii), q, k, v, delta_rows, lse_rows, do)


MEM_W = MEM_HEADS * LANES


def _mem_kv_fwd(mem, gmem, wkv, kg, name):
    m, d = mem.shape

    def body(mem_ref, g_ref, w_ref, kg_ref, k_ref, v_ref, mn_ref):
        xv = mem_ref[...]
        mn = (xv * _rstd(xv, d) * g_ref[...]).astype(BF16)
        mn_ref[...] = mn
        kvm = _dot(mn, w_ref[...])
        v_ref[...] = kvm[:, MEM_W:].astype(BF16)
        for h in range(MEM_HEADS):
            sl = slice(h * LANES, (h + 1) * LANES)
            kh = kvm[:, sl]
            k_ref[:, sl] = (kh * _rstd(kh, LANES) * kg_ref[...]).astype(BF16)

    full = lambda a: pl.BlockSpec(a.shape, lambda i: (0, 0))
    return pl.pallas_call(
        body, name=name, grid=(1,), in_specs=[full(mem), full(gmem), full(wkv), full(kg)],
        out_specs=[pl.BlockSpec((m, MEM_W), lambda i: (0, 0)), pl.BlockSpec((m, MEM_W), lambda i: (0, 0)),
                   pl.BlockSpec((m, d), lambda i: (0, 0))],
        out_shape=[jax.ShapeDtypeStruct((m, MEM_W), BF16), jax.ShapeDtypeStruct((m, MEM_W), BF16),
                   jax.ShapeDtypeStruct((m, d), BF16)],
        compiler_params=_cparams())(mem, gmem, wkv, kg)


def _mem_softmax(qn, kh):
    s = _dot_nt(qn, kh) * (LANES ** -0.5)
    e = jnp.exp(s - jnp.max(s, axis=1, keepdims=True))
    return e / jnp.sum(e, axis=1, keepdims=True)


def _mem_attn_fwd(zqm, qg, km, vm, name):
    t = zqm.shape[0]
    tm = _row_tile(t, 512)

    def body(q_ref, qg_ref, k_ref, v_ref, o_ref):
        for h in range(MEM_HEADS):
            sl = slice(h * LANES, (h + 1) * LANES)
            qh = q_ref[:, sl]
            qn = (qh * _rstd(qh, LANES) * qg_ref[...]).astype(BF16)
            p = _mem_softmax(qn, k_ref[:, sl])
            o_ref[:, sl] = _dot(p.astype(BF16), v_ref[:, sl]).astype(BF16)

    row = pl.BlockSpec((tm, MEM_W), lambda i: (i, 0))
    full = lambda a: pl.BlockSpec(a.shape, lambda i: (0, 0))
    return pl.pallas_call(
        body, name=name, grid=(t // tm,), in_specs=[row, full(qg), full(km), full(vm)], out_specs=row,
        out_shape=jax.ShapeDtypeStruct((t, MEM_W), BF16), compiler_params=_cparams())(zqm, qg, km, vm)


def _mem_attn_bwd(zqm, dyc, qg, km, vm, name):
    t = zqm.shape[0]
    m = km.shape[0]
    tm = _row_tile(t, 256)

    def body(q_ref, dy_ref, qg_ref, k_ref, v_ref, dz_ref, dk_ref, dv_ref, dqg_ref):
        @pl.when(pl.program_id(0) == 0)
        def _():
            dk_ref[...] = jnp.zeros_like(dk_ref)
            dv_ref[...] = jnp.zeros_like(dv_ref)
            dqg_ref[...] = jnp.zeros_like(dqg_ref)

        qgv = qg_ref[...]
        dqg_acc = jnp.zeros((tm, LANES), F32)
        heads = [slice(h * LANES, (h + 1) * LANES) for h in range(MEM_HEADS)]
        dps = [_dot_nt(dy_ref[:, sl], v_ref[:, sl]) for sl in heads]
        for h, sl in enumerate(heads):
            qh = q_ref[:, sl]
            r = _rstd(qh, LANES)
            qn = (qh * r * qgv).astype(BF16)
            kh = k_ref[:, sl]
            p = _mem_softmax(qn, kh)
            dov = dy_ref[:, sl]
            dv_ref[:, sl] += _dot_tn(p.astype(BF16), dov)
            dp = dps[h]
            ds = (p * (dp - jnp.sum(dp * p, axis=1, keepdims=True)) * (LANES ** -0.5)).astype(BF16)
            dk_ref[:, sl] += _dot_tn(ds, qn)
            dqh, dgr = _rms_vjp(qh, r, qgv, _dot(ds, kh), LANES)
            dz_ref[:, sl] = dqh.astype(BF16)
            dqg_acc += dgr
        dqg_ref[...] += jnp.sum(dqg_acc, axis=0, keepdims=True)

    row = pl.BlockSpec((tm, MEM_W), lambda i: (i, 0))
    full = lambda a: pl.BlockSpec(a.shape, lambda i: (0, 0))
    acc = pl.BlockSpec((m, MEM_W), lambda i: (0, 0))
    return pl.pallas_call(
        body, name=name, grid=(t // tm,), in_specs=[row, row, full(qg), full(km), full(vm)],
        out_specs=[row, acc, acc, pl.BlockSpec((1, LANES), lambda i: (0, 0))],
        out_shape=[jax.ShapeDtypeStruct((t, MEM_W), BF16), jax.ShapeDtypeStruct((m, MEM_W), F32),
                   jax.ShapeDtypeStruct((m, MEM_W), F32), jax.ShapeDtypeStruct((1, LANES), F32)],
        compiler_params=_cparams())(zqm, dyc, qg, km, vm)


def _mem_kv_bwd(mem, gmem, wkv, kg, dkn, dvm, name):
    m, d = mem.shape

    def body(mem_ref, g_ref, w_ref, kg_ref, dk_ref, dv_ref, dw_ref, dkg_ref, dg_ref, dkv_scr):
        xv = mem_ref[...]
        r = _rstd(xv, d)
        mn = (xv * r * g_ref[...]).astype(BF16)
        kvm = _dot(mn, w_ref[...])
        dkv_scr[:, MEM_W:] = dv_ref[...].astype(BF16)
        dkg_acc = jnp.zeros((m, LANES), F32)
        for h in range(MEM_HEADS):
            sl = slice(h * LANES, (h + 1) * LANES)
            kh = kvm[:, sl]
            dkh, dgr = _rms_vjp(kh, _rstd(kh, LANES), kg_ref[...], dk_ref[:, sl], LANES)
            dkv_scr[:, sl] = dkh.astype(BF16)
            dkg_acc += dgr
        dkg_ref[...] = jnp.sum(dkg_acc, axis=0, keepdims=True)
        dkv = dkv_scr[...]
        dw_ref[...] = _dot_tn(mn, dkv)
        dmn = _dot_nt(dkv, w_ref[...])
        dg_ref[...] = jnp.sum(dmn * xv * r, axis=0, keepdims=True)

    full = lambda a: pl.BlockSpec(a.shape, lambda i: (0, 0))
    return pl.pallas_call(
        body, name=name, grid=(1,),
        in_specs=[full(mem), full(gmem), full(wkv), full(kg), full(dkn), full(dvm)],
        out_specs=[pl.BlockSpec((d, 2 * MEM_W), lambda i: (0, 0)), pl.BlockSpec((1, LANES), lambda i: (0, 0)),
                   pl.BlockSpec((1, d), lambda i: (0, 0))],
        out_shape=[jax.ShapeDtypeStruct((d, 2 * MEM_W), F32), jax.ShapeDtypeStruct((1, LANES), F32),
                   jax.ShapeDtypeStruct((1, d), F32)],
        scratch_shapes=[pltpu.VMEM((m, 2 * MEM_W), BF16)],
        compiler_params=_cparams())(mem, gmem, wkv, kg, dkn, dvm)


def _merge_fwd(x1, ya, yb, yc, zg, bg, wa, wb, wc, wo, name):
    t, d = x1.shape
    tm = _row_tile(t, 256)

    def body(x_ref, ya_ref, yb_ref, yc_ref, zg_ref, bg_ref, wa_ref, wb_ref, wc_ref, wo_ref,
             x2_ref, mg_ref, pa_ref, pb_ref, pc_ref):
        merged = None
        for k, (y_ref, w_ref, p_ref) in enumerate(
                ((ya_ref, wa_ref, pa_ref), (yb_ref, wb_ref, pb_ref), (yc_ref, wc_ref, pc_ref))):
            sl = slice(k * d, (k + 1) * d)
            pr = _dot(y_ref[...], w_ref[...])
            p_ref[...] = pr.astype(BF16)
            term = jax.nn.sigmoid(zg_ref[:, sl] + bg_ref[:, sl]) * pr
            merged = term if merged is None else merged + term
        mb = merged.astype(BF16)
        mg_ref[...] = mb
        x2_ref[...] = x_ref[...] + _dot(mb, wo_ref[...])

    row = lambda n: pl.BlockSpec((tm, n), lambda i: (i, 0))
    full = lambda a: pl.BlockSpec(a.shape, lambda i: (0, 0))
    return pl.pallas_call(
        body, name=name, grid=(t // tm,),
        in_specs=[row(d), row(ya.shape[1]), row(yb.shape[1]), row(yc.shape[1]), row(3 * d), full(bg),
                  full(wa), full(wb), full(wc), full(wo)],
        out_specs=[row(d)] * 5,
        out_shape=[jax.ShapeDtypeStruct((t, d), F32)] + [jax.ShapeDtypeStruct((t, d), BF16)] * 4,
        compiler_params=_cparams())(x1, ya, yb, yc, zg, bg, wa, wb, wc, wo)


def _merge_bwd(dx2, pa, pb, pc, zg, bg, wa, wb, wc, wo, name, ex=None):
    t, d = dx2.shape
    tm = _row_tile(t, 256)

    def body(dx_ref, pa_ref, pb_ref, pc_ref, zg_ref, bg_ref, wa_ref, wb_ref, wc_ref, wo_ref,
             dpa_ref, dpb_ref, dpc_ref, dzg_ref, dbg_ref, dya_ref, dyb_ref, dyc_ref):
        @pl.when(pl.program_id(0) == 0)
        def _():
            dbg_ref[...] = jnp.zeros_like(dbg_ref)

        dm = _dot_nt(dx_ref[...].astype(BF16), wo_ref[...])
        for k, (p_ref, w_ref, dp_ref, dy_ref) in enumerate(
                ((pa_ref, wa_ref, dpa_ref, dya_ref), (pb_ref, wb_ref, dpb_ref, dyb_ref),
                 (pc_ref, wc_ref, dpc_ref, dyc_ref))):
            sl = slice(k * d, (k + 1) * d)
            gate = jax.nn.sigmoid(zg_ref[:, sl] + bg_ref[:, sl])
            dpr = (dm * gate).astype(BF16)
            dp_ref[...] = dpr
            dzg = dm * p_ref[...].astype(F32) * gate * (1.0 - gate)
            dzg_ref[:, sl] = dzg.astype(BF16)
            dbg_ref[:, sl] += jnp.sum(dzg, axis=0, keepdims=True)
            dy_ref[...] = _dot_nt(dpr, w_ref[...]).astype(dy_ref.dtype)

    row = lambda n: pl.BlockSpec((tm, n), lambda i: (i, 0))
    full = lambda a: pl.BlockSpec(a.shape, lambda i: (0, 0))
    na, nb, nc = wa.shape[0], wb.shape[0], wc.shape[0]
    return _call_with_exchange(
        ex, body, name, (t // tm,),
        [row(d), row(d), row(d), row(d), row(3 * d), full(bg), full(wa), full(wb), full(wc), full(wo)],
        [row(d), row(d), row(d), row(3 * d), pl.BlockSpec((1, 3 * d), lambda i: (0, 0)), row(na), row(nb), row(nc)],
        [jax.ShapeDtypeStruct((t, d), BF16)] * 3
        + [jax.ShapeDtypeStruct((t, 3 * d), BF16), jax.ShapeDtypeStruct((1, 3 * d), F32),
           jax.ShapeDtypeStruct((t, na), F32), jax.ShapeDtypeStruct((t, nb), BF16),
           jax.ShapeDtypeStruct((t, nc), BF16)],
        [], (dx2, pa, pb, pc, zg, bg, wa, wb, wc, wo))


def _adamw_math(w, g, m, v):
    bc1 = 1.0 - ADAM_B1 ** ADAM_STEP
    bc2 = 1.0 - ADAM_B2 ** ADAM_STEP
    nm = ADAM_B1 * m + (1.0 - ADAM_B1) * g
    nv = ADAM_B2 * v + (1.0 - ADAM_B2) * (g * g)
    delta = -ADAM_LR * ((nm / bc1) / (jnp.sqrt(nv / bc2) + ADAM_EPS) + ADAM_WD * w)
    return delta, nm, nv


def _div_tile(n, cap, mult):
    best = None
    for cand in range(mult, min(n, cap) + 1, mult):
        if n % cand == 0:
            best = cand
    assert best is not None, (n, cap, mult)
    return best


def _adamw(w, g, m, v, name):
    rows, cols = w.shape
    tr = rows if rows * cols <= 256 * 1024 else _div_tile(rows, 256, 8)

    def body(w_ref, g_ref, m_ref, v_ref, d_ref, nm_ref, nv_ref):
        d_ref[...], nm_ref[...], nv_ref[...] = _adamw_math(w_ref[...], g_ref[...], m_ref[...], v_ref[...])

    blk = pl.BlockSpec((tr, cols), lambda i: (i, 0))
    return pl.pallas_call(
        body, name=name, grid=(rows // tr,), in_specs=[blk] * 4, out_specs=[blk] * 3,
        out_shape=[jax.ShapeDtypeStruct((rows, cols), F32)] * 3, compiler_params=_cparams())(w, g, m, v)


def _adamw_slots(w, slots, m, v, name):
    _, hr, cols = w.shape
    tr = _div_tile(hr, 128, 16)

    def body(w_ref, s_ref, m_ref, v_ref, g_ref, d_ref, nm_ref, nv_ref):
        g = s_ref[0, 0].astype(F32)
        for k in range(1, N_CHIPS):
            g = g + s_ref[0, k].astype(F32)
        g_ref[0] = g
        d_ref[0], nm_ref[0], nv_ref[0] = _adamw_math(w_ref[0], g, m_ref[0], v_ref[0])

    blk = pl.BlockSpec((1, tr, cols), lambda h, i: (h, i, 0))
    return pl.pallas_call(
        body, name=name, grid=(2, hr // tr),
        in_specs=[blk, pl.BlockSpec((1, N_CHIPS, tr, cols), lambda h, i: (h, 0, i, 0)), blk, blk],
        out_specs=[blk] * 4, out_shape=[jax.ShapeDtypeStruct((2, hr, cols), F32)] * 4,
        compiler_params=_cparams())(w, slots, m, v)


ANY = pl.BlockSpec(memory_space=pl.ANY)


def _place():
    x, y, c = lax.axis_index("x"), lax.axis_index("y"), lax.axis_index("c")
    other_chips = [(1 - x, y), (x, 1 - y), (1 - x, 1 - y)]
    return x, y, c, other_chips


def _remote(src, dst, send_sem, recv_sem, to):
    return pltpu.make_async_remote_copy(src_ref=src, dst_ref=dst, send_sem=send_sem, recv_sem=recv_sem,
                                        device_id=to, device_id_type=MESH)


PIECE_BYTES = 384 * 1024


def _row_pieces(half_rows, cols):
    for n in (4, 2):
        if half_rows % (16 * n) == 0 and half_rows * cols * 2 // n >= PIECE_BYTES:
            return [pl.ds(k * (half_rows // n), half_rows // n) for k in range(n)]
    return [pl.ds(0, half_rows)]


def _pieces(arrays, rows_axis):
    return [(w, rows) for w, a in enumerate(arrays) for rows in _row_pieces(a.shape[rows_axis], a.shape[-1])]


def _gather_exchange(shards):
    nw = len(shards)
    pieces = _pieces(shards, 1)
    npc = len(pieces)

    def build(s_refs, g_refs, sems):
        send_sems, recv_sems, local_sems = sems
        x, y, c, chips = _place()
        me = 2 * x + y
        sibling = (x, y, 1 - c)
        mine = [pltpu.make_async_copy(s_refs[w], g_refs[w].at[me], local_sems.at[w]) for w in range(nw)]
        first = [_remote(s_refs[w].at[c, rows], g_refs[w].at[me, c, rows], send_sems.at[k, p], recv_sems.at[k, p],
                         (cx, cy, c)) for k, (cx, cy) in enumerate(chips) for p, (w, rows) in enumerate(pieces)]

        def start():
            for cp in mine + first:
                cp.start()

        arrived = [g_refs[w].at[2 * cx + cy, c, rows] for cx, cy in chips for w, rows in pieces]
        passed = [_remote(slab, slab, send_sems.at[3 + q // npc, q % npc], recv_sems.at[3 + q // npc, q % npc], sibling)
                  for q, slab in enumerate(arrived)]

        def pass_on():
            for q, slab in enumerate(arrived):
                k, p = q // npc, q % npc
                _remote(slab, slab, send_sems.at[k, p], recv_sems.at[k, p], (*chips[k], c)).wait_recv()
                passed[q].start()

        def finish():
            for k, (cx, cy) in enumerate(chips):
                for p, (w, rows) in enumerate(pieces):
                    slab = g_refs[w].at[2 * cx + cy, 1 - c, rows]
                    _remote(slab, slab, send_sems.at[3 + k, p], recv_sems.at[3 + k, p], sibling).wait_recv()
            for cp in first + passed:
                cp.wait_send()
            for cp in mine:
                cp.wait()

        return start, pass_on, finish

    return _Exchange(list(shards), [jax.ShapeDtypeStruct((N_CHIPS,) + s.shape, BF16) for s in shards],
                     [pltpu.SemaphoreType.DMA((6, npc)), pltpu.SemaphoreType.DMA((6, npc)),
                      pltpu.SemaphoreType.DMA((nw,))], build)


def _swap_halves(grads, name):
    nw = len(grads)

    def body(*refs):
        g_refs, sib_refs = refs[:nw], refs[nw:2 * nw]
        send_sems, recv_sems = refs[2 * nw:]
        x, y, c, _ = _place()
        copies = [_remote(g_refs[w].at[s, 1 - c], sib_refs[w].at[s], send_sems.at[s, w], recv_sems.at[s, w],
                          (x, y, 1 - c)) for w in range(nw) for s in range(N_CHIPS)]
        for cp in copies:
            cp.start()
        for cp in copies:
            cp.wait_recv()
        for cp in copies:
            cp.wait_send()

    return pl.pallas_call(
        body, name=name, in_specs=[ANY] * nw, out_specs=[ANY] * nw,
        out_shape=[jax.ShapeDtypeStruct((N_CHIPS,) + g.shape[2:], BF16) for g in grads],
        scratch_shapes=[pltpu.SemaphoreType.DMA((N_CHIPS, nw)), pltpu.SemaphoreType.DMA((N_CHIPS, nw))],
    )(*grads)


def _pair_sum(grad, sib, core, name):
    nchip, _, hr, cols = grad.shape
    tr = _div_tile(hr, 256, 16)

    def body(core_ref, a_ref, b_ref, o_ref):
        o_ref[...] = (a_ref[0].astype(F32) + b_ref[...].astype(F32)).astype(BF16)

    return pl.pallas_call(
        body, name=name,
        grid_spec=pltpu.PrefetchScalarGridSpec(
            num_scalar_prefetch=1, grid=(nchip, hr // tr),
            in_specs=[pl.BlockSpec((1, 1, tr, cols), lambda s, i, core_r: (s, core_r[0], i, 0)),
                      pl.BlockSpec((1, tr, cols), lambda s, i, core_r: (s, i, 0))],
            out_specs=pl.BlockSpec((1, tr, cols), lambda s, i, core_r: (s, i, 0))),
        out_shape=jax.ShapeDtypeStruct((nchip, hr, cols), BF16), compiler_params=_cparams())(core, grad, sib)


def _pair_sum_exchange(sums):
    nw = len(sums)
    pieces = _pieces(sums, 1)
    npc = len(pieces)

    def build(p_refs, o_refs, sems):
        send_sems, recv_sems, local_sems = sems
        x, y, c, chips = _place()
        me = 2 * x + y
        sibling = (x, y, 1 - c)
        mine = [pltpu.make_async_copy(p_refs[w].at[me], o_refs[w].at[c, 3], local_sems.at[w]) for w in range(nw)]
        first = [_remote(p_refs[w].at[2 * cx + cy, rows], o_refs[w].at[c, k, rows], send_sems.at[k, p],
                         recv_sems.at[k, p], (cx, cy, c))
                 for k, (cx, cy) in enumerate(chips) for p, (w, rows) in enumerate(pieces)]

        def start():
            for cp in mine + first:
                cp.start()

        passed = [_remote(o_refs[w].at[c, k, rows], o_refs[w].at[c, k, rows], send_sems.at[3 + k, p],
                          recv_sems.at[3 + k, p], sibling) for k in range(N_CHIPS) for p, (w, rows) in enumerate(pieces)]

        def pass_on():
            for k in range(N_CHIPS):
                own_waited = set()
                for p, (w, rows) in enumerate(pieces):
                    if k < 3:
                        first[k * npc + p].wait_recv()
                    elif w not in own_waited:
                        mine[w].wait()
                        own_waited.add(w)
                    passed[k * npc + p].start()

        def finish():
            for k in range(N_CHIPS):
                for p, (w, rows) in enumerate(pieces):
                    slab = o_refs[w].at[1 - c, k, rows]
                    _remote(slab, slab, send_sems.at[3 + k, p], recv_sems.at[3 + k, p], sibling).wait_recv()
            for cp in first + passed:
                cp.wait_send()

        return start, pass_on, finish

    return _Exchange(list(sums), [jax.ShapeDtypeStruct((2,) + p.shape, BF16) for p in sums],
                     [pltpu.SemaphoreType.DMA((7, npc)), pltpu.SemaphoreType.DMA((7, npc)),
                      pltpu.SemaphoreType.DMA((nw,))], build)


def _small_sum_exchange(vec):
    m_per, n = vec.shape

    def build(ins, outs, scr):
        (x_ref,), (out_ref,) = ins, outs
        gath_ref, sum_ref, send_sems, recv_sems, local_sem, out_sem = scr
        x, y, c, chips = _place()
        me, sibling = (x, y, c), (x, y, 1 - c)

        def rows(px, py, pc):
            return gath_ref.at[pl.ds((4 * px + 2 * py + pc) * m_per, m_per), :]

        def copy(k, block, to, src=None):
            return pltpu.make_async_remote_copy(
                src_ref=rows(*block) if src is None else src, dst_ref=rows(*block),
                send_sem=send_sems.at[k], recv_sem=recv_sems.at[k], device_id=to, device_id_type=MESH)

        mine = pltpu.make_async_copy(x_ref, rows(*me), local_sem)
        first = [copy(0, me, sibling, src=x_ref)] + [copy(1 + j, me, (*chip, c), src=x_ref)
                                                     for j, chip in enumerate(chips)]

        def start():
            for cp in [mine] + first:
                cp.start()

        passed = [copy(4 + j, (*chip, c), sibling) for j, chip in enumerate(chips)]

        def pass_on():
            for j, chip in enumerate(chips):
                copy(1 + j, (*chip, c), me).wait_recv()
                passed[j].start()

        def finish():
            copy(0, sibling, me).wait_recv()
            for j, chip in enumerate(chips):
                copy(4 + j, (*chip, 1 - c), me).wait_recv()
            for cp in first + passed:
                cp.wait_send()
            mine.wait()
            acc = gath_ref[pl.ds(0, m_per), :]
            for k in range(1, N_DEV):
                acc = acc + gath_ref[pl.ds(k * m_per, m_per), :]
            sum_ref[...] = acc
            done = pltpu.make_async_copy(sum_ref, out_ref, out_sem)
            done.start()
            done.wait()

        return start, pass_on, finish

    return _Exchange([vec], [jax.ShapeDtypeStruct((m_per, n), F32)],
                     [pltpu.VMEM((N_DEV * m_per, n), F32), pltpu.VMEM((m_per, n), F32), pltpu.SemaphoreType.DMA((7,)),
                      pltpu.SemaphoreType.DMA((7,)), pltpu.SemaphoreType.DMA, pltpu.SemaphoreType.DMA], build)


def _pack_small(vals, tail=()):
    flat = jnp.concatenate([vals[name].reshape(-1).astype(F32) for name, _ in SMALL] + [v.reshape(1) for v in tail])
    flat = jnp.pad(flat, (0, SMALL_ROWS * LANES - flat.shape[0]))
    return flat.reshape(SMALL_ROWS, LANES)


def _unpack_small(packed):
    flat = packed.reshape(-1)
    out, off = {}, 0
    for name, shape in SMALL:
        n = int(np.prod(shape))
        out[name] = flat[off:off + n].reshape(shape)
        off += n
    return out


def _head_pad_cols(w, heads, real):
    k = w.shape[0]
    return jnp.pad(w.reshape(k, heads, real), ((0, 0), (0, 0), (0, LANES - real))).reshape(k, heads * LANES)


def _rope_tables(positions):
    half = MLA_ROPE // 2
    inv = ROPE_BASE ** (-jnp.arange(half, dtype=F32) / half)
    ang = positions.astype(F32)[:, None] * inv
    cos, sin = jnp.cos(ang), jnp.sin(ang)
    t = positions.shape[0]
    z = lambda n: jnp.zeros((t, n), F32)
    rc = jnp.concatenate([jnp.ones((t, MLA_NOPE), F32), cos, cos, z(LANES - MLA_QK)], axis=1)
    rs1 = jnp.concatenate([z(MLA_NOPE), -sin, z(LANES - MLA_NOPE - half)], axis=1)
    rs2 = jnp.concatenate([z(MLA_NOPE + half), sin, z(LANES - MLA_QK)], axis=1)
    return rc, rs1, rs2


FFN1_WEIGHTS = ("ffn1_w_gu", "ffn1_w_down")
FFN2_WEIGHTS = ("ffn2_w_gu", "ffn2_w_down")
MIXER_WEIGHTS = tuple(n for n, *_ in SHARDED if n not in FFN1_WEIGHTS + FFN2_WEIGHTS)
SHARD_SHAPE = {n: (r, c, kind) for n, r, c, kind in SHARDED}


def _from_blocks(name, gathered):
    r, c, kind = SHARD_SHAPE[name]
    blk = gathered.reshape(N_CHIPS, r, c)
    return blk, (blk.transpose(1, 0, 2).reshape(r, N_CHIPS * c) if kind == "col" else blk.reshape(N_CHIPS * r, c))


def _grad_pair_sums(names, gw, core, tag):
    by_owner = []
    for name in names:
        r, c, kind = SHARD_SHAPE[name]
        if gw[name].dtype == BF16:
            blk = gw[name]
        elif kind == "col":
            blk = gw[name].reshape(r, N_CHIPS, c).transpose(1, 0, 2)
        else:
            blk = gw[name].reshape(N_CHIPS, r, c)
        by_owner.append(blk.astype(BF16).reshape(N_CHIPS, 2, r // 2, c))
    received = _swap_halves(by_owner, "grad_swap_" + tag)
    return [_pair_sum(g, s, core, "pair_sum_" + n) for g, s, n in zip(by_owner, received, names)]


def _device_step(x, mem, positions, tgt, small, shards, core):
    d = D_MODEL
    g_ffn1, g_mix, g_ffn2 = small["ffn1_norm"], small["mix_norm"], small["ffn2_norm"]
    big = {}
    for name, g in zip(FFN1_WEIGHTS, _run_exchange(_gather_exchange([shards[n] for n in FFN1_WEIGHTS]), "gather_ffn1")):
        big[name + "#blocks"], big[name] = _from_blocks(name, g)
    wgu1, wd1 = big["ffn1_w_gu#blocks"], big["ffn1_w_down"].reshape(2, FF_TILE, d)
    x1, gpre1, upre1, h, *rest = _ffn_fwd(x, g_ffn1, wgu1, wd1, "ffn1_fwd", next_gain=g_mix,
                                          ex=_gather_exchange([shards[n] for n in MIXER_WEIGHTS]))
    for name, g in zip(MIXER_WEIGHTS, rest):
        big[name + "#blocks"], big[name] = _from_blocks(name, g)
    w_in = big["w_in"]
    w_uv_, w_cq, w_ckv = w_in[:, :COL_CQ], w_in[:, COL_CQ:COL_CKV], w_in[:, COL_CKV:COL_KR]
    w_kr = jnp.pad(w_in[:, COL_KR:COL_QM], ((0, 0), (MLA_NOPE, LANES - MLA_QK)))
    w_qm, w_g = w_in[:, COL_QM:COL_GATE], w_in[:, COL_GATE:]
    segs = (w_uv_, w_cq, w_ckv, w_kr, w_qm, w_g)
    wuq = _head_pad_cols(big["mla_w_uq"], MLA_HEADS, MLA_QK)
    ukv = big["mla_w_ukv"].reshape(MLA_KV_RANK, MLA_HEADS, 2, MLA_NOPE)
    wuk = _head_pad_cols(ukv[:, :, 0].reshape(MLA_KV_RANK, -1), MLA_HEADS, MLA_NOPE)
    wuv = _head_pad_cols(ukv[:, :, 1].reshape(MLA_KV_RANK, -1), MLA_HEADS, MLA_NOPE)
    wkv = big["mem_w_kv"]
    wa, wc, wo = big["w_branch_a"], big["w_branch_c"], big["w_out"]
    wb = jnp.pad(big["w_branch_b"].reshape(MLA_HEADS, MLA_NOPE, d),
                 ((0, 0), (0, LANES - MLA_NOPE), (0, 0))).reshape(MLA_HEADS * LANES, d)
    qg = jnp.pad(small["mla_q_norm"], ((0, 0), (0, LANES - MLA_QK)))
    kg = jnp.pad(small["mla_k_norm"], ((0, 0), (0, LANES - MLA_QK)))
    causal = jnp.tril(jnp.ones((CHUNK, CHUNK), bool))
    wt_f = jnp.where(causal[None], small["sg_w"][0], 0.0)
    wt, wt_t = wt_f.astype(BF16), wt_f.transpose(0, 2, 1).astype(BF16)
    bias_l = jnp.repeat(small["sg_b"][0].T, 64, axis=1)
    rc, rs1, rs2 = _rope_tables(positions)

    zuv, zcq, zckv, zkr, zqm, zg = _mm_cols(h, segs, [F32] * 5 + [BF16], "in_proj")
    ya = _sgu_fwd(zuv, small["sg_ln_g"], small["sg_ln_b"], wt, bias_l, "sgu_fwd")
    q, k, v, cqn, ckvn = _mla_prep_fwd(zcq, zckv, zkr, small["mla_cq_norm"], small["mla_ckv_norm"], qg, kg,
                                       wuq, wuk, wuv, rc, rs1, rs2, "mla_prep_fwd")
    yb, lse, *rest = _attn_fwd(q, k, v, "mla_attn_fwd", ex=_gather_exchange([shards[n] for n in FFN2_WEIGHTS]))
    for name, g in zip(FFN2_WEIGHTS, rest):
        big[name + "#blocks"], big[name] = _from_blocks(name, g)
    wgu2, wd2 = big["ffn2_w_gu#blocks"], big["ffn2_w_down"].reshape(2, FF_TILE, d)
    km, vm, memn = _mem_kv_fwd(mem, small["mem_norm"], wkv, small["mem_k_norm"], "mem_kv_fwd")
    yc = _mem_attn_fwd(zqm, small["mem_q_norm"], km, vm, "mem_attn_fwd")
    x2, merged, pa, pb, pc = _merge_fwd(x1, ya, yb, yc, zg, small["b_gate"], wa, wb, wc, wo, "merge_fwd")
    dy, loss_row, gpre2, upre2 = _ffn_fwd(x2, g_ffn2, wgu2, wd2, "ffn2_fwd", target=tgt)

    gw, gs, slots = {}, {}, {}

    def ffn_grads(prefix, xin, gain, dyin, gpre, upre, wgu, wd, ex=None, ex_names=(), last=False):
        dx, dgain, xn, dgt, dup, act, *got = _ffn_bwd(xin, gain, dyin, gpre, upre, wgu, wd, prefix + "_bwd", ex=ex)
        slots.update(zip(ex_names, got))
        gs[prefix + "_norm"] = dgain
        gw[prefix + "_w_gu"] = jnp.concatenate(
            [_mm_tn(xn, dgt, prefix + "_dwg", col_blocks=True, out_dtype=BF16),
             _mm_tn(xn, dup, prefix + "_dwu", col_blocks=True, out_dtype=BF16)], axis=0)
        rows_down = SHARD_SHAPE[prefix + "_w_down"][0]
        if last:
            small_sum = _small_sum_exchange(_pack_small(gs, tail=[loss_row[0, 0]]))
            dwd, summed = _mm_tn(act, dyin, prefix + "_dwd", scale=0.5, ex=small_sum, out_dtype=BF16)
            gw[prefix + "_w_down"] = dwd.reshape(N_CHIPS, rows_down, d)
            return dx, summed
        gw[prefix + "_w_down"] = _mm_tn(act, dyin, prefix + "_dwd", scale=0.5, out_dtype=BF16).reshape(
            N_CHIPS, rows_down, d)
        return dx

    dx2 = ffn_grads("ffn2", x2, g_ffn2, dy, gpre2, upre2, wgu2, wd2)
    ffn2_sums = _pair_sum_exchange(_grad_pair_sums(FFN2_WEIGHTS, gw, core, "ffn2"))
    dpa, dpb, dpc, dzg, dbg, dya, dyb, dyc, *got = _merge_bwd(dx2, pa, pb, pc, zg, small["b_gate"], wa, wb, wc, wo,
                                                              "merge_bwd", ex=ffn2_sums)
    slots.update(zip(FFN2_WEIGHTS, got))
    gs["b_gate"] = dbg
    gw["w_out"] = _mm_tn(merged, dx2, "dw_out")
    gw["w_branch_a"] = _mm_tn(ya, dpa, "dw_branch_a")
    gw["w_branch_b"] = _mm_tn(yb, dpb, "dw_branch_b").reshape(MLA_HEADS, LANES, d)[:, :MLA_NOPE].reshape(-1, d)
    gw["w_branch_c"] = _mm_tn(yc, dpc, "dw_branch_c")

    dzuv, dwt, dbl, dlg, dlb = _sgu_bwd(zuv, dya, small["sg_ln_g"], small["sg_ln_b"], wt, wt_t, bias_l, "sgu_bwd")
    gs["sg_w"], gs["sg_b"] = dwt[None], dbl[:, :SG_GROUPS].T[None]
    gs["sg_ln_g"], gs["sg_ln_b"] = dlg, dlb

    delta_rows, lse_rows = _attn_bwd_rows(yb, lse, dyb, "mla_attn_bwd_rows")
    dq, dk, dv = _attn_bwd(q, k, v, delta_rows, lse_rows, dyb, "mla_attn_bwd")
    dzcq, dzckv, dzkr, dql, dkl, dgcq, dgckv, dqg, dkg = _mla_prep_bwd(
        zcq, zckv, zkr, small["mla_cq_norm"], small["mla_ckv_norm"], qg, kg, wuq, wuk, wuv, rc, rs1, rs2,
        dq, dk, dv, "mla_prep_bwd")
    gs["mla_cq_norm"], gs["mla_ckv_norm"] = dgcq, dgckv
    gs["mla_q_norm"], gs["mla_k_norm"] = dqg[:, :MLA_QK], dkg[:, :MLA_QK]
    gw["mla_w_uq"] = _mm_tn(cqn, dql, "dw_uq").reshape(MLA_Q_RANK, MLA_HEADS, LANES)[:, :, :MLA_QK].reshape(
        MLA_Q_RANK, -1)
    dwuk = _mm_tn(ckvn, dkl, "dw_uk").reshape(MLA_KV_RANK, MLA_HEADS, LANES)[:, :, :MLA_NOPE]
    dwuv = _mm_tn(ckvn, dv, "dw_uv").reshape(MLA_KV_RANK, MLA_HEADS, LANES)[:, :, :MLA_NOPE]
    gw["mla_w_ukv"] = jnp.concatenate([dwuk, dwuv], axis=2).reshape(MLA_KV_RANK, -1)

    dzqm, dkn, dvm, dmqg = _mem_attn_bwd(zqm, dyc, small["mem_q_norm"], km, vm, "mem_attn_bwd")
    gs["mem_q_norm"] = dmqg
    gw["mem_w_kv"], gs["mem_k_norm"], gs["mem_norm"] = _mem_kv_bwd(
        mem, small["mem_norm"], wkv, small["mem_k_norm"], dkn, dvm, "mem_kv_bwd")

    dzs = (dzuv, dzcq, dzckv, dzkr, dzqm, dzg)
    dws = list(_mm_tn_cols(h, dzs[:5], "dw_in_narrow")) + [_mm_tn(h, dzg, "dw_in_gate")]
    dws[3] = dws[3][:, MLA_NOPE:MLA_QK]
    gw["w_in"] = jnp.concatenate(dws, axis=1)
    dx1, gs["mix_norm"] = _proj_norm_bwd(dzs, [w.T for w in segs], x1, g_mix, dx2, "in_proj_bwd")
    mixer_sums = _pair_sum_exchange(_grad_pair_sums(MIXER_WEIGHTS, gw, core, "mixer"))
    dx, summed = ffn_grads("ffn1", x, g_ffn1, dx1, gpre1, upre1, wgu1, wd1, ex=mixer_sums, ex_names=MIXER_WEIGHTS,
                           last=True)
    ffn1_sums = _pair_sum_exchange(_grad_pair_sums(FFN1_WEIGHTS, gw, core, "ffn1"))
    slots.update(zip(FFN1_WEIGHTS, _run_exchange(ffn1_sums, "grad_exchange_ffn1")))
    return dx, slots, summed


def kernel(x, mem, positions, ffn1_norm, ffn1_w_gu, ffn1_w_down, mix_norm, w_in, b_gate, sg_ln_g, sg_ln_b, sg_w, sg_b, mla_cq_norm, mla_w_uq, mla_ckv_norm, mla_w_ukv, mla_q_norm, mla_k_norm, mem_norm, mem_w_kv, mem_q_norm, mem_k_norm, w_branch_a, w_branch_b, w_branch_c, w_out, ffn2_norm, ffn2_w_gu, ffn2_w_down, loss_target, m_ffn1_norm, m_ffn1_w_gu, m_ffn1_w_down, m_mix_norm, m_w_in, m_b_gate, m_sg_ln_g, m_sg_ln_b, m_sg_w, m_sg_b, m_mla_cq_norm, m_mla_w_uq, m_mla_ckv_norm, m_mla_w_ukv, m_mla_q_norm, m_mla_k_norm, m_mem_norm, m_mem_w_kv, m_mem_q_norm, m_mem_k_norm, m_w_branch_a, m_w_branch_b, m_w_branch_c, m_w_out, m_ffn2_norm, m_ffn2_w_gu, m_ffn2_w_down, v_ffn1_norm, v_ffn1_w_gu, v_ffn1_w_down, v_mix_norm, v_w_in, v_b_gate, v_sg_ln_g, v_sg_ln_b, v_sg_w, v_sg_b, v_mla_cq_norm, v_mla_w_uq, v_mla_ckv_norm, v_mla_w_ukv, v_mla_q_norm, v_mla_k_norm, v_mem_norm, v_mem_w_kv, v_mem_q_norm, v_mem_k_norm, v_w_branch_a, v_w_branch_b, v_w_branch_c, v_w_out, v_ffn2_norm, v_ffn2_w_gu, v_ffn2_w_down):
    args = dict(locals())
    weights = {n: args[n] for n in WEIGHT_ORDER}
    mom_m = {n: args["m_" + n] for n in WEIGHT_ORDER}
    mom_v = {n: args["v_" + n] for n in WEIGHT_ORDER}
    small = {n: weights[n] for n, _ in SMALL}
    halves = lambda a, r, c: a.reshape(2, r // 2, c)

    shards = {n: halves(weights[n][0].astype(BF16), r, c) for n, r, c, _ in SHARDED}
    core = lax.axis_index("c").astype(jnp.int32).reshape(1)
    dx, slots, summed = _device_step(x[0], mem[0], positions[0], loss_target[0], small, shards, core)
    loss = summed.reshape(-1)[_N_SMALL]
    small_grads = _unpack_small(summed)

    grads, deltas, new_m, new_v = {}, {}, {}, {}
    for name, r, c, _ in SHARDED:
        outs = _adamw_slots(halves(weights[name][0], r, c), slots[name], halves(mom_m[name][0], r, c),
                            halves(mom_v[name][0], r, c), "adamw_" + name)
        shape = weights[name].shape
        grads[name], deltas[name], new_m[name], new_v[name] = [o.reshape(shape) for o in outs]
    dlt, nm, nv = _adamw(_pack_small(small), _pack_small(small_grads), _pack_small({n: mom_m[n] for n, _ in SMALL}),
                         _pack_small({n: mom_v[n] for n, _ in SMALL}), "adamw_small")
    for name, _ in SMALL:
        grads[name] = small_grads[name]
    deltas.update(_unpack_small(dlt))
    new_m.update(_unpack_small(nm))
    new_v.update(_unpack_small(nv))

    return (loss, dx[None], *[grads[n] for n in WEIGHT_ORDER], *[deltas[n] for n in WEIGHT_ORDER],
            *[new_m[n] for n in WEIGHT_ORDER], *[new_v[n] for n in WEIGHT_ORDER])
```

```python
import functools
from typing import Callable, NamedTuple

import numpy as np
import jax
import jax.numpy as jnp
from jax import lax
from jax.experimental import pallas as pl
from jax.experimental.pallas import tpu as pltpu

F32 = jnp.float32
BF16 = jnp.bfloat16

D_MODEL = 1024
D_FF = 2816
FF_TILE = 1408
SG_WIDTH = 512
SG_GROUPS = 8
CHUNK = 128
MLA_HEADS = 8
MLA_QK = 96
MLA_NOPE = 64
MLA_ROPE = 32
MLA_Q_RANK = 384
MLA_KV_RANK = 256
MEM_HEADS = 4
LANES = 128
EPS = 1e-6
NEG = -1e30
ROPE_BASE = 10000.0
N_CHIPS = 4
N_DEV = 8

ADAM_LR = 0.001
ADAM_B1 = 0.9
ADAM_B2 = 0.999
ADAM_EPS = 1e-08
ADAM_WD = 0.01
ADAM_STEP = 10

COL_CQ = 1024
COL_CKV = 1408
COL_KR = 1664
COL_QM = 1696
COL_GATE = 2208

VMEM_LIMIT_BYTES = 56 * 1024 * 1024
INV_SQRT2 = 0.7071067811865476
INV_SQRT_2PI = 0.3989422804014327
LOG2E = 1.4426950408889634
ATTN_SCALE = MLA_QK ** -0.5
V_ONES_LANE = 64
ATTN_SCALE2 = ATTN_SCALE * LOG2E

SHARDED = (
    ("ffn1_w_gu", 1024, 1408, "col"),
    ("ffn1_w_down", 704, 1024, "row"),
    ("w_in", 1024, 1320, "col"),
    ("mla_w_uq", 384, 192, "col"),
    ("mla_w_ukv", 256, 256, "col"),
    ("mem_w_kv", 256, 1024, "row"),
    ("w_branch_a", 512, 256, "col"),
    ("w_branch_b", 512, 256, "col"),
    ("w_branch_c", 512, 256, "col"),
    ("w_out", 256, 1024, "row"),
    ("ffn2_w_gu", 1024, 1408, "col"),
    ("ffn2_w_down", 704, 1024, "row"),
)
SMALL = (
    ("ffn1_norm", (1, 1024)), ("mix_norm", (1, 1024)), ("b_gate", (1, 3072)),
    ("sg_ln_g", (1, 512)), ("sg_ln_b", (1, 512)), ("sg_w", (1, 8, 128, 128)),
    ("sg_b", (1, 8, 128)), ("mla_cq_norm", (1, 384)), ("mla_ckv_norm", (1, 256)),
    ("mla_q_norm", (1, 96)), ("mla_k_norm", (1, 96)), ("mem_norm", (1, 1024)),
    ("mem_q_norm", (1, 128)), ("mem_k_norm", (1, 128)), ("ffn2_norm", (1, 1024)),
)
WEIGHT_ORDER = (
    "ffn1_norm", "ffn1_w_gu", "ffn1_w_down", "mix_norm", "w_in", "b_gate", "sg_ln_g", "sg_ln_b",
    "sg_w", "sg_b", "mla_cq_norm", "mla_w_uq", "mla_ckv_norm", "mla_w_ukv", "mla_q_norm",
    "mla_k_norm", "mem_norm", "mem_w_kv", "mem_q_norm", "mem_k_norm", "w_branch_a", "w_branch_b",
    "w_branch_c", "w_out", "ffn2_norm", "ffn2_w_gu", "ffn2_w_down",
)

_N_SMALL = sum(int(np.prod(s)) for _, s in SMALL)
SMALL_ROWS = -(-_N_SMALL // (LANES * 8)) * 8

MESH = pl.DeviceIdType.MESH


def _cparams():
    return pltpu.CompilerParams(vmem_limit_bytes=VMEM_LIMIT_BYTES)


def _dot(a, b):
    return jnp.dot(a, b, preferred_element_type=F32)


def _dot_nt(a, b):
    return lax.dot_general(a, b, (((1,), (1,)), ((), ())), preferred_element_type=F32)


def _dot_tn(a, b):
    return lax.dot_general(a, b, (((0,), (0,)), ((), ())), preferred_element_type=F32)


def _gelu(x):
    return 0.5 * x * (1.0 + lax.erf(x * INV_SQRT2))


def _gelu_grad(x):
    return 0.5 * (1.0 + lax.erf(x * INV_SQRT2)) + x * jnp.exp(-0.5 * x * x) * INV_SQRT_2PI


def _rstd(x, n):
    return lax.rsqrt(jnp.sum(x * x, axis=-1, keepdims=True) * (1.0 / n) + EPS)


def _rms_vjp(x, r, g, dy, n):
    dxh = dy * g
    dx = r * dxh - x * (r * r * r) * (jnp.sum(dxh * x, axis=-1, keepdims=True) * (1.0 / n))
    return dx, dy * x * r


def _row_tile(t, want):
    return min(t, want)


def _wide_tile(n):
    if n <= 1024:
        return n
    if n % 1024 == 0:
        return 1024
    assert n % FF_TILE == 0, n
    return FF_TILE


def _mm_cols(a, ws, out_dtypes, name, ex=None):
    t, kdim = a.shape
    tm = _row_tile(t, 512)
    n = len(ws)

    def body(*refs):
        av = refs[0][...]
        for w_ref, o_ref in zip(refs[1:1 + n], refs[1 + n:]):
            o_ref[...] = _dot(av, w_ref[...]).astype(o_ref.dtype)

    row = lambda width: pl.BlockSpec((tm, width), lambda i: (i, 0))
    return _call_with_exchange(
        ex, body, name, (t // tm,),
        [row(kdim)] + [pl.BlockSpec(w.shape, lambda i: (0, 0)) for w in ws],
        [row(w.shape[1]) for w in ws],
        [jax.ShapeDtypeStruct((t, w.shape[1]), dt) for w, dt in zip(ws, out_dtypes)], [], (a, *ws))


def _proj_norm_bwd(dzs, wts, x, g, dres, name):
    t, d = x.shape
    tm = _row_tile(t, 256)
    n = len(dzs)

    def body(*refs):
        x_ref, g_ref, r_ref, dx_ref, dg_ref = refs[2 * n:]

        @pl.when(pl.program_id(0) == 0)
        def _():
            dg_ref[...] = jnp.zeros_like(dg_ref)

        dh = None
        for dz_ref, w_ref in zip(refs[:n], refs[n:2 * n]):
            part = _dot(dz_ref[...], w_ref[...])
            dh = part if dh is None else dh + part
        xv = x_ref[...]
        dx, dgr = _rms_vjp(xv, _rstd(xv, d), g_ref[...], dh, d)
        dx_ref[...] = r_ref[...] + dx
        dg_ref[...] += jnp.sum(dgr, axis=0, keepdims=True)

    row = lambda width: pl.BlockSpec((tm, width), lambda i: (i, 0))
    vec = pl.BlockSpec((1, d), lambda i: (0, 0))
    return pl.pallas_call(
        body, name=name, grid=(t // tm,),
        in_specs=[row(dz.shape[1]) for dz in dzs] + [pl.BlockSpec(w.shape, lambda i: (0, 0)) for w in wts]
        + [row(d), vec, row(d)],
        out_specs=[row(d), vec],
        out_shape=[jax.ShapeDtypeStruct((t, d), F32), jax.ShapeDtypeStruct((1, d), F32)],
        compiler_params=_cparams())(*dzs, *wts, x, g, dres)


def _mm_tn_cols(a, bs, name):
    t, m = a.shape
    tk = _row_tile(t, 1024)
    n = len(bs)

    def body(*refs):
        @pl.when(pl.program_id(0) == 0)
        def _():
            for o_ref in refs[1 + n:]:
                o_ref[...] = jnp.zeros_like(o_ref)

        av = refs[0][...].astype(BF16)
        for b_ref, o_ref in zip(refs[1:1 + n], refs[1 + n:]):
            o_ref[...] += _dot_tn(av, b_ref[...].astype(BF16))

    row = lambda width: pl.BlockSpec((tk, width), lambda k: (k, 0))
    return pl.pallas_call(
        body, name=name, grid=(t // tk,), in_specs=[row(m)] + [row(b.shape[1]) for b in bs],
        out_specs=[pl.BlockSpec((m, b.shape[1]), lambda k: (0, 0)) for b in bs],
        out_shape=[jax.ShapeDtypeStruct((m, b.shape[1]), F32) for b in bs],
        compiler_params=_cparams())(a, *bs)


def _mm_tn(a, b, name, scale=1.0, ex=None, col_blocks=False, out_dtype=F32):
    t, m = a.shape
    n = b.shape[1]
    tm, tn = _wide_tile(m), _wide_tile(n)
    tk = _row_tile(t, 2048)
    nk = t // tk
    in_place = out_dtype == F32

    def body(a_ref, b_ref, o_ref, *scr):
        k = pl.program_id(2)
        acc_ref = o_ref if in_place else scr[0]

        @pl.when(k == 0)
        def _():
            acc_ref[...] = jnp.zeros_like(acc_ref)

        prod = _dot_tn(a_ref[...].astype(BF16), b_ref[...].astype(BF16))
        acc_ref[...] += prod.reshape(acc_ref.shape)
        if scale != 1.0 or not in_place:
            @pl.when(k == nk - 1)
            def _():
                o_ref[...] = (acc_ref[...] * scale).astype(out_dtype).reshape(o_ref.shape)

    if col_blocks:
        out_spec = pl.BlockSpec((1, tm, tn), lambda i, j, k: (j, i, 0))
        out_shape = jax.ShapeDtypeStruct((n // tn, m, tn), out_dtype)
    else:
        out_spec = pl.BlockSpec((tm, tn), lambda i, j, k: (i, j))
        out_shape = jax.ShapeDtypeStruct((m, n), out_dtype)
    outs = _call_with_exchange(
        ex, body, name, (m // tm, n // tn, nk),
        [pl.BlockSpec((tk, tm), lambda i, j, k: (k, i)), pl.BlockSpec((tk, tn), lambda i, j, k: (k, j))],
        [out_spec], [out_shape], [] if in_place else [pltpu.VMEM((tm, tn), F32)], (a, b))
    return outs[0] if ex is None else outs


PASS_ON_STEPS_BEFORE_END = 8


class _Exchange(NamedTuple):
    operands: list
    out_shapes: list
    sem_shapes: list
    build: Callable


def _call_with_exchange(ex, body, name, grid, in_specs, out_specs, out_shape, scratch_shapes, operands, prefetch=()):
    n_pre = len(prefetch)
    total = int(np.prod(grid))
    pass_step = max(total // 2, total - PASS_ON_STEPS_BEFORE_END)

    def call(kernel, ins, outs, shapes, scratch):
        if n_pre:
            spec = pltpu.PrefetchScalarGridSpec(num_scalar_prefetch=n_pre, grid=grid, in_specs=ins, out_specs=outs,
                                                scratch_shapes=scratch)
            return pl.pallas_call(kernel, name=name, grid_spec=spec, out_shape=shapes, compiler_params=_cparams())
        return pl.pallas_call(kernel, name=name, grid=grid, in_specs=ins, out_specs=outs, out_shape=shapes,
                              scratch_shapes=scratch, compiler_params=_cparams())

    if ex is None:
        return call(body, in_specs, out_specs, out_shape, scratch_shapes)(*prefetch, *operands)
    n_in, n_out, n_scr = len(in_specs), len(out_specs), len(scratch_shapes)
    k_in, k_out = len(ex.operands), len(ex.out_shapes)

    def carried(*refs):
        pre, refs = refs[:n_pre], refs[n_pre:]
        a, b = n_in, n_in + k_in
        c, e = b + n_out, b + n_out + k_out
        f = e + n_scr
        start, pass_on, finish = ex.build(refs[a:b], refs[c:e], refs[f:])
        step = functools.reduce(lambda lin, ax: lin * grid[ax] + pl.program_id(ax), range(len(grid)), 0)
        pl.when(step == 0)(start)
        body(*pre, *refs[:a], *refs[b:c], *refs[e:f])
        pl.when(step == pass_step)(pass_on)
        pl.when(step == total - 1)(finish)

    return call(carried, list(in_specs) + [ANY] * k_in, list(out_specs) + [ANY] * k_out,
                list(out_shape) + list(ex.out_shapes), list(scratch_shapes) + list(ex.sem_shapes),
                )(*prefetch, *operands, *ex.operands)


def _run_exchange(ex, name):
    k_in, k_out = len(ex.operands), len(ex.out_shapes)

    def body(*refs):
        start, pass_on, finish = ex.build(refs[:k_in], refs[k_in:k_in + k_out], refs[k_in + k_out:])
        start()
        pass_on()
        finish()

    return pl.pallas_call(body, name=name, in_specs=[ANY] * k_in, out_specs=[ANY] * k_out,
                          out_shape=list(ex.out_shapes), scratch_shapes=list(ex.sem_shapes))(*ex.operands)


def _ffn_fwd(x, g, wgu4, wd2, name, ex=None, next_gain=None, target=None):
    t, d = x.shape
    tm = _row_tile(t, 512)
    assert next_gain is None or target is None
    extra = [a for a in (next_gain, target) if a is not None]

    def body(*refs):
        x_ref, g_ref, wg_ref, wu_ref, wd_ref = refs[:5]
        e_ref = refs[5] if extra else None
        outs, (xn_scr, acc_scr) = refs[5 + len(extra):-2], refs[-2:]
        if target is not None:
            dy_ref, loss_ref, gg_ref, uu_ref = outs
        elif next_gain is not None:
            o_ref, gg_ref, uu_ref, h_ref = outs
        else:
            o_ref, gg_ref, uu_ref = outs
        i, j = pl.program_id(0), pl.program_id(1)

        @pl.when(j == 0)
        def _():
            xv = x_ref[...]
            xn_scr[...] = (xv * _rstd(xv, d) * g_ref[...]).astype(BF16)
            acc_scr[...] = jnp.zeros_like(acc_scr)

        if target is not None:
            @pl.when((i == 0) & (j == 0))
            def _():
                loss_ref[...] = jnp.zeros_like(loss_ref)

        xn = xn_scr[...]
        gg = _dot(xn, wg_ref[0])
        uu = _dot(xn, wu_ref[0])
        gg_ref[...] = gg.astype(BF16)
        uu_ref[...] = uu.astype(BF16)
        act = gg * jax.nn.sigmoid(gg) * uu
        acc_scr[...] += _dot(act.astype(BF16), wd_ref[0])

        @pl.when(j == 1)
        def _():
            y = x_ref[...] + 0.5 * acc_scr[...]
            if target is not None:
                e = y - e_ref[...]
                dy_ref[...] = e * (1.0 / d)
                part = 0.5 * jnp.sum(jnp.sum(e * e, axis=-1, keepdims=True) * (1.0 / d), axis=0, keepdims=True)
                loss_ref[...] += jnp.broadcast_to(part, loss_ref.shape)
            else:
                o_ref[...] = y
                if next_gain is not None:
                    h_ref[...] = (y * _rstd(y, d) * e_ref[...]).astype(BF16)

    row = pl.BlockSpec((tm, d), lambda i, j: (i, 0))
    vec = pl.BlockSpec((1, d), lambda i, j: (0, 0))
    ffb = pl.BlockSpec((tm, FF_TILE), lambda i, j: (i, j))
    f32_rows, bf16_ff = jax.ShapeDtypeStruct((t, d), F32), jax.ShapeDtypeStruct((t, D_FF), BF16)
    if target is not None:
        extra_spec, out_specs = [row], [row, pl.BlockSpec((1, LANES), lambda i, j: (0, 0)), ffb, ffb]
        out_shape = [f32_rows, jax.ShapeDtypeStruct((1, LANES), F32), bf16_ff, bf16_ff]
    elif next_gain is not None:
        extra_spec, out_specs = [vec], [row, ffb, ffb, row]
        out_shape = [f32_rows, bf16_ff, bf16_ff, jax.ShapeDtypeStruct((t, d), BF16)]
    else:
        extra_spec, out_specs, out_shape = [], [row, ffb, ffb], [f32_rows, bf16_ff, bf16_ff]
    return _call_with_exchange(
        ex, body, name, (t // tm, 2),
        [row, vec,
         pl.BlockSpec((1, d, FF_TILE), lambda i, j: (j, 0, 0)),
         pl.BlockSpec((1, d, FF_TILE), lambda i, j: (j + 2, 0, 0)),
         pl.BlockSpec((1, FF_TILE, d), lambda i, j: (j, 0, 0))] + extra_spec,
        out_specs, out_shape,
        [pltpu.VMEM((tm, d), BF16), pltpu.VMEM((tm, d), F32)], (x, g, wgu4, wgu4, wd2, *extra))


def _ffn_bwd(x, g, dy, gpre, upre, wgu4, wd2, name, ex=None):
    t, d = x.shape
    tm = _row_tile(t, 512)

    def body(dy_ref, gg_ref, uu_ref, wgu_hbm, wd_hbm, dg_ref, du_ref, act_ref, part_ref, wg_ref, wu_ref, wd_ref):
        j = pl.program_id(0)

        @pl.when(pl.program_id(1) == 0)
        def _():
            pltpu.sync_copy(wgu_hbm.at[j], wg_ref.at[0])
            pltpu.sync_copy(wgu_hbm.at[j + 2], wu_ref.at[0])
            pltpu.sync_copy(wd_hbm.at[j], wd_ref.at[0])

        gg = gg_ref[...].astype(F32)
        uu = uu_ref[...].astype(F32)
        sg = jax.nn.sigmoid(gg)
        silu = gg * sg
        act_ref[...] = (silu * uu).astype(BF16)
        dyh = (0.5 * dy_ref[...]).astype(BF16)
        dact = _dot_nt(dyh, wd_ref[0])
        du = (dact * silu).astype(BF16)
        dgt = (dact * uu * (sg * (1.0 + gg * (1.0 - sg)))).astype(BF16)
        du_ref[...] = du
        dg_ref[...] = dgt
        part_ref[0] = (_dot_nt(dgt, wg_ref[0]) + _dot_nt(du, wu_ref[0])).astype(BF16)

    row = pl.BlockSpec((tm, d), lambda j, i: (i, 0))
    ffb = pl.BlockSpec((tm, FF_TILE), lambda j, i: (i, j))
    dgt, dup, act, parts, *got = _call_with_exchange(
        ex, body, name, (2, t // tm),
        [row, ffb, ffb, ANY, ANY],
        [ffb, ffb, ffb, pl.BlockSpec((1, tm, d), lambda j, i: (j, i, 0))],
        [jax.ShapeDtypeStruct((t, D_FF), BF16)] * 3 + [jax.ShapeDtypeStruct((2, t, d), BF16)],
        [pltpu.VMEM((1, d, FF_TILE), BF16), pltpu.VMEM((1, d, FF_TILE), BF16), pltpu.VMEM((1, FF_TILE, d), BF16)],
        (dy, gpre, upre, wgu4, wd2))

    def norm_body(x_ref, g_ref, p_ref, dy_ref, dx_ref, dgain_ref, xn_ref):
        @pl.when(pl.program_id(0) == 0)
        def _():
            dgain_ref[...] = jnp.zeros_like(dgain_ref)

        xv = x_ref[...]
        r = _rstd(xv, d)
        xn_ref[...] = (xv * r * g_ref[...]).astype(BF16)
        dx, dgr = _rms_vjp(xv, r, g_ref[...], p_ref[0].astype(F32) + p_ref[1].astype(F32), d)
        dx_ref[...] = dy_ref[...] + dx
        dgain_ref[...] += jnp.sum(dgr, axis=0, keepdims=True)

    tn = _row_tile(t, 256)
    nrow = pl.BlockSpec((tn, d), lambda i: (i, 0))
    vec = pl.BlockSpec((1, d), lambda i: (0, 0))
    dx, dgain, xn = pl.pallas_call(
        norm_body, name=name + "_norm", grid=(t // tn,),
        in_specs=[nrow, vec, pl.BlockSpec((2, tn, d), lambda i: (0, i, 0)), nrow],
        out_specs=[nrow, vec, nrow],
        out_shape=[jax.ShapeDtypeStruct((t, d), F32), jax.ShapeDtypeStruct((1, d), F32),
                   jax.ShapeDtypeStruct((t, d), BF16)],
        compiler_params=_cparams())(x, g, parts, dy)
    return [dx, dgain, xn, dgt, dup, act] + got


def _sgu_layernorm(vpre, lg, lb):
    v = _gelu(vpre)
    mu = jnp.mean(v, axis=-1, keepdims=True)
    xc = v - mu
    rstd = lax.rsqrt(jnp.mean(xc * xc, axis=-1, keepdims=True) + EPS)
    xhat = xc * rstd
    return xhat, rstd, xhat * lg + lb


def _sgu_fwd(zuv, lg, lb, wt, bias_l, name):
    t = zuv.shape[0]
    tm = _row_tile(t, 512)

    def body(u_ref, v_ref, lg_ref, lb_ref, wt_ref, bl_ref, o_ref, vln_scr):
        _, _, vln = _sgu_layernorm(v_ref[...], lg_ref[...], lb_ref[...])
        vln_scr[...] = vln.astype(BF16)
        lo = lax.broadcasted_iota(jnp.int32, (CHUNK, LANES), 1) < 64
        for c in range(tm // CHUNK):
            rows = slice(c * CHUNK, (c + 1) * CHUNK)
            for p in range(SG_GROUPS // 2):
                cols = slice(p * LANES, (p + 1) * LANES)
                vp = vln_scr[rows, cols]
                mixed = jnp.where(lo, _dot(wt_ref[2 * p], vp), _dot(wt_ref[2 * p + 1], vp)) + bl_ref[:, cols]
                o_ref[rows, cols] = (_gelu(u_ref[rows, cols]) * mixed).astype(BF16)

    half = lambda k: pl.BlockSpec((tm, SG_WIDTH), lambda i: (i, k))
    vec = pl.BlockSpec((1, SG_WIDTH), lambda i: (0, 0))
    return pl.pallas_call(
        body, name=name, grid=(t // tm,),
        in_specs=[half(0), half(1), vec, vec,
                  pl.BlockSpec((SG_GROUPS, CHUNK, CHUNK), lambda i: (0, 0, 0)),
                  pl.BlockSpec((CHUNK, SG_WIDTH), lambda i: (0, 0))],
        out_specs=pl.BlockSpec((tm, SG_WIDTH), lambda i: (i, 0)),
        out_shape=jax.ShapeDtypeStruct((t, SG_WIDTH), BF16),
        scratch_shapes=[pltpu.VMEM((tm, SG_WIDTH), BF16)],
        compiler_params=_cparams())(zuv, zuv, lg, lb, wt, bias_l)


def _sgu_bwd(zuv, dya, lg, lb, wt, wt_t, bias_l, name):
    t = zuv.shape[0]
    tm = _row_tile(t, 256)
    nsteps = t // tm

    def body(u_ref, v_ref, dy_ref, lg_ref, lb_ref, wt_ref, wtt_ref, bl_ref,
             dz_ref, dwt_ref, dbl_ref, dlg_ref, dlb_ref, vln_scr, dvln_scr, dbacc_scr):
        step = pl.program_id(0)

        @pl.when(step == 0)
        def _():
            dwt_ref[...] = jnp.zeros_like(dwt_ref)
            dlg_ref[...] = jnp.zeros_like(dlg_ref)
            dlb_ref[...] = jnp.zeros_like(dlb_ref)
            dbl_ref[...] = jnp.zeros_like(dbl_ref)
            dbacc_scr[...] = jnp.zeros_like(dbacc_scr)

        vpre = v_ref[...]
        lgv = lg_ref[...]
        xhat, rstd, vln = _sgu_layernorm(vpre, lgv, lb_ref[...])
        vln_scr[...] = vln.astype(BF16)
        lo = lax.broadcasted_iota(jnp.int32, (CHUNK, LANES), 1) < 64
        for c in range(tm // CHUNK):
            rows = slice(c * CHUNK, (c + 1) * CHUNK)
            for p in range(SG_GROUPS // 2):
                cols = slice(p * LANES, (p + 1) * LANES)
                vp = vln_scr[rows, cols]
                mixed = jnp.where(lo, _dot(wt_ref[2 * p], vp), _dot(wt_ref[2 * p + 1], vp)) + bl_ref[:, cols]
                upre = u_ref[rows, cols]
                dyp = dy_ref[rows, cols]
                dz_ref[rows, cols] = (dyp * mixed * _gelu_grad(upre)).astype(BF16)
                dm = dyp * _gelu(upre)
                dbacc_scr[:, cols] += dm
                dlo = jnp.where(lo, dm, 0.0).astype(BF16)
                dhi = jnp.where(lo, 0.0, dm).astype(BF16)
                dvln_scr[rows, cols] = _dot(wtt_ref[2 * p], dlo) + _dot(wtt_ref[2 * p + 1], dhi)
                dwt_ref[2 * p] += _dot_nt(dlo, vp)
                dwt_ref[2 * p + 1] += _dot_nt(dhi, vp)
        dvln = dvln_scr[...]
        dlg_ref[...] += jnp.sum(dvln * xhat, axis=0, keepdims=True)
        dlb_ref[...] += jnp.sum(dvln, axis=0, keepdims=True)
        dxh = dvln * lgv
        dv = rstd * (dxh - jnp.mean(dxh, axis=-1, keepdims=True)
                     - xhat * jnp.mean(dxh * xhat, axis=-1, keepdims=True))
        dz_ref[:, SG_WIDTH:] = (dv * _gelu_grad(vpre)).astype(BF16)

        @pl.when(step == nsteps - 1)
        def _():
            rr = lax.broadcasted_iota(jnp.int32, (CHUNK, CHUNK), 0)
            cc = lax.broadcasted_iota(jnp.int32, (CHUNK, CHUNK), 1)
            tril = (cc <= rr).astype(F32)
            for gidx in range(SG_GROUPS):
                dwt_ref[gidx] = dwt_ref[gidx] * tril
            kk = lax.broadcasted_iota(jnp.int32, (SG_WIDTH, LANES), 0)
            gg = lax.broadcasted_iota(jnp.int32, (SG_WIDTH, LANES), 1)
            sel = ((kk // 64) == gg).astype(F32)
            dbl_ref[...] = jnp.dot(dbacc_scr[...], sel, preferred_element_type=F32,
                                   precision=lax.Precision.HIGHEST)

    half = lambda k: pl.BlockSpec((tm, SG_WIDTH), lambda i: (i, k))
    vec = pl.BlockSpec((1, SG_WIDTH), lambda i: (0, 0))
    wspec = pl.BlockSpec((SG_GROUPS, CHUNK, CHUNK), lambda i: (0, 0, 0))
    return pl.pallas_call(
        body, name=name, grid=(nsteps,),
        in_specs=[half(0), half(1), pl.BlockSpec((tm, SG_WIDTH), lambda i: (i, 0)), vec, vec,
                  wspec, wspec, pl.BlockSpec((CHUNK, SG_WIDTH), lambda i: (0, 0))],
        out_specs=[pl.BlockSpec((tm, 2 * SG_WIDTH), lambda i: (i, 0)), wspec,
                   pl.BlockSpec((CHUNK, LANES), lambda i: (0, 0)), vec, vec],
        out_shape=[jax.ShapeDtypeStruct((t, 2 * SG_WIDTH), BF16),
                   jax.ShapeDtypeStruct((SG_GROUPS, CHUNK, CHUNK), F32),
                   jax.ShapeDtypeStruct((CHUNK, LANES), F32),
                   jax.ShapeDtypeStruct((1, SG_WIDTH), F32), jax.ShapeDtypeStruct((1, SG_WIDTH), F32)],
        scratch_shapes=[pltpu.VMEM((tm, SG_WIDTH), BF16), pltpu.VMEM((tm, SG_WIDTH), F32),
                        pltpu.VMEM((CHUNK, SG_WIDTH), F32)],
        compiler_params=_cparams())(zuv, zuv, dya, lg, lb, wt, wt_t, bias_l)


def _rope(x, c, s1, s2):
    return x * c + pltpu.roll(x, LANES - 16, 1) * s1 + pltpu.roll(x, 16, 1) * s2


def _rope_t(dy, c, s1, s2):
    return dy * c + pltpu.roll(dy * s1, 16, 1) + pltpu.roll(dy * s2, LANES - 16, 1)


def _mla_prep_fwd(zcq, zckv, zkr, gcq, gckv, qg, kg, wuq, wuk, wuv, rc, rs1, rs2, name, ex=None):
    t = zcq.shape[0]
    tm = _row_tile(t, 256)
    hd = MLA_HEADS * LANES

    def body(zcq_ref, zckv_ref, zkr_ref, gcq_ref, gckv_ref, qg_ref, kg_ref, wuq_ref, wuk_ref, wuv_ref,
             c_ref, s1_ref, s2_ref, q_ref, k_ref, v_ref, cqn_ref, ckvn_ref):
        c, s1, s2 = c_ref[...], s1_ref[...], s2_ref[...]
        xq = zcq_ref[...]
        cqn = (xq * _rstd(xq, MLA_Q_RANK) * gcq_ref[...]).astype(BF16)
        cqn_ref[...] = cqn
        ql = _dot(cqn, wuq_ref[...])
        xk = zckv_ref[...]
        ckvn = (xk * _rstd(xk, MLA_KV_RANK) * gckv_ref[...]).astype(BF16)
        ckvn_ref[...] = ckvn
        kl = _dot(ckvn, wuk_ref[...])
        slot_lane = lax.broadcasted_iota(jnp.int32, (tm, hd), 1) % LANES
        v_ref[...] = jnp.where(slot_lane == V_ONES_LANE, 1.0, _dot(ckvn, wuv_ref[...])).astype(BF16)
        kr = zkr_ref[...]
        for h in range(MLA_HEADS):
            sl = slice(h * LANES, (h + 1) * LANES)
            qh = ql[:, sl]
            q_ref[:, sl] = (_rope(qh * _rstd(qh, MLA_QK) * qg_ref[...], c, s1, s2) * ATTN_SCALE2).astype(BF16)
            kh = kl[:, sl] + kr
            k_ref[:, sl] = _rope(kh * _rstd(kh, MLA_QK) * kg_ref[...], c, s1, s2).astype(BF16)

    row = lambda n: pl.BlockSpec((tm, n), lambda i: (i, 0))
    full = lambda a: pl.BlockSpec(a.shape, lambda i: (0, 0))
    return _call_with_exchange(
        ex, body, name, (t // tm,),
        [row(MLA_Q_RANK), row(MLA_KV_RANK), row(LANES), full(gcq), full(gckv), full(qg), full(kg),
         full(wuq), full(wuk), full(wuv), row(LANES), row(LANES), row(LANES)],
        [row(hd), row(hd), row(hd), row(MLA_Q_RANK), row(MLA_KV_RANK)],
        [jax.ShapeDtypeStruct((t, hd), BF16)] * 3
        + [jax.ShapeDtypeStruct((t, MLA_Q_RANK), BF16), jax.ShapeDtypeStruct((t, MLA_KV_RANK), BF16)],
        [], (zcq, zckv, zkr, gcq, gckv, qg, kg, wuq, wuk, wuv, rc, rs1, rs2))


def _mla_prep_bwd(zcq, zckv, zkr, gcq, gckv, qg, kg, wuq, wuk, wuv, rc, rs1, rs2, dq, dk, dv, name):
    t = zcq.shape[0]
    tm = _row_tile(t, 256)
    hd = MLA_HEADS * LANES

    def body(zcq_ref, zckv_ref, zkr_ref, gcq_ref, gckv_ref, qg_ref, kg_ref, wuq_ref, wuk_ref, wuv_ref,
             c_ref, s1_ref, s2_ref, dq_ref, dk_ref, dv_ref,
             dzcq_ref, dzckv_ref, dzkr_ref, dql_ref, dkl_ref, dgcq_ref, dgckv_ref, dqg_ref, dkg_ref):
        @pl.when(pl.program_id(0) == 0)
        def _():
            for ref in (dgcq_ref, dgckv_ref, dqg_ref, dkg_ref):
                ref[...] = jnp.zeros_like(ref)

        c, s1, s2 = c_ref[...], s1_ref[...], s2_ref[...]
        qgv, kgv = qg_ref[...], kg_ref[...]
        xq = zcq_ref[...]
        rq = _rstd(xq, MLA_Q_RANK)
        ql = _dot((xq * rq * gcq_ref[...]).astype(BF16), wuq_ref[...])
        xk = zckv_ref[...]
        rk = _rstd(xk, MLA_KV_RANK)
        kl = _dot((xk * rk * gckv_ref[...]).astype(BF16), wuk_ref[...])
        kr = zkr_ref[...]
        dqg_acc = jnp.zeros((tm, LANES), F32)
        dkg_acc = jnp.zeros((tm, LANES), F32)
        dkr = jnp.zeros((tm, LANES), F32)
        for h in range(MLA_HEADS):
            sl = slice(h * LANES, (h + 1) * LANES)
            qh = ql[:, sl]
            dqh, dgr = _rms_vjp(qh, _rstd(qh, MLA_QK), qgv, _rope_t(dq_ref[:, sl], c, s1, s2), MLA_QK)
            dql_ref[:, sl] = dqh.astype(BF16)
            dqg_acc += dgr
            kh = kl[:, sl] + kr
            dkh, dgr = _rms_vjp(kh, _rstd(kh, MLA_QK), kgv, _rope_t(dk_ref[:, sl], c, s1, s2), MLA_QK)
            dkl_ref[:, sl] = dkh.astype(BF16)
            dkg_acc += dgr
            dkr += dkh
        dqg_ref[...] += jnp.sum(dqg_acc, axis=0, keepdims=True)
        dkg_ref[...] += jnp.sum(dkg_acc, axis=0, keepdims=True)
        lane = lax.broadcasted_iota(jnp.int32, (tm, LANES), 1)
        dzkr_ref[...] = jnp.where((lane >= MLA_NOPE) & (lane < MLA_QK), dkr, 0.0).astype(BF16)
        dcqn = _dot_nt(dql_ref[...], wuq_ref[...])
        dx, dgr = _rms_vjp(xq, rq, gcq_ref[...], dcqn, MLA_Q_RANK)
        dzcq_ref[...] = dx.astype(BF16)
        dgcq_ref[...] += jnp.sum(dgr, axis=0, keepdims=True)
        dckvn = _dot_nt(dkl_ref[...], wuk_ref[...]) + _dot_nt(dv_ref[...].astype(BF16), wuv_ref[...])
        dx, dgr = _rms_vjp(xk, rk, gckv_ref[...], dckvn, MLA_KV_RANK)
        dzckv_ref[...] = dx.astype(BF16)
        dgckv_ref[...] += jnp.sum(dgr, axis=0, keepdims=True)

    row = lambda n: pl.BlockSpec((tm, n), lambda i: (i, 0))
    full = lambda a: pl.BlockSpec(a.shape, lambda i: (0, 0))
    vec = lambda n: pl.BlockSpec((1, n), lambda i: (0, 0))
    return pl.pallas_call(
        body, name=name, grid=(t // tm,),
        in_specs=[row(MLA_Q_RANK), row(MLA_KV_RANK), row(LANES), full(gcq), full(gckv), full(qg), full(kg),
                  full(wuq), full(wuk), full(wuv), row(LANES), row(LANES), row(LANES), row(hd), row(hd), row(hd)],
        out_specs=[row(MLA_Q_RANK), row(MLA_KV_RANK), row(LANES), row(hd), row(hd),
                   vec(MLA_Q_RANK), vec(MLA_KV_RANK), vec(LANES), vec(LANES)],
        out_shape=[jax.ShapeDtypeStruct((t, MLA_Q_RANK), BF16), jax.ShapeDtypeStruct((t, MLA_KV_RANK), BF16),
                   jax.ShapeDtypeStruct((t, LANES), BF16), jax.ShapeDtypeStruct((t, hd), BF16),
                   jax.ShapeDtypeStruct((t, hd), BF16), jax.ShapeDtypeStruct((1, MLA_Q_RANK), F32),
                   jax.ShapeDtypeStruct((1, MLA_KV_RANK), F32), jax.ShapeDtypeStruct((1, LANES), F32),
                   jax.ShapeDtypeStruct((1, LANES), F32)],
        compiler_params=_cparams(),
    )(zcq, zckv, zkr, gcq, gckv, qg, kg, wuq, wuk, wuv, rc, rs1, rs2, dq, dk, dv)


def _attn_tiles(t):
    tq = 512 if t >= 2048 else 128
    return tq, min(t, 4 * tq), min(t, 4 * tq)


def _causal_keep(tq, nk, i, j, tk):
    row = lax.broadcasted_iota(jnp.int32, (tq, nk), 0)
    col = lax.broadcasted_iota(jnp.int32, (tq, nk), 1)
    return (col - row) <= (i * tq - j * tk)


def _causal_keep_t(tq, nk, i, j, tk):
    key = lax.broadcasted_iota(jnp.int32, (nk, tq), 0)
    qry = lax.broadcasted_iota(jnp.int32, (nk, tq), 1)
    return (key - qry) <= (i * tq - j * tk)


ATTN_FWD_HEADS_PER_STEP = 4
ATTN_BWD_HEADS_PER_STEP = 2


def _attn_fwd(q, k, v, name, ex=None):
    t, hd = q.shape
    hp = ATTN_FWD_HEADS_PER_STEP
    tq, tk, _ = _attn_tiles(t)
    pairs = [(i, j) for i in range(t // tq) for j in range(((i + 1) * tq - 1) // tk + 1)]
    ii = np.array([p[0] for p in pairs], np.int32)
    jj = np.array([p[1] for p in pairs], np.int32)

    def body(ii_ref, jj_ref, q_ref, k_ref, v_ref, o_ref, lse_ref, m_scr, acc_scr):
        s_id = pl.program_id(1)
        i, j = ii_ref[s_id], jj_ref[s_id]
        last = j == ((i + 1) * tq - 1) // tk
        ones_lane = lax.broadcasted_iota(jnp.int32, (tq, LANES), 1) == V_ONES_LANE

        @pl.when(j == 0)
        def _():
            m_scr[...] = jnp.full_like(m_scr, NEG)
            acc_scr[...] = jnp.zeros_like(acc_scr)

        def step(masked, nk):
            scores = [_dot_nt(q_ref[:, hh * LANES:(hh + 1) * LANES], k_ref[:nk, hh * LANES:(hh + 1) * LANES])
                      for hh in range(hp)]
            for hh in range(hp):
                sl = slice(hh * LANES, (hh + 1) * LANES)
                s = scores[hh]
                if masked:
                    s = jnp.where(_causal_keep(tq, nk, i, j, tk), s, NEG)
                m_prev = m_scr[hh]
                m_new = jnp.maximum(m_prev, jnp.max(s, axis=1, keepdims=True))
                p = jnp.exp2(s - m_new)
                alpha = jnp.exp2(m_prev - m_new)
                acc = alpha * acc_scr[:, sl] + _dot(p.astype(BF16), v_ref[:nk, sl])
                if masked:
                    l_new = jnp.sum(jnp.where(ones_lane, acc, 0.0), axis=1, keepdims=True)
                    o_ref[:, sl] = (acc / l_new).astype(BF16)
                    lse_ref[:, sl] = jnp.broadcast_to(m_new + jnp.log(l_new) * LOG2E, (tq, LANES))
                else:
                    acc_scr[:, sl] = acc
                    m_scr[hh] = m_new

        @pl.when(jnp.logical_not(last))
        def _():
            step(False, tk)

        r = (((i + 1) * tq - 1) % tk) // tq
        for rr in range(tk // tq):
            @pl.when(last & (r == rr))
            def _():
                step(True, (rr + 1) * tq)

    w = hp * LANES
    qspec = pl.BlockSpec((tq, w), lambda h, s, ii_r, jj_r: (ii_r[s], h))
    kspec = pl.BlockSpec((tk, w), lambda h, s, ii_r, jj_r: (jj_r[s], h))
    return _call_with_exchange(
        ex, body, name, (hd // w, len(pairs)), [qspec, kspec, kspec], [qspec, qspec],
        [jax.ShapeDtypeStruct((t, hd), BF16), jax.ShapeDtypeStruct((t, hd), F32)],
        [pltpu.VMEM((hp, tq, 1), F32), pltpu.VMEM((tq, w), F32)], (q, k, v),
        prefetch=(jnp.asarray(ii), jnp.asarray(jj)))


def _attn_bwd_rows(o, lse, do, name):
    t, hd = o.shape
    heads = hd // LANES
    tm = _row_tile(t, 512)

    def body(o_ref, lse_ref, do_ref, out_ref):
        lane = lax.broadcasted_iota(jnp.int32, (tm, LANES), 1)
        acc = jnp.zeros((tm, LANES), F32)
        for h in range(heads):
            sl = slice(h * LANES, (h + 1) * LANES)
            delta = jnp.sum(do_ref[:, sl].astype(F32) * o_ref[:, sl].astype(F32), axis=1, keepdims=True)
            acc = jnp.where(lane == h, delta, acc)
            acc = jnp.where(lane == heads + h, lse_ref[:, sl], acc)
        out_ref[...] = acc

    row = pl.BlockSpec((tm, hd), lambda i: (i, 0))
    cols = pl.pallas_call(
        body, name=name, grid=(t // tm,), in_specs=[row, row, row],
        out_specs=pl.BlockSpec((tm, LANES), lambda i: (i, 0)),
        out_shape=jax.ShapeDtypeStruct((t, LANES), F32), compiler_params=_cparams())(o, lse, do)
    rows = cols.T
    return rows[:heads].reshape(heads, 1, t), rows[heads:2 * heads].reshape(heads, 1, t)


def _attn_bwd(q, k, v, delta_rows, lse_rows, do, name, ex=None):
    t, hd = q.shape
    hp = ATTN_BWD_HEADS_PER_STEP
    tq, _, tk = _attn_tiles(t)
    nq = t // tq
    pairs = [(i, j) for j in range(t // tk) for i in range((j * tk) // tq, nq)]
    ii = np.array([p[0] for p in pairs], np.int32)
    jj = np.array([p[1] for p in pairs], np.int32)

    def body(jj_ref, ii_ref, q_ref, k_ref, v_ref, delta_ref, lse_ref, do_ref, dq_ref, dk_ref, dv_ref,
             dk_scr, dv_scr, dq_scr):
        s_id = pl.program_id(1)
        i, j = ii_ref[s_id], jj_ref[s_id]

        @pl.when(s_id == 0)
        def _():
            dq_scr[...] = jnp.zeros_like(dq_scr)

        @pl.when(i == (j * tk) // tq)
        def _():
            dk_scr[...] = jnp.zeros_like(dk_scr)
            dv_scr[...] = jnp.zeros_like(dv_scr)

        rows = pl.ds(pl.multiple_of(i * tq, tq), tq)

        def step(masked, nk):
            heads = [slice(hh * LANES, (hh + 1) * LANES) for hh in range(hp)]
            scores = [_dot_nt(k_ref[:nk, sl], q_ref[:, sl]) for sl in heads]
            for hh, sl in enumerate(heads):
                qv, kv, dov = q_ref[:, sl], k_ref[:nk, sl], do_ref[:, sl]
                st = scores[hh]
                if masked:
                    st = jnp.where(_causal_keep_t(tq, nk, i, j, tk), st, NEG)
                pt = jnp.exp2(st - lse_ref[hh])
                dv_scr[:nk, sl] += _dot(pt.astype(BF16), dov)
                dpt = _dot_nt(v_ref[:nk, sl], dov)
                dst = (pt * (dpt - delta_ref[hh]) * ATTN_SCALE).astype(BF16)
                dk_scr[:nk, sl] += _dot(dst, qv)
                dq_scr[rows, sl] += _dot_tn(dst, kv)

        seen = jnp.minimum((i + 1) * tq - j * tk, tk)
        for nk in range(tq, tk + 1, tq):
            @pl.when((seen == nk) & ((i + 1) * tq - j * tk <= tk))
            def _():
                step(True, nk)

        @pl.when((i + 1) * tq - j * tk > tk)
        def _():
            step(False, tk)

        @pl.when(i == nq - 1)
        def _():
            dk_ref[...] = (dk_scr[...] * (1.0 / ATTN_SCALE2)).astype(BF16)
            dv_ref[...] = dv_scr[...].astype(BF16)

        @pl.when(s_id == len(pairs) - 1)
        def _():
            dq_ref[...] = dq_scr[...].astype(BF16)

    w = hp * LANES
    qspec = pl.BlockSpec((tq, w), lambda h, s, jj_r, ii_r: (ii_r[s], h))
    kspec = pl.BlockSpec((tk, w), lambda h, s, jj_r, ii_r: (jj_r[s], h))
    rspec = pl.BlockSpec((hp, 1, tq), lambda h, s, jj_r, ii_r: (h, 0, ii_r[s]))
    return _call_with_exchange(
        ex, body, name, (hd // w, len(pairs)), [qspec, kspec, kspec, rspec, rspec, qspec],
        [pl.BlockSpec((t, w), lambda h, s, jj_r, ii_r: (0, h)), kspec, kspec],
        [jax.ShapeDtypeStruct((t, hd), BF16)] * 3,
        [pltpu.VMEM((tk, w), F32), pltpu.VMEM((tk, w), F32), pltpu.VMEM((t, w), F32)],
        (q, k, v, delta_rows, lse_rows, do), prefetch=(jnp.asarray(jj), jnp.asarray(ii)))


MEM_W = MEM_HEADS * LANES


def _mem_kv_fwd(mem, gmem, wkv, kg, name):
    m, d = mem.shape

    def body(mem_ref, g_ref, w_ref, kg_ref, k_ref, v_ref, mn_ref):
        xv = mem_ref[...]
        mn = (xv * _rstd(xv, d) * g_ref[...]).astype(BF16)
        mn_ref[...] = mn
        kvm = _dot(mn, w_ref[...])
        v_ref[...] = kvm[:, MEM_W:].astype(BF16)
        for h in range(MEM_HEADS):
            sl = slice(h * LANES, (h + 1) * LANES)
            kh = kvm[:, sl]
            k_ref[:, sl] = (kh * _rstd(kh, LANES) * kg_ref[...]).astype(BF16)

    full = lambda a: pl.BlockSpec(a.shape, lambda i: (0, 0))
    return pl.pallas_call(
        body, name=name, grid=(1,), in_specs=[full(mem), full(gmem), full(wkv), full(kg)],
        out_specs=[pl.BlockSpec((m, MEM_W), lambda i: (0, 0)), pl.BlockSpec((m, MEM_W), lambda i: (0, 0)),
                   pl.BlockSpec((m, d), lambda i: (0, 0))],
        out_shape=[jax.ShapeDtypeStruct((m, MEM_W), BF16), jax.ShapeDtypeStruct((m, MEM_W), BF16),
                   jax.ShapeDtypeStruct((m, d), BF16)],
        compiler_params=_cparams())(mem, gmem, wkv, kg)


def _mem_softmax(qn, kh):
    s = _dot_nt(qn, kh) * (LANES ** -0.5)
    e = jnp.exp(s - jnp.max(s, axis=1, keepdims=True))
    return e / jnp.sum(e, axis=1, keepdims=True)


def _mem_attn_fwd(zqm, qg, km, vm, name):
    t = zqm.shape[0]
    tm = _row_tile(t, 512)

    def body(q_ref, qg_ref, k_ref, v_ref, o_ref):
        for h in range(MEM_HEADS):
            sl = slice(h * LANES, (h + 1) * LANES)
            qh = q_ref[:, sl]
            qn = (qh * _rstd(qh, LANES) * qg_ref[...]).astype(BF16)
            p = _mem_softmax(qn, k_ref[:, sl])
            o_ref[:, sl] = _dot(p.astype(BF16), v_ref[:, sl]).astype(BF16)

    row = pl.BlockSpec((tm, MEM_W), lambda i: (i, 0))
    full = lambda a: pl.BlockSpec(a.shape, lambda i: (0, 0))
    return pl.pallas_call(
        body, name=name, grid=(t // tm,), in_specs=[row, full(qg), full(km), full(vm)], out_specs=row,
        out_shape=jax.ShapeDtypeStruct((t, MEM_W), BF16), compiler_params=_cparams())(zqm, qg, km, vm)


def _mem_attn_bwd(zqm, dyc, qg, km, vm, name):
    t = zqm.shape[0]
    m = km.shape[0]
    tm = _row_tile(t, 256)

    def body(q_ref, dy_ref, qg_ref, k_ref, v_ref, dz_ref, dk_ref, dv_ref, dqg_ref):
        @pl.when(pl.program_id(0) == 0)
        def _():
            dk_ref[...] = jnp.zeros_like(dk_ref)
            dv_ref[...] = jnp.zeros_like(dv_ref)
            dqg_ref[...] = jnp.zeros_like(dqg_ref)

        qgv = qg_ref[...]
        dqg_acc = jnp.zeros((tm, LANES), F32)
        heads = [slice(h * LANES, (h + 1) * LANES) for h in range(MEM_HEADS)]
        dps = [_dot_nt(dy_ref[:, sl], v_ref[:, sl]) for sl in heads]
        for h, sl in enumerate(heads):
            qh = q_ref[:, sl]
            r = _rstd(qh, LANES)
            qn = (qh * r * qgv).astype(BF16)
            kh = k_ref[:, sl]
            p = _mem_softmax(qn, kh)
            dov = dy_ref[:, sl]
            dv_ref[:, sl] += _dot_tn(p.astype(BF16), dov)
            dp = dps[h]
            ds = (p * (dp - jnp.sum(dp * p, axis=1, keepdims=True)) * (LANES ** -0.5)).astype(BF16)
            dk_ref[:, sl] += _dot_tn(ds, qn)
            dqh, dgr = _rms_vjp(qh, r, qgv, _dot(ds, kh), LANES)
            dz_ref[:, sl] = dqh.astype(BF16)
            dqg_acc += dgr
        dqg_ref[...] += jnp.sum(dqg_acc, axis=0, keepdims=True)

    row = pl.BlockSpec((tm, MEM_W), lambda i: (i, 0))
    full = lambda a: pl.BlockSpec(a.shape, lambda i: (0, 0))
    acc = pl.BlockSpec((m, MEM_W), lambda i: (0, 0))
    return pl.pallas_call(
        body, name=name, grid=(t // tm,), in_specs=[row, row, full(qg), full(km), full(vm)],
        out_specs=[row, acc, acc, pl.BlockSpec((1, LANES), lambda i: (0, 0))],
        out_shape=[jax.ShapeDtypeStruct((t, MEM_W), BF16), jax.ShapeDtypeStruct((m, MEM_W), F32),
                   jax.ShapeDtypeStruct((m, MEM_W), F32), jax.ShapeDtypeStruct((1, LANES), F32)],
        compiler_params=_cparams())(zqm, dyc, qg, km, vm)


def _mem_kv_bwd(mem, gmem, wkv, kg, dkn, dvm, name):
    m, d = mem.shape

    def body(mem_ref, g_ref, w_ref, kg_ref, dk_ref, dv_ref, dw_ref, dkg_ref, dg_ref, dkv_scr):
        xv = mem_ref[...]
        r = _rstd(xv, d)
        mn = (xv * r * g_ref[...]).astype(BF16)
        kvm = _dot(mn, w_ref[...])
        dkv_scr[:, MEM_W:] = dv_ref[...].astype(BF16)
        dkg_acc = jnp.zeros((m, LANES), F32)
        for h in range(MEM_HEADS):
            sl = slice(h * LANES, (h + 1) * LANES)
            kh = kvm[:, sl]
            dkh, dgr = _rms_vjp(kh, _rstd(kh, LANES), kg_ref[...], dk_ref[:, sl], LANES)
            dkv_scr[:, sl] = dkh.astype(BF16)
            dkg_acc += dgr
        dkg_ref[...] = jnp.sum(dkg_acc, axis=0, keepdims=True)
        dkv = dkv_scr[...]
        dw_ref[...] = _dot_tn(mn, dkv)
        dmn = _dot_nt(dkv, w_ref[...])
        dg_ref[...] = jnp.sum(dmn * xv * r, axis=0, keepdims=True)

    full = lambda a: pl.BlockSpec(a.shape, lambda i: (0, 0))
    return pl.pallas_call(
        body, name=name, grid=(1,),
        in_specs=[full(mem), full(gmem), full(wkv), full(kg), full(dkn), full(dvm)],
        out_specs=[pl.BlockSpec((d, 2 * MEM_W), lambda i: (0, 0)), pl.BlockSpec((1, LANES), lambda i: (0, 0)),
                   pl.BlockSpec((1, d), lambda i: (0, 0))],
        out_shape=[jax.ShapeDtypeStruct((d, 2 * MEM_W), F32), jax.ShapeDtypeStruct((1, LANES), F32),
                   jax.ShapeDtypeStruct((1, d), F32)],
        scratch_shapes=[pltpu.VMEM((m, 2 * MEM_W), BF16)],
        compiler_params=_cparams())(mem, gmem, wkv, kg, dkn, dvm)


def _merge_fwd(x1, ya, yb, yc, zg, bg, wa, wb, wc, wo, name):
    t, d = x1.shape
    tm = _row_tile(t, 256)

    def body(x_ref, ya_ref, yb_ref, yc_ref, zg_ref, bg_ref, wa_ref, wb_ref, wc_ref, wo_ref,
             x2_ref, mg_ref, pa_ref, pb_ref, pc_ref):
        merged = None
        for k, (y_ref, w_ref, p_ref) in enumerate(
                ((ya_ref, wa_ref, pa_ref), (yb_ref, wb_ref, pb_ref), (yc_ref, wc_ref, pc_ref))):
            sl = slice(k * d, (k + 1) * d)
            pr = _dot(y_ref[...], w_ref[...])
            p_ref[...] = pr.astype(BF16)
            term = jax.nn.sigmoid(zg_ref[:, sl] + bg_ref[:, sl]) * pr
            merged = term if merged is None else merged + term
        mb = merged.astype(BF16)
        mg_ref[...] = mb
        x2_ref[...] = x_ref[...] + _dot(mb, wo_ref[...])

    row = lambda n: pl.BlockSpec((tm, n), lambda i: (i, 0))
    full = lambda a: pl.BlockSpec(a.shape, lambda i: (0, 0))
    return pl.pallas_call(
        body, name=name, grid=(t // tm,),
        in_specs=[row(d), row(ya.shape[1]), row(yb.shape[1]), row(yc.shape[1]), row(3 * d), full(bg),
                  full(wa), full(wb), full(wc), full(wo)],
        out_specs=[row(d)] * 5,
        out_shape=[jax.ShapeDtypeStruct((t, d), F32)] + [jax.ShapeDtypeStruct((t, d), BF16)] * 4,
        compiler_params=_cparams())(x1, ya, yb, yc, zg, bg, wa, wb, wc, wo)


def _merge_bwd(dx2, pa, pb, pc, zg, bg, wa, wb, wc, wo, name, ex=None):
    t, d = dx2.shape
    tm = _row_tile(t, 256)

    def body(dx_ref, pa_ref, pb_ref, pc_ref, zg_ref, bg_ref, wa_ref, wb_ref, wc_ref, wo_ref,
             dpa_ref, dpb_ref, dpc_ref, dzg_ref, dbg_ref, dya_ref, dyb_ref, dyc_ref):
        @pl.when(pl.program_id(0) == 0)
        def _():
            dbg_ref[...] = jnp.zeros_like(dbg_ref)

        dm = _dot_nt(dx_ref[...].astype(BF16), wo_ref[...])
        for k, (p_ref, w_ref, dp_ref, dy_ref) in enumerate(
                ((pa_ref, wa_ref, dpa_ref, dya_ref), (pb_ref, wb_ref, dpb_ref, dyb_ref),
                 (pc_ref, wc_ref, dpc_ref, dyc_ref))):
            sl = slice(k * d, (k + 1) * d)
            gate = jax.nn.sigmoid(zg_ref[:, sl] + bg_ref[:, sl])
            dpr = (dm * gate).astype(BF16)
            dp_ref[...] = dpr
            dzg = dm * p_ref[...].astype(F32) * gate * (1.0 - gate)
            dzg_ref[:, sl] = dzg.astype(BF16)
            dbg_ref[:, sl] += jnp.sum(dzg, axis=0, keepdims=True)
            dy_ref[...] = _dot_nt(dpr, w_ref[...]).astype(dy_ref.dtype)

    row = lambda n: pl.BlockSpec((tm, n), lambda i: (i, 0))
    full = lambda a: pl.BlockSpec(a.shape, lambda i: (0, 0))
    na, nb, nc = wa.shape[0], wb.shape[0], wc.shape[0]
    return _call_with_exchange(
        ex, body, name, (t // tm,),
        [row(d), row(d), row(d), row(d), row(3 * d), full(bg), full(wa), full(wb), full(wc), full(wo)],
        [row(d), row(d), row(d), row(3 * d), pl.BlockSpec((1, 3 * d), lambda i: (0, 0)), row(na), row(nb), row(nc)],
        [jax.ShapeDtypeStruct((t, d), BF16)] * 3
        + [jax.ShapeDtypeStruct((t, 3 * d), BF16), jax.ShapeDtypeStruct((1, 3 * d), F32),
           jax.ShapeDtypeStruct((t, na), F32), jax.ShapeDtypeStruct((t, nb), BF16),
           jax.ShapeDtypeStruct((t, nc), BF16)],
        [], (dx2, pa, pb, pc, zg, bg, wa, wb, wc, wo))


def _adamw_math(w, g, m, v):
    bc1 = 1.0 - ADAM_B1 ** ADAM_STEP
    bc2 = 1.0 - ADAM_B2 ** ADAM_STEP
    nm = ADAM_B1 * m + (1.0 - ADAM_B1) * g
    nv = ADAM_B2 * v + (1.0 - ADAM_B2) * (g * g)
    delta = -ADAM_LR * ((nm / bc1) / (jnp.sqrt(nv / bc2) + ADAM_EPS) + ADAM_WD * w)
    return delta, nm, nv


def _div_tile(n, cap, mult):
    best = None
    for cand in range(mult, min(n, cap) + 1, mult):
        if n % cand == 0:
            best = cand
    assert best is not None, (n, cap, mult)
    return best


def _adamw(w, g, m, v, name):
    rows, cols = w.shape
    tr = rows if rows * cols <= 256 * 1024 else _div_tile(rows, 256, 8)

    def body(w_ref, g_ref, m_ref, v_ref, d_ref, nm_ref, nv_ref):
        d_ref[...], nm_ref[...], nv_ref[...] = _adamw_math(w_ref[...], g_ref[...], m_ref[...], v_ref[...])

    blk = pl.BlockSpec((tr, cols), lambda i: (i, 0))
    return pl.pallas_call(
        body, name=name, grid=(rows // tr,), in_specs=[blk] * 4, out_specs=[blk] * 3,
        out_shape=[jax.ShapeDtypeStruct((rows, cols), F32)] * 3, compiler_params=_cparams())(w, g, m, v)


def _adamw_slots(w, slots, m, v, name):
    _, hr, cols = w.shape
    tr = _div_tile(hr, 128, 16)

    def body(w_ref, s_ref, m_ref, v_ref, g_ref, d_ref, nm_ref, nv_ref):
        g = s_ref[0, 0].astype(F32)
        for k in range(1, N_CHIPS):
            g = g + s_ref[0, k].astype(F32)
        g_ref[0] = g
        d_ref[0], nm_ref[0], nv_ref[0] = _adamw_math(w_ref[0], g, m_ref[0], v_ref[0])

    blk = pl.BlockSpec((1, tr, cols), lambda h, i: (h, i, 0))
    return pl.pallas_call(
        body, name=name, grid=(2, hr // tr),
        in_specs=[blk, pl.BlockSpec((1, N_CHIPS, tr, cols), lambda h, i: (h, 0, i, 0)), blk, blk],
        out_specs=[blk] * 4, out_shape=[jax.ShapeDtypeStruct((2, hr, cols), F32)] * 4,
        compiler_params=_cparams())(w, slots, m, v)


ANY = pl.BlockSpec(memory_space=pl.ANY)


def _place():
    x, y, c = lax.axis_index("x"), lax.axis_index("y"), lax.axis_index("c")
    other_chips = [(1 - x, y), (x, 1 - y), (1 - x, 1 - y)]
    return x, y, c, other_chips


def _remote(src, dst, send_sem, recv_sem, to):
    return pltpu.make_async_remote_copy(src_ref=src, dst_ref=dst, send_sem=send_sem, recv_sem=recv_sem,
                                        device_id=to, device_id_type=MESH)


PIECE_BYTES = 384 * 1024


def _row_pieces(half_rows, cols):
    for n in (4, 2):
        if half_rows % (16 * n) == 0 and half_rows * cols * 2 // n >= PIECE_BYTES:
            return [pl.ds(k * (half_rows // n), half_rows // n) for k in range(n)]
    return [pl.ds(0, half_rows)]


def _pieces(arrays, rows_axis):
    return [(w, rows) for w, a in enumerate(arrays) for rows in _row_pieces(a.shape[rows_axis], a.shape[-1])]


def _gather_exchange(shards):
    nw = len(shards)
    pieces = _pieces(shards, 1)
    npc = len(pieces)

    def build(s_refs, g_refs, sems):
        send_sems, recv_sems, local_sems = sems
        x, y, c, chips = _place()
        me = 2 * x + y
        sibling = (x, y, 1 - c)
        mine = [pltpu.make_async_copy(s_refs[w], g_refs[w].at[me], local_sems.at[w]) for w in range(nw)]
        first = [_remote(s_refs[w].at[c, rows], g_refs[w].at[me, c, rows], send_sems.at[k, p], recv_sems.at[k, p],
                         (cx, cy, c)) for k, (cx, cy) in enumerate(chips) for p, (w, rows) in enumerate(pieces)]

        def start():
            for cp in mine + first:
                cp.start()

        arrived = [g_refs[w].at[2 * cx + cy, c, rows] for cx, cy in chips for w, rows in pieces]
        passed = [_remote(slab, slab, send_sems.at[3 + q // npc, q % npc], recv_sems.at[3 + q // npc, q % npc], sibling)
                  for q, slab in enumerate(arrived)]

        def pass_on():
            for q, slab in enumerate(arrived):
                k, p = q // npc, q % npc
                _remote(slab, slab, send_sems.at[k, p], recv_sems.at[k, p], (*chips[k], c)).wait_recv()
                passed[q].start()

        def finish():
            for k, (cx, cy) in enumerate(chips):
                for p, (w, rows) in enumerate(pieces):
                    slab = g_refs[w].at[2 * cx + cy, 1 - c, rows]
                    _remote(slab, slab, send_sems.at[3 + k, p], recv_sems.at[3 + k, p], sibling).wait_recv()
            for cp in first + passed:
                cp.wait_send()
            for cp in mine:
                cp.wait()

        return start, pass_on, finish

    return _Exchange(list(shards), [jax.ShapeDtypeStruct((N_CHIPS,) + s.shape, BF16) for s in shards],
                     [pltpu.SemaphoreType.DMA((6, npc)), pltpu.SemaphoreType.DMA((6, npc)),
                      pltpu.SemaphoreType.DMA((nw,))], build)


def _swap_halves(grads, name):
    nw = len(grads)

    def body(*refs):
        g_refs, sib_refs = refs[:nw], refs[nw:2 * nw]
        send_sems, recv_sems = refs[2 * nw:]
        x, y, c, _ = _place()
        copies = [_remote(g_refs[w].at[s, 1 - c], sib_refs[w].at[s], send_sems.at[s, w], recv_sems.at[s, w],
                          (x, y, 1 - c)) for w in range(nw) for s in range(N_CHIPS)]
        for cp in copies:
            cp.start()
        for cp in copies:
            cp.wait_recv()
        for cp in copies:
            cp.wait_send()

    return pl.pallas_call(
        body, name=name, in_specs=[ANY] * nw, out_specs=[ANY] * nw,
        out_shape=[jax.ShapeDtypeStruct((N_CHIPS,) + g.shape[2:], BF16) for g in grads],
        scratch_shapes=[pltpu.SemaphoreType.DMA((N_CHIPS, nw)), pltpu.SemaphoreType.DMA((N_CHIPS, nw))],
    )(*grads)


def _pair_sum(grad, sib, core, name):
    nchip, _, hr, cols = grad.shape
    tr = _div_tile(hr, 256, 16)

    def body(core_ref, a_ref, b_ref, o_ref):
        o_ref[...] = (a_ref[0].astype(F32) + b_ref[...].astype(F32)).astype(BF16)

    return pl.pallas_call(
        body, name=name,
        grid_spec=pltpu.PrefetchScalarGridSpec(
            num_scalar_prefetch=1, grid=(nchip, hr // tr),
            in_specs=[pl.BlockSpec((1, 1, tr, cols), lambda s, i, core_r: (s, core_r[0], i, 0)),
                      pl.BlockSpec((1, tr, cols), lambda s, i, core_r: (s, i, 0))],
            out_specs=pl.BlockSpec((1, tr, cols), lambda s, i, core_r: (s, i, 0))),
        out_shape=jax.ShapeDtypeStruct((nchip, hr, cols), BF16), compiler_params=_cparams())(core, grad, sib)


def _pair_sum_exchange(sums):
    nw = len(sums)
    pieces = _pieces(sums, 1)
    npc = len(pieces)

    def build(p_refs, o_refs, sems):
        send_sems, recv_sems, local_sems = sems
        x, y, c, chips = _place()
        me = 2 * x + y
        sibling = (x, y, 1 - c)
        mine = [pltpu.make_async_copy(p_refs[w].at[me], o_refs[w].at[c, 3], local_sems.at[w]) for w in range(nw)]
        first = [_remote(p_refs[w].at[2 * cx + cy, rows], o_refs[w].at[c, k, rows], send_sems.at[k, p],
                         recv_sems.at[k, p], (cx, cy, c))
                 for k, (cx, cy) in enumerate(chips) for p, (w, rows) in enumerate(pieces)]

        def start():
            for cp in mine + first:
                cp.start()

        passed = [_remote(o_refs[w].at[c, k, rows], o_refs[w].at[c, k, rows], send_sems.at[3 + k, p],
                          recv_sems.at[3 + k, p], sibling) for k in range(N_CHIPS) for p, (w, rows) in enumerate(pieces)]

        def pass_on():
            for k in range(N_CHIPS):
                own_waited = set()
                for p, (w, rows) in enumerate(pieces):
                    if k < 3:
                        first[k * npc + p].wait_recv()
                    elif w not in own_waited:
                        mine[w].wait()
                        own_waited.add(w)
                    passed[k * npc + p].start()

        def finish():
            for k in range(N_CHIPS):
                for p, (w, rows) in enumerate(pieces):
                    slab = o_refs[w].at[1 - c, k, rows]
                    _remote(slab, slab, send_sems.at[3 + k, p], recv_sems.at[3 + k, p], sibling).wait_recv()
            for cp in first + passed:
                cp.wait_send()

        return start, pass_on, finish

    return _Exchange(list(sums), [jax.ShapeDtypeStruct((2,) + p.shape, BF16) for p in sums],
                     [pltpu.SemaphoreType.DMA((7, npc)), pltpu.SemaphoreType.DMA((7, npc)),
                      pltpu.SemaphoreType.DMA((nw,))], build)


def _small_sum_exchange(vec):
    m_per, n = vec.shape

    def build(ins, outs, scr):
        (x_ref,), (out_ref,) = ins, outs
        gath_ref, sum_ref, send_sems, recv_sems, local_sem, out_sem = scr
        x, y, c, chips = _place()
        me, sibling = (x, y, c), (x, y, 1 - c)

        def rows(px, py, pc):
            return gath_ref.at[pl.ds((4 * px + 2 * py + pc) * m_per, m_per), :]

        def copy(k, block, to, src=None):
            return pltpu.make_async_remote_copy(
                src_ref=rows(*block) if src is None else src, dst_ref=rows(*block),
                send_sem=send_sems.at[k], recv_sem=recv_sems.at[k], device_id=to, device_id_type=MESH)

        mine = pltpu.make_async_copy(x_ref, rows(*me), local_sem)
        first = [copy(0, me, sibling, src=x_ref)] + [copy(1 + j, me, (*chip, c), src=x_ref)
                                                     for j, chip in enumerate(chips)]

        def start():
            for cp in [mine] + first:
                cp.start()

        passed = [copy(4 + j, (*chip, c), sibling) for j, chip in enumerate(chips)]

        def pass_on():
            for j, chip in enumerate(chips):
                copy(1 + j, (*chip, c), me).wait_recv()
                passed[j].start()

        def finish():
            copy(0, sibling, me).wait_recv()
            for j, chip in enumerate(chips):
                copy(4 + j, (*chip, 1 - c), me).wait_recv()
            for cp in first + passed:
                cp.wait_send()
            mine.wait()
            acc = gath_ref[pl.ds(0, m_per), :]
            for k in range(1, N_DEV):
                acc = acc + gath_ref[pl.ds(k * m_per, m_per), :]
            sum_ref[...] = acc
            done = pltpu.make_async_copy(sum_ref, out_ref, out_sem)
            done.start()
            done.wait()

        return start, pass_on, finish

    return _Exchange([vec], [jax.ShapeDtypeStruct((m_per, n), F32)],
                     [pltpu.VMEM((N_DEV * m_per, n), F32), pltpu.VMEM((m_per, n), F32), pltpu.SemaphoreType.DMA((7,)),
                      pltpu.SemaphoreType.DMA((7,)), pltpu.SemaphoreType.DMA, pltpu.SemaphoreType.DMA], build)


def _pack_small(vals, tail=()):
    flat = jnp.concatenate([vals[name].reshape(-1).astype(F32) for name, _ in SMALL] + [v.reshape(1) for v in tail])
    flat = jnp.pad(flat, (0, SMALL_ROWS * LANES - flat.shape[0]))
    return flat.reshape(SMALL_ROWS, LANES)


def _unpack_small(packed):
    flat = packed.reshape(-1)
    out, off = {}, 0
    for name, shape in SMALL:
        n = int(np.prod(shape))
        out[name] = flat[off:off + n].reshape(shape)
        off += n
    return out


def _head_pad_cols(w, heads, real):
    k = w.shape[0]
    return jnp.pad(w.reshape(k, heads, real), ((0, 0), (0, 0), (0, LANES - real))).reshape(k, heads * LANES)


def _rope_tables(positions):
    half = MLA_ROPE // 2
    inv = ROPE_BASE ** (-jnp.arange(half, dtype=F32) / half)
    ang = positions.astype(F32)[:, None] * inv
    cos, sin = jnp.cos(ang), jnp.sin(ang)
    t = positions.shape[0]
    z = lambda n: jnp.zeros((t, n), F32)
    rc = jnp.concatenate([jnp.ones((t, MLA_NOPE), F32), cos, cos, z(LANES - MLA_QK)], axis=1)
    rs1 = jnp.concatenate([z(MLA_NOPE), -sin, z(LANES - MLA_NOPE - half)], axis=1)
    rs2 = jnp.concatenate([z(MLA_NOPE + half), sin, z(LANES - MLA_QK)], axis=1)
    return rc, rs1, rs2


FFN1_WEIGHTS = ("ffn1_w_gu", "ffn1_w_down")
FFN2_WEIGHTS = ("ffn2_w_gu", "ffn2_w_down")
MIXER_WEIGHTS = tuple(n for n, *_ in SHARDED if n not in FFN1_WEIGHTS + FFN2_WEIGHTS)
SHARD_SHAPE = {n: (r, c, kind) for n, r, c, kind in SHARDED}


def _from_blocks(name, gathered):
    r, c, kind = SHARD_SHAPE[name]
    blk = gathered.reshape(N_CHIPS, r, c)
    return blk, (blk.transpose(1, 0, 2).reshape(r, N_CHIPS * c) if kind == "col" else blk.reshape(N_CHIPS * r, c))


def _grad_pair_sums(names, gw, core, tag):
    by_owner = []
    for name in names:
        r, c, kind = SHARD_SHAPE[name]
        if gw[name].dtype == BF16:
            blk = gw[name]
        elif kind == "col":
            blk = gw[name].reshape(r, N_CHIPS, c).transpose(1, 0, 2)
        else:
            blk = gw[name].reshape(N_CHIPS, r, c)
        by_owner.append(blk.astype(BF16).reshape(N_CHIPS, 2, r // 2, c))
    received = _swap_halves(by_owner, "grad_swap_" + tag)
    return [_pair_sum(g, s, core, "pair_sum_" + n) for g, s, n in zip(by_owner, received, names)]


def _device_step(x, mem, positions, tgt, small, shards, core):
    d = D_MODEL
    g_ffn1, g_mix, g_ffn2 = small["ffn1_norm"], small["mix_norm"], small["ffn2_norm"]
    big = {}
    for name, g in zip(FFN1_WEIGHTS, _run_exchange(_gather_exchange([shards[n] for n in FFN1_WEIGHTS]), "gather_ffn1")):
        big[name + "#blocks"], big[name] = _from_blocks(name, g)
    wgu1, wd1 = big["ffn1_w_gu#blocks"], big["ffn1_w_down"].reshape(2, FF_TILE, d)
    x1, gpre1, upre1, h, *rest = _ffn_fwd(x, g_ffn1, wgu1, wd1, "ffn1_fwd", next_gain=g_mix,
                                          ex=_gather_exchange([shards[n] for n in MIXER_WEIGHTS]))
    for name, g in zip(MIXER_WEIGHTS, rest):
        big[name + "#blocks"], big[name] = _from_blocks(name, g)
    w_in = big["w_in"]
    w_uv_, w_cq, w_ckv = w_in[:, :COL_CQ], w_in[:, COL_CQ:COL_CKV], w_in[:, COL_CKV:COL_KR]
    w_kr = jnp.pad(w_in[:, COL_KR:COL_QM], ((0, 0), (MLA_NOPE, LANES - MLA_QK)))
    w_qm, w_g = w_in[:, COL_QM:COL_GATE], w_in[:, COL_GATE:]
    segs = (w_uv_, w_cq, w_ckv, w_kr, w_qm, w_g)
    wuq = _head_pad_cols(big["mla_w_uq"], MLA_HEADS, MLA_QK)
    ukv = big["mla_w_ukv"].reshape(MLA_KV_RANK, MLA_HEADS, 2, MLA_NOPE)
    wuk = _head_pad_cols(ukv[:, :, 0].reshape(MLA_KV_RANK, -1), MLA_HEADS, MLA_NOPE)
    wuv = _head_pad_cols(ukv[:, :, 1].reshape(MLA_KV_RANK, -1), MLA_HEADS, MLA_NOPE)
    wkv = big["mem_w_kv"]
    wa, wc, wo = big["w_branch_a"], big["w_branch_c"], big["w_out"]
    wb = jnp.pad(big["w_branch_b"].reshape(MLA_HEADS, MLA_NOPE, d),
                 ((0, 0), (0, LANES - MLA_NOPE), (0, 0))).reshape(MLA_HEADS * LANES, d)
    qg = jnp.pad(small["mla_q_norm"], ((0, 0), (0, LANES - MLA_QK)))
    kg = jnp.pad(small["mla_k_norm"], ((0, 0), (0, LANES - MLA_QK)))
    causal = jnp.tril(jnp.ones((CHUNK, CHUNK), bool))
    wt_f = jnp.where(causal[None], small["sg_w"][0], 0.0)
    wt, wt_t = wt_f.astype(BF16), wt_f.transpose(0, 2, 1).astype(BF16)
    bias_l = jnp.repeat(small["sg_b"][0].T, 64, axis=1)
    rc, rs1, rs2 = _rope_tables(positions)

    zuv, zcq, zckv, zkr, zqm, zg = _mm_cols(h, segs, [F32] * 5 + [BF16], "in_proj")
    ya = _sgu_fwd(zuv, small["sg_ln_g"], small["sg_ln_b"], wt, bias_l, "sgu_fwd")
    q, k, v, cqn, ckvn = _mla_prep_fwd(zcq, zckv, zkr, small["mla_cq_norm"], small["mla_ckv_norm"], qg, kg,
                                       wuq, wuk, wuv, rc, rs1, rs2, "mla_prep_fwd")
    yb, lse, *rest = _attn_fwd(q, k, v, "mla_attn_fwd", ex=_gather_exchange([shards[n] for n in FFN2_WEIGHTS]))
    for name, g in zip(FFN2_WEIGHTS, rest):
        big[name + "#blocks"], big[name] = _from_blocks(name, g)
    wgu2, wd2 = big["ffn2_w_gu#blocks"], big["ffn2_w_down"].reshape(2, FF_TILE, d)
    km, vm, memn = _mem_kv_fwd(mem, small["mem_norm"], wkv, small["mem_k_norm"], "mem_kv_fwd")
    yc = _mem_attn_fwd(zqm, small["mem_q_norm"], km, vm, "mem_attn_fwd")
    x2, merged, pa, pb, pc = _merge_fwd(x1, ya, yb, yc, zg, small["b_gate"], wa, wb, wc, wo, "merge_fwd")
    dy, loss_row, gpre2, upre2 = _ffn_fwd(x2, g_ffn2, wgu2, wd2, "ffn2_fwd", target=tgt)

    gw, gs, slots = {}, {}, {}

    def ffn_grads(prefix, xin, gain, dyin, gpre, upre, wgu, wd, ex=None, ex_names=(), last=False):
        dx, dgain, xn, dgt, dup, act, *got = _ffn_bwd(xin, gain, dyin, gpre, upre, wgu, wd, prefix + "_bwd", ex=ex)
        slots.update(zip(ex_names, got))
        gs[prefix + "_norm"] = dgain
        gw[prefix + "_w_gu"] = jnp.concatenate(
            [_mm_tn(xn, dgt, prefix + "_dwg", col_blocks=True, out_dtype=BF16),
             _mm_tn(xn, dup, prefix + "_dwu", col_blocks=True, out_dtype=BF16)], axis=0)
        rows_down = SHARD_SHAPE[prefix + "_w_down"][0]
        if last:
            small_sum = _small_sum_exchange(_pack_small(gs, tail=[loss_row[0, 0]]))
            dwd, summed = _mm_tn(act, dyin, prefix + "_dwd", scale=0.5, ex=small_sum, out_dtype=BF16)
            gw[prefix + "_w_down"] = dwd.reshape(N_CHIPS, rows_down, d)
            return dx, summed
        gw[prefix + "_w_down"] = _mm_tn(act, dyin, prefix + "_dwd", scale=0.5, out_dtype=BF16).reshape(
            N_CHIPS, rows_down, d)
        return dx

    dx2 = ffn_grads("ffn2", x2, g_ffn2, dy, gpre2, upre2, wgu2, wd2)
    ffn2_sums = _pair_sum_exchange(_grad_pair_sums(FFN2_WEIGHTS, gw, core, "ffn2"))
    dpa, dpb, dpc, dzg, dbg, dya, dyb, dyc = _merge_bwd(dx2, pa, pb, pc, zg, small["b_gate"], wa, wb, wc, wo,
                                                        "merge_bwd")
    gs["b_gate"] = dbg
    gw["w_out"] = _mm_tn(merged, dx2, "dw_out")
    gw["w_branch_a"] = _mm_tn(ya, dpa, "dw_branch_a")
    gw["w_branch_b"] = _mm_tn(yb, dpb, "dw_branch_b").reshape(MLA_HEADS, LANES, d)[:, :MLA_NOPE].reshape(-1, d)
    gw["w_branch_c"] = _mm_tn(yc, dpc, "dw_branch_c")

    dzuv, dwt, dbl, dlg, dlb = _sgu_bwd(zuv, dya, small["sg_ln_g"], small["sg_ln_b"], wt, wt_t, bias_l, "sgu_bwd")
    gs["sg_w"], gs["sg_b"] = dwt[None], dbl[:, :SG_GROUPS].T[None]
    gs["sg_ln_g"], gs["sg_ln_b"] = dlg, dlb

    delta_rows, lse_rows = _attn_bwd_rows(yb, lse, dyb, "mla_attn_bwd_rows")
    dq, dk, dv, *got = _attn_bwd(q, k, v, delta_rows, lse_rows, dyb, "mla_attn_bwd", ex=ffn2_sums)
    slots.update(zip(FFN2_WEIGHTS, got))
    dzcq, dzckv, dzkr, dql, dkl, dgcq, dgckv, dqg, dkg = _mla_prep_bwd(
        zcq, zckv, zkr, small["mla_cq_norm"], small["mla_ckv_norm"], qg, kg, wuq, wuk, wuv, rc, rs1, rs2,
        dq, dk, dv, "mla_prep_bwd")
    gs["mla_cq_norm"], gs["mla_ckv_norm"] = dgcq, dgckv
    gs["mla_q_norm"], gs["mla_k_norm"] = dqg[:, :MLA_QK], dkg[:, :MLA_QK]
    gw["mla_w_uq"] = _mm_tn(cqn, dql, "dw_uq").reshape(MLA_Q_RANK, MLA_HEADS, LANES)[:, :, :MLA_QK].reshape(
        MLA_Q_RANK, -1)
    dwuk = _mm_tn(ckvn, dkl, "dw_uk").reshape(MLA_KV_RANK, MLA_HEADS, LANES)[:, :, :MLA_NOPE]
    dwuv = _mm_tn(ckvn, dv, "dw_uv").reshape(MLA_KV_RANK, MLA_HEADS, LANES)[:, :, :MLA_NOPE]
    gw["mla_w_ukv"] = jnp.concatenate([dwuk, dwuv], axis=2).reshape(MLA_KV_RANK, -1)

    dzqm, dkn, dvm, dmqg = _mem_attn_bwd(zqm, dyc, small["mem_q_norm"], km, vm, "mem_attn_bwd")
    gs["mem_q_norm"] = dmqg
    gw["mem_w_kv"], gs["mem_k_norm"], gs["mem_norm"] = _mem_kv_bwd(
        mem, small["mem_norm"], wkv, small["mem_k_norm"], dkn, dvm, "mem_kv_bwd")

    dzs = (dzuv, dzcq, dzckv, dzkr, dzqm, dzg)
    dws = list(_mm_tn_cols(h, dzs[:5], "dw_in_narrow")) + [_mm_tn(h, dzg, "dw_in_gate")]
    dws[3] = dws[3][:, MLA_NOPE:MLA_QK]
    gw["w_in"] = jnp.concatenate(dws, axis=1)
    dx1, gs["mix_norm"] = _proj_norm_bwd(dzs, [w.T for w in segs], x1, g_mix, dx2, "in_proj_bwd")
    mixer_sums = _pair_sum_exchange(_grad_pair_sums(MIXER_WEIGHTS, gw, core, "mixer"))
    dx, summed = ffn_grads("ffn1", x, g_ffn1, dx1, gpre1, upre1, wgu1, wd1, ex=mixer_sums, ex_names=MIXER_WEIGHTS,
                           last=True)
    ffn1_sums = _pair_sum_exchange(_grad_pair_sums(FFN1_WEIGHTS, gw, core, "ffn1"))
    slots.update(zip(FFN1_WEIGHTS, _run_exchange(ffn1_sums, "grad_exchange_ffn1")))
    return dx, slots, summed


def kernel(x, mem, positions, ffn1_norm, ffn1_w_gu, ffn1_w_down, mix_norm, w_in, b_gate, sg_ln_g, sg_ln_b, sg_w, sg_b, mla_cq_norm, mla_w_uq, mla_ckv_norm, mla_w_ukv, mla_q_norm, mla_k_norm, mem_norm, mem_w_kv, mem_q_norm, mem_k_norm, w_branch_a, w_branch_b, w_branch_c, w_out, ffn2_norm, ffn2_w_gu, ffn2_w_down, loss_target, m_ffn1_norm, m_ffn1_w_gu, m_ffn1_w_down, m_mix_norm, m_w_in, m_b_gate, m_sg_ln_g, m_sg_ln_b, m_sg_w, m_sg_b, m_mla_cq_norm, m_mla_w_uq, m_mla_ckv_norm, m_mla_w_ukv, m_mla_q_norm, m_mla_k_norm, m_mem_norm, m_mem_w_kv, m_mem_q_norm, m_mem_k_norm, m_w_branch_a, m_w_branch_b, m_w_branch_c, m_w_out, m_ffn2_norm, m_ffn2_w_gu, m_ffn2_w_down, v_ffn1_norm, v_ffn1_w_gu, v_ffn1_w_down, v_mix_norm, v_w_in, v_b_gate, v_sg_ln_g, v_sg_ln_b, v_sg_w, v_sg_b, v_mla_cq_norm, v_mla_w_uq, v_mla_ckv_norm, v_mla_w_ukv, v_mla_q_norm, v_mla_k_norm, v_mem_norm, v_mem_w_kv, v_mem_q_norm, v_mem_k_norm, v_w_branch_a, v_w_branch_b, v_w_branch_c, v_w_out, v_ffn2_norm, v_ffn2_w_gu, v_ffn2_w_down):
    args = dict(locals())
    weights = {n: args[n] for n in WEIGHT_ORDER}
    mom_m = {n: args["m_" + n] for n in WEIGHT_ORDER}
    mom_v = {n: args["v_" + n] for n in WEIGHT_ORDER}
    small = {n: weights[n] for n, _ in SMALL}
    halves = lambda a, r, c: a.reshape(2, r // 2, c)

    shards = {n: halves(weights[n][0].astype(BF16), r, c) for n, r, c, _ in SHARDED}
    core = lax.axis_index("c").astype(jnp.int32).reshape(1)
    dx, slots, summed = _device_step(x[0], mem[0], positions[0], loss_target[0], small, shards, core)
    loss = summed.reshape(-1)[_N_SMALL]
    small_grads = _unpack_small(summed)

    grads, deltas, new_m, new_v = {}, {}, {}, {}
    for name, r, c, _ in SHARDED:
        outs = _adamw_slots(halves(weights[name][0], r, c), slots[name], halves(mom_m[name][0], r, c),
                            halves(mom_v[name][0], r, c), "adamw_" + name)
        shape = weights[name].shape
        grads[name], deltas[name], new_m[name], new_v[name] = [o.reshape(shape) for o in outs]
    dlt, nm, nv = _adamw(_pack_small(small), _pack_small(small_grads), _pack_small({n: mom_m[n] for n, _ in SMALL}),
                         _pack_small({n: mom_v[n] for n, _ in SMALL}), "adamw_small")
    for name, _ in SMALL:
        grads[name] = small_grads[name]
    deltas.update(_unpack_small(dlt))
    new_m.update(_unpack_small(nm))
    new_v.update(_unpack_small(nv))

    return (loss, dx[None], *[grads[n] for n in WEIGHT_ORDER], *[deltas[n] for n in WEIGHT_ORDER],
            *[new_m[n] for n in WEIGHT_ORDER], *[new_v[n] for n in WEIGHT_ORDER])
```

```python
import functools
from typing import Callable, NamedTuple

import numpy as np
import jax
import jax.numpy as jnp
from jax import lax
from jax.experimental import pallas as pl
from jax.experimental.pallas import tpu as pltpu

F32 = jnp.float32
BF16 = jnp.bfloat16

D_MODEL = 1024
D_FF = 2816
FF_TILE = 1408
SG_WIDTH = 512
SG_GROUPS = 8
CHUNK = 128
MLA_HEADS = 8
MLA_QK = 96
MLA_NOPE = 64
MLA_ROPE = 32
MLA_Q_RANK = 384
MLA_KV_RANK = 256
MEM_HEADS = 4
LANES = 128
EPS = 1e-6
NEG = -1e30
ROPE_BASE = 10000.0
N_CHIPS = 4
N_DEV = 8

ADAM_LR = 0.001
ADAM_B1 = 0.9
ADAM_B2 = 0.999
ADAM_EPS = 1e-08
ADAM_WD = 0.01
ADAM_STEP = 10

COL_CQ = 1024
COL_CKV = 1408
COL_KR = 1664
COL_QM = 1696
COL_GATE = 2208

VMEM_LIMIT_BYTES = 56 * 1024 * 1024
INV_SQRT2 = 0.7071067811865476
INV_SQRT_2PI = 0.3989422804014327
LOG2E = 1.4426950408889634
ATTN_SCALE = MLA_QK ** -0.5
V_ONES_LANE = 64
ATTN_SCALE2 = ATTN_SCALE * LOG2E

SHARDED = (
    ("ffn1_w_gu", 1024, 1408, "col"),
    ("ffn1_w_down", 704, 1024, "row"),
    ("w_in", 1024, 1320, "col"),
    ("mla_w_uq", 384, 192, "col"),
    ("mla_w_ukv", 256, 256, "col"),
    ("mem_w_kv", 256, 1024, "row"),
    ("w_branch_a", 512, 256, "col"),
    ("w_branch_b", 512, 256, "col"),
    ("w_branch_c", 512, 256, "col"),
    ("w_out", 256, 1024, "row"),
    ("ffn2_w_gu", 1024, 1408, "col"),
    ("ffn2_w_down", 704, 1024, "row"),
)
SMALL = (
    ("ffn1_norm", (1, 1024)), ("mix_norm", (1, 1024)), ("b_gate", (1, 3072)),
    ("sg_ln_g", (1, 512)), ("sg_ln_b", (1, 512)), ("sg_w", (1, 8, 128, 128)),
    ("sg_b", (1, 8, 128)), ("mla_cq_norm", (1, 384)), ("mla_ckv_norm", (1, 256)),
    ("mla_q_norm", (1, 96)), ("mla_k_norm", (1, 96)), ("mem_norm", (1, 1024)),
    ("mem_q_norm", (1, 128)), ("mem_k_norm", (1, 128)), ("ffn2_norm", (1, 1024)),
)
WEIGHT_ORDER = (
    "ffn1_norm", "ffn1_w_gu", "ffn1_w_down", "mix_norm", "w_in", "b_gate", "sg_ln_g", "sg_ln_b",
    "sg_w", "sg_b", "mla_cq_norm", "mla_w_uq", "mla_ckv_norm", "mla_w_ukv", "mla_q_norm",
    "mla_k_norm", "mem_norm", "mem_w_kv", "mem_q_norm", "mem_k_norm", "w_branch_a", "w_branch_b",
    "w_branch_c", "w_out", "ffn2_norm", "ffn2_w_gu", "ffn2_w_down",
)

_N_SMALL = sum(int(np.prod(s)) for _, s in SMALL)
SMALL_ROWS = -(-_N_SMALL // (LANES * 8)) * 8

MESH = pl.DeviceIdType.MESH


def _cparams():
    return pltpu.CompilerParams(vmem_limit_bytes=VMEM_LIMIT_BYTES)


def _dot(a, b):
    return jnp.dot(a, b, preferred_element_type=F32)


def _dot_nt(a, b):
    return lax.dot_general(a, b, (((1,), (1,)), ((), ())), preferred_element_type=F32)


def _dot_tn(a, b):
    return lax.dot_general(a, b, (((0,), (0,)), ((), ())), preferred_element_type=F32)


def _gelu(x):
    return 0.5 * x * (1.0 + lax.erf(x * INV_SQRT2))


def _gelu_grad(x):
    return 0.5 * (1.0 + lax.erf(x * INV_SQRT2)) + x * jnp.exp(-0.5 * x * x) * INV_SQRT_2PI


def _rstd(x, n):
    return lax.rsqrt(jnp.sum(x * x, axis=-1, keepdims=True) * (1.0 / n) + EPS)


def _rms_vjp(x, r, g, dy, n):
    dxh = dy * g
    dx = r * dxh - x * (r * r * r) * (jnp.sum(dxh * x, axis=-1, keepdims=True) * (1.0 / n))
    return dx, dy * x * r


def _row_tile(t, want):
    return min(t, want)


def _wide_tile(n):
    if n <= 1024:
        return n
    if n % 1024 == 0:
        return 1024
    assert n % FF_TILE == 0, n
    return FF_TILE


def _mm_cols(a, ws, out_dtypes, name, ex=None):
    t, kdim = a.shape
    tm = _row_tile(t, 512)
    n = len(ws)

    def body(*refs):
        av = refs[0][...]
        for w_ref, o_ref in zip(refs[1:1 + n], refs[1 + n:]):
            o_ref[...] = _dot(av, w_ref[...]).astype(o_ref.dtype)

    row = lambda width: pl.BlockSpec((tm, width), lambda i: (i, 0))
    return _call_with_exchange(
        ex, body, name, (t // tm,),
        [row(kdim)] + [pl.BlockSpec(w.shape, lambda i: (0, 0)) for w in ws],
        [row(w.shape[1]) for w in ws],
        [jax.ShapeDtypeStruct((t, w.shape[1]), dt) for w, dt in zip(ws, out_dtypes)], [], (a, *ws))


def _proj_norm_bwd(dzs, wts, x, g, dres, name):
    t, d = x.shape
    tm = _row_tile(t, 256)
    n = len(dzs)

    def body(*refs):
        x_ref, g_ref, r_ref, dx_ref, dg_ref = refs[2 * n:]

        @pl.when(pl.program_id(0) == 0)
        def _():
            dg_ref[...] = jnp.zeros_like(dg_ref)

        dh = None
        for dz_ref, w_ref in zip(refs[:n], refs[n:2 * n]):
            part = _dot(dz_ref[...], w_ref[...])
            dh = part if dh is None else dh + part
        xv = x_ref[...]
        dx, dgr = _rms_vjp(xv, _rstd(xv, d), g_ref[...], dh, d)
        dx_ref[...] = r_ref[...] + dx
        dg_ref[...] += jnp.sum(dgr, axis=0, keepdims=True)

    row = lambda width: pl.BlockSpec((tm, width), lambda i: (i, 0))
    vec = pl.BlockSpec((1, d), lambda i: (0, 0))
    return pl.pallas_call(
        body, name=name, grid=(t // tm,),
        in_specs=[row(dz.shape[1]) for dz in dzs] + [pl.BlockSpec(w.shape, lambda i: (0, 0)) for w in wts]
        + [row(d), vec, row(d)],
        out_specs=[row(d), vec],
        out_shape=[jax.ShapeDtypeStruct((t, d), F32), jax.ShapeDtypeStruct((1, d), F32)],
        compiler_params=_cparams())(*dzs, *wts, x, g, dres)


def _mm_tn_cols(a, bs, name):
    t, m = a.shape
    tk = _row_tile(t, 1024)
    n = len(bs)

    def body(*refs):
        @pl.when(pl.program_id(0) == 0)
        def _():
            for o_ref in refs[1 + n:]:
                o_ref[...] = jnp.zeros_like(o_ref)

        av = refs[0][...].astype(BF16)
        for b_ref, o_ref in zip(refs[1:1 + n], refs[1 + n:]):
            o_ref[...] += _dot_tn(av, b_ref[...].astype(BF16))

    row = lambda width: pl.BlockSpec((tk, width), lambda k: (k, 0))
    return pl.pallas_call(
        body, name=name, grid=(t // tk,), in_specs=[row(m)] + [row(b.shape[1]) for b in bs],
        out_specs=[pl.BlockSpec((m, b.shape[1]), lambda k: (0, 0)) for b in bs],
        out_shape=[jax.ShapeDtypeStruct((m, b.shape[1]), F32) for b in bs],
        compiler_params=_cparams())(a, *bs)


def _mm_tn(a, b, name, scale=1.0, ex=None, col_blocks=False, out_dtype=F32):
    t, m = a.shape
    n = b.shape[1]
    tm, tn = _wide_tile(m), _wide_tile(n)
    tk = _row_tile(t, 2048)
    nk = t // tk
    in_place = out_dtype == F32

    def body(a_ref, b_ref, o_ref, *scr):
        k = pl.program_id(2)
        acc_ref = o_ref if in_place else scr[0]

        @pl.when(k == 0)
        def _():
            acc_ref[...] = jnp.zeros_like(acc_ref)

        prod = _dot_tn(a_ref[...].astype(BF16), b_ref[...].astype(BF16))
        acc_ref[...] += prod.reshape(acc_ref.shape)
        if scale != 1.0 or not in_place:
            @pl.when(k == nk - 1)
            def _():
                o_ref[...] = (acc_ref[...] * scale).astype(out_dtype).reshape(o_ref.shape)

    if col_blocks:
        out_spec = pl.BlockSpec((1, tm, tn), lambda i, j, k: (j, i, 0))
        out_shape = jax.ShapeDtypeStruct((n // tn, m, tn), out_dtype)
    else:
        out_spec = pl.BlockSpec((tm, tn), lambda i, j, k: (i, j))
        out_shape = jax.ShapeDtypeStruct((m, n), out_dtype)
    outs = _call_with_exchange(
        ex, body, name, (m // tm, n // tn, nk),
        [pl.BlockSpec((tk, tm), lambda i, j, k: (k, i)), pl.BlockSpec((tk, tn), lambda i, j, k: (k, j))],
        [out_spec], [out_shape], [] if in_place else [pltpu.VMEM((tm, tn), F32)], (a, b))
    return outs[0] if ex is None else outs


PASS_ON_STEPS_BEFORE_END = 8


class _Exchange(NamedTuple):
    operands: list
    out_shapes: list
    sem_shapes: list
    build: Callable


def _call_with_exchange(ex, body, name, grid, in_specs, out_specs, out_shape, scratch_shapes, operands, prefetch=()):
    n_pre = len(prefetch)
    total = int(np.prod(grid))
    pass_step = max(total // 2, total - PASS_ON_STEPS_BEFORE_END)

    def call(kernel, ins, outs, shapes, scratch):
        if n_pre:
            spec = pltpu.PrefetchScalarGridSpec(num_scalar_prefetch=n_pre, grid=grid, in_specs=ins, out_specs=outs,
                                                scratch_shapes=scratch)
            return pl.pallas_call(kernel, name=name, grid_spec=spec, out_shape=shapes, compiler_params=_cparams())
        return pl.pallas_call(kernel, name=name, grid=grid, in_specs=ins, out_specs=outs, out_shape=shapes,
                              scratch_shapes=scratch, compiler_params=_cparams())

    if ex is None:
        return call(body, in_specs, out_specs, out_shape, scratch_shapes)(*prefetch, *operands)
    n_in, n_out, n_scr = len(in_specs), len(out_specs), len(scratch_shapes)
    k_in, k_out = len(ex.operands), len(ex.out_shapes)

    def carried(*refs):
        pre, refs = refs[:n_pre], refs[n_pre:]
        a, b = n_in, n_in + k_in
        c, e = b + n_out, b + n_out + k_out
        f = e + n_scr
        start, pass_on, finish = ex.build(refs[a:b], refs[c:e], refs[f:])
        step = functools.reduce(lambda lin, ax: lin * grid[ax] + pl.program_id(ax), range(len(grid)), 0)
        pl.when(step == 0)(start)
        body(*pre, *refs[:a], *refs[b:c], *refs[e:f])
        pl.when(step == pass_step)(pass_on)
        pl.when(step == total - 1)(finish)

    return call(carried, list(in_specs) + [ANY] * k_in, list(out_specs) + [ANY] * k_out,
                list(out_shape) + list(ex.out_shapes), list(scratch_shapes) + list(ex.sem_shapes),
                )(*prefetch, *operands, *ex.operands)


def _run_exchange(ex, name):
    k_in, k_out = len(ex.operands), len(ex.out_shapes)

    def body(*refs):
        start, pass_on, finish = ex.build(refs[:k_in], refs[k_in:k_in + k_out], refs[k_in + k_out:])
        start()
        pass_on()
        finish()

    return pl.pallas_call(body, name=name, in_specs=[ANY] * k_in, out_specs=[ANY] * k_out,
                          out_shape=list(ex.out_shapes), scratch_shapes=list(ex.sem_shapes))(*ex.operands)


def _ffn_fwd(x, g, wgu4, wd2, name, ex=None, next_gain=None, target=None):
    t, d = x.shape
    tm = _row_tile(t, 512)
    assert next_gain is None or target is None
    extra = [a for a in (next_gain, target) if a is not None]

    def body(*refs):
        x_ref, g_ref, wg_ref, wu_ref, wd_ref = refs[:5]
        e_ref = refs[5] if extra else None
        outs, (xn_scr, acc_scr) = refs[5 + len(extra):-2], refs[-2:]
        if target is not None:
            dy_ref, loss_ref, gg_ref, uu_ref = outs
        elif next_gain is not None:
            o_ref, gg_ref, uu_ref, h_ref = outs
        else:
            o_ref, gg_ref, uu_ref = outs
        i, j = pl.program_id(0), pl.program_id(1)

        @pl.when(j == 0)
        def _():
            xv = x_ref[...]
            xn_scr[...] = (xv * _rstd(xv, d) * g_ref[...]).astype(BF16)
            acc_scr[...] = jnp.zeros_like(acc_scr)

        if target is not None:
            @pl.when((i == 0) & (j == 0))
            def _():
                loss_ref[...] = jnp.zeros_like(loss_ref)

        xn = xn_scr[...]
        gg = _dot(xn, wg_ref[0])
        uu = _dot(xn, wu_ref[0])
        gg_ref[...] = gg.astype(BF16)
        uu_ref[...] = uu.astype(BF16)
        act = gg * jax.nn.sigmoid(gg) * uu
        acc_scr[...] += _dot(act.astype(BF16), wd_ref[0])

        @pl.when(j == 1)
        def _():
            y = x_ref[...] + 0.5 * acc_scr[...]
            if target is not None:
                e = y - e_ref[...]
                dy_ref[...] = e * (1.0 / d)
                part = 0.5 * jnp.sum(jnp.sum(e * e, axis=-1, keepdims=True) * (1.0 / d), axis=0, keepdims=True)
                loss_ref[...] += jnp.broadcast_to(part, loss_ref.shape)
            else:
                o_ref[...] = y
                if next_gain is not None:
                    h_ref[...] = (y * _rstd(y, d) * e_ref[...]).astype(BF16)

    row = pl.BlockSpec((tm, d), lambda i, j: (i, 0))
    vec = pl.BlockSpec((1, d), lambda i, j: (0, 0))
    ffb = pl.BlockSpec((tm, FF_TILE), lambda i, j: (i, j))
    f32_rows, bf16_ff = jax.ShapeDtypeStruct((t, d), F32), jax.ShapeDtypeStruct((t, D_FF), BF16)
    if target is not None:
        extra_spec, out_specs = [row], [row, pl.BlockSpec((1, LANES), lambda i, j: (0, 0)), ffb, ffb]
        out_shape = [f32_rows, jax.ShapeDtypeStruct((1, LANES), F32), bf16_ff, bf16_ff]
    elif next_gain is not None:
        extra_spec, out_specs = [vec], [row, ffb, ffb, row]
        out_shape = [f32_rows, bf16_ff, bf16_ff, jax.ShapeDtypeStruct((t, d), BF16)]
    else:
        extra_spec, out_specs, out_shape = [], [row, ffb, ffb], [f32_rows, bf16_ff, bf16_ff]
    return _call_with_exchange(
        ex, body, name, (t // tm, 2),
        [row, vec,
         pl.BlockSpec((1, d, FF_TILE), lambda i, j: (j, 0, 0)),
         pl.BlockSpec((1, d, FF_TILE), lambda i, j: (j + 2, 0, 0)),
         pl.BlockSpec((1, FF_TILE, d), lambda i, j: (j, 0, 0))] + extra_spec,
        out_specs, out_shape,
        [pltpu.VMEM((tm, d), BF16), pltpu.VMEM((tm, d), F32)], (x, g, wgu4, wgu4, wd2, *extra))


def _ffn_bwd(x, g, dy, gpre, upre, wgu4, wd2, name, ex=None):
    t, d = x.shape
    tm = _row_tile(t, 512)

    def body(dy_ref, gg_ref, uu_ref, wgu_hbm, wd_hbm, dg_ref, du_ref, act_ref, part_ref, wg_ref, wu_ref, wd_ref):
        j = pl.program_id(0)

        @pl.when(pl.program_id(1) == 0)
        def _():
            pltpu.sync_copy(wgu_hbm.at[j], wg_ref.at[0])
            pltpu.sync_copy(wgu_hbm.at[j + 2], wu_ref.at[0])
            pltpu.sync_copy(wd_hbm.at[j], wd_ref.at[0])

        gg = gg_ref[...].astype(F32)
        uu = uu_ref[...].astype(F32)
        sg = jax.nn.sigmoid(gg)
        silu = gg * sg
        act_ref[...] = (silu * uu).astype(BF16)
        dyh = (0.5 * dy_ref[...]).astype(BF16)
        dact = _dot_nt(dyh, wd_ref[0])
        du = (dact * silu).astype(BF16)
        dgt = (dact * uu * (sg * (1.0 + gg * (1.0 - sg)))).astype(BF16)
        du_ref[...] = du
        dg_ref[...] = dgt
        part_ref[0] = (_dot_nt(dgt, wg_ref[0]) + _dot_nt(du, wu_ref[0])).astype(BF16)

    row = pl.BlockSpec((tm, d), lambda j, i: (i, 0))
    ffb = pl.BlockSpec((tm, FF_TILE), lambda j, i: (i, j))
    dgt, dup, act, parts, *got = _call_with_exchange(
        ex, body, name, (2, t // tm),
        [row, ffb, ffb, ANY, ANY],
        [ffb, ffb, ffb, pl.BlockSpec((1, tm, d), lambda j, i: (j, i, 0))],
        [jax.ShapeDtypeStruct((t, D_FF), BF16)] * 3 + [jax.ShapeDtypeStruct((2, t, d), BF16)],
        [pltpu.VMEM((1, d, FF_TILE), BF16), pltpu.VMEM((1, d, FF_TILE), BF16), pltpu.VMEM((1, FF_TILE, d), BF16)],
        (dy, gpre, upre, wgu4, wd2))

    def norm_body(x_ref, g_ref, p_ref, dy_ref, dx_ref, dgain_ref, xn_ref):
        @pl.when(pl.program_id(0) == 0)
        def _():
            dgain_ref[...] = jnp.zeros_like(dgain_ref)

        xv = x_ref[...]
        r = _rstd(xv, d)
        xn_ref[...] = (xv * r * g_ref[...]).astype(BF16)
        dx, dgr = _rms_vjp(xv, r, g_ref[...], p_ref[0].astype(F32) + p_ref[1].astype(F32), d)
        dx_ref[...] = dy_ref[...] + dx
        dgain_ref[...] += jnp.sum(dgr, axis=0, keepdims=True)

    tn = _row_tile(t, 256)
    nrow = pl.BlockSpec((tn, d), lambda i: (i, 0))
    vec = pl.BlockSpec((1, d), lambda i: (0, 0))
    dx, dgain, xn = pl.pallas_call(
        norm_body, name=name + "_norm", grid=(t // tn,),
        in_specs=[nrow, vec, pl.BlockSpec((2, tn, d), lambda i: (0, i, 0)), nrow],
        out_specs=[nrow, vec, nrow],
        out_shape=[jax.ShapeDtypeStruct((t, d), F32), jax.ShapeDtypeStruct((1, d), F32),
                   jax.ShapeDtypeStruct((t, d), BF16)],
        compiler_params=_cparams())(x, g, parts, dy)
    return [dx, dgain, xn, dgt, dup, act] + got


def _sgu_layernorm(vpre, lg, lb):
    v = _gelu(vpre)
    mu = jnp.mean(v, axis=-1, keepdims=True)
    xc = v - mu
    rstd = lax.rsqrt(jnp.mean(xc * xc, axis=-1, keepdims=True) + EPS)
    xhat = xc * rstd
    return xhat, rstd, xhat * lg + lb


def _sgu_fwd(zuv, lg, lb, wt, bias_l, name):
    t = zuv.shape[0]
    tm = _row_tile(t, 512)

    def body(u_ref, v_ref, lg_ref, lb_ref, wt_ref, bl_ref, o_ref, vln_scr):
        _, _, vln = _sgu_layernorm(v_ref[...], lg_ref[...], lb_ref[...])
        vln_scr[...] = vln.astype(BF16)
        lo = lax.broadcasted_iota(jnp.int32, (CHUNK, LANES), 1) < 64
        for c in range(tm // CHUNK):
            rows = slice(c * CHUNK, (c + 1) * CHUNK)
            for p in range(SG_GROUPS // 2):
                cols = slice(p * LANES, (p + 1) * LANES)
                vp = vln_scr[rows, cols]
                mixed = jnp.where(lo, _dot(wt_ref[2 * p], vp), _dot(wt_ref[2 * p + 1], vp)) + bl_ref[:, cols]
                o_ref[rows, cols] = (_gelu(u_ref[rows, cols]) * mixed).astype(BF16)

    half = lambda k: pl.BlockSpec((tm, SG_WIDTH), lambda i: (i, k))
    vec = pl.BlockSpec((1, SG_WIDTH), lambda i: (0, 0))
    return pl.pallas_call(
        body, name=name, grid=(t // tm,),
        in_specs=[half(0), half(1), vec, vec,
                  pl.BlockSpec((SG_GROUPS, CHUNK, CHUNK), lambda i: (0, 0, 0)),
                  pl.BlockSpec((CHUNK, SG_WIDTH), lambda i: (0, 0))],
        out_specs=pl.BlockSpec((tm, SG_WIDTH), lambda i: (i, 0)),
        out_shape=jax.ShapeDtypeStruct((t, SG_WIDTH), BF16),
        scratch_shapes=[pltpu.VMEM((tm, SG_WIDTH), BF16)],
        compiler_params=_cparams())(zuv, zuv, lg, lb, wt, bias_l)


def _sgu_bwd(zuv, dya, lg, lb, wt, wt_t, bias_l, name):
    t = zuv.shape[0]
    tm = _row_tile(t, 256)
    nsteps = t // tm

    def body(u_ref, v_ref, dy_ref, lg_ref, lb_ref, wt_ref, wtt_ref, bl_ref,
             dz_ref, dwt_ref, dbl_ref, dlg_ref, dlb_ref, vln_scr, dvln_scr, dbacc_scr):
        step = pl.program_id(0)

        @pl.when(step == 0)
        def _():
            dwt_ref[...] = jnp.zeros_like(dwt_ref)
            dlg_ref[...] = jnp.zeros_like(dlg_ref)
            dlb_ref[...] = jnp.zeros_like(dlb_ref)
            dbl_ref[...] = jnp.zeros_like(dbl_ref)
            dbacc_scr[...] = jnp.zeros_like(dbacc_scr)

        vpre = v_ref[...]
        lgv = lg_ref[...]
        xhat, rstd, vln = _sgu_layernorm(vpre, lgv, lb_ref[...])
        vln_scr[...] = vln.astype(BF16)
        lo = lax.broadcasted_iota(jnp.int32, (CHUNK, LANES), 1) < 64
        tiles = [(slice(c * CHUNK, (c + 1) * CHUNK), p, slice(p * LANES, (p + 1) * LANES))
                 for c in range(tm // CHUNK) for p in range(SG_GROUPS // 2)]
        mixes = [(_dot(wt_ref[2 * p], vln_scr[rows, cols]), _dot(wt_ref[2 * p + 1], vln_scr[rows, cols]))
                 for rows, p, cols in tiles]
        for (rows, p, cols), (m_lo, m_hi) in zip(tiles, mixes):
            if True:
                vp = vln_scr[rows, cols]
                mixed = jnp.where(lo, m_lo, m_hi) + bl_ref[:, cols]
                upre = u_ref[rows, cols]
                dyp = dy_ref[rows, cols]
                dz_ref[rows, cols] = (dyp * mixed * _gelu_grad(upre)).astype(BF16)
                dm = dyp * _gelu(upre)
                dbacc_scr[:, cols] += dm
                dlo = jnp.where(lo, dm, 0.0).astype(BF16)
                dhi = jnp.where(lo, 0.0, dm).astype(BF16)
                dvln_scr[rows, cols] = _dot(wtt_ref[2 * p], dlo) + _dot(wtt_ref[2 * p + 1], dhi)
                dwt_ref[2 * p] += _dot_nt(dlo, vp)
                dwt_ref[2 * p + 1] += _dot_nt(dhi, vp)
        dvln = dvln_scr[...]
        dlg_ref[...] += jnp.sum(dvln * xhat, axis=0, keepdims=True)
        dlb_ref[...] += jnp.sum(dvln, axis=0, keepdims=True)
        dxh = dvln * lgv
        dv = rstd * (dxh - jnp.mean(dxh, axis=-1, keepdims=True)
                     - xhat * jnp.mean(dxh * xhat, axis=-1, keepdims=True))
        dz_ref[:, SG_WIDTH:] = (dv * _gelu_grad(vpre)).astype(BF16)

        @pl.when(step == nsteps - 1)
        def _():
            rr = lax.broadcasted_iota(jnp.int32, (CHUNK, CHUNK), 0)
            cc = lax.broadcasted_iota(jnp.int32, (CHUNK, CHUNK), 1)
            tril = (cc <= rr).astype(F32)
            for gidx in range(SG_GROUPS):
                dwt_ref[gidx] = dwt_ref[gidx] * tril
            kk = lax.broadcasted_iota(jnp.int32, (SG_WIDTH, LANES), 0)
            gg = lax.broadcasted_iota(jnp.int32, (SG_WIDTH, LANES), 1)
            sel = ((kk // 64) == gg).astype(F32)
            dbl_ref[...] = jnp.dot(dbacc_scr[...], sel, preferred_element_type=F32,
                                   precision=lax.Precision.HIGHEST)

    half = lambda k: pl.BlockSpec((tm, SG_WIDTH), lambda i: (i, k))
    vec = pl.BlockSpec((1, SG_WIDTH), lambda i: (0, 0))
    wspec = pl.BlockSpec((SG_GROUPS, CHUNK, CHUNK), lambda i: (0, 0, 0))
    return pl.pallas_call(
        body, name=name, grid=(nsteps,),
        in_specs=[half(0), half(1), pl.BlockSpec((tm, SG_WIDTH), lambda i: (i, 0)), vec, vec,
                  wspec, wspec, pl.BlockSpec((CHUNK, SG_WIDTH), lambda i: (0, 0))],
        out_specs=[pl.BlockSpec((tm, 2 * SG_WIDTH), lambda i: (i, 0)), wspec,
                   pl.BlockSpec((CHUNK, LANES), lambda i: (0, 0)), vec, vec],
        out_shape=[jax.ShapeDtypeStruct((t, 2 * SG_WIDTH), BF16),
                   jax.ShapeDtypeStruct((SG_GROUPS, CHUNK, CHUNK), F32),
                   jax.ShapeDtypeStruct((CHUNK, LANES), F32),
                   jax.ShapeDtypeStruct((1, SG_WIDTH), F32), jax.ShapeDtypeStruct((1, SG_WIDTH), F32)],
        scratch_shapes=[pltpu.VMEM((tm, SG_WIDTH), BF16), pltpu.VMEM((tm, SG_WIDTH), F32),
                        pltpu.VMEM((CHUNK, SG_WIDTH), F32)],
        compiler_params=_cparams())(zuv, zuv, dya, lg, lb, wt, wt_t, bias_l)


def _rope(x, c, s1, s2):
    return x * c + pltpu.roll(x, LANES - 16, 1) * s1 + pltpu.roll(x, 16, 1) * s2


def _rope_t(dy, c, s1, s2):
    return dy * c + pltpu.roll(dy * s1, 16, 1) + pltpu.roll(dy * s2, LANES - 16, 1)


def _mla_prep_fwd(zcq, zckv, zkr, gcq, gckv, qg, kg, wuq, wuk, wuv, rc, rs1, rs2, name, ex=None):
    t = zcq.shape[0]
    tm = _row_tile(t, 256)
    hd = MLA_HEADS * LANES

    def body(zcq_ref, zckv_ref, zkr_ref, gcq_ref, gckv_ref, qg_ref, kg_ref, wuq_ref, wuk_ref, wuv_ref,
             c_ref, s1_ref, s2_ref, q_ref, k_ref, v_ref, cqn_ref, ckvn_ref):
        c, s1, s2 = c_ref[...], s1_ref[...], s2_ref[...]
        xq = zcq_ref[...]
        cqn = (xq * _rstd(xq, MLA_Q_RANK) * gcq_ref[...]).astype(BF16)
        cqn_ref[...] = cqn
        ql = _dot(cqn, wuq_ref[...])
        xk = zckv_ref[...]
        ckvn = (xk * _rstd(xk, MLA_KV_RANK) * gckv_ref[...]).astype(BF16)
        ckvn_ref[...] = ckvn
        kl = _dot(ckvn, wuk_ref[...])
        slot_lane = lax.broadcasted_iota(jnp.int32, (tm, hd), 1) % LANES
        v_ref[...] = jnp.where(slot_lane == V_ONES_LANE, 1.0, _dot(ckvn, wuv_ref[...])).astype(BF16)
        kr = zkr_ref[...]
        for h in range(MLA_HEADS):
            sl = slice(h * LANES, (h + 1) * LANES)
            qh = ql[:, sl]
            q_ref[:, sl] = (_rope(qh * _rstd(qh, MLA_QK) * qg_ref[...], c, s1, s2) * ATTN_SCALE2).astype(BF16)
            kh = kl[:, sl] + kr
            k_ref[:, sl] = _rope(kh * _rstd(kh, MLA_QK) * kg_ref[...], c, s1, s2).astype(BF16)

    row = lambda n: pl.BlockSpec((tm, n), lambda i: (i, 0))
    full = lambda a: pl.BlockSpec(a.shape, lambda i: (0, 0))
    return _call_with_exchange(
        ex, body, name, (t // tm,),
        [row(MLA_Q_RANK), row(MLA_KV_RANK), row(LANES), full(gcq), full(gckv), full(qg), full(kg),
         full(wuq), full(wuk), full(wuv), row(LANES), row(LANES), row(LANES)],
        [row(hd), row(hd), row(hd), row(MLA_Q_RANK), row(MLA_KV_RANK)],
        [jax.ShapeDtypeStruct((t, hd), BF16)] * 3
        + [jax.ShapeDtypeStruct((t, MLA_Q_RANK), BF16), jax.ShapeDtypeStruct((t, MLA_KV_RANK), BF16)],
        [], (zcq, zckv, zkr, gcq, gckv, qg, kg, wuq, wuk, wuv, rc, rs1, rs2))


def _mla_prep_bwd(zcq, zckv, zkr, gcq, gckv, qg, kg, wuq, wuk, wuv, rc, rs1, rs2, dq, dk, dv, name):
    t = zcq.shape[0]
    tm = _row_tile(t, 256)
    hd = MLA_HEADS * LANES

    def body(zcq_ref, zckv_ref, zkr_ref, gcq_ref, gckv_ref, qg_ref, kg_ref, wuq_ref, wuk_ref, wuv_ref,
             c_ref, s1_ref, s2_ref, dq_ref, dk_ref, dv_ref,
             dzcq_ref, dzckv_ref, dzkr_ref, dql_ref, dkl_ref, dgcq_ref, dgckv_ref, dqg_ref, dkg_ref):
        @pl.when(pl.program_id(0) == 0)
        def _():
            for ref in (dgcq_ref, dgckv_ref, dqg_ref, dkg_ref):
                ref[...] = jnp.zeros_like(ref)

        c, s1, s2 = c_ref[...], s1_ref[...], s2_ref[...]
        qgv, kgv = qg_ref[...], kg_ref[...]
        xq = zcq_ref[...]
        rq = _rstd(xq, MLA_Q_RANK)
        ql = _dot((xq * rq * gcq_ref[...]).astype(BF16), wuq_ref[...])
        xk = zckv_ref[...]
        rk = _rstd(xk, MLA_KV_RANK)
        kl = _dot((xk * rk * gckv_ref[...]).astype(BF16), wuk_ref[...])
        kr = zkr_ref[...]
        dqg_acc = jnp.zeros((tm, LANES), F32)
        dkg_acc = jnp.zeros((tm, LANES), F32)
        dkr = jnp.zeros((tm, LANES), F32)
        for h in range(MLA_HEADS):
            sl = slice(h * LANES, (h + 1) * LANES)
            qh = ql[:, sl]
            dqh, dgr = _rms_vjp(qh, _rstd(qh, MLA_QK), qgv, _rope_t(dq_ref[:, sl], c, s1, s2), MLA_QK)
            dql_ref[:, sl] = dqh.astype(BF16)
            dqg_acc += dgr
            kh = kl[:, sl] + kr
            dkh, dgr = _rms_vjp(kh, _rstd(kh, MLA_QK), kgv, _rope_t(dk_ref[:, sl], c, s1, s2), MLA_QK)
            dkl_ref[:, sl] = dkh.astype(BF16)
            dkg_acc += dgr
            dkr += dkh
        dqg_ref[...] += jnp.sum(dqg_acc, axis=0, keepdims=True)
        dkg_ref[...] += jnp.sum(dkg_acc, axis=0, keepdims=True)
        lane = lax.broadcasted_iota(jnp.int32, (tm, LANES), 1)
        dzkr_ref[...] = jnp.where((lane >= MLA_NOPE) & (lane < MLA_QK), dkr, 0.0).astype(BF16)
        dcqn = _dot_nt(dql_ref[...], wuq_ref[...])
        dx, dgr = _rms_vjp(xq, rq, gcq_ref[...], dcqn, MLA_Q_RANK)
        dzcq_ref[...] = dx.astype(BF16)
        dgcq_ref[...] += jnp.sum(dgr, axis=0, keepdims=True)
        dckvn = _dot_nt(dkl_ref[...], wuk_ref[...]) + _dot_nt(dv_ref[...].astype(BF16), wuv_ref[...])
        dx, dgr = _rms_vjp(xk, rk, gckv_ref[...], dckvn, MLA_KV_RANK)
        dzckv_ref[...] = dx.astype(BF16)
        dgckv_ref[...] += jnp.sum(dgr, axis=0, keepdims=True)

    row = lambda n: pl.BlockSpec((tm, n), lambda i: (i, 0))
    full = lambda a: pl.BlockSpec(a.shape, lambda i: (0, 0))
    vec = lambda n: pl.BlockSpec((1, n), lambda i: (0, 0))
    return pl.pallas_call(
        body, name=name, grid=(t // tm,),
        in_specs=[row(MLA_Q_RANK), row(MLA_KV_RANK), row(LANES), full(gcq), full(gckv), full(qg), full(kg),
                  full(wuq), full(wuk), full(wuv), row(LANES), row(LANES), row(LANES), row(hd), row(hd), row(hd)],
        out_specs=[row(MLA_Q_RANK), row(MLA_KV_RANK), row(LANES), row(hd), row(hd),
                   vec(MLA_Q_RANK), vec(MLA_KV_RANK), vec(LANES), vec(LANES)],
        out_shape=[jax.ShapeDtypeStruct((t, MLA_Q_RANK), BF16), jax.ShapeDtypeStruct((t, MLA_KV_RANK), BF16),
                   jax.ShapeDtypeStruct((t, LANES), BF16), jax.ShapeDtypeStruct((t, hd), BF16),
                   jax.ShapeDtypeStruct((t, hd), BF16), jax.ShapeDtypeStruct((1, MLA_Q_RANK), F32),
                   jax.ShapeDtypeStruct((1, MLA_KV_RANK), F32), jax.ShapeDtypeStruct((1, LANES), F32),
                   jax.ShapeDtypeStruct((1, LANES), F32)],
        compiler_params=_cparams(),
    )(zcq, zckv, zkr, gcq, gckv, qg, kg, wuq, wuk, wuv, rc, rs1, rs2, dq, dk, dv)


def _attn_tiles(t):
    tq = 512 if t >= 2048 else 128
    return tq, min(t, 4 * tq), min(t, 4 * tq)


def _causal_keep(tq, nk, i, j, tk):
    row = lax.broadcasted_iota(jnp.int32, (tq, nk), 0)
    col = lax.broadcasted_iota(jnp.int32, (tq, nk), 1)
    return (col - row) <= (i * tq - j * tk)


def _causal_keep_t(tq, nk, i, j, tk):
    key = lax.broadcasted_iota(jnp.int32, (nk, tq), 0)
    qry = lax.broadcasted_iota(jnp.int32, (nk, tq), 1)
    return (key - qry) <= (i * tq - j * tk)


ATTN_FWD_HEADS_PER_STEP = 4
ATTN_BWD_HEADS_PER_STEP = 2


def _attn_fwd(q, k, v, name, ex=None):
    t, hd = q.shape
    hp = ATTN_FWD_HEADS_PER_STEP
    tq, tk, _ = _attn_tiles(t)
    pairs = [(i, j) for i in range(t // tq) for j in range(((i + 1) * tq - 1) // tk + 1)]
    ii = np.array([p[0] for p in pairs], np.int32)
    jj = np.array([p[1] for p in pairs], np.int32)

    def body(ii_ref, jj_ref, q_ref, k_ref, v_ref, o_ref, lse_ref, m_scr, acc_scr):
        s_id = pl.program_id(1)
        i, j = ii_ref[s_id], jj_ref[s_id]
        last = j == ((i + 1) * tq - 1) // tk
        ones_lane = lax.broadcasted_iota(jnp.int32, (tq, LANES), 1) == V_ONES_LANE

        @pl.when(j == 0)
        def _():
            m_scr[...] = jnp.full_like(m_scr, NEG)
            acc_scr[...] = jnp.zeros_like(acc_scr)

        def step(masked, nk):
            scores = [_dot_nt(q_ref[:, hh * LANES:(hh + 1) * LANES], k_ref[:nk, hh * LANES:(hh + 1) * LANES])
                      for hh in range(hp)]
            for hh in range(hp):
                sl = slice(hh * LANES, (hh + 1) * LANES)
                s = scores[hh]
                if masked:
                    s = jnp.where(_causal_keep(tq, nk, i, j, tk), s, NEG)
                m_prev = m_scr[hh]
                m_new = jnp.maximum(m_prev, jnp.max(s, axis=1, keepdims=True))
                p = jnp.exp2(s - m_new)
                alpha = jnp.exp2(m_prev - m_new)
                acc = alpha * acc_scr[:, sl] + _dot(p.astype(BF16), v_ref[:nk, sl])
                if masked:
                    l_new = jnp.sum(jnp.where(ones_lane, acc, 0.0), axis=1, keepdims=True)
                    o_ref[:, sl] = (acc / l_new).astype(BF16)
                    lse_ref[:, sl] = jnp.broadcast_to(m_new + jnp.log(l_new) * LOG2E, (tq, LANES))
                else:
                    acc_scr[:, sl] = acc
                    m_scr[hh] = m_new

        @pl.when(jnp.logical_not(last))
        def _():
            step(False, tk)

        r = (((i + 1) * tq - 1) % tk) // tq
        for rr in range(tk // tq):
            @pl.when(last & (r == rr))
            def _():
                step(True, (rr + 1) * tq)

    w = hp * LANES
    qspec = pl.BlockSpec((tq, w), lambda h, s, ii_r, jj_r: (ii_r[s], h))
    kspec = pl.BlockSpec((tk, w), lambda h, s, ii_r, jj_r: (jj_r[s], h))
    return _call_with_exchange(
        ex, body, name, (hd // w, len(pairs)), [qspec, kspec, kspec], [qspec, qspec],
        [jax.ShapeDtypeStruct((t, hd), BF16), jax.ShapeDtypeStruct((t, hd), F32)],
        [pltpu.VMEM((hp, tq, 1), F32), pltpu.VMEM((tq, w), F32)], (q, k, v),
        prefetch=(jnp.asarray(ii), jnp.asarray(jj)))


def _attn_bwd_rows(o, lse, do, name):
    t, hd = o.shape
    heads = hd // LANES
    tm = _row_tile(t, 512)

    def body(o_ref, lse_ref, do_ref, out_ref):
        lane = lax.broadcasted_iota(jnp.int32, (tm, LANES), 1)
        acc = jnp.zeros((tm, LANES), F32)
        for h in range(heads):
            sl = slice(h * LANES, (h + 1) * LANES)
            delta = jnp.sum(do_ref[:, sl].astype(F32) * o_ref[:, sl].astype(F32), axis=1, keepdims=True)
            acc = jnp.where(lane == h, delta, acc)
            acc = jnp.where(lane == heads + h, lse_ref[:, sl], acc)
        out_ref[...] = acc

    row = pl.BlockSpec((tm, hd), lambda i: (i, 0))
    cols = pl.pallas_call(
        body, name=name, grid=(t // tm,), in_specs=[row, row, row],
        out_specs=pl.BlockSpec((tm, LANES), lambda i: (i, 0)),
        out_shape=jax.ShapeDtypeStruct((t, LANES), F32), compiler_params=_cparams())(o, lse, do)
    rows = cols.T
    return rows[:heads].reshape(heads, 1, t), rows[heads:2 * heads].reshape(heads, 1, t)


def _attn_bwd(q, k, v, delta_rows, lse_rows, do, name, ex=None):
    t, hd = q.shape
    hp = ATTN_BWD_HEADS_PER_STEP
    tq, _, tk = _attn_tiles(t)
    nq = t // tq
    pairs = [(i, j) for j in range(t // tk) for i in range((j * tk) // tq, nq)]
    ii = np.array([p[0] for p in pairs], np.int32)
    jj = np.array([p[1] for p in pairs], np.int32)

    def body(jj_ref, ii_ref, q_ref, k_ref, v_ref, delta_ref, lse_ref, do_ref, dq_ref, dk_ref, dv_ref,
             dk_scr, dv_scr, dq_scr):
        s_id = pl.program_id(1)
        i, j = ii_ref[s_id], jj_ref[s_id]

        @pl.when(s_id == 0)
        def _():
            dq_scr[...] = jnp.zeros_like(dq_scr)

        @pl.when(i == (j * tk) // tq)
        def _():
            dk_scr[...] = jnp.zeros_like(dk_scr)
            dv_scr[...] = jnp.zeros_like(dv_scr)

        rows = pl.ds(pl.multiple_of(i * tq, tq), tq)

        def step(masked, nk):
            heads = [slice(hh * LANES, (hh + 1) * LANES) for hh in range(hp)]
            scores = [_dot_nt(k_ref[:nk, sl], q_ref[:, sl]) for sl in heads]
            for hh, sl in enumerate(heads):
                qv, kv, dov = q_ref[:, sl], k_ref[:nk, sl], do_ref[:, sl]
                st = scores[hh]
                if masked:
                    st = jnp.where(_causal_keep_t(tq, nk, i, j, tk), st, NEG)
                pt = jnp.exp2(st - lse_ref[hh])
                dv_scr[:nk, sl] += _dot(pt.astype(BF16), dov)
                dpt = _dot_nt(v_ref[:nk, sl], dov)
                dst = (pt * (dpt - delta_ref[hh]) * ATTN_SCALE).astype(BF16)
                dk_scr[:nk, sl] += _dot(dst, qv)
                dq_scr[rows, sl] += _dot_tn(dst, kv)

        seen = jnp.minimum((i + 1) * tq - j * tk, tk)
        for nk in range(tq, tk + 1, tq):
            @pl.when((seen == nk) & ((i + 1) * tq - j * tk <= tk))
            def _():
                step(True, nk)

        @pl.when((i + 1) * tq - j * tk > tk)
        def _():
            step(False, tk)

        @pl.when(i == nq - 1)
        def _():
            dk_ref[...] = (dk_scr[...] * (1.0 / ATTN_SCALE2)).astype(BF16)
            dv_ref[...] = dv_scr[...].astype(BF16)

        @pl.when(s_id == len(pairs) - 1)
        def _():
            dq_ref[...] = dq_scr[...].astype(BF16)

    w = hp * LANES
    qspec = pl.BlockSpec((tq, w), lambda h, s, jj_r, ii_r: (ii_r[s], h))
    kspec = pl.BlockSpec((tk, w), lambda h, s, jj_r, ii_r: (jj_r[s], h))
    rspec = pl.BlockSpec((hp, 1, tq), lambda h, s, jj_r, ii_r: (h, 0, ii_r[s]))
    return _call_with_exchange(
        ex, body, name, (hd // w, len(pairs)), [qspec, kspec, kspec, rspec, rspec, qspec],
        [pl.BlockSpec((t, w), lambda h, s, jj_r, ii_r: (0, h)), kspec, kspec],
        [jax.ShapeDtypeStruct((t, hd), BF16)] * 3,
        [pltpu.VMEM((tk, w), F32), pltpu.VMEM((tk, w), F32), pltpu.VMEM((t, w), F32)],
        (q, k, v, delta_rows, lse_rows, do), prefetch=(jnp.asarray(jj), jnp.asarray(ii)))


MEM_W = MEM_HEADS * LANES


def _mem_kv_fwd(mem, gmem, wkv, kg, name):
    m, d = mem.shape

    def body(mem_ref, g_ref, w_ref, kg_ref, k_ref, v_ref, mn_ref):
        xv = mem_ref[...]
        mn = (xv * _rstd(xv, d) * g_ref[...]).astype(BF16)
        mn_ref[...] = mn
        kvm = _dot(mn, w_ref[...])
        v_ref[...] = kvm[:, MEM_W:].astype(BF16)
        for h in range(MEM_HEADS):
            sl = slice(h * LANES, (h + 1) * LANES)
            kh = kvm[:, sl]
            k_ref[:, sl] = (kh * _rstd(kh, LANES) * kg_ref[...]).astype(BF16)

    full = lambda a: pl.BlockSpec(a.shape, lambda i: (0, 0))
    return pl.pallas_call(
        body, name=name, grid=(1,), in_specs=[full(mem), full(gmem), full(wkv), full(kg)],
        out_specs=[pl.BlockSpec((m, MEM_W), lambda i: (0, 0)), pl.BlockSpec((m, MEM_W), lambda i: (0, 0)),
                   pl.BlockSpec((m, d), lambda i: (0, 0))],
        out_shape=[jax.ShapeDtypeStruct((m, MEM_W), BF16), jax.ShapeDtypeStruct((m, MEM_W), BF16),
                   jax.ShapeDtypeStruct((m, d), BF16)],
        compiler_params=_cparams())(mem, gmem, wkv, kg)


def _mem_softmax(qn, kh):
    s = _dot_nt(qn, kh) * (LANES ** -0.5)
    e = jnp.exp(s - jnp.max(s, axis=1, keepdims=True))
    return e / jnp.sum(e, axis=1, keepdims=True)


def _mem_attn_fwd(zqm, qg, km, vm, name):
    t = zqm.shape[0]
    tm = _row_tile(t, 512)

    def body(q_ref, qg_ref, k_ref, v_ref, o_ref):
        for h in range(MEM_HEADS):
            sl = slice(h * LANES, (h + 1) * LANES)
            qh = q_ref[:, sl]
            qn = (qh * _rstd(qh, LANES) * qg_ref[...]).astype(BF16)
            p = _mem_softmax(qn, k_ref[:, sl])
            o_ref[:, sl] = _dot(p.astype(BF16), v_ref[:, sl]).astype(BF16)

    row = pl.BlockSpec((tm, MEM_W), lambda i: (i, 0))
    full = lambda a: pl.BlockSpec(a.shape, lambda i: (0, 0))
    return pl.pallas_call(
        body, name=name, grid=(t // tm,), in_specs=[row, full(qg), full(km), full(vm)], out_specs=row,
        out_shape=jax.ShapeDtypeStruct((t, MEM_W), BF16), compiler_params=_cparams())(zqm, qg, km, vm)


def _mem_attn_bwd(zqm, dyc, qg, km, vm, name):
    t = zqm.shape[0]
    m = km.shape[0]
    tm = _row_tile(t, 256)

    def body(q_ref, dy_ref, qg_ref, k_ref, v_ref, dz_ref, dk_ref, dv_ref, dqg_ref):
        @pl.when(pl.program_id(0) == 0)
        def _():
            dk_ref[...] = jnp.zeros_like(dk_ref)
            dv_ref[...] = jnp.zeros_like(dv_ref)
            dqg_ref[...] = jnp.zeros_like(dqg_ref)

        qgv = qg_ref[...]
        dqg_acc = jnp.zeros((tm, LANES), F32)
        heads = [slice(h * LANES, (h + 1) * LANES) for h in range(MEM_HEADS)]
        dps = [_dot_nt(dy_ref[:, sl], v_ref[:, sl]) for sl in heads]
        for h, sl in enumerate(heads):
            qh = q_ref[:, sl]
            r = _rstd(qh, LANES)
            qn = (qh * r * qgv).astype(BF16)
            kh = k_ref[:, sl]
            p = _mem_softmax(qn, kh)
            dov = dy_ref[:, sl]
            dv_ref[:, sl] += _dot_tn(p.astype(BF16), dov)
            dp = dps[h]
            ds = (p * (dp - jnp.sum(dp * p, axis=1, keepdims=True)) * (LANES ** -0.5)).astype(BF16)
            dk_ref[:, sl] += _dot_tn(ds, qn)
            dqh, dgr = _rms_vjp(qh, r, qgv, _dot(ds, kh), LANES)
            dz_ref[:, sl] = dqh.astype(BF16)
            dqg_acc += dgr
        dqg_ref[...] += jnp.sum(dqg_acc, axis=0, keepdims=True)

    row = pl.BlockSpec((tm, MEM_W), lambda i: (i, 0))
    full = lambda a: pl.BlockSpec(a.shape, lambda i: (0, 0))
    acc = pl.BlockSpec((m, MEM_W), lambda i: (0, 0))
    return pl.pallas_call(
        body, name=name, grid=(t // tm,), in_specs=[row, row, full(qg), full(km), full(vm)],
        out_specs=[row, acc, acc, pl.BlockSpec((1, LANES), lambda i: (0, 0))],
        out_shape=[jax.ShapeDtypeStruct((t, MEM_W), BF16), jax.ShapeDtypeStruct((m, MEM_W), F32),
                   jax.ShapeDtypeStruct((m, MEM_W), F32), jax.ShapeDtypeStruct((1, LANES), F32)],
        compiler_params=_cparams())(zqm, dyc, qg, km, vm)


def _mem_kv_bwd(mem, gmem, wkv, kg, dkn, dvm, name):
    m, d = mem.shape

    def body(mem_ref, g_ref, w_ref, kg_ref, dk_ref, dv_ref, dw_ref, dkg_ref, dg_ref, dkv_scr):
        xv = mem_ref[...]
        r = _rstd(xv, d)
        mn = (xv * r * g_ref[...]).astype(BF16)
        kvm = _dot(mn, w_ref[...])
        dkv_scr[:, MEM_W:] = dv_ref[...].astype(BF16)
        dkg_acc = jnp.zeros((m, LANES), F32)
        for h in range(MEM_HEADS):
            sl = slice(h * LANES, (h + 1) * LANES)
            kh = kvm[:, sl]
            dkh, dgr = _rms_vjp(kh, _rstd(kh, LANES), kg_ref[...], dk_ref[:, sl], LANES)
            dkv_scr[:, sl] = dkh.astype(BF16)
            dkg_acc += dgr
        dkg_ref[...] = jnp.sum(dkg_acc, axis=0, keepdims=True)
        dkv = dkv_scr[...]
        dw_ref[...] = _dot_tn(mn, dkv)
        dmn = _dot_nt(dkv, w_ref[...])
        dg_ref[...] = jnp.sum(dmn * xv * r, axis=0, keepdims=True)

    full = lambda a: pl.BlockSpec(a.shape, lambda i: (0, 0))
    return pl.pallas_call(
        body, name=name, grid=(1,),
        in_specs=[full(mem), full(gmem), full(wkv), full(kg), full(dkn), full(dvm)],
        out_specs=[pl.BlockSpec((d, 2 * MEM_W), lambda i: (0, 0)), pl.BlockSpec((1, LANES), lambda i: (0, 0)),
                   pl.BlockSpec((1, d), lambda i: (0, 0))],
        out_shape=[jax.ShapeDtypeStruct((d, 2 * MEM_W), F32), jax.ShapeDtypeStruct((1, LANES), F32),
                   jax.ShapeDtypeStruct((1, d), F32)],
        scratch_shapes=[pltpu.VMEM((m, 2 * MEM_W), BF16)],
        compiler_params=_cparams())(mem, gmem, wkv, kg, dkn, dvm)


def _merge_fwd(x1, ya, yb, yc, zg, bg, wa, wb, wc, wo, name):
    t, d = x1.shape
    tm = _row_tile(t, 256)

    def body(x_ref, ya_ref, yb_ref, yc_ref, zg_ref, bg_ref, wa_ref, wb_ref, wc_ref, wo_ref,
             x2_ref, mg_ref, pa_ref, pb_ref, pc_ref):
        merged = None
        for k, (y_ref, w_ref, p_ref) in enumerate(
                ((ya_ref, wa_ref, pa_ref), (yb_ref, wb_ref, pb_ref), (yc_ref, wc_ref, pc_ref))):
            sl = slice(k * d, (k + 1) * d)
            pr = _dot(y_ref[...], w_ref[...])
            p_ref[...] = pr.astype(BF16)
            term = jax.nn.sigmoid(zg_ref[:, sl] + bg_ref[:, sl]) * pr
            merged = term if merged is None else merged + term
        mb = merged.astype(BF16)
        mg_ref[...] = mb
        x2_ref[...] = x_ref[...] + _dot(mb, wo_ref[...])

    row = lambda n: pl.BlockSpec((tm, n), lambda i: (i, 0))
    full = lambda a: pl.BlockSpec(a.shape, lambda i: (0, 0))
    return pl.pallas_call(
        body, name=name, grid=(t // tm,),
        in_specs=[row(d), row(ya.shape[1]), row(yb.shape[1]), row(yc.shape[1]), row(3 * d), full(bg),
                  full(wa), full(wb), full(wc), full(wo)],
        out_specs=[row(d)] * 5,
        out_shape=[jax.ShapeDtypeStruct((t, d), F32)] + [jax.ShapeDtypeStruct((t, d), BF16)] * 4,
        compiler_params=_cparams())(x1, ya, yb, yc, zg, bg, wa, wb, wc, wo)


def _merge_bwd(dx2, pa, pb, pc, zg, bg, wa, wb, wc, wo, name, ex=None):
    t, d = dx2.shape
    tm = _row_tile(t, 256)

    def body(dx_ref, pa_ref, pb_ref, pc_ref, zg_ref, bg_ref, wa_ref, wb_ref, wc_ref, wo_ref,
             dpa_ref, dpb_ref, dpc_ref, dzg_ref, dbg_ref, dya_ref, dyb_ref, dyc_ref):
        @pl.when(pl.program_id(0) == 0)
        def _():
            dbg_ref[...] = jnp.zeros_like(dbg_ref)

        dm = _dot_nt(dx_ref[...].astype(BF16), wo_ref[...])
        for k, (p_ref, w_ref, dp_ref, dy_ref) in enumerate(
                ((pa_ref, wa_ref, dpa_ref, dya_ref), (pb_ref, wb_ref, dpb_ref, dyb_ref),
                 (pc_ref, wc_ref, dpc_ref, dyc_ref))):
            sl = slice(k * d, (k + 1) * d)
            gate = jax.nn.sigmoid(zg_ref[:, sl] + bg_ref[:, sl])
            dpr = (dm * gate).astype(BF16)
            dp_ref[...] = dpr
            dzg = dm * p_ref[...].astype(F32) * gate * (1.0 - gate)
            dzg_ref[:, sl] = dzg.astype(BF16)
            dbg_ref[:, sl] += jnp.sum(dzg, axis=0, keepdims=True)
            dy_ref[...] = _dot_nt(dpr, w_ref[...]).astype(dy_ref.dtype)

    row = lambda n: pl.BlockSpec((tm, n), lambda i: (i, 0))
    full = lambda a: pl.BlockSpec(a.shape, lambda i: (0, 0))
    na, nb, nc = wa.shape[0], wb.shape[0], wc.shape[0]
    return _call_with_exchange(
        ex, body, name, (t // tm,),
        [row(d), row(d), row(d), row(d), row(3 * d), full(bg), full(wa), full(wb), full(wc), full(wo)],
        [row(d), row(d), row(d), row(3 * d), pl.BlockSpec((1, 3 * d), lambda i: (0, 0)), row(na), row(nb), row(nc)],
        [jax.ShapeDtypeStruct((t, d), BF16)] * 3
        + [jax.ShapeDtypeStruct((t, 3 * d), BF16), jax.ShapeDtypeStruct((1, 3 * d), F32),
           jax.ShapeDtypeStruct((t, na), F32), jax.ShapeDtypeStruct((t, nb), BF16),
           jax.ShapeDtypeStruct((t, nc), BF16)],
        [], (dx2, pa, pb, pc, zg, bg, wa, wb, wc, wo))


def _adamw_math(w, g, m, v):
    bc1 = 1.0 - ADAM_B1 ** ADAM_STEP
    bc2 = 1.0 - ADAM_B2 ** ADAM_STEP
    nm = ADAM_B1 * m + (1.0 - ADAM_B1) * g
    nv = ADAM_B2 * v + (1.0 - ADAM_B2) * (g * g)
    delta = -ADAM_LR * ((nm / bc1) / (jnp.sqrt(nv / bc2) + ADAM_EPS) + ADAM_WD * w)
    return delta, nm, nv


def _div_tile(n, cap, mult):
    best = None
    for cand in range(mult, min(n, cap) + 1, mult):
        if n % cand == 0:
            best = cand
    assert best is not None, (n, cap, mult)
    return best


def _adamw(w, g, m, v, name):
    rows, cols = w.shape
    tr = rows if rows * cols <= 256 * 1024 else _div_tile(rows, 256, 8)

    def body(w_ref, g_ref, m_ref, v_ref, d_ref, nm_ref, nv_ref):
        d_ref[...], nm_ref[...], nv_ref[...] = _adamw_math(w_ref[...], g_ref[...], m_ref[...], v_ref[...])

    blk = pl.BlockSpec((tr, cols), lambda i: (i, 0))
    return pl.pallas_call(
        body, name=name, grid=(rows // tr,), in_specs=[blk] * 4, out_specs=[blk] * 3,
        out_shape=[jax.ShapeDtypeStruct((rows, cols), F32)] * 3, compiler_params=_cparams())(w, g, m, v)


def _adamw_slots(w, slots, m, v, name):
    _, hr, cols = w.shape
    tr = _div_tile(hr, 128, 16)

    def body(w_ref, s_ref, m_ref, v_ref, g_ref, d_ref, nm_ref, nv_ref):
        g = s_ref[0, 0].astype(F32)
        for k in range(1, N_CHIPS):
            g = g + s_ref[0, k].astype(F32)
        g_ref[0] = g
        d_ref[0], nm_ref[0], nv_ref[0] = _adamw_math(w_ref[0], g, m_ref[0], v_ref[0])

    blk = pl.BlockSpec((1, tr, cols), lambda h, i: (h, i, 0))
    return pl.pallas_call(
        body, name=name, grid=(2, hr // tr),
        in_specs=[blk, pl.BlockSpec((1, N_CHIPS, tr, cols), lambda h, i: (h, 0, i, 0)), blk, blk],
        out_specs=[blk] * 4, out_shape=[jax.ShapeDtypeStruct((2, hr, cols), F32)] * 4,
        compiler_params=_cparams())(w, slots, m, v)


ANY = pl.BlockSpec(memory_space=pl.ANY)


def _place():
    x, y, c = lax.axis_index("x"), lax.axis_index("y"), lax.axis_index("c")
    other_chips = [(1 - x, y), (x, 1 - y), (1 - x, 1 - y)]
    return x, y, c, other_chips


def _remote(src, dst, send_sem, recv_sem, to):
    return pltpu.make_async_remote_copy(src_ref=src, dst_ref=dst, send_sem=send_sem, recv_sem=recv_sem,
                                        device_id=to, device_id_type=MESH)


PIECE_BYTES = 384 * 1024


def _row_pieces(half_rows, cols):
    for n in (4, 2):
        if half_rows % (16 * n) == 0 and half_rows * cols * 2 // n >= PIECE_BYTES:
            return [pl.ds(k * (half_rows // n), half_rows // n) for k in range(n)]
    return [pl.ds(0, half_rows)]


def _pieces(arrays, rows_axis):
    return [(w, rows) for w, a in enumerate(arrays) for rows in _row_pieces(a.shape[rows_axis], a.shape[-1])]


def _gather_exchange(shards):
    nw = len(shards)
    pieces = _pieces(shards, 1)
    npc = len(pieces)

    def build(s_refs, g_refs, sems):
        send_sems, recv_sems, local_sems = sems
        x, y, c, chips = _place()
        me = 2 * x + y
        sibling = (x, y, 1 - c)
        mine = [pltpu.make_async_copy(s_refs[w], g_refs[w].at[me], local_sems.at[w]) for w in range(nw)]
        first = [_remote(s_refs[w].at[c, rows], g_refs[w].at[me, c, rows], send_sems.at[k, p], recv_sems.at[k, p],
                         (cx, cy, c)) for k, (cx, cy) in enumerate(chips) for p, (w, rows) in enumerate(pieces)]

        def start():
            for cp in mine + first:
                cp.start()

        arrived = [g_refs[w].at[2 * cx + cy, c, rows] for cx, cy in chips for w, rows in pieces]
        passed = [_remote(slab, slab, send_sems.at[3 + q // npc, q % npc], recv_sems.at[3 + q // npc, q % npc], sibling)
                  for q, slab in enumerate(arrived)]

        def pass_on():
            for q, slab in enumerate(arrived):
                k, p = q // npc, q % npc
                _remote(slab, slab, send_sems.at[k, p], recv_sems.at[k, p], (*chips[k], c)).wait_recv()
                passed[q].start()

        def finish():
            for k, (cx, cy) in enumerate(chips):
                for p, (w, rows) in enumerate(pieces):
                    slab = g_refs[w].at[2 * cx + cy, 1 - c, rows]
                    _remote(slab, slab, send_sems.at[3 + k, p], recv_sems.at[3 + k, p], sibling).wait_recv()
            for cp in first + passed:
                cp.wait_send()
            for cp in mine:
                cp.wait()

        return start, pass_on, finish

    return _Exchange(list(shards), [jax.ShapeDtypeStruct((N_CHIPS,) + s.shape, BF16) for s in shards],
                     [pltpu.SemaphoreType.DMA((6, npc)), pltpu.SemaphoreType.DMA((6, npc)),
                      pltpu.SemaphoreType.DMA((nw,))], build)


def _swap_halves(grads, name):
    nw = len(grads)

    def body(*refs):
        g_refs, sib_refs = refs[:nw], refs[nw:2 * nw]
        send_sems, recv_sems = refs[2 * nw:]
        x, y, c, _ = _place()
        copies = [_remote(g_refs[w].at[s, 1 - c], sib_refs[w].at[s], send_sems.at[s, w], recv_sems.at[s, w],
                          (x, y, 1 - c)) for w in range(nw) for s in range(N_CHIPS)]
        for cp in copies:
            cp.start()
        for cp in copies:
            cp.wait_recv()
        for cp in copies:
            cp.wait_send()

    return pl.pallas_call(
        body, name=name, in_specs=[ANY] * nw, out_specs=[ANY] * nw,
        out_shape=[jax.ShapeDtypeStruct((N_CHIPS,) + g.shape[2:], BF16) for g in grads],
        scratch_shapes=[pltpu.SemaphoreType.DMA((N_CHIPS, nw)), pltpu.SemaphoreType.DMA((N_CHIPS, nw))],
    )(*grads)


def _pair_sum(grad, sib, core, name):
    nchip, _, hr, cols = grad.shape
    tr = _div_tile(hr, 256, 16)

    def body(core_ref, a_ref, b_ref, o_ref):
        o_ref[...] = (a_ref[0].astype(F32) + b_ref[...].astype(F32)).astype(BF16)

    return pl.pallas_call(
        body, name=name,
        grid_spec=pltpu.PrefetchScalarGridSpec(
            num_scalar_prefetch=1, grid=(nchip, hr // tr),
            in_specs=[pl.BlockSpec((1, 1, tr, cols), lambda s, i, core_r: (s, core_r[0], i, 0)),
                      pl.BlockSpec((1, tr, cols), lambda s, i, core_r: (s, i, 0))],
            out_specs=pl.BlockSpec((1, tr, cols), lambda s, i, core_r: (s, i, 0))),
        out_shape=jax.ShapeDtypeStruct((nchip, hr, cols), BF16), compiler_params=_cparams())(core, grad, sib)


def _pair_sum_exchange(sums):
    nw = len(sums)
    pieces = _pieces(sums, 1)
    npc = len(pieces)

    def build(p_refs, o_refs, sems):
        send_sems, recv_sems, local_sems = sems
        x, y, c, chips = _place()
        me = 2 * x + y
        sibling = (x, y, 1 - c)
        mine = [pltpu.make_async_copy(p_refs[w].at[me], o_refs[w].at[c, 3], local_sems.at[w]) for w in range(nw)]
        first = [_remote(p_refs[w].at[2 * cx + cy, rows], o_refs[w].at[c, k, rows], send_sems.at[k, p],
                         recv_sems.at[k, p], (cx, cy, c))
                 for k, (cx, cy) in enumerate(chips) for p, (w, rows) in enumerate(pieces)]

        def start():
            for cp in mine + first:
                cp.start()

        passed = [_remote(o_refs[w].at[c, k, rows], o_refs[w].at[c, k, rows], send_sems.at[3 + k, p],
                          recv_sems.at[3 + k, p], sibling) for k in range(N_CHIPS) for p, (w, rows) in enumerate(pieces)]

        def pass_on():
            for k in range(N_CHIPS):
                own_waited = set()
                for p, (w, rows) in enumerate(pieces):
                    if k < 3:
                        first[k * npc + p].wait_recv()
                    elif w not in own_waited:
                        mine[w].wait()
                        own_waited.add(w)
                    passed[k * npc + p].start()

        def finish():
            for k in range(N_CHIPS):
                for p, (w, rows) in enumerate(pieces):
                    slab = o_refs[w].at[1 - c, k, rows]
                    _remote(slab, slab, send_sems.at[3 + k, p], recv_sems.at[3 + k, p], sibling).wait_recv()
            for cp in first + passed:
                cp.wait_send()

        return start, pass_on, finish

    return _Exchange(list(sums), [jax.ShapeDtypeStruct((2,) + p.shape, BF16) for p in sums],
                     [pltpu.SemaphoreType.DMA((7, npc)), pltpu.SemaphoreType.DMA((7, npc)),
                      pltpu.SemaphoreType.DMA((nw,))], build)


def _small_sum_exchange(vec):
    m_per, n = vec.shape

    def build(ins, outs, scr):
        (x_ref,), (out_ref,) = ins, outs
        gath_ref, sum_ref, send_sems, recv_sems, local_sem, out_sem = scr
        x, y, c, chips = _place()
        me, sibling = (x, y, c), (x, y, 1 - c)

        def rows(px, py, pc):
            return gath_ref.at[pl.ds((4 * px + 2 * py + pc) * m_per, m_per), :]

        def copy(k, block, to, src=None):
            return pltpu.make_async_remote_copy(
                src_ref=rows(*block) if src is None else src, dst_ref=rows(*block),
                send_sem=send_sems.at[k], recv_sem=recv_sems.at[k], device_id=to, device_id_type=MESH)

        mine = pltpu.make_async_copy(x_ref, rows(*me), local_sem)
        first = [copy(0, me, sibling, src=x_ref)] + [copy(1 + j, me, (*chip, c), src=x_ref)
                                                     for j, chip in enumerate(chips)]

        def start():
            for cp in [mine] + first:
                cp.start()

        passed = [copy(4 + j, (*chip, c), sibling) for j, chip in enumerate(chips)]

        def pass_on():
            for j, chip in enumerate(chips):
                copy(1 + j, (*chip, c), me).wait_recv()
                passed[j].start()

        def finish():
            copy(0, sibling, me).wait_recv()
            for j, chip in enumerate(chips):
                copy(4 + j, (*chip, 1 - c), me).wait_recv()
            for cp in first + passed:
                cp.wait_send()
            mine.wait()
            acc = gath_ref[pl.ds(0, m_per), :]
            for k in range(1, N_DEV):
                acc = acc + gath_ref[pl.ds(k * m_per, m_per), :]
            sum_ref[...] = acc
            done = pltpu.make_async_copy(sum_ref, out_ref, out_sem)
            done.start()
            done.wait()

        return start, pass_on, finish

    return _Exchange([vec], [jax.ShapeDtypeStruct((m_per, n), F32)],
                     [pltpu.VMEM((N_DEV * m_per, n), F32), pltpu.VMEM((m_per, n), F32), pltpu.SemaphoreType.DMA((7,)),
                      pltpu.SemaphoreType.DMA((7,)), pltpu.SemaphoreType.DMA, pltpu.SemaphoreType.DMA], build)


def _pack_small(vals, tail=()):
    flat = jnp.concatenate([vals[name].reshape(-1).astype(F32) for name, _ in SMALL] + [v.reshape(1) for v in tail])
    flat = jnp.pad(flat, (0, SMALL_ROWS * LANES - flat.shape[0]))
    return flat.reshape(SMALL_ROWS, LANES)


def _unpack_small(packed):
    flat = packed.reshape(-1)
    out, off = {}, 0
    for name, shape in SMALL:
        n = int(np.prod(shape))
        out[name] = flat[off:off + n].reshape(shape)
        off += n
    return out


def _head_pad_cols(w, heads, real):
    k = w.shape[0]
    return jnp.pad(w.reshape(k, heads, real), ((0, 0), (0, 0), (0, LANES - real))).reshape(k, heads * LANES)


def _rope_tables(positions):
    half = MLA_ROPE // 2
    inv = ROPE_BASE ** (-jnp.arange(half, dtype=F32) / half)
    ang = positions.astype(F32)[:, None] * inv
    cos, sin = jnp.cos(ang), jnp.sin(ang)
    t = positions.shape[0]
    z = lambda n: jnp.zeros((t, n), F32)
    rc = jnp.concatenate([jnp.ones((t, MLA_NOPE), F32), cos, cos, z(LANES - MLA_QK)], axis=1)
    rs1 = jnp.concatenate([z(MLA_NOPE), -sin, z(LANES - MLA_NOPE - half)], axis=1)
    rs2 = jnp.concatenate([z(MLA_NOPE + half), sin, z(LANES - MLA_QK)], axis=1)
    return rc, rs1, rs2


FFN1_WEIGHTS = ("ffn1_w_gu", "ffn1_w_down")
FFN2_WEIGHTS = ("ffn2_w_gu", "ffn2_w_down")
MIXER_WEIGHTS = tuple(n for n, *_ in SHARDED if n not in FFN1_WEIGHTS + FFN2_WEIGHTS)
SHARD_SHAPE = {n: (r, c, kind) for n, r, c, kind in SHARDED}


def _from_blocks(name, gathered):
    r, c, kind = SHARD_SHAPE[name]
    blk = gathered.reshape(N_CHIPS, r, c)
    return blk, (blk.transpose(1, 0, 2).reshape(r, N_CHIPS * c) if kind == "col" else blk.reshape(N_CHIPS * r, c))


def _grad_pair_sums(names, gw, core, tag):
    by_owner = []
    for name in names:
        r, c, kind = SHARD_SHAPE[name]
        if gw[name].dtype == BF16:
            blk = gw[name]
        elif kind == "col":
            blk = gw[name].reshape(r, N_CHIPS, c).transpose(1, 0, 2)
        else:
            blk = gw[name].reshape(N_CHIPS, r, c)
        by_owner.append(blk.astype(BF16).reshape(N_CHIPS, 2, r // 2, c))
    received = _swap_halves(by_owner, "grad_swap_" + tag)
    return [_pair_sum(g, s, core, "pair_sum_" + n) for g, s, n in zip(by_owner, received, names)]


def _device_step(x, mem, positions, tgt, small, shards, core):
    d = D_MODEL
    g_ffn1, g_mix, g_ffn2 = small["ffn1_norm"], small["mix_norm"], small["ffn2_norm"]
    big = {}
    for name, g in zip(FFN1_WEIGHTS, _run_exchange(_gather_exchange([shards[n] for n in FFN1_WEIGHTS]), "gather_ffn1")):
        big[name + "#blocks"], big[name] = _from_blocks(name, g)
    wgu1, wd1 = big["ffn1_w_gu#blocks"], big["ffn1_w_down"].reshape(2, FF_TILE, d)
    x1, gpre1, upre1, h, *rest = _ffn_fwd(x, g_ffn1, wgu1, wd1, "ffn1_fwd", next_gain=g_mix,
                                          ex=_gather_exchange([shards[n] for n in MIXER_WEIGHTS]))
    for name, g in zip(MIXER_WEIGHTS, rest):
        big[name + "#blocks"], big[name] = _from_blocks(name, g)
    w_in = big["w_in"]
    w_uv_, w_cq, w_ckv = w_in[:, :COL_CQ], w_in[:, COL_CQ:COL_CKV], w_in[:, COL_CKV:COL_KR]
    w_kr = jnp.pad(w_in[:, COL_KR:COL_QM], ((0, 0), (MLA_NOPE, LANES - MLA_QK)))
    w_qm, w_g = w_in[:, COL_QM:COL_GATE], w_in[:, COL_GATE:]
    segs = (w_uv_, w_cq, w_ckv, w_kr, w_qm, w_g)
    wuq = _head_pad_cols(big["mla_w_uq"], MLA_HEADS, MLA_QK)
    ukv = big["mla_w_ukv"].reshape(MLA_KV_RANK, MLA_HEADS, 2, MLA_NOPE)
    wuk = _head_pad_cols(ukv[:, :, 0].reshape(MLA_KV_RANK, -1), MLA_HEADS, MLA_NOPE)
    wuv = _head_pad_cols(ukv[:, :, 1].reshape(MLA_KV_RANK, -1), MLA_HEADS, MLA_NOPE)
    wkv = big["mem_w_kv"]
    wa, wc, wo = big["w_branch_a"], big["w_branch_c"], big["w_out"]
    wb = jnp.pad(big["w_branch_b"].reshape(MLA_HEADS, MLA_NOPE, d),
                 ((0, 0), (0, LANES - MLA_NOPE), (0, 0))).reshape(MLA_HEADS * LANES, d)
    qg = jnp.pad(small["mla_q_norm"], ((0, 0), (0, LANES - MLA_QK)))
    kg = jnp.pad(small["mla_k_norm"], ((0, 0), (0, LANES - MLA_QK)))
    causal = jnp.tril(jnp.ones((CHUNK, CHUNK), bool))
    wt_f = jnp.where(causal[None], small["sg_w"][0], 0.0)
    wt, wt_t = wt_f.astype(BF16), wt_f.transpose(0, 2, 1).astype(BF16)
    bias_l = jnp.repeat(small["sg_b"][0].T, 64, axis=1)
    rc, rs1, rs2 = _rope_tables(positions)

    zuv, zcq, zckv, zkr, zqm, zg = _mm_cols(h, segs, [F32] * 5 + [BF16], "in_proj")
    ya = _sgu_fwd(zuv, small["sg_ln_g"], small["sg_ln_b"], wt, bias_l, "sgu_fwd")
    q, k, v, cqn, ckvn = _mla_prep_fwd(zcq, zckv, zkr, small["mla_cq_norm"], small["mla_ckv_norm"], qg, kg,
                                       wuq, wuk, wuv, rc, rs1, rs2, "mla_prep_fwd")
    yb, lse, *rest = _attn_fwd(q, k, v, "mla_attn_fwd", ex=_gather_exchange([shards[n] for n in FFN2_WEIGHTS]))
    for name, g in zip(FFN2_WEIGHTS, rest):
        big[name + "#blocks"], big[name] = _from_blocks(name, g)
    wgu2, wd2 = big["ffn2_w_gu#blocks"], big["ffn2_w_down"].reshape(2, FF_TILE, d)
    km, vm, memn = _mem_kv_fwd(mem, small["mem_norm"], wkv, small["mem_k_norm"], "mem_kv_fwd")
    yc = _mem_attn_fwd(zqm, small["mem_q_norm"], km, vm, "mem_attn_fwd")
    x2, merged, pa, pb, pc = _merge_fwd(x1, ya, yb, yc, zg, small["b_gate"], wa, wb, wc, wo, "merge_fwd")
    dy, loss_row, gpre2, upre2 = _ffn_fwd(x2, g_ffn2, wgu2, wd2, "ffn2_fwd", target=tgt)

    gw, gs, slots = {}, {}, {}

    def ffn_grads(prefix, xin, gain, dyin, gpre, upre, wgu, wd, ex=None, ex_names=(), last=False):
        dx, dgain, xn, dgt, dup, act, *got = _ffn_bwd(xin, gain, dyin, gpre, upre, wgu, wd, prefix + "_bwd", ex=ex)
        slots.update(zip(ex_names, got))
        gs[prefix + "_norm"] = dgain
        gw[prefix + "_w_gu"] = jnp.concatenate(
            [_mm_tn(xn, dgt, prefix + "_dwg", col_blocks=True, out_dtype=BF16),
             _mm_tn(xn, dup, prefix + "_dwu", col_blocks=True, out_dtype=BF16)], axis=0)
        rows_down = SHARD_SHAPE[prefix + "_w_down"][0]
        if last:
            small_sum = _small_sum_exchange(_pack_small(gs, tail=[loss_row[0, 0]]))
            dwd, summed = _mm_tn(act, dyin, prefix + "_dwd", scale=0.5, ex=small_sum, out_dtype=BF16)
            gw[prefix + "_w_down"] = dwd.reshape(N_CHIPS, rows_down, d)
            return dx, summed
        gw[prefix + "_w_down"] = _mm_tn(act, dyin, prefix + "_dwd", scale=0.5, out_dtype=BF16).reshape(
            N_CHIPS, rows_down, d)
        return dx

    dx2 = ffn_grads("ffn2", x2, g_ffn2, dy, gpre2, upre2, wgu2, wd2)
    ffn2_sums = _pair_sum_exchange(_grad_pair_sums(FFN2_WEIGHTS, gw, core, "ffn2"))
    dpa, dpb, dpc, dzg, dbg, dya, dyb, dyc = _merge_bwd(dx2, pa, pb, pc, zg, small["b_gate"], wa, wb, wc, wo,
                                                        "merge_bwd")
    gs["b_gate"] = dbg
    gw["w_out"] = _mm_tn(merged, dx2, "dw_out")
    gw["w_branch_a"] = _mm_tn(ya, dpa, "dw_branch_a")
    gw["w_branch_b"] = _mm_tn(yb, dpb, "dw_branch_b").reshape(MLA_HEADS, LANES, d)[:, :MLA_NOPE].reshape(-1, d)
    gw["w_branch_c"] = _mm_tn(yc, dpc, "dw_branch_c")

    dzuv, dwt, dbl, dlg, dlb = _sgu_bwd(zuv, dya, small["sg_ln_g"], small["sg_ln_b"], wt, wt_t, bias_l, "sgu_bwd")
    gs["sg_w"], gs["sg_b"] = dwt[None], dbl[:, :SG_GROUPS].T[None]
    gs["sg_ln_g"], gs["sg_ln_b"] = dlg, dlb

    delta_rows, lse_rows = _attn_bwd_rows(yb, lse, dyb, "mla_attn_bwd_rows")
    dq, dk, dv, *got = _attn_bwd(q, k, v, delta_rows, lse_rows, dyb, "mla_attn_bwd", ex=ffn2_sums)
    slots.update(zip(FFN2_WEIGHTS, got))
    dzcq, dzckv, dzkr, dql, dkl, dgcq, dgckv, dqg, dkg = _mla_prep_bwd(
        zcq, zckv, zkr, small["mla_cq_norm"], small["mla_ckv_norm"], qg, kg, wuq, wuk, wuv, rc, rs1, rs2,
        dq, dk, dv, "mla_prep_bwd")
    gs["mla_cq_norm"], gs["mla_ckv_norm"] = dgcq, dgckv
    gs["mla_q_norm"], gs["mla_k_norm"] = dqg[:, :MLA_QK], dkg[:, :MLA_QK]
    gw["mla_w_uq"] = _mm_tn(cqn, dql, "dw_uq").reshape(MLA_Q_RANK, MLA_HEADS, LANES)[:, :, :MLA_QK].reshape(
        MLA_Q_RANK, -1)
    dwuk = _mm_tn(ckvn, dkl, "dw_uk").reshape(MLA_KV_RANK, MLA_HEADS, LANES)[:, :, :MLA_NOPE]
    dwuv = _mm_tn(ckvn, dv, "dw_uv").reshape(MLA_KV_RANK, MLA_HEADS, LANES)[:, :, :MLA_NOPE]
    gw["mla_w_ukv"] = jnp.concatenate([dwuk, dwuv], axis=2).reshape(MLA_KV_RANK, -1)

    dzqm, dkn, dvm, dmqg = _mem_attn_bwd(zqm, dyc, small["mem_q_norm"], km, vm, "mem_attn_bwd")
    gs["mem_q_norm"] = dmqg
    gw["mem_w_kv"], gs["mem_k_norm"], gs["mem_norm"] = _mem_kv_bwd(
        mem, small["mem_norm"], wkv, small["mem_k_norm"], dkn, dvm, "mem_kv_bwd")

    dzs = (dzuv, dzcq, dzckv, dzkr, dzqm, dzg)
    dws = list(_mm_tn_cols(h, dzs[:5], "dw_in_narrow")) + [_mm_tn(h, dzg, "dw_in_gate")]
    dws[3] = dws[3][:, MLA_NOPE:MLA_QK]
    gw["w_in"] = jnp.concatenate(dws, axis=1)
    dx1, gs["mix_norm"] = _proj_norm_bwd(dzs, [w.T for w in segs], x1, g_mix, dx2, "in_proj_bwd")
    mixer_sums = _pair_sum_exchange(_grad_pair_sums(MIXER_WEIGHTS, gw, core, "mixer"))
    dx, summed = ffn_grads("ffn1", x, g_ffn1, dx1, gpre1, upre1, wgu1, wd1, ex=mixer_sums, ex_names=MIXER_WEIGHTS,
                           last=True)
    ffn1_sums = _pair_sum_exchange(_grad_pair_sums(FFN1_WEIGHTS, gw, core, "ffn1"))
    slots.update(zip(FFN1_WEIGHTS, _run_exchange(ffn1_sums, "grad_exchange_ffn1")))
    return dx, slots, summed


def kernel(x, mem, positions, ffn1_norm, ffn1_w_gu, ffn1_w_down, mix_norm, w_in, b_gate, sg_ln_g, sg_ln_b, sg_w, sg_b, mla_cq_norm, mla_w_uq, mla_ckv_norm, mla_w_ukv, mla_q_norm, mla_k_norm, mem_norm, mem_w_kv, mem_q_norm, mem_k_norm, w_branch_a, w_branch_b, w_branch_c, w_out, ffn2_norm, ffn2_w_gu, ffn2_w_down, loss_target, m_ffn1_norm, m_ffn1_w_gu, m_ffn1_w_down, m_mix_norm, m_w_in, m_b_gate, m_sg_ln_g, m_sg_ln_b, m_sg_w, m_sg_b, m_mla_cq_norm, m_mla_w_uq, m_mla_ckv_norm, m_mla_w_ukv, m_mla_q_norm, m_mla_k_norm, m_mem_norm, m_mem_w_kv, m_mem_q_norm, m_mem_k_norm, m_w_branch_a, m_w_branch_b, m_w_branch_c, m_w_out, m_ffn2_norm, m_ffn2_w_gu, m_ffn2_w_down, v_ffn1_norm, v_ffn1_w_gu, v_ffn1_w_down, v_mix_norm, v_w_in, v_b_gate, v_sg_ln_g, v_sg_ln_b, v_sg_w, v_sg_b, v_mla_cq_norm, v_mla_w_uq, v_mla_ckv_norm, v_mla_w_ukv, v_mla_q_norm, v_mla_k_norm, v_mem_norm, v_mem_w_kv, v_mem_q_norm, v_mem_k_norm, v_w_branch_a, v_w_branch_b, v_w_branch_c, v_w_out, v_ffn2_norm, v_ffn2_w_gu, v_ffn2_w_down):
    args = dict(locals())
    weights = {n: args[n] for n in WEIGHT_ORDER}
    mom_m = {n: args["m_" + n] for n in WEIGHT_ORDER}
    mom_v = {n: args["v_" + n] for n in WEIGHT_ORDER}
    small = {n: weights[n] for n, _ in SMALL}
    halves = lambda a, r, c: a.reshape(2, r // 2, c)

    shards = {n: halves(weights[n][0].astype(BF16), r, c) for n, r, c, _ in SHARDED}
    core = lax.axis_index("c").astype(jnp.int32).reshape(1)
    dx, slots, summed = _device_step(x[0], mem[0], positions[0], loss_target[0], small, shards, core)
    loss = summed.reshape(-1)[_N_SMALL]
    small_grads = _unpack_small(summed)

    grads, deltas, new_m, new_v = {}, {}, {}, {}
    for name, r, c, _ in SHARDED:
        outs = _adamw_slots(halves(weights[name][0], r, c), slots[name], halves(mom_m[name][0], r, c),
                            halves(mom_v[name][0], r, c), "adamw_" + name)
        shape = weights[name].shape
        grads[name], deltas[name], new_m[name], new_v[name] = [o.reshape(shape) for o in outs]
    dlt, nm, nv = _adamw(_pack_small(small), _pack_small(small_grads), _pack_small({n: mom_m[n] for n, _ in SMALL}),
                         _pack_small({n: mom_v[n] for n, _ in SMALL}), "adamw_small")
    for name, _ in SMALL:
        grads[name] = small_grads[name]
    deltas.update(_unpack_small(dlt))
    new_m.update(_unpack_small(nm))
    new_v.update(_unpack_small(nv))

    return (loss, dx[None], *[grads[n] for n in WEIGHT_ORDER], *[deltas[n] for n in WEIGHT_ORDER],
            *[new_m[n] for n in WEIGHT_ORDER], *[new_v[n] for n in WEIGHT_ORDER])
```
